```python
import math
import jax
import jax.numpy as jnp
from jax import lax
import numpy as np

D_MODEL = 1024
BATCH = 8
SEQ = 2048
DEPTH = 1

HEAD_DIM = 64
N_Q_HEADS = 16
N_KV_HEADS = 4
Q_PER_KV = N_Q_HEADS // N_KV_HEADS
ATTN_WIDTH = N_Q_HEADS * HEAD_DIM
KV_WIDTH = N_KV_HEADS * HEAD_DIM
DILATED_PATTERNS = ((128, 1), (512, 4), (2048, 16))
ATTN_BLOCK = 128
ROPE_THETA = 10000.0

SSM_HEADS = 16
SSM_HEAD_DIM = 64
SSM_WIDTH = SSM_HEADS * SSM_HEAD_DIM
SSM_STATE = 128
SSM_GROUPS = 2
SSM_HEADS_PER_GROUP = SSM_HEADS // SSM_GROUPS
BC_WIDTH = SSM_GROUPS * SSM_STATE
CONV_WIDTH = 4
CONV_CHANNELS = SSM_WIDTH + 2 * BC_WIDTH
SSD_CHUNK = 128
DT_MIN = 0.001
DT_MAX = 0.1
A_INIT_MIN = 1.0
A_INIT_MAX = 16.0

MIX_WIDTH = ATTN_WIDTH + SSM_WIDTH
IN_SPLITS = (
    ATTN_WIDTH,
    ATTN_WIDTH + KV_WIDTH,
    ATTN_WIDTH + 2 * KV_WIDTH,
    ATTN_WIDTH + 2 * KV_WIDTH + CONV_CHANNELS,
    ATTN_WIDTH + 2 * KV_WIDTH + CONV_CHANNELS + SSM_WIDTH,
)
IN_PROJ_COLS = IN_SPLITS[-1] + SSM_HEADS

D_FF = 2816
MACARON_WEIGHT = 0.5
NORM_EPS = 1e-6
POS_OFFSET_MAX = 1024

kernel_name = 'hymba_dilated_attn_mamba2_macaron_sandwich'


def rmsnorm(x, g):
    xf = x.astype(jnp.float32)
    y = xf * lax.rsqrt(jnp.mean(xf * xf, axis=-1, keepdims=True) + NORM_EPS)
    return (y * g.astype(jnp.float32)).astype(x.dtype)


def swiglu(x, w_gate, w_up, w_down):
    return (jax.nn.silu(x @ w_gate) * (x @ w_up)) @ w_down


def rope_cos_sin(positions):
    inv_freq = ROPE_THETA ** (-jnp.arange(0, HEAD_DIM, 2, dtype=jnp.float32) / HEAD_DIM)
    ang = positions.astype(jnp.float32)[..., None] * inv_freq
    ang = jnp.concatenate([ang, ang], axis=-1)
    return jnp.cos(ang), jnp.sin(ang)


def apply_rope(t, cos, sin):
    shape = cos.shape[:2] + (1,) * (t.ndim - 3) + (HEAD_DIM,)
    c = cos.reshape(shape)
    s = sin.reshape(shape)
    tf = t.astype(jnp.float32)
    t1, t2 = jnp.split(tf, 2, axis=-1)
    rot = jnp.concatenate([-t2, t1], axis=-1)
    return (tf * c + rot * s).astype(t.dtype)


def _to_strided_blocks(t, dilation, n_blocks):
    b, s = t.shape[:2]
    sub_len = s // dilation
    t = t.reshape((b, sub_len, dilation) + t.shape[2:])
    pad = n_blocks * ATTN_BLOCK - sub_len
    t = jnp.pad(t, [(0, 0), (0, pad)] + [(0, 0)] * (t.ndim - 2))
    return t.reshape((b, n_blocks, ATTN_BLOCK, dilation) + t.shape[3:])


def _from_strided_blocks(t, seq):
    b, n_blocks, _, dilation = t.shape[:4]
    sub_len = seq // dilation
    t = t.reshape((b, n_blocks * ATTN_BLOCK, dilation) + t.shape[4:])[:, :sub_len]
    return t.reshape((b, seq) + t.shape[3:])


def _with_prev_block(t):
    prev = jnp.concatenate([jnp.zeros_like(t[:, :1]), t[:, :-1]], axis=1)
    return jnp.concatenate([prev, t], axis=2)


def dilated_window_attention(q, k, v, window, dilation):
    b, s = q.shape[:2]
    sub_len = s // dilation
    span = window // dilation
    n_blocks = -(-sub_len // ATTN_BLOCK)
    qb = _to_strided_blocks(q, dilation, n_blocks)
    kk = _with_prev_block(_to_strided_blocks(k, dilation, n_blocks))
    vv = _with_prev_block(_to_strided_blocks(v, dilation, n_blocks))
    scores = jnp.einsum('bnqrkgh,bnsrkh->bnrkgqs', qb, kk, preferred_element_type=jnp.float32)
    qi = jnp.arange(ATTN_BLOCK)[:, None]
    si = jnp.arange(2 * ATTN_BLOCK)[None, :]
    dist = qi + ATTN_BLOCK - si
    band = (dist >= 0) & (dist <= span)
    key_idx = jnp.arange(n_blocks)[:, None, None] * ATTN_BLOCK + si[None] - ATTN_BLOCK
    mask = band[None] & (key_idx >= 0)
    scores = jnp.where(mask[None, :, None, None, None], scores, -jnp.inf)
    m = jnp.max(scores, axis=-1)
    p = jnp.exp(scores - m[..., None])
    l = jnp.sum(p, axis=-1)
    o = jnp.einsum('bnrkgqs,bnsrkh->bnqrkgh', p.astype(vv.dtype), vv, preferred_element_type=jnp.float32)
    l_t = jnp.moveaxis(l, -1, 2)
    o = o / l_t[..., None]
    return (_from_strided_blocks(o, s),
            _from_strided_blocks(jnp.moveaxis(m, -1, 2), s),
            _from_strided_blocks(l_t, s))


def mixture_of_dilations(q, k, v):
    outs, maxes, dens = [], [], []
    for window, dilation in DILATED_PATTERNS:
        o, m, l = dilated_window_attention(q, k, v, window, dilation)
        outs.append(o)
        maxes.append(m)
        dens.append(l)
    m_all = jnp.stack(maxes)
    w = jnp.stack(dens) * jnp.exp(m_all - jnp.max(m_all, axis=0, keepdims=True))
    o_all = jnp.stack(outs)
    return jnp.sum(w[..., None] * o_all, axis=0) / jnp.sum(w, axis=0)[..., None]


def causal_depthwise_conv(u, w, bias):
    y = lax.conv_general_dilated(u, w[:, None, :].astype(u.dtype), window_strides=(1,),
                                 padding=[(CONV_WIDTH - 1, 0)],
                                 dimension_numbers=('NWC', 'WIO', 'NWC'),
                                 feature_group_count=u.shape[-1])
    return y + bias


def ssd_chunked_scan(x, dt, a, b_in, c_in):
    bsz, s = x.shape[:2]
    n_chunks = s // SSD_CHUNK
    g, e, p, n = SSM_GROUPS, SSM_HEADS_PER_GROUP, SSM_HEAD_DIM, SSM_STATE
    xs = (x.astype(jnp.float32) * dt[..., None]).reshape(bsz, n_chunks, SSD_CHUNK, g, e, p)
    a_dt = jnp.moveaxis((dt * a).reshape(bsz, n_chunks, SSD_CHUNK, g, e), 2, -1)
    a_cs = jnp.cumsum(a_dt, axis=-1)
    bc = b_in.astype(jnp.float32).reshape(bsz, n_chunks, SSD_CHUNK, g, n)
    cc = c_in.astype(jnp.float32).reshape(bsz, n_chunks, SSD_CHUNK, g, n)
    idx = jnp.arange(SSD_CHUNK)
    causal = idx[:, None] >= idx[None, :]
    seg = a_cs[..., :, None] - a_cs[..., None, :]
    decay_in = jnp.exp(jnp.where(causal, seg, -jnp.inf))
    cb = jnp.einsum('bclgn,bcsgn->bcgls', cc, bc)
    y_diag = jnp.einsum('bcgels,bcsgep->bclgep', cb[:, :, :, None] * decay_in, xs)
    decay_to_end = jnp.exp(a_cs[..., -1:] - a_cs)
    states = jnp.einsum('bclgn,bcgel,bclgep->bcgepn', bc, decay_to_end, xs)
    chunk_decay = jnp.exp(a_cs[..., -1])

    def step(h, inp):
        st, dec = inp
        return h * dec[..., None, None] + st, h

    h0 = jnp.zeros((bsz, g, e, p, n), jnp.float32)
    _, prev_states = lax.scan(step, h0, (jnp.moveaxis(states, 1, 0), jnp.moveaxis(chunk_decay, 1, 0)))
    prev_states = jnp.moveaxis(prev_states, 0, 1)
    y_off = jnp.einsum('bclgn,bcgepn,bcgel->bclgep', cc, prev_states, jnp.exp(a_cs))
    return (y_diag + y_off).reshape(bsz, s, SSM_HEADS, p)


def hybrid_mixer(h, cos, sin, w_in, conv_w, conv_b, dt_bias, a_log, d_skip, ssm_norm, w_out):
    bsz, s, _ = h.shape
    proj = h @ w_in
    q, k, v, xbc, z, dt_raw = jnp.split(proj, list(IN_SPLITS), axis=-1)

    q = apply_rope(q.reshape(bsz, s, N_KV_HEADS, Q_PER_KV, HEAD_DIM), cos, sin) * (HEAD_DIM ** -0.5)
    k = apply_rope(k.reshape(bsz, s, N_KV_HEADS, HEAD_DIM), cos, sin)
    v = v.reshape(bsz, s, N_KV_HEADS, HEAD_DIM)
    attn_out = mixture_of_dilations(q, k, v).reshape(bsz, s, ATTN_WIDTH)

    xbc = jax.nn.silu(causal_depthwise_conv(xbc, conv_w, conv_b))
    xs, b_in, c_in = jnp.split(xbc, [SSM_WIDTH, SSM_WIDTH + BC_WIDTH], axis=-1)
    xs = xs.reshape(bsz, s, SSM_HEADS, SSM_HEAD_DIM)
    b_in = b_in.reshape(bsz, s, SSM_GROUPS, SSM_STATE)
    c_in = c_in.reshape(bsz, s, SSM_GROUPS, SSM_STATE)
    dt = jax.nn.softplus(dt_raw.astype(jnp.float32) + dt_bias.astype(jnp.float32))
    a = -jnp.exp(a_log.astype(jnp.float32))
    y = ssd_chunked_scan(xs, dt, a, b_in, c_in)
    y = y + d_skip.astype(jnp.float32)[:, None] * xs.astype(jnp.float32)
    y = y.reshape(bsz, s, SSM_WIDTH) * jax.nn.silu(z.astype(jnp.float32))
    y = y.reshape(bsz, s, SSM_GROUPS, SSM_WIDTH // SSM_GROUPS)
    y = y * lax.rsqrt(jnp.mean(y * y, axis=-1, keepdims=True) + NORM_EPS)
    y = y.reshape(bsz, s, SSM_WIDTH) * ssm_norm.astype(jnp.float32)

    mixed = jnp.concatenate([attn_out.astype(h.dtype), y.astype(h.dtype)], axis=-1)
    return mixed @ w_out


def _fwd_setup_inputs(seed: int = 0) -> dict:
    key = jax.random.key(seed)
    ks = jax.random.split(key, 24)
    f32 = jnp.float32

    def dense(k, shape, fan_in):
        return jax.random.normal(k, shape, f32) * fan_in ** -0.5

    def gain(k, shape):
        return 1.0 + 0.05 * jax.random.normal(k, shape, f32)

    x = jax.random.normal(ks[0], (BATCH, SEQ, D_MODEL), f32)
    positions = (jnp.arange(SEQ, dtype=jnp.int32)[None, :]
                 + jax.random.randint(ks[1], (BATCH, 1), 0, POS_OFFSET_MAX, dtype=jnp.int32))
    dt = jnp.exp(jax.random.uniform(ks[11], (DEPTH, SSM_HEADS), f32)
                 * (math.log(DT_MAX) - math.log(DT_MIN)) + math.log(DT_MIN))
    dt_bias = dt + jnp.log(-jnp.expm1(-dt))
    a_log = jnp.log(jax.random.uniform(ks[12], (DEPTH, SSM_HEADS), f32, A_INIT_MIN, A_INIT_MAX))
    return {
        'x': x,
        'positions': positions,
        'ffn1_pre_norm': gain(ks[2], (DEPTH, D_MODEL)),
        'ffn1_w_gate': dense(ks[3], (DEPTH, D_MODEL, D_FF), D_MODEL),
        'ffn1_w_up': dense(ks[4], (DEPTH, D_MODEL, D_FF), D_MODEL),
        'ffn1_w_down': dense(ks[5], (DEPTH, D_FF, D_MODEL), D_FF),
        'ffn1_post_norm': gain(ks[6], (DEPTH, D_MODEL)),
        'mix_pre_norm': gain(ks[7], (DEPTH, D_MODEL)),
        'w_in': dense(ks[8], (DEPTH, D_MODEL, IN_PROJ_COLS), D_MODEL),
        'conv_w': dense(ks[9], (DEPTH, CONV_WIDTH, CONV_CHANNELS), CONV_WIDTH),
        'conv_b': 0.02 * jax.random.normal(ks[10], (DEPTH, CONV_CHANNELS), f32),
        'dt_bias': dt_bias,
        'a_log': a_log,
        'd_skip': 1.0 + 0.1 * jax.random.normal(ks[13], (DEPTH, SSM_HEADS), f32),
        'ssm_norm': gain(ks[14], (DEPTH, SSM_WIDTH)),
        'w_out': dense(ks[15], (DEPTH, MIX_WIDTH, D_MODEL), MIX_WIDTH),
        'mix_post_norm': gain(ks[16], (DEPTH, D_MODEL)),
        'ffn2_pre_norm': gain(ks[17], (DEPTH, D_MODEL)),
        'ffn2_w_gate': dense(ks[18], (DEPTH, D_MODEL, D_FF), D_MODEL),
        'ffn2_w_up': dense(ks[19], (DEPTH, D_MODEL, D_FF), D_MODEL),
        'ffn2_w_down': dense(ks[20], (DEPTH, D_FF, D_MODEL), D_FF),
        'ffn2_post_norm': gain(ks[21], (DEPTH, D_MODEL)),
    }


def _fwd_reference(x, positions, ffn1_pre_norm, ffn1_w_gate, ffn1_w_up, ffn1_w_down, ffn1_post_norm,
              mix_pre_norm, w_in, conv_w, conv_b, dt_bias, a_log, d_skip, ssm_norm, w_out, mix_post_norm,
              ffn2_pre_norm, ffn2_w_gate, ffn2_w_up, ffn2_w_down, ffn2_post_norm):
    cos, sin = rope_cos_sin(positions)
    for i in range(DEPTH):
        h = swiglu(rmsnorm(x, ffn1_pre_norm[i]), ffn1_w_gate[i], ffn1_w_up[i], ffn1_w_down[i])
        x = x + MACARON_WEIGHT * rmsnorm(h, ffn1_post_norm[i])
        h = hybrid_mixer(rmsnorm(x, mix_pre_norm[i]), cos, sin, w_in[i], conv_w[i], conv_b[i],
                         dt_bias[i], a_log[i], d_skip[i], ssm_norm[i], w_out[i])
        x = x + rmsnorm(h, mix_post_norm[i])
        h = swiglu(rmsnorm(x, ffn2_pre_norm[i]), ffn2_w_gate[i], ffn2_w_up[i], ffn2_w_down[i])
        x = x + MACARON_WEIGHT * rmsnorm(h, ffn2_post_norm[i])
    return x


import jax as _jax
import jax.numpy as _jnp

TWIN_FORMAT = 'train_step'
FWD_PARAMS = ['x', 'positions', 'ffn1_pre_norm', 'ffn1_w_gate', 'ffn1_w_up', 'ffn1_w_down', 'ffn1_post_norm', 'mix_pre_norm', 'w_in', 'conv_w', 'conv_b', 'dt_bias', 'a_log', 'd_skip', 'ssm_norm', 'w_out', 'mix_post_norm', 'ffn2_pre_norm', 'ffn2_w_gate', 'ffn2_w_up', 'ffn2_w_down', 'ffn2_post_norm']
TWIN_WEIGHTS = ['ffn1_pre_norm', 'ffn1_w_gate', 'ffn1_w_up', 'ffn1_w_down', 'ffn1_post_norm', 'mix_pre_norm', 'w_in', 'conv_w', 'conv_b', 'dt_bias', 'a_log', 'd_skip', 'ssm_norm', 'w_out', 'mix_post_norm', 'ffn2_pre_norm', 'ffn2_w_gate', 'ffn2_w_up', 'ffn2_w_down', 'ffn2_post_norm']
TWIN_DIFF_INPUT = 'x'
TWIN_INPUTS = ['x', 'positions', 'ffn1_pre_norm', 'ffn1_w_gate', 'ffn1_w_up', 'ffn1_w_down', 'ffn1_post_norm', 'mix_pre_norm', 'w_in', 'conv_w', 'conv_b', 'dt_bias', 'a_log', 'd_skip', 'ssm_norm', 'w_out', 'mix_post_norm', 'ffn2_pre_norm', 'ffn2_w_gate', 'ffn2_w_up', 'ffn2_w_down', 'ffn2_post_norm', 'loss_target', 'm_ffn1_pre_norm', 'm_ffn1_w_gate', 'm_ffn1_w_up', 'm_ffn1_w_down', 'm_ffn1_post_norm', 'm_mix_pre_norm', 'm_w_in', 'm_conv_w', 'm_conv_b', 'm_dt_bias', 'm_a_log', 'm_d_skip', 'm_ssm_norm', 'm_w_out', 'm_mix_post_norm', 'm_ffn2_pre_norm', 'm_ffn2_w_gate', 'm_ffn2_w_up', 'm_ffn2_w_down', 'm_ffn2_post_norm', 'v_ffn1_pre_norm', 'v_ffn1_w_gate', 'v_ffn1_w_up', 'v_ffn1_w_down', 'v_ffn1_post_norm', 'v_mix_pre_norm', 'v_w_in', 'v_conv_w', 'v_conv_b', 'v_dt_bias', 'v_a_log', 'v_d_skip', 'v_ssm_norm', 'v_w_out', 'v_mix_post_norm', 'v_ffn2_pre_norm', 'v_ffn2_w_gate', 'v_ffn2_w_up', 'v_ffn2_w_down', 'v_ffn2_post_norm']
TWIN_OUTPUTS = ['loss', 'grad_x', 'grad_ffn1_pre_norm', 'grad_ffn1_w_gate', 'grad_ffn1_w_up', 'grad_ffn1_w_down', 'grad_ffn1_post_norm', 'grad_mix_pre_norm', 'grad_w_in', 'grad_conv_w', 'grad_conv_b', 'grad_dt_bias', 'grad_a_log', 'grad_d_skip', 'grad_ssm_norm', 'grad_w_out', 'grad_mix_post_norm', 'grad_ffn2_pre_norm', 'grad_ffn2_w_gate', 'grad_ffn2_w_up', 'grad_ffn2_w_down', 'grad_ffn2_post_norm', 'delta_ffn1_pre_norm', 'delta_ffn1_w_gate', 'delta_ffn1_w_up', 'delta_ffn1_w_down', 'delta_ffn1_post_norm', 'delta_mix_pre_norm', 'delta_w_in', 'delta_conv_w', 'delta_conv_b', 'delta_dt_bias', 'delta_a_log', 'delta_d_skip', 'delta_ssm_norm', 'delta_w_out', 'delta_mix_post_norm', 'delta_ffn2_pre_norm', 'delta_ffn2_w_gate', 'delta_ffn2_w_up', 'delta_ffn2_w_down', 'delta_ffn2_post_norm', 'new_m_ffn1_pre_norm', 'new_m_ffn1_w_gate', 'new_m_ffn1_w_up', 'new_m_ffn1_w_down', 'new_m_ffn1_post_norm', 'new_m_mix_pre_norm', 'new_m_w_in', 'new_m_conv_w', 'new_m_conv_b', 'new_m_dt_bias', 'new_m_a_log', 'new_m_d_skip', 'new_m_ssm_norm', 'new_m_w_out', 'new_m_mix_post_norm', 'new_m_ffn2_pre_norm', 'new_m_ffn2_w_gate', 'new_m_ffn2_w_up', 'new_m_ffn2_w_down', 'new_m_ffn2_post_norm', 'new_v_ffn1_pre_norm', 'new_v_ffn1_w_gate', 'new_v_ffn1_w_up', 'new_v_ffn1_w_down', 'new_v_ffn1_post_norm', 'new_v_mix_pre_norm', 'new_v_w_in', 'new_v_conv_w', 'new_v_conv_b', 'new_v_dt_bias', 'new_v_a_log', 'new_v_d_skip', 'new_v_ssm_norm', 'new_v_w_out', 'new_v_mix_post_norm', 'new_v_ffn2_pre_norm', 'new_v_ffn2_w_gate', 'new_v_ffn2_w_up', 'new_v_ffn2_w_down', 'new_v_ffn2_post_norm']
TWIN_LEAF_KINDS = {'loss': 'loss', 'grad_x': 'grad_x', 'grad_ffn1_pre_norm': 'grad_w', 'grad_ffn1_w_gate': 'grad_w', 'grad_ffn1_w_up': 'grad_w', 'grad_ffn1_w_down': 'grad_w', 'grad_ffn1_post_norm': 'grad_w', 'grad_mix_pre_norm': 'grad_w', 'grad_w_in': 'grad_w', 'grad_conv_w': 'grad_w', 'grad_conv_b': 'grad_w', 'grad_dt_bias': 'grad_w', 'grad_a_log': 'grad_w', 'grad_d_skip': 'grad_w', 'grad_ssm_norm': 'grad_w', 'grad_w_out': 'grad_w', 'grad_mix_post_norm': 'grad_w', 'grad_ffn2_pre_norm': 'grad_w', 'grad_ffn2_w_gate': 'grad_w', 'grad_ffn2_w_up': 'grad_w', 'grad_ffn2_w_down': 'grad_w', 'grad_ffn2_post_norm': 'grad_w', 'delta_ffn1_pre_norm': 'delta_w', 'delta_ffn1_w_gate': 'delta_w', 'delta_ffn1_w_up': 'delta_w', 'delta_ffn1_w_down': 'delta_w', 'delta_ffn1_post_norm': 'delta_w', 'delta_mix_pre_norm': 'delta_w', 'delta_w_in': 'delta_w', 'delta_conv_w': 'delta_w', 'delta_conv_b': 'delta_w', 'delta_dt_bias': 'delta_w', 'delta_a_log': 'delta_w', 'delta_d_skip': 'delta_w', 'delta_ssm_norm': 'delta_w', 'delta_w_out': 'delta_w', 'delta_mix_post_norm': 'delta_w', 'delta_ffn2_pre_norm': 'delta_w', 'delta_ffn2_w_gate': 'delta_w', 'delta_ffn2_w_up': 'delta_w', 'delta_ffn2_w_down': 'delta_w', 'delta_ffn2_post_norm': 'delta_w', 'new_m_ffn1_pre_norm': 'new_m', 'new_m_ffn1_w_gate': 'new_m', 'new_m_ffn1_w_up': 'new_m', 'new_m_ffn1_w_down': 'new_m', 'new_m_ffn1_post_norm': 'new_m', 'new_m_mix_pre_norm': 'new_m', 'new_m_w_in': 'new_m', 'new_m_conv_w': 'new_m', 'new_m_conv_b': 'new_m', 'new_m_dt_bias': 'new_m', 'new_m_a_log': 'new_m', 'new_m_d_skip': 'new_m', 'new_m_ssm_norm': 'new_m', 'new_m_w_out': 'new_m', 'new_m_mix_post_norm': 'new_m', 'new_m_ffn2_pre_norm': 'new_m', 'new_m_ffn2_w_gate': 'new_m', 'new_m_ffn2_w_up': 'new_m', 'new_m_ffn2_w_down': 'new_m', 'new_m_ffn2_post_norm': 'new_m', 'new_v_ffn1_pre_norm': 'new_v', 'new_v_ffn1_w_gate': 'new_v', 'new_v_ffn1_w_up': 'new_v', 'new_v_ffn1_w_down': 'new_v', 'new_v_ffn1_post_norm': 'new_v', 'new_v_mix_pre_norm': 'new_v', 'new_v_w_in': 'new_v', 'new_v_conv_w': 'new_v', 'new_v_conv_b': 'new_v', 'new_v_dt_bias': 'new_v', 'new_v_a_log': 'new_v', 'new_v_d_skip': 'new_v', 'new_v_ssm_norm': 'new_v', 'new_v_w_out': 'new_v', 'new_v_mix_post_norm': 'new_v', 'new_v_ffn2_pre_norm': 'new_v', 'new_v_ffn2_w_gate': 'new_v', 'new_v_ffn2_w_up': 'new_v', 'new_v_ffn2_w_down': 'new_v', 'new_v_ffn2_post_norm': 'new_v'}


def _forward(args):
    return _fwd_reference(*[args[k] for k in FWD_PARAMS])


def _output_shape():
    out = _jax.eval_shape(lambda: _forward(_fwd_setup_inputs(0)))
    return out.shape, out.dtype

N_MICROBATCH = 1
ADAM_LR = 0.001
ADAM_B1 = 0.9
ADAM_B2 = 0.999
ADAM_EPS = 1e-08
ADAM_WD = 0.01
ADAM_STEP = 10
PER_EXAMPLE_BATCH_AXIS = {'x': 0, 'positions': 0, 'loss_target': 0}
SHARED_INPUTS = []
_WEIGHT_DTYPES = {'ffn1_pre_norm': _jnp.float32, 'ffn1_w_gate': _jnp.float32, 'ffn1_w_up': _jnp.float32, 'ffn1_w_down': _jnp.float32, 'ffn1_post_norm': _jnp.float32, 'mix_pre_norm': _jnp.float32, 'w_in': _jnp.float32, 'conv_w': _jnp.float32, 'conv_b': _jnp.float32, 'dt_bias': _jnp.float32, 'a_log': _jnp.float32, 'd_skip': _jnp.float32, 'ssm_norm': _jnp.float32, 'w_out': _jnp.float32, 'mix_post_norm': _jnp.float32, 'ffn2_pre_norm': _jnp.float32, 'ffn2_w_gate': _jnp.float32, 'ffn2_w_up': _jnp.float32, 'ffn2_w_down': _jnp.float32, 'ffn2_post_norm': _jnp.float32}
MOMENT_SCALE = {'ffn1_pre_norm': 3.015105e-01, 'ffn1_w_gate': 1.255419e-01, 'ffn1_w_up': 1.284809e-01, 'ffn1_w_down': 2.154545e-01, 'ffn1_post_norm': 3.984799e+00, 'mix_pre_norm': 4.030506e-01, 'w_in': 1.960525e-01, 'conv_w': 2.438191e-01, 'conv_b': 4.852676e-01, 'dt_bias': 9.383494e-01, 'a_log': 9.879446e-01, 'd_skip': 1.854160e+00, 'ssm_norm': 3.350929e-01, 'w_out': 3.274065e-01, 'mix_post_norm': 1.608804e+01, 'ffn2_pre_norm': 2.323718e-01, 'ffn2_w_gate': 7.875065e-02, 'ffn2_w_up': 1.074279e-01, 'ffn2_w_down': 1.784296e-01, 'ffn2_post_norm': 4.001060e+00}


def _to_microbatches(a, axis):
    t = _jnp.moveaxis(a, axis, 0)
    t = t.reshape((N_MICROBATCH, t.shape[0] // N_MICROBATCH) + t.shape[1:])
    return _jnp.moveaxis(t, 1, axis + 1)


def setup_inputs(seed: int = 0) -> dict:
    inp = _fwd_setup_inputs(seed)
    key = _jax.random.fold_in(_jax.random.key(seed), 7919)
    shape, _ = _output_shape()
    out = dict(inp)
    out["loss_target"] = _jax.random.normal(_jax.random.fold_in(key, 0), shape, _jnp.float32)
    for i, name in enumerate(TWIN_WEIGHTS):
        w = inp[name].astype(_jnp.float32)
        if MOMENT_SCALE is None:
            s = _jnp.sqrt(_jnp.mean(_jnp.square(w)) + 1e-30)
        else:
            s = MOMENT_SCALE[name]
        km, kv = _jax.random.split(_jax.random.fold_in(key, i + 1))
        out[name] = w
        out["m_" + name] = s * _jax.random.normal(km, w.shape, _jnp.float32)
        out["v_" + name] = (s * s) * _jax.random.uniform(kv, w.shape, _jnp.float32, 0.5, 1.5)
    if N_MICROBATCH > 1:
        for name, axis in PER_EXAMPLE_BATCH_AXIS.items():
            out[name] = _to_microbatches(out[name], axis)
    return {'x': out['x'], 'positions': out['positions'], 'ffn1_pre_norm': out['ffn1_pre_norm'], 'ffn1_w_gate': out['ffn1_w_gate'], 'ffn1_w_up': out['ffn1_w_up'], 'ffn1_w_down': out['ffn1_w_down'], 'ffn1_post_norm': out['ffn1_post_norm'], 'mix_pre_norm': out['mix_pre_norm'], 'w_in': out['w_in'], 'conv_w': out['conv_w'], 'conv_b': out['conv_b'], 'dt_bias': out['dt_bias'], 'a_log': out['a_log'], 'd_skip': out['d_skip'], 'ssm_norm': out['ssm_norm'], 'w_out': out['w_out'], 'mix_post_norm': out['mix_post_norm'], 'ffn2_pre_norm': out['ffn2_pre_norm'], 'ffn2_w_gate': out['ffn2_w_gate'], 'ffn2_w_up': out['ffn2_w_up'], 'ffn2_w_down': out['ffn2_w_down'], 'ffn2_post_norm': out['ffn2_post_norm'], 'loss_target': out['loss_target'], 'm_ffn1_pre_norm': out['m_ffn1_pre_norm'], 'm_ffn1_w_gate': out['m_ffn1_w_gate'], 'm_ffn1_w_up': out['m_ffn1_w_up'], 'm_ffn1_w_down': out['m_ffn1_w_down'], 'm_ffn1_post_norm': out['m_ffn1_post_norm'], 'm_mix_pre_norm': out['m_mix_pre_norm'], 'm_w_in': out['m_w_in'], 'm_conv_w': out['m_conv_w'], 'm_conv_b': out['m_conv_b'], 'm_dt_bias': out['m_dt_bias'], 'm_a_log': out['m_a_log'], 'm_d_skip': out['m_d_skip'], 'm_ssm_norm': out['m_ssm_norm'], 'm_w_out': out['m_w_out'], 'm_mix_post_norm': out['m_mix_post_norm'], 'm_ffn2_pre_norm': out['m_ffn2_pre_norm'], 'm_ffn2_w_gate': out['m_ffn2_w_gate'], 'm_ffn2_w_up': out['m_ffn2_w_up'], 'm_ffn2_w_down': out['m_ffn2_w_down'], 'm_ffn2_post_norm': out['m_ffn2_post_norm'], 'v_ffn1_pre_norm': out['v_ffn1_pre_norm'], 'v_ffn1_w_gate': out['v_ffn1_w_gate'], 'v_ffn1_w_up': out['v_ffn1_w_up'], 'v_ffn1_w_down': out['v_ffn1_w_down'], 'v_ffn1_post_norm': out['v_ffn1_post_norm'], 'v_mix_pre_norm': out['v_mix_pre_norm'], 'v_w_in': out['v_w_in'], 'v_conv_w': out['v_conv_w'], 'v_conv_b': out['v_conv_b'], 'v_dt_bias': out['v_dt_bias'], 'v_a_log': out['v_a_log'], 'v_d_skip': out['v_d_skip'], 'v_ssm_norm': out['v_ssm_norm'], 'v_w_out': out['v_w_out'], 'v_mix_post_norm': out['v_mix_post_norm'], 'v_ffn2_pre_norm': out['v_ffn2_pre_norm'], 'v_ffn2_w_gate': out['v_ffn2_w_gate'], 'v_ffn2_w_up': out['v_ffn2_w_up'], 'v_ffn2_w_down': out['v_ffn2_w_down'], 'v_ffn2_post_norm': out['v_ffn2_post_norm']}


def _loss(weights, diff, rest, loss_target):
    with _jax.named_scope("forward"):
        args = {**rest, TWIN_DIFF_INPUT: diff, **{k: w.astype(_WEIGHT_DTYPES[k]) for k, w in weights.items()}}
        y = _forward(args)
    with _jax.named_scope("loss_head"):
        err = _jnp.square(y.astype(_jnp.float32) - loss_target)
        return 0.5 * _jnp.sum(_jnp.mean(err, axis=-1)) if err.ndim else 0.5 * err


def _adamw(w, g, m, v):
    m = ADAM_B1 * m + (1.0 - ADAM_B1) * g
    v = ADAM_B2 * v + (1.0 - ADAM_B2) * _jnp.square(g)
    m_hat = m / (1.0 - ADAM_B1 ** ADAM_STEP)
    v_hat = v / (1.0 - ADAM_B2 ** ADAM_STEP)
    delta = -ADAM_LR * (m_hat / (_jnp.sqrt(v_hat) + ADAM_EPS) + ADAM_WD * w)
    return delta, m, v


def reference(x, positions, ffn1_pre_norm, ffn1_w_gate, ffn1_w_up, ffn1_w_down, ffn1_post_norm, mix_pre_norm, w_in, conv_w, conv_b, dt_bias, a_log, d_skip, ssm_norm, w_out, mix_post_norm, ffn2_pre_norm, ffn2_w_gate, ffn2_w_up, ffn2_w_down, ffn2_post_norm, loss_target, m_ffn1_pre_norm, m_ffn1_w_gate, m_ffn1_w_up, m_ffn1_w_down, m_ffn1_post_norm, m_mix_pre_norm, m_w_in, m_conv_w, m_conv_b, m_dt_bias, m_a_log, m_d_skip, m_ssm_norm, m_w_out, m_mix_post_norm, m_ffn2_pre_norm, m_ffn2_w_gate, m_ffn2_w_up, m_ffn2_w_down, m_ffn2_post_norm, v_ffn1_pre_norm, v_ffn1_w_gate, v_ffn1_w_up, v_ffn1_w_down, v_ffn1_post_norm, v_mix_pre_norm, v_w_in, v_conv_w, v_conv_b, v_dt_bias, v_a_log, v_d_skip, v_ssm_norm, v_w_out, v_mix_post_norm, v_ffn2_pre_norm, v_ffn2_w_gate, v_ffn2_w_up, v_ffn2_w_down, v_ffn2_post_norm):
    given = dict(x=x, positions=positions, ffn1_pre_norm=ffn1_pre_norm, ffn1_w_gate=ffn1_w_gate, ffn1_w_up=ffn1_w_up, ffn1_w_down=ffn1_w_down, ffn1_post_norm=ffn1_post_norm, mix_pre_norm=mix_pre_norm, w_in=w_in, conv_w=conv_w, conv_b=conv_b, dt_bias=dt_bias, a_log=a_log, d_skip=d_skip, ssm_norm=ssm_norm, w_out=w_out, mix_post_norm=mix_post_norm, ffn2_pre_norm=ffn2_pre_norm, ffn2_w_gate=ffn2_w_gate, ffn2_w_up=ffn2_w_up, ffn2_w_down=ffn2_w_down, ffn2_post_norm=ffn2_post_norm, loss_target=loss_target, m_ffn1_pre_norm=m_ffn1_pre_norm, m_ffn1_w_gate=m_ffn1_w_gate, m_ffn1_w_up=m_ffn1_w_up, m_ffn1_w_down=m_ffn1_w_down, m_ffn1_post_norm=m_ffn1_post_norm, m_mix_pre_norm=m_mix_pre_norm, m_w_in=m_w_in, m_conv_w=m_conv_w, m_conv_b=m_conv_b, m_dt_bias=m_dt_bias, m_a_log=m_a_log, m_d_skip=m_d_skip, m_ssm_norm=m_ssm_norm, m_w_out=m_w_out, m_mix_post_norm=m_mix_post_norm, m_ffn2_pre_norm=m_ffn2_pre_norm, m_ffn2_w_gate=m_ffn2_w_gate, m_ffn2_w_up=m_ffn2_w_up, m_ffn2_w_down=m_ffn2_w_down, m_ffn2_post_norm=m_ffn2_post_norm, v_ffn1_pre_norm=v_ffn1_pre_norm, v_ffn1_w_gate=v_ffn1_w_gate, v_ffn1_w_up=v_ffn1_w_up, v_ffn1_w_down=v_ffn1_w_down, v_ffn1_post_norm=v_ffn1_post_norm, v_mix_pre_norm=v_mix_pre_norm, v_w_in=v_w_in, v_conv_w=v_conv_w, v_conv_b=v_conv_b, v_dt_bias=v_dt_bias, v_a_log=v_a_log, v_d_skip=v_d_skip, v_ssm_norm=v_ssm_norm, v_w_out=v_w_out, v_mix_post_norm=v_mix_post_norm, v_ffn2_pre_norm=v_ffn2_pre_norm, v_ffn2_w_gate=v_ffn2_w_gate, v_ffn2_w_up=v_ffn2_w_up, v_ffn2_w_down=v_ffn2_w_down, v_ffn2_post_norm=v_ffn2_post_norm)
    weights = {n: given[n] for n in TWIN_WEIGHTS}
    shared = {n: given[n] for n in SHARED_INPUTS}
    per_example = {n: given[n] for n in ['x', 'positions']}
    grad_fn = _jax.value_and_grad(_loss, argnums=(0, 1))

    def one_microbatch(ex, loss_target):
        ex = dict(ex)
        diff = ex.pop(TWIN_DIFF_INPUT)
        return grad_fn(weights, diff, {**shared, **ex}, loss_target)

    if N_MICROBATCH == 1:
        loss, (grad_w, grad_x) = one_microbatch(per_example, given["loss_target"])
    else:
        def body(carry, xs):
            loss_sum, grad_sum = carry
            l_k, (gw_k, gx_k) = one_microbatch(xs[0], xs[1])
            with _jax.named_scope("update"):
                return (loss_sum + l_k, _jax.tree.map(_jnp.add, grad_sum, gw_k)), gx_k

        init = (_jnp.zeros((), _jnp.float32), _jax.tree.map(_jnp.zeros_like, weights))
        (loss, grad_w), grad_x = _jax.lax.scan(body, init, (per_example, given["loss_target"]))
    with _jax.named_scope("update"):
        delta_w, new_m, new_v = {}, {}, {}
        for n in TWIN_WEIGHTS:
            delta_w[n], new_m[n], new_v[n] = _adamw(weights[n], grad_w[n], given["m_" + n], given["v_" + n])
    return (loss, grad_x, *[grad_w[n] for n in TWIN_WEIGHTS], *[delta_w[n] for n in TWIN_WEIGHTS],
            *[new_m[n] for n in TWIN_WEIGHTS], *[new_v[n] for n in TWIN_WEIGHTS])
```

```python
import functools
import math

import jax
import jax.numpy as jnp
from jax import lax
from jax.experimental import pallas as pl
from jax.experimental.pallas import tpu as pltpu

F32 = jnp.float32
BF16 = jnp.bfloat16

S = 2048
D = 1024
FF = 2816
NSH = 4
FS = FF // NSH
HD = 64
NKV = 4
NQ_PER_KV = 4
KVW = NKV * HD
CONV_C = 1536
CONV_K = 4
SSM_W = 1024
NST = 128
NCH = S // 128
WIN_COLS = 4112
WIN_SH = WIN_COLS // NSH
WIN_PAD = 4224
COL_DT = 4096
EPS = 1e-6
NEG = -1e30

ADAM_LR = 0.001
ADAM_B1 = 0.9
ADAM_B2 = 0.999
ADAM_EPS = 1e-08
ADAM_WD = 0.01
ADAM_STEP = 10

VMEM_LIMIT = 56 * 1024 * 1024
TS = 512
TR = 256

NN = (((1,), (0,)), ((), ()))
NT = (((1,), (1,)), ((), ()))
TN = (((0,), (0,)), ((), ()))
MESH = pl.DeviceIdType.MESH


def _cparams(*sem):
    return pltpu.CompilerParams(dimension_semantics=sem, vmem_limit_bytes=VMEM_LIMIT)


def _dot(a, b, dims):
    return lax.dot_general(a.astype(BF16), b.astype(BF16), dims, preferred_element_type=F32)


def _dot_exact(a, b):
    return lax.dot_general(a, b, NN, precision=lax.Precision.HIGHEST, preferred_element_type=F32)


def _sigmoid(v):
    return 1.0 / (1.0 + jnp.exp(-v))


def _mm(name, operands, dims, grid, in_specs, o_spec, out_shape, acc_shape):
    npairs = len(operands) // 2
    nk = grid[-1]
    kaxis = len(grid) - 1

    def body(*refs):
        o_ref, acc = refs[2 * npairs], refs[2 * npairs + 1]
        k = pl.program_id(kaxis)

        @pl.when(k == 0)
        def _():
            acc[...] = jnp.zeros_like(acc)

        t = None
        for i in range(npairs):
            d = _dot(refs[2 * i][...], refs[2 * i + 1][...], dims)
            t = d if t is None else t + d
        acc[...] += t

        @pl.when(k == nk - 1)
        def _():
            o_ref[...] = acc[...].astype(o_ref.dtype)

    return pl.pallas_call(
        body, name=name, grid=grid, in_specs=in_specs, out_specs=o_spec, out_shape=out_shape,
        scratch_shapes=[pltpu.VMEM(acc_shape, F32)],
        compiler_params=_cparams(*(("parallel",) * kaxis + ("arbitrary",))),
    )(*operands)


def _ffn_up(name, n, colw, wi):
    def body(n_ref, wg_ref, wu_ref, g_ref, u_ref, a_ref):
        nb = n_ref[...]
        g = _dot(nb, wg_ref[...], NN)
        u = _dot(nb, wu_ref[...], NN)
        g_ref[...] = g.astype(BF16)
        u_ref[...] = u.astype(BF16)
        a_ref[...] = (g * _sigmoid(g) * u).astype(BF16)

    out = jax.ShapeDtypeStruct((NSH, S, FS), BF16)
    ospec = pl.BlockSpec((None, TS, FS), lambda s, i: (s, i, 0))
    return pl.pallas_call(
        body, name=name, grid=(NSH, S // TS),
        in_specs=[pl.BlockSpec((TS, D), lambda s, i: (i, 0)),
                  pl.BlockSpec((None, None, D, FS), lambda s, i: (s, wi, 0, 0)),
                  pl.BlockSpec((None, None, D, FS), lambda s, i: (s, wi + 1, 0, 0))],
        out_specs=[ospec, ospec, ospec], out_shape=[out, out, out],
        compiler_params=_cparams("parallel", "parallel"),
    )(n, colw, colw)


def _ffn_dact(name, dh, wdw, wi, gate, up):
    def body(dh_ref, wd_ref, g_ref, u_ref, dg_ref, du_ref):
        da = _dot(dh_ref[...], wd_ref[...], NT)
        g = g_ref[...].astype(F32)
        u = u_ref[...].astype(F32)
        sg = _sigmoid(g)
        dg_ref[...] = (da * u * (sg * (1.0 + g * (1.0 - sg)))).astype(BF16)
        du_ref[...] = (da * (g * sg)).astype(BF16)

    out = jax.ShapeDtypeStruct((NSH, S, FS), BF16)
    aspec = pl.BlockSpec((None, TS, FS), lambda s, i: (s, i, 0))
    return pl.pallas_call(
        body, name=name, grid=(NSH, S // TS),
        in_specs=[pl.BlockSpec((TS, D), lambda s, i: (i, 0)),
                  pl.BlockSpec((None, None, FS, D), lambda s, i: (s, wi, 0, 0)),
                  aspec, aspec],
        out_specs=[aspec, aspec], out_shape=[out, out],
        compiler_params=_cparams("parallel", "parallel"),
    )(dh, wdw, gate, up)


def _rstd(v):
    return lax.rsqrt(jnp.mean(v * v, axis=-1, keepdims=True) + EPS)


def _row_spec():
    return pl.BlockSpec((TR, D), lambda i: (i, 0))


def _vec_spec():
    return pl.BlockSpec((1, D), lambda i: (0, 0))


def _acc_rows(ref, v):
    @pl.when(pl.program_id(0) == 0)
    def _():
        ref[...] = jnp.zeros_like(ref)
    ref[...] += jnp.sum(v, axis=0, keepdims=True)


def _prenorm(name, x, g):
    def body(x_ref, g_ref, n_ref):
        xv = x_ref[...]
        n_ref[...] = (xv * _rstd(xv) * g_ref[...]).astype(BF16)

    return pl.pallas_call(
        body, name=name, grid=(S // TR,), in_specs=[_row_spec(), _vec_spec()], out_specs=_row_spec(),
        out_shape=jax.ShapeDtypeStruct((S, D), BF16), compiler_params=_cparams("parallel"),
    )(x, g)


def _postres(name, x, h, p, alpha, gnext):
    def body(x_ref, h_ref, p_ref, g_ref, xo_ref, n_ref):
        hv = h_ref[...]
        xo = x_ref[...] + alpha * (hv * _rstd(hv) * p_ref[...])
        xo_ref[...] = xo
        n_ref[...] = (xo * _rstd(xo) * g_ref[...]).astype(BF16)

    return pl.pallas_call(
        body, name=name, grid=(S // TR,),
        in_specs=[_row_spec(), _row_spec(), _vec_spec(), _vec_spec()],
        out_specs=[_row_spec(), _row_spec()],
        out_shape=[jax.ShapeDtypeStruct((S, D), F32), jax.ShapeDtypeStruct((S, D), BF16)],
        compiler_params=_cparams("parallel"),
    )(x, h, p, gnext)


def _final(x, h, p, tgt, alpha):
    def body(x_ref, h_ref, p_ref, t_ref, dy_ref, dh_ref, dp_ref, loss_ref):
        hv = h_ref[...]
        r = _rstd(hv)
        hn = hv * r
        pv = p_ref[...]
        e = x_ref[...] + alpha * (hn * pv) - t_ref[...]
        dy = e * (1.0 / D)
        dy_ref[...] = dy
        du = alpha * dy * pv
        dh_ref[...] = (r * (du - hn * jnp.mean(du * hn, axis=-1, keepdims=True))).astype(BF16)
        _acc_rows(dp_ref, alpha * dy * hn)
        part = 0.5 * jnp.sum(jnp.mean(e * e, axis=-1, keepdims=True), axis=0, keepdims=True)
        _acc_rows(loss_ref, jnp.broadcast_to(part, (1, 128)))

    return pl.pallas_call(
        body, name="loss_head", grid=(S // TR,),
        in_specs=[_row_spec(), _row_spec(), _vec_spec(), _row_spec()],
        out_specs=[_row_spec(), _row_spec(), _vec_spec(), pl.BlockSpec((1, 128), lambda i: (0, 0))],
        out_shape=[jax.ShapeDtypeStruct((S, D), F32), jax.ShapeDtypeStruct((S, D), BF16),
                   jax.ShapeDtypeStruct((1, D), F32), jax.ShapeDtypeStruct((1, 128), F32)],
        compiler_params=_cparams("arbitrary"),
    )(x, h, p, tgt)


def _mid_bwd(name, dres, dn, x, g, h, p, alpha):
    def body(dr_ref, dn_ref, x_ref, g_ref, h_ref, p_ref, dx_ref, dh_ref, dg_ref, dp_ref):
        xv = x_ref[...]
        xn = xv * _rstd(xv)
        dnv = dn_ref[...]
        dng = dnv * g_ref[...]
        dx = dr_ref[...] + _rstd(xv) * (dng - xn * jnp.mean(dng * xn, axis=-1, keepdims=True))
        dx_ref[...] = dx
        _acc_rows(dg_ref, dnv * xn)
        hv = h_ref[...]
        r = _rstd(hv)
        hn = hv * r
        du = alpha * dx * p_ref[...]
        dh_ref[...] = (r * (du - hn * jnp.mean(du * hn, axis=-1, keepdims=True))).astype(BF16)
        _acc_rows(dp_ref, alpha * dx * hn)

    return pl.pallas_call(
        body, name=name, grid=(S // TR,),
        in_specs=[_row_spec(), _row_spec(), _row_spec(), _vec_spec(), _row_spec(), _vec_spec()],
        out_specs=[_row_spec(), _row_spec(), _vec_spec(), _vec_spec()],
        out_shape=[jax.ShapeDtypeStruct((S, D), F32), jax.ShapeDtypeStruct((S, D), BF16),
                   jax.ShapeDtypeStruct((1, D), F32), jax.ShapeDtypeStruct((1, D), F32)],
        compiler_params=_cparams("arbitrary"),
    )(dres, dn, x, g, h, p)


def _first_bwd(dres, dn, x, g):
    def body(dr_ref, dn_ref, x_ref, g_ref, dx_ref, dg_ref):
        xv = x_ref[...]
        r = _rstd(xv)
        xn = xv * r
        dnv = dn_ref[...]
        dng = dnv * g_ref[...]
        dx_ref[...] = dr_ref[...] + r * (dng - xn * jnp.mean(dng * xn, axis=-1, keepdims=True))
        _acc_rows(dg_ref, dnv * xn)

    return pl.pallas_call(
        body, name="first_bwd", grid=(S // TR,),
        in_specs=[_row_spec(), _row_spec(), _row_spec(), _vec_spec()],
        out_specs=[_row_spec(), _vec_spec()],
        out_shape=[jax.ShapeDtypeStruct((S, D), F32), jax.ShapeDtypeStruct((1, D), F32)],
        compiler_params=_cparams("arbitrary"),
    )(dres, dn, x, g)


def _rope(name, src, col_block, width, cos, sin, sign, scale):
    def body(t_ref, c_ref, s_ref, o_ref):
        t = t_ref[...].astype(F32)
        c = jnp.tile(c_ref[...], (1, width // 128))
        sn = jnp.tile(s_ref[...], (1, width // 128))
        lane = lax.broadcasted_iota(jnp.int32, t.shape, 1) & (HD - 1)
        rot = jnp.where(lane < HD // 2, -pltpu.roll(t, width - HD // 2, 1), pltpu.roll(t, HD // 2, 1))
        o_ref[...] = ((t * c + sign * (rot * sn)) * scale).astype(BF16)

    return pl.pallas_call(
        body, name=name, grid=(S // TR,),
        in_specs=[pl.BlockSpec((TR, width), lambda i: (i, col_block)),
                  pl.BlockSpec((TR, 128), lambda i: (i, 0)), pl.BlockSpec((TR, 128), lambda i: (i, 0))],
        out_specs=pl.BlockSpec((TR, width), lambda i: (i, 0)),
        out_shape=jax.ShapeDtypeStruct((S, width), BF16), compiler_params=_cparams("parallel"),
    )(src, cos, sin)


def _multiplicity(base, shape, q_axis):
    qi = lax.broadcasted_iota(jnp.int32, shape, q_axis) & 127
    ki = lax.broadcasted_iota(jnp.int32, shape, 1 - q_axis)
    d = base + qi - ki
    cnt = ((d <= 128).astype(F32) + (((d & 3) == 0) & (d <= 512)).astype(F32) + ((d & 15) == 0).astype(F32))
    return jnp.where(d >= 0, cnt, 0.0)


QROWS = NQ_PER_KV * 128


def _attn_fwd(qh, kh, vh):
    def body(q_ref, k_ref, v_ref, o_ref, lse_ref):
        qb = pl.program_id(1)
        q = q_ref[...].reshape(QROWS, HD)

        def step(kb, carry):
            m, l, acc = carry
            off = pl.multiple_of(kb * 128, 128)
            k = k_ref[pl.ds(off, 128), :]
            v = v_ref[pl.ds(off, 128), :]
            cnt = _multiplicity((qb - kb) * 128, (QROWS, 128), 0)
            s = jnp.where(cnt > 0.0, _dot(q, k, NT), NEG)
            m_new = jnp.maximum(m, jnp.max(s, axis=1, keepdims=True))
            p = cnt * jnp.exp(s - m_new)
            a = jnp.exp(m - m_new)
            return m_new, a * l + jnp.sum(p, axis=1, keepdims=True), a * acc + _dot(p, v, NN)

        m, l, acc = lax.fori_loop(
            0, qb + 1, step,
            (jnp.full((QROWS, 1), NEG, F32), jnp.zeros((QROWS, 1), F32), jnp.zeros((QROWS, HD), F32)))
        o_ref[...] = (acc / l).reshape(NQ_PER_KV, 128, HD)
        lse_ref[...] = (m + jnp.log(l)).reshape(NQ_PER_KV, 128, 1)

    qspec = pl.BlockSpec((None, NQ_PER_KV, 128, HD), lambda j, i: (j, 0, i, 0))
    kspec = pl.BlockSpec((None, S, HD), lambda j, i: (j, 0, 0))
    return pl.pallas_call(
        body, name="attn_fwd", grid=(NKV, NCH), in_specs=[qspec, kspec, kspec],
        out_specs=[qspec, pl.BlockSpec((None, NQ_PER_KV, 128, 1), lambda j, i: (j, 0, i, 0))],
        out_shape=[jax.ShapeDtypeStruct((NKV, NQ_PER_KV, S, HD), F32),
                   jax.ShapeDtypeStruct((NKV, NQ_PER_KV, S, 1), F32)],
        compiler_params=_cparams("parallel", "parallel"),
    )(qh, kh, vh)


def _attn_dq(qh, kh, vh, doh, oh, lse):
    def body(q_ref, k_ref, v_ref, do_ref, o_ref, lse_ref, dq_ref, dl_ref, dob_ref):
        qb = pl.program_id(1)
        q = q_ref[...].reshape(QROWS, HD)
        do = do_ref[...].reshape(QROWS, HD)
        delta = jnp.sum(do * o_ref[...].reshape(QROWS, HD), axis=1, keepdims=True)
        lsev = lse_ref[...].reshape(QROWS, 1)
        dob = do.astype(BF16)

        def step(kb, dq):
            off = pl.multiple_of(kb * 128, 128)
            k = k_ref[pl.ds(off, 128), :]
            v = v_ref[pl.ds(off, 128), :]
            cnt = _multiplicity((qb - kb) * 128, (QROWS, 128), 0)
            s = jnp.where(cnt > 0.0, _dot(q, k, NT), NEG)
            p = cnt * jnp.exp(s - lsev)
            ds = p * (_dot(dob, v, NT) - delta)
            return dq + _dot(ds, k, NN)

        dq = lax.fori_loop(0, qb + 1, step, jnp.zeros((QROWS, HD), F32))
        dq_ref[...] = dq.reshape(NQ_PER_KV, 128, HD)
        dl_ref[...] = delta.reshape(NQ_PER_KV, 128, 1)
        dob_ref[...] = dob.reshape(NQ_PER_KV, 128, HD)

    qspec = pl.BlockSpec((None, NQ_PER_KV, 128, HD), lambda j, i: (j, 0, i, 0))
    kspec = pl.BlockSpec((None, S, HD), lambda j, i: (j, 0, 0))
    lspec = pl.BlockSpec((None, NQ_PER_KV, 128, 1), lambda j, i: (j, 0, i, 0))
    return pl.pallas_call(
        body, name="attn_dq", grid=(NKV, NCH), in_specs=[qspec, kspec, kspec, qspec, qspec, lspec],
        out_specs=[qspec, lspec, qspec],
        out_shape=[jax.ShapeDtypeStruct((NKV, NQ_PER_KV, S, HD), F32),
                   jax.ShapeDtypeStruct((NKV, NQ_PER_KV, S, 1), F32),
                   jax.ShapeDtypeStruct((NKV, NQ_PER_KV, S, HD), BF16)],
        compiler_params=_cparams("parallel", "parallel"),
    )(qh, kh, vh, doh, oh, lse)


def _attn_dkv(qh, kh, vh, dob, lse_row, delta_row):
    def body(q_ref, k_ref, v_ref, do_ref, lse_ref, dl_ref, dk_ref, dv_ref):
        kb = pl.program_id(1)
        k = k_ref[...]
        v = v_ref[...]

        def step(qb, carry):
            dk, dv = carry
            off = pl.multiple_of(qb * 128, 128)
            q = q_ref[:, pl.ds(off, 128), :].reshape(QROWS, HD)
            do = do_ref[:, pl.ds(off, 128), :].reshape(QROWS, HD)
            cnt = _multiplicity((qb - kb) * 128, (128, QROWS), 1)
            st = jnp.where(cnt > 0.0, _dot(k, q, NT), NEG)
            pt = cnt * jnp.exp(st - lse_ref[qb])
            dst = pt * (_dot(v, do, NT) - dl_ref[qb])
            return dk + _dot(dst, q, NN), dv + _dot(pt, do, NN)

        dk, dv = lax.fori_loop(kb, NCH, step, (jnp.zeros((128, HD), F32), jnp.zeros((128, HD), F32)))
        dk_ref[...] = dk
        dv_ref[...] = dv

    qspec = pl.BlockSpec((None, NQ_PER_KV, S, HD), lambda j, i: (j, 0, 0, 0))
    kspec = pl.BlockSpec((None, 128, HD), lambda j, i: (j, i, 0))
    rspec = pl.BlockSpec((None, NCH, 1, QROWS), lambda j, i: (j, 0, 0, 0))
    return pl.pallas_call(
        body, name="attn_dkv", grid=(NKV, NCH), in_specs=[qspec, kspec, kspec, qspec, rspec, rspec],
        out_specs=[kspec, kspec],
        out_shape=[jax.ShapeDtypeStruct((NKV, S, HD), F32), jax.ShapeDtypeStruct((NKV, S, HD), F32)],
        compiler_params=_cparams("parallel", "parallel"),
    )(qh, kh, vh, dob, lse_row, delta_row)


CONV_BLK = 256
CONV_COL0 = 1536 // CONV_BLK


def _shift_down(u, j, row):
    return jnp.where(row >= j, pltpu.roll(u, j, 0), 0.0)


def _conv_pre(u, w_ref, b_ref, row):
    y = b_ref[...] + w_ref[CONV_K - 1:CONV_K, :] * u
    for j in range(1, CONV_K):
        y = y + w_ref[CONV_K - 1 - j:CONV_K - j, :] * _shift_down(u, j, row)
    return y


def _conv_fwd(proj, convw, convb):
    def body(u_ref, w_ref, b_ref, o_ref):
        u = u_ref[...]
        row = lax.broadcasted_iota(jnp.int32, u.shape, 0)
        y = _conv_pre(u, w_ref, b_ref, row)
        o_ref[...] = y * _sigmoid(y)

    return pl.pallas_call(
        body, name="conv_fwd", grid=(CONV_C // CONV_BLK,),
        in_specs=[pl.BlockSpec((S, CONV_BLK), lambda i: (0, CONV_COL0 + i)),
                  pl.BlockSpec((CONV_K, CONV_BLK), lambda i: (0, i)),
                  pl.BlockSpec((1, CONV_BLK), lambda i: (0, i))],
        out_specs=pl.BlockSpec((S, CONV_BLK), lambda i: (0, i)),
        out_shape=jax.ShapeDtypeStruct((S, CONV_C), F32), compiler_params=_cparams("parallel"),
    )(proj, convw, convb)


def _conv_bwd(dact, proj, convw, convb):
    def body(da_ref, u_ref, w_ref, b_ref, du_ref, dw_ref, db_ref):
        u = u_ref[...]
        row = lax.broadcasted_iota(jnp.int32, u.shape, 0)
        y = _conv_pre(u, w_ref, b_ref, row)
        sg = _sigmoid(y)
        dy = da_ref[...] * (sg * (1.0 + y * (1.0 - sg)))
        db_ref[...] = jnp.sum(dy, axis=0, keepdims=True)
        du = w_ref[CONV_K - 1:CONV_K, :] * dy
        r8 = lax.broadcasted_iota(jnp.int32, (8, CONV_BLK), 0)
        dw = jnp.where(r8 == CONV_K - 1, jnp.sum(dy * u, axis=0, keepdims=True), 0.0)
        for j in range(1, CONV_K):
            du = du + w_ref[CONV_K - 1 - j:CONV_K - j, :] * jnp.where(row < S - j, pltpu.roll(dy, S - j, 0), 0.0)
            dw = dw + jnp.where(r8 == CONV_K - 1 - j,
                                jnp.sum(dy * _shift_down(u, j, row), axis=0, keepdims=True), 0.0)
        du_ref[...] = du.astype(BF16)
        dw_ref[...] = dw

    return pl.pallas_call(
        body, name="conv_bwd", grid=(CONV_C // CONV_BLK,),
        in_specs=[pl.BlockSpec((S, CONV_BLK), lambda i: (0, i)),
                  pl.BlockSpec((S, CONV_BLK), lambda i: (0, CONV_COL0 + i)),
                  pl.BlockSpec((CONV_K, CONV_BLK), lambda i: (0, i)),
                  pl.BlockSpec((1, CONV_BLK), lambda i: (0, i))],
        out_specs=[pl.BlockSpec((S, CONV_BLK), lambda i: (0, i)), pl.BlockSpec((8, CONV_BLK), lambda i: (0, i)),
                   pl.BlockSpec((1, CONV_BLK), lambda i: (0, i))],
        out_shape=[jax.ShapeDtypeStruct((S, CONV_C), BF16), jax.ShapeDtypeStruct((8, CONV_C), F32),
                   jax.ShapeDtypeStruct((1, CONV_C), F32)],
        compiler_params=_cparams("parallel"),
    )(dact, proj, convw, convb)


NPAIR = 8


def _ssd_scalars(dtr_ref, dtb_ref, alog_ref):
    z = dtr_ref[...] + dtb_ref[...]
    dt = jnp.maximum(z, 0.0) + jnp.log(1.0 + jnp.exp(-jnp.abs(z)))
    a = -jnp.exp(alog_ref[...])
    r = lax.broadcasted_iota(jnp.int32, (128, 128), 0)
    c = lax.broadcasted_iota(jnp.int32, (128, 128), 1)
    tri = (r >= c).astype(F32)
    cs = _dot_exact(tri, dt * a)
    return z, dt, a, cs, r, c


def _pair_terms(cs, cst, dt, h1, h2, lo):
    c1, c2 = cs[:, h1:h1 + 1], cs[:, h2:h2 + 1]
    l1, l2 = cs[127:128, h1:h1 + 1], cs[127:128, h2:h2 + 1]
    e_l = jnp.where(lo, jnp.exp(c1), jnp.exp(c2))
    dte1, dte2 = jnp.exp(l1 - c1), jnp.exp(l2 - c2)
    dte_l = jnp.where(lo, dte1, dte2)
    dt_l = jnp.where(lo, dt[:, h1:h1 + 1], dt[:, h2:h2 + 1])
    return c1, c2, jnp.exp(l1), jnp.exp(l2), e_l, dte1, dte2, dte_l, dt_l


def _gate_norm(y, zv, w):
    yg = y * (zv * _sigmoid(zv))
    outs, rs = [], []
    for g in range(2):
        blk = yg[:, 512 * g:512 * (g + 1)]
        r = lax.rsqrt(jnp.mean(blk * blk, axis=-1, keepdims=True) + EPS)
        outs.append(blk * r)
        rs.append(r)
    return jnp.concatenate(outs, axis=1), rs, yg


def _ssd_fwd(xbc, proj, dtb, alog, dskip_l, ssmw):
    def body(x_ref, b_ref, c_ref, dtr_ref, z_ref, dtb_ref, alog_ref, dsk_ref, w_ref, y_ref, yn_ref, hp_ref, h_ref):
        @pl.when(pl.program_id(0) == 0)
        def _():
            h_ref[...] = jnp.zeros_like(h_ref)

        _, dt, _, cs, r, c = _ssd_scalars(dtr_ref, dtb_ref, alog_ref)
        cst = cs.T
        causal = r >= c
        lo = c < HD
        hp_ref[...] = h_ref[...]
        for g in range(2):
            bg = b_ref[:, 128 * g:128 * (g + 1)]
            cg = c_ref[:, 128 * g:128 * (g + 1)]
            cb = _dot(cg, bg, NT)
            for j in range(4):
                pj = 4 * g + j
                h1, h2 = 2 * pj, 2 * pj + 1
                sl = slice(128 * pj, 128 * (pj + 1))
                xp = x_ref[:, sl]
                c1, c2, cd1, cd2, e_l, _, _, dte_l, dt_l = _pair_terms(cs, cst, dt, h1, h2, lo)
                xdt = xp * dt_l
                m1 = cb * jnp.exp(jnp.where(causal, c1 - cst[h1:h1 + 1, :], NEG))
                m2 = cb * jnp.exp(jnp.where(causal, c2 - cst[h2:h2 + 1, :], NEG))
                yd = jnp.where(lo, _dot(m1, xdt, NN), _dot(m2, xdt, NN))
                hp = h_ref[pj]
                yo = _dot(cg, hp, NT) * e_l
                st = _dot(xdt * dte_l, bg, TN)
                h_ref[pj] = hp * jnp.where(r < HD, cd1, cd2) + st
                y_ref[:, sl] = yd + yo + dsk_ref[:, sl] * xp
        yn, _, _ = _gate_norm(y_ref[...], z_ref[...], w_ref[...])
        yn_ref[...] = (yn * w_ref[...]).astype(BF16)

    return pl.pallas_call(
        body, name="ssd_fwd", grid=(NCH,),
        in_specs=[pl.BlockSpec((128, SSM_W), lambda i: (i, 0)),
                  pl.BlockSpec((128, 256), lambda i: (i, 4)), pl.BlockSpec((128, 256), lambda i: (i, 5)),
                  pl.BlockSpec((128, 128), lambda i: (i, COL_DT // 128)),
                  pl.BlockSpec((128, SSM_W), lambda i: (i, 3)),
                  pl.BlockSpec((1, 128), lambda i: (0, 0)), pl.BlockSpec((1, 128), lambda i: (0, 0)),
                  pl.BlockSpec((1, SSM_W), lambda i: (0, 0)), pl.BlockSpec((1, SSM_W), lambda i: (0, 0))],
        out_specs=[pl.BlockSpec((128, SSM_W), lambda i: (i, 0)), pl.BlockSpec((128, SSM_W), lambda i: (i, 0)),
                   pl.BlockSpec((None, NPAIR, 128, 128), lambda i: (i, 0, 0, 0))],
        out_shape=[jax.ShapeDtypeStruct((S, SSM_W), F32), jax.ShapeDtypeStruct((S, SSM_W), BF16),
                   jax.ShapeDtypeStruct((NCH, NPAIR, 128, 128), F32)],
        scratch_shapes=[pltpu.VMEM((NPAIR, 128, 128), F32)],
        compiler_params=_cparams("arbitrary"),
    )(xbc, xbc, xbc, proj, proj, dtb, alog, dskip_l, ssmw)


def _ssd_bwd(dmixed, y, xbc, proj, hprev, dtb, alog, dskip_l, ssmw):
    def body(dyn_ref, y_ref, x_ref, b_ref, c_ref, dtr_ref, z_ref, hp_ref, dtb_ref, alog_ref, dsk_ref, w_ref,
             dxbc_ref, dz_ref, ddt_ref, dw_ref, dsc_ref, g_ref):
        @pl.when(pl.program_id(0) == 0)
        def _():
            g_ref[...] = jnp.zeros_like(g_ref)
            dsc_ref[...] = jnp.zeros_like(dsc_ref)

        z, dt, a, cs, r, c = _ssd_scalars(dtr_ref, dtb_ref, alog_ref)
        cst = cs.T
        causal = r >= c
        lo = c < HD

        yv = y_ref[...]
        zv = z_ref[...]
        wv = w_ref[...]
        ygn, rs, yg = _gate_norm(yv, zv, wv)
        dyn = dyn_ref[...]
        _acc_rows(dw_ref, dyn * ygn)
        dynw = dyn * wv
        parts = []
        for g in range(2):
            sl = slice(512 * g, 512 * (g + 1))
            a_g, n_g = dynw[:, sl], ygn[:, sl]
            parts.append(rs[g] * (a_g - n_g * jnp.mean(a_g * n_g, axis=-1, keepdims=True)))
        dyg = jnp.concatenate(parts, axis=1)
        sz = _sigmoid(zv)
        dz_ref[...] = (dyg * yv * (sz * (1.0 + zv * (1.0 - sz)))).astype(BF16)
        dy_all = dyg * (zv * sz)

        dcs_cols = jnp.zeros((128, 128), F32)
        dcs_rows = jnp.zeros((128, 128), F32)
        ddt_x = jnp.zeros((128, 128), F32)
        dd_row = jnp.zeros((1, 128), F32)
        last = r == 127
        for g in range(2):
            bg = b_ref[:, 128 * g:128 * (g + 1)]
            cg = c_ref[:, 128 * g:128 * (g + 1)]
            cb = _dot(cg, bg, NT)
            dcb = jnp.zeros((128, 128), F32)
            db_acc = jnp.zeros((128, NST), F32)
            dc_acc = jnp.zeros((128, NST), F32)
            for j in range(4):
                pj = 4 * g + j
                h1, h2 = 2 * pj, 2 * pj + 1
                sl = slice(128 * pj, 128 * (pj + 1))
                xp = x_ref[:, sl]
                dyp = dy_all[:, sl]
                c1, c2, cd1, cd2, e_l, dte1, dte2, dte_l, dt_l = _pair_terms(cs, cst, dt, h1, h2, lo)
                xdt = xp * dt_l
                hp = hp_ref[pj]
                gp = g_ref[pj]
                dxp = dsk_ref[:, sl] * dyp
                dyx = dyp * xp
                dd_row = dd_row + jnp.where(c[0:1, :] == h1, jnp.sum(jnp.where(lo, dyx, 0.0), keepdims=True), 0.0) \
                    + jnp.where(c[0:1, :] == h2, jnp.sum(jnp.where(lo, 0.0, dyx), keepdims=True), 0.0)
                dzs = dyp * e_l
                dc_acc = dc_acc + _dot(dzs, hp, NN)
                g_from = _dot(dzs, cg, TN)
                ryo = dyp * (_dot(cg, hp, NT) * e_l)
                k1 = jnp.sum(jnp.where(lo, ryo, 0.0), axis=1, keepdims=True)
                k2 = jnp.sum(jnp.where(lo, 0.0, ryo), axis=1, keepdims=True)
                qm = _dot(bg, gp, NT)
                dxdt = qm * dte_l
                qx = qm * xdt
                t1 = jnp.sum(jnp.where(lo, qx, 0.0), axis=1, keepdims=True) * dte1
                t2 = jnp.sum(jnp.where(lo, 0.0, qx), axis=1, keepdims=True) * dte2
                db_acc = db_acc + _dot(xdt * dte_l, gp, NN)
                gh = gp * hp
                dl1 = jnp.sum(t1, keepdims=True) + jnp.sum(jnp.where(r < HD, gh, 0.0), keepdims=True) * cd1
                dl2 = jnp.sum(t2, keepdims=True) + jnp.sum(jnp.where(r < HD, 0.0, gh), keepdims=True) * cd2
                g_ref[pj] = g_from + jnp.where(r < HD, cd1, cd2) * gp
                k1 = k1 - t1 + jnp.where(last[:, 0:1], dl1, 0.0)
                k2 = k2 - t2 + jnp.where(last[:, 0:1], dl2, 0.0)
                for hh, ch, msk in ((h1, c1, lo), (h2, c2, jnp.logical_not(lo))):
                    lm = jnp.exp(jnp.where(causal, ch - cst[hh:hh + 1, :], NEG))
                    mm = cb * lm
                    dm = jnp.where(causal, _dot(jnp.where(msk, dyp, 0.0), xdt, NT), 0.0)
                    w = dm * mm
                    kk = jnp.sum(w, axis=1, keepdims=True)
                    if hh == h1:
                        k1 = k1 + kk
                    else:
                        k2 = k2 + kk
                    dcs_rows = dcs_rows + jnp.where(r == hh, jnp.sum(w, axis=0, keepdims=True), 0.0)
                    dcb = dcb + dm * lm
                    dxdt = dxdt + jnp.where(msk, _dot(mm, dyp, TN), 0.0)
                dcs_cols = dcs_cols + jnp.where(c == h1, k1, 0.0) + jnp.where(c == h2, k2, 0.0)
                dxx = dxdt * xp
                ddt_x = ddt_x + jnp.where(c == h1, jnp.sum(jnp.where(lo, dxx, 0.0), axis=1, keepdims=True), 0.0) \
                    + jnp.where(c == h2, jnp.sum(jnp.where(lo, 0.0, dxx), axis=1, keepdims=True), 0.0)
                dxbc_ref[:, sl] = dxp + dxdt * dt_l
            dxbc_ref[:, SSM_W + 128 * g:SSM_W + 128 * (g + 1)] = db_acc + _dot(dcb, cg, TN)
            dxbc_ref[:, SSM_W + 256 + 128 * g:SSM_W + 256 + 128 * (g + 1)] = dc_acc + _dot(dcb, bg, NN)

        dcs = dcs_cols - dcs_rows.T
        dad = _dot_exact((c >= r).astype(F32), dcs)
        ddt = dad * a + ddt_x
        ddtr = jnp.where(c < 16, ddt * _sigmoid(z), 0.0)
        ddt_ref[...] = ddtr.astype(BF16)
        r8 = lax.broadcasted_iota(jnp.int32, (8, 128), 0)
        dsc_ref[...] += (jnp.where(r8 == 0, jnp.sum(ddtr, axis=0, keepdims=True), 0.0)
                         + jnp.where(r8 == 1, jnp.sum(dad * dt, axis=0, keepdims=True) * a, 0.0)
                         + jnp.where(r8 == 2, dd_row, 0.0))

    rev = NCH - 1
    return pl.pallas_call(
        body, name="ssd_bwd", grid=(NCH,),
        in_specs=[pl.BlockSpec((128, SSM_W), lambda i: (rev - i, 1)),
                  pl.BlockSpec((128, SSM_W), lambda i: (rev - i, 0)),
                  pl.BlockSpec((128, SSM_W), lambda i: (rev - i, 0)),
                  pl.BlockSpec((128, 256), lambda i: (rev - i, 4)), pl.BlockSpec((128, 256), lambda i: (rev - i, 5)),
                  pl.BlockSpec((128, 128), lambda i: (rev - i, COL_DT // 128)),
                  pl.BlockSpec((128, SSM_W), lambda i: (rev - i, 3)),
                  pl.BlockSpec((None, NPAIR, 128, 128), lambda i: (rev - i, 0, 0, 0)),
                  pl.BlockSpec((1, 128), lambda i: (0, 0)), pl.BlockSpec((1, 128), lambda i: (0, 0)),
                  pl.BlockSpec((1, SSM_W), lambda i: (0, 0)), pl.BlockSpec((1, SSM_W), lambda i: (0, 0))],
        out_specs=[pl.BlockSpec((128, CONV_C), lambda i: (rev - i, 0)),
                   pl.BlockSpec((128, SSM_W), lambda i: (rev - i, 0)),
                   pl.BlockSpec((128, 128), lambda i: (rev - i, 0)),
                   pl.BlockSpec((1, SSM_W), lambda i: (0, 0)), pl.BlockSpec((8, 128), lambda i: (0, 0))],
        out_shape=[jax.ShapeDtypeStruct((S, CONV_C), F32), jax.ShapeDtypeStruct((S, SSM_W), BF16),
                   jax.ShapeDtypeStruct((S, 128), BF16), jax.ShapeDtypeStruct((1, SSM_W), F32),
                   jax.ShapeDtypeStruct((8, 128), F32)],
        scratch_shapes=[pltpu.VMEM((NPAIR, 128, 128), F32)],
        compiler_params=_cparams("arbitrary"),
    )(dmixed, y, xbc, xbc, xbc, proj, proj, hprev, dtb, alog, dskip_l, ssmw)


def _cast_stack(name, arrs, tr):
    n = len(arrs)
    rows, cols = arrs[0].shape

    def body(*refs):
        for i in range(n):
            refs[n][i] = refs[i][...].astype(BF16)

    return pl.pallas_call(
        body, name=name, grid=(rows // tr,),
        in_specs=[pl.BlockSpec((tr, cols), lambda i: (i, 0))] * n,
        out_specs=pl.BlockSpec((n, tr, cols), lambda i: (0, i, 0)),
        out_shape=jax.ShapeDtypeStruct((n, rows, cols), BF16), compiler_params=_cparams("parallel"),
    )(*arrs)


def _pair_sum(name, c_idx, ps, rs, th):
    n = len(ps)
    _, _, h, cols = ps[0].shape

    def body(c_ref, *refs):
        for i in range(n):
            refs[2 * n + i][...] = (refs[i][...].astype(F32) + refs[n + i][...].astype(F32)).astype(BF16)

    spec = pl.BlockSpec((None, th, cols), lambda s, i, cr: (s, i, 0))
    return pl.pallas_call(
        body, name=name,
        grid_spec=pltpu.PrefetchScalarGridSpec(
            num_scalar_prefetch=1, grid=(NSH, h // th),
            in_specs=[pl.BlockSpec((None, None, th, cols), lambda s, i, cr: (s, cr[0], i, 0))] * n + [spec] * n,
            out_specs=[spec] * n),
        out_shape=[jax.ShapeDtypeStruct((NSH, h, cols), BF16)] * n,
        compiler_params=_cparams("parallel", "parallel"),
    )(c_idx, *ps, *rs)


def _chip_sum(name, ts, th):
    n = len(ts)
    _, h, cols = ts[0].shape

    def body(*refs):
        for i in range(n):
            t = refs[i][...].astype(F32)
            refs[n + i][...] = ((t[0] + t[1]) + t[2]) + t[3]

    return pl.pallas_call(
        body, name=name, grid=(h // th,),
        in_specs=[pl.BlockSpec((NSH, th, cols), lambda i: (0, i, 0))] * n,
        out_specs=[pl.BlockSpec((th, cols), lambda i: (i, 0))] * n,
        out_shape=[jax.ShapeDtypeStruct((h, cols), F32)] * n, compiler_params=_cparams("parallel"),
    )(*ts)


def _adamw(name, ws, gs, ms, vs, tr):
    n = len(ws)
    rows, cols = ws[0].shape
    c1 = 1.0 / (1.0 - ADAM_B1 ** ADAM_STEP)
    c2 = 1.0 / (1.0 - ADAM_B2 ** ADAM_STEP)

    def body(*refs):
        for i in range(n):
            w, g, m, v = (refs[k * n + i][...] for k in range(4))
            m2 = ADAM_B1 * m + (1.0 - ADAM_B1) * g
            v2 = ADAM_B2 * v + (1.0 - ADAM_B2) * (g * g)
            refs[4 * n + 3 * i][...] = -ADAM_LR * ((m2 * c1) / (jnp.sqrt(v2 * c2) + ADAM_EPS) + ADAM_WD * w)
            refs[4 * n + 3 * i + 1][...] = m2
            refs[4 * n + 3 * i + 2][...] = v2

    spec = pl.BlockSpec((tr, cols), lambda i: (i, 0))
    outs = pl.pallas_call(
        body, name=name, grid=(rows // tr,), in_specs=[spec] * (4 * n), out_specs=[spec] * (3 * n),
        out_shape=[jax.ShapeDtypeStruct((rows, cols), F32)] * (3 * n), compiler_params=_cparams("parallel"),
    )(*ws, *gs, *ms, *vs)
    return [tuple(outs[3 * i:3 * i + 3]) for i in range(n)]


def _place():
    x, y, c = lax.axis_index("x"), lax.axis_index("y"), lax.axis_index("c")
    chips = [(1 - x, y), (x, 1 - y), (1 - x, 1 - y)]
    return x, y, c, chips


def _any_specs(n):
    return [pl.BlockSpec(memory_space=pl.ANY)] * n


def _gather_weights(colsh, wdsh, winsh, woutsh, cw):
    ins = [colsh, wdsh, winsh, woutsh]
    halves = [a.shape[0] // 2 for a in ins]
    nb = len(ins)

    def body(*refs):
        src, cw_in = refs[:nb], refs[nb]
        dst, cw_out = refs[nb + 1:2 * nb + 1], refs[2 * nb + 1]
        send, recv, fsend, frecv, local, cws, cwr = refs[2 * nb + 2:]
        x, y, c, chips = _place()
        me = 2 * x + y

        def half(ref, b, s, hc):
            return ref.at[s, pl.ds(hc * halves[b], halves[b])]

        own = [pltpu.make_async_copy(src[b], dst[b].at[me], local.at[b]) for b in range(nb)]
        own.append(pltpu.make_async_copy(cw_in, cw_out.at[me], local.at[nb]))
        for cp in own:
            cp.start()
        first, passed = [], []
        for j, chip in enumerate(chips):
            for b in range(nb):
                first.append(pltpu.make_async_remote_copy(
                    src_ref=src[b].at[pl.ds(c * halves[b], halves[b])], dst_ref=half(dst[b], b, me, c),
                    send_sem=send.at[j * nb + b], recv_sem=recv.at[j * nb + b],
                    device_id=(chip[0], chip[1], c), device_id_type=MESH))
            first.append(pltpu.make_async_remote_copy(
                src_ref=cw_in, dst_ref=cw_out.at[me], send_sem=cws.at[j], recv_sem=cwr.at[j],
                device_id=(chip[0], chip[1], c), device_id_type=MESH))
        for cp in first:
            cp.start()
        for j, chip in enumerate(chips):
            s = 2 * chip[0] + chip[1]
            for b in range(nb):
                landed = half(dst[b], b, s, c)
                pltpu.make_async_remote_copy(
                    src_ref=landed, dst_ref=landed, send_sem=send.at[j * nb + b], recv_sem=recv.at[j * nb + b],
                    device_id=(x, y, c), device_id_type=MESH).wait_recv()
                fw = pltpu.make_async_remote_copy(
                    src_ref=landed, dst_ref=landed, send_sem=fsend.at[j * nb + b], recv_sem=frecv.at[j * nb + b],
                    device_id=(x, y, 1 - c), device_id_type=MESH)
                fw.start()
                passed.append(fw)
        for j, chip in enumerate(chips):
            s = 2 * chip[0] + chip[1]
            for b in range(nb):
                other = half(dst[b], b, s, 1 - c)
                pltpu.make_async_remote_copy(
                    src_ref=other, dst_ref=other, send_sem=fsend.at[j * nb + b], recv_sem=frecv.at[j * nb + b],
                    device_id=(x, y, c), device_id_type=MESH).wait_recv()
            pltpu.make_async_remote_copy(
                src_ref=cw_in, dst_ref=cw_out.at[s], send_sem=cws.at[j], recv_sem=cwr.at[j],
                device_id=(x, y, c), device_id_type=MESH).wait_recv()
        for cp in first + passed:
            cp.wait_send()
        for cp in own:
            cp.wait()

    outs = [jax.ShapeDtypeStruct((NSH,) + a.shape, a.dtype) for a in ins + [cw]]
    return pl.pallas_call(
        body, name="gather_weights", in_specs=_any_specs(nb + 1), out_specs=_any_specs(nb + 1), out_shape=outs,
        scratch_shapes=[pltpu.SemaphoreType.DMA((3 * nb,)), pltpu.SemaphoreType.DMA((3 * nb,)),
                        pltpu.SemaphoreType.DMA((3 * nb,)), pltpu.SemaphoreType.DMA((3 * nb,)),
                        pltpu.SemaphoreType.DMA((nb + 1,)), pltpu.SemaphoreType.DMA((3,)),
                        pltpu.SemaphoreType.DMA((3,))],
    )(*ins, cw)


def _to_sibling(ps):
    n = len(ps)

    def body(*refs):
        src, dst, send, recv = refs[:n], refs[n:2 * n], refs[2 * n], refs[2 * n + 1]
        x, y, c, _ = _place()
        cps = [pltpu.make_async_remote_copy(
            src_ref=src[i].at[:, 1 - c], dst_ref=dst[i], send_sem=send.at[i], recv_sem=recv.at[i],
            device_id=(x, y, 1 - c), device_id_type=MESH) for i in range(n)]
        for cp in cps:
            cp.start()
        for cp in cps:
            cp.wait()

    outs = [jax.ShapeDtypeStruct((NSH,) + p.shape[2:], p.dtype) for p in ps]
    return pl.pallas_call(
        body, name="grads_to_sibling", in_specs=_any_specs(n), out_specs=_any_specs(n), out_shape=outs,
        scratch_shapes=[pltpu.SemaphoreType.DMA((n,)), pltpu.SemaphoreType.DMA((n,))],
    )(*ps)


def _to_chips(cs):
    n = len(cs)

    def body(*refs):
        src, dst, send, recv, local = refs[:n], refs[n:2 * n], refs[2 * n], refs[2 * n + 1], refs[2 * n + 2]
        x, y, c, chips = _place()
        me = 2 * x + y
        own = [pltpu.make_async_copy(src[i].at[me], dst[i].at[me], local.at[i]) for i in range(n)]
        for cp in own:
            cp.start()
        cps = []
        for j, chip in enumerate(chips):
            s = 2 * chip[0] + chip[1]
            for i in range(n):
                cps.append(pltpu.make_async_remote_copy(
                    src_ref=src[i].at[s], dst_ref=dst[i].at[me], send_sem=send.at[j * n + i],
                    recv_sem=recv.at[j * n + i], device_id=(chip[0], chip[1], c), device_id_type=MESH))
        for cp in cps:
            cp.start()
        for j, chip in enumerate(chips):
            s = 2 * chip[0] + chip[1]
            for i in range(n):
                pltpu.make_async_remote_copy(
                    src_ref=src[i].at[s], dst_ref=dst[i].at[s], send_sem=send.at[j * n + i],
                    recv_sem=recv.at[j * n + i], device_id=(x, y, c), device_id_type=MESH).wait_recv()
        for cp in cps:
            cp.wait_send()
        for cp in own:
            cp.wait()

    return pl.pallas_call(
        body, name="grads_to_chips", in_specs=_any_specs(n), out_specs=_any_specs(n),
        out_shape=[jax.ShapeDtypeStruct(a.shape, a.dtype) for a in cs],
        scratch_shapes=[pltpu.SemaphoreType.DMA((3 * n,)), pltpu.SemaphoreType.DMA((3 * n,)),
                        pltpu.SemaphoreType.DMA((n,))],
    )(*cs)


def _swap_halves(gs):
    n = len(gs)

    def body(*refs):
        src, dst, send, recv, local = refs[:n], refs[n:2 * n], refs[2 * n], refs[2 * n + 1], refs[2 * n + 2]
        x, y, c, _ = _place()
        own = [pltpu.make_async_copy(src[i], dst[i].at[c], local.at[i]) for i in range(n)]
        for cp in own:
            cp.start()
        cps = [pltpu.make_async_remote_copy(
            src_ref=src[i], dst_ref=dst[i].at[c], send_sem=send.at[i], recv_sem=recv.at[i],
            device_id=(x, y, 1 - c), device_id_type=MESH) for i in range(n)]
        for cp in cps:
            cp.start()
        for i in range(n):
            pltpu.make_async_remote_copy(
                src_ref=src[i], dst_ref=dst[i].at[1 - c], send_sem=send.at[i], recv_sem=recv.at[i],
                device_id=(x, y, c), device_id_type=MESH).wait_recv()
        for cp in cps:
            cp.wait_send()
        for cp in own:
            cp.wait()

    return pl.pallas_call(
        body, name="grads_swap_halves", in_specs=_any_specs(n), out_specs=_any_specs(n),
        out_shape=[jax.ShapeDtypeStruct((2,) + g.shape, g.dtype) for g in gs],
        scratch_shapes=[pltpu.SemaphoreType.DMA((n,)), pltpu.SemaphoreType.DMA((n,)),
                        pltpu.SemaphoreType.DMA((n,))],
    )(*gs)


SMALL_ROWS = 16


def _allreduce_small(vec):
    def body(v_ref, o_ref, buf, send, recv):
        x, y, c, _ = _place()
        me = 4 * x + 2 * y + c
        buf[me] = v_ref[...]
        cps = []
        for k in range(1, 8):
            peer = (x ^ (k >> 2), y ^ ((k >> 1) & 1), c ^ (k & 1))
            cps.append(pltpu.make_async_remote_copy(
                src_ref=v_ref, dst_ref=buf.at[me], send_sem=send.at[k - 1], recv_sem=recv.at[k - 1],
                device_id=peer, device_id_type=MESH))
        for cp in cps:
            cp.start()
        for k in range(1, 8):
            pltpu.make_async_remote_copy(
                src_ref=v_ref, dst_ref=buf.at[me ^ k], send_sem=send.at[k - 1], recv_sem=recv.at[k - 1],
                device_id=(x, y, c), device_id_type=MESH).wait_recv()
        for cp in cps:
            cp.wait_send()
        t = buf[0]
        for d in range(1, 8):
            t = t + buf[d]
        o_ref[...] = t

    return pl.pallas_call(
        body, name="allreduce_small",
        in_specs=[pl.BlockSpec(memory_space=pltpu.VMEM)], out_specs=pl.BlockSpec(memory_space=pltpu.VMEM),
        out_shape=jax.ShapeDtypeStruct((SMALL_ROWS, D), F32),
        scratch_shapes=[pltpu.VMEM((8, SMALL_ROWS, D), F32), pltpu.SemaphoreType.DMA((7,)),
                        pltpu.SemaphoreType.DMA((7,))],
    )(vec)


ROPE_THETA = 10000.0
SMALL_1K = ("ffn1_pre_norm", "ffn1_post_norm", "mix_pre_norm", "ssm_norm", "mix_post_norm",
            "ffn2_pre_norm", "ffn2_post_norm")
SMALL_16 = ("dt_bias", "a_log", "d_skip")
OFF_CONVB = 7 * D
OFF_16 = OFF_CONVB + CONV_C
OFF_CONVW = OFF_16 + 48
OFF_LOSS = OFF_CONVW + CONV_K * CONV_C
SMALL_LEN = SMALL_ROWS * D


def _sds(shape, dtype):
    return jax.ShapeDtypeStruct(shape, dtype)


def _ffn_down(name, act, wdw, wi):
    return _mm(name, [act, wdw], NN, (S // TS, NSH),
               [pl.BlockSpec((None, TS, FS), lambda i, s: (s, i, 0)),
                pl.BlockSpec((None, None, FS, D), lambda i, s: (s, wi, 0, 0))],
               pl.BlockSpec((TS, D), lambda i, s: (i, 0)), _sds((S, D), F32), (TS, D))


def _ffn_bwd(tag, dh, n, gate, up, act, colw, wdw, ci, wi):
    dgate, dup = _ffn_dact(tag + "_dact", dh, wdw, wi, gate, up)
    aspec = pl.BlockSpec((None, TS, FS), lambda s, k: (s, k, 0))
    nspec = pl.BlockSpec((TS, D), lambda s, k: (k, 0))
    dwd = _mm(tag + "_dwd", [act, dh], TN, (NSH, S // TS), [aspec, nspec],
              pl.BlockSpec((None, FS, D), lambda s, k: (s, 0, 0)), _sds((NSH, FS, D), BF16), (FS, D))
    wspec = pl.BlockSpec((None, D, FS), lambda s, k: (s, 0, 0))
    dwg = _mm(tag + "_dwg", [n, dgate], TN, (NSH, S // TS), [nspec, aspec], wspec, _sds((NSH, D, FS), BF16), (D, FS))
    dwu = _mm(tag + "_dwu", [n, dup], TN, (NSH, S // TS), [nspec, aspec], wspec, _sds((NSH, D, FS), BF16), (D, FS))
    a2 = pl.BlockSpec((None, TS, FS), lambda i, s: (s, i, 0))
    dn = _mm(tag + "_dn", [dgate, colw, dup, colw], NT, (S // TS, NSH),
             [a2, pl.BlockSpec((None, None, D, FS), lambda i, s: (s, ci, 0, 0)),
              a2, pl.BlockSpec((None, None, D, FS), lambda i, s: (s, ci + 1, 0, 0))],
             pl.BlockSpec((TS, D), lambda i, s: (i, 0)), _sds((S, D), F32), (TS, D))
    return dn, dwg, dwu, dwd


def _heads(t, n):
    return t.reshape(S, n, HD).transpose(1, 0, 2)


def _unheads(t):
    return t.transpose(1, 0, 2).reshape(S, t.shape[0] * HD)


def _row_stats(t):
    return t.reshape(NKV, NQ_PER_KV, NCH, 128).transpose(0, 2, 1, 3).reshape(NKV, NCH, 1, QROWS)


def _pad128(v):
    return jnp.pad(v, ((0, 0), (0, 128 - v.shape[1])))


def _local_step(x, positions, tgt, sp, colw, wdw, win_pad, wout, convw):
    inv_freq = ROPE_THETA ** (-jnp.arange(0, HD, 2, dtype=F32) / HD)
    ang = positions.astype(F32)[:, None] * inv_freq
    ang = jnp.concatenate([ang, ang, ang, ang], axis=-1)
    cos, sin = jnp.cos(ang), jnp.sin(ang)
    dtb, alog = _pad128(sp["dt_bias"]), _pad128(sp["a_log"])
    dskip_l = jnp.repeat(sp["d_skip"], HD, axis=1)
    convb = sp["conv_b"]

    n1 = _prenorm("prenorm1", x, sp["ffn1_pre_norm"])
    gate1, up1, act1 = _ffn_up("ffn1_up", n1, colw, 0)
    h1 = _ffn_down("ffn1_down", act1, wdw, 0)
    x1, n2 = _postres("postres1", x, h1, sp["ffn1_post_norm"], 0.5, sp["mix_pre_norm"])

    pw = WIN_PAD // 3
    proj = _mm("in_proj", [n2, win_pad], NN, (S // TS, 3, 1),
               [pl.BlockSpec((TS, D), lambda i, j, k: (i, 0)), pl.BlockSpec((D, pw), lambda i, j, k: (0, j))],
               pl.BlockSpec((TS, pw), lambda i, j, k: (i, j)), _sds((S, WIN_PAD), F32), (TS, pw))
    q_rot = _rope("rope_q", proj, 0, D, cos, sin, 1.0, HD ** -0.5)
    k_rot = _rope("rope_k", proj, D // KVW, KVW, cos, sin, 1.0, 1.0)
    qh = q_rot.reshape(S, NKV, NQ_PER_KV, HD).transpose(1, 2, 0, 3)
    kh = _heads(k_rot, NKV)
    vh = _heads(proj[:, D + KVW:D + 2 * KVW].astype(BF16), NKV)
    oh, lse = _attn_fwd(qh, kh, vh)
    attn = oh.transpose(2, 0, 1, 3).reshape(S, D).astype(BF16)
    xbc = _conv_fwd(proj, convw, convb)
    y, yn, hprev = _ssd_fwd(xbc, proj, dtb, alog, dskip_l, sp["ssm_norm"])
    mixed = jnp.concatenate([attn, yn], axis=1)
    h2 = _mm("out_proj", [mixed, wout], NN, (S // TS, 1),
             [pl.BlockSpec((TS, 2 * D), lambda i, k: (i, 0)), pl.BlockSpec((2 * D, D), lambda i, k: (0, 0))],
             pl.BlockSpec((TS, D), lambda i, k: (i, 0)), _sds((S, D), F32), (TS, D))
    x2, n3 = _postres("postres2", x1, h2, sp["mix_post_norm"], 1.0, sp["ffn2_pre_norm"])

    gate2, up2, act2 = _ffn_up("ffn2_up", n3, colw, 2)
    h3 = _ffn_down("ffn2_down", act2, wdw, 1)
    dy, dh3, dp3, loss = _final(x2, h3, sp["ffn2_post_norm"], tgt, 0.5)

    dn3, dwg2, dwu2, dwd2 = _ffn_bwd("ffn2", dh3, n3, gate2, up2, act2, colw, wdw, 2, 1)
    dx2, dh2, dg3, dp2 = _mid_bwd("mid_bwd2", dy, dn3, x2, sp["ffn2_pre_norm"], h2, sp["mix_post_norm"], 1.0)

    dmixed = _mm("out_proj_dx", [dh2, wout], NT, (S // TS, 1),
                 [pl.BlockSpec((TS, D), lambda i, k: (i, 0)), pl.BlockSpec((2 * D, D), lambda i, k: (0, 0))],
                 pl.BlockSpec((TS, 2 * D), lambda i, k: (i, 0)), _sds((S, 2 * D), F32), (TS, 2 * D))
    dwout = _mm("out_proj_dw", [mixed, dh2], TN, (2, S // TS),
                [pl.BlockSpec((TS, D), lambda m, k: (k, m)), pl.BlockSpec((TS, D), lambda m, k: (k, 0))],
                pl.BlockSpec((D, D), lambda m, k: (m, 0)), _sds((2 * D, D), BF16), (D, D))
    dxbc, dz, ddt, dssm, dsc = _ssd_bwd(dmixed, y, xbc, proj, hprev, dtb, alog, dskip_l, sp["ssm_norm"])
    du, dcw8, dcb = _conv_bwd(dxbc, proj, convw, convb)
    doh = dmixed[:, :D].reshape(S, NKV, NQ_PER_KV, HD).transpose(1, 2, 0, 3)
    dqh, delta, dob = _attn_dq(qh, kh, vh, doh, oh, lse)
    dkh, dvh = _attn_dkv(qh, kh, vh, dob, _row_stats(lse), _row_stats(delta))
    dq = _rope("rope_dq", dqh.transpose(2, 0, 1, 3).reshape(S, D), 0, D, cos, sin, -1.0, HD ** -0.5)
    dk = _rope("rope_dk", _unheads(dkh), 0, KVW, cos, sin, -1.0, 1.0)
    dproj = jnp.concatenate([dq, dk, _unheads(dvh).astype(BF16), du, dz, ddt], axis=1)
    dn2 = _mm("in_proj_dx", [dproj, win_pad], NT, (S // TS, 3),
              [pl.BlockSpec((TS, pw), lambda i, k: (i, k)), pl.BlockSpec((D, pw), lambda i, k: (0, k))],
              pl.BlockSpec((TS, D), lambda i, k: (i, 0)), _sds((S, D), F32), (TS, D))
    dwin = _mm("in_proj_dw", [n2, dproj], TN, (3, S // TS),
               [pl.BlockSpec((TS, D), lambda j, k: (k, 0)), pl.BlockSpec((TS, pw), lambda j, k: (k, j))],
               pl.BlockSpec((D, pw), lambda j, k: (0, j)), _sds((D, WIN_PAD), BF16), (D, pw))
    dx1, dh1, dg2, dp1 = _mid_bwd("mid_bwd1", dx2, dn2, x1, sp["mix_pre_norm"], h1, sp["ffn1_post_norm"], 0.5)

    dn1, dwg1, dwu1, dwd1 = _ffn_bwd("ffn1", dh1, n1, gate1, up1, act1, colw, wdw, 0, 0)
    grad_x, dg1 = _first_bwd(dx1, dn1, x, sp["ffn1_pre_norm"])

    small = jnp.concatenate([
        dg1[0], dp1[0], dg2[0], dssm[0], dp2[0], dg3[0], dp3[0], dcb[0],
        dsc[0, :16], dsc[1, :16], dsc[2, :16], dcw8[:CONV_K].reshape(-1), loss[0, :1]])
    small = jnp.pad(small, (0, SMALL_LEN - small.shape[0])).reshape(SMALL_ROWS, D)
    dwin_sh = dwin[:, :WIN_COLS].reshape(D, NSH, WIN_SH).transpose(1, 0, 2)
    partials = [dwg1, dwu1, dwg2, dwu2, dwd1, dwd2, dwin_sh, dwout.reshape(NSH, 2 * D // NSH, D)]
    return grad_x, partials, small


WEIGHTS = ("ffn1_pre_norm", "ffn1_w_gate", "ffn1_w_up", "ffn1_w_down", "ffn1_post_norm", "mix_pre_norm", "w_in",
           "conv_w", "conv_b", "dt_bias", "a_log", "d_skip", "ssm_norm", "w_out", "mix_post_norm", "ffn2_pre_norm",
           "ffn2_w_gate", "ffn2_w_up", "ffn2_w_down", "ffn2_post_norm")
BIG = ("ffn1_w_gate", "ffn1_w_up", "ffn2_w_gate", "ffn2_w_up", "ffn1_w_down", "ffn2_w_down", "w_in", "w_out")
SMALL_ORDER = SMALL_1K + ("conv_b",) + SMALL_16
CONVW_SH = CONV_C // NSH


def _pack_small(d, prefix, shard_of_convw):
    flat = jnp.concatenate([d[prefix + n][0] for n in SMALL_ORDER] + [shard_of_convw.reshape(-1)])
    return jnp.pad(flat, (0, SMALL_LEN - flat.shape[0])).reshape(SMALL_ROWS, D)


def _unpack_small(block, like):
    flat = block.reshape(-1)
    out, off = {}, 0
    for n in SMALL_ORDER:
        size = like[n].shape[1]
        out[n] = flat[off:off + size].reshape(1, size)
        off += size
    out["conv_w"] = flat[off:off + CONV_K * CONVW_SH].reshape(1, CONV_K, CONVW_SH)
    return out


def kernel(x, positions, ffn1_pre_norm, ffn1_w_gate, ffn1_w_up, ffn1_w_down, ffn1_post_norm, mix_pre_norm, w_in, conv_w, conv_b, dt_bias, a_log, d_skip, ssm_norm, w_out, mix_post_norm, ffn2_pre_norm, ffn2_w_gate, ffn2_w_up, ffn2_w_down, ffn2_post_norm, loss_target, m_ffn1_pre_norm, m_ffn1_w_gate, m_ffn1_w_up, m_ffn1_w_down, m_ffn1_post_norm, m_mix_pre_norm, m_w_in, m_conv_w, m_conv_b, m_dt_bias, m_a_log, m_d_skip, m_ssm_norm, m_w_out, m_mix_post_norm, m_ffn2_pre_norm, m_ffn2_w_gate, m_ffn2_w_up, m_ffn2_w_down, m_ffn2_post_norm, v_ffn1_pre_norm, v_ffn1_w_gate, v_ffn1_w_up, v_ffn1_w_down, v_ffn1_post_norm, v_mix_pre_norm, v_w_in, v_conv_w, v_conv_b, v_dt_bias, v_a_log, v_d_skip, v_ssm_norm, v_w_out, v_mix_post_norm, v_ffn2_pre_norm, v_ffn2_w_gate, v_ffn2_w_up, v_ffn2_w_down, v_ffn2_post_norm):
    given = dict(locals())
    xi, yi, ci = lax.axis_index("x"), lax.axis_index("y"), lax.axis_index("c")

    colsh = _cast_stack("cast_gate_up", [ffn1_w_gate[0], ffn1_w_up[0], ffn2_w_gate[0], ffn2_w_up[0]], 256)
    wdsh = _cast_stack("cast_down", [ffn1_w_down[0], ffn2_w_down[0]], 352)
    winsh = _cast_stack("cast_w_in", [w_in[0]], 256)[0]
    woutsh = _cast_stack("cast_w_out", [w_out[0]], 256)[0]
    colw, wdw, winf, woutf, cwf = _gather_weights(colsh, wdsh, winsh, woutsh, conv_w[0])
    win_pad = jnp.pad(winf.transpose(1, 0, 2).reshape(D, WIN_COLS), ((0, 0), (0, WIN_PAD - WIN_COLS)))
    convw = cwf.transpose(1, 0, 2).reshape(CONV_K, CONV_C)

    sp = {n: given[n] for n in SMALL_ORDER}
    grad_x, partials, small = _local_step(x[0], positions[0], loss_target[0], sp, colw, wdw, win_pad,
                                          woutf.reshape(2 * D, D), convw)

    ps = [p.reshape(NSH, 2, p.shape[1] // 2, p.shape[2]) for p in partials]
    rs = _to_sibling(ps)
    c_idx = jnp.reshape(ci, (1,)).astype(jnp.int32)
    groups = ((0, 4, 256), (4, 6, 176), (6, 7, 256), (7, 8, 256))
    csums = []
    for a, b, th in groups:
        csums += _pair_sum("pair_sum_%d" % a, c_idx, ps[a:b], rs[a:b], th)
    ts = _to_chips(csums)
    halves = []
    for a, b, th in groups:
        halves += _chip_sum("chip_sum_%d" % a, ts[a:b], th)
    full = _swap_halves(halves)
    big_grads = {n: g.reshape(given[n].shape[1:]) for n, g in zip(BIG, full)}

    tot = _allreduce_small(small).reshape(-1)
    loss = tot[OFF_LOSS]
    small_grads, off = {}, 0
    for n in SMALL_ORDER:
        size = given[n].shape[1]
        small_grads[n] = tot[off:off + size].reshape(1, size)
        off += size
    dconvw = tot[OFF_CONVW:OFF_CONVW + CONV_K * CONV_C].reshape(CONV_K, NSH, CONVW_SH)
    dconvw = lax.dynamic_index_in_dim(dconvw, 2 * xi + yi, axis=1, keepdims=False)
    small_grads["conv_w"] = dconvw.reshape(1, CONV_K, CONVW_SH)

    upd = {}
    for names, tr in ((BIG[0:4], 256), (BIG[4:6], 176), (BIG[6:7], 256), (BIG[7:8], 256)):
        res = _adamw("adamw_" + names[0], [given[n][0] for n in names], [big_grads[n] for n in names],
                     [given["m_" + n][0] for n in names], [given["v_" + n][0] for n in names], tr)
        for n, (dl, m2, v2) in zip(names, res):
            upd[n] = (dl[None], m2[None], v2[None])
    (dl, m2, v2), = _adamw(
        "adamw_small", [_pack_small(given, "", conv_w[0])], [_pack_small(small_grads, "", dconvw)],
        [_pack_small(given, "m_", m_conv_w[0])], [_pack_small(given, "v_", v_conv_w[0])], SMALL_ROWS)
    dl, m2, v2 = (_unpack_small(t, given) for t in (dl, m2, v2))
    for n in SMALL_ORDER + ("conv_w",):
        upd[n] = (dl[n], m2[n], v2[n])

    grads = dict(small_grads)
    grads.update({n: g[None] for n, g in big_grads.items()})
    return (loss, grad_x[None], *[grads[n] for n in WEIGHTS], *[upd[n][0] for n in WEIGHTS],
            *[upd[n][1] for n in WEIGHTS], *[upd[n][2] for n in WEIGHTS])
```

```python
import functools
import math

import jax
import jax.numpy as jnp
from jax import lax
from jax.experimental import pallas as pl
from jax.experimental.pallas import tpu as pltpu

F32 = jnp.float32
BF16 = jnp.bfloat16

S = 2048
D = 1024
FF = 2816
NSH = 4
FS = FF // NSH
HD = 64
NKV = 4
NQ_PER_KV = 4
KVW = NKV * HD
CONV_C = 1536
CONV_K = 4
SSM_W = 1024
NST = 128
NCH = S // 128
WIN_COLS = 4112
WIN_SH = WIN_COLS // NSH
WIN_PAD = 4224
COL_DT = 4096
EPS = 1e-6
NEG = -1e30

ADAM_LR = 0.001
ADAM_B1 = 0.9
ADAM_B2 = 0.999
ADAM_EPS = 1e-08
ADAM_WD = 0.01
ADAM_STEP = 10

VMEM_LIMIT = 56 * 1024 * 1024
TS = 512
TR = 256

NN = (((1,), (0,)), ((), ()))
NT = (((1,), (1,)), ((), ()))
TN = (((0,), (0,)), ((), ()))
MESH = pl.DeviceIdType.MESH


def _cparams(*sem):
    return pltpu.CompilerParams(dimension_semantics=sem, vmem_limit_bytes=VMEM_LIMIT)


def _dot(a, b, dims):
    return lax.dot_general(a.astype(BF16), b.astype(BF16), dims, preferred_element_type=F32)


def _dot_exact(a, b):
    return lax.dot_general(a, b, NN, precision=lax.Precision.HIGHEST, preferred_element_type=F32)


def _sigmoid(v):
    return 1.0 / (1.0 + jnp.exp(-v))


def _mm(name, operands, dims, grid, in_specs, o_spec, out_shape, acc_shape):
    npairs = len(operands) // 2
    nk = grid[-1]
    kaxis = len(grid) - 1

    def body(*refs):
        o_ref, acc = refs[2 * npairs], refs[2 * npairs + 1]
        k = pl.program_id(kaxis)

        @pl.when(k == 0)
        def _():
            acc[...] = jnp.zeros_like(acc)

        t = None
        for i in range(npairs):
            d = _dot(refs[2 * i][...], refs[2 * i + 1][...], dims)
            t = d if t is None else t + d
        acc[...] += t

        @pl.when(k == nk - 1)
        def _():
            o_ref[...] = acc[...].astype(o_ref.dtype)

    return pl.pallas_call(
        body, name=name, grid=grid, in_specs=in_specs, out_specs=o_spec, out_shape=out_shape,
        scratch_shapes=[pltpu.VMEM(acc_shape, F32)],
        compiler_params=_cparams(*(("parallel",) * kaxis + ("arbitrary",))),
    )(*operands)


def _ffn_up(name, n, colw, wi):
    def body(n_ref, wg_ref, wu_ref, g_ref, u_ref, a_ref):
        nb = n_ref[...]
        g = _dot(nb, wg_ref[...], NN)
        u = _dot(nb, wu_ref[...], NN)
        g_ref[...] = g.astype(BF16)
        u_ref[...] = u.astype(BF16)
        a_ref[...] = (g * _sigmoid(g) * u).astype(BF16)

    out = jax.ShapeDtypeStruct((NSH, S, FS), BF16)
    ospec = pl.BlockSpec((None, TS, FS), lambda s, i: (s, i, 0))
    return pl.pallas_call(
        body, name=name, grid=(NSH, S // TS),
        in_specs=[pl.BlockSpec((TS, D), lambda s, i: (i, 0)),
                  pl.BlockSpec((None, None, D, FS), lambda s, i: (s, wi, 0, 0)),
                  pl.BlockSpec((None, None, D, FS), lambda s, i: (s, wi + 1, 0, 0))],
        out_specs=[ospec, ospec, ospec], out_shape=[out, out, out],
        compiler_params=_cparams("parallel", "parallel"),
    )(n, colw, colw)


def _ffn_dact(name, dh, wdw, wi, gate, up):
    def body(dh_ref, wd_ref, g_ref, u_ref, dg_ref, du_ref):
        da = _dot(dh_ref[...], wd_ref[...], NT)
        g = g_ref[...].astype(F32)
        u = u_ref[...].astype(F32)
        sg = _sigmoid(g)
        dg_ref[...] = (da * u * (sg * (1.0 + g * (1.0 - sg)))).astype(BF16)
        du_ref[...] = (da * (g * sg)).astype(BF16)

    out = jax.ShapeDtypeStruct((NSH, S, FS), BF16)
    aspec = pl.BlockSpec((None, TS, FS), lambda s, i: (s, i, 0))
    return pl.pallas_call(
        body, name=name, grid=(NSH, S // TS),
        in_specs=[pl.BlockSpec((TS, D), lambda s, i: (i, 0)),
                  pl.BlockSpec((None, None, FS, D), lambda s, i: (s, wi, 0, 0)),
                  aspec, aspec],
        out_specs=[aspec, aspec], out_shape=[out, out],
        compiler_params=_cparams("parallel", "parallel"),
    )(dh, wdw, gate, up)


def _rstd(v):
    return lax.rsqrt(jnp.mean(v * v, axis=-1, keepdims=True) + EPS)


def _row_spec():
    return pl.BlockSpec((TR, D), lambda i: (i, 0))


def _vec_spec():
    return pl.BlockSpec((1, D), lambda i: (0, 0))


def _acc_rows(ref, v):
    @pl.when(pl.program_id(0) == 0)
    def _():
        ref[...] = jnp.zeros_like(ref)
    ref[...] += jnp.sum(v, axis=0, keepdims=True)


def _prenorm(name, x, g):
    def body(x_ref, g_ref, n_ref):
        xv = x_ref[...]
        n_ref[...] = (xv * _rstd(xv) * g_ref[...]).astype(BF16)

    return pl.pallas_call(
        body, name=name, grid=(S // TR,), in_specs=[_row_spec(), _vec_spec()], out_specs=_row_spec(),
        out_shape=jax.ShapeDtypeStruct((S, D), BF16), compiler_params=_cparams("parallel"),
    )(x, g)


def _postres(name, x, h, p, alpha, gnext):
    def body(x_ref, h_ref, p_ref, g_ref, xo_ref, n_ref):
        hv = h_ref[...]
        xo = x_ref[...] + alpha * (hv * _rstd(hv) * p_ref[...])
        xo_ref[...] = xo
        n_ref[...] = (xo * _rstd(xo) * g_ref[...]).astype(BF16)

    return pl.pallas_call(
        body, name=name, grid=(S // TR,),
        in_specs=[_row_spec(), _row_spec(), _vec_spec(), _vec_spec()],
        out_specs=[_row_spec(), _row_spec()],
        out_shape=[jax.ShapeDtypeStruct((S, D), F32), jax.ShapeDtypeStruct((S, D), BF16)],
        compiler_params=_cparams("parallel"),
    )(x, h, p, gnext)


def _final(x, h, p, tgt, alpha):
    def body(x_ref, h_ref, p_ref, t_ref, dy_ref, dh_ref, dp_ref, loss_ref):
        hv = h_ref[...]
        r = _rstd(hv)
        hn = hv * r
        pv = p_ref[...]
        e = x_ref[...] + alpha * (hn * pv) - t_ref[...]
        dy = e * (1.0 / D)
        dy_ref[...] = dy
        du = alpha * dy * pv
        dh_ref[...] = (r * (du - hn * jnp.mean(du * hn, axis=-1, keepdims=True))).astype(BF16)
        _acc_rows(dp_ref, alpha * dy * hn)
        part = 0.5 * jnp.sum(jnp.mean(e * e, axis=-1, keepdims=True), axis=0, keepdims=True)
        _acc_rows(loss_ref, jnp.broadcast_to(part, (1, 128)))

    return pl.pallas_call(
        body, name="loss_head", grid=(S // TR,),
        in_specs=[_row_spec(), _row_spec(), _vec_spec(), _row_spec()],
        out_specs=[_row_spec(), _row_spec(), _vec_spec(), pl.BlockSpec((1, 128), lambda i: (0, 0))],
        out_shape=[jax.ShapeDtypeStruct((S, D), F32), jax.ShapeDtypeStruct((S, D), BF16),
                   jax.ShapeDtypeStruct((1, D), F32), jax.ShapeDtypeStruct((1, 128), F32)],
        compiler_params=_cparams("arbitrary"),
    )(x, h, p, tgt)


def _mid_bwd(name, dres, dn, x, g, h, p, alpha):
    def body(dr_ref, dn_ref, x_ref, g_ref, h_ref, p_ref, dx_ref, dh_ref, dg_ref, dp_ref):
        xv = x_ref[...]
        xn = xv * _rstd(xv)
        dnv = dn_ref[...]
        dng = dnv * g_ref[...]
        dx = dr_ref[...] + _rstd(xv) * (dng - xn * jnp.mean(dng * xn, axis=-1, keepdims=True))
        dx_ref[...] = dx
        _acc_rows(dg_ref, dnv * xn)
        hv = h_ref[...]
        r = _rstd(hv)
        hn = hv * r
        du = alpha * dx * p_ref[...]
        dh_ref[...] = (r * (du - hn * jnp.mean(du * hn, axis=-1, keepdims=True))).astype(BF16)
        _acc_rows(dp_ref, alpha * dx * hn)

    return pl.pallas_call(
        body, name=name, grid=(S // TR,),
        in_specs=[_row_spec(), _row_spec(), _row_spec(), _vec_spec(), _row_spec(), _vec_spec()],
        out_specs=[_row_spec(), _row_spec(), _vec_spec(), _vec_spec()],
        out_shape=[jax.ShapeDtypeStruct((S, D), F32), jax.ShapeDtypeStruct((S, D), BF16),
                   jax.ShapeDtypeStruct((1, D), F32), jax.ShapeDtypeStruct((1, D), F32)],
        compiler_params=_cparams("arbitrary"),
    )(dres, dn, x, g, h, p)


def _first_bwd(dres, dn, x, g):
    def body(dr_ref, dn_ref, x_ref, g_ref, dx_ref, dg_ref):
        xv = x_ref[...]
        r = _rstd(xv)
        xn = xv * r
        dnv = dn_ref[...]
        dng = dnv * g_ref[...]
        dx_ref[...] = dr_ref[...] + r * (dng - xn * jnp.mean(dng * xn, axis=-1, keepdims=True))
        _acc_rows(dg_ref, dnv * xn)

    return pl.pallas_call(
        body, name="first_bwd", grid=(S // TR,),
        in_specs=[_row_spec(), _row_spec(), _row_spec(), _vec_spec()],
        out_specs=[_row_spec(), _vec_spec()],
        out_shape=[jax.ShapeDtypeStruct((S, D), F32), jax.ShapeDtypeStruct((1, D), F32)],
        compiler_params=_cparams("arbitrary"),
    )(dres, dn, x, g)


def _rope(name, src, col_block, width, cos, sin, sign, scale):
    def body(t_ref, c_ref, s_ref, o_ref):
        t = t_ref[...].astype(F32)
        c = jnp.tile(c_ref[...], (1, width // 128))
        sn = jnp.tile(s_ref[...], (1, width // 128))
        lane = lax.broadcasted_iota(jnp.int32, t.shape, 1) & (HD - 1)
        rot = jnp.where(lane < HD // 2, -pltpu.roll(t, width - HD // 2, 1), pltpu.roll(t, HD // 2, 1))
        o_ref[...] = ((t * c + sign * (rot * sn)) * scale).astype(BF16)

    return pl.pallas_call(
        body, name=name, grid=(S // TR,),
        in_specs=[pl.BlockSpec((TR, width), lambda i: (i, col_block)),
                  pl.BlockSpec((TR, 128), lambda i: (i, 0)), pl.BlockSpec((TR, 128), lambda i: (i, 0))],
        out_specs=pl.BlockSpec((TR, width), lambda i: (i, 0)),
        out_shape=jax.ShapeDtypeStruct((S, width), BF16), compiler_params=_cparams("parallel"),
    )(src, cos, sin)


QROWS = NQ_PER_KV * 128


def _bias_table():
    db = lax.broadcasted_iota(jnp.int32, (NCH, 128, QROWS), 0)
    ki = lax.broadcasted_iota(jnp.int32, (NCH, 128, QROWS), 1)
    qi = lax.broadcasted_iota(jnp.int32, (NCH, 128, QROWS), 2) & 127
    d = db * 128 + qi - ki
    cnt = ((d <= 128).astype(F32) + (((d & 3) == 0) & (d <= 512)).astype(F32) + ((d & 15) == 0).astype(F32))
    return jnp.where((d >= 0) & (cnt > 0.0), jnp.log(jnp.maximum(cnt, 1.0)), NEG)


def _qt_spec():
    return pl.BlockSpec((None, None, HD, QROWS), lambda j, i: (j, i, 0, 0))


def _stat_spec():
    return pl.BlockSpec((None, None, 1, QROWS), lambda j, i: (j, i, 0, 0))


def _attn_fwd(qt, kh, vt, bias):
    def body(q_ref, k_ref, v_ref, b_ref, o_ref, lse_ref):
        qb = pl.program_id(1)
        q = q_ref[...]

        def step(kb, carry):
            m, l, acc = carry
            off = pl.multiple_of(kb * 128, 128)
            s = _dot(k_ref[pl.ds(off, 128), :], q, NN) + b_ref[qb - kb]
            m_new = jnp.maximum(m, jnp.max(s, axis=0, keepdims=True))
            p = jnp.exp(s - m_new)
            a = jnp.exp(m - m_new)
            return (m_new, a * l + jnp.sum(p, axis=0, keepdims=True),
                    a * acc + _dot(v_ref[:, pl.ds(off, 128)], p, NN))

        m, l, acc = lax.fori_loop(
            0, qb + 1, step,
            (jnp.full((1, QROWS), NEG, F32), jnp.zeros((1, QROWS), F32), jnp.zeros((HD, QROWS), F32)))
        o_ref[...] = acc / l
        lse_ref[...] = m + jnp.log(l)

    return pl.pallas_call(
        body, name="attn_fwd", grid=(NKV, NCH),
        in_specs=[_qt_spec(), pl.BlockSpec((None, S, HD), lambda j, i: (j, 0, 0)),
                  pl.BlockSpec((None, HD, S), lambda j, i: (j, 0, 0)),
                  pl.BlockSpec((NCH, 128, QROWS), lambda j, i: (0, 0, 0))],
        out_specs=[_qt_spec(), _stat_spec()],
        out_shape=[jax.ShapeDtypeStruct((NKV, NCH, HD, QROWS), F32),
                   jax.ShapeDtypeStruct((NKV, NCH, 1, QROWS), F32)],
        compiler_params=_cparams("parallel", "parallel"),
    )(qt, kh, vt, bias)


def _attn_delta(ot, dot_):
    def body(o_ref, do_ref, dl_ref):
        dl_ref[...] = jnp.sum(o_ref[...] * do_ref[...].astype(F32), axis=0, keepdims=True)

    return pl.pallas_call(
        body, name="attn_delta", grid=(NKV, NCH), in_specs=[_qt_spec(), _qt_spec()], out_specs=_stat_spec(),
        out_shape=jax.ShapeDtypeStruct((NKV, NCH, 1, QROWS), F32),
        compiler_params=_cparams("parallel", "parallel"),
    )(ot, dot_)


def _attn_bwd(qt, q2, kh, kt, vh, dot_, do2, lse, delta, bias):
    def body(qt_ref, q2_ref, k_ref, kt_ref, v_ref, dot_ref, do2_ref, lse_ref, dl_ref, b_ref, dq_ref, dk_ref, dv_ref):
        kb = pl.program_id(1)

        @pl.when(kb == 0)
        def _():
            dq_ref[...] = jnp.zeros_like(dq_ref)

        k = k_ref[...]
        kt_ = kt_ref[...]
        v = v_ref[...]

        def step(qb, carry):
            dk, dv = carry
            st = _dot(k, qt_ref[qb], NN) + b_ref[qb - kb]
            pt = jnp.exp(st - lse_ref[qb])
            dst = pt * (_dot(v, dot_ref[qb], NN) - dl_ref[qb])
            dq_ref[qb] += _dot(kt_, dst, NN)
            return dk + _dot(dst, q2_ref[qb], NN), dv + _dot(pt, do2_ref[qb], NN)

        dk, dv = lax.fori_loop(kb, NCH, step, (jnp.zeros((128, HD), F32), jnp.zeros((128, HD), F32)))
        dk_ref[...] = dk
        dv_ref[...] = dv

    tspec = pl.BlockSpec((None, NCH, HD, QROWS), lambda j, i: (j, 0, 0, 0))
    rspec = pl.BlockSpec((None, NCH, QROWS, HD), lambda j, i: (j, 0, 0, 0))
    kspec = pl.BlockSpec((None, 128, HD), lambda j, i: (j, i, 0))
    sspec = pl.BlockSpec((None, NCH, 1, QROWS), lambda j, i: (j, 0, 0, 0))
    return pl.pallas_call(
        body, name="attn_bwd", grid=(NKV, NCH),
        in_specs=[tspec, rspec, kspec, pl.BlockSpec((None, HD, 128), lambda j, i: (j, 0, i)), kspec, tspec, rspec,
                  sspec, sspec, pl.BlockSpec((NCH, 128, QROWS), lambda j, i: (0, 0, 0))],
        out_specs=[tspec, kspec, kspec],
        out_shape=[jax.ShapeDtypeStruct((NKV, NCH, HD, QROWS), F32),
                   jax.ShapeDtypeStruct((NKV, S, HD), F32), jax.ShapeDtypeStruct((NKV, S, HD), F32)],
        compiler_params=_cparams("parallel", "arbitrary"),
    )(qt, q2, kh, kt, vh, dot_, do2, lse, delta, bias)


CONV_BLK = 256
CONV_COL0 = 1536 // CONV_BLK


def _shift_down(u, j, row):
    return jnp.where(row >= j, pltpu.roll(u, j, 0), 0.0)


def _conv_pre(u, w_ref, b_ref, row):
    y = b_ref[...] + w_ref[CONV_K - 1:CONV_K, :] * u
    for j in range(1, CONV_K):
        y = y + w_ref[CONV_K - 1 - j:CONV_K - j, :] * _shift_down(u, j, row)
    return y


def _conv_fwd(proj, convw, convb):
    def body(u_ref, w_ref, b_ref, o_ref):
        u = u_ref[...]
        row = lax.broadcasted_iota(jnp.int32, u.shape, 0)
        y = _conv_pre(u, w_ref, b_ref, row)
        o_ref[...] = y * _sigmoid(y)

    return pl.pallas_call(
        body, name="conv_fwd", grid=(CONV_C // CONV_BLK,),
        in_specs=[pl.BlockSpec((S, CONV_BLK), lambda i: (0, CONV_COL0 + i)),
                  pl.BlockSpec((CONV_K, CONV_BLK), lambda i: (0, i)),
                  pl.BlockSpec((1, CONV_BLK), lambda i: (0, i))],
        out_specs=pl.BlockSpec((S, CONV_BLK), lambda i: (0, i)),
        out_shape=jax.ShapeDtypeStruct((S, CONV_C), F32), compiler_params=_cparams("parallel"),
    )(proj, convw, convb)


def _conv_bwd(dact, proj, convw, convb):
    def body(da_ref, u_ref, w_ref, b_ref, du_ref, dw_ref, db_ref):
        u = u_ref[...]
        row = lax.broadcasted_iota(jnp.int32, u.shape, 0)
        y = _conv_pre(u, w_ref, b_ref, row)
        sg = _sigmoid(y)
        dy = da_ref[...] * (sg * (1.0 + y * (1.0 - sg)))
        db_ref[...] = jnp.sum(dy, axis=0, keepdims=True)
        du = w_ref[CONV_K - 1:CONV_K, :] * dy
        r8 = lax.broadcasted_iota(jnp.int32, (8, CONV_BLK), 0)
        dw = jnp.where(r8 == CONV_K - 1, jnp.sum(dy * u, axis=0, keepdims=True), 0.0)
        for j in range(1, CONV_K):
            du = du + w_ref[CONV_K - 1 - j:CONV_K - j, :] * jnp.where(row < S - j, pltpu.roll(dy, S - j, 0), 0.0)
            dw = dw + jnp.where(r8 == CONV_K - 1 - j,
                                jnp.sum(dy * _shift_down(u, j, row), axis=0, keepdims=True), 0.0)
        du_ref[...] = du.astype(BF16)
        dw_ref[...] = dw

    return pl.pallas_call(
        body, name="conv_bwd", grid=(CONV_C // CONV_BLK,),
        in_specs=[pl.BlockSpec((S, CONV_BLK), lambda i: (0, i)),
                  pl.BlockSpec((S, CONV_BLK), lambda i: (0, CONV_COL0 + i)),
                  pl.BlockSpec((CONV_K, CONV_BLK), lambda i: (0, i)),
                  pl.BlockSpec((1, CONV_BLK), lambda i: (0, i))],
        out_specs=[pl.BlockSpec((S, CONV_BLK), lambda i: (0, i)), pl.BlockSpec((8, CONV_BLK), lambda i: (0, i)),
                   pl.BlockSpec((1, CONV_BLK), lambda i: (0, i))],
        out_shape=[jax.ShapeDtypeStruct((S, CONV_C), BF16), jax.ShapeDtypeStruct((8, CONV_C), F32),
                   jax.ShapeDtypeStruct((1, CONV_C), F32)],
        compiler_params=_cparams("parallel"),
    )(dact, proj, convw, convb)


NPAIR = 8


def _ssd_scalars(dtr_ref, dtb_ref, alog_ref):
    z = dtr_ref[...] + dtb_ref[...]
    dt = jnp.maximum(z, 0.0) + jnp.log(1.0 + jnp.exp(-jnp.abs(z)))
    a = -jnp.exp(alog_ref[...])
    r = lax.broadcasted_iota(jnp.int32, (128, 128), 0)
    c = lax.broadcasted_iota(jnp.int32, (128, 128), 1)
    tri = (r >= c).astype(F32)
    cs = _dot_exact(tri, dt * a)
    return z, dt, a, cs, r, c


def _pair_terms(cs, cst, dt, h1, h2, lo):
    c1, c2 = cs[:, h1:h1 + 1], cs[:, h2:h2 + 1]
    l1, l2 = cs[127:128, h1:h1 + 1], cs[127:128, h2:h2 + 1]
    e_l = jnp.where(lo, jnp.exp(c1), jnp.exp(c2))
    dte1, dte2 = jnp.exp(l1 - c1), jnp.exp(l2 - c2)
    dte_l = jnp.where(lo, dte1, dte2)
    dt_l = jnp.where(lo, dt[:, h1:h1 + 1], dt[:, h2:h2 + 1])
    return c1, c2, jnp.exp(l1), jnp.exp(l2), e_l, dte1, dte2, dte_l, dt_l


def _gate_norm(y, zv, w):
    yg = y * (zv * _sigmoid(zv))
    outs, rs = [], []
    for g in range(2):
        blk = yg[:, 512 * g:512 * (g + 1)]
        r = lax.rsqrt(jnp.mean(blk * blk, axis=-1, keepdims=True) + EPS)
        outs.append(blk * r)
        rs.append(r)
    return jnp.concatenate(outs, axis=1), rs, yg


def _ssd_fwd(xbc, proj, dtb, alog, dskip_l, ssmw):
    def body(x_ref, b_ref, c_ref, dtr_ref, z_ref, dtb_ref, alog_ref, dsk_ref, w_ref, y_ref, yn_ref, hp_ref, h_ref):
        @pl.when(pl.program_id(0) == 0)
        def _():
            h_ref[...] = jnp.zeros_like(h_ref)

        _, dt, _, cs, r, c = _ssd_scalars(dtr_ref, dtb_ref, alog_ref)
        cst = cs.T
        causal = r >= c
        lo = c < HD
        hp_ref[...] = h_ref[...]
        for g in range(2):
            bg = b_ref[:, 128 * g:128 * (g + 1)]
            cg = c_ref[:, 128 * g:128 * (g + 1)]
            cb = _dot(cg, bg, NT)
            for j in range(4):
                pj = 4 * g + j
                h1, h2 = 2 * pj, 2 * pj + 1
                sl = slice(128 * pj, 128 * (pj + 1))
                xp = x_ref[:, sl]
                c1, c2, cd1, cd2, e_l, _, _, dte_l, dt_l = _pair_terms(cs, cst, dt, h1, h2, lo)
                xdt = xp * dt_l
                m1 = cb * jnp.exp(jnp.where(causal, c1 - cst[h1:h1 + 1, :], NEG))
                m2 = cb * jnp.exp(jnp.where(causal, c2 - cst[h2:h2 + 1, :], NEG))
                yd = jnp.where(lo, _dot(m1, xdt, NN), _dot(m2, xdt, NN))
                hp = h_ref[pj]
                yo = _dot(cg, hp, NT) * e_l
                st = _dot(xdt * dte_l, bg, TN)
                h_ref[pj] = hp * jnp.where(r < HD, cd1, cd2) + st
                y_ref[:, sl] = yd + yo + dsk_ref[:, sl] * xp
        yn, _, _ = _gate_norm(y_ref[...], z_ref[...], w_ref[...])
        yn_ref[...] = (yn * w_ref[...]).astype(BF16)

    return pl.pallas_call(
        body, name="ssd_fwd", grid=(NCH,),
        in_specs=[pl.BlockSpec((128, SSM_W), lambda i: (i, 0)),
                  pl.BlockSpec((128, 256), lambda i: (i, 4)), pl.BlockSpec((128, 256), lambda i: (i, 5)),
                  pl.BlockSpec((128, 128), lambda i: (i, COL_DT // 128)),
                  pl.BlockSpec((128, SSM_W), lambda i: (i, 3)),
                  pl.BlockSpec((1, 128), lambda i: (0, 0)), pl.BlockSpec((1, 128), lambda i: (0, 0)),
                  pl.BlockSpec((1, SSM_W), lambda i: (0, 0)), pl.BlockSpec((1, SSM_W), lambda i: (0, 0))],
        out_specs=[pl.BlockSpec((128, SSM_W), lambda i: (i, 0)), pl.BlockSpec((128, SSM_W), lambda i: (i, 0)),
                   pl.BlockSpec((None, NPAIR, 128, 128), lambda i: (i, 0, 0, 0))],
        out_shape=[jax.ShapeDtypeStruct((S, SSM_W), F32), jax.ShapeDtypeStruct((S, SSM_W), BF16),
                   jax.ShapeDtypeStruct((NCH, NPAIR, 128, 128), F32)],
        scratch_shapes=[pltpu.VMEM((NPAIR, 128, 128), F32)],
        compiler_params=_cparams("arbitrary"),
    )(xbc, xbc, xbc, proj, proj, dtb, alog, dskip_l, ssmw)


def _ssd_bwd(dmixed, y, xbc, proj, hprev, dtb, alog, dskip_l, ssmw):
    def body(dyn_ref, y_ref, x_ref, b_ref, c_ref, dtr_ref, z_ref, hp_ref, dtb_ref, alog_ref, dsk_ref, w_ref,
             dxbc_ref, dz_ref, ddt_ref, dw_ref, dsc_ref, g_ref):
        @pl.when(pl.program_id(0) == 0)
        def _():
            g_ref[...] = jnp.zeros_like(g_ref)
            dsc_ref[...] = jnp.zeros_like(dsc_ref)

        z, dt, a, cs, r, c = _ssd_scalars(dtr_ref, dtb_ref, alog_ref)
        cst = cs.T
        causal = r >= c
        lo = c < HD

        yv = y_ref[...]
        zv = z_ref[...]
        wv = w_ref[...]
        ygn, rs, yg = _gate_norm(yv, zv, wv)
        dyn = dyn_ref[...]
        _acc_rows(dw_ref, dyn * ygn)
        dynw = dyn * wv
        parts = []
        for g in range(2):
            sl = slice(512 * g, 512 * (g + 1))
            a_g, n_g = dynw[:, sl], ygn[:, sl]
            parts.append(rs[g] * (a_g - n_g * jnp.mean(a_g * n_g, axis=-1, keepdims=True)))
        dyg = jnp.concatenate(parts, axis=1)
        sz = _sigmoid(zv)
        dz_ref[...] = (dyg * yv * (sz * (1.0 + zv * (1.0 - sz)))).astype(BF16)
        dy_all = dyg * (zv * sz)

        dcs_cols = jnp.zeros((128, 128), F32)
        dcs_rows = jnp.zeros((128, 128), F32)
        ddt_x = jnp.zeros((128, 128), F32)
        dd_row = jnp.zeros((1, 128), F32)
        last = r == 127
        for g in range(2):
            bg = b_ref[:, 128 * g:128 * (g + 1)]
            cg = c_ref[:, 128 * g:128 * (g + 1)]
            cb = _dot(cg, bg, NT)
            dcb = jnp.zeros((128, 128), F32)
            db_acc = jnp.zeros((128, NST), F32)
            dc_acc = jnp.zeros((128, NST), F32)
            for j in range(4):
                pj = 4 * g + j
                h1, h2 = 2 * pj, 2 * pj + 1
                sl = slice(128 * pj, 128 * (pj + 1))
                xp = x_ref[:, sl]
                dyp = dy_all[:, sl]
                c1, c2, cd1, cd2, e_l, dte1, dte2, dte_l, dt_l = _pair_terms(cs, cst, dt, h1, h2, lo)
                xdt = xp * dt_l
                hp = hp_ref[pj]
                gp = g_ref[pj]
                dxp = dsk_ref[:, sl] * dyp
                dyx = dyp * xp
                dd_row = dd_row + jnp.where(c[0:1, :] == h1, jnp.sum(jnp.where(lo, dyx, 0.0), keepdims=True), 0.0) \
                    + jnp.where(c[0:1, :] == h2, jnp.sum(jnp.where(lo, 0.0, dyx), keepdims=True), 0.0)
                dzs = dyp * e_l
                dc_acc = dc_acc + _dot(dzs, hp, NN)
                g_from = _dot(dzs, cg, TN)
                ryo = dyp * (_dot(cg, hp, NT) * e_l)
                k1 = jnp.sum(jnp.where(lo, ryo, 0.0), axis=1, keepdims=True)
                k2 = jnp.sum(jnp.where(lo, 0.0, ryo), axis=1, keepdims=True)
                qm = _dot(bg, gp, NT)
                dxdt = qm * dte_l
                qx = qm * xdt
                t1 = jnp.sum(jnp.where(lo, qx, 0.0), axis=1, keepdims=True) * dte1
                t2 = jnp.sum(jnp.where(lo, 0.0, qx), axis=1, keepdims=True) * dte2
                db_acc = db_acc + _dot(xdt * dte_l, gp, NN)
                gh = gp * hp
                dl1 = jnp.sum(t1, keepdims=True) + jnp.sum(jnp.where(r < HD, gh, 0.0), keepdims=True) * cd1
                dl2 = jnp.sum(t2, keepdims=True) + jnp.sum(jnp.where(r < HD, 0.0, gh), keepdims=True) * cd2
                g_ref[pj] = g_from + jnp.where(r < HD, cd1, cd2) * gp
                k1 = k1 - t1 + jnp.where(last[:, 0:1], dl1, 0.0)
                k2 = k2 - t2 + jnp.where(last[:, 0:1], dl2, 0.0)
                for hh, ch, msk in ((h1, c1, lo), (h2, c2, jnp.logical_not(lo))):
                    lm = jnp.exp(jnp.where(causal, ch - cst[hh:hh + 1, :], NEG))
                    mm = cb * lm
                    dm = jnp.where(causal, _dot(jnp.where(msk, dyp, 0.0), xdt, NT), 0.0)
                    w = dm * mm
                    kk = jnp.sum(w, axis=1, keepdims=True)
                    if hh == h1:
                        k1 = k1 + kk
                    else:
                        k2 = k2 + kk
                    dcs_rows = dcs_rows + jnp.where(r == hh, jnp.sum(w, axis=0, keepdims=True), 0.0)
                    dcb = dcb + dm * lm
                    dxdt = dxdt + jnp.where(msk, _dot(mm, dyp, TN), 0.0)
                dcs_cols = dcs_cols + jnp.where(c == h1, k1, 0.0) + jnp.where(c == h2, k2, 0.0)
                dxx = dxdt * xp
                ddt_x = ddt_x + jnp.where(c == h1, jnp.sum(jnp.where(lo, dxx, 0.0), axis=1, keepdims=True), 0.0) \
                    + jnp.where(c == h2, jnp.sum(jnp.where(lo, 0.0, dxx), axis=1, keepdims=True), 0.0)
                dxbc_ref[:, sl] = dxp + dxdt * dt_l
            dxbc_ref[:, SSM_W + 128 * g:SSM_W + 128 * (g + 1)] = db_acc + _dot(dcb, cg, TN)
            dxbc_ref[:, SSM_W + 256 + 128 * g:SSM_W + 256 + 128 * (g + 1)] = dc_acc + _dot(dcb, bg, NN)

        dcs = dcs_cols - dcs_rows.T
        dad = _dot_exact((c >= r).astype(F32), dcs)
        ddt = dad * a + ddt_x
        ddtr = jnp.where(c < 16, ddt * _sigmoid(z), 0.0)
        ddt_ref[...] = ddtr.astype(BF16)
        r8 = lax.broadcasted_iota(jnp.int32, (8, 128), 0)
        dsc_ref[...] += (jnp.where(r8 == 0, jnp.sum(ddtr, axis=0, keepdims=True), 0.0)
                         + jnp.where(r8 == 1, jnp.sum(dad * dt, axis=0, keepdims=True) * a, 0.0)
                         + jnp.where(r8 == 2, dd_row, 0.0))

    rev = NCH - 1
    return pl.pallas_call(
        body, name="ssd_bwd", grid=(NCH,),
        in_specs=[pl.BlockSpec((128, SSM_W), lambda i: (rev - i, 1)),
                  pl.BlockSpec((128, SSM_W), lambda i: (rev - i, 0)),
                  pl.BlockSpec((128, SSM_W), lambda i: (rev - i, 0)),
                  pl.BlockSpec((128, 256), lambda i: (rev - i, 4)), pl.BlockSpec((128, 256), lambda i: (rev - i, 5)),
                  pl.BlockSpec((128, 128), lambda i: (rev - i, COL_DT // 128)),
                  pl.BlockSpec((128, SSM_W), lambda i: (rev - i, 3)),
                  pl.BlockSpec((None, NPAIR, 128, 128), lambda i: (rev - i, 0, 0, 0)),
                  pl.BlockSpec((1, 128), lambda i: (0, 0)), pl.BlockSpec((1, 128), lambda i: (0, 0)),
                  pl.BlockSpec((1, SSM_W), lambda i: (0, 0)), pl.BlockSpec((1, SSM_W), lambda i: (0, 0))],
        out_specs=[pl.BlockSpec((128, CONV_C), lambda i: (rev - i, 0)),
                   pl.BlockSpec((128, SSM_W), lambda i: (rev - i, 0)),
                   pl.BlockSpec((128, 128), lambda i: (rev - i, 0)),
                   pl.BlockSpec((1, SSM_W), lambda i: (0, 0)), pl.BlockSpec((8, 128), lambda i: (0, 0))],
        out_shape=[jax.ShapeDtypeStruct((S, CONV_C), F32), jax.ShapeDtypeStruct((S, SSM_W), BF16),
                   jax.ShapeDtypeStruct((S, 128), BF16), jax.ShapeDtypeStruct((1, SSM_W), F32),
                   jax.ShapeDtypeStruct((8, 128), F32)],
        scratch_shapes=[pltpu.VMEM((NPAIR, 128, 128), F32)],
        compiler_params=_cparams("arbitrary"),
    )(dmixed, y, xbc, xbc, xbc, proj, proj, hprev, dtb, alog, dskip_l, ssmw)


def _cast_stack(name, slot, arrs, tr):
    n = len(arrs)
    rows, cols = arrs[0].shape

    def body(s_ref, *refs):
        for i in range(n):
            refs[n][i] = refs[i][...].astype(BF16)

    return pl.pallas_call(
        body, name=name,
        grid_spec=pltpu.PrefetchScalarGridSpec(
            num_scalar_prefetch=1, grid=(rows // tr,),
            in_specs=[pl.BlockSpec((tr, cols), lambda i, sr: (i, 0))] * n,
            out_specs=pl.BlockSpec((None, n, tr, cols), lambda i, sr: (sr[0], 0, i, 0))),
        out_shape=jax.ShapeDtypeStruct((NSH, n, rows, cols), BF16), compiler_params=_cparams("parallel"),
    )(slot, *arrs)


def _pair_sum(name, c_idx, ps, rs, th):
    n = len(ps)
    _, _, h, cols = ps[0].shape

    def body(c_ref, *refs):
        for i in range(n):
            refs[2 * n + i][...] = (refs[i][...].astype(F32) + refs[n + i][...].astype(F32)).astype(BF16)

    spec = pl.BlockSpec((None, th, cols), lambda s, i, cr: (s, i, 0))
    return pl.pallas_call(
        body, name=name,
        grid_spec=pltpu.PrefetchScalarGridSpec(
            num_scalar_prefetch=1, grid=(NSH, h // th),
            in_specs=[pl.BlockSpec((None, None, th, cols), lambda s, i, cr: (s, cr[0], i, 0))] * n + [spec] * n,
            out_specs=[spec] * n),
        out_shape=[jax.ShapeDtypeStruct((NSH, h, cols), BF16)] * n,
        compiler_params=_cparams("parallel", "parallel"),
    )(c_idx, *ps, *rs)


def _chip_sum(name, place, cs, ts, th):
    n = len(ts)
    _, h, cols = ts[0].shape

    def body(p_ref, *refs):
        for i in range(n):
            t = refs[n + i][...].astype(F32)
            refs[2 * n + i][...] = ((refs[i][...].astype(F32) + t[0]) + t[1]) + t[2]

    return pl.pallas_call(
        body, name=name,
        grid_spec=pltpu.PrefetchScalarGridSpec(
            num_scalar_prefetch=1, grid=(h // th,),
            in_specs=[pl.BlockSpec((None, th, cols), lambda i, pr: (pr[0], i, 0))] * n
            + [pl.BlockSpec((3, th, cols), lambda i, pr: (0, i, 0))] * n,
            out_specs=[pl.BlockSpec((None, th, cols), lambda i, pr: (pr[1], i, 0))] * n),
        out_shape=[jax.ShapeDtypeStruct((2, h, cols), F32)] * n, compiler_params=_cparams("parallel"),
    )(place, *cs, *ts)


def _adamw(name, ws, gs, ms, vs, tr):
    n = len(ws)
    rows, cols = ws[0].shape
    c1 = 1.0 / (1.0 - ADAM_B1 ** ADAM_STEP)
    c2 = 1.0 / (1.0 - ADAM_B2 ** ADAM_STEP)

    def body(*refs):
        for i in range(n):
            w, g, m, v = (refs[k * n + i][...] for k in range(4))
            m2 = ADAM_B1 * m + (1.0 - ADAM_B1) * g
            v2 = ADAM_B2 * v + (1.0 - ADAM_B2) * (g * g)
            refs[4 * n + 3 * i][...] = -ADAM_LR * ((m2 * c1) / (jnp.sqrt(v2 * c2) + ADAM_EPS) + ADAM_WD * w)
            refs[4 * n + 3 * i + 1][...] = m2
            refs[4 * n + 3 * i + 2][...] = v2

    spec = pl.BlockSpec((tr, cols), lambda i: (i, 0))
    outs = pl.pallas_call(
        body, name=name, grid=(rows // tr,), in_specs=[spec] * (4 * n), out_specs=[spec] * (3 * n),
        out_shape=[jax.ShapeDtypeStruct((rows, cols), F32)] * (3 * n), compiler_params=_cparams("parallel"),
    )(*ws, *gs, *ms, *vs)
    return [tuple(outs[3 * i:3 * i + 3]) for i in range(n)]


def _place():
    x, y, c = lax.axis_index("x"), lax.axis_index("y"), lax.axis_index("c")
    chips = [(1 - x, y), (x, 1 - y), (1 - x, 1 - y)]
    return x, y, c, chips


def _any_specs(n):
    return [pl.BlockSpec(memory_space=pl.ANY)] * n


def _gather_weights(colsh, wdsh, winsh, woutsh, cw):
    ins = [colsh, wdsh, winsh, woutsh]
    halves = [a.shape[1] // 2 for a in ins]
    nb = len(ins)

    def body(*refs):
        cw_in = refs[nb]
        dst, cw_out = refs[nb + 1:2 * nb + 1], refs[2 * nb + 1]
        send, recv, fsend, frecv, local, cws, cwr = refs[2 * nb + 2:]
        x, y, c, chips = _place()
        me = 2 * x + y

        def half(ref, b, s, hc):
            return ref.at[s, pl.ds(hc * halves[b], halves[b])]

        own = [pltpu.make_async_copy(cw_in, cw_out.at[me], local.at[0])]
        for cp in own:
            cp.start()
        first, passed = [], []
        for j, chip in enumerate(chips):
            for b in range(nb):
                mine = half(dst[b], b, me, c)
                first.append(pltpu.make_async_remote_copy(
                    src_ref=mine, dst_ref=mine, send_sem=send.at[j * nb + b], recv_sem=recv.at[j * nb + b],
                    device_id=(chip[0], chip[1], c), device_id_type=MESH))
            first.append(pltpu.make_async_remote_copy(
                src_ref=cw_in, dst_ref=cw_out.at[me], send_sem=cws.at[j], recv_sem=cwr.at[j],
                device_id=(chip[0], chip[1], c), device_id_type=MESH))
        for cp in first:
            cp.start()
        for j, chip in enumerate(chips):
            s = 2 * chip[0] + chip[1]
            for b in range(nb):
                landed = half(dst[b], b, s, c)
                pltpu.make_async_remote_copy(
                    src_ref=landed, dst_ref=landed, send_sem=send.at[j * nb + b], recv_sem=recv.at[j * nb + b],
                    device_id=(x, y, c), device_id_type=MESH).wait_recv()
                fw = pltpu.make_async_remote_copy(
                    src_ref=landed, dst_ref=landed, send_sem=fsend.at[j * nb + b], recv_sem=frecv.at[j * nb + b],
                    device_id=(x, y, 1 - c), device_id_type=MESH)
                fw.start()
                passed.append(fw)
        for j, chip in enumerate(chips):
            s = 2 * chip[0] + chip[1]
            for b in range(nb):
                other = half(dst[b], b, s, 1 - c)
                pltpu.make_async_remote_copy(
                    src_ref=other, dst_ref=other, send_sem=fsend.at[j * nb + b], recv_sem=frecv.at[j * nb + b],
                    device_id=(x, y, c), device_id_type=MESH).wait_recv()
            pltpu.make_async_remote_copy(
                src_ref=cw_in, dst_ref=cw_out.at[s], send_sem=cws.at[j], recv_sem=cwr.at[j],
                device_id=(x, y, c), device_id_type=MESH).wait_recv()
        for cp in first + passed:
            cp.wait_send()
        for cp in own:
            cp.wait()

    outs = [jax.ShapeDtypeStruct(a.shape, a.dtype) for a in ins] + [jax.ShapeDtypeStruct((NSH,) + cw.shape, cw.dtype)]
    return pl.pallas_call(
        body, name="gather_weights", in_specs=_any_specs(nb + 1), out_specs=_any_specs(nb + 1), out_shape=outs,
        input_output_aliases={b: b for b in range(nb)},
        scratch_shapes=[pltpu.SemaphoreType.DMA((3 * nb,)), pltpu.SemaphoreType.DMA((3 * nb,)),
                        pltpu.SemaphoreType.DMA((3 * nb,)), pltpu.SemaphoreType.DMA((3 * nb,)),
                        pltpu.SemaphoreType.DMA((1,)), pltpu.SemaphoreType.DMA((3,)),
                        pltpu.SemaphoreType.DMA((3,))],
    )(*ins, cw)


def _to_sibling(ps):
    n = len(ps)

    def body(*refs):
        src, dst, send, recv = refs[:n], refs[n:2 * n], refs[2 * n], refs[2 * n + 1]
        x, y, c, _ = _place()
        cps = [pltpu.make_async_remote_copy(
            src_ref=src[i].at[:, 1 - c], dst_ref=dst[i], send_sem=send.at[i], recv_sem=recv.at[i],
            device_id=(x, y, 1 - c), device_id_type=MESH) for i in range(n)]
        for cp in cps:
            cp.start()
        for cp in cps:
            cp.wait()

    outs = [jax.ShapeDtypeStruct((NSH,) + p.shape[2:], p.dtype) for p in ps]
    return pl.pallas_call(
        body, name="grads_to_sibling", in_specs=_any_specs(n), out_specs=_any_specs(n), out_shape=outs,
        scratch_shapes=[pltpu.SemaphoreType.DMA((n,)), pltpu.SemaphoreType.DMA((n,))],
    )(*ps)


def _to_chips(cs):
    n = len(cs)

    def body(*refs):
        src, dst, send, recv = refs[:n], refs[n:2 * n], refs[2 * n], refs[2 * n + 1]
        x, y, c, chips = _place()
        cps = []
        for j, chip in enumerate(chips):
            s = 2 * chip[0] + chip[1]
            for i in range(n):
                cps.append(pltpu.make_async_remote_copy(
                    src_ref=src[i].at[s], dst_ref=dst[i].at[j], send_sem=send.at[j * n + i],
                    recv_sem=recv.at[j * n + i], device_id=(chip[0], chip[1], c), device_id_type=MESH))
        for cp in cps:
            cp.start()
        for cp in cps:
            cp.wait()

    return pl.pallas_call(
        body, name="grads_to_chips", in_specs=_any_specs(n), out_specs=_any_specs(n),
        out_shape=[jax.ShapeDtypeStruct((3,) + a.shape[1:], a.dtype) for a in cs],
        scratch_shapes=[pltpu.SemaphoreType.DMA((3 * n,)), pltpu.SemaphoreType.DMA((3 * n,))],
    )(*cs)


def _swap_halves(gs):
    n = len(gs)

    def body(*refs):
        dst, send, recv = refs[n:2 * n], refs[2 * n], refs[2 * n + 1]
        x, y, c, _ = _place()
        cps = [pltpu.make_async_remote_copy(
            src_ref=dst[i].at[c], dst_ref=dst[i].at[c], send_sem=send.at[i], recv_sem=recv.at[i],
            device_id=(x, y, 1 - c), device_id_type=MESH) for i in range(n)]
        for cp in cps:
            cp.start()
        for i in range(n):
            other = dst[i].at[1 - c]
            pltpu.make_async_remote_copy(
                src_ref=other, dst_ref=other, send_sem=send.at[i], recv_sem=recv.at[i],
                device_id=(x, y, c), device_id_type=MESH).wait_recv()
        for cp in cps:
            cp.wait_send()

    return pl.pallas_call(
        body, name="grads_swap_halves", in_specs=_any_specs(n), out_specs=_any_specs(n),
        out_shape=[jax.ShapeDtypeStruct(g.shape, g.dtype) for g in gs],
        input_output_aliases={i: i for i in range(n)},
        scratch_shapes=[pltpu.SemaphoreType.DMA((n,)), pltpu.SemaphoreType.DMA((n,))],
    )(*gs)


SMALL_ROWS = 16


def _allreduce_small(vec):
    def body(v_ref, o_ref, buf, send, recv):
        x, y, c, _ = _place()
        me = 4 * x + 2 * y + c
        buf[me] = v_ref[...]
        cps = []
        for k in range(1, 8):
            peer = (x ^ (k >> 2), y ^ ((k >> 1) & 1), c ^ (k & 1))
            cps.append(pltpu.make_async_remote_copy(
                src_ref=v_ref, dst_ref=buf.at[me], send_sem=send.at[k - 1], recv_sem=recv.at[k - 1],
                device_id=peer, device_id_type=MESH))
        for cp in cps:
            cp.start()
        for k in range(1, 8):
            pltpu.make_async_remote_copy(
                src_ref=v_ref, dst_ref=buf.at[me ^ k], send_sem=send.at[k - 1], recv_sem=recv.at[k - 1],
                device_id=(x, y, c), device_id_type=MESH).wait_recv()
        for cp in cps:
            cp.wait_send()
        t = buf[0]
        for d in range(1, 8):
            t = t + buf[d]
        o_ref[...] = t

    return pl.pallas_call(
        body, name="allreduce_small",
        in_specs=[pl.BlockSpec(memory_space=pltpu.VMEM)], out_specs=pl.BlockSpec(memory_space=pltpu.VMEM),
        out_shape=jax.ShapeDtypeStruct((SMALL_ROWS, D), F32),
        scratch_shapes=[pltpu.VMEM((8, SMALL_ROWS, D), F32), pltpu.SemaphoreType.DMA((7,)),
                        pltpu.SemaphoreType.DMA((7,))],
    )(vec)


ROPE_THETA = 10000.0
SMALL_1K = ("ffn1_pre_norm", "ffn1_post_norm", "mix_pre_norm", "ssm_norm", "mix_post_norm",
            "ffn2_pre_norm", "ffn2_post_norm")
SMALL_16 = ("dt_bias", "a_log", "d_skip")
OFF_CONVB = 7 * D
OFF_16 = OFF_CONVB + CONV_C
OFF_CONVW = OFF_16 + 48
OFF_LOSS = OFF_CONVW + CONV_K * CONV_C
SMALL_LEN = SMALL_ROWS * D


def _sds(shape, dtype):
    return jax.ShapeDtypeStruct(shape, dtype)


def _ffn_down(name, act, wdw, wi):
    return _mm(name, [act, wdw], NN, (S // TS, NSH),
               [pl.BlockSpec((None, TS, FS), lambda i, s: (s, i, 0)),
                pl.BlockSpec((None, None, FS, D), lambda i, s: (s, wi, 0, 0))],
               pl.BlockSpec((TS, D), lambda i, s: (i, 0)), _sds((S, D), F32), (TS, D))


def _ffn_bwd(tag, dh, n, gate, up, act, colw, wdw, ci, wi):
    dgate, dup = _ffn_dact(tag + "_dact", dh, wdw, wi, gate, up)
    aspec = pl.BlockSpec((None, TS, FS), lambda s, k: (s, k, 0))
    nspec = pl.BlockSpec((TS, D), lambda s, k: (k, 0))
    dwd = _mm(tag + "_dwd", [act, dh], TN, (NSH, S // TS), [aspec, nspec],
              pl.BlockSpec((None, FS, D), lambda s, k: (s, 0, 0)), _sds((NSH, FS, D), BF16), (FS, D))
    wspec = pl.BlockSpec((None, D, FS), lambda s, k: (s, 0, 0))
    dwg = _mm(tag + "_dwg", [n, dgate], TN, (NSH, S // TS), [nspec, aspec], wspec, _sds((NSH, D, FS), BF16), (D, FS))
    dwu = _mm(tag + "_dwu", [n, dup], TN, (NSH, S // TS), [nspec, aspec], wspec, _sds((NSH, D, FS), BF16), (D, FS))
    a2 = pl.BlockSpec((None, TS, FS), lambda i, s: (s, i, 0))
    dn = _mm(tag + "_dn", [dgate, colw, dup, colw], NT, (S // TS, NSH),
             [a2, pl.BlockSpec((None, None, D, FS), lambda i, s: (s, ci, 0, 0)),
              a2, pl.BlockSpec((None, None, D, FS), lambda i, s: (s, ci + 1, 0, 0))],
             pl.BlockSpec((TS, D), lambda i, s: (i, 0)), _sds((S, D), F32), (TS, D))
    return dn, dwg, dwu, dwd


def _heads(t, n):
    return t.reshape(S, n, HD).transpose(1, 0, 2)


def _unheads(t):
    return t.transpose(1, 0, 2).reshape(S, t.shape[0] * HD)


def _heads_t(t, n):
    return t.reshape(S, n, HD).transpose(1, 2, 0)


def _blocks5(t):
    return t.reshape(NCH, 128, NKV, NQ_PER_KV, HD)


def _to_blocks_t(t):
    return _blocks5(t).transpose(2, 0, 4, 3, 1).reshape(NKV, NCH, HD, QROWS)


def _to_blocks(t):
    return _blocks5(t).transpose(2, 0, 3, 1, 4).reshape(NKV, NCH, QROWS, HD)


def _from_blocks_t(t):
    return t.reshape(NKV, NCH, HD, NQ_PER_KV, 128).transpose(1, 4, 0, 3, 2).reshape(S, D)


def _pad128(v):
    return jnp.pad(v, ((0, 0), (0, 128 - v.shape[1])))


def _local_step(x, positions, tgt, sp, colw, wdw, win_pad, wout, convw):
    inv_freq = ROPE_THETA ** (-jnp.arange(0, HD, 2, dtype=F32) / HD)
    ang = positions.astype(F32)[:, None] * inv_freq
    ang = jnp.concatenate([ang, ang, ang, ang], axis=-1)
    cos, sin = jnp.cos(ang), jnp.sin(ang)
    dtb, alog = _pad128(sp["dt_bias"]), _pad128(sp["a_log"])
    dskip_l = jnp.repeat(sp["d_skip"], HD, axis=1)
    convb = sp["conv_b"]

    n1 = _prenorm("prenorm1", x, sp["ffn1_pre_norm"])
    gate1, up1, act1 = _ffn_up("ffn1_up", n1, colw, 0)
    h1 = _ffn_down("ffn1_down", act1, wdw, 0)
    x1, n2 = _postres("postres1", x, h1, sp["ffn1_post_norm"], 0.5, sp["mix_pre_norm"])

    pw = WIN_PAD // 3
    proj = _mm("in_proj", [n2, win_pad], NN, (S // TS, 3, 1),
               [pl.BlockSpec((TS, D), lambda i, j, k: (i, 0)), pl.BlockSpec((D, pw), lambda i, j, k: (0, j))],
               pl.BlockSpec((TS, pw), lambda i, j, k: (i, j)), _sds((S, WIN_PAD), F32), (TS, pw))
    q_rot = _rope("rope_q", proj, 0, D, cos, sin, 1.0, HD ** -0.5)
    k_rot = _rope("rope_k", proj, D // KVW, KVW, cos, sin, 1.0, 1.0)
    v_bf = proj[:, D + KVW:D + 2 * KVW].astype(BF16)
    qt, kh, vh = _to_blocks_t(q_rot), _heads(k_rot, NKV), _heads(v_bf, NKV)
    kt, vt = _heads_t(k_rot, NKV), _heads_t(v_bf, NKV)
    bias = _bias_table()
    ot, lse = _attn_fwd(qt, kh, vt, bias)
    attn = _from_blocks_t(ot).astype(BF16)
    xbc = _conv_fwd(proj, convw, convb)
    y, yn, hprev = _ssd_fwd(xbc, proj, dtb, alog, dskip_l, sp["ssm_norm"])
    mixed = jnp.concatenate([attn, yn], axis=1)
    h2 = _mm("out_proj", [mixed, wout], NN, (S // TS, 1),
             [pl.BlockSpec((TS, 2 * D), lambda i, k: (i, 0)), pl.BlockSpec((2 * D, D), lambda i, k: (0, 0))],
             pl.BlockSpec((TS, D), lambda i, k: (i, 0)), _sds((S, D), F32), (TS, D))
    x2, n3 = _postres("postres2", x1, h2, sp["mix_post_norm"], 1.0, sp["ffn2_pre_norm"])

    gate2, up2, act2 = _ffn_up("ffn2_up", n3, colw, 2)
    h3 = _ffn_down("ffn2_down", act2, wdw, 1)
    dy, dh3, dp3, loss = _final(x2, h3, sp["ffn2_post_norm"], tgt, 0.5)

    dn3, dwg2, dwu2, dwd2 = _ffn_bwd("ffn2", dh3, n3, gate2, up2, act2, colw, wdw, 2, 1)
    dx2, dh2, dg3, dp2 = _mid_bwd("mid_bwd2", dy, dn3, x2, sp["ffn2_pre_norm"], h2, sp["mix_post_norm"], 1.0)

    dmixed = _mm("out_proj_dx", [dh2, wout], NT, (S // TS, 1),
                 [pl.BlockSpec((TS, D), lambda i, k: (i, 0)), pl.BlockSpec((2 * D, D), lambda i, k: (0, 0))],
                 pl.BlockSpec((TS, 2 * D), lambda i, k: (i, 0)), _sds((S, 2 * D), F32), (TS, 2 * D))
    dwout = _mm("out_proj_dw", [mixed, dh2], TN, (2, S // TS),
                [pl.BlockSpec((TS, D), lambda m, k: (k, m)), pl.BlockSpec((TS, D), lambda m, k: (k, 0))],
                pl.BlockSpec((D, D), lambda m, k: (m, 0)), _sds((2 * D, D), BF16), (D, D))
    dxbc, dz, ddt, dssm, dsc = _ssd_bwd(dmixed, y, xbc, proj, hprev, dtb, alog, dskip_l, sp["ssm_norm"])
    du, dcw8, dcb = _conv_bwd(dxbc, proj, convw, convb)
    do_bf = dmixed[:, :D].astype(BF16)
    dot_ = _to_blocks_t(do_bf)
    dqt, dkh, dvh = _attn_bwd(qt, _to_blocks(q_rot), kh, kt, vh, dot_, _to_blocks(do_bf), lse,
                              _attn_delta(ot, dot_), bias)
    dq = _rope("rope_dq", _from_blocks_t(dqt), 0, D, cos, sin, -1.0, HD ** -0.5)
    dk = _rope("rope_dk", _unheads(dkh), 0, KVW, cos, sin, -1.0, 1.0)
    dproj = jnp.concatenate([dq, dk, _unheads(dvh).astype(BF16), du, dz, ddt], axis=1)
    dn2 = _mm("in_proj_dx", [dproj, win_pad], NT, (S // TS, 3),
              [pl.BlockSpec((TS, pw), lambda i, k: (i, k)), pl.BlockSpec((D, pw), lambda i, k: (0, k))],
              pl.BlockSpec((TS, D), lambda i, k: (i, 0)), _sds((S, D), F32), (TS, D))
    dwin = _mm("in_proj_dw", [n2, dproj], TN, (3, S // TS),
               [pl.BlockSpec((TS, D), lambda j, k: (k, 0)), pl.BlockSpec((TS, pw), lambda j, k: (k, j))],
               pl.BlockSpec((D, pw), lambda j, k: (0, j)), _sds((D, WIN_PAD), BF16), (D, pw))
    dx1, dh1, dg2, dp1 = _mid_bwd("mid_bwd1", dx2, dn2, x1, sp["mix_pre_norm"], h1, sp["ffn1_post_norm"], 0.5)

    dn1, dwg1, dwu1, dwd1 = _ffn_bwd("ffn1", dh1, n1, gate1, up1, act1, colw, wdw, 0, 0)
    grad_x, dg1 = _first_bwd(dx1, dn1, x, sp["ffn1_pre_norm"])

    small = jnp.concatenate([
        dg1[0], dp1[0], dg2[0], dssm[0], dp2[0], dg3[0], dp3[0], dcb[0],
        dsc[0, :16], dsc[1, :16], dsc[2, :16], dcw8[:CONV_K].reshape(-1), loss[0, :1]])
    small = jnp.pad(small, (0, SMALL_LEN - small.shape[0])).reshape(SMALL_ROWS, D)
    dwin_sh = dwin[:, :WIN_COLS].reshape(D, NSH, WIN_SH).transpose(1, 0, 2)
    partials = [dwg1, dwu1, dwg2, dwu2, dwd1, dwd2, dwin_sh, dwout.reshape(NSH, 2 * D // NSH, D)]
    return grad_x, partials, small


WEIGHTS = ("ffn1_pre_norm", "ffn1_w_gate", "ffn1_w_up", "ffn1_w_down", "ffn1_post_norm", "mix_pre_norm", "w_in",
           "conv_w", "conv_b", "dt_bias", "a_log", "d_skip", "ssm_norm", "w_out", "mix_post_norm", "ffn2_pre_norm",
           "ffn2_w_gate", "ffn2_w_up", "ffn2_w_down", "ffn2_post_norm")
BIG = ("ffn1_w_gate", "ffn1_w_up", "ffn2_w_gate", "ffn2_w_up", "ffn1_w_down", "ffn2_w_down", "w_in", "w_out")
SMALL_ORDER = SMALL_1K + ("conv_b",) + SMALL_16
CONVW_SH = CONV_C // NSH


def _pack_small(d, prefix, shard_of_convw):
    flat = jnp.concatenate([d[prefix + n][0] for n in SMALL_ORDER] + [shard_of_convw.reshape(-1)])
    return jnp.pad(flat, (0, SMALL_LEN - flat.shape[0])).reshape(SMALL_ROWS, D)


def _unpack_small(block, like):
    flat = block.reshape(-1)
    out, off = {}, 0
    for n in SMALL_ORDER:
        size = like[n].shape[1]
        out[n] = flat[off:off + size].reshape(1, size)
        off += size
    out["conv_w"] = flat[off:off + CONV_K * CONVW_SH].reshape(1, CONV_K, CONVW_SH)
    return out


def kernel(x, positions, ffn1_pre_norm, ffn1_w_gate, ffn1_w_up, ffn1_w_down, ffn1_post_norm, mix_pre_norm, w_in, conv_w, conv_b, dt_bias, a_log, d_skip, ssm_norm, w_out, mix_post_norm, ffn2_pre_norm, ffn2_w_gate, ffn2_w_up, ffn2_w_down, ffn2_post_norm, loss_target, m_ffn1_pre_norm, m_ffn1_w_gate, m_ffn1_w_up, m_ffn1_w_down, m_ffn1_post_norm, m_mix_pre_norm, m_w_in, m_conv_w, m_conv_b, m_dt_bias, m_a_log, m_d_skip, m_ssm_norm, m_w_out, m_mix_post_norm, m_ffn2_pre_norm, m_ffn2_w_gate, m_ffn2_w_up, m_ffn2_w_down, m_ffn2_post_norm, v_ffn1_pre_norm, v_ffn1_w_gate, v_ffn1_w_up, v_ffn1_w_down, v_ffn1_post_norm, v_mix_pre_norm, v_w_in, v_conv_w, v_conv_b, v_dt_bias, v_a_log, v_d_skip, v_ssm_norm, v_w_out, v_mix_post_norm, v_ffn2_pre_norm, v_ffn2_w_gate, v_ffn2_w_up, v_ffn2_w_down, v_ffn2_post_norm):
    given = dict(locals())
    xi, yi, ci = lax.axis_index("x"), lax.axis_index("y"), lax.axis_index("c")

    shard = jnp.reshape(2 * xi + yi, (1,)).astype(jnp.int32)
    colsh = _cast_stack("cast_gate_up", shard, [ffn1_w_gate[0], ffn1_w_up[0], ffn2_w_gate[0], ffn2_w_up[0]], 256)
    wdsh = _cast_stack("cast_down", shard, [ffn1_w_down[0], ffn2_w_down[0]], 352)
    winsh = _cast_stack("cast_w_in", shard, [w_in[0]], 256).reshape(NSH, D, WIN_SH)
    woutsh = _cast_stack("cast_w_out", shard, [w_out[0]], 256).reshape(NSH, 2 * D // NSH, D)
    colw, wdw, winf, woutf, cwf = _gather_weights(colsh, wdsh, winsh, woutsh, conv_w[0])
    win_pad = jnp.pad(winf.transpose(1, 0, 2).reshape(D, WIN_COLS), ((0, 0), (0, WIN_PAD - WIN_COLS)))
    convw = cwf.transpose(1, 0, 2).reshape(CONV_K, CONV_C)

    sp = {n: given[n] for n in SMALL_ORDER}
    grad_x, partials, small = _local_step(x[0], positions[0], loss_target[0], sp, colw, wdw, win_pad,
                                          woutf.reshape(2 * D, D), convw)

    ps = [p.reshape(NSH, 2, p.shape[1] // 2, p.shape[2]) for p in partials]
    rs = _to_sibling(ps)
    c_idx = jnp.reshape(ci, (1,)).astype(jnp.int32)
    groups = ((0, 4, 256), (4, 6, 176), (6, 7, 256), (7, 8, 256))
    csums = []
    for a, b, th in groups:
        csums += _pair_sum("pair_sum_%d" % a, c_idx, ps[a:b], rs[a:b], th)
    ts = _to_chips(csums)
    place = jnp.stack([2 * xi + yi, ci]).astype(jnp.int32)
    halves = []
    for a, b, th in groups:
        halves += _chip_sum("chip_sum_%d" % a, place, csums[a:b], ts[a:b], th)
    full = _swap_halves(halves)
    big_grads = {n: g.reshape(given[n].shape[1:]) for n, g in zip(BIG, full)}

    tot = _allreduce_small(small).reshape(-1)
    loss = tot[OFF_LOSS]
    small_grads, off = {}, 0
    for n in SMALL_ORDER:
        size = given[n].shape[1]
        small_grads[n] = tot[off:off + size].reshape(1, size)
        off += size
    dconvw = tot[OFF_CONVW:OFF_CONVW + CONV_K * CONV_C].reshape(CONV_K, NSH, CONVW_SH)
    dconvw = lax.dynamic_index_in_dim(dconvw, 2 * xi + yi, axis=1, keepdims=False)
    small_grads["conv_w"] = dconvw.reshape(1, CONV_K, CONVW_SH)

    upd = {}
    for names, tr in ((BIG[0:4], 256), (BIG[4:6], 176), (BIG[6:7], 256), (BIG[7:8], 256)):
        res = _adamw("adamw_" + names[0], [given[n][0] for n in names], [big_grads[n] for n in names],
                     [given["m_" + n][0] for n in names], [given["v_" + n][0] for n in names], tr)
        for n, (dl, m2, v2) in zip(names, res):
            upd[n] = (dl[None], m2[None], v2[None])
    (dl, m2, v2), = _adamw(
        "adamw_small", [_pack_small(given, "", conv_w[0])], [_pack_small(small_grads, "", dconvw)],
        [_pack_small(given, "m_", m_conv_w[0])], [_pack_small(given, "v_", v_conv_w[0])], SMALL_ROWS)
    dl, m2, v2 = (_unpack_small(t, given) for t in (dl, m2, v2))
    for n in SMALL_ORDER + ("conv_w",):
        upd[n] = (dl[n], m2[n], v2[n])

    grads = dict(small_grads)
    grads.update({n: g[None] for n, g in big_grads.items()})
    return (loss, grad_x[None], *[grads[n] for n in WEIGHTS], *[upd[n][0] for n in WEIGHTS],
            *[upd[n][1] for n in WEIGHTS], *[upd[n][2] for n in WEIGHTS])
```

```python
import functools
import math

import jax
import jax.numpy as jnp
from jax import lax
from jax.experimental import pallas as pl
from jax.experimental.pallas import tpu as pltpu

F32 = jnp.float32
BF16 = jnp.bfloat16

S = 2048
D = 1024
FF = 2816
NSH = 4
FS = FF // NSH
HALF = D // 2
HD = 64
NKV = 4
NQ_PER_KV = 4
KVW = NKV * HD
CONV_C = 1536
CONV_K = 4
SSM_W = 1024
NST = 128
NCH = S // 128
WIN_COLS = 4112
WIN_SH = WIN_COLS // NSH
WIN_PAD = 4224
COL_DT = 4096
EPS = 1e-6
NEG = -1e30

ADAM_LR = 0.001
ADAM_B1 = 0.9
ADAM_B2 = 0.999
ADAM_EPS = 1e-08
ADAM_WD = 0.01
ADAM_STEP = 10

VMEM_LIMIT = 56 * 1024 * 1024
TS = 512
TR = 256

NN = (((1,), (0,)), ((), ()))
NT = (((1,), (1,)), ((), ()))
TN = (((0,), (0,)), ((), ()))
MESH = pl.DeviceIdType.MESH


def _cparams(*sem):
    return pltpu.CompilerParams(dimension_semantics=sem, vmem_limit_bytes=VMEM_LIMIT)


def _dot(a, b, dims):
    return lax.dot_general(a.astype(BF16), b.astype(BF16), dims, preferred_element_type=F32)


def _dot_exact(a, b):
    return lax.dot_general(a, b, NN, precision=lax.Precision.HIGHEST, preferred_element_type=F32)


def _sigmoid(v):
    return 1.0 / (1.0 + jnp.exp(-v))


def _mm(name, operands, dims, grid, in_specs, o_spec, out_shape, acc_shape):
    npairs = len(operands) // 2
    nk = grid[-1]
    kaxis = len(grid) - 1

    def body(*refs):
        o_ref, acc = refs[2 * npairs], refs[2 * npairs + 1]
        k = pl.program_id(kaxis)

        @pl.when(k == 0)
        def _():
            acc[...] = jnp.zeros_like(acc)

        t = None
        for i in range(npairs):
            d = _dot(refs[2 * i][...], refs[2 * i + 1][...], dims)
            t = d if t is None else t + d
        acc[...] += t

        @pl.when(k == nk - 1)
        def _():
            o_ref[...] = acc[...].astype(o_ref.dtype)

    return pl.pallas_call(
        body, name=name, grid=grid, in_specs=in_specs, out_specs=o_spec, out_shape=out_shape,
        scratch_shapes=[pltpu.VMEM(acc_shape, F32)],
        compiler_params=_cparams(*(("parallel",) * kaxis + ("arbitrary",))),
    )(*operands)


def _ffn_up(name, n, ffnw, wi):
    def body(n_ref, wg_ref, wu_ref, g_ref, u_ref, a_ref):
        nb = n_ref[...]
        g = _dot(nb, wg_ref[...], NT)
        u = _dot(nb, wu_ref[...], NT)
        g_ref[...] = g.astype(BF16)
        u_ref[...] = u.astype(BF16)
        a_ref[...] = (g * _sigmoid(g) * u).astype(BF16)

    out = jax.ShapeDtypeStruct((NSH, S, FS), BF16)
    ospec = pl.BlockSpec((None, TS, FS), lambda s, i: (s, i, 0))
    return pl.pallas_call(
        body, name=name, grid=(NSH, S // TS),
        in_specs=[pl.BlockSpec((TS, D), lambda s, i: (i, 0)),
                  pl.BlockSpec((None, None, FS, D), lambda s, i: (s, wi, 0, 0)),
                  pl.BlockSpec((None, None, FS, D), lambda s, i: (s, wi + 1, 0, 0))],
        out_specs=[ospec, ospec, ospec], out_shape=[out, out, out],
        compiler_params=_cparams("parallel", "parallel"),
    )(n, ffnw, ffnw)


def _ffn_dact(name, dh, ffnw, wi, gate, up):
    def body(dh_ref, wd_ref, g_ref, u_ref, dg_ref, du_ref):
        da = _dot(dh_ref[...], wd_ref[...], NT)
        g = g_ref[...].astype(F32)
        u = u_ref[...].astype(F32)
        sg = _sigmoid(g)
        dg_ref[...] = (da * u * (sg * (1.0 + g * (1.0 - sg)))).astype(BF16)
        du_ref[...] = (da * (g * sg)).astype(BF16)

    out = jax.ShapeDtypeStruct((NSH, S, FS), BF16)
    aspec = pl.BlockSpec((None, TS, FS), lambda s, i: (s, i, 0))
    return pl.pallas_call(
        body, name=name, grid=(NSH, S // TS),
        in_specs=[pl.BlockSpec((TS, D), lambda s, i: (i, 0)),
                  pl.BlockSpec((None, None, FS, D), lambda s, i: (s, wi + 2, 0, 0)),
                  aspec, aspec],
        out_specs=[aspec, aspec], out_shape=[out, out],
        compiler_params=_cparams("parallel", "parallel"),
    )(dh, ffnw, gate, up)


def _rstd(v):
    return lax.rsqrt(jnp.mean(v * v, axis=-1, keepdims=True) + EPS)


def _row_spec():
    return pl.BlockSpec((TR, D), lambda i: (i, 0))


def _vec_spec():
    return pl.BlockSpec((1, D), lambda i: (0, 0))


def _acc_rows(ref, v):
    @pl.when(pl.program_id(0) == 0)
    def _():
        ref[...] = jnp.zeros_like(ref)
    ref[...] += jnp.sum(v, axis=0, keepdims=True)


def _prenorm(name, x, g):
    def body(x_ref, g_ref, n_ref):
        xv = x_ref[...]
        n_ref[...] = (xv * _rstd(xv) * g_ref[...]).astype(BF16)

    return pl.pallas_call(
        body, name=name, grid=(S // TR,), in_specs=[_row_spec(), _vec_spec()], out_specs=_row_spec(),
        out_shape=jax.ShapeDtypeStruct((S, D), BF16), compiler_params=_cparams("parallel"),
    )(x, g)


def _postres(name, x, h, p, alpha, gnext):
    def body(x_ref, h_ref, p_ref, g_ref, xo_ref, n_ref):
        hv = h_ref[...]
        xo = x_ref[...] + alpha * (hv * _rstd(hv) * p_ref[...])
        xo_ref[...] = xo
        n_ref[...] = (xo * _rstd(xo) * g_ref[...]).astype(BF16)

    return pl.pallas_call(
        body, name=name, grid=(S // TR,),
        in_specs=[_row_spec(), _row_spec(), _vec_spec(), _vec_spec()],
        out_specs=[_row_spec(), _row_spec()],
        out_shape=[jax.ShapeDtypeStruct((S, D), F32), jax.ShapeDtypeStruct((S, D), BF16)],
        compiler_params=_cparams("parallel"),
    )(x, h, p, gnext)


def _final(x, h, p, tgt, alpha):
    def body(x_ref, h_ref, p_ref, t_ref, dy_ref, dh_ref, dp_ref, loss_ref):
        hv = h_ref[...]
        r = _rstd(hv)
        hn = hv * r
        pv = p_ref[...]
        e = x_ref[...] + alpha * (hn * pv) - t_ref[...]
        dy = e * (1.0 / D)
        dy_ref[...] = dy
        du = alpha * dy * pv
        dh_ref[...] = (r * (du - hn * jnp.mean(du * hn, axis=-1, keepdims=True))).astype(BF16)
        _acc_rows(dp_ref, alpha * dy * hn)
        part = 0.5 * jnp.sum(jnp.mean(e * e, axis=-1, keepdims=True), axis=0, keepdims=True)
        _acc_rows(loss_ref, jnp.broadcast_to(part, (1, 128)))

    return pl.pallas_call(
        body, name="loss_head", grid=(S // TR,),
        in_specs=[_row_spec(), _row_spec(), _vec_spec(), _row_spec()],
        out_specs=[_row_spec(), _row_spec(), _vec_spec(), pl.BlockSpec((1, 128), lambda i: (0, 0))],
        out_shape=[jax.ShapeDtypeStruct((S, D), F32), jax.ShapeDtypeStruct((S, D), BF16),
                   jax.ShapeDtypeStruct((1, D), F32), jax.ShapeDtypeStruct((1, 128), F32)],
        compiler_params=_cparams("arbitrary"),
    )(x, h, p, tgt)


def _mid_bwd(name, dres, dn, x, g, h, p, alpha):
    def body(dr_ref, dn_ref, x_ref, g_ref, h_ref, p_ref, dx_ref, dh_ref, dg_ref, dp_ref):
        xv = x_ref[...]
        xn = xv * _rstd(xv)
        dnv = dn_ref[...]
        dng = dnv * g_ref[...]
        dx = dr_ref[...] + _rstd(xv) * (dng - xn * jnp.mean(dng * xn, axis=-1, keepdims=True))
        dx_ref[...] = dx
        _acc_rows(dg_ref, dnv * xn)
        hv = h_ref[...]
        r = _rstd(hv)
        hn = hv * r
        du = alpha * dx * p_ref[...]
        dh_ref[...] = (r * (du - hn * jnp.mean(du * hn, axis=-1, keepdims=True))).astype(BF16)
        _acc_rows(dp_ref, alpha * dx * hn)

    return pl.pallas_call(
        body, name=name, grid=(S // TR,),
        in_specs=[_row_spec(), _row_spec(), _row_spec(), _vec_spec(), _row_spec(), _vec_spec()],
        out_specs=[_row_spec(), _row_spec(), _vec_spec(), _vec_spec()],
        out_shape=[jax.ShapeDtypeStruct((S, D), F32), jax.ShapeDtypeStruct((S, D), BF16),
                   jax.ShapeDtypeStruct((1, D), F32), jax.ShapeDtypeStruct((1, D), F32)],
        compiler_params=_cparams("arbitrary"),
    )(dres, dn, x, g, h, p)


def _first_bwd(dres, dn, x, g):
    def body(dr_ref, dn_ref, x_ref, g_ref, dx_ref, dg_ref):
        xv = x_ref[...]
        r = _rstd(xv)
        xn = xv * r
        dnv = dn_ref[...]
        dng = dnv * g_ref[...]
        dx_ref[...] = dr_ref[...] + r * (dng - xn * jnp.mean(dng * xn, axis=-1, keepdims=True))
        _acc_rows(dg_ref, dnv * xn)

    return pl.pallas_call(
        body, name="first_bwd", grid=(S // TR,),
        in_specs=[_row_spec(), _row_spec(), _row_spec(), _vec_spec()],
        out_specs=[_row_spec(), _vec_spec()],
        out_shape=[jax.ShapeDtypeStruct((S, D), F32), jax.ShapeDtypeStruct((1, D), F32)],
        compiler_params=_cparams("arbitrary"),
    )(dres, dn, x, g)


def _rope(name, src, col_block, width, cos, sin, sign, scale):
    def body(t_ref, c_ref, s_ref, o_ref):
        t = t_ref[...].astype(F32)
        c = jnp.tile(c_ref[...], (1, width // 128))
        sn = jnp.tile(s_ref[...], (1, width // 128))
        lane = lax.broadcasted_iota(jnp.int32, t.shape, 1) & (HD - 1)
        rot = jnp.where(lane < HD // 2, -pltpu.roll(t, width - HD // 2, 1), pltpu.roll(t, HD // 2, 1))
        o_ref[...] = ((t * c + sign * (rot * sn)) * scale).astype(BF16)

    return pl.pallas_call(
        body, name=name, grid=(S // TR,),
        in_specs=[pl.BlockSpec((TR, width), lambda i: (i, col_block)),
                  pl.BlockSpec((TR, 128), lambda i: (i, 0)), pl.BlockSpec((TR, 128), lambda i: (i, 0))],
        out_specs=pl.BlockSpec((TR, width), lambda i: (i, 0)),
        out_shape=jax.ShapeDtypeStruct((S, width), BF16), compiler_params=_cparams("parallel"),
    )(src, cos, sin)


QROWS = NQ_PER_KV * 128


NBIAS = NCH + 1


def _bias_table():
    db = lax.broadcasted_iota(jnp.int32, (NBIAS, 128, QROWS), 0) - 1
    ki = lax.broadcasted_iota(jnp.int32, (NBIAS, 128, QROWS), 1)
    qi = lax.broadcasted_iota(jnp.int32, (NBIAS, 128, QROWS), 2) & 127
    d = db * 128 + qi - ki
    cnt = ((d <= 128).astype(F32) + (((d & 3) == 0) & (d <= 512)).astype(F32) + ((d & 15) == 0).astype(F32))
    return jnp.where((d >= 0) & (cnt > 0.0), jnp.log(jnp.maximum(cnt, 1.0)), NEG)


def _qt_spec():
    return pl.BlockSpec((None, None, HD, QROWS), lambda j, i: (j, i, 0, 0))


def _stat_spec():
    return pl.BlockSpec((None, None, 1, QROWS), lambda j, i: (j, i, 0, 0))


def _attn_fwd(qt, kh, vt, bias):
    def body(q_ref, k_ref, v_ref, b_ref, o_ref, lse_ref):
        qb = pl.program_id(1)
        q = q_ref[...]

        def step(i, carry):
            m, l, acc = carry
            off = pl.multiple_of(i * 256, 256)
            s = _dot(k_ref[pl.ds(off, 256), :], q, NN)
            s = jnp.concatenate([s[:128] + b_ref[qb - 2 * i + 1], s[128:] + b_ref[qb - 2 * i]], axis=0)
            m_new = jnp.maximum(m, jnp.max(s, axis=0, keepdims=True))
            p = jnp.exp(s - m_new)
            a = jnp.exp(m - m_new)
            return (m_new, a * l + jnp.sum(p, axis=0, keepdims=True),
                    a * acc + _dot(v_ref[:, pl.ds(off, 256)], p, NN))

        m, l, acc = lax.fori_loop(
            0, qb // 2 + 1, step,
            (jnp.full((1, QROWS), NEG, F32), jnp.zeros((1, QROWS), F32), jnp.zeros((HD, QROWS), F32)))
        o_ref[...] = acc / l
        lse_ref[...] = m + jnp.log(l)

    return pl.pallas_call(
        body, name="attn_fwd", grid=(NKV, NCH),
        in_specs=[_qt_spec(), pl.BlockSpec((None, S, HD), lambda j, i: (j, 0, 0)),
                  pl.BlockSpec((None, HD, S), lambda j, i: (j, 0, 0)),
                  pl.BlockSpec((NBIAS, 128, QROWS), lambda j, i: (0, 0, 0))],
        out_specs=[_qt_spec(), _stat_spec()],
        out_shape=[jax.ShapeDtypeStruct((NKV, NCH, HD, QROWS), F32),
                   jax.ShapeDtypeStruct((NKV, NCH, 1, QROWS), F32)],
        compiler_params=_cparams("parallel", "parallel"),
    )(qt, kh, vt, bias)


def _attn_delta(ot, dot_):
    def body(o_ref, do_ref, dl_ref):
        dl_ref[...] = jnp.sum(o_ref[...] * do_ref[...].astype(F32), axis=0, keepdims=True)

    return pl.pallas_call(
        body, name="attn_delta", grid=(NKV, NCH), in_specs=[_qt_spec(), _qt_spec()], out_specs=_stat_spec(),
        out_shape=jax.ShapeDtypeStruct((NKV, NCH, 1, QROWS), F32),
        compiler_params=_cparams("parallel", "parallel"),
    )(ot, dot_)


def _attn_bwd(qt, q2, kh, kt, vh, dot_, do2, lse, delta, bias):
    def body(qt_ref, q2_ref, k_ref, kt_ref, v_ref, dot_ref, do2_ref, lse_ref, dl_ref, b_ref, dq_ref, dk_ref, dv_ref):
        kb = pl.program_id(1)

        @pl.when(kb == 0)
        def _():
            dq_ref[...] = jnp.zeros_like(dq_ref)

        k = k_ref[...]
        kt_ = kt_ref[...]
        v = v_ref[...]

        def step(j, carry):
            dk, dv = carry
            for qb in (2 * j, 2 * j + 1):
                st = _dot(k, qt_ref[qb], NN) + b_ref[qb - kb + 1]
                pt = jnp.exp(st - lse_ref[qb])
                dst = pt * (_dot(v, dot_ref[qb], NN) - dl_ref[qb])
                dq_ref[qb] += _dot(kt_, dst, NN)
                dk = dk + _dot(dst, q2_ref[qb], NN)
                dv = dv + _dot(pt, do2_ref[qb], NN)
            return dk, dv

        dk, dv = lax.fori_loop(kb // 2, NCH // 2, step, (jnp.zeros((128, HD), F32), jnp.zeros((128, HD), F32)))
        dk_ref[...] = dk
        dv_ref[...] = dv

    tspec = pl.BlockSpec((None, NCH, HD, QROWS), lambda j, i: (j, 0, 0, 0))
    rspec = pl.BlockSpec((None, NCH, QROWS, HD), lambda j, i: (j, 0, 0, 0))
    kspec = pl.BlockSpec((None, 128, HD), lambda j, i: (j, i, 0))
    sspec = pl.BlockSpec((None, NCH, 1, QROWS), lambda j, i: (j, 0, 0, 0))
    return pl.pallas_call(
        body, name="attn_bwd", grid=(NKV, NCH),
        in_specs=[tspec, rspec, kspec, pl.BlockSpec((None, HD, 128), lambda j, i: (j, 0, i)), kspec, tspec, rspec,
                  sspec, sspec, pl.BlockSpec((NBIAS, 128, QROWS), lambda j, i: (0, 0, 0))],
        out_specs=[tspec, kspec, kspec],
        out_shape=[jax.ShapeDtypeStruct((NKV, NCH, HD, QROWS), F32),
                   jax.ShapeDtypeStruct((NKV, S, HD), F32), jax.ShapeDtypeStruct((NKV, S, HD), F32)],
        compiler_params=_cparams("parallel", "arbitrary"),
    )(qt, q2, kh, kt, vh, dot_, do2, lse, delta, bias)


CONV_BLK = 256
CONV_COL0 = 1536 // CONV_BLK


def _shift_down(u, j, row):
    return jnp.where(row >= j, pltpu.roll(u, j, 0), 0.0)


def _conv_pre(u, w_ref, b_ref, row):
    y = b_ref[...] + w_ref[CONV_K - 1:CONV_K, :] * u
    for j in range(1, CONV_K):
        y = y + w_ref[CONV_K - 1 - j:CONV_K - j, :] * _shift_down(u, j, row)
    return y


def _conv_fwd(proj, convw, convb):
    def body(u_ref, w_ref, b_ref, o_ref):
        u = u_ref[...]
        row = lax.broadcasted_iota(jnp.int32, u.shape, 0)
        y = _conv_pre(u, w_ref, b_ref, row)
        o_ref[...] = y * _sigmoid(y)

    return pl.pallas_call(
        body, name="conv_fwd", grid=(CONV_C // CONV_BLK,),
        in_specs=[pl.BlockSpec((S, CONV_BLK), lambda i: (0, CONV_COL0 + i)),
                  pl.BlockSpec((CONV_K, CONV_BLK), lambda i: (0, i)),
                  pl.BlockSpec((1, CONV_BLK), lambda i: (0, i))],
        out_specs=pl.BlockSpec((S, CONV_BLK), lambda i: (0, i)),
        out_shape=jax.ShapeDtypeStruct((S, CONV_C), F32), compiler_params=_cparams("parallel"),
    )(proj, convw, convb)


def _conv_bwd(dact, proj, convw, convb):
    def body(da_ref, u_ref, w_ref, b_ref, du_ref, dw_ref, db_ref):
        u = u_ref[...]
        row = lax.broadcasted_iota(jnp.int32, u.shape, 0)
        y = _conv_pre(u, w_ref, b_ref, row)
        sg = _sigmoid(y)
        dy = da_ref[...] * (sg * (1.0 + y * (1.0 - sg)))
        db_ref[...] = jnp.sum(dy, axis=0, keepdims=True)
        du = w_ref[CONV_K - 1:CONV_K, :] * dy
        r8 = lax.broadcasted_iota(jnp.int32, (8, CONV_BLK), 0)
        dw = jnp.where(r8 == CONV_K - 1, jnp.sum(dy * u, axis=0, keepdims=True), 0.0)
        for j in range(1, CONV_K):
            du = du + w_ref[CONV_K - 1 - j:CONV_K - j, :] * jnp.where(row < S - j, pltpu.roll(dy, S - j, 0), 0.0)
            dw = dw + jnp.where(r8 == CONV_K - 1 - j,
                                jnp.sum(dy * _shift_down(u, j, row), axis=0, keepdims=True), 0.0)
        du_ref[...] = du.astype(BF16)
        dw_ref[...] = dw

    return pl.pallas_call(
        body, name="conv_bwd", grid=(CONV_C // CONV_BLK,),
        in_specs=[pl.BlockSpec((S, CONV_BLK), lambda i: (0, i)),
                  pl.BlockSpec((S, CONV_BLK), lambda i: (0, CONV_COL0 + i)),
                  pl.BlockSpec((CONV_K, CONV_BLK), lambda i: (0, i)),
                  pl.BlockSpec((1, CONV_BLK), lambda i: (0, i))],
        out_specs=[pl.BlockSpec((S, CONV_BLK), lambda i: (0, i)), pl.BlockSpec((8, CONV_BLK), lambda i: (0, i)),
                   pl.BlockSpec((1, CONV_BLK), lambda i: (0, i))],
        out_shape=[jax.ShapeDtypeStruct((S, CONV_C), BF16), jax.ShapeDtypeStruct((8, CONV_C), F32),
                   jax.ShapeDtypeStruct((1, CONV_C), F32)],
        compiler_params=_cparams("parallel"),
    )(dact, proj, convw, convb)


NPAIR = 8


def _ssd_scalars(dtr_ref, dtb_ref, alog_ref):
    z = dtr_ref[...] + dtb_ref[...]
    dt = jnp.maximum(z, 0.0) + jnp.log(1.0 + jnp.exp(-jnp.abs(z)))
    a = -jnp.exp(alog_ref[...])
    r = lax.broadcasted_iota(jnp.int32, (128, 128), 0)
    c = lax.broadcasted_iota(jnp.int32, (128, 128), 1)
    tri = (r >= c).astype(F32)
    cs = _dot_exact(tri, dt * a)
    return z, dt, a, cs, r, c


def _pair_terms(cs, cst, dt, h1, h2, lo):
    c1, c2 = cs[:, h1:h1 + 1], cs[:, h2:h2 + 1]
    l1, l2 = cs[127:128, h1:h1 + 1], cs[127:128, h2:h2 + 1]
    e_l = jnp.where(lo, jnp.exp(c1), jnp.exp(c2))
    dte1, dte2 = jnp.exp(l1 - c1), jnp.exp(l2 - c2)
    dte_l = jnp.where(lo, dte1, dte2)
    dt_l = jnp.where(lo, dt[:, h1:h1 + 1], dt[:, h2:h2 + 1])
    return c1, c2, jnp.exp(l1), jnp.exp(l2), e_l, dte1, dte2, dte_l, dt_l


def _gate_norm(y, zv, w):
    yg = y * (zv * _sigmoid(zv))
    outs, rs = [], []
    for g in range(2):
        blk = yg[:, 512 * g:512 * (g + 1)]
        r = lax.rsqrt(jnp.mean(blk * blk, axis=-1, keepdims=True) + EPS)
        outs.append(blk * r)
        rs.append(r)
    return jnp.concatenate(outs, axis=1), rs, yg


def _ssd_fwd(xbc, proj, dtb, alog, dskip_l, ssmw):
    def body(x_ref, b_ref, c_ref, dtr_ref, z_ref, dtb_ref, alog_ref, dsk_ref, w_ref, y_ref, yn_ref, hp_ref, h_ref):
        @pl.when(pl.program_id(0) == 0)
        def _():
            h_ref[...] = jnp.zeros_like(h_ref)

        _, dt, _, cs, r, c = _ssd_scalars(dtr_ref, dtb_ref, alog_ref)
        cst = cs.T
        causal = r >= c
        lo = c < HD
        hp_ref[...] = h_ref[...]
        for g in range(2):
            bg = b_ref[:, 128 * g:128 * (g + 1)]
            cg = c_ref[:, 128 * g:128 * (g + 1)]
            cb = _dot(cg, bg, NT)
            for j in range(4):
                pj = 4 * g + j
                h1, h2 = 2 * pj, 2 * pj + 1
                sl = slice(128 * pj, 128 * (pj + 1))
                xp = x_ref[:, sl]
                c1, c2, cd1, cd2, e_l, _, _, dte_l, dt_l = _pair_terms(cs, cst, dt, h1, h2, lo)
                xdt = xp * dt_l
                m1 = cb * jnp.exp(jnp.where(causal, c1 - cst[h1:h1 + 1, :], NEG))
                m2 = cb * jnp.exp(jnp.where(causal, c2 - cst[h2:h2 + 1, :], NEG))
                yd = jnp.where(lo, _dot(m1, xdt, NN), _dot(m2, xdt, NN))
                hp = h_ref[pj]
                yo = _dot(cg, hp, NT) * e_l
                st = _dot(xdt * dte_l, bg, TN)
                h_ref[pj] = hp * jnp.where(r < HD, cd1, cd2) + st
                y_ref[:, sl] = yd + yo + dsk_ref[:, sl] * xp
        yn, _, _ = _gate_norm(y_ref[...], z_ref[...], w_ref[...])
        yn_ref[...] = (yn * w_ref[...]).astype(BF16)

    return pl.pallas_call(
        body, name="ssd_fwd", grid=(NCH,),
        in_specs=[pl.BlockSpec((128, SSM_W), lambda i: (i, 0)),
                  pl.BlockSpec((128, 256), lambda i: (i, 4)), pl.BlockSpec((128, 256), lambda i: (i, 5)),
                  pl.BlockSpec((128, 128), lambda i: (i, COL_DT // 128)),
                  pl.BlockSpec((128, SSM_W), lambda i: (i, 3)),
                  pl.BlockSpec((1, 128), lambda i: (0, 0)), pl.BlockSpec((1, 128), lambda i: (0, 0)),
                  pl.BlockSpec((1, SSM_W), lambda i: (0, 0)), pl.BlockSpec((1, SSM_W), lambda i: (0, 0))],
        out_specs=[pl.BlockSpec((128, SSM_W), lambda i: (i, 0)), pl.BlockSpec((128, SSM_W), lambda i: (i, 0)),
                   pl.BlockSpec((None, NPAIR, 128, 128), lambda i: (i, 0, 0, 0))],
        out_shape=[jax.ShapeDtypeStruct((S, SSM_W), F32), jax.ShapeDtypeStruct((S, SSM_W), BF16),
                   jax.ShapeDtypeStruct((NCH, NPAIR, 128, 128), F32)],
        scratch_shapes=[pltpu.VMEM((NPAIR, 128, 128), F32)],
        compiler_params=_cparams("arbitrary"),
    )(xbc, xbc, xbc, proj, proj, dtb, alog, dskip_l, ssmw)


def _ssd_bwd(dmixed, y, xbc, proj, hprev, dtb, alog, dskip_l, ssmw):
    def body(dyn_ref, y_ref, x_ref, b_ref, c_ref, dtr_ref, z_ref, hp_ref, dtb_ref, alog_ref, dsk_ref, w_ref,
             dxbc_ref, dz_ref, ddt_ref, dw_ref, dsc_ref, g_ref):
        @pl.when(pl.program_id(0) == 0)
        def _():
            g_ref[...] = jnp.zeros_like(g_ref)
            dsc_ref[...] = jnp.zeros_like(dsc_ref)

        z, dt, a, cs, r, c = _ssd_scalars(dtr_ref, dtb_ref, alog_ref)
        cst = cs.T
        causal = r >= c
        lo = c < HD

        yv = y_ref[...]
        zv = z_ref[...]
        wv = w_ref[...]
        ygn, rs, yg = _gate_norm(yv, zv, wv)
        dyn = dyn_ref[...]
        _acc_rows(dw_ref, dyn * ygn)
        dynw = dyn * wv
        parts = []
        for g in range(2):
            sl = slice(512 * g, 512 * (g + 1))
            a_g, n_g = dynw[:, sl], ygn[:, sl]
            parts.append(rs[g] * (a_g - n_g * jnp.mean(a_g * n_g, axis=-1, keepdims=True)))
        dyg = jnp.concatenate(parts, axis=1)
        sz = _sigmoid(zv)
        dz_ref[...] = (dyg * yv * (sz * (1.0 + zv * (1.0 - sz)))).astype(BF16)
        dy_all = dyg * (zv * sz)

        dcs_cols = jnp.zeros((128, 128), F32)
        dcs_rows = jnp.zeros((128, 128), F32)
        ddt_x = jnp.zeros((128, 128), F32)
        dd_row = jnp.zeros((1, 128), F32)
        last = r == 127
        for g in range(2):
            bg = b_ref[:, 128 * g:128 * (g + 1)]
            cg = c_ref[:, 128 * g:128 * (g + 1)]
            cb = _dot(cg, bg, NT)
            dcb = jnp.zeros((128, 128), F32)
            db_acc = jnp.zeros((128, NST), F32)
            dc_acc = jnp.zeros((128, NST), F32)
            for j in range(4):
                pj = 4 * g + j
                h1, h2 = 2 * pj, 2 * pj + 1
                sl = slice(128 * pj, 128 * (pj + 1))
                xp = x_ref[:, sl]
                dyp = dy_all[:, sl]
                c1, c2, cd1, cd2, e_l, dte1, dte2, dte_l, dt_l = _pair_terms(cs, cst, dt, h1, h2, lo)
                xdt = xp * dt_l
                hp = hp_ref[pj]
                gp = g_ref[pj]
                dxp = dsk_ref[:, sl] * dyp
                dyx = dyp * xp
                dd_row = dd_row + jnp.where(c[0:1, :] == h1, jnp.sum(jnp.where(lo, dyx, 0.0), keepdims=True), 0.0) \
                    + jnp.where(c[0:1, :] == h2, jnp.sum(jnp.where(lo, 0.0, dyx), keepdims=True), 0.0)
                dzs = dyp * e_l
                dc_acc = dc_acc + _dot(dzs, hp, NN)
                g_from = _dot(dzs, cg, TN)
                ryo = dyp * (_dot(cg, hp, NT) * e_l)
                k1 = jnp.sum(jnp.where(lo, ryo, 0.0), axis=1, keepdims=True)
                k2 = jnp.sum(jnp.where(lo, 0.0, ryo), axis=1, keepdims=True)
                qm = _dot(bg, gp, NT)
                dxdt = qm * dte_l
                qx = qm * xdt
                t1 = jnp.sum(jnp.where(lo, qx, 0.0), axis=1, keepdims=True) * dte1
                t2 = jnp.sum(jnp.where(lo, 0.0, qx), axis=1, keepdims=True) * dte2
                db_acc = db_acc + _dot(xdt * dte_l, gp, NN)
                gh = gp * hp
                dl1 = jnp.sum(t1, keepdims=True) + jnp.sum(jnp.where(r < HD, gh, 0.0), keepdims=True) * cd1
                dl2 = jnp.sum(t2, keepdims=True) + jnp.sum(jnp.where(r < HD, 0.0, gh), keepdims=True) * cd2
                g_ref[pj] = g_from + jnp.where(r < HD, cd1, cd2) * gp
                k1 = k1 - t1 + jnp.where(last[:, 0:1], dl1, 0.0)
                k2 = k2 - t2 + jnp.where(last[:, 0:1], dl2, 0.0)
                for hh, ch, msk in ((h1, c1, lo), (h2, c2, jnp.logical_not(lo))):
                    lm = jnp.exp(jnp.where(causal, ch - cst[hh:hh + 1, :], NEG))
                    mm = cb * lm
                    dm = jnp.where(causal, _dot(jnp.where(msk, dyp, 0.0), xdt, NT), 0.0)
                    w = dm * mm
                    kk = jnp.sum(w, axis=1, keepdims=True)
                    if hh == h1:
                        k1 = k1 + kk
                    else:
                        k2 = k2 + kk
                    dcs_rows = dcs_rows + jnp.where(r == hh, jnp.sum(w, axis=0, keepdims=True), 0.0)
                    dcb = dcb + dm * lm
                    dxdt = dxdt + jnp.where(msk, _dot(mm, dyp, TN), 0.0)
                dcs_cols = dcs_cols + jnp.where(c == h1, k1, 0.0) + jnp.where(c == h2, k2, 0.0)
                dxx = dxdt * xp
                ddt_x = ddt_x + jnp.where(c == h1, jnp.sum(jnp.where(lo, dxx, 0.0), axis=1, keepdims=True), 0.0) \
                    + jnp.where(c == h2, jnp.sum(jnp.where(lo, 0.0, dxx), axis=1, keepdims=True), 0.0)
                dxbc_ref[:, sl] = dxp + dxdt * dt_l
            dxbc_ref[:, SSM_W + 128 * g:SSM_W + 128 * (g + 1)] = db_acc + _dot(dcb, cg, TN)
            dxbc_ref[:, SSM_W + 256 + 128 * g:SSM_W + 256 + 128 * (g + 1)] = dc_acc + _dot(dcb, bg, NN)

        dcs = dcs_cols - dcs_rows.T
        dad = _dot_exact((c >= r).astype(F32), dcs)
        ddt = dad * a + ddt_x
        ddtr = jnp.where(c < 16, ddt * _sigmoid(z), 0.0)
        ddt_ref[...] = ddtr.astype(BF16)
        r8 = lax.broadcasted_iota(jnp.int32, (8, 128), 0)
        dsc_ref[...] += (jnp.where(r8 == 0, jnp.sum(ddtr, axis=0, keepdims=True), 0.0)
                         + jnp.where(r8 == 1, jnp.sum(dad * dt, axis=0, keepdims=True) * a, 0.0)
                         + jnp.where(r8 == 2, dd_row, 0.0))

    rev = NCH - 1
    return pl.pallas_call(
        body, name="ssd_bwd", grid=(NCH,),
        in_specs=[pl.BlockSpec((128, SSM_W), lambda i: (rev - i, 1)),
                  pl.BlockSpec((128, SSM_W), lambda i: (rev - i, 0)),
                  pl.BlockSpec((128, SSM_W), lambda i: (rev - i, 0)),
                  pl.BlockSpec((128, 256), lambda i: (rev - i, 4)), pl.BlockSpec((128, 256), lambda i: (rev - i, 5)),
                  pl.BlockSpec((128, 128), lambda i: (rev - i, COL_DT // 128)),
                  pl.BlockSpec((128, SSM_W), lambda i: (rev - i, 3)),
                  pl.BlockSpec((None, NPAIR, 128, 128), lambda i: (rev - i, 0, 0, 0)),
                  pl.BlockSpec((1, 128), lambda i: (0, 0)), pl.BlockSpec((1, 128), lambda i: (0, 0)),
                  pl.BlockSpec((1, SSM_W), lambda i: (0, 0)), pl.BlockSpec((1, SSM_W), lambda i: (0, 0))],
        out_specs=[pl.BlockSpec((128, CONV_C), lambda i: (rev - i, 0)),
                   pl.BlockSpec((128, SSM_W), lambda i: (rev - i, 0)),
                   pl.BlockSpec((128, 128), lambda i: (rev - i, 0)),
                   pl.BlockSpec((1, SSM_W), lambda i: (0, 0)), pl.BlockSpec((8, 128), lambda i: (0, 0))],
        out_shape=[jax.ShapeDtypeStruct((S, CONV_C), F32), jax.ShapeDtypeStruct((S, SSM_W), BF16),
                   jax.ShapeDtypeStruct((S, 128), BF16), jax.ShapeDtypeStruct((1, SSM_W), F32),
                   jax.ShapeDtypeStruct((8, 128), F32)],
        scratch_shapes=[pltpu.VMEM((NPAIR, 128, 128), F32)],
        compiler_params=_cparams("arbitrary"),
    )(dmixed, y, xbc, xbc, xbc, proj, proj, hprev, dtb, alog, dskip_l, ssmw)


def _cast_stack(name, slot, arrs, tr, tc):
    n = len(arrs)
    rows, cols = arrs[0].shape

    def body(s_ref, *refs):
        for i in range(n):
            refs[n][i] = refs[i][...].astype(BF16)

    return pl.pallas_call(
        body, name=name,
        grid_spec=pltpu.PrefetchScalarGridSpec(
            num_scalar_prefetch=1, grid=(rows // tr, cols // tc),
            in_specs=[pl.BlockSpec((tr, tc), lambda i, j, sr: (i, j))] * n,
            out_specs=pl.BlockSpec((None, n, tr, tc), lambda i, j, sr: (sr[0], 0, i, j))),
        out_shape=jax.ShapeDtypeStruct((NSH, n, rows, cols), BF16),
        compiler_params=_cparams("parallel", "parallel"),
    )(slot, *arrs)


def _pair_sum(name, c_idx, ps, rs, th):
    n = len(ps)
    _, rows, _ = ps[0].shape

    def body(c_ref, *refs):
        for i in range(n):
            refs[2 * n + i][...] = (refs[i][...].astype(F32) + refs[n + i][...].astype(F32)).astype(BF16)

    spec = pl.BlockSpec((None, th, HALF), lambda s, i, cr: (s, i, 0))
    return pl.pallas_call(
        body, name=name,
        grid_spec=pltpu.PrefetchScalarGridSpec(
            num_scalar_prefetch=1, grid=(NSH, rows // th),
            in_specs=[pl.BlockSpec((None, th, HALF), lambda s, i, cr: (s, i, cr[0]))] * n + [spec] * n,
            out_specs=[spec] * n),
        out_shape=[jax.ShapeDtypeStruct((NSH, rows, HALF), BF16)] * n,
        compiler_params=_cparams("parallel", "parallel"),
    )(c_idx, *ps, *rs)


def _chip_sum(name, place, cs, ts, th):
    n = len(ts)
    _, rows, _ = ts[0].shape

    def body(p_ref, *refs):
        for i in range(n):
            t = refs[n + i][...].astype(F32)
            refs[2 * n + i][...] = ((refs[i][...].astype(F32) + t[0]) + t[1]) + t[2]

    return pl.pallas_call(
        body, name=name,
        grid_spec=pltpu.PrefetchScalarGridSpec(
            num_scalar_prefetch=1, grid=(rows // th,),
            in_specs=[pl.BlockSpec((None, th, HALF), lambda i, pr: (pr[0], i, 0))] * n
            + [pl.BlockSpec((3, th, HALF), lambda i, pr: (0, i, 0))] * n,
            out_specs=[pl.BlockSpec((th, HALF), lambda i, pr: (i, pr[1]))] * n),
        out_shape=[jax.ShapeDtypeStruct((rows, D), F32)] * n, compiler_params=_cparams("parallel"),
    )(place, *cs, *ts)


def _adamw(name, ws, gs, ms, vs, tr, tc):
    n = len(ws)
    rows, cols = ws[0].shape
    c1 = 1.0 / (1.0 - ADAM_B1 ** ADAM_STEP)
    c2 = 1.0 / (1.0 - ADAM_B2 ** ADAM_STEP)

    def body(*refs):
        for i in range(n):
            w, g, m, v = (refs[k * n + i][...] for k in range(4))
            m2 = ADAM_B1 * m + (1.0 - ADAM_B1) * g
            v2 = ADAM_B2 * v + (1.0 - ADAM_B2) * (g * g)
            refs[4 * n + 3 * i][...] = -ADAM_LR * ((m2 * c1) / (jnp.sqrt(v2 * c2) + ADAM_EPS) + ADAM_WD * w)
            refs[4 * n + 3 * i + 1][...] = m2
            refs[4 * n + 3 * i + 2][...] = v2

    spec = pl.BlockSpec((tr, tc), lambda i, j: (i, j))
    outs = pl.pallas_call(
        body, name=name, grid=(rows // tr, cols // tc), in_specs=[spec] * (4 * n), out_specs=[spec] * (3 * n),
        out_shape=[jax.ShapeDtypeStruct((rows, cols), F32)] * (3 * n),
        compiler_params=_cparams("parallel", "parallel"),
    )(*ws, *gs, *ms, *vs)
    return [tuple(outs[3 * i:3 * i + 3]) for i in range(n)]


def _place():
    x, y, c = lax.axis_index("x"), lax.axis_index("y"), lax.axis_index("c")
    chips = [(1 - x, y), (x, 1 - y), (1 - x, 1 - y)]
    return x, y, c, chips


def _any_specs(n):
    return [pl.BlockSpec(memory_space=pl.ANY)] * n


def _gather_weights(ffnw, wint, wout, cw):
    ins = [ffnw, wint, wout]
    halves = [a.shape[1] // 2 for a in ins]
    nb = len(ins)

    def body(*refs):
        cw_in = refs[nb]
        dst, cw_out = refs[nb + 1:2 * nb + 1], refs[2 * nb + 1]
        send, recv, fsend, frecv, local, cws, cwr = refs[2 * nb + 2:]
        x, y, c, chips = _place()
        me = 2 * x + y

        def half(ref, b, s, hc):
            if b == 1:
                return ref.at[s, :, pl.ds(hc * HALF, HALF)]
            return ref.at[s, pl.ds(hc * halves[b], halves[b])]

        own = [pltpu.make_async_copy(cw_in, cw_out.at[me], local.at[0])]
        for cp in own:
            cp.start()
        first, passed = [], []
        for j, chip in enumerate(chips):
            for b in range(nb):
                mine = half(dst[b], b, me, c)
                first.append(pltpu.make_async_remote_copy(
                    src_ref=mine, dst_ref=mine, send_sem=send.at[j * nb + b], recv_sem=recv.at[j * nb + b],
                    device_id=(chip[0], chip[1], c), device_id_type=MESH))
            first.append(pltpu.make_async_remote_copy(
                src_ref=cw_in, dst_ref=cw_out.at[me], send_sem=cws.at[j], recv_sem=cwr.at[j],
                device_id=(chip[0], chip[1], c), device_id_type=MESH))
        for cp in first:
            cp.start()
        for j, chip in enumerate(chips):
            s = 2 * chip[0] + chip[1]
            for b in range(nb):
                landed = half(dst[b], b, s, c)
                pltpu.make_async_remote_copy(
                    src_ref=landed, dst_ref=landed, send_sem=send.at[j * nb + b], recv_sem=recv.at[j * nb + b],
                    device_id=(x, y, c), device_id_type=MESH).wait_recv()
                fw = pltpu.make_async_remote_copy(
                    src_ref=landed, dst_ref=landed, send_sem=fsend.at[j * nb + b], recv_sem=frecv.at[j * nb + b],
                    device_id=(x, y, 1 - c), device_id_type=MESH)
                fw.start()
                passed.append(fw)
        for j, chip in enumerate(chips):
            s = 2 * chip[0] + chip[1]
            for b in range(nb):
                other = half(dst[b], b, s, 1 - c)
                pltpu.make_async_remote_copy(
                    src_ref=other, dst_ref=other, send_sem=fsend.at[j * nb + b], recv_sem=frecv.at[j * nb + b],
                    device_id=(x, y, c), device_id_type=MESH).wait_recv()
            pltpu.make_async_remote_copy(
                src_ref=cw_in, dst_ref=cw_out.at[s], send_sem=cws.at[j], recv_sem=cwr.at[j],
                device_id=(x, y, c), device_id_type=MESH).wait_recv()
        for cp in first + passed:
            cp.wait_send()
        for cp in own:
            cp.wait()

    outs = [jax.ShapeDtypeStruct(a.shape, a.dtype) for a in ins] + [jax.ShapeDtypeStruct((NSH,) + cw.shape, cw.dtype)]
    return pl.pallas_call(
        body, name="gather_weights", in_specs=_any_specs(nb + 1), out_specs=_any_specs(nb + 1), out_shape=outs,
        input_output_aliases={b: b for b in range(nb)},
        scratch_shapes=[pltpu.SemaphoreType.DMA((3 * nb,)), pltpu.SemaphoreType.DMA((3 * nb,)),
                        pltpu.SemaphoreType.DMA((3 * nb,)), pltpu.SemaphoreType.DMA((3 * nb,)),
                        pltpu.SemaphoreType.DMA((1,)), pltpu.SemaphoreType.DMA((3,)),
                        pltpu.SemaphoreType.DMA((3,))],
    )(*ins, cw)


def _to_sibling(ps):
    n = len(ps)

    def body(*refs):
        src, dst, send, recv = refs[:n], refs[n:2 * n], refs[2 * n], refs[2 * n + 1]
        x, y, c, _ = _place()
        cps = [pltpu.make_async_remote_copy(
            src_ref=src[i].at[:, :, pl.ds((1 - c) * HALF, HALF)], dst_ref=dst[i], send_sem=send.at[i],
            recv_sem=recv.at[i], device_id=(x, y, 1 - c), device_id_type=MESH) for i in range(n)]
        for cp in cps:
            cp.start()
        for cp in cps:
            cp.wait()

    outs = [jax.ShapeDtypeStruct(p.shape[:2] + (HALF,), p.dtype) for p in ps]
    return pl.pallas_call(
        body, name="grads_to_sibling", in_specs=_any_specs(n), out_specs=_any_specs(n), out_shape=outs,
        scratch_shapes=[pltpu.SemaphoreType.DMA((n,)), pltpu.SemaphoreType.DMA((n,))],
    )(*ps)


def _to_chips(cs):
    n = len(cs)

    def body(*refs):
        src, dst, send, recv = refs[:n], refs[n:2 * n], refs[2 * n], refs[2 * n + 1]
        x, y, c, chips = _place()
        cps = []
        for j, chip in enumerate(chips):
            s = 2 * chip[0] + chip[1]
            for i in range(n):
                cps.append(pltpu.make_async_remote_copy(
                    src_ref=src[i].at[s], dst_ref=dst[i].at[j], send_sem=send.at[j * n + i],
                    recv_sem=recv.at[j * n + i], device_id=(chip[0], chip[1], c), device_id_type=MESH))
        for cp in cps:
            cp.start()
        for cp in cps:
            cp.wait()

    return pl.pallas_call(
        body, name="grads_to_chips", in_specs=_any_specs(n), out_specs=_any_specs(n),
        out_shape=[jax.ShapeDtypeStruct((3,) + a.shape[1:], a.dtype) for a in cs],
        scratch_shapes=[pltpu.SemaphoreType.DMA((3 * n,)), pltpu.SemaphoreType.DMA((3 * n,))],
    )(*cs)


def _swap_halves(gs):
    n = len(gs)

    def body(*refs):
        dst, send, recv = refs[n:2 * n], refs[2 * n], refs[2 * n + 1]
        x, y, c, _ = _place()
        cps = []
        for i in range(n):
            mine = dst[i].at[:, pl.ds(c * HALF, HALF)]
            cps.append(pltpu.make_async_remote_copy(
                src_ref=mine, dst_ref=mine, send_sem=send.at[i], recv_sem=recv.at[i],
                device_id=(x, y, 1 - c), device_id_type=MESH))
        for cp in cps:
            cp.start()
        for i in range(n):
            other = dst[i].at[:, pl.ds((1 - c) * HALF, HALF)]
            pltpu.make_async_remote_copy(
                src_ref=other, dst_ref=other, send_sem=send.at[i], recv_sem=recv.at[i],
                device_id=(x, y, c), device_id_type=MESH).wait_recv()
        for cp in cps:
            cp.wait_send()

    return pl.pallas_call(
        body, name="grads_swap_halves", in_specs=_any_specs(n), out_specs=_any_specs(n),
        out_shape=[jax.ShapeDtypeStruct(g.shape, g.dtype) for g in gs],
        input_output_aliases={i: i for i in range(n)},
        scratch_shapes=[pltpu.SemaphoreType.DMA((n,)), pltpu.SemaphoreType.DMA((n,))],
    )(*gs)


SMALL_ROWS = 16


def _allreduce_small(vec):
    def body(v_ref, o_ref, buf, send, recv):
        x, y, c, _ = _place()
        me = 4 * x + 2 * y + c
        buf[me] = v_ref[...]
        cps = []
        for k in range(1, 8):
            peer = (x ^ (k >> 2), y ^ ((k >> 1) & 1), c ^ (k & 1))
            cps.append(pltpu.make_async_remote_copy(
                src_ref=v_ref, dst_ref=buf.at[me], send_sem=send.at[k - 1], recv_sem=recv.at[k - 1],
                device_id=peer, device_id_type=MESH))
        for cp in cps:
            cp.start()
        for k in range(1, 8):
            pltpu.make_async_remote_copy(
                src_ref=v_ref, dst_ref=buf.at[me ^ k], send_sem=send.at[k - 1], recv_sem=recv.at[k - 1],
                device_id=(x, y, c), device_id_type=MESH).wait_recv()
        for cp in cps:
            cp.wait_send()
        t = buf[0]
        for d in range(1, 8):
            t = t + buf[d]
        o_ref[...] = t

    return pl.pallas_call(
        body, name="allreduce_small",
        in_specs=[pl.BlockSpec(memory_space=pltpu.VMEM)], out_specs=pl.BlockSpec(memory_space=pltpu.VMEM),
        out_shape=jax.ShapeDtypeStruct((SMALL_ROWS, D), F32),
        scratch_shapes=[pltpu.VMEM((8, SMALL_ROWS, D), F32), pltpu.SemaphoreType.DMA((7,)),
                        pltpu.SemaphoreType.DMA((7,))],
    )(vec)


ROPE_THETA = 10000.0
SMALL_1K = ("ffn1_pre_norm", "ffn1_post_norm", "mix_pre_norm", "ssm_norm", "mix_post_norm",
            "ffn2_pre_norm", "ffn2_post_norm")
SMALL_16 = ("dt_bias", "a_log", "d_skip")
OFF_CONVB = 7 * D
OFF_16 = OFF_CONVB + CONV_C
OFF_CONVW = OFF_16 + 48
OFF_LOSS = OFF_CONVW + CONV_K * CONV_C
SMALL_LEN = SMALL_ROWS * D


def _sds(shape, dtype):
    return jax.ShapeDtypeStruct(shape, dtype)


def _ffn_down(name, act, ffnw, wi):
    return _mm(name, [act, ffnw], NN, (S // TS, NSH),
               [pl.BlockSpec((None, TS, FS), lambda i, s: (s, i, 0)),
                pl.BlockSpec((None, None, FS, D), lambda i, s: (s, wi + 2, 0, 0))],
               pl.BlockSpec((TS, D), lambda i, s: (i, 0)), _sds((S, D), F32), (TS, D))


def _ffn_bwd(tag, dh, n, gate, up, act, ffnw, wi):
    dgate, dup = _ffn_dact(tag + "_dact", dh, ffnw, wi, gate, up)
    aspec = pl.BlockSpec((None, TS, FS), lambda s, k: (s, k, 0))
    nspec = pl.BlockSpec((TS, D), lambda s, k: (k, 0))
    wspec = pl.BlockSpec((None, FS, D), lambda s, k: (s, 0, 0))
    dws = [_mm(tag + nm, [a, b], TN, (NSH, S // TS), [aspec, nspec], wspec, _sds((NSH, FS, D), BF16), (FS, D))
           for nm, a, b in (("_dwg", dgate, n), ("_dwu", dup, n), ("_dwd", act, dh))]
    a2 = pl.BlockSpec((None, TS, FS), lambda i, s: (s, i, 0))
    dn = _mm(tag + "_dn", [dgate, ffnw, dup, ffnw], NN, (S // TS, NSH),
             [a2, pl.BlockSpec((None, None, FS, D), lambda i, s: (s, wi, 0, 0)),
              a2, pl.BlockSpec((None, None, FS, D), lambda i, s: (s, wi + 1, 0, 0))],
             pl.BlockSpec((TS, D), lambda i, s: (i, 0)), _sds((S, D), F32), (TS, D))
    return dn, dws


def _heads(t, n):
    return t.reshape(S, n, HD).transpose(1, 0, 2)


def _unheads(t):
    return t.transpose(1, 0, 2).reshape(S, t.shape[0] * HD)


def _heads_t(t, n):
    return t.reshape(S, n, HD).transpose(1, 2, 0)


def _blocks5(t):
    return t.reshape(NCH, 128, NKV, NQ_PER_KV, HD)


def _to_blocks_t(t):
    return _blocks5(t).transpose(2, 0, 4, 3, 1).reshape(NKV, NCH, HD, QROWS)


def _to_blocks(t):
    return _blocks5(t).transpose(2, 0, 3, 1, 4).reshape(NKV, NCH, QROWS, HD)


def _from_blocks_t(t):
    return t.reshape(NKV, NCH, HD, NQ_PER_KV, 128).transpose(1, 4, 0, 3, 2).reshape(S, D)


def _pad128(v):
    return jnp.pad(v, ((0, 0), (0, 128 - v.shape[1])))


def _local_step(x, positions, tgt, sp, ffnw, wint_pad, wout, convw):
    inv_freq = ROPE_THETA ** (-jnp.arange(0, HD, 2, dtype=F32) / HD)
    ang = positions.astype(F32)[:, None] * inv_freq
    ang = jnp.concatenate([ang, ang, ang, ang], axis=-1)
    cos, sin = jnp.cos(ang), jnp.sin(ang)
    dtb, alog = _pad128(sp["dt_bias"]), _pad128(sp["a_log"])
    dskip_l = jnp.repeat(sp["d_skip"], HD, axis=1)
    convb = sp["conv_b"]

    n1 = _prenorm("prenorm1", x, sp["ffn1_pre_norm"])
    gate1, up1, act1 = _ffn_up("ffn1_up", n1, ffnw, 0)
    h1 = _ffn_down("ffn1_down", act1, ffnw, 0)
    x1, n2 = _postres("postres1", x, h1, sp["ffn1_post_norm"], 0.5, sp["mix_pre_norm"])

    pw = WIN_PAD // 3
    proj = _mm("in_proj", [n2, wint_pad], NT, (S // TS, 3, 1),
               [pl.BlockSpec((TS, D), lambda i, j, k: (i, 0)), pl.BlockSpec((pw, D), lambda i, j, k: (j, 0))],
               pl.BlockSpec((TS, pw), lambda i, j, k: (i, j)), _sds((S, WIN_PAD), F32), (TS, pw))
    q_rot = _rope("rope_q", proj, 0, D, cos, sin, 1.0, HD ** -0.5)
    k_rot = _rope("rope_k", proj, D // KVW, KVW, cos, sin, 1.0, 1.0)
    v_bf = proj[:, D + KVW:D + 2 * KVW].astype(BF16)
    qt, kh, vh = _to_blocks_t(q_rot), _heads(k_rot, NKV), _heads(v_bf, NKV)
    kt, vt = _heads_t(k_rot, NKV), _heads_t(v_bf, NKV)
    bias = _bias_table()
    ot, lse = _attn_fwd(qt, kh, vt, bias)
    attn = _from_blocks_t(ot).astype(BF16)
    xbc = _conv_fwd(proj, convw, convb)
    y, yn, hprev = _ssd_fwd(xbc, proj, dtb, alog, dskip_l, sp["ssm_norm"])
    mixed = jnp.concatenate([attn, yn], axis=1)
    h2 = _mm("out_proj", [mixed, wout], NN, (S // TS, 1),
             [pl.BlockSpec((TS, 2 * D), lambda i, k: (i, 0)), pl.BlockSpec((2 * D, D), lambda i, k: (0, 0))],
             pl.BlockSpec((TS, D), lambda i, k: (i, 0)), _sds((S, D), F32), (TS, D))
    x2, n3 = _postres("postres2", x1, h2, sp["mix_post_norm"], 1.0, sp["ffn2_pre_norm"])

    gate2, up2, act2 = _ffn_up("ffn2_up", n3, ffnw, 3)
    h3 = _ffn_down("ffn2_down", act2, ffnw, 3)
    dy, dh3, dp3, loss = _final(x2, h3, sp["ffn2_post_norm"], tgt, 0.5)

    dn3, dws2 = _ffn_bwd("ffn2", dh3, n3, gate2, up2, act2, ffnw, 3)
    dx2, dh2, dg3, dp2 = _mid_bwd("mid_bwd2", dy, dn3, x2, sp["ffn2_pre_norm"], h2, sp["mix_post_norm"], 1.0)

    dmixed = _mm("out_proj_dx", [dh2, wout], NT, (S // TS, 1),
                 [pl.BlockSpec((TS, D), lambda i, k: (i, 0)), pl.BlockSpec((2 * D, D), lambda i, k: (0, 0))],
                 pl.BlockSpec((TS, 2 * D), lambda i, k: (i, 0)), _sds((S, 2 * D), F32), (TS, 2 * D))
    dwout = _mm("out_proj_dw", [mixed, dh2], TN, (2, S // TS),
                [pl.BlockSpec((TS, D), lambda m, k: (k, m)), pl.BlockSpec((TS, D), lambda m, k: (k, 0))],
                pl.BlockSpec((D, D), lambda m, k: (m, 0)), _sds((2 * D, D), BF16), (D, D))
    dxbc, dz, ddt, dssm, dsc = _ssd_bwd(dmixed, y, xbc, proj, hprev, dtb, alog, dskip_l, sp["ssm_norm"])
    du, dcw8, dcb = _conv_bwd(dxbc, proj, convw, convb)
    do_bf = dmixed[:, :D].astype(BF16)
    dot_ = _to_blocks_t(do_bf)
    dqt, dkh, dvh = _attn_bwd(qt, _to_blocks(q_rot), kh, kt, vh, dot_, _to_blocks(do_bf), lse,
                              _attn_delta(ot, dot_), bias)
    dq = _rope("rope_dq", _from_blocks_t(dqt), 0, D, cos, sin, -1.0, HD ** -0.5)
    dk = _rope("rope_dk", _unheads(dkh), 0, KVW, cos, sin, -1.0, 1.0)
    dproj = jnp.concatenate([dq, dk, _unheads(dvh).astype(BF16), du, dz, ddt], axis=1)
    dn2 = _mm("in_proj_dx", [dproj, wint_pad], NN, (S // TS, 3),
              [pl.BlockSpec((TS, pw), lambda i, k: (i, k)), pl.BlockSpec((pw, D), lambda i, k: (k, 0))],
              pl.BlockSpec((TS, D), lambda i, k: (i, 0)), _sds((S, D), F32), (TS, D))
    dwint = _mm("in_proj_dw", [dproj, n2], TN, (3, S // TS),
                [pl.BlockSpec((TS, pw), lambda j, k: (k, j)), pl.BlockSpec((TS, D), lambda j, k: (k, 0))],
                pl.BlockSpec((pw, D), lambda j, k: (j, 0)), _sds((WIN_PAD, D), BF16), (pw, D))
    dx1, dh1, dg2, dp1 = _mid_bwd("mid_bwd1", dx2, dn2, x1, sp["mix_pre_norm"], h1, sp["ffn1_post_norm"], 0.5)

    dn1, dws1 = _ffn_bwd("ffn1", dh1, n1, gate1, up1, act1, ffnw, 0)
    grad_x, dg1 = _first_bwd(dx1, dn1, x, sp["ffn1_pre_norm"])

    small = jnp.concatenate([
        dg1[0], dp1[0], dg2[0], dssm[0], dp2[0], dg3[0], dp3[0], dcb[0],
        dsc[0, :16], dsc[1, :16], dsc[2, :16], dcw8[:CONV_K].reshape(-1), loss[0, :1]])
    small = jnp.pad(small, (0, SMALL_LEN - small.shape[0])).reshape(SMALL_ROWS, D)
    partials = dws1 + dws2 + [dwint[:WIN_COLS].reshape(NSH, WIN_SH, D), dwout.reshape(NSH, 2 * D // NSH, D)]
    return grad_x, partials, small


WEIGHTS = ("ffn1_pre_norm", "ffn1_w_gate", "ffn1_w_up", "ffn1_w_down", "ffn1_post_norm", "mix_pre_norm", "w_in",
           "conv_w", "conv_b", "dt_bias", "a_log", "d_skip", "ssm_norm", "w_out", "mix_post_norm", "ffn2_pre_norm",
           "ffn2_w_gate", "ffn2_w_up", "ffn2_w_down", "ffn2_post_norm")
BIG = ("ffn1_w_gate", "ffn1_w_up", "ffn1_w_down", "ffn2_w_gate", "ffn2_w_up", "ffn2_w_down", "w_in", "w_out")
TRANSPOSED = ("ffn1_w_gate", "ffn1_w_up", "ffn2_w_gate", "ffn2_w_up", "w_in")
SMALL_ORDER = SMALL_1K + ("conv_b",) + SMALL_16
CONVW_SH = CONV_C // NSH


def _shard2d(t, name):
    return t[0].T if name in TRANSPOSED else t[0]


def _unshard2d(t, name):
    return (t.T if name in TRANSPOSED else t)[None]


def _pack_small(d, prefix, shard_of_convw):
    flat = jnp.concatenate([d[prefix + n][0] for n in SMALL_ORDER] + [shard_of_convw.reshape(-1)])
    return jnp.pad(flat, (0, SMALL_LEN - flat.shape[0])).reshape(SMALL_ROWS, D)


def _unpack_small(block, like):
    flat = block.reshape(-1)
    out, off = {}, 0
    for n in SMALL_ORDER:
        size = like[n].shape[1]
        out[n] = flat[off:off + size].reshape(1, size)
        off += size
    out["conv_w"] = flat[off:off + CONV_K * CONVW_SH].reshape(1, CONV_K, CONVW_SH)
    return out


def kernel(x, positions, ffn1_pre_norm, ffn1_w_gate, ffn1_w_up, ffn1_w_down, ffn1_post_norm, mix_pre_norm, w_in, conv_w, conv_b, dt_bias, a_log, d_skip, ssm_norm, w_out, mix_post_norm, ffn2_pre_norm, ffn2_w_gate, ffn2_w_up, ffn2_w_down, ffn2_post_norm, loss_target, m_ffn1_pre_norm, m_ffn1_w_gate, m_ffn1_w_up, m_ffn1_w_down, m_ffn1_post_norm, m_mix_pre_norm, m_w_in, m_conv_w, m_conv_b, m_dt_bias, m_a_log, m_d_skip, m_ssm_norm, m_w_out, m_mix_post_norm, m_ffn2_pre_norm, m_ffn2_w_gate, m_ffn2_w_up, m_ffn2_w_down, m_ffn2_post_norm, v_ffn1_pre_norm, v_ffn1_w_gate, v_ffn1_w_up, v_ffn1_w_down, v_ffn1_post_norm, v_mix_pre_norm, v_w_in, v_conv_w, v_conv_b, v_dt_bias, v_a_log, v_d_skip, v_ssm_norm, v_w_out, v_mix_post_norm, v_ffn2_pre_norm, v_ffn2_w_gate, v_ffn2_w_up, v_ffn2_w_down, v_ffn2_post_norm):
    given = dict(locals())
    xi, yi, ci = lax.axis_index("x"), lax.axis_index("y"), lax.axis_index("c")

    shard = jnp.reshape(2 * xi + yi, (1,)).astype(jnp.int32)
    big = {p + n: _shard2d(given[p + n], n) for n in BIG for p in ("", "m_", "v_")}
    ffnsh = _cast_stack("cast_ffn", shard, [big[n] for n in BIG[:6]], 176, D)
    winsh = _cast_stack("cast_w_in", shard, [big["w_in"]], WIN_SH, 256).reshape(NSH, WIN_SH, D)
    woutsh = _cast_stack("cast_w_out", shard, [big["w_out"]], 256, D).reshape(NSH, 2 * D // NSH, D)
    ffnw, wint, woutf, cwf = _gather_weights(ffnsh, winsh, woutsh, conv_w[0])
    wint_pad = jnp.pad(wint.reshape(WIN_COLS, D), ((0, WIN_PAD - WIN_COLS), (0, 0)))
    convw = cwf.transpose(1, 0, 2).reshape(CONV_K, CONV_C)

    sp = {n: given[n] for n in SMALL_ORDER}
    grad_x, ps, small = _local_step(x[0], positions[0], loss_target[0], sp, ffnw, wint_pad,
                                    woutf.reshape(2 * D, D), convw)

    rs = _to_sibling(ps)
    c_idx = jnp.reshape(ci, (1,)).astype(jnp.int32)
    groups = ((0, 6, 352), (6, 7, WIN_SH), (7, 8, 256))
    csums = []
    for a, b, th in groups:
        csums += _pair_sum("pair_sum_%d" % a, c_idx, ps[a:b], rs[a:b], th)
    ts = _to_chips(csums)
    place = jnp.stack([2 * xi + yi, ci]).astype(jnp.int32)
    halves = []
    for a, b, th in groups:
        halves += _chip_sum("chip_sum_%d" % a, place, csums[a:b], ts[a:b], th)
    big_grads = dict(zip(BIG, _swap_halves(halves)))

    tot = _allreduce_small(small).reshape(-1)
    loss = tot[OFF_LOSS]
    small_grads, off = {}, 0
    for n in SMALL_ORDER:
        size = given[n].shape[1]
        small_grads[n] = tot[off:off + size].reshape(1, size)
        off += size
    dconvw = tot[OFF_CONVW:OFF_CONVW + CONV_K * CONV_C].reshape(CONV_K, NSH, CONVW_SH)
    dconvw = lax.dynamic_index_in_dim(dconvw, 2 * xi + yi, axis=1, keepdims=False)
    small_grads["conv_w"] = dconvw.reshape(1, CONV_K, CONVW_SH)

    upd = {}
    for names, tr, tc in ((BIG[0:3], 176, D), (BIG[3:6], 176, D), (BIG[6:7], WIN_SH, 256), (BIG[7:8], 256, D)):
        res = _adamw("adamw_" + names[0], [big[n] for n in names], [big_grads[n] for n in names],
                     [big["m_" + n] for n in names], [big["v_" + n] for n in names], tr, tc)
        for n, r in zip(names, res):
            upd[n] = tuple(_unshard2d(t, n) for t in r)
    (dl, m2, v2), = _adamw(
        "adamw_small", [_pack_small(given, "", conv_w[0])], [_pack_small(small_grads, "", dconvw)],
        [_pack_small(given, "m_", m_conv_w[0])], [_pack_small(given, "v_", v_conv_w[0])], SMALL_ROWS, D)
    dl, m2, v2 = (_unpack_small(t, given) for t in (dl, m2, v2))
    for n in SMALL_ORDER + ("conv_w",):
        upd[n] = (dl[n], m2[n], v2[n])

    grads = dict(small_grads)
    grads.update({n: _unshard2d(g, n) for n, g in big_grads.items()})
    return (loss, grad_x[None], *[grads[n] for n in WEIGHTS], *[upd[n][0] for n in WEIGHTS],
            *[upd[n][1] for n in WEIGHTS], *[upd[n][2] for n in WEIGHTS])
```

```python
import functools
import typing

import jax
import jax.numpy as jnp
from jax import lax
from jax.experimental import pallas as pl
from jax.experimental.pallas import tpu as pltpu

F32 = jnp.float32
BF16 = jnp.bfloat16

S = 2048
D = 1024
FF = 2816
NSH = 4
FS = FF // NSH
HALF = D // 2
HD = 64
NKV = 4
NQ_PER_KV = 4
KVW = NKV * HD
CONV_C = 1536
CONV_K = 4
SSM_W = 1024
NST = 128
NCH = S // 128
WIN_COLS = 4112
WIN_SH = WIN_COLS // NSH
WIN_PAD = 4224
COL_DT = 4096
EPS = 1e-6
NEG = -1e30

ADAM_LR = 0.001
ADAM_B1 = 0.9
ADAM_B2 = 0.999
ADAM_EPS = 1e-08
ADAM_WD = 0.01
ADAM_STEP = 10

VMEM_LIMIT = 56 * 1024 * 1024
TS = 512
TR = 256

NN = (((1,), (0,)), ((), ()))
NT = (((1,), (1,)), ((), ()))
TN = (((0,), (0,)), ((), ()))
MESH = pl.DeviceIdType.MESH


def _cparams(*sem):
    return pltpu.CompilerParams(dimension_semantics=sem, vmem_limit_bytes=VMEM_LIMIT)


def _dot(a, b, dims):
    return lax.dot_general(a.astype(BF16), b.astype(BF16), dims, preferred_element_type=F32)


def _dot_exact(a, b):
    return lax.dot_general(a, b, NN, precision=lax.Precision.HIGHEST, preferred_element_type=F32)


def _sigmoid(v):
    return 1.0 / (1.0 + jnp.exp(-v))


class _Rider(typing.NamedTuple):
    operands: list
    out_shapes: list
    aliases: dict
    sems: list
    start: typing.Callable
    finish: typing.Callable


def _call(body, name, grid, in_specs, out_specs, out_shape, operands, scratch=(), sem=(), rider=None):
    multi = isinstance(out_shape, (list, tuple))
    if rider is None:
        return pl.pallas_call(
            body, name=name, grid=grid, in_specs=in_specs, out_specs=out_specs, out_shape=out_shape,
            scratch_shapes=list(scratch), compiler_params=_cparams(*sem))(*operands)
    outs = list(out_shape) if multi else [out_shape]
    ospecs = list(out_specs) if multi else [out_specs]
    n_in, n_out, n_scr = len(operands), len(outs), len(scratch)
    ri, ro = len(rider.operands), len(rider.out_shapes)

    def wrapped(*refs):
        o0 = n_in + ri
        s0 = o0 + n_out + ro
        rin, rout, rsem = refs[n_in:o0], refs[o0 + n_out:s0], refs[s0 + n_scr:]
        ids = [pl.program_id(a) for a in range(len(grid))]
        first = functools.reduce(jnp.logical_and, [i == 0 for i in ids])
        last = functools.reduce(jnp.logical_and, [i == g - 1 for i, g in zip(ids, grid)])

        @pl.when(first)
        def _():
            rider.start(rin, rout, rsem)

        body(*refs[:n_in], *refs[o0:o0 + n_out], *refs[s0:s0 + n_scr])

        @pl.when(last)
        def _():
            rider.finish(rin, rout, rsem)

    hbm = pl.BlockSpec(memory_space=pl.ANY)
    res = pl.pallas_call(
        wrapped, name=name, grid=grid, in_specs=list(in_specs) + [hbm] * ri, out_specs=ospecs + [hbm] * ro,
        out_shape=outs + list(rider.out_shapes), scratch_shapes=list(scratch) + list(rider.sems),
        input_output_aliases={n_in + k: n_out + v for k, v in rider.aliases.items()},
        compiler_params=_cparams(*(("arbitrary",) * len(grid))))(*operands, *rider.operands)
    main = list(res[:n_out])
    return (main if multi else main[0]), list(res[n_out:])


def _mm(name, operands, dims, grid, in_specs, o_spec, out_shape, acc_shape, rider=None):
    npairs = len(operands) // 2
    nk = grid[-1]
    kaxis = len(grid) - 1

    def body(*refs):
        o_ref, acc = refs[2 * npairs], refs[2 * npairs + 1]
        k = pl.program_id(kaxis)

        @pl.when(k == 0)
        def _():
            acc[...] = jnp.zeros_like(acc)

        t = None
        for i in range(npairs):
            d = _dot(refs[2 * i][...], refs[2 * i + 1][...], dims)
            t = d if t is None else t + d
        acc[...] += t

        @pl.when(k == nk - 1)
        def _():
            o_ref[...] = acc[...].astype(o_ref.dtype)

    return _call(body, name, grid, in_specs, o_spec, out_shape, operands, [pltpu.VMEM(acc_shape, F32)],
                 ("parallel",) * kaxis + ("arbitrary",), rider)


def _ffn_up(name, n, ffnw, wi, rider=None):
    def body(n_ref, wg_ref, wu_ref, g_ref, u_ref, a_ref):
        nb = n_ref[...]
        g = _dot(nb, wg_ref[...], NT)
        u = _dot(nb, wu_ref[...], NT)
        g_ref[...] = g.astype(BF16)
        u_ref[...] = u.astype(BF16)
        a_ref[...] = (g * _sigmoid(g) * u).astype(BF16)

    out = jax.ShapeDtypeStruct((NSH, S, FS), BF16)
    ospec = pl.BlockSpec((None, TS, FS), lambda s, i: (s, i, 0))
    return _call(
        body, name, (NSH, S // TS),
        [pl.BlockSpec((TS, D), lambda s, i: (i, 0)),
         pl.BlockSpec((None, None, FS, D), lambda s, i: (s, wi, 0, 0)),
         pl.BlockSpec((None, None, FS, D), lambda s, i: (s, wi + 1, 0, 0))],
        [ospec, ospec, ospec], [out, out, out], (n, ffnw, ffnw), sem=("parallel", "parallel"), rider=rider)


def _ffn_dact(name, dh, ffnw, wi, gate, up, rider=None):
    def body(dh_ref, wd_ref, g_ref, u_ref, dg_ref, du_ref):
        da = _dot(dh_ref[...], wd_ref[...], NT)
        g = g_ref[...].astype(F32)
        u = u_ref[...].astype(F32)
        sg = _sigmoid(g)
        dg_ref[...] = (da * u * (sg * (1.0 + g * (1.0 - sg)))).astype(BF16)
        du_ref[...] = (da * (g * sg)).astype(BF16)

    out = jax.ShapeDtypeStruct((NSH, S, FS), BF16)
    aspec = pl.BlockSpec((None, TS, FS), lambda s, i: (s, i, 0))
    return _call(
        body, name, (NSH, S // TS),
        [pl.BlockSpec((TS, D), lambda s, i: (i, 0)),
         pl.BlockSpec((None, None, FS, D), lambda s, i: (s, wi + 2, 0, 0)), aspec, aspec],
        [aspec, aspec], [out, out], (dh, ffnw, gate, up), sem=("parallel", "parallel"), rider=rider)


def _rstd(v):
    return lax.rsqrt(jnp.mean(v * v, axis=-1, keepdims=True) + EPS)


def _row_spec():
    return pl.BlockSpec((TR, D), lambda i: (i, 0))


def _vec_spec():
    return pl.BlockSpec((1, D), lambda i: (0, 0))


def _acc_rows(ref, v):
    @pl.when(pl.program_id(0) == 0)
    def _():
        ref[...] = jnp.zeros_like(ref)
    ref[...] += jnp.sum(v, axis=0, keepdims=True)


def _prenorm(name, x, g):
    def body(x_ref, g_ref, n_ref):
        xv = x_ref[...]
        n_ref[...] = (xv * _rstd(xv) * g_ref[...]).astype(BF16)

    return pl.pallas_call(
        body, name=name, grid=(S // TR,), in_specs=[_row_spec(), _vec_spec()], out_specs=_row_spec(),
        out_shape=jax.ShapeDtypeStruct((S, D), BF16), compiler_params=_cparams("parallel"),
    )(x, g)


def _postres(name, x, h, p, alpha, gnext):
    def body(x_ref, h_ref, p_ref, g_ref, xo_ref, n_ref):
        hv = h_ref[...]
        xo = x_ref[...] + alpha * (hv * _rstd(hv) * p_ref[...])
        xo_ref[...] = xo
        n_ref[...] = (xo * _rstd(xo) * g_ref[...]).astype(BF16)

    return pl.pallas_call(
        body, name=name, grid=(S // TR,),
        in_specs=[_row_spec(), _row_spec(), _vec_spec(), _vec_spec()],
        out_specs=[_row_spec(), _row_spec()],
        out_shape=[jax.ShapeDtypeStruct((S, D), F32), jax.ShapeDtypeStruct((S, D), BF16)],
        compiler_params=_cparams("parallel"),
    )(x, h, p, gnext)


def _final(x, h, p, tgt, alpha):
    def body(x_ref, h_ref, p_ref, t_ref, dy_ref, dh_ref, dp_ref, loss_ref):
        hv = h_ref[...]
        r = _rstd(hv)
        hn = hv * r
        pv = p_ref[...]
        e = x_ref[...] + alpha * (hn * pv) - t_ref[...]
        dy = e * (1.0 / D)
        dy_ref[...] = dy
        du = alpha * dy * pv
        dh_ref[...] = (r * (du - hn * jnp.mean(du * hn, axis=-1, keepdims=True))).astype(BF16)
        _acc_rows(dp_ref, alpha * dy * hn)
        part = 0.5 * jnp.sum(jnp.mean(e * e, axis=-1, keepdims=True), axis=0, keepdims=True)
        _acc_rows(loss_ref, jnp.broadcast_to(part, (1, 128)))

    return pl.pallas_call(
        body, name="loss_head", grid=(S // TR,),
        in_specs=[_row_spec(), _row_spec(), _vec_spec(), _row_spec()],
        out_specs=[_row_spec(), _row_spec(), _vec_spec(), pl.BlockSpec((1, 128), lambda i: (0, 0))],
        out_shape=[jax.ShapeDtypeStruct((S, D), F32), jax.ShapeDtypeStruct((S, D), BF16),
                   jax.ShapeDtypeStruct((1, D), F32), jax.ShapeDtypeStruct((1, 128), F32)],
        compiler_params=_cparams("arbitrary"),
    )(x, h, p, tgt)


def _mid_bwd(name, dres, dn, x, g, h, p, alpha):
    def body(dr_ref, dn_ref, x_ref, g_ref, h_ref, p_ref, dx_ref, dh_ref, dg_ref, dp_ref):
        xv = x_ref[...]
        xn = xv * _rstd(xv)
        dnv = dn_ref[...]
        dng = dnv * g_ref[...]
        dx = dr_ref[...] + _rstd(xv) * (dng - xn * jnp.mean(dng * xn, axis=-1, keepdims=True))
        dx_ref[...] = dx
        _acc_rows(dg_ref, dnv * xn)
        hv = h_ref[...]
        r = _rstd(hv)
        hn = hv * r
        du = alpha * dx * p_ref[...]
        dh_ref[...] = (r * (du - hn * jnp.mean(du * hn, axis=-1, keepdims=True))).astype(BF16)
        _acc_rows(dp_ref, alpha * dx * hn)

    return pl.pallas_call(
        body, name=name, grid=(S // TR,),
        in_specs=[_row_spec(), _row_spec(), _row_spec(), _vec_spec(), _row_spec(), _vec_spec()],
        out_specs=[_row_spec(), _row_spec(), _vec_spec(), _vec_spec()],
        out_shape=[jax.ShapeDtypeStruct((S, D), F32), jax.ShapeDtypeStruct((S, D), BF16),
                   jax.ShapeDtypeStruct((1, D), F32), jax.ShapeDtypeStruct((1, D), F32)],
        compiler_params=_cparams("arbitrary"),
    )(dres, dn, x, g, h, p)


def _first_bwd(dres, dn, x, g):
    def body(dr_ref, dn_ref, x_ref, g_ref, dx_ref, dg_ref):
        xv = x_ref[...]
        r = _rstd(xv)
        xn = xv * r
        dnv = dn_ref[...]
        dng = dnv * g_ref[...]
        dx_ref[...] = dr_ref[...] + r * (dng - xn * jnp.mean(dng * xn, axis=-1, keepdims=True))
        _acc_rows(dg_ref, dnv * xn)

    return pl.pallas_call(
        body, name="first_bwd", grid=(S // TR,),
        in_specs=[_row_spec(), _row_spec(), _row_spec(), _vec_spec()],
        out_specs=[_row_spec(), _vec_spec()],
        out_shape=[jax.ShapeDtypeStruct((S, D), F32), jax.ShapeDtypeStruct((1, D), F32)],
        compiler_params=_cparams("arbitrary"),
    )(dres, dn, x, g)


def _rope(name, src, col_block, width, cos, sin, sign, scale):
    def body(t_ref, c_ref, s_ref, o_ref):
        t = t_ref[...].astype(F32)
        c = jnp.tile(c_ref[...], (1, width // 128))
        sn = jnp.tile(s_ref[...], (1, width // 128))
        lane = lax.broadcasted_iota(jnp.int32, t.shape, 1) & (HD - 1)
        rot = jnp.where(lane < HD // 2, -pltpu.roll(t, width - HD // 2, 1), pltpu.roll(t, HD // 2, 1))
        o_ref[...] = ((t * c + sign * (rot * sn)) * scale).astype(BF16)

    return pl.pallas_call(
        body, name=name, grid=(S // TR,),
        in_specs=[pl.BlockSpec((TR, width), lambda i: (i, col_block)),
                  pl.BlockSpec((TR, 128), lambda i: (i, 0)), pl.BlockSpec((TR, 128), lambda i: (i, 0))],
        out_specs=pl.BlockSpec((TR, width), lambda i: (i, 0)),
        out_shape=jax.ShapeDtypeStruct((S, width), BF16), compiler_params=_cparams("parallel"),
    )(src, cos, sin)


QROWS = NQ_PER_KV * 128


NBIAS = NCH + 1


def _bias_table():
    db = lax.broadcasted_iota(jnp.int32, (NBIAS, 128, QROWS), 0) - 1
    ki = lax.broadcasted_iota(jnp.int32, (NBIAS, 128, QROWS), 1)
    qi = lax.broadcasted_iota(jnp.int32, (NBIAS, 128, QROWS), 2) & 127
    d = db * 128 + qi - ki
    cnt = ((d <= 128).astype(F32) + (((d & 3) == 0) & (d <= 512)).astype(F32) + ((d & 15) == 0).astype(F32))
    return jnp.where((d >= 0) & (cnt > 0.0), jnp.log(jnp.maximum(cnt, 1.0)), NEG)


def _qt_spec():
    return pl.BlockSpec((None, None, HD, QROWS), lambda j, i: (j, i, 0, 0))


def _stat_spec():
    return pl.BlockSpec((None, None, 1, QROWS), lambda j, i: (j, i, 0, 0))


def _attn_fwd(qt, kh, vt, bias, rider=None):
    def body(q_ref, k_ref, v_ref, b_ref, o_ref, lse_ref):
        qb = pl.program_id(1)
        q = q_ref[...]

        def step(i, carry):
            m, l, acc = carry
            off = pl.multiple_of(i * 256, 256)
            s = _dot(k_ref[pl.ds(off, 256), :], q, NN)
            s = jnp.concatenate([s[:128] + b_ref[qb - 2 * i + 1], s[128:] + b_ref[qb - 2 * i]], axis=0)
            m_new = jnp.maximum(m, jnp.max(s, axis=0, keepdims=True))
            p = jnp.exp(s - m_new)
            a = jnp.exp(m - m_new)
            return (m_new, a * l + jnp.sum(p, axis=0, keepdims=True),
                    a * acc + _dot(v_ref[:, pl.ds(off, 256)], p, NN))

        m, l, acc = lax.fori_loop(
            0, qb // 2 + 1, step,
            (jnp.full((1, QROWS), NEG, F32), jnp.zeros((1, QROWS), F32), jnp.zeros((HD, QROWS), F32)))
        o_ref[...] = acc / l
        lse_ref[...] = m + jnp.log(l)

    return _call(
        body, "attn_fwd", (NKV, NCH),
        [_qt_spec(), pl.BlockSpec((None, S, HD), lambda j, i: (j, 0, 0)),
         pl.BlockSpec((None, HD, S), lambda j, i: (j, 0, 0)),
         pl.BlockSpec((NBIAS, 128, QROWS), lambda j, i: (0, 0, 0))],
        [_qt_spec(), _stat_spec()],
        [jax.ShapeDtypeStruct((NKV, NCH, HD, QROWS), F32), jax.ShapeDtypeStruct((NKV, NCH, 1, QROWS), F32)],
        (qt, kh, vt, bias), sem=("parallel", "parallel"), rider=rider)


def _attn_delta(ot, dot_):
    def body(o_ref, do_ref, dl_ref):
        dl_ref[...] = jnp.sum(o_ref[...] * do_ref[...].astype(F32), axis=0, keepdims=True)

    return pl.pallas_call(
        body, name="attn_delta", grid=(NKV, NCH), in_specs=[_qt_spec(), _qt_spec()], out_specs=_stat_spec(),
        out_shape=jax.ShapeDtypeStruct((NKV, NCH, 1, QROWS), F32),
        compiler_params=_cparams("parallel", "parallel"),
    )(ot, dot_)


def _attn_bwd(qt, q2, kh, kt, vh, dot_, do2, lse, delta, bias):
    def body(qt_ref, q2_ref, k_ref, kt_ref, v_ref, dot_ref, do2_ref, lse_ref, dl_ref, b_ref, dq_ref, dk_ref, dv_ref):
        kb = pl.program_id(1)

        @pl.when(kb == 0)
        def _():
            dq_ref[...] = jnp.zeros_like(dq_ref)

        k = k_ref[...]
        kt_ = kt_ref[...]
        v = v_ref[...]

        def step(j, carry):
            dk, dv = carry
            for qb in (2 * j, 2 * j + 1):
                st = _dot(k, qt_ref[qb], NN) + b_ref[qb - kb + 1]
                pt = jnp.exp(st - lse_ref[qb])
                dst = pt * (_dot(v, dot_ref[qb], NN) - dl_ref[qb])
                dq_ref[qb] += _dot(kt_, dst, NN)
                dk = dk + _dot(dst, q2_ref[qb], NN)
                dv = dv + _dot(pt, do2_ref[qb], NN)
            return dk, dv

        dk, dv = lax.fori_loop(kb // 2, NCH // 2, step, (jnp.zeros((128, HD), F32), jnp.zeros((128, HD), F32)))
        dk_ref[...] = dk
        dv_ref[...] = dv

    tspec = pl.BlockSpec((None, NCH, HD, QROWS), lambda j, i: (j, 0, 0, 0))
    rspec = pl.BlockSpec((None, NCH, QROWS, HD), lambda j, i: (j, 0, 0, 0))
    kspec = pl.BlockSpec((None, 128, HD), lambda j, i: (j, i, 0))
    sspec = pl.BlockSpec((None, NCH, 1, QROWS), lambda j, i: (j, 0, 0, 0))
    return pl.pallas_call(
        body, name="attn_bwd", grid=(NKV, NCH),
        in_specs=[tspec, rspec, kspec, pl.BlockSpec((None, HD, 128), lambda j, i: (j, 0, i)), kspec, tspec, rspec,
                  sspec, sspec, pl.BlockSpec((NBIAS, 128, QROWS), lambda j, i: (0, 0, 0))],
        out_specs=[tspec, kspec, kspec],
        out_shape=[jax.ShapeDtypeStruct((NKV, NCH, HD, QROWS), F32),
                   jax.ShapeDtypeStruct((NKV, S, HD), F32), jax.ShapeDtypeStruct((NKV, S, HD), F32)],
        compiler_params=_cparams("parallel", "arbitrary"),
    )(qt, q2, kh, kt, vh, dot_, do2, lse, delta, bias)


CONV_BLK = 256
CONV_COL0 = 1536 // CONV_BLK


def _shift_down(u, j, row):
    return jnp.where(row >= j, pltpu.roll(u, j, 0), 0.0)


def _conv_pre(u, w_ref, b_ref, row):
    y = b_ref[...] + w_ref[CONV_K - 1:CONV_K, :] * u
    for j in range(1, CONV_K):
        y = y + w_ref[CONV_K - 1 - j:CONV_K - j, :] * _shift_down(u, j, row)
    return y


def _conv_fwd(proj, convw, convb):
    def body(u_ref, w_ref, b_ref, o_ref):
        u = u_ref[...]
        row = lax.broadcasted_iota(jnp.int32, u.shape, 0)
        y = _conv_pre(u, w_ref, b_ref, row)
        o_ref[...] = y * _sigmoid(y)

    return pl.pallas_call(
        body, name="conv_fwd", grid=(CONV_C // CONV_BLK,),
        in_specs=[pl.BlockSpec((S, CONV_BLK), lambda i: (0, CONV_COL0 + i)),
                  pl.BlockSpec((CONV_K, CONV_BLK), lambda i: (0, i)),
                  pl.BlockSpec((1, CONV_BLK), lambda i: (0, i))],
        out_specs=pl.BlockSpec((S, CONV_BLK), lambda i: (0, i)),
        out_shape=jax.ShapeDtypeStruct((S, CONV_C), F32), compiler_params=_cparams("parallel"),
    )(proj, convw, convb)


def _conv_bwd(dact, proj, convw, convb):
    def body(da_ref, u_ref, w_ref, b_ref, du_ref, dw_ref, db_ref):
        u = u_ref[...]
        row = lax.broadcasted_iota(jnp.int32, u.shape, 0)
        y = _conv_pre(u, w_ref, b_ref, row)
        sg = _sigmoid(y)
        dy = da_ref[...] * (sg * (1.0 + y * (1.0 - sg)))
        db_ref[...] = jnp.sum(dy, axis=0, keepdims=True)
        du = w_ref[CONV_K - 1:CONV_K, :] * dy
        r8 = lax.broadcasted_iota(jnp.int32, (8, CONV_BLK), 0)
        dw = jnp.where(r8 == CONV_K - 1, jnp.sum(dy * u, axis=0, keepdims=True), 0.0)
        for j in range(1, CONV_K):
            du = du + w_ref[CONV_K - 1 - j:CONV_K - j, :] * jnp.where(row < S - j, pltpu.roll(dy, S - j, 0), 0.0)
            dw = dw + jnp.where(r8 == CONV_K - 1 - j,
                                jnp.sum(dy * _shift_down(u, j, row), axis=0, keepdims=True), 0.0)
        du_ref[...] = du.astype(BF16)
        dw_ref[...] = dw

    return pl.pallas_call(
        body, name="conv_bwd", grid=(CONV_C // CONV_BLK,),
        in_specs=[pl.BlockSpec((S, CONV_BLK), lambda i: (0, i)),
                  pl.BlockSpec((S, CONV_BLK), lambda i: (0, CONV_COL0 + i)),
                  pl.BlockSpec((CONV_K, CONV_BLK), lambda i: (0, i)),
                  pl.BlockSpec((1, CONV_BLK), lambda i: (0, i))],
        out_specs=[pl.BlockSpec((S, CONV_BLK), lambda i: (0, i)), pl.BlockSpec((8, CONV_BLK), lambda i: (0, i)),
                   pl.BlockSpec((1, CONV_BLK), lambda i: (0, i))],
        out_shape=[jax.ShapeDtypeStruct((S, CONV_C), BF16), jax.ShapeDtypeStruct((8, CONV_C), F32),
                   jax.ShapeDtypeStruct((1, CONV_C), F32)],
        compiler_params=_cparams("parallel"),
    )(dact, proj, convw, convb)


NPAIR = 8


def _ssd_scalars(dtr_ref, dtb_ref, alog_ref):
    z = dtr_ref[...] + dtb_ref[...]
    dt = jnp.maximum(z, 0.0) + jnp.log(1.0 + jnp.exp(-jnp.abs(z)))
    a = -jnp.exp(alog_ref[...])
    r = lax.broadcasted_iota(jnp.int32, (128, 128), 0)
    c = lax.broadcasted_iota(jnp.int32, (128, 128), 1)
    tri = (r >= c).astype(F32)
    cs = _dot_exact(tri, dt * a)
    return z, dt, a, cs, r, c


def _pair_terms(cs, cst, dt, h1, h2, lo):
    c1, c2 = cs[:, h1:h1 + 1], cs[:, h2:h2 + 1]
    l1, l2 = cs[127:128, h1:h1 + 1], cs[127:128, h2:h2 + 1]
    e_l = jnp.where(lo, jnp.exp(c1), jnp.exp(c2))
    dte1, dte2 = jnp.exp(l1 - c1), jnp.exp(l2 - c2)
    dte_l = jnp.where(lo, dte1, dte2)
    dt_l = jnp.where(lo, dt[:, h1:h1 + 1], dt[:, h2:h2 + 1])
    return c1, c2, jnp.exp(l1), jnp.exp(l2), e_l, dte1, dte2, dte_l, dt_l


def _gate_norm(y, zv, w):
    yg = y * (zv * _sigmoid(zv))
    outs, rs = [], []
    for g in range(2):
        blk = yg[:, 512 * g:512 * (g + 1)]
        r = lax.rsqrt(jnp.mean(blk * blk, axis=-1, keepdims=True) + EPS)
        outs.append(blk * r)
        rs.append(r)
    return jnp.concatenate(outs, axis=1), rs, yg


def _ssd_fwd(xbc, proj, dtb, alog, dskip_l, ssmw):
    def body(x_ref, b_ref, c_ref, dtr_ref, z_ref, dtb_ref, alog_ref, dsk_ref, w_ref, y_ref, yn_ref, hp_ref, h_ref):
        @pl.when(pl.program_id(0) == 0)
        def _():
            h_ref[...] = jnp.zeros_like(h_ref)

        _, dt, _, cs, r, c = _ssd_scalars(dtr_ref, dtb_ref, alog_ref)
        cst = cs.T
        causal = r >= c
        lo = c < HD
        hp_ref[...] = h_ref[...]
        for g in range(2):
            bg = b_ref[:, 128 * g:128 * (g + 1)]
            cg = c_ref[:, 128 * g:128 * (g + 1)]
            cb = _dot(cg, bg, NT)
            for j in range(4):
                pj = 4 * g + j
                h1, h2 = 2 * pj, 2 * pj + 1
                sl = slice(128 * pj, 128 * (pj + 1))
                xp = x_ref[:, sl]
                c1, c2, cd1, cd2, e_l, _, _, dte_l, dt_l = _pair_terms(cs, cst, dt, h1, h2, lo)
                xdt = xp * dt_l
                m1 = cb * jnp.exp(jnp.where(causal, c1 - cst[h1:h1 + 1, :], NEG))
                m2 = cb * jnp.exp(jnp.where(causal, c2 - cst[h2:h2 + 1, :], NEG))
                yd = jnp.where(lo, _dot(m1, xdt, NN), _dot(m2, xdt, NN))
                hp = h_ref[pj]
                yo = _dot(cg, hp, NT) * e_l
                st = _dot(xdt * dte_l, bg, TN)
                h_ref[pj] = hp * jnp.where(r < HD, cd1, cd2) + st
                y_ref[:, sl] = yd + yo + dsk_ref[:, sl] * xp
        yn, _, _ = _gate_norm(y_ref[...], z_ref[...], w_ref[...])
        yn_ref[...] = (yn * w_ref[...]).astype(BF16)

    return pl.pallas_call(
        body, name="ssd_fwd", grid=(NCH,),
        in_specs=[pl.BlockSpec((128, SSM_W), lambda i: (i, 0)),
                  pl.BlockSpec((128, 256), lambda i: (i, 4)), pl.BlockSpec((128, 256), lambda i: (i, 5)),
                  pl.BlockSpec((128, 128), lambda i: (i, COL_DT // 128)),
                  pl.BlockSpec((128, SSM_W), lambda i: (i, 3)),
                  pl.BlockSpec((1, 128), lambda i: (0, 0)), pl.BlockSpec((1, 128), lambda i: (0, 0)),
                  pl.BlockSpec((1, SSM_W), lambda i: (0, 0)), pl.BlockSpec((1, SSM_W), lambda i: (0, 0))],
        out_specs=[pl.BlockSpec((128, SSM_W), lambda i: (i, 0)), pl.BlockSpec((128, SSM_W), lambda i: (i, 0)),
                   pl.BlockSpec((None, NPAIR, 128, 128), lambda i: (i, 0, 0, 0))],
        out_shape=[jax.ShapeDtypeStruct((S, SSM_W), F32), jax.ShapeDtypeStruct((S, SSM_W), BF16),
                   jax.ShapeDtypeStruct((NCH, NPAIR, 128, 128), F32)],
        scratch_shapes=[pltpu.VMEM((NPAIR, 128, 128), F32)],
        compiler_params=_cparams("arbitrary"),
    )(xbc, xbc, xbc, proj, proj, dtb, alog, dskip_l, ssmw)


def _ssd_bwd(dmixed, y, xbc, proj, hprev, dtb, alog, dskip_l, ssmw, rider=None):
    def body(dyn_ref, y_ref, x_ref, b_ref, c_ref, dtr_ref, z_ref, hp_ref, dtb_ref, alog_ref, dsk_ref, w_ref,
             dxbc_ref, dz_ref, ddt_ref, dw_ref, dsc_ref, g_ref):
        @pl.when(pl.program_id(0) == 0)
        def _():
            g_ref[...] = jnp.zeros_like(g_ref)
            dsc_ref[...] = jnp.zeros_like(dsc_ref)

        z, dt, a, cs, r, c = _ssd_scalars(dtr_ref, dtb_ref, alog_ref)
        cst = cs.T
        causal = r >= c
        lo = c < HD

        yv = y_ref[...]
        zv = z_ref[...]
        wv = w_ref[...]
        ygn, rs, yg = _gate_norm(yv, zv, wv)
        dyn = dyn_ref[...]
        _acc_rows(dw_ref, dyn * ygn)
        dynw = dyn * wv
        parts = []
        for g in range(2):
            sl = slice(512 * g, 512 * (g + 1))
            a_g, n_g = dynw[:, sl], ygn[:, sl]
            parts.append(rs[g] * (a_g - n_g * jnp.mean(a_g * n_g, axis=-1, keepdims=True)))
        dyg = jnp.concatenate(parts, axis=1)
        sz = _sigmoid(zv)
        dz_ref[...] = (dyg * yv * (sz * (1.0 + zv * (1.0 - sz)))).astype(BF16)
        dy_all = dyg * (zv * sz)

        dcs_cols = jnp.zeros((128, 128), F32)
        dcs_rows = jnp.zeros((128, 128), F32)
        ddt_x = jnp.zeros((128, 128), F32)
        dd_row = jnp.zeros((1, 128), F32)
        last = r == 127
        for g in range(2):
            bg = b_ref[:, 128 * g:128 * (g + 1)]
            cg = c_ref[:, 128 * g:128 * (g + 1)]
            cb = _dot(cg, bg, NT)
            dcb = jnp.zeros((128, 128), F32)
            db_acc = jnp.zeros((128, NST), F32)
            dc_acc = jnp.zeros((128, NST), F32)
            for j in range(4):
                pj = 4 * g + j
                h1, h2 = 2 * pj, 2 * pj + 1
                sl = slice(128 * pj, 128 * (pj + 1))
                xp = x_ref[:, sl]
                dyp = dy_all[:, sl]
                c1, c2, cd1, cd2, e_l, dte1, dte2, dte_l, dt_l = _pair_terms(cs, cst, dt, h1, h2, lo)
                xdt = xp * dt_l
                hp = hp_ref[pj]
                gp = g_ref[pj]
                dxp = dsk_ref[:, sl] * dyp
                dyx = dyp * xp
                dd_row = dd_row + jnp.where(c[0:1, :] == h1, jnp.sum(jnp.where(lo, dyx, 0.0), keepdims=True), 0.0) \
                    + jnp.where(c[0:1, :] == h2, jnp.sum(jnp.where(lo, 0.0, dyx), keepdims=True), 0.0)
                dzs = dyp * e_l
                dc_acc = dc_acc + _dot(dzs, hp, NN)
                g_from = _dot(dzs, cg, TN)
                ryo = dyp * (_dot(cg, hp, NT) * e_l)
                k1 = jnp.sum(jnp.where(lo, ryo, 0.0), axis=1, keepdims=True)
                k2 = jnp.sum(jnp.where(lo, 0.0, ryo), axis=1, keepdims=True)
                qm = _dot(bg, gp, NT)
                dxdt = qm * dte_l
                qx = qm * xdt
                t1 = jnp.sum(jnp.where(lo, qx, 0.0), axis=1, keepdims=True) * dte1
                t2 = jnp.sum(jnp.where(lo, 0.0, qx), axis=1, keepdims=True) * dte2
                db_acc = db_acc + _dot(xdt * dte_l, gp, NN)
                gh = gp * hp
                dl1 = jnp.sum(t1, keepdims=True) + jnp.sum(jnp.where(r < HD, gh, 0.0), keepdims=True) * cd1
                dl2 = jnp.sum(t2, keepdims=True) + jnp.sum(jnp.where(r < HD, 0.0, gh), keepdims=True) * cd2
                g_ref[pj] = g_from + jnp.where(r < HD, cd1, cd2) * gp
                k1 = k1 - t1 + jnp.where(last[:, 0:1], dl1, 0.0)
                k2 = k2 - t2 + jnp.where(last[:, 0:1], dl2, 0.0)
                for hh, ch, msk in ((h1, c1, lo), (h2, c2, jnp.logical_not(lo))):
                    lm = jnp.exp(jnp.where(causal, ch - cst[hh:hh + 1, :], NEG))
                    mm = cb * lm
                    dm = jnp.where(causal, _dot(jnp.where(msk, dyp, 0.0), xdt, NT), 0.0)
                    w = dm * mm
                    kk = jnp.sum(w, axis=1, keepdims=True)
                    if hh == h1:
                        k1 = k1 + kk
                    else:
                        k2 = k2 + kk
                    dcs_rows = dcs_rows + jnp.where(r == hh, jnp.sum(w, axis=0, keepdims=True), 0.0)
                    dcb = dcb + dm * lm
                    dxdt = dxdt + jnp.where(msk, _dot(mm, dyp, TN), 0.0)
                dcs_cols = dcs_cols + jnp.where(c == h1, k1, 0.0) + jnp.where(c == h2, k2, 0.0)
                dxx = dxdt * xp
                ddt_x = ddt_x + jnp.where(c == h1, jnp.sum(jnp.where(lo, dxx, 0.0), axis=1, keepdims=True), 0.0) \
                    + jnp.where(c == h2, jnp.sum(jnp.where(lo, 0.0, dxx), axis=1, keepdims=True), 0.0)
                dxbc_ref[:, sl] = dxp + dxdt * dt_l
            dxbc_ref[:, SSM_W + 128 * g:SSM_W + 128 * (g + 1)] = db_acc + _dot(dcb, cg, TN)
            dxbc_ref[:, SSM_W + 256 + 128 * g:SSM_W + 256 + 128 * (g + 1)] = dc_acc + _dot(dcb, bg, NN)

        dcs = dcs_cols - dcs_rows.T
        dad = _dot_exact((c >= r).astype(F32), dcs)
        ddt = dad * a + ddt_x
        ddtr = jnp.where(c < 16, ddt * _sigmoid(z), 0.0)
        ddt_ref[...] = ddtr.astype(BF16)
        r8 = lax.broadcasted_iota(jnp.int32, (8, 128), 0)
        dsc_ref[...] += (jnp.where(r8 == 0, jnp.sum(ddtr, axis=0, keepdims=True), 0.0)
                         + jnp.where(r8 == 1, jnp.sum(dad * dt, axis=0, keepdims=True) * a, 0.0)
                         + jnp.where(r8 == 2, dd_row, 0.0))

    rev = NCH - 1
    return _call(
        body, "ssd_bwd", (NCH,),
        [pl.BlockSpec((128, SSM_W), lambda i: (rev - i, 1)),
         pl.BlockSpec((128, SSM_W), lambda i: (rev - i, 0)),
         pl.BlockSpec((128, SSM_W), lambda i: (rev - i, 0)),
         pl.BlockSpec((128, 256), lambda i: (rev - i, 4)), pl.BlockSpec((128, 256), lambda i: (rev - i, 5)),
         pl.BlockSpec((128, 128), lambda i: (rev - i, COL_DT // 128)),
         pl.BlockSpec((128, SSM_W), lambda i: (rev - i, 3)),
         pl.BlockSpec((None, NPAIR, 128, 128), lambda i: (rev - i, 0, 0, 0)),
         pl.BlockSpec((1, 128), lambda i: (0, 0)), pl.BlockSpec((1, 128), lambda i: (0, 0)),
         pl.BlockSpec((1, SSM_W), lambda i: (0, 0)), pl.BlockSpec((1, SSM_W), lambda i: (0, 0))],
        [pl.BlockSpec((128, CONV_C), lambda i: (rev - i, 0)),
         pl.BlockSpec((128, SSM_W), lambda i: (rev - i, 0)),
         pl.BlockSpec((128, 128), lambda i: (rev - i, 0)),
         pl.BlockSpec((1, SSM_W), lambda i: (0, 0)), pl.BlockSpec((8, 128), lambda i: (0, 0))],
        [jax.ShapeDtypeStruct((S, CONV_C), F32), jax.ShapeDtypeStruct((S, SSM_W), BF16),
         jax.ShapeDtypeStruct((S, 128), BF16), jax.ShapeDtypeStruct((1, SSM_W), F32),
         jax.ShapeDtypeStruct((8, 128), F32)],
        (dmixed, y, xbc, xbc, xbc, proj, proj, hprev, dtb, alog, dskip_l, ssmw),
        [pltpu.VMEM((NPAIR, 128, 128), F32)], ("arbitrary",), rider)


def _cast_stack(name, slot, arrs, tr, tc):
    n = len(arrs)
    rows, cols = arrs[0].shape

    def body(s_ref, *refs):
        for i in range(n):
            refs[n][i] = refs[i][...].astype(BF16)

    return pl.pallas_call(
        body, name=name,
        grid_spec=pltpu.PrefetchScalarGridSpec(
            num_scalar_prefetch=1, grid=(rows // tr, cols // tc),
            in_specs=[pl.BlockSpec((tr, tc), lambda i, j, sr: (i, j))] * n,
            out_specs=pl.BlockSpec((None, n, tr, tc), lambda i, j, sr: (sr[0], 0, i, j))),
        out_shape=jax.ShapeDtypeStruct((NSH, n, rows, cols), BF16),
        compiler_params=_cparams("parallel", "parallel"),
    )(slot, *arrs)


def _pair_sum(name, c_idx, ps, rs, th):
    n = len(ps)
    _, rows, _ = ps[0].shape

    def body(c_ref, *refs):
        for i in range(n):
            refs[2 * n + i][...] = (refs[i][...].astype(F32) + refs[n + i][...].astype(F32)).astype(BF16)

    spec = pl.BlockSpec((None, th, HALF), lambda s, i, cr: (s, i, 0))
    return pl.pallas_call(
        body, name=name,
        grid_spec=pltpu.PrefetchScalarGridSpec(
            num_scalar_prefetch=1, grid=(NSH, rows // th),
            in_specs=[pl.BlockSpec((None, th, HALF), lambda s, i, cr: (s, i, cr[0]))] * n + [spec] * n,
            out_specs=[spec] * n),
        out_shape=[jax.ShapeDtypeStruct((NSH, rows, HALF), BF16)] * n,
        compiler_params=_cparams("parallel", "parallel"),
    )(c_idx, *ps, *rs)


def _chip_sum(name, place, cs, ts, th):
    n = len(ts)
    _, rows, _ = ts[0].shape

    def body(p_ref, *refs):
        for i in range(n):
            t = refs[n + i][...].astype(F32)
            refs[2 * n + i][...] = ((refs[i][...].astype(F32) + t[0]) + t[1]) + t[2]

    return pl.pallas_call(
        body, name=name,
        grid_spec=pltpu.PrefetchScalarGridSpec(
            num_scalar_prefetch=1, grid=(rows // th,),
            in_specs=[pl.BlockSpec((None, th, HALF), lambda i, pr: (pr[0], i, 0))] * n
            + [pl.BlockSpec((3, th, HALF), lambda i, pr: (0, i, 0))] * n,
            out_specs=[pl.BlockSpec((th, HALF), lambda i, pr: (i, pr[1]))] * n),
        out_shape=[jax.ShapeDtypeStruct((rows, D), F32)] * n, compiler_params=_cparams("parallel"),
    )(place, *cs, *ts)


def _adamw(name, ws, gs, ms, vs, tr, tc):
    n = len(ws)
    rows, cols = ws[0].shape
    c1 = 1.0 / (1.0 - ADAM_B1 ** ADAM_STEP)
    c2 = 1.0 / (1.0 - ADAM_B2 ** ADAM_STEP)

    def body(*refs):
        for i in range(n):
            w, g, m, v = (refs[k * n + i][...] for k in range(4))
            m2 = ADAM_B1 * m + (1.0 - ADAM_B1) * g
            v2 = ADAM_B2 * v + (1.0 - ADAM_B2) * (g * g)
            refs[4 * n + 3 * i][...] = -ADAM_LR * ((m2 * c1) / (jnp.sqrt(v2 * c2) + ADAM_EPS) + ADAM_WD * w)
            refs[4 * n + 3 * i + 1][...] = m2
            refs[4 * n + 3 * i + 2][...] = v2

    spec = pl.BlockSpec((tr, tc), lambda i, j: (i, j))
    outs = pl.pallas_call(
        body, name=name, grid=(rows // tr, cols // tc), in_specs=[spec] * (4 * n), out_specs=[spec] * (3 * n),
        out_shape=[jax.ShapeDtypeStruct((rows, cols), F32)] * (3 * n),
        compiler_params=_cparams("parallel", "parallel"),
    )(*ws, *gs, *ms, *vs)
    return [tuple(outs[3 * i:3 * i + 3]) for i in range(n)]


def _place():
    x, y, c = lax.axis_index("x"), lax.axis_index("y"), lax.axis_index("c")
    chips = [(1 - x, y), (x, 1 - y), (1 - x, 1 - y)]
    return x, y, c, chips


def _any_specs(n):
    return [pl.BlockSpec(memory_space=pl.ANY)] * n


def _rcopy(src, dst, send_sem, recv_sem, dev):
    return pltpu.make_async_remote_copy(src_ref=src, dst_ref=dst, send_sem=send_sem, recv_sem=recv_sem,
                                        device_id=dev, device_id_type=MESH)


def _gather_rider(bufs, views):
    n = len(bufs)

    def start(rin, rout, sems):
        send, recv = sems[0], sems[1]
        x, y, c, chips = _place()
        for j, chip in enumerate(chips):
            for b in range(n):
                mine = views[b](rout[b], 2 * x + y, c)
                _rcopy(mine, mine, send.at[j * n + b], recv.at[j * n + b], (chip[0], chip[1], c)).start()

    def finish(rin, rout, sems):
        send, recv, fsend, frecv = sems
        x, y, c, chips = _place()
        passed = []
        for j, chip in enumerate(chips):
            for b in range(n):
                landed = views[b](rout[b], 2 * chip[0] + chip[1], c)
                _rcopy(landed, landed, send.at[j * n + b], recv.at[j * n + b], (x, y, c)).wait_recv()
                fw = _rcopy(landed, landed, fsend.at[j * n + b], frecv.at[j * n + b], (x, y, 1 - c))
                fw.start()
                passed.append(fw)
        for j, chip in enumerate(chips):
            for b in range(n):
                other = views[b](rout[b], 2 * chip[0] + chip[1], 1 - c)
                _rcopy(other, other, fsend.at[j * n + b], frecv.at[j * n + b], (x, y, c)).wait_recv()
        for j, chip in enumerate(chips):
            for b in range(n):
                mine = views[b](rout[b], 2 * x + y, c)
                _rcopy(mine, mine, send.at[j * n + b], recv.at[j * n + b], (x, y, c)).wait_send()
        for fw in passed:
            fw.wait_send()

    return _Rider(list(bufs), [jax.ShapeDtypeStruct(a.shape, a.dtype) for a in bufs], {b: b for b in range(n)},
                  [pltpu.SemaphoreType.DMA((3 * n,))] * 4, start, finish)


def _small_gather_rider(cw):
    def descs(rin, rout, sems, x, y, c, chips):
        return [_rcopy(rin[0], rout[0].at[2 * x + y], sems[1].at[j], sems[2].at[j], (chip[0], chip[1], c))
                for j, chip in enumerate(chips)]

    def start(rin, rout, sems):
        x, y, c, chips = _place()
        pltpu.make_async_copy(rin[0], rout[0].at[2 * x + y], sems[0].at[0]).start()
        for cp in descs(rin, rout, sems, x, y, c, chips):
            cp.start()

    def finish(rin, rout, sems):
        x, y, c, chips = _place()
        for j, chip in enumerate(chips):
            _rcopy(rin[0], rout[0].at[2 * chip[0] + chip[1]], sems[1].at[j], sems[2].at[j], (x, y, c)).wait_recv()
        for cp in descs(rin, rout, sems, x, y, c, chips):
            cp.wait_send()
        pltpu.make_async_copy(rin[0], rout[0].at[2 * x + y], sems[0].at[0]).wait()

    return _Rider([cw], [jax.ShapeDtypeStruct((NSH,) + cw.shape, cw.dtype)], {},
                  [pltpu.SemaphoreType.DMA((1,)), pltpu.SemaphoreType.DMA((3,)), pltpu.SemaphoreType.DMA((3,))],
                  start, finish)


def _to_chips_rider(cs):
    n = len(cs)

    def descs(rin, rout, sems):
        x, y, c, chips = _place()
        return [_rcopy(rin[i].at[2 * chip[0] + chip[1]], rout[i].at[j], sems[0].at[j * n + i], sems[1].at[j * n + i],
                       (chip[0], chip[1], c)) for j, chip in enumerate(chips) for i in range(n)]

    def start(rin, rout, sems):
        for cp in descs(rin, rout, sems):
            cp.start()

    def finish(rin, rout, sems):
        for cp in descs(rin, rout, sems):
            cp.wait()

    return _Rider(list(cs), [jax.ShapeDtypeStruct((3,) + a.shape[1:], a.dtype) for a in cs], {},
                  [pltpu.SemaphoreType.DMA((3 * n,))] * 2, start, finish)


def _run_riders(name, riders):
    n_in = [len(r.operands) for r in riders]
    n_out = [len(r.out_shapes) for r in riders]
    n_sem = [len(r.sems) for r in riders]

    def body(*refs):
        parts, at = [], 0
        for counts in (n_in, n_out, n_sem):
            group = []
            for k in counts:
                group.append(refs[at:at + k])
                at += k
            parts.append(group)
        for i, r in enumerate(riders):
            r.start(parts[0][i], parts[1][i], parts[2][i])
        for i, r in enumerate(riders):
            r.finish(parts[0][i], parts[1][i], parts[2][i])

    aliases = {}
    for i, r in enumerate(riders):
        for k, v in r.aliases.items():
            aliases[sum(n_in[:i]) + k] = sum(n_out[:i]) + v
    res = pl.pallas_call(
        body, name=name, in_specs=_any_specs(sum(n_in)), out_specs=_any_specs(sum(n_out)),
        out_shape=[s for r in riders for s in r.out_shapes], input_output_aliases=aliases,
        scratch_shapes=[s for r in riders for s in r.sems],
    )(*[a for r in riders for a in r.operands])
    out, at = [], 0
    for k in n_out:
        out.append(list(res[at:at + k]))
        at += k
    return out


def _to_sibling(name, ps):
    n = len(ps)

    def body(*refs):
        src, dst, send, recv = refs[:n], refs[n:2 * n], refs[2 * n], refs[2 * n + 1]
        x, y, c, _ = _place()
        cps = [pltpu.make_async_remote_copy(
            src_ref=src[i].at[:, :, pl.ds((1 - c) * HALF, HALF)], dst_ref=dst[i], send_sem=send.at[i],
            recv_sem=recv.at[i], device_id=(x, y, 1 - c), device_id_type=MESH) for i in range(n)]
        for cp in cps:
            cp.start()
        for cp in cps:
            cp.wait()

    outs = [jax.ShapeDtypeStruct(p.shape[:2] + (HALF,), p.dtype) for p in ps]
    return pl.pallas_call(
        body, name=name, in_specs=_any_specs(n), out_specs=_any_specs(n), out_shape=outs,
        scratch_shapes=[pltpu.SemaphoreType.DMA((n,)), pltpu.SemaphoreType.DMA((n,))],
    )(*ps)


def _swap_halves(gs):
    n = len(gs)

    def body(*refs):
        dst, send, recv = refs[n:2 * n], refs[2 * n], refs[2 * n + 1]
        x, y, c, _ = _place()
        cps = []
        for i in range(n):
            mine = dst[i].at[:, pl.ds(c * HALF, HALF)]
            cps.append(pltpu.make_async_remote_copy(
                src_ref=mine, dst_ref=mine, send_sem=send.at[i], recv_sem=recv.at[i],
                device_id=(x, y, 1 - c), device_id_type=MESH))
        for cp in cps:
            cp.start()
        for i in range(n):
            other = dst[i].at[:, pl.ds((1 - c) * HALF, HALF)]
            pltpu.make_async_remote_copy(
                src_ref=other, dst_ref=other, send_sem=send.at[i], recv_sem=recv.at[i],
                device_id=(x, y, c), device_id_type=MESH).wait_recv()
        for cp in cps:
            cp.wait_send()

    return pl.pallas_call(
        body, name="grads_swap_halves", in_specs=_any_specs(n), out_specs=_any_specs(n),
        out_shape=[jax.ShapeDtypeStruct(g.shape, g.dtype) for g in gs],
        input_output_aliases={i: i for i in range(n)},
        scratch_shapes=[pltpu.SemaphoreType.DMA((n,)), pltpu.SemaphoreType.DMA((n,))],
    )(*gs)


SMALL_ROWS = 16


def _allreduce_small(vec):
    def body(v_ref, o_ref, buf, send, recv):
        x, y, c, _ = _place()
        me = 4 * x + 2 * y + c
        buf[me] = v_ref[...]
        cps = []
        for k in range(1, 8):
            peer = (x ^ (k >> 2), y ^ ((k >> 1) & 1), c ^ (k & 1))
            cps.append(pltpu.make_async_remote_copy(
                src_ref=v_ref, dst_ref=buf.at[me], send_sem=send.at[k - 1], recv_sem=recv.at[k - 1],
                device_id=peer, device_id_type=MESH))
        for cp in cps:
            cp.start()
        for k in range(1, 8):
            pltpu.make_async_remote_copy(
                src_ref=v_ref, dst_ref=buf.at[me ^ k], send_sem=send.at[k - 1], recv_sem=recv.at[k - 1],
                device_id=(x, y, c), device_id_type=MESH).wait_recv()
        for cp in cps:
            cp.wait_send()
        t = buf[0]
        for d in range(1, 8):
            t = t + buf[d]
        o_ref[...] = t

    return pl.pallas_call(
        body, name="allreduce_small",
        in_specs=[pl.BlockSpec(memory_space=pltpu.VMEM)], out_specs=pl.BlockSpec(memory_space=pltpu.VMEM),
        out_shape=jax.ShapeDtypeStruct((SMALL_ROWS, D), F32),
        scratch_shapes=[pltpu.VMEM((8, SMALL_ROWS, D), F32), pltpu.SemaphoreType.DMA((7,)),
                        pltpu.SemaphoreType.DMA((7,))],
    )(vec)


def _col_half(ref, slot, hc):
    return ref.at[slot, :, pl.ds(hc * HALF, HALF)]


def _ffn_half(first, ref, slot, hc):
    return ref.at[slot, pl.ds(first, 3), :, pl.ds(hc * HALF, HALF)]


def _row_tile(rows):
    for t in range(512, 15, -16):
        if rows % t == 0:
            return t
    return rows


def _same_shape_runs(arrs):
    runs, a = [], 0
    for b in range(1, len(arrs) + 1):
        if b == len(arrs) or arrs[b].shape != arrs[a].shape:
            runs.append((a, b))
            a = b
    return runs


class _Comm:
    def __init__(self):
        x, y, c = lax.axis_index("x"), lax.axis_index("y"), lax.axis_index("c")
        self.c_idx = jnp.reshape(c, (1,)).astype(jnp.int32)
        self.place = jnp.stack([2 * x + y, c]).astype(jnp.int32)
        self.groups = {}

    def gather_mixer(self, wint, wout):
        return _gather_rider([wint, wout], [_col_half, _col_half])

    def gather_ffn(self, ffnw, first):
        return _gather_rider([ffnw], [functools.partial(_ffn_half, first)])

    def reduce_rider(self, tag, names, ps):
        rs = _to_sibling("grads_to_sibling_" + tag, ps)
        csums = []
        for a, b in _same_shape_runs(ps):
            csums += _pair_sum("pair_sum_%s%d" % (tag, a), self.c_idx, ps[a:b], rs[a:b], _row_tile(ps[a].shape[1]))
        self.groups[tag] = [names, csums, None]
        return _to_chips_rider(csums)

    def landed(self, tag, ts):
        self.groups[tag][2] = ts

    def finish(self):
        names, halves = [], []
        for tag, (group_names, csums, ts) in self.groups.items():
            names += group_names
            for a, b in _same_shape_runs(csums):
                halves += _chip_sum("chip_sum_%s%d" % (tag, a), self.place, csums[a:b], ts[a:b],
                                    _row_tile(csums[a].shape[1]))
        return dict(zip(names, _swap_halves(halves)))


ROPE_THETA = 10000.0
SMALL_1K = ("ffn1_pre_norm", "ffn1_post_norm", "mix_pre_norm", "ssm_norm", "mix_post_norm",
            "ffn2_pre_norm", "ffn2_post_norm")
SMALL_16 = ("dt_bias", "a_log", "d_skip")
OFF_CONVB = 7 * D
OFF_16 = OFF_CONVB + CONV_C
OFF_CONVW = OFF_16 + 48
OFF_LOSS = OFF_CONVW + CONV_K * CONV_C
SMALL_LEN = SMALL_ROWS * D


def _sds(shape, dtype):
    return jax.ShapeDtypeStruct(shape, dtype)


def _ffn_down(name, act, ffnw, wi):
    return _mm(name, [act, ffnw], NN, (S // TS, NSH),
               [pl.BlockSpec((None, TS, FS), lambda i, s: (s, i, 0)),
                pl.BlockSpec((None, None, FS, D), lambda i, s: (s, wi + 2, 0, 0))],
               pl.BlockSpec((TS, D), lambda i, s: (i, 0)), _sds((S, D), F32), (TS, D))


def _ridden(res, rider):
    return res if rider is not None else (res, None)


def _ffn_bwd(tag, dh, n, gate, up, act, ffnw, wi, dact_rider=None, dn_rider_of=None):
    (dgate, dup), got1 = _ridden(_ffn_dact(tag + "_dact", dh, ffnw, wi, gate, up, dact_rider), dact_rider)
    aspec = pl.BlockSpec((None, TS, FS), lambda s, k: (s, k, 0))
    nspec = pl.BlockSpec((TS, D), lambda s, k: (k, 0))
    wspec = pl.BlockSpec((None, FS, D), lambda s, k: (s, 0, 0))
    dws = [_mm(tag + nm, [a, b], TN, (NSH, S // TS), [aspec, nspec], wspec, _sds((NSH, FS, D), BF16), (FS, D))
           for nm, a, b in (("_dwg", dgate, n), ("_dwu", dup, n), ("_dwd", act, dh))]
    dn_rider = dn_rider_of(dws) if dn_rider_of is not None else None
    a2 = pl.BlockSpec((None, TS, FS), lambda i, s: (s, i, 0))
    dn, got2 = _ridden(_mm(
        tag + "_dn", [dgate, ffnw, dup, ffnw], NN, (S // TS, NSH),
        [a2, pl.BlockSpec((None, None, FS, D), lambda i, s: (s, wi, 0, 0)),
         a2, pl.BlockSpec((None, None, FS, D), lambda i, s: (s, wi + 1, 0, 0))],
        pl.BlockSpec((TS, D), lambda i, s: (i, 0)), _sds((S, D), F32), (TS, D), dn_rider), dn_rider)
    return dn, dws, got1, got2


def _heads(t, n):
    return t.reshape(S, n, HD).transpose(1, 0, 2)


def _unheads(t):
    return t.transpose(1, 0, 2).reshape(S, t.shape[0] * HD)


def _heads_t(t, n):
    return t.reshape(S, n, HD).transpose(1, 2, 0)


def _blocks5(t):
    return t.reshape(NCH, 128, NKV, NQ_PER_KV, HD)


def _to_blocks_t(t):
    return _blocks5(t).transpose(2, 0, 4, 3, 1).reshape(NKV, NCH, HD, QROWS)


def _to_blocks(t):
    return _blocks5(t).transpose(2, 0, 3, 1, 4).reshape(NKV, NCH, QROWS, HD)


def _from_blocks_t(t):
    return t.reshape(NKV, NCH, HD, NQ_PER_KV, 128).transpose(1, 4, 0, 3, 2).reshape(S, D)


def _pad128(v):
    return jnp.pad(v, ((0, 0), (0, 128 - v.shape[1])))


def _local_step(x, positions, tgt, sp, ffnw, wint, wout, convw, comm=None):
    inv_freq = ROPE_THETA ** (-jnp.arange(0, HD, 2, dtype=F32) / HD)
    ang = positions.astype(F32)[:, None] * inv_freq
    ang = jnp.concatenate([ang, ang, ang, ang], axis=-1)
    cos, sin = jnp.cos(ang), jnp.sin(ang)
    dtb, alog = _pad128(sp["dt_bias"]), _pad128(sp["a_log"])
    dskip_l = jnp.repeat(sp["d_skip"], HD, axis=1)
    convb = sp["conv_b"]

    n1 = _prenorm("prenorm1", x, sp["ffn1_pre_norm"])
    rider = comm.gather_mixer(wint, wout) if comm else None
    (gate1, up1, act1), got = _ridden(_ffn_up("ffn1_up", n1, ffnw, 0, rider), rider)
    if comm:
        wint, wout = got
    wint_pad = jnp.pad(wint.reshape(WIN_COLS, D), ((0, WIN_PAD - WIN_COLS), (0, 0)))
    wout = wout.reshape(2 * D, D)
    h1 = _ffn_down("ffn1_down", act1, ffnw, 0)
    x1, n2 = _postres("postres1", x, h1, sp["ffn1_post_norm"], 0.5, sp["mix_pre_norm"])

    pw = WIN_PAD // 3
    proj = _mm("in_proj", [n2, wint_pad], NT, (S // TS, 3, 1),
               [pl.BlockSpec((TS, D), lambda i, j, k: (i, 0)), pl.BlockSpec((pw, D), lambda i, j, k: (j, 0))],
               pl.BlockSpec((TS, pw), lambda i, j, k: (i, j)), _sds((S, WIN_PAD), F32), (TS, pw))
    q_rot = _rope("rope_q", proj, 0, D, cos, sin, 1.0, HD ** -0.5)
    k_rot = _rope("rope_k", proj, D // KVW, KVW, cos, sin, 1.0, 1.0)
    v_bf = proj[:, D + KVW:D + 2 * KVW].astype(BF16)
    qt, kh, vh = _to_blocks_t(q_rot), _heads(k_rot, NKV), _heads(v_bf, NKV)
    kt, vt = _heads_t(k_rot, NKV), _heads_t(v_bf, NKV)
    bias = _bias_table()
    rider = comm.gather_ffn(ffnw, 3) if comm else None
    (ot, lse), got = _ridden(_attn_fwd(qt, kh, vt, bias, rider), rider)
    if comm:
        ffnw, = got
    attn = _from_blocks_t(ot).astype(BF16)
    xbc = _conv_fwd(proj, convw, convb)
    y, yn, hprev = _ssd_fwd(xbc, proj, dtb, alog, dskip_l, sp["ssm_norm"])
    mixed = jnp.concatenate([attn, yn], axis=1)
    h2 = _mm("out_proj", [mixed, wout], NN, (S // TS, 1),
             [pl.BlockSpec((TS, 2 * D), lambda i, k: (i, 0)), pl.BlockSpec((2 * D, D), lambda i, k: (0, 0))],
             pl.BlockSpec((TS, D), lambda i, k: (i, 0)), _sds((S, D), F32), (TS, D))
    x2, n3 = _postres("postres2", x1, h2, sp["mix_post_norm"], 1.0, sp["ffn2_pre_norm"])

    gate2, up2, act2 = _ffn_up("ffn2_up", n3, ffnw, 3)
    h3 = _ffn_down("ffn2_down", act2, ffnw, 3)
    dy, dh3, dp3, loss = _final(x2, h3, sp["ffn2_post_norm"], tgt, 0.5)

    dn3, dws2, _, _ = _ffn_bwd("ffn2", dh3, n3, gate2, up2, act2, ffnw, 3)
    dx2, dh2, dg3, dp2 = _mid_bwd("mid_bwd2", dy, dn3, x2, sp["ffn2_pre_norm"], h2, sp["mix_post_norm"], 1.0)

    dmixed = _mm("out_proj_dx", [dh2, wout], NT, (S // TS, 1),
                 [pl.BlockSpec((TS, D), lambda i, k: (i, 0)), pl.BlockSpec((2 * D, D), lambda i, k: (0, 0))],
                 pl.BlockSpec((TS, 2 * D), lambda i, k: (i, 0)), _sds((S, 2 * D), F32), (TS, 2 * D))
    dwout = _mm("out_proj_dw", [mixed, dh2], TN, (2, S // TS),
                [pl.BlockSpec((TS, D), lambda m, k: (k, m)), pl.BlockSpec((TS, D), lambda m, k: (k, 0))],
                pl.BlockSpec((D, D), lambda m, k: (m, 0)), _sds((2 * D, D), BF16), (D, D))
    dwout = dwout.reshape(NSH, 2 * D // NSH, D)
    rider = comm.reduce_rider("a", BIG[3:6] + ("w_out",), dws2 + [dwout]) if comm else None
    (dxbc, dz, ddt, dssm, dsc), got = _ridden(
        _ssd_bwd(dmixed, y, xbc, proj, hprev, dtb, alog, dskip_l, sp["ssm_norm"], rider), rider)
    if comm:
        comm.landed("a", got)
    du, dcw8, dcb = _conv_bwd(dxbc, proj, convw, convb)
    do_bf = dmixed[:, :D].astype(BF16)
    dot_ = _to_blocks_t(do_bf)
    dqt, dkh, dvh = _attn_bwd(qt, _to_blocks(q_rot), kh, kt, vh, dot_, _to_blocks(do_bf), lse,
                              _attn_delta(ot, dot_), bias)
    dq = _rope("rope_dq", _from_blocks_t(dqt), 0, D, cos, sin, -1.0, HD ** -0.5)
    dk = _rope("rope_dk", _unheads(dkh), 0, KVW, cos, sin, -1.0, 1.0)
    dproj = jnp.concatenate([dq, dk, _unheads(dvh).astype(BF16), du, dz, ddt], axis=1)
    dn2 = _mm("in_proj_dx", [dproj, wint_pad], NN, (S // TS, 3),
              [pl.BlockSpec((TS, pw), lambda i, k: (i, k)), pl.BlockSpec((pw, D), lambda i, k: (k, 0))],
              pl.BlockSpec((TS, D), lambda i, k: (i, 0)), _sds((S, D), F32), (TS, D))
    dwint = _mm("in_proj_dw", [dproj, n2], TN, (3, S // TS),
                [pl.BlockSpec((TS, pw), lambda j, k: (k, j)), pl.BlockSpec((TS, D), lambda j, k: (k, 0))],
                pl.BlockSpec((pw, D), lambda j, k: (j, 0)), _sds((WIN_PAD, D), BF16), (pw, D))
    dx1, dh1, dg2, dp1 = _mid_bwd("mid_bwd1", dx2, dn2, x1, sp["mix_pre_norm"], h1, sp["ffn1_post_norm"], 0.5)

    dwint = dwint[:WIN_COLS].reshape(NSH, WIN_SH, D)
    dn1, dws1, got1, got2 = _ffn_bwd(
        "ffn1", dh1, n1, gate1, up1, act1, ffnw, 0,
        comm.reduce_rider("b", ("w_in",), [dwint]) if comm else None,
        (lambda dws: comm.reduce_rider("c", BIG[0:3], dws)) if comm else None)
    grad_x, dg1 = _first_bwd(dx1, dn1, x, sp["ffn1_pre_norm"])

    small = jnp.concatenate([
        dg1[0], dp1[0], dg2[0], dssm[0], dp2[0], dg3[0], dp3[0], dcb[0],
        dsc[0, :16], dsc[1, :16], dsc[2, :16], dcw8[:CONV_K].reshape(-1), loss[0, :1]])
    small = jnp.pad(small, (0, SMALL_LEN - small.shape[0])).reshape(SMALL_ROWS, D)
    if comm is None:
        return grad_x, dws1 + dws2 + [dwint, dwout], small
    comm.landed("b", got1)
    comm.landed("c", got2)
    return grad_x, comm.finish(), small


WEIGHTS = ("ffn1_pre_norm", "ffn1_w_gate", "ffn1_w_up", "ffn1_w_down", "ffn1_post_norm", "mix_pre_norm", "w_in",
           "conv_w", "conv_b", "dt_bias", "a_log", "d_skip", "ssm_norm", "w_out", "mix_post_norm", "ffn2_pre_norm",
           "ffn2_w_gate", "ffn2_w_up", "ffn2_w_down", "ffn2_post_norm")
BIG = ("ffn1_w_gate", "ffn1_w_up", "ffn1_w_down", "ffn2_w_gate", "ffn2_w_up", "ffn2_w_down", "w_in", "w_out")
TRANSPOSED = ("ffn1_w_gate", "ffn1_w_up", "ffn2_w_gate", "ffn2_w_up", "w_in")
SMALL_ORDER = SMALL_1K + ("conv_b",) + SMALL_16
CONVW_SH = CONV_C // NSH


def _shard2d(t, name):
    return t[0].T if name in TRANSPOSED else t[0]


def _unshard2d(t, name):
    return (t.T if name in TRANSPOSED else t)[None]


def _pack_small(d, prefix, shard_of_convw):
    flat = jnp.concatenate([d[prefix + n][0] for n in SMALL_ORDER] + [shard_of_convw.reshape(-1)])
    return jnp.pad(flat, (0, SMALL_LEN - flat.shape[0])).reshape(SMALL_ROWS, D)


def _unpack_small(block, like):
    flat = block.reshape(-1)
    out, off = {}, 0
    for n in SMALL_ORDER:
        size = like[n].shape[1]
        out[n] = flat[off:off + size].reshape(1, size)
        off += size
    out["conv_w"] = flat[off:off + CONV_K * CONVW_SH].reshape(1, CONV_K, CONVW_SH)
    return out


def kernel(x, positions, ffn1_pre_norm, ffn1_w_gate, ffn1_w_up, ffn1_w_down, ffn1_post_norm, mix_pre_norm, w_in, conv_w, conv_b, dt_bias, a_log, d_skip, ssm_norm, w_out, mix_post_norm, ffn2_pre_norm, ffn2_w_gate, ffn2_w_up, ffn2_w_down, ffn2_post_norm, loss_target, m_ffn1_pre_norm, m_ffn1_w_gate, m_ffn1_w_up, m_ffn1_w_down, m_ffn1_post_norm, m_mix_pre_norm, m_w_in, m_conv_w, m_conv_b, m_dt_bias, m_a_log, m_d_skip, m_ssm_norm, m_w_out, m_mix_post_norm, m_ffn2_pre_norm, m_ffn2_w_gate, m_ffn2_w_up, m_ffn2_w_down, m_ffn2_post_norm, v_ffn1_pre_norm, v_ffn1_w_gate, v_ffn1_w_up, v_ffn1_w_down, v_ffn1_post_norm, v_mix_pre_norm, v_w_in, v_conv_w, v_conv_b, v_dt_bias, v_a_log, v_d_skip, v_ssm_norm, v_w_out, v_mix_post_norm, v_ffn2_pre_norm, v_ffn2_w_gate, v_ffn2_w_up, v_ffn2_w_down, v_ffn2_post_norm):
    given = dict(locals())
    xi, yi = lax.axis_index("x"), lax.axis_index("y")

    shard = jnp.reshape(2 * xi + yi, (1,)).astype(jnp.int32)
    big = {p + n: _shard2d(given[p + n], n) for n in BIG for p in ("", "m_", "v_")}
    ffnsh = _cast_stack("cast_ffn", shard, [big[n] for n in BIG[:6]], 176, D)
    winsh = _cast_stack("cast_w_in", shard, [big["w_in"]], WIN_SH, 256).reshape(NSH, WIN_SH, D)
    woutsh = _cast_stack("cast_w_out", shard, [big["w_out"]], 256, D).reshape(NSH, 2 * D // NSH, D)
    comm = _Comm()
    (ffnw,), (cwf,) = _run_riders("gather_ffn1", [comm.gather_ffn(ffnsh, 0), _small_gather_rider(conv_w[0])])
    convw = cwf.transpose(1, 0, 2).reshape(CONV_K, CONV_C)

    sp = {n: given[n] for n in SMALL_ORDER}
    grad_x, big_grads, small = _local_step(x[0], positions[0], loss_target[0], sp, ffnw, winsh, woutsh, convw, comm)

    tot = _allreduce_small(small).reshape(-1)
    loss = tot[OFF_LOSS]
    small_grads, off = {}, 0
    for n in SMALL_ORDER:
        size = given[n].shape[1]
        small_grads[n] = tot[off:off + size].reshape(1, size)
        off += size
    dconvw = tot[OFF_CONVW:OFF_CONVW + CONV_K * CONV_C].reshape(CONV_K, NSH, CONVW_SH)
    dconvw = lax.dynamic_index_in_dim(dconvw, 2 * xi + yi, axis=1, keepdims=False)
    small_grads["conv_w"] = dconvw.reshape(1, CONV_K, CONVW_SH)

    upd = {}
    for names, tr, tc in ((BIG[0:3], 176, D), (BIG[3:6], 176, D), (BIG[6:7], WIN_SH, 256), (BIG[7:8], 256, D)):
        res = _adamw("adamw_" + names[0], [big[n] for n in names], [big_grads[n] for n in names],
                     [big["m_" + n] for n in names], [big["v_" + n] for n in names], tr, tc)
        for n, r in zip(names, res):
            upd[n] = tuple(_unshard2d(t, n) for t in r)
    (dl, m2, v2), = _adamw(
        "adamw_small", [_pack_small(given, "", conv_w[0])], [_pack_small(small_grads, "", dconvw)],
        [_pack_small(given, "m_", m_conv_w[0])], [_pack_small(given, "v_", v_conv_w[0])], SMALL_ROWS, D)
    dl, m2, v2 = (_unpack_small(t, given) for t in (dl, m2, v2))
    for n in SMALL_ORDER + ("conv_w",):
        upd[n] = (dl[n], m2[n], v2[n])

    grads = dict(small_grads)
    grads.update({n: _unshard2d(g, n) for n, g in big_grads.items()})
    return (loss, grad_x[None], *[grads[n] for n in WEIGHTS], *[upd[n][0] for n in WEIGHTS],
            *[upd[n][1] for n in WEIGHTS], *[upd[n][2] for n in WEIGHTS])
```

```python
import functools
import typing

import jax
import jax.numpy as jnp
from jax import lax
from jax.experimental import pallas as pl
from jax.experimental.pallas import tpu as pltpu

F32 = jnp.float32
BF16 = jnp.bfloat16

S = 2048
D = 1024
FF = 2816
NSH = 4
FS = FF // NSH
HALF = D // 2
HD = 64
NKV = 4
NQ_PER_KV = 4
KVW = NKV * HD
CONV_C = 1536
CONV_K = 4
SSM_W = 1024
NST = 128
NCH = S // 128
WIN_COLS = 4112
WIN_SH = WIN_COLS // NSH
WIN_PAD = 4224
COL_DT = 4096
EPS = 1e-6
NEG = -1e30

ADAM_LR = 0.001
ADAM_B1 = 0.9
ADAM_B2 = 0.999
ADAM_EPS = 1e-08
ADAM_WD = 0.01
ADAM_STEP = 10

VMEM_LIMIT = 56 * 1024 * 1024
TS = 512
TR = 256

NN = (((1,), (0,)), ((), ()))
NT = (((1,), (1,)), ((), ()))
TN = (((0,), (0,)), ((), ()))
MESH = pl.DeviceIdType.MESH


def _cparams(*sem):
    return pltpu.CompilerParams(dimension_semantics=sem, vmem_limit_bytes=VMEM_LIMIT)


def _dot(a, b, dims):
    return lax.dot_general(a.astype(BF16), b.astype(BF16), dims, preferred_element_type=F32)


def _dot_exact(a, b):
    return lax.dot_general(a, b, NN, precision=lax.Precision.HIGHEST, preferred_element_type=F32)


def _sigmoid(v):
    return 1.0 / (1.0 + jnp.exp(-v))


class _Rider(typing.NamedTuple):
    operands: list
    out_shapes: list
    aliases: dict
    sems: list
    start: typing.Callable
    finish: typing.Callable


def _call(body, name, grid, in_specs, out_specs, out_shape, operands, scratch=(), sem=(), rider=None):
    multi = isinstance(out_shape, (list, tuple))
    if rider is None:
        return pl.pallas_call(
            body, name=name, grid=grid, in_specs=in_specs, out_specs=out_specs, out_shape=out_shape,
            scratch_shapes=list(scratch), compiler_params=_cparams(*sem))(*operands)
    outs = list(out_shape) if multi else [out_shape]
    ospecs = list(out_specs) if multi else [out_specs]
    n_in, n_out, n_scr = len(operands), len(outs), len(scratch)
    ri, ro = len(rider.operands), len(rider.out_shapes)

    def wrapped(*refs):
        o0 = n_in + ri
        s0 = o0 + n_out + ro
        rin, rout, rsem = refs[n_in:o0], refs[o0 + n_out:s0], refs[s0 + n_scr:]
        ids = [pl.program_id(a) for a in range(len(grid))]
        first = functools.reduce(jnp.logical_and, [i == 0 for i in ids])
        last = functools.reduce(jnp.logical_and, [i == g - 1 for i, g in zip(ids, grid)])

        @pl.when(first)
        def _():
            rider.start(rin, rout, rsem)

        body(*refs[:n_in], *refs[o0:o0 + n_out], *refs[s0:s0 + n_scr])

        @pl.when(last)
        def _():
            rider.finish(rin, rout, rsem)

    hbm = pl.BlockSpec(memory_space=pl.ANY)
    res = pl.pallas_call(
        wrapped, name=name, grid=grid, in_specs=list(in_specs) + [hbm] * ri, out_specs=ospecs + [hbm] * ro,
        out_shape=outs + list(rider.out_shapes), scratch_shapes=list(scratch) + list(rider.sems),
        input_output_aliases={n_in + k: n_out + v for k, v in rider.aliases.items()},
        compiler_params=_cparams(*(("arbitrary",) * len(grid))))(*operands, *rider.operands)
    main = list(res[:n_out])
    return (main if multi else main[0]), list(res[n_out:])


def _mm(name, operands, dims, grid, in_specs, o_spec, out_shape, rider=None):
    npairs = len(operands) // 2

    def body(*refs):
        t = None
        for i in range(npairs):
            a, b = refs[2 * i], refs[2 * i + 1]
            parts = [(a[s], b[s]) for s in range(a.shape[0])] if len(a.shape) == 3 else [(a[...], b[...])]
            for pa, pb in parts:
                d = _dot(pa, pb, dims)
                t = d if t is None else t + d
        refs[2 * npairs][...] = t.astype(refs[2 * npairs].dtype)

    return _call(body, name, grid, in_specs, o_spec, out_shape, operands, (), ("parallel",) * len(grid), rider)


def _ffn_up(name, n, ffnw, wi, rider=None):
    def body(n_ref, wg_ref, wu_ref, g_ref, u_ref, a_ref):
        nb = n_ref[...]
        g = _dot(nb, wg_ref[...], NT)
        u = _dot(nb, wu_ref[...], NT)
        g_ref[...] = g.astype(BF16)
        u_ref[...] = u.astype(BF16)
        a_ref[...] = (g * _sigmoid(g) * u).astype(BF16)

    out = jax.ShapeDtypeStruct((NSH, S, FS), BF16)
    ospec = pl.BlockSpec((None, TS, FS), lambda s, i: (s, i, 0))
    return _call(
        body, name, (NSH, S // TS),
        [pl.BlockSpec((TS, D), lambda s, i: (i, 0)),
         pl.BlockSpec((None, None, FS, D), lambda s, i: (s, wi, 0, 0)),
         pl.BlockSpec((None, None, FS, D), lambda s, i: (s, wi + 1, 0, 0))],
        [ospec, ospec, ospec], [out, out, out], (n, ffnw, ffnw), sem=("parallel", "parallel"), rider=rider)


def _ffn_dact(name, dh, ffnw, wi, gate, up, rider=None):
    def body(dh_ref, wd_ref, g_ref, u_ref, dg_ref, du_ref):
        da = _dot(dh_ref[...], wd_ref[...], NT)
        g = g_ref[...].astype(F32)
        u = u_ref[...].astype(F32)
        sg = _sigmoid(g)
        dg_ref[...] = (da * u * (sg * (1.0 + g * (1.0 - sg)))).astype(BF16)
        du_ref[...] = (da * (g * sg)).astype(BF16)

    out = jax.ShapeDtypeStruct((NSH, S, FS), BF16)
    aspec = pl.BlockSpec((None, TS, FS), lambda s, i: (s, i, 0))
    return _call(
        body, name, (NSH, S // TS),
        [pl.BlockSpec((TS, D), lambda s, i: (i, 0)),
         pl.BlockSpec((None, None, FS, D), lambda s, i: (s, wi + 2, 0, 0)), aspec, aspec],
        [aspec, aspec], [out, out], (dh, ffnw, gate, up), sem=("parallel", "parallel"), rider=rider)


def _rstd(v):
    return lax.rsqrt(jnp.mean(v * v, axis=-1, keepdims=True) + EPS)


def _row_spec():
    return pl.BlockSpec((TR, D), lambda i: (i, 0))


def _vec_spec():
    return pl.BlockSpec((1, D), lambda i: (0, 0))


def _acc_rows(ref, v):
    @pl.when(pl.program_id(0) == 0)
    def _():
        ref[...] = jnp.zeros_like(ref)
    ref[...] += jnp.sum(v, axis=0, keepdims=True)


def _prenorm(name, x, g):
    def body(x_ref, g_ref, n_ref):
        xv = x_ref[...]
        n_ref[...] = (xv * _rstd(xv) * g_ref[...]).astype(BF16)

    return pl.pallas_call(
        body, name=name, grid=(S // TR,), in_specs=[_row_spec(), _vec_spec()], out_specs=_row_spec(),
        out_shape=jax.ShapeDtypeStruct((S, D), BF16), compiler_params=_cparams("parallel"),
    )(x, g)


def _postres(name, x, h, p, alpha, gnext):
    def body(x_ref, h_ref, p_ref, g_ref, xo_ref, n_ref):
        hv = h_ref[...]
        xo = x_ref[...] + alpha * (hv * _rstd(hv) * p_ref[...])
        xo_ref[...] = xo
        n_ref[...] = (xo * _rstd(xo) * g_ref[...]).astype(BF16)

    return pl.pallas_call(
        body, name=name, grid=(S // TR,),
        in_specs=[_row_spec(), _row_spec(), _vec_spec(), _vec_spec()],
        out_specs=[_row_spec(), _row_spec()],
        out_shape=[jax.ShapeDtypeStruct((S, D), F32), jax.ShapeDtypeStruct((S, D), BF16)],
        compiler_params=_cparams("parallel"),
    )(x, h, p, gnext)


def _final(x, h, p, tgt, alpha):
    def body(x_ref, h_ref, p_ref, t_ref, dy_ref, dh_ref, dp_ref, loss_ref):
        hv = h_ref[...]
        r = _rstd(hv)
        hn = hv * r
        pv = p_ref[...]
        e = x_ref[...] + alpha * (hn * pv) - t_ref[...]
        dy = e * (1.0 / D)
        dy_ref[...] = dy
        du = alpha * dy * pv
        dh_ref[...] = (r * (du - hn * jnp.mean(du * hn, axis=-1, keepdims=True))).astype(BF16)
        _acc_rows(dp_ref, alpha * dy * hn)
        part = 0.5 * jnp.sum(jnp.mean(e * e, axis=-1, keepdims=True), axis=0, keepdims=True)
        _acc_rows(loss_ref, jnp.broadcast_to(part, (1, 128)))

    return pl.pallas_call(
        body, name="loss_head", grid=(S // TR,),
        in_specs=[_row_spec(), _row_spec(), _vec_spec(), _row_spec()],
        out_specs=[_row_spec(), _row_spec(), _vec_spec(), pl.BlockSpec((1, 128), lambda i: (0, 0))],
        out_shape=[jax.ShapeDtypeStruct((S, D), F32), jax.ShapeDtypeStruct((S, D), BF16),
                   jax.ShapeDtypeStruct((1, D), F32), jax.ShapeDtypeStruct((1, 128), F32)],
        compiler_params=_cparams("arbitrary"),
    )(x, h, p, tgt)


def _mid_bwd(name, dres, dn, x, g, h, p, alpha):
    def body(dr_ref, dn_ref, x_ref, g_ref, h_ref, p_ref, dx_ref, dh_ref, dg_ref, dp_ref):
        xv = x_ref[...]
        xn = xv * _rstd(xv)
        dnv = dn_ref[...]
        dng = dnv * g_ref[...]
        dx = dr_ref[...] + _rstd(xv) * (dng - xn * jnp.mean(dng * xn, axis=-1, keepdims=True))
        dx_ref[...] = dx
        _acc_rows(dg_ref, dnv * xn)
        hv = h_ref[...]
        r = _rstd(hv)
        hn = hv * r
        du = alpha * dx * p_ref[...]
        dh_ref[...] = (r * (du - hn * jnp.mean(du * hn, axis=-1, keepdims=True))).astype(BF16)
        _acc_rows(dp_ref, alpha * dx * hn)

    return pl.pallas_call(
        body, name=name, grid=(S // TR,),
        in_specs=[_row_spec(), _row_spec(), _row_spec(), _vec_spec(), _row_spec(), _vec_spec()],
        out_specs=[_row_spec(), _row_spec(), _vec_spec(), _vec_spec()],
        out_shape=[jax.ShapeDtypeStruct((S, D), F32), jax.ShapeDtypeStruct((S, D), BF16),
                   jax.ShapeDtypeStruct((1, D), F32), jax.ShapeDtypeStruct((1, D), F32)],
        compiler_params=_cparams("arbitrary"),
    )(dres, dn, x, g, h, p)


def _first_bwd(dres, dn, x, g):
    def body(dr_ref, dn_ref, x_ref, g_ref, dx_ref, dg_ref):
        xv = x_ref[...]
        r = _rstd(xv)
        xn = xv * r
        dnv = dn_ref[...]
        dng = dnv * g_ref[...]
        dx_ref[...] = dr_ref[...] + r * (dng - xn * jnp.mean(dng * xn, axis=-1, keepdims=True))
        _acc_rows(dg_ref, dnv * xn)

    return pl.pallas_call(
        body, name="first_bwd", grid=(S // TR,),
        in_specs=[_row_spec(), _row_spec(), _row_spec(), _vec_spec()],
        out_specs=[_row_spec(), _vec_spec()],
        out_shape=[jax.ShapeDtypeStruct((S, D), F32), jax.ShapeDtypeStruct((1, D), F32)],
        compiler_params=_cparams("arbitrary"),
    )(dres, dn, x, g)


def _rope(name, src, col_block, width, cos, sin, sign, scale):
    def body(t_ref, c_ref, s_ref, o_ref):
        t = t_ref[...].astype(F32)
        c = jnp.tile(c_ref[...], (1, width // 128))
        sn = jnp.tile(s_ref[...], (1, width // 128))
        lane = lax.broadcasted_iota(jnp.int32, t.shape, 1) & (HD - 1)
        rot = jnp.where(lane < HD // 2, -pltpu.roll(t, width - HD // 2, 1), pltpu.roll(t, HD // 2, 1))
        o_ref[...] = ((t * c + sign * (rot * sn)) * scale).astype(BF16)

    return pl.pallas_call(
        body, name=name, grid=(S // TR,),
        in_specs=[pl.BlockSpec((TR, width), lambda i: (i, col_block)),
                  pl.BlockSpec((TR, 128), lambda i: (i, 0)), pl.BlockSpec((TR, 128), lambda i: (i, 0))],
        out_specs=pl.BlockSpec((TR, width), lambda i: (i, 0)),
        out_shape=jax.ShapeDtypeStruct((S, width), BF16), compiler_params=_cparams("parallel"),
    )(src, cos, sin)


QROWS = NQ_PER_KV * 128


NBIAS = NCH + 1


def _bias_table():
    db = lax.broadcasted_iota(jnp.int32, (NBIAS, 128, QROWS), 0) - 1
    ki = lax.broadcasted_iota(jnp.int32, (NBIAS, 128, QROWS), 1)
    qi = lax.broadcasted_iota(jnp.int32, (NBIAS, 128, QROWS), 2) & 127
    d = db * 128 + qi - ki
    cnt = ((d <= 128).astype(F32) + (((d & 3) == 0) & (d <= 512)).astype(F32) + ((d & 15) == 0).astype(F32))
    return jnp.where((d >= 0) & (cnt > 0.0), jnp.log(jnp.maximum(cnt, 1.0)), NEG)


def _qt_spec():
    return pl.BlockSpec((None, None, HD, QROWS), lambda j, i: (j, i, 0, 0))


def _stat_spec():
    return pl.BlockSpec((None, None, 1, QROWS), lambda j, i: (j, i, 0, 0))


def _attn_fwd(qt, kh, vt, bias, rider=None):
    def body(q_ref, k_ref, v_ref, b_ref, o_ref, lse_ref):
        qb = pl.program_id(1)
        q = q_ref[...]

        def step(i, carry):
            m, l, acc = carry
            off = pl.multiple_of(i * 256, 256)
            s = _dot(k_ref[pl.ds(off, 256), :], q, NN)
            s = jnp.concatenate([s[:128] + b_ref[qb - 2 * i + 1], s[128:] + b_ref[qb - 2 * i]], axis=0)
            m_new = jnp.maximum(m, jnp.max(s, axis=0, keepdims=True))
            p = jnp.exp(s - m_new)
            a = jnp.exp(m - m_new)
            return (m_new, a * l + jnp.sum(p, axis=0, keepdims=True),
                    a * acc + _dot(v_ref[:, pl.ds(off, 256)], p, NN))

        m, l, acc = lax.fori_loop(
            0, qb // 2 + 1, step,
            (jnp.full((1, QROWS), NEG, F32), jnp.zeros((1, QROWS), F32), jnp.zeros((HD, QROWS), F32)))
        o_ref[...] = acc / l
        lse_ref[...] = m + jnp.log(l)

    return _call(
        body, "attn_fwd", (NKV, NCH),
        [_qt_spec(), pl.BlockSpec((None, S, HD), lambda j, i: (j, 0, 0)),
         pl.BlockSpec((None, HD, S), lambda j, i: (j, 0, 0)),
         pl.BlockSpec((NBIAS, 128, QROWS), lambda j, i: (0, 0, 0))],
        [_qt_spec(), _stat_spec()],
        [jax.ShapeDtypeStruct((NKV, NCH, HD, QROWS), F32), jax.ShapeDtypeStruct((NKV, NCH, 1, QROWS), F32)],
        (qt, kh, vt, bias), sem=("parallel", "parallel"), rider=rider)


def _attn_delta(ot, dot_):
    def body(o_ref, do_ref, dl_ref):
        dl_ref[...] = jnp.sum(o_ref[...] * do_ref[...].astype(F32), axis=1, keepdims=True)

    spec = pl.BlockSpec((None, NCH, HD, QROWS), lambda j: (j, 0, 0, 0))
    return pl.pallas_call(
        body, name="attn_delta", grid=(NKV,), in_specs=[spec, spec],
        out_specs=pl.BlockSpec((None, NCH, 1, QROWS), lambda j: (j, 0, 0, 0)),
        out_shape=jax.ShapeDtypeStruct((NKV, NCH, 1, QROWS), F32), compiler_params=_cparams("parallel"),
    )(ot, dot_)


def _attn_bwd(qt, q2, kh, kt, vh, dot_, do2, lse, delta, bias):
    def body(qt_ref, q2_ref, k_ref, kt_ref, v_ref, dot_ref, do2_ref, lse_ref, dl_ref, b_ref, dq_ref, dk_ref, dv_ref):
        kb = pl.program_id(1)

        @pl.when(kb == 0)
        def _():
            dq_ref[...] = jnp.zeros_like(dq_ref)

        k = k_ref[...]
        kt_ = kt_ref[...]
        v = v_ref[...]

        def step(j, carry):
            dk, dv = carry
            for qb in (2 * j, 2 * j + 1):
                st = _dot(k, qt_ref[qb], NN) + b_ref[qb - kb + 1]
                pt = jnp.exp(st - lse_ref[qb])
                dst = pt * (_dot(v, dot_ref[qb], NN) - dl_ref[qb])
                dq_ref[qb] += _dot(kt_, dst, NN)
                dk = dk + _dot(dst, q2_ref[qb], NN)
                dv = dv + _dot(pt, do2_ref[qb], NN)
            return dk, dv

        dk, dv = lax.fori_loop(kb // 2, NCH // 2, step, (jnp.zeros((128, HD), F32), jnp.zeros((128, HD), F32)))
        dk_ref[...] = dk
        dv_ref[...] = dv

    tspec = pl.BlockSpec((None, NCH, HD, QROWS), lambda j, i: (j, 0, 0, 0))
    rspec = pl.BlockSpec((None, NCH, QROWS, HD), lambda j, i: (j, 0, 0, 0))
    kspec = pl.BlockSpec((None, 128, HD), lambda j, i: (j, i, 0))
    sspec = pl.BlockSpec((None, NCH, 1, QROWS), lambda j, i: (j, 0, 0, 0))
    return pl.pallas_call(
        body, name="attn_bwd", grid=(NKV, NCH),
        in_specs=[tspec, rspec, kspec, pl.BlockSpec((None, HD, 128), lambda j, i: (j, 0, i)), kspec, tspec, rspec,
                  sspec, sspec, pl.BlockSpec((NBIAS, 128, QROWS), lambda j, i: (0, 0, 0))],
        out_specs=[tspec, kspec, kspec],
        out_shape=[jax.ShapeDtypeStruct((NKV, NCH, HD, QROWS), F32),
                   jax.ShapeDtypeStruct((NKV, S, HD), F32), jax.ShapeDtypeStruct((NKV, S, HD), F32)],
        compiler_params=_cparams("parallel", "arbitrary"),
    )(qt, q2, kh, kt, vh, dot_, do2, lse, delta, bias)


CONV_BLK = 256
CONV_COL0 = 1536 // CONV_BLK


def _shift_down(u, j, row):
    return jnp.where(row >= j, pltpu.roll(u, j, 0), 0.0)


def _conv_pre(u, w_ref, b_ref, row):
    y = b_ref[...] + w_ref[CONV_K - 1:CONV_K, :] * u
    for j in range(1, CONV_K):
        y = y + w_ref[CONV_K - 1 - j:CONV_K - j, :] * _shift_down(u, j, row)
    return y


def _conv_fwd(proj, convw, convb):
    def body(u_ref, w_ref, b_ref, o_ref):
        u = u_ref[...]
        row = lax.broadcasted_iota(jnp.int32, u.shape, 0)
        y = _conv_pre(u, w_ref, b_ref, row)
        o_ref[...] = y * _sigmoid(y)

    return pl.pallas_call(
        body, name="conv_fwd", grid=(CONV_C // CONV_BLK,),
        in_specs=[pl.BlockSpec((S, CONV_BLK), lambda i: (0, CONV_COL0 + i)),
                  pl.BlockSpec((CONV_K, CONV_BLK), lambda i: (0, i)),
                  pl.BlockSpec((1, CONV_BLK), lambda i: (0, i))],
        out_specs=pl.BlockSpec((S, CONV_BLK), lambda i: (0, i)),
        out_shape=jax.ShapeDtypeStruct((S, CONV_C), F32), compiler_params=_cparams("parallel"),
    )(proj, convw, convb)


def _conv_bwd(dact, proj, convw, convb):
    def body(da_ref, u_ref, w_ref, b_ref, du_ref, dw_ref, db_ref):
        u = u_ref[...]
        row = lax.broadcasted_iota(jnp.int32, u.shape, 0)
        y = _conv_pre(u, w_ref, b_ref, row)
        sg = _sigmoid(y)
        dy = da_ref[...] * (sg * (1.0 + y * (1.0 - sg)))
        db_ref[...] = jnp.sum(dy, axis=0, keepdims=True)
        du = w_ref[CONV_K - 1:CONV_K, :] * dy
        r8 = lax.broadcasted_iota(jnp.int32, (8, CONV_BLK), 0)
        dw = jnp.where(r8 == CONV_K - 1, jnp.sum(dy * u, axis=0, keepdims=True), 0.0)
        for j in range(1, CONV_K):
            du = du + w_ref[CONV_K - 1 - j:CONV_K - j, :] * jnp.where(row < S - j, pltpu.roll(dy, S - j, 0), 0.0)
            dw = dw + jnp.where(r8 == CONV_K - 1 - j,
                                jnp.sum(dy * _shift_down(u, j, row), axis=0, keepdims=True), 0.0)
        du_ref[...] = du.astype(BF16)
        dw_ref[...] = dw

    return pl.pallas_call(
        body, name="conv_bwd", grid=(CONV_C // CONV_BLK,),
        in_specs=[pl.BlockSpec((S, CONV_BLK), lambda i: (0, i)),
                  pl.BlockSpec((S, CONV_BLK), lambda i: (0, CONV_COL0 + i)),
                  pl.BlockSpec((CONV_K, CONV_BLK), lambda i: (0, i)),
                  pl.BlockSpec((1, CONV_BLK), lambda i: (0, i))],
        out_specs=[pl.BlockSpec((S, CONV_BLK), lambda i: (0, i)), pl.BlockSpec((8, CONV_BLK), lambda i: (0, i)),
                   pl.BlockSpec((1, CONV_BLK), lambda i: (0, i))],
        out_shape=[jax.ShapeDtypeStruct((S, CONV_C), BF16), jax.ShapeDtypeStruct((8, CONV_C), F32),
                   jax.ShapeDtypeStruct((1, CONV_C), F32)],
        compiler_params=_cparams("parallel"),
    )(dact, proj, convw, convb)


NPAIR = 8


def _ssd_scalars(dtr_ref, dtb_ref, alog_ref):
    z = dtr_ref[...] + dtb_ref[...]
    dt = jnp.maximum(z, 0.0) + jnp.log(1.0 + jnp.exp(-jnp.abs(z)))
    a = -jnp.exp(alog_ref[...])
    r = lax.broadcasted_iota(jnp.int32, (128, 128), 0)
    c = lax.broadcasted_iota(jnp.int32, (128, 128), 1)
    tri = (r >= c).astype(F32)
    cs = _dot_exact(tri, dt * a)
    return z, dt, a, cs, r, c


def _pair_terms(cs, cst, dt, h1, h2, lo):
    c1, c2 = cs[:, h1:h1 + 1], cs[:, h2:h2 + 1]
    l1, l2 = cs[127:128, h1:h1 + 1], cs[127:128, h2:h2 + 1]
    e_l = jnp.where(lo, jnp.exp(c1), jnp.exp(c2))
    dte1, dte2 = jnp.exp(l1 - c1), jnp.exp(l2 - c2)
    dte_l = jnp.where(lo, dte1, dte2)
    dt_l = jnp.where(lo, dt[:, h1:h1 + 1], dt[:, h2:h2 + 1])
    return c1, c2, jnp.exp(l1), jnp.exp(l2), e_l, dte1, dte2, dte_l, dt_l


def _gate_norm(y, zv, w):
    yg = y * (zv * _sigmoid(zv))
    outs, rs = [], []
    for g in range(2):
        blk = yg[:, 512 * g:512 * (g + 1)]
        r = lax.rsqrt(jnp.mean(blk * blk, axis=-1, keepdims=True) + EPS)
        outs.append(blk * r)
        rs.append(r)
    return jnp.concatenate(outs, axis=1), rs, yg


def _ssd_fwd(xbc, proj, dtb, alog, dskip_l, ssmw):
    def body(x_ref, b_ref, c_ref, dtr_ref, z_ref, dtb_ref, alog_ref, dsk_ref, w_ref, y_ref, yn_ref, hp_ref, h_ref):
        @pl.when(pl.program_id(0) == 0)
        def _():
            h_ref[...] = jnp.zeros_like(h_ref)

        _, dt, _, cs, r, c = _ssd_scalars(dtr_ref, dtb_ref, alog_ref)
        cst = cs.T
        causal = r >= c
        lo = c < HD
        hp_ref[...] = h_ref[...]
        for g in range(2):
            bg = b_ref[:, 128 * g:128 * (g + 1)]
            cg = c_ref[:, 128 * g:128 * (g + 1)]
            cb = _dot(cg, bg, NT)
            for j in range(4):
                pj = 4 * g + j
                h1, h2 = 2 * pj, 2 * pj + 1
                sl = slice(128 * pj, 128 * (pj + 1))
                xp = x_ref[:, sl]
                c1, c2, cd1, cd2, e_l, _, _, dte_l, dt_l = _pair_terms(cs, cst, dt, h1, h2, lo)
                xdt = xp * dt_l
                m1 = cb * jnp.exp(jnp.where(causal, c1 - cst[h1:h1 + 1, :], NEG))
                m2 = cb * jnp.exp(jnp.where(causal, c2 - cst[h2:h2 + 1, :], NEG))
                yd = jnp.where(lo, _dot(m1, xdt, NN), _dot(m2, xdt, NN))
                hp = h_ref[pj]
                yo = _dot(cg, hp, NT) * e_l
                st = _dot(xdt * dte_l, bg, TN)
                h_ref[pj] = hp * jnp.where(r < HD, cd1, cd2) + st
                y_ref[:, sl] = yd + yo + dsk_ref[:, sl] * xp
        yn, _, _ = _gate_norm(y_ref[...], z_ref[...], w_ref[...])
        yn_ref[...] = (yn * w_ref[...]).astype(BF16)

    return pl.pallas_call(
        body, name="ssd_fwd", grid=(NCH,),
        in_specs=[pl.BlockSpec((128, SSM_W), lambda i: (i, 0)),
                  pl.BlockSpec((128, 256), lambda i: (i, 4)), pl.BlockSpec((128, 256), lambda i: (i, 5)),
                  pl.BlockSpec((128, 128), lambda i: (i, COL_DT // 128)),
                  pl.BlockSpec((128, SSM_W), lambda i: (i, 3)),
                  pl.BlockSpec((1, 128), lambda i: (0, 0)), pl.BlockSpec((1, 128), lambda i: (0, 0)),
                  pl.BlockSpec((1, SSM_W), lambda i: (0, 0)), pl.BlockSpec((1, SSM_W), lambda i: (0, 0))],
        out_specs=[pl.BlockSpec((128, SSM_W), lambda i: (i, 0)), pl.BlockSpec((128, SSM_W), lambda i: (i, 0)),
                   pl.BlockSpec((None, NPAIR, 128, 128), lambda i: (i, 0, 0, 0))],
        out_shape=[jax.ShapeDtypeStruct((S, SSM_W), F32), jax.ShapeDtypeStruct((S, SSM_W), BF16),
                   jax.ShapeDtypeStruct((NCH, NPAIR, 128, 128), F32)],
        scratch_shapes=[pltpu.VMEM((NPAIR, 128, 128), F32)],
        compiler_params=_cparams("arbitrary"),
    )(xbc, xbc, xbc, proj, proj, dtb, alog, dskip_l, ssmw)


def _ssd_bwd(dmixed, y, xbc, proj, hprev, dtb, alog, dskip_l, ssmw, rider=None):
    def body(dyn_ref, y_ref, x_ref, b_ref, c_ref, dtr_ref, z_ref, hp_ref, dtb_ref, alog_ref, dsk_ref, w_ref,
             dxbc_ref, dz_ref, ddt_ref, dw_ref, dsc_ref, g_ref):
        @pl.when(pl.program_id(0) == 0)
        def _():
            g_ref[...] = jnp.zeros_like(g_ref)
            dsc_ref[...] = jnp.zeros_like(dsc_ref)

        z, dt, a, cs, r, c = _ssd_scalars(dtr_ref, dtb_ref, alog_ref)
        cst = cs.T
        causal = r >= c
        lo = c < HD

        yv = y_ref[...]
        zv = z_ref[...]
        wv = w_ref[...]
        ygn, rs, yg = _gate_norm(yv, zv, wv)
        dyn = dyn_ref[...]
        _acc_rows(dw_ref, dyn * ygn)
        dynw = dyn * wv
        parts = []
        for g in range(2):
            sl = slice(512 * g, 512 * (g + 1))
            a_g, n_g = dynw[:, sl], ygn[:, sl]
            parts.append(rs[g] * (a_g - n_g * jnp.mean(a_g * n_g, axis=-1, keepdims=True)))
        dyg = jnp.concatenate(parts, axis=1)
        sz = _sigmoid(zv)
        dz_ref[...] = (dyg * yv * (sz * (1.0 + zv * (1.0 - sz)))).astype(BF16)
        dy_all = dyg * (zv * sz)

        dcs_cols = jnp.zeros((128, 128), F32)
        dcs_rows = jnp.zeros((128, 128), F32)
        ddt_x = jnp.zeros((128, 128), F32)
        dd_row = jnp.zeros((1, 128), F32)
        last = r == 127
        for g in range(2):
            bg = b_ref[:, 128 * g:128 * (g + 1)]
            cg = c_ref[:, 128 * g:128 * (g + 1)]
            cb = _dot(cg, bg, NT)
            dcb = jnp.zeros((128, 128), F32)
            db_acc = jnp.zeros((128, NST), F32)
            dc_acc = jnp.zeros((128, NST), F32)
            for j in range(4):
                pj = 4 * g + j
                h1, h2 = 2 * pj, 2 * pj + 1
                sl = slice(128 * pj, 128 * (pj + 1))
                xp = x_ref[:, sl]
                dyp = dy_all[:, sl]
                c1, c2, cd1, cd2, e_l, dte1, dte2, dte_l, dt_l = _pair_terms(cs, cst, dt, h1, h2, lo)
                xdt = xp * dt_l
                hp = hp_ref[pj]
                gp = g_ref[pj]
                dxp = dsk_ref[:, sl] * dyp
                dyx = dyp * xp
                dd_row = dd_row + jnp.where(c[0:1, :] == h1, jnp.sum(jnp.where(lo, dyx, 0.0), keepdims=True), 0.0) \
                    + jnp.where(c[0:1, :] == h2, jnp.sum(jnp.where(lo, 0.0, dyx), keepdims=True), 0.0)
                dzs = dyp * e_l
                dc_acc = dc_acc + _dot(dzs, hp, NN)
                g_from = _dot(dzs, cg, TN)
                ryo = dyp * (_dot(cg, hp, NT) * e_l)
                k1 = jnp.sum(jnp.where(lo, ryo, 0.0), axis=1, keepdims=True)
                k2 = jnp.sum(jnp.where(lo, 0.0, ryo), axis=1, keepdims=True)
                qm = _dot(bg, gp, NT)
                dxdt = qm * dte_l
                qx = qm * xdt
                t1 = jnp.sum(jnp.where(lo, qx, 0.0), axis=1, keepdims=True) * dte1
                t2 = jnp.sum(jnp.where(lo, 0.0, qx), axis=1, keepdims=True) * dte2
                db_acc = db_acc + _dot(xdt * dte_l, gp, NN)
                gh = gp * hp
                dl1 = jnp.sum(t1, keepdims=True) + jnp.sum(jnp.where(r < HD, gh, 0.0), keepdims=True) * cd1
                dl2 = jnp.sum(t2, keepdims=True) + jnp.sum(jnp.where(r < HD, 0.0, gh), keepdims=True) * cd2
                g_ref[pj] = g_from + jnp.where(r < HD, cd1, cd2) * gp
                k1 = k1 - t1 + jnp.where(last[:, 0:1], dl1, 0.0)
                k2 = k2 - t2 + jnp.where(last[:, 0:1], dl2, 0.0)
                for hh, ch, msk in ((h1, c1, lo), (h2, c2, jnp.logical_not(lo))):
                    lm = jnp.exp(jnp.where(causal, ch - cst[hh:hh + 1, :], NEG))
                    mm = cb * lm
                    dm = jnp.where(causal, _dot(jnp.where(msk, dyp, 0.0), xdt, NT), 0.0)
                    w = dm * mm
                    kk = jnp.sum(w, axis=1, keepdims=True)
                    if hh == h1:
                        k1 = k1 + kk
                    else:
                        k2 = k2 + kk
                    dcs_rows = dcs_rows + jnp.where(r == hh, jnp.sum(w, axis=0, keepdims=True), 0.0)
                    dcb = dcb + dm * lm
                    dxdt = dxdt + jnp.where(msk, _dot(mm, dyp, TN), 0.0)
                dcs_cols = dcs_cols + jnp.where(c == h1, k1, 0.0) + jnp.where(c == h2, k2, 0.0)
                dxx = dxdt * xp
                ddt_x = ddt_x + jnp.where(c == h1, jnp.sum(jnp.where(lo, dxx, 0.0), axis=1, keepdims=True), 0.0) \
                    + jnp.where(c == h2, jnp.sum(jnp.where(lo, 0.0, dxx), axis=1, keepdims=True), 0.0)
                dxbc_ref[:, sl] = dxp + dxdt * dt_l
            dxbc_ref[:, SSM_W + 128 * g:SSM_W + 128 * (g + 1)] = db_acc + _dot(dcb, cg, TN)
            dxbc_ref[:, SSM_W + 256 + 128 * g:SSM_W + 256 + 128 * (g + 1)] = dc_acc + _dot(dcb, bg, NN)

        dcs = dcs_cols - dcs_rows.T
        dad = _dot_exact((c >= r).astype(F32), dcs)
        ddt = dad * a + ddt_x
        ddtr = jnp.where(c < 16, ddt * _sigmoid(z), 0.0)
        ddt_ref[...] = ddtr.astype(BF16)
        r8 = lax.broadcasted_iota(jnp.int32, (8, 128), 0)
        dsc_ref[...] += (jnp.where(r8 == 0, jnp.sum(ddtr, axis=0, keepdims=True), 0.0)
                         + jnp.where(r8 == 1, jnp.sum(dad * dt, axis=0, keepdims=True) * a, 0.0)
                         + jnp.where(r8 == 2, dd_row, 0.0))

    rev = NCH - 1
    return _call(
        body, "ssd_bwd", (NCH,),
        [pl.BlockSpec((128, SSM_W), lambda i: (rev - i, 1)),
         pl.BlockSpec((128, SSM_W), lambda i: (rev - i, 0)),
         pl.BlockSpec((128, SSM_W), lambda i: (rev - i, 0)),
         pl.BlockSpec((128, 256), lambda i: (rev - i, 4)), pl.BlockSpec((128, 256), lambda i: (rev - i, 5)),
         pl.BlockSpec((128, 128), lambda i: (rev - i, COL_DT // 128)),
         pl.BlockSpec((128, SSM_W), lambda i: (rev - i, 3)),
         pl.BlockSpec((None, NPAIR, 128, 128), lambda i: (rev - i, 0, 0, 0)),
         pl.BlockSpec((1, 128), lambda i: (0, 0)), pl.BlockSpec((1, 128), lambda i: (0, 0)),
         pl.BlockSpec((1, SSM_W), lambda i: (0, 0)), pl.BlockSpec((1, SSM_W), lambda i: (0, 0))],
        [pl.BlockSpec((128, CONV_C), lambda i: (rev - i, 0)),
         pl.BlockSpec((128, SSM_W), lambda i: (rev - i, 0)),
         pl.BlockSpec((128, 128), lambda i: (rev - i, 0)),
         pl.BlockSpec((1, SSM_W), lambda i: (0, 0)), pl.BlockSpec((8, 128), lambda i: (0, 0))],
        [jax.ShapeDtypeStruct((S, CONV_C), F32), jax.ShapeDtypeStruct((S, SSM_W), BF16),
         jax.ShapeDtypeStruct((S, 128), BF16), jax.ShapeDtypeStruct((1, SSM_W), F32),
         jax.ShapeDtypeStruct((8, 128), F32)],
        (dmixed, y, xbc, xbc, xbc, proj, proj, hprev, dtb, alog, dskip_l, ssmw),
        [pltpu.VMEM((NPAIR, 128, 128), F32)], ("arbitrary",), rider)


def _cast_stack(name, slot, arrs, tr, tc):
    n = len(arrs)
    rows, cols = arrs[0].shape

    def body(s_ref, *refs):
        for i in range(n):
            refs[n][i] = refs[i][...].astype(BF16)

    return pl.pallas_call(
        body, name=name,
        grid_spec=pltpu.PrefetchScalarGridSpec(
            num_scalar_prefetch=1, grid=(rows // tr, cols // tc),
            in_specs=[pl.BlockSpec((tr, tc), lambda i, j, sr: (i, j))] * n,
            out_specs=pl.BlockSpec((None, n, tr, tc), lambda i, j, sr: (sr[0], 0, i, j))),
        out_shape=jax.ShapeDtypeStruct((NSH, n, rows, cols), BF16),
        compiler_params=_cparams("parallel", "parallel"),
    )(slot, *arrs)


def _pair_sum(name, c_idx, ps, rs, th):
    n = len(ps)
    _, rows, _ = ps[0].shape

    def body(c_ref, *refs):
        for i in range(n):
            refs[2 * n + i][...] = (refs[i][...].astype(F32) + refs[n + i][...].astype(F32)).astype(BF16)

    spec = pl.BlockSpec((None, th, HALF), lambda s, i, cr: (s, i, 0))
    return pl.pallas_call(
        body, name=name,
        grid_spec=pltpu.PrefetchScalarGridSpec(
            num_scalar_prefetch=1, grid=(NSH, rows // th),
            in_specs=[pl.BlockSpec((None, th, HALF), lambda s, i, cr: (s, i, cr[0]))] * n + [spec] * n,
            out_specs=[spec] * n),
        out_shape=[jax.ShapeDtypeStruct((NSH, rows, HALF), BF16)] * n,
        compiler_params=_cparams("parallel", "parallel"),
    )(c_idx, *ps, *rs)


def _chip_sum(name, place, cs, ts, th):
    n = len(ts)
    _, rows, _ = ts[0].shape

    def body(p_ref, *refs):
        for i in range(n):
            t = refs[n + i][...].astype(F32)
            refs[2 * n + i][...] = ((refs[i][...].astype(F32) + t[0]) + t[1]) + t[2]

    return pl.pallas_call(
        body, name=name,
        grid_spec=pltpu.PrefetchScalarGridSpec(
            num_scalar_prefetch=1, grid=(rows // th,),
            in_specs=[pl.BlockSpec((None, th, HALF), lambda i, pr: (pr[0], i, 0))] * n
            + [pl.BlockSpec((3, th, HALF), lambda i, pr: (0, i, 0))] * n,
            out_specs=[pl.BlockSpec((th, HALF), lambda i, pr: (i, pr[1]))] * n),
        out_shape=[jax.ShapeDtypeStruct((rows, D), F32)] * n, compiler_params=_cparams("parallel"),
    )(place, *cs, *ts)


def _adamw(name, ws, gs, ms, vs, tr, tc):
    n = len(ws)
    shape = ws[0].shape
    rows, cols, mid = shape[0], shape[-1], shape[1:-1]
    c1 = 1.0 / (1.0 - ADAM_B1 ** ADAM_STEP)
    c2 = 1.0 / (1.0 - ADAM_B2 ** ADAM_STEP)

    def body(*refs):
        for i in range(n):
            w, g, m, v = (refs[k * n + i][...] for k in range(4))
            m2 = ADAM_B1 * m + (1.0 - ADAM_B1) * g
            v2 = ADAM_B2 * v + (1.0 - ADAM_B2) * (g * g)
            refs[4 * n + 3 * i][...] = -ADAM_LR * ((m2 * c1) / (jnp.sqrt(v2 * c2) + ADAM_EPS) + ADAM_WD * w)
            refs[4 * n + 3 * i + 1][...] = m2
            refs[4 * n + 3 * i + 2][...] = v2

    spec = pl.BlockSpec((tr,) + mid + (tc,), lambda i, j: (i,) + (0,) * len(mid) + (j,))
    outs = pl.pallas_call(
        body, name=name, grid=(rows // tr, cols // tc), in_specs=[spec] * (4 * n), out_specs=[spec] * (3 * n),
        out_shape=[jax.ShapeDtypeStruct(shape, F32)] * (3 * n),
        compiler_params=_cparams("parallel", "parallel"),
    )(*ws, *gs, *ms, *vs)
    return [tuple(outs[3 * i:3 * i + 3]) for i in range(n)]


def _place():
    x, y, c = lax.axis_index("x"), lax.axis_index("y"), lax.axis_index("c")
    chips = [(1 - x, y), (x, 1 - y), (1 - x, 1 - y)]
    return x, y, c, chips


def _any_specs(n):
    return [pl.BlockSpec(memory_space=pl.ANY)] * n


def _rcopy(src, dst, send_sem, recv_sem, dev):
    return pltpu.make_async_remote_copy(src_ref=src, dst_ref=dst, send_sem=send_sem, recv_sem=recv_sem,
                                        device_id=dev, device_id_type=MESH)


def _gather_rider(bufs, views):
    n = len(bufs)

    def start(rin, rout, sems):
        send, recv = sems[0], sems[1]
        x, y, c, chips = _place()
        for j, chip in enumerate(chips):
            for b in range(n):
                mine = views[b](rout[b], 2 * x + y, c)
                _rcopy(mine, mine, send.at[j * n + b], recv.at[j * n + b], (chip[0], chip[1], c)).start()

    def finish(rin, rout, sems):
        send, recv, fsend, frecv = sems
        x, y, c, chips = _place()
        passed = []
        for j, chip in enumerate(chips):
            for b in range(n):
                landed = views[b](rout[b], 2 * chip[0] + chip[1], c)
                _rcopy(landed, landed, send.at[j * n + b], recv.at[j * n + b], (x, y, c)).wait_recv()
                fw = _rcopy(landed, landed, fsend.at[j * n + b], frecv.at[j * n + b], (x, y, 1 - c))
                fw.start()
                passed.append(fw)
        for j, chip in enumerate(chips):
            for b in range(n):
                other = views[b](rout[b], 2 * chip[0] + chip[1], 1 - c)
                _rcopy(other, other, fsend.at[j * n + b], frecv.at[j * n + b], (x, y, c)).wait_recv()
        for j, chip in enumerate(chips):
            for b in range(n):
                mine = views[b](rout[b], 2 * x + y, c)
                _rcopy(mine, mine, send.at[j * n + b], recv.at[j * n + b], (x, y, c)).wait_send()
        for fw in passed:
            fw.wait_send()

    return _Rider(list(bufs), [jax.ShapeDtypeStruct(a.shape, a.dtype) for a in bufs], {b: b for b in range(n)},
                  [pltpu.SemaphoreType.DMA((3 * n,))] * 4, start, finish)


def _small_gather_rider(cw):
    def descs(rin, rout, sems, x, y, c, chips):
        return [_rcopy(rin[0], rout[0].at[2 * x + y], sems[1].at[j], sems[2].at[j], (chip[0], chip[1], c))
                for j, chip in enumerate(chips)]

    def start(rin, rout, sems):
        x, y, c, chips = _place()
        pltpu.make_async_copy(rin[0], rout[0].at[2 * x + y], sems[0].at[0]).start()
        for cp in descs(rin, rout, sems, x, y, c, chips):
            cp.start()

    def finish(rin, rout, sems):
        x, y, c, chips = _place()
        for j, chip in enumerate(chips):
            _rcopy(rin[0], rout[0].at[2 * chip[0] + chip[1]], sems[1].at[j], sems[2].at[j], (x, y, c)).wait_recv()
        for cp in descs(rin, rout, sems, x, y, c, chips):
            cp.wait_send()
        pltpu.make_async_copy(rin[0], rout[0].at[2 * x + y], sems[0].at[0]).wait()

    return _Rider([cw], [jax.ShapeDtypeStruct((NSH,) + cw.shape, cw.dtype)], {},
                  [pltpu.SemaphoreType.DMA((1,)), pltpu.SemaphoreType.DMA((3,)), pltpu.SemaphoreType.DMA((3,))],
                  start, finish)


def _to_chips_rider(cs):
    n = len(cs)

    def descs(rin, rout, sems):
        x, y, c, chips = _place()
        return [_rcopy(rin[i].at[2 * chip[0] + chip[1]], rout[i].at[j], sems[0].at[j * n + i], sems[1].at[j * n + i],
                       (chip[0], chip[1], c)) for j, chip in enumerate(chips) for i in range(n)]

    def start(rin, rout, sems):
        for cp in descs(rin, rout, sems):
            cp.start()

    def finish(rin, rout, sems):
        for cp in descs(rin, rout, sems):
            cp.wait()

    return _Rider(list(cs), [jax.ShapeDtypeStruct((3,) + a.shape[1:], a.dtype) for a in cs], {},
                  [pltpu.SemaphoreType.DMA((3 * n,))] * 2, start, finish)


def _run_riders(name, riders):
    n_in = [len(r.operands) for r in riders]
    n_out = [len(r.out_shapes) for r in riders]
    n_sem = [len(r.sems) for r in riders]

    def body(*refs):
        parts, at = [], 0
        for counts in (n_in, n_out, n_sem):
            group = []
            for k in counts:
                group.append(refs[at:at + k])
                at += k
            parts.append(group)
        for i, r in enumerate(riders):
            r.start(parts[0][i], parts[1][i], parts[2][i])
        for i, r in enumerate(riders):
            r.finish(parts[0][i], parts[1][i], parts[2][i])

    aliases = {}
    for i, r in enumerate(riders):
        for k, v in r.aliases.items():
            aliases[sum(n_in[:i]) + k] = sum(n_out[:i]) + v
    res = pl.pallas_call(
        body, name=name, in_specs=_any_specs(sum(n_in)), out_specs=_any_specs(sum(n_out)),
        out_shape=[s for r in riders for s in r.out_shapes], input_output_aliases=aliases,
        scratch_shapes=[s for r in riders for s in r.sems],
    )(*[a for r in riders for a in r.operands])
    out, at = [], 0
    for k in n_out:
        out.append(list(res[at:at + k]))
        at += k
    return out


def _to_sibling(name, ps):
    n = len(ps)

    def body(*refs):
        src, dst, send, recv = refs[:n], refs[n:2 * n], refs[2 * n], refs[2 * n + 1]
        x, y, c, _ = _place()
        cps = [pltpu.make_async_remote_copy(
            src_ref=src[i].at[:, :, pl.ds((1 - c) * HALF, HALF)], dst_ref=dst[i], send_sem=send.at[i],
            recv_sem=recv.at[i], device_id=(x, y, 1 - c), device_id_type=MESH) for i in range(n)]
        for cp in cps:
            cp.start()
        for cp in cps:
            cp.wait()

    outs = [jax.ShapeDtypeStruct(p.shape[:2] + (HALF,), p.dtype) for p in ps]
    return pl.pallas_call(
        body, name=name, in_specs=_any_specs(n), out_specs=_any_specs(n), out_shape=outs,
        scratch_shapes=[pltpu.SemaphoreType.DMA((n,)), pltpu.SemaphoreType.DMA((n,))],
    )(*ps)


def _swap_halves(gs):
    n = len(gs)

    def body(*refs):
        dst, send, recv = refs[n:2 * n], refs[2 * n], refs[2 * n + 1]
        x, y, c, _ = _place()
        cps = []
        for i in range(n):
            mine = dst[i].at[:, pl.ds(c * HALF, HALF)]
            cps.append(pltpu.make_async_remote_copy(
                src_ref=mine, dst_ref=mine, send_sem=send.at[i], recv_sem=recv.at[i],
                device_id=(x, y, 1 - c), device_id_type=MESH))
        for cp in cps:
            cp.start()
        for i in range(n):
            other = dst[i].at[:, pl.ds((1 - c) * HALF, HALF)]
            pltpu.make_async_remote_copy(
                src_ref=other, dst_ref=other, send_sem=send.at[i], recv_sem=recv.at[i],
                device_id=(x, y, c), device_id_type=MESH).wait_recv()
        for cp in cps:
            cp.wait_send()

    return pl.pallas_call(
        body, name="grads_swap_halves", in_specs=_any_specs(n), out_specs=_any_specs(n),
        out_shape=[jax.ShapeDtypeStruct(g.shape, g.dtype) for g in gs],
        input_output_aliases={i: i for i in range(n)},
        scratch_shapes=[pltpu.SemaphoreType.DMA((n,)), pltpu.SemaphoreType.DMA((n,))],
    )(*gs)


SMALL_ROWS = 16


def _allreduce_small(vec):
    def body(v_ref, o_ref, buf, send, recv):
        x, y, c, _ = _place()
        me = 4 * x + 2 * y + c
        buf[me] = v_ref[...]
        cps = []
        for k in range(1, 8):
            peer = (x ^ (k >> 2), y ^ ((k >> 1) & 1), c ^ (k & 1))
            cps.append(pltpu.make_async_remote_copy(
                src_ref=v_ref, dst_ref=buf.at[me], send_sem=send.at[k - 1], recv_sem=recv.at[k - 1],
                device_id=peer, device_id_type=MESH))
        for cp in cps:
            cp.start()
        for k in range(1, 8):
            pltpu.make_async_remote_copy(
                src_ref=v_ref, dst_ref=buf.at[me ^ k], send_sem=send.at[k - 1], recv_sem=recv.at[k - 1],
                device_id=(x, y, c), device_id_type=MESH).wait_recv()
        for cp in cps:
            cp.wait_send()
        t = buf[0]
        for d in range(1, 8):
            t = t + buf[d]
        o_ref[...] = t

    return pl.pallas_call(
        body, name="allreduce_small",
        in_specs=[pl.BlockSpec(memory_space=pltpu.VMEM)], out_specs=pl.BlockSpec(memory_space=pltpu.VMEM),
        out_shape=jax.ShapeDtypeStruct((SMALL_ROWS, D), F32),
        scratch_shapes=[pltpu.VMEM((8, SMALL_ROWS, D), F32), pltpu.SemaphoreType.DMA((7,)),
                        pltpu.SemaphoreType.DMA((7,))],
    )(vec)


def _col_half(ref, slot, hc):
    return ref.at[slot, :, pl.ds(hc * HALF, HALF)]


def _ffn_half(first, ref, slot, hc):
    return ref.at[slot, pl.ds(first, 3), :, pl.ds(hc * HALF, HALF)]


def _row_tile(rows):
    for t in range(512, 15, -16):
        if rows % t == 0:
            return t
    return rows


def _same_shape_runs(arrs):
    runs, a = [], 0
    for b in range(1, len(arrs) + 1):
        if b == len(arrs) or arrs[b].shape != arrs[a].shape:
            runs.append((a, b))
            a = b
    return runs


class _Comm:
    def __init__(self):
        x, y, c = lax.axis_index("x"), lax.axis_index("y"), lax.axis_index("c")
        self.c_idx = jnp.reshape(c, (1,)).astype(jnp.int32)
        self.place = jnp.stack([2 * x + y, c]).astype(jnp.int32)
        self.groups = {}

    def gather_mixer(self, wint, wout):
        return _gather_rider([wint, wout], [_col_half, _col_half])

    def gather_ffn(self, ffnw, first):
        return _gather_rider([ffnw], [functools.partial(_ffn_half, first)])

    def reduce_rider(self, tag, names, ps):
        rs = _to_sibling("grads_to_sibling_" + tag, ps)
        csums = []
        for a, b in _same_shape_runs(ps):
            csums += _pair_sum("pair_sum_%s%d" % (tag, a), self.c_idx, ps[a:b], rs[a:b], _row_tile(ps[a].shape[1]))
        self.groups[tag] = [names, csums, None]
        return _to_chips_rider(csums)

    def landed(self, tag, ts):
        self.groups[tag][2] = ts

    def finish(self):
        names, halves = [], []
        for tag, (group_names, csums, ts) in self.groups.items():
            names += group_names
            for a, b in _same_shape_runs(csums):
                halves += _chip_sum("chip_sum_%s%d" % (tag, a), self.place, csums[a:b], ts[a:b],
                                    _row_tile(csums[a].shape[1]))
        return dict(zip(names, _swap_halves(halves)))


ROPE_THETA = 10000.0
SMALL_1K = ("ffn1_pre_norm", "ffn1_post_norm", "mix_pre_norm", "ssm_norm", "mix_post_norm",
            "ffn2_pre_norm", "ffn2_post_norm")
SMALL_16 = ("dt_bias", "a_log", "d_skip")
OFF_CONVB = 7 * D
OFF_16 = OFF_CONVB + CONV_C
OFF_CONVW = OFF_16 + 48
OFF_LOSS = OFF_CONVW + CONV_K * CONV_C
SMALL_LEN = SMALL_ROWS * D


def _sds(shape, dtype):
    return jax.ShapeDtypeStruct(shape, dtype)


def _ffn_down(name, act, ffnw, wi):
    return _mm(name, [act, ffnw], NN, (S // TS,),
               [pl.BlockSpec((NSH, TS, FS), lambda i: (0, i, 0)),
                pl.BlockSpec((NSH, None, FS, D), lambda i: (0, wi + 2, 0, 0))],
               pl.BlockSpec((TS, D), lambda i: (i, 0)), _sds((S, D), F32))


def _ridden(res, rider):
    return res if rider is not None else (res, None)


def _ffn_bwd(tag, dh, n, gate, up, act, ffnw, wi, dact_rider=None, dn_rider_of=None):
    (dgate, dup), got1 = _ridden(_ffn_dact(tag + "_dact", dh, ffnw, wi, gate, up, dact_rider), dact_rider)
    aspec = pl.BlockSpec((None, S, FS), lambda s: (s, 0, 0))
    nspec = pl.BlockSpec((S, D), lambda s: (0, 0))
    wspec = pl.BlockSpec((None, FS, D), lambda s: (s, 0, 0))
    dws = [_mm(tag + nm, [a, b], TN, (NSH,), [aspec, nspec], wspec, _sds((NSH, FS, D), BF16))
           for nm, a, b in (("_dwg", dgate, n), ("_dwu", dup, n), ("_dwd", act, dh))]
    dn_rider = dn_rider_of(dws) if dn_rider_of is not None else None
    a2 = pl.BlockSpec((NSH, TS, FS), lambda i: (0, i, 0))
    dn, got2 = _ridden(_mm(
        tag + "_dn", [dgate, ffnw, dup, ffnw], NN, (S // TS,),
        [a2, pl.BlockSpec((NSH, None, FS, D), lambda i: (0, wi, 0, 0)),
         a2, pl.BlockSpec((NSH, None, FS, D), lambda i: (0, wi + 1, 0, 0))],
        pl.BlockSpec((TS, D), lambda i: (i, 0)), _sds((S, D), F32), dn_rider), dn_rider)
    return dn, dws, got1, got2


def _heads(t, n):
    return t.reshape(S, n, HD).transpose(1, 0, 2)


def _unheads(t):
    return t.transpose(1, 0, 2).reshape(S, t.shape[0] * HD)


def _heads_t(t, n):
    return t.reshape(S, n, HD).transpose(1, 2, 0)


def _blocks5(t):
    return t.reshape(NCH, 128, NKV, NQ_PER_KV, HD)


def _to_blocks_t(t):
    return _blocks5(t).transpose(2, 0, 4, 3, 1).reshape(NKV, NCH, HD, QROWS)


def _to_blocks(t):
    return _blocks5(t).transpose(2, 0, 3, 1, 4).reshape(NKV, NCH, QROWS, HD)


def _from_blocks_t(t):
    return t.reshape(NKV, NCH, HD, NQ_PER_KV, 128).transpose(1, 4, 0, 3, 2).reshape(S, D)


def _pad128(v):
    return jnp.pad(v, ((0, 0), (0, 128 - v.shape[1])))


def _local_step(x, positions, tgt, sp, ffnw, wint, wout, convw, comm=None):
    inv_freq = ROPE_THETA ** (-jnp.arange(0, HD, 2, dtype=F32) / HD)
    ang = positions.astype(F32)[:, None] * inv_freq
    ang = jnp.concatenate([ang, ang, ang, ang], axis=-1)
    cos, sin = jnp.cos(ang), jnp.sin(ang)
    dtb, alog = _pad128(sp["dt_bias"]), _pad128(sp["a_log"])
    dskip_l = jnp.repeat(sp["d_skip"], HD, axis=1)
    convb = sp["conv_b"]

    n1 = _prenorm("prenorm1", x, sp["ffn1_pre_norm"])
    rider = comm.gather_mixer(wint, wout) if comm else None
    (gate1, up1, act1), got = _ridden(_ffn_up("ffn1_up", n1, ffnw, 0, rider), rider)
    if comm:
        wint, wout = got
    wint_pad = jnp.pad(wint.reshape(WIN_COLS, D), ((0, WIN_PAD - WIN_COLS), (0, 0)))
    wout = wout.reshape(2 * D, D)
    h1 = _ffn_down("ffn1_down", act1, ffnw, 0)
    x1, n2 = _postres("postres1", x, h1, sp["ffn1_post_norm"], 0.5, sp["mix_pre_norm"])

    pw = WIN_PAD // 3
    proj = _mm("in_proj", [n2, wint_pad], NT, (S // TS, 3),
               [pl.BlockSpec((TS, D), lambda i, j: (i, 0)), pl.BlockSpec((pw, D), lambda i, j: (j, 0))],
               pl.BlockSpec((TS, pw), lambda i, j: (i, j)), _sds((S, WIN_PAD), F32))
    q_rot = _rope("rope_q", proj, 0, D, cos, sin, 1.0, HD ** -0.5)
    k_rot = _rope("rope_k", proj, D // KVW, KVW, cos, sin, 1.0, 1.0)
    v_bf = proj[:, D + KVW:D + 2 * KVW].astype(BF16)
    qt, kh, vh = _to_blocks_t(q_rot), _heads(k_rot, NKV), _heads(v_bf, NKV)
    kt, vt = _heads_t(k_rot, NKV), _heads_t(v_bf, NKV)
    bias = _bias_table()
    rider = comm.gather_ffn(ffnw, 3) if comm else None
    (ot, lse), got = _ridden(_attn_fwd(qt, kh, vt, bias, rider), rider)
    if comm:
        ffnw, = got
    attn = _from_blocks_t(ot).astype(BF16)
    xbc = _conv_fwd(proj, convw, convb)
    y, yn, hprev = _ssd_fwd(xbc, proj, dtb, alog, dskip_l, sp["ssm_norm"])
    mixed = jnp.concatenate([attn, yn], axis=1)
    h2 = _mm("out_proj", [mixed, wout], NN, (S // TS,),
             [pl.BlockSpec((TS, 2 * D), lambda i: (i, 0)), pl.BlockSpec((2 * D, D), lambda i: (0, 0))],
             pl.BlockSpec((TS, D), lambda i: (i, 0)), _sds((S, D), F32))
    x2, n3 = _postres("postres2", x1, h2, sp["mix_post_norm"], 1.0, sp["ffn2_pre_norm"])

    gate2, up2, act2 = _ffn_up("ffn2_up", n3, ffnw, 3)
    h3 = _ffn_down("ffn2_down", act2, ffnw, 3)
    dy, dh3, dp3, loss = _final(x2, h3, sp["ffn2_post_norm"], tgt, 0.5)

    dn3, dws2, _, _ = _ffn_bwd("ffn2", dh3, n3, gate2, up2, act2, ffnw, 3)
    dx2, dh2, dg3, dp2 = _mid_bwd("mid_bwd2", dy, dn3, x2, sp["ffn2_pre_norm"], h2, sp["mix_post_norm"], 1.0)

    dmixed = _mm("out_proj_dx", [dh2, wout], NT, (S // TS,),
                 [pl.BlockSpec((TS, D), lambda i: (i, 0)), pl.BlockSpec((2 * D, D), lambda i: (0, 0))],
                 pl.BlockSpec((TS, 2 * D), lambda i: (i, 0)), _sds((S, 2 * D), F32))
    dwout = _mm("out_proj_dw", [mixed, dh2], TN, (2,),
                [pl.BlockSpec((S, D), lambda m: (0, m)), pl.BlockSpec((S, D), lambda m: (0, 0))],
                pl.BlockSpec((D, D), lambda m: (m, 0)), _sds((2 * D, D), BF16))
    dwout = dwout.reshape(NSH, 2 * D // NSH, D)
    rider = comm.reduce_rider("a", BIG[3:6] + ("w_out",), dws2 + [dwout]) if comm else None
    (dxbc, dz, ddt, dssm, dsc), got = _ridden(
        _ssd_bwd(dmixed, y, xbc, proj, hprev, dtb, alog, dskip_l, sp["ssm_norm"], rider), rider)
    if comm:
        comm.landed("a", got)
    du, dcw8, dcb = _conv_bwd(dxbc, proj, convw, convb)
    do_bf = dmixed[:, :D].astype(BF16)
    dot_ = _to_blocks_t(do_bf)
    dqt, dkh, dvh = _attn_bwd(qt, _to_blocks(q_rot), kh, kt, vh, dot_, _to_blocks(do_bf), lse,
                              _attn_delta(ot, dot_), bias)
    dq = _rope("rope_dq", _from_blocks_t(dqt), 0, D, cos, sin, -1.0, HD ** -0.5)
    dk = _rope("rope_dk", _unheads(dkh), 0, KVW, cos, sin, -1.0, 1.0)
    dproj = jnp.concatenate([dq, dk, _unheads(dvh).astype(BF16), du, dz, ddt], axis=1)
    dn2 = _mm("in_proj_dx", [dproj, wint_pad], NN, (S // TS,),
              [pl.BlockSpec((TS, WIN_PAD), lambda i: (i, 0)), pl.BlockSpec((WIN_PAD, D), lambda i: (0, 0))],
              pl.BlockSpec((TS, D), lambda i: (i, 0)), _sds((S, D), F32))
    dwint = _mm("in_proj_dw", [dproj, n2], TN, (3,),
                [pl.BlockSpec((S, pw), lambda j: (0, j)), pl.BlockSpec((S, D), lambda j: (0, 0))],
                pl.BlockSpec((pw, D), lambda j: (j, 0)), _sds((WIN_PAD, D), BF16))
    dx1, dh1, dg2, dp1 = _mid_bwd("mid_bwd1", dx2, dn2, x1, sp["mix_pre_norm"], h1, sp["ffn1_post_norm"], 0.5)

    dwint = dwint[:WIN_COLS].reshape(NSH, WIN_SH, D)
    dn1, dws1, got1, got2 = _ffn_bwd(
        "ffn1", dh1, n1, gate1, up1, act1, ffnw, 0,
        comm.reduce_rider("b", ("w_in",), [dwint]) if comm else None,
        (lambda dws: comm.reduce_rider("c", BIG[0:3], dws)) if comm else None)
    grad_x, dg1 = _first_bwd(dx1, dn1, x, sp["ffn1_pre_norm"])

    small = jnp.concatenate([
        dg1[0], dp1[0], dg2[0], dssm[0], dp2[0], dg3[0], dp3[0], dcb[0],
        dsc[0, :16], dsc[1, :16], dsc[2, :16], dcw8[:CONV_K].reshape(-1), loss[0, :1]])
    small = jnp.pad(small, (0, SMALL_LEN - small.shape[0])).reshape(SMALL_ROWS, D)
    if comm is None:
        return grad_x, dws1 + dws2 + [dwint, dwout], small
    comm.landed("b", got1)
    comm.landed("c", got2)
    return grad_x, comm.finish(), small


WEIGHTS = ("ffn1_pre_norm", "ffn1_w_gate", "ffn1_w_up", "ffn1_w_down", "ffn1_post_norm", "mix_pre_norm", "w_in",
           "conv_w", "conv_b", "dt_bias", "a_log", "d_skip", "ssm_norm", "w_out", "mix_post_norm", "ffn2_pre_norm",
           "ffn2_w_gate", "ffn2_w_up", "ffn2_w_down", "ffn2_post_norm")
BIG = ("ffn1_w_gate", "ffn1_w_up", "ffn1_w_down", "ffn2_w_gate", "ffn2_w_up", "ffn2_w_down", "w_in", "w_out")
TRANSPOSED = ("ffn1_w_gate", "ffn1_w_up", "ffn2_w_gate", "ffn2_w_up", "w_in")
SMALL_ORDER = SMALL_1K + ("conv_b",) + SMALL_16
CONVW_SH = CONV_C // NSH


def _shard2d(t, name):
    return t[0].T if name in TRANSPOSED else t[0]


def _unshard2d(t, name):
    return (t.T if name in TRANSPOSED else t)[None]


def _rows3d(t):
    return t.transpose(2, 0, 1)


def _pack_small(d, prefix, shard_of_convw):
    flat = jnp.concatenate([d[prefix + n][0] for n in SMALL_ORDER] + [shard_of_convw.reshape(-1)])
    return jnp.pad(flat, (0, SMALL_LEN - flat.shape[0])).reshape(SMALL_ROWS, D)


def _unpack_small(block, like):
    flat = block.reshape(-1)
    out, off = {}, 0
    for n in SMALL_ORDER:
        size = like[n].shape[1]
        out[n] = flat[off:off + size].reshape(1, size)
        off += size
    out["conv_w"] = flat[off:off + CONV_K * CONVW_SH].reshape(1, CONV_K, CONVW_SH)
    return out


def kernel(x, positions, ffn1_pre_norm, ffn1_w_gate, ffn1_w_up, ffn1_w_down, ffn1_post_norm, mix_pre_norm, w_in, conv_w, conv_b, dt_bias, a_log, d_skip, ssm_norm, w_out, mix_post_norm, ffn2_pre_norm, ffn2_w_gate, ffn2_w_up, ffn2_w_down, ffn2_post_norm, loss_target, m_ffn1_pre_norm, m_ffn1_w_gate, m_ffn1_w_up, m_ffn1_w_down, m_ffn1_post_norm, m_mix_pre_norm, m_w_in, m_conv_w, m_conv_b, m_dt_bias, m_a_log, m_d_skip, m_ssm_norm, m_w_out, m_mix_post_norm, m_ffn2_pre_norm, m_ffn2_w_gate, m_ffn2_w_up, m_ffn2_w_down, m_ffn2_post_norm, v_ffn1_pre_norm, v_ffn1_w_gate, v_ffn1_w_up, v_ffn1_w_down, v_ffn1_post_norm, v_mix_pre_norm, v_w_in, v_conv_w, v_conv_b, v_dt_bias, v_a_log, v_d_skip, v_ssm_norm, v_w_out, v_mix_post_norm, v_ffn2_pre_norm, v_ffn2_w_gate, v_ffn2_w_up, v_ffn2_w_down, v_ffn2_post_norm):
    given = dict(locals())
    xi, yi = lax.axis_index("x"), lax.axis_index("y")

    shard = jnp.reshape(2 * xi + yi, (1,)).astype(jnp.int32)
    big = {p + n: _shard2d(given[p + n], n) for n in BIG for p in ("", "m_", "v_")}
    ffnsh = _cast_stack("cast_ffn", shard, [big[n] for n in BIG[:6]], 176, D)
    winsh = _cast_stack("cast_w_in", shard, [big["w_in"]], WIN_SH, 256).reshape(NSH, WIN_SH, D)
    woutsh = _cast_stack("cast_w_out", shard, [big["w_out"]], 256, D).reshape(NSH, 2 * D // NSH, D)
    comm = _Comm()
    (ffnw,), (cwf,) = _run_riders("gather_ffn1", [comm.gather_ffn(ffnsh, 0), _small_gather_rider(conv_w[0])])
    convw = cwf.transpose(1, 0, 2).reshape(CONV_K, CONV_C)

    sp = {n: given[n] for n in SMALL_ORDER}
    grad_x, big_grads, small = _local_step(x[0], positions[0], loss_target[0], sp, ffnw, winsh, woutsh, convw, comm)

    tot = _allreduce_small(small).reshape(-1)
    loss = tot[OFF_LOSS]
    small_grads, off = {}, 0
    for n in SMALL_ORDER:
        size = given[n].shape[1]
        small_grads[n] = tot[off:off + size].reshape(1, size)
        off += size
    dconvw = tot[OFF_CONVW:OFF_CONVW + CONV_K * CONV_C].reshape(CONV_K, NSH, CONVW_SH)
    dconvw = lax.dynamic_index_in_dim(dconvw, 2 * xi + yi, axis=1, keepdims=False)
    small_grads["conv_w"] = dconvw.reshape(1, CONV_K, CONVW_SH)

    upd = {}
    for names, tr in ((BIG[0:3], 176), (BIG[3:6], 176), (BIG[7:8], 256)):
        res = _adamw("adamw_" + names[0], [big[n] for n in names], [big_grads[n] for n in names],
                     [big["m_" + n] for n in names], [big["v_" + n] for n in names], tr, D)
        for n, r in zip(names, res):
            upd[n] = tuple(_unshard2d(t, n) for t in r)
    g_win = big_grads["w_in"].reshape(WIN_SH, 1, D)
    res, = _adamw("adamw_w_in", [_rows3d(w_in)], [g_win], [_rows3d(m_w_in)], [_rows3d(v_w_in)], WIN_SH // 4, D)
    upd["w_in"] = tuple(t.transpose(1, 2, 0) for t in res)
    (dl, m2, v2), = _adamw(
        "adamw_small", [_pack_small(given, "", conv_w[0])], [_pack_small(small_grads, "", dconvw)],
        [_pack_small(given, "m_", m_conv_w[0])], [_pack_small(given, "v_", v_conv_w[0])], SMALL_ROWS, D)
    dl, m2, v2 = (_unpack_small(t, given) for t in (dl, m2, v2))
    for n in SMALL_ORDER + ("conv_w",):
        upd[n] = (dl[n], m2[n], v2[n])

    grads = dict(small_grads)
    grads.update({n: _unshard2d(g, n) for n, g in big_grads.items() if n != "w_in"})
    grads["w_in"] = g_win.transpose(1, 2, 0)
    return (loss, grad_x[None], *[grads[n] for n in WEIGHTS], *[upd[n][0] for n in WEIGHTS],
            *[upd[n][1] for n in WEIGHTS], *[upd[n][2] for n in WEIGHTS])
```

```python
import functools
import typing

import jax
import jax.numpy as jnp
from jax import lax
from jax.experimental import pallas as pl
from jax.experimental.pallas import tpu as pltpu

F32 = jnp.float32
BF16 = jnp.bfloat16

S = 2048
D = 1024
FF = 2816
NSH = 4
FS = FF // NSH
HALF = D // 2
HD = 64
NKV = 4
NQ_PER_KV = 4
KVW = NKV * HD
CONV_C = 1536
CONV_K = 4
SSM_W = 1024
NST = 128
NCH = S // 128
WIN_COLS = 4112
WIN_SH = WIN_COLS // NSH
WIN_PAD = 4224
COL_DT = 4096
EPS = 1e-6
NEG = -1e30

ADAM_LR = 0.001
ADAM_B1 = 0.9
ADAM_B2 = 0.999
ADAM_EPS = 1e-08
ADAM_WD = 0.01
ADAM_STEP = 10

VMEM_LIMIT = 56 * 1024 * 1024
TS = 512
TR = 256

NN = (((1,), (0,)), ((), ()))
NT = (((1,), (1,)), ((), ()))
TN = (((0,), (0,)), ((), ()))
MESH = pl.DeviceIdType.MESH


def _cparams(*sem):
    return pltpu.CompilerParams(dimension_semantics=sem, vmem_limit_bytes=VMEM_LIMIT)


def _dot(a, b, dims):
    return lax.dot_general(a.astype(BF16), b.astype(BF16), dims, preferred_element_type=F32)


def _dot_exact(a, b):
    return lax.dot_general(a, b, NN, precision=lax.Precision.HIGHEST, preferred_element_type=F32)


def _sigmoid(v):
    return 1.0 / (1.0 + jnp.exp(-v))


class _Rider(typing.NamedTuple):
    operands: list
    out_shapes: list
    aliases: dict
    sems: list
    start: typing.Callable
    finish: typing.Callable


def _call(body, name, grid, in_specs, out_specs, out_shape, operands, scratch=(), sem=(), rider=None):
    multi = isinstance(out_shape, (list, tuple))
    if rider is None:
        return pl.pallas_call(
            body, name=name, grid=grid, in_specs=in_specs, out_specs=out_specs, out_shape=out_shape,
            scratch_shapes=list(scratch), compiler_params=_cparams(*sem))(*operands)
    outs = list(out_shape) if multi else [out_shape]
    ospecs = list(out_specs) if multi else [out_specs]
    n_in, n_out, n_scr = len(operands), len(outs), len(scratch)
    ri, ro = len(rider.operands), len(rider.out_shapes)

    def wrapped(*refs):
        o0 = n_in + ri
        s0 = o0 + n_out + ro
        rin, rout, rsem = refs[n_in:o0], refs[o0 + n_out:s0], refs[s0 + n_scr:]
        ids = [pl.program_id(a) for a in range(len(grid))]
        first = functools.reduce(jnp.logical_and, [i == 0 for i in ids])
        last = functools.reduce(jnp.logical_and, [i == g - 1 for i, g in zip(ids, grid)])

        @pl.when(first)
        def _():
            rider.start(rin, rout, rsem)

        body(*refs[:n_in], *refs[o0:o0 + n_out], *refs[s0:s0 + n_scr])

        @pl.when(last)
        def _():
            rider.finish(rin, rout, rsem)

    hbm = pl.BlockSpec(memory_space=pl.ANY)
    res = pl.pallas_call(
        wrapped, name=name, grid=grid, in_specs=list(in_specs) + [hbm] * ri, out_specs=ospecs + [hbm] * ro,
        out_shape=outs + list(rider.out_shapes), scratch_shapes=list(scratch) + list(rider.sems),
        input_output_aliases={n_in + k: n_out + v for k, v in rider.aliases.items()},
        compiler_params=_cparams(*(("arbitrary",) * len(grid))))(*operands, *rider.operands)
    main = list(res[:n_out])
    return (main if multi else main[0]), list(res[n_out:])


def _mm(name, operands, dims, grid, in_specs, o_spec, out_shape, rider=None):
    npairs = len(operands) // 2

    def body(*refs):
        t = None
        for i in range(npairs):
            a, b = refs[2 * i], refs[2 * i + 1]
            parts = [(a[s], b[s]) for s in range(a.shape[0])] if len(a.shape) == 3 else [(a[...], b[...])]
            for pa, pb in parts:
                d = _dot(pa, pb, dims)
                t = d if t is None else t + d
        refs[2 * npairs][...] = t.astype(refs[2 * npairs].dtype)

    return _call(body, name, grid, in_specs, o_spec, out_shape, operands, (), ("parallel",) * len(grid), rider)


def _ffn_up(name, n, ffnw, wi, rider=None):
    def body(n_ref, wg_ref, wu_ref, g_ref, u_ref, a_ref):
        nb = n_ref[...]
        g = _dot(nb, wg_ref[...], NT)
        u = _dot(nb, wu_ref[...], NT)
        g_ref[...] = g.astype(BF16)
        u_ref[...] = u.astype(BF16)
        a_ref[...] = (g * _sigmoid(g) * u).astype(BF16)

    out = jax.ShapeDtypeStruct((NSH, S, FS), BF16)
    ospec = pl.BlockSpec((None, TS, FS), lambda s, i: (s, i, 0))
    return _call(
        body, name, (NSH, S // TS),
        [pl.BlockSpec((TS, D), lambda s, i: (i, 0)),
         pl.BlockSpec((None, None, FS, D), lambda s, i: (s, wi, 0, 0)),
         pl.BlockSpec((None, None, FS, D), lambda s, i: (s, wi + 1, 0, 0))],
        [ospec, ospec, ospec], [out, out, out], (n, ffnw, ffnw), sem=("parallel", "parallel"), rider=rider)


def _ffn_dact(name, dh, ffnw, wi, gate, up, rider=None):
    def body(dh_ref, wd_ref, g_ref, u_ref, dg_ref, du_ref):
        da = _dot(dh_ref[...], wd_ref[...], NT)
        g = g_ref[...].astype(F32)
        u = u_ref[...].astype(F32)
        sg = _sigmoid(g)
        dg_ref[...] = (da * u * (sg * (1.0 + g * (1.0 - sg)))).astype(BF16)
        du_ref[...] = (da * (g * sg)).astype(BF16)

    out = jax.ShapeDtypeStruct((NSH, S, FS), BF16)
    aspec = pl.BlockSpec((None, TS, FS), lambda s, i: (s, i, 0))
    return _call(
        body, name, (NSH, S // TS),
        [pl.BlockSpec((TS, D), lambda s, i: (i, 0)),
         pl.BlockSpec((None, None, FS, D), lambda s, i: (s, wi + 2, 0, 0)), aspec, aspec],
        [aspec, aspec], [out, out], (dh, ffnw, gate, up), sem=("parallel", "parallel"), rider=rider)


def _rstd(v):
    return lax.rsqrt(jnp.mean(v * v, axis=-1, keepdims=True) + EPS)


def _row_spec():
    return pl.BlockSpec((TR, D), lambda i: (i, 0))


def _vec_spec():
    return pl.BlockSpec((1, D), lambda i: (0, 0))


def _acc_rows(ref, v):
    @pl.when(pl.program_id(0) == 0)
    def _():
        ref[...] = jnp.zeros_like(ref)
    ref[...] += jnp.sum(v, axis=0, keepdims=True)


def _prenorm(name, x, g):
    def body(x_ref, g_ref, n_ref):
        xv = x_ref[...]
        n_ref[...] = (xv * _rstd(xv) * g_ref[...]).astype(BF16)

    return pl.pallas_call(
        body, name=name, grid=(S // TR,), in_specs=[_row_spec(), _vec_spec()], out_specs=_row_spec(),
        out_shape=jax.ShapeDtypeStruct((S, D), BF16), compiler_params=_cparams("parallel"),
    )(x, g)


def _postres(name, x, h, p, alpha, gnext):
    def body(x_ref, h_ref, p_ref, g_ref, xo_ref, n_ref):
        hv = h_ref[...]
        xo = x_ref[...] + alpha * (hv * _rstd(hv) * p_ref[...])
        xo_ref[...] = xo
        n_ref[...] = (xo * _rstd(xo) * g_ref[...]).astype(BF16)

    return pl.pallas_call(
        body, name=name, grid=(S // TR,),
        in_specs=[_row_spec(), _row_spec(), _vec_spec(), _vec_spec()],
        out_specs=[_row_spec(), _row_spec()],
        out_shape=[jax.ShapeDtypeStruct((S, D), F32), jax.ShapeDtypeStruct((S, D), BF16)],
        compiler_params=_cparams("parallel"),
    )(x, h, p, gnext)


def _final(x, h, p, tgt, alpha):
    def body(x_ref, h_ref, p_ref, t_ref, dy_ref, dh_ref, dp_ref, loss_ref):
        hv = h_ref[...]
        r = _rstd(hv)
        hn = hv * r
        pv = p_ref[...]
        e = x_ref[...] + alpha * (hn * pv) - t_ref[...]
        dy = e * (1.0 / D)
        dy_ref[...] = dy
        du = alpha * dy * pv
        dh_ref[...] = (r * (du - hn * jnp.mean(du * hn, axis=-1, keepdims=True))).astype(BF16)
        _acc_rows(dp_ref, alpha * dy * hn)
        part = 0.5 * jnp.sum(jnp.mean(e * e, axis=-1, keepdims=True), axis=0, keepdims=True)
        _acc_rows(loss_ref, jnp.broadcast_to(part, (1, 128)))

    return pl.pallas_call(
        body, name="loss_head", grid=(S // TR,),
        in_specs=[_row_spec(), _row_spec(), _vec_spec(), _row_spec()],
        out_specs=[_row_spec(), _row_spec(), _vec_spec(), pl.BlockSpec((1, 128), lambda i: (0, 0))],
        out_shape=[jax.ShapeDtypeStruct((S, D), F32), jax.ShapeDtypeStruct((S, D), BF16),
                   jax.ShapeDtypeStruct((1, D), F32), jax.ShapeDtypeStruct((1, 128), F32)],
        compiler_params=_cparams("arbitrary"),
    )(x, h, p, tgt)


def _mid_bwd(name, dres, dn, x, g, h, p, alpha):
    def body(dr_ref, dn_ref, x_ref, g_ref, h_ref, p_ref, dx_ref, dh_ref, dg_ref, dp_ref):
        xv = x_ref[...]
        xn = xv * _rstd(xv)
        dnv = dn_ref[...]
        dng = dnv * g_ref[...]
        dx = dr_ref[...] + _rstd(xv) * (dng - xn * jnp.mean(dng * xn, axis=-1, keepdims=True))
        dx_ref[...] = dx
        _acc_rows(dg_ref, dnv * xn)
        hv = h_ref[...]
        r = _rstd(hv)
        hn = hv * r
        du = alpha * dx * p_ref[...]
        dh_ref[...] = (r * (du - hn * jnp.mean(du * hn, axis=-1, keepdims=True))).astype(BF16)
        _acc_rows(dp_ref, alpha * dx * hn)

    return pl.pallas_call(
        body, name=name, grid=(S // TR,),
        in_specs=[_row_spec(), _row_spec(), _row_spec(), _vec_spec(), _row_spec(), _vec_spec()],
        out_specs=[_row_spec(), _row_spec(), _vec_spec(), _vec_spec()],
        out_shape=[jax.ShapeDtypeStruct((S, D), F32), jax.ShapeDtypeStruct((S, D), BF16),
                   jax.ShapeDtypeStruct((1, D), F32), jax.ShapeDtypeStruct((1, D), F32)],
        compiler_params=_cparams("arbitrary"),
    )(dres, dn, x, g, h, p)


def _first_bwd(dres, dn, x, g):
    def body(dr_ref, dn_ref, x_ref, g_ref, dx_ref, dg_ref):
        xv = x_ref[...]
        r = _rstd(xv)
        xn = xv * r
        dnv = dn_ref[...]
        dng = dnv * g_ref[...]
        dx_ref[...] = dr_ref[...] + r * (dng - xn * jnp.mean(dng * xn, axis=-1, keepdims=True))
        _acc_rows(dg_ref, dnv * xn)

    return pl.pallas_call(
        body, name="first_bwd", grid=(S // TR,),
        in_specs=[_row_spec(), _row_spec(), _row_spec(), _vec_spec()],
        out_specs=[_row_spec(), _vec_spec()],
        out_shape=[jax.ShapeDtypeStruct((S, D), F32), jax.ShapeDtypeStruct((1, D), F32)],
        compiler_params=_cparams("arbitrary"),
    )(dres, dn, x, g)


def _rope(name, src, col_block, width, cos, sin, sign, scale):
    def body(t_ref, c_ref, s_ref, o_ref):
        t = t_ref[...].astype(F32)
        c = jnp.tile(c_ref[...], (1, width // 128))
        sn = jnp.tile(s_ref[...], (1, width // 128))
        lane = lax.broadcasted_iota(jnp.int32, t.shape, 1) & (HD - 1)
        rot = jnp.where(lane < HD // 2, -pltpu.roll(t, width - HD // 2, 1), pltpu.roll(t, HD // 2, 1))
        o_ref[...] = ((t * c + sign * (rot * sn)) * scale).astype(BF16)

    return pl.pallas_call(
        body, name=name, grid=(S // TR,),
        in_specs=[pl.BlockSpec((TR, width), lambda i: (i, col_block)),
                  pl.BlockSpec((TR, 128), lambda i: (i, 0)), pl.BlockSpec((TR, 128), lambda i: (i, 0))],
        out_specs=pl.BlockSpec((TR, width), lambda i: (i, 0)),
        out_shape=jax.ShapeDtypeStruct((S, width), BF16), compiler_params=_cparams("parallel"),
    )(src, cos, sin)


QROWS = NQ_PER_KV * 128


NBIAS = NCH + 1
KV_PER_STEP = 2


def _bias_table():
    db = lax.broadcasted_iota(jnp.int32, (NBIAS, 128, QROWS), 0) - 1
    ki = lax.broadcasted_iota(jnp.int32, (NBIAS, 128, QROWS), 1)
    qi = lax.broadcasted_iota(jnp.int32, (NBIAS, 128, QROWS), 2) & 127
    d = db * 128 + qi - ki
    cnt = ((d <= 128).astype(F32) + (((d & 3) == 0) & (d <= 512)).astype(F32) + ((d & 15) == 0).astype(F32))
    return jnp.where((d >= 0) & (cnt > 0.0), jnp.log(jnp.maximum(cnt, 1.0)), NEG)


def _qt_spec():
    return pl.BlockSpec((None, None, HD, QROWS), lambda j, i: (j, i, 0, 0))


def _stat_spec():
    return pl.BlockSpec((None, None, 1, QROWS), lambda j, i: (j, i, 0, 0))


def _attn_fwd(qt, kh, vt, bias, rider=None):
    def body(q_ref, k_ref, v_ref, b_ref, o_ref, lse_ref):
        qb = pl.program_id(1)

        def step(i, carry):
            off = pl.multiple_of(i * 256, 256)
            bias2 = jnp.concatenate([b_ref[qb - 2 * i + 1], b_ref[qb - 2 * i]], axis=0)
            out = []
            for h in range(KV_PER_STEP):
                m, l, acc = carry[3 * h:3 * h + 3]
                s = _dot(k_ref[h, pl.ds(off, 256), :], q_ref[h], NN) + bias2
                m_new = jnp.maximum(m, jnp.max(s, axis=0, keepdims=True))
                p = jnp.exp(s - m_new)
                a = jnp.exp(m - m_new)
                out += [m_new, a * l + jnp.sum(p, axis=0, keepdims=True),
                        a * acc + _dot(v_ref[h, :, pl.ds(off, 256)], p, NN)]
            return tuple(out)

        init = (jnp.full((1, QROWS), NEG, F32), jnp.zeros((1, QROWS), F32), jnp.zeros((HD, QROWS), F32))
        res = lax.fori_loop(0, qb // 2 + 1, step, init * KV_PER_STEP)
        for h in range(KV_PER_STEP):
            m, l, acc = res[3 * h:3 * h + 3]
            o_ref[h] = acc / l
            lse_ref[h] = m + jnp.log(l)

    kvs = KV_PER_STEP
    qspec = pl.BlockSpec((kvs, None, HD, QROWS), lambda j, i: (j, i, 0, 0))
    return _call(
        body, "attn_fwd", (NKV // kvs, NCH),
        [qspec, pl.BlockSpec((kvs, S, HD), lambda j, i: (j, 0, 0)),
         pl.BlockSpec((kvs, HD, S), lambda j, i: (j, 0, 0)),
         pl.BlockSpec((NBIAS, 128, QROWS), lambda j, i: (0, 0, 0))],
        [qspec, pl.BlockSpec((kvs, None, 1, QROWS), lambda j, i: (j, i, 0, 0))],
        [jax.ShapeDtypeStruct((NKV, NCH, HD, QROWS), F32), jax.ShapeDtypeStruct((NKV, NCH, 1, QROWS), F32)],
        (qt, kh, vt, bias), sem=("parallel", "parallel"), rider=rider)


def _attn_delta(ot, dot_):
    def body(o_ref, do_ref, dl_ref):
        dl_ref[...] = jnp.sum(o_ref[...] * do_ref[...].astype(F32), axis=1, keepdims=True)

    spec = pl.BlockSpec((None, NCH, HD, QROWS), lambda j: (j, 0, 0, 0))
    return pl.pallas_call(
        body, name="attn_delta", grid=(NKV,), in_specs=[spec, spec],
        out_specs=pl.BlockSpec((None, NCH, 1, QROWS), lambda j: (j, 0, 0, 0)),
        out_shape=jax.ShapeDtypeStruct((NKV, NCH, 1, QROWS), F32), compiler_params=_cparams("parallel"),
    )(ot, dot_)


def _attn_bwd(qt, q2, kh, kt, vh, dot_, do2, lse, delta, bias):
    def body(qt_ref, q2_ref, k_ref, kt_ref, v_ref, dot_ref, do2_ref, lse_ref, dl_ref, b_ref, dq_ref, dk_ref, dv_ref):
        kb = pl.program_id(1)

        @pl.when(kb == 0)
        def _():
            dq_ref[...] = jnp.zeros_like(dq_ref)

        def step(j, carry):
            out = list(carry)
            for h in range(KV_PER_STEP):
                k, kt_, v = k_ref[h], kt_ref[h], v_ref[h]
                for qb in (2 * j, 2 * j + 1):
                    st = _dot(k, qt_ref[h, qb], NN) + b_ref[qb - kb + 1]
                    pt = jnp.exp(st - lse_ref[h, qb])
                    dst = pt * (_dot(v, dot_ref[h, qb], NN) - dl_ref[h, qb])
                    dq_ref[h, qb] += _dot(kt_, dst, NN)
                    out[2 * h] = out[2 * h] + _dot(dst, q2_ref[h, qb], NN)
                    out[2 * h + 1] = out[2 * h + 1] + _dot(pt, do2_ref[h, qb], NN)
            return tuple(out)

        res = lax.fori_loop(kb // 2, NCH // 2, step, (jnp.zeros((128, HD), F32),) * (2 * KV_PER_STEP))
        for h in range(KV_PER_STEP):
            dk_ref[h] = res[2 * h]
            dv_ref[h] = res[2 * h + 1]

    kvs = KV_PER_STEP
    tspec = pl.BlockSpec((kvs, NCH, HD, QROWS), lambda j, i: (j, 0, 0, 0))
    rspec = pl.BlockSpec((kvs, NCH, QROWS, HD), lambda j, i: (j, 0, 0, 0))
    kspec = pl.BlockSpec((kvs, 128, HD), lambda j, i: (j, i, 0))
    sspec = pl.BlockSpec((kvs, NCH, 1, QROWS), lambda j, i: (j, 0, 0, 0))
    return pl.pallas_call(
        body, name="attn_bwd", grid=(NKV // kvs, NCH),
        in_specs=[tspec, rspec, kspec, pl.BlockSpec((kvs, HD, 128), lambda j, i: (j, 0, i)), kspec, tspec, rspec,
                  sspec, sspec, pl.BlockSpec((NBIAS, 128, QROWS), lambda j, i: (0, 0, 0))],
        out_specs=[tspec, kspec, kspec],
        out_shape=[jax.ShapeDtypeStruct((NKV, NCH, HD, QROWS), F32),
                   jax.ShapeDtypeStruct((NKV, S, HD), F32), jax.ShapeDtypeStruct((NKV, S, HD), F32)],
        compiler_params=_cparams("parallel", "arbitrary"),
    )(qt, q2, kh, kt, vh, dot_, do2, lse, delta, bias)


CONV_BLK = 256
CONV_COL0 = 1536 // CONV_BLK


def _shift_down(u, j, row):
    return jnp.where(row >= j, pltpu.roll(u, j, 0), 0.0)


def _conv_pre(u, w_ref, b_ref, row):
    y = b_ref[...] + w_ref[CONV_K - 1:CONV_K, :] * u
    for j in range(1, CONV_K):
        y = y + w_ref[CONV_K - 1 - j:CONV_K - j, :] * _shift_down(u, j, row)
    return y


def _conv_fwd(proj, convw, convb):
    def body(u_ref, w_ref, b_ref, o_ref):
        u = u_ref[...]
        row = lax.broadcasted_iota(jnp.int32, u.shape, 0)
        y = _conv_pre(u, w_ref, b_ref, row)
        o_ref[...] = y * _sigmoid(y)

    return pl.pallas_call(
        body, name="conv_fwd", grid=(CONV_C // CONV_BLK,),
        in_specs=[pl.BlockSpec((S, CONV_BLK), lambda i: (0, CONV_COL0 + i)),
                  pl.BlockSpec((CONV_K, CONV_BLK), lambda i: (0, i)),
                  pl.BlockSpec((1, CONV_BLK), lambda i: (0, i))],
        out_specs=pl.BlockSpec((S, CONV_BLK), lambda i: (0, i)),
        out_shape=jax.ShapeDtypeStruct((S, CONV_C), F32), compiler_params=_cparams("parallel"),
    )(proj, convw, convb)


def _conv_bwd(dact, proj, convw, convb):
    def body(da_ref, u_ref, w_ref, b_ref, du_ref, dw_ref, db_ref):
        u = u_ref[...]
        row = lax.broadcasted_iota(jnp.int32, u.shape, 0)
        y = _conv_pre(u, w_ref, b_ref, row)
        sg = _sigmoid(y)
        dy = da_ref[...] * (sg * (1.0 + y * (1.0 - sg)))
        db_ref[...] = jnp.sum(dy, axis=0, keepdims=True)
        du = w_ref[CONV_K - 1:CONV_K, :] * dy
        r8 = lax.broadcasted_iota(jnp.int32, (8, CONV_BLK), 0)
        dw = jnp.where(r8 == CONV_K - 1, jnp.sum(dy * u, axis=0, keepdims=True), 0.0)
        for j in range(1, CONV_K):
            du = du + w_ref[CONV_K - 1 - j:CONV_K - j, :] * jnp.where(row < S - j, pltpu.roll(dy, S - j, 0), 0.0)
            dw = dw + jnp.where(r8 == CONV_K - 1 - j,
                                jnp.sum(dy * _shift_down(u, j, row), axis=0, keepdims=True), 0.0)
        du_ref[...] = du.astype(BF16)
        dw_ref[...] = dw

    return pl.pallas_call(
        body, name="conv_bwd", grid=(CONV_C // CONV_BLK,),
        in_specs=[pl.BlockSpec((S, CONV_BLK), lambda i: (0, i)),
                  pl.BlockSpec((S, CONV_BLK), lambda i: (0, CONV_COL0 + i)),
                  pl.BlockSpec((CONV_K, CONV_BLK), lambda i: (0, i)),
                  pl.BlockSpec((1, CONV_BLK), lambda i: (0, i))],
        out_specs=[pl.BlockSpec((S, CONV_BLK), lambda i: (0, i)), pl.BlockSpec((8, CONV_BLK), lambda i: (0, i)),
                   pl.BlockSpec((1, CONV_BLK), lambda i: (0, i))],
        out_shape=[jax.ShapeDtypeStruct((S, CONV_C), BF16), jax.ShapeDtypeStruct((8, CONV_C), F32),
                   jax.ShapeDtypeStruct((1, CONV_C), F32)],
        compiler_params=_cparams("parallel"),
    )(dact, proj, convw, convb)


NPAIR = 8


def _ssd_scalars(dtr_ref, dtb_ref, alog_ref):
    z = dtr_ref[...] + dtb_ref[...]
    dt = jnp.maximum(z, 0.0) + jnp.log(1.0 + jnp.exp(-jnp.abs(z)))
    a = -jnp.exp(alog_ref[...])
    r = lax.broadcasted_iota(jnp.int32, (128, 128), 0)
    c = lax.broadcasted_iota(jnp.int32, (128, 128), 1)
    tri = (r >= c).astype(F32)
    cs = _dot_exact(tri, dt * a)
    return z, dt, a, cs, r, c


def _pair_terms(cs, cst, dt, h1, h2, lo):
    c1, c2 = cs[:, h1:h1 + 1], cs[:, h2:h2 + 1]
    l1, l2 = cs[127:128, h1:h1 + 1], cs[127:128, h2:h2 + 1]
    e_l = jnp.where(lo, jnp.exp(c1), jnp.exp(c2))
    dte1, dte2 = jnp.exp(l1 - c1), jnp.exp(l2 - c2)
    dte_l = jnp.where(lo, dte1, dte2)
    dt_l = jnp.where(lo, dt[:, h1:h1 + 1], dt[:, h2:h2 + 1])
    return c1, c2, jnp.exp(l1), jnp.exp(l2), e_l, dte1, dte2, dte_l, dt_l


def _gate_norm(y, zv, w):
    yg = y * (zv * _sigmoid(zv))
    outs, rs = [], []
    for g in range(2):
        blk = yg[:, 512 * g:512 * (g + 1)]
        r = lax.rsqrt(jnp.mean(blk * blk, axis=-1, keepdims=True) + EPS)
        outs.append(blk * r)
        rs.append(r)
    return jnp.concatenate(outs, axis=1), rs, yg


def _ssd_fwd(xbc, proj, dtb, alog, dskip_l, ssmw):
    def body(x_ref, b_ref, c_ref, dtr_ref, z_ref, dtb_ref, alog_ref, dsk_ref, w_ref, y_ref, yn_ref, hp_ref, h_ref):
        @pl.when(pl.program_id(0) == 0)
        def _():
            h_ref[...] = jnp.zeros_like(h_ref)

        _, dt, _, cs, r, c = _ssd_scalars(dtr_ref, dtb_ref, alog_ref)
        cst = cs.T
        causal = r >= c
        lo = c < HD
        hp_ref[...] = h_ref[...]
        for g in range(2):
            bg = b_ref[:, 128 * g:128 * (g + 1)]
            cg = c_ref[:, 128 * g:128 * (g + 1)]
            cb = _dot(cg, bg, NT)
            for j in range(4):
                pj = 4 * g + j
                h1, h2 = 2 * pj, 2 * pj + 1
                sl = slice(128 * pj, 128 * (pj + 1))
                xp = x_ref[:, sl]
                c1, c2, cd1, cd2, e_l, _, _, dte_l, dt_l = _pair_terms(cs, cst, dt, h1, h2, lo)
                xdt = xp * dt_l
                m1 = cb * jnp.exp(jnp.where(causal, c1 - cst[h1:h1 + 1, :], NEG))
                m2 = cb * jnp.exp(jnp.where(causal, c2 - cst[h2:h2 + 1, :], NEG))
                yd = jnp.where(lo, _dot(m1, xdt, NN), _dot(m2, xdt, NN))
                hp = h_ref[pj]
                yo = _dot(cg, hp, NT) * e_l
                st = _dot(xdt * dte_l, bg, TN)
                h_ref[pj] = hp * jnp.where(r < HD, cd1, cd2) + st
                y_ref[:, sl] = yd + yo + dsk_ref[:, sl] * xp
        yn, _, _ = _gate_norm(y_ref[...], z_ref[...], w_ref[...])
        yn_ref[...] = (yn * w_ref[...]).astype(BF16)

    return pl.pallas_call(
        body, name="ssd_fwd", grid=(NCH,),
        in_specs=[pl.BlockSpec((128, SSM_W), lambda i: (i, 0)),
                  pl.BlockSpec((128, 256), lambda i: (i, 4)), pl.BlockSpec((128, 256), lambda i: (i, 5)),
                  pl.BlockSpec((128, 128), lambda i: (i, COL_DT // 128)),
                  pl.BlockSpec((128, SSM_W), lambda i: (i, 3)),
                  pl.BlockSpec((1, 128), lambda i: (0, 0)), pl.BlockSpec((1, 128), lambda i: (0, 0)),
                  pl.BlockSpec((1, SSM_W), lambda i: (0, 0)), pl.BlockSpec((1, SSM_W), lambda i: (0, 0))],
        out_specs=[pl.BlockSpec((128, SSM_W), lambda i: (i, 0)), pl.BlockSpec((128, SSM_W), lambda i: (i, 0)),
                   pl.BlockSpec((None, NPAIR, 128, 128), lambda i: (i, 0, 0, 0))],
        out_shape=[jax.ShapeDtypeStruct((S, SSM_W), F32), jax.ShapeDtypeStruct((S, SSM_W), BF16),
                   jax.ShapeDtypeStruct((NCH, NPAIR, 128, 128), F32)],
        scratch_shapes=[pltpu.VMEM((NPAIR, 128, 128), F32)],
        compiler_params=_cparams("arbitrary"),
    )(xbc, xbc, xbc, proj, proj, dtb, alog, dskip_l, ssmw)


def _ssd_bwd(dmixed, y, xbc, proj, hprev, dtb, alog, dskip_l, ssmw, rider=None):
    def body(dyn_ref, y_ref, x_ref, b_ref, c_ref, dtr_ref, z_ref, hp_ref, dtb_ref, alog_ref, dsk_ref, w_ref,
             dxbc_ref, dz_ref, ddt_ref, dw_ref, dsc_ref, g_ref):
        @pl.when(pl.program_id(0) == 0)
        def _():
            g_ref[...] = jnp.zeros_like(g_ref)
            dsc_ref[...] = jnp.zeros_like(dsc_ref)

        z, dt, a, cs, r, c = _ssd_scalars(dtr_ref, dtb_ref, alog_ref)
        cst = cs.T
        causal = r >= c
        lo = c < HD

        yv = y_ref[...]
        zv = z_ref[...]
        wv = w_ref[...]
        ygn, rs, yg = _gate_norm(yv, zv, wv)
        dyn = dyn_ref[...]
        _acc_rows(dw_ref, dyn * ygn)
        dynw = dyn * wv
        parts = []
        for g in range(2):
            sl = slice(512 * g, 512 * (g + 1))
            a_g, n_g = dynw[:, sl], ygn[:, sl]
            parts.append(rs[g] * (a_g - n_g * jnp.mean(a_g * n_g, axis=-1, keepdims=True)))
        dyg = jnp.concatenate(parts, axis=1)
        sz = _sigmoid(zv)
        dz_ref[...] = (dyg * yv * (sz * (1.0 + zv * (1.0 - sz)))).astype(BF16)
        dy_all = dyg * (zv * sz)

        dcs_cols = jnp.zeros((128, 128), F32)
        dcs_rows = jnp.zeros((128, 128), F32)
        ddt_x = jnp.zeros((128, 128), F32)
        dd_row = jnp.zeros((1, 128), F32)
        last = r == 127
        x_all, b_all, c_all, dsk_all = x_ref[...], b_ref[...], c_ref[...], dsk_ref[...]
        hp_all, g_all = hp_ref[...], g_ref[...]
        g_new, dx_parts, db_parts, dc_parts = [], [], [], []
        for g in range(2):
            bg = b_all[:, 128 * g:128 * (g + 1)]
            cg = c_all[:, 128 * g:128 * (g + 1)]
            cb = _dot(cg, bg, NT)
            dcb = jnp.zeros((128, 128), F32)
            db_acc = jnp.zeros((128, NST), F32)
            dc_acc = jnp.zeros((128, NST), F32)
            for j in range(4):
                pj = 4 * g + j
                h1, h2 = 2 * pj, 2 * pj + 1
                sl = slice(128 * pj, 128 * (pj + 1))
                xp = x_all[:, sl]
                dyp = dy_all[:, sl]
                c1, c2, cd1, cd2, e_l, dte1, dte2, dte_l, dt_l = _pair_terms(cs, cst, dt, h1, h2, lo)
                xdt = xp * dt_l
                hp = hp_all[pj]
                gp = g_all[pj]
                dxp = dsk_all[:, sl] * dyp
                dyx = dyp * xp
                dd_row = dd_row + jnp.where(c[0:1, :] == h1, jnp.sum(jnp.where(lo, dyx, 0.0), keepdims=True), 0.0) \
                    + jnp.where(c[0:1, :] == h2, jnp.sum(jnp.where(lo, 0.0, dyx), keepdims=True), 0.0)
                dzs = dyp * e_l
                dc_acc = dc_acc + _dot(dzs, hp, NN)
                g_from = _dot(dzs, cg, TN)
                ryo = dyp * (_dot(cg, hp, NT) * e_l)
                k1 = jnp.sum(jnp.where(lo, ryo, 0.0), axis=1, keepdims=True)
                k2 = jnp.sum(jnp.where(lo, 0.0, ryo), axis=1, keepdims=True)
                qm = _dot(bg, gp, NT)
                dxdt = qm * dte_l
                qx = qm * xdt
                t1 = jnp.sum(jnp.where(lo, qx, 0.0), axis=1, keepdims=True) * dte1
                t2 = jnp.sum(jnp.where(lo, 0.0, qx), axis=1, keepdims=True) * dte2
                db_acc = db_acc + _dot(xdt * dte_l, gp, NN)
                gh = gp * hp
                dl1 = jnp.sum(t1, keepdims=True) + jnp.sum(jnp.where(r < HD, gh, 0.0), keepdims=True) * cd1
                dl2 = jnp.sum(t2, keepdims=True) + jnp.sum(jnp.where(r < HD, 0.0, gh), keepdims=True) * cd2
                g_new.append(g_from + jnp.where(r < HD, cd1, cd2) * gp)
                k1 = k1 - t1 + jnp.where(last[:, 0:1], dl1, 0.0)
                k2 = k2 - t2 + jnp.where(last[:, 0:1], dl2, 0.0)
                for hh, ch, msk in ((h1, c1, lo), (h2, c2, jnp.logical_not(lo))):
                    lm = jnp.exp(jnp.where(causal, ch - cst[hh:hh + 1, :], NEG))
                    mm = cb * lm
                    dm = jnp.where(causal, _dot(jnp.where(msk, dyp, 0.0), xdt, NT), 0.0)
                    w = dm * mm
                    kk = jnp.sum(w, axis=1, keepdims=True)
                    if hh == h1:
                        k1 = k1 + kk
                    else:
                        k2 = k2 + kk
                    dcs_rows = dcs_rows + jnp.where(r == hh, jnp.sum(w, axis=0, keepdims=True), 0.0)
                    dcb = dcb + dm * lm
                    dxdt = dxdt + jnp.where(msk, _dot(mm, dyp, TN), 0.0)
                dcs_cols = dcs_cols + jnp.where(c == h1, k1, 0.0) + jnp.where(c == h2, k2, 0.0)
                dxx = dxdt * xp
                ddt_x = ddt_x + jnp.where(c == h1, jnp.sum(jnp.where(lo, dxx, 0.0), axis=1, keepdims=True), 0.0) \
                    + jnp.where(c == h2, jnp.sum(jnp.where(lo, 0.0, dxx), axis=1, keepdims=True), 0.0)
                dx_parts.append(dxp + dxdt * dt_l)
            db_parts.append(db_acc + _dot(dcb, cg, TN))
            dc_parts.append(dc_acc + _dot(dcb, bg, NN))
        g_ref[...] = jnp.stack(g_new)
        dxbc_ref[...] = jnp.concatenate(dx_parts + db_parts + dc_parts, axis=1)

        dcs = dcs_cols - dcs_rows.T
        dad = _dot_exact((c >= r).astype(F32), dcs)
        ddt = dad * a + ddt_x
        ddtr = jnp.where(c < 16, ddt * _sigmoid(z), 0.0)
        ddt_ref[...] = ddtr.astype(BF16)
        r8 = lax.broadcasted_iota(jnp.int32, (8, 128), 0)
        dsc_ref[...] += (jnp.where(r8 == 0, jnp.sum(ddtr, axis=0, keepdims=True), 0.0)
                         + jnp.where(r8 == 1, jnp.sum(dad * dt, axis=0, keepdims=True) * a, 0.0)
                         + jnp.where(r8 == 2, dd_row, 0.0))

    rev = NCH - 1
    return _call(
        body, "ssd_bwd", (NCH,),
        [pl.BlockSpec((128, SSM_W), lambda i: (rev - i, 1)),
         pl.BlockSpec((128, SSM_W), lambda i: (rev - i, 0)),
         pl.BlockSpec((128, SSM_W), lambda i: (rev - i, 0)),
         pl.BlockSpec((128, 256), lambda i: (rev - i, 4)), pl.BlockSpec((128, 256), lambda i: (rev - i, 5)),
         pl.BlockSpec((128, 128), lambda i: (rev - i, COL_DT // 128)),
         pl.BlockSpec((128, SSM_W), lambda i: (rev - i, 3)),
         pl.BlockSpec((None, NPAIR, 128, 128), lambda i: (rev - i, 0, 0, 0)),
         pl.BlockSpec((1, 128), lambda i: (0, 0)), pl.BlockSpec((1, 128), lambda i: (0, 0)),
         pl.BlockSpec((1, SSM_W), lambda i: (0, 0)), pl.BlockSpec((1, SSM_W), lambda i: (0, 0))],
        [pl.BlockSpec((128, CONV_C), lambda i: (rev - i, 0)),
         pl.BlockSpec((128, SSM_W), lambda i: (rev - i, 0)),
         pl.BlockSpec((128, 128), lambda i: (rev - i, 0)),
         pl.BlockSpec((1, SSM_W), lambda i: (0, 0)), pl.BlockSpec((8, 128), lambda i: (0, 0))],
        [jax.ShapeDtypeStruct((S, CONV_C), F32), jax.ShapeDtypeStruct((S, SSM_W), BF16),
         jax.ShapeDtypeStruct((S, 128), BF16), jax.ShapeDtypeStruct((1, SSM_W), F32),
         jax.ShapeDtypeStruct((8, 128), F32)],
        (dmixed, y, xbc, xbc, xbc, proj, proj, hprev, dtb, alog, dskip_l, ssmw),
        [pltpu.VMEM((NPAIR, 128, 128), F32)], ("arbitrary",), rider)


def _cast_stack(name, slot, arrs, tr, tc):
    n = len(arrs)
    rows, cols = arrs[0].shape

    def body(s_ref, *refs):
        for i in range(n):
            refs[n][i] = refs[i][...].astype(BF16)

    return pl.pallas_call(
        body, name=name,
        grid_spec=pltpu.PrefetchScalarGridSpec(
            num_scalar_prefetch=1, grid=(rows // tr, cols // tc),
            in_specs=[pl.BlockSpec((tr, tc), lambda i, j, sr: (i, j))] * n,
            out_specs=pl.BlockSpec((None, n, tr, tc), lambda i, j, sr: (sr[0], 0, i, j))),
        out_shape=jax.ShapeDtypeStruct((NSH, n, rows, cols), BF16),
        compiler_params=_cparams("parallel", "parallel"),
    )(slot, *arrs)


def _pair_sum(name, c_idx, ps, rs, th):
    n = len(ps)
    _, rows, _ = ps[0].shape

    def body(c_ref, *refs):
        for i in range(n):
            refs[2 * n + i][...] = (refs[i][...].astype(F32) + refs[n + i][...].astype(F32)).astype(BF16)

    spec = pl.BlockSpec((None, th, HALF), lambda s, i, cr: (s, i, 0))
    return pl.pallas_call(
        body, name=name,
        grid_spec=pltpu.PrefetchScalarGridSpec(
            num_scalar_prefetch=1, grid=(NSH, rows // th),
            in_specs=[pl.BlockSpec((None, th, HALF), lambda s, i, cr: (s, i, cr[0]))] * n + [spec] * n,
            out_specs=[spec] * n),
        out_shape=[jax.ShapeDtypeStruct((NSH, rows, HALF), BF16)] * n,
        compiler_params=_cparams("parallel", "parallel"),
    )(c_idx, *ps, *rs)


def _chip_sum(name, place, cs, ts, th):
    n = len(ts)
    _, rows, _ = ts[0].shape

    def body(p_ref, *refs):
        for i in range(n):
            t = refs[n + i][...].astype(F32)
            refs[2 * n + i][...] = ((refs[i][...].astype(F32) + t[0]) + t[1]) + t[2]

    return pl.pallas_call(
        body, name=name,
        grid_spec=pltpu.PrefetchScalarGridSpec(
            num_scalar_prefetch=1, grid=(rows // th,),
            in_specs=[pl.BlockSpec((None, th, HALF), lambda i, pr: (pr[0], i, 0))] * n
            + [pl.BlockSpec((3, th, HALF), lambda i, pr: (0, i, 0))] * n,
            out_specs=[pl.BlockSpec((th, HALF), lambda i, pr: (i, pr[1]))] * n),
        out_shape=[jax.ShapeDtypeStruct((rows, D), F32)] * n, compiler_params=_cparams("parallel"),
    )(place, *cs, *ts)


def _adamw(name, ws, gs, ms, vs, tr, tc):
    n = len(ws)
    shape = ws[0].shape
    rows, cols, mid = shape[0], shape[-1], shape[1:-1]
    c1 = 1.0 / (1.0 - ADAM_B1 ** ADAM_STEP)
    c2 = 1.0 / (1.0 - ADAM_B2 ** ADAM_STEP)

    def body(*refs):
        for i in range(n):
            w, g, m, v = (refs[k * n + i][...] for k in range(4))
            m2 = ADAM_B1 * m + (1.0 - ADAM_B1) * g
            v2 = ADAM_B2 * v + (1.0 - ADAM_B2) * (g * g)
            refs[4 * n + 3 * i][...] = -ADAM_LR * ((m2 * c1) / (jnp.sqrt(v2 * c2) + ADAM_EPS) + ADAM_WD * w)
            refs[4 * n + 3 * i + 1][...] = m2
            refs[4 * n + 3 * i + 2][...] = v2

    spec = pl.BlockSpec((tr,) + mid + (tc,), lambda i, j: (i,) + (0,) * len(mid) + (j,))
    outs = pl.pallas_call(
        body, name=name, grid=(rows // tr, cols // tc), in_specs=[spec] * (4 * n), out_specs=[spec] * (3 * n),
        out_shape=[jax.ShapeDtypeStruct(shape, F32)] * (3 * n),
        compiler_params=_cparams("parallel", "parallel"),
    )(*ws, *gs, *ms, *vs)
    return [tuple(outs[3 * i:3 * i + 3]) for i in range(n)]


def _place():
    x, y, c = lax.axis_index("x"), lax.axis_index("y"), lax.axis_index("c")
    chips = [(1 - x, y), (x, 1 - y), (1 - x, 1 - y)]
    return x, y, c, chips


def _any_specs(n):
    return [pl.BlockSpec(memory_space=pl.ANY)] * n


def _rcopy(src, dst, send_sem, recv_sem, dev):
    return pltpu.make_async_remote_copy(src_ref=src, dst_ref=dst, send_sem=send_sem, recv_sem=recv_sem,
                                        device_id=dev, device_id_type=MESH)


def _gather_rider(bufs, views):
    n = len(bufs)

    def start(rin, rout, sems):
        send, recv = sems[0], sems[1]
        x, y, c, chips = _place()
        for j, chip in enumerate(chips):
            for b in range(n):
                mine = views[b](rout[b], 2 * x + y, c)
                _rcopy(mine, mine, send.at[j * n + b], recv.at[j * n + b], (chip[0], chip[1], c)).start()

    def finish(rin, rout, sems):
        send, recv, fsend, frecv = sems
        x, y, c, chips = _place()
        passed = []
        for j, chip in enumerate(chips):
            for b in range(n):
                landed = views[b](rout[b], 2 * chip[0] + chip[1], c)
                _rcopy(landed, landed, send.at[j * n + b], recv.at[j * n + b], (x, y, c)).wait_recv()
                fw = _rcopy(landed, landed, fsend.at[j * n + b], frecv.at[j * n + b], (x, y, 1 - c))
                fw.start()
                passed.append(fw)
        for j, chip in enumerate(chips):
            for b in range(n):
                other = views[b](rout[b], 2 * chip[0] + chip[1], 1 - c)
                _rcopy(other, other, fsend.at[j * n + b], frecv.at[j * n + b], (x, y, c)).wait_recv()
        for j, chip in enumerate(chips):
            for b in range(n):
                mine = views[b](rout[b], 2 * x + y, c)
                _rcopy(mine, mine, send.at[j * n + b], recv.at[j * n + b], (x, y, c)).wait_send()
        for fw in passed:
            fw.wait_send()

    return _Rider(list(bufs), [jax.ShapeDtypeStruct(a.shape, a.dtype) for a in bufs], {b: b for b in range(n)},
                  [pltpu.SemaphoreType.DMA((3 * n,))] * 4, start, finish)


def _small_gather_rider(cw):
    def descs(rin, rout, sems, x, y, c, chips):
        return [_rcopy(rin[0], rout[0].at[2 * x + y], sems[1].at[j], sems[2].at[j], (chip[0], chip[1], c))
                for j, chip in enumerate(chips)]

    def start(rin, rout, sems):
        x, y, c, chips = _place()
        pltpu.make_async_copy(rin[0], rout[0].at[2 * x + y], sems[0].at[0]).start()
        for cp in descs(rin, rout, sems, x, y, c, chips):
            cp.start()

    def finish(rin, rout, sems):
        x, y, c, chips = _place()
        for j, chip in enumerate(chips):
            _rcopy(rin[0], rout[0].at[2 * chip[0] + chip[1]], sems[1].at[j], sems[2].at[j], (x, y, c)).wait_recv()
        for cp in descs(rin, rout, sems, x, y, c, chips):
            cp.wait_send()
        pltpu.make_async_copy(rin[0], rout[0].at[2 * x + y], sems[0].at[0]).wait()

    return _Rider([cw], [jax.ShapeDtypeStruct((NSH,) + cw.shape, cw.dtype)], {},
                  [pltpu.SemaphoreType.DMA((1,)), pltpu.SemaphoreType.DMA((3,)), pltpu.SemaphoreType.DMA((3,))],
                  start, finish)


def _to_chips_rider(cs):
    n = len(cs)

    def descs(rin, rout, sems):
        x, y, c, chips = _place()
        return [_rcopy(rin[i].at[2 * chip[0] + chip[1]], rout[i].at[j], sems[0].at[j * n + i], sems[1].at[j * n + i],
                       (chip[0], chip[1], c)) for j, chip in enumerate(chips) for i in range(n)]

    def start(rin, rout, sems):
        for cp in descs(rin, rout, sems):
            cp.start()

    def finish(rin, rout, sems):
        for cp in descs(rin, rout, sems):
            cp.wait()

    return _Rider(list(cs), [jax.ShapeDtypeStruct((3,) + a.shape[1:], a.dtype) for a in cs], {},
                  [pltpu.SemaphoreType.DMA((3 * n,))] * 2, start, finish)


def _run_riders(name, riders):
    n_in = [len(r.operands) for r in riders]
    n_out = [len(r.out_shapes) for r in riders]
    n_sem = [len(r.sems) for r in riders]

    def body(*refs):
        parts, at = [], 0
        for counts in (n_in, n_out, n_sem):
            group = []
            for k in counts:
                group.append(refs[at:at + k])
                at += k
            parts.append(group)
        for i, r in enumerate(riders):
            r.start(parts[0][i], parts[1][i], parts[2][i])
        for i, r in enumerate(riders):
            r.finish(parts[0][i], parts[1][i], parts[2][i])

    aliases = {}
    for i, r in enumerate(riders):
        for k, v in r.aliases.items():
            aliases[sum(n_in[:i]) + k] = sum(n_out[:i]) + v
    res = pl.pallas_call(
        body, name=name, in_specs=_any_specs(sum(n_in)), out_specs=_any_specs(sum(n_out)),
        out_shape=[s for r in riders for s in r.out_shapes], input_output_aliases=aliases,
        scratch_shapes=[s for r in riders for s in r.sems],
    )(*[a for r in riders for a in r.operands])
    out, at = [], 0
    for k in n_out:
        out.append(list(res[at:at + k]))
        at += k
    return out


def _to_sibling(name, ps):
    n = len(ps)

    def body(*refs):
        src, dst, send, recv = refs[:n], refs[n:2 * n], refs[2 * n], refs[2 * n + 1]
        x, y, c, _ = _place()
        cps = [pltpu.make_async_remote_copy(
            src_ref=src[i].at[:, :, pl.ds((1 - c) * HALF, HALF)], dst_ref=dst[i], send_sem=send.at[i],
            recv_sem=recv.at[i], device_id=(x, y, 1 - c), device_id_type=MESH) for i in range(n)]
        for cp in cps:
            cp.start()
        for cp in cps:
            cp.wait()

    outs = [jax.ShapeDtypeStruct(p.shape[:2] + (HALF,), p.dtype) for p in ps]
    return pl.pallas_call(
        body, name=name, in_specs=_any_specs(n), out_specs=_any_specs(n), out_shape=outs,
        scratch_shapes=[pltpu.SemaphoreType.DMA((n,)), pltpu.SemaphoreType.DMA((n,))],
    )(*ps)


def _swap_halves(gs):
    n = len(gs)

    def body(*refs):
        dst, send, recv = refs[n:2 * n], refs[2 * n], refs[2 * n + 1]
        x, y, c, _ = _place()
        cps = []
        for i in range(n):
            mine = dst[i].at[:, pl.ds(c * HALF, HALF)]
            cps.append(pltpu.make_async_remote_copy(
                src_ref=mine, dst_ref=mine, send_sem=send.at[i], recv_sem=recv.at[i],
                device_id=(x, y, 1 - c), device_id_type=MESH))
        for cp in cps:
            cp.start()
        for i in range(n):
            other = dst[i].at[:, pl.ds((1 - c) * HALF, HALF)]
            pltpu.make_async_remote_copy(
                src_ref=other, dst_ref=other, send_sem=send.at[i], recv_sem=recv.at[i],
                device_id=(x, y, c), device_id_type=MESH).wait_recv()
        for cp in cps:
            cp.wait_send()

    return pl.pallas_call(
        body, name="grads_swap_halves", in_specs=_any_specs(n), out_specs=_any_specs(n),
        out_shape=[jax.ShapeDtypeStruct(g.shape, g.dtype) for g in gs],
        input_output_aliases={i: i for i in range(n)},
        scratch_shapes=[pltpu.SemaphoreType.DMA((n,)), pltpu.SemaphoreType.DMA((n,))],
    )(*gs)


SMALL_ROWS = 16


def _allreduce_small(vec):
    def body(v_ref, o_ref, buf, send, recv):
        x, y, c, _ = _place()
        me = 4 * x + 2 * y + c
        buf[me] = v_ref[...]
        cps = []
        for k in range(1, 8):
            peer = (x ^ (k >> 2), y ^ ((k >> 1) & 1), c ^ (k & 1))
            cps.append(pltpu.make_async_remote_copy(
                src_ref=v_ref, dst_ref=buf.at[me], send_sem=send.at[k - 1], recv_sem=recv.at[k - 1],
                device_id=peer, device_id_type=MESH))
        for cp in cps:
            cp.start()
        for k in range(1, 8):
            pltpu.make_async_remote_copy(
                src_ref=v_ref, dst_ref=buf.at[me ^ k], send_sem=send.at[k - 1], recv_sem=recv.at[k - 1],
                device_id=(x, y, c), device_id_type=MESH).wait_recv()
        for cp in cps:
            cp.wait_send()
        t = buf[0]
        for d in range(1, 8):
            t = t + buf[d]
        o_ref[...] = t

    return pl.pallas_call(
        body, name="allreduce_small",
        in_specs=[pl.BlockSpec(memory_space=pltpu.VMEM)], out_specs=pl.BlockSpec(memory_space=pltpu.VMEM),
        out_shape=jax.ShapeDtypeStruct((SMALL_ROWS, D), F32),
        scratch_shapes=[pltpu.VMEM((8, SMALL_ROWS, D), F32), pltpu.SemaphoreType.DMA((7,)),
                        pltpu.SemaphoreType.DMA((7,))],
    )(vec)


def _col_half(ref, slot, hc):
    return ref.at[slot, :, pl.ds(hc * HALF, HALF)]


def _ffn_half(first, ref, slot, hc):
    return ref.at[slot, pl.ds(first, 3), :, pl.ds(hc * HALF, HALF)]


def _row_tile(rows):
    for t in range(512, 15, -16):
        if rows % t == 0:
            return t
    return rows


def _same_shape_runs(arrs):
    runs, a = [], 0
    for b in range(1, len(arrs) + 1):
        if b == len(arrs) or arrs[b].shape != arrs[a].shape:
            runs.append((a, b))
            a = b
    return runs


class _Comm:
    def __init__(self):
        x, y, c = lax.axis_index("x"), lax.axis_index("y"), lax.axis_index("c")
        self.c_idx = jnp.reshape(c, (1,)).astype(jnp.int32)
        self.place = jnp.stack([2 * x + y, c]).astype(jnp.int32)
        self.groups = {}

    def gather_mixer(self, wint, wout):
        return _gather_rider([wint, wout], [_col_half, _col_half])

    def gather_ffn(self, ffnw, first):
        return _gather_rider([ffnw], [functools.partial(_ffn_half, first)])

    def reduce_rider(self, tag, names, ps):
        rs = _to_sibling("grads_to_sibling_" + tag, ps)
        csums = []
        for a, b in _same_shape_runs(ps):
            csums += _pair_sum("pair_sum_%s%d" % (tag, a), self.c_idx, ps[a:b], rs[a:b], _row_tile(ps[a].shape[1]))
        self.groups[tag] = [names, csums, None]
        return _to_chips_rider(csums)

    def landed(self, tag, ts):
        self.groups[tag][2] = ts

    def finish(self):
        names, halves = [], []
        for tag, (group_names, csums, ts) in self.groups.items():
            names += group_names
            for a, b in _same_shape_runs(csums):
                halves += _chip_sum("chip_sum_%s%d" % (tag, a), self.place, csums[a:b], ts[a:b],
                                    _row_tile(csums[a].shape[1]))
        return dict(zip(names, _swap_halves(halves)))


ROPE_THETA = 10000.0
SMALL_1K = ("ffn1_pre_norm", "ffn1_post_norm", "mix_pre_norm", "ssm_norm", "mix_post_norm",
            "ffn2_pre_norm", "ffn2_post_norm")
SMALL_16 = ("dt_bias", "a_log", "d_skip")
OFF_CONVB = 7 * D
OFF_16 = OFF_CONVB + CONV_C
OFF_CONVW = OFF_16 + 48
OFF_LOSS = OFF_CONVW + CONV_K * CONV_C
SMALL_LEN = SMALL_ROWS * D


def _sds(shape, dtype):
    return jax.ShapeDtypeStruct(shape, dtype)


def _ffn_down(name, act, ffnw, wi):
    return _mm(name, [act, ffnw], NN, (S // TS,),
               [pl.BlockSpec((NSH, TS, FS), lambda i: (0, i, 0)),
                pl.BlockSpec((NSH, None, FS, D), lambda i: (0, wi + 2, 0, 0))],
               pl.BlockSpec((TS, D), lambda i: (i, 0)), _sds((S, D), F32))


def _ridden(res, rider):
    return res if rider is not None else (res, None)


def _ffn_bwd(tag, dh, n, gate, up, act, ffnw, wi, dact_rider=None, dn_rider_of=None):
    (dgate, dup), got1 = _ridden(_ffn_dact(tag + "_dact", dh, ffnw, wi, gate, up, dact_rider), dact_rider)
    aspec = pl.BlockSpec((None, S, FS), lambda s: (s, 0, 0))
    nspec = pl.BlockSpec((S, D), lambda s: (0, 0))
    wspec = pl.BlockSpec((None, FS, D), lambda s: (s, 0, 0))
    dws = [_mm(tag + nm, [a, b], TN, (NSH,), [aspec, nspec], wspec, _sds((NSH, FS, D), BF16))
           for nm, a, b in (("_dwg", dgate, n), ("_dwu", dup, n), ("_dwd", act, dh))]
    dn_rider = dn_rider_of(dws) if dn_rider_of is not None else None
    a2 = pl.BlockSpec((NSH, TS, FS), lambda i: (0, i, 0))
    dn, got2 = _ridden(_mm(
        tag + "_dn", [dgate, ffnw, dup, ffnw], NN, (S // TS,),
        [a2, pl.BlockSpec((NSH, None, FS, D), lambda i: (0, wi, 0, 0)),
         a2, pl.BlockSpec((NSH, None, FS, D), lambda i: (0, wi + 1, 0, 0))],
        pl.BlockSpec((TS, D), lambda i: (i, 0)), _sds((S, D), F32), dn_rider), dn_rider)
    return dn, dws, got1, got2


def _heads(t, n):
    return t.reshape(S, n, HD).transpose(1, 0, 2)


def _unheads(t):
    return t.transpose(1, 0, 2).reshape(S, t.shape[0] * HD)


def _heads_t(t, n):
    return t.reshape(S, n, HD).transpose(1, 2, 0)


def _blocks5(t):
    return t.reshape(NCH, 128, NKV, NQ_PER_KV, HD)


def _to_blocks_t(t):
    return _blocks5(t).transpose(2, 0, 4, 3, 1).reshape(NKV, NCH, HD, QROWS)


def _to_blocks(t):
    return _blocks5(t).transpose(2, 0, 3, 1, 4).reshape(NKV, NCH, QROWS, HD)


def _from_blocks_t(t):
    return t.reshape(NKV, NCH, HD, NQ_PER_KV, 128).transpose(1, 4, 0, 3, 2).reshape(S, D)


def _pad128(v):
    return jnp.pad(v, ((0, 0), (0, 128 - v.shape[1])))


def _local_step(x, positions, tgt, sp, ffnw, wint, wout, convw, comm=None):
    inv_freq = ROPE_THETA ** (-jnp.arange(0, HD, 2, dtype=F32) / HD)
    ang = positions.astype(F32)[:, None] * inv_freq
    ang = jnp.concatenate([ang, ang, ang, ang], axis=-1)
    cos, sin = jnp.cos(ang), jnp.sin(ang)
    dtb, alog = _pad128(sp["dt_bias"]), _pad128(sp["a_log"])
    dskip_l = jnp.repeat(sp["d_skip"], HD, axis=1)
    convb = sp["conv_b"]

    n1 = _prenorm("prenorm1", x, sp["ffn1_pre_norm"])
    rider = comm.gather_mixer(wint, wout) if comm else None
    (gate1, up1, act1), got = _ridden(_ffn_up("ffn1_up", n1, ffnw, 0, rider), rider)
    if comm:
        wint, wout = got
    wint_pad = jnp.pad(wint.reshape(WIN_COLS, D), ((0, WIN_PAD - WIN_COLS), (0, 0)))
    wout = wout.reshape(2 * D, D)
    h1 = _ffn_down("ffn1_down", act1, ffnw, 0)
    x1, n2 = _postres("postres1", x, h1, sp["ffn1_post_norm"], 0.5, sp["mix_pre_norm"])

    pw = WIN_PAD // 3
    proj = _mm("in_proj", [n2, wint_pad], NT, (S // TS, 3),
               [pl.BlockSpec((TS, D), lambda i, j: (i, 0)), pl.BlockSpec((pw, D), lambda i, j: (j, 0))],
               pl.BlockSpec((TS, pw), lambda i, j: (i, j)), _sds((S, WIN_PAD), F32))
    q_rot = _rope("rope_q", proj, 0, D, cos, sin, 1.0, HD ** -0.5)
    k_rot = _rope("rope_k", proj, D // KVW, KVW, cos, sin, 1.0, 1.0)
    v_bf = proj[:, D + KVW:D + 2 * KVW].astype(BF16)
    qt, kh, vh = _to_blocks_t(q_rot), _heads(k_rot, NKV), _heads(v_bf, NKV)
    kt, vt = _heads_t(k_rot, NKV), _heads_t(v_bf, NKV)
    bias = _bias_table()
    rider = comm.gather_ffn(ffnw, 3) if comm else None
    (ot, lse), got = _ridden(_attn_fwd(qt, kh, vt, bias, rider), rider)
    if comm:
        ffnw, = got
    attn = _from_blocks_t(ot).astype(BF16)
    xbc = _conv_fwd(proj, convw, convb)
    y, yn, hprev = _ssd_fwd(xbc, proj, dtb, alog, dskip_l, sp["ssm_norm"])
    mixed = jnp.concatenate([attn, yn], axis=1)
    h2 = _mm("out_proj", [mixed, wout], NN, (S // TS,),
             [pl.BlockSpec((TS, 2 * D), lambda i: (i, 0)), pl.BlockSpec((2 * D, D), lambda i: (0, 0))],
             pl.BlockSpec((TS, D), lambda i: (i, 0)), _sds((S, D), F32))
    x2, n3 = _postres("postres2", x1, h2, sp["mix_post_norm"], 1.0, sp["ffn2_pre_norm"])

    gate2, up2, act2 = _ffn_up("ffn2_up", n3, ffnw, 3)
    h3 = _ffn_down("ffn2_down", act2, ffnw, 3)
    dy, dh3, dp3, loss = _final(x2, h3, sp["ffn2_post_norm"], tgt, 0.5)

    dn3, dws2, _, _ = _ffn_bwd("ffn2", dh3, n3, gate2, up2, act2, ffnw, 3)
    dx2, dh2, dg3, dp2 = _mid_bwd("mid_bwd2", dy, dn3, x2, sp["ffn2_pre_norm"], h2, sp["mix_post_norm"], 1.0)

    dmixed = _mm("out_proj_dx", [dh2, wout], NT, (S // TS,),
                 [pl.BlockSpec((TS, D), lambda i: (i, 0)), pl.BlockSpec((2 * D, D), lambda i: (0, 0))],
                 pl.BlockSpec((TS, 2 * D), lambda i: (i, 0)), _sds((S, 2 * D), F32))
    dwout = _mm("out_proj_dw", [mixed, dh2], TN, (2,),
                [pl.BlockSpec((S, D), lambda m: (0, m)), pl.BlockSpec((S, D), lambda m: (0, 0))],
                pl.BlockSpec((D, D), lambda m: (m, 0)), _sds((2 * D, D), BF16))
    dwout = dwout.reshape(NSH, 2 * D // NSH, D)
    rider = comm.reduce_rider("a", BIG[3:6] + ("w_out",), dws2 + [dwout]) if comm else None
    (dxbc, dz, ddt, dssm, dsc), got = _ridden(
        _ssd_bwd(dmixed, y, xbc, proj, hprev, dtb, alog, dskip_l, sp["ssm_norm"], rider), rider)
    if comm:
        comm.landed("a", got)
    du, dcw8, dcb = _conv_bwd(dxbc, proj, convw, convb)
    do_bf = dmixed[:, :D].astype(BF16)
    dot_ = _to_blocks_t(do_bf)
    dqt, dkh, dvh = _attn_bwd(qt, _to_blocks(q_rot), kh, kt, vh, dot_, _to_blocks(do_bf), lse,
                              _attn_delta(ot, dot_), bias)
    dq = _rope("rope_dq", _from_blocks_t(dqt), 0, D, cos, sin, -1.0, HD ** -0.5)
    dk = _rope("rope_dk", _unheads(dkh), 0, KVW, cos, sin, -1.0, 1.0)
    dproj = jnp.concatenate([dq, dk, _unheads(dvh).astype(BF16), du, dz, ddt], axis=1)
    dn2 = _mm("in_proj_dx", [dproj, wint_pad], NN, (S // TS,),
              [pl.BlockSpec((TS, WIN_PAD), lambda i: (i, 0)), pl.BlockSpec((WIN_PAD, D), lambda i: (0, 0))],
              pl.BlockSpec((TS, D), lambda i: (i, 0)), _sds((S, D), F32))
    dwint = _mm("in_proj_dw", [dproj, n2], TN, (3,),
                [pl.BlockSpec((S, pw), lambda j: (0, j)), pl.BlockSpec((S, D), lambda j: (0, 0))],
                pl.BlockSpec((pw, D), lambda j: (j, 0)), _sds((WIN_PAD, D), BF16))
    dx1, dh1, dg2, dp1 = _mid_bwd("mid_bwd1", dx2, dn2, x1, sp["mix_pre_norm"], h1, sp["ffn1_post_norm"], 0.5)

    dwint = dwint[:WIN_COLS].reshape(NSH, WIN_SH, D)
    dn1, dws1, got1, got2 = _ffn_bwd(
        "ffn1", dh1, n1, gate1, up1, act1, ffnw, 0,
        comm.reduce_rider("b", ("w_in",), [dwint]) if comm else None,
        (lambda dws: comm.reduce_rider("c", BIG[0:3], dws)) if comm else None)
    grad_x, dg1 = _first_bwd(dx1, dn1, x, sp["ffn1_pre_norm"])

    small = jnp.concatenate([
        dg1[0], dp1[0], dg2[0], dssm[0], dp2[0], dg3[0], dp3[0], dcb[0],
        dsc[0, :16], dsc[1, :16], dsc[2, :16], dcw8[:CONV_K].reshape(-1), loss[0, :1]])
    small = jnp.pad(small, (0, SMALL_LEN - small.shape[0])).reshape(SMALL_ROWS, D)
    if comm is None:
        return grad_x, dws1 + dws2 + [dwint, dwout], small
    comm.landed("b", got1)
    comm.landed("c", got2)
    return grad_x, comm.finish(), small


WEIGHTS = ("ffn1_pre_norm", "ffn1_w_gate", "ffn1_w_up", "ffn1_w_down", "ffn1_post_norm", "mix_pre_norm", "w_in",
           "conv_w", "conv_b", "dt_bias", "a_log", "d_skip", "ssm_norm", "w_out", "mix_post_norm", "ffn2_pre_norm",
           "ffn2_w_gate", "ffn2_w_up", "ffn2_w_down", "ffn2_post_norm")
BIG = ("ffn1_w_gate", "ffn1_w_up", "ffn1_w_down", "ffn2_w_gate", "ffn2_w_up", "ffn2_w_down", "w_in", "w_out")
TRANSPOSED = ("ffn1_w_gate", "ffn1_w_up", "ffn2_w_gate", "ffn2_w_up", "w_in")
SMALL_ORDER = SMALL_1K + ("conv_b",) + SMALL_16
CONVW_SH = CONV_C // NSH


def _shard2d(t, name):
    return t[0].T if name in TRANSPOSED else t[0]


def _unshard2d(t, name):
    return (t.T if name in TRANSPOSED else t)[None]


def _rows3d(t):
    return t.transpose(2, 0, 1)


def _pack_small(d, prefix, shard_of_convw):
    flat = jnp.concatenate([d[prefix + n][0] for n in SMALL_ORDER] + [shard_of_convw.reshape(-1)])
    return jnp.pad(flat, (0, SMALL_LEN - flat.shape[0])).reshape(SMALL_ROWS, D)


def _unpack_small(block, like):
    flat = block.reshape(-1)
    out, off = {}, 0
    for n in SMALL_ORDER:
        size = like[n].shape[1]
        out[n] = flat[off:off + size].reshape(1, size)
        off += size
    out["conv_w"] = flat[off:off + CONV_K * CONVW_SH].reshape(1, CONV_K, CONVW_SH)
    return out


def kernel(x, positions, ffn1_pre_norm, ffn1_w_gate, ffn1_w_up, ffn1_w_down, ffn1_post_norm, mix_pre_norm, w_in, conv_w, conv_b, dt_bias, a_log, d_skip, ssm_norm, w_out, mix_post_norm, ffn2_pre_norm, ffn2_w_gate, ffn2_w_up, ffn2_w_down, ffn2_post_norm, loss_target, m_ffn1_pre_norm, m_ffn1_w_gate, m_ffn1_w_up, m_ffn1_w_down, m_ffn1_post_norm, m_mix_pre_norm, m_w_in, m_conv_w, m_conv_b, m_dt_bias, m_a_log, m_d_skip, m_ssm_norm, m_w_out, m_mix_post_norm, m_ffn2_pre_norm, m_ffn2_w_gate, m_ffn2_w_up, m_ffn2_w_down, m_ffn2_post_norm, v_ffn1_pre_norm, v_ffn1_w_gate, v_ffn1_w_up, v_ffn1_w_down, v_ffn1_post_norm, v_mix_pre_norm, v_w_in, v_conv_w, v_conv_b, v_dt_bias, v_a_log, v_d_skip, v_ssm_norm, v_w_out, v_mix_post_norm, v_ffn2_pre_norm, v_ffn2_w_gate, v_ffn2_w_up, v_ffn2_w_down, v_ffn2_post_norm):
    given = dict(locals())
    xi, yi = lax.axis_index("x"), lax.axis_index("y")

    shard = jnp.reshape(2 * xi + yi, (1,)).astype(jnp.int32)
    big = {p + n: _shard2d(given[p + n], n) for n in BIG for p in ("", "m_", "v_")}
    ffnsh = _cast_stack("cast_ffn", shard, [big[n] for n in BIG[:6]], 176, D)
    winsh = _cast_stack("cast_w_in", shard, [big["w_in"]], WIN_SH, 256).reshape(NSH, WIN_SH, D)
    woutsh = _cast_stack("cast_w_out", shard, [big["w_out"]], 256, D).reshape(NSH, 2 * D // NSH, D)
    comm = _Comm()
    (ffnw,), (cwf,) = _run_riders("gather_ffn1", [comm.gather_ffn(ffnsh, 0), _small_gather_rider(conv_w[0])])
    convw = cwf.transpose(1, 0, 2).reshape(CONV_K, CONV_C)

    sp = {n: given[n] for n in SMALL_ORDER}
    grad_x, big_grads, small = _local_step(x[0], positions[0], loss_target[0], sp, ffnw, winsh, woutsh, convw, comm)

    tot = _allreduce_small(small).reshape(-1)
    loss = tot[OFF_LOSS]
    small_grads, off = {}, 0
    for n in SMALL_ORDER:
        size = given[n].shape[1]
        small_grads[n] = tot[off:off + size].reshape(1, size)
        off += size
    dconvw = tot[OFF_CONVW:OFF_CONVW + CONV_K * CONV_C].reshape(CONV_K, NSH, CONVW_SH)
    dconvw = lax.dynamic_index_in_dim(dconvw, 2 * xi + yi, axis=1, keepdims=False)
    small_grads["conv_w"] = dconvw.reshape(1, CONV_K, CONVW_SH)

    upd = {}
    for names, tr in ((BIG[0:3], 176), (BIG[3:6], 176), (BIG[7:8], 256)):
        res = _adamw("adamw_" + names[0], [big[n] for n in names], [big_grads[n] for n in names],
                     [big["m_" + n] for n in names], [big["v_" + n] for n in names], tr, D)
        for n, r in zip(names, res):
            upd[n] = tuple(_unshard2d(t, n) for t in r)
    g_win = big_grads["w_in"].reshape(WIN_SH, 1, D)
    res, = _adamw("adamw_w_in", [_rows3d(w_in)], [g_win], [_rows3d(m_w_in)], [_rows3d(v_w_in)], WIN_SH // 4, D)
    upd["w_in"] = tuple(t.transpose(1, 2, 0) for t in res)
    (dl, m2, v2), = _adamw(
        "adamw_small", [_pack_small(given, "", conv_w[0])], [_pack_small(small_grads, "", dconvw)],
        [_pack_small(given, "m_", m_conv_w[0])], [_pack_small(given, "v_", v_conv_w[0])], SMALL_ROWS, D)
    dl, m2, v2 = (_unpack_small(t, given) for t in (dl, m2, v2))
    for n in SMALL_ORDER + ("conv_w",):
        upd[n] = (dl[n], m2[n], v2[n])

    grads = dict(small_grads)
    grads.update({n: _unshard2d(g, n) for n, g in big_grads.items() if n != "w_in"})
    grads["w_in"] = g_win.transpose(1, 2, 0)
    return (loss, grad_x[None], *[grads[n] for n in WEIGHTS], *[upd[n][0] for n in WEIGHTS],
            *[upd[n][1] for n in WEIGHTS], *[upd[n][2] for n in WEIGHTS])
```

```python
import functools
import typing

import jax
import jax.numpy as jnp
from jax import lax
from jax.experimental import pallas as pl
from jax.experimental.pallas import tpu as pltpu

F32 = jnp.float32
BF16 = jnp.bfloat16

S = 2048
D = 1024
FF = 2816
NSH = 4
FS = FF // NSH
HALF = D // 2
HD = 64
NKV = 4
NQ_PER_KV = 4
KVW = NKV * HD
CONV_C = 1536
CONV_K = 4
SSM_W = 1024
NST = 128
NCH = S // 128
WIN_COLS = 4112
WIN_SH = WIN_COLS // NSH
WIN_PAD = 4224
COL_DT = 4096
EPS = 1e-6
NEG = -1e30

ADAM_LR = 0.001
ADAM_B1 = 0.9
ADAM_B2 = 0.999
ADAM_EPS = 1e-08
ADAM_WD = 0.01
ADAM_STEP = 10

VMEM_LIMIT = 56 * 1024 * 1024
TS = 512
TR = 256

NN = (((1,), (0,)), ((), ()))
NT = (((1,), (1,)), ((), ()))
TN = (((0,), (0,)), ((), ()))
MESH = pl.DeviceIdType.MESH


def _cparams(*sem):
    return pltpu.CompilerParams(dimension_semantics=sem, vmem_limit_bytes=VMEM_LIMIT)


def _dot(a, b, dims):
    return lax.dot_general(a.astype(BF16), b.astype(BF16), dims, preferred_element_type=F32)


def _dot_exact(a, b):
    return lax.dot_general(a, b, NN, precision=lax.Precision.HIGHEST, preferred_element_type=F32)


def _sigmoid(v):
    return 1.0 / (1.0 + jnp.exp(-v))


class _Rider(typing.NamedTuple):
    operands: list
    out_shapes: list
    aliases: dict
    sems: list
    start: typing.Callable
    finish: typing.Callable


def _call(body, name, grid, in_specs, out_specs, out_shape, operands, scratch=(), sem=(), rider=None):
    multi = isinstance(out_shape, (list, tuple))
    if rider is None:
        return pl.pallas_call(
            body, name=name, grid=grid, in_specs=in_specs, out_specs=out_specs, out_shape=out_shape,
            scratch_shapes=list(scratch), compiler_params=_cparams(*sem))(*operands)
    outs = list(out_shape) if multi else [out_shape]
    ospecs = list(out_specs) if multi else [out_specs]
    n_in, n_out, n_scr = len(operands), len(outs), len(scratch)
    ri, ro = len(rider.operands), len(rider.out_shapes)

    def wrapped(*refs):
        o0 = n_in + ri
        s0 = o0 + n_out + ro
        rin, rout, rsem = refs[n_in:o0], refs[o0 + n_out:s0], refs[s0 + n_scr:]
        ids = [pl.program_id(a) for a in range(len(grid))]
        first = functools.reduce(jnp.logical_and, [i == 0 for i in ids])
        last = functools.reduce(jnp.logical_and, [i == g - 1 for i, g in zip(ids, grid)])

        @pl.when(first)
        def _():
            rider.start(rin, rout, rsem)

        body(*refs[:n_in], *refs[o0:o0 + n_out], *refs[s0:s0 + n_scr])

        @pl.when(last)
        def _():
            rider.finish(rin, rout, rsem)

    hbm = pl.BlockSpec(memory_space=pl.ANY)
    res = pl.pallas_call(
        wrapped, name=name, grid=grid, in_specs=list(in_specs) + [hbm] * ri, out_specs=ospecs + [hbm] * ro,
        out_shape=outs + list(rider.out_shapes), scratch_shapes=list(scratch) + list(rider.sems),
        input_output_aliases={n_in + k: n_out + v for k, v in rider.aliases.items()},
        compiler_params=_cparams(*(("arbitrary",) * len(grid))))(*operands, *rider.operands)
    main = list(res[:n_out])
    return (main if multi else main[0]), list(res[n_out:])


def _mm(name, operands, dims, grid, in_specs, o_spec, out_shape, rider=None):
    npairs = len(operands) // 2

    def body(*refs):
        t = None
        for i in range(npairs):
            a, b = refs[2 * i], refs[2 * i + 1]
            parts = [(a[s], b[s]) for s in range(a.shape[0])] if len(a.shape) == 3 else [(a[...], b[...])]
            for pa, pb in parts:
                d = _dot(pa, pb, dims)
                t = d if t is None else t + d
        refs[2 * npairs][...] = t.astype(refs[2 * npairs].dtype)

    return _call(body, name, grid, in_specs, o_spec, out_shape, operands, (), ("parallel",) * len(grid), rider)


def _ffn_up(name, n, ffnw, wi, rider=None):
    def body(n_ref, wg_ref, wu_ref, g_ref, u_ref, a_ref):
        nb = n_ref[...]
        g = _dot(nb, wg_ref[...], NT)
        u = _dot(nb, wu_ref[...], NT)
        g_ref[...] = g.astype(BF16)
        u_ref[...] = u.astype(BF16)
        a_ref[...] = (g * _sigmoid(g) * u).astype(BF16)

    out = jax.ShapeDtypeStruct((NSH, S, FS), BF16)
    ospec = pl.BlockSpec((None, TS, FS), lambda s, i: (s, i, 0))
    return _call(
        body, name, (NSH, S // TS),
        [pl.BlockSpec((TS, D), lambda s, i: (i, 0)),
         pl.BlockSpec((None, None, FS, D), lambda s, i: (s, wi, 0, 0)),
         pl.BlockSpec((None, None, FS, D), lambda s, i: (s, wi + 1, 0, 0))],
        [ospec, ospec, ospec], [out, out, out], (n, ffnw, ffnw), sem=("parallel", "parallel"), rider=rider)


def _ffn_dact(name, dh, ffnw, wi, gate, up, rider=None):
    def body(dh_ref, wd_ref, g_ref, u_ref, dg_ref, du_ref):
        da = _dot(dh_ref[...], wd_ref[...], NT)
        g = g_ref[...].astype(F32)
        u = u_ref[...].astype(F32)
        sg = _sigmoid(g)
        dg_ref[...] = (da * u * (sg * (1.0 + g * (1.0 - sg)))).astype(BF16)
        du_ref[...] = (da * (g * sg)).astype(BF16)

    out = jax.ShapeDtypeStruct((NSH, S, FS), BF16)
    aspec = pl.BlockSpec((None, TS, FS), lambda s, i: (s, i, 0))
    return _call(
        body, name, (NSH, S // TS),
        [pl.BlockSpec((TS, D), lambda s, i: (i, 0)),
         pl.BlockSpec((None, None, FS, D), lambda s, i: (s, wi + 2, 0, 0)), aspec, aspec],
        [aspec, aspec], [out, out], (dh, ffnw, gate, up), sem=("parallel", "parallel"), rider=rider)


def _rstd(v):
    return lax.rsqrt(jnp.mean(v * v, axis=-1, keepdims=True) + EPS)


def _row_spec():
    return pl.BlockSpec((TR, D), lambda i: (i, 0))


def _vec_spec():
    return pl.BlockSpec((1, D), lambda i: (0, 0))


def _acc_rows(ref, v):
    @pl.when(pl.program_id(0) == 0)
    def _():
        ref[...] = jnp.zeros_like(ref)
    ref[...] += jnp.sum(v, axis=0, keepdims=True)


def _prenorm(name, x, g):
    def body(x_ref, g_ref, n_ref):
        xv = x_ref[...]
        n_ref[...] = (xv * _rstd(xv) * g_ref[...]).astype(BF16)

    return pl.pallas_call(
        body, name=name, grid=(S // TR,), in_specs=[_row_spec(), _vec_spec()], out_specs=_row_spec(),
        out_shape=jax.ShapeDtypeStruct((S, D), BF16), compiler_params=_cparams("parallel"),
    )(x, g)


def _postres(name, x, h, p, alpha, gnext):
    def body(x_ref, h_ref, p_ref, g_ref, xo_ref, n_ref):
        hv = h_ref[...]
        xo = x_ref[...] + alpha * (hv * _rstd(hv) * p_ref[...])
        xo_ref[...] = xo
        n_ref[...] = (xo * _rstd(xo) * g_ref[...]).astype(BF16)

    return pl.pallas_call(
        body, name=name, grid=(S // TR,),
        in_specs=[_row_spec(), _row_spec(), _vec_spec(), _vec_spec()],
        out_specs=[_row_spec(), _row_spec()],
        out_shape=[jax.ShapeDtypeStruct((S, D), F32), jax.ShapeDtypeStruct((S, D), BF16)],
        compiler_params=_cparams("parallel"),
    )(x, h, p, gnext)


def _final(x, h, p, tgt, alpha):
    def body(x_ref, h_ref, p_ref, t_ref, dy_ref, dh_ref, dp_ref, loss_ref):
        hv = h_ref[...]
        r = _rstd(hv)
        hn = hv * r
        pv = p_ref[...]
        e = x_ref[...] + alpha * (hn * pv) - t_ref[...]
        dy = e * (1.0 / D)
        dy_ref[...] = dy
        du = alpha * dy * pv
        dh_ref[...] = (r * (du - hn * jnp.mean(du * hn, axis=-1, keepdims=True))).astype(BF16)
        _acc_rows(dp_ref, alpha * dy * hn)
        part = 0.5 * jnp.sum(jnp.mean(e * e, axis=-1, keepdims=True), axis=0, keepdims=True)
        _acc_rows(loss_ref, jnp.broadcast_to(part, (1, 128)))

    return pl.pallas_call(
        body, name="loss_head", grid=(S // TR,),
        in_specs=[_row_spec(), _row_spec(), _vec_spec(), _row_spec()],
        out_specs=[_row_spec(), _row_spec(), _vec_spec(), pl.BlockSpec((1, 128), lambda i: (0, 0))],
        out_shape=[jax.ShapeDtypeStruct((S, D), F32), jax.ShapeDtypeStruct((S, D), BF16),
                   jax.ShapeDtypeStruct((1, D), F32), jax.ShapeDtypeStruct((1, 128), F32)],
        compiler_params=_cparams("arbitrary"),
    )(x, h, p, tgt)


def _mid_bwd(name, dres, dn, x, g, h, p, alpha):
    def body(dr_ref, dn_ref, x_ref, g_ref, h_ref, p_ref, dx_ref, dh_ref, dg_ref, dp_ref):
        xv = x_ref[...]
        xn = xv * _rstd(xv)
        dnv = dn_ref[...]
        dng = dnv * g_ref[...]
        dx = dr_ref[...] + _rstd(xv) * (dng - xn * jnp.mean(dng * xn, axis=-1, keepdims=True))
        dx_ref[...] = dx
        _acc_rows(dg_ref, dnv * xn)
        hv = h_ref[...]
        r = _rstd(hv)
        hn = hv * r
        du = alpha * dx * p_ref[...]
        dh_ref[...] = (r * (du - hn * jnp.mean(du * hn, axis=-1, keepdims=True))).astype(BF16)
        _acc_rows(dp_ref, alpha * dx * hn)

    return pl.pallas_call(
        body, name=name, grid=(S // TR,),
        in_specs=[_row_spec(), _row_spec(), _row_spec(), _vec_spec(), _row_spec(), _vec_spec()],
        out_specs=[_row_spec(), _row_spec(), _vec_spec(), _vec_spec()],
        out_shape=[jax.ShapeDtypeStruct((S, D), F32), jax.ShapeDtypeStruct((S, D), BF16),
                   jax.ShapeDtypeStruct((1, D), F32), jax.ShapeDtypeStruct((1, D), F32)],
        compiler_params=_cparams("arbitrary"),
    )(dres, dn, x, g, h, p)


def _first_bwd(dres, dn, x, g):
    def body(dr_ref, dn_ref, x_ref, g_ref, dx_ref, dg_ref):
        xv = x_ref[...]
        r = _rstd(xv)
        xn = xv * r
        dnv = dn_ref[...]
        dng = dnv * g_ref[...]
        dx_ref[...] = dr_ref[...] + r * (dng - xn * jnp.mean(dng * xn, axis=-1, keepdims=True))
        _acc_rows(dg_ref, dnv * xn)

    return pl.pallas_call(
        body, name="first_bwd", grid=(S // TR,),
        in_specs=[_row_spec(), _row_spec(), _row_spec(), _vec_spec()],
        out_specs=[_row_spec(), _vec_spec()],
        out_shape=[jax.ShapeDtypeStruct((S, D), F32), jax.ShapeDtypeStruct((1, D), F32)],
        compiler_params=_cparams("arbitrary"),
    )(dres, dn, x, g)


def _rope(name, src, col_block, width, cos, sin, sign, scale):
    def body(t_ref, c_ref, s_ref, o_ref):
        t = t_ref[...].astype(F32)
        c = jnp.tile(c_ref[...], (1, width // 128))
        sn = jnp.tile(s_ref[...], (1, width // 128))
        lane = lax.broadcasted_iota(jnp.int32, t.shape, 1) & (HD - 1)
        rot = jnp.where(lane < HD // 2, -pltpu.roll(t, width - HD // 2, 1), pltpu.roll(t, HD // 2, 1))
        o_ref[...] = ((t * c + sign * (rot * sn)) * scale).astype(BF16)

    return pl.pallas_call(
        body, name=name, grid=(S // TR,),
        in_specs=[pl.BlockSpec((TR, width), lambda i: (i, col_block)),
                  pl.BlockSpec((TR, 128), lambda i: (i, 0)), pl.BlockSpec((TR, 128), lambda i: (i, 0))],
        out_specs=pl.BlockSpec((TR, width), lambda i: (i, 0)),
        out_shape=jax.ShapeDtypeStruct((S, width), BF16), compiler_params=_cparams("parallel"),
    )(src, cos, sin)


QROWS = NQ_PER_KV * 128


NBIAS = NCH + 1
KV_PER_STEP = 2


def _bias_table():
    db = lax.broadcasted_iota(jnp.int32, (NBIAS, 128, QROWS), 0) - 1
    ki = lax.broadcasted_iota(jnp.int32, (NBIAS, 128, QROWS), 1)
    qi = lax.broadcasted_iota(jnp.int32, (NBIAS, 128, QROWS), 2) & 127
    d = db * 128 + qi - ki
    cnt = ((d <= 128).astype(F32) + (((d & 3) == 0) & (d <= 512)).astype(F32) + ((d & 15) == 0).astype(F32))
    return jnp.where((d >= 0) & (cnt > 0.0), jnp.log(jnp.maximum(cnt, 1.0)), NEG)


def _qt_spec():
    return pl.BlockSpec((None, None, HD, QROWS), lambda j, i: (j, i, 0, 0))


def _stat_spec():
    return pl.BlockSpec((None, None, 1, QROWS), lambda j, i: (j, i, 0, 0))


def _attn_fwd(qt, kh, vt, bias, rider=None):
    def body(q_ref, k_ref, v_ref, b_ref, o_ref, lse_ref):
        qb = pl.program_id(1)

        def keys(carry, off, size, bias_):
            out = []
            for h in range(KV_PER_STEP):
                m, l, acc = carry[3 * h:3 * h + 3]
                s = _dot(k_ref[h, pl.ds(off, size), :], q_ref[h], NN) + bias_
                m_new = jnp.maximum(m, jnp.max(s, axis=0, keepdims=True))
                p = jnp.exp(s - m_new)
                a = jnp.exp(m - m_new)
                out += [m_new, a * l + jnp.sum(p, axis=0, keepdims=True),
                        a * acc + _dot(v_ref[h, :, pl.ds(off, size)], p, NN)]
            return tuple(out)

        def pair(i, carry):
            bias2 = jnp.concatenate([b_ref[qb - 2 * i + 1], b_ref[qb - 2 * i]], axis=0)
            return keys(carry, pl.multiple_of(i * 256, 256), 256, bias2)

        init = (jnp.full((1, QROWS), NEG, F32), jnp.zeros((1, QROWS), F32), jnp.zeros((HD, QROWS), F32))
        res = lax.fori_loop(0, (qb + 1) // 2, pair, init * KV_PER_STEP)
        res = lax.cond(qb % 2 == 0,
                       lambda c: keys(c, pl.multiple_of(qb * 128, 128), 128, b_ref[1]), lambda c: c, res)
        for h in range(KV_PER_STEP):
            m, l, acc = res[3 * h:3 * h + 3]
            o_ref[h] = acc / l
            lse_ref[h] = m + jnp.log(l)

    kvs = KV_PER_STEP
    qspec = pl.BlockSpec((kvs, None, HD, QROWS), lambda j, i: (j, i, 0, 0))
    return _call(
        body, "attn_fwd", (NKV // kvs, NCH),
        [qspec, pl.BlockSpec((kvs, S, HD), lambda j, i: (j, 0, 0)),
         pl.BlockSpec((kvs, HD, S), lambda j, i: (j, 0, 0)),
         pl.BlockSpec((NBIAS, 128, QROWS), lambda j, i: (0, 0, 0))],
        [qspec, pl.BlockSpec((kvs, None, 1, QROWS), lambda j, i: (j, i, 0, 0))],
        [jax.ShapeDtypeStruct((NKV, NCH, HD, QROWS), F32), jax.ShapeDtypeStruct((NKV, NCH, 1, QROWS), F32)],
        (qt, kh, vt, bias), sem=("parallel", "parallel"), rider=rider)


def _attn_delta(ot, dot_):
    def body(o_ref, do_ref, dl_ref):
        dl_ref[...] = jnp.sum(o_ref[...] * do_ref[...].astype(F32), axis=1, keepdims=True)

    spec = pl.BlockSpec((None, NCH, HD, QROWS), lambda j: (j, 0, 0, 0))
    return pl.pallas_call(
        body, name="attn_delta", grid=(NKV,), in_specs=[spec, spec],
        out_specs=pl.BlockSpec((None, NCH, 1, QROWS), lambda j: (j, 0, 0, 0)),
        out_shape=jax.ShapeDtypeStruct((NKV, NCH, 1, QROWS), F32), compiler_params=_cparams("parallel"),
    )(ot, dot_)


def _attn_bwd(qt, kh, kt, vh, dot_, lse, delta, bias):
    def body(qt_ref, k_ref, kt_ref, v_ref, dot_ref, lse_ref, dl_ref, b_ref, dq_ref, dk_ref, dv_ref):
        kb = pl.program_id(1)

        @pl.when(kb == 0)
        def _():
            dq_ref[...] = jnp.zeros_like(dq_ref)

        def blocks(carry, qbs):
            out = list(carry)
            for h in range(KV_PER_STEP):
                k, kt_, v = k_ref[h], kt_ref[h], v_ref[h]
                for qb in qbs:
                    st = _dot(k, qt_ref[h, qb], NN) + b_ref[qb - kb + 1]
                    pt = jnp.exp(st - lse_ref[h, qb])
                    dst = pt * (_dot(v, dot_ref[h, qb], NN) - dl_ref[h, qb])
                    dq_ref[h, qb] += _dot(kt_, dst, NN)
                    out[2 * h] = out[2 * h] + _dot(dst, qt_ref[h, qb], NT)
                    out[2 * h + 1] = out[2 * h + 1] + _dot(pt, dot_ref[h, qb], NT)
            return tuple(out)

        res = (jnp.zeros((128, HD), F32),) * (2 * KV_PER_STEP)
        res = lax.cond(kb % 2 == 1, lambda c: blocks(c, (kb,)), lambda c: c, res)
        res = lax.fori_loop((kb + 1) // 2, NCH // 2, lambda j, c: blocks(c, (2 * j, 2 * j + 1)), res)
        for h in range(KV_PER_STEP):
            dk_ref[h] = res[2 * h]
            dv_ref[h] = res[2 * h + 1]

    kvs = KV_PER_STEP
    tspec = pl.BlockSpec((kvs, NCH, HD, QROWS), lambda j, i: (j, 0, 0, 0))
    kspec = pl.BlockSpec((kvs, 128, HD), lambda j, i: (j, i, 0))
    sspec = pl.BlockSpec((kvs, NCH, 1, QROWS), lambda j, i: (j, 0, 0, 0))
    return pl.pallas_call(
        body, name="attn_bwd", grid=(NKV // kvs, NCH),
        in_specs=[tspec, kspec, pl.BlockSpec((kvs, HD, 128), lambda j, i: (j, 0, i)), kspec, tspec,
                  sspec, sspec, pl.BlockSpec((NBIAS, 128, QROWS), lambda j, i: (0, 0, 0))],
        out_specs=[tspec, kspec, kspec],
        out_shape=[jax.ShapeDtypeStruct((NKV, NCH, HD, QROWS), F32),
                   jax.ShapeDtypeStruct((NKV, S, HD), F32), jax.ShapeDtypeStruct((NKV, S, HD), F32)],
        compiler_params=_cparams("parallel", "arbitrary"),
    )(qt, kh, kt, vh, dot_, lse, delta, bias)


CONV_BLK = 256
CONV_COL0 = 1536 // CONV_BLK


def _shift_down(u, j, row):
    return jnp.where(row >= j, pltpu.roll(u, j, 0), 0.0)


def _conv_pre(u, w_ref, b_ref, row):
    y = b_ref[...] + w_ref[CONV_K - 1:CONV_K, :] * u
    for j in range(1, CONV_K):
        y = y + w_ref[CONV_K - 1 - j:CONV_K - j, :] * _shift_down(u, j, row)
    return y


def _conv_fwd(proj, convw, convb):
    def body(u_ref, w_ref, b_ref, o_ref):
        u = u_ref[...]
        row = lax.broadcasted_iota(jnp.int32, u.shape, 0)
        y = _conv_pre(u, w_ref, b_ref, row)
        o_ref[...] = y * _sigmoid(y)

    return pl.pallas_call(
        body, name="conv_fwd", grid=(CONV_C // CONV_BLK,),
        in_specs=[pl.BlockSpec((S, CONV_BLK), lambda i: (0, CONV_COL0 + i)),
                  pl.BlockSpec((CONV_K, CONV_BLK), lambda i: (0, i)),
                  pl.BlockSpec((1, CONV_BLK), lambda i: (0, i))],
        out_specs=pl.BlockSpec((S, CONV_BLK), lambda i: (0, i)),
        out_shape=jax.ShapeDtypeStruct((S, CONV_C), F32), compiler_params=_cparams("parallel"),
    )(proj, convw, convb)


def _conv_bwd(dact, proj, convw, convb):
    def body(da_ref, u_ref, w_ref, b_ref, du_ref, dw_ref, db_ref):
        u = u_ref[...]
        row = lax.broadcasted_iota(jnp.int32, u.shape, 0)
        y = _conv_pre(u, w_ref, b_ref, row)
        sg = _sigmoid(y)
        dy = da_ref[...] * (sg * (1.0 + y * (1.0 - sg)))
        db_ref[...] = jnp.sum(dy, axis=0, keepdims=True)
        du = w_ref[CONV_K - 1:CONV_K, :] * dy
        r8 = lax.broadcasted_iota(jnp.int32, (8, CONV_BLK), 0)
        dw = jnp.where(r8 == CONV_K - 1, jnp.sum(dy * u, axis=0, keepdims=True), 0.0)
        for j in range(1, CONV_K):
            du = du + w_ref[CONV_K - 1 - j:CONV_K - j, :] * jnp.where(row < S - j, pltpu.roll(dy, S - j, 0), 0.0)
            dw = dw + jnp.where(r8 == CONV_K - 1 - j,
                                jnp.sum(dy * _shift_down(u, j, row), axis=0, keepdims=True), 0.0)
        du_ref[...] = du.astype(BF16)
        dw_ref[...] = dw

    return pl.pallas_call(
        body, name="conv_bwd", grid=(CONV_C // CONV_BLK,),
        in_specs=[pl.BlockSpec((S, CONV_BLK), lambda i: (0, i)),
                  pl.BlockSpec((S, CONV_BLK), lambda i: (0, CONV_COL0 + i)),
                  pl.BlockSpec((CONV_K, CONV_BLK), lambda i: (0, i)),
                  pl.BlockSpec((1, CONV_BLK), lambda i: (0, i))],
        out_specs=[pl.BlockSpec((S, CONV_BLK), lambda i: (0, i)), pl.BlockSpec((8, CONV_BLK), lambda i: (0, i)),
                   pl.BlockSpec((1, CONV_BLK), lambda i: (0, i))],
        out_shape=[jax.ShapeDtypeStruct((S, CONV_C), BF16), jax.ShapeDtypeStruct((8, CONV_C), F32),
                   jax.ShapeDtypeStruct((1, CONV_C), F32)],
        compiler_params=_cparams("parallel"),
    )(dact, proj, convw, convb)


NPAIR = 8


def _ssd_scalars(dtr_ref, dtb_ref, alog_ref):
    z = dtr_ref[...] + dtb_ref[...]
    dt = jnp.maximum(z, 0.0) + jnp.log(1.0 + jnp.exp(-jnp.abs(z)))
    a = -jnp.exp(alog_ref[...])
    r = lax.broadcasted_iota(jnp.int32, (128, 128), 0)
    c = lax.broadcasted_iota(jnp.int32, (128, 128), 1)
    tri = (r >= c).astype(F32)
    cs = _dot_exact(tri, dt * a)
    return z, dt, a, cs, r, c


def _pair_terms(cs, cst, dt, h1, h2, lo):
    c1, c2 = cs[:, h1:h1 + 1], cs[:, h2:h2 + 1]
    l1, l2 = cs[127:128, h1:h1 + 1], cs[127:128, h2:h2 + 1]
    e_l = jnp.where(lo, jnp.exp(c1), jnp.exp(c2))
    dte1, dte2 = jnp.exp(l1 - c1), jnp.exp(l2 - c2)
    dte_l = jnp.where(lo, dte1, dte2)
    dt_l = jnp.where(lo, dt[:, h1:h1 + 1], dt[:, h2:h2 + 1])
    return c1, c2, jnp.exp(l1), jnp.exp(l2), e_l, dte1, dte2, dte_l, dt_l


def _gate_norm(y, zv, w):
    yg = y * (zv * _sigmoid(zv))
    outs, rs = [], []
    for g in range(2):
        blk = yg[:, 512 * g:512 * (g + 1)]
        r = lax.rsqrt(jnp.mean(blk * blk, axis=-1, keepdims=True) + EPS)
        outs.append(blk * r)
        rs.append(r)
    return jnp.concatenate(outs, axis=1), rs, yg


def _ssd_fwd(xbc, proj, dtb, alog, dskip_l, ssmw):
    def body(x_ref, b_ref, c_ref, dtr_ref, z_ref, dtb_ref, alog_ref, dsk_ref, w_ref, y_ref, yn_ref, hp_ref, h_ref):
        @pl.when(pl.program_id(0) == 0)
        def _():
            h_ref[...] = jnp.zeros_like(h_ref)

        _, dt, _, cs, r, c = _ssd_scalars(dtr_ref, dtb_ref, alog_ref)
        cst = cs.T
        causal = r >= c
        lo = c < HD
        hp_ref[...] = h_ref[...]
        for g in range(2):
            bg = b_ref[:, 128 * g:128 * (g + 1)]
            cg = c_ref[:, 128 * g:128 * (g + 1)]
            cb = _dot(cg, bg, NT)
            for j in range(4):
                pj = 4 * g + j
                h1, h2 = 2 * pj, 2 * pj + 1
                sl = slice(128 * pj, 128 * (pj + 1))
                xp = x_ref[:, sl]
                c1, c2, cd1, cd2, e_l, _, _, dte_l, dt_l = _pair_terms(cs, cst, dt, h1, h2, lo)
                xdt = xp * dt_l
                m1 = cb * jnp.exp(jnp.where(causal, c1 - cst[h1:h1 + 1, :], NEG))
                m2 = cb * jnp.exp(jnp.where(causal, c2 - cst[h2:h2 + 1, :], NEG))
                yd = jnp.where(lo, _dot(m1, xdt, NN), _dot(m2, xdt, NN))
                hp = h_ref[pj]
                yo = _dot(cg, hp, NT) * e_l
                st = _dot(xdt * dte_l, bg, TN)
                h_ref[pj] = hp * jnp.where(r < HD, cd1, cd2) + st
                y_ref[:, sl] = yd + yo + dsk_ref[:, sl] * xp
        yn, _, _ = _gate_norm(y_ref[...], z_ref[...], w_ref[...])
        yn_ref[...] = (yn * w_ref[...]).astype(BF16)

    return pl.pallas_call(
        body, name="ssd_fwd", grid=(NCH,),
        in_specs=[pl.BlockSpec((128, SSM_W), lambda i: (i, 0)),
                  pl.BlockSpec((128, 256), lambda i: (i, 4)), pl.BlockSpec((128, 256), lambda i: (i, 5)),
                  pl.BlockSpec((128, 128), lambda i: (i, COL_DT // 128)),
                  pl.BlockSpec((128, SSM_W), lambda i: (i, 3)),
                  pl.BlockSpec((1, 128), lambda i: (0, 0)), pl.BlockSpec((1, 128), lambda i: (0, 0)),
                  pl.BlockSpec((1, SSM_W), lambda i: (0, 0)), pl.BlockSpec((1, SSM_W), lambda i: (0, 0))],
        out_specs=[pl.BlockSpec((128, SSM_W), lambda i: (i, 0)), pl.BlockSpec((128, SSM_W), lambda i: (i, 0)),
                   pl.BlockSpec((None, NPAIR, 128, 128), lambda i: (i, 0, 0, 0))],
        out_shape=[jax.ShapeDtypeStruct((S, SSM_W), F32), jax.ShapeDtypeStruct((S, SSM_W), BF16),
                   jax.ShapeDtypeStruct((NCH, NPAIR, 128, 128), F32)],
        scratch_shapes=[pltpu.VMEM((NPAIR, 128, 128), F32)],
        compiler_params=_cparams("arbitrary"),
    )(xbc, xbc, xbc, proj, proj, dtb, alog, dskip_l, ssmw)


def _ssd_bwd(dmixed, y, xbc, proj, hprev, dtb, alog, dskip_l, ssmw, rider=None):
    def body(dyn_ref, y_ref, x_ref, b_ref, c_ref, dtr_ref, z_ref, hp_ref, dtb_ref, alog_ref, dsk_ref, w_ref,
             dxbc_ref, dz_ref, ddt_ref, dw_ref, dsc_ref, g_ref):
        @pl.when(pl.program_id(0) == 0)
        def _():
            g_ref[...] = jnp.zeros_like(g_ref)
            dsc_ref[...] = jnp.zeros_like(dsc_ref)

        z, dt, a, cs, r, c = _ssd_scalars(dtr_ref, dtb_ref, alog_ref)
        cst = cs.T
        causal = r >= c
        lo = c < HD

        yv = y_ref[...]
        zv = z_ref[...]
        wv = w_ref[...]
        ygn, rs, yg = _gate_norm(yv, zv, wv)
        dyn = dyn_ref[...]
        _acc_rows(dw_ref, dyn * ygn)
        dynw = dyn * wv
        parts = []
        for g in range(2):
            sl = slice(512 * g, 512 * (g + 1))
            a_g, n_g = dynw[:, sl], ygn[:, sl]
            parts.append(rs[g] * (a_g - n_g * jnp.mean(a_g * n_g, axis=-1, keepdims=True)))
        dyg = jnp.concatenate(parts, axis=1)
        sz = _sigmoid(zv)
        dz_ref[...] = (dyg * yv * (sz * (1.0 + zv * (1.0 - sz)))).astype(BF16)
        dy_all = dyg * (zv * sz)

        dcs_cols = jnp.zeros((128, 128), F32)
        dcs_rows = jnp.zeros((128, 128), F32)
        ddt_x = jnp.zeros((128, 128), F32)
        dd_row = jnp.zeros((1, 128), F32)
        last = r == 127
        x_all, b_all, c_all, dsk_all = x_ref[...], b_ref[...], c_ref[...], dsk_ref[...]
        hp_all, g_all = hp_ref[...], g_ref[...]
        g_new, dx_parts, db_parts, dc_parts = [], [], [], []
        for g in range(2):
            bg = b_all[:, 128 * g:128 * (g + 1)]
            cg = c_all[:, 128 * g:128 * (g + 1)]
            cb = _dot(cg, bg, NT)
            dcb = jnp.zeros((128, 128), F32)
            db_acc = jnp.zeros((128, NST), F32)
            dc_acc = jnp.zeros((128, NST), F32)
            for j in range(4):
                pj = 4 * g + j
                h1, h2 = 2 * pj, 2 * pj + 1
                sl = slice(128 * pj, 128 * (pj + 1))
                xp = x_all[:, sl]
                dyp = dy_all[:, sl]
                c1, c2, cd1, cd2, e_l, dte1, dte2, dte_l, dt_l = _pair_terms(cs, cst, dt, h1, h2, lo)
                xdt = xp * dt_l
                hp = hp_all[pj]
                gp = g_all[pj]
                dxp = dsk_all[:, sl] * dyp
                dyx = dyp * xp
                dd_row = dd_row + jnp.where(c[0:1, :] == h1, jnp.sum(jnp.where(lo, dyx, 0.0), keepdims=True), 0.0) \
                    + jnp.where(c[0:1, :] == h2, jnp.sum(jnp.where(lo, 0.0, dyx), keepdims=True), 0.0)
                dzs = dyp * e_l
                dc_acc = dc_acc + _dot(dzs, hp, NN)
                g_from = _dot(dzs, cg, TN)
                ryo = dyp * (_dot(cg, hp, NT) * e_l)
                k1 = jnp.sum(jnp.where(lo, ryo, 0.0), axis=1, keepdims=True)
                k2 = jnp.sum(jnp.where(lo, 0.0, ryo), axis=1, keepdims=True)
                qm = _dot(bg, gp, NT)
                dxdt = qm * dte_l
                qx = qm * xdt
                t1 = jnp.sum(jnp.where(lo, qx, 0.0), axis=1, keepdims=True) * dte1
                t2 = jnp.sum(jnp.where(lo, 0.0, qx), axis=1, keepdims=True) * dte2
                db_acc = db_acc + _dot(xdt * dte_l, gp, NN)
                gh = gp * hp
                dl1 = jnp.sum(t1, keepdims=True) + jnp.sum(jnp.where(r < HD, gh, 0.0), keepdims=True) * cd1
                dl2 = jnp.sum(t2, keepdims=True) + jnp.sum(jnp.where(r < HD, 0.0, gh), keepdims=True) * cd2
                g_new.append(g_from + jnp.where(r < HD, cd1, cd2) * gp)
                k1 = k1 - t1 + jnp.where(last[:, 0:1], dl1, 0.0)
                k2 = k2 - t2 + jnp.where(last[:, 0:1], dl2, 0.0)
                for hh, ch, msk in ((h1, c1, lo), (h2, c2, jnp.logical_not(lo))):
                    lm = jnp.exp(jnp.where(causal, ch - cst[hh:hh + 1, :], NEG))
                    mm = cb * lm
                    dm = jnp.where(causal, _dot(jnp.where(msk, dyp, 0.0), xdt, NT), 0.0)
                    w = dm * mm
                    kk = jnp.sum(w, axis=1, keepdims=True)
                    if hh == h1:
                        k1 = k1 + kk
                    else:
                        k2 = k2 + kk
                    dcs_rows = dcs_rows + jnp.where(r == hh, jnp.sum(w, axis=0, keepdims=True), 0.0)
                    dcb = dcb + dm * lm
                    dxdt = dxdt + jnp.where(msk, _dot(mm, dyp, TN), 0.0)
                dcs_cols = dcs_cols + jnp.where(c == h1, k1, 0.0) + jnp.where(c == h2, k2, 0.0)
                dxx = dxdt * xp
                ddt_x = ddt_x + jnp.where(c == h1, jnp.sum(jnp.where(lo, dxx, 0.0), axis=1, keepdims=True), 0.0) \
                    + jnp.where(c == h2, jnp.sum(jnp.where(lo, 0.0, dxx), axis=1, keepdims=True), 0.0)
                dx_parts.append(dxp + dxdt * dt_l)
            db_parts.append(db_acc + _dot(dcb, cg, TN))
            dc_parts.append(dc_acc + _dot(dcb, bg, NN))
        g_ref[...] = jnp.stack(g_new)
        dxbc_ref[...] = jnp.concatenate(dx_parts + db_parts + dc_parts, axis=1)

        dcs = dcs_cols - dcs_rows.T
        dad = _dot_exact((c >= r).astype(F32), dcs)
        ddt = dad * a + ddt_x
        ddtr = jnp.where(c < 16, ddt * _sigmoid(z), 0.0)
        ddt_ref[...] = ddtr.astype(BF16)
        r8 = lax.broadcasted_iota(jnp.int32, (8, 128), 0)
        dsc_ref[...] += (jnp.where(r8 == 0, jnp.sum(ddtr, axis=0, keepdims=True), 0.0)
                         + jnp.where(r8 == 1, jnp.sum(dad * dt, axis=0, keepdims=True) * a, 0.0)
                         + jnp.where(r8 == 2, dd_row, 0.0))

    rev = NCH - 1
    return _call(
        body, "ssd_bwd", (NCH,),
        [pl.BlockSpec((128, SSM_W), lambda i: (rev - i, 1)),
         pl.BlockSpec((128, SSM_W), lambda i: (rev - i, 0)),
         pl.BlockSpec((128, SSM_W), lambda i: (rev - i, 0)),
         pl.BlockSpec((128, 256), lambda i: (rev - i, 4)), pl.BlockSpec((128, 256), lambda i: (rev - i, 5)),
         pl.BlockSpec((128, 128), lambda i: (rev - i, COL_DT // 128)),
         pl.BlockSpec((128, SSM_W), lambda i: (rev - i, 3)),
         pl.BlockSpec((None, NPAIR, 128, 128), lambda i: (rev - i, 0, 0, 0)),
         pl.BlockSpec((1, 128), lambda i: (0, 0)), pl.BlockSpec((1, 128), lambda i: (0, 0)),
         pl.BlockSpec((1, SSM_W), lambda i: (0, 0)), pl.BlockSpec((1, SSM_W), lambda i: (0, 0))],
        [pl.BlockSpec((128, CONV_C), lambda i: (rev - i, 0)),
         pl.BlockSpec((128, SSM_W), lambda i: (rev - i, 0)),
         pl.BlockSpec((128, 128), lambda i: (rev - i, 0)),
         pl.BlockSpec((1, SSM_W), lambda i: (0, 0)), pl.BlockSpec((8, 128), lambda i: (0, 0))],
        [jax.ShapeDtypeStruct((S, CONV_C), F32), jax.ShapeDtypeStruct((S, SSM_W), BF16),
         jax.ShapeDtypeStruct((S, 128), BF16), jax.ShapeDtypeStruct((1, SSM_W), F32),
         jax.ShapeDtypeStruct((8, 128), F32)],
        (dmixed, y, xbc, xbc, xbc, proj, proj, hprev, dtb, alog, dskip_l, ssmw),
        [pltpu.VMEM((NPAIR, 128, 128), F32)], ("arbitrary",), rider)


def _cast_stack(name, slot, arrs, tr, tc):
    n = len(arrs)
    rows, cols = arrs[0].shape

    def body(s_ref, *refs):
        for i in range(n):
            refs[n][i] = refs[i][...].astype(BF16)

    return pl.pallas_call(
        body, name=name,
        grid_spec=pltpu.PrefetchScalarGridSpec(
            num_scalar_prefetch=1, grid=(rows // tr, cols // tc),
            in_specs=[pl.BlockSpec((tr, tc), lambda i, j, sr: (i, j))] * n,
            out_specs=pl.BlockSpec((None, n, tr, tc), lambda i, j, sr: (sr[0], 0, i, j))),
        out_shape=jax.ShapeDtypeStruct((NSH, n, rows, cols), BF16),
        compiler_params=_cparams("parallel", "parallel"),
    )(slot, *arrs)


def _pair_sum(name, c_idx, ps, rs, th):
    n = len(ps)
    _, rows, _ = ps[0].shape

    def body(c_ref, *refs):
        for i in range(n):
            refs[2 * n + i][...] = (refs[i][...].astype(F32) + refs[n + i][...].astype(F32)).astype(BF16)

    spec = pl.BlockSpec((None, th, HALF), lambda s, i, cr: (s, i, 0))
    return pl.pallas_call(
        body, name=name,
        grid_spec=pltpu.PrefetchScalarGridSpec(
            num_scalar_prefetch=1, grid=(NSH, rows // th),
            in_specs=[pl.BlockSpec((None, th, HALF), lambda s, i, cr: (s, i, cr[0]))] * n + [spec] * n,
            out_specs=[spec] * n),
        out_shape=[jax.ShapeDtypeStruct((NSH, rows, HALF), BF16)] * n,
        compiler_params=_cparams("parallel", "parallel"),
    )(c_idx, *ps, *rs)


def _chip_sum(name, place, cs, ts, th):
    n = len(ts)
    _, rows, _ = ts[0].shape

    def body(p_ref, *refs):
        for i in range(n):
            t = refs[n + i][...].astype(F32)
            refs[2 * n + i][...] = ((refs[i][...].astype(F32) + t[0]) + t[1]) + t[2]

    return pl.pallas_call(
        body, name=name,
        grid_spec=pltpu.PrefetchScalarGridSpec(
            num_scalar_prefetch=1, grid=(rows // th,),
            in_specs=[pl.BlockSpec((None, th, HALF), lambda i, pr: (pr[0], i, 0))] * n
            + [pl.BlockSpec((3, th, HALF), lambda i, pr: (0, i, 0))] * n,
            out_specs=[pl.BlockSpec((th, HALF), lambda i, pr: (i, pr[1]))] * n),
        out_shape=[jax.ShapeDtypeStruct((rows, D), F32)] * n, compiler_params=_cparams("parallel"),
    )(place, *cs, *ts)


def _adamw(name, ws, gs, ms, vs, tr, tc):
    n = len(ws)
    shape = ws[0].shape
    rows, cols, mid = shape[0], shape[-1], shape[1:-1]
    c1 = 1.0 / (1.0 - ADAM_B1 ** ADAM_STEP)
    c2 = 1.0 / (1.0 - ADAM_B2 ** ADAM_STEP)

    def body(*refs):
        for i in range(n):
            w, g, m, v = (refs[k * n + i][...] for k in range(4))
            m2 = ADAM_B1 * m + (1.0 - ADAM_B1) * g
            v2 = ADAM_B2 * v + (1.0 - ADAM_B2) * (g * g)
            refs[4 * n + 3 * i][...] = -ADAM_LR * ((m2 * c1) / (jnp.sqrt(v2 * c2) + ADAM_EPS) + ADAM_WD * w)
            refs[4 * n + 3 * i + 1][...] = m2
            refs[4 * n + 3 * i + 2][...] = v2

    spec = pl.BlockSpec((tr,) + mid + (tc,), lambda i, j: (i,) + (0,) * len(mid) + (j,))
    outs = pl.pallas_call(
        body, name=name, grid=(rows // tr, cols // tc), in_specs=[spec] * (4 * n), out_specs=[spec] * (3 * n),
        out_shape=[jax.ShapeDtypeStruct(shape, F32)] * (3 * n),
        compiler_params=_cparams("parallel", "parallel"),
    )(*ws, *gs, *ms, *vs)
    return [tuple(outs[3 * i:3 * i + 3]) for i in range(n)]


def _place():
    x, y, c = lax.axis_index("x"), lax.axis_index("y"), lax.axis_index("c")
    chips = [(1 - x, y), (x, 1 - y), (1 - x, 1 - y)]
    return x, y, c, chips


def _any_specs(n):
    return [pl.BlockSpec(memory_space=pl.ANY)] * n


def _rcopy(src, dst, send_sem, recv_sem, dev):
    return pltpu.make_async_remote_copy(src_ref=src, dst_ref=dst, send_sem=send_sem, recv_sem=recv_sem,
                                        device_id=dev, device_id_type=MESH)


def _gather_rider(bufs, views):
    n = len(bufs)

    def start(rin, rout, sems):
        send, recv = sems[0], sems[1]
        x, y, c, chips = _place()
        for j, chip in enumerate(chips):
            for b in range(n):
                mine = views[b](rout[b], 2 * x + y, c)
                _rcopy(mine, mine, send.at[j * n + b], recv.at[j * n + b], (chip[0], chip[1], c)).start()

    def finish(rin, rout, sems):
        send, recv, fsend, frecv = sems
        x, y, c, chips = _place()
        passed = []
        for j, chip in enumerate(chips):
            for b in range(n):
                landed = views[b](rout[b], 2 * chip[0] + chip[1], c)
                _rcopy(landed, landed, send.at[j * n + b], recv.at[j * n + b], (x, y, c)).wait_recv()
                fw = _rcopy(landed, landed, fsend.at[j * n + b], frecv.at[j * n + b], (x, y, 1 - c))
                fw.start()
                passed.append(fw)
        for j, chip in enumerate(chips):
            for b in range(n):
                other = views[b](rout[b], 2 * chip[0] + chip[1], 1 - c)
                _rcopy(other, other, fsend.at[j * n + b], frecv.at[j * n + b], (x, y, c)).wait_recv()
        for j, chip in enumerate(chips):
            for b in range(n):
                mine = views[b](rout[b], 2 * x + y, c)
                _rcopy(mine, mine, send.at[j * n + b], recv.at[j * n + b], (x, y, c)).wait_send()
        for fw in passed:
            fw.wait_send()

    return _Rider(list(bufs), [jax.ShapeDtypeStruct(a.shape, a.dtype) for a in bufs], {b: b for b in range(n)},
                  [pltpu.SemaphoreType.DMA((3 * n,))] * 4, start, finish)


def _small_gather_rider(cw):
    def descs(rin, rout, sems, x, y, c, chips):
        return [_rcopy(rin[0], rout[0].at[2 * x + y], sems[1].at[j], sems[2].at[j], (chip[0], chip[1], c))
                for j, chip in enumerate(chips)]

    def start(rin, rout, sems):
        x, y, c, chips = _place()
        pltpu.make_async_copy(rin[0], rout[0].at[2 * x + y], sems[0].at[0]).start()
        for cp in descs(rin, rout, sems, x, y, c, chips):
            cp.start()

    def finish(rin, rout, sems):
        x, y, c, chips = _place()
        for j, chip in enumerate(chips):
            _rcopy(rin[0], rout[0].at[2 * chip[0] + chip[1]], sems[1].at[j], sems[2].at[j], (x, y, c)).wait_recv()
        for cp in descs(rin, rout, sems, x, y, c, chips):
            cp.wait_send()
        pltpu.make_async_copy(rin[0], rout[0].at[2 * x + y], sems[0].at[0]).wait()

    return _Rider([cw], [jax.ShapeDtypeStruct((NSH,) + cw.shape, cw.dtype)], {},
                  [pltpu.SemaphoreType.DMA((1,)), pltpu.SemaphoreType.DMA((3,)), pltpu.SemaphoreType.DMA((3,))],
                  start, finish)


def _to_chips_rider(cs):
    n = len(cs)

    def descs(rin, rout, sems):
        x, y, c, chips = _place()
        return [_rcopy(rin[i].at[2 * chip[0] + chip[1]], rout[i].at[j], sems[0].at[j * n + i], sems[1].at[j * n + i],
                       (chip[0], chip[1], c)) for j, chip in enumerate(chips) for i in range(n)]

    def start(rin, rout, sems):
        for cp in descs(rin, rout, sems):
            cp.start()

    def finish(rin, rout, sems):
        for cp in descs(rin, rout, sems):
            cp.wait()

    return _Rider(list(cs), [jax.ShapeDtypeStruct((3,) + a.shape[1:], a.dtype) for a in cs], {},
                  [pltpu.SemaphoreType.DMA((3 * n,))] * 2, start, finish)


def _run_riders(name, riders):
    n_in = [len(r.operands) for r in riders]
    n_out = [len(r.out_shapes) for r in riders]
    n_sem = [len(r.sems) for r in riders]

    def body(*refs):
        parts, at = [], 0
        for counts in (n_in, n_out, n_sem):
            group = []
            for k in counts:
                group.append(refs[at:at + k])
                at += k
            parts.append(group)
        for i, r in enumerate(riders):
            r.start(parts[0][i], parts[1][i], parts[2][i])
        for i, r in enumerate(riders):
            r.finish(parts[0][i], parts[1][i], parts[2][i])

    aliases = {}
    for i, r in enumerate(riders):
        for k, v in r.aliases.items():
            aliases[sum(n_in[:i]) + k] = sum(n_out[:i]) + v
    res = pl.pallas_call(
        body, name=name, in_specs=_any_specs(sum(n_in)), out_specs=_any_specs(sum(n_out)),
        out_shape=[s for r in riders for s in r.out_shapes], input_output_aliases=aliases,
        scratch_shapes=[s for r in riders for s in r.sems],
    )(*[a for r in riders for a in r.operands])
    out, at = [], 0
    for k in n_out:
        out.append(list(res[at:at + k]))
        at += k
    return out


def _to_sibling(name, ps):
    n = len(ps)

    def body(*refs):
        src, dst, send, recv = refs[:n], refs[n:2 * n], refs[2 * n], refs[2 * n + 1]
        x, y, c, _ = _place()
        cps = [pltpu.make_async_remote_copy(
            src_ref=src[i].at[:, :, pl.ds((1 - c) * HALF, HALF)], dst_ref=dst[i], send_sem=send.at[i],
            recv_sem=recv.at[i], device_id=(x, y, 1 - c), device_id_type=MESH) for i in range(n)]
        for cp in cps:
            cp.start()
        for cp in cps:
            cp.wait()

    outs = [jax.ShapeDtypeStruct(p.shape[:2] + (HALF,), p.dtype) for p in ps]
    return pl.pallas_call(
        body, name=name, in_specs=_any_specs(n), out_specs=_any_specs(n), out_shape=outs,
        scratch_shapes=[pltpu.SemaphoreType.DMA((n,)), pltpu.SemaphoreType.DMA((n,))],
    )(*ps)


def _swap_halves(gs):
    n = len(gs)

    def body(*refs):
        dst, send, recv = refs[n:2 * n], refs[2 * n], refs[2 * n + 1]
        x, y, c, _ = _place()
        cps = []
        for i in range(n):
            mine = dst[i].at[:, pl.ds(c * HALF, HALF)]
            cps.append(pltpu.make_async_remote_copy(
                src_ref=mine, dst_ref=mine, send_sem=send.at[i], recv_sem=recv.at[i],
                device_id=(x, y, 1 - c), device_id_type=MESH))
        for cp in cps:
            cp.start()
        for i in range(n):
            other = dst[i].at[:, pl.ds((1 - c) * HALF, HALF)]
            pltpu.make_async_remote_copy(
                src_ref=other, dst_ref=other, send_sem=send.at[i], recv_sem=recv.at[i],
                device_id=(x, y, c), device_id_type=MESH).wait_recv()
        for cp in cps:
            cp.wait_send()

    return pl.pallas_call(
        body, name="grads_swap_halves", in_specs=_any_specs(n), out_specs=_any_specs(n),
        out_shape=[jax.ShapeDtypeStruct(g.shape, g.dtype) for g in gs],
        input_output_aliases={i: i for i in range(n)},
        scratch_shapes=[pltpu.SemaphoreType.DMA((n,)), pltpu.SemaphoreType.DMA((n,))],
    )(*gs)


SMALL_ROWS = 16


def _allreduce_small(vec):
    def body(v_ref, o_ref, buf, send, recv):
        x, y, c, _ = _place()
        me = 4 * x + 2 * y + c
        buf[me] = v_ref[...]
        cps = []
        for k in range(1, 8):
            peer = (x ^ (k >> 2), y ^ ((k >> 1) & 1), c ^ (k & 1))
            cps.append(pltpu.make_async_remote_copy(
                src_ref=v_ref, dst_ref=buf.at[me], send_sem=send.at[k - 1], recv_sem=recv.at[k - 1],
                device_id=peer, device_id_type=MESH))
        for cp in cps:
            cp.start()
        for k in range(1, 8):
            pltpu.make_async_remote_copy(
                src_ref=v_ref, dst_ref=buf.at[me ^ k], send_sem=send.at[k - 1], recv_sem=recv.at[k - 1],
                device_id=(x, y, c), device_id_type=MESH).wait_recv()
        for cp in cps:
            cp.wait_send()
        t = buf[0]
        for d in range(1, 8):
            t = t + buf[d]
        o_ref[...] = t

    return pl.pallas_call(
        body, name="allreduce_small",
        in_specs=[pl.BlockSpec(memory_space=pltpu.VMEM)], out_specs=pl.BlockSpec(memory_space=pltpu.VMEM),
        out_shape=jax.ShapeDtypeStruct((SMALL_ROWS, D), F32),
        scratch_shapes=[pltpu.VMEM((8, SMALL_ROWS, D), F32), pltpu.SemaphoreType.DMA((7,)),
                        pltpu.SemaphoreType.DMA((7,))],
    )(vec)


def _col_half(ref, slot, hc):
    return ref.at[slot, :, pl.ds(hc * HALF, HALF)]


def _ffn_half(first, ref, slot, hc):
    return ref.at[slot, pl.ds(first, 3), :, pl.ds(hc * HALF, HALF)]


def _row_tile(rows):
    for t in range(512, 15, -16):
        if rows % t == 0:
            return t
    return rows


def _same_shape_runs(arrs):
    runs, a = [], 0
    for b in range(1, len(arrs) + 1):
        if b == len(arrs) or arrs[b].shape != arrs[a].shape:
            runs.append((a, b))
            a = b
    return runs


class _Comm:
    def __init__(self):
        x, y, c = lax.axis_index("x"), lax.axis_index("y"), lax.axis_index("c")
        self.c_idx = jnp.reshape(c, (1,)).astype(jnp.int32)
        self.place = jnp.stack([2 * x + y, c]).astype(jnp.int32)
        self.groups = {}

    def gather_mixer(self, wint, wout):
        return _gather_rider([wint, wout], [_col_half, _col_half])

    def gather_ffn(self, ffnw, first):
        return _gather_rider([ffnw], [functools.partial(_ffn_half, first)])

    def reduce_rider(self, tag, names, ps):
        rs = _to_sibling("grads_to_sibling_" + tag, ps)
        csums = []
        for a, b in _same_shape_runs(ps):
            csums += _pair_sum("pair_sum_%s%d" % (tag, a), self.c_idx, ps[a:b], rs[a:b], _row_tile(ps[a].shape[1]))
        self.groups[tag] = [names, csums, None]
        return _to_chips_rider(csums)

    def landed(self, tag, ts):
        self.groups[tag][2] = ts

    def finish(self):
        names, halves = [], []
        for tag, (group_names, csums, ts) in self.groups.items():
            names += group_names
            for a, b in _same_shape_runs(csums):
                halves += _chip_sum("chip_sum_%s%d" % (tag, a), self.place, csums[a:b], ts[a:b],
                                    _row_tile(csums[a].shape[1]))
        return dict(zip(names, _swap_halves(halves)))


ROPE_THETA = 10000.0
SMALL_1K = ("ffn1_pre_norm", "ffn1_post_norm", "mix_pre_norm", "ssm_norm", "mix_post_norm",
            "ffn2_pre_norm", "ffn2_post_norm")
SMALL_16 = ("dt_bias", "a_log", "d_skip")
OFF_CONVB = 7 * D
OFF_16 = OFF_CONVB + CONV_C
OFF_CONVW = OFF_16 + 48
OFF_LOSS = OFF_CONVW + CONV_K * CONV_C
SMALL_LEN = SMALL_ROWS * D


def _sds(shape, dtype):
    return jax.ShapeDtypeStruct(shape, dtype)


def _ffn_down(name, act, ffnw, wi):
    return _mm(name, [act, ffnw], NN, (S // TS,),
               [pl.BlockSpec((NSH, TS, FS), lambda i: (0, i, 0)),
                pl.BlockSpec((NSH, None, FS, D), lambda i: (0, wi + 2, 0, 0))],
               pl.BlockSpec((TS, D), lambda i: (i, 0)), _sds((S, D), F32))


def _ridden(res, rider):
    return res if rider is not None else (res, None)


def _ffn_bwd(tag, dh, n, gate, up, act, ffnw, wi, dact_rider=None, dn_rider_of=None):
    (dgate, dup), got1 = _ridden(_ffn_dact(tag + "_dact", dh, ffnw, wi, gate, up, dact_rider), dact_rider)
    aspec = pl.BlockSpec((None, S, FS), lambda s: (s, 0, 0))
    nspec = pl.BlockSpec((S, D), lambda s: (0, 0))
    wspec = pl.BlockSpec((None, FS, D), lambda s: (s, 0, 0))
    dws = [_mm(tag + nm, [a, b], TN, (NSH,), [aspec, nspec], wspec, _sds((NSH, FS, D), BF16))
           for nm, a, b in (("_dwg", dgate, n), ("_dwu", dup, n), ("_dwd", act, dh))]
    dn_rider = dn_rider_of(dws) if dn_rider_of is not None else None
    a2 = pl.BlockSpec((NSH, TS, FS), lambda i: (0, i, 0))
    dn, got2 = _ridden(_mm(
        tag + "_dn", [dgate, ffnw, dup, ffnw], NN, (S // TS,),
        [a2, pl.BlockSpec((NSH, None, FS, D), lambda i: (0, wi, 0, 0)),
         a2, pl.BlockSpec((NSH, None, FS, D), lambda i: (0, wi + 1, 0, 0))],
        pl.BlockSpec((TS, D), lambda i: (i, 0)), _sds((S, D), F32), dn_rider), dn_rider)
    return dn, dws, got1, got2


def _heads(t, n):
    return t.reshape(S, n, HD).transpose(1, 0, 2)


def _unheads(t):
    return t.transpose(1, 0, 2).reshape(S, t.shape[0] * HD)


def _heads_t(t, n):
    return t.reshape(S, n, HD).transpose(1, 2, 0)


def _blocks5(t):
    return t.reshape(NCH, 128, NKV, NQ_PER_KV, HD)


def _to_blocks_t(t):
    return _blocks5(t).transpose(2, 0, 4, 3, 1).reshape(NKV, NCH, HD, QROWS)


def _from_blocks_t(t):
    return t.reshape(NKV, NCH, HD, NQ_PER_KV, 128).transpose(1, 4, 0, 3, 2).reshape(S, D)


def _pad128(v):
    return jnp.pad(v, ((0, 0), (0, 128 - v.shape[1])))


def _local_step(x, positions, tgt, sp, ffnw, wint, wout, convw, comm=None):
    inv_freq = ROPE_THETA ** (-jnp.arange(0, HD, 2, dtype=F32) / HD)
    ang = positions.astype(F32)[:, None] * inv_freq
    ang = jnp.concatenate([ang, ang, ang, ang], axis=-1)
    cos, sin = jnp.cos(ang), jnp.sin(ang)
    dtb, alog = _pad128(sp["dt_bias"]), _pad128(sp["a_log"])
    dskip_l = jnp.repeat(sp["d_skip"], HD, axis=1)
    convb = sp["conv_b"]

    n1 = _prenorm("prenorm1", x, sp["ffn1_pre_norm"])
    rider = comm.gather_mixer(wint, wout) if comm else None
    (gate1, up1, act1), got = _ridden(_ffn_up("ffn1_up", n1, ffnw, 0, rider), rider)
    if comm:
        wint, wout = got
    wint_pad = jnp.pad(wint.reshape(WIN_COLS, D), ((0, WIN_PAD - WIN_COLS), (0, 0)))
    wout = wout.reshape(2 * D, D)
    h1 = _ffn_down("ffn1_down", act1, ffnw, 0)
    x1, n2 = _postres("postres1", x, h1, sp["ffn1_post_norm"], 0.5, sp["mix_pre_norm"])

    pw = WIN_PAD // 3
    proj = _mm("in_proj", [n2, wint_pad], NT, (S // TS, 3),
               [pl.BlockSpec((TS, D), lambda i, j: (i, 0)), pl.BlockSpec((pw, D), lambda i, j: (j, 0))],
               pl.BlockSpec((TS, pw), lambda i, j: (i, j)), _sds((S, WIN_PAD), F32))
    q_rot = _rope("rope_q", proj, 0, D, cos, sin, 1.0, HD ** -0.5)
    k_rot = _rope("rope_k", proj, D // KVW, KVW, cos, sin, 1.0, 1.0)
    v_bf = proj[:, D + KVW:D + 2 * KVW].astype(BF16)
    qt, kh, vh = _to_blocks_t(q_rot), _heads(k_rot, NKV), _heads(v_bf, NKV)
    kt, vt = _heads_t(k_rot, NKV), _heads_t(v_bf, NKV)
    bias = _bias_table()
    rider = comm.gather_ffn(ffnw, 3) if comm else None
    (ot, lse), got = _ridden(_attn_fwd(qt, kh, vt, bias, rider), rider)
    if comm:
        ffnw, = got
    attn = _from_blocks_t(ot).astype(BF16)
    xbc = _conv_fwd(proj, convw, convb)
    y, yn, hprev = _ssd_fwd(xbc, proj, dtb, alog, dskip_l, sp["ssm_norm"])
    mixed = jnp.concatenate([attn, yn], axis=1)
    h2 = _mm("out_proj", [mixed, wout], NN, (S // TS,),
             [pl.BlockSpec((TS, 2 * D), lambda i: (i, 0)), pl.BlockSpec((2 * D, D), lambda i: (0, 0))],
             pl.BlockSpec((TS, D), lambda i: (i, 0)), _sds((S, D), F32))
    x2, n3 = _postres("postres2", x1, h2, sp["mix_post_norm"], 1.0, sp["ffn2_pre_norm"])

    gate2, up2, act2 = _ffn_up("ffn2_up", n3, ffnw, 3)
    h3 = _ffn_down("ffn2_down", act2, ffnw, 3)
    dy, dh3, dp3, loss = _final(x2, h3, sp["ffn2_post_norm"], tgt, 0.5)

    dn3, dws2, _, _ = _ffn_bwd("ffn2", dh3, n3, gate2, up2, act2, ffnw, 3)
    dx2, dh2, dg3, dp2 = _mid_bwd("mid_bwd2", dy, dn3, x2, sp["ffn2_pre_norm"], h2, sp["mix_post_norm"], 1.0)

    dmixed = _mm("out_proj_dx", [dh2, wout], NT, (S // TS,),
                 [pl.BlockSpec((TS, D), lambda i: (i, 0)), pl.BlockSpec((2 * D, D), lambda i: (0, 0))],
                 pl.BlockSpec((TS, 2 * D), lambda i: (i, 0)), _sds((S, 2 * D), F32))
    dwout = _mm("out_proj_dw", [mixed, dh2], TN, (2,),
                [pl.BlockSpec((S, D), lambda m: (0, m)), pl.BlockSpec((S, D), lambda m: (0, 0))],
                pl.BlockSpec((D, D), lambda m: (m, 0)), _sds((2 * D, D), BF16))
    dwout = dwout.reshape(NSH, 2 * D // NSH, D)
    rider = comm.reduce_rider("a", BIG[3:6] + ("w_out",), dws2 + [dwout]) if comm else None
    (dxbc, dz, ddt, dssm, dsc), got = _ridden(
        _ssd_bwd(dmixed, y, xbc, proj, hprev, dtb, alog, dskip_l, sp["ssm_norm"], rider), rider)
    if comm:
        comm.landed("a", got)
    du, dcw8, dcb = _conv_bwd(dxbc, proj, convw, convb)
    do_bf = dmixed[:, :D].astype(BF16)
    dot_ = _to_blocks_t(do_bf)
    dqt, dkh, dvh = _attn_bwd(qt, kh, kt, vh, dot_, lse, _attn_delta(ot, dot_), bias)
    dq = _rope("rope_dq", _from_blocks_t(dqt), 0, D, cos, sin, -1.0, HD ** -0.5)
    dk = _rope("rope_dk", _unheads(dkh), 0, KVW, cos, sin, -1.0, 1.0)
    dproj = jnp.concatenate([dq, dk, _unheads(dvh).astype(BF16), du, dz, ddt], axis=1)
    dn2 = _mm("in_proj_dx", [dproj, wint_pad], NN, (S // TS,),
              [pl.BlockSpec((TS, WIN_PAD), lambda i: (i, 0)), pl.BlockSpec((WIN_PAD, D), lambda i: (0, 0))],
              pl.BlockSpec((TS, D), lambda i: (i, 0)), _sds((S, D), F32))
    dwint = _mm("in_proj_dw", [dproj, n2], TN, (3,),
                [pl.BlockSpec((S, pw), lambda j: (0, j)), pl.BlockSpec((S, D), lambda j: (0, 0))],
                pl.BlockSpec((pw, D), lambda j: (j, 0)), _sds((WIN_PAD, D), BF16))
    dx1, dh1, dg2, dp1 = _mid_bwd("mid_bwd1", dx2, dn2, x1, sp["mix_pre_norm"], h1, sp["ffn1_post_norm"], 0.5)

    dwint = dwint[:WIN_COLS].reshape(NSH, WIN_SH, D)
    dn1, dws1, got1, got2 = _ffn_bwd(
        "ffn1", dh1, n1, gate1, up1, act1, ffnw, 0,
        comm.reduce_rider("b", ("w_in",), [dwint]) if comm else None,
        (lambda dws: comm.reduce_rider("c", BIG[0:3], dws)) if comm else None)
    grad_x, dg1 = _first_bwd(dx1, dn1, x, sp["ffn1_pre_norm"])

    small = jnp.concatenate([
        dg1[0], dp1[0], dg2[0], dssm[0], dp2[0], dg3[0], dp3[0], dcb[0],
        dsc[0, :16], dsc[1, :16], dsc[2, :16], dcw8[:CONV_K].reshape(-1), loss[0, :1]])
    small = jnp.pad(small, (0, SMALL_LEN - small.shape[0])).reshape(SMALL_ROWS, D)
    if comm is None:
        return grad_x, dws1 + dws2 + [dwint, dwout], small
    comm.landed("b", got1)
    comm.landed("c", got2)
    return grad_x, comm.finish(), small


WEIGHTS = ("ffn1_pre_norm", "ffn1_w_gate", "ffn1_w_up", "ffn1_w_down", "ffn1_post_norm", "mix_pre_norm", "w_in",
           "conv_w", "conv_b", "dt_bias", "a_log", "d_skip", "ssm_norm", "w_out", "mix_post_norm", "ffn2_pre_norm",
           "ffn2_w_gate", "ffn2_w_up", "ffn2_w_down", "ffn2_post_norm")
BIG = ("ffn1_w_gate", "ffn1_w_up", "ffn1_w_down", "ffn2_w_gate", "ffn2_w_up", "ffn2_w_down", "w_in", "w_out")
TRANSPOSED = ("ffn1_w_gate", "ffn1_w_up", "ffn2_w_gate", "ffn2_w_up", "w_in")
SMALL_ORDER = SMALL_1K + ("conv_b",) + SMALL_16
CONVW_SH = CONV_C // NSH


def _shard2d(t, name):
    return t[0].T if name in TRANSPOSED else t[0]


def _unshard2d(t, name):
    return (t.T if name in TRANSPOSED else t)[None]


def _rows3d(t):
    return t.transpose(2, 0, 1)


def _pack_small(d, prefix, shard_of_convw):
    flat = jnp.concatenate([d[prefix + n][0] for n in SMALL_ORDER] + [shard_of_convw.reshape(-1)])
    return jnp.pad(flat, (0, SMALL_LEN - flat.shape[0])).reshape(SMALL_ROWS, D)


def _unpack_small(block, like):
    flat = block.reshape(-1)
    out, off = {}, 0
    for n in SMALL_ORDER:
        size = like[n].shape[1]
        out[n] = flat[off:off + size].reshape(1, size)
        off += size
    out["conv_w"] = flat[off:off + CONV_K * CONVW_SH].reshape(1, CONV_K, CONVW_SH)
    return out


def kernel(x, positions, ffn1_pre_norm, ffn1_w_gate, ffn1_w_up, ffn1_w_down, ffn1_post_norm, mix_pre_norm, w_in, conv_w, conv_b, dt_bias, a_log, d_skip, ssm_norm, w_out, mix_post_norm, ffn2_pre_norm, ffn2_w_gate, ffn2_w_up, ffn2_w_down, ffn2_post_norm, loss_target, m_ffn1_pre_norm, m_ffn1_w_gate, m_ffn1_w_up, m_ffn1_w_down, m_ffn1_post_norm, m_mix_pre_norm, m_w_in, m_conv_w, m_conv_b, m_dt_bias, m_a_log, m_d_skip, m_ssm_norm, m_w_out, m_mix_post_norm, m_ffn2_pre_norm, m_ffn2_w_gate, m_ffn2_w_up, m_ffn2_w_down, m_ffn2_post_norm, v_ffn1_pre_norm, v_ffn1_w_gate, v_ffn1_w_up, v_ffn1_w_down, v_ffn1_post_norm, v_mix_pre_norm, v_w_in, v_conv_w, v_conv_b, v_dt_bias, v_a_log, v_d_skip, v_ssm_norm, v_w_out, v_mix_post_norm, v_ffn2_pre_norm, v_ffn2_w_gate, v_ffn2_w_up, v_ffn2_w_down, v_ffn2_post_norm):
    given = dict(locals())
    xi, yi = lax.axis_index("x"), lax.axis_index("y")

    shard = jnp.reshape(2 * xi + yi, (1,)).astype(jnp.int32)
    big = {p + n: _shard2d(given[p + n], n) for n in BIG for p in ("", "m_", "v_")}
    ffnsh = _cast_stack("cast_ffn", shard, [big[n] for n in BIG[:6]], 176, D)
    winsh = _cast_stack("cast_w_in", shard, [big["w_in"]], WIN_SH, 256).reshape(NSH, WIN_SH, D)
    woutsh = _cast_stack("cast_w_out", shard, [big["w_out"]], 256, D).reshape(NSH, 2 * D // NSH, D)
    comm = _Comm()
    (ffnw,), (cwf,) = _run_riders("gather_ffn1", [comm.gather_ffn(ffnsh, 0), _small_gather_rider(conv_w[0])])
    convw = cwf.transpose(1, 0, 2).reshape(CONV_K, CONV_C)

    sp = {n: given[n] for n in SMALL_ORDER}
    grad_x, big_grads, small = _local_step(x[0], positions[0], loss_target[0], sp, ffnw, winsh, woutsh, convw, comm)

    tot = _allreduce_small(small).reshape(-1)
    loss = tot[OFF_LOSS]
    small_grads, off = {}, 0
    for n in SMALL_ORDER:
        size = given[n].shape[1]
        small_grads[n] = tot[off:off + size].reshape(1, size)
        off += size
    dconvw = tot[OFF_CONVW:OFF_CONVW + CONV_K * CONV_C].reshape(CONV_K, NSH, CONVW_SH)
    dconvw = lax.dynamic_index_in_dim(dconvw, 2 * xi + yi, axis=1, keepdims=False)
    small_grads["conv_w"] = dconvw.reshape(1, CONV_K, CONVW_SH)

    upd = {}
    for names, tr in ((BIG[0:3], 176), (BIG[3:6], 176), (BIG[7:8], 256)):
        res = _adamw("adamw_" + names[0], [big[n] for n in names], [big_grads[n] for n in names],
                     [big["m_" + n] for n in names], [big["v_" + n] for n in names], tr, D)
        for n, r in zip(names, res):
            upd[n] = tuple(_unshard2d(t, n) for t in r)
    g_win = big_grads["w_in"].reshape(WIN_SH, 1, D)
    res, = _adamw("adamw_w_in", [_rows3d(w_in)], [g_win], [_rows3d(m_w_in)], [_rows3d(v_w_in)], WIN_SH // 4, D)
    upd["w_in"] = tuple(t.transpose(1, 2, 0) for t in res)
    (dl, m2, v2), = _adamw(
        "adamw_small", [_pack_small(given, "", conv_w[0])], [_pack_small(small_grads, "", dconvw)],
        [_pack_small(given, "m_", m_conv_w[0])], [_pack_small(given, "v_", v_conv_w[0])], SMALL_ROWS, D)
    dl, m2, v2 = (_unpack_small(t, given) for t in (dl, m2, v2))
    for n in SMALL_ORDER + ("conv_w",):
        upd[n] = (dl[n], m2[n], v2[n])

    grads = dict(small_grads)
    grads.update({n: _unshard2d(g, n) for n, g in big_grads.items() if n != "w_in"})
    grads["w_in"] = g_win.transpose(1, 2, 0)
    return (loss, grad_x[None], *[grads[n] for n in WEIGHTS], *[upd[n][0] for n in WEIGHTS],
            *[upd[n][1] for n in WEIGHTS], *[upd[n][2] for n in WEIGHTS])
```

```python
import functools
import typing

import jax
import jax.numpy as jnp
from jax import lax
from jax.experimental import pallas as pl
from jax.experimental.pallas import tpu as pltpu

F32 = jnp.float32
BF16 = jnp.bfloat16

S = 2048
D = 1024
FF = 2816
NSH = 4
FS = FF // NSH
HALF = D // 2
HD = 64
NKV = 4
NQ_PER_KV = 4
KVW = NKV * HD
CONV_C = 1536
CONV_K = 4
SSM_W = 1024
NST = 128
NCH = S // 128
WIN_COLS = 4112
WIN_SH = WIN_COLS // NSH
WIN_PAD = 4224
COL_DT = 4096
EPS = 1e-6
NEG = -1e30

ADAM_LR = 0.001
ADAM_B1 = 0.9
ADAM_B2 = 0.999
ADAM_EPS = 1e-08
ADAM_WD = 0.01
ADAM_STEP = 10

VMEM_LIMIT = 56 * 1024 * 1024
TS = 512
TR = 256

NN = (((1,), (0,)), ((), ()))
NT = (((1,), (1,)), ((), ()))
TN = (((0,), (0,)), ((), ()))
MESH = pl.DeviceIdType.MESH


def _cparams(*sem):
    return pltpu.CompilerParams(dimension_semantics=sem, vmem_limit_bytes=VMEM_LIMIT)


def _dot(a, b, dims):
    return lax.dot_general(a.astype(BF16), b.astype(BF16), dims, preferred_element_type=F32)


def _dot_exact(a, b):
    return lax.dot_general(a, b, NN, precision=lax.Precision.HIGHEST, preferred_element_type=F32)


def _sigmoid(v):
    return 1.0 / (1.0 + jnp.exp(-v))


class _Rider(typing.NamedTuple):
    operands: list
    out_shapes: list
    aliases: dict
    sems: list
    start: typing.Callable
    finish: typing.Callable


def _call(body, name, grid, in_specs, out_specs, out_shape, operands, scratch=(), sem=(), rider=None):
    multi = isinstance(out_shape, (list, tuple))
    if rider is None:
        return pl.pallas_call(
            body, name=name, grid=grid, in_specs=in_specs, out_specs=out_specs, out_shape=out_shape,
            scratch_shapes=list(scratch), compiler_params=_cparams(*sem))(*operands)
    outs = list(out_shape) if multi else [out_shape]
    ospecs = list(out_specs) if multi else [out_specs]
    n_in, n_out, n_scr = len(operands), len(outs), len(scratch)
    ri, ro = len(rider.operands), len(rider.out_shapes)

    def wrapped(*refs):
        o0 = n_in + ri
        s0 = o0 + n_out + ro
        rin, rout, rsem = refs[n_in:o0], refs[o0 + n_out:s0], refs[s0 + n_scr:]
        ids = [pl.program_id(a) for a in range(len(grid))]
        first = functools.reduce(jnp.logical_and, [i == 0 for i in ids])
        last = functools.reduce(jnp.logical_and, [i == g - 1 for i, g in zip(ids, grid)])

        @pl.when(first)
        def _():
            rider.start(rin, rout, rsem)

        body(*refs[:n_in], *refs[o0:o0 + n_out], *refs[s0:s0 + n_scr])

        @pl.when(last)
        def _():
            rider.finish(rin, rout, rsem)

    hbm = pl.BlockSpec(memory_space=pl.ANY)
    res = pl.pallas_call(
        wrapped, name=name, grid=grid, in_specs=list(in_specs) + [hbm] * ri, out_specs=ospecs + [hbm] * ro,
        out_shape=outs + list(rider.out_shapes), scratch_shapes=list(scratch) + list(rider.sems),
        input_output_aliases={n_in + k: n_out + v for k, v in rider.aliases.items()},
        compiler_params=_cparams(*(("arbitrary",) * len(grid))))(*operands, *rider.operands)
    main = list(res[:n_out])
    return (main if multi else main[0]), list(res[n_out:])


def _mm(name, operands, dims, grid, in_specs, o_spec, out_shape, rider=None):
    npairs = len(operands) // 2

    def body(*refs):
        t = None
        for i in range(npairs):
            a, b = refs[2 * i], refs[2 * i + 1]
            parts = [(a[s], b[s]) for s in range(a.shape[0])] if len(a.shape) == 3 else [(a[...], b[...])]
            for pa, pb in parts:
                d = _dot(pa, pb, dims)
                t = d if t is None else t + d
        refs[2 * npairs][...] = t.astype(refs[2 * npairs].dtype)

    return _call(body, name, grid, in_specs, o_spec, out_shape, operands, (), ("parallel",) * len(grid), rider)


class _FfnW(typing.NamedTuple):
    gu: jax.Array
    g0: int
    dn: jax.Array
    d0: int


def _ffn_up(name, n, w, rider=None):
    def body(n_ref, wg_ref, wu_ref, fg_ref, fu_ref, a_ref):
        nb = n_ref[...]
        g = _dot(nb, wg_ref[...], NT)
        u = _dot(nb, wu_ref[...], NT)
        sg = _sigmoid(g)
        silu = g * sg
        fg_ref[...] = (u * (sg * (1.0 + g * (1.0 - sg)))).astype(BF16)
        fu_ref[...] = silu.astype(BF16)
        a_ref[...] = (silu * u).astype(BF16)

    out = jax.ShapeDtypeStruct((NSH, S, FS), BF16)
    ospec = pl.BlockSpec((None, TS, FS), lambda s, i: (s, i, 0))
    return _call(
        body, name, (NSH, S // TS),
        [pl.BlockSpec((TS, D), lambda s, i: (i, 0)),
         pl.BlockSpec((None, None, FS, D), lambda s, i: (s, w.g0, 0, 0)),
         pl.BlockSpec((None, None, FS, D), lambda s, i: (s, w.g0 + 1, 0, 0))],
        [ospec, ospec, ospec], [out, out, out], (n, w.gu, w.gu), sem=("parallel", "parallel"), rider=rider)


def _ffn_dact(name, dh, w, fgate, fup, rider=None):
    def body(dh_ref, wd_ref, fg_ref, fu_ref, dg_ref, du_ref):
        da = _dot(dh_ref[...], wd_ref[...], NT)
        dg_ref[...] = (da * fg_ref[...].astype(F32)).astype(BF16)
        du_ref[...] = (da * fu_ref[...].astype(F32)).astype(BF16)

    out = jax.ShapeDtypeStruct((NSH, S, FS), BF16)
    aspec = pl.BlockSpec((None, TS, FS), lambda s, i: (s, i, 0))
    return _call(
        body, name, (NSH, S // TS),
        [pl.BlockSpec((TS, D), lambda s, i: (i, 0)),
         pl.BlockSpec((None, None, FS, D), lambda s, i: (s, w.d0, 0, 0)), aspec, aspec],
        [aspec, aspec], [out, out], (dh, w.dn, fgate, fup), sem=("parallel", "parallel"), rider=rider)


def _rstd(v):
    return lax.rsqrt(jnp.mean(v * v, axis=-1, keepdims=True) + EPS)


def _row_spec():
    return pl.BlockSpec((TR, D), lambda i: (i, 0))


def _vec_spec():
    return pl.BlockSpec((1, D), lambda i: (0, 0))


def _acc_rows(ref, v):
    @pl.when(pl.program_id(0) == 0)
    def _():
        ref[...] = jnp.zeros_like(ref)
    ref[...] += jnp.sum(v, axis=0, keepdims=True)


def _prenorm(name, x, g):
    def body(x_ref, g_ref, n_ref):
        xv = x_ref[...]
        n_ref[...] = (xv * _rstd(xv) * g_ref[...]).astype(BF16)

    return pl.pallas_call(
        body, name=name, grid=(S // TR,), in_specs=[_row_spec(), _vec_spec()], out_specs=_row_spec(),
        out_shape=jax.ShapeDtypeStruct((S, D), BF16), compiler_params=_cparams("parallel"),
    )(x, g)


def _postres(name, x, h, p, alpha, gnext):
    def body(x_ref, h_ref, p_ref, g_ref, xo_ref, n_ref):
        hv = h_ref[...]
        xo = x_ref[...] + alpha * (hv * _rstd(hv) * p_ref[...])
        xo_ref[...] = xo
        n_ref[...] = (xo * _rstd(xo) * g_ref[...]).astype(BF16)

    return pl.pallas_call(
        body, name=name, grid=(S // TR,),
        in_specs=[_row_spec(), _row_spec(), _vec_spec(), _vec_spec()],
        out_specs=[_row_spec(), _row_spec()],
        out_shape=[jax.ShapeDtypeStruct((S, D), F32), jax.ShapeDtypeStruct((S, D), BF16)],
        compiler_params=_cparams("parallel"),
    )(x, h, p, gnext)


def _final(x, h, p, tgt, alpha):
    def body(x_ref, h_ref, p_ref, t_ref, dy_ref, dh_ref, dp_ref, loss_ref):
        hv = h_ref[...]
        r = _rstd(hv)
        hn = hv * r
        pv = p_ref[...]
        e = x_ref[...] + alpha * (hn * pv) - t_ref[...]
        dy = e * (1.0 / D)
        dy_ref[...] = dy
        du = alpha * dy * pv
        dh_ref[...] = (r * (du - hn * jnp.mean(du * hn, axis=-1, keepdims=True))).astype(BF16)
        _acc_rows(dp_ref, alpha * dy * hn)
        part = 0.5 * jnp.sum(jnp.mean(e * e, axis=-1, keepdims=True), axis=0, keepdims=True)
        _acc_rows(loss_ref, jnp.broadcast_to(part, (1, 128)))

    return pl.pallas_call(
        body, name="loss_head", grid=(S // TR,),
        in_specs=[_row_spec(), _row_spec(), _vec_spec(), _row_spec()],
        out_specs=[_row_spec(), _row_spec(), _vec_spec(), pl.BlockSpec((1, 128), lambda i: (0, 0))],
        out_shape=[jax.ShapeDtypeStruct((S, D), F32), jax.ShapeDtypeStruct((S, D), BF16),
                   jax.ShapeDtypeStruct((1, D), F32), jax.ShapeDtypeStruct((1, 128), F32)],
        compiler_params=_cparams("arbitrary"),
    )(x, h, p, tgt)


def _mid_bwd(name, dres, dn, x, g, h, p, alpha):
    def body(dr_ref, dn_ref, x_ref, g_ref, h_ref, p_ref, dx_ref, dh_ref, dg_ref, dp_ref):
        xv = x_ref[...]
        xn = xv * _rstd(xv)
        dnv = dn_ref[...]
        dng = dnv * g_ref[...]
        dx = dr_ref[...] + _rstd(xv) * (dng - xn * jnp.mean(dng * xn, axis=-1, keepdims=True))
        dx_ref[...] = dx
        _acc_rows(dg_ref, dnv * xn)
        hv = h_ref[...]
        r = _rstd(hv)
        hn = hv * r
        du = alpha * dx * p_ref[...]
        dh_ref[...] = (r * (du - hn * jnp.mean(du * hn, axis=-1, keepdims=True))).astype(BF16)
        _acc_rows(dp_ref, alpha * dx * hn)

    return pl.pallas_call(
        body, name=name, grid=(S // TR,),
        in_specs=[_row_spec(), _row_spec(), _row_spec(), _vec_spec(), _row_spec(), _vec_spec()],
        out_specs=[_row_spec(), _row_spec(), _vec_spec(), _vec_spec()],
        out_shape=[jax.ShapeDtypeStruct((S, D), F32), jax.ShapeDtypeStruct((S, D), BF16),
                   jax.ShapeDtypeStruct((1, D), F32), jax.ShapeDtypeStruct((1, D), F32)],
        compiler_params=_cparams("arbitrary"),
    )(dres, dn, x, g, h, p)


def _first_bwd(dres, dn, x, g):
    def body(dr_ref, dn_ref, x_ref, g_ref, dx_ref, dg_ref):
        xv = x_ref[...]
        r = _rstd(xv)
        xn = xv * r
        dnv = dn_ref[...]
        dng = dnv * g_ref[...]
        dx_ref[...] = dr_ref[...] + r * (dng - xn * jnp.mean(dng * xn, axis=-1, keepdims=True))
        _acc_rows(dg_ref, dnv * xn)

    return pl.pallas_call(
        body, name="first_bwd", grid=(S // TR,),
        in_specs=[_row_spec(), _row_spec(), _row_spec(), _vec_spec()],
        out_specs=[_row_spec(), _vec_spec()],
        out_shape=[jax.ShapeDtypeStruct((S, D), F32), jax.ShapeDtypeStruct((1, D), F32)],
        compiler_params=_cparams("arbitrary"),
    )(dres, dn, x, g)


def _rope(name, src, col_block, width, cos, sin, sign, scale):
    def body(t_ref, c_ref, s_ref, o_ref):
        t = t_ref[...].astype(F32)
        c = jnp.tile(c_ref[...], (1, width // 128))
        sn = jnp.tile(s_ref[...], (1, width // 128))
        lane = lax.broadcasted_iota(jnp.int32, t.shape, 1) & (HD - 1)
        rot = jnp.where(lane < HD // 2, -pltpu.roll(t, width - HD // 2, 1), pltpu.roll(t, HD // 2, 1))
        o_ref[...] = ((t * c + sign * (rot * sn)) * scale).astype(BF16)

    return pl.pallas_call(
        body, name=name, grid=(S // TR,),
        in_specs=[pl.BlockSpec((TR, width), lambda i: (i, col_block)),
                  pl.BlockSpec((TR, 128), lambda i: (i, 0)), pl.BlockSpec((TR, 128), lambda i: (i, 0))],
        out_specs=pl.BlockSpec((TR, width), lambda i: (i, 0)),
        out_shape=jax.ShapeDtypeStruct((S, width), BF16), compiler_params=_cparams("parallel"),
    )(src, cos, sin)


QROWS = NQ_PER_KV * 128


NBIAS = NCH + 1
KV_PER_STEP = 2


def _bias_table():
    db = lax.broadcasted_iota(jnp.int32, (NBIAS, 128, QROWS), 0) - 1
    ki = lax.broadcasted_iota(jnp.int32, (NBIAS, 128, QROWS), 1)
    qi = lax.broadcasted_iota(jnp.int32, (NBIAS, 128, QROWS), 2) & 127
    d = db * 128 + qi - ki
    cnt = ((d <= 128).astype(F32) + (((d & 3) == 0) & (d <= 512)).astype(F32) + ((d & 15) == 0).astype(F32))
    return jnp.where((d >= 0) & (cnt > 0.0), jnp.log(jnp.maximum(cnt, 1.0)), NEG)


def _qt_spec():
    return pl.BlockSpec((None, None, HD, QROWS), lambda j, i: (j, i, 0, 0))


def _stat_spec():
    return pl.BlockSpec((None, None, 1, QROWS), lambda j, i: (j, i, 0, 0))


def _attn_fwd(qt, kh, vt, bias, rider=None):
    def body(q_ref, k_ref, v_ref, b_ref, o_ref, lse_ref):
        qb = pl.program_id(1)

        def keys(carry, off, size, bias_):
            out = []
            for h in range(KV_PER_STEP):
                m, l, acc = carry[3 * h:3 * h + 3]
                s = _dot(k_ref[h, pl.ds(off, size), :], q_ref[h], NN) + bias_
                m_new = jnp.maximum(m, jnp.max(s, axis=0, keepdims=True))
                p = jnp.exp(s - m_new)
                a = jnp.exp(m - m_new)
                out += [m_new, a * l + jnp.sum(p, axis=0, keepdims=True),
                        a * acc + _dot(v_ref[h, :, pl.ds(off, size)], p, NN)]
            return tuple(out)

        def pair(i, carry):
            bias2 = jnp.concatenate([b_ref[qb - 2 * i + 1], b_ref[qb - 2 * i]], axis=0)
            return keys(carry, pl.multiple_of(i * 256, 256), 256, bias2)

        init = (jnp.full((1, QROWS), NEG, F32), jnp.zeros((1, QROWS), F32), jnp.zeros((HD, QROWS), F32))
        res = lax.fori_loop(0, (qb + 1) // 2, pair, init * KV_PER_STEP)
        res = lax.cond(qb % 2 == 0,
                       lambda c: keys(c, pl.multiple_of(qb * 128, 128), 128, b_ref[1]), lambda c: c, res)
        for h in range(KV_PER_STEP):
            m, l, acc = res[3 * h:3 * h + 3]
            o_ref[h] = acc / l
            lse_ref[h] = m + jnp.log(l)

    kvs = KV_PER_STEP
    qspec = pl.BlockSpec((kvs, None, HD, QROWS), lambda j, i: (j, i, 0, 0))
    return _call(
        body, "attn_fwd", (NKV // kvs, NCH),
        [qspec, pl.BlockSpec((kvs, S, HD), lambda j, i: (j, 0, 0)),
         pl.BlockSpec((kvs, HD, S), lambda j, i: (j, 0, 0)),
         pl.BlockSpec((NBIAS, 128, QROWS), lambda j, i: (0, 0, 0))],
        [qspec, pl.BlockSpec((kvs, None, 1, QROWS), lambda j, i: (j, i, 0, 0))],
        [jax.ShapeDtypeStruct((NKV, NCH, HD, QROWS), F32), jax.ShapeDtypeStruct((NKV, NCH, 1, QROWS), F32)],
        (qt, kh, vt, bias), sem=("parallel", "parallel"), rider=rider)


def _attn_delta(ot, dot_):
    def body(o_ref, do_ref, dl_ref):
        dl_ref[...] = jnp.sum(o_ref[...] * do_ref[...].astype(F32), axis=1, keepdims=True)

    spec = pl.BlockSpec((None, NCH, HD, QROWS), lambda j: (j, 0, 0, 0))
    return pl.pallas_call(
        body, name="attn_delta", grid=(NKV,), in_specs=[spec, spec],
        out_specs=pl.BlockSpec((None, NCH, 1, QROWS), lambda j: (j, 0, 0, 0)),
        out_shape=jax.ShapeDtypeStruct((NKV, NCH, 1, QROWS), F32), compiler_params=_cparams("parallel"),
    )(ot, dot_)


def _attn_bwd(qt, kh, kt, vh, dot_, lse, delta, bias):
    def body(qt_ref, k_ref, kt_ref, v_ref, dot_ref, lse_ref, dl_ref, b_ref, dq_ref, dk_ref, dv_ref):
        kb = pl.program_id(1)

        @pl.when(kb == 0)
        def _():
            dq_ref[...] = jnp.zeros_like(dq_ref)

        def blocks(carry, qbs):
            out = list(carry)
            for h in range(KV_PER_STEP):
                k, kt_, v = k_ref[h], kt_ref[h], v_ref[h]
                for qb in qbs:
                    st = _dot(k, qt_ref[h, qb], NN) + b_ref[qb - kb + 1]
                    pt = jnp.exp(st - lse_ref[h, qb])
                    dst = pt * (_dot(v, dot_ref[h, qb], NN) - dl_ref[h, qb])
                    dq_ref[h, qb] += _dot(kt_, dst, NN)
                    out[2 * h] = out[2 * h] + _dot(dst, qt_ref[h, qb], NT)
                    out[2 * h + 1] = out[2 * h + 1] + _dot(pt, dot_ref[h, qb], NT)
            return tuple(out)

        res = (jnp.zeros((128, HD), F32),) * (2 * KV_PER_STEP)
        res = lax.cond(kb % 2 == 1, lambda c: blocks(c, (kb,)), lambda c: c, res)
        res = lax.fori_loop((kb + 1) // 2, NCH // 2, lambda j, c: blocks(c, (2 * j, 2 * j + 1)), res)
        for h in range(KV_PER_STEP):
            dk_ref[h] = res[2 * h]
            dv_ref[h] = res[2 * h + 1]

    kvs = KV_PER_STEP
    tspec = pl.BlockSpec((kvs, NCH, HD, QROWS), lambda j, i: (j, 0, 0, 0))
    kspec = pl.BlockSpec((kvs, 128, HD), lambda j, i: (j, i, 0))
    sspec = pl.BlockSpec((kvs, NCH, 1, QROWS), lambda j, i: (j, 0, 0, 0))
    return pl.pallas_call(
        body, name="attn_bwd", grid=(NKV // kvs, NCH),
        in_specs=[tspec, kspec, pl.BlockSpec((kvs, HD, 128), lambda j, i: (j, 0, i)), kspec, tspec,
                  sspec, sspec, pl.BlockSpec((NBIAS, 128, QROWS), lambda j, i: (0, 0, 0))],
        out_specs=[tspec, kspec, kspec],
        out_shape=[jax.ShapeDtypeStruct((NKV, NCH, HD, QROWS), F32),
                   jax.ShapeDtypeStruct((NKV, S, HD), F32), jax.ShapeDtypeStruct((NKV, S, HD), F32)],
        compiler_params=_cparams("parallel", "arbitrary"),
    )(qt, kh, kt, vh, dot_, lse, delta, bias)


CONV_BLK = 256
CONV_COL0 = 1536 // CONV_BLK


def _shift_down(u, j, row):
    return jnp.where(row >= j, pltpu.roll(u, j, 0), 0.0)


def _conv_pre(u, w_ref, b_ref, row):
    y = b_ref[...] + w_ref[CONV_K - 1:CONV_K, :] * u
    for j in range(1, CONV_K):
        y = y + w_ref[CONV_K - 1 - j:CONV_K - j, :] * _shift_down(u, j, row)
    return y


def _conv_fwd(proj, convw, convb):
    def body(u_ref, w_ref, b_ref, o_ref):
        u = u_ref[...]
        row = lax.broadcasted_iota(jnp.int32, u.shape, 0)
        y = _conv_pre(u, w_ref, b_ref, row)
        o_ref[...] = y * _sigmoid(y)

    return pl.pallas_call(
        body, name="conv_fwd", grid=(CONV_C // CONV_BLK,),
        in_specs=[pl.BlockSpec((S, CONV_BLK), lambda i: (0, CONV_COL0 + i)),
                  pl.BlockSpec((CONV_K, CONV_BLK), lambda i: (0, i)),
                  pl.BlockSpec((1, CONV_BLK), lambda i: (0, i))],
        out_specs=pl.BlockSpec((S, CONV_BLK), lambda i: (0, i)),
        out_shape=jax.ShapeDtypeStruct((S, CONV_C), F32), compiler_params=_cparams("parallel"),
    )(proj, convw, convb)


def _conv_bwd(dact, proj, convw, convb):
    def body(da_ref, u_ref, w_ref, b_ref, du_ref, dw_ref, db_ref):
        u = u_ref[...]
        row = lax.broadcasted_iota(jnp.int32, u.shape, 0)
        y = _conv_pre(u, w_ref, b_ref, row)
        sg = _sigmoid(y)
        dy = da_ref[...] * (sg * (1.0 + y * (1.0 - sg)))
        db_ref[...] = jnp.sum(dy, axis=0, keepdims=True)
        du = w_ref[CONV_K - 1:CONV_K, :] * dy
        r8 = lax.broadcasted_iota(jnp.int32, (8, CONV_BLK), 0)
        dw = jnp.where(r8 == CONV_K - 1, jnp.sum(dy * u, axis=0, keepdims=True), 0.0)
        for j in range(1, CONV_K):
            du = du + w_ref[CONV_K - 1 - j:CONV_K - j, :] * jnp.where(row < S - j, pltpu.roll(dy, S - j, 0), 0.0)
            dw = dw + jnp.where(r8 == CONV_K - 1 - j,
                                jnp.sum(dy * _shift_down(u, j, row), axis=0, keepdims=True), 0.0)
        du_ref[...] = du.astype(BF16)
        dw_ref[...] = dw

    return pl.pallas_call(
        body, name="conv_bwd", grid=(CONV_C // CONV_BLK,),
        in_specs=[pl.BlockSpec((S, CONV_BLK), lambda i: (0, i)),
                  pl.BlockSpec((S, CONV_BLK), lambda i: (0, CONV_COL0 + i)),
                  pl.BlockSpec((CONV_K, CONV_BLK), lambda i: (0, i)),
                  pl.BlockSpec((1, CONV_BLK), lambda i: (0, i))],
        out_specs=[pl.BlockSpec((S, CONV_BLK), lambda i: (0, i)), pl.BlockSpec((8, CONV_BLK), lambda i: (0, i)),
                   pl.BlockSpec((1, CONV_BLK), lambda i: (0, i))],
        out_shape=[jax.ShapeDtypeStruct((S, CONV_C), BF16), jax.ShapeDtypeStruct((8, CONV_C), F32),
                   jax.ShapeDtypeStruct((1, CONV_C), F32)],
        compiler_params=_cparams("parallel"),
    )(dact, proj, convw, convb)


NPAIR = 8


def _ssd_scalars(dtr_ref, dtb_ref, alog_ref):
    z = dtr_ref[...] + dtb_ref[...]
    dt = jnp.maximum(z, 0.0) + jnp.log(1.0 + jnp.exp(-jnp.abs(z)))
    a = -jnp.exp(alog_ref[...])
    r = lax.broadcasted_iota(jnp.int32, (128, 128), 0)
    c = lax.broadcasted_iota(jnp.int32, (128, 128), 1)
    tri = (r >= c).astype(F32)
    cs = _dot_exact(tri, dt * a)
    return z, dt, a, cs, r, c


def _pair_terms(cs, cst, dt, h1, h2, lo):
    c1, c2 = cs[:, h1:h1 + 1], cs[:, h2:h2 + 1]
    l1, l2 = cs[127:128, h1:h1 + 1], cs[127:128, h2:h2 + 1]
    e_l = jnp.where(lo, jnp.exp(c1), jnp.exp(c2))
    dte1, dte2 = jnp.exp(l1 - c1), jnp.exp(l2 - c2)
    dte_l = jnp.where(lo, dte1, dte2)
    dt_l = jnp.where(lo, dt[:, h1:h1 + 1], dt[:, h2:h2 + 1])
    return c1, c2, jnp.exp(l1), jnp.exp(l2), e_l, dte1, dte2, dte_l, dt_l


def _gate_norm(y, zv, w):
    yg = y * (zv * _sigmoid(zv))
    outs, rs = [], []
    for g in range(2):
        blk = yg[:, 512 * g:512 * (g + 1)]
        r = lax.rsqrt(jnp.mean(blk * blk, axis=-1, keepdims=True) + EPS)
        outs.append(blk * r)
        rs.append(r)
    return jnp.concatenate(outs, axis=1), rs, yg


def _ssd_fwd(xbc, proj, dtb, alog, dskip_l, ssmw):
    def body(x_ref, b_ref, c_ref, dtr_ref, z_ref, dtb_ref, alog_ref, dsk_ref, w_ref, y_ref, yn_ref, hp_ref, h_ref):
        @pl.when(pl.program_id(0) == 0)
        def _():
            h_ref[...] = jnp.zeros_like(h_ref)

        _, dt, _, cs, r, c = _ssd_scalars(dtr_ref, dtb_ref, alog_ref)
        cst = cs.T
        causal = r >= c
        lo = c < HD
        hp_ref[...] = h_ref[...]
        for g in range(2):
            bg = b_ref[:, 128 * g:128 * (g + 1)]
            cg = c_ref[:, 128 * g:128 * (g + 1)]
            cb = _dot(cg, bg, NT)
            for j in range(4):
                pj = 4 * g + j
                h1, h2 = 2 * pj, 2 * pj + 1
                sl = slice(128 * pj, 128 * (pj + 1))
                xp = x_ref[:, sl]
                c1, c2, cd1, cd2, e_l, _, _, dte_l, dt_l = _pair_terms(cs, cst, dt, h1, h2, lo)
                xdt = xp * dt_l
                m1 = cb * jnp.exp(jnp.where(causal, c1 - cst[h1:h1 + 1, :], NEG))
                m2 = cb * jnp.exp(jnp.where(causal, c2 - cst[h2:h2 + 1, :], NEG))
                yd = jnp.where(lo, _dot(m1, xdt, NN), _dot(m2, xdt, NN))
                hp = h_ref[pj]
                yo = _dot(cg, hp, NT) * e_l
                st = _dot(xdt * dte_l, bg, TN)
                h_ref[pj] = hp * jnp.where(r < HD, cd1, cd2) + st
                y_ref[:, sl] = yd + yo + dsk_ref[:, sl] * xp
        yn, _, _ = _gate_norm(y_ref[...], z_ref[...], w_ref[...])
        yn_ref[...] = (yn * w_ref[...]).astype(BF16)

    return pl.pallas_call(
        body, name="ssd_fwd", grid=(NCH,),
        in_specs=[pl.BlockSpec((128, SSM_W), lambda i: (i, 0)),
                  pl.BlockSpec((128, 256), lambda i: (i, 4)), pl.BlockSpec((128, 256), lambda i: (i, 5)),
                  pl.BlockSpec((128, 128), lambda i: (i, COL_DT // 128)),
                  pl.BlockSpec((128, SSM_W), lambda i: (i, 3)),
                  pl.BlockSpec((1, 128), lambda i: (0, 0)), pl.BlockSpec((1, 128), lambda i: (0, 0)),
                  pl.BlockSpec((1, SSM_W), lambda i: (0, 0)), pl.BlockSpec((1, SSM_W), lambda i: (0, 0))],
        out_specs=[pl.BlockSpec((128, SSM_W), lambda i: (i, 0)), pl.BlockSpec((128, SSM_W), lambda i: (i, 0)),
                   pl.BlockSpec((None, NPAIR, 128, 128), lambda i: (i, 0, 0, 0))],
        out_shape=[jax.ShapeDtypeStruct((S, SSM_W), F32), jax.ShapeDtypeStruct((S, SSM_W), BF16),
                   jax.ShapeDtypeStruct((NCH, NPAIR, 128, 128), F32)],
        scratch_shapes=[pltpu.VMEM((NPAIR, 128, 128), F32)],
        compiler_params=_cparams("arbitrary"),
    )(xbc, xbc, xbc, proj, proj, dtb, alog, dskip_l, ssmw)


def _ssd_bwd(dmixed, y, xbc, proj, hprev, dtb, alog, dskip_l, ssmw, rider=None):
    def body(dyn_ref, y_ref, x_ref, b_ref, c_ref, dtr_ref, z_ref, hp_ref, dtb_ref, alog_ref, dsk_ref, w_ref,
             dxbc_ref, dz_ref, ddt_ref, dw_ref, dsc_ref, g_ref):
        @pl.when(pl.program_id(0) == 0)
        def _():
            g_ref[...] = jnp.zeros_like(g_ref)
            dsc_ref[...] = jnp.zeros_like(dsc_ref)

        z, dt, a, cs, r, c = _ssd_scalars(dtr_ref, dtb_ref, alog_ref)
        cst = cs.T
        causal = r >= c
        lo = c < HD

        yv = y_ref[...]
        zv = z_ref[...]
        wv = w_ref[...]
        ygn, rs, yg = _gate_norm(yv, zv, wv)
        dyn = dyn_ref[...]
        _acc_rows(dw_ref, dyn * ygn)
        dynw = dyn * wv
        parts = []
        for g in range(2):
            sl = slice(512 * g, 512 * (g + 1))
            a_g, n_g = dynw[:, sl], ygn[:, sl]
            parts.append(rs[g] * (a_g - n_g * jnp.mean(a_g * n_g, axis=-1, keepdims=True)))
        dyg = jnp.concatenate(parts, axis=1)
        sz = _sigmoid(zv)
        dz_ref[...] = (dyg * yv * (sz * (1.0 + zv * (1.0 - sz)))).astype(BF16)
        dy_all = dyg * (zv * sz)

        dcs_cols = jnp.zeros((128, 128), F32)
        dcs_rows = jnp.zeros((128, 128), F32)
        ddt_x = jnp.zeros((128, 128), F32)
        dd_row = jnp.zeros((1, 128), F32)
        last = r == 127
        x_all, b_all, c_all, dsk_all = x_ref[...], b_ref[...], c_ref[...], dsk_ref[...]
        hp_all, g_all = hp_ref[...], g_ref[...]
        g_new, dx_parts, db_parts, dc_parts = [], [], [], []
        for g in range(2):
            bg = b_all[:, 128 * g:128 * (g + 1)]
            cg = c_all[:, 128 * g:128 * (g + 1)]
            cb = _dot(cg, bg, NT)
            dcb = jnp.zeros((128, 128), F32)
            db_acc = jnp.zeros((128, NST), F32)
            dc_acc = jnp.zeros((128, NST), F32)
            for j in range(4):
                pj = 4 * g + j
                h1, h2 = 2 * pj, 2 * pj + 1
                sl = slice(128 * pj, 128 * (pj + 1))
                xp = x_all[:, sl]
                dyp = dy_all[:, sl]
                c1, c2, cd1, cd2, e_l, dte1, dte2, dte_l, dt_l = _pair_terms(cs, cst, dt, h1, h2, lo)
                xdt = xp * dt_l
                hp = hp_all[pj]
                gp = g_all[pj]
                dxp = dsk_all[:, sl] * dyp
                dyx = dyp * xp
                dd_row = dd_row + jnp.where(c[0:1, :] == h1, jnp.sum(jnp.where(lo, dyx, 0.0), keepdims=True), 0.0) \
                    + jnp.where(c[0:1, :] == h2, jnp.sum(jnp.where(lo, 0.0, dyx), keepdims=True), 0.0)
                dzs = dyp * e_l
                dc_acc = dc_acc + _dot(dzs, hp, NN)
                g_from = _dot(dzs, cg, TN)
                ryo = dyp * (_dot(cg, hp, NT) * e_l)
                k1 = jnp.sum(jnp.where(lo, ryo, 0.0), axis=1, keepdims=True)
                k2 = jnp.sum(jnp.where(lo, 0.0, ryo), axis=1, keepdims=True)
                qm = _dot(bg, gp, NT)
                dxdt = qm * dte_l
                qx = qm * xdt
                t1 = jnp.sum(jnp.where(lo, qx, 0.0), axis=1, keepdims=True) * dte1
                t2 = jnp.sum(jnp.where(lo, 0.0, qx), axis=1, keepdims=True) * dte2
                db_acc = db_acc + _dot(xdt * dte_l, gp, NN)
                gh = gp * hp
                dl1 = jnp.sum(t1, keepdims=True) + jnp.sum(jnp.where(r < HD, gh, 0.0), keepdims=True) * cd1
                dl2 = jnp.sum(t2, keepdims=True) + jnp.sum(jnp.where(r < HD, 0.0, gh), keepdims=True) * cd2
                g_new.append(g_from + jnp.where(r < HD, cd1, cd2) * gp)
                k1 = k1 - t1 + jnp.where(last[:, 0:1], dl1, 0.0)
                k2 = k2 - t2 + jnp.where(last[:, 0:1], dl2, 0.0)
                for hh, ch, msk in ((h1, c1, lo), (h2, c2, jnp.logical_not(lo))):
                    lm = jnp.exp(jnp.where(causal, ch - cst[hh:hh + 1, :], NEG))
                    mm = cb * lm
                    dm = jnp.where(causal, _dot(jnp.where(msk, dyp, 0.0), xdt, NT), 0.0)
                    w = dm * mm
                    kk = jnp.sum(w, axis=1, keepdims=True)
                    if hh == h1:
                        k1 = k1 + kk
                    else:
                        k2 = k2 + kk
                    dcs_rows = dcs_rows + jnp.where(r == hh, jnp.sum(w, axis=0, keepdims=True), 0.0)
                    dcb = dcb + dm * lm
                    dxdt = dxdt + jnp.where(msk, _dot(mm, dyp, TN), 0.0)
                dcs_cols = dcs_cols + jnp.where(c == h1, k1, 0.0) + jnp.where(c == h2, k2, 0.0)
                dxx = dxdt * xp
                ddt_x = ddt_x + jnp.where(c == h1, jnp.sum(jnp.where(lo, dxx, 0.0), axis=1, keepdims=True), 0.0) \
                    + jnp.where(c == h2, jnp.sum(jnp.where(lo, 0.0, dxx), axis=1, keepdims=True), 0.0)
                dx_parts.append(dxp + dxdt * dt_l)
            db_parts.append(db_acc + _dot(dcb, cg, TN))
            dc_parts.append(dc_acc + _dot(dcb, bg, NN))
        g_ref[...] = jnp.stack(g_new)
        dxbc_ref[...] = jnp.concatenate(dx_parts + db_parts + dc_parts, axis=1)

        dcs = dcs_cols - dcs_rows.T
        dad = _dot_exact((c >= r).astype(F32), dcs)
        ddt = dad * a + ddt_x
        ddtr = jnp.where(c < 16, ddt * _sigmoid(z), 0.0)
        ddt_ref[...] = ddtr.astype(BF16)
        r8 = lax.broadcasted_iota(jnp.int32, (8, 128), 0)
        dsc_ref[...] += (jnp.where(r8 == 0, jnp.sum(ddtr, axis=0, keepdims=True), 0.0)
                         + jnp.where(r8 == 1, jnp.sum(dad * dt, axis=0, keepdims=True) * a, 0.0)
                         + jnp.where(r8 == 2, dd_row, 0.0))

    rev = NCH - 1
    return _call(
        body, "ssd_bwd", (NCH,),
        [pl.BlockSpec((128, SSM_W), lambda i: (rev - i, 1)),
         pl.BlockSpec((128, SSM_W), lambda i: (rev - i, 0)),
         pl.BlockSpec((128, SSM_W), lambda i: (rev - i, 0)),
         pl.BlockSpec((128, 256), lambda i: (rev - i, 4)), pl.BlockSpec((128, 256), lambda i: (rev - i, 5)),
         pl.BlockSpec((128, 128), lambda i: (rev - i, COL_DT // 128)),
         pl.BlockSpec((128, SSM_W), lambda i: (rev - i, 3)),
         pl.BlockSpec((None, NPAIR, 128, 128), lambda i: (rev - i, 0, 0, 0)),
         pl.BlockSpec((1, 128), lambda i: (0, 0)), pl.BlockSpec((1, 128), lambda i: (0, 0)),
         pl.BlockSpec((1, SSM_W), lambda i: (0, 0)), pl.BlockSpec((1, SSM_W), lambda i: (0, 0))],
        [pl.BlockSpec((128, CONV_C), lambda i: (rev - i, 0)),
         pl.BlockSpec((128, SSM_W), lambda i: (rev - i, 0)),
         pl.BlockSpec((128, 128), lambda i: (rev - i, 0)),
         pl.BlockSpec((1, SSM_W), lambda i: (0, 0)), pl.BlockSpec((8, 128), lambda i: (0, 0))],
        [jax.ShapeDtypeStruct((S, CONV_C), F32), jax.ShapeDtypeStruct((S, SSM_W), BF16),
         jax.ShapeDtypeStruct((S, 128), BF16), jax.ShapeDtypeStruct((1, SSM_W), F32),
         jax.ShapeDtypeStruct((8, 128), F32)],
        (dmixed, y, xbc, xbc, xbc, proj, proj, hprev, dtb, alog, dskip_l, ssmw),
        [pltpu.VMEM((NPAIR, 128, 128), F32)], ("arbitrary",), rider)


def _cast_stack(name, slot, arrs, tr, tc):
    n = len(arrs)
    rows, cols = arrs[0].shape

    def body(s_ref, *refs):
        for i in range(n):
            refs[n][i] = refs[i][...].astype(BF16)

    return pl.pallas_call(
        body, name=name,
        grid_spec=pltpu.PrefetchScalarGridSpec(
            num_scalar_prefetch=1, grid=(rows // tr, cols // tc),
            in_specs=[pl.BlockSpec((tr, tc), lambda i, j, sr: (i, j))] * n,
            out_specs=pl.BlockSpec((None, n, tr, tc), lambda i, j, sr: (sr[0], 0, i, j))),
        out_shape=jax.ShapeDtypeStruct((NSH, n, rows, cols), BF16),
        compiler_params=_cparams("parallel", "parallel"),
    )(slot, *arrs)


def _pair_sum(name, c_idx, ps, rs, th):
    n = len(ps)
    _, rows, _ = ps[0].shape

    def body(c_ref, *refs):
        for i in range(n):
            refs[2 * n + i][...] = (refs[i][...].astype(F32) + refs[n + i][...].astype(F32)).astype(BF16)

    spec = pl.BlockSpec((None, th, HALF), lambda s, i, cr: (s, i, 0))
    return pl.pallas_call(
        body, name=name,
        grid_spec=pltpu.PrefetchScalarGridSpec(
            num_scalar_prefetch=1, grid=(NSH, rows // th),
            in_specs=[pl.BlockSpec((None, th, HALF), lambda s, i, cr: (s, i, cr[0]))] * n + [spec] * n,
            out_specs=[spec] * n),
        out_shape=[jax.ShapeDtypeStruct((NSH, rows, HALF), BF16)] * n,
        compiler_params=_cparams("parallel", "parallel"),
    )(c_idx, *ps, *rs)


def _chip_sum(name, place, cs, ts, th):
    n = len(ts)
    _, rows, _ = ts[0].shape

    def body(p_ref, *refs):
        for i in range(n):
            t = refs[n + i][...].astype(F32)
            refs[2 * n + i][...] = ((refs[i][...].astype(F32) + t[0]) + t[1]) + t[2]

    return pl.pallas_call(
        body, name=name,
        grid_spec=pltpu.PrefetchScalarGridSpec(
            num_scalar_prefetch=1, grid=(rows // th,),
            in_specs=[pl.BlockSpec((None, th, HALF), lambda i, pr: (pr[0], i, 0))] * n
            + [pl.BlockSpec((3, th, HALF), lambda i, pr: (0, i, 0))] * n,
            out_specs=[pl.BlockSpec((th, HALF), lambda i, pr: (i, pr[1]))] * n),
        out_shape=[jax.ShapeDtypeStruct((rows, D), F32)] * n, compiler_params=_cparams("parallel"),
    )(place, *cs, *ts)


def _adamw(name, ws, gs, ms, vs, tr, tc):
    n = len(ws)
    shape = ws[0].shape
    rows, cols, mid = shape[0], shape[-1], shape[1:-1]
    c1 = 1.0 / (1.0 - ADAM_B1 ** ADAM_STEP)
    c2 = 1.0 / (1.0 - ADAM_B2 ** ADAM_STEP)

    def body(*refs):
        for i in range(n):
            w, g, m, v = (refs[k * n + i][...] for k in range(4))
            m2 = ADAM_B1 * m + (1.0 - ADAM_B1) * g
            v2 = ADAM_B2 * v + (1.0 - ADAM_B2) * (g * g)
            refs[4 * n + 3 * i][...] = -ADAM_LR * ((m2 * c1) / (jnp.sqrt(v2 * c2) + ADAM_EPS) + ADAM_WD * w)
            refs[4 * n + 3 * i + 1][...] = m2
            refs[4 * n + 3 * i + 2][...] = v2

    spec = pl.BlockSpec((tr,) + mid + (tc,), lambda i, j: (i,) + (0,) * len(mid) + (j,))
    outs = pl.pallas_call(
        body, name=name, grid=(rows // tr, cols // tc), in_specs=[spec] * (4 * n), out_specs=[spec] * (3 * n),
        out_shape=[jax.ShapeDtypeStruct(shape, F32)] * (3 * n),
        compiler_params=_cparams("parallel", "parallel"),
    )(*ws, *gs, *ms, *vs)
    return [tuple(outs[3 * i:3 * i + 3]) for i in range(n)]


def _place():
    x, y, c = lax.axis_index("x"), lax.axis_index("y"), lax.axis_index("c")
    chips = [(1 - x, y), (x, 1 - y), (1 - x, 1 - y)]
    return x, y, c, chips


def _any_specs(n):
    return [pl.BlockSpec(memory_space=pl.ANY)] * n


def _rcopy(src, dst, send_sem, recv_sem, dev):
    return pltpu.make_async_remote_copy(src_ref=src, dst_ref=dst, send_sem=send_sem, recv_sem=recv_sem,
                                        device_id=dev, device_id_type=MESH)


def _gather_rider(bufs, views):
    n = len(bufs)

    def start(rin, rout, sems):
        send, recv = sems[0], sems[1]
        x, y, c, chips = _place()
        for j, chip in enumerate(chips):
            for b in range(n):
                mine = views[b](rout[b], 2 * x + y, c)
                _rcopy(mine, mine, send.at[j * n + b], recv.at[j * n + b], (chip[0], chip[1], c)).start()

    def finish(rin, rout, sems):
        send, recv, fsend, frecv = sems
        x, y, c, chips = _place()
        passed = []
        for j, chip in enumerate(chips):
            for b in range(n):
                landed = views[b](rout[b], 2 * chip[0] + chip[1], c)
                _rcopy(landed, landed, send.at[j * n + b], recv.at[j * n + b], (x, y, c)).wait_recv()
                fw = _rcopy(landed, landed, fsend.at[j * n + b], frecv.at[j * n + b], (x, y, 1 - c))
                fw.start()
                passed.append(fw)
        for j, chip in enumerate(chips):
            for b in range(n):
                other = views[b](rout[b], 2 * chip[0] + chip[1], 1 - c)
                _rcopy(other, other, fsend.at[j * n + b], frecv.at[j * n + b], (x, y, c)).wait_recv()
        for j, chip in enumerate(chips):
            for b in range(n):
                mine = views[b](rout[b], 2 * x + y, c)
                _rcopy(mine, mine, send.at[j * n + b], recv.at[j * n + b], (x, y, c)).wait_send()
        for fw in passed:
            fw.wait_send()

    return _Rider(list(bufs), [jax.ShapeDtypeStruct(a.shape, a.dtype) for a in bufs], {b: b for b in range(n)},
                  [pltpu.SemaphoreType.DMA((3 * n,))] * 4, start, finish)


def _small_gather_rider(cw):
    def descs(rin, rout, sems, x, y, c, chips):
        return [_rcopy(rin[0], rout[0].at[2 * x + y], sems[1].at[j], sems[2].at[j], (chip[0], chip[1], c))
                for j, chip in enumerate(chips)]

    def start(rin, rout, sems):
        x, y, c, chips = _place()
        pltpu.make_async_copy(rin[0], rout[0].at[2 * x + y], sems[0].at[0]).start()
        for cp in descs(rin, rout, sems, x, y, c, chips):
            cp.start()

    def finish(rin, rout, sems):
        x, y, c, chips = _place()
        for j, chip in enumerate(chips):
            _rcopy(rin[0], rout[0].at[2 * chip[0] + chip[1]], sems[1].at[j], sems[2].at[j], (x, y, c)).wait_recv()
        for cp in descs(rin, rout, sems, x, y, c, chips):
            cp.wait_send()
        pltpu.make_async_copy(rin[0], rout[0].at[2 * x + y], sems[0].at[0]).wait()

    return _Rider([cw], [jax.ShapeDtypeStruct((NSH,) + cw.shape, cw.dtype)], {},
                  [pltpu.SemaphoreType.DMA((1,)), pltpu.SemaphoreType.DMA((3,)), pltpu.SemaphoreType.DMA((3,))],
                  start, finish)


def _to_chips_rider(cs):
    n = len(cs)

    def descs(rin, rout, sems):
        x, y, c, chips = _place()
        return [_rcopy(rin[i].at[2 * chip[0] + chip[1]], rout[i].at[j], sems[0].at[j * n + i], sems[1].at[j * n + i],
                       (chip[0], chip[1], c)) for j, chip in enumerate(chips) for i in range(n)]

    def start(rin, rout, sems):
        for cp in descs(rin, rout, sems):
            cp.start()

    def finish(rin, rout, sems):
        for cp in descs(rin, rout, sems):
            cp.wait()

    return _Rider(list(cs), [jax.ShapeDtypeStruct((3,) + a.shape[1:], a.dtype) for a in cs], {},
                  [pltpu.SemaphoreType.DMA((3 * n,))] * 2, start, finish)


def _run_riders(name, riders):
    n_in = [len(r.operands) for r in riders]
    n_out = [len(r.out_shapes) for r in riders]
    n_sem = [len(r.sems) for r in riders]

    def body(*refs):
        parts, at = [], 0
        for counts in (n_in, n_out, n_sem):
            group = []
            for k in counts:
                group.append(refs[at:at + k])
                at += k
            parts.append(group)
        for i, r in enumerate(riders):
            r.start(parts[0][i], parts[1][i], parts[2][i])
        for i, r in enumerate(riders):
            r.finish(parts[0][i], parts[1][i], parts[2][i])

    aliases = {}
    for i, r in enumerate(riders):
        for k, v in r.aliases.items():
            aliases[sum(n_in[:i]) + k] = sum(n_out[:i]) + v
    res = pl.pallas_call(
        body, name=name, in_specs=_any_specs(sum(n_in)), out_specs=_any_specs(sum(n_out)),
        out_shape=[s for r in riders for s in r.out_shapes], input_output_aliases=aliases,
        scratch_shapes=[s for r in riders for s in r.sems],
    )(*[a for r in riders for a in r.operands])
    out, at = [], 0
    for k in n_out:
        out.append(list(res[at:at + k]))
        at += k
    return out


def _to_sibling(name, ps):
    n = len(ps)

    def body(*refs):
        src, dst, send, recv = refs[:n], refs[n:2 * n], refs[2 * n], refs[2 * n + 1]
        x, y, c, _ = _place()
        cps = [pltpu.make_async_remote_copy(
            src_ref=src[i].at[:, :, pl.ds((1 - c) * HALF, HALF)], dst_ref=dst[i], send_sem=send.at[i],
            recv_sem=recv.at[i], device_id=(x, y, 1 - c), device_id_type=MESH) for i in range(n)]
        for cp in cps:
            cp.start()
        for cp in cps:
            cp.wait()

    outs = [jax.ShapeDtypeStruct(p.shape[:2] + (HALF,), p.dtype) for p in ps]
    return pl.pallas_call(
        body, name=name, in_specs=_any_specs(n), out_specs=_any_specs(n), out_shape=outs,
        scratch_shapes=[pltpu.SemaphoreType.DMA((n,)), pltpu.SemaphoreType.DMA((n,))],
    )(*ps)


def _swap_halves(gs):
    n = len(gs)

    def body(*refs):
        dst, send, recv = refs[n:2 * n], refs[2 * n], refs[2 * n + 1]
        x, y, c, _ = _place()
        cps = []
        for i in range(n):
            mine = dst[i].at[:, pl.ds(c * HALF, HALF)]
            cps.append(pltpu.make_async_remote_copy(
                src_ref=mine, dst_ref=mine, send_sem=send.at[i], recv_sem=recv.at[i],
                device_id=(x, y, 1 - c), device_id_type=MESH))
        for cp in cps:
            cp.start()
        for i in range(n):
            other = dst[i].at[:, pl.ds((1 - c) * HALF, HALF)]
            pltpu.make_async_remote_copy(
                src_ref=other, dst_ref=other, send_sem=send.at[i], recv_sem=recv.at[i],
                device_id=(x, y, c), device_id_type=MESH).wait_recv()
        for cp in cps:
            cp.wait_send()

    return pl.pallas_call(
        body, name="grads_swap_halves", in_specs=_any_specs(n), out_specs=_any_specs(n),
        out_shape=[jax.ShapeDtypeStruct(g.shape, g.dtype) for g in gs],
        input_output_aliases={i: i for i in range(n)},
        scratch_shapes=[pltpu.SemaphoreType.DMA((n,)), pltpu.SemaphoreType.DMA((n,))],
    )(*gs)


SMALL_ROWS = 16


def _allreduce_small(vec):
    def body(v_ref, o_ref, buf, send, recv):
        x, y, c, _ = _place()
        me = 4 * x + 2 * y + c
        buf[me] = v_ref[...]
        cps = []
        for k in range(1, 8):
            peer = (x ^ (k >> 2), y ^ ((k >> 1) & 1), c ^ (k & 1))
            cps.append(pltpu.make_async_remote_copy(
                src_ref=v_ref, dst_ref=buf.at[me], send_sem=send.at[k - 1], recv_sem=recv.at[k - 1],
                device_id=peer, device_id_type=MESH))
        for cp in cps:
            cp.start()
        for k in range(1, 8):
            pltpu.make_async_remote_copy(
                src_ref=v_ref, dst_ref=buf.at[me ^ k], send_sem=send.at[k - 1], recv_sem=recv.at[k - 1],
                device_id=(x, y, c), device_id_type=MESH).wait_recv()
        for cp in cps:
            cp.wait_send()
        t = buf[0]
        for d in range(1, 8):
            t = t + buf[d]
        o_ref[...] = t

    return pl.pallas_call(
        body, name="allreduce_small",
        in_specs=[pl.BlockSpec(memory_space=pltpu.VMEM)], out_specs=pl.BlockSpec(memory_space=pltpu.VMEM),
        out_shape=jax.ShapeDtypeStruct((SMALL_ROWS, D), F32),
        scratch_shapes=[pltpu.VMEM((8, SMALL_ROWS, D), F32), pltpu.SemaphoreType.DMA((7,)),
                        pltpu.SemaphoreType.DMA((7,))],
    )(vec)


def _col_half(ref, slot, hc):
    return ref.at[slot, :, pl.ds(hc * HALF, HALF)]


def _stack_half(ref, slot, hc):
    return ref.at[slot, :, :, pl.ds(hc * HALF, HALF)]


def _row_tile(rows):
    for t in range(512, 15, -16):
        if rows % t == 0:
            return t
    return rows


def _same_shape_runs(arrs):
    runs, a = [], 0
    for b in range(1, len(arrs) + 1):
        if b == len(arrs) or arrs[b].shape != arrs[a].shape:
            runs.append((a, b))
            a = b
    return runs


class _Comm:
    def __init__(self):
        x, y, c = lax.axis_index("x"), lax.axis_index("y"), lax.axis_index("c")
        self.c_idx = jnp.reshape(c, (1,)).astype(jnp.int32)
        self.place = jnp.stack([2 * x + y, c]).astype(jnp.int32)
        self.groups = {}

    @staticmethod
    def gather(*bufs):
        return _gather_rider(list(bufs), [_col_half if b.ndim == 3 else _stack_half for b in bufs])

    def reduce_rider(self, tag, names, ps):
        rs = _to_sibling("grads_to_sibling_" + tag, ps)
        csums = []
        for a, b in _same_shape_runs(ps):
            csums += _pair_sum("pair_sum_%s%d" % (tag, a), self.c_idx, ps[a:b], rs[a:b], _row_tile(ps[a].shape[1]))
        self.groups[tag] = [names, csums, None]
        return _to_chips_rider(csums)

    def landed(self, tag, ts):
        self.groups[tag][2] = ts

    def finish(self):
        names, halves = [], []
        for tag, (group_names, csums, ts) in self.groups.items():
            names += group_names
            for a, b in _same_shape_runs(csums):
                halves += _chip_sum("chip_sum_%s%d" % (tag, a), self.place, csums[a:b], ts[a:b],
                                    _row_tile(csums[a].shape[1]))
        return dict(zip(names, _swap_halves(halves)))


ROPE_THETA = 10000.0
SMALL_1K = ("ffn1_pre_norm", "ffn1_post_norm", "mix_pre_norm", "ssm_norm", "mix_post_norm",
            "ffn2_pre_norm", "ffn2_post_norm")
SMALL_16 = ("dt_bias", "a_log", "d_skip")
OFF_CONVB = 7 * D
OFF_16 = OFF_CONVB + CONV_C
OFF_CONVW = OFF_16 + 48
OFF_LOSS = OFF_CONVW + CONV_K * CONV_C
SMALL_LEN = SMALL_ROWS * D


def _sds(shape, dtype):
    return jax.ShapeDtypeStruct(shape, dtype)


def _ridden(res, rider):
    return res if rider is not None else (res, None)


def _ffn_down(name, act, w, rider=None):
    return _mm(name, [act, w.dn], NN, (S // TS,),
               [pl.BlockSpec((NSH, TS, FS), lambda i: (0, i, 0)),
                pl.BlockSpec((NSH, None, FS, D), lambda i: (0, w.d0, 0, 0))],
               pl.BlockSpec((TS, D), lambda i: (i, 0)), _sds((S, D), F32), rider)


def _ffn_dw(name, a, b, rider=None):
    return _mm(name, [a, b], TN, (NSH,),
               [pl.BlockSpec((None, S, FS), lambda s: (s, 0, 0)), pl.BlockSpec((S, D), lambda s: (0, 0))],
               pl.BlockSpec((None, FS, D), lambda s: (s, 0, 0)), _sds((NSH, FS, D), BF16), rider)


def _ffn_dn(name, dgate, dup, w, rider=None):
    a2 = pl.BlockSpec((NSH, TS, FS), lambda i: (0, i, 0))
    return _mm(name, [dgate, w.gu, dup, w.gu], NN, (S // TS,),
               [a2, pl.BlockSpec((NSH, None, FS, D), lambda i: (0, w.g0, 0, 0)),
                a2, pl.BlockSpec((NSH, None, FS, D), lambda i: (0, w.g0 + 1, 0, 0))],
               pl.BlockSpec((TS, D), lambda i: (i, 0)), _sds((S, D), F32), rider)


def _heads(t, n):
    return t.reshape(S, n, HD).transpose(1, 0, 2)


def _unheads(t):
    return t.transpose(1, 0, 2).reshape(S, t.shape[0] * HD)


def _heads_t(t, n):
    return t.reshape(S, n, HD).transpose(1, 2, 0)


def _blocks5(t):
    return t.reshape(NCH, 128, NKV, NQ_PER_KV, HD)


def _to_blocks_t(t):
    return _blocks5(t).transpose(2, 0, 4, 3, 1).reshape(NKV, NCH, HD, QROWS)


def _from_blocks_t(t):
    return t.reshape(NKV, NCH, HD, NQ_PER_KV, 128).transpose(1, 4, 0, 3, 2).reshape(S, D)


def _pad128(v):
    return jnp.pad(v, ((0, 0), (0, 128 - v.shape[1])))


def _local_step(x, positions, tgt, sp, gu1, d1, f2, wint, wout, convw, comm=None):
    inv_freq = ROPE_THETA ** (-jnp.arange(0, HD, 2, dtype=F32) / HD)
    ang = positions.astype(F32)[:, None] * inv_freq
    ang = jnp.concatenate([ang, ang, ang, ang], axis=-1)
    cos, sin = jnp.cos(ang), jnp.sin(ang)
    dtb, alog = _pad128(sp["dt_bias"]), _pad128(sp["a_log"])
    dskip_l = jnp.repeat(sp["d_skip"], HD, axis=1)
    convb = sp["conv_b"]

    n1 = _prenorm("prenorm1", x, sp["ffn1_pre_norm"])
    rider = comm.gather(d1) if comm else None
    (fg1, fu1, act1), got = _ridden(_ffn_up("ffn1_up", n1, _FfnW(gu1, 0, d1, 0), rider), rider)
    if comm:
        d1, = got
    w1 = _FfnW(gu1, 0, d1, 0)
    rider = comm.gather(wint) if comm else None
    h1, got = _ridden(_ffn_down("ffn1_down", act1, w1, rider), rider)
    if comm:
        wint, = got
    wint_pad = jnp.pad(wint.reshape(WIN_COLS, D), ((0, WIN_PAD - WIN_COLS), (0, 0)))
    x1, n2 = _postres("postres1", x, h1, sp["ffn1_post_norm"], 0.5, sp["mix_pre_norm"])

    pw = WIN_PAD // 3
    proj = _mm("in_proj", [n2, wint_pad], NT, (S // TS, 3),
               [pl.BlockSpec((TS, D), lambda i, j: (i, 0)), pl.BlockSpec((pw, D), lambda i, j: (j, 0))],
               pl.BlockSpec((TS, pw), lambda i, j: (i, j)), _sds((S, WIN_PAD), F32))
    q_rot = _rope("rope_q", proj, 0, D, cos, sin, 1.0, HD ** -0.5)
    k_rot = _rope("rope_k", proj, D // KVW, KVW, cos, sin, 1.0, 1.0)
    v_bf = proj[:, D + KVW:D + 2 * KVW].astype(BF16)
    qt, kh, vh = _to_blocks_t(q_rot), _heads(k_rot, NKV), _heads(v_bf, NKV)
    kt, vt = _heads_t(k_rot, NKV), _heads_t(v_bf, NKV)
    bias = _bias_table()
    rider = comm.gather(f2, wout) if comm else None
    (ot, lse), got = _ridden(_attn_fwd(qt, kh, vt, bias, rider), rider)
    if comm:
        f2, wout = got
    w2 = _FfnW(f2, 0, f2, 2)
    wout = wout.reshape(2 * D, D)
    attn = _from_blocks_t(ot).astype(BF16)
    xbc = _conv_fwd(proj, convw, convb)
    y, yn, hprev = _ssd_fwd(xbc, proj, dtb, alog, dskip_l, sp["ssm_norm"])
    mixed = jnp.concatenate([attn, yn], axis=1)
    h2 = _mm("out_proj", [mixed, wout], NN, (S // TS,),
             [pl.BlockSpec((TS, 2 * D), lambda i: (i, 0)), pl.BlockSpec((2 * D, D), lambda i: (0, 0))],
             pl.BlockSpec((TS, D), lambda i: (i, 0)), _sds((S, D), F32))
    x2, n3 = _postres("postres2", x1, h2, sp["mix_post_norm"], 1.0, sp["ffn2_pre_norm"])

    fg2, fu2, act2 = _ffn_up("ffn2_up", n3, w2)
    h3 = _ffn_down("ffn2_down", act2, w2)
    dy, dh3, dp3, loss = _final(x2, h3, sp["ffn2_post_norm"], tgt, 0.5)

    dgate2, dup2 = _ffn_dact("ffn2_dact", dh3, w2, fg2, fu2)
    dws2 = [_ffn_dw("ffn2_dwg", dgate2, n3), _ffn_dw("ffn2_dwu", dup2, n3), _ffn_dw("ffn2_dwd", act2, dh3)]
    dn3 = _ffn_dn("ffn2_dn", dgate2, dup2, w2)
    dx2, dh2, dg3, dp2 = _mid_bwd("mid_bwd2", dy, dn3, x2, sp["ffn2_pre_norm"], h2, sp["mix_post_norm"], 1.0)

    dmixed = _mm("out_proj_dx", [dh2, wout], NT, (S // TS,),
                 [pl.BlockSpec((TS, D), lambda i: (i, 0)), pl.BlockSpec((2 * D, D), lambda i: (0, 0))],
                 pl.BlockSpec((TS, 2 * D), lambda i: (i, 0)), _sds((S, 2 * D), F32))
    dwout = _mm("out_proj_dw", [mixed, dh2], TN, (2,),
                [pl.BlockSpec((S, D), lambda m: (0, m)), pl.BlockSpec((S, D), lambda m: (0, 0))],
                pl.BlockSpec((D, D), lambda m: (m, 0)), _sds((2 * D, D), BF16))
    dwout = dwout.reshape(NSH, 2 * D // NSH, D)
    rider = comm.reduce_rider("a", BIG[3:6] + ("w_out",), dws2 + [dwout]) if comm else None
    (dxbc, dz, ddt, dssm, dsc), got = _ridden(
        _ssd_bwd(dmixed, y, xbc, proj, hprev, dtb, alog, dskip_l, sp["ssm_norm"], rider), rider)
    if comm:
        comm.landed("a", got)
    du, dcw8, dcb = _conv_bwd(dxbc, proj, convw, convb)
    do_bf = dmixed[:, :D].astype(BF16)
    dot_ = _to_blocks_t(do_bf)
    dqt, dkh, dvh = _attn_bwd(qt, kh, kt, vh, dot_, lse, _attn_delta(ot, dot_), bias)
    dq = _rope("rope_dq", _from_blocks_t(dqt), 0, D, cos, sin, -1.0, HD ** -0.5)
    dk = _rope("rope_dk", _unheads(dkh), 0, KVW, cos, sin, -1.0, 1.0)
    dproj = jnp.concatenate([dq, dk, _unheads(dvh).astype(BF16), du, dz, ddt], axis=1)
    dwint = _mm("in_proj_dw", [dproj, n2], TN, (3,),
                [pl.BlockSpec((S, pw), lambda j: (0, j)), pl.BlockSpec((S, D), lambda j: (0, 0))],
                pl.BlockSpec((pw, D), lambda j: (j, 0)), _sds((WIN_PAD, D), BF16))
    dwint = dwint[:WIN_COLS].reshape(NSH, WIN_SH, D)

    def riding(tag, names, ps, call):
        rider = comm.reduce_rider(tag, names, ps) if comm else None
        res, got = _ridden(call(rider), rider)
        if comm:
            comm.landed(tag, got)
        return res

    dn2 = riding("b", ("w_in",), [dwint], lambda rider: _mm(
        "in_proj_dx", [dproj, wint_pad], NN, (S // TS,),
        [pl.BlockSpec((TS, WIN_PAD), lambda i: (i, 0)), pl.BlockSpec((WIN_PAD, D), lambda i: (0, 0))],
        pl.BlockSpec((TS, D), lambda i: (i, 0)), _sds((S, D), F32), rider))
    dx1, dh1, dg2, dp1 = _mid_bwd("mid_bwd1", dx2, dn2, x1, sp["mix_pre_norm"], h1, sp["ffn1_post_norm"], 0.5)

    dwd1 = _ffn_dw("ffn1_dwd", act1, dh1)
    dgate1, dup1 = riding("d", BIG[2:3], [dwd1], lambda rider: _ffn_dact("ffn1_dact", dh1, w1, fg1, fu1, rider))
    dwg1 = _ffn_dw("ffn1_dwg", dgate1, n1)
    dwu1 = riding("g", BIG[0:1], [dwg1], lambda rider: _ffn_dw("ffn1_dwu", dup1, n1, rider))
    dn1 = riding("u", BIG[1:2], [dwu1], lambda rider: _ffn_dn("ffn1_dn", dgate1, dup1, w1, rider))
    dws1 = [dwg1, dwu1, dwd1]
    grad_x, dg1 = _first_bwd(dx1, dn1, x, sp["ffn1_pre_norm"])

    small = jnp.concatenate([
        dg1[0], dp1[0], dg2[0], dssm[0], dp2[0], dg3[0], dp3[0], dcb[0],
        dsc[0, :16], dsc[1, :16], dsc[2, :16], dcw8[:CONV_K].reshape(-1), loss[0, :1]])
    small = jnp.pad(small, (0, SMALL_LEN - small.shape[0])).reshape(SMALL_ROWS, D)
    if comm is None:
        return grad_x, dws1 + dws2 + [dwint, dwout], small
    return grad_x, comm.finish(), small


WEIGHTS = ("ffn1_pre_norm", "ffn1_w_gate", "ffn1_w_up", "ffn1_w_down", "ffn1_post_norm", "mix_pre_norm", "w_in",
           "conv_w", "conv_b", "dt_bias", "a_log", "d_skip", "ssm_norm", "w_out", "mix_post_norm", "ffn2_pre_norm",
           "ffn2_w_gate", "ffn2_w_up", "ffn2_w_down", "ffn2_post_norm")
BIG = ("ffn1_w_gate", "ffn1_w_up", "ffn1_w_down", "ffn2_w_gate", "ffn2_w_up", "ffn2_w_down", "w_in", "w_out")
TRANSPOSED = ("ffn1_w_gate", "ffn1_w_up", "ffn2_w_gate", "ffn2_w_up", "w_in")
SMALL_ORDER = SMALL_1K + ("conv_b",) + SMALL_16
CONVW_SH = CONV_C // NSH


def _shard2d(t, name):
    return t[0].T if name in TRANSPOSED else t[0]


def _unshard2d(t, name):
    return (t.T if name in TRANSPOSED else t)[None]


def _rows3d(t):
    return t.transpose(2, 0, 1)


def _pack_small(d, prefix, shard_of_convw):
    flat = jnp.concatenate([d[prefix + n][0] for n in SMALL_ORDER] + [shard_of_convw.reshape(-1)])
    return jnp.pad(flat, (0, SMALL_LEN - flat.shape[0])).reshape(SMALL_ROWS, D)


def _unpack_small(block, like):
    flat = block.reshape(-1)
    out, off = {}, 0
    for n in SMALL_ORDER:
        size = like[n].shape[1]
        out[n] = flat[off:off + size].reshape(1, size)
        off += size
    out["conv_w"] = flat[off:off + CONV_K * CONVW_SH].reshape(1, CONV_K, CONVW_SH)
    return out


def kernel(x, positions, ffn1_pre_norm, ffn1_w_gate, ffn1_w_up, ffn1_w_down, ffn1_post_norm, mix_pre_norm, w_in, conv_w, conv_b, dt_bias, a_log, d_skip, ssm_norm, w_out, mix_post_norm, ffn2_pre_norm, ffn2_w_gate, ffn2_w_up, ffn2_w_down, ffn2_post_norm, loss_target, m_ffn1_pre_norm, m_ffn1_w_gate, m_ffn1_w_up, m_ffn1_w_down, m_ffn1_post_norm, m_mix_pre_norm, m_w_in, m_conv_w, m_conv_b, m_dt_bias, m_a_log, m_d_skip, m_ssm_norm, m_w_out, m_mix_post_norm, m_ffn2_pre_norm, m_ffn2_w_gate, m_ffn2_w_up, m_ffn2_w_down, m_ffn2_post_norm, v_ffn1_pre_norm, v_ffn1_w_gate, v_ffn1_w_up, v_ffn1_w_down, v_ffn1_post_norm, v_mix_pre_norm, v_w_in, v_conv_w, v_conv_b, v_dt_bias, v_a_log, v_d_skip, v_ssm_norm, v_w_out, v_mix_post_norm, v_ffn2_pre_norm, v_ffn2_w_gate, v_ffn2_w_up, v_ffn2_w_down, v_ffn2_post_norm):
    given = dict(locals())
    xi, yi = lax.axis_index("x"), lax.axis_index("y")

    shard = jnp.reshape(2 * xi + yi, (1,)).astype(jnp.int32)
    big = {p + n: _shard2d(given[p + n], n) for n in BIG for p in ("", "m_", "v_")}
    gu1 = _cast_stack("cast_ffn1_gate_up", shard, [big[n] for n in BIG[0:2]], 176, D)
    d1 = _cast_stack("cast_ffn1_down", shard, [big[BIG[2]]], 176, D)
    f2 = _cast_stack("cast_ffn2", shard, [big[n] for n in BIG[3:6]], 176, D)
    winsh = _cast_stack("cast_w_in", shard, [big["w_in"]], WIN_SH, 256).reshape(NSH, WIN_SH, D)
    woutsh = _cast_stack("cast_w_out", shard, [big["w_out"]], 256, D).reshape(NSH, 2 * D // NSH, D)
    comm = _Comm()
    (gu1,), (cwf,) = _run_riders("gather_ffn1_gate_up", [comm.gather(gu1), _small_gather_rider(conv_w[0])])
    convw = cwf.transpose(1, 0, 2).reshape(CONV_K, CONV_C)

    sp = {n: given[n] for n in SMALL_ORDER}
    grad_x, big_grads, small = _local_step(x[0], positions[0], loss_target[0], sp, gu1, d1, f2, winsh, woutsh,
                                           convw, comm)

    tot = _allreduce_small(small).reshape(-1)
    loss = tot[OFF_LOSS]
    small_grads, off = {}, 0
    for n in SMALL_ORDER:
        size = given[n].shape[1]
        small_grads[n] = tot[off:off + size].reshape(1, size)
        off += size
    dconvw = tot[OFF_CONVW:OFF_CONVW + CONV_K * CONV_C].reshape(CONV_K, NSH, CONVW_SH)
    dconvw = lax.dynamic_index_in_dim(dconvw, 2 * xi + yi, axis=1, keepdims=False)
    small_grads["conv_w"] = dconvw.reshape(1, CONV_K, CONVW_SH)

    upd = {}
    for names, tr in ((BIG[0:3], 176), (BIG[3:6], 176), (BIG[7:8], 256)):
        res = _adamw("adamw_" + names[0], [big[n] for n in names], [big_grads[n] for n in names],
                     [big["m_" + n] for n in names], [big["v_" + n] for n in names], tr, D)
        for n, r in zip(names, res):
            upd[n] = tuple(_unshard2d(t, n) for t in r)
    g_win = big_grads["w_in"].reshape(WIN_SH, 1, D)
    res, = _adamw("adamw_w_in", [_rows3d(w_in)], [g_win], [_rows3d(m_w_in)], [_rows3d(v_w_in)], WIN_SH // 4, D)
    upd["w_in"] = tuple(t.transpose(1, 2, 0) for t in res)
    (dl, m2, v2), = _adamw(
        "adamw_small", [_pack_small(given, "", conv_w[0])], [_pack_small(small_grads, "", dconvw)],
        [_pack_small(given, "m_", m_conv_w[0])], [_pack_small(given, "v_", v_conv_w[0])], SMALL_ROWS, D)
    dl, m2, v2 = (_unpack_small(t, given) for t in (dl, m2, v2))
    for n in SMALL_ORDER + ("conv_w",):
        upd[n] = (dl[n], m2[n], v2[n])

    grads = dict(small_grads)
    grads.update({n: _unshard2d(g, n) for n, g in big_grads.items() if n != "w_in"})
    grads["w_in"] = g_win.transpose(1, 2, 0)
    return (loss, grad_x[None], *[grads[n] for n in WEIGHTS], *[upd[n][0] for n in WEIGHTS],
            *[upd[n][1] for n in WEIGHTS], *[upd[n][2] for n in WEIGHTS])
```

```python
import functools
import typing

import jax
import jax.numpy as jnp
from jax import lax
from jax.experimental import pallas as pl
from jax.experimental.pallas import tpu as pltpu

F32 = jnp.float32
BF16 = jnp.bfloat16

S = 2048
D = 1024
FF = 2816
NSH = 4
FS = FF // NSH
HALF = D // 2
HD = 64
NKV = 4
NQ_PER_KV = 4
KVW = NKV * HD
QCOLS = NQ_PER_KV * HD
CONV_C = 1536
CONV_K = 4
SSM_W = 1024
NST = 128
NCH = S // 128
WIN_COLS = 4112
WIN_SH = WIN_COLS // NSH
WIN_PAD = 4224
COL_DT = 4096
EPS = 1e-6
NEG = -1e30

ADAM_LR = 0.001
ADAM_B1 = 0.9
ADAM_B2 = 0.999
ADAM_EPS = 1e-08
ADAM_WD = 0.01
ADAM_STEP = 10

VMEM_LIMIT = 56 * 1024 * 1024
TS = 512
TR = 256

NN = (((1,), (0,)), ((), ()))
NT = (((1,), (1,)), ((), ()))
TN = (((0,), (0,)), ((), ()))
MESH = pl.DeviceIdType.MESH


def _cparams(*sem):
    return pltpu.CompilerParams(dimension_semantics=sem, vmem_limit_bytes=VMEM_LIMIT)


def _dot(a, b, dims):
    return lax.dot_general(a.astype(BF16), b.astype(BF16), dims, preferred_element_type=F32)


def _dot_exact(a, b):
    return lax.dot_general(a, b, NN, precision=lax.Precision.HIGHEST, preferred_element_type=F32)


def _sigmoid(v):
    return 1.0 / (1.0 + jnp.exp(-v))


class _Rider(typing.NamedTuple):
    operands: list
    out_shapes: list
    aliases: dict
    sems: list
    start: typing.Callable
    finish: typing.Callable


def _call(body, name, grid, in_specs, out_specs, out_shape, operands, scratch=(), sem=(), rider=None):
    multi = isinstance(out_shape, (list, tuple))
    if rider is None:
        return pl.pallas_call(
            body, name=name, grid=grid, in_specs=in_specs, out_specs=out_specs, out_shape=out_shape,
            scratch_shapes=list(scratch), compiler_params=_cparams(*sem))(*operands)
    outs = list(out_shape) if multi else [out_shape]
    ospecs = list(out_specs) if multi else [out_specs]
    n_in, n_out, n_scr = len(operands), len(outs), len(scratch)
    ri, ro = len(rider.operands), len(rider.out_shapes)

    def wrapped(*refs):
        o0 = n_in + ri
        s0 = o0 + n_out + ro
        rin, rout, rsem = refs[n_in:o0], refs[o0 + n_out:s0], refs[s0 + n_scr:]
        ids = [pl.program_id(a) for a in range(len(grid))]
        first = functools.reduce(jnp.logical_and, [i == 0 for i in ids])
        last = functools.reduce(jnp.logical_and, [i == g - 1 for i, g in zip(ids, grid)])

        @pl.when(first)
        def _():
            rider.start(rin, rout, rsem)

        body(*refs[:n_in], *refs[o0:o0 + n_out], *refs[s0:s0 + n_scr])

        @pl.when(last)
        def _():
            rider.finish(rin, rout, rsem)

    hbm = pl.BlockSpec(memory_space=pl.ANY)
    res = pl.pallas_call(
        wrapped, name=name, grid=grid, in_specs=list(in_specs) + [hbm] * ri, out_specs=ospecs + [hbm] * ro,
        out_shape=outs + list(rider.out_shapes), scratch_shapes=list(scratch) + list(rider.sems),
        input_output_aliases={n_in + k: n_out + v for k, v in rider.aliases.items()},
        compiler_params=_cparams(*(("arbitrary",) * len(grid))))(*operands, *rider.operands)
    main = list(res[:n_out])
    return (main if multi else main[0]), list(res[n_out:])


def _mm(name, operands, dims, grid, in_specs, o_spec, out_shape, rider=None):
    npairs = len(operands) // 2

    def body(*refs):
        t = None
        for i in range(npairs):
            a, b = refs[2 * i], refs[2 * i + 1]
            parts = [(a[s], b[s]) for s in range(a.shape[0])] if len(a.shape) == 3 else [(a[...], b[...])]
            for pa, pb in parts:
                d = _dot(pa, pb, dims)
                t = d if t is None else t + d
        refs[2 * npairs][...] = t.astype(refs[2 * npairs].dtype)

    return _call(body, name, grid, in_specs, o_spec, out_shape, operands, (), ("parallel",) * len(grid), rider)


class _FfnW(typing.NamedTuple):
    gu: jax.Array
    g0: int
    dn: jax.Array
    d0: int


def _ffn_up(name, n, w, rider=None):
    def body(n_ref, wg_ref, wu_ref, fg_ref, fu_ref, a_ref):
        nb = n_ref[...]
        g = _dot(nb, wg_ref[...], NT)
        u = _dot(nb, wu_ref[...], NT)
        sg = _sigmoid(g)
        silu = g * sg
        fg_ref[...] = (u * (sg * (1.0 + g * (1.0 - sg)))).astype(BF16)
        fu_ref[...] = silu.astype(BF16)
        a_ref[...] = (silu * u).astype(BF16)

    out = jax.ShapeDtypeStruct((NSH, S, FS), BF16)
    ospec = pl.BlockSpec((None, TS, FS), lambda s, i: (s, i, 0))
    return _call(
        body, name, (NSH, S // TS),
        [pl.BlockSpec((TS, D), lambda s, i: (i, 0)),
         pl.BlockSpec((None, None, FS, D), lambda s, i: (s, w.g0, 0, 0)),
         pl.BlockSpec((None, None, FS, D), lambda s, i: (s, w.g0 + 1, 0, 0))],
        [ospec, ospec, ospec], [out, out, out], (n, w.gu, w.gu), sem=("parallel", "parallel"), rider=rider)


def _ffn_dact(name, dh, w, fgate, fup, rider=None):
    def body(dh_ref, wd_ref, fg_ref, fu_ref, dg_ref, du_ref):
        da = _dot(dh_ref[...], wd_ref[...], NT)
        dg_ref[...] = (da * fg_ref[...].astype(F32)).astype(BF16)
        du_ref[...] = (da * fu_ref[...].astype(F32)).astype(BF16)

    out = jax.ShapeDtypeStruct((NSH, S, FS), BF16)
    aspec = pl.BlockSpec((None, TS, FS), lambda s, i: (s, i, 0))
    return _call(
        body, name, (NSH, S // TS),
        [pl.BlockSpec((TS, D), lambda s, i: (i, 0)),
         pl.BlockSpec((None, None, FS, D), lambda s, i: (s, w.d0, 0, 0)), aspec, aspec],
        [aspec, aspec], [out, out], (dh, w.dn, fgate, fup), sem=("parallel", "parallel"), rider=rider)


def _rstd(v):
    return lax.rsqrt(jnp.mean(v * v, axis=-1, keepdims=True) + EPS)


def _row_spec():
    return pl.BlockSpec((TR, D), lambda i: (i, 0))


def _vec_spec():
    return pl.BlockSpec((1, D), lambda i: (0, 0))


def _acc_rows(ref, v):
    @pl.when(pl.program_id(0) == 0)
    def _():
        ref[...] = jnp.zeros_like(ref)
    ref[...] += jnp.sum(v, axis=0, keepdims=True)


def _prenorm(name, x, g):
    def body(x_ref, g_ref, n_ref):
        xv = x_ref[...]
        n_ref[...] = (xv * _rstd(xv) * g_ref[...]).astype(BF16)

    return pl.pallas_call(
        body, name=name, grid=(S // TR,), in_specs=[_row_spec(), _vec_spec()], out_specs=_row_spec(),
        out_shape=jax.ShapeDtypeStruct((S, D), BF16), compiler_params=_cparams("parallel"),
    )(x, g)


def _postres(name, x, h, p, alpha, gnext):
    def body(x_ref, h_ref, p_ref, g_ref, xo_ref, n_ref):
        hv = h_ref[...]
        xo = x_ref[...] + alpha * (hv * _rstd(hv) * p_ref[...])
        xo_ref[...] = xo
        n_ref[...] = (xo * _rstd(xo) * g_ref[...]).astype(BF16)

    return pl.pallas_call(
        body, name=name, grid=(S // TR,),
        in_specs=[_row_spec(), _row_spec(), _vec_spec(), _vec_spec()],
        out_specs=[_row_spec(), _row_spec()],
        out_shape=[jax.ShapeDtypeStruct((S, D), F32), jax.ShapeDtypeStruct((S, D), BF16)],
        compiler_params=_cparams("parallel"),
    )(x, h, p, gnext)


def _final(x, h, p, tgt, alpha):
    def body(x_ref, h_ref, p_ref, t_ref, dy_ref, dh_ref, dp_ref, loss_ref):
        hv = h_ref[...]
        r = _rstd(hv)
        hn = hv * r
        pv = p_ref[...]
        e = x_ref[...] + alpha * (hn * pv) - t_ref[...]
        dy = e * (1.0 / D)
        dy_ref[...] = dy
        du = alpha * dy * pv
        dh_ref[...] = (r * (du - hn * jnp.mean(du * hn, axis=-1, keepdims=True))).astype(BF16)
        _acc_rows(dp_ref, alpha * dy * hn)
        part = 0.5 * jnp.sum(jnp.mean(e * e, axis=-1, keepdims=True), axis=0, keepdims=True)
        _acc_rows(loss_ref, jnp.broadcast_to(part, (1, 128)))

    return pl.pallas_call(
        body, name="loss_head", grid=(S // TR,),
        in_specs=[_row_spec(), _row_spec(), _vec_spec(), _row_spec()],
        out_specs=[_row_spec(), _row_spec(), _vec_spec(), pl.BlockSpec((1, 128), lambda i: (0, 0))],
        out_shape=[jax.ShapeDtypeStruct((S, D), F32), jax.ShapeDtypeStruct((S, D), BF16),
                   jax.ShapeDtypeStruct((1, D), F32), jax.ShapeDtypeStruct((1, 128), F32)],
        compiler_params=_cparams("arbitrary"),
    )(x, h, p, tgt)


def _mid_bwd(name, dres, dn, x, g, h, p, alpha):
    def body(dr_ref, dn_ref, x_ref, g_ref, h_ref, p_ref, dx_ref, dh_ref, dg_ref, dp_ref):
        xv = x_ref[...]
        xn = xv * _rstd(xv)
        dnv = dn_ref[...]
        dng = dnv * g_ref[...]
        dx = dr_ref[...] + _rstd(xv) * (dng - xn * jnp.mean(dng * xn, axis=-1, keepdims=True))
        dx_ref[...] = dx
        _acc_rows(dg_ref, dnv * xn)
        hv = h_ref[...]
        r = _rstd(hv)
        hn = hv * r
        du = alpha * dx * p_ref[...]
        dh_ref[...] = (r * (du - hn * jnp.mean(du * hn, axis=-1, keepdims=True))).astype(BF16)
        _acc_rows(dp_ref, alpha * dx * hn)

    return pl.pallas_call(
        body, name=name, grid=(S // TR,),
        in_specs=[_row_spec(), _row_spec(), _row_spec(), _vec_spec(), _row_spec(), _vec_spec()],
        out_specs=[_row_spec(), _row_spec(), _vec_spec(), _vec_spec()],
        out_shape=[jax.ShapeDtypeStruct((S, D), F32), jax.ShapeDtypeStruct((S, D), BF16),
                   jax.ShapeDtypeStruct((1, D), F32), jax.ShapeDtypeStruct((1, D), F32)],
        compiler_params=_cparams("arbitrary"),
    )(dres, dn, x, g, h, p)


def _first_bwd(dres, dn, x, g):
    def body(dr_ref, dn_ref, x_ref, g_ref, dx_ref, dg_ref):
        xv = x_ref[...]
        r = _rstd(xv)
        xn = xv * r
        dnv = dn_ref[...]
        dng = dnv * g_ref[...]
        dx_ref[...] = dr_ref[...] + r * (dng - xn * jnp.mean(dng * xn, axis=-1, keepdims=True))
        _acc_rows(dg_ref, dnv * xn)

    return pl.pallas_call(
        body, name="first_bwd", grid=(S // TR,),
        in_specs=[_row_spec(), _row_spec(), _row_spec(), _vec_spec()],
        out_specs=[_row_spec(), _vec_spec()],
        out_shape=[jax.ShapeDtypeStruct((S, D), F32), jax.ShapeDtypeStruct((1, D), F32)],
        compiler_params=_cparams("arbitrary"),
    )(dres, dn, x, g)


def _rotate(t, c128, s128, sign, scale):
    width = t.shape[1]
    c = jnp.tile(c128, (1, width // 128))
    sn = jnp.tile(s128, (1, width // 128))
    lane = lax.broadcasted_iota(jnp.int32, t.shape, 1) & (HD - 1)
    rot = jnp.where(lane < HD // 2, -pltpu.roll(t, width - HD // 2, 1), pltpu.roll(t, HD // 2, 1))
    return (t * c + sign * (rot * sn)) * scale


def _rows_to_blocks(y):
    out = []
    for j in range(NKV):
        yt = y[:, QCOLS * j:QCOLS * (j + 1)].T
        out.append(jnp.concatenate([yt[HD * g:HD * (g + 1)] for g in range(NQ_PER_KV)], axis=1))
    return out


def _blocks_to_rows(blocks):
    cols = []
    for b in blocks:
        stacked = jnp.concatenate([b[:, 128 * g:128 * (g + 1)] for g in range(NQ_PER_KV)], axis=0)
        cols.append(stacked.T)
    return jnp.concatenate(cols, axis=1)


def _rope_q(proj, cos, sin):
    def body(t_ref, c_ref, s_ref, o_ref):
        y = _rotate(t_ref[...], c_ref[...], s_ref[...], 1.0, HD ** -0.5)
        for j, blk in enumerate(_rows_to_blocks(y)):
            o_ref[j] = blk.astype(BF16)

    return pl.pallas_call(
        body, name="rope_q", grid=(NCH,),
        in_specs=[pl.BlockSpec((128, D), lambda i: (i, 0)),
                  pl.BlockSpec((128, 128), lambda i: (i, 0)), pl.BlockSpec((128, 128), lambda i: (i, 0))],
        out_specs=pl.BlockSpec((NKV, None, HD, QROWS), lambda i: (0, i, 0, 0)),
        out_shape=jax.ShapeDtypeStruct((NKV, NCH, HD, QROWS), BF16), compiler_params=_cparams("parallel"),
    )(proj, cos, sin)


def _rope_dq(dqt, cos, sin):
    def body(t_ref, c_ref, s_ref, o_ref):
        t = _blocks_to_rows([t_ref[j] for j in range(NKV)])
        o_ref[...] = _rotate(t, c_ref[...], s_ref[...], -1.0, HD ** -0.5).astype(BF16)

    return pl.pallas_call(
        body, name="rope_dq", grid=(NCH,),
        in_specs=[pl.BlockSpec((NKV, None, HD, QROWS), lambda i: (0, i, 0, 0)),
                  pl.BlockSpec((128, 128), lambda i: (i, 0)), pl.BlockSpec((128, 128), lambda i: (i, 0))],
        out_specs=pl.BlockSpec((128, D), lambda i: (i, 0)),
        out_shape=jax.ShapeDtypeStruct((S, D), BF16), compiler_params=_cparams("parallel"),
    )(dqt, cos, sin)


def _rope(name, src, col_block, width, cos, sin, sign, scale):
    def body(t_ref, c_ref, s_ref, o_ref):
        o_ref[...] = _rotate(t_ref[...].astype(F32), c_ref[...], s_ref[...], sign, scale).astype(BF16)

    return pl.pallas_call(
        body, name=name, grid=(S // TR,),
        in_specs=[pl.BlockSpec((TR, width), lambda i: (i, col_block)),
                  pl.BlockSpec((TR, 128), lambda i: (i, 0)), pl.BlockSpec((TR, 128), lambda i: (i, 0))],
        out_specs=pl.BlockSpec((TR, width), lambda i: (i, 0)),
        out_shape=jax.ShapeDtypeStruct((S, width), BF16), compiler_params=_cparams("parallel"),
    )(src, cos, sin)


QROWS = NQ_PER_KV * 128


NBIAS = NCH + 1
KV_PER_STEP = 4


def _bias_table():
    db = lax.broadcasted_iota(jnp.int32, (NBIAS, 128, QROWS), 0) - 1
    ki = lax.broadcasted_iota(jnp.int32, (NBIAS, 128, QROWS), 1)
    qi = lax.broadcasted_iota(jnp.int32, (NBIAS, 128, QROWS), 2) & 127
    d = db * 128 + qi - ki
    cnt = ((d <= 128).astype(F32) + (((d & 3) == 0) & (d <= 512)).astype(F32) + ((d & 15) == 0).astype(F32))
    return jnp.where((d >= 0) & (cnt > 0.0), jnp.log(jnp.maximum(cnt, 1.0)), NEG)


def _qt_spec():
    return pl.BlockSpec((None, None, HD, QROWS), lambda j, i: (j, i, 0, 0))


def _stat_spec():
    return pl.BlockSpec((None, None, 1, QROWS), lambda j, i: (j, i, 0, 0))


def _attn_fwd(qt, kh, vt, bias, rider=None):
    def body(q_ref, k_ref, v_ref, b_ref, o_ref, lse_ref, rows_ref):
        qb = pl.program_id(1)

        def keys(carry, off, size, bias_):
            out = []
            for h in range(KV_PER_STEP):
                m, l, acc = carry[3 * h:3 * h + 3]
                s = _dot(k_ref[h, pl.ds(off, size), :], q_ref[h], NN) + bias_
                m_new = jnp.maximum(m, jnp.max(s, axis=0, keepdims=True))
                p = jnp.exp(s - m_new)
                a = jnp.exp(m - m_new)
                out += [m_new, a * l + jnp.sum(p, axis=0, keepdims=True),
                        a * acc + _dot(v_ref[h, :, pl.ds(off, size)], p, NN)]
            return tuple(out)

        def pair(i, carry):
            bias2 = jnp.concatenate([b_ref[qb - 2 * i + 1], b_ref[qb - 2 * i]], axis=0)
            return keys(carry, pl.multiple_of(i * 256, 256), 256, bias2)

        init = (jnp.full((1, QROWS), NEG, F32), jnp.zeros((1, QROWS), F32), jnp.zeros((HD, QROWS), F32))
        res = lax.fori_loop(0, (qb + 1) // 2, pair, init * KV_PER_STEP)
        res = lax.cond(qb % 2 == 0,
                       lambda c: keys(c, pl.multiple_of(qb * 128, 128), 128, b_ref[1]), lambda c: c, res)
        outs = []
        for h in range(KV_PER_STEP):
            m, l, acc = res[3 * h:3 * h + 3]
            outs.append(acc / l)
            o_ref[h] = outs[h]
            lse_ref[h] = m + jnp.log(l)
        rows_ref[...] = _blocks_to_rows(outs).astype(BF16)

    kvs = KV_PER_STEP
    qspec = pl.BlockSpec((kvs, None, HD, QROWS), lambda j, i: (j, i, 0, 0))
    return _call(
        body, "attn_fwd", (NKV // kvs, NCH),
        [qspec, pl.BlockSpec((kvs, S, HD), lambda j, i: (j, 0, 0)),
         pl.BlockSpec((kvs, HD, S), lambda j, i: (j, 0, 0)),
         pl.BlockSpec((NBIAS, 128, QROWS), lambda j, i: (0, 0, 0))],
        [qspec, pl.BlockSpec((kvs, None, 1, QROWS), lambda j, i: (j, i, 0, 0)),
         pl.BlockSpec((128, QCOLS * kvs), lambda j, i: (i, j))],
        [jax.ShapeDtypeStruct((NKV, NCH, HD, QROWS), F32), jax.ShapeDtypeStruct((NKV, NCH, 1, QROWS), F32),
         jax.ShapeDtypeStruct((S, D), BF16)],
        (qt, kh, vt, bias), sem=("parallel", "parallel"), rider=rider)


def _attn_delta(ot, dot_):
    def body(o_ref, do_ref, dl_ref):
        dl_ref[...] = jnp.sum(o_ref[...] * do_ref[...].astype(F32), axis=1, keepdims=True)

    spec = pl.BlockSpec((None, NCH, HD, QROWS), lambda j: (j, 0, 0, 0))
    return pl.pallas_call(
        body, name="attn_delta", grid=(NKV,), in_specs=[spec, spec],
        out_specs=pl.BlockSpec((None, NCH, 1, QROWS), lambda j: (j, 0, 0, 0)),
        out_shape=jax.ShapeDtypeStruct((NKV, NCH, 1, QROWS), F32), compiler_params=_cparams("parallel"),
    )(ot, dot_)


def _attn_bwd(qt, kh, kt, vh, dot_, lse, delta, bias):
    def body(qt_ref, k_ref, kt_ref, v_ref, dot_ref, lse_ref, dl_ref, b_ref, dq_ref, dk_ref, dv_ref):
        kb = pl.program_id(1)

        @pl.when(kb == 0)
        def _():
            dq_ref[...] = jnp.zeros_like(dq_ref)

        def blocks(carry, qbs):
            out = list(carry)
            for h in range(KV_PER_STEP):
                k, kt_, v = k_ref[h], kt_ref[h], v_ref[h]
                for qb in qbs:
                    st = _dot(k, qt_ref[h, qb], NN) + b_ref[qb - kb + 1]
                    pt = jnp.exp(st - lse_ref[h, qb])
                    dst = pt * (_dot(v, dot_ref[h, qb], NN) - dl_ref[h, qb])
                    dq_ref[h, qb] += _dot(kt_, dst, NN)
                    out[2 * h] = out[2 * h] + _dot(dst, qt_ref[h, qb], NT)
                    out[2 * h + 1] = out[2 * h + 1] + _dot(pt, dot_ref[h, qb], NT)
            return tuple(out)

        res = (jnp.zeros((128, HD), F32),) * (2 * KV_PER_STEP)
        res = lax.cond(kb % 2 == 1, lambda c: blocks(c, (kb,)), lambda c: c, res)
        res = lax.fori_loop((kb + 1) // 2, NCH // 2, lambda j, c: blocks(c, (2 * j, 2 * j + 1)), res)
        for h in range(KV_PER_STEP):
            dk_ref[h] = res[2 * h]
            dv_ref[h] = res[2 * h + 1]

    kvs = KV_PER_STEP
    tspec = pl.BlockSpec((kvs, NCH, HD, QROWS), lambda j, i: (j, 0, 0, 0))
    kspec = pl.BlockSpec((kvs, 128, HD), lambda j, i: (j, i, 0))
    sspec = pl.BlockSpec((kvs, NCH, 1, QROWS), lambda j, i: (j, 0, 0, 0))
    return pl.pallas_call(
        body, name="attn_bwd", grid=(NKV // kvs, NCH),
        in_specs=[tspec, kspec, pl.BlockSpec((kvs, HD, 128), lambda j, i: (j, 0, i)), kspec, tspec,
                  sspec, sspec, pl.BlockSpec((NBIAS, 128, QROWS), lambda j, i: (0, 0, 0))],
        out_specs=[tspec, kspec, kspec],
        out_shape=[jax.ShapeDtypeStruct((NKV, NCH, HD, QROWS), F32),
                   jax.ShapeDtypeStruct((NKV, S, HD), F32), jax.ShapeDtypeStruct((NKV, S, HD), F32)],
        compiler_params=_cparams("parallel", "arbitrary"),
    )(qt, kh, kt, vh, dot_, lse, delta, bias)


CONV_BLK = 256
CONV_COL0 = 1536 // CONV_BLK


def _shift_down(u, j, row):
    return jnp.where(row >= j, pltpu.roll(u, j, 0), 0.0)


def _conv_pre(u, w_ref, b_ref, row):
    y = b_ref[...] + w_ref[CONV_K - 1:CONV_K, :] * u
    for j in range(1, CONV_K):
        y = y + w_ref[CONV_K - 1 - j:CONV_K - j, :] * _shift_down(u, j, row)
    return y


def _conv_fwd(proj, convw, convb):
    def body(u_ref, w_ref, b_ref, o_ref):
        u = u_ref[...]
        row = lax.broadcasted_iota(jnp.int32, u.shape, 0)
        y = _conv_pre(u, w_ref, b_ref, row)
        o_ref[...] = y * _sigmoid(y)

    return pl.pallas_call(
        body, name="conv_fwd", grid=(CONV_C // CONV_BLK,),
        in_specs=[pl.BlockSpec((S, CONV_BLK), lambda i: (0, CONV_COL0 + i)),
                  pl.BlockSpec((CONV_K, CONV_BLK), lambda i: (0, i)),
                  pl.BlockSpec((1, CONV_BLK), lambda i: (0, i))],
        out_specs=pl.BlockSpec((S, CONV_BLK), lambda i: (0, i)),
        out_shape=jax.ShapeDtypeStruct((S, CONV_C), F32), compiler_params=_cparams("parallel"),
    )(proj, convw, convb)


def _conv_bwd(dact, proj, convw, convb):
    def body(da_ref, u_ref, w_ref, b_ref, du_ref, dw_ref, db_ref):
        u = u_ref[...]
        row = lax.broadcasted_iota(jnp.int32, u.shape, 0)
        y = _conv_pre(u, w_ref, b_ref, row)
        sg = _sigmoid(y)
        dy = da_ref[...] * (sg * (1.0 + y * (1.0 - sg)))
        db_ref[...] = jnp.sum(dy, axis=0, keepdims=True)
        du = w_ref[CONV_K - 1:CONV_K, :] * dy
        r8 = lax.broadcasted_iota(jnp.int32, (8, CONV_BLK), 0)
        dw = jnp.where(r8 == CONV_K - 1, jnp.sum(dy * u, axis=0, keepdims=True), 0.0)
        for j in range(1, CONV_K):
            du = du + w_ref[CONV_K - 1 - j:CONV_K - j, :] * jnp.where(row < S - j, pltpu.roll(dy, S - j, 0), 0.0)
            dw = dw + jnp.where(r8 == CONV_K - 1 - j,
                                jnp.sum(dy * _shift_down(u, j, row), axis=0, keepdims=True), 0.0)
        du_ref[...] = du.astype(BF16)
        dw_ref[...] = dw

    return pl.pallas_call(
        body, name="conv_bwd", grid=(CONV_C // CONV_BLK,),
        in_specs=[pl.BlockSpec((S, CONV_BLK), lambda i: (0, i)),
                  pl.BlockSpec((S, CONV_BLK), lambda i: (0, CONV_COL0 + i)),
                  pl.BlockSpec((CONV_K, CONV_BLK), lambda i: (0, i)),
                  pl.BlockSpec((1, CONV_BLK), lambda i: (0, i))],
        out_specs=[pl.BlockSpec((S, CONV_BLK), lambda i: (0, i)), pl.BlockSpec((8, CONV_BLK), lambda i: (0, i)),
                   pl.BlockSpec((1, CONV_BLK), lambda i: (0, i))],
        out_shape=[jax.ShapeDtypeStruct((S, CONV_C), BF16), jax.ShapeDtypeStruct((8, CONV_C), F32),
                   jax.ShapeDtypeStruct((1, CONV_C), F32)],
        compiler_params=_cparams("parallel"),
    )(dact, proj, convw, convb)


NPAIR = 8


def _ssd_scalars(dtr_ref, dtb_ref, alog_ref):
    z = dtr_ref[...] + dtb_ref[...]
    dt = jnp.maximum(z, 0.0) + jnp.log(1.0 + jnp.exp(-jnp.abs(z)))
    a = -jnp.exp(alog_ref[...])
    r = lax.broadcasted_iota(jnp.int32, (128, 128), 0)
    c = lax.broadcasted_iota(jnp.int32, (128, 128), 1)
    tri = (r >= c).astype(F32)
    cs = _dot_exact(tri, dt * a)
    return z, dt, a, cs, r, c


def _pair_terms(cs, cst, dt, h1, h2, lo):
    c1, c2 = cs[:, h1:h1 + 1], cs[:, h2:h2 + 1]
    l1, l2 = cs[127:128, h1:h1 + 1], cs[127:128, h2:h2 + 1]
    e_l = jnp.where(lo, jnp.exp(c1), jnp.exp(c2))
    dte1, dte2 = jnp.exp(l1 - c1), jnp.exp(l2 - c2)
    dte_l = jnp.where(lo, dte1, dte2)
    dt_l = jnp.where(lo, dt[:, h1:h1 + 1], dt[:, h2:h2 + 1])
    return c1, c2, jnp.exp(l1), jnp.exp(l2), e_l, dte1, dte2, dte_l, dt_l


def _gate_norm(y, zv, w):
    yg = y * (zv * _sigmoid(zv))
    outs, rs = [], []
    for g in range(2):
        blk = yg[:, 512 * g:512 * (g + 1)]
        r = lax.rsqrt(jnp.mean(blk * blk, axis=-1, keepdims=True) + EPS)
        outs.append(blk * r)
        rs.append(r)
    return jnp.concatenate(outs, axis=1), rs, yg


def _ssd_fwd(xbc, proj, dtb, alog, dskip_l, ssmw):
    def body(x_ref, b_ref, c_ref, dtr_ref, z_ref, dtb_ref, alog_ref, dsk_ref, w_ref, y_ref, yn_ref, hp_ref, h_ref):
        @pl.when(pl.program_id(0) == 0)
        def _():
            h_ref[...] = jnp.zeros_like(h_ref)

        _, dt, _, cs, r, c = _ssd_scalars(dtr_ref, dtb_ref, alog_ref)
        cst = cs.T
        causal = r >= c
        lo = c < HD
        hp_ref[...] = h_ref[...]
        for g in range(2):
            bg = b_ref[:, 128 * g:128 * (g + 1)]
            cg = c_ref[:, 128 * g:128 * (g + 1)]
            cb = _dot(cg, bg, NT)
            for j in range(4):
                pj = 4 * g + j
                h1, h2 = 2 * pj, 2 * pj + 1
                sl = slice(128 * pj, 128 * (pj + 1))
                xp = x_ref[:, sl]
                c1, c2, cd1, cd2, e_l, _, _, dte_l, dt_l = _pair_terms(cs, cst, dt, h1, h2, lo)
                xdt = xp * dt_l
                m1 = cb * jnp.exp(jnp.where(causal, c1 - cst[h1:h1 + 1, :], NEG))
                m2 = cb * jnp.exp(jnp.where(causal, c2 - cst[h2:h2 + 1, :], NEG))
                yd = jnp.where(lo, _dot(m1, xdt, NN), _dot(m2, xdt, NN))
                hp = h_ref[pj]
                yo = _dot(cg, hp, NT) * e_l
                st = _dot(xdt * dte_l, bg, TN)
                h_ref[pj] = hp * jnp.where(r < HD, cd1, cd2) + st
                y_ref[:, sl] = yd + yo + dsk_ref[:, sl] * xp
        yn, _, _ = _gate_norm(y_ref[...], z_ref[...], w_ref[...])
        yn_ref[...] = (yn * w_ref[...]).astype(BF16)

    return pl.pallas_call(
        body, name="ssd_fwd", grid=(NCH,),
        in_specs=[pl.BlockSpec((128, SSM_W), lambda i: (i, 0)),
                  pl.BlockSpec((128, 256), lambda i: (i, 4)), pl.BlockSpec((128, 256), lambda i: (i, 5)),
                  pl.BlockSpec((128, 128), lambda i: (i, COL_DT // 128)),
                  pl.BlockSpec((128, SSM_W), lambda i: (i, 3)),
                  pl.BlockSpec((1, 128), lambda i: (0, 0)), pl.BlockSpec((1, 128), lambda i: (0, 0)),
                  pl.BlockSpec((1, SSM_W), lambda i: (0, 0)), pl.BlockSpec((1, SSM_W), lambda i: (0, 0))],
        out_specs=[pl.BlockSpec((128, SSM_W), lambda i: (i, 0)), pl.BlockSpec((128, SSM_W), lambda i: (i, 0)),
                   pl.BlockSpec((None, NPAIR, 128, 128), lambda i: (i, 0, 0, 0))],
        out_shape=[jax.ShapeDtypeStruct((S, SSM_W), F32), jax.ShapeDtypeStruct((S, SSM_W), BF16),
                   jax.ShapeDtypeStruct((NCH, NPAIR, 128, 128), F32)],
        scratch_shapes=[pltpu.VMEM((NPAIR, 128, 128), F32)],
        compiler_params=_cparams("arbitrary"),
    )(xbc, xbc, xbc, proj, proj, dtb, alog, dskip_l, ssmw)


def _ssd_bwd(dmixed, y, xbc, proj, hprev, dtb, alog, dskip_l, ssmw, rider=None):
    def body(dyn_ref, y_ref, x_ref, b_ref, c_ref, dtr_ref, z_ref, hp_ref, dtb_ref, alog_ref, dsk_ref, w_ref,
             dxbc_ref, dz_ref, ddt_ref, dw_ref, dsc_ref, g_ref):
        @pl.when(pl.program_id(0) == 0)
        def _():
            g_ref[...] = jnp.zeros_like(g_ref)
            dsc_ref[...] = jnp.zeros_like(dsc_ref)

        z, dt, a, cs, r, c = _ssd_scalars(dtr_ref, dtb_ref, alog_ref)
        cst = cs.T
        causal = r >= c
        lo = c < HD

        yv = y_ref[...]
        zv = z_ref[...]
        wv = w_ref[...]
        ygn, rs, yg = _gate_norm(yv, zv, wv)
        dyn = dyn_ref[...]
        _acc_rows(dw_ref, dyn * ygn)
        dynw = dyn * wv
        parts = []
        for g in range(2):
            sl = slice(512 * g, 512 * (g + 1))
            a_g, n_g = dynw[:, sl], ygn[:, sl]
            parts.append(rs[g] * (a_g - n_g * jnp.mean(a_g * n_g, axis=-1, keepdims=True)))
        dyg = jnp.concatenate(parts, axis=1)
        sz = _sigmoid(zv)
        dz_ref[...] = (dyg * yv * (sz * (1.0 + zv * (1.0 - sz)))).astype(BF16)
        dy_all = dyg * (zv * sz)

        dcs_cols = jnp.zeros((128, 128), F32)
        dcs_rows = jnp.zeros((128, 128), F32)
        ddt_x = jnp.zeros((128, 128), F32)
        dd_row = jnp.zeros((1, 128), F32)
        last = r == 127
        x_all, b_all, c_all, dsk_all = x_ref[...], b_ref[...], c_ref[...], dsk_ref[...]
        hp_all, g_all = hp_ref[...], g_ref[...]
        g_new, dx_parts, db_parts, dc_parts = [], [], [], []
        for g in range(2):
            bg = b_all[:, 128 * g:128 * (g + 1)]
            cg = c_all[:, 128 * g:128 * (g + 1)]
            cb = _dot(cg, bg, NT)
            dcb = jnp.zeros((128, 128), F32)
            db_acc = jnp.zeros((128, NST), F32)
            dc_acc = jnp.zeros((128, NST), F32)
            for j in range(4):
                pj = 4 * g + j
                h1, h2 = 2 * pj, 2 * pj + 1
                sl = slice(128 * pj, 128 * (pj + 1))
                xp = x_all[:, sl]
                dyp = dy_all[:, sl]
                c1, c2, cd1, cd2, e_l, dte1, dte2, dte_l, dt_l = _pair_terms(cs, cst, dt, h1, h2, lo)
                xdt = xp * dt_l
                hp = hp_all[pj]
                gp = g_all[pj]
                dxp = dsk_all[:, sl] * dyp
                dyx = dyp * xp
                dd_row = dd_row + jnp.where(c[0:1, :] == h1, jnp.sum(jnp.where(lo, dyx, 0.0), keepdims=True), 0.0) \
                    + jnp.where(c[0:1, :] == h2, jnp.sum(jnp.where(lo, 0.0, dyx), keepdims=True), 0.0)
                dzs = dyp * e_l
                dc_acc = dc_acc + _dot(dzs, hp, NN)
                g_from = _dot(dzs, cg, TN)
                ryo = dyp * (_dot(cg, hp, NT) * e_l)
                k1 = jnp.sum(jnp.where(lo, ryo, 0.0), axis=1, keepdims=True)
                k2 = jnp.sum(jnp.where(lo, 0.0, ryo), axis=1, keepdims=True)
                qm = _dot(bg, gp, NT)
                dxdt = qm * dte_l
                qx = qm * xdt
                t1 = jnp.sum(jnp.where(lo, qx, 0.0), axis=1, keepdims=True) * dte1
                t2 = jnp.sum(jnp.where(lo, 0.0, qx), axis=1, keepdims=True) * dte2
                db_acc = db_acc + _dot(xdt * dte_l, gp, NN)
                gh = gp * hp
                dl1 = jnp.sum(t1, keepdims=True) + jnp.sum(jnp.where(r < HD, gh, 0.0), keepdims=True) * cd1
                dl2 = jnp.sum(t2, keepdims=True) + jnp.sum(jnp.where(r < HD, 0.0, gh), keepdims=True) * cd2
                g_new.append(g_from + jnp.where(r < HD, cd1, cd2) * gp)
                k1 = k1 - t1 + jnp.where(last[:, 0:1], dl1, 0.0)
                k2 = k2 - t2 + jnp.where(last[:, 0:1], dl2, 0.0)
                for hh, ch, msk in ((h1, c1, lo), (h2, c2, jnp.logical_not(lo))):
                    lm = jnp.exp(jnp.where(causal, ch - cst[hh:hh + 1, :], NEG))
                    mm = cb * lm
                    dm = jnp.where(causal, _dot(jnp.where(msk, dyp, 0.0), xdt, NT), 0.0)
                    w = dm * mm
                    kk = jnp.sum(w, axis=1, keepdims=True)
                    if hh == h1:
                        k1 = k1 + kk
                    else:
                        k2 = k2 + kk
                    dcs_rows = dcs_rows + jnp.where(r == hh, jnp.sum(w, axis=0, keepdims=True), 0.0)
                    dcb = dcb + dm * lm
                    dxdt = dxdt + jnp.where(msk, _dot(mm, dyp, TN), 0.0)
                dcs_cols = dcs_cols + jnp.where(c == h1, k1, 0.0) + jnp.where(c == h2, k2, 0.0)
                dxx = dxdt * xp
                ddt_x = ddt_x + jnp.where(c == h1, jnp.sum(jnp.where(lo, dxx, 0.0), axis=1, keepdims=True), 0.0) \
                    + jnp.where(c == h2, jnp.sum(jnp.where(lo, 0.0, dxx), axis=1, keepdims=True), 0.0)
                dx_parts.append(dxp + dxdt * dt_l)
            db_parts.append(db_acc + _dot(dcb, cg, TN))
            dc_parts.append(dc_acc + _dot(dcb, bg, NN))
        g_ref[...] = jnp.stack(g_new)
        dxbc_ref[...] = jnp.concatenate(dx_parts + db_parts + dc_parts, axis=1)

        dcs = dcs_cols - dcs_rows.T
        dad = _dot_exact((c >= r).astype(F32), dcs)
        ddt = dad * a + ddt_x
        ddtr = jnp.where(c < 16, ddt * _sigmoid(z), 0.0)
        ddt_ref[...] = ddtr.astype(BF16)
        r8 = lax.broadcasted_iota(jnp.int32, (8, 128), 0)
        dsc_ref[...] += (jnp.where(r8 == 0, jnp.sum(ddtr, axis=0, keepdims=True), 0.0)
                         + jnp.where(r8 == 1, jnp.sum(dad * dt, axis=0, keepdims=True) * a, 0.0)
                         + jnp.where(r8 == 2, dd_row, 0.0))

    rev = NCH - 1
    return _call(
        body, "ssd_bwd", (NCH,),
        [pl.BlockSpec((128, SSM_W), lambda i: (rev - i, 0)),
         pl.BlockSpec((128, SSM_W), lambda i: (rev - i, 0)),
         pl.BlockSpec((128, SSM_W), lambda i: (rev - i, 0)),
         pl.BlockSpec((128, 256), lambda i: (rev - i, 4)), pl.BlockSpec((128, 256), lambda i: (rev - i, 5)),
         pl.BlockSpec((128, 128), lambda i: (rev - i, COL_DT // 128)),
         pl.BlockSpec((128, SSM_W), lambda i: (rev - i, 3)),
         pl.BlockSpec((None, NPAIR, 128, 128), lambda i: (rev - i, 0, 0, 0)),
         pl.BlockSpec((1, 128), lambda i: (0, 0)), pl.BlockSpec((1, 128), lambda i: (0, 0)),
         pl.BlockSpec((1, SSM_W), lambda i: (0, 0)), pl.BlockSpec((1, SSM_W), lambda i: (0, 0))],
        [pl.BlockSpec((128, CONV_C), lambda i: (rev - i, 0)),
         pl.BlockSpec((128, SSM_W), lambda i: (rev - i, 0)),
         pl.BlockSpec((128, 128), lambda i: (rev - i, 0)),
         pl.BlockSpec((1, SSM_W), lambda i: (0, 0)), pl.BlockSpec((8, 128), lambda i: (0, 0))],
        [jax.ShapeDtypeStruct((S, CONV_C), F32), jax.ShapeDtypeStruct((S, SSM_W), BF16),
         jax.ShapeDtypeStruct((S, 128), BF16), jax.ShapeDtypeStruct((1, SSM_W), F32),
         jax.ShapeDtypeStruct((8, 128), F32)],
        (dmixed, y, xbc, xbc, xbc, proj, proj, hprev, dtb, alog, dskip_l, ssmw),
        [pltpu.VMEM((NPAIR, 128, 128), F32)], ("arbitrary",), rider)


def _cast_stack(name, slot, arrs, tr, tc):
    n = len(arrs)
    rows, cols = arrs[0].shape

    def body(s_ref, *refs):
        for i in range(n):
            refs[n][i] = refs[i][...].astype(BF16)

    return pl.pallas_call(
        body, name=name,
        grid_spec=pltpu.PrefetchScalarGridSpec(
            num_scalar_prefetch=1, grid=(rows // tr, cols // tc),
            in_specs=[pl.BlockSpec((tr, tc), lambda i, j, sr: (i, j))] * n,
            out_specs=pl.BlockSpec((None, n, tr, tc), lambda i, j, sr: (sr[0], 0, i, j))),
        out_shape=jax.ShapeDtypeStruct((NSH, n, rows, cols), BF16),
        compiler_params=_cparams("parallel", "parallel"),
    )(slot, *arrs)


def _pair_sum(name, c_idx, ps, th):
    n = len(ps)
    _, rows, _ = ps[0].shape

    def body(c_ref, *refs):
        mine, whole, out, theirs = refs[:n], refs[n:2 * n], refs[2 * n:3 * n], refs[3 * n:4 * n]
        send, recv = refs[4 * n], refs[4 * n + 1]
        s, i = pl.program_id(0), pl.program_id(1)

        @pl.when((s == 0) & (i == 0))
        def _():
            x, y, c, _ = _place()
            cps = [_rcopy(whole[k].at[:, :, pl.ds((1 - c) * HALF, HALF)], theirs[k], send.at[k], recv.at[k],
                          (x, y, 1 - c)) for k in range(n)]
            for cp in cps:
                cp.start()
            for cp in cps:
                cp.wait()

        rows_i = slice(None) if th == rows else pl.ds(pl.multiple_of(i * th, th), th)
        for k in range(n):
            out[k][...] = (mine[k][...].astype(F32) + theirs[k][s, rows_i, :].astype(F32)).astype(BF16)

    spec = pl.BlockSpec((None, th, HALF), lambda s, i, cr: (s, i, 0))
    return pl.pallas_call(
        body, name=name,
        grid_spec=pltpu.PrefetchScalarGridSpec(
            num_scalar_prefetch=1, grid=(NSH, rows // th),
            in_specs=[pl.BlockSpec((None, th, HALF), lambda s, i, cr: (s, i, cr[0]))] * n + _any_specs(n),
            out_specs=[spec] * n,
            scratch_shapes=[pltpu.VMEM((NSH, rows, HALF), BF16)] * n
            + [pltpu.SemaphoreType.DMA((n,)), pltpu.SemaphoreType.DMA((n,))]),
        out_shape=[jax.ShapeDtypeStruct((NSH, rows, HALF), BF16)] * n,
        compiler_params=_cparams("arbitrary", "arbitrary"),
    )(c_idx, *ps, *ps)


def _chip_sum(name, place, cs, ts, th):
    n = len(ts)
    _, rows, _ = ts[0].shape

    def body(p_ref, *refs):
        for i in range(n):
            t = refs[n + i][...].astype(F32)
            refs[2 * n + i][...] = ((refs[i][...].astype(F32) + t[0]) + t[1]) + t[2]

    return pl.pallas_call(
        body, name=name,
        grid_spec=pltpu.PrefetchScalarGridSpec(
            num_scalar_prefetch=1, grid=(rows // th,),
            in_specs=[pl.BlockSpec((None, th, HALF), lambda i, pr: (pr[0], i, 0))] * n
            + [pl.BlockSpec((3, th, HALF), lambda i, pr: (0, i, 0))] * n,
            out_specs=[pl.BlockSpec((th, HALF), lambda i, pr: (i, pr[1]))] * n),
        out_shape=[jax.ShapeDtypeStruct((rows, D), F32)] * n, compiler_params=_cparams("parallel"),
    )(place, *cs, *ts)


def _adamw(name, ws, gs, ms, vs, tr, tc):
    n = len(ws)
    shape = ws[0].shape
    rows, cols, mid = shape[0], shape[-1], shape[1:-1]
    c1 = 1.0 / (1.0 - ADAM_B1 ** ADAM_STEP)
    c2 = 1.0 / (1.0 - ADAM_B2 ** ADAM_STEP)

    def body(*refs):
        for i in range(n):
            w, g, m, v = (refs[k * n + i][...] for k in range(4))
            m2 = ADAM_B1 * m + (1.0 - ADAM_B1) * g
            v2 = ADAM_B2 * v + (1.0 - ADAM_B2) * (g * g)
            refs[4 * n + 3 * i][...] = -ADAM_LR * ((m2 * c1) / (jnp.sqrt(v2 * c2) + ADAM_EPS) + ADAM_WD * w)
            refs[4 * n + 3 * i + 1][...] = m2
            refs[4 * n + 3 * i + 2][...] = v2

    spec = pl.BlockSpec((tr,) + mid + (tc,), lambda i, j: (i,) + (0,) * len(mid) + (j,))
    outs = pl.pallas_call(
        body, name=name, grid=(rows // tr, cols // tc), in_specs=[spec] * (4 * n), out_specs=[spec] * (3 * n),
        out_shape=[jax.ShapeDtypeStruct(shape, F32)] * (3 * n),
        compiler_params=_cparams("parallel", "parallel"),
    )(*ws, *gs, *ms, *vs)
    return [tuple(outs[3 * i:3 * i + 3]) for i in range(n)]


def _place():
    x, y, c = lax.axis_index("x"), lax.axis_index("y"), lax.axis_index("c")
    chips = [(1 - x, y), (x, 1 - y), (1 - x, 1 - y)]
    return x, y, c, chips


def _any_specs(n):
    return [pl.BlockSpec(memory_space=pl.ANY)] * n


def _rcopy(src, dst, send_sem, recv_sem, dev):
    return pltpu.make_async_remote_copy(src_ref=src, dst_ref=dst, send_sem=send_sem, recv_sem=recv_sem,
                                        device_id=dev, device_id_type=MESH)


def _gather_rider(bufs, views):
    n = len(bufs)

    def start(rin, rout, sems):
        send, recv = sems[0], sems[1]
        x, y, c, chips = _place()
        for j, chip in enumerate(chips):
            for b in range(n):
                mine = views[b](rout[b], 2 * x + y, c)
                _rcopy(mine, mine, send.at[j * n + b], recv.at[j * n + b], (chip[0], chip[1], c)).start()

    def finish(rin, rout, sems):
        send, recv, fsend, frecv = sems
        x, y, c, chips = _place()
        passed = []
        for j, chip in enumerate(chips):
            for b in range(n):
                landed = views[b](rout[b], 2 * chip[0] + chip[1], c)
                _rcopy(landed, landed, send.at[j * n + b], recv.at[j * n + b], (x, y, c)).wait_recv()
                fw = _rcopy(landed, landed, fsend.at[j * n + b], frecv.at[j * n + b], (x, y, 1 - c))
                fw.start()
                passed.append(fw)
        for j, chip in enumerate(chips):
            for b in range(n):
                other = views[b](rout[b], 2 * chip[0] + chip[1], 1 - c)
                _rcopy(other, other, fsend.at[j * n + b], frecv.at[j * n + b], (x, y, c)).wait_recv()
        for j, chip in enumerate(chips):
            for b in range(n):
                mine = views[b](rout[b], 2 * x + y, c)
                _rcopy(mine, mine, send.at[j * n + b], recv.at[j * n + b], (x, y, c)).wait_send()
        for fw in passed:
            fw.wait_send()

    return _Rider(list(bufs), [jax.ShapeDtypeStruct(a.shape, a.dtype) for a in bufs], {b: b for b in range(n)},
                  [pltpu.SemaphoreType.DMA((3 * n,))] * 4, start, finish)


def _small_gather_rider(cw):
    def descs(rin, rout, sems, x, y, c, chips):
        return [_rcopy(rin[0], rout[0].at[2 * x + y], sems[1].at[j], sems[2].at[j], (chip[0], chip[1], c))
                for j, chip in enumerate(chips)]

    def start(rin, rout, sems):
        x, y, c, chips = _place()
        pltpu.make_async_copy(rin[0], rout[0].at[2 * x + y], sems[0].at[0]).start()
        for cp in descs(rin, rout, sems, x, y, c, chips):
            cp.start()

    def finish(rin, rout, sems):
        x, y, c, chips = _place()
        for j, chip in enumerate(chips):
            _rcopy(rin[0], rout[0].at[2 * chip[0] + chip[1]], sems[1].at[j], sems[2].at[j], (x, y, c)).wait_recv()
        for cp in descs(rin, rout, sems, x, y, c, chips):
            cp.wait_send()
        pltpu.make_async_copy(rin[0], rout[0].at[2 * x + y], sems[0].at[0]).wait()

    return _Rider([cw], [jax.ShapeDtypeStruct((NSH,) + cw.shape, cw.dtype)], {},
                  [pltpu.SemaphoreType.DMA((1,)), pltpu.SemaphoreType.DMA((3,)), pltpu.SemaphoreType.DMA((3,))],
                  start, finish)


def _to_chips_rider(cs):
    n = len(cs)

    def descs(rin, rout, sems):
        x, y, c, chips = _place()
        return [_rcopy(rin[i].at[2 * chip[0] + chip[1]], rout[i].at[j], sems[0].at[j * n + i], sems[1].at[j * n + i],
                       (chip[0], chip[1], c)) for j, chip in enumerate(chips) for i in range(n)]

    def start(rin, rout, sems):
        for cp in descs(rin, rout, sems):
            cp.start()

    def finish(rin, rout, sems):
        for cp in descs(rin, rout, sems):
            cp.wait()

    return _Rider(list(cs), [jax.ShapeDtypeStruct((3,) + a.shape[1:], a.dtype) for a in cs], {},
                  [pltpu.SemaphoreType.DMA((3 * n,))] * 2, start, finish)


def _run_riders(name, riders):
    n_in = [len(r.operands) for r in riders]
    n_out = [len(r.out_shapes) for r in riders]
    n_sem = [len(r.sems) for r in riders]

    def body(*refs):
        parts, at = [], 0
        for counts in (n_in, n_out, n_sem):
            group = []
            for k in counts:
                group.append(refs[at:at + k])
                at += k
            parts.append(group)
        for i, r in enumerate(riders):
            r.start(parts[0][i], parts[1][i], parts[2][i])
        for i, r in enumerate(riders):
            r.finish(parts[0][i], parts[1][i], parts[2][i])

    aliases = {}
    for i, r in enumerate(riders):
        for k, v in r.aliases.items():
            aliases[sum(n_in[:i]) + k] = sum(n_out[:i]) + v
    res = pl.pallas_call(
        body, name=name, in_specs=_any_specs(sum(n_in)), out_specs=_any_specs(sum(n_out)),
        out_shape=[s for r in riders for s in r.out_shapes], input_output_aliases=aliases,
        scratch_shapes=[s for r in riders for s in r.sems],
    )(*[a for r in riders for a in r.operands])
    out, at = [], 0
    for k in n_out:
        out.append(list(res[at:at + k]))
        at += k
    return out


def _swap_halves(gs):
    n = len(gs)

    def body(*refs):
        dst, send, recv = refs[n:2 * n], refs[2 * n], refs[2 * n + 1]
        x, y, c, _ = _place()
        cps = []
        for i in range(n):
            mine = dst[i].at[:, pl.ds(c * HALF, HALF)]
            cps.append(pltpu.make_async_remote_copy(
                src_ref=mine, dst_ref=mine, send_sem=send.at[i], recv_sem=recv.at[i],
                device_id=(x, y, 1 - c), device_id_type=MESH))
        for cp in cps:
            cp.start()
        for i in range(n):
            other = dst[i].at[:, pl.ds((1 - c) * HALF, HALF)]
            pltpu.make_async_remote_copy(
                src_ref=other, dst_ref=other, send_sem=send.at[i], recv_sem=recv.at[i],
                device_id=(x, y, c), device_id_type=MESH).wait_recv()
        for cp in cps:
            cp.wait_send()

    return pl.pallas_call(
        body, name="grads_swap_halves", in_specs=_any_specs(n), out_specs=_any_specs(n),
        out_shape=[jax.ShapeDtypeStruct(g.shape, g.dtype) for g in gs],
        input_output_aliases={i: i for i in range(n)},
        scratch_shapes=[pltpu.SemaphoreType.DMA((n,)), pltpu.SemaphoreType.DMA((n,))],
    )(*gs)


SMALL_ROWS = 16


def _allreduce_small(vec):
    def body(v_ref, o_ref, buf, send, recv):
        x, y, c, _ = _place()
        me = 4 * x + 2 * y + c
        buf[me] = v_ref[...]
        cps = []
        for k in range(1, 8):
            peer = (x ^ (k >> 2), y ^ ((k >> 1) & 1), c ^ (k & 1))
            cps.append(pltpu.make_async_remote_copy(
                src_ref=v_ref, dst_ref=buf.at[me], send_sem=send.at[k - 1], recv_sem=recv.at[k - 1],
                device_id=peer, device_id_type=MESH))
        for cp in cps:
            cp.start()
        for k in range(1, 8):
            pltpu.make_async_remote_copy(
                src_ref=v_ref, dst_ref=buf.at[me ^ k], send_sem=send.at[k - 1], recv_sem=recv.at[k - 1],
                device_id=(x, y, c), device_id_type=MESH).wait_recv()
        for cp in cps:
            cp.wait_send()
        t = buf[0]
        for d in range(1, 8):
            t = t + buf[d]
        o_ref[...] = t

    return pl.pallas_call(
        body, name="allreduce_small",
        in_specs=[pl.BlockSpec(memory_space=pltpu.VMEM)], out_specs=pl.BlockSpec(memory_space=pltpu.VMEM),
        out_shape=jax.ShapeDtypeStruct((SMALL_ROWS, D), F32),
        scratch_shapes=[pltpu.VMEM((8, SMALL_ROWS, D), F32), pltpu.SemaphoreType.DMA((7,)),
                        pltpu.SemaphoreType.DMA((7,))],
    )(vec)


def _col_half(ref, slot, hc):
    return ref.at[slot, :, pl.ds(hc * HALF, HALF)]


def _stack_half(ref, slot, hc):
    return ref.at[slot, :, :, pl.ds(hc * HALF, HALF)]


def _row_tile(rows):
    for t in range(512, 15, -16):
        if rows % t == 0:
            return t
    return rows


def _same_shape_runs(arrs):
    runs, a = [], 0
    for b in range(1, len(arrs) + 1):
        if b == len(arrs) or arrs[b].shape != arrs[a].shape:
            runs.append((a, b))
            a = b
    return runs


class _Comm:
    def __init__(self):
        x, y, c = lax.axis_index("x"), lax.axis_index("y"), lax.axis_index("c")
        self.c_idx = jnp.reshape(c, (1,)).astype(jnp.int32)
        self.place = jnp.stack([2 * x + y, c]).astype(jnp.int32)
        self.groups = {}

    @staticmethod
    def gather(*bufs):
        return _gather_rider(list(bufs), [_col_half if b.ndim == 3 else _stack_half for b in bufs])

    def reduce_rider(self, tag, names, ps):
        csums = []
        for a, b in _same_shape_runs(ps):
            csums += _pair_sum("pair_sum_%s%d" % (tag, a), self.c_idx, ps[a:b], _row_tile(ps[a].shape[1]))
        self.groups[tag] = [names, csums, None]
        return _to_chips_rider(csums)

    def landed(self, tag, ts):
        self.groups[tag][2] = ts

    def finish(self):
        names, halves = [], []
        for tag, (group_names, csums, ts) in self.groups.items():
            names += group_names
            for a, b in _same_shape_runs(csums):
                halves += _chip_sum("chip_sum_%s%d" % (tag, a), self.place, csums[a:b], ts[a:b],
                                    _row_tile(csums[a].shape[1]))
        return dict(zip(names, _swap_halves(halves)))


ROPE_THETA = 10000.0
SMALL_1K = ("ffn1_pre_norm", "ffn1_post_norm", "mix_pre_norm", "ssm_norm", "mix_post_norm",
            "ffn2_pre_norm", "ffn2_post_norm")
SMALL_16 = ("dt_bias", "a_log", "d_skip")
OFF_CONVB = 7 * D
OFF_16 = OFF_CONVB + CONV_C
OFF_CONVW = OFF_16 + 48
OFF_LOSS = OFF_CONVW + CONV_K * CONV_C
SMALL_LEN = SMALL_ROWS * D


def _sds(shape, dtype):
    return jax.ShapeDtypeStruct(shape, dtype)


def _ridden(res, rider):
    return res if rider is not None else (res, None)


def _ffn_down(name, act, w, rider=None):
    return _mm(name, [act, w.dn], NN, (S // TS,),
               [pl.BlockSpec((NSH, TS, FS), lambda i: (0, i, 0)),
                pl.BlockSpec((NSH, None, FS, D), lambda i: (0, w.d0, 0, 0))],
               pl.BlockSpec((TS, D), lambda i: (i, 0)), _sds((S, D), F32), rider)


def _ffn_dw(name, a, b, rider=None):
    return _mm(name, [a, b], TN, (NSH,),
               [pl.BlockSpec((None, S, FS), lambda s: (s, 0, 0)), pl.BlockSpec((S, D), lambda s: (0, 0))],
               pl.BlockSpec((None, FS, D), lambda s: (s, 0, 0)), _sds((NSH, FS, D), BF16), rider)


def _ffn_dn(name, dgate, dup, w, rider=None):
    a2 = pl.BlockSpec((NSH, TS, FS), lambda i: (0, i, 0))
    return _mm(name, [dgate, w.gu, dup, w.gu], NN, (S // TS,),
               [a2, pl.BlockSpec((NSH, None, FS, D), lambda i: (0, w.g0, 0, 0)),
                a2, pl.BlockSpec((NSH, None, FS, D), lambda i: (0, w.g0 + 1, 0, 0))],
               pl.BlockSpec((TS, D), lambda i: (i, 0)), _sds((S, D), F32), rider)


def _out_proj_dx(dh, wout):
    def body(dh_ref, w_ref, dyn_ref, do_ref):
        dm = _dot(dh_ref[...], w_ref[...], NT)
        dyn_ref[...] = dm[:, D:]
        for b in range(TS // 128):
            for j, blk in enumerate(_rows_to_blocks(dm[128 * b:128 * (b + 1), :D])):
                do_ref[j, b] = blk.astype(BF16)

    return pl.pallas_call(
        body, name="out_proj_dx", grid=(S // TS,),
        in_specs=[pl.BlockSpec((TS, D), lambda i: (i, 0)), pl.BlockSpec((2 * D, D), lambda i: (0, 0))],
        out_specs=[pl.BlockSpec((TS, D), lambda i: (i, 0)),
                   pl.BlockSpec((NKV, TS // 128, HD, QROWS), lambda i: (0, i, 0, 0))],
        out_shape=[_sds((S, D), F32), _sds((NKV, NCH, HD, QROWS), BF16)], compiler_params=_cparams("parallel"),
    )(dh, wout)


def _heads(t, n):
    return t.reshape(S, n, HD).transpose(1, 0, 2)


def _unheads(t):
    return t.transpose(1, 0, 2).reshape(S, t.shape[0] * HD)


def _heads_t(t, n):
    return t.reshape(S, n, HD).transpose(1, 2, 0)


def _pad128(v):
    return jnp.pad(v, ((0, 0), (0, 128 - v.shape[1])))


def _local_step(x, positions, tgt, sp, gu1, d1, f2, wint, wout, convw, comm=None):
    inv_freq = ROPE_THETA ** (-jnp.arange(0, HD, 2, dtype=F32) / HD)
    ang = positions.astype(F32)[:, None] * inv_freq
    ang = jnp.concatenate([ang, ang, ang, ang], axis=-1)
    cos, sin = jnp.cos(ang), jnp.sin(ang)
    dtb, alog = _pad128(sp["dt_bias"]), _pad128(sp["a_log"])
    dskip_l = jnp.repeat(sp["d_skip"], HD, axis=1)
    convb = sp["conv_b"]

    n1 = _prenorm("prenorm1", x, sp["ffn1_pre_norm"])
    rider = comm.gather(d1) if comm else None
    (fg1, fu1, act1), got = _ridden(_ffn_up("ffn1_up", n1, _FfnW(gu1, 0, d1, 0), rider), rider)
    if comm:
        d1, = got
    w1 = _FfnW(gu1, 0, d1, 0)
    rider = comm.gather(wint) if comm else None
    h1, got = _ridden(_ffn_down("ffn1_down", act1, w1, rider), rider)
    if comm:
        wint, = got
    wint_pad = jnp.pad(wint.reshape(WIN_COLS, D), ((0, WIN_PAD - WIN_COLS), (0, 0)))
    x1, n2 = _postres("postres1", x, h1, sp["ffn1_post_norm"], 0.5, sp["mix_pre_norm"])

    pw = WIN_PAD // 3
    proj = _mm("in_proj", [n2, wint_pad], NT, (S // TS, 3),
               [pl.BlockSpec((TS, D), lambda i, j: (i, 0)), pl.BlockSpec((pw, D), lambda i, j: (j, 0))],
               pl.BlockSpec((TS, pw), lambda i, j: (i, j)), _sds((S, WIN_PAD), F32))
    qt = _rope_q(proj, cos, sin)
    k_rot = _rope("rope_k", proj, D // KVW, KVW, cos, sin, 1.0, 1.0)
    v_bf = proj[:, D + KVW:D + 2 * KVW].astype(BF16)
    kh, vh = _heads(k_rot, NKV), _heads(v_bf, NKV)
    kt, vt = _heads_t(k_rot, NKV), _heads_t(v_bf, NKV)
    bias = _bias_table()
    rider = comm.gather(f2, wout) if comm else None
    (ot, lse, attn), got = _ridden(_attn_fwd(qt, kh, vt, bias, rider), rider)
    if comm:
        f2, wout = got
    w2 = _FfnW(f2, 0, f2, 2)
    wout = wout.reshape(2 * D, D)
    xbc = _conv_fwd(proj, convw, convb)
    y, yn, hprev = _ssd_fwd(xbc, proj, dtb, alog, dskip_l, sp["ssm_norm"])
    mixed = jnp.concatenate([attn, yn], axis=1)
    h2 = _mm("out_proj", [mixed, wout], NN, (S // TS,),
             [pl.BlockSpec((TS, 2 * D), lambda i: (i, 0)), pl.BlockSpec((2 * D, D), lambda i: (0, 0))],
             pl.BlockSpec((TS, D), lambda i: (i, 0)), _sds((S, D), F32))
    x2, n3 = _postres("postres2", x1, h2, sp["mix_post_norm"], 1.0, sp["ffn2_pre_norm"])

    fg2, fu2, act2 = _ffn_up("ffn2_up", n3, w2)
    h3 = _ffn_down("ffn2_down", act2, w2)
    dy, dh3, dp3, loss = _final(x2, h3, sp["ffn2_post_norm"], tgt, 0.5)

    dgate2, dup2 = _ffn_dact("ffn2_dact", dh3, w2, fg2, fu2)
    dws2 = [_ffn_dw("ffn2_dwg", dgate2, n3), _ffn_dw("ffn2_dwu", dup2, n3), _ffn_dw("ffn2_dwd", act2, dh3)]
    dn3 = _ffn_dn("ffn2_dn", dgate2, dup2, w2)
    dx2, dh2, dg3, dp2 = _mid_bwd("mid_bwd2", dy, dn3, x2, sp["ffn2_pre_norm"], h2, sp["mix_post_norm"], 1.0)

    dyn, dot_ = _out_proj_dx(dh2, wout)
    dwout = _mm("out_proj_dw", [mixed, dh2], TN, (2,),
                [pl.BlockSpec((S, D), lambda m: (0, m)), pl.BlockSpec((S, D), lambda m: (0, 0))],
                pl.BlockSpec((D, D), lambda m: (m, 0)), _sds((2 * D, D), BF16))
    dwout = dwout.reshape(NSH, 2 * D // NSH, D)
    rider = comm.reduce_rider("a", BIG[3:6] + ("w_out",), dws2 + [dwout]) if comm else None
    (dxbc, dz, ddt, dssm, dsc), got = _ridden(
        _ssd_bwd(dyn, y, xbc, proj, hprev, dtb, alog, dskip_l, sp["ssm_norm"], rider), rider)
    if comm:
        comm.landed("a", got)
    du, dcw8, dcb = _conv_bwd(dxbc, proj, convw, convb)
    dqt, dkh, dvh = _attn_bwd(qt, kh, kt, vh, dot_, lse, _attn_delta(ot, dot_), bias)
    dq = _rope_dq(dqt, cos, sin)
    dk = _rope("rope_dk", _unheads(dkh), 0, KVW, cos, sin, -1.0, 1.0)
    dproj = jnp.concatenate([dq, dk, _unheads(dvh).astype(BF16), du, dz, ddt], axis=1)
    dwint = _mm("in_proj_dw", [dproj, n2], TN, (3,),
                [pl.BlockSpec((S, pw), lambda j: (0, j)), pl.BlockSpec((S, D), lambda j: (0, 0))],
                pl.BlockSpec((pw, D), lambda j: (j, 0)), _sds((WIN_PAD, D), BF16))
    dwint = dwint[:WIN_COLS].reshape(NSH, WIN_SH, D)

    def riding(tag, names, ps, call):
        rider = comm.reduce_rider(tag, names, ps) if comm else None
        res, got = _ridden(call(rider), rider)
        if comm:
            comm.landed(tag, got)
        return res

    dn2 = riding("b", ("w_in",), [dwint], lambda rider: _mm(
        "in_proj_dx", [dproj, wint_pad], NN, (S // TS,),
        [pl.BlockSpec((TS, WIN_PAD), lambda i: (i, 0)), pl.BlockSpec((WIN_PAD, D), lambda i: (0, 0))],
        pl.BlockSpec((TS, D), lambda i: (i, 0)), _sds((S, D), F32), rider))
    dx1, dh1, dg2, dp1 = _mid_bwd("mid_bwd1", dx2, dn2, x1, sp["mix_pre_norm"], h1, sp["ffn1_post_norm"], 0.5)

    dwd1 = _ffn_dw("ffn1_dwd", act1, dh1)
    dgate1, dup1 = riding("d", BIG[2:3], [dwd1], lambda rider: _ffn_dact("ffn1_dact", dh1, w1, fg1, fu1, rider))
    dwg1 = _ffn_dw("ffn1_dwg", dgate1, n1)
    dwu1 = riding("g", BIG[0:1], [dwg1], lambda rider: _ffn_dw("ffn1_dwu", dup1, n1, rider))
    dn1 = riding("u", BIG[1:2], [dwu1], lambda rider: _ffn_dn("ffn1_dn", dgate1, dup1, w1, rider))
    dws1 = [dwg1, dwu1, dwd1]
    grad_x, dg1 = _first_bwd(dx1, dn1, x, sp["ffn1_pre_norm"])

    small = jnp.concatenate([
        dg1[0], dp1[0], dg2[0], dssm[0], dp2[0], dg3[0], dp3[0], dcb[0],
        dsc[0, :16], dsc[1, :16], dsc[2, :16], dcw8[:CONV_K].reshape(-1), loss[0, :1]])
    small = jnp.pad(small, (0, SMALL_LEN - small.shape[0])).reshape(SMALL_ROWS, D)
    if comm is None:
        return grad_x, dws1 + dws2 + [dwint, dwout], small
    return grad_x, comm.finish(), small


WEIGHTS = ("ffn1_pre_norm", "ffn1_w_gate", "ffn1_w_up", "ffn1_w_down", "ffn1_post_norm", "mix_pre_norm", "w_in",
           "conv_w", "conv_b", "dt_bias", "a_log", "d_skip", "ssm_norm", "w_out", "mix_post_norm", "ffn2_pre_norm",
           "ffn2_w_gate", "ffn2_w_up", "ffn2_w_down", "ffn2_post_norm")
BIG = ("ffn1_w_gate", "ffn1_w_up", "ffn1_w_down", "ffn2_w_gate", "ffn2_w_up", "ffn2_w_down", "w_in", "w_out")
TRANSPOSED = ("ffn1_w_gate", "ffn1_w_up", "ffn2_w_gate", "ffn2_w_up", "w_in")
SMALL_ORDER = SMALL_1K + ("conv_b",) + SMALL_16
CONVW_SH = CONV_C // NSH


def _shard2d(t, name):
    return t[0].T if name in TRANSPOSED else t[0]


def _unshard2d(t, name):
    return (t.T if name in TRANSPOSED else t)[None]


def _rows3d(t):
    return t.transpose(2, 0, 1)


def _pack_small(d, prefix, shard_of_convw):
    flat = jnp.concatenate([d[prefix + n][0] for n in SMALL_ORDER] + [shard_of_convw.reshape(-1)])
    return jnp.pad(flat, (0, SMALL_LEN - flat.shape[0])).reshape(SMALL_ROWS, D)


def _unpack_small(block, like):
    flat = block.reshape(-1)
    out, off = {}, 0
    for n in SMALL_ORDER:
        size = like[n].shape[1]
        out[n] = flat[off:off + size].reshape(1, size)
        off += size
    out["conv_w"] = flat[off:off + CONV_K * CONVW_SH].reshape(1, CONV_K, CONVW_SH)
    return out


def kernel(x, positions, ffn1_pre_norm, ffn1_w_gate, ffn1_w_up, ffn1_w_down, ffn1_post_norm, mix_pre_norm, w_in, conv_w, conv_b, dt_bias, a_log, d_skip, ssm_norm, w_out, mix_post_norm, ffn2_pre_norm, ffn2_w_gate, ffn2_w_up, ffn2_w_down, ffn2_post_norm, loss_target, m_ffn1_pre_norm, m_ffn1_w_gate, m_ffn1_w_up, m_ffn1_w_down, m_ffn1_post_norm, m_mix_pre_norm, m_w_in, m_conv_w, m_conv_b, m_dt_bias, m_a_log, m_d_skip, m_ssm_norm, m_w_out, m_mix_post_norm, m_ffn2_pre_norm, m_ffn2_w_gate, m_ffn2_w_up, m_ffn2_w_down, m_ffn2_post_norm, v_ffn1_pre_norm, v_ffn1_w_gate, v_ffn1_w_up, v_ffn1_w_down, v_ffn1_post_norm, v_mix_pre_norm, v_w_in, v_conv_w, v_conv_b, v_dt_bias, v_a_log, v_d_skip, v_ssm_norm, v_w_out, v_mix_post_norm, v_ffn2_pre_norm, v_ffn2_w_gate, v_ffn2_w_up, v_ffn2_w_down, v_ffn2_post_norm):
    given = dict(locals())
    xi, yi = lax.axis_index("x"), lax.axis_index("y")

    shard = jnp.reshape(2 * xi + yi, (1,)).astype(jnp.int32)
    big = {p + n: _shard2d(given[p + n], n) for n in BIG for p in ("", "m_", "v_")}
    gu1 = _cast_stack("cast_ffn1_gate_up", shard, [big[n] for n in BIG[0:2]], 176, D)
    d1 = _cast_stack("cast_ffn1_down", shard, [big[BIG[2]]], 176, D)
    f2 = _cast_stack("cast_ffn2", shard, [big[n] for n in BIG[3:6]], 176, D)
    winsh = _cast_stack("cast_w_in", shard, [big["w_in"]], WIN_SH, 256).reshape(NSH, WIN_SH, D)
    woutsh = _cast_stack("cast_w_out", shard, [big["w_out"]], 256, D).reshape(NSH, 2 * D // NSH, D)
    comm = _Comm()
    (gu1,), (cwf,) = _run_riders("gather_ffn1_gate_up", [comm.gather(gu1), _small_gather_rider(conv_w[0])])
    convw = cwf.transpose(1, 0, 2).reshape(CONV_K, CONV_C)

    sp = {n: given[n] for n in SMALL_ORDER}
    grad_x, big_grads, small = _local_step(x[0], positions[0], loss_target[0], sp, gu1, d1, f2, winsh, woutsh,
                                           convw, comm)

    tot = _allreduce_small(small).reshape(-1)
    loss = tot[OFF_LOSS]
    small_grads, off = {}, 0
    for n in SMALL_ORDER:
        size = given[n].shape[1]
        small_grads[n] = tot[off:off + size].reshape(1, size)
        off += size
    dconvw = tot[OFF_CONVW:OFF_CONVW + CONV_K * CONV_C].reshape(CONV_K, NSH, CONVW_SH)
    dconvw = lax.dynamic_index_in_dim(dconvw, 2 * xi + yi, axis=1, keepdims=False)
    small_grads["conv_w"] = dconvw.reshape(1, CONV_K, CONVW_SH)

    upd = {}
    for names, tr in ((BIG[0:3], 176), (BIG[3:6], 176), (BIG[7:8], 256)):
        res = _adamw("adamw_" + names[0], [big[n] for n in names], [big_grads[n] for n in names],
                     [big["m_" + n] for n in names], [big["v_" + n] for n in names], tr, D)
        for n, r in zip(names, res):
            upd[n] = tuple(_unshard2d(t, n) for t in r)
    g_win = big_grads["w_in"].reshape(WIN_SH, 1, D)
    res, = _adamw("adamw_w_in", [_rows3d(w_in)], [g_win], [_rows3d(m_w_in)], [_rows3d(v_w_in)], WIN_SH // 4, D)
    upd["w_in"] = tuple(t.transpose(1, 2, 0) for t in res)
    (dl, m2, v2), = _adamw(
        "adamw_small", [_pack_small(given, "", conv_w[0])], [_pack_small(small_grads, "", dconvw)],
        [_pack_small(given, "m_", m_conv_w[0])], [_pack_small(given, "v_", v_conv_w[0])], SMALL_ROWS, D)
    dl, m2, v2 = (_unpack_small(t, given) for t in (dl, m2, v2))
    for n in SMALL_ORDER + ("conv_w",):
        upd[n] = (dl[n], m2[n], v2[n])

    grads = dict(small_grads)
    grads.update({n: _unshard2d(g, n) for n, g in big_grads.items() if n != "w_in"})
    grads["w_in"] = g_win.transpose(1, 2, 0)
    return (loss, grad_x[None], *[grads[n] for n in WEIGHTS], *[upd[n][0] for n in WEIGHTS],
            *[upd[n][1] for n in WEIGHTS], *[upd[n][2] for n in WEIGHTS])
```

```python
import functools
import typing

import jax
import jax.numpy as jnp
from jax import lax
from jax.experimental import pallas as pl
from jax.experimental.pallas import tpu as pltpu

F32 = jnp.float32
BF16 = jnp.bfloat16

S = 2048
D = 1024
FF = 2816
NSH = 4
FS = FF // NSH
HALF = D // 2
HD = 64
NKV = 4
NQ_PER_KV = 4
KVW = NKV * HD
QCOLS = NQ_PER_KV * HD
CONV_C = 1536
CONV_K = 4
SSM_W = 1024
NST = 128
NCH = S // 128
WIN_COLS = 4112
WIN_SH = WIN_COLS // NSH
WIN_PAD = 4224
COL_DT = 4096
EPS = 1e-6
NEG = -1e30

ADAM_LR = 0.001
ADAM_B1 = 0.9
ADAM_B2 = 0.999
ADAM_EPS = 1e-08
ADAM_WD = 0.01
ADAM_STEP = 10

VMEM_LIMIT = 56 * 1024 * 1024
TS = 512
TR = 256

NN = (((1,), (0,)), ((), ()))
NT = (((1,), (1,)), ((), ()))
TN = (((0,), (0,)), ((), ()))
MESH = pl.DeviceIdType.MESH


def _cparams(*sem):
    return pltpu.CompilerParams(dimension_semantics=sem, vmem_limit_bytes=VMEM_LIMIT)


def _dot(a, b, dims):
    return lax.dot_general(a.astype(BF16), b.astype(BF16), dims, preferred_element_type=F32)


def _dot_exact(a, b):
    return lax.dot_general(a, b, NN, precision=lax.Precision.HIGHEST, preferred_element_type=F32)


def _sigmoid(v):
    return 1.0 / (1.0 + jnp.exp(-v))


class _Rider(typing.NamedTuple):
    operands: list
    out_shapes: list
    aliases: dict
    sems: list
    start: typing.Callable
    finish: typing.Callable


def _call(body, name, grid, in_specs, out_specs, out_shape, operands, scratch=(), sem=(), rider=None):
    multi = isinstance(out_shape, (list, tuple))
    if rider is None:
        return pl.pallas_call(
            body, name=name, grid=grid, in_specs=in_specs, out_specs=out_specs, out_shape=out_shape,
            scratch_shapes=list(scratch), compiler_params=_cparams(*sem))(*operands)
    outs = list(out_shape) if multi else [out_shape]
    ospecs = list(out_specs) if multi else [out_specs]
    n_in, n_out, n_scr = len(operands), len(outs), len(scratch)
    ri, ro = len(rider.operands), len(rider.out_shapes)

    def wrapped(*refs):
        o0 = n_in + ri
        s0 = o0 + n_out + ro
        rin, rout, rsem = refs[n_in:o0], refs[o0 + n_out:s0], refs[s0 + n_scr:]
        ids = [pl.program_id(a) for a in range(len(grid))]
        first = functools.reduce(jnp.logical_and, [i == 0 for i in ids])
        last = functools.reduce(jnp.logical_and, [i == g - 1 for i, g in zip(ids, grid)])

        @pl.when(first)
        def _():
            rider.start(rin, rout, rsem)

        body(*refs[:n_in], *refs[o0:o0 + n_out], *refs[s0:s0 + n_scr])

        @pl.when(last)
        def _():
            rider.finish(rin, rout, rsem)

    hbm = pl.BlockSpec(memory_space=pl.ANY)
    res = pl.pallas_call(
        wrapped, name=name, grid=grid, in_specs=list(in_specs) + [hbm] * ri, out_specs=ospecs + [hbm] * ro,
        out_shape=outs + list(rider.out_shapes), scratch_shapes=list(scratch) + list(rider.sems),
        input_output_aliases={n_in + k: n_out + v for k, v in rider.aliases.items()},
        compiler_params=_cparams(*(("arbitrary",) * len(grid))))(*operands, *rider.operands)
    main = list(res[:n_out])
    return (main if multi else main[0]), list(res[n_out:])


def _mm(name, operands, dims, grid, in_specs, o_spec, out_shape, rider=None):
    npairs = len(operands) // 2

    def body(*refs):
        t = None
        for i in range(npairs):
            a, b = refs[2 * i], refs[2 * i + 1]
            parts = [(a[s], b[s]) for s in range(a.shape[0])] if len(a.shape) == 3 else [(a[...], b[...])]
            for pa, pb in parts:
                d = _dot(pa, pb, dims)
                t = d if t is None else t + d
        refs[2 * npairs][...] = t.astype(refs[2 * npairs].dtype)

    return _call(body, name, grid, in_specs, o_spec, out_shape, operands, (), ("parallel",) * len(grid), rider)


class _FfnW(typing.NamedTuple):
    gu: jax.Array
    g0: int
    dn: jax.Array
    d0: int


def _ffn_up(name, n, w, rider=None):
    def body(n_ref, wg_ref, wu_ref, fg_ref, fu_ref, a_ref):
        nb = n_ref[...]
        g = _dot(nb, wg_ref[...], NT)
        u = _dot(nb, wu_ref[...], NT)
        sg = _sigmoid(g)
        silu = g * sg
        fg_ref[...] = (u * (sg * (1.0 + g * (1.0 - sg)))).astype(BF16)
        fu_ref[...] = silu.astype(BF16)
        a_ref[...] = (silu * u).astype(BF16)

    out = jax.ShapeDtypeStruct((NSH, S, FS), BF16)
    ospec = pl.BlockSpec((None, TS, FS), lambda s, i: (s, i, 0))
    return _call(
        body, name, (NSH, S // TS),
        [pl.BlockSpec((TS, D), lambda s, i: (i, 0)),
         pl.BlockSpec((None, None, FS, D), lambda s, i: (s, w.g0, 0, 0)),
         pl.BlockSpec((None, None, FS, D), lambda s, i: (s, w.g0 + 1, 0, 0))],
        [ospec, ospec, ospec], [out, out, out], (n, w.gu, w.gu), sem=("parallel", "parallel"), rider=rider)


def _ffn_dact(name, dh, w, fgate, fup, rider=None):
    def body(dh_ref, wd_ref, fg_ref, fu_ref, dg_ref, du_ref):
        da = _dot(dh_ref[...], wd_ref[...], NT)
        dg_ref[...] = (da * fg_ref[...].astype(F32)).astype(BF16)
        du_ref[...] = (da * fu_ref[...].astype(F32)).astype(BF16)

    out = jax.ShapeDtypeStruct((NSH, S, FS), BF16)
    aspec = pl.BlockSpec((None, TS, FS), lambda s, i: (s, i, 0))
    return _call(
        body, name, (NSH, S // TS),
        [pl.BlockSpec((TS, D), lambda s, i: (i, 0)),
         pl.BlockSpec((None, None, FS, D), lambda s, i: (s, w.d0, 0, 0)), aspec, aspec],
        [aspec, aspec], [out, out], (dh, w.dn, fgate, fup), sem=("parallel", "parallel"), rider=rider)


def _rstd(v):
    return lax.rsqrt(jnp.mean(v * v, axis=-1, keepdims=True) + EPS)


def _row_spec():
    return pl.BlockSpec((TR, D), lambda i: (i, 0))


def _vec_spec():
    return pl.BlockSpec((1, D), lambda i: (0, 0))


def _acc_rows(ref, v):
    @pl.when(pl.program_id(0) == 0)
    def _():
        ref[...] = jnp.zeros_like(ref)
    ref[...] += jnp.sum(v, axis=0, keepdims=True)


def _prenorm(name, x, g):
    def body(x_ref, g_ref, n_ref):
        xv = x_ref[...]
        n_ref[...] = (xv * _rstd(xv) * g_ref[...]).astype(BF16)

    return pl.pallas_call(
        body, name=name, grid=(S // TR,), in_specs=[_row_spec(), _vec_spec()], out_specs=_row_spec(),
        out_shape=jax.ShapeDtypeStruct((S, D), BF16), compiler_params=_cparams("parallel"),
    )(x, g)


def _postres(name, x, h, p, alpha, gnext):
    def body(x_ref, h_ref, p_ref, g_ref, xo_ref, n_ref):
        hv = h_ref[...]
        xo = x_ref[...] + alpha * (hv * _rstd(hv) * p_ref[...])
        xo_ref[...] = xo
        n_ref[...] = (xo * _rstd(xo) * g_ref[...]).astype(BF16)

    return pl.pallas_call(
        body, name=name, grid=(S // TR,),
        in_specs=[_row_spec(), _row_spec(), _vec_spec(), _vec_spec()],
        out_specs=[_row_spec(), _row_spec()],
        out_shape=[jax.ShapeDtypeStruct((S, D), F32), jax.ShapeDtypeStruct((S, D), BF16)],
        compiler_params=_cparams("parallel"),
    )(x, h, p, gnext)


def _final(x, h, p, tgt, alpha):
    def body(x_ref, h_ref, p_ref, t_ref, dy_ref, dh_ref, dp_ref, loss_ref):
        hv = h_ref[...]
        r = _rstd(hv)
        hn = hv * r
        pv = p_ref[...]
        e = x_ref[...] + alpha * (hn * pv) - t_ref[...]
        dy = e * (1.0 / D)
        dy_ref[...] = dy
        du = alpha * dy * pv
        dh_ref[...] = (r * (du - hn * jnp.mean(du * hn, axis=-1, keepdims=True))).astype(BF16)
        _acc_rows(dp_ref, alpha * dy * hn)
        part = 0.5 * jnp.sum(jnp.mean(e * e, axis=-1, keepdims=True), axis=0, keepdims=True)
        _acc_rows(loss_ref, jnp.broadcast_to(part, (1, 128)))

    return pl.pallas_call(
        body, name="loss_head", grid=(S // TR,),
        in_specs=[_row_spec(), _row_spec(), _vec_spec(), _row_spec()],
        out_specs=[_row_spec(), _row_spec(), _vec_spec(), pl.BlockSpec((1, 128), lambda i: (0, 0))],
        out_shape=[jax.ShapeDtypeStruct((S, D), F32), jax.ShapeDtypeStruct((S, D), BF16),
                   jax.ShapeDtypeStruct((1, D), F32), jax.ShapeDtypeStruct((1, 128), F32)],
        compiler_params=_cparams("arbitrary"),
    )(x, h, p, tgt)


def _mid_bwd(name, dres, dn, x, g, h, p, alpha):
    def body(dr_ref, dn_ref, x_ref, g_ref, h_ref, p_ref, dx_ref, dh_ref, dg_ref, dp_ref):
        xv = x_ref[...]
        xn = xv * _rstd(xv)
        dnv = dn_ref[...]
        dng = dnv * g_ref[...]
        dx = dr_ref[...] + _rstd(xv) * (dng - xn * jnp.mean(dng * xn, axis=-1, keepdims=True))
        dx_ref[...] = dx
        _acc_rows(dg_ref, dnv * xn)
        hv = h_ref[...]
        r = _rstd(hv)
        hn = hv * r
        du = alpha * dx * p_ref[...]
        dh_ref[...] = (r * (du - hn * jnp.mean(du * hn, axis=-1, keepdims=True))).astype(BF16)
        _acc_rows(dp_ref, alpha * dx * hn)

    return pl.pallas_call(
        body, name=name, grid=(S // TR,),
        in_specs=[_row_spec(), _row_spec(), _row_spec(), _vec_spec(), _row_spec(), _vec_spec()],
        out_specs=[_row_spec(), _row_spec(), _vec_spec(), _vec_spec()],
        out_shape=[jax.ShapeDtypeStruct((S, D), F32), jax.ShapeDtypeStruct((S, D), BF16),
                   jax.ShapeDtypeStruct((1, D), F32), jax.ShapeDtypeStruct((1, D), F32)],
        compiler_params=_cparams("arbitrary"),
    )(dres, dn, x, g, h, p)


def _first_bwd(dres, dn, x, g):
    def body(dr_ref, dn_ref, x_ref, g_ref, dx_ref, dg_ref):
        xv = x_ref[...]
        r = _rstd(xv)
        xn = xv * r
        dnv = dn_ref[...]
        dng = dnv * g_ref[...]
        dx_ref[...] = dr_ref[...] + r * (dng - xn * jnp.mean(dng * xn, axis=-1, keepdims=True))
        _acc_rows(dg_ref, dnv * xn)

    return pl.pallas_call(
        body, name="first_bwd", grid=(S // TR,),
        in_specs=[_row_spec(), _row_spec(), _row_spec(), _vec_spec()],
        out_specs=[_row_spec(), _vec_spec()],
        out_shape=[jax.ShapeDtypeStruct((S, D), F32), jax.ShapeDtypeStruct((1, D), F32)],
        compiler_params=_cparams("arbitrary"),
    )(dres, dn, x, g)


def _rotate(t, c128, s128, sign, scale):
    width = t.shape[1]
    c = jnp.tile(c128, (1, width // 128))
    sn = jnp.tile(s128, (1, width // 128))
    lane = lax.broadcasted_iota(jnp.int32, t.shape, 1) & (HD - 1)
    rot = jnp.where(lane < HD // 2, -pltpu.roll(t, width - HD // 2, 1), pltpu.roll(t, HD // 2, 1))
    return (t * c + sign * (rot * sn)) * scale


def _rows_to_blocks(y):
    out = []
    for j in range(NKV):
        yt = y[:, QCOLS * j:QCOLS * (j + 1)].T
        out.append(jnp.concatenate([yt[HD * g:HD * (g + 1)] for g in range(NQ_PER_KV)], axis=1))
    return out


def _blocks_to_rows(blocks):
    cols = []
    for b in blocks:
        stacked = jnp.concatenate([b[:, 128 * g:128 * (g + 1)] for g in range(NQ_PER_KV)], axis=0)
        cols.append(stacked.T)
    return jnp.concatenate(cols, axis=1)


def _rope_q(proj, cos, sin):
    def body(t_ref, c_ref, s_ref, o_ref):
        y = _rotate(t_ref[...], c_ref[...], s_ref[...], 1.0, HD ** -0.5)
        for j, blk in enumerate(_rows_to_blocks(y)):
            o_ref[j] = blk.astype(BF16)

    return pl.pallas_call(
        body, name="rope_q", grid=(NCH,),
        in_specs=[pl.BlockSpec((128, D), lambda i: (i, 0)),
                  pl.BlockSpec((128, 128), lambda i: (i, 0)), pl.BlockSpec((128, 128), lambda i: (i, 0))],
        out_specs=pl.BlockSpec((NKV, None, HD, QROWS), lambda i: (0, i, 0, 0)),
        out_shape=jax.ShapeDtypeStruct((NKV, NCH, HD, QROWS), BF16), compiler_params=_cparams("parallel"),
    )(proj, cos, sin)


def _rope_dq(dqt, cos, sin, dproj):
    def body(t_ref, c_ref, s_ref, buf_ref, o_ref):
        t = _blocks_to_rows([t_ref[j] for j in range(NKV)])
        o_ref[...] = _rotate(t, c_ref[...], s_ref[...], -1.0, HD ** -0.5).astype(BF16)

    return pl.pallas_call(
        body, name="rope_dq", grid=(NCH,),
        in_specs=[pl.BlockSpec((NKV, None, HD, QROWS), lambda i: (0, i, 0, 0)),
                  pl.BlockSpec((128, 128), lambda i: (i, 0)), pl.BlockSpec((128, 128), lambda i: (i, 0)),
                  pl.BlockSpec(memory_space=pl.ANY)],
        out_specs=pl.BlockSpec((128, D), lambda i: (i, 0)),
        out_shape=jax.ShapeDtypeStruct(dproj.shape, BF16), input_output_aliases={3: 0},
        compiler_params=_cparams("parallel"),
    )(dqt, cos, sin, dproj)


def _rope(name, src, col_block, width, cos, sin, sign, scale, into=None):
    def body(t_ref, c_ref, s_ref, *rest):
        rest[-1][...] = _rotate(t_ref[...].astype(F32), c_ref[...], s_ref[...], sign, scale).astype(BF16)

    in_specs = [pl.BlockSpec((TR, width), lambda i: (i, col_block)),
                pl.BlockSpec((TR, 128), lambda i: (i, 0)), pl.BlockSpec((TR, 128), lambda i: (i, 0))]
    if into is None:
        return pl.pallas_call(
            body, name=name, grid=(S // TR,), in_specs=in_specs,
            out_specs=pl.BlockSpec((TR, width), lambda i: (i, 0)),
            out_shape=jax.ShapeDtypeStruct((S, width), BF16), compiler_params=_cparams("parallel"),
        )(src, cos, sin)
    buf, out_block = into
    return pl.pallas_call(
        body, name=name, grid=(S // TR,), in_specs=in_specs + [pl.BlockSpec(memory_space=pl.ANY)],
        out_specs=pl.BlockSpec((TR, width), lambda i: (i, out_block)),
        out_shape=jax.ShapeDtypeStruct(buf.shape, BF16), input_output_aliases={3: 0},
        compiler_params=_cparams("parallel"),
    )(src, cos, sin, buf)


QROWS = NQ_PER_KV * 128


NBIAS = NCH + 1
KV_PER_STEP = 4


def _bias_table():
    db = lax.broadcasted_iota(jnp.int32, (NBIAS, 128, QROWS), 0) - 1
    ki = lax.broadcasted_iota(jnp.int32, (NBIAS, 128, QROWS), 1)
    qi = lax.broadcasted_iota(jnp.int32, (NBIAS, 128, QROWS), 2) & 127
    d = db * 128 + qi - ki
    cnt = ((d <= 128).astype(F32) + (((d & 3) == 0) & (d <= 512)).astype(F32) + ((d & 15) == 0).astype(F32))
    return jnp.where((d >= 0) & (cnt > 0.0), jnp.log(jnp.maximum(cnt, 1.0)), NEG)


def _qt_spec():
    return pl.BlockSpec((None, None, HD, QROWS), lambda j, i: (j, i, 0, 0))


def _stat_spec():
    return pl.BlockSpec((None, None, 1, QROWS), lambda j, i: (j, i, 0, 0))


def _attn_fwd(qt, kh, vt, bias, rider=None):
    def body(q_ref, k_ref, v_ref, b_ref, o_ref, lse_ref, rows_ref):
        qb = pl.program_id(1)

        def keys(carry, off, size, bias_):
            out = []
            for h in range(KV_PER_STEP):
                m, l, acc = carry[3 * h:3 * h + 3]
                s = _dot(k_ref[h, pl.ds(off, size), :], q_ref[h], NN) + bias_
                m_new = jnp.maximum(m, jnp.max(s, axis=0, keepdims=True))
                p = jnp.exp(s - m_new)
                a = jnp.exp(m - m_new)
                out += [m_new, a * l + jnp.sum(p, axis=0, keepdims=True),
                        a * acc + _dot(v_ref[h, :, pl.ds(off, size)], p, NN)]
            return tuple(out)

        def pair(i, carry):
            bias2 = jnp.concatenate([b_ref[qb - 2 * i + 1], b_ref[qb - 2 * i]], axis=0)
            return keys(carry, pl.multiple_of(i * 256, 256), 256, bias2)

        init = (jnp.full((1, QROWS), NEG, F32), jnp.zeros((1, QROWS), F32), jnp.zeros((HD, QROWS), F32))
        res = lax.fori_loop(0, (qb + 1) // 2, pair, init * KV_PER_STEP)
        res = lax.cond(qb % 2 == 0,
                       lambda c: keys(c, pl.multiple_of(qb * 128, 128), 128, b_ref[1]), lambda c: c, res)
        outs = []
        for h in range(KV_PER_STEP):
            m, l, acc = res[3 * h:3 * h + 3]
            outs.append(acc / l)
            o_ref[h] = outs[h]
            lse_ref[h] = m + jnp.log(l)
        rows_ref[...] = _blocks_to_rows(outs).astype(BF16)

    kvs = KV_PER_STEP
    qspec = pl.BlockSpec((kvs, None, HD, QROWS), lambda j, i: (j, i, 0, 0))
    return _call(
        body, "attn_fwd", (NKV // kvs, NCH),
        [qspec, pl.BlockSpec((kvs, S, HD), lambda j, i: (j, 0, 0)),
         pl.BlockSpec((kvs, HD, S), lambda j, i: (j, 0, 0)),
         pl.BlockSpec((NBIAS, 128, QROWS), lambda j, i: (0, 0, 0))],
        [qspec, pl.BlockSpec((kvs, None, 1, QROWS), lambda j, i: (j, i, 0, 0)),
         pl.BlockSpec((128, QCOLS * kvs), lambda j, i: (i, j))],
        [jax.ShapeDtypeStruct((NKV, NCH, HD, QROWS), F32), jax.ShapeDtypeStruct((NKV, NCH, 1, QROWS), F32),
         jax.ShapeDtypeStruct((S, D), BF16)],
        (qt, kh, vt, bias), sem=("parallel", "parallel"), rider=rider)


def _attn_delta(ot, dot_):
    def body(o_ref, do_ref, dl_ref):
        dl_ref[...] = jnp.sum(o_ref[...] * do_ref[...].astype(F32), axis=1, keepdims=True)

    spec = pl.BlockSpec((None, NCH, HD, QROWS), lambda j: (j, 0, 0, 0))
    return pl.pallas_call(
        body, name="attn_delta", grid=(NKV,), in_specs=[spec, spec],
        out_specs=pl.BlockSpec((None, NCH, 1, QROWS), lambda j: (j, 0, 0, 0)),
        out_shape=jax.ShapeDtypeStruct((NKV, NCH, 1, QROWS), F32), compiler_params=_cparams("parallel"),
    )(ot, dot_)


def _attn_bwd(qt, kh, kt, vh, dot_, lse, delta, bias):
    def body(qt_ref, k_ref, kt_ref, v_ref, dot_ref, lse_ref, dl_ref, b_ref, dq_ref, dk_ref, dv_ref):
        kb = pl.program_id(1)

        @pl.when(kb == 0)
        def _():
            dq_ref[...] = jnp.zeros_like(dq_ref)

        def blocks(carry, qbs):
            out = list(carry)
            for h in range(KV_PER_STEP):
                k, kt_, v = k_ref[h], kt_ref[h], v_ref[h]
                for qb in qbs:
                    st = _dot(k, qt_ref[h, qb], NN) + b_ref[qb - kb + 1]
                    pt = jnp.exp(st - lse_ref[h, qb])
                    dst = pt * (_dot(v, dot_ref[h, qb], NN) - dl_ref[h, qb])
                    dq_ref[h, qb] += _dot(kt_, dst, NN)
                    out[2 * h] = out[2 * h] + _dot(dst, qt_ref[h, qb], NT)
                    out[2 * h + 1] = out[2 * h + 1] + _dot(pt, dot_ref[h, qb], NT)
            return tuple(out)

        res = (jnp.zeros((128, HD), F32),) * (2 * KV_PER_STEP)
        res = lax.cond(kb % 2 == 1, lambda c: blocks(c, (kb,)), lambda c: c, res)
        res = lax.fori_loop((kb + 1) // 2, NCH // 2, lambda j, c: blocks(c, (2 * j, 2 * j + 1)), res)
        for h in range(KV_PER_STEP):
            dk_ref[h] = res[2 * h]
            dv_ref[h] = res[2 * h + 1]

    kvs = KV_PER_STEP
    tspec = pl.BlockSpec((kvs, NCH, HD, QROWS), lambda j, i: (j, 0, 0, 0))
    kspec = pl.BlockSpec((kvs, 128, HD), lambda j, i: (j, i, 0))
    sspec = pl.BlockSpec((kvs, NCH, 1, QROWS), lambda j, i: (j, 0, 0, 0))
    return pl.pallas_call(
        body, name="attn_bwd", grid=(NKV // kvs, NCH),
        in_specs=[tspec, kspec, pl.BlockSpec((kvs, HD, 128), lambda j, i: (j, 0, i)), kspec, tspec,
                  sspec, sspec, pl.BlockSpec((NBIAS, 128, QROWS), lambda j, i: (0, 0, 0))],
        out_specs=[tspec, kspec, kspec],
        out_shape=[jax.ShapeDtypeStruct((NKV, NCH, HD, QROWS), F32),
                   jax.ShapeDtypeStruct((NKV, S, HD), F32), jax.ShapeDtypeStruct((NKV, S, HD), F32)],
        compiler_params=_cparams("parallel", "arbitrary"),
    )(qt, kh, kt, vh, dot_, lse, delta, bias)


CONV_BLK = 256
CONV_COL0 = 1536 // CONV_BLK


def _shift_down(u, j, row):
    return jnp.where(row >= j, pltpu.roll(u, j, 0), 0.0)


def _conv_pre(u, w_ref, b_ref, row):
    y = b_ref[...] + w_ref[CONV_K - 1:CONV_K, :] * u
    for j in range(1, CONV_K):
        y = y + w_ref[CONV_K - 1 - j:CONV_K - j, :] * _shift_down(u, j, row)
    return y


def _conv_fwd(proj, convw, convb):
    def body(u_ref, w_ref, b_ref, o_ref):
        u = u_ref[...]
        row = lax.broadcasted_iota(jnp.int32, u.shape, 0)
        y = _conv_pre(u, w_ref, b_ref, row)
        o_ref[...] = y * _sigmoid(y)

    return pl.pallas_call(
        body, name="conv_fwd", grid=(CONV_C // CONV_BLK,),
        in_specs=[pl.BlockSpec((S, CONV_BLK), lambda i: (0, CONV_COL0 + i)),
                  pl.BlockSpec((CONV_K, CONV_BLK), lambda i: (0, i)),
                  pl.BlockSpec((1, CONV_BLK), lambda i: (0, i))],
        out_specs=pl.BlockSpec((S, CONV_BLK), lambda i: (0, i)),
        out_shape=jax.ShapeDtypeStruct((S, CONV_C), F32), compiler_params=_cparams("parallel"),
    )(proj, convw, convb)


def _conv_bwd(dact, proj, convw, convb, dproj):
    def body(da_ref, u_ref, w_ref, b_ref, buf_ref, du_ref, dw_ref, db_ref):
        u = u_ref[...]
        row = lax.broadcasted_iota(jnp.int32, u.shape, 0)
        y = _conv_pre(u, w_ref, b_ref, row)
        sg = _sigmoid(y)
        dy = da_ref[...] * (sg * (1.0 + y * (1.0 - sg)))
        db_ref[...] = jnp.sum(dy, axis=0, keepdims=True)
        du = w_ref[CONV_K - 1:CONV_K, :] * dy
        r8 = lax.broadcasted_iota(jnp.int32, (8, CONV_BLK), 0)
        dw = jnp.where(r8 == CONV_K - 1, jnp.sum(dy * u, axis=0, keepdims=True), 0.0)
        for j in range(1, CONV_K):
            du = du + w_ref[CONV_K - 1 - j:CONV_K - j, :] * jnp.where(row < S - j, pltpu.roll(dy, S - j, 0), 0.0)
            dw = dw + jnp.where(r8 == CONV_K - 1 - j,
                                jnp.sum(dy * _shift_down(u, j, row), axis=0, keepdims=True), 0.0)
        du_ref[...] = du.astype(BF16)
        dw_ref[...] = dw

    return pl.pallas_call(
        body, name="conv_bwd", grid=(CONV_C // CONV_BLK,),
        in_specs=[pl.BlockSpec((S, CONV_BLK), lambda i: (0, i)),
                  pl.BlockSpec((S, CONV_BLK), lambda i: (0, CONV_COL0 + i)),
                  pl.BlockSpec((CONV_K, CONV_BLK), lambda i: (0, i)),
                  pl.BlockSpec((1, CONV_BLK), lambda i: (0, i)), pl.BlockSpec(memory_space=pl.ANY)],
        out_specs=[pl.BlockSpec((S, CONV_BLK), lambda i: (0, CONV_COL0 + i)),
                   pl.BlockSpec((8, CONV_BLK), lambda i: (0, i)), pl.BlockSpec((1, CONV_BLK), lambda i: (0, i))],
        out_shape=[jax.ShapeDtypeStruct(dproj.shape, BF16), jax.ShapeDtypeStruct((8, CONV_C), F32),
                   jax.ShapeDtypeStruct((1, CONV_C), F32)],
        input_output_aliases={4: 0}, compiler_params=_cparams("parallel"),
    )(dact, proj, convw, convb, dproj)


NPAIR = 8


def _ssd_scalars(dtr_ref, dtb_ref, alog_ref):
    z = dtr_ref[...] + dtb_ref[...]
    dt = jnp.maximum(z, 0.0) + jnp.log(1.0 + jnp.exp(-jnp.abs(z)))
    a = -jnp.exp(alog_ref[...])
    r = lax.broadcasted_iota(jnp.int32, (128, 128), 0)
    c = lax.broadcasted_iota(jnp.int32, (128, 128), 1)
    tri = (r >= c).astype(F32)
    cs = _dot_exact(tri, dt * a)
    return z, dt, a, cs, r, c


def _pair_terms(cs, cst, dt, h1, h2, lo):
    c1, c2 = cs[:, h1:h1 + 1], cs[:, h2:h2 + 1]
    l1, l2 = cs[127:128, h1:h1 + 1], cs[127:128, h2:h2 + 1]
    e_l = jnp.where(lo, jnp.exp(c1), jnp.exp(c2))
    dte1, dte2 = jnp.exp(l1 - c1), jnp.exp(l2 - c2)
    dte_l = jnp.where(lo, dte1, dte2)
    dt_l = jnp.where(lo, dt[:, h1:h1 + 1], dt[:, h2:h2 + 1])
    return c1, c2, jnp.exp(l1), jnp.exp(l2), e_l, dte1, dte2, dte_l, dt_l


def _gate_norm(y, zv, w):
    yg = y * (zv * _sigmoid(zv))
    outs, rs = [], []
    for g in range(2):
        blk = yg[:, 512 * g:512 * (g + 1)]
        r = lax.rsqrt(jnp.mean(blk * blk, axis=-1, keepdims=True) + EPS)
        outs.append(blk * r)
        rs.append(r)
    return jnp.concatenate(outs, axis=1), rs, yg


def _ssd_fwd(xbc, proj, dtb, alog, dskip_l, ssmw):
    def body(x_ref, b_ref, c_ref, dtr_ref, z_ref, dtb_ref, alog_ref, dsk_ref, w_ref, y_ref, yn_ref, hp_ref, h_ref):
        @pl.when(pl.program_id(0) == 0)
        def _():
            h_ref[...] = jnp.zeros_like(h_ref)

        _, dt, _, cs, r, c = _ssd_scalars(dtr_ref, dtb_ref, alog_ref)
        cst = cs.T
        causal = r >= c
        lo = c < HD
        hp_ref[...] = h_ref[...]
        for g in range(2):
            bg = b_ref[:, 128 * g:128 * (g + 1)]
            cg = c_ref[:, 128 * g:128 * (g + 1)]
            cb = _dot(cg, bg, NT)
            for j in range(4):
                pj = 4 * g + j
                h1, h2 = 2 * pj, 2 * pj + 1
                sl = slice(128 * pj, 128 * (pj + 1))
                xp = x_ref[:, sl]
                c1, c2, cd1, cd2, e_l, _, _, dte_l, dt_l = _pair_terms(cs, cst, dt, h1, h2, lo)
                xdt = xp * dt_l
                m1 = cb * jnp.exp(jnp.where(causal, c1 - cst[h1:h1 + 1, :], NEG))
                m2 = cb * jnp.exp(jnp.where(causal, c2 - cst[h2:h2 + 1, :], NEG))
                yd = jnp.where(lo, _dot(m1, xdt, NN), _dot(m2, xdt, NN))
                hp = h_ref[pj]
                yo = _dot(cg, hp, NT) * e_l
                st = _dot(xdt * dte_l, bg, TN)
                h_ref[pj] = hp * jnp.where(r < HD, cd1, cd2) + st
                y_ref[:, sl] = yd + yo + dsk_ref[:, sl] * xp
        yn, _, _ = _gate_norm(y_ref[...], z_ref[...], w_ref[...])
        yn_ref[...] = (yn * w_ref[...]).astype(BF16)

    return pl.pallas_call(
        body, name="ssd_fwd", grid=(NCH,),
        in_specs=[pl.BlockSpec((128, SSM_W), lambda i: (i, 0)),
                  pl.BlockSpec((128, 256), lambda i: (i, 4)), pl.BlockSpec((128, 256), lambda i: (i, 5)),
                  pl.BlockSpec((128, 128), lambda i: (i, COL_DT // 128)),
                  pl.BlockSpec((128, SSM_W), lambda i: (i, 3)),
                  pl.BlockSpec((1, 128), lambda i: (0, 0)), pl.BlockSpec((1, 128), lambda i: (0, 0)),
                  pl.BlockSpec((1, SSM_W), lambda i: (0, 0)), pl.BlockSpec((1, SSM_W), lambda i: (0, 0))],
        out_specs=[pl.BlockSpec((128, SSM_W), lambda i: (i, 0)), pl.BlockSpec((128, SSM_W), lambda i: (i, 0)),
                   pl.BlockSpec((None, NPAIR, 128, 128), lambda i: (i, 0, 0, 0))],
        out_shape=[jax.ShapeDtypeStruct((S, SSM_W), F32), jax.ShapeDtypeStruct((S, SSM_W), BF16),
                   jax.ShapeDtypeStruct((NCH, NPAIR, 128, 128), F32)],
        scratch_shapes=[pltpu.VMEM((NPAIR, 128, 128), F32)],
        compiler_params=_cparams("arbitrary"),
    )(xbc, xbc, xbc, proj, proj, dtb, alog, dskip_l, ssmw)


def _ssd_bwd(dmixed, y, xbc, proj, hprev, dtb, alog, dskip_l, ssmw, rider=None):
    def body(dyn_ref, y_ref, x_ref, b_ref, c_ref, dtr_ref, z_ref, hp_ref, dtb_ref, alog_ref, dsk_ref, w_ref,
             dxbc_ref, dz_ref, ddt_ref, dw_ref, dsc_ref, g_ref):
        @pl.when(pl.program_id(0) == 0)
        def _():
            g_ref[...] = jnp.zeros_like(g_ref)
            dsc_ref[...] = jnp.zeros_like(dsc_ref)

        z, dt, a, cs, r, c = _ssd_scalars(dtr_ref, dtb_ref, alog_ref)
        cst = cs.T
        causal = r >= c
        lo = c < HD

        yv = y_ref[...]
        zv = z_ref[...]
        wv = w_ref[...]
        ygn, rs, yg = _gate_norm(yv, zv, wv)
        dyn = dyn_ref[...]
        _acc_rows(dw_ref, dyn * ygn)
        dynw = dyn * wv
        parts = []
        for g in range(2):
            sl = slice(512 * g, 512 * (g + 1))
            a_g, n_g = dynw[:, sl], ygn[:, sl]
            parts.append(rs[g] * (a_g - n_g * jnp.mean(a_g * n_g, axis=-1, keepdims=True)))
        dyg = jnp.concatenate(parts, axis=1)
        sz = _sigmoid(zv)
        dz_ref[...] = (dyg * yv * (sz * (1.0 + zv * (1.0 - sz)))).astype(BF16)
        dy_all = dyg * (zv * sz)

        dcs_cols = jnp.zeros((128, 128), F32)
        dcs_rows = jnp.zeros((128, 128), F32)
        ddt_x = jnp.zeros((128, 128), F32)
        dd_row = jnp.zeros((1, 128), F32)
        last = r == 127
        x_all, b_all, c_all, dsk_all = x_ref[...], b_ref[...], c_ref[...], dsk_ref[...]
        hp_all, g_all = hp_ref[...], g_ref[...]
        g_new, dx_parts, db_parts, dc_parts = [], [], [], []
        for g in range(2):
            bg = b_all[:, 128 * g:128 * (g + 1)]
            cg = c_all[:, 128 * g:128 * (g + 1)]
            cb = _dot(cg, bg, NT)
            dcb = jnp.zeros((128, 128), F32)
            db_acc = jnp.zeros((128, NST), F32)
            dc_acc = jnp.zeros((128, NST), F32)
            for j in range(4):
                pj = 4 * g + j
                h1, h2 = 2 * pj, 2 * pj + 1
                sl = slice(128 * pj, 128 * (pj + 1))
                xp = x_all[:, sl]
                dyp = dy_all[:, sl]
                c1, c2, cd1, cd2, e_l, dte1, dte2, dte_l, dt_l = _pair_terms(cs, cst, dt, h1, h2, lo)
                xdt = xp * dt_l
                hp = hp_all[pj]
                gp = g_all[pj]
                dxp = dsk_all[:, sl] * dyp
                dyx = dyp * xp
                dd_row = dd_row + jnp.where(c[0:1, :] == h1, jnp.sum(jnp.where(lo, dyx, 0.0), keepdims=True), 0.0) \
                    + jnp.where(c[0:1, :] == h2, jnp.sum(jnp.where(lo, 0.0, dyx), keepdims=True), 0.0)
                dzs = dyp * e_l
                dc_acc = dc_acc + _dot(dzs, hp, NN)
                g_from = _dot(dzs, cg, TN)
                ryo = dyp * (_dot(cg, hp, NT) * e_l)
                k1 = jnp.sum(jnp.where(lo, ryo, 0.0), axis=1, keepdims=True)
                k2 = jnp.sum(jnp.where(lo, 0.0, ryo), axis=1, keepdims=True)
                qm = _dot(bg, gp, NT)
                dxdt = qm * dte_l
                qx = qm * xdt
                t1 = jnp.sum(jnp.where(lo, qx, 0.0), axis=1, keepdims=True) * dte1
                t2 = jnp.sum(jnp.where(lo, 0.0, qx), axis=1, keepdims=True) * dte2
                db_acc = db_acc + _dot(xdt * dte_l, gp, NN)
                gh = gp * hp
                dl1 = jnp.sum(t1, keepdims=True) + jnp.sum(jnp.where(r < HD, gh, 0.0), keepdims=True) * cd1
                dl2 = jnp.sum(t2, keepdims=True) + jnp.sum(jnp.where(r < HD, 0.0, gh), keepdims=True) * cd2
                g_new.append(g_from + jnp.where(r < HD, cd1, cd2) * gp)
                k1 = k1 - t1 + jnp.where(last[:, 0:1], dl1, 0.0)
                k2 = k2 - t2 + jnp.where(last[:, 0:1], dl2, 0.0)
                for hh, ch, msk in ((h1, c1, lo), (h2, c2, jnp.logical_not(lo))):
                    lm = jnp.exp(jnp.where(causal, ch - cst[hh:hh + 1, :], NEG))
                    mm = cb * lm
                    dm = jnp.where(causal, _dot(jnp.where(msk, dyp, 0.0), xdt, NT), 0.0)
                    w = dm * mm
                    kk = jnp.sum(w, axis=1, keepdims=True)
                    if hh == h1:
                        k1 = k1 + kk
                    else:
                        k2 = k2 + kk
                    dcs_rows = dcs_rows + jnp.where(r == hh, jnp.sum(w, axis=0, keepdims=True), 0.0)
                    dcb = dcb + dm * lm
                    dxdt = dxdt + jnp.where(msk, _dot(mm, dyp, TN), 0.0)
                dcs_cols = dcs_cols + jnp.where(c == h1, k1, 0.0) + jnp.where(c == h2, k2, 0.0)
                dxx = dxdt * xp
                ddt_x = ddt_x + jnp.where(c == h1, jnp.sum(jnp.where(lo, dxx, 0.0), axis=1, keepdims=True), 0.0) \
                    + jnp.where(c == h2, jnp.sum(jnp.where(lo, 0.0, dxx), axis=1, keepdims=True), 0.0)
                dx_parts.append(dxp + dxdt * dt_l)
            db_parts.append(db_acc + _dot(dcb, cg, TN))
            dc_parts.append(dc_acc + _dot(dcb, bg, NN))
        g_ref[...] = jnp.stack(g_new)
        dxbc_ref[...] = jnp.concatenate(dx_parts + db_parts + dc_parts, axis=1)

        dcs = dcs_cols - dcs_rows.T
        dad = _dot_exact((c >= r).astype(F32), dcs)
        ddt = dad * a + ddt_x
        ddtr = jnp.where(c < 16, ddt * _sigmoid(z), 0.0)
        ddt_ref[...] = ddtr.astype(BF16)
        r8 = lax.broadcasted_iota(jnp.int32, (8, 128), 0)
        dsc_ref[...] += (jnp.where(r8 == 0, jnp.sum(ddtr, axis=0, keepdims=True), 0.0)
                         + jnp.where(r8 == 1, jnp.sum(dad * dt, axis=0, keepdims=True) * a, 0.0)
                         + jnp.where(r8 == 2, dd_row, 0.0))

    rev = NCH - 1
    return _call(
        body, "ssd_bwd", (NCH,),
        [pl.BlockSpec((128, SSM_W), lambda i: (rev - i, 0)),
         pl.BlockSpec((128, SSM_W), lambda i: (rev - i, 0)),
         pl.BlockSpec((128, SSM_W), lambda i: (rev - i, 0)),
         pl.BlockSpec((128, 256), lambda i: (rev - i, 4)), pl.BlockSpec((128, 256), lambda i: (rev - i, 5)),
         pl.BlockSpec((128, 128), lambda i: (rev - i, COL_DT // 128)),
         pl.BlockSpec((128, SSM_W), lambda i: (rev - i, 3)),
         pl.BlockSpec((None, NPAIR, 128, 128), lambda i: (rev - i, 0, 0, 0)),
         pl.BlockSpec((1, 128), lambda i: (0, 0)), pl.BlockSpec((1, 128), lambda i: (0, 0)),
         pl.BlockSpec((1, SSM_W), lambda i: (0, 0)), pl.BlockSpec((1, SSM_W), lambda i: (0, 0))],
        [pl.BlockSpec((128, CONV_C), lambda i: (rev - i, 0)),
         pl.BlockSpec((128, SSM_W), lambda i: (rev - i, 3)),
         pl.BlockSpec((128, 128), lambda i: (rev - i, 0)),
         pl.BlockSpec((1, SSM_W), lambda i: (0, 0)), pl.BlockSpec((8, 128), lambda i: (0, 0))],
        [jax.ShapeDtypeStruct((S, CONV_C), F32), jax.ShapeDtypeStruct((S, WIN_PAD), BF16),
         jax.ShapeDtypeStruct((S, 128), BF16), jax.ShapeDtypeStruct((1, SSM_W), F32),
         jax.ShapeDtypeStruct((8, 128), F32)],
        (dmixed, y, xbc, xbc, xbc, proj, proj, hprev, dtb, alog, dskip_l, ssmw),
        [pltpu.VMEM((NPAIR, 128, 128), F32)], ("arbitrary",), rider)


def _cast_stack(name, slot, arrs, tr, tc):
    n = len(arrs)
    rows, cols = arrs[0].shape

    def body(s_ref, *refs):
        for i in range(n):
            refs[n][i] = refs[i][...].astype(BF16)

    return pl.pallas_call(
        body, name=name,
        grid_spec=pltpu.PrefetchScalarGridSpec(
            num_scalar_prefetch=1, grid=(rows // tr, cols // tc),
            in_specs=[pl.BlockSpec((tr, tc), lambda i, j, sr: (i, j))] * n,
            out_specs=pl.BlockSpec((None, n, tr, tc), lambda i, j, sr: (sr[0], 0, i, j))),
        out_shape=jax.ShapeDtypeStruct((NSH, n, rows, cols), BF16),
        compiler_params=_cparams("parallel", "parallel"),
    )(slot, *arrs)


def _pair_sum(name, c_idx, ps, th):
    n = len(ps)
    _, rows, _ = ps[0].shape

    def body(c_ref, *refs):
        mine, whole, out, theirs = refs[:n], refs[n:2 * n], refs[2 * n:3 * n], refs[3 * n:4 * n]
        send, recv = refs[4 * n], refs[4 * n + 1]
        s, i = pl.program_id(0), pl.program_id(1)

        @pl.when((s == 0) & (i == 0))
        def _():
            x, y, c, _ = _place()
            cps = [_rcopy(whole[k].at[:, :, pl.ds((1 - c) * HALF, HALF)], theirs[k], send.at[k], recv.at[k],
                          (x, y, 1 - c)) for k in range(n)]
            for cp in cps:
                cp.start()
            for cp in cps:
                cp.wait()

        rows_i = slice(None) if th == rows else pl.ds(pl.multiple_of(i * th, th), th)
        for k in range(n):
            out[k][...] = (mine[k][...].astype(F32) + theirs[k][s, rows_i, :].astype(F32)).astype(BF16)

    spec = pl.BlockSpec((None, th, HALF), lambda s, i, cr: (s, i, 0))
    return pl.pallas_call(
        body, name=name,
        grid_spec=pltpu.PrefetchScalarGridSpec(
            num_scalar_prefetch=1, grid=(NSH, rows // th),
            in_specs=[pl.BlockSpec((None, th, HALF), lambda s, i, cr: (s, i, cr[0]))] * n + _any_specs(n),
            out_specs=[spec] * n,
            scratch_shapes=[pltpu.VMEM((NSH, rows, HALF), BF16)] * n
            + [pltpu.SemaphoreType.DMA((n,)), pltpu.SemaphoreType.DMA((n,))]),
        out_shape=[jax.ShapeDtypeStruct((NSH, rows, HALF), BF16)] * n,
        compiler_params=_cparams("arbitrary", "arbitrary"),
    )(c_idx, *ps, *ps)


def _chip_sum(name, place, cs, ts, th):
    n = len(ts)
    _, rows, _ = ts[0].shape

    def body(p_ref, *refs):
        for i in range(n):
            t = refs[n + i][...].astype(F32)
            refs[2 * n + i][...] = ((refs[i][...].astype(F32) + t[0]) + t[1]) + t[2]

    return pl.pallas_call(
        body, name=name,
        grid_spec=pltpu.PrefetchScalarGridSpec(
            num_scalar_prefetch=1, grid=(rows // th,),
            in_specs=[pl.BlockSpec((None, th, HALF), lambda i, pr: (pr[0], i, 0))] * n
            + [pl.BlockSpec((3, th, HALF), lambda i, pr: (0, i, 0))] * n,
            out_specs=[pl.BlockSpec((th, HALF), lambda i, pr: (i, pr[1]))] * n),
        out_shape=[jax.ShapeDtypeStruct((rows, D), F32)] * n, compiler_params=_cparams("parallel"),
    )(place, *cs, *ts)


def _adamw(name, ws, gs, ms, vs, tr, tc):
    n = len(ws)
    shape = ws[0].shape
    rows, cols, mid = shape[0], shape[-1], shape[1:-1]
    c1 = 1.0 / (1.0 - ADAM_B1 ** ADAM_STEP)
    c2 = 1.0 / (1.0 - ADAM_B2 ** ADAM_STEP)

    def body(*refs):
        for i in range(n):
            w, g, m, v = (refs[k * n + i][...] for k in range(4))
            m2 = ADAM_B1 * m + (1.0 - ADAM_B1) * g
            v2 = ADAM_B2 * v + (1.0 - ADAM_B2) * (g * g)
            refs[4 * n + 3 * i][...] = -ADAM_LR * ((m2 * c1) / (jnp.sqrt(v2 * c2) + ADAM_EPS) + ADAM_WD * w)
            refs[4 * n + 3 * i + 1][...] = m2
            refs[4 * n + 3 * i + 2][...] = v2

    spec = pl.BlockSpec((tr,) + mid + (tc,), lambda i, j: (i,) + (0,) * len(mid) + (j,))
    outs = pl.pallas_call(
        body, name=name, grid=(rows // tr, cols // tc), in_specs=[spec] * (4 * n), out_specs=[spec] * (3 * n),
        out_shape=[jax.ShapeDtypeStruct(shape, F32)] * (3 * n),
        compiler_params=_cparams("parallel", "parallel"),
    )(*ws, *gs, *ms, *vs)
    return [tuple(outs[3 * i:3 * i + 3]) for i in range(n)]


def _place():
    x, y, c = lax.axis_index("x"), lax.axis_index("y"), lax.axis_index("c")
    chips = [(1 - x, y), (x, 1 - y), (1 - x, 1 - y)]
    return x, y, c, chips


def _any_specs(n):
    return [pl.BlockSpec(memory_space=pl.ANY)] * n


def _rcopy(src, dst, send_sem, recv_sem, dev):
    return pltpu.make_async_remote_copy(src_ref=src, dst_ref=dst, send_sem=send_sem, recv_sem=recv_sem,
                                        device_id=dev, device_id_type=MESH)


def _gather_rider(bufs, views):
    n = len(bufs)

    def start(rin, rout, sems):
        send, recv = sems[0], sems[1]
        x, y, c, chips = _place()
        for j, chip in enumerate(chips):
            for b in range(n):
                mine = views[b](rout[b], 2 * x + y, c)
                _rcopy(mine, mine, send.at[j * n + b], recv.at[j * n + b], (chip[0], chip[1], c)).start()

    def finish(rin, rout, sems):
        send, recv, fsend, frecv = sems
        x, y, c, chips = _place()
        passed = []
        for j, chip in enumerate(chips):
            for b in range(n):
                landed = views[b](rout[b], 2 * chip[0] + chip[1], c)
                _rcopy(landed, landed, send.at[j * n + b], recv.at[j * n + b], (x, y, c)).wait_recv()
                fw = _rcopy(landed, landed, fsend.at[j * n + b], frecv.at[j * n + b], (x, y, 1 - c))
                fw.start()
                passed.append(fw)
        for j, chip in enumerate(chips):
            for b in range(n):
                other = views[b](rout[b], 2 * chip[0] + chip[1], 1 - c)
                _rcopy(other, other, fsend.at[j * n + b], frecv.at[j * n + b], (x, y, c)).wait_recv()
        for j, chip in enumerate(chips):
            for b in range(n):
                mine = views[b](rout[b], 2 * x + y, c)
                _rcopy(mine, mine, send.at[j * n + b], recv.at[j * n + b], (x, y, c)).wait_send()
        for fw in passed:
            fw.wait_send()

    return _Rider(list(bufs), [jax.ShapeDtypeStruct(a.shape, a.dtype) for a in bufs], {b: b for b in range(n)},
                  [pltpu.SemaphoreType.DMA((3 * n,))] * 4, start, finish)


def _small_gather_rider(cw):
    def descs(rin, rout, sems, x, y, c, chips):
        return [_rcopy(rin[0], rout[0].at[2 * x + y], sems[1].at[j], sems[2].at[j], (chip[0], chip[1], c))
                for j, chip in enumerate(chips)]

    def start(rin, rout, sems):
        x, y, c, chips = _place()
        pltpu.make_async_copy(rin[0], rout[0].at[2 * x + y], sems[0].at[0]).start()
        for cp in descs(rin, rout, sems, x, y, c, chips):
            cp.start()

    def finish(rin, rout, sems):
        x, y, c, chips = _place()
        for j, chip in enumerate(chips):
            _rcopy(rin[0], rout[0].at[2 * chip[0] + chip[1]], sems[1].at[j], sems[2].at[j], (x, y, c)).wait_recv()
        for cp in descs(rin, rout, sems, x, y, c, chips):
            cp.wait_send()
        pltpu.make_async_copy(rin[0], rout[0].at[2 * x + y], sems[0].at[0]).wait()

    return _Rider([cw], [jax.ShapeDtypeStruct((NSH,) + cw.shape, cw.dtype)], {},
                  [pltpu.SemaphoreType.DMA((1,)), pltpu.SemaphoreType.DMA((3,)), pltpu.SemaphoreType.DMA((3,))],
                  start, finish)


def _to_chips_rider(cs):
    n = len(cs)

    def descs(rin, rout, sems):
        x, y, c, chips = _place()
        return [_rcopy(rin[i].at[2 * chip[0] + chip[1]], rout[i].at[j], sems[0].at[j * n + i], sems[1].at[j * n + i],
                       (chip[0], chip[1], c)) for j, chip in enumerate(chips) for i in range(n)]

    def start(rin, rout, sems):
        for cp in descs(rin, rout, sems):
            cp.start()

    def finish(rin, rout, sems):
        for cp in descs(rin, rout, sems):
            cp.wait()

    return _Rider(list(cs), [jax.ShapeDtypeStruct((3,) + a.shape[1:], a.dtype) for a in cs], {},
                  [pltpu.SemaphoreType.DMA((3 * n,))] * 2, start, finish)


def _run_riders(name, riders):
    n_in = [len(r.operands) for r in riders]
    n_out = [len(r.out_shapes) for r in riders]
    n_sem = [len(r.sems) for r in riders]

    def body(*refs):
        parts, at = [], 0
        for counts in (n_in, n_out, n_sem):
            group = []
            for k in counts:
                group.append(refs[at:at + k])
                at += k
            parts.append(group)
        for i, r in enumerate(riders):
            r.start(parts[0][i], parts[1][i], parts[2][i])
        for i, r in enumerate(riders):
            r.finish(parts[0][i], parts[1][i], parts[2][i])

    aliases = {}
    for i, r in enumerate(riders):
        for k, v in r.aliases.items():
            aliases[sum(n_in[:i]) + k] = sum(n_out[:i]) + v
    res = pl.pallas_call(
        body, name=name, in_specs=_any_specs(sum(n_in)), out_specs=_any_specs(sum(n_out)),
        out_shape=[s for r in riders for s in r.out_shapes], input_output_aliases=aliases,
        scratch_shapes=[s for r in riders for s in r.sems],
    )(*[a for r in riders for a in r.operands])
    out, at = [], 0
    for k in n_out:
        out.append(list(res[at:at + k]))
        at += k
    return out


def _swap_halves(gs):
    n = len(gs)

    def body(*refs):
        dst, send, recv = refs[n:2 * n], refs[2 * n], refs[2 * n + 1]
        x, y, c, _ = _place()
        cps = []
        for i in range(n):
            mine = dst[i].at[:, pl.ds(c * HALF, HALF)]
            cps.append(pltpu.make_async_remote_copy(
                src_ref=mine, dst_ref=mine, send_sem=send.at[i], recv_sem=recv.at[i],
                device_id=(x, y, 1 - c), device_id_type=MESH))
        for cp in cps:
            cp.start()
        for i in range(n):
            other = dst[i].at[:, pl.ds((1 - c) * HALF, HALF)]
            pltpu.make_async_remote_copy(
                src_ref=other, dst_ref=other, send_sem=send.at[i], recv_sem=recv.at[i],
                device_id=(x, y, c), device_id_type=MESH).wait_recv()
        for cp in cps:
            cp.wait_send()

    return pl.pallas_call(
        body, name="grads_swap_halves", in_specs=_any_specs(n), out_specs=_any_specs(n),
        out_shape=[jax.ShapeDtypeStruct(g.shape, g.dtype) for g in gs],
        input_output_aliases={i: i for i in range(n)},
        scratch_shapes=[pltpu.SemaphoreType.DMA((n,)), pltpu.SemaphoreType.DMA((n,))],
    )(*gs)


SMALL_ROWS = 16


def _allreduce_small(vec):
    def body(v_ref, o_ref, buf, send, recv):
        x, y, c, _ = _place()
        me = 4 * x + 2 * y + c
        buf[me] = v_ref[...]
        cps = []
        for k in range(1, 8):
            peer = (x ^ (k >> 2), y ^ ((k >> 1) & 1), c ^ (k & 1))
            cps.append(pltpu.make_async_remote_copy(
                src_ref=v_ref, dst_ref=buf.at[me], send_sem=send.at[k - 1], recv_sem=recv.at[k - 1],
                device_id=peer, device_id_type=MESH))
        for cp in cps:
            cp.start()
        for k in range(1, 8):
            pltpu.make_async_remote_copy(
                src_ref=v_ref, dst_ref=buf.at[me ^ k], send_sem=send.at[k - 1], recv_sem=recv.at[k - 1],
                device_id=(x, y, c), device_id_type=MESH).wait_recv()
        for cp in cps:
            cp.wait_send()
        t = buf[0]
        for d in range(1, 8):
            t = t + buf[d]
        o_ref[...] = t

    return pl.pallas_call(
        body, name="allreduce_small",
        in_specs=[pl.BlockSpec(memory_space=pltpu.VMEM)], out_specs=pl.BlockSpec(memory_space=pltpu.VMEM),
        out_shape=jax.ShapeDtypeStruct((SMALL_ROWS, D), F32),
        scratch_shapes=[pltpu.VMEM((8, SMALL_ROWS, D), F32), pltpu.SemaphoreType.DMA((7,)),
                        pltpu.SemaphoreType.DMA((7,))],
    )(vec)


def _col_half(ref, slot, hc):
    return ref.at[slot, :, pl.ds(hc * HALF, HALF)]


def _stack_half(ref, slot, hc):
    return ref.at[slot, :, :, pl.ds(hc * HALF, HALF)]


def _row_tile(rows):
    for t in range(512, 15, -16):
        if rows % t == 0:
            return t
    return rows


def _same_shape_runs(arrs):
    runs, a = [], 0
    for b in range(1, len(arrs) + 1):
        if b == len(arrs) or arrs[b].shape != arrs[a].shape:
            runs.append((a, b))
            a = b
    return runs


class _Comm:
    def __init__(self):
        x, y, c = lax.axis_index("x"), lax.axis_index("y"), lax.axis_index("c")
        self.c_idx = jnp.reshape(c, (1,)).astype(jnp.int32)
        self.place = jnp.stack([2 * x + y, c]).astype(jnp.int32)
        self.groups = {}

    @staticmethod
    def gather(*bufs):
        return _gather_rider(list(bufs), [_col_half if b.ndim == 3 else _stack_half for b in bufs])

    def reduce_rider(self, tag, names, ps):
        csums = []
        for a, b in _same_shape_runs(ps):
            csums += _pair_sum("pair_sum_%s%d" % (tag, a), self.c_idx, ps[a:b], _row_tile(ps[a].shape[1]))
        self.groups[tag] = [names, csums, None]
        return _to_chips_rider(csums)

    def landed(self, tag, ts):
        self.groups[tag][2] = ts

    def finish(self):
        names, halves = [], []
        for tag, (group_names, csums, ts) in self.groups.items():
            names += group_names
            for a, b in _same_shape_runs(csums):
                halves += _chip_sum("chip_sum_%s%d" % (tag, a), self.place, csums[a:b], ts[a:b],
                                    _row_tile(csums[a].shape[1]))
        return dict(zip(names, _swap_halves(halves)))


ROPE_THETA = 10000.0
SMALL_1K = ("ffn1_pre_norm", "ffn1_post_norm", "mix_pre_norm", "ssm_norm", "mix_post_norm",
            "ffn2_pre_norm", "ffn2_post_norm")
SMALL_16 = ("dt_bias", "a_log", "d_skip")
OFF_CONVB = 7 * D
OFF_16 = OFF_CONVB + CONV_C
OFF_CONVW = OFF_16 + 48
OFF_LOSS = OFF_CONVW + CONV_K * CONV_C
SMALL_LEN = SMALL_ROWS * D


def _sds(shape, dtype):
    return jax.ShapeDtypeStruct(shape, dtype)


def _ridden(res, rider):
    return res if rider is not None else (res, None)


def _ffn_down(name, act, w, rider=None):
    return _mm(name, [act, w.dn], NN, (S // TS,),
               [pl.BlockSpec((NSH, TS, FS), lambda i: (0, i, 0)),
                pl.BlockSpec((NSH, None, FS, D), lambda i: (0, w.d0, 0, 0))],
               pl.BlockSpec((TS, D), lambda i: (i, 0)), _sds((S, D), F32), rider)


def _ffn_dw(name, a, b, rider=None):
    return _mm(name, [a, b], TN, (NSH,),
               [pl.BlockSpec((None, S, FS), lambda s: (s, 0, 0)), pl.BlockSpec((S, D), lambda s: (0, 0))],
               pl.BlockSpec((None, FS, D), lambda s: (s, 0, 0)), _sds((NSH, FS, D), BF16), rider)


def _ffn_dn(name, dgate, dup, w, rider=None):
    a2 = pl.BlockSpec((NSH, TS, FS), lambda i: (0, i, 0))
    return _mm(name, [dgate, w.gu, dup, w.gu], NN, (S // TS,),
               [a2, pl.BlockSpec((NSH, None, FS, D), lambda i: (0, w.g0, 0, 0)),
                a2, pl.BlockSpec((NSH, None, FS, D), lambda i: (0, w.g0 + 1, 0, 0))],
               pl.BlockSpec((TS, D), lambda i: (i, 0)), _sds((S, D), F32), rider)


def _out_proj_dx(dh, wout):
    def body(dh_ref, w_ref, dyn_ref, do_ref):
        dm = _dot(dh_ref[...], w_ref[...], NT)
        dyn_ref[...] = dm[:, D:]
        for b in range(TS // 128):
            for j, blk in enumerate(_rows_to_blocks(dm[128 * b:128 * (b + 1), :D])):
                do_ref[j, b] = blk.astype(BF16)

    return pl.pallas_call(
        body, name="out_proj_dx", grid=(S // TS,),
        in_specs=[pl.BlockSpec((TS, D), lambda i: (i, 0)), pl.BlockSpec((2 * D, D), lambda i: (0, 0))],
        out_specs=[pl.BlockSpec((TS, D), lambda i: (i, 0)),
                   pl.BlockSpec((NKV, TS // 128, HD, QROWS), lambda i: (0, i, 0, 0))],
        out_shape=[_sds((S, D), F32), _sds((NKV, NCH, HD, QROWS), BF16)], compiler_params=_cparams("parallel"),
    )(dh, wout)


def _heads(t, n):
    return t.reshape(S, n, HD).transpose(1, 0, 2)


def _unheads(t):
    return t.transpose(1, 0, 2).reshape(S, t.shape[0] * HD)


def _heads_t(t, n):
    return t.reshape(S, n, HD).transpose(1, 2, 0)


def _pad128(v):
    return jnp.pad(v, ((0, 0), (0, 128 - v.shape[1])))


def _local_step(x, positions, tgt, sp, gu1, d1, f2, wint, wout, convw, comm=None):
    inv_freq = ROPE_THETA ** (-jnp.arange(0, HD, 2, dtype=F32) / HD)
    ang = positions.astype(F32)[:, None] * inv_freq
    ang = jnp.concatenate([ang, ang, ang, ang], axis=-1)
    cos, sin = jnp.cos(ang), jnp.sin(ang)
    dtb, alog = _pad128(sp["dt_bias"]), _pad128(sp["a_log"])
    dskip_l = jnp.repeat(sp["d_skip"], HD, axis=1)
    convb = sp["conv_b"]

    n1 = _prenorm("prenorm1", x, sp["ffn1_pre_norm"])
    rider = comm.gather(d1) if comm else None
    (fg1, fu1, act1), got = _ridden(_ffn_up("ffn1_up", n1, _FfnW(gu1, 0, d1, 0), rider), rider)
    if comm:
        d1, = got
    w1 = _FfnW(gu1, 0, d1, 0)
    rider = comm.gather(wint) if comm else None
    h1, got = _ridden(_ffn_down("ffn1_down", act1, w1, rider), rider)
    if comm:
        wint, = got
    wint_pad = jnp.pad(wint.reshape(WIN_COLS, D), ((0, WIN_PAD - WIN_COLS), (0, 0)))
    x1, n2 = _postres("postres1", x, h1, sp["ffn1_post_norm"], 0.5, sp["mix_pre_norm"])

    pw = WIN_PAD // 3
    proj = _mm("in_proj", [n2, wint_pad], NT, (S // TS, 3),
               [pl.BlockSpec((TS, D), lambda i, j: (i, 0)), pl.BlockSpec((pw, D), lambda i, j: (j, 0))],
               pl.BlockSpec((TS, pw), lambda i, j: (i, j)), _sds((S, WIN_PAD), F32))
    qt = _rope_q(proj, cos, sin)
    k_rot = _rope("rope_k", proj, D // KVW, KVW, cos, sin, 1.0, 1.0)
    v_bf = proj[:, D + KVW:D + 2 * KVW].astype(BF16)
    kh, vh = _heads(k_rot, NKV), _heads(v_bf, NKV)
    kt, vt = _heads_t(k_rot, NKV), _heads_t(v_bf, NKV)
    bias = _bias_table()
    rider = comm.gather(f2, wout) if comm else None
    (ot, lse, attn), got = _ridden(_attn_fwd(qt, kh, vt, bias, rider), rider)
    if comm:
        f2, wout = got
    w2 = _FfnW(f2, 0, f2, 2)
    wout = wout.reshape(2 * D, D)
    xbc = _conv_fwd(proj, convw, convb)
    y, yn, hprev = _ssd_fwd(xbc, proj, dtb, alog, dskip_l, sp["ssm_norm"])
    mixed = jnp.concatenate([attn, yn], axis=1)
    h2 = _mm("out_proj", [mixed, wout], NN, (S // TS,),
             [pl.BlockSpec((TS, 2 * D), lambda i: (i, 0)), pl.BlockSpec((2 * D, D), lambda i: (0, 0))],
             pl.BlockSpec((TS, D), lambda i: (i, 0)), _sds((S, D), F32))
    x2, n3 = _postres("postres2", x1, h2, sp["mix_post_norm"], 1.0, sp["ffn2_pre_norm"])

    fg2, fu2, act2 = _ffn_up("ffn2_up", n3, w2)
    h3 = _ffn_down("ffn2_down", act2, w2)
    dy, dh3, dp3, loss = _final(x2, h3, sp["ffn2_post_norm"], tgt, 0.5)

    dgate2, dup2 = _ffn_dact("ffn2_dact", dh3, w2, fg2, fu2)
    dws2 = [_ffn_dw("ffn2_dwg", dgate2, n3), _ffn_dw("ffn2_dwu", dup2, n3), _ffn_dw("ffn2_dwd", act2, dh3)]
    dn3 = _ffn_dn("ffn2_dn", dgate2, dup2, w2)
    dx2, dh2, dg3, dp2 = _mid_bwd("mid_bwd2", dy, dn3, x2, sp["ffn2_pre_norm"], h2, sp["mix_post_norm"], 1.0)

    dyn, dot_ = _out_proj_dx(dh2, wout)
    dwout = _mm("out_proj_dw", [mixed, dh2], TN, (2,),
                [pl.BlockSpec((S, D), lambda m: (0, m)), pl.BlockSpec((S, D), lambda m: (0, 0))],
                pl.BlockSpec((D, D), lambda m: (m, 0)), _sds((2 * D, D), BF16))
    dwout = dwout.reshape(NSH, 2 * D // NSH, D)
    rider = comm.reduce_rider("a", BIG[3:6] + ("w_out",), dws2 + [dwout]) if comm else None
    (dxbc, dproj, ddt, dssm, dsc), got = _ridden(
        _ssd_bwd(dyn, y, xbc, proj, hprev, dtb, alog, dskip_l, sp["ssm_norm"], rider), rider)
    if comm:
        comm.landed("a", got)
    dproj, dcw8, dcb = _conv_bwd(dxbc, proj, convw, convb, dproj)
    dqt, dkh, dvh = _attn_bwd(qt, kh, kt, vh, dot_, lse, _attn_delta(ot, dot_), bias)
    dproj = _rope_dq(dqt, cos, sin, dproj)
    dproj = _rope("rope_dk", _unheads(dkh), 0, KVW, cos, sin, -1.0, 1.0, into=(dproj, D // KVW))
    dproj = lax.dynamic_update_slice(dproj, _unheads(dvh).astype(BF16), (0, D + KVW))
    dproj = lax.dynamic_update_slice(dproj, ddt, (0, COL_DT))
    dwint = _mm("in_proj_dw", [dproj, n2], TN, (3,),
                [pl.BlockSpec((S, pw), lambda j: (0, j)), pl.BlockSpec((S, D), lambda j: (0, 0))],
                pl.BlockSpec((pw, D), lambda j: (j, 0)), _sds((WIN_PAD, D), BF16))
    dwint = dwint[:WIN_COLS].reshape(NSH, WIN_SH, D)

    def riding(tag, names, ps, call):
        rider = comm.reduce_rider(tag, names, ps) if comm else None
        res, got = _ridden(call(rider), rider)
        if comm:
            comm.landed(tag, got)
        return res

    dn2 = riding("b", ("w_in",), [dwint], lambda rider: _mm(
        "in_proj_dx", [dproj, wint_pad], NN, (S // TS,),
        [pl.BlockSpec((TS, WIN_PAD), lambda i: (i, 0)), pl.BlockSpec((WIN_PAD, D), lambda i: (0, 0))],
        pl.BlockSpec((TS, D), lambda i: (i, 0)), _sds((S, D), F32), rider))
    dx1, dh1, dg2, dp1 = _mid_bwd("mid_bwd1", dx2, dn2, x1, sp["mix_pre_norm"], h1, sp["ffn1_post_norm"], 0.5)

    dwd1 = _ffn_dw("ffn1_dwd", act1, dh1)
    dgate1, dup1 = riding("d", BIG[2:3], [dwd1], lambda rider: _ffn_dact("ffn1_dact", dh1, w1, fg1, fu1, rider))
    dwg1, dwu1 = _ffn_dw("ffn1_dwg", dgate1, n1), _ffn_dw("ffn1_dwu", dup1, n1)
    dn1 = riding("g", BIG[0:2], [dwg1, dwu1], lambda rider: _ffn_dn("ffn1_dn", dgate1, dup1, w1, rider))
    dws1 = [dwg1, dwu1, dwd1]
    grad_x, dg1 = _first_bwd(dx1, dn1, x, sp["ffn1_pre_norm"])

    small = jnp.concatenate([
        dg1[0], dp1[0], dg2[0], dssm[0], dp2[0], dg3[0], dp3[0], dcb[0],
        dsc[0, :16], dsc[1, :16], dsc[2, :16], dcw8[:CONV_K].reshape(-1), loss[0, :1]])
    small = jnp.pad(small, (0, SMALL_LEN - small.shape[0])).reshape(SMALL_ROWS, D)
    if comm is None:
        return grad_x, dws1 + dws2 + [dwint, dwout], small
    return grad_x, comm.finish(), small


WEIGHTS = ("ffn1_pre_norm", "ffn1_w_gate", "ffn1_w_up", "ffn1_w_down", "ffn1_post_norm", "mix_pre_norm", "w_in",
           "conv_w", "conv_b", "dt_bias", "a_log", "d_skip", "ssm_norm", "w_out", "mix_post_norm", "ffn2_pre_norm",
           "ffn2_w_gate", "ffn2_w_up", "ffn2_w_down", "ffn2_post_norm")
BIG = ("ffn1_w_gate", "ffn1_w_up", "ffn1_w_down", "ffn2_w_gate", "ffn2_w_up", "ffn2_w_down", "w_in", "w_out")
TRANSPOSED = ("ffn1_w_gate", "ffn1_w_up", "ffn2_w_gate", "ffn2_w_up", "w_in")
SMALL_ORDER = SMALL_1K + ("conv_b",) + SMALL_16
CONVW_SH = CONV_C // NSH


def _shard2d(t, name):
    return t[0].T if name in TRANSPOSED else t[0]


def _unshard2d(t, name):
    return (t.T if name in TRANSPOSED else t)[None]


def _rows3d(t):
    return t.transpose(2, 0, 1)


def _pack_small(d, prefix, shard_of_convw):
    flat = jnp.concatenate([d[prefix + n][0] for n in SMALL_ORDER] + [shard_of_convw.reshape(-1)])
    return jnp.pad(flat, (0, SMALL_LEN - flat.shape[0])).reshape(SMALL_ROWS, D)


def _unpack_small(block, like):
    flat = block.reshape(-1)
    out, off = {}, 0
    for n in SMALL_ORDER:
        size = like[n].shape[1]
        out[n] = flat[off:off + size].reshape(1, size)
        off += size
    out["conv_w"] = flat[off:off + CONV_K * CONVW_SH].reshape(1, CONV_K, CONVW_SH)
    return out


def kernel(x, positions, ffn1_pre_norm, ffn1_w_gate, ffn1_w_up, ffn1_w_down, ffn1_post_norm, mix_pre_norm, w_in, conv_w, conv_b, dt_bias, a_log, d_skip, ssm_norm, w_out, mix_post_norm, ffn2_pre_norm, ffn2_w_gate, ffn2_w_up, ffn2_w_down, ffn2_post_norm, loss_target, m_ffn1_pre_norm, m_ffn1_w_gate, m_ffn1_w_up, m_ffn1_w_down, m_ffn1_post_norm, m_mix_pre_norm, m_w_in, m_conv_w, m_conv_b, m_dt_bias, m_a_log, m_d_skip, m_ssm_norm, m_w_out, m_mix_post_norm, m_ffn2_pre_norm, m_ffn2_w_gate, m_ffn2_w_up, m_ffn2_w_down, m_ffn2_post_norm, v_ffn1_pre_norm, v_ffn1_w_gate, v_ffn1_w_up, v_ffn1_w_down, v_ffn1_post_norm, v_mix_pre_norm, v_w_in, v_conv_w, v_conv_b, v_dt_bias, v_a_log, v_d_skip, v_ssm_norm, v_w_out, v_mix_post_norm, v_ffn2_pre_norm, v_ffn2_w_gate, v_ffn2_w_up, v_ffn2_w_down, v_ffn2_post_norm):
    given = dict(locals())
    xi, yi = lax.axis_index("x"), lax.axis_index("y")

    shard = jnp.reshape(2 * xi + yi, (1,)).astype(jnp.int32)
    big = {p + n: _shard2d(given[p + n], n) for n in BIG for p in ("", "m_", "v_")}
    gu1 = _cast_stack("cast_ffn1_gate_up", shard, [big[n] for n in BIG[0:2]], 176, D)
    d1 = _cast_stack("cast_ffn1_down", shard, [big[BIG[2]]], 176, D)
    f2 = _cast_stack("cast_ffn2", shard, [big[n] for n in BIG[3:6]], 176, D)
    winsh = _cast_stack("cast_w_in", shard, [big["w_in"]], WIN_SH, 256).reshape(NSH, WIN_SH, D)
    woutsh = _cast_stack("cast_w_out", shard, [big["w_out"]], 256, D).reshape(NSH, 2 * D // NSH, D)
    comm = _Comm()
    (gu1,), (cwf,) = _run_riders("gather_ffn1_gate_up", [comm.gather(gu1), _small_gather_rider(conv_w[0])])
    convw = cwf.transpose(1, 0, 2).reshape(CONV_K, CONV_C)

    sp = {n: given[n] for n in SMALL_ORDER}
    grad_x, big_grads, small = _local_step(x[0], positions[0], loss_target[0], sp, gu1, d1, f2, winsh, woutsh,
                                           convw, comm)

    tot = _allreduce_small(small).reshape(-1)
    loss = tot[OFF_LOSS]
    small_grads, off = {}, 0
    for n in SMALL_ORDER:
        size = given[n].shape[1]
        small_grads[n] = tot[off:off + size].reshape(1, size)
        off += size
    dconvw = tot[OFF_CONVW:OFF_CONVW + CONV_K * CONV_C].reshape(CONV_K, NSH, CONVW_SH)
    dconvw = lax.dynamic_index_in_dim(dconvw, 2 * xi + yi, axis=1, keepdims=False)
    small_grads["conv_w"] = dconvw.reshape(1, CONV_K, CONVW_SH)

    upd = {}
    for names, tr in ((BIG[0:3], 176), (BIG[3:6], 176), (BIG[7:8], 256)):
        res = _adamw("adamw_" + names[0], [big[n] for n in names], [big_grads[n] for n in names],
                     [big["m_" + n] for n in names], [big["v_" + n] for n in names], tr, D)
        for n, r in zip(names, res):
            upd[n] = tuple(_unshard2d(t, n) for t in r)
    g_win = big_grads["w_in"].reshape(WIN_SH, 1, D)
    res, = _adamw("adamw_w_in", [_rows3d(w_in)], [g_win], [_rows3d(m_w_in)], [_rows3d(v_w_in)], WIN_SH // 4, D)
    upd["w_in"] = tuple(t.transpose(1, 2, 0) for t in res)
    (dl, m2, v2), = _adamw(
        "adamw_small", [_pack_small(given, "", conv_w[0])], [_pack_small(small_grads, "", dconvw)],
        [_pack_small(given, "m_", m_conv_w[0])], [_pack_small(given, "v_", v_conv_w[0])], SMALL_ROWS, D)
    dl, m2, v2 = (_unpack_small(t, given) for t in (dl, m2, v2))
    for n in SMALL_ORDER + ("conv_w",):
        upd[n] = (dl[n], m2[n], v2[n])

    grads = dict(small_grads)
    grads.update({n: _unshard2d(g, n) for n, g in big_grads.items() if n != "w_in"})
    grads["w_in"] = g_win.transpose(1, 2, 0)
    return (loss, grad_x[None], *[grads[n] for n in WEIGHTS], *[upd[n][0] for n in WEIGHTS],
            *[upd[n][1] for n in WEIGHTS], *[upd[n][2] for n in WEIGHTS])
```

```python
import functools
import typing

import jax
import jax.numpy as jnp
from jax import lax
from jax.experimental import pallas as pl
from jax.experimental.pallas import tpu as pltpu

F32 = jnp.float32
BF16 = jnp.bfloat16

S = 2048
D = 1024
FF = 2816
NSH = 4
FS = FF // NSH
HALF = D // 2
HD = 64
NKV = 4
NQ_PER_KV = 4
KVW = NKV * HD
QCOLS = NQ_PER_KV * HD
CONV_C = 1536
CONV_K = 4
SSM_W = 1024
NST = 128
NCH = S // 128
WIN_COLS = 4112
WIN_SH = WIN_COLS // NSH
WIN_PAD = 4224
COL_DT = 4096
EPS = 1e-6
NEG = -1e30

ADAM_LR = 0.001
ADAM_B1 = 0.9
ADAM_B2 = 0.999
ADAM_EPS = 1e-08
ADAM_WD = 0.01
ADAM_STEP = 10

VMEM_LIMIT = 56 * 1024 * 1024
TS = 512
TR = 256

NN = (((1,), (0,)), ((), ()))
NT = (((1,), (1,)), ((), ()))
TN = (((0,), (0,)), ((), ()))
MESH = pl.DeviceIdType.MESH


def _cparams(*sem):
    return pltpu.CompilerParams(dimension_semantics=sem, vmem_limit_bytes=VMEM_LIMIT)


def _dot(a, b, dims):
    return lax.dot_general(a.astype(BF16), b.astype(BF16), dims, preferred_element_type=F32)


def _dot_exact(a, b):
    return lax.dot_general(a, b, NN, precision=lax.Precision.HIGHEST, preferred_element_type=F32)


def _sigmoid(v):
    return 1.0 / (1.0 + jnp.exp(-v))


class _Rider(typing.NamedTuple):
    operands: list
    out_shapes: list
    aliases: dict
    sems: list
    start: typing.Callable
    finish: typing.Callable


def _call(body, name, grid, in_specs, out_specs, out_shape, operands, scratch=(), sem=(), rider=None):
    multi = isinstance(out_shape, (list, tuple))
    if rider is None:
        return pl.pallas_call(
            body, name=name, grid=grid, in_specs=in_specs, out_specs=out_specs, out_shape=out_shape,
            scratch_shapes=list(scratch), compiler_params=_cparams(*sem))(*operands)
    outs = list(out_shape) if multi else [out_shape]
    ospecs = list(out_specs) if multi else [out_specs]
    n_in, n_out, n_scr = len(operands), len(outs), len(scratch)
    ri, ro = len(rider.operands), len(rider.out_shapes)

    def wrapped(*refs):
        o0 = n_in + ri
        s0 = o0 + n_out + ro
        rin, rout, rsem = refs[n_in:o0], refs[o0 + n_out:s0], refs[s0 + n_scr:]
        ids = [pl.program_id(a) for a in range(len(grid))]
        first = functools.reduce(jnp.logical_and, [i == 0 for i in ids])
        last = functools.reduce(jnp.logical_and, [i == g - 1 for i, g in zip(ids, grid)])

        @pl.when(first)
        def _():
            rider.start(rin, rout, rsem)

        body(*refs[:n_in], *refs[o0:o0 + n_out], *refs[s0:s0 + n_scr])

        @pl.when(last)
        def _():
            rider.finish(rin, rout, rsem)

    hbm = pl.BlockSpec(memory_space=pl.ANY)
    res = pl.pallas_call(
        wrapped, name=name, grid=grid, in_specs=list(in_specs) + [hbm] * ri, out_specs=ospecs + [hbm] * ro,
        out_shape=outs + list(rider.out_shapes), scratch_shapes=list(scratch) + list(rider.sems),
        input_output_aliases={n_in + k: n_out + v for k, v in rider.aliases.items()},
        compiler_params=_cparams(*(("arbitrary",) * len(grid))))(*operands, *rider.operands)
    main = list(res[:n_out])
    return (main if multi else main[0]), list(res[n_out:])


class _Tail(typing.NamedTuple):
    fn: typing.Callable
    operands: list
    in_specs: list


def _mm(name, operands, dims, grid, in_specs, o_spec, out_shape, rider=None, tail=None):
    npairs = len(operands) // 2
    extra = [] if tail is None else list(tail.operands)
    nin = 2 * npairs + len(extra)

    def body(*refs):
        t = None
        for i in range(npairs):
            a, b = refs[2 * i], refs[2 * i + 1]
            parts = [(a[s], b[s]) for s in range(a.shape[0])] if len(a.shape) == 3 else [(a[...], b[...])]
            for pa, pb in parts:
                d = _dot(pa, pb, dims)
                t = d if t is None else t + d
        if tail is None:
            refs[nin][...] = t.astype(refs[nin].dtype)
        else:
            tail.fn(t, refs[2 * npairs:nin], refs[nin:])

    sem = ("parallel" if tail is None else "arbitrary",) * len(grid)
    specs = list(in_specs) + ([] if tail is None else list(tail.in_specs))
    return _call(body, name, grid, specs, o_spec, out_shape, list(operands) + extra, (), sem, rider)


class _FfnW(typing.NamedTuple):
    gu: jax.Array
    g0: int
    dn: jax.Array
    d0: int


def _ffn_up(name, n, w, rider=None):
    def body(n_ref, wg_ref, wu_ref, fg_ref, fu_ref, a_ref):
        nb = n_ref[...]
        g = _dot(nb, wg_ref[...], NT)
        u = _dot(nb, wu_ref[...], NT)
        sg = _sigmoid(g)
        silu = g * sg
        fg_ref[...] = (u * (sg * (1.0 + g * (1.0 - sg)))).astype(BF16)
        fu_ref[...] = silu.astype(BF16)
        a_ref[...] = (silu * u).astype(BF16)

    out = jax.ShapeDtypeStruct((NSH, S, FS), BF16)
    ospec = pl.BlockSpec((None, TS, FS), lambda s, i: (s, i, 0))
    return _call(
        body, name, (NSH, S // TS),
        [pl.BlockSpec((TS, D), lambda s, i: (i, 0)),
         pl.BlockSpec((None, None, FS, D), lambda s, i: (s, w.g0, 0, 0)),
         pl.BlockSpec((None, None, FS, D), lambda s, i: (s, w.g0 + 1, 0, 0))],
        [ospec, ospec, ospec], [out, out, out], (n, w.gu, w.gu), sem=("parallel", "parallel"), rider=rider)


def _ffn_dact(name, dh, w, fgate, fup, rider=None):
    def body(dh_ref, wd_ref, fg_ref, fu_ref, dg_ref, du_ref):
        da = _dot(dh_ref[...], wd_ref[...], NT)
        dg_ref[...] = (da * fg_ref[...].astype(F32)).astype(BF16)
        du_ref[...] = (da * fu_ref[...].astype(F32)).astype(BF16)

    out = jax.ShapeDtypeStruct((NSH, S, FS), BF16)
    aspec = pl.BlockSpec((None, TS, FS), lambda s, i: (s, i, 0))
    return _call(
        body, name, (NSH, S // TS),
        [pl.BlockSpec((TS, D), lambda s, i: (i, 0)),
         pl.BlockSpec((None, None, FS, D), lambda s, i: (s, w.d0, 0, 0)), aspec, aspec],
        [aspec, aspec], [out, out], (dh, w.dn, fgate, fup), sem=("parallel", "parallel"), rider=rider)


def _rstd(v):
    return lax.rsqrt(jnp.mean(v * v, axis=-1, keepdims=True) + EPS)


def _row_spec():
    return pl.BlockSpec((TR, D), lambda i: (i, 0))


def _vec_spec():
    return pl.BlockSpec((1, D), lambda i: (0, 0))


def _acc_rows(ref, v):
    @pl.when(pl.program_id(0) == 0)
    def _():
        ref[...] = jnp.zeros_like(ref)
    ref[...] += jnp.sum(v, axis=0, keepdims=True)


def _prenorm(name, x, g):
    def body(x_ref, g_ref, n_ref):
        xv = x_ref[...]
        n_ref[...] = (xv * _rstd(xv) * g_ref[...]).astype(BF16)

    return pl.pallas_call(
        body, name=name, grid=(S // TR,), in_specs=[_row_spec(), _vec_spec()], out_specs=_row_spec(),
        out_shape=jax.ShapeDtypeStruct((S, D), BF16), compiler_params=_cparams("parallel"),
    )(x, g)


def _rows_spec(rows):
    return pl.BlockSpec((rows, D), lambda i: (i, 0))


def _rows_f32():
    return jax.ShapeDtypeStruct((S, D), F32)


def _rows_bf16():
    return jax.ShapeDtypeStruct((S, D), BF16)


def _vec_f32():
    return jax.ShapeDtypeStruct((1, D), F32)


def _tail_postres(rows, x, p, alpha, gnext):
    def fn(h, ins, outs):
        x_ref, p_ref, g_ref = ins
        h_ref, xo_ref, n_ref = outs
        h_ref[...] = h
        xo = x_ref[...] + alpha * (h * _rstd(h) * p_ref[...])
        xo_ref[...] = xo
        n_ref[...] = (xo * _rstd(xo) * g_ref[...]).astype(BF16)

    rs = _rows_spec(rows)
    return (_Tail(fn, [x, p, gnext], [rs, _vec_spec(), _vec_spec()]), [rs, rs, rs],
            [_rows_f32(), _rows_f32(), _rows_bf16()])


def _tail_final(rows, x, p, tgt, alpha):
    def fn(h, ins, outs):
        x_ref, p_ref, t_ref = ins
        dy_ref, dh_ref, dp_ref, loss_ref = outs
        r = _rstd(h)
        hn = h * r
        pv = p_ref[...]
        e = x_ref[...] + alpha * (hn * pv) - t_ref[...]
        dy = e * (1.0 / D)
        dy_ref[...] = dy
        du = alpha * dy * pv
        dh_ref[...] = (r * (du - hn * jnp.mean(du * hn, axis=-1, keepdims=True))).astype(BF16)
        _acc_rows(dp_ref, alpha * dy * hn)
        part = 0.5 * jnp.sum(jnp.mean(e * e, axis=-1, keepdims=True), axis=0, keepdims=True)
        _acc_rows(loss_ref, jnp.broadcast_to(part, (1, 128)))

    rs = _rows_spec(rows)
    return (_Tail(fn, [x, p, tgt], [rs, _vec_spec(), rs]),
            [rs, rs, _vec_spec(), pl.BlockSpec((1, 128), lambda i: (0, 0))],
            [_rows_f32(), _rows_bf16(), _vec_f32(), jax.ShapeDtypeStruct((1, 128), F32)])


def _norm_bwd(dn, xv, g_ref, dg_ref):
    r = _rstd(xv)
    xn = xv * r
    dng = dn * g_ref[...]
    _acc_rows(dg_ref, dn * xn)
    return r * (dng - xn * jnp.mean(dng * xn, axis=-1, keepdims=True))


def _tail_mid_bwd(rows, dres, x, g, h, p, alpha):
    def fn(dn, ins, outs):
        dr_ref, x_ref, g_ref, h_ref, p_ref = ins
        dx_ref, dh_ref, dg_ref, dp_ref = outs
        dx = dr_ref[...] + _norm_bwd(dn, x_ref[...], g_ref, dg_ref)
        dx_ref[...] = dx
        hv = h_ref[...]
        r = _rstd(hv)
        hn = hv * r
        du = alpha * dx * p_ref[...]
        dh_ref[...] = (r * (du - hn * jnp.mean(du * hn, axis=-1, keepdims=True))).astype(BF16)
        _acc_rows(dp_ref, alpha * dx * hn)

    rs = _rows_spec(rows)
    return (_Tail(fn, [dres, x, g, h, p], [rs, rs, _vec_spec(), rs, _vec_spec()]),
            [rs, rs, _vec_spec(), _vec_spec()], [_rows_f32(), _rows_bf16(), _vec_f32(), _vec_f32()])


def _tail_first_bwd(rows, dres, x, g):
    def fn(dn, ins, outs):
        dr_ref, x_ref, g_ref = ins
        dx_ref, dg_ref = outs
        dx_ref[...] = dr_ref[...] + _norm_bwd(dn, x_ref[...], g_ref, dg_ref)

    rs = _rows_spec(rows)
    return (_Tail(fn, [dres, x, g], [rs, rs, _vec_spec()]), [rs, _vec_spec()], [_rows_f32(), _vec_f32()])


def _rotate(t, c128, s128, sign, scale):
    width = t.shape[1]
    c = jnp.tile(c128, (1, width // 128))
    sn = jnp.tile(s128, (1, width // 128))
    lane = lax.broadcasted_iota(jnp.int32, t.shape, 1) & (HD - 1)
    rot = jnp.where(lane < HD // 2, -pltpu.roll(t, width - HD // 2, 1), pltpu.roll(t, HD // 2, 1))
    return (t * c + sign * (rot * sn)) * scale


def _rows_to_blocks(y):
    out = []
    for j in range(NKV):
        yt = y[:, QCOLS * j:QCOLS * (j + 1)].T
        out.append(jnp.concatenate([yt[HD * g:HD * (g + 1)] for g in range(NQ_PER_KV)], axis=1))
    return out


def _blocks_to_rows(blocks):
    cols = []
    for b in blocks:
        stacked = jnp.concatenate([b[:, 128 * g:128 * (g + 1)] for g in range(NQ_PER_KV)], axis=0)
        cols.append(stacked.T)
    return jnp.concatenate(cols, axis=1)


def _rope_q(proj, cos, sin):
    def body(t_ref, c_ref, s_ref, o_ref):
        y = _rotate(t_ref[...], c_ref[...], s_ref[...], 1.0, HD ** -0.5)
        for j, blk in enumerate(_rows_to_blocks(y)):
            o_ref[j] = blk.astype(BF16)

    return pl.pallas_call(
        body, name="rope_q", grid=(NCH,),
        in_specs=[pl.BlockSpec((128, D), lambda i: (i, 0)),
                  pl.BlockSpec((128, 128), lambda i: (i, 0)), pl.BlockSpec((128, 128), lambda i: (i, 0))],
        out_specs=pl.BlockSpec((NKV, None, HD, QROWS), lambda i: (0, i, 0, 0)),
        out_shape=jax.ShapeDtypeStruct((NKV, NCH, HD, QROWS), BF16), compiler_params=_cparams("parallel"),
    )(proj, cos, sin)


def _rope_dq(dqt, cos, sin, dproj):
    def body(t_ref, c_ref, s_ref, buf_ref, o_ref):
        t = _blocks_to_rows([t_ref[j] for j in range(NKV)])
        o_ref[...] = _rotate(t, c_ref[...], s_ref[...], -1.0, HD ** -0.5).astype(BF16)

    return pl.pallas_call(
        body, name="rope_dq", grid=(NCH,),
        in_specs=[pl.BlockSpec((NKV, None, HD, QROWS), lambda i: (0, i, 0, 0)),
                  pl.BlockSpec((128, 128), lambda i: (i, 0)), pl.BlockSpec((128, 128), lambda i: (i, 0)),
                  pl.BlockSpec(memory_space=pl.ANY)],
        out_specs=pl.BlockSpec((128, D), lambda i: (i, 0)),
        out_shape=jax.ShapeDtypeStruct(dproj.shape, BF16), input_output_aliases={3: 0},
        compiler_params=_cparams("parallel"),
    )(dqt, cos, sin, dproj)


def _rope(name, src, col_block, width, cos, sin, sign, scale, into=None):
    def body(t_ref, c_ref, s_ref, *rest):
        rest[-1][...] = _rotate(t_ref[...].astype(F32), c_ref[...], s_ref[...], sign, scale).astype(BF16)

    in_specs = [pl.BlockSpec((TR, width), lambda i: (i, col_block)),
                pl.BlockSpec((TR, 128), lambda i: (i, 0)), pl.BlockSpec((TR, 128), lambda i: (i, 0))]
    if into is None:
        return pl.pallas_call(
            body, name=name, grid=(S // TR,), in_specs=in_specs,
            out_specs=pl.BlockSpec((TR, width), lambda i: (i, 0)),
            out_shape=jax.ShapeDtypeStruct((S, width), BF16), compiler_params=_cparams("parallel"),
        )(src, cos, sin)
    buf, out_block = into
    return pl.pallas_call(
        body, name=name, grid=(S // TR,), in_specs=in_specs + [pl.BlockSpec(memory_space=pl.ANY)],
        out_specs=pl.BlockSpec((TR, width), lambda i: (i, out_block)),
        out_shape=jax.ShapeDtypeStruct(buf.shape, BF16), input_output_aliases={3: 0},
        compiler_params=_cparams("parallel"),
    )(src, cos, sin, buf)


QROWS = NQ_PER_KV * 128


NBIAS = NCH + 1
KV_PER_STEP = 4


def _bias_table():
    db = lax.broadcasted_iota(jnp.int32, (NBIAS, 128, QROWS), 0) - 1
    ki = lax.broadcasted_iota(jnp.int32, (NBIAS, 128, QROWS), 1)
    qi = lax.broadcasted_iota(jnp.int32, (NBIAS, 128, QROWS), 2) & 127
    d = db * 128 + qi - ki
    cnt = ((d <= 128).astype(F32) + (((d & 3) == 0) & (d <= 512)).astype(F32) + ((d & 15) == 0).astype(F32))
    return jnp.where((d >= 0) & (cnt > 0.0), jnp.log(jnp.maximum(cnt, 1.0)), NEG)


def _qt_spec():
    return pl.BlockSpec((None, None, HD, QROWS), lambda j, i: (j, i, 0, 0))


def _stat_spec():
    return pl.BlockSpec((None, None, 1, QROWS), lambda j, i: (j, i, 0, 0))


def _attn_fwd(qt, kh, vt, bias, rider=None):
    def body(q_ref, k_ref, v_ref, b_ref, o_ref, lse_ref, rows_ref):
        qb = pl.program_id(1)

        def keys(carry, off, size, bias_):
            out = []
            for h in range(KV_PER_STEP):
                m, l, acc = carry[3 * h:3 * h + 3]
                s = _dot(k_ref[h, pl.ds(off, size), :], q_ref[h], NN) + bias_
                m_new = jnp.maximum(m, jnp.max(s, axis=0, keepdims=True))
                p = jnp.exp(s - m_new)
                a = jnp.exp(m - m_new)
                out += [m_new, a * l + jnp.sum(p, axis=0, keepdims=True),
                        a * acc + _dot(v_ref[h, :, pl.ds(off, size)], p, NN)]
            return tuple(out)

        def pair(i, carry):
            bias2 = jnp.concatenate([b_ref[qb - 2 * i + 1], b_ref[qb - 2 * i]], axis=0)
            return keys(carry, pl.multiple_of(i * 256, 256), 256, bias2)

        init = (jnp.full((1, QROWS), NEG, F32), jnp.zeros((1, QROWS), F32), jnp.zeros((HD, QROWS), F32))
        res = lax.fori_loop(0, (qb + 1) // 2, pair, init * KV_PER_STEP)
        res = lax.cond(qb % 2 == 0,
                       lambda c: keys(c, pl.multiple_of(qb * 128, 128), 128, b_ref[1]), lambda c: c, res)
        outs = []
        for h in range(KV_PER_STEP):
            m, l, acc = res[3 * h:3 * h + 3]
            outs.append(acc / l)
            o_ref[h] = outs[h]
            lse_ref[h] = m + jnp.log(l)
        rows_ref[...] = _blocks_to_rows(outs).astype(BF16)

    kvs = KV_PER_STEP
    qspec = pl.BlockSpec((kvs, None, HD, QROWS), lambda j, i: (j, i, 0, 0))
    return _call(
        body, "attn_fwd", (NKV // kvs, NCH),
        [qspec, pl.BlockSpec((kvs, S, HD), lambda j, i: (j, 0, 0)),
         pl.BlockSpec((kvs, HD, S), lambda j, i: (j, 0, 0)),
         pl.BlockSpec((NBIAS, 128, QROWS), lambda j, i: (0, 0, 0))],
        [qspec, pl.BlockSpec((kvs, None, 1, QROWS), lambda j, i: (j, i, 0, 0)),
         pl.BlockSpec((128, QCOLS * kvs), lambda j, i: (i, j))],
        [jax.ShapeDtypeStruct((NKV, NCH, HD, QROWS), F32), jax.ShapeDtypeStruct((NKV, NCH, 1, QROWS), F32),
         jax.ShapeDtypeStruct((S, D), BF16)],
        (qt, kh, vt, bias), sem=("parallel", "parallel"), rider=rider)


def _attn_delta(ot, dot_):
    def body(o_ref, do_ref, dl_ref):
        dl_ref[...] = jnp.sum(o_ref[...] * do_ref[...].astype(F32), axis=1, keepdims=True)

    spec = pl.BlockSpec((None, NCH, HD, QROWS), lambda j: (j, 0, 0, 0))
    return pl.pallas_call(
        body, name="attn_delta", grid=(NKV,), in_specs=[spec, spec],
        out_specs=pl.BlockSpec((None, NCH, 1, QROWS), lambda j: (j, 0, 0, 0)),
        out_shape=jax.ShapeDtypeStruct((NKV, NCH, 1, QROWS), F32), compiler_params=_cparams("parallel"),
    )(ot, dot_)


def _attn_bwd(qt, kh, kt, vh, dot_, lse, delta, bias):
    def body(qt_ref, k_ref, kt_ref, v_ref, dot_ref, lse_ref, dl_ref, b_ref, dq_ref, dk_ref, dv_ref):
        kb = pl.program_id(1)

        @pl.when(kb == 0)
        def _():
            dq_ref[...] = jnp.zeros_like(dq_ref)

        def blocks(carry, qbs):
            out = list(carry)
            for h in range(KV_PER_STEP):
                k, kt_, v = k_ref[h], kt_ref[h], v_ref[h]
                for qb in qbs:
                    st = _dot(k, qt_ref[h, qb], NN) + b_ref[qb - kb + 1]
                    pt = jnp.exp(st - lse_ref[h, qb])
                    dst = pt * (_dot(v, dot_ref[h, qb], NN) - dl_ref[h, qb])
                    dq_ref[h, qb] += _dot(kt_, dst, NN)
                    out[2 * h] = out[2 * h] + _dot(dst, qt_ref[h, qb], NT)
                    out[2 * h + 1] = out[2 * h + 1] + _dot(pt, dot_ref[h, qb], NT)
            return tuple(out)

        res = (jnp.zeros((128, HD), F32),) * (2 * KV_PER_STEP)
        res = lax.cond(kb % 2 == 1, lambda c: blocks(c, (kb,)), lambda c: c, res)
        res = lax.fori_loop((kb + 1) // 2, NCH // 2, lambda j, c: blocks(c, (2 * j, 2 * j + 1)), res)
        for h in range(KV_PER_STEP):
            dk_ref[h] = res[2 * h]
            dv_ref[h] = res[2 * h + 1]

    kvs = KV_PER_STEP
    tspec = pl.BlockSpec((kvs, NCH, HD, QROWS), lambda j, i: (j, 0, 0, 0))
    kspec = pl.BlockSpec((kvs, 128, HD), lambda j, i: (j, i, 0))
    sspec = pl.BlockSpec((kvs, NCH, 1, QROWS), lambda j, i: (j, 0, 0, 0))
    return pl.pallas_call(
        body, name="attn_bwd", grid=(NKV // kvs, NCH),
        in_specs=[tspec, kspec, pl.BlockSpec((kvs, HD, 128), lambda j, i: (j, 0, i)), kspec, tspec,
                  sspec, sspec, pl.BlockSpec((NBIAS, 128, QROWS), lambda j, i: (0, 0, 0))],
        out_specs=[tspec, kspec, kspec],
        out_shape=[jax.ShapeDtypeStruct((NKV, NCH, HD, QROWS), F32),
                   jax.ShapeDtypeStruct((NKV, S, HD), F32), jax.ShapeDtypeStruct((NKV, S, HD), F32)],
        compiler_params=_cparams("parallel", "arbitrary"),
    )(qt, kh, kt, vh, dot_, lse, delta, bias)


CONV_BLK = 256
CONV_COL0 = 1536 // CONV_BLK


def _shift_down(u, j, row):
    return jnp.where(row >= j, pltpu.roll(u, j, 0), 0.0)


def _conv_pre(u, w_ref, b_ref, row):
    y = b_ref[...] + w_ref[CONV_K - 1:CONV_K, :] * u
    for j in range(1, CONV_K):
        y = y + w_ref[CONV_K - 1 - j:CONV_K - j, :] * _shift_down(u, j, row)
    return y


def _conv_fwd(proj, convw, convb):
    def body(u_ref, w_ref, b_ref, o_ref):
        u = u_ref[...]
        row = lax.broadcasted_iota(jnp.int32, u.shape, 0)
        y = _conv_pre(u, w_ref, b_ref, row)
        o_ref[...] = y * _sigmoid(y)

    return pl.pallas_call(
        body, name="conv_fwd", grid=(CONV_C // CONV_BLK,),
        in_specs=[pl.BlockSpec((S, CONV_BLK), lambda i: (0, CONV_COL0 + i)),
                  pl.BlockSpec((CONV_K, CONV_BLK), lambda i: (0, i)),
                  pl.BlockSpec((1, CONV_BLK), lambda i: (0, i))],
        out_specs=pl.BlockSpec((S, CONV_BLK), lambda i: (0, i)),
        out_shape=jax.ShapeDtypeStruct((S, CONV_C), F32), compiler_params=_cparams("parallel"),
    )(proj, convw, convb)


def _conv_bwd(dact, proj, convw, convb, dproj):
    def body(da_ref, u_ref, w_ref, b_ref, buf_ref, du_ref, dw_ref, db_ref):
        u = u_ref[...]
        row = lax.broadcasted_iota(jnp.int32, u.shape, 0)
        y = _conv_pre(u, w_ref, b_ref, row)
        sg = _sigmoid(y)
        dy = da_ref[...] * (sg * (1.0 + y * (1.0 - sg)))
        db_ref[...] = jnp.sum(dy, axis=0, keepdims=True)
        du = w_ref[CONV_K - 1:CONV_K, :] * dy
        r8 = lax.broadcasted_iota(jnp.int32, (8, CONV_BLK), 0)
        dw = jnp.where(r8 == CONV_K - 1, jnp.sum(dy * u, axis=0, keepdims=True), 0.0)
        for j in range(1, CONV_K):
            du = du + w_ref[CONV_K - 1 - j:CONV_K - j, :] * jnp.where(row < S - j, pltpu.roll(dy, S - j, 0), 0.0)
            dw = dw + jnp.where(r8 == CONV_K - 1 - j,
                                jnp.sum(dy * _shift_down(u, j, row), axis=0, keepdims=True), 0.0)
        du_ref[...] = du.astype(BF16)
        dw_ref[...] = dw

    return pl.pallas_call(
        body, name="conv_bwd", grid=(CONV_C // CONV_BLK,),
        in_specs=[pl.BlockSpec((S, CONV_BLK), lambda i: (0, i)),
                  pl.BlockSpec((S, CONV_BLK), lambda i: (0, CONV_COL0 + i)),
                  pl.BlockSpec((CONV_K, CONV_BLK), lambda i: (0, i)),
                  pl.BlockSpec((1, CONV_BLK), lambda i: (0, i)), pl.BlockSpec(memory_space=pl.ANY)],
        out_specs=[pl.BlockSpec((S, CONV_BLK), lambda i: (0, CONV_COL0 + i)),
                   pl.BlockSpec((8, CONV_BLK), lambda i: (0, i)), pl.BlockSpec((1, CONV_BLK), lambda i: (0, i))],
        out_shape=[jax.ShapeDtypeStruct(dproj.shape, BF16), jax.ShapeDtypeStruct((8, CONV_C), F32),
                   jax.ShapeDtypeStruct((1, CONV_C), F32)],
        input_output_aliases={4: 0}, compiler_params=_cparams("parallel"),
    )(dact, proj, convw, convb, dproj)


NPAIR = 8


def _ssd_scalars(dtr_ref, dtb_ref, alog_ref):
    z = dtr_ref[...] + dtb_ref[...]
    dt = jnp.maximum(z, 0.0) + jnp.log(1.0 + jnp.exp(-jnp.abs(z)))
    a = -jnp.exp(alog_ref[...])
    r = lax.broadcasted_iota(jnp.int32, (128, 128), 0)
    c = lax.broadcasted_iota(jnp.int32, (128, 128), 1)
    tri = (r >= c).astype(F32)
    cs = _dot_exact(tri, dt * a)
    return z, dt, a, cs, r, c


def _pair_terms(cs, cst, dt, h1, h2, lo):
    c1, c2 = cs[:, h1:h1 + 1], cs[:, h2:h2 + 1]
    l1, l2 = cs[127:128, h1:h1 + 1], cs[127:128, h2:h2 + 1]
    e_l = jnp.where(lo, jnp.exp(c1), jnp.exp(c2))
    dte1, dte2 = jnp.exp(l1 - c1), jnp.exp(l2 - c2)
    dte_l = jnp.where(lo, dte1, dte2)
    dt_l = jnp.where(lo, dt[:, h1:h1 + 1], dt[:, h2:h2 + 1])
    return c1, c2, jnp.exp(l1), jnp.exp(l2), e_l, dte1, dte2, dte_l, dt_l


def _gate_norm(y, zv, w):
    yg = y * (zv * _sigmoid(zv))
    outs, rs = [], []
    for g in range(2):
        blk = yg[:, 512 * g:512 * (g + 1)]
        r = lax.rsqrt(jnp.mean(blk * blk, axis=-1, keepdims=True) + EPS)
        outs.append(blk * r)
        rs.append(r)
    return jnp.concatenate(outs, axis=1), rs, yg


def _ssd_fwd(xbc, proj, dtb, alog, dskip_l, ssmw):
    def body(x_ref, b_ref, c_ref, dtr_ref, z_ref, dtb_ref, alog_ref, dsk_ref, w_ref, y_ref, yn_ref, hp_ref, h_ref):
        @pl.when(pl.program_id(0) == 0)
        def _():
            h_ref[...] = jnp.zeros_like(h_ref)

        _, dt, _, cs, r, c = _ssd_scalars(dtr_ref, dtb_ref, alog_ref)
        cst = cs.T
        causal = r >= c
        lo = c < HD
        hp_ref[...] = h_ref[...]
        for g in range(2):
            bg = b_ref[:, 128 * g:128 * (g + 1)]
            cg = c_ref[:, 128 * g:128 * (g + 1)]
            cb = _dot(cg, bg, NT)
            for j in range(4):
                pj = 4 * g + j
                h1, h2 = 2 * pj, 2 * pj + 1
                sl = slice(128 * pj, 128 * (pj + 1))
                xp = x_ref[:, sl]
                c1, c2, cd1, cd2, e_l, _, _, dte_l, dt_l = _pair_terms(cs, cst, dt, h1, h2, lo)
                xdt = xp * dt_l
                m1 = cb * jnp.exp(jnp.where(causal, c1 - cst[h1:h1 + 1, :], NEG))
                m2 = cb * jnp.exp(jnp.where(causal, c2 - cst[h2:h2 + 1, :], NEG))
                yd = jnp.where(lo, _dot(m1, xdt, NN), _dot(m2, xdt, NN))
                hp = h_ref[pj]
                yo = _dot(cg, hp, NT) * e_l
                st = _dot(xdt * dte_l, bg, TN)
                h_ref[pj] = hp * jnp.where(r < HD, cd1, cd2) + st
                y_ref[:, sl] = yd + yo + dsk_ref[:, sl] * xp
        yn, _, _ = _gate_norm(y_ref[...], z_ref[...], w_ref[...])
        yn_ref[...] = (yn * w_ref[...]).astype(BF16)

    return pl.pallas_call(
        body, name="ssd_fwd", grid=(NCH,),
        in_specs=[pl.BlockSpec((128, SSM_W), lambda i: (i, 0)),
                  pl.BlockSpec((128, 256), lambda i: (i, 4)), pl.BlockSpec((128, 256), lambda i: (i, 5)),
                  pl.BlockSpec((128, 128), lambda i: (i, COL_DT // 128)),
                  pl.BlockSpec((128, SSM_W), lambda i: (i, 3)),
                  pl.BlockSpec((1, 128), lambda i: (0, 0)), pl.BlockSpec((1, 128), lambda i: (0, 0)),
                  pl.BlockSpec((1, SSM_W), lambda i: (0, 0)), pl.BlockSpec((1, SSM_W), lambda i: (0, 0))],
        out_specs=[pl.BlockSpec((128, SSM_W), lambda i: (i, 0)), pl.BlockSpec((128, SSM_W), lambda i: (i, 0)),
                   pl.BlockSpec((None, NPAIR, 128, 128), lambda i: (i, 0, 0, 0))],
        out_shape=[jax.ShapeDtypeStruct((S, SSM_W), F32), jax.ShapeDtypeStruct((S, SSM_W), BF16),
                   jax.ShapeDtypeStruct((NCH, NPAIR, 128, 128), F32)],
        scratch_shapes=[pltpu.VMEM((NPAIR, 128, 128), F32)],
        compiler_params=_cparams("arbitrary"),
    )(xbc, xbc, xbc, proj, proj, dtb, alog, dskip_l, ssmw)


def _ssd_bwd(dmixed, y, xbc, proj, hprev, dtb, alog, dskip_l, ssmw, rider=None):
    def body(dyn_ref, y_ref, x_ref, b_ref, c_ref, dtr_ref, z_ref, hp_ref, dtb_ref, alog_ref, dsk_ref, w_ref,
             dxbc_ref, dz_ref, ddt_ref, dw_ref, dsc_ref, g_ref):
        @pl.when(pl.program_id(0) == 0)
        def _():
            g_ref[...] = jnp.zeros_like(g_ref)
            dsc_ref[...] = jnp.zeros_like(dsc_ref)

        z, dt, a, cs, r, c = _ssd_scalars(dtr_ref, dtb_ref, alog_ref)
        cst = cs.T
        causal = r >= c
        lo = c < HD

        yv = y_ref[...]
        zv = z_ref[...]
        wv = w_ref[...]
        ygn, rs, yg = _gate_norm(yv, zv, wv)
        dyn = dyn_ref[...]
        _acc_rows(dw_ref, dyn * ygn)
        dynw = dyn * wv
        parts = []
        for g in range(2):
            sl = slice(512 * g, 512 * (g + 1))
            a_g, n_g = dynw[:, sl], ygn[:, sl]
            parts.append(rs[g] * (a_g - n_g * jnp.mean(a_g * n_g, axis=-1, keepdims=True)))
        dyg = jnp.concatenate(parts, axis=1)
        sz = _sigmoid(zv)
        dz_ref[...] = (dyg * yv * (sz * (1.0 + zv * (1.0 - sz)))).astype(BF16)
        dy_all = dyg * (zv * sz)

        dcs_cols = jnp.zeros((128, 128), F32)
        dcs_rows = jnp.zeros((128, 128), F32)
        ddt_x = jnp.zeros((128, 128), F32)
        dd_row = jnp.zeros((1, 128), F32)
        last = r == 127
        x_all, b_all, c_all, dsk_all = x_ref[...], b_ref[...], c_ref[...], dsk_ref[...]
        hp_all, g_all = hp_ref[...], g_ref[...]
        g_new, dx_parts, db_parts, dc_parts = [], [], [], []
        for g in range(2):
            bg = b_all[:, 128 * g:128 * (g + 1)]
            cg = c_all[:, 128 * g:128 * (g + 1)]
            cb = _dot(cg, bg, NT)
            dcb = jnp.zeros((128, 128), F32)
            db_acc = jnp.zeros((128, NST), F32)
            dc_acc = jnp.zeros((128, NST), F32)
            for j in range(4):
                pj = 4 * g + j
                h1, h2 = 2 * pj, 2 * pj + 1
                sl = slice(128 * pj, 128 * (pj + 1))
                xp = x_all[:, sl]
                dyp = dy_all[:, sl]
                c1, c2, cd1, cd2, e_l, dte1, dte2, dte_l, dt_l = _pair_terms(cs, cst, dt, h1, h2, lo)
                xdt = xp * dt_l
                hp = hp_all[pj]
                gp = g_all[pj]
                dxp = dsk_all[:, sl] * dyp
                dyx = dyp * xp
                dd_row = dd_row + jnp.where(c[0:1, :] == h1, jnp.sum(jnp.where(lo, dyx, 0.0), keepdims=True), 0.0) \
                    + jnp.where(c[0:1, :] == h2, jnp.sum(jnp.where(lo, 0.0, dyx), keepdims=True), 0.0)
                dzs = dyp * e_l
                dc_acc = dc_acc + _dot(dzs, hp, NN)
                g_from = _dot(dzs, cg, TN)
                ryo = dyp * (_dot(cg, hp, NT) * e_l)
                k1 = jnp.sum(jnp.where(lo, ryo, 0.0), axis=1, keepdims=True)
                k2 = jnp.sum(jnp.where(lo, 0.0, ryo), axis=1, keepdims=True)
                qm = _dot(bg, gp, NT)
                dxdt = qm * dte_l
                qx = qm * xdt
                t1 = jnp.sum(jnp.where(lo, qx, 0.0), axis=1, keepdims=True) * dte1
                t2 = jnp.sum(jnp.where(lo, 0.0, qx), axis=1, keepdims=True) * dte2
                db_acc = db_acc + _dot(xdt * dte_l, gp, NN)
                gh = gp * hp
                dl1 = jnp.sum(t1, keepdims=True) + jnp.sum(jnp.where(r < HD, gh, 0.0), keepdims=True) * cd1
                dl2 = jnp.sum(t2, keepdims=True) + jnp.sum(jnp.where(r < HD, 0.0, gh), keepdims=True) * cd2
                g_new.append(g_from + jnp.where(r < HD, cd1, cd2) * gp)
                k1 = k1 - t1 + jnp.where(last[:, 0:1], dl1, 0.0)
                k2 = k2 - t2 + jnp.where(last[:, 0:1], dl2, 0.0)
                for hh, ch, msk in ((h1, c1, lo), (h2, c2, jnp.logical_not(lo))):
                    lm = jnp.exp(jnp.where(causal, ch - cst[hh:hh + 1, :], NEG))
                    mm = cb * lm
                    dm = jnp.where(causal, _dot(jnp.where(msk, dyp, 0.0), xdt, NT), 0.0)
                    w = dm * mm
                    kk = jnp.sum(w, axis=1, keepdims=True)
                    if hh == h1:
                        k1 = k1 + kk
                    else:
                        k2 = k2 + kk
                    dcs_rows = dcs_rows + jnp.where(r == hh, jnp.sum(w, axis=0, keepdims=True), 0.0)
                    dcb = dcb + dm * lm
                    dxdt = dxdt + jnp.where(msk, _dot(mm, dyp, TN), 0.0)
                dcs_cols = dcs_cols + jnp.where(c == h1, k1, 0.0) + jnp.where(c == h2, k2, 0.0)
                dxx = dxdt * xp
                ddt_x = ddt_x + jnp.where(c == h1, jnp.sum(jnp.where(lo, dxx, 0.0), axis=1, keepdims=True), 0.0) \
                    + jnp.where(c == h2, jnp.sum(jnp.where(lo, 0.0, dxx), axis=1, keepdims=True), 0.0)
                dx_parts.append(dxp + dxdt * dt_l)
            db_parts.append(db_acc + _dot(dcb, cg, TN))
            dc_parts.append(dc_acc + _dot(dcb, bg, NN))
        g_ref[...] = jnp.stack(g_new)
        dxbc_ref[...] = jnp.concatenate(dx_parts + db_parts + dc_parts, axis=1)

        dcs = dcs_cols - dcs_rows.T
        dad = _dot_exact((c >= r).astype(F32), dcs)
        ddt = dad * a + ddt_x
        ddtr = jnp.where(c < 16, ddt * _sigmoid(z), 0.0)
        ddt_ref[...] = ddtr.astype(BF16)
        r8 = lax.broadcasted_iota(jnp.int32, (8, 128), 0)
        dsc_ref[...] += (jnp.where(r8 == 0, jnp.sum(ddtr, axis=0, keepdims=True), 0.0)
                         + jnp.where(r8 == 1, jnp.sum(dad * dt, axis=0, keepdims=True) * a, 0.0)
                         + jnp.where(r8 == 2, dd_row, 0.0))

    rev = NCH - 1
    return _call(
        body, "ssd_bwd", (NCH,),
        [pl.BlockSpec((128, SSM_W), lambda i: (rev - i, 0)),
         pl.BlockSpec((128, SSM_W), lambda i: (rev - i, 0)),
         pl.BlockSpec((128, SSM_W), lambda i: (rev - i, 0)),
         pl.BlockSpec((128, 256), lambda i: (rev - i, 4)), pl.BlockSpec((128, 256), lambda i: (rev - i, 5)),
         pl.BlockSpec((128, 128), lambda i: (rev - i, COL_DT // 128)),
         pl.BlockSpec((128, SSM_W), lambda i: (rev - i, 3)),
         pl.BlockSpec((None, NPAIR, 128, 128), lambda i: (rev - i, 0, 0, 0)),
         pl.BlockSpec((1, 128), lambda i: (0, 0)), pl.BlockSpec((1, 128), lambda i: (0, 0)),
         pl.BlockSpec((1, SSM_W), lambda i: (0, 0)), pl.BlockSpec((1, SSM_W), lambda i: (0, 0))],
        [pl.BlockSpec((128, CONV_C), lambda i: (rev - i, 0)),
         pl.BlockSpec((128, SSM_W), lambda i: (rev - i, 3)),
         pl.BlockSpec((128, 128), lambda i: (rev - i, 0)),
         pl.BlockSpec((1, SSM_W), lambda i: (0, 0)), pl.BlockSpec((8, 128), lambda i: (0, 0))],
        [jax.ShapeDtypeStruct((S, CONV_C), F32), jax.ShapeDtypeStruct((S, WIN_PAD), BF16),
         jax.ShapeDtypeStruct((S, 128), BF16), jax.ShapeDtypeStruct((1, SSM_W), F32),
         jax.ShapeDtypeStruct((8, 128), F32)],
        (dmixed, y, xbc, xbc, xbc, proj, proj, hprev, dtb, alog, dskip_l, ssmw),
        [pltpu.VMEM((NPAIR, 128, 128), F32)], ("arbitrary",), rider)


def _cast_stack(name, slot, arrs, tr, tc):
    n = len(arrs)
    rows, cols = arrs[0].shape

    def body(s_ref, *refs):
        for i in range(n):
            refs[n][i] = refs[i][...].astype(BF16)

    return pl.pallas_call(
        body, name=name,
        grid_spec=pltpu.PrefetchScalarGridSpec(
            num_scalar_prefetch=1, grid=(rows // tr, cols // tc),
            in_specs=[pl.BlockSpec((tr, tc), lambda i, j, sr: (i, j))] * n,
            out_specs=pl.BlockSpec((None, n, tr, tc), lambda i, j, sr: (sr[0], 0, i, j))),
        out_shape=jax.ShapeDtypeStruct((NSH, n, rows, cols), BF16),
        compiler_params=_cparams("parallel", "parallel"),
    )(slot, *arrs)


def _pair_sum(name, c_idx, ps, th):
    n = len(ps)
    _, rows, _ = ps[0].shape

    def body(c_ref, *refs):
        mine, whole, out, theirs = refs[:n], refs[n:2 * n], refs[2 * n:3 * n], refs[3 * n:4 * n]
        send, recv = refs[4 * n], refs[4 * n + 1]
        s, i = pl.program_id(0), pl.program_id(1)

        @pl.when((s == 0) & (i == 0))
        def _():
            x, y, c, _ = _place()
            cps = [_rcopy(whole[k].at[:, :, pl.ds((1 - c) * HALF, HALF)], theirs[k], send.at[k], recv.at[k],
                          (x, y, 1 - c)) for k in range(n)]
            for cp in cps:
                cp.start()
            for cp in cps:
                cp.wait()

        rows_i = slice(None) if th == rows else pl.ds(pl.multiple_of(i * th, th), th)
        for k in range(n):
            out[k][...] = (mine[k][...].astype(F32) + theirs[k][s, rows_i, :].astype(F32)).astype(BF16)

    spec = pl.BlockSpec((None, th, HALF), lambda s, i, cr: (s, i, 0))
    return pl.pallas_call(
        body, name=name,
        grid_spec=pltpu.PrefetchScalarGridSpec(
            num_scalar_prefetch=1, grid=(NSH, rows // th),
            in_specs=[pl.BlockSpec((None, th, HALF), lambda s, i, cr: (s, i, cr[0]))] * n + _any_specs(n),
            out_specs=[spec] * n,
            scratch_shapes=[pltpu.VMEM((NSH, rows, HALF), BF16)] * n
            + [pltpu.SemaphoreType.DMA((n,)), pltpu.SemaphoreType.DMA((n,))]),
        out_shape=[jax.ShapeDtypeStruct((NSH, rows, HALF), BF16)] * n,
        compiler_params=_cparams("arbitrary", "arbitrary"),
    )(c_idx, *ps, *ps)


def _chip_sum(name, place, cs, ts, th):
    n = len(ts)
    _, rows, _ = ts[0].shape

    def body(p_ref, *refs):
        for i in range(n):
            t = refs[n + i][...].astype(F32)
            refs[2 * n + i][...] = ((refs[i][...].astype(F32) + t[0]) + t[1]) + t[2]

    return pl.pallas_call(
        body, name=name,
        grid_spec=pltpu.PrefetchScalarGridSpec(
            num_scalar_prefetch=1, grid=(rows // th,),
            in_specs=[pl.BlockSpec((None, th, HALF), lambda i, pr: (pr[0], i, 0))] * n
            + [pl.BlockSpec((3, th, HALF), lambda i, pr: (0, i, 0))] * n,
            out_specs=[pl.BlockSpec((th, HALF), lambda i, pr: (i, pr[1]))] * n),
        out_shape=[jax.ShapeDtypeStruct((rows, D), F32)] * n, compiler_params=_cparams("parallel"),
    )(place, *cs, *ts)


def _adamw(name, ws, gs, ms, vs, tr, tc):
    n = len(ws)
    shape = ws[0].shape
    rows, cols, mid = shape[0], shape[-1], shape[1:-1]
    c1 = 1.0 / (1.0 - ADAM_B1 ** ADAM_STEP)
    c2 = 1.0 / (1.0 - ADAM_B2 ** ADAM_STEP)

    def body(*refs):
        for i in range(n):
            w, g, m, v = (refs[k * n + i][...] for k in range(4))
            m2 = ADAM_B1 * m + (1.0 - ADAM_B1) * g
            v2 = ADAM_B2 * v + (1.0 - ADAM_B2) * (g * g)
            refs[4 * n + 3 * i][...] = -ADAM_LR * ((m2 * c1) / (jnp.sqrt(v2 * c2) + ADAM_EPS) + ADAM_WD * w)
            refs[4 * n + 3 * i + 1][...] = m2
            refs[4 * n + 3 * i + 2][...] = v2

    spec = pl.BlockSpec((tr,) + mid + (tc,), lambda i, j: (i,) + (0,) * len(mid) + (j,))
    outs = pl.pallas_call(
        body, name=name, grid=(rows // tr, cols // tc), in_specs=[spec] * (4 * n), out_specs=[spec] * (3 * n),
        out_shape=[jax.ShapeDtypeStruct(shape, F32)] * (3 * n),
        compiler_params=_cparams("parallel", "parallel"),
    )(*ws, *gs, *ms, *vs)
    return [tuple(outs[3 * i:3 * i + 3]) for i in range(n)]


def _place():
    x, y, c = lax.axis_index("x"), lax.axis_index("y"), lax.axis_index("c")
    chips = [(1 - x, y), (x, 1 - y), (1 - x, 1 - y)]
    return x, y, c, chips


def _any_specs(n):
    return [pl.BlockSpec(memory_space=pl.ANY)] * n


def _rcopy(src, dst, send_sem, recv_sem, dev):
    return pltpu.make_async_remote_copy(src_ref=src, dst_ref=dst, send_sem=send_sem, recv_sem=recv_sem,
                                        device_id=dev, device_id_type=MESH)


def _gather_rider(bufs, views):
    n = len(bufs)

    def start(rin, rout, sems):
        send, recv = sems[0], sems[1]
        x, y, c, chips = _place()
        for j, chip in enumerate(chips):
            for b in range(n):
                mine = views[b](rout[b], 2 * x + y, c)
                _rcopy(mine, mine, send.at[j * n + b], recv.at[j * n + b], (chip[0], chip[1], c)).start()

    def finish(rin, rout, sems):
        send, recv, fsend, frecv = sems
        x, y, c, chips = _place()
        passed = []
        for j, chip in enumerate(chips):
            for b in range(n):
                landed = views[b](rout[b], 2 * chip[0] + chip[1], c)
                _rcopy(landed, landed, send.at[j * n + b], recv.at[j * n + b], (x, y, c)).wait_recv()
                fw = _rcopy(landed, landed, fsend.at[j * n + b], frecv.at[j * n + b], (x, y, 1 - c))
                fw.start()
                passed.append(fw)
        for j, chip in enumerate(chips):
            for b in range(n):
                other = views[b](rout[b], 2 * chip[0] + chip[1], 1 - c)
                _rcopy(other, other, fsend.at[j * n + b], frecv.at[j * n + b], (x, y, c)).wait_recv()
        for j, chip in enumerate(chips):
            for b in range(n):
                mine = views[b](rout[b], 2 * x + y, c)
                _rcopy(mine, mine, send.at[j * n + b], recv.at[j * n + b], (x, y, c)).wait_send()
        for fw in passed:
            fw.wait_send()

    return _Rider(list(bufs), [jax.ShapeDtypeStruct(a.shape, a.dtype) for a in bufs], {b: b for b in range(n)},
                  [pltpu.SemaphoreType.DMA((3 * n,))] * 4, start, finish)


def _small_gather_rider(cw):
    def descs(rin, rout, sems, x, y, c, chips):
        return [_rcopy(rin[0], rout[0].at[2 * x + y], sems[1].at[j], sems[2].at[j], (chip[0], chip[1], c))
                for j, chip in enumerate(chips)]

    def start(rin, rout, sems):
        x, y, c, chips = _place()
        pltpu.make_async_copy(rin[0], rout[0].at[2 * x + y], sems[0].at[0]).start()
        for cp in descs(rin, rout, sems, x, y, c, chips):
            cp.start()

    def finish(rin, rout, sems):
        x, y, c, chips = _place()
        for j, chip in enumerate(chips):
            _rcopy(rin[0], rout[0].at[2 * chip[0] + chip[1]], sems[1].at[j], sems[2].at[j], (x, y, c)).wait_recv()
        for cp in descs(rin, rout, sems, x, y, c, chips):
            cp.wait_send()
        pltpu.make_async_copy(rin[0], rout[0].at[2 * x + y], sems[0].at[0]).wait()

    return _Rider([cw], [jax.ShapeDtypeStruct((NSH,) + cw.shape, cw.dtype)], {},
                  [pltpu.SemaphoreType.DMA((1,)), pltpu.SemaphoreType.DMA((3,)), pltpu.SemaphoreType.DMA((3,))],
                  start, finish)


def _to_chips_rider(cs):
    n = len(cs)

    def descs(rin, rout, sems):
        x, y, c, chips = _place()
        return [_rcopy(rin[i].at[2 * chip[0] + chip[1]], rout[i].at[j], sems[0].at[j * n + i], sems[1].at[j * n + i],
                       (chip[0], chip[1], c)) for j, chip in enumerate(chips) for i in range(n)]

    def start(rin, rout, sems):
        for cp in descs(rin, rout, sems):
            cp.start()

    def finish(rin, rout, sems):
        for cp in descs(rin, rout, sems):
            cp.wait()

    return _Rider(list(cs), [jax.ShapeDtypeStruct((3,) + a.shape[1:], a.dtype) for a in cs], {},
                  [pltpu.SemaphoreType.DMA((3 * n,))] * 2, start, finish)


def _run_riders(name, riders):
    n_in = [len(r.operands) for r in riders]
    n_out = [len(r.out_shapes) for r in riders]
    n_sem = [len(r.sems) for r in riders]

    def body(*refs):
        parts, at = [], 0
        for counts in (n_in, n_out, n_sem):
            group = []
            for k in counts:
                group.append(refs[at:at + k])
                at += k
            parts.append(group)
        for i, r in enumerate(riders):
            r.start(parts[0][i], parts[1][i], parts[2][i])
        for i, r in enumerate(riders):
            r.finish(parts[0][i], parts[1][i], parts[2][i])

    aliases = {}
    for i, r in enumerate(riders):
        for k, v in r.aliases.items():
            aliases[sum(n_in[:i]) + k] = sum(n_out[:i]) + v
    res = pl.pallas_call(
        body, name=name, in_specs=_any_specs(sum(n_in)), out_specs=_any_specs(sum(n_out)),
        out_shape=[s for r in riders for s in r.out_shapes], input_output_aliases=aliases,
        scratch_shapes=[s for r in riders for s in r.sems],
    )(*[a for r in riders for a in r.operands])
    out, at = [], 0
    for k in n_out:
        out.append(list(res[at:at + k]))
        at += k
    return out


def _swap_halves(gs):
    n = len(gs)

    def body(*refs):
        dst, send, recv = refs[n:2 * n], refs[2 * n], refs[2 * n + 1]
        x, y, c, _ = _place()
        cps = []
        for i in range(n):
            mine = dst[i].at[:, pl.ds(c * HALF, HALF)]
            cps.append(pltpu.make_async_remote_copy(
                src_ref=mine, dst_ref=mine, send_sem=send.at[i], recv_sem=recv.at[i],
                device_id=(x, y, 1 - c), device_id_type=MESH))
        for cp in cps:
            cp.start()
        for i in range(n):
            other = dst[i].at[:, pl.ds((1 - c) * HALF, HALF)]
            pltpu.make_async_remote_copy(
                src_ref=other, dst_ref=other, send_sem=send.at[i], recv_sem=recv.at[i],
                device_id=(x, y, c), device_id_type=MESH).wait_recv()
        for cp in cps:
            cp.wait_send()

    return pl.pallas_call(
        body, name="grads_swap_halves", in_specs=_any_specs(n), out_specs=_any_specs(n),
        out_shape=[jax.ShapeDtypeStruct(g.shape, g.dtype) for g in gs],
        input_output_aliases={i: i for i in range(n)},
        scratch_shapes=[pltpu.SemaphoreType.DMA((n,)), pltpu.SemaphoreType.DMA((n,))],
    )(*gs)


SMALL_ROWS = 16


def _allreduce_small(vec):
    def body(v_ref, o_ref, buf, send, recv):
        x, y, c, _ = _place()
        me = 4 * x + 2 * y + c
        buf[me] = v_ref[...]
        cps = []
        for k in range(1, 8):
            peer = (x ^ (k >> 2), y ^ ((k >> 1) & 1), c ^ (k & 1))
            cps.append(pltpu.make_async_remote_copy(
                src_ref=v_ref, dst_ref=buf.at[me], send_sem=send.at[k - 1], recv_sem=recv.at[k - 1],
                device_id=peer, device_id_type=MESH))
        for cp in cps:
            cp.start()
        for k in range(1, 8):
            pltpu.make_async_remote_copy(
                src_ref=v_ref, dst_ref=buf.at[me ^ k], send_sem=send.at[k - 1], recv_sem=recv.at[k - 1],
                device_id=(x, y, c), device_id_type=MESH).wait_recv()
        for cp in cps:
            cp.wait_send()
        t = buf[0]
        for d in range(1, 8):
            t = t + buf[d]
        o_ref[...] = t

    return pl.pallas_call(
        body, name="allreduce_small",
        in_specs=[pl.BlockSpec(memory_space=pltpu.VMEM)], out_specs=pl.BlockSpec(memory_space=pltpu.VMEM),
        out_shape=jax.ShapeDtypeStruct((SMALL_ROWS, D), F32),
        scratch_shapes=[pltpu.VMEM((8, SMALL_ROWS, D), F32), pltpu.SemaphoreType.DMA((7,)),
                        pltpu.SemaphoreType.DMA((7,))],
    )(vec)


def _col_half(ref, slot, hc):
    return ref.at[slot, :, pl.ds(hc * HALF, HALF)]


def _stack_half(ref, slot, hc):
    return ref.at[slot, :, :, pl.ds(hc * HALF, HALF)]


def _row_tile(rows):
    for t in range(512, 15, -16):
        if rows % t == 0:
            return t
    return rows


def _same_shape_runs(arrs):
    runs, a = [], 0
    for b in range(1, len(arrs) + 1):
        if b == len(arrs) or arrs[b].shape != arrs[a].shape:
            runs.append((a, b))
            a = b
    return runs


class _Comm:
    def __init__(self):
        x, y, c = lax.axis_index("x"), lax.axis_index("y"), lax.axis_index("c")
        self.c_idx = jnp.reshape(c, (1,)).astype(jnp.int32)
        self.place = jnp.stack([2 * x + y, c]).astype(jnp.int32)
        self.groups = {}

    @staticmethod
    def gather(*bufs):
        return _gather_rider(list(bufs), [_col_half if b.ndim == 3 else _stack_half for b in bufs])

    def reduce_rider(self, tag, names, ps):
        csums = []
        for a, b in _same_shape_runs(ps):
            csums += _pair_sum("pair_sum_%s%d" % (tag, a), self.c_idx, ps[a:b], _row_tile(ps[a].shape[1]))
        self.groups[tag] = [names, csums, None]
        return _to_chips_rider(csums)

    def landed(self, tag, ts):
        self.groups[tag][2] = ts

    def finish(self):
        names, halves = [], []
        for tag, (group_names, csums, ts) in self.groups.items():
            names += group_names
            for a, b in _same_shape_runs(csums):
                halves += _chip_sum("chip_sum_%s%d" % (tag, a), self.place, csums[a:b], ts[a:b],
                                    _row_tile(csums[a].shape[1]))
        return dict(zip(names, _swap_halves(halves)))


ROPE_THETA = 10000.0
SMALL_1K = ("ffn1_pre_norm", "ffn1_post_norm", "mix_pre_norm", "ssm_norm", "mix_post_norm",
            "ffn2_pre_norm", "ffn2_post_norm")
SMALL_16 = ("dt_bias", "a_log", "d_skip")
OFF_CONVB = 7 * D
OFF_16 = OFF_CONVB + CONV_C
OFF_CONVW = OFF_16 + 48
OFF_LOSS = OFF_CONVW + CONV_K * CONV_C
SMALL_LEN = SMALL_ROWS * D


def _sds(shape, dtype):
    return jax.ShapeDtypeStruct(shape, dtype)


def _ridden(res, rider):
    return res if rider is not None else (res, None)


def _ffn_down(name, act, w, tail_of, rider=None):
    tail, o_specs, o_shapes = tail_of(TS)
    return _mm(name, [act, w.dn], NN, (S // TS,),
               [pl.BlockSpec((NSH, TS, FS), lambda i: (0, i, 0)),
                pl.BlockSpec((NSH, None, FS, D), lambda i: (0, w.d0, 0, 0))], o_specs, o_shapes, rider, tail)


def _ffn_dw(name, a, b, rider=None):
    return _mm(name, [a, b], TN, (NSH,),
               [pl.BlockSpec((None, S, FS), lambda s: (s, 0, 0)), pl.BlockSpec((S, D), lambda s: (0, 0))],
               pl.BlockSpec((None, FS, D), lambda s: (s, 0, 0)), _sds((NSH, FS, D), BF16), rider)


def _ffn_dn(name, dgate, dup, w, tail_of, rider=None):
    rows = TS // 2
    tail, o_specs, o_shapes = tail_of(rows)
    a2 = pl.BlockSpec((NSH, rows, FS), lambda i: (0, i, 0))
    return _mm(name, [dgate, w.gu, dup, w.gu], NN, (S // rows,),
               [a2, pl.BlockSpec((NSH, None, FS, D), lambda i: (0, w.g0, 0, 0)),
                a2, pl.BlockSpec((NSH, None, FS, D), lambda i: (0, w.g0 + 1, 0, 0))], o_specs, o_shapes, rider, tail)


def _out_proj_dx(dh, wout):
    def body(dh_ref, w_ref, dyn_ref, do_ref):
        dm = _dot(dh_ref[...], w_ref[...], NT)
        dyn_ref[...] = dm[:, D:]
        for b in range(TS // 128):
            for j, blk in enumerate(_rows_to_blocks(dm[128 * b:128 * (b + 1), :D])):
                do_ref[j, b] = blk.astype(BF16)

    return pl.pallas_call(
        body, name="out_proj_dx", grid=(S // TS,),
        in_specs=[pl.BlockSpec((TS, D), lambda i: (i, 0)), pl.BlockSpec((2 * D, D), lambda i: (0, 0))],
        out_specs=[pl.BlockSpec((TS, D), lambda i: (i, 0)),
                   pl.BlockSpec((NKV, TS // 128, HD, QROWS), lambda i: (0, i, 0, 0))],
        out_shape=[_sds((S, D), F32), _sds((NKV, NCH, HD, QROWS), BF16)], compiler_params=_cparams("parallel"),
    )(dh, wout)


def _heads(t, n):
    return t.reshape(S, n, HD).transpose(1, 0, 2)


def _unheads(t):
    return t.transpose(1, 0, 2).reshape(S, t.shape[0] * HD)


def _heads_t(t, n):
    return t.reshape(S, n, HD).transpose(1, 2, 0)


def _pad128(v):
    return jnp.pad(v, ((0, 0), (0, 128 - v.shape[1])))


def _local_step(x, positions, tgt, sp, gu1, d1, f2, wint, wout, convw, comm=None):
    inv_freq = ROPE_THETA ** (-jnp.arange(0, HD, 2, dtype=F32) / HD)
    ang = positions.astype(F32)[:, None] * inv_freq
    ang = jnp.concatenate([ang, ang, ang, ang], axis=-1)
    cos, sin = jnp.cos(ang), jnp.sin(ang)
    dtb, alog = _pad128(sp["dt_bias"]), _pad128(sp["a_log"])
    dskip_l = jnp.repeat(sp["d_skip"], HD, axis=1)
    convb = sp["conv_b"]

    n1 = _prenorm("prenorm1", x, sp["ffn1_pre_norm"])
    rider = comm.gather(d1) if comm else None
    (fg1, fu1, act1), got = _ridden(_ffn_up("ffn1_up", n1, _FfnW(gu1, 0, d1, 0), rider), rider)
    if comm:
        d1, = got
    w1 = _FfnW(gu1, 0, d1, 0)
    rider = comm.gather(wint) if comm else None
    (h1, x1, n2), got = _ridden(_ffn_down(
        "ffn1_down", act1, w1,
        lambda rows: _tail_postres(rows, x, sp["ffn1_post_norm"], 0.5, sp["mix_pre_norm"]), rider), rider)
    if comm:
        wint, = got
    wint_pad = jnp.pad(wint.reshape(WIN_COLS, D), ((0, WIN_PAD - WIN_COLS), (0, 0)))

    pw = WIN_PAD // 3
    proj = _mm("in_proj", [n2, wint_pad], NT, (S // TS, 3),
               [pl.BlockSpec((TS, D), lambda i, j: (i, 0)), pl.BlockSpec((pw, D), lambda i, j: (j, 0))],
               pl.BlockSpec((TS, pw), lambda i, j: (i, j)), _sds((S, WIN_PAD), F32))
    qt = _rope_q(proj, cos, sin)
    k_rot = _rope("rope_k", proj, D // KVW, KVW, cos, sin, 1.0, 1.0)
    v_bf = proj[:, D + KVW:D + 2 * KVW].astype(BF16)
    kh, vh = _heads(k_rot, NKV), _heads(v_bf, NKV)
    kt, vt = _heads_t(k_rot, NKV), _heads_t(v_bf, NKV)
    bias = _bias_table()
    rider = comm.gather(f2, wout) if comm else None
    (ot, lse, attn), got = _ridden(_attn_fwd(qt, kh, vt, bias, rider), rider)
    if comm:
        f2, wout = got
    w2 = _FfnW(f2, 0, f2, 2)
    wout = wout.reshape(2 * D, D)
    xbc = _conv_fwd(proj, convw, convb)
    y, yn, hprev = _ssd_fwd(xbc, proj, dtb, alog, dskip_l, sp["ssm_norm"])
    mixed = jnp.concatenate([attn, yn], axis=1)
    tail, o_specs, o_shapes = _tail_postres(TS, x1, sp["mix_post_norm"], 1.0, sp["ffn2_pre_norm"])
    h2, x2, n3 = _mm("out_proj", [mixed, wout], NN, (S // TS,),
                     [pl.BlockSpec((TS, 2 * D), lambda i: (i, 0)), pl.BlockSpec((2 * D, D), lambda i: (0, 0))],
                     o_specs, o_shapes, None, tail)

    fg2, fu2, act2 = _ffn_up("ffn2_up", n3, w2)
    dy, dh3, dp3, loss = _ffn_down(
        "ffn2_down", act2, w2, lambda rows: _tail_final(rows, x2, sp["ffn2_post_norm"], tgt, 0.5))

    dgate2, dup2 = _ffn_dact("ffn2_dact", dh3, w2, fg2, fu2)
    dws2 = [_ffn_dw("ffn2_dwg", dgate2, n3), _ffn_dw("ffn2_dwu", dup2, n3), _ffn_dw("ffn2_dwd", act2, dh3)]
    dx2, dh2, dg3, dp2 = _ffn_dn(
        "ffn2_dn", dgate2, dup2, w2,
        lambda rows: _tail_mid_bwd(rows, dy, x2, sp["ffn2_pre_norm"], h2, sp["mix_post_norm"], 1.0))

    dyn, dot_ = _out_proj_dx(dh2, wout)
    dwout = _mm("out_proj_dw", [mixed, dh2], TN, (2,),
                [pl.BlockSpec((S, D), lambda m: (0, m)), pl.BlockSpec((S, D), lambda m: (0, 0))],
                pl.BlockSpec((D, D), lambda m: (m, 0)), _sds((2 * D, D), BF16))
    dwout = dwout.reshape(NSH, 2 * D // NSH, D)
    rider = comm.reduce_rider("a", BIG[3:6] + ("w_out",), dws2 + [dwout]) if comm else None
    (dxbc, dproj, ddt, dssm, dsc), got = _ridden(
        _ssd_bwd(dyn, y, xbc, proj, hprev, dtb, alog, dskip_l, sp["ssm_norm"], rider), rider)
    if comm:
        comm.landed("a", got)
    dproj, dcw8, dcb = _conv_bwd(dxbc, proj, convw, convb, dproj)
    dqt, dkh, dvh = _attn_bwd(qt, kh, kt, vh, dot_, lse, _attn_delta(ot, dot_), bias)
    dproj = _rope_dq(dqt, cos, sin, dproj)
    dproj = _rope("rope_dk", _unheads(dkh), 0, KVW, cos, sin, -1.0, 1.0, into=(dproj, D // KVW))
    dproj = lax.dynamic_update_slice(dproj, _unheads(dvh).astype(BF16), (0, D + KVW))
    dproj = lax.dynamic_update_slice(dproj, ddt, (0, COL_DT))
    dwint = _mm("in_proj_dw", [dproj, n2], TN, (3,),
                [pl.BlockSpec((S, pw), lambda j: (0, j)), pl.BlockSpec((S, D), lambda j: (0, 0))],
                pl.BlockSpec((pw, D), lambda j: (j, 0)), _sds((WIN_PAD, D), BF16))
    dwint = dwint[:WIN_COLS].reshape(NSH, WIN_SH, D)

    def riding(tag, names, ps, call):
        rider = comm.reduce_rider(tag, names, ps) if comm else None
        res, got = _ridden(call(rider), rider)
        if comm:
            comm.landed(tag, got)
        return res

    tail, o_specs, o_shapes = _tail_mid_bwd(TS, dx2, x1, sp["mix_pre_norm"], h1, sp["ffn1_post_norm"], 0.5)
    dx1, dh1, dg2, dp1 = riding("b", ("w_in",), [dwint], lambda rider: _mm(
        "in_proj_dx", [dproj, wint_pad], NN, (S // TS,),
        [pl.BlockSpec((TS, WIN_PAD), lambda i: (i, 0)), pl.BlockSpec((WIN_PAD, D), lambda i: (0, 0))],
        o_specs, o_shapes, rider, tail))

    dwd1 = _ffn_dw("ffn1_dwd", act1, dh1)
    dgate1, dup1 = riding("d", BIG[2:3], [dwd1], lambda rider: _ffn_dact("ffn1_dact", dh1, w1, fg1, fu1, rider))
    dwg1, dwu1 = _ffn_dw("ffn1_dwg", dgate1, n1), _ffn_dw("ffn1_dwu", dup1, n1)
    grad_x, dg1 = riding("g", BIG[0:2], [dwg1, dwu1], lambda rider: _ffn_dn(
        "ffn1_dn", dgate1, dup1, w1, lambda rows: _tail_first_bwd(rows, dx1, x, sp["ffn1_pre_norm"]), rider))
    dws1 = [dwg1, dwu1, dwd1]

    small = jnp.concatenate([
        dg1[0], dp1[0], dg2[0], dssm[0], dp2[0], dg3[0], dp3[0], dcb[0],
        dsc[0, :16], dsc[1, :16], dsc[2, :16], dcw8[:CONV_K].reshape(-1), loss[0, :1]])
    small = jnp.pad(small, (0, SMALL_LEN - small.shape[0])).reshape(SMALL_ROWS, D)
    if comm is None:
        return grad_x, dws1 + dws2 + [dwint, dwout], small
    return grad_x, comm.finish(), small


WEIGHTS = ("ffn1_pre_norm", "ffn1_w_gate", "ffn1_w_up", "ffn1_w_down", "ffn1_post_norm", "mix_pre_norm", "w_in",
           "conv_w", "conv_b", "dt_bias", "a_log", "d_skip", "ssm_norm", "w_out", "mix_post_norm", "ffn2_pre_norm",
           "ffn2_w_gate", "ffn2_w_up", "ffn2_w_down", "ffn2_post_norm")
BIG = ("ffn1_w_gate", "ffn1_w_up", "ffn1_w_down", "ffn2_w_gate", "ffn2_w_up", "ffn2_w_down", "w_in", "w_out")
TRANSPOSED = ("ffn1_w_gate", "ffn1_w_up", "ffn2_w_gate", "ffn2_w_up", "w_in")
SMALL_ORDER = SMALL_1K + ("conv_b",) + SMALL_16
CONVW_SH = CONV_C // NSH


def _shard2d(t, name):
    return t[0].T if name in TRANSPOSED else t[0]


def _unshard2d(t, name):
    return (t.T if name in TRANSPOSED else t)[None]


def _rows3d(t):
    return t.transpose(2, 0, 1)


def _pack_small(d, prefix, shard_of_convw):
    flat = jnp.concatenate([d[prefix + n][0] for n in SMALL_ORDER] + [shard_of_convw.reshape(-1)])
    return jnp.pad(flat, (0, SMALL_LEN - flat.shape[0])).reshape(SMALL_ROWS, D)


def _unpack_small(block, like):
    flat = block.reshape(-1)
    out, off = {}, 0
    for n in SMALL_ORDER:
        size = like[n].shape[1]
        out[n] = flat[off:off + size].reshape(1, size)
        off += size
    out["conv_w"] = flat[off:off + CONV_K * CONVW_SH].reshape(1, CONV_K, CONVW_SH)
    return out


def kernel(x, positions, ffn1_pre_norm, ffn1_w_gate, ffn1_w_up, ffn1_w_down, ffn1_post_norm, mix_pre_norm, w_in, conv_w, conv_b, dt_bias, a_log, d_skip, ssm_norm, w_out, mix_post_norm, ffn2_pre_norm, ffn2_w_gate, ffn2_w_up, ffn2_w_down, ffn2_post_norm, loss_target, m_ffn1_pre_norm, m_ffn1_w_gate, m_ffn1_w_up, m_ffn1_w_down, m_ffn1_post_norm, m_mix_pre_norm, m_w_in, m_conv_w, m_conv_b, m_dt_bias, m_a_log, m_d_skip, m_ssm_norm, m_w_out, m_mix_post_norm, m_ffn2_pre_norm, m_ffn2_w_gate, m_ffn2_w_up, m_ffn2_w_down, m_ffn2_post_norm, v_ffn1_pre_norm, v_ffn1_w_gate, v_ffn1_w_up, v_ffn1_w_down, v_ffn1_post_norm, v_mix_pre_norm, v_w_in, v_conv_w, v_conv_b, v_dt_bias, v_a_log, v_d_skip, v_ssm_norm, v_w_out, v_mix_post_norm, v_ffn2_pre_norm, v_ffn2_w_gate, v_ffn2_w_up, v_ffn2_w_down, v_ffn2_post_norm):
    given = dict(locals())
    xi, yi = lax.axis_index("x"), lax.axis_index("y")

    shard = jnp.reshape(2 * xi + yi, (1,)).astype(jnp.int32)
    big = {p + n: _shard2d(given[p + n], n) for n in BIG for p in ("", "m_", "v_")}
    gu1 = _cast_stack("cast_ffn1_gate_up", shard, [big[n] for n in BIG[0:2]], 176, D)
    d1 = _cast_stack("cast_ffn1_down", shard, [big[BIG[2]]], 176, D)
    f2 = _cast_stack("cast_ffn2", shard, [big[n] for n in BIG[3:6]], 176, D)
    winsh = _cast_stack("cast_w_in", shard, [big["w_in"]], WIN_SH, 256).reshape(NSH, WIN_SH, D)
    woutsh = _cast_stack("cast_w_out", shard, [big["w_out"]], 256, D).reshape(NSH, 2 * D // NSH, D)
    comm = _Comm()
    (gu1,), (cwf,) = _run_riders("gather_ffn1_gate_up", [comm.gather(gu1), _small_gather_rider(conv_w[0])])
    convw = cwf.transpose(1, 0, 2).reshape(CONV_K, CONV_C)

    sp = {n: given[n] for n in SMALL_ORDER}
    grad_x, big_grads, small = _local_step(x[0], positions[0], loss_target[0], sp, gu1, d1, f2, winsh, woutsh,
                                           convw, comm)

    tot = _allreduce_small(small).reshape(-1)
    loss = tot[OFF_LOSS]
    small_grads, off = {}, 0
    for n in SMALL_ORDER:
        size = given[n].shape[1]
        small_grads[n] = tot[off:off + size].reshape(1, size)
        off += size
    dconvw = tot[OFF_CONVW:OFF_CONVW + CONV_K * CONV_C].reshape(CONV_K, NSH, CONVW_SH)
    dconvw = lax.dynamic_index_in_dim(dconvw, 2 * xi + yi, axis=1, keepdims=False)
    small_grads["conv_w"] = dconvw.reshape(1, CONV_K, CONVW_SH)

    upd = {}
    for names, tr in ((BIG[0:3], 176), (BIG[3:6], 176), (BIG[7:8], 256)):
        res = _adamw("adamw_" + names[0], [big[n] for n in names], [big_grads[n] for n in names],
                     [big["m_" + n] for n in names], [big["v_" + n] for n in names], tr, D)
        for n, r in zip(names, res):
            upd[n] = tuple(_unshard2d(t, n) for t in r)
    g_win = big_grads["w_in"].reshape(WIN_SH, 1, D)
    res, = _adamw("adamw_w_in", [_rows3d(w_in)], [g_win], [_rows3d(m_w_in)], [_rows3d(v_w_in)], WIN_SH // 4, D)
    upd["w_in"] = tuple(t.transpose(1, 2, 0) for t in res)
    (dl, m2, v2), = _adamw(
        "adamw_small", [_pack_small(given, "", conv_w[0])], [_pack_small(small_grads, "", dconvw)],
        [_pack_small(given, "m_", m_conv_w[0])], [_pack_small(given, "v_", v_conv_w[0])], SMALL_ROWS, D)
    dl, m2, v2 = (_unpack_small(t, given) for t in (dl, m2, v2))
    for n in SMALL_ORDER + ("conv_w",):
        upd[n] = (dl[n], m2[n], v2[n])

    grads = dict(small_grads)
    grads.update({n: _unshard2d(g, n) for n, g in big_grads.items() if n != "w_in"})
    grads["w_in"] = g_win.transpose(1, 2, 0)
    return (loss, grad_x[None], *[grads[n] for n in WEIGHTS], *[upd[n][0] for n in WEIGHTS],
            *[upd[n][1] for n in WEIGHTS], *[upd[n][2] for n in WEIGHTS])
```

```python
import functools
import typing

import jax
import jax.numpy as jnp
from jax import lax
from jax.experimental import pallas as pl
from jax.experimental.pallas import tpu as pltpu

F32 = jnp.float32
BF16 = jnp.bfloat16

S = 2048
D = 1024
FF = 2816
NSH = 4
FS = FF // NSH
HALF = D // 2
HD = 64
NKV = 4
NQ_PER_KV = 4
KVW = NKV * HD
QCOLS = NQ_PER_KV * HD
CONV_C = 1536
CONV_K = 4
SSM_W = 1024
NST = 128
NCH = S // 128
WIN_COLS = 4112
WIN_SH = WIN_COLS // NSH
WIN_PAD = 4224
COL_DT = 4096
EPS = 1e-6
NEG = -1e30

ADAM_LR = 0.001
ADAM_B1 = 0.9
ADAM_B2 = 0.999
ADAM_EPS = 1e-08
ADAM_WD = 0.01
ADAM_STEP = 10

VMEM_LIMIT = 56 * 1024 * 1024
TS = 512
TR = 256

NN = (((1,), (0,)), ((), ()))
NT = (((1,), (1,)), ((), ()))
TN = (((0,), (0,)), ((), ()))
MESH = pl.DeviceIdType.MESH


def _cparams(*sem):
    return pltpu.CompilerParams(dimension_semantics=sem, vmem_limit_bytes=VMEM_LIMIT)


def _dot(a, b, dims):
    return lax.dot_general(a.astype(BF16), b.astype(BF16), dims, preferred_element_type=F32)


def _bf16_pieces(v):
    hi = v.astype(BF16)
    rest = v - hi.astype(F32)
    mid = rest.astype(BF16)
    return hi, mid, (rest - mid.astype(F32)).astype(BF16)


def _dot_exact(a, b, ones="a"):
    if ones == "a":
        sel = a.astype(BF16)
        parts = [lax.dot_general(sel, p, NN, preferred_element_type=F32) for p in _bf16_pieces(b)]
    else:
        sel = b.astype(BF16)
        parts = [lax.dot_general(p, sel, NN, preferred_element_type=F32) for p in _bf16_pieces(a)]
    return (parts[2] + parts[1]) + parts[0]


def _sigmoid(v):
    return 1.0 / (1.0 + jnp.exp(-v))


class _Rider(typing.NamedTuple):
    operands: list
    out_shapes: list
    aliases: dict
    sems: list
    start: typing.Callable
    finish: typing.Callable


def _call(body, name, grid, in_specs, out_specs, out_shape, operands, scratch=(), sem=(), rider=None):
    multi = isinstance(out_shape, (list, tuple))
    if rider is None:
        return pl.pallas_call(
            body, name=name, grid=grid, in_specs=in_specs, out_specs=out_specs, out_shape=out_shape,
            scratch_shapes=list(scratch), compiler_params=_cparams(*sem))(*operands)
    outs = list(out_shape) if multi else [out_shape]
    ospecs = list(out_specs) if multi else [out_specs]
    n_in, n_out, n_scr = len(operands), len(outs), len(scratch)
    ri, ro = len(rider.operands), len(rider.out_shapes)

    def wrapped(*refs):
        o0 = n_in + ri
        s0 = o0 + n_out + ro
        rin, rout, rsem = refs[n_in:o0], refs[o0 + n_out:s0], refs[s0 + n_scr:]
        ids = [pl.program_id(a) for a in range(len(grid))]
        first = functools.reduce(jnp.logical_and, [i == 0 for i in ids])
        last = functools.reduce(jnp.logical_and, [i == g - 1 for i, g in zip(ids, grid)])

        @pl.when(first)
        def _():
            rider.start(rin, rout, rsem)

        body(*refs[:n_in], *refs[o0:o0 + n_out], *refs[s0:s0 + n_scr])

        @pl.when(last)
        def _():
            rider.finish(rin, rout, rsem)

    hbm = pl.BlockSpec(memory_space=pl.ANY)
    res = pl.pallas_call(
        wrapped, name=name, grid=grid, in_specs=list(in_specs) + [hbm] * ri, out_specs=ospecs + [hbm] * ro,
        out_shape=outs + list(rider.out_shapes), scratch_shapes=list(scratch) + list(rider.sems),
        input_output_aliases={n_in + k: n_out + v for k, v in rider.aliases.items()},
        compiler_params=_cparams(*(("arbitrary",) * len(grid))))(*operands, *rider.operands)
    main = list(res[:n_out])
    return (main if multi else main[0]), list(res[n_out:])


class _Tail(typing.NamedTuple):
    fn: typing.Callable
    operands: list
    in_specs: list


def _mm(name, operands, dims, grid, in_specs, o_spec, out_shape, rider=None, tail=None):
    npairs = len(operands) // 2
    extra = [] if tail is None else list(tail.operands)
    nin = 2 * npairs + len(extra)

    def body(*refs):
        t = None
        for i in range(npairs):
            a, b = refs[2 * i], refs[2 * i + 1]
            parts = [(a[s], b[s]) for s in range(a.shape[0])] if len(a.shape) == 3 else [(a[...], b[...])]
            for pa, pb in parts:
                d = _dot(pa, pb, dims)
                t = d if t is None else t + d
        if tail is None:
            refs[nin][...] = t.astype(refs[nin].dtype)
        else:
            tail.fn(t, refs[2 * npairs:nin], refs[nin:])

    sem = ("parallel" if tail is None else "arbitrary",) * len(grid)
    specs = list(in_specs) + ([] if tail is None else list(tail.in_specs))
    return _call(body, name, grid, specs, o_spec, out_shape, list(operands) + extra, (), sem, rider)


class _FfnW(typing.NamedTuple):
    gu: jax.Array
    g0: int
    dn: jax.Array
    d0: int


def _ffn_up(name, n, w, rider=None):
    def body(n_ref, wg_ref, wu_ref, fg_ref, fu_ref, a_ref):
        nb = n_ref[...]
        g = _dot(nb, wg_ref[...], NT)
        u = _dot(nb, wu_ref[...], NT)
        sg = _sigmoid(g)
        silu = g * sg
        fg_ref[...] = (u * (sg * (1.0 + g * (1.0 - sg)))).astype(BF16)
        fu_ref[...] = silu.astype(BF16)
        a_ref[...] = (silu * u).astype(BF16)

    out = jax.ShapeDtypeStruct((NSH, S, FS), BF16)
    ospec = pl.BlockSpec((None, TS, FS), lambda s, i: (s, i, 0))
    return _call(
        body, name, (NSH, S // TS),
        [pl.BlockSpec((TS, D), lambda s, i: (i, 0)),
         pl.BlockSpec((None, None, FS, D), lambda s, i: (s, w.g0, 0, 0)),
         pl.BlockSpec((None, None, FS, D), lambda s, i: (s, w.g0 + 1, 0, 0))],
        [ospec, ospec, ospec], [out, out, out], (n, w.gu, w.gu), sem=("parallel", "parallel"), rider=rider)


def _ffn_dact(name, dh, w, fgate, fup, rider=None):
    def body(dh_ref, wd_ref, fg_ref, fu_ref, dg_ref, du_ref):
        da = _dot(dh_ref[...], wd_ref[...], NT)
        dg_ref[...] = (da * fg_ref[...].astype(F32)).astype(BF16)
        du_ref[...] = (da * fu_ref[...].astype(F32)).astype(BF16)

    out = jax.ShapeDtypeStruct((NSH, S, FS), BF16)
    aspec = pl.BlockSpec((None, TS, FS), lambda s, i: (s, i, 0))
    return _call(
        body, name, (NSH, S // TS),
        [pl.BlockSpec((TS, D), lambda s, i: (i, 0)),
         pl.BlockSpec((None, None, FS, D), lambda s, i: (s, w.d0, 0, 0)), aspec, aspec],
        [aspec, aspec], [out, out], (dh, w.dn, fgate, fup), sem=("parallel", "parallel"), rider=rider)


def _rstd(v):
    return lax.rsqrt(jnp.mean(v * v, axis=-1, keepdims=True) + EPS)


def _row_spec():
    return pl.BlockSpec((TR, D), lambda i: (i, 0))


def _vec_spec():
    return pl.BlockSpec((1, D), lambda i: (0, 0))


def _acc_rows(ref, v):
    @pl.when(pl.program_id(0) == 0)
    def _():
        ref[...] = jnp.zeros_like(ref)
    ref[...] += jnp.sum(v, axis=0, keepdims=True)


def _prenorm(name, x, g):
    def body(x_ref, g_ref, n_ref):
        xv = x_ref[...]
        n_ref[...] = (xv * _rstd(xv) * g_ref[...]).astype(BF16)

    return pl.pallas_call(
        body, name=name, grid=(S // TR,), in_specs=[_row_spec(), _vec_spec()], out_specs=_row_spec(),
        out_shape=jax.ShapeDtypeStruct((S, D), BF16), compiler_params=_cparams("parallel"),
    )(x, g)


def _rows_spec(rows):
    return pl.BlockSpec((rows, D), lambda i: (i, 0))


def _rows_f32():
    return jax.ShapeDtypeStruct((S, D), F32)


def _rows_bf16():
    return jax.ShapeDtypeStruct((S, D), BF16)


def _vec_f32():
    return jax.ShapeDtypeStruct((1, D), F32)


def _tail_postres(rows, x, p, alpha, gnext):
    def fn(h, ins, outs):
        x_ref, p_ref, g_ref = ins
        h_ref, xo_ref, n_ref = outs
        h_ref[...] = h
        xo = x_ref[...] + alpha * (h * _rstd(h) * p_ref[...])
        xo_ref[...] = xo
        n_ref[...] = (xo * _rstd(xo) * g_ref[...]).astype(BF16)

    rs = _rows_spec(rows)
    return (_Tail(fn, [x, p, gnext], [rs, _vec_spec(), _vec_spec()]), [rs, rs, rs],
            [_rows_f32(), _rows_f32(), _rows_bf16()])


def _tail_final(rows, x, p, tgt, alpha):
    def fn(h, ins, outs):
        x_ref, p_ref, t_ref = ins
        dy_ref, dh_ref, dp_ref, loss_ref = outs
        r = _rstd(h)
        hn = h * r
        pv = p_ref[...]
        e = x_ref[...] + alpha * (hn * pv) - t_ref[...]
        dy = e * (1.0 / D)
        dy_ref[...] = dy
        du = alpha * dy * pv
        dh_ref[...] = (r * (du - hn * jnp.mean(du * hn, axis=-1, keepdims=True))).astype(BF16)
        _acc_rows(dp_ref, alpha * dy * hn)
        part = 0.5 * jnp.sum(jnp.mean(e * e, axis=-1, keepdims=True), axis=0, keepdims=True)
        _acc_rows(loss_ref, jnp.broadcast_to(part, (1, 128)))

    rs = _rows_spec(rows)
    return (_Tail(fn, [x, p, tgt], [rs, _vec_spec(), rs]),
            [rs, rs, _vec_spec(), pl.BlockSpec((1, 128), lambda i: (0, 0))],
            [_rows_f32(), _rows_bf16(), _vec_f32(), jax.ShapeDtypeStruct((1, 128), F32)])


def _norm_bwd(dn, xv, g_ref, dg_ref):
    r = _rstd(xv)
    xn = xv * r
    dng = dn * g_ref[...]
    _acc_rows(dg_ref, dn * xn)
    return r * (dng - xn * jnp.mean(dng * xn, axis=-1, keepdims=True))


def _tail_mid_bwd(rows, dres, x, g, h, p, alpha):
    def fn(dn, ins, outs):
        dr_ref, x_ref, g_ref, h_ref, p_ref = ins
        dx_ref, dh_ref, dg_ref, dp_ref = outs
        dx = dr_ref[...] + _norm_bwd(dn, x_ref[...], g_ref, dg_ref)
        dx_ref[...] = dx
        hv = h_ref[...]
        r = _rstd(hv)
        hn = hv * r
        du = alpha * dx * p_ref[...]
        dh_ref[...] = (r * (du - hn * jnp.mean(du * hn, axis=-1, keepdims=True))).astype(BF16)
        _acc_rows(dp_ref, alpha * dx * hn)

    rs = _rows_spec(rows)
    return (_Tail(fn, [dres, x, g, h, p], [rs, rs, _vec_spec(), rs, _vec_spec()]),
            [rs, rs, _vec_spec(), _vec_spec()], [_rows_f32(), _rows_bf16(), _vec_f32(), _vec_f32()])


def _tail_first_bwd(rows, dres, x, g):
    def fn(dn, ins, outs):
        dr_ref, x_ref, g_ref = ins
        dx_ref, dg_ref = outs
        dx_ref[...] = dr_ref[...] + _norm_bwd(dn, x_ref[...], g_ref, dg_ref)

    rs = _rows_spec(rows)
    return (_Tail(fn, [dres, x, g], [rs, rs, _vec_spec()]), [rs, _vec_spec()], [_rows_f32(), _vec_f32()])


def _rotate(t, c128, s128, sign, scale):
    width = t.shape[1]
    c = jnp.tile(c128, (1, width // 128))
    sn = jnp.tile(s128, (1, width // 128))
    lane = lax.broadcasted_iota(jnp.int32, t.shape, 1) & (HD - 1)
    rot = jnp.where(lane < HD // 2, -pltpu.roll(t, width - HD // 2, 1), pltpu.roll(t, HD // 2, 1))
    return (t * c + sign * (rot * sn)) * scale


def _rows_to_blocks(y):
    out = []
    for j in range(NKV):
        yt = y[:, QCOLS * j:QCOLS * (j + 1)].T
        out.append(jnp.concatenate([yt[HD * g:HD * (g + 1)] for g in range(NQ_PER_KV)], axis=1))
    return out


def _blocks_to_rows(blocks):
    cols = []
    for b in blocks:
        stacked = jnp.concatenate([b[:, 128 * g:128 * (g + 1)] for g in range(NQ_PER_KV)], axis=0)
        cols.append(stacked.T)
    return jnp.concatenate(cols, axis=1)


def _rope_q(proj, cos, sin):
    def body(t_ref, c_ref, s_ref, o_ref):
        y = _rotate(t_ref[...], c_ref[...], s_ref[...], 1.0, HD ** -0.5)
        for j, blk in enumerate(_rows_to_blocks(y)):
            o_ref[j] = blk.astype(BF16)

    return pl.pallas_call(
        body, name="rope_q", grid=(NCH,),
        in_specs=[pl.BlockSpec((128, D), lambda i: (i, 0)),
                  pl.BlockSpec((128, 128), lambda i: (i, 0)), pl.BlockSpec((128, 128), lambda i: (i, 0))],
        out_specs=pl.BlockSpec((NKV, None, HD, QROWS), lambda i: (0, i, 0, 0)),
        out_shape=jax.ShapeDtypeStruct((NKV, NCH, HD, QROWS), BF16), compiler_params=_cparams("parallel"),
    )(proj, cos, sin)


def _rope_dq(dqt, cos, sin, dproj):
    def body(t_ref, c_ref, s_ref, buf_ref, o_ref):
        t = _blocks_to_rows([t_ref[j] for j in range(NKV)])
        o_ref[...] = _rotate(t, c_ref[...], s_ref[...], -1.0, HD ** -0.5).astype(BF16)

    return pl.pallas_call(
        body, name="rope_dq", grid=(NCH,),
        in_specs=[pl.BlockSpec((NKV, None, HD, QROWS), lambda i: (0, i, 0, 0)),
                  pl.BlockSpec((128, 128), lambda i: (i, 0)), pl.BlockSpec((128, 128), lambda i: (i, 0)),
                  pl.BlockSpec(memory_space=pl.ANY)],
        out_specs=pl.BlockSpec((128, D), lambda i: (i, 0)),
        out_shape=jax.ShapeDtypeStruct(dproj.shape, BF16), input_output_aliases={3: 0},
        compiler_params=_cparams("parallel"),
    )(dqt, cos, sin, dproj)


def _rope(name, src, col_block, width, cos, sin, sign, scale, into=None):
    def body(t_ref, c_ref, s_ref, *rest):
        rest[-1][...] = _rotate(t_ref[...].astype(F32), c_ref[...], s_ref[...], sign, scale).astype(BF16)

    in_specs = [pl.BlockSpec((TR, width), lambda i: (i, col_block)),
                pl.BlockSpec((TR, 128), lambda i: (i, 0)), pl.BlockSpec((TR, 128), lambda i: (i, 0))]
    if into is None:
        return pl.pallas_call(
            body, name=name, grid=(S // TR,), in_specs=in_specs,
            out_specs=pl.BlockSpec((TR, width), lambda i: (i, 0)),
            out_shape=jax.ShapeDtypeStruct((S, width), BF16), compiler_params=_cparams("parallel"),
        )(src, cos, sin)
    buf, out_block = into
    return pl.pallas_call(
        body, name=name, grid=(S // TR,), in_specs=in_specs + [pl.BlockSpec(memory_space=pl.ANY)],
        out_specs=pl.BlockSpec((TR, width), lambda i: (i, out_block)),
        out_shape=jax.ShapeDtypeStruct(buf.shape, BF16), input_output_aliases={3: 0},
        compiler_params=_cparams("parallel"),
    )(src, cos, sin, buf)


QROWS = NQ_PER_KV * 128


NBIAS = NCH + 1
KV_PER_STEP = 4


def _bias_table():
    db = lax.broadcasted_iota(jnp.int32, (NBIAS, 128, QROWS), 0) - 1
    ki = lax.broadcasted_iota(jnp.int32, (NBIAS, 128, QROWS), 1)
    qi = lax.broadcasted_iota(jnp.int32, (NBIAS, 128, QROWS), 2) & 127
    d = db * 128 + qi - ki
    cnt = ((d <= 128).astype(F32) + (((d & 3) == 0) & (d <= 512)).astype(F32) + ((d & 15) == 0).astype(F32))
    return jnp.where((d >= 0) & (cnt > 0.0), jnp.log(jnp.maximum(cnt, 1.0)), NEG)


def _qt_spec():
    return pl.BlockSpec((None, None, HD, QROWS), lambda j, i: (j, i, 0, 0))


def _stat_spec():
    return pl.BlockSpec((None, None, 1, QROWS), lambda j, i: (j, i, 0, 0))


def _attn_fwd(qt, kh, vt, bias, rider=None):
    def body(q_ref, k_ref, v_ref, b_ref, o_ref, lse_ref, rows_ref):
        qb = pl.program_id(1)

        def keys(carry, off, size, bias_):
            out = []
            for h in range(KV_PER_STEP):
                m, l, acc = carry[3 * h:3 * h + 3]
                s = _dot(k_ref[h, pl.ds(off, size), :], q_ref[h], NN) + bias_
                m_new = jnp.maximum(m, jnp.max(s, axis=0, keepdims=True))
                p = jnp.exp(s - m_new)
                a = jnp.exp(m - m_new)
                out += [m_new, a * l + jnp.sum(p, axis=0, keepdims=True),
                        a * acc + _dot(v_ref[h, :, pl.ds(off, size)], p, NN)]
            return tuple(out)

        def pair(i, carry):
            bias2 = jnp.concatenate([b_ref[qb - 2 * i + 1], b_ref[qb - 2 * i]], axis=0)
            return keys(carry, pl.multiple_of(i * 256, 256), 256, bias2)

        init = (jnp.full((1, QROWS), NEG, F32), jnp.zeros((1, QROWS), F32), jnp.zeros((HD, QROWS), F32))
        res = lax.fori_loop(0, (qb + 1) // 2, pair, init * KV_PER_STEP)
        res = lax.cond(qb % 2 == 0,
                       lambda c: keys(c, pl.multiple_of(qb * 128, 128), 128, b_ref[1]), lambda c: c, res)
        outs = []
        for h in range(KV_PER_STEP):
            m, l, acc = res[3 * h:3 * h + 3]
            outs.append(acc / l)
            o_ref[h] = outs[h]
            lse_ref[h] = m + jnp.log(l)
        rows_ref[...] = _blocks_to_rows(outs).astype(BF16)

    kvs = KV_PER_STEP
    qspec = pl.BlockSpec((kvs, None, HD, QROWS), lambda j, i: (j, i, 0, 0))
    return _call(
        body, "attn_fwd", (NKV // kvs, NCH),
        [qspec, pl.BlockSpec((kvs, S, HD), lambda j, i: (j, 0, 0)),
         pl.BlockSpec((kvs, HD, S), lambda j, i: (j, 0, 0)),
         pl.BlockSpec((NBIAS, 128, QROWS), lambda j, i: (0, 0, 0))],
        [qspec, pl.BlockSpec((kvs, None, 1, QROWS), lambda j, i: (j, i, 0, 0)),
         pl.BlockSpec((128, QCOLS * kvs), lambda j, i: (i, j))],
        [jax.ShapeDtypeStruct((NKV, NCH, HD, QROWS), F32), jax.ShapeDtypeStruct((NKV, NCH, 1, QROWS), F32),
         jax.ShapeDtypeStruct((S, D), BF16)],
        (qt, kh, vt, bias), sem=("parallel", "parallel"), rider=rider)


def _attn_delta(ot, dot_):
    def body(o_ref, do_ref, dl_ref):
        dl_ref[...] = jnp.sum(o_ref[...] * do_ref[...].astype(F32), axis=1, keepdims=True)

    spec = pl.BlockSpec((None, NCH, HD, QROWS), lambda j: (j, 0, 0, 0))
    return pl.pallas_call(
        body, name="attn_delta", grid=(NKV,), in_specs=[spec, spec],
        out_specs=pl.BlockSpec((None, NCH, 1, QROWS), lambda j: (j, 0, 0, 0)),
        out_shape=jax.ShapeDtypeStruct((NKV, NCH, 1, QROWS), F32), compiler_params=_cparams("parallel"),
    )(ot, dot_)


def _attn_bwd(qt, kh, kt, vh, dot_, lse, delta, bias):
    def body(qt_ref, k_ref, kt_ref, v_ref, dot_ref, lse_ref, dl_ref, b_ref, dq_ref, dk_ref, dv_ref):
        kb = pl.program_id(1)

        @pl.when(kb == 0)
        def _():
            dq_ref[...] = jnp.zeros_like(dq_ref)

        def blocks(carry, qbs):
            out = list(carry)
            for h in range(KV_PER_STEP):
                k, kt_, v = k_ref[h], kt_ref[h], v_ref[h]
                for qb in qbs:
                    st = _dot(k, qt_ref[h, qb], NN) + b_ref[qb - kb + 1]
                    pt = jnp.exp(st - lse_ref[h, qb])
                    dst = pt * (_dot(v, dot_ref[h, qb], NN) - dl_ref[h, qb])
                    dq_ref[h, qb] += _dot(kt_, dst, NN)
                    out[2 * h] = out[2 * h] + _dot(dst, qt_ref[h, qb], NT)
                    out[2 * h + 1] = out[2 * h + 1] + _dot(pt, dot_ref[h, qb], NT)
            return tuple(out)

        res = (jnp.zeros((128, HD), F32),) * (2 * KV_PER_STEP)
        res = lax.cond(kb % 2 == 1, lambda c: blocks(c, (kb,)), lambda c: c, res)
        res = lax.fori_loop((kb + 1) // 2, NCH // 2, lambda j, c: blocks(c, (2 * j, 2 * j + 1)), res)
        for h in range(KV_PER_STEP):
            dk_ref[h] = res[2 * h]
            dv_ref[h] = res[2 * h + 1]

    kvs = KV_PER_STEP
    tspec = pl.BlockSpec((kvs, NCH, HD, QROWS), lambda j, i: (j, 0, 0, 0))
    kspec = pl.BlockSpec((kvs, 128, HD), lambda j, i: (j, i, 0))
    sspec = pl.BlockSpec((kvs, NCH, 1, QROWS), lambda j, i: (j, 0, 0, 0))
    return pl.pallas_call(
        body, name="attn_bwd", grid=(NKV // kvs, NCH),
        in_specs=[tspec, kspec, pl.BlockSpec((kvs, HD, 128), lambda j, i: (j, 0, i)), kspec, tspec,
                  sspec, sspec, pl.BlockSpec((NBIAS, 128, QROWS), lambda j, i: (0, 0, 0))],
        out_specs=[tspec, kspec, kspec],
        out_shape=[jax.ShapeDtypeStruct((NKV, NCH, HD, QROWS), F32),
                   jax.ShapeDtypeStruct((NKV, S, HD), F32), jax.ShapeDtypeStruct((NKV, S, HD), F32)],
        compiler_params=_cparams("parallel", "arbitrary"),
    )(qt, kh, kt, vh, dot_, lse, delta, bias)


CONV_BLK = 256
CONV_COL0 = 1536 // CONV_BLK


def _shift_down(u, j, row):
    return jnp.where(row >= j, pltpu.roll(u, j, 0), 0.0)


def _conv_pre(u, w_ref, b_ref, row):
    y = b_ref[...] + w_ref[CONV_K - 1:CONV_K, :] * u
    for j in range(1, CONV_K):
        y = y + w_ref[CONV_K - 1 - j:CONV_K - j, :] * _shift_down(u, j, row)
    return y


def _conv_fwd(proj, convw, convb):
    def body(u_ref, w_ref, b_ref, o_ref):
        u = u_ref[...]
        row = lax.broadcasted_iota(jnp.int32, u.shape, 0)
        y = _conv_pre(u, w_ref, b_ref, row)
        o_ref[...] = y * _sigmoid(y)

    return pl.pallas_call(
        body, name="conv_fwd", grid=(CONV_C // CONV_BLK,),
        in_specs=[pl.BlockSpec((S, CONV_BLK), lambda i: (0, CONV_COL0 + i)),
                  pl.BlockSpec((CONV_K, CONV_BLK), lambda i: (0, i)),
                  pl.BlockSpec((1, CONV_BLK), lambda i: (0, i))],
        out_specs=pl.BlockSpec((S, CONV_BLK), lambda i: (0, i)),
        out_shape=jax.ShapeDtypeStruct((S, CONV_C), F32), compiler_params=_cparams("parallel"),
    )(proj, convw, convb)


def _conv_bwd(dact, proj, convw, convb, dproj):
    def body(da_ref, u_ref, w_ref, b_ref, buf_ref, du_ref, dw_ref, db_ref):
        u = u_ref[...]
        row = lax.broadcasted_iota(jnp.int32, u.shape, 0)
        y = _conv_pre(u, w_ref, b_ref, row)
        sg = _sigmoid(y)
        dy = da_ref[...] * (sg * (1.0 + y * (1.0 - sg)))
        db_ref[...] = jnp.sum(dy, axis=0, keepdims=True)
        du = w_ref[CONV_K - 1:CONV_K, :] * dy
        r8 = lax.broadcasted_iota(jnp.int32, (8, CONV_BLK), 0)
        dw = jnp.where(r8 == CONV_K - 1, jnp.sum(dy * u, axis=0, keepdims=True), 0.0)
        for j in range(1, CONV_K):
            du = du + w_ref[CONV_K - 1 - j:CONV_K - j, :] * jnp.where(row < S - j, pltpu.roll(dy, S - j, 0), 0.0)
            dw = dw + jnp.where(r8 == CONV_K - 1 - j,
                                jnp.sum(dy * _shift_down(u, j, row), axis=0, keepdims=True), 0.0)
        du_ref[...] = du.astype(BF16)
        dw_ref[...] = dw

    return pl.pallas_call(
        body, name="conv_bwd", grid=(CONV_C // CONV_BLK,),
        in_specs=[pl.BlockSpec((S, CONV_BLK), lambda i: (0, i)),
                  pl.BlockSpec((S, CONV_BLK), lambda i: (0, CONV_COL0 + i)),
                  pl.BlockSpec((CONV_K, CONV_BLK), lambda i: (0, i)),
                  pl.BlockSpec((1, CONV_BLK), lambda i: (0, i)), pl.BlockSpec(memory_space=pl.ANY)],
        out_specs=[pl.BlockSpec((S, CONV_BLK), lambda i: (0, CONV_COL0 + i)),
                   pl.BlockSpec((8, CONV_BLK), lambda i: (0, i)), pl.BlockSpec((1, CONV_BLK), lambda i: (0, i))],
        out_shape=[jax.ShapeDtypeStruct(dproj.shape, BF16), jax.ShapeDtypeStruct((8, CONV_C), F32),
                   jax.ShapeDtypeStruct((1, CONV_C), F32)],
        input_output_aliases={4: 0}, compiler_params=_cparams("parallel"),
    )(dact, proj, convw, convb, dproj)


NPAIR = 8


def _ssd_scalars(dtr_ref, dtb_ref, alog_ref):
    z = dtr_ref[...] + dtb_ref[...]
    dt = jnp.maximum(z, 0.0) + jnp.log(1.0 + jnp.exp(-jnp.abs(z)))
    a = -jnp.exp(alog_ref[...])
    r = lax.broadcasted_iota(jnp.int32, (128, 128), 0)
    c = lax.broadcasted_iota(jnp.int32, (128, 128), 1)
    tri = (r >= c).astype(F32)
    cs = _dot_exact(tri, dt * a)
    return z, dt, a, cs, r, c


def _by_lane(cs, dt):
    head = lax.broadcasted_iota(jnp.int32, (128, SSM_W), 0)
    lane = lax.broadcasted_iota(jnp.int32, (128, SSM_W), 1)
    sel = (head == lane // HD).astype(F32)
    cs_l = _dot_exact(cs, sel, "b")
    last_l = cs_l[127:128, :]
    return sel, jnp.exp(cs_l), jnp.exp(last_l - cs_l), _dot_exact(dt, sel, "b")


def _pair_terms(cs, h1, h2):
    return (cs[:, h1:h1 + 1], cs[:, h2:h2 + 1],
            jnp.exp(cs[127:128, h1:h1 + 1]), jnp.exp(cs[127:128, h2:h2 + 1]))


def _gate_norm(y, zv, w):
    yg = y * (zv * _sigmoid(zv))
    outs, rs = [], []
    for g in range(2):
        blk = yg[:, 512 * g:512 * (g + 1)]
        r = lax.rsqrt(jnp.mean(blk * blk, axis=-1, keepdims=True) + EPS)
        outs.append(blk * r)
        rs.append(r)
    return jnp.concatenate(outs, axis=1), rs, yg


def _ssd_fwd(xbc, proj, dtb, alog, dskip_l, ssmw):
    def body(x_ref, b_ref, c_ref, dtr_ref, z_ref, dtb_ref, alog_ref, dsk_ref, w_ref, y_ref, yn_ref, hp_ref, h_ref):
        @pl.when(pl.program_id(0) == 0)
        def _():
            h_ref[...] = jnp.zeros_like(h_ref)

        _, dt, _, cs, r, c = _ssd_scalars(dtr_ref, dtb_ref, alog_ref)
        cst = cs.T
        causal = r >= c
        lo = c < HD
        _, e_all, dte_all, dt_all = _by_lane(cs, dt)
        hp_ref[...] = h_ref[...]
        for g in range(2):
            bg = b_ref[:, 128 * g:128 * (g + 1)]
            cg = c_ref[:, 128 * g:128 * (g + 1)]
            cb = _dot(cg, bg, NT)
            for j in range(4):
                pj = 4 * g + j
                h1, h2 = 2 * pj, 2 * pj + 1
                sl = slice(128 * pj, 128 * (pj + 1))
                xp = x_ref[:, sl]
                c1, c2, cd1, cd2 = _pair_terms(cs, h1, h2)
                e_l, dte_l = e_all[:, sl], dte_all[:, sl]
                xdt = xp * dt_all[:, sl]
                m1 = cb * jnp.exp(jnp.where(causal, c1 - cst[h1:h1 + 1, :], NEG))
                m2 = cb * jnp.exp(jnp.where(causal, c2 - cst[h2:h2 + 1, :], NEG))
                yd = jnp.where(lo, _dot(m1, xdt, NN), _dot(m2, xdt, NN))
                hp = h_ref[pj]
                yo = _dot(cg, hp, NT) * e_l
                st = _dot(xdt * dte_l, bg, TN)
                h_ref[pj] = hp * jnp.where(r < HD, cd1, cd2) + st
                y_ref[:, sl] = yd + yo + dsk_ref[:, sl] * xp
        yn, _, _ = _gate_norm(y_ref[...], z_ref[...], w_ref[...])
        yn_ref[...] = (yn * w_ref[...]).astype(BF16)

    return pl.pallas_call(
        body, name="ssd_fwd", grid=(NCH,),
        in_specs=[pl.BlockSpec((128, SSM_W), lambda i: (i, 0)),
                  pl.BlockSpec((128, 256), lambda i: (i, 4)), pl.BlockSpec((128, 256), lambda i: (i, 5)),
                  pl.BlockSpec((128, 128), lambda i: (i, COL_DT // 128)),
                  pl.BlockSpec((128, SSM_W), lambda i: (i, 3)),
                  pl.BlockSpec((1, 128), lambda i: (0, 0)), pl.BlockSpec((1, 128), lambda i: (0, 0)),
                  pl.BlockSpec((1, SSM_W), lambda i: (0, 0)), pl.BlockSpec((1, SSM_W), lambda i: (0, 0))],
        out_specs=[pl.BlockSpec((128, SSM_W), lambda i: (i, 0)), pl.BlockSpec((128, SSM_W), lambda i: (i, 0)),
                   pl.BlockSpec((None, NPAIR, 128, 128), lambda i: (i, 0, 0, 0))],
        out_shape=[jax.ShapeDtypeStruct((S, SSM_W), F32), jax.ShapeDtypeStruct((S, SSM_W), BF16),
                   jax.ShapeDtypeStruct((NCH, NPAIR, 128, 128), F32)],
        scratch_shapes=[pltpu.VMEM((NPAIR, 128, 128), F32)],
        compiler_params=_cparams("arbitrary"),
    )(xbc, xbc, xbc, proj, proj, dtb, alog, dskip_l, ssmw)


def _ssd_bwd(dmixed, y, xbc, proj, hprev, dtb, alog, dskip_l, ssmw, rider=None):
    def body(dyn_ref, y_ref, x_ref, b_ref, c_ref, dtr_ref, z_ref, hp_ref, dtb_ref, alog_ref, dsk_ref, w_ref,
             dxbc_ref, dz_ref, ddt_ref, dw_ref, dsc_ref, g_ref):
        @pl.when(pl.program_id(0) == 0)
        def _():
            g_ref[...] = jnp.zeros_like(g_ref)
            dsc_ref[...] = jnp.zeros_like(dsc_ref)

        z, dt, a, cs, r, c = _ssd_scalars(dtr_ref, dtb_ref, alog_ref)
        cst = cs.T
        causal = r >= c
        lo = c < HD

        yv = y_ref[...]
        zv = z_ref[...]
        wv = w_ref[...]
        ygn, rs, yg = _gate_norm(yv, zv, wv)
        dyn = dyn_ref[...]
        _acc_rows(dw_ref, dyn * ygn)
        dynw = dyn * wv
        parts = []
        for g in range(2):
            sl = slice(512 * g, 512 * (g + 1))
            a_g, n_g = dynw[:, sl], ygn[:, sl]
            parts.append(rs[g] * (a_g - n_g * jnp.mean(a_g * n_g, axis=-1, keepdims=True)))
        dyg = jnp.concatenate(parts, axis=1)
        sz = _sigmoid(zv)
        dz_ref[...] = (dyg * yv * (sz * (1.0 + zv * (1.0 - sz)))).astype(BF16)
        dy_all = dyg * (zv * sz)

        dcs_cols = jnp.zeros((128, 128), F32)
        dcs_rows = jnp.zeros((128, 128), F32)
        sel, e_all, dte_all, dt_all = _by_lane(cs, dt)
        x_all, b_all, c_all, dsk_all = x_ref[...], b_ref[...], c_ref[...], dsk_ref[...]
        hp_all, g_all = hp_ref[...], g_ref[...]
        g_new, dx_parts, db_parts, dc_parts = [], [], [], []
        dyx_parts, ryo_parts, qx_parts, dxx_parts, gh_parts = [], [], [], [], []
        for g in range(2):
            bg = b_all[:, 128 * g:128 * (g + 1)]
            cg = c_all[:, 128 * g:128 * (g + 1)]
            cb = _dot(cg, bg, NT)
            dcb = jnp.zeros((128, 128), F32)
            db_acc = jnp.zeros((128, NST), F32)
            dc_acc = jnp.zeros((128, NST), F32)
            for j in range(4):
                pj = 4 * g + j
                h1, h2 = 2 * pj, 2 * pj + 1
                sl = slice(128 * pj, 128 * (pj + 1))
                xp = x_all[:, sl]
                dyp = dy_all[:, sl]
                c1, c2, cd1, cd2 = _pair_terms(cs, h1, h2)
                e_l, dte_l, dt_l = e_all[:, sl], dte_all[:, sl], dt_all[:, sl]
                xdt = xp * dt_l
                hp = hp_all[pj]
                gp = g_all[pj]
                dyx_parts.append(dyp * xp)
                dzs = dyp * e_l
                dc_acc = dc_acc + _dot(dzs, hp, NN)
                ryo_parts.append(dyp * (_dot(cg, hp, NT) * e_l))
                qm = _dot(bg, gp, NT)
                dxdt = qm * dte_l
                qx_parts.append(qm * xdt)
                db_acc = db_acc + _dot(xdt * dte_l, gp, NN)
                gh_parts.append(gp * hp)
                g_new.append(_dot(dzs, cg, TN) + jnp.where(r < HD, cd1, cd2) * gp)
                for hh, ch, msk in ((h1, c1, lo), (h2, c2, jnp.logical_not(lo))):
                    lm = jnp.exp(jnp.where(causal, ch - cst[hh:hh + 1, :], NEG))
                    mm = cb * lm
                    dm = jnp.where(causal, _dot(jnp.where(msk, dyp, 0.0), xdt, NT), 0.0)
                    w = dm * mm
                    dcs_cols = dcs_cols + jnp.where(c == hh, jnp.sum(w, axis=1, keepdims=True), 0.0)
                    dcs_rows = dcs_rows + jnp.where(r == hh, jnp.sum(w, axis=0, keepdims=True), 0.0)
                    dcb = dcb + dm * lm
                    dxdt = dxdt + jnp.where(msk, _dot(mm, dyp, TN), 0.0)
                dxx_parts.append(dxdt * xp)
                dx_parts.append(dsk_all[:, sl] * dyp + dxdt * dt_l)
            db_parts.append(db_acc + _dot(dcb, cg, TN))
            dc_parts.append(dc_acc + _dot(dcb, bg, NN))
        g_ref[...] = jnp.stack(g_new)
        dxbc_ref[...] = jnp.concatenate(dx_parts + db_parts + dc_parts, axis=1)

        selt = (lax.broadcasted_iota(jnp.int32, (SSM_W, 128), 0) // HD
                == lax.broadcasted_iota(jnp.int32, (SSM_W, 128), 1)).astype(F32)

        def by_head(parts):
            return _dot_exact(jnp.concatenate(parts, axis=1), selt, "b")

        ddt_x = by_head(dxx_parts)
        dd_row = jnp.sum(by_head(dyx_parts), axis=0, keepdims=True)
        t_all = by_head(qx_parts) * jnp.exp(cs[127:128, :] - cs)
        gh = jnp.sum(_dot_exact(sel, jnp.concatenate(gh_parts, axis=0)), axis=1, keepdims=True)
        gh_row = jnp.broadcast_to(gh, (128, 128)).T[0:1, :]
        at_end = jnp.sum(t_all, axis=0, keepdims=True) + gh_row * jnp.exp(cs[127:128, :])
        dcs = by_head(ryo_parts) - t_all + dcs_cols + jnp.where(r == 127, at_end, 0.0) - dcs_rows.T
        dad = _dot_exact((c >= r).astype(F32), dcs)
        ddt = dad * a + ddt_x
        ddtr = jnp.where(c < 16, ddt * _sigmoid(z), 0.0)
        ddt_ref[...] = ddtr.astype(BF16)
        r8 = lax.broadcasted_iota(jnp.int32, (8, 128), 0)
        dsc_ref[...] += (jnp.where(r8 == 0, jnp.sum(ddtr, axis=0, keepdims=True), 0.0)
                         + jnp.where(r8 == 1, jnp.sum(dad * dt, axis=0, keepdims=True) * a, 0.0)
                         + jnp.where(r8 == 2, dd_row, 0.0))

    rev = NCH - 1
    return _call(
        body, "ssd_bwd", (NCH,),
        [pl.BlockSpec((128, SSM_W), lambda i: (rev - i, 0)),
         pl.BlockSpec((128, SSM_W), lambda i: (rev - i, 0)),
         pl.BlockSpec((128, SSM_W), lambda i: (rev - i, 0)),
         pl.BlockSpec((128, 256), lambda i: (rev - i, 4)), pl.BlockSpec((128, 256), lambda i: (rev - i, 5)),
         pl.BlockSpec((128, 128), lambda i: (rev - i, COL_DT // 128)),
         pl.BlockSpec((128, SSM_W), lambda i: (rev - i, 3)),
         pl.BlockSpec((None, NPAIR, 128, 128), lambda i: (rev - i, 0, 0, 0)),
         pl.BlockSpec((1, 128), lambda i: (0, 0)), pl.BlockSpec((1, 128), lambda i: (0, 0)),
         pl.BlockSpec((1, SSM_W), lambda i: (0, 0)), pl.BlockSpec((1, SSM_W), lambda i: (0, 0))],
        [pl.BlockSpec((128, CONV_C), lambda i: (rev - i, 0)),
         pl.BlockSpec((128, SSM_W), lambda i: (rev - i, 3)),
         pl.BlockSpec((128, 128), lambda i: (rev - i, 0)),
         pl.BlockSpec((1, SSM_W), lambda i: (0, 0)), pl.BlockSpec((8, 128), lambda i: (0, 0))],
        [jax.ShapeDtypeStruct((S, CONV_C), F32), jax.ShapeDtypeStruct((S, WIN_PAD), BF16),
         jax.ShapeDtypeStruct((S, 128), BF16), jax.ShapeDtypeStruct((1, SSM_W), F32),
         jax.ShapeDtypeStruct((8, 128), F32)],
        (dmixed, y, xbc, xbc, xbc, proj, proj, hprev, dtb, alog, dskip_l, ssmw),
        [pltpu.VMEM((NPAIR, 128, 128), F32)], ("arbitrary",), rider)


def _cast_stack(name, slot, arrs, tr, tc):
    n = len(arrs)
    rows, cols = arrs[0].shape

    def body(s_ref, *refs):
        for i in range(n):
            refs[n][i] = refs[i][...].astype(BF16)

    return pl.pallas_call(
        body, name=name,
        grid_spec=pltpu.PrefetchScalarGridSpec(
            num_scalar_prefetch=1, grid=(rows // tr, cols // tc),
            in_specs=[pl.BlockSpec((tr, tc), lambda i, j, sr: (i, j))] * n,
            out_specs=pl.BlockSpec((None, n, tr, tc), lambda i, j, sr: (sr[0], 0, i, j))),
        out_shape=jax.ShapeDtypeStruct((NSH, n, rows, cols), BF16),
        compiler_params=_cparams("parallel", "parallel"),
    )(slot, *arrs)


def _pair_sum(name, c_idx, ps, th):
    n = len(ps)
    _, rows, _ = ps[0].shape

    def body(c_ref, *refs):
        mine, whole, out, theirs = refs[:n], refs[n:2 * n], refs[2 * n:3 * n], refs[3 * n:4 * n]
        send, recv = refs[4 * n], refs[4 * n + 1]
        s, i = pl.program_id(0), pl.program_id(1)

        @pl.when((s == 0) & (i == 0))
        def _():
            x, y, c, _ = _place()
            cps = [_rcopy(whole[k].at[:, :, pl.ds((1 - c) * HALF, HALF)], theirs[k], send.at[k], recv.at[k],
                          (x, y, 1 - c)) for k in range(n)]
            for cp in cps:
                cp.start()
            for cp in cps:
                cp.wait()

        rows_i = slice(None) if th == rows else pl.ds(pl.multiple_of(i * th, th), th)
        for k in range(n):
            out[k][...] = (mine[k][...].astype(F32) + theirs[k][s, rows_i, :].astype(F32)).astype(BF16)

    spec = pl.BlockSpec((None, th, HALF), lambda s, i, cr: (s, i, 0))
    return pl.pallas_call(
        body, name=name,
        grid_spec=pltpu.PrefetchScalarGridSpec(
            num_scalar_prefetch=1, grid=(NSH, rows // th),
            in_specs=[pl.BlockSpec((None, th, HALF), lambda s, i, cr: (s, i, cr[0]))] * n + _any_specs(n),
            out_specs=[spec] * n,
            scratch_shapes=[pltpu.VMEM((NSH, rows, HALF), BF16)] * n
            + [pltpu.SemaphoreType.DMA((n,)), pltpu.SemaphoreType.DMA((n,))]),
        out_shape=[jax.ShapeDtypeStruct((NSH, rows, HALF), BF16)] * n,
        compiler_params=_cparams("arbitrary", "arbitrary"),
    )(c_idx, *ps, *ps)


def _chip_sum(name, place, cs, ts, th):
    n = len(ts)
    _, rows, _ = ts[0].shape

    def body(p_ref, *refs):
        for i in range(n):
            t = refs[n + i][...].astype(F32)
            refs[2 * n + i][...] = ((refs[i][...].astype(F32) + t[0]) + t[1]) + t[2]

    return pl.pallas_call(
        body, name=name,
        grid_spec=pltpu.PrefetchScalarGridSpec(
            num_scalar_prefetch=1, grid=(rows // th,),
            in_specs=[pl.BlockSpec((None, th, HALF), lambda i, pr: (pr[0], i, 0))] * n
            + [pl.BlockSpec((3, th, HALF), lambda i, pr: (0, i, 0))] * n,
            out_specs=[pl.BlockSpec((th, HALF), lambda i, pr: (i, pr[1]))] * n),
        out_shape=[jax.ShapeDtypeStruct((rows, D), F32)] * n, compiler_params=_cparams("parallel"),
    )(place, *cs, *ts)


def _adamw(name, ws, gs, ms, vs, tr, tc):
    n = len(ws)
    shape = ws[0].shape
    rows, cols, mid = shape[0], shape[-1], shape[1:-1]
    c1 = 1.0 / (1.0 - ADAM_B1 ** ADAM_STEP)
    c2 = 1.0 / (1.0 - ADAM_B2 ** ADAM_STEP)

    def body(*refs):
        for i in range(n):
            w, g, m, v = (refs[k * n + i][...] for k in range(4))
            m2 = ADAM_B1 * m + (1.0 - ADAM_B1) * g
            v2 = ADAM_B2 * v + (1.0 - ADAM_B2) * (g * g)
            refs[4 * n + 3 * i][...] = -ADAM_LR * ((m2 * c1) / (jnp.sqrt(v2 * c2) + ADAM_EPS) + ADAM_WD * w)
            refs[4 * n + 3 * i + 1][...] = m2
            refs[4 * n + 3 * i + 2][...] = v2

    spec = pl.BlockSpec((tr,) + mid + (tc,), lambda i, j: (i,) + (0,) * len(mid) + (j,))
    outs = pl.pallas_call(
        body, name=name, grid=(rows // tr, cols // tc), in_specs=[spec] * (4 * n), out_specs=[spec] * (3 * n),
        out_shape=[jax.ShapeDtypeStruct(shape, F32)] * (3 * n),
        compiler_params=_cparams("parallel", "parallel"),
    )(*ws, *gs, *ms, *vs)
    return [tuple(outs[3 * i:3 * i + 3]) for i in range(n)]


def _place():
    x, y, c = lax.axis_index("x"), lax.axis_index("y"), lax.axis_index("c")
    chips = [(1 - x, y), (x, 1 - y), (1 - x, 1 - y)]
    return x, y, c, chips


def _any_specs(n):
    return [pl.BlockSpec(memory_space=pl.ANY)] * n


def _rcopy(src, dst, send_sem, recv_sem, dev):
    return pltpu.make_async_remote_copy(src_ref=src, dst_ref=dst, send_sem=send_sem, recv_sem=recv_sem,
                                        device_id=dev, device_id_type=MESH)


def _gather_rider(bufs, views):
    n = len(bufs)

    def start(rin, rout, sems):
        send, recv = sems[0], sems[1]
        x, y, c, chips = _place()
        for j, chip in enumerate(chips):
            for b in range(n):
                mine = views[b](rout[b], 2 * x + y, c)
                _rcopy(mine, mine, send.at[j * n + b], recv.at[j * n + b], (chip[0], chip[1], c)).start()

    def finish(rin, rout, sems):
        send, recv, fsend, frecv = sems
        x, y, c, chips = _place()
        passed = []
        for j, chip in enumerate(chips):
            for b in range(n):
                landed = views[b](rout[b], 2 * chip[0] + chip[1], c)
                _rcopy(landed, landed, send.at[j * n + b], recv.at[j * n + b], (x, y, c)).wait_recv()
                fw = _rcopy(landed, landed, fsend.at[j * n + b], frecv.at[j * n + b], (x, y, 1 - c))
                fw.start()
                passed.append(fw)
        for j, chip in enumerate(chips):
            for b in range(n):
                other = views[b](rout[b], 2 * chip[0] + chip[1], 1 - c)
                _rcopy(other, other, fsend.at[j * n + b], frecv.at[j * n + b], (x, y, c)).wait_recv()
        for j, chip in enumerate(chips):
            for b in range(n):
                mine = views[b](rout[b], 2 * x + y, c)
                _rcopy(mine, mine, send.at[j * n + b], recv.at[j * n + b], (x, y, c)).wait_send()
        for fw in passed:
            fw.wait_send()

    return _Rider(list(bufs), [jax.ShapeDtypeStruct(a.shape, a.dtype) for a in bufs], {b: b for b in range(n)},
                  [pltpu.SemaphoreType.DMA((3 * n,))] * 4, start, finish)


def _small_gather_rider(cw):
    def descs(rin, rout, sems, x, y, c, chips):
        return [_rcopy(rin[0], rout[0].at[2 * x + y], sems[1].at[j], sems[2].at[j], (chip[0], chip[1], c))
                for j, chip in enumerate(chips)]

    def start(rin, rout, sems):
        x, y, c, chips = _place()
        pltpu.make_async_copy(rin[0], rout[0].at[2 * x + y], sems[0].at[0]).start()
        for cp in descs(rin, rout, sems, x, y, c, chips):
            cp.start()

    def finish(rin, rout, sems):
        x, y, c, chips = _place()
        for j, chip in enumerate(chips):
            _rcopy(rin[0], rout[0].at[2 * chip[0] + chip[1]], sems[1].at[j], sems[2].at[j], (x, y, c)).wait_recv()
        for cp in descs(rin, rout, sems, x, y, c, chips):
            cp.wait_send()
        pltpu.make_async_copy(rin[0], rout[0].at[2 * x + y], sems[0].at[0]).wait()

    return _Rider([cw], [jax.ShapeDtypeStruct((NSH,) + cw.shape, cw.dtype)], {},
                  [pltpu.SemaphoreType.DMA((1,)), pltpu.SemaphoreType.DMA((3,)), pltpu.SemaphoreType.DMA((3,))],
                  start, finish)


def _to_chips_rider(cs):
    n = len(cs)

    def descs(rin, rout, sems):
        x, y, c, chips = _place()
        return [_rcopy(rin[i].at[2 * chip[0] + chip[1]], rout[i].at[j], sems[0].at[j * n + i], sems[1].at[j * n + i],
                       (chip[0], chip[1], c)) for j, chip in enumerate(chips) for i in range(n)]

    def start(rin, rout, sems):
        for cp in descs(rin, rout, sems):
            cp.start()

    def finish(rin, rout, sems):
        for cp in descs(rin, rout, sems):
            cp.wait()

    return _Rider(list(cs), [jax.ShapeDtypeStruct((3,) + a.shape[1:], a.dtype) for a in cs], {},
                  [pltpu.SemaphoreType.DMA((3 * n,))] * 2, start, finish)


def _run_riders(name, riders):
    n_in = [len(r.operands) for r in riders]
    n_out = [len(r.out_shapes) for r in riders]
    n_sem = [len(r.sems) for r in riders]

    def body(*refs):
        parts, at = [], 0
        for counts in (n_in, n_out, n_sem):
            group = []
            for k in counts:
                group.append(refs[at:at + k])
                at += k
            parts.append(group)
        for i, r in enumerate(riders):
            r.start(parts[0][i], parts[1][i], parts[2][i])
        for i, r in enumerate(riders):
            r.finish(parts[0][i], parts[1][i], parts[2][i])

    aliases = {}
    for i, r in enumerate(riders):
        for k, v in r.aliases.items():
            aliases[sum(n_in[:i]) + k] = sum(n_out[:i]) + v
    res = pl.pallas_call(
        body, name=name, in_specs=_any_specs(sum(n_in)), out_specs=_any_specs(sum(n_out)),
        out_shape=[s for r in riders for s in r.out_shapes], input_output_aliases=aliases,
        scratch_shapes=[s for r in riders for s in r.sems],
    )(*[a for r in riders for a in r.operands])
    out, at = [], 0
    for k in n_out:
        out.append(list(res[at:at + k]))
        at += k
    return out


def _swap_halves(gs):
    n = len(gs)

    def body(*refs):
        dst, send, recv = refs[n:2 * n], refs[2 * n], refs[2 * n + 1]
        x, y, c, _ = _place()
        cps = []
        for i in range(n):
            mine = dst[i].at[:, pl.ds(c * HALF, HALF)]
            cps.append(pltpu.make_async_remote_copy(
                src_ref=mine, dst_ref=mine, send_sem=send.at[i], recv_sem=recv.at[i],
                device_id=(x, y, 1 - c), device_id_type=MESH))
        for cp in cps:
            cp.start()
        for i in range(n):
            other = dst[i].at[:, pl.ds((1 - c) * HALF, HALF)]
            pltpu.make_async_remote_copy(
                src_ref=other, dst_ref=other, send_sem=send.at[i], recv_sem=recv.at[i],
                device_id=(x, y, c), device_id_type=MESH).wait_recv()
        for cp in cps:
            cp.wait_send()

    return pl.pallas_call(
        body, name="grads_swap_halves", in_specs=_any_specs(n), out_specs=_any_specs(n),
        out_shape=[jax.ShapeDtypeStruct(g.shape, g.dtype) for g in gs],
        input_output_aliases={i: i for i in range(n)},
        scratch_shapes=[pltpu.SemaphoreType.DMA((n,)), pltpu.SemaphoreType.DMA((n,))],
    )(*gs)


SMALL_ROWS = 16


def _allreduce_small(vec):
    def body(v_ref, o_ref, buf, send, recv):
        x, y, c, _ = _place()
        me = 4 * x + 2 * y + c
        buf[me] = v_ref[...]
        cps = []
        for k in range(1, 8):
            peer = (x ^ (k >> 2), y ^ ((k >> 1) & 1), c ^ (k & 1))
            cps.append(pltpu.make_async_remote_copy(
                src_ref=v_ref, dst_ref=buf.at[me], send_sem=send.at[k - 1], recv_sem=recv.at[k - 1],
                device_id=peer, device_id_type=MESH))
        for cp in cps:
            cp.start()
        for k in range(1, 8):
            pltpu.make_async_remote_copy(
                src_ref=v_ref, dst_ref=buf.at[me ^ k], send_sem=send.at[k - 1], recv_sem=recv.at[k - 1],
                device_id=(x, y, c), device_id_type=MESH).wait_recv()
        for cp in cps:
            cp.wait_send()
        t = buf[0]
        for d in range(1, 8):
            t = t + buf[d]
        o_ref[...] = t

    return pl.pallas_call(
        body, name="allreduce_small",
        in_specs=[pl.BlockSpec(memory_space=pltpu.VMEM)], out_specs=pl.BlockSpec(memory_space=pltpu.VMEM),
        out_shape=jax.ShapeDtypeStruct((SMALL_ROWS, D), F32),
        scratch_shapes=[pltpu.VMEM((8, SMALL_ROWS, D), F32), pltpu.SemaphoreType.DMA((7,)),
                        pltpu.SemaphoreType.DMA((7,))],
    )(vec)


def _col_half(ref, slot, hc):
    return ref.at[slot, :, pl.ds(hc * HALF, HALF)]


def _stack_half(ref, slot, hc):
    return ref.at[slot, :, :, pl.ds(hc * HALF, HALF)]


def _row_tile(rows):
    for t in range(512, 15, -16):
        if rows % t == 0:
            return t
    return rows


def _same_shape_runs(arrs):
    runs, a = [], 0
    for b in range(1, len(arrs) + 1):
        if b == len(arrs) or arrs[b].shape != arrs[a].shape:
            runs.append((a, b))
            a = b
    return runs


class _Comm:
    def __init__(self):
        x, y, c = lax.axis_index("x"), lax.axis_index("y"), lax.axis_index("c")
        self.c_idx = jnp.reshape(c, (1,)).astype(jnp.int32)
        self.place = jnp.stack([2 * x + y, c]).astype(jnp.int32)
        self.groups = {}

    @staticmethod
    def gather(*bufs):
        return _gather_rider(list(bufs), [_col_half if b.ndim == 3 else _stack_half for b in bufs])

    def reduce_rider(self, tag, names, ps):
        csums = []
        for a, b in _same_shape_runs(ps):
            csums += _pair_sum("pair_sum_%s%d" % (tag, a), self.c_idx, ps[a:b], _row_tile(ps[a].shape[1]))
        self.groups[tag] = [names, csums, None]
        return _to_chips_rider(csums)

    def landed(self, tag, ts):
        self.groups[tag][2] = ts

    def finish(self):
        names, halves = [], []
        for tag, (group_names, csums, ts) in self.groups.items():
            names += group_names
            for a, b in _same_shape_runs(csums):
                halves += _chip_sum("chip_sum_%s%d" % (tag, a), self.place, csums[a:b], ts[a:b],
                                    _row_tile(csums[a].shape[1]))
        return dict(zip(names, _swap_halves(halves)))


ROPE_THETA = 10000.0
SMALL_1K = ("ffn1_pre_norm", "ffn1_post_norm", "mix_pre_norm", "ssm_norm", "mix_post_norm",
            "ffn2_pre_norm", "ffn2_post_norm")
SMALL_16 = ("dt_bias", "a_log", "d_skip")
OFF_CONVB = 7 * D
OFF_16 = OFF_CONVB + CONV_C
OFF_CONVW = OFF_16 + 48
OFF_LOSS = OFF_CONVW + CONV_K * CONV_C
SMALL_LEN = SMALL_ROWS * D


def _sds(shape, dtype):
    return jax.ShapeDtypeStruct(shape, dtype)


def _ridden(res, rider):
    return res if rider is not None else (res, None)


def _ffn_down(name, act, w, tail_of, rider=None):
    tail, o_specs, o_shapes = tail_of(TS)
    return _mm(name, [act, w.dn], NN, (S // TS,),
               [pl.BlockSpec((NSH, TS, FS), lambda i: (0, i, 0)),
                pl.BlockSpec((NSH, None, FS, D), lambda i: (0, w.d0, 0, 0))], o_specs, o_shapes, rider, tail)


def _ffn_dw(name, a, b, rider=None):
    return _mm(name, [a, b], TN, (NSH,),
               [pl.BlockSpec((None, S, FS), lambda s: (s, 0, 0)), pl.BlockSpec((S, D), lambda s: (0, 0))],
               pl.BlockSpec((None, FS, D), lambda s: (s, 0, 0)), _sds((NSH, FS, D), BF16), rider)


def _ffn_dn(name, dgate, dup, w, tail_of, rider=None):
    rows = TS // 2
    tail, o_specs, o_shapes = tail_of(rows)
    a2 = pl.BlockSpec((NSH, rows, FS), lambda i: (0, i, 0))
    return _mm(name, [dgate, w.gu, dup, w.gu], NN, (S // rows,),
               [a2, pl.BlockSpec((NSH, None, FS, D), lambda i: (0, w.g0, 0, 0)),
                a2, pl.BlockSpec((NSH, None, FS, D), lambda i: (0, w.g0 + 1, 0, 0))], o_specs, o_shapes, rider, tail)


def _out_proj_dx(dh, wout):
    def body(dh_ref, w_ref, dyn_ref, do_ref):
        dm = _dot(dh_ref[...], w_ref[...], NT)
        dyn_ref[...] = dm[:, D:]
        for b in range(TS // 128):
            for j, blk in enumerate(_rows_to_blocks(dm[128 * b:128 * (b + 1), :D])):
                do_ref[j, b] = blk.astype(BF16)

    return pl.pallas_call(
        body, name="out_proj_dx", grid=(S // TS,),
        in_specs=[pl.BlockSpec((TS, D), lambda i: (i, 0)), pl.BlockSpec((2 * D, D), lambda i: (0, 0))],
        out_specs=[pl.BlockSpec((TS, D), lambda i: (i, 0)),
                   pl.BlockSpec((NKV, TS // 128, HD, QROWS), lambda i: (0, i, 0, 0))],
        out_shape=[_sds((S, D), F32), _sds((NKV, NCH, HD, QROWS), BF16)], compiler_params=_cparams("parallel"),
    )(dh, wout)


def _heads(t, n):
    return t.reshape(S, n, HD).transpose(1, 0, 2)


def _unheads(t):
    return t.transpose(1, 0, 2).reshape(S, t.shape[0] * HD)


def _heads_t(t, n):
    return t.reshape(S, n, HD).transpose(1, 2, 0)


def _pad128(v):
    return jnp.pad(v, ((0, 0), (0, 128 - v.shape[1])))


def _local_step(x, positions, tgt, sp, gu1, d1, f2, wint, wout, convw, comm=None):
    inv_freq = ROPE_THETA ** (-jnp.arange(0, HD, 2, dtype=F32) / HD)
    ang = positions.astype(F32)[:, None] * inv_freq
    ang = jnp.concatenate([ang, ang, ang, ang], axis=-1)
    cos, sin = jnp.cos(ang), jnp.sin(ang)
    dtb, alog = _pad128(sp["dt_bias"]), _pad128(sp["a_log"])
    dskip_l = jnp.repeat(sp["d_skip"], HD, axis=1)
    convb = sp["conv_b"]

    n1 = _prenorm("prenorm1", x, sp["ffn1_pre_norm"])
    rider = comm.gather(d1) if comm else None
    (fg1, fu1, act1), got = _ridden(_ffn_up("ffn1_up", n1, _FfnW(gu1, 0, d1, 0), rider), rider)
    if comm:
        d1, = got
    w1 = _FfnW(gu1, 0, d1, 0)
    rider = comm.gather(wint) if comm else None
    (h1, x1, n2), got = _ridden(_ffn_down(
        "ffn1_down", act1, w1,
        lambda rows: _tail_postres(rows, x, sp["ffn1_post_norm"], 0.5, sp["mix_pre_norm"]), rider), rider)
    if comm:
        wint, = got
    wint_pad = jnp.pad(wint.reshape(WIN_COLS, D), ((0, WIN_PAD - WIN_COLS), (0, 0)))

    pw = WIN_PAD // 3
    proj = _mm("in_proj", [n2, wint_pad], NT, (S // TS, 3),
               [pl.BlockSpec((TS, D), lambda i, j: (i, 0)), pl.BlockSpec((pw, D), lambda i, j: (j, 0))],
               pl.BlockSpec((TS, pw), lambda i, j: (i, j)), _sds((S, WIN_PAD), F32))
    qt = _rope_q(proj, cos, sin)
    k_rot = _rope("rope_k", proj, D // KVW, KVW, cos, sin, 1.0, 1.0)
    v_bf = proj[:, D + KVW:D + 2 * KVW].astype(BF16)
    kh, vh = _heads(k_rot, NKV), _heads(v_bf, NKV)
    kt, vt = _heads_t(k_rot, NKV), _heads_t(v_bf, NKV)
    bias = _bias_table()
    rider = comm.gather(f2, wout) if comm else None
    (ot, lse, attn), got = _ridden(_attn_fwd(qt, kh, vt, bias, rider), rider)
    if comm:
        f2, wout = got
    w2 = _FfnW(f2, 0, f2, 2)
    wout = wout.reshape(2 * D, D)
    xbc = _conv_fwd(proj, convw, convb)
    y, yn, hprev = _ssd_fwd(xbc, proj, dtb, alog, dskip_l, sp["ssm_norm"])
    mixed = jnp.concatenate([attn, yn], axis=1)
    tail, o_specs, o_shapes = _tail_postres(TS, x1, sp["mix_post_norm"], 1.0, sp["ffn2_pre_norm"])
    h2, x2, n3 = _mm("out_proj", [mixed, wout], NN, (S // TS,),
                     [pl.BlockSpec((TS, 2 * D), lambda i: (i, 0)), pl.BlockSpec((2 * D, D), lambda i: (0, 0))],
                     o_specs, o_shapes, None, tail)

    fg2, fu2, act2 = _ffn_up("ffn2_up", n3, w2)
    dy, dh3, dp3, loss = _ffn_down(
        "ffn2_down", act2, w2, lambda rows: _tail_final(rows, x2, sp["ffn2_post_norm"], tgt, 0.5))

    dgate2, dup2 = _ffn_dact("ffn2_dact", dh3, w2, fg2, fu2)
    dws2 = [_ffn_dw("ffn2_dwg", dgate2, n3), _ffn_dw("ffn2_dwu", dup2, n3), _ffn_dw("ffn2_dwd", act2, dh3)]
    dx2, dh2, dg3, dp2 = _ffn_dn(
        "ffn2_dn", dgate2, dup2, w2,
        lambda rows: _tail_mid_bwd(rows, dy, x2, sp["ffn2_pre_norm"], h2, sp["mix_post_norm"], 1.0))

    dyn, dot_ = _out_proj_dx(dh2, wout)
    dwout = _mm("out_proj_dw", [mixed, dh2], TN, (2,),
                [pl.BlockSpec((S, D), lambda m: (0, m)), pl.BlockSpec((S, D), lambda m: (0, 0))],
                pl.BlockSpec((D, D), lambda m: (m, 0)), _sds((2 * D, D), BF16))
    dwout = dwout.reshape(NSH, 2 * D // NSH, D)
    rider = comm.reduce_rider("a", BIG[3:6] + ("w_out",), dws2 + [dwout]) if comm else None
    (dxbc, dproj, ddt, dssm, dsc), got = _ridden(
        _ssd_bwd(dyn, y, xbc, proj, hprev, dtb, alog, dskip_l, sp["ssm_norm"], rider), rider)
    if comm:
        comm.landed("a", got)
    dproj, dcw8, dcb = _conv_bwd(dxbc, proj, convw, convb, dproj)
    dqt, dkh, dvh = _attn_bwd(qt, kh, kt, vh, dot_, lse, _attn_delta(ot, dot_), bias)
    dproj = _rope_dq(dqt, cos, sin, dproj)
    dproj = _rope("rope_dk", _unheads(dkh), 0, KVW, cos, sin, -1.0, 1.0, into=(dproj, D // KVW))
    dproj = lax.dynamic_update_slice(dproj, _unheads(dvh).astype(BF16), (0, D + KVW))
    dproj = lax.dynamic_update_slice(dproj, ddt, (0, COL_DT))
    dwint = _mm("in_proj_dw", [dproj, n2], TN, (3,),
                [pl.BlockSpec((S, pw), lambda j: (0, j)), pl.BlockSpec((S, D), lambda j: (0, 0))],
                pl.BlockSpec((pw, D), lambda j: (j, 0)), _sds((WIN_PAD, D), BF16))
    dwint = dwint[:WIN_COLS].reshape(NSH, WIN_SH, D)

    def riding(tag, names, ps, call):
        rider = comm.reduce_rider(tag, names, ps) if comm else None
        res, got = _ridden(call(rider), rider)
        if comm:
            comm.landed(tag, got)
        return res

    tail, o_specs, o_shapes = _tail_mid_bwd(TS, dx2, x1, sp["mix_pre_norm"], h1, sp["ffn1_post_norm"], 0.5)
    dx1, dh1, dg2, dp1 = riding("b", ("w_in",), [dwint], lambda rider: _mm(
        "in_proj_dx", [dproj, wint_pad], NN, (S // TS,),
        [pl.BlockSpec((TS, WIN_PAD), lambda i: (i, 0)), pl.BlockSpec((WIN_PAD, D), lambda i: (0, 0))],
        o_specs, o_shapes, rider, tail))

    dwd1 = _ffn_dw("ffn1_dwd", act1, dh1)
    dgate1, dup1 = riding("d", BIG[2:3], [dwd1], lambda rider: _ffn_dact("ffn1_dact", dh1, w1, fg1, fu1, rider))
    dwg1, dwu1 = _ffn_dw("ffn1_dwg", dgate1, n1), _ffn_dw("ffn1_dwu", dup1, n1)
    grad_x, dg1 = riding("g", BIG[0:2], [dwg1, dwu1], lambda rider: _ffn_dn(
        "ffn1_dn", dgate1, dup1, w1, lambda rows: _tail_first_bwd(rows, dx1, x, sp["ffn1_pre_norm"]), rider))
    dws1 = [dwg1, dwu1, dwd1]

    small = jnp.concatenate([
        dg1[0], dp1[0], dg2[0], dssm[0], dp2[0], dg3[0], dp3[0], dcb[0],
        dsc[0, :16], dsc[1, :16], dsc[2, :16], dcw8[:CONV_K].reshape(-1), loss[0, :1]])
    small = jnp.pad(small, (0, SMALL_LEN - small.shape[0])).reshape(SMALL_ROWS, D)
    if comm is None:
        return grad_x, dws1 + dws2 + [dwint, dwout], small
    return grad_x, comm.finish(), small


WEIGHTS = ("ffn1_pre_norm", "ffn1_w_gate", "ffn1_w_up", "ffn1_w_down", "ffn1_post_norm", "mix_pre_norm", "w_in",
           "conv_w", "conv_b", "dt_bias", "a_log", "d_skip", "ssm_norm", "w_out", "mix_post_norm", "ffn2_pre_norm",
           "ffn2_w_gate", "ffn2_w_up", "ffn2_w_down", "ffn2_post_norm")
BIG = ("ffn1_w_gate", "ffn1_w_up", "ffn1_w_down", "ffn2_w_gate", "ffn2_w_up", "ffn2_w_down", "w_in", "w_out")
TRANSPOSED = ("ffn1_w_gate", "ffn1_w_up", "ffn2_w_gate", "ffn2_w_up", "w_in")
SMALL_ORDER = SMALL_1K + ("conv_b",) + SMALL_16
CONVW_SH = CONV_C // NSH


def _shard2d(t, name):
    return t[0].T if name in TRANSPOSED else t[0]


def _unshard2d(t, name):
    return (t.T if name in TRANSPOSED else t)[None]


def _rows3d(t):
    return t.transpose(2, 0, 1)


def _pack_small(d, prefix, shard_of_convw):
    flat = jnp.concatenate([d[prefix + n][0] for n in SMALL_ORDER] + [shard_of_convw.reshape(-1)])
    return jnp.pad(flat, (0, SMALL_LEN - flat.shape[0])).reshape(SMALL_ROWS, D)


def _unpack_small(block, like):
    flat = block.reshape(-1)
    out, off = {}, 0
    for n in SMALL_ORDER:
        size = like[n].shape[1]
        out[n] = flat[off:off + size].reshape(1, size)
        off += size
    out["conv_w"] = flat[off:off + CONV_K * CONVW_SH].reshape(1, CONV_K, CONVW_SH)
    return out


def kernel(x, positions, ffn1_pre_norm, ffn1_w_gate, ffn1_w_up, ffn1_w_down, ffn1_post_norm, mix_pre_norm, w_in, conv_w, conv_b, dt_bias, a_log, d_skip, ssm_norm, w_out, mix_post_norm, ffn2_pre_norm, ffn2_w_gate, ffn2_w_up, ffn2_w_down, ffn2_post_norm, loss_target, m_ffn1_pre_norm, m_ffn1_w_gate, m_ffn1_w_up, m_ffn1_w_down, m_ffn1_post_norm, m_mix_pre_norm, m_w_in, m_conv_w, m_conv_b, m_dt_bias, m_a_log, m_d_skip, m_ssm_norm, m_w_out, m_mix_post_norm, m_ffn2_pre_norm, m_ffn2_w_gate, m_ffn2_w_up, m_ffn2_w_down, m_ffn2_post_norm, v_ffn1_pre_norm, v_ffn1_w_gate, v_ffn1_w_up, v_ffn1_w_down, v_ffn1_post_norm, v_mix_pre_norm, v_w_in, v_conv_w, v_conv_b, v_dt_bias, v_a_log, v_d_skip, v_ssm_norm, v_w_out, v_mix_post_norm, v_ffn2_pre_norm, v_ffn2_w_gate, v_ffn2_w_up, v_ffn2_w_down, v_ffn2_post_norm):
    given = dict(locals())
    xi, yi = lax.axis_index("x"), lax.axis_index("y")

    shard = jnp.reshape(2 * xi + yi, (1,)).astype(jnp.int32)
    big = {p + n: _shard2d(given[p + n], n) for n in BIG for p in ("", "m_", "v_")}
    gu1 = _cast_stack("cast_ffn1_gate_up", shard, [big[n] for n in BIG[0:2]], 176, D)
    d1 = _cast_stack("cast_ffn1_down", shard, [big[BIG[2]]], 176, D)
    f2 = _cast_stack("cast_ffn2", shard, [big[n] for n in BIG[3:6]], 176, D)
    winsh = _cast_stack("cast_w_in", shard, [big["w_in"]], WIN_SH, 256).reshape(NSH, WIN_SH, D)
    woutsh = _cast_stack("cast_w_out", shard, [big["w_out"]], 256, D).reshape(NSH, 2 * D // NSH, D)
    comm = _Comm()
    (gu1,), (cwf,) = _run_riders("gather_ffn1_gate_up", [comm.gather(gu1), _small_gather_rider(conv_w[0])])
    convw = cwf.transpose(1, 0, 2).reshape(CONV_K, CONV_C)

    sp = {n: given[n] for n in SMALL_ORDER}
    grad_x, big_grads, small = _local_step(x[0], positions[0], loss_target[0], sp, gu1, d1, f2, winsh, woutsh,
                                           convw, comm)

    tot = _allreduce_small(small).reshape(-1)
    loss = tot[OFF_LOSS]
    small_grads, off = {}, 0
    for n in SMALL_ORDER:
        size = given[n].shape[1]
        small_grads[n] = tot[off:off + size].reshape(1, size)
        off += size
    dconvw = tot[OFF_CONVW:OFF_CONVW + CONV_K * CONV_C].reshape(CONV_K, NSH, CONVW_SH)
    dconvw = lax.dynamic_index_in_dim(dconvw, 2 * xi + yi, axis=1, keepdims=False)
    small_grads["conv_w"] = dconvw.reshape(1, CONV_K, CONVW_SH)

    upd = {}
    for names, tr in ((BIG[0:3], 176), (BIG[3:6], 176), (BIG[7:8], 256)):
        res = _adamw("adamw_" + names[0], [big[n] for n in names], [big_grads[n] for n in names],
                     [big["m_" + n] for n in names], [big["v_" + n] for n in names], tr, D)
        for n, r in zip(names, res):
            upd[n] = tuple(_unshard2d(t, n) for t in r)
    g_win = big_grads["w_in"].reshape(WIN_SH, 1, D)
    res, = _adamw("adamw_w_in", [_rows3d(w_in)], [g_win], [_rows3d(m_w_in)], [_rows3d(v_w_in)], WIN_SH // 4, D)
    upd["w_in"] = tuple(t.transpose(1, 2, 0) for t in res)
    (dl, m2, v2), = _adamw(
        "adamw_small", [_pack_small(given, "", conv_w[0])], [_pack_small(small_grads, "", dconvw)],
        [_pack_small(given, "m_", m_conv_w[0])], [_pack_small(given, "v_", v_conv_w[0])], SMALL_ROWS, D)
    dl, m2, v2 = (_unpack_small(t, given) for t in (dl, m2, v2))
    for n in SMALL_ORDER + ("conv_w",):
        upd[n] = (dl[n], m2[n], v2[n])

    grads = dict(small_grads)
    grads.update({n: _unshard2d(g, n) for n, g in big_grads.items() if n != "w_in"})
    grads["w_in"] = g_win.transpose(1, 2, 0)
    return (loss, grad_x[None], *[grads[n] for n in WEIGHTS], *[upd[n][0] for n in WEIGHTS],
            *[upd[n][1] for n in WEIGHTS], *[upd[n][2] for n in WEIGHTS])
```

```python
import functools
import typing

import jax
import jax.numpy as jnp
from jax import lax
from jax.experimental import pallas as pl
from jax.experimental.pallas import tpu as pltpu

F32 = jnp.float32
BF16 = jnp.bfloat16

S = 2048
D = 1024
FF = 2816
NSH = 4
FS = FF // NSH
HALF = D // 2
HD = 64
NKV = 4
NQ_PER_KV = 4
KVW = NKV * HD
QCOLS = NQ_PER_KV * HD
CONV_C = 1536
CONV_K = 4
SSM_W = 1024
NST = 128
NCH = S // 128
WIN_COLS = 4112
WIN_SH = WIN_COLS // NSH
WIN_PAD = 4224
COL_DT = 4096
EPS = 1e-6
NEG = -1e30

ADAM_LR = 0.001
ADAM_B1 = 0.9
ADAM_B2 = 0.999
ADAM_EPS = 1e-08
ADAM_WD = 0.01
ADAM_STEP = 10

VMEM_LIMIT = 56 * 1024 * 1024
TS = 512
TR = 256

NN = (((1,), (0,)), ((), ()))
NT = (((1,), (1,)), ((), ()))
TN = (((0,), (0,)), ((), ()))
MESH = pl.DeviceIdType.MESH


def _cparams(*sem):
    return pltpu.CompilerParams(dimension_semantics=sem, vmem_limit_bytes=VMEM_LIMIT)


def _dot(a, b, dims):
    return lax.dot_general(a.astype(BF16), b.astype(BF16), dims, preferred_element_type=F32)


def _bf16_pieces(v):
    hi = v.astype(BF16)
    rest = v - hi.astype(F32)
    mid = rest.astype(BF16)
    return hi, mid, (rest - mid.astype(F32)).astype(BF16)


def _dot_exact(a, b, ones="a"):
    if ones == "a":
        sel = a.astype(BF16)
        parts = [lax.dot_general(sel, p, NN, preferred_element_type=F32) for p in _bf16_pieces(b)]
    else:
        sel = b.astype(BF16)
        parts = [lax.dot_general(p, sel, NN, preferred_element_type=F32) for p in _bf16_pieces(a)]
    return (parts[2] + parts[1]) + parts[0]


def _sigmoid(v):
    return 1.0 / (1.0 + jnp.exp(-v))


class _Rider(typing.NamedTuple):
    operands: list
    out_shapes: list
    aliases: dict
    sems: list
    start: typing.Callable
    finish: typing.Callable


def _call(body, name, grid, in_specs, out_specs, out_shape, operands, scratch=(), sem=(), rider=None):
    multi = isinstance(out_shape, (list, tuple))
    if rider is None:
        return pl.pallas_call(
            body, name=name, grid=grid, in_specs=in_specs, out_specs=out_specs, out_shape=out_shape,
            scratch_shapes=list(scratch), compiler_params=_cparams(*sem))(*operands)
    outs = list(out_shape) if multi else [out_shape]
    ospecs = list(out_specs) if multi else [out_specs]
    n_in, n_out, n_scr = len(operands), len(outs), len(scratch)
    ri, ro = len(rider.operands), len(rider.out_shapes)

    def wrapped(*refs):
        o0 = n_in + ri
        s0 = o0 + n_out + ro
        rin, rout, rsem = refs[n_in:o0], refs[o0 + n_out:s0], refs[s0 + n_scr:]
        ids = [pl.program_id(a) for a in range(len(grid))]
        first = functools.reduce(jnp.logical_and, [i == 0 for i in ids])
        last = functools.reduce(jnp.logical_and, [i == g - 1 for i, g in zip(ids, grid)])

        @pl.when(first)
        def _():
            rider.start(rin, rout, rsem)

        body(*refs[:n_in], *refs[o0:o0 + n_out], *refs[s0:s0 + n_scr])

        @pl.when(last)
        def _():
            rider.finish(rin, rout, rsem)

    hbm = pl.BlockSpec(memory_space=pl.ANY)
    res = pl.pallas_call(
        wrapped, name=name, grid=grid, in_specs=list(in_specs) + [hbm] * ri, out_specs=ospecs + [hbm] * ro,
        out_shape=outs + list(rider.out_shapes), scratch_shapes=list(scratch) + list(rider.sems),
        input_output_aliases={n_in + k: n_out + v for k, v in rider.aliases.items()},
        compiler_params=_cparams(*(("arbitrary",) * len(grid))))(*operands, *rider.operands)
    main = list(res[:n_out])
    return (main if multi else main[0]), list(res[n_out:])


class _Tail(typing.NamedTuple):
    fn: typing.Callable
    operands: list
    in_specs: list


def _mm(name, operands, dims, grid, in_specs, o_spec, out_shape, rider=None, tail=None):
    npairs = len(operands) // 2
    extra = [] if tail is None else list(tail.operands)
    nin = 2 * npairs + len(extra)

    def body(*refs):
        t = None
        for i in range(npairs):
            a, b = refs[2 * i], refs[2 * i + 1]
            parts = [(a[s], b[s]) for s in range(a.shape[0])] if len(a.shape) == 3 else [(a[...], b[...])]
            for pa, pb in parts:
                d = _dot(pa, pb, dims)
                t = d if t is None else t + d
        if tail is None:
            refs[nin][...] = t.astype(refs[nin].dtype)
        else:
            tail.fn(t, refs[2 * npairs:nin], refs[nin:])

    sem = ("parallel" if tail is None else "arbitrary",) * len(grid)
    specs = list(in_specs) + ([] if tail is None else list(tail.in_specs))
    return _call(body, name, grid, specs, o_spec, out_shape, list(operands) + extra, (), sem, rider)


class _FfnW(typing.NamedTuple):
    gu: jax.Array
    g0: int
    dn: jax.Array
    d0: int


def _ffn_up(name, n, w, rider=None):
    def body(n_ref, wg_ref, wu_ref, fg_ref, fu_ref, a_ref):
        nb = n_ref[...]
        g = _dot(nb, wg_ref[...], NT)
        u = _dot(nb, wu_ref[...], NT)
        sg = _sigmoid(g)
        silu = g * sg
        fg_ref[...] = (u * (sg * (1.0 + g * (1.0 - sg)))).astype(BF16)
        fu_ref[...] = silu.astype(BF16)
        a_ref[...] = (silu * u).astype(BF16)

    out = jax.ShapeDtypeStruct((NSH, S, FS), BF16)
    ospec = pl.BlockSpec((None, TS, FS), lambda s, i: (s, i, 0))
    return _call(
        body, name, (NSH, S // TS),
        [pl.BlockSpec((TS, D), lambda s, i: (i, 0)),
         pl.BlockSpec((None, None, FS, D), lambda s, i: (s, w.g0, 0, 0)),
         pl.BlockSpec((None, None, FS, D), lambda s, i: (s, w.g0 + 1, 0, 0))],
        [ospec, ospec, ospec], [out, out, out], (n, w.gu, w.gu), sem=("parallel", "parallel"), rider=rider)


def _ffn_dact(name, dh, w, fgate, fup, rider=None):
    def body(dh_ref, wd_ref, fg_ref, fu_ref, dg_ref, du_ref):
        da = _dot(dh_ref[...], wd_ref[...], NT)
        dg_ref[...] = (da * fg_ref[...].astype(F32)).astype(BF16)
        du_ref[...] = (da * fu_ref[...].astype(F32)).astype(BF16)

    out = jax.ShapeDtypeStruct((NSH, S, FS), BF16)
    aspec = pl.BlockSpec((None, TS, FS), lambda s, i: (s, i, 0))
    return _call(
        body, name, (NSH, S // TS),
        [pl.BlockSpec((TS, D), lambda s, i: (i, 0)),
         pl.BlockSpec((None, None, FS, D), lambda s, i: (s, w.d0, 0, 0)), aspec, aspec],
        [aspec, aspec], [out, out], (dh, w.dn, fgate, fup), sem=("parallel", "parallel"), rider=rider)


def _rstd(v):
    return lax.rsqrt(jnp.mean(v * v, axis=-1, keepdims=True) + EPS)


def _row_spec():
    return pl.BlockSpec((TR, D), lambda i: (i, 0))


def _vec_spec():
    return pl.BlockSpec((1, D), lambda i: (0, 0))


def _acc_rows(ref, v):
    @pl.when(pl.program_id(0) == 0)
    def _():
        ref[...] = jnp.zeros_like(ref)
    ref[...] += jnp.sum(v, axis=0, keepdims=True)


def _prenorm(name, x, g):
    def body(x_ref, g_ref, n_ref):
        xv = x_ref[...]
        n_ref[...] = (xv * _rstd(xv) * g_ref[...]).astype(BF16)

    return pl.pallas_call(
        body, name=name, grid=(S // TR,), in_specs=[_row_spec(), _vec_spec()], out_specs=_row_spec(),
        out_shape=jax.ShapeDtypeStruct((S, D), BF16), compiler_params=_cparams("parallel"),
    )(x, g)


def _rows_spec(rows):
    return pl.BlockSpec((rows, D), lambda i: (i, 0))


def _rows_f32():
    return jax.ShapeDtypeStruct((S, D), F32)


def _rows_bf16():
    return jax.ShapeDtypeStruct((S, D), BF16)


def _vec_f32():
    return jax.ShapeDtypeStruct((1, D), F32)


def _tail_postres(rows, x, p, alpha, gnext):
    def fn(h, ins, outs):
        x_ref, p_ref, g_ref = ins
        h_ref, xo_ref, n_ref = outs
        h_ref[...] = h
        xo = x_ref[...] + alpha * (h * _rstd(h) * p_ref[...])
        xo_ref[...] = xo
        n_ref[...] = (xo * _rstd(xo) * g_ref[...]).astype(BF16)

    rs = _rows_spec(rows)
    return (_Tail(fn, [x, p, gnext], [rs, _vec_spec(), _vec_spec()]), [rs, rs, rs],
            [_rows_f32(), _rows_f32(), _rows_bf16()])


def _tail_final(rows, x, p, tgt, alpha):
    def fn(h, ins, outs):
        x_ref, p_ref, t_ref = ins
        dy_ref, dh_ref, dp_ref, loss_ref = outs
        r = _rstd(h)
        hn = h * r
        pv = p_ref[...]
        e = x_ref[...] + alpha * (hn * pv) - t_ref[...]
        dy = e * (1.0 / D)
        dy_ref[...] = dy
        du = alpha * dy * pv
        dh_ref[...] = (r * (du - hn * jnp.mean(du * hn, axis=-1, keepdims=True))).astype(BF16)
        _acc_rows(dp_ref, alpha * dy * hn)
        part = 0.5 * jnp.sum(jnp.mean(e * e, axis=-1, keepdims=True), axis=0, keepdims=True)
        _acc_rows(loss_ref, jnp.broadcast_to(part, (1, 128)))

    rs = _rows_spec(rows)
    return (_Tail(fn, [x, p, tgt], [rs, _vec_spec(), rs]),
            [rs, rs, _vec_spec(), pl.BlockSpec((1, 128), lambda i: (0, 0))],
            [_rows_f32(), _rows_bf16(), _vec_f32(), jax.ShapeDtypeStruct((1, 128), F32)])


def _norm_bwd(dn, xv, g_ref, dg_ref):
    r = _rstd(xv)
    xn = xv * r
    dng = dn * g_ref[...]
    _acc_rows(dg_ref, dn * xn)
    return r * (dng - xn * jnp.mean(dng * xn, axis=-1, keepdims=True))


def _tail_mid_bwd(rows, dres, x, g, h, p, alpha):
    def fn(dn, ins, outs):
        dr_ref, x_ref, g_ref, h_ref, p_ref = ins
        dx_ref, dh_ref, dg_ref, dp_ref = outs
        dx = dr_ref[...] + _norm_bwd(dn, x_ref[...], g_ref, dg_ref)
        dx_ref[...] = dx
        hv = h_ref[...]
        r = _rstd(hv)
        hn = hv * r
        du = alpha * dx * p_ref[...]
        dh_ref[...] = (r * (du - hn * jnp.mean(du * hn, axis=-1, keepdims=True))).astype(BF16)
        _acc_rows(dp_ref, alpha * dx * hn)

    rs = _rows_spec(rows)
    return (_Tail(fn, [dres, x, g, h, p], [rs, rs, _vec_spec(), rs, _vec_spec()]),
            [rs, rs, _vec_spec(), _vec_spec()], [_rows_f32(), _rows_bf16(), _vec_f32(), _vec_f32()])


def _tail_first_bwd(rows, dres, x, g):
    def fn(dn, ins, outs):
        dr_ref, x_ref, g_ref = ins
        dx_ref, dg_ref = outs
        dx_ref[...] = dr_ref[...] + _norm_bwd(dn, x_ref[...], g_ref, dg_ref)

    rs = _rows_spec(rows)
    return (_Tail(fn, [dres, x, g], [rs, rs, _vec_spec()]), [rs, _vec_spec()], [_rows_f32(), _vec_f32()])


def _rotate(t, c128, s128, sign, scale):
    width = t.shape[1]
    c = jnp.tile(c128, (1, width // 128))
    sn = jnp.tile(s128, (1, width // 128))
    lane = lax.broadcasted_iota(jnp.int32, t.shape, 1) & (HD - 1)
    rot = jnp.where(lane < HD // 2, -pltpu.roll(t, width - HD // 2, 1), pltpu.roll(t, HD // 2, 1))
    return (t * c + sign * (rot * sn)) * scale


def _rows_to_blocks(y):
    out = []
    for j in range(NKV):
        yt = y[:, QCOLS * j:QCOLS * (j + 1)].T
        out.append(jnp.concatenate([yt[HD * g:HD * (g + 1)] for g in range(NQ_PER_KV)], axis=1))
    return out


def _blocks_to_rows(blocks):
    cols = []
    for b in blocks:
        stacked = jnp.concatenate([b[:, 128 * g:128 * (g + 1)] for g in range(NQ_PER_KV)], axis=0)
        cols.append(stacked.T)
    return jnp.concatenate(cols, axis=1)


def _rope_q(proj, cos, sin):
    def body(t_ref, c_ref, s_ref, o_ref):
        y = _rotate(t_ref[...], c_ref[...], s_ref[...], 1.0, HD ** -0.5)
        for j, blk in enumerate(_rows_to_blocks(y)):
            o_ref[j] = blk.astype(BF16)

    return pl.pallas_call(
        body, name="rope_q", grid=(NCH,),
        in_specs=[pl.BlockSpec((128, D), lambda i: (i, 0)),
                  pl.BlockSpec((128, 128), lambda i: (i, 0)), pl.BlockSpec((128, 128), lambda i: (i, 0))],
        out_specs=pl.BlockSpec((NKV, None, HD, QROWS), lambda i: (0, i, 0, 0)),
        out_shape=jax.ShapeDtypeStruct((NKV, NCH, HD, QROWS), BF16), compiler_params=_cparams("parallel"),
    )(proj, cos, sin)


def _rope_dq(dqt, cos, sin, dproj):
    def body(t_ref, c_ref, s_ref, buf_ref, o_ref):
        t = _blocks_to_rows([t_ref[j] for j in range(NKV)])
        o_ref[...] = _rotate(t, c_ref[...], s_ref[...], -1.0, HD ** -0.5).astype(BF16)

    return pl.pallas_call(
        body, name="rope_dq", grid=(NCH,),
        in_specs=[pl.BlockSpec((NKV, None, HD, QROWS), lambda i: (0, i, 0, 0)),
                  pl.BlockSpec((128, 128), lambda i: (i, 0)), pl.BlockSpec((128, 128), lambda i: (i, 0)),
                  pl.BlockSpec(memory_space=pl.ANY)],
        out_specs=pl.BlockSpec((128, D), lambda i: (i, 0)),
        out_shape=jax.ShapeDtypeStruct(dproj.shape, BF16), input_output_aliases={3: 0},
        compiler_params=_cparams("parallel"),
    )(dqt, cos, sin, dproj)


def _rope(name, src, col_block, width, cos, sin, sign, scale, into=None):
    def body(t_ref, c_ref, s_ref, *rest):
        rest[-1][...] = _rotate(t_ref[...].astype(F32), c_ref[...], s_ref[...], sign, scale).astype(BF16)

    in_specs = [pl.BlockSpec((TR, width), lambda i: (i, col_block)),
                pl.BlockSpec((TR, 128), lambda i: (i, 0)), pl.BlockSpec((TR, 128), lambda i: (i, 0))]
    if into is None:
        return pl.pallas_call(
            body, name=name, grid=(S // TR,), in_specs=in_specs,
            out_specs=pl.BlockSpec((TR, width), lambda i: (i, 0)),
            out_shape=jax.ShapeDtypeStruct((S, width), BF16), compiler_params=_cparams("parallel"),
        )(src, cos, sin)
    buf, out_block = into
    return pl.pallas_call(
        body, name=name, grid=(S // TR,), in_specs=in_specs + [pl.BlockSpec(memory_space=pl.ANY)],
        out_specs=pl.BlockSpec((TR, width), lambda i: (i, out_block)),
        out_shape=jax.ShapeDtypeStruct(buf.shape, BF16), input_output_aliases={3: 0},
        compiler_params=_cparams("parallel"),
    )(src, cos, sin, buf)


QROWS = NQ_PER_KV * 128


NBIAS = NCH + 1
KV_PER_STEP = 4


def _bias_table():
    db = lax.broadcasted_iota(jnp.int32, (NBIAS, 128, QROWS), 0) - 1
    ki = lax.broadcasted_iota(jnp.int32, (NBIAS, 128, QROWS), 1)
    qi = lax.broadcasted_iota(jnp.int32, (NBIAS, 128, QROWS), 2) & 127
    d = db * 128 + qi - ki
    cnt = ((d <= 128).astype(F32) + (((d & 3) == 0) & (d <= 512)).astype(F32) + ((d & 15) == 0).astype(F32))
    return jnp.where((d >= 0) & (cnt > 0.0), jnp.log(jnp.maximum(cnt, 1.0)), NEG)


def _qt_spec():
    return pl.BlockSpec((None, None, HD, QROWS), lambda j, i: (j, i, 0, 0))


def _stat_spec():
    return pl.BlockSpec((None, None, 1, QROWS), lambda j, i: (j, i, 0, 0))


def _attn_fwd(qt, kh, vt, bias, rider=None):
    def body(q_ref, k_ref, v_ref, b_ref, o_ref, lse_ref, rows_ref):
        qb = pl.program_id(1)

        def keys(carry, off, size, bias_):
            out = []
            for h in range(KV_PER_STEP):
                m, l, acc = carry[3 * h:3 * h + 3]
                s = _dot(k_ref[h, pl.ds(off, size), :], q_ref[h], NN) + bias_
                m_new = jnp.maximum(m, jnp.max(s, axis=0, keepdims=True))
                p = jnp.exp(s - m_new)
                a = jnp.exp(m - m_new)
                out += [m_new, a * l + jnp.sum(p, axis=0, keepdims=True),
                        a * acc + _dot(v_ref[h, :, pl.ds(off, size)], p, NN)]
            return tuple(out)

        def pair(i, carry):
            bias2 = jnp.concatenate([b_ref[qb - 2 * i + 1], b_ref[qb - 2 * i]], axis=0)
            return keys(carry, pl.multiple_of(i * 256, 256), 256, bias2)

        init = (jnp.full((1, QROWS), NEG, F32), jnp.zeros((1, QROWS), F32), jnp.zeros((HD, QROWS), F32))
        res = lax.fori_loop(0, (qb + 1) // 2, pair, init * KV_PER_STEP)
        res = lax.cond(qb % 2 == 0,
                       lambda c: keys(c, pl.multiple_of(qb * 128, 128), 128, b_ref[1]), lambda c: c, res)
        outs = []
        for h in range(KV_PER_STEP):
            m, l, acc = res[3 * h:3 * h + 3]
            outs.append(acc / l)
            o_ref[h] = outs[h]
            lse_ref[h] = m + jnp.log(l)
        rows_ref[...] = _blocks_to_rows(outs).astype(BF16)

    kvs = KV_PER_STEP
    qspec = pl.BlockSpec((kvs, None, HD, QROWS), lambda j, i: (j, i, 0, 0))
    return _call(
        body, "attn_fwd", (NKV // kvs, NCH),
        [qspec, pl.BlockSpec((kvs, S, HD), lambda j, i: (j, 0, 0)),
         pl.BlockSpec((kvs, HD, S), lambda j, i: (j, 0, 0)),
         pl.BlockSpec((NBIAS, 128, QROWS), lambda j, i: (0, 0, 0))],
        [qspec, pl.BlockSpec((kvs, None, 1, QROWS), lambda j, i: (j, i, 0, 0)),
         pl.BlockSpec((128, QCOLS * kvs), lambda j, i: (i, j))],
        [jax.ShapeDtypeStruct((NKV, NCH, HD, QROWS), F32), jax.ShapeDtypeStruct((NKV, NCH, 1, QROWS), F32),
         jax.ShapeDtypeStruct((S, D), BF16)],
        (qt, kh, vt, bias), sem=("parallel", "parallel"), rider=rider)


def _attn_delta(ot, dot_):
    def body(o_ref, do_ref, dl_ref):
        dl_ref[...] = jnp.sum(o_ref[...] * do_ref[...].astype(F32), axis=1, keepdims=True)

    spec = pl.BlockSpec((None, NCH, HD, QROWS), lambda j: (j, 0, 0, 0))
    return pl.pallas_call(
        body, name="attn_delta", grid=(NKV,), in_specs=[spec, spec],
        out_specs=pl.BlockSpec((None, NCH, 1, QROWS), lambda j: (j, 0, 0, 0)),
        out_shape=jax.ShapeDtypeStruct((NKV, NCH, 1, QROWS), F32), compiler_params=_cparams("parallel"),
    )(ot, dot_)


def _attn_bwd(qt, kh, kt, vh, dot_, lse, delta, bias, rider=None):
    def body(qt_ref, k_ref, kt_ref, v_ref, dot_ref, lse_ref, dl_ref, b_ref, dq_ref, dk_ref, dv_ref):
        kb = pl.program_id(1)

        @pl.when(kb == 0)
        def _():
            dq_ref[...] = jnp.zeros_like(dq_ref)

        def blocks(carry, qbs):
            out = list(carry)
            for h in range(KV_PER_STEP):
                k, kt_, v = k_ref[h], kt_ref[h], v_ref[h]
                for qb in qbs:
                    st = _dot(k, qt_ref[h, qb], NN) + b_ref[qb - kb + 1]
                    pt = jnp.exp(st - lse_ref[h, qb])
                    dst = pt * (_dot(v, dot_ref[h, qb], NN) - dl_ref[h, qb])
                    dq_ref[h, qb] += _dot(kt_, dst, NN)
                    out[2 * h] = out[2 * h] + _dot(dst, qt_ref[h, qb], NT)
                    out[2 * h + 1] = out[2 * h + 1] + _dot(pt, dot_ref[h, qb], NT)
            return tuple(out)

        res = (jnp.zeros((128, HD), F32),) * (2 * KV_PER_STEP)
        res = lax.cond(kb % 2 == 1, lambda c: blocks(c, (kb,)), lambda c: c, res)
        res = lax.fori_loop((kb + 1) // 2, NCH // 2, lambda j, c: blocks(c, (2 * j, 2 * j + 1)), res)
        for h in range(KV_PER_STEP):
            dk_ref[h] = res[2 * h]
            dv_ref[h] = res[2 * h + 1]

    kvs = KV_PER_STEP
    tspec = pl.BlockSpec((kvs, NCH, HD, QROWS), lambda j, i: (j, 0, 0, 0))
    kspec = pl.BlockSpec((kvs, 128, HD), lambda j, i: (j, i, 0))
    sspec = pl.BlockSpec((kvs, NCH, 1, QROWS), lambda j, i: (j, 0, 0, 0))
    return _call(
        body, "attn_bwd", (NKV // kvs, NCH),
        [tspec, kspec, pl.BlockSpec((kvs, HD, 128), lambda j, i: (j, 0, i)), kspec, tspec,
         sspec, sspec, pl.BlockSpec((NBIAS, 128, QROWS), lambda j, i: (0, 0, 0))],
        [tspec, kspec, kspec],
        [jax.ShapeDtypeStruct((NKV, NCH, HD, QROWS), F32),
         jax.ShapeDtypeStruct((NKV, S, HD), F32), jax.ShapeDtypeStruct((NKV, S, HD), F32)],
        (qt, kh, kt, vh, dot_, lse, delta, bias), sem=("parallel", "arbitrary"), rider=rider)


CONV_BLK = 256
CONV_COL0 = 1536 // CONV_BLK


def _shift_down(u, j, row):
    return jnp.where(row >= j, pltpu.roll(u, j, 0), 0.0)


def _conv_pre(u, w_ref, b_ref, row):
    y = b_ref[...] + w_ref[CONV_K - 1:CONV_K, :] * u
    for j in range(1, CONV_K):
        y = y + w_ref[CONV_K - 1 - j:CONV_K - j, :] * _shift_down(u, j, row)
    return y


def _conv_fwd(proj, convw, convb):
    def body(u_ref, w_ref, b_ref, o_ref):
        u = u_ref[...]
        row = lax.broadcasted_iota(jnp.int32, u.shape, 0)
        y = _conv_pre(u, w_ref, b_ref, row)
        o_ref[...] = y * _sigmoid(y)

    return pl.pallas_call(
        body, name="conv_fwd", grid=(CONV_C // CONV_BLK,),
        in_specs=[pl.BlockSpec((S, CONV_BLK), lambda i: (0, CONV_COL0 + i)),
                  pl.BlockSpec((CONV_K, CONV_BLK), lambda i: (0, i)),
                  pl.BlockSpec((1, CONV_BLK), lambda i: (0, i))],
        out_specs=pl.BlockSpec((S, CONV_BLK), lambda i: (0, i)),
        out_shape=jax.ShapeDtypeStruct((S, CONV_C), F32), compiler_params=_cparams("parallel"),
    )(proj, convw, convb)


def _conv_bwd(dact, proj, convw, convb, dproj):
    def body(da_ref, u_ref, w_ref, b_ref, buf_ref, du_ref, dw_ref, db_ref):
        u = u_ref[...]
        row = lax.broadcasted_iota(jnp.int32, u.shape, 0)
        y = _conv_pre(u, w_ref, b_ref, row)
        sg = _sigmoid(y)
        dy = da_ref[...] * (sg * (1.0 + y * (1.0 - sg)))
        db_ref[...] = jnp.sum(dy, axis=0, keepdims=True)
        du = w_ref[CONV_K - 1:CONV_K, :] * dy
        r8 = lax.broadcasted_iota(jnp.int32, (8, CONV_BLK), 0)
        dw = jnp.where(r8 == CONV_K - 1, jnp.sum(dy * u, axis=0, keepdims=True), 0.0)
        for j in range(1, CONV_K):
            du = du + w_ref[CONV_K - 1 - j:CONV_K - j, :] * jnp.where(row < S - j, pltpu.roll(dy, S - j, 0), 0.0)
            dw = dw + jnp.where(r8 == CONV_K - 1 - j,
                                jnp.sum(dy * _shift_down(u, j, row), axis=0, keepdims=True), 0.0)
        du_ref[...] = du.astype(BF16)
        dw_ref[...] = dw

    return pl.pallas_call(
        body, name="conv_bwd", grid=(CONV_C // CONV_BLK,),
        in_specs=[pl.BlockSpec((S, CONV_BLK), lambda i: (0, i)),
                  pl.BlockSpec((S, CONV_BLK), lambda i: (0, CONV_COL0 + i)),
                  pl.BlockSpec((CONV_K, CONV_BLK), lambda i: (0, i)),
                  pl.BlockSpec((1, CONV_BLK), lambda i: (0, i)), pl.BlockSpec(memory_space=pl.ANY)],
        out_specs=[pl.BlockSpec((S, CONV_BLK), lambda i: (0, CONV_COL0 + i)),
                   pl.BlockSpec((8, CONV_BLK), lambda i: (0, i)), pl.BlockSpec((1, CONV_BLK), lambda i: (0, i))],
        out_shape=[jax.ShapeDtypeStruct(dproj.shape, BF16), jax.ShapeDtypeStruct((8, CONV_C), F32),
                   jax.ShapeDtypeStruct((1, CONV_C), F32)],
        input_output_aliases={4: 0}, compiler_params=_cparams("parallel"),
    )(dact, proj, convw, convb, dproj)


NPAIR = 8


def _ssd_scalars(dtr_ref, dtb_ref, alog_ref):
    z = dtr_ref[...] + dtb_ref[...]
    dt = jnp.maximum(z, 0.0) + jnp.log(1.0 + jnp.exp(-jnp.abs(z)))
    a = -jnp.exp(alog_ref[...])
    r = lax.broadcasted_iota(jnp.int32, (128, 128), 0)
    c = lax.broadcasted_iota(jnp.int32, (128, 128), 1)
    tri = (r >= c).astype(F32)
    cs = _dot_exact(tri, dt * a)
    return z, dt, a, cs, r, c


def _by_lane(cs, dt):
    head = lax.broadcasted_iota(jnp.int32, (128, SSM_W), 0)
    lane = lax.broadcasted_iota(jnp.int32, (128, SSM_W), 1)
    sel = (head == lane // HD).astype(F32)
    cs_l = _dot_exact(cs, sel, "b")
    last_l = cs_l[127:128, :]
    return sel, jnp.exp(cs_l), jnp.exp(last_l - cs_l), _dot_exact(dt, sel, "b")


def _pair_terms(cs, h1, h2):
    return (cs[:, h1:h1 + 1], cs[:, h2:h2 + 1],
            jnp.exp(cs[127:128, h1:h1 + 1]), jnp.exp(cs[127:128, h2:h2 + 1]))


def _gate_norm(y, zv, w):
    yg = y * (zv * _sigmoid(zv))
    outs, rs = [], []
    for g in range(2):
        blk = yg[:, 512 * g:512 * (g + 1)]
        r = lax.rsqrt(jnp.mean(blk * blk, axis=-1, keepdims=True) + EPS)
        outs.append(blk * r)
        rs.append(r)
    return jnp.concatenate(outs, axis=1), rs, yg


def _ssd_fwd(xbc, proj, dtb, alog, dskip_l, ssmw):
    def body(x_ref, b_ref, c_ref, dtr_ref, z_ref, dtb_ref, alog_ref, dsk_ref, w_ref, y_ref, yn_ref, hp_ref, h_ref):
        @pl.when(pl.program_id(0) == 0)
        def _():
            h_ref[...] = jnp.zeros_like(h_ref)

        _, dt, _, cs, r, c = _ssd_scalars(dtr_ref, dtb_ref, alog_ref)
        cst = cs.T
        causal = r >= c
        lo = c < HD
        _, e_all, dte_all, dt_all = _by_lane(cs, dt)
        hp_ref[...] = h_ref[...]
        for g in range(2):
            bg = b_ref[:, 128 * g:128 * (g + 1)]
            cg = c_ref[:, 128 * g:128 * (g + 1)]
            cb = _dot(cg, bg, NT)
            for j in range(4):
                pj = 4 * g + j
                h1, h2 = 2 * pj, 2 * pj + 1
                sl = slice(128 * pj, 128 * (pj + 1))
                xp = x_ref[:, sl]
                c1, c2, cd1, cd2 = _pair_terms(cs, h1, h2)
                e_l, dte_l = e_all[:, sl], dte_all[:, sl]
                xdt = xp * dt_all[:, sl]
                m1 = cb * jnp.exp(jnp.where(causal, c1 - cst[h1:h1 + 1, :], NEG))
                m2 = cb * jnp.exp(jnp.where(causal, c2 - cst[h2:h2 + 1, :], NEG))
                yd = jnp.where(lo, _dot(m1, xdt, NN), _dot(m2, xdt, NN))
                hp = h_ref[pj]
                yo = _dot(cg, hp, NT) * e_l
                st = _dot(xdt * dte_l, bg, TN)
                h_ref[pj] = hp * jnp.where(r < HD, cd1, cd2) + st
                y_ref[:, sl] = yd + yo + dsk_ref[:, sl] * xp
        yn, _, _ = _gate_norm(y_ref[...], z_ref[...], w_ref[...])
        yn_ref[...] = (yn * w_ref[...]).astype(BF16)

    return pl.pallas_call(
        body, name="ssd_fwd", grid=(NCH,),
        in_specs=[pl.BlockSpec((128, SSM_W), lambda i: (i, 0)),
                  pl.BlockSpec((128, 256), lambda i: (i, 4)), pl.BlockSpec((128, 256), lambda i: (i, 5)),
                  pl.BlockSpec((128, 128), lambda i: (i, COL_DT // 128)),
                  pl.BlockSpec((128, SSM_W), lambda i: (i, 3)),
                  pl.BlockSpec((1, 128), lambda i: (0, 0)), pl.BlockSpec((1, 128), lambda i: (0, 0)),
                  pl.BlockSpec((1, SSM_W), lambda i: (0, 0)), pl.BlockSpec((1, SSM_W), lambda i: (0, 0))],
        out_specs=[pl.BlockSpec((128, SSM_W), lambda i: (i, 0)), pl.BlockSpec((128, SSM_W), lambda i: (i, 0)),
                   pl.BlockSpec((None, NPAIR, 128, 128), lambda i: (i, 0, 0, 0))],
        out_shape=[jax.ShapeDtypeStruct((S, SSM_W), F32), jax.ShapeDtypeStruct((S, SSM_W), BF16),
                   jax.ShapeDtypeStruct((NCH, NPAIR, 128, 128), F32)],
        scratch_shapes=[pltpu.VMEM((NPAIR, 128, 128), F32)],
        compiler_params=_cparams("arbitrary"),
    )(xbc, xbc, xbc, proj, proj, dtb, alog, dskip_l, ssmw)


def _ssd_bwd(dmixed, y, xbc, proj, hprev, dtb, alog, dskip_l, ssmw, rider=None):
    def body(dyn_ref, y_ref, x_ref, b_ref, c_ref, dtr_ref, z_ref, hp_ref, dtb_ref, alog_ref, dsk_ref, w_ref,
             dxbc_ref, dz_ref, ddt_ref, dw_ref, dsc_ref, g_ref):
        @pl.when(pl.program_id(0) == 0)
        def _():
            g_ref[...] = jnp.zeros_like(g_ref)
            dsc_ref[...] = jnp.zeros_like(dsc_ref)

        z, dt, a, cs, r, c = _ssd_scalars(dtr_ref, dtb_ref, alog_ref)
        cst = cs.T
        causal = r >= c
        lo = c < HD

        yv = y_ref[...]
        zv = z_ref[...]
        wv = w_ref[...]
        ygn, rs, yg = _gate_norm(yv, zv, wv)
        dyn = dyn_ref[...]
        _acc_rows(dw_ref, dyn * ygn)
        dynw = dyn * wv
        parts = []
        for g in range(2):
            sl = slice(512 * g, 512 * (g + 1))
            a_g, n_g = dynw[:, sl], ygn[:, sl]
            parts.append(rs[g] * (a_g - n_g * jnp.mean(a_g * n_g, axis=-1, keepdims=True)))
        dyg = jnp.concatenate(parts, axis=1)
        sz = _sigmoid(zv)
        dz_ref[...] = (dyg * yv * (sz * (1.0 + zv * (1.0 - sz)))).astype(BF16)
        dy_all = dyg * (zv * sz)

        dcs_cols = jnp.zeros((128, 128), F32)
        dcs_rows = jnp.zeros((128, 128), F32)
        sel, e_all, dte_all, dt_all = _by_lane(cs, dt)
        x_all, b_all, c_all, dsk_all = x_ref[...], b_ref[...], c_ref[...], dsk_ref[...]
        hp_all, g_all = hp_ref[...], g_ref[...]
        g_new, dx_parts, db_parts, dc_parts = [], [], [], []
        dyx_parts, ryo_parts, qx_parts, dxx_parts, gh_parts = [], [], [], [], []
        for g in range(2):
            bg = b_all[:, 128 * g:128 * (g + 1)]
            cg = c_all[:, 128 * g:128 * (g + 1)]
            cb = _dot(cg, bg, NT)
            dcb = jnp.zeros((128, 128), F32)
            db_acc = jnp.zeros((128, NST), F32)
            dc_acc = jnp.zeros((128, NST), F32)
            for j in range(4):
                pj = 4 * g + j
                h1, h2 = 2 * pj, 2 * pj + 1
                sl = slice(128 * pj, 128 * (pj + 1))
                xp = x_all[:, sl]
                dyp = dy_all[:, sl]
                c1, c2, cd1, cd2 = _pair_terms(cs, h1, h2)
                e_l, dte_l, dt_l = e_all[:, sl], dte_all[:, sl], dt_all[:, sl]
                xdt = xp * dt_l
                hp = hp_all[pj]
                gp = g_all[pj]
                dyx_parts.append(dyp * xp)
                dzs = dyp * e_l
                dc_acc = dc_acc + _dot(dzs, hp, NN)
                ryo_parts.append(dyp * (_dot(cg, hp, NT) * e_l))
                qm = _dot(bg, gp, NT)
                dxdt = qm * dte_l
                qx_parts.append(qm * xdt)
                db_acc = db_acc + _dot(xdt * dte_l, gp, NN)
                gh_parts.append(gp * hp)
                g_new.append(_dot(dzs, cg, TN) + jnp.where(r < HD, cd1, cd2) * gp)
                for hh, ch, msk in ((h1, c1, lo), (h2, c2, jnp.logical_not(lo))):
                    lm = jnp.exp(jnp.where(causal, ch - cst[hh:hh + 1, :], NEG))
                    mm = cb * lm
                    dm = jnp.where(causal, _dot(jnp.where(msk, dyp, 0.0), xdt, NT), 0.0)
                    w = dm * mm
                    dcs_cols = dcs_cols + jnp.where(c == hh, jnp.sum(w, axis=1, keepdims=True), 0.0)
                    dcs_rows = dcs_rows + jnp.where(r == hh, jnp.sum(w, axis=0, keepdims=True), 0.0)
                    dcb = dcb + dm * lm
                    dxdt = dxdt + jnp.where(msk, _dot(mm, dyp, TN), 0.0)
                dxx_parts.append(dxdt * xp)
                dx_parts.append(dsk_all[:, sl] * dyp + dxdt * dt_l)
            db_parts.append(db_acc + _dot(dcb, cg, TN))
            dc_parts.append(dc_acc + _dot(dcb, bg, NN))
        g_ref[...] = jnp.stack(g_new)
        dxbc_ref[...] = jnp.concatenate(dx_parts + db_parts + dc_parts, axis=1)

        selt = (lax.broadcasted_iota(jnp.int32, (SSM_W, 128), 0) // HD
                == lax.broadcasted_iota(jnp.int32, (SSM_W, 128), 1)).astype(F32)

        def by_head(parts):
            return _dot_exact(jnp.concatenate(parts, axis=1), selt, "b")

        ddt_x = by_head(dxx_parts)
        dd_row = jnp.sum(by_head(dyx_parts), axis=0, keepdims=True)
        t_all = by_head(qx_parts) * jnp.exp(cs[127:128, :] - cs)
        gh = jnp.sum(_dot_exact(sel, jnp.concatenate(gh_parts, axis=0)), axis=1, keepdims=True)
        gh_row = jnp.broadcast_to(gh, (128, 128)).T[0:1, :]
        at_end = jnp.sum(t_all, axis=0, keepdims=True) + gh_row * jnp.exp(cs[127:128, :])
        dcs = by_head(ryo_parts) - t_all + dcs_cols + jnp.where(r == 127, at_end, 0.0) - dcs_rows.T
        dad = _dot_exact((c >= r).astype(F32), dcs)
        ddt = dad * a + ddt_x
        ddtr = jnp.where(c < 16, ddt * _sigmoid(z), 0.0)
        ddt_ref[...] = ddtr.astype(BF16)
        r8 = lax.broadcasted_iota(jnp.int32, (8, 128), 0)
        dsc_ref[...] += (jnp.where(r8 == 0, jnp.sum(ddtr, axis=0, keepdims=True), 0.0)
                         + jnp.where(r8 == 1, jnp.sum(dad * dt, axis=0, keepdims=True) * a, 0.0)
                         + jnp.where(r8 == 2, dd_row, 0.0))

    rev = NCH - 1
    return _call(
        body, "ssd_bwd", (NCH,),
        [pl.BlockSpec((128, SSM_W), lambda i: (rev - i, 0)),
         pl.BlockSpec((128, SSM_W), lambda i: (rev - i, 0)),
         pl.BlockSpec((128, SSM_W), lambda i: (rev - i, 0)),
         pl.BlockSpec((128, 256), lambda i: (rev - i, 4)), pl.BlockSpec((128, 256), lambda i: (rev - i, 5)),
         pl.BlockSpec((128, 128), lambda i: (rev - i, COL_DT // 128)),
         pl.BlockSpec((128, SSM_W), lambda i: (rev - i, 3)),
         pl.BlockSpec((None, NPAIR, 128, 128), lambda i: (rev - i, 0, 0, 0)),
         pl.BlockSpec((1, 128), lambda i: (0, 0)), pl.BlockSpec((1, 128), lambda i: (0, 0)),
         pl.BlockSpec((1, SSM_W), lambda i: (0, 0)), pl.BlockSpec((1, SSM_W), lambda i: (0, 0))],
        [pl.BlockSpec((128, CONV_C), lambda i: (rev - i, 0)),
         pl.BlockSpec((128, SSM_W), lambda i: (rev - i, 3)),
         pl.BlockSpec((128, 128), lambda i: (rev - i, 0)),
         pl.BlockSpec((1, SSM_W), lambda i: (0, 0)), pl.BlockSpec((8, 128), lambda i: (0, 0))],
        [jax.ShapeDtypeStruct((S, CONV_C), F32), jax.ShapeDtypeStruct((S, WIN_PAD), BF16),
         jax.ShapeDtypeStruct((S, 128), BF16), jax.ShapeDtypeStruct((1, SSM_W), F32),
         jax.ShapeDtypeStruct((8, 128), F32)],
        (dmixed, y, xbc, xbc, xbc, proj, proj, hprev, dtb, alog, dskip_l, ssmw),
        [pltpu.VMEM((NPAIR, 128, 128), F32)], ("arbitrary",), rider)


def _cast_stack(name, slot, arrs, tr, tc):
    n = len(arrs)
    rows, cols = arrs[0].shape

    def body(s_ref, *refs):
        for i in range(n):
            refs[n][i] = refs[i][...].astype(BF16)

    return pl.pallas_call(
        body, name=name,
        grid_spec=pltpu.PrefetchScalarGridSpec(
            num_scalar_prefetch=1, grid=(rows // tr, cols // tc),
            in_specs=[pl.BlockSpec((tr, tc), lambda i, j, sr: (i, j))] * n,
            out_specs=pl.BlockSpec((None, n, tr, tc), lambda i, j, sr: (sr[0], 0, i, j))),
        out_shape=jax.ShapeDtypeStruct((NSH, n, rows, cols), BF16),
        compiler_params=_cparams("parallel", "parallel"),
    )(slot, *arrs)


def _pair_sum(name, c_idx, ps, th):
    n = len(ps)
    _, rows, _ = ps[0].shape

    def body(c_ref, *refs):
        mine, whole, out, theirs = refs[:n], refs[n:2 * n], refs[2 * n:3 * n], refs[3 * n:4 * n]
        send, recv = refs[4 * n], refs[4 * n + 1]
        s, i = pl.program_id(0), pl.program_id(1)

        @pl.when((s == 0) & (i == 0))
        def _():
            x, y, c, _ = _place()
            cps = [_rcopy(whole[k].at[:, :, pl.ds((1 - c) * HALF, HALF)], theirs[k], send.at[k], recv.at[k],
                          (x, y, 1 - c)) for k in range(n)]
            for cp in cps:
                cp.start()
            for cp in cps:
                cp.wait()

        rows_i = slice(None) if th == rows else pl.ds(pl.multiple_of(i * th, th), th)
        for k in range(n):
            out[k][...] = (mine[k][...].astype(F32) + theirs[k][s, rows_i, :].astype(F32)).astype(BF16)

    spec = pl.BlockSpec((None, th, HALF), lambda s, i, cr: (s, i, 0))
    return pl.pallas_call(
        body, name=name,
        grid_spec=pltpu.PrefetchScalarGridSpec(
            num_scalar_prefetch=1, grid=(NSH, rows // th),
            in_specs=[pl.BlockSpec((None, th, HALF), lambda s, i, cr: (s, i, cr[0]))] * n + _any_specs(n),
            out_specs=[spec] * n,
            scratch_shapes=[pltpu.VMEM((NSH, rows, HALF), BF16)] * n
            + [pltpu.SemaphoreType.DMA((n,)), pltpu.SemaphoreType.DMA((n,))]),
        out_shape=[jax.ShapeDtypeStruct((NSH, rows, HALF), BF16)] * n,
        compiler_params=_cparams("arbitrary", "arbitrary"),
    )(c_idx, *ps, *ps)


def _chip_sum(name, place, cs, ts, th):
    n = len(ts)
    _, rows, _ = ts[0].shape

    def body(p_ref, *refs):
        for i in range(n):
            t = refs[n + i][...].astype(F32)
            refs[2 * n + i][...] = ((refs[i][...].astype(F32) + t[0]) + t[1]) + t[2]

    return pl.pallas_call(
        body, name=name,
        grid_spec=pltpu.PrefetchScalarGridSpec(
            num_scalar_prefetch=1, grid=(rows // th,),
            in_specs=[pl.BlockSpec((None, th, HALF), lambda i, pr: (pr[0], i, 0))] * n
            + [pl.BlockSpec((3, th, HALF), lambda i, pr: (0, i, 0))] * n,
            out_specs=[pl.BlockSpec((th, HALF), lambda i, pr: (i, pr[1]))] * n),
        out_shape=[jax.ShapeDtypeStruct((rows, D), F32)] * n, compiler_params=_cparams("parallel"),
    )(place, *cs, *ts)


def _adamw(name, ws, gs, ms, vs, tr, tc):
    n = len(ws)
    shape = ws[0].shape
    rows, cols, mid = shape[0], shape[-1], shape[1:-1]
    c1 = 1.0 / (1.0 - ADAM_B1 ** ADAM_STEP)
    c2 = 1.0 / (1.0 - ADAM_B2 ** ADAM_STEP)

    def body(*refs):
        for i in range(n):
            w, g, m, v = (refs[k * n + i][...] for k in range(4))
            m2 = ADAM_B1 * m + (1.0 - ADAM_B1) * g
            v2 = ADAM_B2 * v + (1.0 - ADAM_B2) * (g * g)
            refs[4 * n + 3 * i][...] = -ADAM_LR * ((m2 * c1) / (jnp.sqrt(v2 * c2) + ADAM_EPS) + ADAM_WD * w)
            refs[4 * n + 3 * i + 1][...] = m2
            refs[4 * n + 3 * i + 2][...] = v2

    spec = pl.BlockSpec((tr,) + mid + (tc,), lambda i, j: (i,) + (0,) * len(mid) + (j,))
    outs = pl.pallas_call(
        body, name=name, grid=(rows // tr, cols // tc), in_specs=[spec] * (4 * n), out_specs=[spec] * (3 * n),
        out_shape=[jax.ShapeDtypeStruct(shape, F32)] * (3 * n),
        compiler_params=_cparams("parallel", "parallel"),
    )(*ws, *gs, *ms, *vs)
    return [tuple(outs[3 * i:3 * i + 3]) for i in range(n)]


def _place():
    x, y, c = lax.axis_index("x"), lax.axis_index("y"), lax.axis_index("c")
    chips = [(1 - x, y), (x, 1 - y), (1 - x, 1 - y)]
    return x, y, c, chips


def _any_specs(n):
    return [pl.BlockSpec(memory_space=pl.ANY)] * n


def _rcopy(src, dst, send_sem, recv_sem, dev):
    return pltpu.make_async_remote_copy(src_ref=src, dst_ref=dst, send_sem=send_sem, recv_sem=recv_sem,
                                        device_id=dev, device_id_type=MESH)


def _gather_rider(bufs, views):
    n = len(bufs)

    def start(rin, rout, sems):
        send, recv = sems[0], sems[1]
        x, y, c, chips = _place()
        for j, chip in enumerate(chips):
            for b in range(n):
                mine = views[b](rout[b], 2 * x + y, c)
                _rcopy(mine, mine, send.at[j * n + b], recv.at[j * n + b], (chip[0], chip[1], c)).start()

    def finish(rin, rout, sems):
        send, recv, fsend, frecv = sems
        x, y, c, chips = _place()
        passed = []
        for j, chip in enumerate(chips):
            for b in range(n):
                landed = views[b](rout[b], 2 * chip[0] + chip[1], c)
                _rcopy(landed, landed, send.at[j * n + b], recv.at[j * n + b], (x, y, c)).wait_recv()
                fw = _rcopy(landed, landed, fsend.at[j * n + b], frecv.at[j * n + b], (x, y, 1 - c))
                fw.start()
                passed.append(fw)
        for j, chip in enumerate(chips):
            for b in range(n):
                other = views[b](rout[b], 2 * chip[0] + chip[1], 1 - c)
                _rcopy(other, other, fsend.at[j * n + b], frecv.at[j * n + b], (x, y, c)).wait_recv()
        for j, chip in enumerate(chips):
            for b in range(n):
                mine = views[b](rout[b], 2 * x + y, c)
                _rcopy(mine, mine, send.at[j * n + b], recv.at[j * n + b], (x, y, c)).wait_send()
        for fw in passed:
            fw.wait_send()

    return _Rider(list(bufs), [jax.ShapeDtypeStruct(a.shape, a.dtype) for a in bufs], {b: b for b in range(n)},
                  [pltpu.SemaphoreType.DMA((3 * n,))] * 4, start, finish)


def _small_gather_rider(cw):
    def descs(rin, rout, sems, x, y, c, chips):
        return [_rcopy(rin[0], rout[0].at[2 * x + y], sems[1].at[j], sems[2].at[j], (chip[0], chip[1], c))
                for j, chip in enumerate(chips)]

    def start(rin, rout, sems):
        x, y, c, chips = _place()
        pltpu.make_async_copy(rin[0], rout[0].at[2 * x + y], sems[0].at[0]).start()
        for cp in descs(rin, rout, sems, x, y, c, chips):
            cp.start()

    def finish(rin, rout, sems):
        x, y, c, chips = _place()
        for j, chip in enumerate(chips):
            _rcopy(rin[0], rout[0].at[2 * chip[0] + chip[1]], sems[1].at[j], sems[2].at[j], (x, y, c)).wait_recv()
        for cp in descs(rin, rout, sems, x, y, c, chips):
            cp.wait_send()
        pltpu.make_async_copy(rin[0], rout[0].at[2 * x + y], sems[0].at[0]).wait()

    return _Rider([cw], [jax.ShapeDtypeStruct((NSH,) + cw.shape, cw.dtype)], {},
                  [pltpu.SemaphoreType.DMA((1,)), pltpu.SemaphoreType.DMA((3,)), pltpu.SemaphoreType.DMA((3,))],
                  start, finish)


def _to_chips_rider(cs):
    n = len(cs)

    def descs(rin, rout, sems):
        x, y, c, chips = _place()
        return [_rcopy(rin[i].at[2 * chip[0] + chip[1]], rout[i].at[j], sems[0].at[j * n + i], sems[1].at[j * n + i],
                       (chip[0], chip[1], c)) for j, chip in enumerate(chips) for i in range(n)]

    def start(rin, rout, sems):
        for cp in descs(rin, rout, sems):
            cp.start()

    def finish(rin, rout, sems):
        for cp in descs(rin, rout, sems):
            cp.wait()

    return _Rider(list(cs), [jax.ShapeDtypeStruct((3,) + a.shape[1:], a.dtype) for a in cs], {},
                  [pltpu.SemaphoreType.DMA((3 * n,))] * 2, start, finish)


def _run_riders(name, riders):
    n_in = [len(r.operands) for r in riders]
    n_out = [len(r.out_shapes) for r in riders]
    n_sem = [len(r.sems) for r in riders]

    def body(*refs):
        parts, at = [], 0
        for counts in (n_in, n_out, n_sem):
            group = []
            for k in counts:
                group.append(refs[at:at + k])
                at += k
            parts.append(group)
        for i, r in enumerate(riders):
            r.start(parts[0][i], parts[1][i], parts[2][i])
        for i, r in enumerate(riders):
            r.finish(parts[0][i], parts[1][i], parts[2][i])

    aliases = {}
    for i, r in enumerate(riders):
        for k, v in r.aliases.items():
            aliases[sum(n_in[:i]) + k] = sum(n_out[:i]) + v
    res = pl.pallas_call(
        body, name=name, in_specs=_any_specs(sum(n_in)), out_specs=_any_specs(sum(n_out)),
        out_shape=[s for r in riders for s in r.out_shapes], input_output_aliases=aliases,
        scratch_shapes=[s for r in riders for s in r.sems],
    )(*[a for r in riders for a in r.operands])
    out, at = [], 0
    for k in n_out:
        out.append(list(res[at:at + k]))
        at += k
    return out


def _swap_halves(gs):
    n = len(gs)

    def body(*refs):
        dst, send, recv = refs[n:2 * n], refs[2 * n], refs[2 * n + 1]
        x, y, c, _ = _place()
        cps = []
        for i in range(n):
            mine = dst[i].at[:, pl.ds(c * HALF, HALF)]
            cps.append(pltpu.make_async_remote_copy(
                src_ref=mine, dst_ref=mine, send_sem=send.at[i], recv_sem=recv.at[i],
                device_id=(x, y, 1 - c), device_id_type=MESH))
        for cp in cps:
            cp.start()
        for i in range(n):
            other = dst[i].at[:, pl.ds((1 - c) * HALF, HALF)]
            pltpu.make_async_remote_copy(
                src_ref=other, dst_ref=other, send_sem=send.at[i], recv_sem=recv.at[i],
                device_id=(x, y, c), device_id_type=MESH).wait_recv()
        for cp in cps:
            cp.wait_send()

    return pl.pallas_call(
        body, name="grads_swap_halves", in_specs=_any_specs(n), out_specs=_any_specs(n),
        out_shape=[jax.ShapeDtypeStruct(g.shape, g.dtype) for g in gs],
        input_output_aliases={i: i for i in range(n)},
        scratch_shapes=[pltpu.SemaphoreType.DMA((n,)), pltpu.SemaphoreType.DMA((n,))],
    )(*gs)


SMALL_ROWS = 16


def _allreduce_small(vec):
    def body(v_ref, o_ref, buf, send, recv):
        x, y, c, _ = _place()
        me = 4 * x + 2 * y + c
        buf[me] = v_ref[...]
        cps = []
        for k in range(1, 8):
            peer = (x ^ (k >> 2), y ^ ((k >> 1) & 1), c ^ (k & 1))
            cps.append(pltpu.make_async_remote_copy(
                src_ref=v_ref, dst_ref=buf.at[me], send_sem=send.at[k - 1], recv_sem=recv.at[k - 1],
                device_id=peer, device_id_type=MESH))
        for cp in cps:
            cp.start()
        for k in range(1, 8):
            pltpu.make_async_remote_copy(
                src_ref=v_ref, dst_ref=buf.at[me ^ k], send_sem=send.at[k - 1], recv_sem=recv.at[k - 1],
                device_id=(x, y, c), device_id_type=MESH).wait_recv()
        for cp in cps:
            cp.wait_send()
        t = buf[0]
        for d in range(1, 8):
            t = t + buf[d]
        o_ref[...] = t

    return pl.pallas_call(
        body, name="allreduce_small",
        in_specs=[pl.BlockSpec(memory_space=pltpu.VMEM)], out_specs=pl.BlockSpec(memory_space=pltpu.VMEM),
        out_shape=jax.ShapeDtypeStruct((SMALL_ROWS, D), F32),
        scratch_shapes=[pltpu.VMEM((8, SMALL_ROWS, D), F32), pltpu.SemaphoreType.DMA((7,)),
                        pltpu.SemaphoreType.DMA((7,))],
    )(vec)


def _col_half(ref, slot, hc):
    return ref.at[slot, :, pl.ds(hc * HALF, HALF)]


def _stack_half(ref, slot, hc):
    return ref.at[slot, :, :, pl.ds(hc * HALF, HALF)]


def _row_tile(rows):
    for t in range(512, 15, -16):
        if rows % t == 0:
            return t
    return rows


def _same_shape_runs(arrs):
    runs, a = [], 0
    for b in range(1, len(arrs) + 1):
        if b == len(arrs) or arrs[b].shape != arrs[a].shape:
            runs.append((a, b))
            a = b
    return runs


class _Comm:
    def __init__(self):
        x, y, c = lax.axis_index("x"), lax.axis_index("y"), lax.axis_index("c")
        self.c_idx = jnp.reshape(c, (1,)).astype(jnp.int32)
        self.place = jnp.stack([2 * x + y, c]).astype(jnp.int32)
        self.groups = {}

    @staticmethod
    def gather(*bufs):
        return _gather_rider(list(bufs), [_col_half if b.ndim == 3 else _stack_half for b in bufs])

    def reduce_rider(self, tag, names, ps):
        csums = []
        for a, b in _same_shape_runs(ps):
            csums += _pair_sum("pair_sum_%s%d" % (tag, a), self.c_idx, ps[a:b], _row_tile(ps[a].shape[1]))
        self.groups[tag] = [names, csums, None]
        return _to_chips_rider(csums)

    def landed(self, tag, ts):
        self.groups[tag][2] = ts

    def finish(self):
        names, halves = [], []
        for tag, (group_names, csums, ts) in self.groups.items():
            names += group_names
            for a, b in _same_shape_runs(csums):
                halves += _chip_sum("chip_sum_%s%d" % (tag, a), self.place, csums[a:b], ts[a:b],
                                    _row_tile(csums[a].shape[1]))
        return dict(zip(names, _swap_halves(halves)))


ROPE_THETA = 10000.0
SMALL_1K = ("ffn1_pre_norm", "ffn1_post_norm", "mix_pre_norm", "ssm_norm", "mix_post_norm",
            "ffn2_pre_norm", "ffn2_post_norm")
SMALL_16 = ("dt_bias", "a_log", "d_skip")
OFF_CONVB = 7 * D
OFF_16 = OFF_CONVB + CONV_C
OFF_CONVW = OFF_16 + 48
OFF_LOSS = OFF_CONVW + CONV_K * CONV_C
SMALL_LEN = SMALL_ROWS * D


def _sds(shape, dtype):
    return jax.ShapeDtypeStruct(shape, dtype)


def _ridden(res, rider):
    return res if rider is not None else (res, None)


def _ffn_down(name, act, w, tail_of, rider=None):
    tail, o_specs, o_shapes = tail_of(TS)
    return _mm(name, [act, w.dn], NN, (S // TS,),
               [pl.BlockSpec((NSH, TS, FS), lambda i: (0, i, 0)),
                pl.BlockSpec((NSH, None, FS, D), lambda i: (0, w.d0, 0, 0))], o_specs, o_shapes, rider, tail)


def _ffn_dw(name, a, b, rider=None):
    return _mm(name, [a, b], TN, (NSH,),
               [pl.BlockSpec((None, S, FS), lambda s: (s, 0, 0)), pl.BlockSpec((S, D), lambda s: (0, 0))],
               pl.BlockSpec((None, FS, D), lambda s: (s, 0, 0)), _sds((NSH, FS, D), BF16), rider)


def _ffn_dn(name, dgate, dup, w, tail_of, rider=None):
    rows = TS // 2
    tail, o_specs, o_shapes = tail_of(rows)
    a2 = pl.BlockSpec((NSH, rows, FS), lambda i: (0, i, 0))
    return _mm(name, [dgate, w.gu, dup, w.gu], NN, (S // rows,),
               [a2, pl.BlockSpec((NSH, None, FS, D), lambda i: (0, w.g0, 0, 0)),
                a2, pl.BlockSpec((NSH, None, FS, D), lambda i: (0, w.g0 + 1, 0, 0))], o_specs, o_shapes, rider, tail)


def _out_proj_dx(dh, wout):
    def body(dh_ref, w_ref, dyn_ref, do_ref):
        dm = _dot(dh_ref[...], w_ref[...], NT)
        dyn_ref[...] = dm[:, D:]
        for b in range(TS // 128):
            for j, blk in enumerate(_rows_to_blocks(dm[128 * b:128 * (b + 1), :D])):
                do_ref[j, b] = blk.astype(BF16)

    return pl.pallas_call(
        body, name="out_proj_dx", grid=(S // TS,),
        in_specs=[pl.BlockSpec((TS, D), lambda i: (i, 0)), pl.BlockSpec((2 * D, D), lambda i: (0, 0))],
        out_specs=[pl.BlockSpec((TS, D), lambda i: (i, 0)),
                   pl.BlockSpec((NKV, TS // 128, HD, QROWS), lambda i: (0, i, 0, 0))],
        out_shape=[_sds((S, D), F32), _sds((NKV, NCH, HD, QROWS), BF16)], compiler_params=_cparams("parallel"),
    )(dh, wout)


def _heads(t, n):
    return t.reshape(S, n, HD).transpose(1, 0, 2)


def _unheads(t):
    return t.transpose(1, 0, 2).reshape(S, t.shape[0] * HD)


def _heads_t(t, n):
    return t.reshape(S, n, HD).transpose(1, 2, 0)


def _pad128(v):
    return jnp.pad(v, ((0, 0), (0, 128 - v.shape[1])))


def _local_step(x, positions, tgt, sp, gu1, d1, f2, wint, wout, convw, comm=None):
    inv_freq = ROPE_THETA ** (-jnp.arange(0, HD, 2, dtype=F32) / HD)
    ang = positions.astype(F32)[:, None] * inv_freq
    ang = jnp.concatenate([ang, ang, ang, ang], axis=-1)
    cos, sin = jnp.cos(ang), jnp.sin(ang)
    dtb, alog = _pad128(sp["dt_bias"]), _pad128(sp["a_log"])
    dskip_l = jnp.repeat(sp["d_skip"], HD, axis=1)
    convb = sp["conv_b"]

    n1 = _prenorm("prenorm1", x, sp["ffn1_pre_norm"])
    rider = comm.gather(d1) if comm else None
    (fg1, fu1, act1), got = _ridden(_ffn_up("ffn1_up", n1, _FfnW(gu1, 0, d1, 0), rider), rider)
    if comm:
        d1, = got
    w1 = _FfnW(gu1, 0, d1, 0)
    rider = comm.gather(wint) if comm else None
    (h1, x1, n2), got = _ridden(_ffn_down(
        "ffn1_down", act1, w1,
        lambda rows: _tail_postres(rows, x, sp["ffn1_post_norm"], 0.5, sp["mix_pre_norm"]), rider), rider)
    if comm:
        wint, = got
    wint_pad = jnp.pad(wint.reshape(WIN_COLS, D), ((0, WIN_PAD - WIN_COLS), (0, 0)))

    pw = WIN_PAD // 3
    proj = _mm("in_proj", [n2, wint_pad], NT, (S // TS, 3),
               [pl.BlockSpec((TS, D), lambda i, j: (i, 0)), pl.BlockSpec((pw, D), lambda i, j: (j, 0))],
               pl.BlockSpec((TS, pw), lambda i, j: (i, j)), _sds((S, WIN_PAD), F32))
    qt = _rope_q(proj, cos, sin)
    k_rot = _rope("rope_k", proj, D // KVW, KVW, cos, sin, 1.0, 1.0)
    v_bf = proj[:, D + KVW:D + 2 * KVW].astype(BF16)
    kh, vh = _heads(k_rot, NKV), _heads(v_bf, NKV)
    kt, vt = _heads_t(k_rot, NKV), _heads_t(v_bf, NKV)
    bias = _bias_table()
    rider = comm.gather(f2, wout) if comm else None
    (ot, lse, attn), got = _ridden(_attn_fwd(qt, kh, vt, bias, rider), rider)
    if comm:
        f2, wout = got
    w2 = _FfnW(f2, 0, f2, 2)
    wout = wout.reshape(2 * D, D)
    xbc = _conv_fwd(proj, convw, convb)
    y, yn, hprev = _ssd_fwd(xbc, proj, dtb, alog, dskip_l, sp["ssm_norm"])
    mixed = jnp.concatenate([attn, yn], axis=1)
    tail, o_specs, o_shapes = _tail_postres(TS, x1, sp["mix_post_norm"], 1.0, sp["ffn2_pre_norm"])
    h2, x2, n3 = _mm("out_proj", [mixed, wout], NN, (S // TS,),
                     [pl.BlockSpec((TS, 2 * D), lambda i: (i, 0)), pl.BlockSpec((2 * D, D), lambda i: (0, 0))],
                     o_specs, o_shapes, None, tail)

    fg2, fu2, act2 = _ffn_up("ffn2_up", n3, w2)
    dy, dh3, dp3, loss = _ffn_down(
        "ffn2_down", act2, w2, lambda rows: _tail_final(rows, x2, sp["ffn2_post_norm"], tgt, 0.5))

    dgate2, dup2 = _ffn_dact("ffn2_dact", dh3, w2, fg2, fu2)
    dws2 = [_ffn_dw("ffn2_dwg", dgate2, n3), _ffn_dw("ffn2_dwu", dup2, n3), _ffn_dw("ffn2_dwd", act2, dh3)]
    dx2, dh2, dg3, dp2 = _ffn_dn(
        "ffn2_dn", dgate2, dup2, w2,
        lambda rows: _tail_mid_bwd(rows, dy, x2, sp["ffn2_pre_norm"], h2, sp["mix_post_norm"], 1.0))

    dyn, dot_ = _out_proj_dx(dh2, wout)
    dwout = _mm("out_proj_dw", [mixed, dh2], TN, (2,),
                [pl.BlockSpec((S, D), lambda m: (0, m)), pl.BlockSpec((S, D), lambda m: (0, 0))],
                pl.BlockSpec((D, D), lambda m: (m, 0)), _sds((2 * D, D), BF16))
    dwout = dwout.reshape(NSH, 2 * D // NSH, D)

    def riding(tag, names, ps, call):
        rider = comm.reduce_rider(tag, names, ps) if comm else None
        res, got = _ridden(call(rider), rider)
        if comm:
            comm.landed(tag, got)
        return res

    dxbc, dproj, ddt, dssm, dsc = _ssd_bwd(dyn, y, xbc, proj, hprev, dtb, alog, dskip_l, sp["ssm_norm"])
    dproj, dcw8, dcb = _conv_bwd(dxbc, proj, convw, convb, dproj)
    delta = _attn_delta(ot, dot_)
    dqt, dkh, dvh = riding("a", BIG[3:6] + ("w_out",), dws2 + [dwout], lambda rider: _attn_bwd(
        qt, kh, kt, vh, dot_, lse, delta, bias, rider))
    dproj = _rope_dq(dqt, cos, sin, dproj)
    dproj = _rope("rope_dk", _unheads(dkh), 0, KVW, cos, sin, -1.0, 1.0, into=(dproj, D // KVW))
    dproj = lax.dynamic_update_slice(dproj, _unheads(dvh).astype(BF16), (0, D + KVW))
    dproj = lax.dynamic_update_slice(dproj, ddt, (0, COL_DT))
    dwint = _mm("in_proj_dw", [dproj, n2], TN, (3,),
                [pl.BlockSpec((S, pw), lambda j: (0, j)), pl.BlockSpec((S, D), lambda j: (0, 0))],
                pl.BlockSpec((pw, D), lambda j: (j, 0)), _sds((WIN_PAD, D), BF16))
    dwint = dwint[:WIN_COLS].reshape(NSH, WIN_SH, D)

    tail, o_specs, o_shapes = _tail_mid_bwd(TS, dx2, x1, sp["mix_pre_norm"], h1, sp["ffn1_post_norm"], 0.5)
    dx1, dh1, dg2, dp1 = riding("b", ("w_in",), [dwint], lambda rider: _mm(
        "in_proj_dx", [dproj, wint_pad], NN, (S // TS,),
        [pl.BlockSpec((TS, WIN_PAD), lambda i: (i, 0)), pl.BlockSpec((WIN_PAD, D), lambda i: (0, 0))],
        o_specs, o_shapes, rider, tail))

    dwd1 = _ffn_dw("ffn1_dwd", act1, dh1)
    dgate1, dup1 = riding("d", BIG[2:3], [dwd1], lambda rider: _ffn_dact("ffn1_dact", dh1, w1, fg1, fu1, rider))
    dwg1, dwu1 = _ffn_dw("ffn1_dwg", dgate1, n1), _ffn_dw("ffn1_dwu", dup1, n1)
    grad_x, dg1 = riding("g", BIG[0:2], [dwg1, dwu1], lambda rider: _ffn_dn(
        "ffn1_dn", dgate1, dup1, w1, lambda rows: _tail_first_bwd(rows, dx1, x, sp["ffn1_pre_norm"]), rider))
    dws1 = [dwg1, dwu1, dwd1]

    small = jnp.concatenate([
        dg1[0], dp1[0], dg2[0], dssm[0], dp2[0], dg3[0], dp3[0], dcb[0],
        dsc[0, :16], dsc[1, :16], dsc[2, :16], dcw8[:CONV_K].reshape(-1), loss[0, :1]])
    small = jnp.pad(small, (0, SMALL_LEN - small.shape[0])).reshape(SMALL_ROWS, D)
    if comm is None:
        return grad_x, dws1 + dws2 + [dwint, dwout], small
    return grad_x, comm.finish(), small


WEIGHTS = ("ffn1_pre_norm", "ffn1_w_gate", "ffn1_w_up", "ffn1_w_down", "ffn1_post_norm", "mix_pre_norm", "w_in",
           "conv_w", "conv_b", "dt_bias", "a_log", "d_skip", "ssm_norm", "w_out", "mix_post_norm", "ffn2_pre_norm",
           "ffn2_w_gate", "ffn2_w_up", "ffn2_w_down", "ffn2_post_norm")
BIG = ("ffn1_w_gate", "ffn1_w_up", "ffn1_w_down", "ffn2_w_gate", "ffn2_w_up", "ffn2_w_down", "w_in", "w_out")
TRANSPOSED = ("ffn1_w_gate", "ffn1_w_up", "ffn2_w_gate", "ffn2_w_up", "w_in")
SMALL_ORDER = SMALL_1K + ("conv_b",) + SMALL_16
CONVW_SH = CONV_C // NSH


def _shard2d(t, name):
    return t[0].T if name in TRANSPOSED else t[0]


def _unshard2d(t, name):
    return (t.T if name in TRANSPOSED else t)[None]


def _rows3d(t):
    return t.transpose(2, 0, 1)


def _pack_small(d, prefix, shard_of_convw):
    flat = jnp.concatenate([d[prefix + n][0] for n in SMALL_ORDER] + [shard_of_convw.reshape(-1)])
    return jnp.pad(flat, (0, SMALL_LEN - flat.shape[0])).reshape(SMALL_ROWS, D)


def _unpack_small(block, like):
    flat = block.reshape(-1)
    out, off = {}, 0
    for n in SMALL_ORDER:
        size = like[n].shape[1]
        out[n] = flat[off:off + size].reshape(1, size)
        off += size
    out["conv_w"] = flat[off:off + CONV_K * CONVW_SH].reshape(1, CONV_K, CONVW_SH)
    return out


def kernel(x, positions, ffn1_pre_norm, ffn1_w_gate, ffn1_w_up, ffn1_w_down, ffn1_post_norm, mix_pre_norm, w_in, conv_w, conv_b, dt_bias, a_log, d_skip, ssm_norm, w_out, mix_post_norm, ffn2_pre_norm, ffn2_w_gate, ffn2_w_up, ffn2_w_down, ffn2_post_norm, loss_target, m_ffn1_pre_norm, m_ffn1_w_gate, m_ffn1_w_up, m_ffn1_w_down, m_ffn1_post_norm, m_mix_pre_norm, m_w_in, m_conv_w, m_conv_b, m_dt_bias, m_a_log, m_d_skip, m_ssm_norm, m_w_out, m_mix_post_norm, m_ffn2_pre_norm, m_ffn2_w_gate, m_ffn2_w_up, m_ffn2_w_down, m_ffn2_post_norm, v_ffn1_pre_norm, v_ffn1_w_gate, v_ffn1_w_up, v_ffn1_w_down, v_ffn1_post_norm, v_mix_pre_norm, v_w_in, v_conv_w, v_conv_b, v_dt_bias, v_a_log, v_d_skip, v_ssm_norm, v_w_out, v_mix_post_norm, v_ffn2_pre_norm, v_ffn2_w_gate, v_ffn2_w_up, v_ffn2_w_down, v_ffn2_post_norm):
    given = dict(locals())
    xi, yi = lax.axis_index("x"), lax.axis_index("y")

    shard = jnp.reshape(2 * xi + yi, (1,)).astype(jnp.int32)
    big = {p + n: _shard2d(given[p + n], n) for n in BIG for p in ("", "m_", "v_")}
    gu1 = _cast_stack("cast_ffn1_gate_up", shard, [big[n] for n in BIG[0:2]], 176, D)
    d1 = _cast_stack("cast_ffn1_down", shard, [big[BIG[2]]], 176, D)
    f2 = _cast_stack("cast_ffn2", shard, [big[n] for n in BIG[3:6]], 176, D)
    winsh = _cast_stack("cast_w_in", shard, [big["w_in"]], WIN_SH, 256).reshape(NSH, WIN_SH, D)
    woutsh = _cast_stack("cast_w_out", shard, [big["w_out"]], 256, D).reshape(NSH, 2 * D // NSH, D)
    comm = _Comm()
    (gu1,), (cwf,) = _run_riders("gather_ffn1_gate_up", [comm.gather(gu1), _small_gather_rider(conv_w[0])])
    convw = cwf.transpose(1, 0, 2).reshape(CONV_K, CONV_C)

    sp = {n: given[n] for n in SMALL_ORDER}
    grad_x, big_grads, small = _local_step(x[0], positions[0], loss_target[0], sp, gu1, d1, f2, winsh, woutsh,
                                           convw, comm)

    tot = _allreduce_small(small).reshape(-1)
    loss = tot[OFF_LOSS]
    small_grads, off = {}, 0
    for n in SMALL_ORDER:
        size = given[n].shape[1]
        small_grads[n] = tot[off:off + size].reshape(1, size)
        off += size
    dconvw = tot[OFF_CONVW:OFF_CONVW + CONV_K * CONV_C].reshape(CONV_K, NSH, CONVW_SH)
    dconvw = lax.dynamic_index_in_dim(dconvw, 2 * xi + yi, axis=1, keepdims=False)
    small_grads["conv_w"] = dconvw.reshape(1, CONV_K, CONVW_SH)

    upd = {}
    for names, tr in ((BIG[0:3], 176), (BIG[3:6], 176), (BIG[7:8], 256)):
        res = _adamw("adamw_" + names[0], [big[n] for n in names], [big_grads[n] for n in names],
                     [big["m_" + n] for n in names], [big["v_" + n] for n in names], tr, D)
        for n, r in zip(names, res):
            upd[n] = tuple(_unshard2d(t, n) for t in r)
    g_win = big_grads["w_in"].reshape(WIN_SH, 1, D)
    res, = _adamw("adamw_w_in", [_rows3d(w_in)], [g_win], [_rows3d(m_w_in)], [_rows3d(v_w_in)], WIN_SH // 4, D)
    upd["w_in"] = tuple(t.transpose(1, 2, 0) for t in res)
    (dl, m2, v2), = _adamw(
        "adamw_small", [_pack_small(given, "", conv_w[0])], [_pack_small(small_grads, "", dconvw)],
        [_pack_small(given, "m_", m_conv_w[0])], [_pack_small(given, "v_", v_conv_w[0])], SMALL_ROWS, D)
    dl, m2, v2 = (_unpack_small(t, given) for t in (dl, m2, v2))
    for n in SMALL_ORDER + ("conv_w",):
        upd[n] = (dl[n], m2[n], v2[n])

    grads = dict(small_grads)
    grads.update({n: _unshard2d(g, n) for n, g in big_grads.items() if n != "w_in"})
    grads["w_in"] = g_win.transpose(1, 2, 0)
    return (loss, grad_x[None], *[grads[n] for n in WEIGHTS], *[upd[n][0] for n in WEIGHTS],
            *[upd[n][1] for n in WEIGHTS], *[upd[n][2] for n in WEIGHTS])
```

```python
import functools
import typing

import jax
import jax.numpy as jnp
from jax import lax
from jax.experimental import pallas as pl
from jax.experimental.pallas import tpu as pltpu

F32 = jnp.float32
BF16 = jnp.bfloat16

S = 2048
D = 1024
FF = 2816
NSH = 4
FS = FF // NSH
HALF = D // 2
HD = 64
NKV = 4
NQ_PER_KV = 4
KVW = NKV * HD
QCOLS = NQ_PER_KV * HD
CONV_C = 1536
CONV_K = 4
SSM_W = 1024
NST = 128
NCH = S // 128
WIN_COLS = 4112
WIN_SH = WIN_COLS // NSH
WIN_PAD = 4224
COL_DT = 4096
EPS = 1e-6
NEG = -1e30

ADAM_LR = 0.001
ADAM_B1 = 0.9
ADAM_B2 = 0.999
ADAM_EPS = 1e-08
ADAM_WD = 0.01
ADAM_STEP = 10

VMEM_LIMIT = 56 * 1024 * 1024
TS = 512
TR = 256

NN = (((1,), (0,)), ((), ()))
NT = (((1,), (1,)), ((), ()))
TN = (((0,), (0,)), ((), ()))
MESH = pl.DeviceIdType.MESH


def _cparams(*sem):
    return pltpu.CompilerParams(dimension_semantics=sem, vmem_limit_bytes=VMEM_LIMIT)


def _dot(a, b, dims):
    return lax.dot_general(a.astype(BF16), b.astype(BF16), dims, preferred_element_type=F32)


def _bf16_pieces(v):
    hi = v.astype(BF16)
    rest = v - hi.astype(F32)
    mid = rest.astype(BF16)
    return hi, mid, (rest - mid.astype(F32)).astype(BF16)


def _dot_exact(a, b, ones="a"):
    if ones == "a":
        sel = a.astype(BF16)
        parts = [lax.dot_general(sel, p, NN, preferred_element_type=F32) for p in _bf16_pieces(b)]
    else:
        sel = b.astype(BF16)
        parts = [lax.dot_general(p, sel, NN, preferred_element_type=F32) for p in _bf16_pieces(a)]
    return (parts[2] + parts[1]) + parts[0]


def _sigmoid(v):
    return 1.0 / (1.0 + jnp.exp(-v))


class _Rider(typing.NamedTuple):
    operands: list
    out_shapes: list
    aliases: dict
    sems: list
    start: typing.Callable
    finish: typing.Callable


def _call(body, name, grid, in_specs, out_specs, out_shape, operands, scratch=(), sem=(), rider=None):
    multi = isinstance(out_shape, (list, tuple))
    if rider is None:
        return pl.pallas_call(
            body, name=name, grid=grid, in_specs=in_specs, out_specs=out_specs, out_shape=out_shape,
            scratch_shapes=list(scratch), compiler_params=_cparams(*sem))(*operands)
    outs = list(out_shape) if multi else [out_shape]
    ospecs = list(out_specs) if multi else [out_specs]
    n_in, n_out, n_scr = len(operands), len(outs), len(scratch)
    ri, ro = len(rider.operands), len(rider.out_shapes)

    def wrapped(*refs):
        o0 = n_in + ri
        s0 = o0 + n_out + ro
        rin, rout, rsem = refs[n_in:o0], refs[o0 + n_out:s0], refs[s0 + n_scr:]
        ids = [pl.program_id(a) for a in range(len(grid))]
        first = functools.reduce(jnp.logical_and, [i == 0 for i in ids])
        last = functools.reduce(jnp.logical_and, [i == g - 1 for i, g in zip(ids, grid)])

        @pl.when(first)
        def _():
            rider.start(rin, rout, rsem)

        body(*refs[:n_in], *refs[o0:o0 + n_out], *refs[s0:s0 + n_scr])

        @pl.when(last)
        def _():
            rider.finish(rin, rout, rsem)

    hbm = pl.BlockSpec(memory_space=pl.ANY)
    res = pl.pallas_call(
        wrapped, name=name, grid=grid, in_specs=list(in_specs) + [hbm] * ri, out_specs=ospecs + [hbm] * ro,
        out_shape=outs + list(rider.out_shapes), scratch_shapes=list(scratch) + list(rider.sems),
        input_output_aliases={n_in + k: n_out + v for k, v in rider.aliases.items()},
        compiler_params=_cparams(*(("arbitrary",) * len(grid))))(*operands, *rider.operands)
    main = list(res[:n_out])
    return (main if multi else main[0]), list(res[n_out:])


class _Tail(typing.NamedTuple):
    fn: typing.Callable
    operands: list
    in_specs: list


def _mm(name, operands, dims, grid, in_specs, o_spec, out_shape, rider=None, tail=None):
    npairs = len(operands) // 2
    extra = [] if tail is None else list(tail.operands)
    nin = 2 * npairs + len(extra)

    def body(*refs):
        t = None
        for i in range(npairs):
            a, b = refs[2 * i], refs[2 * i + 1]
            parts = [(a[s], b[s]) for s in range(a.shape[0])] if len(a.shape) == 3 else [(a[...], b[...])]
            for pa, pb in parts:
                d = _dot(pa, pb, dims)
                t = d if t is None else t + d
        if tail is None:
            refs[nin][...] = t.astype(refs[nin].dtype)
        else:
            tail.fn(t, refs[2 * npairs:nin], refs[nin:])

    sem = ("parallel" if tail is None else "arbitrary",) * len(grid)
    specs = list(in_specs) + ([] if tail is None else list(tail.in_specs))
    return _call(body, name, grid, specs, o_spec, out_shape, list(operands) + extra, (), sem, rider)


class _FfnW(typing.NamedTuple):
    gu: jax.Array
    g0: int
    dn: jax.Array
    d0: int


def _ffn_up(name, n, w, rider=None):
    def body(n_ref, wg_ref, wu_ref, fg_ref, fu_ref, a_ref):
        nb = n_ref[...]
        g = _dot(nb, wg_ref[...], NT)
        u = _dot(nb, wu_ref[...], NT)
        sg = _sigmoid(g)
        silu = g * sg
        fg_ref[...] = (u * (sg * (1.0 + g * (1.0 - sg)))).astype(BF16)
        fu_ref[...] = silu.astype(BF16)
        a_ref[...] = (silu * u).astype(BF16)

    out = jax.ShapeDtypeStruct((NSH, S, FS), BF16)
    ospec = pl.BlockSpec((None, TS, FS), lambda s, i: (s, i, 0))
    return _call(
        body, name, (NSH, S // TS),
        [pl.BlockSpec((TS, D), lambda s, i: (i, 0)),
         pl.BlockSpec((None, None, FS, D), lambda s, i: (s, w.g0, 0, 0)),
         pl.BlockSpec((None, None, FS, D), lambda s, i: (s, w.g0 + 1, 0, 0))],
        [ospec, ospec, ospec], [out, out, out], (n, w.gu, w.gu), sem=("parallel", "parallel"), rider=rider)


def _ffn_dact(name, dh, w, fgate, fup, rider=None):
    def body(dh_ref, wd_ref, fg_ref, fu_ref, dg_ref, du_ref):
        da = _dot(dh_ref[...], wd_ref[...], NT)
        dg_ref[...] = (da * fg_ref[...].astype(F32)).astype(BF16)
        du_ref[...] = (da * fu_ref[...].astype(F32)).astype(BF16)

    out = jax.ShapeDtypeStruct((NSH, S, FS), BF16)
    aspec = pl.BlockSpec((None, TS, FS), lambda s, i: (s, i, 0))
    return _call(
        body, name, (NSH, S // TS),
        [pl.BlockSpec((TS, D), lambda s, i: (i, 0)),
         pl.BlockSpec((None, None, FS, D), lambda s, i: (s, w.d0, 0, 0)), aspec, aspec],
        [aspec, aspec], [out, out], (dh, w.dn, fgate, fup), sem=("parallel", "parallel"), rider=rider)


def _rstd(v):
    return lax.rsqrt(jnp.mean(v * v, axis=-1, keepdims=True) + EPS)


def _row_spec():
    return pl.BlockSpec((TR, D), lambda i: (i, 0))


def _vec_spec():
    return pl.BlockSpec((1, D), lambda i: (0, 0))


def _acc_rows(ref, v):
    @pl.when(pl.program_id(0) == 0)
    def _():
        ref[...] = jnp.zeros_like(ref)
    ref[...] += jnp.sum(v, axis=0, keepdims=True)


def _prenorm(name, x, g):
    def body(x_ref, g_ref, n_ref):
        xv = x_ref[...]
        n_ref[...] = (xv * _rstd(xv) * g_ref[...]).astype(BF16)

    return pl.pallas_call(
        body, name=name, grid=(S // TR,), in_specs=[_row_spec(), _vec_spec()], out_specs=_row_spec(),
        out_shape=jax.ShapeDtypeStruct((S, D), BF16), compiler_params=_cparams("parallel"),
    )(x, g)


def _rows_spec(rows):
    return pl.BlockSpec((rows, D), lambda i: (i, 0))


def _rows_f32():
    return jax.ShapeDtypeStruct((S, D), F32)


def _rows_bf16():
    return jax.ShapeDtypeStruct((S, D), BF16)


def _vec_f32():
    return jax.ShapeDtypeStruct((1, D), F32)


def _tail_postres(rows, x, p, alpha, gnext):
    def fn(h, ins, outs):
        x_ref, p_ref, g_ref = ins
        h_ref, xo_ref, n_ref = outs
        h_ref[...] = h
        xo = x_ref[...] + alpha * (h * _rstd(h) * p_ref[...])
        xo_ref[...] = xo
        n_ref[...] = (xo * _rstd(xo) * g_ref[...]).astype(BF16)

    rs = _rows_spec(rows)
    return (_Tail(fn, [x, p, gnext], [rs, _vec_spec(), _vec_spec()]), [rs, rs, rs],
            [_rows_f32(), _rows_f32(), _rows_bf16()])


def _tail_final(rows, x, p, tgt, alpha):
    def fn(h, ins, outs):
        x_ref, p_ref, t_ref = ins
        dy_ref, dh_ref, dp_ref, loss_ref = outs
        r = _rstd(h)
        hn = h * r
        pv = p_ref[...]
        e = x_ref[...] + alpha * (hn * pv) - t_ref[...]
        dy = e * (1.0 / D)
        dy_ref[...] = dy
        du = alpha * dy * pv
        dh_ref[...] = (r * (du - hn * jnp.mean(du * hn, axis=-1, keepdims=True))).astype(BF16)
        _acc_rows(dp_ref, alpha * dy * hn)
        part = 0.5 * jnp.sum(jnp.mean(e * e, axis=-1, keepdims=True), axis=0, keepdims=True)
        _acc_rows(loss_ref, jnp.broadcast_to(part, (1, 128)))

    rs = _rows_spec(rows)
    return (_Tail(fn, [x, p, tgt], [rs, _vec_spec(), rs]),
            [rs, rs, _vec_spec(), pl.BlockSpec((1, 128), lambda i: (0, 0))],
            [_rows_f32(), _rows_bf16(), _vec_f32(), jax.ShapeDtypeStruct((1, 128), F32)])


def _norm_bwd(dn, xv, g_ref, dg_ref):
    r = _rstd(xv)
    xn = xv * r
    dng = dn * g_ref[...]
    _acc_rows(dg_ref, dn * xn)
    return r * (dng - xn * jnp.mean(dng * xn, axis=-1, keepdims=True))


def _tail_mid_bwd(rows, dres, x, g, h, p, alpha):
    def fn(dn, ins, outs):
        dr_ref, x_ref, g_ref, h_ref, p_ref = ins
        dx_ref, dh_ref, dg_ref, dp_ref = outs
        dx = dr_ref[...] + _norm_bwd(dn, x_ref[...], g_ref, dg_ref)
        dx_ref[...] = dx
        hv = h_ref[...]
        r = _rstd(hv)
        hn = hv * r
        du = alpha * dx * p_ref[...]
        dh_ref[...] = (r * (du - hn * jnp.mean(du * hn, axis=-1, keepdims=True))).astype(BF16)
        _acc_rows(dp_ref, alpha * dx * hn)

    rs = _rows_spec(rows)
    return (_Tail(fn, [dres, x, g, h, p], [rs, rs, _vec_spec(), rs, _vec_spec()]),
            [rs, rs, _vec_spec(), _vec_spec()], [_rows_f32(), _rows_bf16(), _vec_f32(), _vec_f32()])


def _tail_first_bwd(rows, dres, x, g):
    def fn(dn, ins, outs):
        dr_ref, x_ref, g_ref = ins
        dx_ref, dg_ref = outs
        dx_ref[...] = dr_ref[...] + _norm_bwd(dn, x_ref[...], g_ref, dg_ref)

    rs = _rows_spec(rows)
    return (_Tail(fn, [dres, x, g], [rs, rs, _vec_spec()]), [rs, _vec_spec()], [_rows_f32(), _vec_f32()])


def _rotate(t, c128, s128, sign, scale):
    width = t.shape[1]
    c = jnp.tile(c128, (1, width // 128))
    sn = jnp.tile(s128, (1, width // 128))
    lane = lax.broadcasted_iota(jnp.int32, t.shape, 1) & (HD - 1)
    rot = jnp.where(lane < HD // 2, -pltpu.roll(t, width - HD // 2, 1), pltpu.roll(t, HD // 2, 1))
    return (t * c + sign * (rot * sn)) * scale


def _rows_to_blocks(y):
    out = []
    for j in range(NKV):
        yt = y[:, QCOLS * j:QCOLS * (j + 1)].T
        out.append(jnp.concatenate([yt[HD * g:HD * (g + 1)] for g in range(NQ_PER_KV)], axis=1))
    return out


def _blocks_to_rows(blocks):
    cols = []
    for b in blocks:
        stacked = jnp.concatenate([b[:, 128 * g:128 * (g + 1)] for g in range(NQ_PER_KV)], axis=0)
        cols.append(stacked.T)
    return jnp.concatenate(cols, axis=1)


def _rope_q(proj, cos, sin):
    def body(t_ref, c_ref, s_ref, o_ref):
        y = _rotate(t_ref[...], c_ref[...], s_ref[...], 1.0, HD ** -0.5)
        for j, blk in enumerate(_rows_to_blocks(y)):
            o_ref[j] = blk.astype(BF16)

    return pl.pallas_call(
        body, name="rope_q", grid=(NCH,),
        in_specs=[pl.BlockSpec((128, D), lambda i: (i, 0)),
                  pl.BlockSpec((128, 128), lambda i: (i, 0)), pl.BlockSpec((128, 128), lambda i: (i, 0))],
        out_specs=pl.BlockSpec((NKV, None, HD, QROWS), lambda i: (0, i, 0, 0)),
        out_shape=jax.ShapeDtypeStruct((NKV, NCH, HD, QROWS), BF16), compiler_params=_cparams("parallel"),
    )(proj, cos, sin)


def _rope_dq(dqt, cos, sin, dproj):
    def body(t_ref, c_ref, s_ref, buf_ref, o_ref):
        t = _blocks_to_rows([t_ref[j] for j in range(NKV)])
        o_ref[...] = _rotate(t, c_ref[...], s_ref[...], -1.0, HD ** -0.5).astype(BF16)

    return pl.pallas_call(
        body, name="rope_dq", grid=(NCH,),
        in_specs=[pl.BlockSpec((NKV, None, HD, QROWS), lambda i: (0, i, 0, 0)),
                  pl.BlockSpec((128, 128), lambda i: (i, 0)), pl.BlockSpec((128, 128), lambda i: (i, 0)),
                  pl.BlockSpec(memory_space=pl.ANY)],
        out_specs=pl.BlockSpec((128, D), lambda i: (i, 0)),
        out_shape=jax.ShapeDtypeStruct(dproj.shape, BF16), input_output_aliases={3: 0},
        compiler_params=_cparams("parallel"),
    )(dqt, cos, sin, dproj)


def _rope(name, src, col_block, width, cos, sin, sign, scale, into=None):
    def body(t_ref, c_ref, s_ref, *rest):
        rest[-1][...] = _rotate(t_ref[...].astype(F32), c_ref[...], s_ref[...], sign, scale).astype(BF16)

    in_specs = [pl.BlockSpec((TR, width), lambda i: (i, col_block)),
                pl.BlockSpec((TR, 128), lambda i: (i, 0)), pl.BlockSpec((TR, 128), lambda i: (i, 0))]
    if into is None:
        return pl.pallas_call(
            body, name=name, grid=(S // TR,), in_specs=in_specs,
            out_specs=pl.BlockSpec((TR, width), lambda i: (i, 0)),
            out_shape=jax.ShapeDtypeStruct((S, width), BF16), compiler_params=_cparams("parallel"),
        )(src, cos, sin)
    buf, out_block = into
    return pl.pallas_call(
        body, name=name, grid=(S // TR,), in_specs=in_specs + [pl.BlockSpec(memory_space=pl.ANY)],
        out_specs=pl.BlockSpec((TR, width), lambda i: (i, out_block)),
        out_shape=jax.ShapeDtypeStruct(buf.shape, BF16), input_output_aliases={3: 0},
        compiler_params=_cparams("parallel"),
    )(src, cos, sin, buf)


QROWS = NQ_PER_KV * 128


NBIAS = NCH + 1
KV_PER_STEP = 4


def _bias_table():
    db = lax.broadcasted_iota(jnp.int32, (NBIAS, 128, QROWS), 0) - 1
    ki = lax.broadcasted_iota(jnp.int32, (NBIAS, 128, QROWS), 1)
    qi = lax.broadcasted_iota(jnp.int32, (NBIAS, 128, QROWS), 2) & 127
    d = db * 128 + qi - ki
    cnt = ((d <= 128).astype(F32) + (((d & 3) == 0) & (d <= 512)).astype(F32) + ((d & 15) == 0).astype(F32))
    return jnp.where((d >= 0) & (cnt > 0.0), jnp.log(jnp.maximum(cnt, 1.0)), NEG)


def _qt_spec():
    return pl.BlockSpec((None, None, HD, QROWS), lambda j, i: (j, i, 0, 0))


def _stat_spec():
    return pl.BlockSpec((None, None, 1, QROWS), lambda j, i: (j, i, 0, 0))


def _attn_fwd(qt, kh, vt, bias, rider=None):
    def body(q_ref, k_ref, v_ref, b_ref, o_ref, lse_ref, rows_ref):
        qb = pl.program_id(1)

        def keys(carry, off, size, bias_):
            out = []
            for h in range(KV_PER_STEP):
                m, l, acc = carry[3 * h:3 * h + 3]
                s = _dot(k_ref[h, pl.ds(off, size), :], q_ref[h], NN) + bias_
                m_new = jnp.maximum(m, jnp.max(s, axis=0, keepdims=True))
                p = jnp.exp(s - m_new)
                a = jnp.exp(m - m_new)
                out += [m_new, a * l + jnp.sum(p, axis=0, keepdims=True),
                        a * acc + _dot(v_ref[h, :, pl.ds(off, size)], p, NN)]
            return tuple(out)

        def pair(i, carry):
            bias2 = jnp.concatenate([b_ref[qb - 2 * i + 1], b_ref[qb - 2 * i]], axis=0)
            return keys(carry, pl.multiple_of(i * 256, 256), 256, bias2)

        init = (jnp.full((1, QROWS), NEG, F32), jnp.zeros((1, QROWS), F32), jnp.zeros((HD, QROWS), F32))
        res = lax.fori_loop(0, (qb + 1) // 2, pair, init * KV_PER_STEP)
        res = lax.cond(qb % 2 == 0,
                       lambda c: keys(c, pl.multiple_of(qb * 128, 128), 128, b_ref[1]), lambda c: c, res)
        outs = []
        for h in range(KV_PER_STEP):
            m, l, acc = res[3 * h:3 * h + 3]
            outs.append(acc / l)
            o_ref[h] = outs[h]
            lse_ref[h] = m + jnp.log(l)
        rows_ref[...] = _blocks_to_rows(outs).astype(BF16)

    kvs = KV_PER_STEP
    qspec = pl.BlockSpec((kvs, None, HD, QROWS), lambda j, i: (j, i, 0, 0))
    return _call(
        body, "attn_fwd", (NKV // kvs, NCH),
        [qspec, pl.BlockSpec((kvs, S, HD), lambda j, i: (j, 0, 0)),
         pl.BlockSpec((kvs, HD, S), lambda j, i: (j, 0, 0)),
         pl.BlockSpec((NBIAS, 128, QROWS), lambda j, i: (0, 0, 0))],
        [qspec, pl.BlockSpec((kvs, None, 1, QROWS), lambda j, i: (j, i, 0, 0)),
         pl.BlockSpec((128, QCOLS * kvs), lambda j, i: (i, j))],
        [jax.ShapeDtypeStruct((NKV, NCH, HD, QROWS), F32), jax.ShapeDtypeStruct((NKV, NCH, 1, QROWS), F32),
         jax.ShapeDtypeStruct((S, D), BF16)],
        (qt, kh, vt, bias), sem=("parallel", "parallel"), rider=rider)


def _attn_delta(ot, dot_):
    def body(o_ref, do_ref, dl_ref):
        dl_ref[...] = jnp.sum(o_ref[...] * do_ref[...].astype(F32), axis=1, keepdims=True)

    spec = pl.BlockSpec((None, NCH, HD, QROWS), lambda j: (j, 0, 0, 0))
    return pl.pallas_call(
        body, name="attn_delta", grid=(NKV,), in_specs=[spec, spec],
        out_specs=pl.BlockSpec((None, NCH, 1, QROWS), lambda j: (j, 0, 0, 0)),
        out_shape=jax.ShapeDtypeStruct((NKV, NCH, 1, QROWS), F32), compiler_params=_cparams("parallel"),
    )(ot, dot_)


def _attn_bwd(qt, kh, kt, vh, dot_, lse, delta, bias, rider=None):
    def body(qt_ref, k_ref, kt_ref, v_ref, dot_ref, lse_ref, dl_ref, b_ref, dq_ref, dk_ref, dv_ref):
        kb = pl.program_id(1)

        @pl.when(kb == 0)
        def _():
            dq_ref[...] = jnp.zeros_like(dq_ref)

        def blocks(carry, qbs):
            out = list(carry)
            for h in range(KV_PER_STEP):
                k, kt_, v = k_ref[h], kt_ref[h], v_ref[h]
                for qb in qbs:
                    st = _dot(k, qt_ref[h, qb], NN) + b_ref[qb - kb + 1]
                    pt = jnp.exp(st - lse_ref[h, qb])
                    dst = pt * (_dot(v, dot_ref[h, qb], NN) - dl_ref[h, qb])
                    dq_ref[h, qb] += _dot(kt_, dst, NN)
                    out[2 * h] = out[2 * h] + _dot(dst, qt_ref[h, qb], NT)
                    out[2 * h + 1] = out[2 * h + 1] + _dot(pt, dot_ref[h, qb], NT)
            return tuple(out)

        res = (jnp.zeros((128, HD), F32),) * (2 * KV_PER_STEP)
        res = lax.cond(kb % 2 == 1, lambda c: blocks(c, (kb,)), lambda c: c, res)
        res = lax.fori_loop((kb + 1) // 2, NCH // 2, lambda j, c: blocks(c, (2 * j, 2 * j + 1)), res)
        for h in range(KV_PER_STEP):
            dk_ref[h] = res[2 * h]
            dv_ref[h] = res[2 * h + 1]

    kvs = KV_PER_STEP
    tspec = pl.BlockSpec((kvs, NCH, HD, QROWS), lambda j, i: (j, 0, 0, 0))
    kspec = pl.BlockSpec((kvs, 128, HD), lambda j, i: (j, i, 0))
    sspec = pl.BlockSpec((kvs, NCH, 1, QROWS), lambda j, i: (j, 0, 0, 0))
    return _call(
        body, "attn_bwd", (NKV // kvs, NCH),
        [tspec, kspec, pl.BlockSpec((kvs, HD, 128), lambda j, i: (j, 0, i)), kspec, tspec,
         sspec, sspec, pl.BlockSpec((NBIAS, 128, QROWS), lambda j, i: (0, 0, 0))],
        [tspec, kspec, kspec],
        [jax.ShapeDtypeStruct((NKV, NCH, HD, QROWS), F32),
         jax.ShapeDtypeStruct((NKV, S, HD), F32), jax.ShapeDtypeStruct((NKV, S, HD), F32)],
        (qt, kh, kt, vh, dot_, lse, delta, bias), sem=("parallel", "arbitrary"), rider=rider)


CONV_BLK = 256
CONV_COL0 = 1536 // CONV_BLK


def _shift_down(u, j, row):
    return jnp.where(row >= j, pltpu.roll(u, j, 0), 0.0)


def _conv_pre(u, w_ref, b_ref, row):
    y = b_ref[...] + w_ref[CONV_K - 1:CONV_K, :] * u
    for j in range(1, CONV_K):
        y = y + w_ref[CONV_K - 1 - j:CONV_K - j, :] * _shift_down(u, j, row)
    return y


def _conv_fwd(proj, convw, convb):
    def body(u_ref, w_ref, b_ref, o_ref):
        u = u_ref[...]
        row = lax.broadcasted_iota(jnp.int32, u.shape, 0)
        y = _conv_pre(u, w_ref, b_ref, row)
        o_ref[...] = y * _sigmoid(y)

    return pl.pallas_call(
        body, name="conv_fwd", grid=(CONV_C // CONV_BLK,),
        in_specs=[pl.BlockSpec((S, CONV_BLK), lambda i: (0, CONV_COL0 + i)),
                  pl.BlockSpec((CONV_K, CONV_BLK), lambda i: (0, i)),
                  pl.BlockSpec((1, CONV_BLK), lambda i: (0, i))],
        out_specs=pl.BlockSpec((S, CONV_BLK), lambda i: (0, i)),
        out_shape=jax.ShapeDtypeStruct((S, CONV_C), F32), compiler_params=_cparams("parallel"),
    )(proj, convw, convb)


def _conv_bwd(dact, proj, convw, convb, dproj):
    def body(da_ref, u_ref, w_ref, b_ref, buf_ref, du_ref, dw_ref, db_ref):
        u = u_ref[...]
        row = lax.broadcasted_iota(jnp.int32, u.shape, 0)
        y = _conv_pre(u, w_ref, b_ref, row)
        sg = _sigmoid(y)
        dy = da_ref[...] * (sg * (1.0 + y * (1.0 - sg)))
        db_ref[...] = jnp.sum(dy, axis=0, keepdims=True)
        du = w_ref[CONV_K - 1:CONV_K, :] * dy
        r8 = lax.broadcasted_iota(jnp.int32, (8, CONV_BLK), 0)
        dw = jnp.where(r8 == CONV_K - 1, jnp.sum(dy * u, axis=0, keepdims=True), 0.0)
        for j in range(1, CONV_K):
            du = du + w_ref[CONV_K - 1 - j:CONV_K - j, :] * jnp.where(row < S - j, pltpu.roll(dy, S - j, 0), 0.0)
            dw = dw + jnp.where(r8 == CONV_K - 1 - j,
                                jnp.sum(dy * _shift_down(u, j, row), axis=0, keepdims=True), 0.0)
        du_ref[...] = du.astype(BF16)
        dw_ref[...] = dw

    return pl.pallas_call(
        body, name="conv_bwd", grid=(CONV_C // CONV_BLK,),
        in_specs=[pl.BlockSpec((S, CONV_BLK), lambda i: (0, i)),
                  pl.BlockSpec((S, CONV_BLK), lambda i: (0, CONV_COL0 + i)),
                  pl.BlockSpec((CONV_K, CONV_BLK), lambda i: (0, i)),
                  pl.BlockSpec((1, CONV_BLK), lambda i: (0, i)), pl.BlockSpec(memory_space=pl.ANY)],
        out_specs=[pl.BlockSpec((S, CONV_BLK), lambda i: (0, CONV_COL0 + i)),
                   pl.BlockSpec((8, CONV_BLK), lambda i: (0, i)), pl.BlockSpec((1, CONV_BLK), lambda i: (0, i))],
        out_shape=[jax.ShapeDtypeStruct(dproj.shape, BF16), jax.ShapeDtypeStruct((8, CONV_C), F32),
                   jax.ShapeDtypeStruct((1, CONV_C), F32)],
        input_output_aliases={4: 0}, compiler_params=_cparams("parallel"),
    )(dact, proj, convw, convb, dproj)


NPAIR = 8


def _ssd_scalars(dtr_ref, dtb_ref, alog_ref):
    z = dtr_ref[...] + dtb_ref[...]
    dt = jnp.maximum(z, 0.0) + jnp.log(1.0 + jnp.exp(-jnp.abs(z)))
    a = -jnp.exp(alog_ref[...])
    r = lax.broadcasted_iota(jnp.int32, (128, 128), 0)
    c = lax.broadcasted_iota(jnp.int32, (128, 128), 1)
    tri = (r >= c).astype(F32)
    cs = _dot_exact(tri, dt * a)
    return z, dt, a, cs, r, c


def _by_lane(cs, dt):
    head = lax.broadcasted_iota(jnp.int32, (128, SSM_W), 0)
    lane = lax.broadcasted_iota(jnp.int32, (128, SSM_W), 1)
    sel = (head == lane // HD).astype(F32)
    cs_l = _dot_exact(cs, sel, "b")
    last_l = cs_l[127:128, :]
    return sel, jnp.exp(cs_l), jnp.exp(last_l - cs_l), _dot_exact(dt, sel, "b")


def _pair_terms(cs, h1, h2):
    return (cs[:, h1:h1 + 1], cs[:, h2:h2 + 1],
            jnp.exp(cs[127:128, h1:h1 + 1]), jnp.exp(cs[127:128, h2:h2 + 1]))


def _gate_norm(y, zv, w):
    yg = y * (zv * _sigmoid(zv))
    outs, rs = [], []
    for g in range(2):
        blk = yg[:, 512 * g:512 * (g + 1)]
        r = lax.rsqrt(jnp.mean(blk * blk, axis=-1, keepdims=True) + EPS)
        outs.append(blk * r)
        rs.append(r)
    return jnp.concatenate(outs, axis=1), rs, yg


def _ssd_fwd(xbc, proj, dtb, alog, dskip_l, ssmw):
    def body(x_ref, b_ref, c_ref, dtr_ref, z_ref, dtb_ref, alog_ref, dsk_ref, w_ref, y_ref, yn_ref, hp_ref, h_ref):
        @pl.when(pl.program_id(0) == 0)
        def _():
            h_ref[...] = jnp.zeros_like(h_ref)

        _, dt, _, cs, r, c = _ssd_scalars(dtr_ref, dtb_ref, alog_ref)
        cst = cs.T
        causal = r >= c
        lo = c < HD
        _, e_all, dte_all, dt_all = _by_lane(cs, dt)
        hp_ref[...] = h_ref[...]
        for g in range(2):
            bg = b_ref[:, 128 * g:128 * (g + 1)]
            cg = c_ref[:, 128 * g:128 * (g + 1)]
            cb = _dot(cg, bg, NT)
            for j in range(4):
                pj = 4 * g + j
                h1, h2 = 2 * pj, 2 * pj + 1
                sl = slice(128 * pj, 128 * (pj + 1))
                xp = x_ref[:, sl]
                c1, c2, cd1, cd2 = _pair_terms(cs, h1, h2)
                e_l, dte_l = e_all[:, sl], dte_all[:, sl]
                xdt = xp * dt_all[:, sl]
                m1 = cb * jnp.exp(jnp.where(causal, c1 - cst[h1:h1 + 1, :], NEG))
                m2 = cb * jnp.exp(jnp.where(causal, c2 - cst[h2:h2 + 1, :], NEG))
                yd = jnp.where(lo, _dot(m1, xdt, NN), _dot(m2, xdt, NN))
                hp = h_ref[pj]
                yo = _dot(cg, hp, NT) * e_l
                st = _dot(xdt * dte_l, bg, TN)
                h_ref[pj] = hp * jnp.where(r < HD, cd1, cd2) + st
                y_ref[:, sl] = yd + yo + dsk_ref[:, sl] * xp
        yn, _, _ = _gate_norm(y_ref[...], z_ref[...], w_ref[...])
        yn_ref[...] = (yn * w_ref[...]).astype(BF16)

    return pl.pallas_call(
        body, name="ssd_fwd", grid=(NCH,),
        in_specs=[pl.BlockSpec((128, SSM_W), lambda i: (i, 0)),
                  pl.BlockSpec((128, 256), lambda i: (i, 4)), pl.BlockSpec((128, 256), lambda i: (i, 5)),
                  pl.BlockSpec((128, 128), lambda i: (i, COL_DT // 128)),
                  pl.BlockSpec((128, SSM_W), lambda i: (i, 3)),
                  pl.BlockSpec((1, 128), lambda i: (0, 0)), pl.BlockSpec((1, 128), lambda i: (0, 0)),
                  pl.BlockSpec((1, SSM_W), lambda i: (0, 0)), pl.BlockSpec((1, SSM_W), lambda i: (0, 0))],
        out_specs=[pl.BlockSpec((128, SSM_W), lambda i: (i, 0)), pl.BlockSpec((128, SSM_W), lambda i: (i, 0)),
                   pl.BlockSpec((None, NPAIR, 128, 128), lambda i: (i, 0, 0, 0))],
        out_shape=[jax.ShapeDtypeStruct((S, SSM_W), F32), jax.ShapeDtypeStruct((S, SSM_W), BF16),
                   jax.ShapeDtypeStruct((NCH, NPAIR, 128, 128), F32)],
        scratch_shapes=[pltpu.VMEM((NPAIR, 128, 128), F32)],
        compiler_params=_cparams("arbitrary"),
    )(xbc, xbc, xbc, proj, proj, dtb, alog, dskip_l, ssmw)


def _ssd_bwd(dmixed, y, xbc, proj, hprev, dtb, alog, dskip_l, ssmw, rider=None):
    def body(dyn_ref, y_ref, x_ref, b_ref, c_ref, dtr_ref, z_ref, hp_ref, dtb_ref, alog_ref, dsk_ref, w_ref,
             dxbc_ref, dz_ref, ddt_ref, dw_ref, dsc_ref, g_ref):
        @pl.when(pl.program_id(0) == 0)
        def _():
            g_ref[...] = jnp.zeros_like(g_ref)
            dsc_ref[...] = jnp.zeros_like(dsc_ref)

        z, dt, a, cs, r, c = _ssd_scalars(dtr_ref, dtb_ref, alog_ref)
        cst = cs.T
        causal = r >= c
        lo = c < HD

        yv = y_ref[...]
        zv = z_ref[...]
        wv = w_ref[...]
        ygn, rs, yg = _gate_norm(yv, zv, wv)
        dyn = dyn_ref[...]
        _acc_rows(dw_ref, dyn * ygn)
        dynw = dyn * wv
        parts = []
        for g in range(2):
            sl = slice(512 * g, 512 * (g + 1))
            a_g, n_g = dynw[:, sl], ygn[:, sl]
            parts.append(rs[g] * (a_g - n_g * jnp.mean(a_g * n_g, axis=-1, keepdims=True)))
        dyg = jnp.concatenate(parts, axis=1)
        sz = _sigmoid(zv)
        dz_ref[...] = (dyg * yv * (sz * (1.0 + zv * (1.0 - sz)))).astype(BF16)
        dy_all = dyg * (zv * sz)

        dcs_cols = jnp.zeros((128, 128), F32)
        dcs_rows = jnp.zeros((128, 128), F32)
        sel, e_all, dte_all, dt_all = _by_lane(cs, dt)
        x_all, b_all, c_all, dsk_all = x_ref[...], b_ref[...], c_ref[...], dsk_ref[...]
        hp_all, g_all = hp_ref[...], g_ref[...]
        g_new, dx_parts, db_parts, dc_parts = [], [], [], []
        dyx_parts, ryo_parts, qx_parts, dxx_parts, gh_parts = [], [], [], [], []
        for g in range(2):
            bg = b_all[:, 128 * g:128 * (g + 1)]
            cg = c_all[:, 128 * g:128 * (g + 1)]
            cb = _dot(cg, bg, NT)
            dcb = jnp.zeros((128, 128), F32)
            db_acc = jnp.zeros((128, NST), F32)
            dc_acc = jnp.zeros((128, NST), F32)
            for j in range(4):
                pj = 4 * g + j
                h1, h2 = 2 * pj, 2 * pj + 1
                sl = slice(128 * pj, 128 * (pj + 1))
                xp = x_all[:, sl]
                dyp = dy_all[:, sl]
                c1, c2, cd1, cd2 = _pair_terms(cs, h1, h2)
                e_l, dte_l, dt_l = e_all[:, sl], dte_all[:, sl], dt_all[:, sl]
                xdt = xp * dt_l
                hp = hp_all[pj]
                gp = g_all[pj]
                dyx_parts.append(dyp * xp)
                dzs = dyp * e_l
                dc_acc = dc_acc + _dot(dzs, hp, NN)
                ryo_parts.append(dyp * (_dot(cg, hp, NT) * e_l))
                qm = _dot(bg, gp, NT)
                dxdt = qm * dte_l
                qx_parts.append(qm * xdt)
                db_acc = db_acc + _dot(xdt * dte_l, gp, NN)
                gh_parts.append(gp * hp)
                g_new.append(_dot(dzs, cg, TN) + jnp.where(r < HD, cd1, cd2) * gp)
                for hh, ch, msk in ((h1, c1, lo), (h2, c2, jnp.logical_not(lo))):
                    lm = jnp.exp(jnp.where(causal, ch - cst[hh:hh + 1, :], NEG))
                    mm = cb * lm
                    dm = jnp.where(causal, _dot(jnp.where(msk, dyp, 0.0), xdt, NT), 0.0)
                    w = dm * mm
                    dcs_cols = dcs_cols + jnp.where(c == hh, jnp.sum(w, axis=1, keepdims=True), 0.0)
                    dcs_rows = dcs_rows + jnp.where(r == hh, jnp.sum(w, axis=0, keepdims=True), 0.0)
                    dcb = dcb + dm * lm
                    dxdt = dxdt + jnp.where(msk, _dot(mm, dyp, TN), 0.0)
                dxx_parts.append(dxdt * xp)
                dx_parts.append(dsk_all[:, sl] * dyp + dxdt * dt_l)
            db_parts.append(db_acc + _dot(dcb, cg, TN))
            dc_parts.append(dc_acc + _dot(dcb, bg, NN))
        g_ref[...] = jnp.stack(g_new)
        dxbc_ref[...] = jnp.concatenate(dx_parts + db_parts + dc_parts, axis=1)

        selt = (lax.broadcasted_iota(jnp.int32, (SSM_W, 128), 0) // HD
                == lax.broadcasted_iota(jnp.int32, (SSM_W, 128), 1)).astype(F32)

        def by_head(parts):
            return _dot_exact(jnp.concatenate(parts, axis=1), selt, "b")

        ddt_x = by_head(dxx_parts)
        dd_row = jnp.sum(by_head(dyx_parts), axis=0, keepdims=True)
        t_all = by_head(qx_parts) * jnp.exp(cs[127:128, :] - cs)
        gh = jnp.sum(_dot_exact(sel, jnp.concatenate(gh_parts, axis=0)), axis=1, keepdims=True)
        gh_row = jnp.broadcast_to(gh, (128, 128)).T[0:1, :]
        at_end = jnp.sum(t_all, axis=0, keepdims=True) + gh_row * jnp.exp(cs[127:128, :])
        dcs = by_head(ryo_parts) - t_all + dcs_cols + jnp.where(r == 127, at_end, 0.0) - dcs_rows.T
        dad = _dot_exact((c >= r).astype(F32), dcs)
        ddt = dad * a + ddt_x
        ddtr = jnp.where(c < 16, ddt * _sigmoid(z), 0.0)
        ddt_ref[...] = ddtr.astype(BF16)
        r8 = lax.broadcasted_iota(jnp.int32, (8, 128), 0)
        dsc_ref[...] += (jnp.where(r8 == 0, jnp.sum(ddtr, axis=0, keepdims=True), 0.0)
                         + jnp.where(r8 == 1, jnp.sum(dad * dt, axis=0, keepdims=True) * a, 0.0)
                         + jnp.where(r8 == 2, dd_row, 0.0))

    rev = NCH - 1
    return _call(
        body, "ssd_bwd", (NCH,),
        [pl.BlockSpec((128, SSM_W), lambda i: (rev - i, 0)),
         pl.BlockSpec((128, SSM_W), lambda i: (rev - i, 0)),
         pl.BlockSpec((128, SSM_W), lambda i: (rev - i, 0)),
         pl.BlockSpec((128, 256), lambda i: (rev - i, 4)), pl.BlockSpec((128, 256), lambda i: (rev - i, 5)),
         pl.BlockSpec((128, 128), lambda i: (rev - i, COL_DT // 128)),
         pl.BlockSpec((128, SSM_W), lambda i: (rev - i, 3)),
         pl.BlockSpec((None, NPAIR, 128, 128), lambda i: (rev - i, 0, 0, 0)),
         pl.BlockSpec((1, 128), lambda i: (0, 0)), pl.BlockSpec((1, 128), lambda i: (0, 0)),
         pl.BlockSpec((1, SSM_W), lambda i: (0, 0)), pl.BlockSpec((1, SSM_W), lambda i: (0, 0))],
        [pl.BlockSpec((128, CONV_C), lambda i: (rev - i, 0)),
         pl.BlockSpec((128, SSM_W), lambda i: (rev - i, 3)),
         pl.BlockSpec((128, 128), lambda i: (rev - i, 0)),
         pl.BlockSpec((1, SSM_W), lambda i: (0, 0)), pl.BlockSpec((8, 128), lambda i: (0, 0))],
        [jax.ShapeDtypeStruct((S, CONV_C), F32), jax.ShapeDtypeStruct((S, WIN_PAD), BF16),
         jax.ShapeDtypeStruct((S, 128), BF16), jax.ShapeDtypeStruct((1, SSM_W), F32),
         jax.ShapeDtypeStruct((8, 128), F32)],
        (dmixed, y, xbc, xbc, xbc, proj, proj, hprev, dtb, alog, dskip_l, ssmw),
        [pltpu.VMEM((NPAIR, 128, 128), F32)], ("arbitrary",), rider)


def _cast_stack(name, slot, arrs, tr, tc):
    n = len(arrs)
    rows, cols = arrs[0].shape

    def body(s_ref, *refs):
        for i in range(n):
            refs[n][i] = refs[i][...].astype(BF16)

    return pl.pallas_call(
        body, name=name,
        grid_spec=pltpu.PrefetchScalarGridSpec(
            num_scalar_prefetch=1, grid=(rows // tr, cols // tc),
            in_specs=[pl.BlockSpec((tr, tc), lambda i, j, sr: (i, j))] * n,
            out_specs=pl.BlockSpec((None, n, tr, tc), lambda i, j, sr: (sr[0], 0, i, j))),
        out_shape=jax.ShapeDtypeStruct((NSH, n, rows, cols), BF16),
        compiler_params=_cparams("parallel", "parallel"),
    )(slot, *arrs)


def _pair_sum(name, c_idx, ps, th):
    n = len(ps)
    _, rows, _ = ps[0].shape

    def body(c_ref, *refs):
        mine, whole, out, theirs = refs[:n], refs[n:2 * n], refs[2 * n:3 * n], refs[3 * n:4 * n]
        send, recv = refs[4 * n], refs[4 * n + 1]
        s, i = pl.program_id(0), pl.program_id(1)

        @pl.when((s == 0) & (i == 0))
        def _():
            x, y, c, _ = _place()
            cps = [_rcopy(whole[k].at[:, :, pl.ds((1 - c) * HALF, HALF)], theirs[k], send.at[k], recv.at[k],
                          (x, y, 1 - c)) for k in range(n)]
            for cp in cps:
                cp.start()
            for cp in cps:
                cp.wait()

        rows_i = slice(None) if th == rows else pl.ds(pl.multiple_of(i * th, th), th)
        for k in range(n):
            out[k][...] = (mine[k][...].astype(F32) + theirs[k][s, rows_i, :].astype(F32)).astype(BF16)

    spec = pl.BlockSpec((None, th, HALF), lambda s, i, cr: (s, i, 0))
    return pl.pallas_call(
        body, name=name,
        grid_spec=pltpu.PrefetchScalarGridSpec(
            num_scalar_prefetch=1, grid=(NSH, rows // th),
            in_specs=[pl.BlockSpec((None, th, HALF), lambda s, i, cr: (s, i, cr[0]))] * n + _any_specs(n),
            out_specs=[spec] * n,
            scratch_shapes=[pltpu.VMEM((NSH, rows, HALF), BF16)] * n
            + [pltpu.SemaphoreType.DMA((n,)), pltpu.SemaphoreType.DMA((n,))]),
        out_shape=[jax.ShapeDtypeStruct((NSH, rows, HALF), BF16)] * n,
        compiler_params=_cparams("arbitrary", "arbitrary"),
    )(c_idx, *ps, *ps)


def _chip_sum(name, place, cs, ts, th):
    n = len(ts)
    _, rows, _ = ts[0].shape

    def body(p_ref, *refs):
        for i in range(n):
            t = refs[n + i][...].astype(F32)
            refs[2 * n + i][...] = ((refs[i][...].astype(F32) + t[0]) + t[1]) + t[2]

    return pl.pallas_call(
        body, name=name,
        grid_spec=pltpu.PrefetchScalarGridSpec(
            num_scalar_prefetch=1, grid=(rows // th,),
            in_specs=[pl.BlockSpec((None, th, HALF), lambda i, pr: (pr[0], i, 0))] * n
            + [pl.BlockSpec((3, th, HALF), lambda i, pr: (0, i, 0))] * n,
            out_specs=[pl.BlockSpec((th, HALF), lambda i, pr: (i, pr[1]))] * n),
        out_shape=[jax.ShapeDtypeStruct((rows, D), F32)] * n, compiler_params=_cparams("parallel"),
    )(place, *cs, *ts)


def _adamw(name, ws, gs, ms, vs, tr, tc):
    n = len(ws)
    shape = ws[0].shape
    rows, cols, mid = shape[0], shape[-1], shape[1:-1]
    c1 = 1.0 / (1.0 - ADAM_B1 ** ADAM_STEP)
    c2 = 1.0 / (1.0 - ADAM_B2 ** ADAM_STEP)

    def body(*refs):
        for i in range(n):
            w, g, m, v = (refs[k * n + i][...] for k in range(4))
            m2 = ADAM_B1 * m + (1.0 - ADAM_B1) * g
            v2 = ADAM_B2 * v + (1.0 - ADAM_B2) * (g * g)
            refs[4 * n + 4 * i][...] = -ADAM_LR * ((m2 * c1) / (jnp.sqrt(v2 * c2) + ADAM_EPS) + ADAM_WD * w)
            refs[4 * n + 4 * i + 1][...] = m2
            refs[4 * n + 4 * i + 2][...] = v2
            refs[4 * n + 4 * i + 3][...] = g

    spec = pl.BlockSpec((tr,) + mid + (tc,), lambda i, j: (i,) + (0,) * len(mid) + (j,))
    outs = pl.pallas_call(
        body, name=name, grid=(rows // tr, cols // tc), in_specs=[spec] * (4 * n), out_specs=[spec] * (4 * n),
        out_shape=[jax.ShapeDtypeStruct(shape, F32)] * (4 * n),
        compiler_params=_cparams("parallel", "parallel"),
    )(*ws, *gs, *ms, *vs)
    return [tuple(outs[4 * i:4 * i + 4]) for i in range(n)]


def _place():
    x, y, c = lax.axis_index("x"), lax.axis_index("y"), lax.axis_index("c")
    chips = [(1 - x, y), (x, 1 - y), (1 - x, 1 - y)]
    return x, y, c, chips


def _any_specs(n):
    return [pl.BlockSpec(memory_space=pl.ANY)] * n


def _rcopy(src, dst, send_sem, recv_sem, dev):
    return pltpu.make_async_remote_copy(src_ref=src, dst_ref=dst, send_sem=send_sem, recv_sem=recv_sem,
                                        device_id=dev, device_id_type=MESH)


def _gather_rider(bufs, views):
    n = len(bufs)

    def start(rin, rout, sems):
        send, recv = sems[0], sems[1]
        x, y, c, chips = _place()
        for j, chip in enumerate(chips):
            for b in range(n):
                mine = views[b](rout[b], 2 * x + y, c)
                _rcopy(mine, mine, send.at[j * n + b], recv.at[j * n + b], (chip[0], chip[1], c)).start()

    def finish(rin, rout, sems):
        send, recv, fsend, frecv = sems
        x, y, c, chips = _place()
        passed = []
        for j, chip in enumerate(chips):
            for b in range(n):
                landed = views[b](rout[b], 2 * chip[0] + chip[1], c)
                _rcopy(landed, landed, send.at[j * n + b], recv.at[j * n + b], (x, y, c)).wait_recv()
                fw = _rcopy(landed, landed, fsend.at[j * n + b], frecv.at[j * n + b], (x, y, 1 - c))
                fw.start()
                passed.append(fw)
        for j, chip in enumerate(chips):
            for b in range(n):
                other = views[b](rout[b], 2 * chip[0] + chip[1], 1 - c)
                _rcopy(other, other, fsend.at[j * n + b], frecv.at[j * n + b], (x, y, c)).wait_recv()
        for j, chip in enumerate(chips):
            for b in range(n):
                mine = views[b](rout[b], 2 * x + y, c)
                _rcopy(mine, mine, send.at[j * n + b], recv.at[j * n + b], (x, y, c)).wait_send()
        for fw in passed:
            fw.wait_send()

    return _Rider(list(bufs), [jax.ShapeDtypeStruct(a.shape, a.dtype) for a in bufs], {b: b for b in range(n)},
                  [pltpu.SemaphoreType.DMA((3 * n,))] * 4, start, finish)


def _small_gather_rider(cw):
    def descs(rin, rout, sems, x, y, c, chips):
        return [_rcopy(rin[0], rout[0].at[2 * x + y], sems[1].at[j], sems[2].at[j], (chip[0], chip[1], c))
                for j, chip in enumerate(chips)]

    def start(rin, rout, sems):
        x, y, c, chips = _place()
        pltpu.make_async_copy(rin[0], rout[0].at[2 * x + y], sems[0].at[0]).start()
        for cp in descs(rin, rout, sems, x, y, c, chips):
            cp.start()

    def finish(rin, rout, sems):
        x, y, c, chips = _place()
        for j, chip in enumerate(chips):
            _rcopy(rin[0], rout[0].at[2 * chip[0] + chip[1]], sems[1].at[j], sems[2].at[j], (x, y, c)).wait_recv()
        for cp in descs(rin, rout, sems, x, y, c, chips):
            cp.wait_send()
        pltpu.make_async_copy(rin[0], rout[0].at[2 * x + y], sems[0].at[0]).wait()

    return _Rider([cw], [jax.ShapeDtypeStruct((NSH,) + cw.shape, cw.dtype)], {},
                  [pltpu.SemaphoreType.DMA((1,)), pltpu.SemaphoreType.DMA((3,)), pltpu.SemaphoreType.DMA((3,))],
                  start, finish)


def _to_chips_rider(cs):
    n = len(cs)

    def descs(rin, rout, sems):
        x, y, c, chips = _place()
        return [_rcopy(rin[i].at[2 * chip[0] + chip[1]], rout[i].at[j], sems[0].at[j * n + i], sems[1].at[j * n + i],
                       (chip[0], chip[1], c)) for j, chip in enumerate(chips) for i in range(n)]

    def start(rin, rout, sems):
        for cp in descs(rin, rout, sems):
            cp.start()

    def finish(rin, rout, sems):
        for cp in descs(rin, rout, sems):
            cp.wait()

    return _Rider(list(cs), [jax.ShapeDtypeStruct((3,) + a.shape[1:], a.dtype) for a in cs], {},
                  [pltpu.SemaphoreType.DMA((3 * n,))] * 2, start, finish)


def _run_riders(name, riders):
    n_in = [len(r.operands) for r in riders]
    n_out = [len(r.out_shapes) for r in riders]
    n_sem = [len(r.sems) for r in riders]

    def body(*refs):
        parts, at = [], 0
        for counts in (n_in, n_out, n_sem):
            group = []
            for k in counts:
                group.append(refs[at:at + k])
                at += k
            parts.append(group)
        for i, r in enumerate(riders):
            r.start(parts[0][i], parts[1][i], parts[2][i])
        for i, r in enumerate(riders):
            r.finish(parts[0][i], parts[1][i], parts[2][i])

    aliases = {}
    for i, r in enumerate(riders):
        for k, v in r.aliases.items():
            aliases[sum(n_in[:i]) + k] = sum(n_out[:i]) + v
    res = pl.pallas_call(
        body, name=name, in_specs=_any_specs(sum(n_in)), out_specs=_any_specs(sum(n_out)),
        out_shape=[s for r in riders for s in r.out_shapes], input_output_aliases=aliases,
        scratch_shapes=[s for r in riders for s in r.sems],
    )(*[a for r in riders for a in r.operands])
    out, at = [], 0
    for k in n_out:
        out.append(list(res[at:at + k]))
        at += k
    return out


def _swap_halves(gs):
    n = len(gs)

    def body(*refs):
        dst, send, recv = refs[n:2 * n], refs[2 * n], refs[2 * n + 1]
        x, y, c, _ = _place()
        cps = []
        for i in range(n):
            mine = dst[i].at[:, pl.ds(c * HALF, HALF)]
            cps.append(pltpu.make_async_remote_copy(
                src_ref=mine, dst_ref=mine, send_sem=send.at[i], recv_sem=recv.at[i],
                device_id=(x, y, 1 - c), device_id_type=MESH))
        for cp in cps:
            cp.start()
        for i in range(n):
            other = dst[i].at[:, pl.ds((1 - c) * HALF, HALF)]
            pltpu.make_async_remote_copy(
                src_ref=other, dst_ref=other, send_sem=send.at[i], recv_sem=recv.at[i],
                device_id=(x, y, c), device_id_type=MESH).wait_recv()
        for cp in cps:
            cp.wait_send()

    return pl.pallas_call(
        body, name="grads_swap_halves", in_specs=_any_specs(n), out_specs=_any_specs(n),
        out_shape=[jax.ShapeDtypeStruct(g.shape, g.dtype) for g in gs],
        input_output_aliases={i: i for i in range(n)},
        scratch_shapes=[pltpu.SemaphoreType.DMA((n,)), pltpu.SemaphoreType.DMA((n,))],
    )(*gs)


SMALL_ROWS = 16


def _allreduce_small(vec):
    def body(v_ref, o_ref, buf, send, recv):
        x, y, c, _ = _place()
        me = 4 * x + 2 * y + c
        buf[me] = v_ref[...]
        cps = []
        for k in range(1, 8):
            peer = (x ^ (k >> 2), y ^ ((k >> 1) & 1), c ^ (k & 1))
            cps.append(pltpu.make_async_remote_copy(
                src_ref=v_ref, dst_ref=buf.at[me], send_sem=send.at[k - 1], recv_sem=recv.at[k - 1],
                device_id=peer, device_id_type=MESH))
        for cp in cps:
            cp.start()
        for k in range(1, 8):
            pltpu.make_async_remote_copy(
                src_ref=v_ref, dst_ref=buf.at[me ^ k], send_sem=send.at[k - 1], recv_sem=recv.at[k - 1],
                device_id=(x, y, c), device_id_type=MESH).wait_recv()
        for cp in cps:
            cp.wait_send()
        t = buf[0]
        for d in range(1, 8):
            t = t + buf[d]
        o_ref[...] = t

    return pl.pallas_call(
        body, name="allreduce_small",
        in_specs=[pl.BlockSpec(memory_space=pltpu.VMEM)], out_specs=pl.BlockSpec(memory_space=pltpu.VMEM),
        out_shape=jax.ShapeDtypeStruct((SMALL_ROWS, D), F32),
        scratch_shapes=[pltpu.VMEM((8, SMALL_ROWS, D), F32), pltpu.SemaphoreType.DMA((7,)),
                        pltpu.SemaphoreType.DMA((7,))],
    )(vec)


def _col_half(ref, slot, hc):
    return ref.at[slot, :, pl.ds(hc * HALF, HALF)]


def _stack_half(ref, slot, hc):
    return ref.at[slot, :, :, pl.ds(hc * HALF, HALF)]


def _row_tile(rows):
    for t in range(512, 15, -16):
        if rows % t == 0:
            return t
    return rows


def _same_shape_runs(arrs):
    runs, a = [], 0
    for b in range(1, len(arrs) + 1):
        if b == len(arrs) or arrs[b].shape != arrs[a].shape:
            runs.append((a, b))
            a = b
    return runs


class _Comm:
    def __init__(self):
        x, y, c = lax.axis_index("x"), lax.axis_index("y"), lax.axis_index("c")
        self.c_idx = jnp.reshape(c, (1,)).astype(jnp.int32)
        self.place = jnp.stack([2 * x + y, c]).astype(jnp.int32)
        self.groups = {}

    @staticmethod
    def gather(*bufs):
        return _gather_rider(list(bufs), [_col_half if b.ndim == 3 else _stack_half for b in bufs])

    def reduce_rider(self, tag, names, ps):
        csums = []
        for a, b in _same_shape_runs(ps):
            csums += _pair_sum("pair_sum_%s%d" % (tag, a), self.c_idx, ps[a:b], _row_tile(ps[a].shape[1]))
        self.groups[tag] = [names, csums, None]
        return _to_chips_rider(csums)

    def landed(self, tag, ts):
        self.groups[tag][2] = ts

    def finish(self):
        names, csums, ts = [], [], []
        for group_names, group_csums, group_ts in self.groups.values():
            names += group_names
            csums += group_csums
            ts += group_ts
        order = sorted(range(len(names)), key=lambda i: csums[i].shape[1])
        names, csums, ts = ([v[i] for i in order] for v in (names, csums, ts))
        halves = []
        for a, b in _same_shape_runs(csums):
            halves += _chip_sum("chip_sum_%d" % a, self.place, csums[a:b], ts[a:b], _row_tile(csums[a].shape[1]))
        return dict(zip(names, _swap_halves(halves)))


ROPE_THETA = 10000.0
SMALL_1K = ("ffn1_pre_norm", "ffn1_post_norm", "mix_pre_norm", "ssm_norm", "mix_post_norm",
            "ffn2_pre_norm", "ffn2_post_norm")
SMALL_16 = ("dt_bias", "a_log", "d_skip")
OFF_CONVB = 7 * D
OFF_16 = OFF_CONVB + CONV_C
OFF_CONVW = OFF_16 + 48
OFF_LOSS = OFF_CONVW + CONV_K * CONV_C
SMALL_LEN = SMALL_ROWS * D


def _sds(shape, dtype):
    return jax.ShapeDtypeStruct(shape, dtype)


def _ridden(res, rider):
    return res if rider is not None else (res, None)


def _ffn_down(name, act, w, tail_of, rider=None):
    tail, o_specs, o_shapes = tail_of(TS)
    return _mm(name, [act, w.dn], NN, (S // TS,),
               [pl.BlockSpec((NSH, TS, FS), lambda i: (0, i, 0)),
                pl.BlockSpec((NSH, None, FS, D), lambda i: (0, w.d0, 0, 0))], o_specs, o_shapes, rider, tail)


def _ffn_dw(name, a, b, rider=None):
    return _mm(name, [a, b], TN, (NSH,),
               [pl.BlockSpec((None, S, FS), lambda s: (s, 0, 0)), pl.BlockSpec((S, D), lambda s: (0, 0))],
               pl.BlockSpec((None, FS, D), lambda s: (s, 0, 0)), _sds((NSH, FS, D), BF16), rider)


def _ffn_dn(name, dgate, dup, w, tail_of, rider=None):
    rows = TS // 2
    tail, o_specs, o_shapes = tail_of(rows)
    a2 = pl.BlockSpec((NSH, rows, FS), lambda i: (0, i, 0))
    return _mm(name, [dgate, w.gu, dup, w.gu], NN, (S // rows,),
               [a2, pl.BlockSpec((NSH, None, FS, D), lambda i: (0, w.g0, 0, 0)),
                a2, pl.BlockSpec((NSH, None, FS, D), lambda i: (0, w.g0 + 1, 0, 0))], o_specs, o_shapes, rider, tail)


def _out_proj_dx(dh, wout):
    def body(dh_ref, w_ref, dyn_ref, do_ref):
        dm = _dot(dh_ref[...], w_ref[...], NT)
        dyn_ref[...] = dm[:, D:]
        for b in range(TS // 128):
            for j, blk in enumerate(_rows_to_blocks(dm[128 * b:128 * (b + 1), :D])):
                do_ref[j, b] = blk.astype(BF16)

    return pl.pallas_call(
        body, name="out_proj_dx", grid=(S // TS,),
        in_specs=[pl.BlockSpec((TS, D), lambda i: (i, 0)), pl.BlockSpec((2 * D, D), lambda i: (0, 0))],
        out_specs=[pl.BlockSpec((TS, D), lambda i: (i, 0)),
                   pl.BlockSpec((NKV, TS // 128, HD, QROWS), lambda i: (0, i, 0, 0))],
        out_shape=[_sds((S, D), F32), _sds((NKV, NCH, HD, QROWS), BF16)], compiler_params=_cparams("parallel"),
    )(dh, wout)


def _heads(t, n):
    return t.reshape(S, n, HD).transpose(1, 0, 2)


def _unheads(t):
    return t.transpose(1, 0, 2).reshape(S, t.shape[0] * HD)


def _heads_t(t, n):
    return t.reshape(S, n, HD).transpose(1, 2, 0)


def _pad128(v):
    return jnp.pad(v, ((0, 0), (0, 128 - v.shape[1])))


def _local_step(x, positions, tgt, sp, gu1, d1, f2, wint, wout, convw, comm=None):
    inv_freq = ROPE_THETA ** (-jnp.arange(0, HD, 2, dtype=F32) / HD)
    ang = positions.astype(F32)[:, None] * inv_freq
    ang = jnp.concatenate([ang, ang, ang, ang], axis=-1)
    cos, sin = jnp.cos(ang), jnp.sin(ang)
    dtb, alog = _pad128(sp["dt_bias"]), _pad128(sp["a_log"])
    dskip_l = jnp.repeat(sp["d_skip"], HD, axis=1)
    convb = sp["conv_b"]

    n1 = _prenorm("prenorm1", x, sp["ffn1_pre_norm"])
    rider = comm.gather(d1) if comm else None
    (fg1, fu1, act1), got = _ridden(_ffn_up("ffn1_up", n1, _FfnW(gu1, 0, d1, 0), rider), rider)
    if comm:
        d1, = got
    w1 = _FfnW(gu1, 0, d1, 0)
    rider = comm.gather(wint) if comm else None
    (h1, x1, n2), got = _ridden(_ffn_down(
        "ffn1_down", act1, w1,
        lambda rows: _tail_postres(rows, x, sp["ffn1_post_norm"], 0.5, sp["mix_pre_norm"]), rider), rider)
    if comm:
        wint, = got
    wint_pad = jnp.pad(wint.reshape(WIN_COLS, D), ((0, WIN_PAD - WIN_COLS), (0, 0)))

    pw = WIN_PAD // 3
    proj = _mm("in_proj", [n2, wint_pad], NT, (S // TS, 3),
               [pl.BlockSpec((TS, D), lambda i, j: (i, 0)), pl.BlockSpec((pw, D), lambda i, j: (j, 0))],
               pl.BlockSpec((TS, pw), lambda i, j: (i, j)), _sds((S, WIN_PAD), F32))
    qt = _rope_q(proj, cos, sin)
    k_rot = _rope("rope_k", proj, D // KVW, KVW, cos, sin, 1.0, 1.0)
    v_bf = proj[:, D + KVW:D + 2 * KVW].astype(BF16)
    kh, vh = _heads(k_rot, NKV), _heads(v_bf, NKV)
    kt, vt = _heads_t(k_rot, NKV), _heads_t(v_bf, NKV)
    bias = _bias_table()
    rider = comm.gather(f2, wout) if comm else None
    (ot, lse, attn), got = _ridden(_attn_fwd(qt, kh, vt, bias, rider), rider)
    if comm:
        f2, wout = got
    w2 = _FfnW(f2, 0, f2, 2)
    wout = wout.reshape(2 * D, D)
    xbc = _conv_fwd(proj, convw, convb)
    y, yn, hprev = _ssd_fwd(xbc, proj, dtb, alog, dskip_l, sp["ssm_norm"])
    mixed = jnp.concatenate([attn, yn], axis=1)
    tail, o_specs, o_shapes = _tail_postres(TS, x1, sp["mix_post_norm"], 1.0, sp["ffn2_pre_norm"])
    h2, x2, n3 = _mm("out_proj", [mixed, wout], NN, (S // TS,),
                     [pl.BlockSpec((TS, 2 * D), lambda i: (i, 0)), pl.BlockSpec((2 * D, D), lambda i: (0, 0))],
                     o_specs, o_shapes, None, tail)

    fg2, fu2, act2 = _ffn_up("ffn2_up", n3, w2)
    dy, dh3, dp3, loss = _ffn_down(
        "ffn2_down", act2, w2, lambda rows: _tail_final(rows, x2, sp["ffn2_post_norm"], tgt, 0.5))

    dgate2, dup2 = _ffn_dact("ffn2_dact", dh3, w2, fg2, fu2)
    dws2 = [_ffn_dw("ffn2_dwg", dgate2, n3), _ffn_dw("ffn2_dwu", dup2, n3), _ffn_dw("ffn2_dwd", act2, dh3)]
    dx2, dh2, dg3, dp2 = _ffn_dn(
        "ffn2_dn", dgate2, dup2, w2,
        lambda rows: _tail_mid_bwd(rows, dy, x2, sp["ffn2_pre_norm"], h2, sp["mix_post_norm"], 1.0))

    dyn, dot_ = _out_proj_dx(dh2, wout)
    dwout = _mm("out_proj_dw", [mixed, dh2], TN, (2,),
                [pl.BlockSpec((S, D), lambda m: (0, m)), pl.BlockSpec((S, D), lambda m: (0, 0))],
                pl.BlockSpec((D, D), lambda m: (m, 0)), _sds((2 * D, D), BF16))
    dwout = dwout.reshape(NSH, 2 * D // NSH, D)

    def riding(tag, names, ps, call):
        rider = comm.reduce_rider(tag, names, ps) if comm else None
        res, got = _ridden(call(rider), rider)
        if comm:
            comm.landed(tag, got)
        return res

    dxbc, dproj, ddt, dssm, dsc = _ssd_bwd(dyn, y, xbc, proj, hprev, dtb, alog, dskip_l, sp["ssm_norm"])
    dproj, dcw8, dcb = _conv_bwd(dxbc, proj, convw, convb, dproj)
    delta = _attn_delta(ot, dot_)
    dqt, dkh, dvh = riding("a", BIG[3:6] + ("w_out",), dws2 + [dwout], lambda rider: _attn_bwd(
        qt, kh, kt, vh, dot_, lse, delta, bias, rider))
    dproj = _rope_dq(dqt, cos, sin, dproj)
    dproj = _rope("rope_dk", _unheads(dkh), 0, KVW, cos, sin, -1.0, 1.0, into=(dproj, D // KVW))
    dproj = lax.dynamic_update_slice(dproj, _unheads(dvh).astype(BF16), (0, D + KVW))
    dproj = lax.dynamic_update_slice(dproj, ddt, (0, COL_DT))
    dwint = _mm("in_proj_dw", [dproj, n2], TN, (3,),
                [pl.BlockSpec((S, pw), lambda j: (0, j)), pl.BlockSpec((S, D), lambda j: (0, 0))],
                pl.BlockSpec((pw, D), lambda j: (j, 0)), _sds((WIN_PAD, D), BF16))
    dwint = dwint[:WIN_COLS].reshape(NSH, WIN_SH, D)

    tail, o_specs, o_shapes = _tail_mid_bwd(TS, dx2, x1, sp["mix_pre_norm"], h1, sp["ffn1_post_norm"], 0.5)
    dx1, dh1, dg2, dp1 = riding("b", ("w_in",), [dwint], lambda rider: _mm(
        "in_proj_dx", [dproj, wint_pad], NN, (S // TS,),
        [pl.BlockSpec((TS, WIN_PAD), lambda i: (i, 0)), pl.BlockSpec((WIN_PAD, D), lambda i: (0, 0))],
        o_specs, o_shapes, rider, tail))

    dwd1 = _ffn_dw("ffn1_dwd", act1, dh1)
    dgate1, dup1 = riding("d", BIG[2:3], [dwd1], lambda rider: _ffn_dact("ffn1_dact", dh1, w1, fg1, fu1, rider))
    dwg1, dwu1 = _ffn_dw("ffn1_dwg", dgate1, n1), _ffn_dw("ffn1_dwu", dup1, n1)
    grad_x, dg1 = riding("g", BIG[0:2], [dwg1, dwu1], lambda rider: _ffn_dn(
        "ffn1_dn", dgate1, dup1, w1, lambda rows: _tail_first_bwd(rows, dx1, x, sp["ffn1_pre_norm"]), rider))
    dws1 = [dwg1, dwu1, dwd1]

    small = jnp.concatenate([
        dg1[0], dp1[0], dg2[0], dssm[0], dp2[0], dg3[0], dp3[0], dcb[0],
        dsc[0, :16], dsc[1, :16], dsc[2, :16], dcw8[:CONV_K].reshape(-1), loss[0, :1]])
    small = jnp.pad(small, (0, SMALL_LEN - small.shape[0])).reshape(SMALL_ROWS, D)
    if comm is None:
        return grad_x, dws1 + dws2 + [dwint, dwout], small
    return grad_x, comm.finish(), small


WEIGHTS = ("ffn1_pre_norm", "ffn1_w_gate", "ffn1_w_up", "ffn1_w_down", "ffn1_post_norm", "mix_pre_norm", "w_in",
           "conv_w", "conv_b", "dt_bias", "a_log", "d_skip", "ssm_norm", "w_out", "mix_post_norm", "ffn2_pre_norm",
           "ffn2_w_gate", "ffn2_w_up", "ffn2_w_down", "ffn2_post_norm")
BIG = ("ffn1_w_gate", "ffn1_w_up", "ffn1_w_down", "ffn2_w_gate", "ffn2_w_up", "ffn2_w_down", "w_in", "w_out")
TRANSPOSED = ("ffn1_w_gate", "ffn1_w_up", "ffn2_w_gate", "ffn2_w_up", "w_in")
SMALL_ORDER = SMALL_1K + ("conv_b",) + SMALL_16
CONVW_SH = CONV_C // NSH


def _shard2d(t, name):
    return t[0].T if name in TRANSPOSED else t[0]


def _unshard2d(t, name):
    return (t.T if name in TRANSPOSED else t)[None]


def _rows3d(t):
    return t.transpose(2, 0, 1)


def _pack_small(d, prefix, shard_of_convw):
    flat = jnp.concatenate([d[prefix + n][0] for n in SMALL_ORDER] + [shard_of_convw.reshape(-1)])
    return jnp.pad(flat, (0, SMALL_LEN - flat.shape[0])).reshape(SMALL_ROWS, D)


def _unpack_small(block, like):
    flat = block.reshape(-1)
    out, off = {}, 0
    for n in SMALL_ORDER:
        size = like[n].shape[1]
        out[n] = flat[off:off + size].reshape(1, size)
        off += size
    out["conv_w"] = flat[off:off + CONV_K * CONVW_SH].reshape(1, CONV_K, CONVW_SH)
    return out


def kernel(x, positions, ffn1_pre_norm, ffn1_w_gate, ffn1_w_up, ffn1_w_down, ffn1_post_norm, mix_pre_norm, w_in, conv_w, conv_b, dt_bias, a_log, d_skip, ssm_norm, w_out, mix_post_norm, ffn2_pre_norm, ffn2_w_gate, ffn2_w_up, ffn2_w_down, ffn2_post_norm, loss_target, m_ffn1_pre_norm, m_ffn1_w_gate, m_ffn1_w_up, m_ffn1_w_down, m_ffn1_post_norm, m_mix_pre_norm, m_w_in, m_conv_w, m_conv_b, m_dt_bias, m_a_log, m_d_skip, m_ssm_norm, m_w_out, m_mix_post_norm, m_ffn2_pre_norm, m_ffn2_w_gate, m_ffn2_w_up, m_ffn2_w_down, m_ffn2_post_norm, v_ffn1_pre_norm, v_ffn1_w_gate, v_ffn1_w_up, v_ffn1_w_down, v_ffn1_post_norm, v_mix_pre_norm, v_w_in, v_conv_w, v_conv_b, v_dt_bias, v_a_log, v_d_skip, v_ssm_norm, v_w_out, v_mix_post_norm, v_ffn2_pre_norm, v_ffn2_w_gate, v_ffn2_w_up, v_ffn2_w_down, v_ffn2_post_norm):
    given = dict(locals())
    xi, yi = lax.axis_index("x"), lax.axis_index("y")

    shard = jnp.reshape(2 * xi + yi, (1,)).astype(jnp.int32)
    big = {p + n: _shard2d(given[p + n], n) for n in BIG for p in ("", "m_", "v_")}
    gu1 = _cast_stack("cast_ffn1_gate_up", shard, [big[n] for n in BIG[0:2]], 176, D)
    d1 = _cast_stack("cast_ffn1_down", shard, [big[BIG[2]]], 176, D)
    f2 = _cast_stack("cast_ffn2", shard, [big[n] for n in BIG[3:6]], 176, D)
    winsh = _cast_stack("cast_w_in", shard, [big["w_in"]], WIN_SH, 256).reshape(NSH, WIN_SH, D)
    woutsh = _cast_stack("cast_w_out", shard, [big["w_out"]], 256, D).reshape(NSH, 2 * D // NSH, D)
    comm = _Comm()
    (gu1,), (cwf,) = _run_riders("gather_ffn1_gate_up", [comm.gather(gu1), _small_gather_rider(conv_w[0])])
    convw = cwf.transpose(1, 0, 2).reshape(CONV_K, CONV_C)

    sp = {n: given[n] for n in SMALL_ORDER}
    grad_x, big_grads, small = _local_step(x[0], positions[0], loss_target[0], sp, gu1, d1, f2, winsh, woutsh,
                                           convw, comm)

    tot = _allreduce_small(small).reshape(-1)
    loss = tot[OFF_LOSS]
    small_grads, off = {}, 0
    for n in SMALL_ORDER:
        size = given[n].shape[1]
        small_grads[n] = tot[off:off + size].reshape(1, size)
        off += size
    dconvw = tot[OFF_CONVW:OFF_CONVW + CONV_K * CONV_C].reshape(CONV_K, NSH, CONVW_SH)
    dconvw = lax.dynamic_index_in_dim(dconvw, 2 * xi + yi, axis=1, keepdims=False)
    small_grads["conv_w"] = dconvw.reshape(1, CONV_K, CONVW_SH)

    upd = {}
    for names, tr in ((BIG[0:3], 176), (BIG[3:6], 176), (BIG[7:8], 256)):
        res = _adamw("adamw_" + names[0], [big[n] for n in names], [big_grads[n] for n in names],
                     [big["m_" + n] for n in names], [big["v_" + n] for n in names], tr, D)
        for n, r in zip(names, res):
            upd[n] = tuple(_unshard2d(t, n) for t in r)
    g_win = big_grads["w_in"].reshape(WIN_SH, 1, D)
    res, = _adamw("adamw_w_in", [_rows3d(w_in)], [g_win], [_rows3d(m_w_in)], [_rows3d(v_w_in)], WIN_SH // 4, D)
    upd["w_in"] = tuple(t.transpose(1, 2, 0) for t in res)
    (dl, m2, v2, _), = _adamw(
        "adamw_small", [_pack_small(given, "", conv_w[0])], [_pack_small(small_grads, "", dconvw)],
        [_pack_small(given, "m_", m_conv_w[0])], [_pack_small(given, "v_", v_conv_w[0])], SMALL_ROWS, D)
    dl, m2, v2 = (_unpack_small(t, given) for t in (dl, m2, v2))
    for n in SMALL_ORDER + ("conv_w",):
        upd[n] = (dl[n], m2[n], v2[n], small_grads[n])

    return (loss, grad_x[None], *[upd[n][3] for n in WEIGHTS], *[upd[n][0] for n in WEIGHTS],
            *[upd[n][1] for n in WEIGHTS], *[upd[n][2] for n in WEIGHTS])
```

```python
import functools
import typing

import jax
import jax.numpy as jnp
from jax import lax
from jax.experimental import pallas as pl
from jax.experimental.pallas import tpu as pltpu

F32 = jnp.float32
BF16 = jnp.bfloat16

S = 2048
D = 1024
FF = 2816
NSH = 4
FS = FF // NSH
HALF = D // 2
HD = 64
NKV = 4
NQ_PER_KV = 4
KVW = NKV * HD
QCOLS = NQ_PER_KV * HD
CONV_C = 1536
CONV_K = 4
SSM_W = 1024
NST = 128
NCH = S // 128
WIN_COLS = 4112
WIN_SH = WIN_COLS // NSH
WIN_PAD = 4224
COL_DT = 4096
EPS = 1e-6
NEG = -1e30

ADAM_LR = 0.001
ADAM_B1 = 0.9
ADAM_B2 = 0.999
ADAM_EPS = 1e-08
ADAM_WD = 0.01
ADAM_STEP = 10

VMEM_LIMIT = 56 * 1024 * 1024
TS = 512
TR = 256

NN = (((1,), (0,)), ((), ()))
NT = (((1,), (1,)), ((), ()))
TN = (((0,), (0,)), ((), ()))
MESH = pl.DeviceIdType.MESH


def _cparams(*sem):
    return pltpu.CompilerParams(dimension_semantics=sem, vmem_limit_bytes=VMEM_LIMIT)


def _dot(a, b, dims):
    return lax.dot_general(a.astype(BF16), b.astype(BF16), dims, preferred_element_type=F32)


def _bf16_pieces(v):
    hi = v.astype(BF16)
    rest = v - hi.astype(F32)
    mid = rest.astype(BF16)
    return hi, mid, (rest - mid.astype(F32)).astype(BF16)


def _dot_exact(a, b, ones="a"):
    if ones == "a":
        sel = a.astype(BF16)
        parts = [lax.dot_general(sel, p, NN, preferred_element_type=F32) for p in _bf16_pieces(b)]
    else:
        sel = b.astype(BF16)
        parts = [lax.dot_general(p, sel, NN, preferred_element_type=F32) for p in _bf16_pieces(a)]
    return (parts[2] + parts[1]) + parts[0]


def _sigmoid(v):
    return 1.0 / (1.0 + jnp.exp(-v))


class _Rider(typing.NamedTuple):
    operands: list
    out_shapes: list
    aliases: dict
    sems: list
    start: typing.Callable
    finish: typing.Callable


def _call(body, name, grid, in_specs, out_specs, out_shape, operands, scratch=(), sem=(), rider=None):
    multi = isinstance(out_shape, (list, tuple))
    if rider is None:
        return pl.pallas_call(
            body, name=name, grid=grid, in_specs=in_specs, out_specs=out_specs, out_shape=out_shape,
            scratch_shapes=list(scratch), compiler_params=_cparams(*sem))(*operands)
    outs = list(out_shape) if multi else [out_shape]
    ospecs = list(out_specs) if multi else [out_specs]
    n_in, n_out, n_scr = len(operands), len(outs), len(scratch)
    ri, ro = len(rider.operands), len(rider.out_shapes)

    def wrapped(*refs):
        o0 = n_in + ri
        s0 = o0 + n_out + ro
        rin, rout, rsem = refs[n_in:o0], refs[o0 + n_out:s0], refs[s0 + n_scr:]
        ids = [pl.program_id(a) for a in range(len(grid))]
        first = functools.reduce(jnp.logical_and, [i == 0 for i in ids])
        last = functools.reduce(jnp.logical_and, [i == g - 1 for i, g in zip(ids, grid)])

        @pl.when(first)
        def _():
            rider.start(rin, rout, rsem)

        body(*refs[:n_in], *refs[o0:o0 + n_out], *refs[s0:s0 + n_scr])

        @pl.when(last)
        def _():
            rider.finish(rin, rout, rsem)

    hbm = pl.BlockSpec(memory_space=pl.ANY)
    res = pl.pallas_call(
        wrapped, name=name, grid=grid, in_specs=list(in_specs) + [hbm] * ri, out_specs=ospecs + [hbm] * ro,
        out_shape=outs + list(rider.out_shapes), scratch_shapes=list(scratch) + list(rider.sems),
        input_output_aliases={n_in + k: n_out + v for k, v in rider.aliases.items()},
        compiler_params=_cparams(*(("arbitrary",) * len(grid))))(*operands, *rider.operands)
    main = list(res[:n_out])
    return (main if multi else main[0]), list(res[n_out:])


class _Tail(typing.NamedTuple):
    fn: typing.Callable
    operands: list
    in_specs: list


def _mm(name, operands, dims, grid, in_specs, o_spec, out_shape, rider=None, tail=None):
    npairs = len(operands) // 2
    extra = [] if tail is None else list(tail.operands)
    nin = 2 * npairs + len(extra)

    def body(*refs):
        t = None
        for i in range(npairs):
            a, b = refs[2 * i], refs[2 * i + 1]
            parts = [(a[s], b[s]) for s in range(a.shape[0])] if len(a.shape) == 3 else [(a[...], b[...])]
            for pa, pb in parts:
                d = _dot(pa, pb, dims)
                t = d if t is None else t + d
        if tail is None:
            refs[nin][...] = t.astype(refs[nin].dtype)
        else:
            tail.fn(t, refs[2 * npairs:nin], refs[nin:])

    sem = ("parallel" if tail is None else "arbitrary",) * len(grid)
    specs = list(in_specs) + ([] if tail is None else list(tail.in_specs))
    return _call(body, name, grid, specs, o_spec, out_shape, list(operands) + extra, (), sem, rider)


class _FfnW(typing.NamedTuple):
    gu: jax.Array
    g0: int
    dn: jax.Array
    d0: int


def _ffn_up(name, n, w, rider=None):
    def body(n_ref, wg_ref, wu_ref, fg_ref, fu_ref, a_ref):
        nb = n_ref[...]
        g = _dot(nb, wg_ref[...], NT)
        u = _dot(nb, wu_ref[...], NT)
        sg = _sigmoid(g)
        silu = g * sg
        fg_ref[...] = (u * (sg * (1.0 + g * (1.0 - sg)))).astype(BF16)
        fu_ref[...] = silu.astype(BF16)
        a_ref[...] = (silu * u).astype(BF16)

    out = jax.ShapeDtypeStruct((NSH, S, FS), BF16)
    ospec = pl.BlockSpec((None, TS, FS), lambda s, i: (s, i, 0))
    return _call(
        body, name, (NSH, S // TS),
        [pl.BlockSpec((TS, D), lambda s, i: (i, 0)),
         pl.BlockSpec((None, None, FS, D), lambda s, i: (s, w.g0, 0, 0)),
         pl.BlockSpec((None, None, FS, D), lambda s, i: (s, w.g0 + 1, 0, 0))],
        [ospec, ospec, ospec], [out, out, out], (n, w.gu, w.gu), sem=("parallel", "parallel"), rider=rider)


def _ffn_dact(name, dh, w, fgate, fup, rider=None):
    def body(dh_ref, wd_ref, fg_ref, fu_ref, dg_ref, du_ref):
        da = _dot(dh_ref[...], wd_ref[...], NT)
        dg_ref[...] = (da * fg_ref[...].astype(F32)).astype(BF16)
        du_ref[...] = (da * fu_ref[...].astype(F32)).astype(BF16)

    out = jax.ShapeDtypeStruct((NSH, S, FS), BF16)
    aspec = pl.BlockSpec((None, TS, FS), lambda s, i: (s, i, 0))
    return _call(
        body, name, (NSH, S // TS),
        [pl.BlockSpec((TS, D), lambda s, i: (i, 0)),
         pl.BlockSpec((None, None, FS, D), lambda s, i: (s, w.d0, 0, 0)), aspec, aspec],
        [aspec, aspec], [out, out], (dh, w.dn, fgate, fup), sem=("parallel", "parallel"), rider=rider)


def _rstd(v):
    return lax.rsqrt(jnp.mean(v * v, axis=-1, keepdims=True) + EPS)


def _row_spec():
    return pl.BlockSpec((TR, D), lambda i: (i, 0))


def _vec_spec():
    return pl.BlockSpec((1, D), lambda i: (0, 0))


def _acc_rows(ref, v):
    @pl.when(pl.program_id(0) == 0)
    def _():
        ref[...] = jnp.zeros_like(ref)
    ref[...] += jnp.sum(v, axis=0, keepdims=True)


def _prenorm(name, x, g):
    def body(x_ref, g_ref, n_ref):
        xv = x_ref[...]
        n_ref[...] = (xv * _rstd(xv) * g_ref[...]).astype(BF16)

    return pl.pallas_call(
        body, name=name, grid=(S // TR,), in_specs=[_row_spec(), _vec_spec()], out_specs=_row_spec(),
        out_shape=jax.ShapeDtypeStruct((S, D), BF16), compiler_params=_cparams("parallel"),
    )(x, g)


def _rows_spec(rows):
    return pl.BlockSpec((rows, D), lambda i: (i, 0))


def _rows_f32():
    return jax.ShapeDtypeStruct((S, D), F32)


def _rows_bf16():
    return jax.ShapeDtypeStruct((S, D), BF16)


def _vec_f32():
    return jax.ShapeDtypeStruct((1, D), F32)


def _tail_postres(rows, x, p, alpha, gnext):
    def fn(h, ins, outs):
        x_ref, p_ref, g_ref = ins
        h_ref, xo_ref, n_ref = outs
        h_ref[...] = h
        xo = x_ref[...] + alpha * (h * _rstd(h) * p_ref[...])
        xo_ref[...] = xo
        n_ref[...] = (xo * _rstd(xo) * g_ref[...]).astype(BF16)

    rs = _rows_spec(rows)
    return (_Tail(fn, [x, p, gnext], [rs, _vec_spec(), _vec_spec()]), [rs, rs, rs],
            [_rows_f32(), _rows_f32(), _rows_bf16()])


def _tail_final(rows, x, p, tgt, alpha):
    def fn(h, ins, outs):
        x_ref, p_ref, t_ref = ins
        dy_ref, dh_ref, dp_ref, loss_ref = outs
        r = _rstd(h)
        hn = h * r
        pv = p_ref[...]
        e = x_ref[...] + alpha * (hn * pv) - t_ref[...]
        dy = e * (1.0 / D)
        dy_ref[...] = dy
        du = alpha * dy * pv
        dh_ref[...] = (r * (du - hn * jnp.mean(du * hn, axis=-1, keepdims=True))).astype(BF16)
        _acc_rows(dp_ref, alpha * dy * hn)
        part = 0.5 * jnp.sum(jnp.mean(e * e, axis=-1, keepdims=True), axis=0, keepdims=True)
        _acc_rows(loss_ref, jnp.broadcast_to(part, (1, 128)))

    rs = _rows_spec(rows)
    return (_Tail(fn, [x, p, tgt], [rs, _vec_spec(), rs]),
            [rs, rs, _vec_spec(), pl.BlockSpec((1, 128), lambda i: (0, 0))],
            [_rows_f32(), _rows_bf16(), _vec_f32(), jax.ShapeDtypeStruct((1, 128), F32)])


def _norm_bwd(dn, xv, g_ref, dg_ref):
    r = _rstd(xv)
    xn = xv * r
    dng = dn * g_ref[...]
    _acc_rows(dg_ref, dn * xn)
    return r * (dng - xn * jnp.mean(dng * xn, axis=-1, keepdims=True))


def _tail_mid_bwd(rows, dres, x, g, h, p, alpha):
    def fn(dn, ins, outs):
        dr_ref, x_ref, g_ref, h_ref, p_ref = ins
        dx_ref, dh_ref, dg_ref, dp_ref = outs
        dx = dr_ref[...] + _norm_bwd(dn, x_ref[...], g_ref, dg_ref)
        dx_ref[...] = dx
        hv = h_ref[...]
        r = _rstd(hv)
        hn = hv * r
        du = alpha * dx * p_ref[...]
        dh_ref[...] = (r * (du - hn * jnp.mean(du * hn, axis=-1, keepdims=True))).astype(BF16)
        _acc_rows(dp_ref, alpha * dx * hn)

    rs = _rows_spec(rows)
    return (_Tail(fn, [dres, x, g, h, p], [rs, rs, _vec_spec(), rs, _vec_spec()]),
            [rs, rs, _vec_spec(), _vec_spec()], [_rows_f32(), _rows_bf16(), _vec_f32(), _vec_f32()])


def _tail_first_bwd(rows, dres, x, g):
    def fn(dn, ins, outs):
        dr_ref, x_ref, g_ref = ins
        dx_ref, dg_ref = outs
        dx_ref[...] = dr_ref[...] + _norm_bwd(dn, x_ref[...], g_ref, dg_ref)

    rs = _rows_spec(rows)
    return (_Tail(fn, [dres, x, g], [rs, rs, _vec_spec()]), [rs, _vec_spec()], [_rows_f32(), _vec_f32()])


def _rotate(t, c128, s128, sign, scale):
    width = t.shape[1]
    c = jnp.tile(c128, (1, width // 128))
    sn = jnp.tile(s128, (1, width // 128))
    lane = lax.broadcasted_iota(jnp.int32, t.shape, 1) & (HD - 1)
    rot = jnp.where(lane < HD // 2, -pltpu.roll(t, width - HD // 2, 1), pltpu.roll(t, HD // 2, 1))
    return (t * c + sign * (rot * sn)) * scale


def _rows_to_blocks(y):
    out = []
    for j in range(NKV):
        yt = y[:, QCOLS * j:QCOLS * (j + 1)].T
        out.append(jnp.concatenate([yt[HD * g:HD * (g + 1)] for g in range(NQ_PER_KV)], axis=1))
    return out


def _blocks_to_rows(blocks):
    cols = []
    for b in blocks:
        stacked = jnp.concatenate([b[:, 128 * g:128 * (g + 1)] for g in range(NQ_PER_KV)], axis=0)
        cols.append(stacked.T)
    return jnp.concatenate(cols, axis=1)


def _rope_q(proj, cos, sin):
    def body(t_ref, c_ref, s_ref, o_ref):
        y = _rotate(t_ref[...], c_ref[...], s_ref[...], 1.0, HD ** -0.5)
        for j, blk in enumerate(_rows_to_blocks(y)):
            o_ref[j] = blk.astype(BF16)

    return pl.pallas_call(
        body, name="rope_q", grid=(NCH,),
        in_specs=[pl.BlockSpec((128, D), lambda i: (i, 0)),
                  pl.BlockSpec((128, 128), lambda i: (i, 0)), pl.BlockSpec((128, 128), lambda i: (i, 0))],
        out_specs=pl.BlockSpec((NKV, None, HD, QROWS), lambda i: (0, i, 0, 0)),
        out_shape=jax.ShapeDtypeStruct((NKV, NCH, HD, QROWS), BF16), compiler_params=_cparams("parallel"),
    )(proj, cos, sin)


def _rope_dq(dqt, cos, sin, dproj):
    def body(t_ref, c_ref, s_ref, buf_ref, o_ref):
        t = _blocks_to_rows([t_ref[j] for j in range(NKV)])
        o_ref[...] = _rotate(t, c_ref[...], s_ref[...], -1.0, HD ** -0.5).astype(BF16)

    return pl.pallas_call(
        body, name="rope_dq", grid=(NCH,),
        in_specs=[pl.BlockSpec((NKV, None, HD, QROWS), lambda i: (0, i, 0, 0)),
                  pl.BlockSpec((128, 128), lambda i: (i, 0)), pl.BlockSpec((128, 128), lambda i: (i, 0)),
                  pl.BlockSpec(memory_space=pl.ANY)],
        out_specs=pl.BlockSpec((128, D), lambda i: (i, 0)),
        out_shape=jax.ShapeDtypeStruct(dproj.shape, BF16), input_output_aliases={3: 0},
        compiler_params=_cparams("parallel"),
    )(dqt, cos, sin, dproj)


def _rope(name, src, col_block, width, cos, sin, sign, scale, into=None):
    def body(t_ref, c_ref, s_ref, *rest):
        rest[-1][...] = _rotate(t_ref[...].astype(F32), c_ref[...], s_ref[...], sign, scale).astype(BF16)

    in_specs = [pl.BlockSpec((TR, width), lambda i: (i, col_block)),
                pl.BlockSpec((TR, 128), lambda i: (i, 0)), pl.BlockSpec((TR, 128), lambda i: (i, 0))]
    if into is None:
        return pl.pallas_call(
            body, name=name, grid=(S // TR,), in_specs=in_specs,
            out_specs=pl.BlockSpec((TR, width), lambda i: (i, 0)),
            out_shape=jax.ShapeDtypeStruct((S, width), BF16), compiler_params=_cparams("parallel"),
        )(src, cos, sin)
    buf, out_block = into
    return pl.pallas_call(
        body, name=name, grid=(S // TR,), in_specs=in_specs + [pl.BlockSpec(memory_space=pl.ANY)],
        out_specs=pl.BlockSpec((TR, width), lambda i: (i, out_block)),
        out_shape=jax.ShapeDtypeStruct(buf.shape, BF16), input_output_aliases={3: 0},
        compiler_params=_cparams("parallel"),
    )(src, cos, sin, buf)


QROWS = NQ_PER_KV * 128


NBIAS = NCH + 1
KV_PER_STEP = 4


def _bias_table():
    db = lax.broadcasted_iota(jnp.int32, (NBIAS, 128, QROWS), 0) - 1
    ki = lax.broadcasted_iota(jnp.int32, (NBIAS, 128, QROWS), 1)
    qi = lax.broadcasted_iota(jnp.int32, (NBIAS, 128, QROWS), 2) & 127
    d = db * 128 + qi - ki
    cnt = ((d <= 128).astype(F32) + (((d & 3) == 0) & (d <= 512)).astype(F32) + ((d & 15) == 0).astype(F32))
    return jnp.where((d >= 0) & (cnt > 0.0), jnp.log(jnp.maximum(cnt, 1.0)), NEG)


def _qt_spec():
    return pl.BlockSpec((None, None, HD, QROWS), lambda j, i: (j, i, 0, 0))


def _stat_spec():
    return pl.BlockSpec((None, None, 1, QROWS), lambda j, i: (j, i, 0, 0))


def _attn_fwd(qt, kh, vt, bias, rider=None):
    def body(q_ref, k_ref, v_ref, b_ref, o_ref, lse_ref, rows_ref, m_ref, l_ref, acc_ref):
        qb = pl.program_id(1)
        m_ref[...] = jnp.full_like(m_ref, NEG)
        l_ref[...] = jnp.zeros_like(l_ref)
        acc_ref[...] = jnp.zeros_like(acc_ref)

        def keys(off, size, bias_):
            for h in range(KV_PER_STEP):
                m = m_ref[h]
                s = _dot(k_ref[h, pl.ds(off, size), :], q_ref[h], NN) + bias_
                m_new = jnp.maximum(m, jnp.max(s, axis=0, keepdims=True))
                p = jnp.exp(s - m_new)
                a = jnp.exp(m - m_new)
                m_ref[h] = m_new
                l_ref[h] = a * l_ref[h] + jnp.sum(p, axis=0, keepdims=True)
                acc_ref[h] = a * acc_ref[h] + _dot(v_ref[h, :, pl.ds(off, size)], p, NN)

        @pl.loop(0, (qb + 1) // 2)
        def _(i):
            bias2 = jnp.concatenate([b_ref[qb - 2 * i + 1], b_ref[qb - 2 * i]], axis=0)
            keys(pl.multiple_of(i * 256, 256), 256, bias2)

        @pl.when(qb % 2 == 0)
        def _():
            keys(pl.multiple_of(qb * 128, 128), 128, b_ref[1])

        outs = []
        for h in range(KV_PER_STEP):
            outs.append(acc_ref[h] / l_ref[h])
            o_ref[h] = outs[h]
            lse_ref[h] = m_ref[h] + jnp.log(l_ref[h])
        rows_ref[...] = _blocks_to_rows(outs).astype(BF16)

    kvs = KV_PER_STEP
    qspec = pl.BlockSpec((kvs, None, HD, QROWS), lambda j, i: (j, i, 0, 0))
    return _call(
        body, "attn_fwd", (NKV // kvs, NCH),
        [qspec, pl.BlockSpec((kvs, S, HD), lambda j, i: (j, 0, 0)),
         pl.BlockSpec((kvs, HD, S), lambda j, i: (j, 0, 0)),
         pl.BlockSpec((NBIAS, 128, QROWS), lambda j, i: (0, 0, 0))],
        [qspec, pl.BlockSpec((kvs, None, 1, QROWS), lambda j, i: (j, i, 0, 0)),
         pl.BlockSpec((128, QCOLS * kvs), lambda j, i: (i, j))],
        [jax.ShapeDtypeStruct((NKV, NCH, HD, QROWS), F32), jax.ShapeDtypeStruct((NKV, NCH, 1, QROWS), F32),
         jax.ShapeDtypeStruct((S, D), BF16)],
        (qt, kh, vt, bias),
        [pltpu.VMEM((kvs, 1, QROWS), F32), pltpu.VMEM((kvs, 1, QROWS), F32), pltpu.VMEM((kvs, HD, QROWS), F32)],
        ("parallel", "parallel"), rider)


def _attn_delta(ot, dot_):
    def body(o_ref, do_ref, dl_ref):
        dl_ref[...] = jnp.sum(o_ref[...] * do_ref[...].astype(F32), axis=1, keepdims=True)

    spec = pl.BlockSpec((None, NCH, HD, QROWS), lambda j: (j, 0, 0, 0))
    return pl.pallas_call(
        body, name="attn_delta", grid=(NKV,), in_specs=[spec, spec],
        out_specs=pl.BlockSpec((None, NCH, 1, QROWS), lambda j: (j, 0, 0, 0)),
        out_shape=jax.ShapeDtypeStruct((NKV, NCH, 1, QROWS), F32), compiler_params=_cparams("parallel"),
    )(ot, dot_)


def _attn_bwd(qt, kh, kt, vh, dot_, lse, delta, bias, rider=None):
    def body(qt_ref, k_ref, kt_ref, v_ref, dot_ref, lse_ref, dl_ref, b_ref, dq_ref, dk_ref, dv_ref):
        kb = pl.program_id(1)

        @pl.when(kb == 0)
        def _():
            dq_ref[...] = jnp.zeros_like(dq_ref)

        def blocks(carry, qbs):
            out = list(carry)
            for h in range(KV_PER_STEP):
                k, kt_, v = k_ref[h], kt_ref[h], v_ref[h]
                for qb in qbs:
                    st = _dot(k, qt_ref[h, qb], NN) + b_ref[qb - kb + 1]
                    pt = jnp.exp(st - lse_ref[h, qb])
                    dst = pt * (_dot(v, dot_ref[h, qb], NN) - dl_ref[h, qb])
                    dq_ref[h, qb] += _dot(kt_, dst, NN)
                    out[2 * h] = out[2 * h] + _dot(dst, qt_ref[h, qb], NT)
                    out[2 * h + 1] = out[2 * h + 1] + _dot(pt, dot_ref[h, qb], NT)
            return tuple(out)

        res = (jnp.zeros((128, HD), F32),) * (2 * KV_PER_STEP)
        res = lax.cond(kb % 2 == 1, lambda c: blocks(c, (kb,)), lambda c: c, res)
        res = lax.fori_loop((kb + 1) // 2, NCH // 2, lambda j, c: blocks(c, (2 * j, 2 * j + 1)), res)
        for h in range(KV_PER_STEP):
            dk_ref[h] = res[2 * h]
            dv_ref[h] = res[2 * h + 1]

    kvs = KV_PER_STEP
    tspec = pl.BlockSpec((kvs, NCH, HD, QROWS), lambda j, i: (j, 0, 0, 0))
    kspec = pl.BlockSpec((kvs, 128, HD), lambda j, i: (j, i, 0))
    sspec = pl.BlockSpec((kvs, NCH, 1, QROWS), lambda j, i: (j, 0, 0, 0))
    return _call(
        body, "attn_bwd", (NKV // kvs, NCH),
        [tspec, kspec, pl.BlockSpec((kvs, HD, 128), lambda j, i: (j, 0, i)), kspec, tspec,
         sspec, sspec, pl.BlockSpec((NBIAS, 128, QROWS), lambda j, i: (0, 0, 0))],
        [tspec, kspec, kspec],
        [jax.ShapeDtypeStruct((NKV, NCH, HD, QROWS), F32),
         jax.ShapeDtypeStruct((NKV, S, HD), F32), jax.ShapeDtypeStruct((NKV, S, HD), F32)],
        (qt, kh, kt, vh, dot_, lse, delta, bias), sem=("parallel", "arbitrary"), rider=rider)


CONV_BLK = 256
CONV_COL0 = 1536 // CONV_BLK


def _shift_down(u, j, row):
    return jnp.where(row >= j, pltpu.roll(u, j, 0), 0.0)


def _conv_pre(u, w_ref, b_ref, row):
    y = b_ref[...] + w_ref[CONV_K - 1:CONV_K, :] * u
    for j in range(1, CONV_K):
        y = y + w_ref[CONV_K - 1 - j:CONV_K - j, :] * _shift_down(u, j, row)
    return y


def _conv_fwd(proj, convw, convb):
    def body(u_ref, w_ref, b_ref, o_ref):
        u = u_ref[...]
        row = lax.broadcasted_iota(jnp.int32, u.shape, 0)
        y = _conv_pre(u, w_ref, b_ref, row)
        o_ref[...] = y * _sigmoid(y)

    return pl.pallas_call(
        body, name="conv_fwd", grid=(CONV_C // CONV_BLK,),
        in_specs=[pl.BlockSpec((S, CONV_BLK), lambda i: (0, CONV_COL0 + i)),
                  pl.BlockSpec((CONV_K, CONV_BLK), lambda i: (0, i)),
                  pl.BlockSpec((1, CONV_BLK), lambda i: (0, i))],
        out_specs=pl.BlockSpec((S, CONV_BLK), lambda i: (0, i)),
        out_shape=jax.ShapeDtypeStruct((S, CONV_C), F32), compiler_params=_cparams("parallel"),
    )(proj, convw, convb)


def _conv_bwd(dact, proj, convw, convb, dproj):
    def body(da_ref, u_ref, w_ref, b_ref, buf_ref, du_ref, dw_ref, db_ref):
        u = u_ref[...]
        row = lax.broadcasted_iota(jnp.int32, u.shape, 0)
        y = _conv_pre(u, w_ref, b_ref, row)
        sg = _sigmoid(y)
        dy = da_ref[...] * (sg * (1.0 + y * (1.0 - sg)))
        db_ref[...] = jnp.sum(dy, axis=0, keepdims=True)
        du = w_ref[CONV_K - 1:CONV_K, :] * dy
        r8 = lax.broadcasted_iota(jnp.int32, (8, CONV_BLK), 0)
        dw = jnp.where(r8 == CONV_K - 1, jnp.sum(dy * u, axis=0, keepdims=True), 0.0)
        for j in range(1, CONV_K):
            du = du + w_ref[CONV_K - 1 - j:CONV_K - j, :] * jnp.where(row < S - j, pltpu.roll(dy, S - j, 0), 0.0)
            dw = dw + jnp.where(r8 == CONV_K - 1 - j,
                                jnp.sum(dy * _shift_down(u, j, row), axis=0, keepdims=True), 0.0)
        du_ref[...] = du.astype(BF16)
        dw_ref[...] = dw

    return pl.pallas_call(
        body, name="conv_bwd", grid=(CONV_C // CONV_BLK,),
        in_specs=[pl.BlockSpec((S, CONV_BLK), lambda i: (0, i)),
                  pl.BlockSpec((S, CONV_BLK), lambda i: (0, CONV_COL0 + i)),
                  pl.BlockSpec((CONV_K, CONV_BLK), lambda i: (0, i)),
                  pl.BlockSpec((1, CONV_BLK), lambda i: (0, i)), pl.BlockSpec(memory_space=pl.ANY)],
        out_specs=[pl.BlockSpec((S, CONV_BLK), lambda i: (0, CONV_COL0 + i)),
                   pl.BlockSpec((8, CONV_BLK), lambda i: (0, i)), pl.BlockSpec((1, CONV_BLK), lambda i: (0, i))],
        out_shape=[jax.ShapeDtypeStruct(dproj.shape, BF16), jax.ShapeDtypeStruct((8, CONV_C), F32),
                   jax.ShapeDtypeStruct((1, CONV_C), F32)],
        input_output_aliases={4: 0}, compiler_params=_cparams("parallel"),
    )(dact, proj, convw, convb, dproj)


NPAIR = 8


def _ssd_scalars(dtr_ref, dtb_ref, alog_ref):
    z = dtr_ref[...] + dtb_ref[...]
    dt = jnp.maximum(z, 0.0) + jnp.log(1.0 + jnp.exp(-jnp.abs(z)))
    a = -jnp.exp(alog_ref[...])
    r = lax.broadcasted_iota(jnp.int32, (128, 128), 0)
    c = lax.broadcasted_iota(jnp.int32, (128, 128), 1)
    tri = (r >= c).astype(F32)
    cs = _dot_exact(tri, dt * a)
    return z, dt, a, cs, r, c


def _by_lane(cs, dt):
    head = lax.broadcasted_iota(jnp.int32, (128, SSM_W), 0)
    lane = lax.broadcasted_iota(jnp.int32, (128, SSM_W), 1)
    sel = (head == lane // HD).astype(F32)
    cs_l = _dot_exact(cs, sel, "b")
    last_l = cs_l[127:128, :]
    return sel, jnp.exp(cs_l), jnp.exp(last_l - cs_l), _dot_exact(dt, sel, "b")


def _pair_terms(cs, h1, h2):
    return (cs[:, h1:h1 + 1], cs[:, h2:h2 + 1],
            jnp.exp(cs[127:128, h1:h1 + 1]), jnp.exp(cs[127:128, h2:h2 + 1]))


def _gate_norm(y, zv, w):
    yg = y * (zv * _sigmoid(zv))
    outs, rs = [], []
    for g in range(2):
        blk = yg[:, 512 * g:512 * (g + 1)]
        r = lax.rsqrt(jnp.mean(blk * blk, axis=-1, keepdims=True) + EPS)
        outs.append(blk * r)
        rs.append(r)
    return jnp.concatenate(outs, axis=1), rs, yg


def _ssd_fwd(xbc, proj, dtb, alog, dskip_l, ssmw):
    def body(x_ref, b_ref, c_ref, dtr_ref, z_ref, dtb_ref, alog_ref, dsk_ref, w_ref, y_ref, yn_ref, hp_ref, h_ref):
        @pl.when(pl.program_id(0) == 0)
        def _():
            h_ref[...] = jnp.zeros_like(h_ref)

        _, dt, _, cs, r, c = _ssd_scalars(dtr_ref, dtb_ref, alog_ref)
        cst = cs.T
        causal = r >= c
        lo = c < HD
        _, e_all, dte_all, dt_all = _by_lane(cs, dt)
        hp_ref[...] = h_ref[...]
        for g in range(2):
            bg = b_ref[:, 128 * g:128 * (g + 1)]
            cg = c_ref[:, 128 * g:128 * (g + 1)]
            cb = _dot(cg, bg, NT)
            for j in range(4):
                pj = 4 * g + j
                h1, h2 = 2 * pj, 2 * pj + 1
                sl = slice(128 * pj, 128 * (pj + 1))
                xp = x_ref[:, sl]
                c1, c2, cd1, cd2 = _pair_terms(cs, h1, h2)
                e_l, dte_l = e_all[:, sl], dte_all[:, sl]
                xdt = xp * dt_all[:, sl]
                m1 = cb * jnp.exp(jnp.where(causal, c1 - cst[h1:h1 + 1, :], NEG))
                m2 = cb * jnp.exp(jnp.where(causal, c2 - cst[h2:h2 + 1, :], NEG))
                yd = jnp.where(lo, _dot(m1, xdt, NN), _dot(m2, xdt, NN))
                hp = h_ref[pj]
                yo = _dot(cg, hp, NT) * e_l
                st = _dot(xdt * dte_l, bg, TN)
                h_ref[pj] = hp * jnp.where(r < HD, cd1, cd2) + st
                y_ref[:, sl] = yd + yo + dsk_ref[:, sl] * xp
        yn, _, _ = _gate_norm(y_ref[...], z_ref[...], w_ref[...])
        yn_ref[...] = (yn * w_ref[...]).astype(BF16)

    return pl.pallas_call(
        body, name="ssd_fwd", grid=(NCH,),
        in_specs=[pl.BlockSpec((128, SSM_W), lambda i: (i, 0)),
                  pl.BlockSpec((128, 256), lambda i: (i, 4)), pl.BlockSpec((128, 256), lambda i: (i, 5)),
                  pl.BlockSpec((128, 128), lambda i: (i, COL_DT // 128)),
                  pl.BlockSpec((128, SSM_W), lambda i: (i, 3)),
                  pl.BlockSpec((1, 128), lambda i: (0, 0)), pl.BlockSpec((1, 128), lambda i: (0, 0)),
                  pl.BlockSpec((1, SSM_W), lambda i: (0, 0)), pl.BlockSpec((1, SSM_W), lambda i: (0, 0))],
        out_specs=[pl.BlockSpec((128, SSM_W), lambda i: (i, 0)), pl.BlockSpec((128, SSM_W), lambda i: (i, 0)),
                   pl.BlockSpec((None, NPAIR, 128, 128), lambda i: (i, 0, 0, 0))],
        out_shape=[jax.ShapeDtypeStruct((S, SSM_W), F32), jax.ShapeDtypeStruct((S, SSM_W), BF16),
                   jax.ShapeDtypeStruct((NCH, NPAIR, 128, 128), F32)],
        scratch_shapes=[pltpu.VMEM((NPAIR, 128, 128), F32)],
        compiler_params=_cparams("arbitrary"),
    )(xbc, xbc, xbc, proj, proj, dtb, alog, dskip_l, ssmw)


def _ssd_bwd(dmixed, y, xbc, proj, hprev, dtb, alog, dskip_l, ssmw, rider=None):
    def body(dyn_ref, y_ref, x_ref, b_ref, c_ref, dtr_ref, z_ref, hp_ref, dtb_ref, alog_ref, dsk_ref, w_ref,
             dxbc_ref, dz_ref, ddt_ref, dw_ref, dsc_ref, g_ref):
        @pl.when(pl.program_id(0) == 0)
        def _():
            g_ref[...] = jnp.zeros_like(g_ref)
            dsc_ref[...] = jnp.zeros_like(dsc_ref)

        z, dt, a, cs, r, c = _ssd_scalars(dtr_ref, dtb_ref, alog_ref)
        cst = cs.T
        causal = r >= c
        lo = c < HD

        yv = y_ref[...]
        zv = z_ref[...]
        wv = w_ref[...]
        ygn, rs, yg = _gate_norm(yv, zv, wv)
        dyn = dyn_ref[...]
        _acc_rows(dw_ref, dyn * ygn)
        dynw = dyn * wv
        parts = []
        for g in range(2):
            sl = slice(512 * g, 512 * (g + 1))
            a_g, n_g = dynw[:, sl], ygn[:, sl]
            parts.append(rs[g] * (a_g - n_g * jnp.mean(a_g * n_g, axis=-1, keepdims=True)))
        dyg = jnp.concatenate(parts, axis=1)
        sz = _sigmoid(zv)
        dz_ref[...] = (dyg * yv * (sz * (1.0 + zv * (1.0 - sz)))).astype(BF16)
        dy_all = dyg * (zv * sz)

        dcs_cols = jnp.zeros((128, 128), F32)
        dcs_rows = jnp.zeros((128, 128), F32)
        sel, e_all, dte_all, dt_all = _by_lane(cs, dt)
        x_all, b_all, c_all, dsk_all = x_ref[...], b_ref[...], c_ref[...], dsk_ref[...]
        hp_all, g_all = hp_ref[...], g_ref[...]
        g_new, dx_parts, db_parts, dc_parts = [], [], [], []
        dyx_parts, ryo_parts, qx_parts, dxx_parts, gh_parts = [], [], [], [], []
        for g in range(2):
            bg = b_all[:, 128 * g:128 * (g + 1)]
            cg = c_all[:, 128 * g:128 * (g + 1)]
            cb = _dot(cg, bg, NT)
            dcb = jnp.zeros((128, 128), F32)
            db_acc = jnp.zeros((128, NST), F32)
            dc_acc = jnp.zeros((128, NST), F32)
            for j in range(4):
                pj = 4 * g + j
                h1, h2 = 2 * pj, 2 * pj + 1
                sl = slice(128 * pj, 128 * (pj + 1))
                xp = x_all[:, sl]
                dyp = dy_all[:, sl]
                c1, c2, cd1, cd2 = _pair_terms(cs, h1, h2)
                e_l, dte_l, dt_l = e_all[:, sl], dte_all[:, sl], dt_all[:, sl]
                xdt = xp * dt_l
                hp = hp_all[pj]
                gp = g_all[pj]
                dyx_parts.append(dyp * xp)
                dzs = dyp * e_l
                dc_acc = dc_acc + _dot(dzs, hp, NN)
                ryo_parts.append(dyp * (_dot(cg, hp, NT) * e_l))
                qm = _dot(bg, gp, NT)
                dxdt = qm * dte_l
                qx_parts.append(qm * xdt)
                db_acc = db_acc + _dot(xdt * dte_l, gp, NN)
                gh_parts.append(gp * hp)
                g_new.append(_dot(dzs, cg, TN) + jnp.where(r < HD, cd1, cd2) * gp)
                for hh, ch, msk in ((h1, c1, lo), (h2, c2, jnp.logical_not(lo))):
                    lm = jnp.exp(jnp.where(causal, ch - cst[hh:hh + 1, :], NEG))
                    mm = cb * lm
                    dm = jnp.where(causal, _dot(jnp.where(msk, dyp, 0.0), xdt, NT), 0.0)
                    w = dm * mm
                    dcs_cols = dcs_cols + jnp.where(c == hh, jnp.sum(w, axis=1, keepdims=True), 0.0)
                    dcs_rows = dcs_rows + jnp.where(r == hh, jnp.sum(w, axis=0, keepdims=True), 0.0)
                    dcb = dcb + dm * lm
                    dxdt = dxdt + jnp.where(msk, _dot(mm, dyp, TN), 0.0)
                dxx_parts.append(dxdt * xp)
                dx_parts.append(dsk_all[:, sl] * dyp + dxdt * dt_l)
            db_parts.append(db_acc + _dot(dcb, cg, TN))
            dc_parts.append(dc_acc + _dot(dcb, bg, NN))
        g_ref[...] = jnp.stack(g_new)
        dxbc_ref[...] = jnp.concatenate(dx_parts + db_parts + dc_parts, axis=1)

        selt = (lax.broadcasted_iota(jnp.int32, (SSM_W, 128), 0) // HD
                == lax.broadcasted_iota(jnp.int32, (SSM_W, 128), 1)).astype(F32)

        def by_head(parts):
            return _dot_exact(jnp.concatenate(parts, axis=1), selt, "b")

        ddt_x = by_head(dxx_parts)
        dd_row = jnp.sum(by_head(dyx_parts), axis=0, keepdims=True)
        t_all = by_head(qx_parts) * jnp.exp(cs[127:128, :] - cs)
        gh = jnp.sum(_dot_exact(sel, jnp.concatenate(gh_parts, axis=0)), axis=1, keepdims=True)
        gh_row = jnp.broadcast_to(gh, (128, 128)).T[0:1, :]
        at_end = jnp.sum(t_all, axis=0, keepdims=True) + gh_row * jnp.exp(cs[127:128, :])
        dcs = by_head(ryo_parts) - t_all + dcs_cols + jnp.where(r == 127, at_end, 0.0) - dcs_rows.T
        dad = _dot_exact((c >= r).astype(F32), dcs)
        ddt = dad * a + ddt_x
        ddtr = jnp.where(c < 16, ddt * _sigmoid(z), 0.0)
        ddt_ref[...] = ddtr.astype(BF16)
        r8 = lax.broadcasted_iota(jnp.int32, (8, 128), 0)
        dsc_ref[...] += (jnp.where(r8 == 0, jnp.sum(ddtr, axis=0, keepdims=True), 0.0)
                         + jnp.where(r8 == 1, jnp.sum(dad * dt, axis=0, keepdims=True) * a, 0.0)
                         + jnp.where(r8 == 2, dd_row, 0.0))

    rev = NCH - 1
    return _call(
        body, "ssd_bwd", (NCH,),
        [pl.BlockSpec((128, SSM_W), lambda i: (rev - i, 0)),
         pl.BlockSpec((128, SSM_W), lambda i: (rev - i, 0)),
         pl.BlockSpec((128, SSM_W), lambda i: (rev - i, 0)),
         pl.BlockSpec((128, 256), lambda i: (rev - i, 4)), pl.BlockSpec((128, 256), lambda i: (rev - i, 5)),
         pl.BlockSpec((128, 128), lambda i: (rev - i, COL_DT // 128)),
         pl.BlockSpec((128, SSM_W), lambda i: (rev - i, 3)),
         pl.BlockSpec((None, NPAIR, 128, 128), lambda i: (rev - i, 0, 0, 0)),
         pl.BlockSpec((1, 128), lambda i: (0, 0)), pl.BlockSpec((1, 128), lambda i: (0, 0)),
         pl.BlockSpec((1, SSM_W), lambda i: (0, 0)), pl.BlockSpec((1, SSM_W), lambda i: (0, 0))],
        [pl.BlockSpec((128, CONV_C), lambda i: (rev - i, 0)),
         pl.BlockSpec((128, SSM_W), lambda i: (rev - i, 3)),
         pl.BlockSpec((128, 128), lambda i: (rev - i, 0)),
         pl.BlockSpec((1, SSM_W), lambda i: (0, 0)), pl.BlockSpec((8, 128), lambda i: (0, 0))],
        [jax.ShapeDtypeStruct((S, CONV_C), F32), jax.ShapeDtypeStruct((S, WIN_PAD), BF16),
         jax.ShapeDtypeStruct((S, 128), BF16), jax.ShapeDtypeStruct((1, SSM_W), F32),
         jax.ShapeDtypeStruct((8, 128), F32)],
        (dmixed, y, xbc, xbc, xbc, proj, proj, hprev, dtb, alog, dskip_l, ssmw),
        [pltpu.VMEM((NPAIR, 128, 128), F32)], ("arbitrary",), rider)


def _cast_stack(name, slot, arrs, tr, tc):
    n = len(arrs)
    rows, cols = arrs[0].shape

    def body(s_ref, *refs):
        for i in range(n):
            refs[n][i] = refs[i][...].astype(BF16)

    return pl.pallas_call(
        body, name=name,
        grid_spec=pltpu.PrefetchScalarGridSpec(
            num_scalar_prefetch=1, grid=(rows // tr, cols // tc),
            in_specs=[pl.BlockSpec((tr, tc), lambda i, j, sr: (i, j))] * n,
            out_specs=pl.BlockSpec((None, n, tr, tc), lambda i, j, sr: (sr[0], 0, i, j))),
        out_shape=jax.ShapeDtypeStruct((NSH, n, rows, cols), BF16),
        compiler_params=_cparams("parallel", "parallel"),
    )(slot, *arrs)


def _pair_sum(name, c_idx, ps, th):
    n = len(ps)
    _, rows, _ = ps[0].shape

    def body(c_ref, *refs):
        mine, whole, out, theirs = refs[:n], refs[n:2 * n], refs[2 * n:3 * n], refs[3 * n:4 * n]
        send, recv = refs[4 * n], refs[4 * n + 1]
        s, i = pl.program_id(0), pl.program_id(1)
        x, y, c, _ = _place()

        def copies(slot):
            return [_rcopy(whole[k].at[slot, :, pl.ds((1 - c) * HALF, HALF)], theirs[k].at[slot],
                           send.at[slot * n + k], recv.at[slot * n + k], (x, y, 1 - c)) for k in range(n)]

        @pl.when((s == 0) & (i == 0))
        def _():
            for slot in range(NSH):
                for cp in copies(slot):
                    cp.start()

        @pl.when(i == 0)
        def _():
            for slot in range(NSH):
                @pl.when(s == slot)
                def _():
                    for cp in copies(slot):
                        cp.wait()

        rows_i = slice(None) if th == rows else pl.ds(pl.multiple_of(i * th, th), th)
        for k in range(n):
            out[k][...] = (mine[k][...].astype(F32) + theirs[k][s, rows_i, :].astype(F32)).astype(BF16)

    spec = pl.BlockSpec((None, th, HALF), lambda s, i, cr: (s, i, 0))
    return pl.pallas_call(
        body, name=name,
        grid_spec=pltpu.PrefetchScalarGridSpec(
            num_scalar_prefetch=1, grid=(NSH, rows // th),
            in_specs=[pl.BlockSpec((None, th, HALF), lambda s, i, cr: (s, i, cr[0]))] * n + _any_specs(n),
            out_specs=[spec] * n,
            scratch_shapes=[pltpu.VMEM((NSH, rows, HALF), BF16)] * n
            + [pltpu.SemaphoreType.DMA((NSH * n,)), pltpu.SemaphoreType.DMA((NSH * n,))]),
        out_shape=[jax.ShapeDtypeStruct((NSH, rows, HALF), BF16)] * n,
        compiler_params=_cparams("arbitrary", "arbitrary"),
    )(c_idx, *ps, *ps)


def _chip_sum(name, place, cs, ts, th):
    n = len(ts)
    _, rows, _ = ts[0].shape

    def body(p_ref, *refs):
        for i in range(n):
            t = refs[n + i][...].astype(F32)
            refs[2 * n + i][...] = ((refs[i][...].astype(F32) + t[0]) + t[1]) + t[2]

    return pl.pallas_call(
        body, name=name,
        grid_spec=pltpu.PrefetchScalarGridSpec(
            num_scalar_prefetch=1, grid=(rows // th,),
            in_specs=[pl.BlockSpec((None, th, HALF), lambda i, pr: (pr[0], i, 0))] * n
            + [pl.BlockSpec((3, th, HALF), lambda i, pr: (0, i, 0))] * n,
            out_specs=[pl.BlockSpec((th, HALF), lambda i, pr: (i, pr[1]))] * n),
        out_shape=[jax.ShapeDtypeStruct((rows, D), F32)] * n, compiler_params=_cparams("parallel"),
    )(place, *cs, *ts)


def _adamw(name, ws, gs, ms, vs, tr, tc):
    n = len(ws)
    shape = ws[0].shape
    rows, cols, mid = shape[0], shape[-1], shape[1:-1]
    c1 = 1.0 / (1.0 - ADAM_B1 ** ADAM_STEP)
    c2 = 1.0 / (1.0 - ADAM_B2 ** ADAM_STEP)

    def body(*refs):
        for i in range(n):
            w, g, m, v = (refs[k * n + i][...] for k in range(4))
            m2 = ADAM_B1 * m + (1.0 - ADAM_B1) * g
            v2 = ADAM_B2 * v + (1.0 - ADAM_B2) * (g * g)
            refs[4 * n + 4 * i][...] = -ADAM_LR * ((m2 * c1) / (jnp.sqrt(v2 * c2) + ADAM_EPS) + ADAM_WD * w)
            refs[4 * n + 4 * i + 1][...] = m2
            refs[4 * n + 4 * i + 2][...] = v2
            refs[4 * n + 4 * i + 3][...] = g

    spec = pl.BlockSpec((tr,) + mid + (tc,), lambda i, j: (i,) + (0,) * len(mid) + (j,))
    outs = pl.pallas_call(
        body, name=name, grid=(rows // tr, cols // tc), in_specs=[spec] * (4 * n), out_specs=[spec] * (4 * n),
        out_shape=[jax.ShapeDtypeStruct(shape, F32)] * (4 * n),
        compiler_params=_cparams("parallel", "parallel"),
    )(*ws, *gs, *ms, *vs)
    return [tuple(outs[4 * i:4 * i + 4]) for i in range(n)]


def _place():
    x, y, c = lax.axis_index("x"), lax.axis_index("y"), lax.axis_index("c")
    chips = [(1 - x, y), (x, 1 - y), (1 - x, 1 - y)]
    return x, y, c, chips


def _any_specs(n):
    return [pl.BlockSpec(memory_space=pl.ANY)] * n


def _rcopy(src, dst, send_sem, recv_sem, dev):
    return pltpu.make_async_remote_copy(src_ref=src, dst_ref=dst, send_sem=send_sem, recv_sem=recv_sem,
                                        device_id=dev, device_id_type=MESH)


def _gather_rider(bufs, views):
    n = len(bufs)

    def start(rin, rout, sems):
        send, recv = sems[0], sems[1]
        x, y, c, chips = _place()
        for j, chip in enumerate(chips):
            for b in range(n):
                mine = views[b](rout[b], 2 * x + y, c)
                _rcopy(mine, mine, send.at[j * n + b], recv.at[j * n + b], (chip[0], chip[1], c)).start()

    def finish(rin, rout, sems):
        send, recv, fsend, frecv = sems
        x, y, c, chips = _place()
        passed = []
        for j, chip in enumerate(chips):
            for b in range(n):
                landed = views[b](rout[b], 2 * chip[0] + chip[1], c)
                _rcopy(landed, landed, send.at[j * n + b], recv.at[j * n + b], (x, y, c)).wait_recv()
                fw = _rcopy(landed, landed, fsend.at[j * n + b], frecv.at[j * n + b], (x, y, 1 - c))
                fw.start()
                passed.append(fw)
        for j, chip in enumerate(chips):
            for b in range(n):
                other = views[b](rout[b], 2 * chip[0] + chip[1], 1 - c)
                _rcopy(other, other, fsend.at[j * n + b], frecv.at[j * n + b], (x, y, c)).wait_recv()
        for j, chip in enumerate(chips):
            for b in range(n):
                mine = views[b](rout[b], 2 * x + y, c)
                _rcopy(mine, mine, send.at[j * n + b], recv.at[j * n + b], (x, y, c)).wait_send()
        for fw in passed:
            fw.wait_send()

    return _Rider(list(bufs), [jax.ShapeDtypeStruct(a.shape, a.dtype) for a in bufs], {b: b for b in range(n)},
                  [pltpu.SemaphoreType.DMA((3 * n,))] * 4, start, finish)


def _small_gather_rider(cw):
    def descs(rin, rout, sems, x, y, c, chips):
        return [_rcopy(rin[0], rout[0].at[2 * x + y], sems[1].at[j], sems[2].at[j], (chip[0], chip[1], c))
                for j, chip in enumerate(chips)]

    def start(rin, rout, sems):
        x, y, c, chips = _place()
        pltpu.make_async_copy(rin[0], rout[0].at[2 * x + y], sems[0].at[0]).start()
        for cp in descs(rin, rout, sems, x, y, c, chips):
            cp.start()

    def finish(rin, rout, sems):
        x, y, c, chips = _place()
        for j, chip in enumerate(chips):
            _rcopy(rin[0], rout[0].at[2 * chip[0] + chip[1]], sems[1].at[j], sems[2].at[j], (x, y, c)).wait_recv()
        for cp in descs(rin, rout, sems, x, y, c, chips):
            cp.wait_send()
        pltpu.make_async_copy(rin[0], rout[0].at[2 * x + y], sems[0].at[0]).wait()

    return _Rider([cw], [jax.ShapeDtypeStruct((NSH,) + cw.shape, cw.dtype)], {},
                  [pltpu.SemaphoreType.DMA((1,)), pltpu.SemaphoreType.DMA((3,)), pltpu.SemaphoreType.DMA((3,))],
                  start, finish)


def _to_chips_rider(cs):
    n = len(cs)

    def descs(rin, rout, sems):
        x, y, c, chips = _place()
        return [_rcopy(rin[i].at[2 * chip[0] + chip[1]], rout[i].at[j], sems[0].at[j * n + i], sems[1].at[j * n + i],
                       (chip[0], chip[1], c)) for j, chip in enumerate(chips) for i in range(n)]

    def start(rin, rout, sems):
        for cp in descs(rin, rout, sems):
            cp.start()

    def finish(rin, rout, sems):
        for cp in descs(rin, rout, sems):
            cp.wait()

    return _Rider(list(cs), [jax.ShapeDtypeStruct((3,) + a.shape[1:], a.dtype) for a in cs], {},
                  [pltpu.SemaphoreType.DMA((3 * n,))] * 2, start, finish)


def _run_riders(name, riders):
    n_in = [len(r.operands) for r in riders]
    n_out = [len(r.out_shapes) for r in riders]
    n_sem = [len(r.sems) for r in riders]

    def body(*refs):
        parts, at = [], 0
        for counts in (n_in, n_out, n_sem):
            group = []
            for k in counts:
                group.append(refs[at:at + k])
                at += k
            parts.append(group)
        for i, r in enumerate(riders):
            r.start(parts[0][i], parts[1][i], parts[2][i])
        for i, r in enumerate(riders):
            r.finish(parts[0][i], parts[1][i], parts[2][i])

    aliases = {}
    for i, r in enumerate(riders):
        for k, v in r.aliases.items():
            aliases[sum(n_in[:i]) + k] = sum(n_out[:i]) + v
    res = pl.pallas_call(
        body, name=name, in_specs=_any_specs(sum(n_in)), out_specs=_any_specs(sum(n_out)),
        out_shape=[s for r in riders for s in r.out_shapes], input_output_aliases=aliases,
        scratch_shapes=[s for r in riders for s in r.sems],
    )(*[a for r in riders for a in r.operands])
    out, at = [], 0
    for k in n_out:
        out.append(list(res[at:at + k]))
        at += k
    return out


def _swap_halves(gs):
    n = len(gs)

    def body(*refs):
        dst, send, recv = refs[n:2 * n], refs[2 * n], refs[2 * n + 1]
        x, y, c, _ = _place()
        cps = []
        for i in range(n):
            mine = dst[i].at[:, pl.ds(c * HALF, HALF)]
            cps.append(pltpu.make_async_remote_copy(
                src_ref=mine, dst_ref=mine, send_sem=send.at[i], recv_sem=recv.at[i],
                device_id=(x, y, 1 - c), device_id_type=MESH))
        for cp in cps:
            cp.start()
        for i in range(n):
            other = dst[i].at[:, pl.ds((1 - c) * HALF, HALF)]
            pltpu.make_async_remote_copy(
                src_ref=other, dst_ref=other, send_sem=send.at[i], recv_sem=recv.at[i],
                device_id=(x, y, c), device_id_type=MESH).wait_recv()
        for cp in cps:
            cp.wait_send()

    return pl.pallas_call(
        body, name="grads_swap_halves", in_specs=_any_specs(n), out_specs=_any_specs(n),
        out_shape=[jax.ShapeDtypeStruct(g.shape, g.dtype) for g in gs],
        input_output_aliases={i: i for i in range(n)},
        scratch_shapes=[pltpu.SemaphoreType.DMA((n,)), pltpu.SemaphoreType.DMA((n,))],
    )(*gs)


SMALL_ROWS = 16


def _allreduce_small(vec):
    def body(v_ref, o_ref, buf, send, recv):
        x, y, c, _ = _place()
        me = 4 * x + 2 * y + c
        buf[me] = v_ref[...]
        cps = []
        for k in range(1, 8):
            peer = (x ^ (k >> 2), y ^ ((k >> 1) & 1), c ^ (k & 1))
            cps.append(pltpu.make_async_remote_copy(
                src_ref=v_ref, dst_ref=buf.at[me], send_sem=send.at[k - 1], recv_sem=recv.at[k - 1],
                device_id=peer, device_id_type=MESH))
        for cp in cps:
            cp.start()
        for k in range(1, 8):
            pltpu.make_async_remote_copy(
                src_ref=v_ref, dst_ref=buf.at[me ^ k], send_sem=send.at[k - 1], recv_sem=recv.at[k - 1],
                device_id=(x, y, c), device_id_type=MESH).wait_recv()
        for cp in cps:
            cp.wait_send()
        t = buf[0]
        for d in range(1, 8):
            t = t + buf[d]
        o_ref[...] = t

    return pl.pallas_call(
        body, name="allreduce_small",
        in_specs=[pl.BlockSpec(memory_space=pltpu.VMEM)], out_specs=pl.BlockSpec(memory_space=pltpu.VMEM),
        out_shape=jax.ShapeDtypeStruct((SMALL_ROWS, D), F32),
        scratch_shapes=[pltpu.VMEM((8, SMALL_ROWS, D), F32), pltpu.SemaphoreType.DMA((7,)),
                        pltpu.SemaphoreType.DMA((7,))],
    )(vec)


def _col_half(ref, slot, hc):
    return ref.at[slot, :, pl.ds(hc * HALF, HALF)]


def _stack_half(ref, slot, hc):
    return ref.at[slot, :, :, pl.ds(hc * HALF, HALF)]


def _row_tile(rows):
    for t in range(512, 15, -16):
        if rows % t == 0:
            return t
    return rows


def _same_shape_runs(arrs):
    runs, a = [], 0
    for b in range(1, len(arrs) + 1):
        if b == len(arrs) or arrs[b].shape != arrs[a].shape:
            runs.append((a, b))
            a = b
    return runs


class _Comm:
    def __init__(self):
        x, y, c = lax.axis_index("x"), lax.axis_index("y"), lax.axis_index("c")
        self.c_idx = jnp.reshape(c, (1,)).astype(jnp.int32)
        self.place = jnp.stack([2 * x + y, c]).astype(jnp.int32)
        self.groups = {}

    @staticmethod
    def gather(*bufs):
        return _gather_rider(list(bufs), [_col_half if b.ndim == 3 else _stack_half for b in bufs])

    def reduce_rider(self, tag, names, ps):
        csums = []
        for a, b in _same_shape_runs(ps):
            csums += _pair_sum("pair_sum_%s%d" % (tag, a), self.c_idx, ps[a:b], _row_tile(ps[a].shape[1]))
        self.groups[tag] = [names, csums, None]
        return _to_chips_rider(csums)

    def landed(self, tag, ts):
        self.groups[tag][2] = ts

    def finish(self):
        names, csums, ts = [], [], []
        for group_names, group_csums, group_ts in self.groups.values():
            names += group_names
            csums += group_csums
            ts += group_ts
        order = sorted(range(len(names)), key=lambda i: csums[i].shape[1])
        names, csums, ts = ([v[i] for i in order] for v in (names, csums, ts))
        halves = []
        for a, b in _same_shape_runs(csums):
            halves += _chip_sum("chip_sum_%d" % a, self.place, csums[a:b], ts[a:b], _row_tile(csums[a].shape[1]))
        return dict(zip(names, _swap_halves(halves)))


ROPE_THETA = 10000.0
SMALL_1K = ("ffn1_pre_norm", "ffn1_post_norm", "mix_pre_norm", "ssm_norm", "mix_post_norm",
            "ffn2_pre_norm", "ffn2_post_norm")
SMALL_16 = ("dt_bias", "a_log", "d_skip")
OFF_CONVB = 7 * D
OFF_16 = OFF_CONVB + CONV_C
OFF_CONVW = OFF_16 + 48
OFF_LOSS = OFF_CONVW + CONV_K * CONV_C
SMALL_LEN = SMALL_ROWS * D


def _sds(shape, dtype):
    return jax.ShapeDtypeStruct(shape, dtype)


def _ridden(res, rider):
    return res if rider is not None else (res, None)


def _ffn_down(name, act, w, tail_of, rider=None):
    tail, o_specs, o_shapes = tail_of(TS)
    return _mm(name, [act, w.dn], NN, (S // TS,),
               [pl.BlockSpec((NSH, TS, FS), lambda i: (0, i, 0)),
                pl.BlockSpec((NSH, None, FS, D), lambda i: (0, w.d0, 0, 0))], o_specs, o_shapes, rider, tail)


def _ffn_dw(name, a, b, rider=None):
    return _mm(name, [a, b], TN, (NSH,),
               [pl.BlockSpec((None, S, FS), lambda s: (s, 0, 0)), pl.BlockSpec((S, D), lambda s: (0, 0))],
               pl.BlockSpec((None, FS, D), lambda s: (s, 0, 0)), _sds((NSH, FS, D), BF16), rider)


def _ffn_dn(name, dgate, dup, w, tail_of, rider=None):
    rows = TS // 2
    tail, o_specs, o_shapes = tail_of(rows)
    a2 = pl.BlockSpec((NSH, rows, FS), lambda i: (0, i, 0))
    return _mm(name, [dgate, w.gu, dup, w.gu], NN, (S // rows,),
               [a2, pl.BlockSpec((NSH, None, FS, D), lambda i: (0, w.g0, 0, 0)),
                a2, pl.BlockSpec((NSH, None, FS, D), lambda i: (0, w.g0 + 1, 0, 0))], o_specs, o_shapes, rider, tail)


def _out_proj_dx(dh, wout):
    def body(dh_ref, w_ref, dyn_ref, do_ref):
        dm = _dot(dh_ref[...], w_ref[...], NT)
        dyn_ref[...] = dm[:, D:]
        for b in range(TS // 128):
            for j, blk in enumerate(_rows_to_blocks(dm[128 * b:128 * (b + 1), :D])):
                do_ref[j, b] = blk.astype(BF16)

    return pl.pallas_call(
        body, name="out_proj_dx", grid=(S // TS,),
        in_specs=[pl.BlockSpec((TS, D), lambda i: (i, 0)), pl.BlockSpec((2 * D, D), lambda i: (0, 0))],
        out_specs=[pl.BlockSpec((TS, D), lambda i: (i, 0)),
                   pl.BlockSpec((NKV, TS // 128, HD, QROWS), lambda i: (0, i, 0, 0))],
        out_shape=[_sds((S, D), F32), _sds((NKV, NCH, HD, QROWS), BF16)], compiler_params=_cparams("parallel"),
    )(dh, wout)


def _heads(t, n):
    return t.reshape(S, n, HD).transpose(1, 0, 2)


def _unheads(t):
    return t.transpose(1, 0, 2).reshape(S, t.shape[0] * HD)


def _heads_t(t, n):
    return t.reshape(S, n, HD).transpose(1, 2, 0)


def _pad128(v):
    return jnp.pad(v, ((0, 0), (0, 128 - v.shape[1])))


def _local_step(x, positions, tgt, sp, gu1, d1, f2, wint, wout, convw, comm=None):
    inv_freq = ROPE_THETA ** (-jnp.arange(0, HD, 2, dtype=F32) / HD)
    ang = positions.astype(F32)[:, None] * inv_freq
    ang = jnp.concatenate([ang, ang, ang, ang], axis=-1)
    cos, sin = jnp.cos(ang), jnp.sin(ang)
    dtb, alog = _pad128(sp["dt_bias"]), _pad128(sp["a_log"])
    dskip_l = jnp.repeat(sp["d_skip"], HD, axis=1)
    convb = sp["conv_b"]

    n1 = _prenorm("prenorm1", x, sp["ffn1_pre_norm"])
    rider = comm.gather(d1) if comm else None
    (fg1, fu1, act1), got = _ridden(_ffn_up("ffn1_up", n1, _FfnW(gu1, 0, d1, 0), rider), rider)
    if comm:
        d1, = got
    w1 = _FfnW(gu1, 0, d1, 0)
    rider = comm.gather(wint) if comm else None
    (h1, x1, n2), got = _ridden(_ffn_down(
        "ffn1_down", act1, w1,
        lambda rows: _tail_postres(rows, x, sp["ffn1_post_norm"], 0.5, sp["mix_pre_norm"]), rider), rider)
    if comm:
        wint, = got
    wint_pad = jnp.pad(wint.reshape(WIN_COLS, D), ((0, WIN_PAD - WIN_COLS), (0, 0)))

    pw = WIN_PAD // 3
    proj = _mm("in_proj", [n2, wint_pad], NT, (S // TS, 3),
               [pl.BlockSpec((TS, D), lambda i, j: (i, 0)), pl.BlockSpec((pw, D), lambda i, j: (j, 0))],
               pl.BlockSpec((TS, pw), lambda i, j: (i, j)), _sds((S, WIN_PAD), F32))
    qt = _rope_q(proj, cos, sin)
    k_rot = _rope("rope_k", proj, D // KVW, KVW, cos, sin, 1.0, 1.0)
    v_bf = proj[:, D + KVW:D + 2 * KVW].astype(BF16)
    kh, vh = _heads(k_rot, NKV), _heads(v_bf, NKV)
    kt, vt = _heads_t(k_rot, NKV), _heads_t(v_bf, NKV)
    bias = _bias_table()
    rider = comm.gather(f2, wout) if comm else None
    (ot, lse, attn), got = _ridden(_attn_fwd(qt, kh, vt, bias, rider), rider)
    if comm:
        f2, wout = got
    w2 = _FfnW(f2, 0, f2, 2)
    wout = wout.reshape(2 * D, D)
    xbc = _conv_fwd(proj, convw, convb)
    y, yn, hprev = _ssd_fwd(xbc, proj, dtb, alog, dskip_l, sp["ssm_norm"])
    mixed = jnp.concatenate([attn, yn], axis=1)
    tail, o_specs, o_shapes = _tail_postres(TS, x1, sp["mix_post_norm"], 1.0, sp["ffn2_pre_norm"])
    h2, x2, n3 = _mm("out_proj", [mixed, wout], NN, (S // TS,),
                     [pl.BlockSpec((TS, 2 * D), lambda i: (i, 0)), pl.BlockSpec((2 * D, D), lambda i: (0, 0))],
                     o_specs, o_shapes, None, tail)

    fg2, fu2, act2 = _ffn_up("ffn2_up", n3, w2)
    dy, dh3, dp3, loss = _ffn_down(
        "ffn2_down", act2, w2, lambda rows: _tail_final(rows, x2, sp["ffn2_post_norm"], tgt, 0.5))

    dgate2, dup2 = _ffn_dact("ffn2_dact", dh3, w2, fg2, fu2)
    dws2 = [_ffn_dw("ffn2_dwg", dgate2, n3), _ffn_dw("ffn2_dwu", dup2, n3), _ffn_dw("ffn2_dwd", act2, dh3)]
    dx2, dh2, dg3, dp2 = _ffn_dn(
        "ffn2_dn", dgate2, dup2, w2,
        lambda rows: _tail_mid_bwd(rows, dy, x2, sp["ffn2_pre_norm"], h2, sp["mix_post_norm"], 1.0))

    dyn, dot_ = _out_proj_dx(dh2, wout)
    dwout = _mm("out_proj_dw", [mixed, dh2], TN, (2,),
                [pl.BlockSpec((S, D), lambda m: (0, m)), pl.BlockSpec((S, D), lambda m: (0, 0))],
                pl.BlockSpec((D, D), lambda m: (m, 0)), _sds((2 * D, D), BF16))
    dwout = dwout.reshape(NSH, 2 * D // NSH, D)

    def riding(tag, names, ps, call):
        rider = comm.reduce_rider(tag, names, ps) if comm else None
        res, got = _ridden(call(rider), rider)
        if comm:
            comm.landed(tag, got)
        return res

    dxbc, dproj, ddt, dssm, dsc = _ssd_bwd(dyn, y, xbc, proj, hprev, dtb, alog, dskip_l, sp["ssm_norm"])
    dproj, dcw8, dcb = _conv_bwd(dxbc, proj, convw, convb, dproj)
    delta = _attn_delta(ot, dot_)
    dqt, dkh, dvh = riding("a", BIG[3:6] + ("w_out",), dws2 + [dwout], lambda rider: _attn_bwd(
        qt, kh, kt, vh, dot_, lse, delta, bias, rider))
    dproj = _rope_dq(dqt, cos, sin, dproj)
    dproj = _rope("rope_dk", _unheads(dkh), 0, KVW, cos, sin, -1.0, 1.0, into=(dproj, D // KVW))
    dproj = lax.dynamic_update_slice(dproj, _unheads(dvh).astype(BF16), (0, D + KVW))
    dproj = lax.dynamic_update_slice(dproj, ddt, (0, COL_DT))
    dwint = _mm("in_proj_dw", [dproj, n2], TN, (3,),
                [pl.BlockSpec((S, pw), lambda j: (0, j)), pl.BlockSpec((S, D), lambda j: (0, 0))],
                pl.BlockSpec((pw, D), lambda j: (j, 0)), _sds((WIN_PAD, D), BF16))
    dwint = dwint[:WIN_COLS].reshape(NSH, WIN_SH, D)

    tail, o_specs, o_shapes = _tail_mid_bwd(TS, dx2, x1, sp["mix_pre_norm"], h1, sp["ffn1_post_norm"], 0.5)
    dx1, dh1, dg2, dp1 = riding("b", ("w_in",), [dwint], lambda rider: _mm(
        "in_proj_dx", [dproj, wint_pad], NN, (S // TS,),
        [pl.BlockSpec((TS, WIN_PAD), lambda i: (i, 0)), pl.BlockSpec((WIN_PAD, D), lambda i: (0, 0))],
        o_specs, o_shapes, rider, tail))

    dwd1 = _ffn_dw("ffn1_dwd", act1, dh1)
    dgate1, dup1 = riding("d", BIG[2:3], [dwd1], lambda rider: _ffn_dact("ffn1_dact", dh1, w1, fg1, fu1, rider))
    dwg1, dwu1 = _ffn_dw("ffn1_dwg", dgate1, n1), _ffn_dw("ffn1_dwu", dup1, n1)
    grad_x, dg1 = riding("g", BIG[0:2], [dwg1, dwu1], lambda rider: _ffn_dn(
        "ffn1_dn", dgate1, dup1, w1, lambda rows: _tail_first_bwd(rows, dx1, x, sp["ffn1_pre_norm"]), rider))
    dws1 = [dwg1, dwu1, dwd1]

    small = jnp.concatenate([
        dg1[0], dp1[0], dg2[0], dssm[0], dp2[0], dg3[0], dp3[0], dcb[0],
        dsc[0, :16], dsc[1, :16], dsc[2, :16], dcw8[:CONV_K].reshape(-1), loss[0, :1]])
    small = jnp.pad(small, (0, SMALL_LEN - small.shape[0])).reshape(SMALL_ROWS, D)
    if comm is None:
        return grad_x, dws1 + dws2 + [dwint, dwout], small
    return grad_x, comm.finish(), small


WEIGHTS = ("ffn1_pre_norm", "ffn1_w_gate", "ffn1_w_up", "ffn1_w_down", "ffn1_post_norm", "mix_pre_norm", "w_in",
           "conv_w", "conv_b", "dt_bias", "a_log", "d_skip", "ssm_norm", "w_out", "mix_post_norm", "ffn2_pre_norm",
           "ffn2_w_gate", "ffn2_w_up", "ffn2_w_down", "ffn2_post_norm")
BIG = ("ffn1_w_gate", "ffn1_w_up", "ffn1_w_down", "ffn2_w_gate", "ffn2_w_up", "ffn2_w_down", "w_in", "w_out")
TRANSPOSED = ("ffn1_w_gate", "ffn1_w_up", "ffn2_w_gate", "ffn2_w_up", "w_in")
SMALL_ORDER = SMALL_1K + ("conv_b",) + SMALL_16
CONVW_SH = CONV_C // NSH


def _shard2d(t, name):
    return t[0].T if name in TRANSPOSED else t[0]


def _unshard2d(t, name):
    return (t.T if name in TRANSPOSED else t)[None]


def _rows3d(t):
    return t.transpose(2, 0, 1)


def _pack_small(d, prefix, shard_of_convw):
    flat = jnp.concatenate([d[prefix + n][0] for n in SMALL_ORDER] + [shard_of_convw.reshape(-1)])
    return jnp.pad(flat, (0, SMALL_LEN - flat.shape[0])).reshape(SMALL_ROWS, D)


def _unpack_small(block, like):
    flat = block.reshape(-1)
    out, off = {}, 0
    for n in SMALL_ORDER:
        size = like[n].shape[1]
        out[n] = flat[off:off + size].reshape(1, size)
        off += size
    out["conv_w"] = flat[off:off + CONV_K * CONVW_SH].reshape(1, CONV_K, CONVW_SH)
    return out


def kernel(x, positions, ffn1_pre_norm, ffn1_w_gate, ffn1_w_up, ffn1_w_down, ffn1_post_norm, mix_pre_norm, w_in, conv_w, conv_b, dt_bias, a_log, d_skip, ssm_norm, w_out, mix_post_norm, ffn2_pre_norm, ffn2_w_gate, ffn2_w_up, ffn2_w_down, ffn2_post_norm, loss_target, m_ffn1_pre_norm, m_ffn1_w_gate, m_ffn1_w_up, m_ffn1_w_down, m_ffn1_post_norm, m_mix_pre_norm, m_w_in, m_conv_w, m_conv_b, m_dt_bias, m_a_log, m_d_skip, m_ssm_norm, m_w_out, m_mix_post_norm, m_ffn2_pre_norm, m_ffn2_w_gate, m_ffn2_w_up, m_ffn2_w_down, m_ffn2_post_norm, v_ffn1_pre_norm, v_ffn1_w_gate, v_ffn1_w_up, v_ffn1_w_down, v_ffn1_post_norm, v_mix_pre_norm, v_w_in, v_conv_w, v_conv_b, v_dt_bias, v_a_log, v_d_skip, v_ssm_norm, v_w_out, v_mix_post_norm, v_ffn2_pre_norm, v_ffn2_w_gate, v_ffn2_w_up, v_ffn2_w_down, v_ffn2_post_norm):
    given = dict(locals())
    xi, yi = lax.axis_index("x"), lax.axis_index("y")

    shard = jnp.reshape(2 * xi + yi, (1,)).astype(jnp.int32)
    big = {p + n: _shard2d(given[p + n], n) for n in BIG for p in ("", "m_", "v_")}
    gu1 = _cast_stack("cast_ffn1_gate_up", shard, [big[n] for n in BIG[0:2]], 176, D)
    d1 = _cast_stack("cast_ffn1_down", shard, [big[BIG[2]]], 176, D)
    f2 = _cast_stack("cast_ffn2", shard, [big[n] for n in BIG[3:6]], 176, D)
    winsh = _cast_stack("cast_w_in", shard, [big["w_in"]], WIN_SH, 256).reshape(NSH, WIN_SH, D)
    woutsh = _cast_stack("cast_w_out", shard, [big["w_out"]], 256, D).reshape(NSH, 2 * D // NSH, D)
    comm = _Comm()
    (gu1,), (cwf,) = _run_riders("gather_ffn1_gate_up", [comm.gather(gu1), _small_gather_rider(conv_w[0])])
    convw = cwf.transpose(1, 0, 2).reshape(CONV_K, CONV_C)

    sp = {n: given[n] for n in SMALL_ORDER}
    grad_x, big_grads, small = _local_step(x[0], positions[0], loss_target[0], sp, gu1, d1, f2, winsh, woutsh,
                                           convw, comm)

    tot = _allreduce_small(small).reshape(-1)
    loss = tot[OFF_LOSS]
    small_grads, off = {}, 0
    for n in SMALL_ORDER:
        size = given[n].shape[1]
        small_grads[n] = tot[off:off + size].reshape(1, size)
        off += size
    dconvw = tot[OFF_CONVW:OFF_CONVW + CONV_K * CONV_C].reshape(CONV_K, NSH, CONVW_SH)
    dconvw = lax.dynamic_index_in_dim(dconvw, 2 * xi + yi, axis=1, keepdims=False)
    small_grads["conv_w"] = dconvw.reshape(1, CONV_K, CONVW_SH)

    upd = {}
    for names, tr in ((BIG[0:3], 176), (BIG[3:6], 176), (BIG[7:8], 256)):
        res = _adamw("adamw_" + names[0], [big[n] for n in names], [big_grads[n] for n in names],
                     [big["m_" + n] for n in names], [big["v_" + n] for n in names], tr, D)
        for n, r in zip(names, res):
            upd[n] = tuple(_unshard2d(t, n) for t in r)
    g_win = big_grads["w_in"].reshape(WIN_SH, 1, D)
    res, = _adamw("adamw_w_in", [_rows3d(w_in)], [g_win], [_rows3d(m_w_in)], [_rows3d(v_w_in)], WIN_SH // 4, D)
    upd["w_in"] = tuple(t.transpose(1, 2, 0) for t in res)
    (dl, m2, v2, _), = _adamw(
        "adamw_small", [_pack_small(given, "", conv_w[0])], [_pack_small(small_grads, "", dconvw)],
        [_pack_small(given, "m_", m_conv_w[0])], [_pack_small(given, "v_", v_conv_w[0])], SMALL_ROWS, D)
    dl, m2, v2 = (_unpack_small(t, given) for t in (dl, m2, v2))
    for n in SMALL_ORDER + ("conv_w",):
        upd[n] = (dl[n], m2[n], v2[n], small_grads[n])

    return (loss, grad_x[None], *[upd[n][3] for n in WEIGHTS], *[upd[n][0] for n in WEIGHTS],
            *[upd[n][1] for n in WEIGHTS], *[upd[n][2] for n in WEIGHTS])
```

```python
import functools
import typing

import jax
import jax.numpy as jnp
from jax import lax
from jax.experimental import pallas as pl
from jax.experimental.pallas import tpu as pltpu

F32 = jnp.float32
BF16 = jnp.bfloat16

S = 2048
D = 1024
FF = 2816
NSH = 4
FS = FF // NSH
HALF = D // 2
HD = 64
NKV = 4
NQ_PER_KV = 4
KVW = NKV * HD
QCOLS = NQ_PER_KV * HD
CONV_C = 1536
CONV_K = 4
SSM_W = 1024
NST = 128
NCH = S // 128
WIN_COLS = 4112
WIN_SH = WIN_COLS // NSH
WIN_PAD = 4224
COL_DT = 4096
EPS = 1e-6
NEG = -1e30

ADAM_LR = 0.001
ADAM_B1 = 0.9
ADAM_B2 = 0.999
ADAM_EPS = 1e-08
ADAM_WD = 0.01
ADAM_STEP = 10

VMEM_LIMIT = 56 * 1024 * 1024
TS = 512
TR = 256

NN = (((1,), (0,)), ((), ()))
NT = (((1,), (1,)), ((), ()))
TN = (((0,), (0,)), ((), ()))
MESH = pl.DeviceIdType.MESH


def _cparams(*sem):
    return pltpu.CompilerParams(dimension_semantics=sem, vmem_limit_bytes=VMEM_LIMIT)


def _dot(a, b, dims):
    return lax.dot_general(a.astype(BF16), b.astype(BF16), dims, preferred_element_type=F32)


def _bf16_pieces(v):
    hi = v.astype(BF16)
    rest = v - hi.astype(F32)
    mid = rest.astype(BF16)
    return hi, mid, (rest - mid.astype(F32)).astype(BF16)


def _dot_exact(a, b, ones="a"):
    if ones == "a":
        sel = a.astype(BF16)
        parts = [lax.dot_general(sel, p, NN, preferred_element_type=F32) for p in _bf16_pieces(b)]
    else:
        sel = b.astype(BF16)
        parts = [lax.dot_general(p, sel, NN, preferred_element_type=F32) for p in _bf16_pieces(a)]
    return (parts[2] + parts[1]) + parts[0]


def _sigmoid(v):
    return 1.0 / (1.0 + jnp.exp(-v))


class _Rider(typing.NamedTuple):
    operands: list
    out_shapes: list
    aliases: dict
    sems: list
    start: typing.Callable
    finish: typing.Callable


def _call(body, name, grid, in_specs, out_specs, out_shape, operands, scratch=(), sem=(), rider=None):
    multi = isinstance(out_shape, (list, tuple))
    if rider is None:
        return pl.pallas_call(
            body, name=name, grid=grid, in_specs=in_specs, out_specs=out_specs, out_shape=out_shape,
            scratch_shapes=list(scratch), compiler_params=_cparams(*sem))(*operands)
    outs = list(out_shape) if multi else [out_shape]
    ospecs = list(out_specs) if multi else [out_specs]
    n_in, n_out, n_scr = len(operands), len(outs), len(scratch)
    ri, ro = len(rider.operands), len(rider.out_shapes)

    def wrapped(*refs):
        o0 = n_in + ri
        s0 = o0 + n_out + ro
        rin, rout, rsem = refs[n_in:o0], refs[o0 + n_out:s0], refs[s0 + n_scr:]
        ids = [pl.program_id(a) for a in range(len(grid))]
        first = functools.reduce(jnp.logical_and, [i == 0 for i in ids])
        last = functools.reduce(jnp.logical_and, [i == g - 1 for i, g in zip(ids, grid)])

        @pl.when(first)
        def _():
            rider.start(rin, rout, rsem)

        body(*refs[:n_in], *refs[o0:o0 + n_out], *refs[s0:s0 + n_scr])

        @pl.when(last)
        def _():
            rider.finish(rin, rout, rsem)

    hbm = pl.BlockSpec(memory_space=pl.ANY)
    res = pl.pallas_call(
        wrapped, name=name, grid=grid, in_specs=list(in_specs) + [hbm] * ri, out_specs=ospecs + [hbm] * ro,
        out_shape=outs + list(rider.out_shapes), scratch_shapes=list(scratch) + list(rider.sems),
        input_output_aliases={n_in + k: n_out + v for k, v in rider.aliases.items()},
        compiler_params=_cparams(*(("arbitrary",) * len(grid))))(*operands, *rider.operands)
    main = list(res[:n_out])
    return (main if multi else main[0]), list(res[n_out:])


class _Tail(typing.NamedTuple):
    fn: typing.Callable
    operands: list
    in_specs: list


def _mm(name, operands, dims, grid, in_specs, o_spec, out_shape, rider=None, tail=None):
    npairs = len(operands) // 2
    extra = [] if tail is None else list(tail.operands)
    nin = 2 * npairs + len(extra)

    def body(*refs):
        t = None
        for i in range(npairs):
            a, b = refs[2 * i], refs[2 * i + 1]
            parts = [(a[s], b[s]) for s in range(a.shape[0])] if len(a.shape) == 3 else [(a[...], b[...])]
            for pa, pb in parts:
                d = _dot(pa, pb, dims)
                t = d if t is None else t + d
        if tail is None:
            refs[nin][...] = t.astype(refs[nin].dtype)
        else:
            tail.fn(t, refs[2 * npairs:nin], refs[nin:])

    sem = ("parallel" if tail is None else "arbitrary",) * len(grid)
    specs = list(in_specs) + ([] if tail is None else list(tail.in_specs))
    return _call(body, name, grid, specs, o_spec, out_shape, list(operands) + extra, (), sem, rider)


class _FfnW(typing.NamedTuple):
    gu: jax.Array
    g0: int
    dn: jax.Array
    d0: int


def _ffn_up(name, n, w, rider=None):
    def body(n_ref, wg_ref, wu_ref, fg_ref, fu_ref, a_ref):
        nb = n_ref[...]
        g = _dot(nb, wg_ref[...], NT)
        u = _dot(nb, wu_ref[...], NT)
        sg = _sigmoid(g)
        silu = g * sg
        fg_ref[...] = (u * (sg * (1.0 + g * (1.0 - sg)))).astype(BF16)
        fu_ref[...] = silu.astype(BF16)
        a_ref[...] = (silu * u).astype(BF16)

    out = jax.ShapeDtypeStruct((NSH, S, FS), BF16)
    ospec = pl.BlockSpec((None, TS, FS), lambda s, i: (s, i, 0))
    return _call(
        body, name, (NSH, S // TS),
        [pl.BlockSpec((TS, D), lambda s, i: (i, 0)),
         pl.BlockSpec((None, None, FS, D), lambda s, i: (s, w.g0, 0, 0)),
         pl.BlockSpec((None, None, FS, D), lambda s, i: (s, w.g0 + 1, 0, 0))],
        [ospec, ospec, ospec], [out, out, out], (n, w.gu, w.gu), sem=("parallel", "parallel"), rider=rider)


def _ffn_dact(name, dh, w, fgate, fup, rider=None):
    def body(dh_ref, wd_ref, fg_ref, fu_ref, dg_ref, du_ref):
        da = _dot(dh_ref[...], wd_ref[...], NT)
        dg_ref[...] = (da * fg_ref[...].astype(F32)).astype(BF16)
        du_ref[...] = (da * fu_ref[...].astype(F32)).astype(BF16)

    out = jax.ShapeDtypeStruct((NSH, S, FS), BF16)
    aspec = pl.BlockSpec((None, TS, FS), lambda s, i: (s, i, 0))
    return _call(
        body, name, (NSH, S // TS),
        [pl.BlockSpec((TS, D), lambda s, i: (i, 0)),
         pl.BlockSpec((None, None, FS, D), lambda s, i: (s, w.d0, 0, 0)), aspec, aspec],
        [aspec, aspec], [out, out], (dh, w.dn, fgate, fup), sem=("parallel", "parallel"), rider=rider)


def _rstd(v):
    return lax.rsqrt(jnp.mean(v * v, axis=-1, keepdims=True) + EPS)


def _row_spec():
    return pl.BlockSpec((TR, D), lambda i: (i, 0))


def _vec_spec():
    return pl.BlockSpec((1, D), lambda i: (0, 0))


def _acc_rows(ref, v):
    @pl.when(pl.program_id(0) == 0)
    def _():
        ref[...] = jnp.zeros_like(ref)
    ref[...] += jnp.sum(v, axis=0, keepdims=True)


def _prenorm(name, x, g):
    def body(x_ref, g_ref, n_ref):
        xv = x_ref[...]
        n_ref[...] = (xv * _rstd(xv) * g_ref[...]).astype(BF16)

    return pl.pallas_call(
        body, name=name, grid=(S // TR,), in_specs=[_row_spec(), _vec_spec()], out_specs=_row_spec(),
        out_shape=jax.ShapeDtypeStruct((S, D), BF16), compiler_params=_cparams("parallel"),
    )(x, g)


def _rows_spec(rows):
    return pl.BlockSpec((rows, D), lambda i: (i, 0))


def _rows_f32():
    return jax.ShapeDtypeStruct((S, D), F32)


def _rows_bf16():
    return jax.ShapeDtypeStruct((S, D), BF16)


def _vec_f32():
    return jax.ShapeDtypeStruct((1, D), F32)


def _tail_postres(rows, x, p, alpha, gnext):
    def fn(h, ins, outs):
        x_ref, p_ref, g_ref = ins
        h_ref, xo_ref, n_ref = outs
        h_ref[...] = h
        xo = x_ref[...] + alpha * (h * _rstd(h) * p_ref[...])
        xo_ref[...] = xo
        n_ref[...] = (xo * _rstd(xo) * g_ref[...]).astype(BF16)

    rs = _rows_spec(rows)
    return (_Tail(fn, [x, p, gnext], [rs, _vec_spec(), _vec_spec()]), [rs, rs, rs],
            [_rows_f32(), _rows_f32(), _rows_bf16()])


def _tail_final(rows, x, p, tgt, alpha):
    def fn(h, ins, outs):
        x_ref, p_ref, t_ref = ins
        dy_ref, dh_ref, dp_ref, loss_ref = outs
        r = _rstd(h)
        hn = h * r
        pv = p_ref[...]
        e = x_ref[...] + alpha * (hn * pv) - t_ref[...]
        dy = e * (1.0 / D)
        dy_ref[...] = dy
        du = alpha * dy * pv
        dh_ref[...] = (r * (du - hn * jnp.mean(du * hn, axis=-1, keepdims=True))).astype(BF16)
        _acc_rows(dp_ref, alpha * dy * hn)
        part = 0.5 * jnp.sum(jnp.mean(e * e, axis=-1, keepdims=True), axis=0, keepdims=True)
        _acc_rows(loss_ref, jnp.broadcast_to(part, (1, 128)))

    rs = _rows_spec(rows)
    return (_Tail(fn, [x, p, tgt], [rs, _vec_spec(), rs]),
            [rs, rs, _vec_spec(), pl.BlockSpec((1, 128), lambda i: (0, 0))],
            [_rows_f32(), _rows_bf16(), _vec_f32(), jax.ShapeDtypeStruct((1, 128), F32)])


def _norm_bwd(dn, xv, g_ref, dg_ref):
    r = _rstd(xv)
    xn = xv * r
    dng = dn * g_ref[...]
    _acc_rows(dg_ref, dn * xn)
    return r * (dng - xn * jnp.mean(dng * xn, axis=-1, keepdims=True))


def _tail_mid_bwd(rows, dres, x, g, h, p, alpha):
    def fn(dn, ins, outs):
        dr_ref, x_ref, g_ref, h_ref, p_ref = ins
        dx_ref, dh_ref, dg_ref, dp_ref = outs
        dx = dr_ref[...] + _norm_bwd(dn, x_ref[...], g_ref, dg_ref)
        dx_ref[...] = dx
        hv = h_ref[...]
        r = _rstd(hv)
        hn = hv * r
        du = alpha * dx * p_ref[...]
        dh_ref[...] = (r * (du - hn * jnp.mean(du * hn, axis=-1, keepdims=True))).astype(BF16)
        _acc_rows(dp_ref, alpha * dx * hn)

    rs = _rows_spec(rows)
    return (_Tail(fn, [dres, x, g, h, p], [rs, rs, _vec_spec(), rs, _vec_spec()]),
            [rs, rs, _vec_spec(), _vec_spec()], [_rows_f32(), _rows_bf16(), _vec_f32(), _vec_f32()])


def _tail_first_bwd(rows, dres, x, g):
    def fn(dn, ins, outs):
        dr_ref, x_ref, g_ref = ins
        dx_ref, dg_ref = outs
        dx_ref[...] = dr_ref[...] + _norm_bwd(dn, x_ref[...], g_ref, dg_ref)

    rs = _rows_spec(rows)
    return (_Tail(fn, [dres, x, g], [rs, rs, _vec_spec()]), [rs, _vec_spec()], [_rows_f32(), _vec_f32()])


def _rotate(t, c128, s128, sign, scale):
    width = t.shape[1]
    c = jnp.tile(c128, (1, width // 128))
    sn = jnp.tile(s128, (1, width // 128))
    lane = lax.broadcasted_iota(jnp.int32, t.shape, 1) & (HD - 1)
    rot = jnp.where(lane < HD // 2, -pltpu.roll(t, width - HD // 2, 1), pltpu.roll(t, HD // 2, 1))
    return (t * c + sign * (rot * sn)) * scale


def _rows_to_blocks(y):
    out = []
    for j in range(NKV):
        yt = y[:, QCOLS * j:QCOLS * (j + 1)].T
        out.append(jnp.concatenate([yt[HD * g:HD * (g + 1)] for g in range(NQ_PER_KV)], axis=1))
    return out


def _blocks_to_rows(blocks):
    cols = []
    for b in blocks:
        stacked = jnp.concatenate([b[:, 128 * g:128 * (g + 1)] for g in range(NQ_PER_KV)], axis=0)
        cols.append(stacked.T)
    return jnp.concatenate(cols, axis=1)


def _rope_q(proj, cos, sin):
    def body(t_ref, c_ref, s_ref, o_ref):
        y = _rotate(t_ref[...], c_ref[...], s_ref[...], 1.0, HD ** -0.5)
        for j, blk in enumerate(_rows_to_blocks(y)):
            o_ref[j] = blk.astype(BF16)

    return pl.pallas_call(
        body, name="rope_q", grid=(NCH,),
        in_specs=[pl.BlockSpec((128, D), lambda i: (i, 0)),
                  pl.BlockSpec((128, 128), lambda i: (i, 0)), pl.BlockSpec((128, 128), lambda i: (i, 0))],
        out_specs=pl.BlockSpec((NKV, None, HD, QROWS), lambda i: (0, i, 0, 0)),
        out_shape=jax.ShapeDtypeStruct((NKV, NCH, HD, QROWS), BF16), compiler_params=_cparams("parallel"),
    )(proj, cos, sin)


def _rope_dq(dqt, cos, sin, dproj):
    def body(t_ref, c_ref, s_ref, buf_ref, o_ref):
        t = _blocks_to_rows([t_ref[j] for j in range(NKV)])
        o_ref[...] = _rotate(t, c_ref[...], s_ref[...], -1.0, HD ** -0.5).astype(BF16)

    return pl.pallas_call(
        body, name="rope_dq", grid=(NCH,),
        in_specs=[pl.BlockSpec((NKV, None, HD, QROWS), lambda i: (0, i, 0, 0)),
                  pl.BlockSpec((128, 128), lambda i: (i, 0)), pl.BlockSpec((128, 128), lambda i: (i, 0)),
                  pl.BlockSpec(memory_space=pl.ANY)],
        out_specs=pl.BlockSpec((128, D), lambda i: (i, 0)),
        out_shape=jax.ShapeDtypeStruct(dproj.shape, BF16), input_output_aliases={3: 0},
        compiler_params=_cparams("parallel"),
    )(dqt, cos, sin, dproj)


def _rope_dkv(dkt, dvt, cos, sin, dproj):
    def body(k_ref, v_ref, c_ref, s_ref, buf_ref, o_ref):
        dk = jnp.concatenate([k_ref[j] for j in range(NKV)], axis=0).T
        dv = jnp.concatenate([v_ref[j] for j in range(NKV)], axis=0).T
        dk = _rotate(dk, c_ref[...], s_ref[...], -1.0, 1.0)
        o_ref[...] = jnp.concatenate([dk, dv], axis=1).astype(BF16)

    tspec = pl.BlockSpec((NKV, HD, 128), lambda i: (0, 0, i))
    return pl.pallas_call(
        body, name="rope_dkv", grid=(NCH,),
        in_specs=[tspec, tspec, pl.BlockSpec((128, 128), lambda i: (i, 0)), pl.BlockSpec((128, 128), lambda i: (i, 0)),
                  pl.BlockSpec(memory_space=pl.ANY)],
        out_specs=pl.BlockSpec((128, 2 * KVW), lambda i: (i, D // (2 * KVW))),
        out_shape=jax.ShapeDtypeStruct(dproj.shape, BF16), input_output_aliases={4: 0},
        compiler_params=_cparams("parallel"),
    )(dkt, dvt, cos, sin, dproj)


def _rope_kv(proj, cos, sin):
    def body(t_ref, c_ref, s_ref, k_ref, v_ref, kt_ref, vt_ref):
        t = t_ref[...]
        k = _rotate(t[:, :KVW], c_ref[...], s_ref[...], 1.0, 1.0).astype(BF16)
        v = t[:, KVW:].astype(BF16)
        k_ref[...] = k
        v_ref[...] = v
        kt, vt = k.astype(F32).T, v.astype(F32).T
        for j in range(NKV):
            kt_ref[j] = kt[HD * j:HD * (j + 1)].astype(BF16)
            vt_ref[j] = vt[HD * j:HD * (j + 1)].astype(BF16)

    rows = pl.BlockSpec((128, KVW), lambda i: (i, 0))
    tspec = pl.BlockSpec((NKV, HD, 128), lambda i: (0, 0, i))
    return pl.pallas_call(
        body, name="rope_kv", grid=(NCH,),
        in_specs=[pl.BlockSpec((128, 2 * KVW), lambda i: (i, D // (2 * KVW))),
                  pl.BlockSpec((128, 128), lambda i: (i, 0)), pl.BlockSpec((128, 128), lambda i: (i, 0))],
        out_specs=[rows, rows, tspec, tspec],
        out_shape=[jax.ShapeDtypeStruct((S, KVW), BF16)] * 2 + [jax.ShapeDtypeStruct((NKV, HD, S), BF16)] * 2,
        compiler_params=_cparams("parallel"),
    )(proj, cos, sin)


QROWS = NQ_PER_KV * 128


NBIAS = NCH + 1
KV_PER_STEP = 4


def _bias_table():
    db = lax.broadcasted_iota(jnp.int32, (NBIAS, 128, QROWS), 0) - 1
    ki = lax.broadcasted_iota(jnp.int32, (NBIAS, 128, QROWS), 1)
    qi = lax.broadcasted_iota(jnp.int32, (NBIAS, 128, QROWS), 2) & 127
    d = db * 128 + qi - ki
    cnt = ((d <= 128).astype(F32) + (((d & 3) == 0) & (d <= 512)).astype(F32) + ((d & 15) == 0).astype(F32))
    return jnp.where((d >= 0) & (cnt > 0.0), jnp.log(jnp.maximum(cnt, 1.0)), NEG)


def _qt_spec():
    return pl.BlockSpec((None, None, HD, QROWS), lambda j, i: (j, i, 0, 0))


def _stat_spec():
    return pl.BlockSpec((None, None, 1, QROWS), lambda j, i: (j, i, 0, 0))


def _attn_fwd(qt, kh, vt, bias, rider=None):
    def body(q_ref, k_ref, v_ref, b_ref, o_ref, lse_ref, rows_ref, m_ref, l_ref, acc_ref):
        qb = pl.program_id(1)
        m_ref[...] = jnp.full_like(m_ref, NEG)
        l_ref[...] = jnp.zeros_like(l_ref)
        acc_ref[...] = jnp.zeros_like(acc_ref)

        def keys(off, size, bias_):
            for h in range(KV_PER_STEP):
                m = m_ref[h]
                s = _dot(k_ref[h, pl.ds(off, size), :], q_ref[h], NN) + bias_
                m_new = jnp.maximum(m, jnp.max(s, axis=0, keepdims=True))
                p = jnp.exp(s - m_new)
                a = jnp.exp(m - m_new)
                m_ref[h] = m_new
                l_ref[h] = a * l_ref[h] + jnp.sum(p, axis=0, keepdims=True)
                acc_ref[h] = a * acc_ref[h] + _dot(v_ref[h, :, pl.ds(off, size)], p, NN)

        @pl.loop(0, (qb + 1) // 2)
        def _(i):
            bias2 = jnp.concatenate([b_ref[qb - 2 * i + 1], b_ref[qb - 2 * i]], axis=0)
            keys(pl.multiple_of(i * 256, 256), 256, bias2)

        @pl.when(qb % 2 == 0)
        def _():
            keys(pl.multiple_of(qb * 128, 128), 128, b_ref[1])

        outs = []
        for h in range(KV_PER_STEP):
            outs.append(acc_ref[h] / l_ref[h])
            o_ref[h] = outs[h]
            lse_ref[h] = m_ref[h] + jnp.log(l_ref[h])
        rows_ref[...] = _blocks_to_rows(outs).astype(BF16)

    kvs = KV_PER_STEP
    qspec = pl.BlockSpec((kvs, None, HD, QROWS), lambda j, i: (j, i, 0, 0))
    return _call(
        body, "attn_fwd", (NKV // kvs, NCH),
        [qspec, pl.BlockSpec((kvs, S, HD), lambda j, i: (j, 0, 0)),
         pl.BlockSpec((kvs, HD, S), lambda j, i: (j, 0, 0)),
         pl.BlockSpec((NBIAS, 128, QROWS), lambda j, i: (0, 0, 0))],
        [qspec, pl.BlockSpec((kvs, None, 1, QROWS), lambda j, i: (j, i, 0, 0)),
         pl.BlockSpec((128, QCOLS * kvs), lambda j, i: (i, j))],
        [jax.ShapeDtypeStruct((NKV, NCH, HD, QROWS), F32), jax.ShapeDtypeStruct((NKV, NCH, 1, QROWS), F32),
         jax.ShapeDtypeStruct((S, D), BF16)],
        (qt, kh, vt, bias),
        [pltpu.VMEM((kvs, 1, QROWS), F32), pltpu.VMEM((kvs, 1, QROWS), F32), pltpu.VMEM((kvs, HD, QROWS), F32)],
        ("parallel", "parallel"), rider)


def _attn_delta(ot, dot_):
    def body(o_ref, do_ref, dl_ref):
        dl_ref[...] = jnp.sum(o_ref[...] * do_ref[...].astype(F32), axis=1, keepdims=True)

    spec = pl.BlockSpec((None, NCH, HD, QROWS), lambda j: (j, 0, 0, 0))
    return pl.pallas_call(
        body, name="attn_delta", grid=(NKV,), in_specs=[spec, spec],
        out_specs=pl.BlockSpec((None, NCH, 1, QROWS), lambda j: (j, 0, 0, 0)),
        out_shape=jax.ShapeDtypeStruct((NKV, NCH, 1, QROWS), F32), compiler_params=_cparams("parallel"),
    )(ot, dot_)


def _attn_bwd(qt, kh, kt, vh, dot_, lse, delta, bias, rider=None):
    def body(qt_ref, k_ref, kt_ref, v_ref, dot_ref, lse_ref, dl_ref, b_ref, dq_ref, dk_ref, dv_ref):
        kb = pl.program_id(1)

        @pl.when(kb == 0)
        def _():
            dq_ref[...] = jnp.zeros_like(dq_ref)

        def blocks(carry, qbs):
            out = list(carry)
            for h in range(KV_PER_STEP):
                k, kt_, v = k_ref[h], kt_ref[h], v_ref[h]
                for qb in qbs:
                    st = _dot(k, qt_ref[h, qb], NN) + b_ref[qb - kb + 1]
                    pt = jnp.exp(st - lse_ref[h, qb])
                    dst = pt * (_dot(v, dot_ref[h, qb], NN) - dl_ref[h, qb])
                    dq_ref[h, qb] += _dot(kt_, dst, NN)
                    out[2 * h] = out[2 * h] + _dot(qt_ref[h, qb], dst, NT)
                    out[2 * h + 1] = out[2 * h + 1] + _dot(dot_ref[h, qb], pt, NT)
            return tuple(out)

        res = (jnp.zeros((HD, 128), F32),) * (2 * KV_PER_STEP)
        res = lax.cond(kb % 2 == 1, lambda c: blocks(c, (kb,)), lambda c: c, res)
        res = lax.fori_loop((kb + 1) // 2, NCH // 2, lambda j, c: blocks(c, (2 * j, 2 * j + 1)), res)
        for h in range(KV_PER_STEP):
            dk_ref[h] = res[2 * h]
            dv_ref[h] = res[2 * h + 1]

    kvs = KV_PER_STEP
    tspec = pl.BlockSpec((kvs, NCH, HD, QROWS), lambda j, i: (j, 0, 0, 0))
    kspec = pl.BlockSpec((kvs, 128, HD), lambda j, i: (j, i, 0))
    ktspec = pl.BlockSpec((kvs, HD, 128), lambda j, i: (j, 0, i))
    sspec = pl.BlockSpec((kvs, NCH, 1, QROWS), lambda j, i: (j, 0, 0, 0))
    return _call(
        body, "attn_bwd", (NKV // kvs, NCH),
        [tspec, kspec, ktspec, kspec, tspec, sspec, sspec,
         pl.BlockSpec((NBIAS, 128, QROWS), lambda j, i: (0, 0, 0))],
        [tspec, ktspec, ktspec],
        [jax.ShapeDtypeStruct((NKV, NCH, HD, QROWS), F32),
         jax.ShapeDtypeStruct((NKV, HD, S), F32), jax.ShapeDtypeStruct((NKV, HD, S), F32)],
        (qt, kh, kt, vh, dot_, lse, delta, bias), sem=("parallel", "arbitrary"), rider=rider)


CONV_BLK = 256
CONV_COL0 = 1536 // CONV_BLK


def _shift_down(u, j, row):
    return jnp.where(row >= j, pltpu.roll(u, j, 0), 0.0)


def _conv_pre(u, w_ref, b_ref, row):
    y = b_ref[...] + w_ref[CONV_K - 1:CONV_K, :] * u
    for j in range(1, CONV_K):
        y = y + w_ref[CONV_K - 1 - j:CONV_K - j, :] * _shift_down(u, j, row)
    return y


def _conv_fwd(proj, convw, convb):
    def body(u_ref, w_ref, b_ref, o_ref):
        u = u_ref[...]
        row = lax.broadcasted_iota(jnp.int32, u.shape, 0)
        y = _conv_pre(u, w_ref, b_ref, row)
        o_ref[...] = y * _sigmoid(y)

    return pl.pallas_call(
        body, name="conv_fwd", grid=(CONV_C // CONV_BLK,),
        in_specs=[pl.BlockSpec((S, CONV_BLK), lambda i: (0, CONV_COL0 + i)),
                  pl.BlockSpec((CONV_K, CONV_BLK), lambda i: (0, i)),
                  pl.BlockSpec((1, CONV_BLK), lambda i: (0, i))],
        out_specs=pl.BlockSpec((S, CONV_BLK), lambda i: (0, i)),
        out_shape=jax.ShapeDtypeStruct((S, CONV_C), F32), compiler_params=_cparams("parallel"),
    )(proj, convw, convb)


def _conv_bwd(dact, proj, convw, convb, dproj):
    def body(da_ref, u_ref, w_ref, b_ref, buf_ref, du_ref, dw_ref, db_ref):
        u = u_ref[...]
        row = lax.broadcasted_iota(jnp.int32, u.shape, 0)
        y = _conv_pre(u, w_ref, b_ref, row)
        sg = _sigmoid(y)
        dy = da_ref[...] * (sg * (1.0 + y * (1.0 - sg)))
        db_ref[...] = jnp.sum(dy, axis=0, keepdims=True)
        du = w_ref[CONV_K - 1:CONV_K, :] * dy
        r8 = lax.broadcasted_iota(jnp.int32, (8, CONV_BLK), 0)
        dw = jnp.where(r8 == CONV_K - 1, jnp.sum(dy * u, axis=0, keepdims=True), 0.0)
        for j in range(1, CONV_K):
            du = du + w_ref[CONV_K - 1 - j:CONV_K - j, :] * jnp.where(row < S - j, pltpu.roll(dy, S - j, 0), 0.0)
            dw = dw + jnp.where(r8 == CONV_K - 1 - j,
                                jnp.sum(dy * _shift_down(u, j, row), axis=0, keepdims=True), 0.0)
        du_ref[...] = du.astype(BF16)
        dw_ref[...] = dw

    return pl.pallas_call(
        body, name="conv_bwd", grid=(CONV_C // CONV_BLK,),
        in_specs=[pl.BlockSpec((S, CONV_BLK), lambda i: (0, i)),
                  pl.BlockSpec((S, CONV_BLK), lambda i: (0, CONV_COL0 + i)),
                  pl.BlockSpec((CONV_K, CONV_BLK), lambda i: (0, i)),
                  pl.BlockSpec((1, CONV_BLK), lambda i: (0, i)), pl.BlockSpec(memory_space=pl.ANY)],
        out_specs=[pl.BlockSpec((S, CONV_BLK), lambda i: (0, CONV_COL0 + i)),
                   pl.BlockSpec((8, CONV_BLK), lambda i: (0, i)), pl.BlockSpec((1, CONV_BLK), lambda i: (0, i))],
        out_shape=[jax.ShapeDtypeStruct(dproj.shape, BF16), jax.ShapeDtypeStruct((8, CONV_C), F32),
                   jax.ShapeDtypeStruct((1, CONV_C), F32)],
        input_output_aliases={4: 0}, compiler_params=_cparams("parallel"),
    )(dact, proj, convw, convb, dproj)


NPAIR = 8


def _ssd_scalars(dtr_ref, dtb_ref, alog_ref):
    z = dtr_ref[...] + dtb_ref[...]
    dt = jnp.maximum(z, 0.0) + jnp.log(1.0 + jnp.exp(-jnp.abs(z)))
    a = -jnp.exp(alog_ref[...])
    r = lax.broadcasted_iota(jnp.int32, (128, 128), 0)
    c = lax.broadcasted_iota(jnp.int32, (128, 128), 1)
    tri = (r >= c).astype(F32)
    cs = _dot_exact(tri, dt * a)
    return z, dt, a, cs, r, c


def _by_lane(cs, dt):
    head = lax.broadcasted_iota(jnp.int32, (128, SSM_W), 0)
    lane = lax.broadcasted_iota(jnp.int32, (128, SSM_W), 1)
    sel = (head == lane // HD).astype(F32)
    cs_l = _dot_exact(cs, sel, "b")
    last_l = cs_l[127:128, :]
    return sel, jnp.exp(cs_l), jnp.exp(last_l - cs_l), _dot_exact(dt, sel, "b")


def _pair_terms(cs, h1, h2):
    return (cs[:, h1:h1 + 1], cs[:, h2:h2 + 1],
            jnp.exp(cs[127:128, h1:h1 + 1]), jnp.exp(cs[127:128, h2:h2 + 1]))


def _gate_norm(y, zv, w):
    yg = y * (zv * _sigmoid(zv))
    outs, rs = [], []
    for g in range(2):
        blk = yg[:, 512 * g:512 * (g + 1)]
        r = lax.rsqrt(jnp.mean(blk * blk, axis=-1, keepdims=True) + EPS)
        outs.append(blk * r)
        rs.append(r)
    return jnp.concatenate(outs, axis=1), rs, yg


def _ssd_fwd(xbc, proj, dtb, alog, dskip_l, ssmw):
    def body(x_ref, b_ref, c_ref, dtr_ref, z_ref, dtb_ref, alog_ref, dsk_ref, w_ref, y_ref, yn_ref, hp_ref, h_ref):
        @pl.when(pl.program_id(0) == 0)
        def _():
            h_ref[...] = jnp.zeros_like(h_ref)

        _, dt, _, cs, r, c = _ssd_scalars(dtr_ref, dtb_ref, alog_ref)
        cst = cs.T
        causal = r >= c
        lo = c < HD
        _, e_all, dte_all, dt_all = _by_lane(cs, dt)
        hp_ref[...] = h_ref[...]
        for g in range(2):
            bg = b_ref[:, 128 * g:128 * (g + 1)]
            cg = c_ref[:, 128 * g:128 * (g + 1)]
            cb = _dot(cg, bg, NT)
            for j in range(4):
                pj = 4 * g + j
                h1, h2 = 2 * pj, 2 * pj + 1
                sl = slice(128 * pj, 128 * (pj + 1))
                xp = x_ref[:, sl]
                c1, c2, cd1, cd2 = _pair_terms(cs, h1, h2)
                e_l, dte_l = e_all[:, sl], dte_all[:, sl]
                xdt = xp * dt_all[:, sl]
                m1 = cb * jnp.exp(jnp.where(causal, c1 - cst[h1:h1 + 1, :], NEG))
                m2 = cb * jnp.exp(jnp.where(causal, c2 - cst[h2:h2 + 1, :], NEG))
                yd = jnp.where(lo, _dot(m1, xdt, NN), _dot(m2, xdt, NN))
                hp = h_ref[pj]
                yo = _dot(cg, hp, NT) * e_l
                st = _dot(xdt * dte_l, bg, TN)
                h_ref[pj] = hp * jnp.where(r < HD, cd1, cd2) + st
                y_ref[:, sl] = yd + yo + dsk_ref[:, sl] * xp
        yn, _, _ = _gate_norm(y_ref[...], z_ref[...], w_ref[...])
        yn_ref[...] = (yn * w_ref[...]).astype(BF16)

    return pl.pallas_call(
        body, name="ssd_fwd", grid=(NCH,),
        in_specs=[pl.BlockSpec((128, SSM_W), lambda i: (i, 0)),
                  pl.BlockSpec((128, 256), lambda i: (i, 4)), pl.BlockSpec((128, 256), lambda i: (i, 5)),
                  pl.BlockSpec((128, 128), lambda i: (i, COL_DT // 128)),
                  pl.BlockSpec((128, SSM_W), lambda i: (i, 3)),
                  pl.BlockSpec((1, 128), lambda i: (0, 0)), pl.BlockSpec((1, 128), lambda i: (0, 0)),
                  pl.BlockSpec((1, SSM_W), lambda i: (0, 0)), pl.BlockSpec((1, SSM_W), lambda i: (0, 0))],
        out_specs=[pl.BlockSpec((128, SSM_W), lambda i: (i, 0)), pl.BlockSpec((128, SSM_W), lambda i: (i, 0)),
                   pl.BlockSpec((None, NPAIR, 128, 128), lambda i: (i, 0, 0, 0))],
        out_shape=[jax.ShapeDtypeStruct((S, SSM_W), F32), jax.ShapeDtypeStruct((S, SSM_W), BF16),
                   jax.ShapeDtypeStruct((NCH, NPAIR, 128, 128), F32)],
        scratch_shapes=[pltpu.VMEM((NPAIR, 128, 128), F32)],
        compiler_params=_cparams("arbitrary"),
    )(xbc, xbc, xbc, proj, proj, dtb, alog, dskip_l, ssmw)


def _ssd_bwd(dmixed, y, xbc, proj, hprev, dtb, alog, dskip_l, ssmw, rider=None):
    def body(dyn_ref, y_ref, x_ref, b_ref, c_ref, dtr_ref, z_ref, hp_ref, dtb_ref, alog_ref, dsk_ref, w_ref,
             dxbc_ref, dz_ref, ddt_ref, dw_ref, dsc_ref, g_ref):
        @pl.when(pl.program_id(0) == 0)
        def _():
            g_ref[...] = jnp.zeros_like(g_ref)
            dsc_ref[...] = jnp.zeros_like(dsc_ref)

        z, dt, a, cs, r, c = _ssd_scalars(dtr_ref, dtb_ref, alog_ref)
        cst = cs.T
        causal = r >= c
        lo = c < HD

        yv = y_ref[...]
        zv = z_ref[...]
        wv = w_ref[...]
        ygn, rs, yg = _gate_norm(yv, zv, wv)
        dyn = dyn_ref[...]
        _acc_rows(dw_ref, dyn * ygn)
        dynw = dyn * wv
        parts = []
        for g in range(2):
            sl = slice(512 * g, 512 * (g + 1))
            a_g, n_g = dynw[:, sl], ygn[:, sl]
            parts.append(rs[g] * (a_g - n_g * jnp.mean(a_g * n_g, axis=-1, keepdims=True)))
        dyg = jnp.concatenate(parts, axis=1)
        sz = _sigmoid(zv)
        dz_ref[...] = (dyg * yv * (sz * (1.0 + zv * (1.0 - sz)))).astype(BF16)
        dy_all = dyg * (zv * sz)

        dcs_cols = jnp.zeros((128, 128), F32)
        dcs_rows = jnp.zeros((128, 128), F32)
        sel, e_all, dte_all, dt_all = _by_lane(cs, dt)
        x_all, b_all, c_all, dsk_all = x_ref[...], b_ref[...], c_ref[...], dsk_ref[...]
        hp_all, g_all = hp_ref[...], g_ref[...]
        g_new, dx_parts, db_parts, dc_parts = [], [], [], []
        dyx_parts, ryo_parts, qx_parts, dxx_parts, gh_parts = [], [], [], [], []
        for g in range(2):
            bg = b_all[:, 128 * g:128 * (g + 1)]
            cg = c_all[:, 128 * g:128 * (g + 1)]
            cb = _dot(cg, bg, NT)
            dcb = jnp.zeros((128, 128), F32)
            db_acc = jnp.zeros((128, NST), F32)
            dc_acc = jnp.zeros((128, NST), F32)
            for j in range(4):
                pj = 4 * g + j
                h1, h2 = 2 * pj, 2 * pj + 1
                sl = slice(128 * pj, 128 * (pj + 1))
                xp = x_all[:, sl]
                dyp = dy_all[:, sl]
                c1, c2, cd1, cd2 = _pair_terms(cs, h1, h2)
                e_l, dte_l, dt_l = e_all[:, sl], dte_all[:, sl], dt_all[:, sl]
                xdt = xp * dt_l
                hp = hp_all[pj]
                gp = g_all[pj]
                dyx_parts.append(dyp * xp)
                dzs = dyp * e_l
                dc_acc = dc_acc + _dot(dzs, hp, NN)
                ryo_parts.append(dyp * (_dot(cg, hp, NT) * e_l))
                qm = _dot(bg, gp, NT)
                dxdt = qm * dte_l
                qx_parts.append(qm * xdt)
                db_acc = db_acc + _dot(xdt * dte_l, gp, NN)
                gh_parts.append(gp * hp)
                g_new.append(_dot(dzs, cg, TN) + jnp.where(r < HD, cd1, cd2) * gp)
                for hh, ch, msk in ((h1, c1, lo), (h2, c2, jnp.logical_not(lo))):
                    lm = jnp.exp(jnp.where(causal, ch - cst[hh:hh + 1, :], NEG))
                    mm = cb * lm
                    dm = jnp.where(causal, _dot(jnp.where(msk, dyp, 0.0), xdt, NT), 0.0)
                    w = dm * mm
                    dcs_cols = dcs_cols + jnp.where(c == hh, jnp.sum(w, axis=1, keepdims=True), 0.0)
                    dcs_rows = dcs_rows + jnp.where(r == hh, jnp.sum(w, axis=0, keepdims=True), 0.0)
                    dcb = dcb + dm * lm
                    dxdt = dxdt + jnp.where(msk, _dot(mm, dyp, TN), 0.0)
                dxx_parts.append(dxdt * xp)
                dx_parts.append(dsk_all[:, sl] * dyp + dxdt * dt_l)
            db_parts.append(db_acc + _dot(dcb, cg, TN))
            dc_parts.append(dc_acc + _dot(dcb, bg, NN))
        g_ref[...] = jnp.stack(g_new)
        dxbc_ref[...] = jnp.concatenate(dx_parts + db_parts + dc_parts, axis=1)

        selt = (lax.broadcasted_iota(jnp.int32, (SSM_W, 128), 0) // HD
                == lax.broadcasted_iota(jnp.int32, (SSM_W, 128), 1)).astype(F32)

        def by_head(parts):
            return _dot_exact(jnp.concatenate(parts, axis=1), selt, "b")

        ddt_x = by_head(dxx_parts)
        dd_row = jnp.sum(by_head(dyx_parts), axis=0, keepdims=True)
        t_all = by_head(qx_parts) * jnp.exp(cs[127:128, :] - cs)
        gh = jnp.sum(_dot_exact(sel, jnp.concatenate(gh_parts, axis=0)), axis=1, keepdims=True)
        gh_row = jnp.broadcast_to(gh, (128, 128)).T[0:1, :]
        at_end = jnp.sum(t_all, axis=0, keepdims=True) + gh_row * jnp.exp(cs[127:128, :])
        dcs = by_head(ryo_parts) - t_all + dcs_cols + jnp.where(r == 127, at_end, 0.0) - dcs_rows.T
        dad = _dot_exact((c >= r).astype(F32), dcs)
        ddt = dad * a + ddt_x
        ddtr = jnp.where(c < 16, ddt * _sigmoid(z), 0.0)
        ddt_ref[...] = ddtr.astype(BF16)
        r8 = lax.broadcasted_iota(jnp.int32, (8, 128), 0)
        dsc_ref[...] += (jnp.where(r8 == 0, jnp.sum(ddtr, axis=0, keepdims=True), 0.0)
                         + jnp.where(r8 == 1, jnp.sum(dad * dt, axis=0, keepdims=True) * a, 0.0)
                         + jnp.where(r8 == 2, dd_row, 0.0))

    rev = NCH - 1
    return _call(
        body, "ssd_bwd", (NCH,),
        [pl.BlockSpec((128, SSM_W), lambda i: (rev - i, 0)),
         pl.BlockSpec((128, SSM_W), lambda i: (rev - i, 0)),
         pl.BlockSpec((128, SSM_W), lambda i: (rev - i, 0)),
         pl.BlockSpec((128, 256), lambda i: (rev - i, 4)), pl.BlockSpec((128, 256), lambda i: (rev - i, 5)),
         pl.BlockSpec((128, 128), lambda i: (rev - i, COL_DT // 128)),
         pl.BlockSpec((128, SSM_W), lambda i: (rev - i, 3)),
         pl.BlockSpec((None, NPAIR, 128, 128), lambda i: (rev - i, 0, 0, 0)),
         pl.BlockSpec((1, 128), lambda i: (0, 0)), pl.BlockSpec((1, 128), lambda i: (0, 0)),
         pl.BlockSpec((1, SSM_W), lambda i: (0, 0)), pl.BlockSpec((1, SSM_W), lambda i: (0, 0))],
        [pl.BlockSpec((128, CONV_C), lambda i: (rev - i, 0)),
         pl.BlockSpec((128, SSM_W), lambda i: (rev - i, 3)),
         pl.BlockSpec((128, 128), lambda i: (rev - i, 0)),
         pl.BlockSpec((1, SSM_W), lambda i: (0, 0)), pl.BlockSpec((8, 128), lambda i: (0, 0))],
        [jax.ShapeDtypeStruct((S, CONV_C), F32), jax.ShapeDtypeStruct((S, WIN_PAD), BF16),
         jax.ShapeDtypeStruct((S, 128), BF16), jax.ShapeDtypeStruct((1, SSM_W), F32),
         jax.ShapeDtypeStruct((8, 128), F32)],
        (dmixed, y, xbc, xbc, xbc, proj, proj, hprev, dtb, alog, dskip_l, ssmw),
        [pltpu.VMEM((NPAIR, 128, 128), F32)], ("arbitrary",), rider)


def _cast_stack(name, slot, arrs, tr, tc):
    n = len(arrs)
    rows, cols = arrs[0].shape

    def body(s_ref, *refs):
        for i in range(n):
            refs[n][i] = refs[i][...].astype(BF16)

    return pl.pallas_call(
        body, name=name,
        grid_spec=pltpu.PrefetchScalarGridSpec(
            num_scalar_prefetch=1, grid=(rows // tr, cols // tc),
            in_specs=[pl.BlockSpec((tr, tc), lambda i, j, sr: (i, j))] * n,
            out_specs=pl.BlockSpec((None, n, tr, tc), lambda i, j, sr: (sr[0], 0, i, j))),
        out_shape=jax.ShapeDtypeStruct((NSH, n, rows, cols), BF16),
        compiler_params=_cparams("parallel", "parallel"),
    )(slot, *arrs)


def _pair_sum(name, c_idx, ps, th):
    n = len(ps)
    _, rows, _ = ps[0].shape

    def body(c_ref, *refs):
        mine, whole, out, theirs = refs[:n], refs[n:2 * n], refs[2 * n:3 * n], refs[3 * n:4 * n]
        send, recv = refs[4 * n], refs[4 * n + 1]
        s, i = pl.program_id(0), pl.program_id(1)
        x, y, c, _ = _place()

        def copies(slot):
            return [_rcopy(whole[k].at[slot, :, pl.ds((1 - c) * HALF, HALF)], theirs[k].at[slot],
                           send.at[slot * n + k], recv.at[slot * n + k], (x, y, 1 - c)) for k in range(n)]

        @pl.when((s == 0) & (i == 0))
        def _():
            for slot in range(NSH):
                for cp in copies(slot):
                    cp.start()

        @pl.when(i == 0)
        def _():
            for slot in range(NSH):
                @pl.when(s == slot)
                def _():
                    for cp in copies(slot):
                        cp.wait()

        rows_i = slice(None) if th == rows else pl.ds(pl.multiple_of(i * th, th), th)
        for k in range(n):
            out[k][...] = (mine[k][...].astype(F32) + theirs[k][s, rows_i, :].astype(F32)).astype(BF16)

    spec = pl.BlockSpec((None, th, HALF), lambda s, i, cr: (s, i, 0))
    return pl.pallas_call(
        body, name=name,
        grid_spec=pltpu.PrefetchScalarGridSpec(
            num_scalar_prefetch=1, grid=(NSH, rows // th),
            in_specs=[pl.BlockSpec((None, th, HALF), lambda s, i, cr: (s, i, cr[0]))] * n + _any_specs(n),
            out_specs=[spec] * n,
            scratch_shapes=[pltpu.VMEM((NSH, rows, HALF), BF16)] * n
            + [pltpu.SemaphoreType.DMA((NSH * n,)), pltpu.SemaphoreType.DMA((NSH * n,))]),
        out_shape=[jax.ShapeDtypeStruct((NSH, rows, HALF), BF16)] * n,
        compiler_params=_cparams("arbitrary", "arbitrary"),
    )(c_idx, *ps, *ps)


def _chip_sum(name, place, cs, ts, th):
    n = len(ts)
    _, rows, _ = ts[0].shape

    def body(p_ref, *refs):
        for i in range(n):
            t = refs[n + i][...].astype(F32)
            refs[2 * n + i][...] = ((refs[i][...].astype(F32) + t[0]) + t[1]) + t[2]

    return pl.pallas_call(
        body, name=name,
        grid_spec=pltpu.PrefetchScalarGridSpec(
            num_scalar_prefetch=1, grid=(rows // th,),
            in_specs=[pl.BlockSpec((None, th, HALF), lambda i, pr: (pr[0], i, 0))] * n
            + [pl.BlockSpec((3, th, HALF), lambda i, pr: (0, i, 0))] * n,
            out_specs=[pl.BlockSpec((th, HALF), lambda i, pr: (i, pr[1]))] * n),
        out_shape=[jax.ShapeDtypeStruct((rows, D), F32)] * n, compiler_params=_cparams("parallel"),
    )(place, *cs, *ts)


def _adamw(name, ws, gs, ms, vs, tr, tc):
    n = len(ws)
    shape = ws[0].shape
    rows, cols, mid = shape[0], shape[-1], shape[1:-1]
    c1 = 1.0 / (1.0 - ADAM_B1 ** ADAM_STEP)
    c2 = 1.0 / (1.0 - ADAM_B2 ** ADAM_STEP)

    def body(*refs):
        for i in range(n):
            w, g, m, v = (refs[k * n + i][...] for k in range(4))
            m2 = ADAM_B1 * m + (1.0 - ADAM_B1) * g
            v2 = ADAM_B2 * v + (1.0 - ADAM_B2) * (g * g)
            refs[4 * n + 4 * i][...] = -ADAM_LR * ((m2 * c1) / (jnp.sqrt(v2 * c2) + ADAM_EPS) + ADAM_WD * w)
            refs[4 * n + 4 * i + 1][...] = m2
            refs[4 * n + 4 * i + 2][...] = v2
            refs[4 * n + 4 * i + 3][...] = g

    spec = pl.BlockSpec((tr,) + mid + (tc,), lambda i, j: (i,) + (0,) * len(mid) + (j,))
    outs = pl.pallas_call(
        body, name=name, grid=(rows // tr, cols // tc), in_specs=[spec] * (4 * n), out_specs=[spec] * (4 * n),
        out_shape=[jax.ShapeDtypeStruct(shape, F32)] * (4 * n),
        compiler_params=_cparams("parallel", "parallel"),
    )(*ws, *gs, *ms, *vs)
    return [tuple(outs[4 * i:4 * i + 4]) for i in range(n)]


def _place():
    x, y, c = lax.axis_index("x"), lax.axis_index("y"), lax.axis_index("c")
    chips = [(1 - x, y), (x, 1 - y), (1 - x, 1 - y)]
    return x, y, c, chips


def _any_specs(n):
    return [pl.BlockSpec(memory_space=pl.ANY)] * n


def _rcopy(src, dst, send_sem, recv_sem, dev):
    return pltpu.make_async_remote_copy(src_ref=src, dst_ref=dst, send_sem=send_sem, recv_sem=recv_sem,
                                        device_id=dev, device_id_type=MESH)


def _gather_rider(bufs, views):
    n = len(bufs)

    def start(rin, rout, sems):
        send, recv = sems[0], sems[1]
        x, y, c, chips = _place()
        for j, chip in enumerate(chips):
            for b in range(n):
                mine = views[b](rout[b], 2 * x + y, c)
                _rcopy(mine, mine, send.at[j * n + b], recv.at[j * n + b], (chip[0], chip[1], c)).start()

    def finish(rin, rout, sems):
        send, recv, fsend, frecv = sems
        x, y, c, chips = _place()
        passed = []
        for j, chip in enumerate(chips):
            for b in range(n):
                landed = views[b](rout[b], 2 * chip[0] + chip[1], c)
                _rcopy(landed, landed, send.at[j * n + b], recv.at[j * n + b], (x, y, c)).wait_recv()
                fw = _rcopy(landed, landed, fsend.at[j * n + b], frecv.at[j * n + b], (x, y, 1 - c))
                fw.start()
                passed.append(fw)
        for j, chip in enumerate(chips):
            for b in range(n):
                other = views[b](rout[b], 2 * chip[0] + chip[1], 1 - c)
                _rcopy(other, other, fsend.at[j * n + b], frecv.at[j * n + b], (x, y, c)).wait_recv()
        for j, chip in enumerate(chips):
            for b in range(n):
                mine = views[b](rout[b], 2 * x + y, c)
                _rcopy(mine, mine, send.at[j * n + b], recv.at[j * n + b], (x, y, c)).wait_send()
        for fw in passed:
            fw.wait_send()

    return _Rider(list(bufs), [jax.ShapeDtypeStruct(a.shape, a.dtype) for a in bufs], {b: b for b in range(n)},
                  [pltpu.SemaphoreType.DMA((3 * n,))] * 4, start, finish)


def _small_gather_rider(cw):
    def descs(rin, rout, sems, x, y, c, chips):
        return [_rcopy(rin[0], rout[0].at[2 * x + y], sems[1].at[j], sems[2].at[j], (chip[0], chip[1], c))
                for j, chip in enumerate(chips)]

    def start(rin, rout, sems):
        x, y, c, chips = _place()
        pltpu.make_async_copy(rin[0], rout[0].at[2 * x + y], sems[0].at[0]).start()
        for cp in descs(rin, rout, sems, x, y, c, chips):
            cp.start()

    def finish(rin, rout, sems):
        x, y, c, chips = _place()
        for j, chip in enumerate(chips):
            _rcopy(rin[0], rout[0].at[2 * chip[0] + chip[1]], sems[1].at[j], sems[2].at[j], (x, y, c)).wait_recv()
        for cp in descs(rin, rout, sems, x, y, c, chips):
            cp.wait_send()
        pltpu.make_async_copy(rin[0], rout[0].at[2 * x + y], sems[0].at[0]).wait()

    return _Rider([cw], [jax.ShapeDtypeStruct((NSH,) + cw.shape, cw.dtype)], {},
                  [pltpu.SemaphoreType.DMA((1,)), pltpu.SemaphoreType.DMA((3,)), pltpu.SemaphoreType.DMA((3,))],
                  start, finish)


def _to_chips_rider(cs):
    n = len(cs)

    def descs(rin, rout, sems):
        x, y, c, chips = _place()
        return [_rcopy(rin[i].at[2 * chip[0] + chip[1]], rout[i].at[j], sems[0].at[j * n + i], sems[1].at[j * n + i],
                       (chip[0], chip[1], c)) for j, chip in enumerate(chips) for i in range(n)]

    def start(rin, rout, sems):
        for cp in descs(rin, rout, sems):
            cp.start()

    def finish(rin, rout, sems):
        for cp in descs(rin, rout, sems):
            cp.wait()

    return _Rider(list(cs), [jax.ShapeDtypeStruct((3,) + a.shape[1:], a.dtype) for a in cs], {},
                  [pltpu.SemaphoreType.DMA((3 * n,))] * 2, start, finish)


def _run_riders(name, riders):
    n_in = [len(r.operands) for r in riders]
    n_out = [len(r.out_shapes) for r in riders]
    n_sem = [len(r.sems) for r in riders]

    def body(*refs):
        parts, at = [], 0
        for counts in (n_in, n_out, n_sem):
            group = []
            for k in counts:
                group.append(refs[at:at + k])
                at += k
            parts.append(group)
        for i, r in enumerate(riders):
            r.start(parts[0][i], parts[1][i], parts[2][i])
        for i, r in enumerate(riders):
            r.finish(parts[0][i], parts[1][i], parts[2][i])

    aliases = {}
    for i, r in enumerate(riders):
        for k, v in r.aliases.items():
            aliases[sum(n_in[:i]) + k] = sum(n_out[:i]) + v
    res = pl.pallas_call(
        body, name=name, in_specs=_any_specs(sum(n_in)), out_specs=_any_specs(sum(n_out)),
        out_shape=[s for r in riders for s in r.out_shapes], input_output_aliases=aliases,
        scratch_shapes=[s for r in riders for s in r.sems],
    )(*[a for r in riders for a in r.operands])
    out, at = [], 0
    for k in n_out:
        out.append(list(res[at:at + k]))
        at += k
    return out


def _swap_halves(gs):
    n = len(gs)

    def body(*refs):
        dst, send, recv = refs[n:2 * n], refs[2 * n], refs[2 * n + 1]
        x, y, c, _ = _place()
        cps = []
        for i in range(n):
            mine = dst[i].at[:, pl.ds(c * HALF, HALF)]
            cps.append(pltpu.make_async_remote_copy(
                src_ref=mine, dst_ref=mine, send_sem=send.at[i], recv_sem=recv.at[i],
                device_id=(x, y, 1 - c), device_id_type=MESH))
        for cp in cps:
            cp.start()
        for i in range(n):
            other = dst[i].at[:, pl.ds((1 - c) * HALF, HALF)]
            pltpu.make_async_remote_copy(
                src_ref=other, dst_ref=other, send_sem=send.at[i], recv_sem=recv.at[i],
                device_id=(x, y, c), device_id_type=MESH).wait_recv()
        for cp in cps:
            cp.wait_send()

    return pl.pallas_call(
        body, name="grads_swap_halves", in_specs=_any_specs(n), out_specs=_any_specs(n),
        out_shape=[jax.ShapeDtypeStruct(g.shape, g.dtype) for g in gs],
        input_output_aliases={i: i for i in range(n)},
        scratch_shapes=[pltpu.SemaphoreType.DMA((n,)), pltpu.SemaphoreType.DMA((n,))],
    )(*gs)


SMALL_ROWS = 16


def _allreduce_small(vec):
    def body(v_ref, o_ref, buf, send, recv):
        x, y, c, _ = _place()
        me = 4 * x + 2 * y + c
        buf[me] = v_ref[...]
        cps = []
        for k in range(1, 8):
            peer = (x ^ (k >> 2), y ^ ((k >> 1) & 1), c ^ (k & 1))
            cps.append(pltpu.make_async_remote_copy(
                src_ref=v_ref, dst_ref=buf.at[me], send_sem=send.at[k - 1], recv_sem=recv.at[k - 1],
                device_id=peer, device_id_type=MESH))
        for cp in cps:
            cp.start()
        for k in range(1, 8):
            pltpu.make_async_remote_copy(
                src_ref=v_ref, dst_ref=buf.at[me ^ k], send_sem=send.at[k - 1], recv_sem=recv.at[k - 1],
                device_id=(x, y, c), device_id_type=MESH).wait_recv()
        for cp in cps:
            cp.wait_send()
        t = buf[0]
        for d in range(1, 8):
            t = t + buf[d]
        o_ref[...] = t

    return pl.pallas_call(
        body, name="allreduce_small",
        in_specs=[pl.BlockSpec(memory_space=pltpu.VMEM)], out_specs=pl.BlockSpec(memory_space=pltpu.VMEM),
        out_shape=jax.ShapeDtypeStruct((SMALL_ROWS, D), F32),
        scratch_shapes=[pltpu.VMEM((8, SMALL_ROWS, D), F32), pltpu.SemaphoreType.DMA((7,)),
                        pltpu.SemaphoreType.DMA((7,))],
    )(vec)


def _col_half(ref, slot, hc):
    return ref.at[slot, :, pl.ds(hc * HALF, HALF)]


def _stack_half(ref, slot, hc):
    return ref.at[slot, :, :, pl.ds(hc * HALF, HALF)]


def _row_tile(rows):
    for t in range(512, 15, -16):
        if rows % t == 0:
            return t
    return rows


def _same_shape_runs(arrs):
    runs, a = [], 0
    for b in range(1, len(arrs) + 1):
        if b == len(arrs) or arrs[b].shape != arrs[a].shape:
            runs.append((a, b))
            a = b
    return runs


class _Comm:
    def __init__(self):
        x, y, c = lax.axis_index("x"), lax.axis_index("y"), lax.axis_index("c")
        self.c_idx = jnp.reshape(c, (1,)).astype(jnp.int32)
        self.place = jnp.stack([2 * x + y, c]).astype(jnp.int32)
        self.groups = {}

    @staticmethod
    def gather(*bufs):
        return _gather_rider(list(bufs), [_col_half if b.ndim == 3 else _stack_half for b in bufs])

    def reduce_rider(self, tag, names, ps):
        csums = []
        for a, b in _same_shape_runs(ps):
            csums += _pair_sum("pair_sum_%s%d" % (tag, a), self.c_idx, ps[a:b], _row_tile(ps[a].shape[1]))
        self.groups[tag] = [names, csums, None]
        return _to_chips_rider(csums)

    def landed(self, tag, ts):
        self.groups[tag][2] = ts

    def finish(self):
        names, csums, ts = [], [], []
        for group_names, group_csums, group_ts in self.groups.values():
            names += group_names
            csums += group_csums
            ts += group_ts
        order = sorted(range(len(names)), key=lambda i: csums[i].shape[1])
        names, csums, ts = ([v[i] for i in order] for v in (names, csums, ts))
        halves = []
        for a, b in _same_shape_runs(csums):
            halves += _chip_sum("chip_sum_%d" % a, self.place, csums[a:b], ts[a:b], _row_tile(csums[a].shape[1]))
        return dict(zip(names, _swap_halves(halves)))


ROPE_THETA = 10000.0
SMALL_1K = ("ffn1_pre_norm", "ffn1_post_norm", "mix_pre_norm", "ssm_norm", "mix_post_norm",
            "ffn2_pre_norm", "ffn2_post_norm")
SMALL_16 = ("dt_bias", "a_log", "d_skip")
OFF_CONVB = 7 * D
OFF_16 = OFF_CONVB + CONV_C
OFF_CONVW = OFF_16 + 48
OFF_LOSS = OFF_CONVW + CONV_K * CONV_C
SMALL_LEN = SMALL_ROWS * D


def _sds(shape, dtype):
    return jax.ShapeDtypeStruct(shape, dtype)


def _ridden(res, rider):
    return res if rider is not None else (res, None)


def _ffn_down(name, act, w, tail_of, rider=None):
    tail, o_specs, o_shapes = tail_of(TS)
    return _mm(name, [act, w.dn], NN, (S // TS,),
               [pl.BlockSpec((NSH, TS, FS), lambda i: (0, i, 0)),
                pl.BlockSpec((NSH, None, FS, D), lambda i: (0, w.d0, 0, 0))], o_specs, o_shapes, rider, tail)


def _ffn_dw(name, a, b, rider=None):
    return _mm(name, [a, b], TN, (NSH,),
               [pl.BlockSpec((None, S, FS), lambda s: (s, 0, 0)), pl.BlockSpec((S, D), lambda s: (0, 0))],
               pl.BlockSpec((None, FS, D), lambda s: (s, 0, 0)), _sds((NSH, FS, D), BF16), rider)


def _ffn_dn(name, dgate, dup, w, tail_of, rider=None):
    rows = TS // 2
    tail, o_specs, o_shapes = tail_of(rows)
    a2 = pl.BlockSpec((NSH, rows, FS), lambda i: (0, i, 0))
    return _mm(name, [dgate, w.gu, dup, w.gu], NN, (S // rows,),
               [a2, pl.BlockSpec((NSH, None, FS, D), lambda i: (0, w.g0, 0, 0)),
                a2, pl.BlockSpec((NSH, None, FS, D), lambda i: (0, w.g0 + 1, 0, 0))], o_specs, o_shapes, rider, tail)


def _out_proj_dx(dh, wout):
    def body(dh_ref, w_ref, dyn_ref, do_ref):
        dm = _dot(dh_ref[...], w_ref[...], NT)
        dyn_ref[...] = dm[:, D:]
        for b in range(TS // 128):
            for j, blk in enumerate(_rows_to_blocks(dm[128 * b:128 * (b + 1), :D])):
                do_ref[j, b] = blk.astype(BF16)

    return pl.pallas_call(
        body, name="out_proj_dx", grid=(S // TS,),
        in_specs=[pl.BlockSpec((TS, D), lambda i: (i, 0)), pl.BlockSpec((2 * D, D), lambda i: (0, 0))],
        out_specs=[pl.BlockSpec((TS, D), lambda i: (i, 0)),
                   pl.BlockSpec((NKV, TS // 128, HD, QROWS), lambda i: (0, i, 0, 0))],
        out_shape=[_sds((S, D), F32), _sds((NKV, NCH, HD, QROWS), BF16)], compiler_params=_cparams("parallel"),
    )(dh, wout)


def _heads(t, n):
    return t.reshape(S, n, HD).transpose(1, 0, 2)


def _pad128(v):
    return jnp.pad(v, ((0, 0), (0, 128 - v.shape[1])))


def _local_step(x, positions, tgt, sp, gu1, d1, f2, wint, wout, convw, comm=None):
    inv_freq = ROPE_THETA ** (-jnp.arange(0, HD, 2, dtype=F32) / HD)
    ang = positions.astype(F32)[:, None] * inv_freq
    ang = jnp.concatenate([ang, ang, ang, ang], axis=-1)
    cos, sin = jnp.cos(ang), jnp.sin(ang)
    dtb, alog = _pad128(sp["dt_bias"]), _pad128(sp["a_log"])
    dskip_l = jnp.repeat(sp["d_skip"], HD, axis=1)
    convb = sp["conv_b"]

    n1 = _prenorm("prenorm1", x, sp["ffn1_pre_norm"])
    rider = comm.gather(d1) if comm else None
    (fg1, fu1, act1), got = _ridden(_ffn_up("ffn1_up", n1, _FfnW(gu1, 0, d1, 0), rider), rider)
    if comm:
        d1, = got
    w1 = _FfnW(gu1, 0, d1, 0)
    rider = comm.gather(wint) if comm else None
    (h1, x1, n2), got = _ridden(_ffn_down(
        "ffn1_down", act1, w1,
        lambda rows: _tail_postres(rows, x, sp["ffn1_post_norm"], 0.5, sp["mix_pre_norm"]), rider), rider)
    if comm:
        wint, = got
    wint_pad = jnp.pad(wint.reshape(WIN_COLS, D), ((0, WIN_PAD - WIN_COLS), (0, 0)))

    pw = WIN_PAD // 3
    proj = _mm("in_proj", [n2, wint_pad], NT, (S // TS, 3),
               [pl.BlockSpec((TS, D), lambda i, j: (i, 0)), pl.BlockSpec((pw, D), lambda i, j: (j, 0))],
               pl.BlockSpec((TS, pw), lambda i, j: (i, j)), _sds((S, WIN_PAD), F32))
    qt = _rope_q(proj, cos, sin)
    k_rot, v_bf, kt, vt = _rope_kv(proj, cos, sin)
    kh, vh = _heads(k_rot, NKV), _heads(v_bf, NKV)
    bias = _bias_table()
    rider = comm.gather(f2, wout) if comm else None
    (ot, lse, attn), got = _ridden(_attn_fwd(qt, kh, vt, bias, rider), rider)
    if comm:
        f2, wout = got
    w2 = _FfnW(f2, 0, f2, 2)
    wout = wout.reshape(2 * D, D)
    xbc = _conv_fwd(proj, convw, convb)
    y, yn, hprev = _ssd_fwd(xbc, proj, dtb, alog, dskip_l, sp["ssm_norm"])
    mixed = jnp.concatenate([attn, yn], axis=1)
    tail, o_specs, o_shapes = _tail_postres(TS, x1, sp["mix_post_norm"], 1.0, sp["ffn2_pre_norm"])
    h2, x2, n3 = _mm("out_proj", [mixed, wout], NN, (S // TS,),
                     [pl.BlockSpec((TS, 2 * D), lambda i: (i, 0)), pl.BlockSpec((2 * D, D), lambda i: (0, 0))],
                     o_specs, o_shapes, None, tail)

    fg2, fu2, act2 = _ffn_up("ffn2_up", n3, w2)
    dy, dh3, dp3, loss = _ffn_down(
        "ffn2_down", act2, w2, lambda rows: _tail_final(rows, x2, sp["ffn2_post_norm"], tgt, 0.5))

    dgate2, dup2 = _ffn_dact("ffn2_dact", dh3, w2, fg2, fu2)
    dws2 = [_ffn_dw("ffn2_dwg", dgate2, n3), _ffn_dw("ffn2_dwu", dup2, n3), _ffn_dw("ffn2_dwd", act2, dh3)]
    dx2, dh2, dg3, dp2 = _ffn_dn(
        "ffn2_dn", dgate2, dup2, w2,
        lambda rows: _tail_mid_bwd(rows, dy, x2, sp["ffn2_pre_norm"], h2, sp["mix_post_norm"], 1.0))

    dyn, dot_ = _out_proj_dx(dh2, wout)
    dwout = _mm("out_proj_dw", [mixed, dh2], TN, (2,),
                [pl.BlockSpec((S, D), lambda m: (0, m)), pl.BlockSpec((S, D), lambda m: (0, 0))],
                pl.BlockSpec((D, D), lambda m: (m, 0)), _sds((2 * D, D), BF16))
    dwout = dwout.reshape(NSH, 2 * D // NSH, D)

    def riding(tag, names, ps, call):
        rider = comm.reduce_rider(tag, names, ps) if comm else None
        res, got = _ridden(call(rider), rider)
        if comm:
            comm.landed(tag, got)
        return res

    dxbc, dproj, ddt, dssm, dsc = _ssd_bwd(dyn, y, xbc, proj, hprev, dtb, alog, dskip_l, sp["ssm_norm"])
    dproj, dcw8, dcb = _conv_bwd(dxbc, proj, convw, convb, dproj)
    delta = _attn_delta(ot, dot_)
    dqt, dkh, dvh = riding("a", BIG[3:6] + ("w_out",), dws2 + [dwout], lambda rider: _attn_bwd(
        qt, kh, kt, vh, dot_, lse, delta, bias, rider))
    dproj = _rope_dq(dqt, cos, sin, dproj)
    dproj = _rope_dkv(dkh, dvh, cos, sin, dproj)
    dproj = lax.dynamic_update_slice(dproj, ddt, (0, COL_DT))
    dwint = _mm("in_proj_dw", [dproj, n2], TN, (3,),
                [pl.BlockSpec((S, pw), lambda j: (0, j)), pl.BlockSpec((S, D), lambda j: (0, 0))],
                pl.BlockSpec((pw, D), lambda j: (j, 0)), _sds((WIN_PAD, D), BF16))
    dwint = dwint[:WIN_COLS].reshape(NSH, WIN_SH, D)

    tail, o_specs, o_shapes = _tail_mid_bwd(TS, dx2, x1, sp["mix_pre_norm"], h1, sp["ffn1_post_norm"], 0.5)
    dx1, dh1, dg2, dp1 = riding("b", ("w_in",), [dwint], lambda rider: _mm(
        "in_proj_dx", [dproj, wint_pad], NN, (S // TS,),
        [pl.BlockSpec((TS, WIN_PAD), lambda i: (i, 0)), pl.BlockSpec((WIN_PAD, D), lambda i: (0, 0))],
        o_specs, o_shapes, rider, tail))

    dwd1 = _ffn_dw("ffn1_dwd", act1, dh1)
    dgate1, dup1 = riding("d", BIG[2:3], [dwd1], lambda rider: _ffn_dact("ffn1_dact", dh1, w1, fg1, fu1, rider))
    dwg1, dwu1 = _ffn_dw("ffn1_dwg", dgate1, n1), _ffn_dw("ffn1_dwu", dup1, n1)
    grad_x, dg1 = riding("g", BIG[0:2], [dwg1, dwu1], lambda rider: _ffn_dn(
        "ffn1_dn", dgate1, dup1, w1, lambda rows: _tail_first_bwd(rows, dx1, x, sp["ffn1_pre_norm"]), rider))
    dws1 = [dwg1, dwu1, dwd1]

    small = jnp.concatenate([
        dg1[0], dp1[0], dg2[0], dssm[0], dp2[0], dg3[0], dp3[0], dcb[0],
        dsc[0, :16], dsc[1, :16], dsc[2, :16], dcw8[:CONV_K].reshape(-1), loss[0, :1]])
    small = jnp.pad(small, (0, SMALL_LEN - small.shape[0])).reshape(SMALL_ROWS, D)
    if comm is None:
        return grad_x, dws1 + dws2 + [dwint, dwout], small
    return grad_x, comm.finish(), small


WEIGHTS = ("ffn1_pre_norm", "ffn1_w_gate", "ffn1_w_up", "ffn1_w_down", "ffn1_post_norm", "mix_pre_norm", "w_in",
           "conv_w", "conv_b", "dt_bias", "a_log", "d_skip", "ssm_norm", "w_out", "mix_post_norm", "ffn2_pre_norm",
           "ffn2_w_gate", "ffn2_w_up", "ffn2_w_down", "ffn2_post_norm")
BIG = ("ffn1_w_gate", "ffn1_w_up", "ffn1_w_down", "ffn2_w_gate", "ffn2_w_up", "ffn2_w_down", "w_in", "w_out")
TRANSPOSED = ("ffn1_w_gate", "ffn1_w_up", "ffn2_w_gate", "ffn2_w_up", "w_in")
SMALL_ORDER = SMALL_1K + ("conv_b",) + SMALL_16
CONVW_SH = CONV_C // NSH


def _shard2d(t, name):
    return t[0].T if name in TRANSPOSED else t[0]


def _unshard2d(t, name):
    return (t.T if name in TRANSPOSED else t)[None]


def _rows3d(t):
    return t.transpose(2, 0, 1)


def _pack_small(d, prefix, shard_of_convw):
    flat = jnp.concatenate([d[prefix + n][0] for n in SMALL_ORDER] + [shard_of_convw.reshape(-1)])
    return jnp.pad(flat, (0, SMALL_LEN - flat.shape[0])).reshape(SMALL_ROWS, D)


def _unpack_small(block, like):
    flat = block.reshape(-1)
    out, off = {}, 0
    for n in SMALL_ORDER:
        size = like[n].shape[1]
        out[n] = flat[off:off + size].reshape(1, size)
        off += size
    out["conv_w"] = flat[off:off + CONV_K * CONVW_SH].reshape(1, CONV_K, CONVW_SH)
    return out


def kernel(x, positions, ffn1_pre_norm, ffn1_w_gate, ffn1_w_up, ffn1_w_down, ffn1_post_norm, mix_pre_norm, w_in, conv_w, conv_b, dt_bias, a_log, d_skip, ssm_norm, w_out, mix_post_norm, ffn2_pre_norm, ffn2_w_gate, ffn2_w_up, ffn2_w_down, ffn2_post_norm, loss_target, m_ffn1_pre_norm, m_ffn1_w_gate, m_ffn1_w_up, m_ffn1_w_down, m_ffn1_post_norm, m_mix_pre_norm, m_w_in, m_conv_w, m_conv_b, m_dt_bias, m_a_log, m_d_skip, m_ssm_norm, m_w_out, m_mix_post_norm, m_ffn2_pre_norm, m_ffn2_w_gate, m_ffn2_w_up, m_ffn2_w_down, m_ffn2_post_norm, v_ffn1_pre_norm, v_ffn1_w_gate, v_ffn1_w_up, v_ffn1_w_down, v_ffn1_post_norm, v_mix_pre_norm, v_w_in, v_conv_w, v_conv_b, v_dt_bias, v_a_log, v_d_skip, v_ssm_norm, v_w_out, v_mix_post_norm, v_ffn2_pre_norm, v_ffn2_w_gate, v_ffn2_w_up, v_ffn2_w_down, v_ffn2_post_norm):
    given = dict(locals())
    xi, yi = lax.axis_index("x"), lax.axis_index("y")

    shard = jnp.reshape(2 * xi + yi, (1,)).astype(jnp.int32)
    big = {p + n: _shard2d(given[p + n], n) for n in BIG for p in ("", "m_", "v_")}
    gu1 = _cast_stack("cast_ffn1_gate_up", shard, [big[n] for n in BIG[0:2]], 176, D)
    d1 = _cast_stack("cast_ffn1_down", shard, [big[BIG[2]]], 176, D)
    f2 = _cast_stack("cast_ffn2", shard, [big[n] for n in BIG[3:6]], 176, D)
    winsh = _cast_stack("cast_w_in", shard, [big["w_in"]], WIN_SH, 256).reshape(NSH, WIN_SH, D)
    woutsh = _cast_stack("cast_w_out", shard, [big["w_out"]], 256, D).reshape(NSH, 2 * D // NSH, D)
    comm = _Comm()
    (gu1,), (cwf,) = _run_riders("gather_ffn1_gate_up", [comm.gather(gu1), _small_gather_rider(conv_w[0])])
    convw = cwf.transpose(1, 0, 2).reshape(CONV_K, CONV_C)

    sp = {n: given[n] for n in SMALL_ORDER}
    grad_x, big_grads, small = _local_step(x[0], positions[0], loss_target[0], sp, gu1, d1, f2, winsh, woutsh,
                                           convw, comm)

    tot = _allreduce_small(small).reshape(-1)
    loss = tot[OFF_LOSS]
    small_grads, off = {}, 0
    for n in SMALL_ORDER:
        size = given[n].shape[1]
        small_grads[n] = tot[off:off + size].reshape(1, size)
        off += size
    dconvw = tot[OFF_CONVW:OFF_CONVW + CONV_K * CONV_C].reshape(CONV_K, NSH, CONVW_SH)
    dconvw = lax.dynamic_index_in_dim(dconvw, 2 * xi + yi, axis=1, keepdims=False)
    small_grads["conv_w"] = dconvw.reshape(1, CONV_K, CONVW_SH)

    upd = {}
    for names, tr in ((BIG[0:3], 176), (BIG[3:6], 176), (BIG[7:8], 256)):
        res = _adamw("adamw_" + names[0], [big[n] for n in names], [big_grads[n] for n in names],
                     [big["m_" + n] for n in names], [big["v_" + n] for n in names], tr, D)
        for n, r in zip(names, res):
            upd[n] = tuple(_unshard2d(t, n) for t in r)
    g_win = big_grads["w_in"].reshape(WIN_SH, 1, D)
    res, = _adamw("adamw_w_in", [_rows3d(w_in)], [g_win], [_rows3d(m_w_in)], [_rows3d(v_w_in)], WIN_SH // 4, D)
    upd["w_in"] = tuple(t.transpose(1, 2, 0) for t in res)
    (dl, m2, v2, _), = _adamw(
        "adamw_small", [_pack_small(given, "", conv_w[0])], [_pack_small(small_grads, "", dconvw)],
        [_pack_small(given, "m_", m_conv_w[0])], [_pack_small(given, "v_", v_conv_w[0])], SMALL_ROWS, D)
    dl, m2, v2 = (_unpack_small(t, given) for t in (dl, m2, v2))
    for n in SMALL_ORDER + ("conv_w",):
        upd[n] = (dl[n], m2[n], v2[n], small_grads[n])

    return (loss, grad_x[None], *[upd[n][3] for n in WEIGHTS], *[upd[n][0] for n in WEIGHTS],
            *[upd[n][1] for n in WEIGHTS], *[upd[n][2] for n in WEIGHTS])
```

```python
import functools
import typing

import jax
import jax.numpy as jnp
from jax import lax
from jax.experimental import pallas as pl
from jax.experimental.pallas import tpu as pltpu

F32 = jnp.float32
BF16 = jnp.bfloat16

S = 2048
D = 1024
FF = 2816
NSH = 4
FS = FF // NSH
HALF = D // 2
HD = 64
NKV = 4
NQ_PER_KV = 4
KVW = NKV * HD
QCOLS = NQ_PER_KV * HD
CONV_C = 1536
CONV_K = 4
SSM_W = 1024
NST = 128
NCH = S // 128
WIN_COLS = 4112
WIN_SH = WIN_COLS // NSH
WIN_PAD = 4224
COL_DT = 4096
EPS = 1e-6
NEG = -1e30

ADAM_LR = 0.001
ADAM_B1 = 0.9
ADAM_B2 = 0.999
ADAM_EPS = 1e-08
ADAM_WD = 0.01
ADAM_STEP = 10

VMEM_LIMIT = 56 * 1024 * 1024
TS = 512
TR = 256

NN = (((1,), (0,)), ((), ()))
NT = (((1,), (1,)), ((), ()))
TN = (((0,), (0,)), ((), ()))
MESH = pl.DeviceIdType.MESH


def _cparams(*sem):
    return pltpu.CompilerParams(dimension_semantics=sem, vmem_limit_bytes=VMEM_LIMIT)


def _dot(a, b, dims):
    return lax.dot_general(a.astype(BF16), b.astype(BF16), dims, preferred_element_type=F32)


def _bf16_pieces(v):
    hi = v.astype(BF16)
    rest = v - hi.astype(F32)
    mid = rest.astype(BF16)
    return hi, mid, (rest - mid.astype(F32)).astype(BF16)


def _dot_exact(a, b, ones="a"):
    if ones == "a":
        sel = a.astype(BF16)
        parts = [lax.dot_general(sel, p, NN, preferred_element_type=F32) for p in _bf16_pieces(b)]
    else:
        sel = b.astype(BF16)
        parts = [lax.dot_general(p, sel, NN, preferred_element_type=F32) for p in _bf16_pieces(a)]
    return (parts[2] + parts[1]) + parts[0]


def _sigmoid(v):
    return 1.0 / (1.0 + jnp.exp(-v))


class _Rider(typing.NamedTuple):
    operands: list
    out_shapes: list
    aliases: dict
    sems: list
    start: typing.Callable
    finish: typing.Callable


def _call(body, name, grid, in_specs, out_specs, out_shape, operands, scratch=(), sem=(), rider=None):
    multi = isinstance(out_shape, (list, tuple))
    if rider is None:
        return pl.pallas_call(
            body, name=name, grid=grid, in_specs=in_specs, out_specs=out_specs, out_shape=out_shape,
            scratch_shapes=list(scratch), compiler_params=_cparams(*sem))(*operands)
    outs = list(out_shape) if multi else [out_shape]
    ospecs = list(out_specs) if multi else [out_specs]
    n_in, n_out, n_scr = len(operands), len(outs), len(scratch)
    ri, ro = len(rider.operands), len(rider.out_shapes)

    def wrapped(*refs):
        o0 = n_in + ri
        s0 = o0 + n_out + ro
        rin, rout, rsem = refs[n_in:o0], refs[o0 + n_out:s0], refs[s0 + n_scr:]
        ids = [pl.program_id(a) for a in range(len(grid))]
        first = functools.reduce(jnp.logical_and, [i == 0 for i in ids])
        last = functools.reduce(jnp.logical_and, [i == g - 1 for i, g in zip(ids, grid)])

        @pl.when(first)
        def _():
            rider.start(rin, rout, rsem)

        body(*refs[:n_in], *refs[o0:o0 + n_out], *refs[s0:s0 + n_scr])

        @pl.when(last)
        def _():
            rider.finish(rin, rout, rsem)

    hbm = pl.BlockSpec(memory_space=pl.ANY)
    res = pl.pallas_call(
        wrapped, name=name, grid=grid, in_specs=list(in_specs) + [hbm] * ri, out_specs=ospecs + [hbm] * ro,
        out_shape=outs + list(rider.out_shapes), scratch_shapes=list(scratch) + list(rider.sems),
        input_output_aliases={n_in + k: n_out + v for k, v in rider.aliases.items()},
        compiler_params=_cparams(*(("arbitrary",) * len(grid))))(*operands, *rider.operands)
    main = list(res[:n_out])
    return (main if multi else main[0]), list(res[n_out:])


class _Tail(typing.NamedTuple):
    fn: typing.Callable
    operands: list
    in_specs: list


def _mm(name, operands, dims, grid, in_specs, o_spec, out_shape, rider=None, tail=None):
    npairs = len(operands) // 2
    extra = [] if tail is None else list(tail.operands)
    nin = 2 * npairs + len(extra)

    def body(*refs):
        t = None
        for i in range(npairs):
            a, b = refs[2 * i], refs[2 * i + 1]
            parts = [(a[s], b[s]) for s in range(a.shape[0])] if len(a.shape) == 3 else [(a[...], b[...])]
            for pa, pb in parts:
                d = _dot(pa, pb, dims)
                t = d if t is None else t + d
        if tail is None:
            refs[nin][...] = t.astype(refs[nin].dtype)
        else:
            tail.fn(t, refs[2 * npairs:nin], refs[nin:])

    sem = ("parallel" if tail is None else "arbitrary",) * len(grid)
    specs = list(in_specs) + ([] if tail is None else list(tail.in_specs))
    return _call(body, name, grid, specs, o_spec, out_shape, list(operands) + extra, (), sem, rider)


class _FfnW(typing.NamedTuple):
    gu: jax.Array
    g0: int
    dn: jax.Array
    d0: int


def _ffn_up(name, n, w, rider=None):
    def body(n_ref, wg_ref, wu_ref, fg_ref, fu_ref, a_ref):
        nb = n_ref[...]
        g = _dot(nb, wg_ref[...], NT)
        u = _dot(nb, wu_ref[...], NT)
        sg = _sigmoid(g)
        silu = g * sg
        fg_ref[...] = (u * (sg * (1.0 + g * (1.0 - sg)))).astype(BF16)
        fu_ref[...] = silu.astype(BF16)
        a_ref[...] = (silu * u).astype(BF16)

    out = jax.ShapeDtypeStruct((NSH, S, FS), BF16)
    ospec = pl.BlockSpec((None, TS, FS), lambda s, i: (s, i, 0))
    return _call(
        body, name, (NSH, S // TS),
        [pl.BlockSpec((TS, D), lambda s, i: (i, 0)),
         pl.BlockSpec((None, None, FS, D), lambda s, i: (s, w.g0, 0, 0)),
         pl.BlockSpec((None, None, FS, D), lambda s, i: (s, w.g0 + 1, 0, 0))],
        [ospec, ospec, ospec], [out, out, out], (n, w.gu, w.gu), sem=("parallel", "parallel"), rider=rider)


def _ffn_dact(name, dh, w, fgate, fup, rider=None):
    def body(dh_ref, wd_ref, fg_ref, fu_ref, dg_ref, du_ref):
        da = _dot(dh_ref[...], wd_ref[...], NT)
        dg_ref[...] = (da * fg_ref[...].astype(F32)).astype(BF16)
        du_ref[...] = (da * fu_ref[...].astype(F32)).astype(BF16)

    out = jax.ShapeDtypeStruct((NSH, S, FS), BF16)
    aspec = pl.BlockSpec((None, TS, FS), lambda s, i: (s, i, 0))
    return _call(
        body, name, (NSH, S // TS),
        [pl.BlockSpec((TS, D), lambda s, i: (i, 0)),
         pl.BlockSpec((None, None, FS, D), lambda s, i: (s, w.d0, 0, 0)), aspec, aspec],
        [aspec, aspec], [out, out], (dh, w.dn, fgate, fup), sem=("parallel", "parallel"), rider=rider)


def _rstd(v):
    return lax.rsqrt(jnp.mean(v * v, axis=-1, keepdims=True) + EPS)


def _row_spec():
    return pl.BlockSpec((TR, D), lambda i: (i, 0))


def _vec_spec():
    return pl.BlockSpec((1, D), lambda i: (0, 0))


def _acc_rows(ref, v):
    @pl.when(pl.program_id(0) == 0)
    def _():
        ref[...] = jnp.zeros_like(ref)
    ref[...] += jnp.sum(v, axis=0, keepdims=True)


def _prenorm(name, x, g):
    def body(x_ref, g_ref, n_ref):
        xv = x_ref[...]
        n_ref[...] = (xv * _rstd(xv) * g_ref[...]).astype(BF16)

    return pl.pallas_call(
        body, name=name, grid=(S // TR,), in_specs=[_row_spec(), _vec_spec()], out_specs=_row_spec(),
        out_shape=jax.ShapeDtypeStruct((S, D), BF16), compiler_params=_cparams("parallel"),
    )(x, g)


def _rows_spec(rows):
    return pl.BlockSpec((rows, D), lambda i: (i, 0))


def _rows_f32():
    return jax.ShapeDtypeStruct((S, D), F32)


def _rows_bf16():
    return jax.ShapeDtypeStruct((S, D), BF16)


def _vec_f32():
    return jax.ShapeDtypeStruct((1, D), F32)


def _tail_postres(rows, x, p, alpha, gnext):
    def fn(h, ins, outs):
        x_ref, p_ref, g_ref = ins
        h_ref, xo_ref, n_ref = outs
        h_ref[...] = h
        xo = x_ref[...] + alpha * (h * _rstd(h) * p_ref[...])
        xo_ref[...] = xo
        n_ref[...] = (xo * _rstd(xo) * g_ref[...]).astype(BF16)

    rs = _rows_spec(rows)
    return (_Tail(fn, [x, p, gnext], [rs, _vec_spec(), _vec_spec()]), [rs, rs, rs],
            [_rows_f32(), _rows_f32(), _rows_bf16()])


def _tail_final(rows, x, p, tgt, alpha):
    def fn(h, ins, outs):
        x_ref, p_ref, t_ref = ins
        dy_ref, dh_ref, dp_ref, loss_ref = outs
        r = _rstd(h)
        hn = h * r
        pv = p_ref[...]
        e = x_ref[...] + alpha * (hn * pv) - t_ref[...]
        dy = e * (1.0 / D)
        dy_ref[...] = dy
        du = alpha * dy * pv
        dh_ref[...] = (r * (du - hn * jnp.mean(du * hn, axis=-1, keepdims=True))).astype(BF16)
        _acc_rows(dp_ref, alpha * dy * hn)
        part = 0.5 * jnp.sum(jnp.mean(e * e, axis=-1, keepdims=True), axis=0, keepdims=True)
        _acc_rows(loss_ref, jnp.broadcast_to(part, (1, 128)))

    rs = _rows_spec(rows)
    return (_Tail(fn, [x, p, tgt], [rs, _vec_spec(), rs]),
            [rs, rs, _vec_spec(), pl.BlockSpec((1, 128), lambda i: (0, 0))],
            [_rows_f32(), _rows_bf16(), _vec_f32(), jax.ShapeDtypeStruct((1, 128), F32)])


def _norm_bwd(dn, xv, g_ref, dg_ref):
    r = _rstd(xv)
    xn = xv * r
    dng = dn * g_ref[...]
    _acc_rows(dg_ref, dn * xn)
    return r * (dng - xn * jnp.mean(dng * xn, axis=-1, keepdims=True))


def _tail_mid_bwd(rows, dres, x, g, h, p, alpha):
    def fn(dn, ins, outs):
        dr_ref, x_ref, g_ref, h_ref, p_ref = ins
        dx_ref, dh_ref, dg_ref, dp_ref = outs
        dx = dr_ref[...] + _norm_bwd(dn, x_ref[...], g_ref, dg_ref)
        dx_ref[...] = dx
        hv = h_ref[...]
        r = _rstd(hv)
        hn = hv * r
        du = alpha * dx * p_ref[...]
        dh_ref[...] = (r * (du - hn * jnp.mean(du * hn, axis=-1, keepdims=True))).astype(BF16)
        _acc_rows(dp_ref, alpha * dx * hn)

    rs = _rows_spec(rows)
    return (_Tail(fn, [dres, x, g, h, p], [rs, rs, _vec_spec(), rs, _vec_spec()]),
            [rs, rs, _vec_spec(), _vec_spec()], [_rows_f32(), _rows_bf16(), _vec_f32(), _vec_f32()])


def _tail_first_bwd(rows, dres, x, g):
    def fn(dn, ins, outs):
        dr_ref, x_ref, g_ref = ins
        dx_ref, dg_ref = outs
        dx_ref[...] = dr_ref[...] + _norm_bwd(dn, x_ref[...], g_ref, dg_ref)

    rs = _rows_spec(rows)
    return (_Tail(fn, [dres, x, g], [rs, rs, _vec_spec()]), [rs, _vec_spec()], [_rows_f32(), _vec_f32()])


def _rotate(t, c128, s128, sign, scale):
    width = t.shape[1]
    c = jnp.tile(c128, (1, width // 128))
    sn = jnp.tile(s128, (1, width // 128))
    lane = lax.broadcasted_iota(jnp.int32, t.shape, 1) & (HD - 1)
    rot = jnp.where(lane < HD // 2, -pltpu.roll(t, width - HD // 2, 1), pltpu.roll(t, HD // 2, 1))
    return (t * c + sign * (rot * sn)) * scale


def _rows_to_blocks(y):
    out = []
    for j in range(NKV):
        yt = y[:, QCOLS * j:QCOLS * (j + 1)].T
        out.append(jnp.concatenate([yt[HD * g:HD * (g + 1)] for g in range(NQ_PER_KV)], axis=1))
    return out


def _blocks_to_rows(blocks):
    cols = []
    for b in blocks:
        stacked = jnp.concatenate([b[:, 128 * g:128 * (g + 1)] for g in range(NQ_PER_KV)], axis=0)
        cols.append(stacked.T)
    return jnp.concatenate(cols, axis=1)


def _rope_q(proj, cos, sin):
    def body(t_ref, c_ref, s_ref, o_ref):
        y = _rotate(t_ref[...], c_ref[...], s_ref[...], 1.0, HD ** -0.5)
        for j, blk in enumerate(_rows_to_blocks(y)):
            o_ref[j] = blk.astype(BF16)

    return pl.pallas_call(
        body, name="rope_q", grid=(NCH,),
        in_specs=[pl.BlockSpec((128, D), lambda i: (i, 0)),
                  pl.BlockSpec((128, 128), lambda i: (i, 0)), pl.BlockSpec((128, 128), lambda i: (i, 0))],
        out_specs=pl.BlockSpec((NKV, None, HD, QROWS), lambda i: (0, i, 0, 0)),
        out_shape=jax.ShapeDtypeStruct((NKV, NCH, HD, QROWS), BF16), compiler_params=_cparams("parallel"),
    )(proj, cos, sin)


def _rope_dq(dqt, cos, sin, dproj):
    def body(t_ref, c_ref, s_ref, buf_ref, o_ref):
        t = _blocks_to_rows([t_ref[j] for j in range(NKV)])
        o_ref[...] = _rotate(t, c_ref[...], s_ref[...], -1.0, HD ** -0.5).astype(BF16)

    return pl.pallas_call(
        body, name="rope_dq", grid=(NCH,),
        in_specs=[pl.BlockSpec((NKV, None, HD, QROWS), lambda i: (0, i, 0, 0)),
                  pl.BlockSpec((128, 128), lambda i: (i, 0)), pl.BlockSpec((128, 128), lambda i: (i, 0)),
                  pl.BlockSpec(memory_space=pl.ANY)],
        out_specs=pl.BlockSpec((128, D), lambda i: (i, 0)),
        out_shape=jax.ShapeDtypeStruct(dproj.shape, BF16), input_output_aliases={3: 0},
        compiler_params=_cparams("parallel"),
    )(dqt, cos, sin, dproj)


def _rope_dkv(dkt, dvt, cos, sin, dproj):
    def body(k_ref, v_ref, c_ref, s_ref, buf_ref, o_ref):
        dk = jnp.concatenate([k_ref[j] for j in range(NKV)], axis=0).T
        dv = jnp.concatenate([v_ref[j] for j in range(NKV)], axis=0).T
        dk = _rotate(dk, c_ref[...], s_ref[...], -1.0, 1.0)
        o_ref[...] = jnp.concatenate([dk, dv], axis=1).astype(BF16)

    tspec = pl.BlockSpec((NKV, HD, 128), lambda i: (0, 0, i))
    return pl.pallas_call(
        body, name="rope_dkv", grid=(NCH,),
        in_specs=[tspec, tspec, pl.BlockSpec((128, 128), lambda i: (i, 0)), pl.BlockSpec((128, 128), lambda i: (i, 0)),
                  pl.BlockSpec(memory_space=pl.ANY)],
        out_specs=pl.BlockSpec((128, 2 * KVW), lambda i: (i, D // (2 * KVW))),
        out_shape=jax.ShapeDtypeStruct(dproj.shape, BF16), input_output_aliases={4: 0},
        compiler_params=_cparams("parallel"),
    )(dkt, dvt, cos, sin, dproj)


def _rope_kv(proj, cos, sin):
    def body(t_ref, c_ref, s_ref, k_ref, v_ref, kt_ref, vt_ref):
        t = t_ref[...]
        k = _rotate(t[:, :KVW], c_ref[...], s_ref[...], 1.0, 1.0).astype(BF16)
        v = t[:, KVW:].astype(BF16)
        k_ref[...] = k
        v_ref[...] = v
        kt, vt = k.astype(F32).T, v.astype(F32).T
        for j in range(NKV):
            kt_ref[j] = kt[HD * j:HD * (j + 1)].astype(BF16)
            vt_ref[j] = vt[HD * j:HD * (j + 1)].astype(BF16)

    rows = pl.BlockSpec((128, KVW), lambda i: (i, 0))
    tspec = pl.BlockSpec((NKV, HD, 128), lambda i: (0, 0, i))
    return pl.pallas_call(
        body, name="rope_kv", grid=(NCH,),
        in_specs=[pl.BlockSpec((128, 2 * KVW), lambda i: (i, D // (2 * KVW))),
                  pl.BlockSpec((128, 128), lambda i: (i, 0)), pl.BlockSpec((128, 128), lambda i: (i, 0))],
        out_specs=[rows, rows, tspec, tspec],
        out_shape=[jax.ShapeDtypeStruct((S, KVW), BF16)] * 2 + [jax.ShapeDtypeStruct((NKV, HD, S), BF16)] * 2,
        compiler_params=_cparams("parallel"),
    )(proj, cos, sin)


QROWS = NQ_PER_KV * 128


NBIAS = NCH + 1
KV_PER_STEP = 4


def _bias_table():
    db = lax.broadcasted_iota(jnp.int32, (NBIAS, 128, QROWS), 0) - 1
    ki = lax.broadcasted_iota(jnp.int32, (NBIAS, 128, QROWS), 1)
    qi = lax.broadcasted_iota(jnp.int32, (NBIAS, 128, QROWS), 2) & 127
    d = db * 128 + qi - ki
    cnt = ((d <= 128).astype(F32) + (((d & 3) == 0) & (d <= 512)).astype(F32) + ((d & 15) == 0).astype(F32))
    return jnp.where((d >= 0) & (cnt > 0.0), jnp.log(jnp.maximum(cnt, 1.0)), NEG)


def _qt_spec():
    return pl.BlockSpec((None, None, HD, QROWS), lambda j, i: (j, i, 0, 0))


def _stat_spec():
    return pl.BlockSpec((None, None, 1, QROWS), lambda j, i: (j, i, 0, 0))


def _attn_fwd(qt, kh, vt, bias, rider=None):
    def body(q_ref, k_ref, v_ref, b_ref, o_ref, lse_ref, rows_ref, m_ref, l_ref, acc_ref):
        qb = pl.program_id(1)
        m_ref[...] = jnp.full_like(m_ref, NEG)
        l_ref[...] = jnp.zeros_like(l_ref)
        acc_ref[...] = jnp.zeros_like(acc_ref)

        def keys(off, size, bias_):
            for h in range(KV_PER_STEP):
                m = m_ref[h]
                s = _dot(k_ref[h, pl.ds(off, size), :], q_ref[h], NN) + bias_
                m_new = jnp.maximum(m, jnp.max(s, axis=0, keepdims=True))
                p = jnp.exp(s - m_new)
                a = jnp.exp(m - m_new)
                m_ref[h] = m_new
                l_ref[h] = a * l_ref[h] + jnp.sum(p, axis=0, keepdims=True)
                acc_ref[h] = a * acc_ref[h] + _dot(v_ref[h, :, pl.ds(off, size)], p, NN)

        @pl.loop(0, (qb + 1) // 2)
        def _(i):
            bias2 = jnp.concatenate([b_ref[qb - 2 * i + 1], b_ref[qb - 2 * i]], axis=0)
            keys(pl.multiple_of(i * 256, 256), 256, bias2)

        @pl.when(qb % 2 == 0)
        def _():
            keys(pl.multiple_of(qb * 128, 128), 128, b_ref[1])

        outs = []
        for h in range(KV_PER_STEP):
            outs.append(acc_ref[h] / l_ref[h])
            o_ref[h] = outs[h]
            lse_ref[h] = m_ref[h] + jnp.log(l_ref[h])
        rows_ref[...] = _blocks_to_rows(outs).astype(BF16)

    kvs = KV_PER_STEP
    qspec = pl.BlockSpec((kvs, None, HD, QROWS), lambda j, i: (j, i, 0, 0))
    return _call(
        body, "attn_fwd", (NKV // kvs, NCH),
        [qspec, pl.BlockSpec((kvs, S, HD), lambda j, i: (j, 0, 0)),
         pl.BlockSpec((kvs, HD, S), lambda j, i: (j, 0, 0)),
         pl.BlockSpec((NBIAS, 128, QROWS), lambda j, i: (0, 0, 0))],
        [qspec, pl.BlockSpec((kvs, None, 1, QROWS), lambda j, i: (j, i, 0, 0)),
         pl.BlockSpec((128, QCOLS * kvs), lambda j, i: (i, j))],
        [jax.ShapeDtypeStruct((NKV, NCH, HD, QROWS), F32), jax.ShapeDtypeStruct((NKV, NCH, 1, QROWS), F32),
         jax.ShapeDtypeStruct((S, D), BF16)],
        (qt, kh, vt, bias),
        [pltpu.VMEM((kvs, 1, QROWS), F32), pltpu.VMEM((kvs, 1, QROWS), F32), pltpu.VMEM((kvs, HD, QROWS), F32)],
        ("parallel", "parallel"), rider)


def _attn_delta(ot, dot_):
    def body(o_ref, do_ref, dl_ref):
        dl_ref[...] = jnp.sum(o_ref[...] * do_ref[...].astype(F32), axis=1, keepdims=True)

    spec = pl.BlockSpec((None, NCH, HD, QROWS), lambda j: (j, 0, 0, 0))
    return pl.pallas_call(
        body, name="attn_delta", grid=(NKV,), in_specs=[spec, spec],
        out_specs=pl.BlockSpec((None, NCH, 1, QROWS), lambda j: (j, 0, 0, 0)),
        out_shape=jax.ShapeDtypeStruct((NKV, NCH, 1, QROWS), F32), compiler_params=_cparams("parallel"),
    )(ot, dot_)


def _attn_bwd(qt, kh, kt, vh, dot_, lse, delta, bias, rider=None):
    def body(qt_ref, k_ref, kt_ref, v_ref, dot_ref, lse_ref, dl_ref, b_ref, dq_ref, dk_ref, dv_ref):
        kb = pl.program_id(1)

        @pl.when(kb == 0)
        def _():
            dq_ref[...] = jnp.zeros_like(dq_ref)

        def blocks(carry, qbs):
            out = list(carry)
            for h in range(KV_PER_STEP):
                k, kt_, v = k_ref[h], kt_ref[h], v_ref[h]
                for qb in qbs:
                    st = _dot(k, qt_ref[h, qb], NN) + b_ref[qb - kb + 1]
                    pt = jnp.exp(st - lse_ref[h, qb])
                    dst = pt * (_dot(v, dot_ref[h, qb], NN) - dl_ref[h, qb])
                    dq_ref[h, qb] += _dot(kt_, dst, NN)
                    out[2 * h] = out[2 * h] + _dot(qt_ref[h, qb], dst, NT)
                    out[2 * h + 1] = out[2 * h + 1] + _dot(dot_ref[h, qb], pt, NT)
            return tuple(out)

        res = (jnp.zeros((HD, 128), F32),) * (2 * KV_PER_STEP)
        res = lax.cond(kb % 2 == 1, lambda c: blocks(c, (kb,)), lambda c: c, res)
        res = lax.fori_loop((kb + 1) // 2, NCH // 2, lambda j, c: blocks(c, (2 * j, 2 * j + 1)), res)
        for h in range(KV_PER_STEP):
            dk_ref[h] = res[2 * h]
            dv_ref[h] = res[2 * h + 1]

    kvs = KV_PER_STEP
    tspec = pl.BlockSpec((kvs, NCH, HD, QROWS), lambda j, i: (j, 0, 0, 0))
    kspec = pl.BlockSpec((kvs, 128, HD), lambda j, i: (j, i, 0))
    ktspec = pl.BlockSpec((kvs, HD, 128), lambda j, i: (j, 0, i))
    sspec = pl.BlockSpec((kvs, NCH, 1, QROWS), lambda j, i: (j, 0, 0, 0))
    return _call(
        body, "attn_bwd", (NKV // kvs, NCH),
        [tspec, kspec, ktspec, kspec, tspec, sspec, sspec,
         pl.BlockSpec((NBIAS, 128, QROWS), lambda j, i: (0, 0, 0))],
        [tspec, ktspec, ktspec],
        [jax.ShapeDtypeStruct((NKV, NCH, HD, QROWS), F32),
         jax.ShapeDtypeStruct((NKV, HD, S), F32), jax.ShapeDtypeStruct((NKV, HD, S), F32)],
        (qt, kh, kt, vh, dot_, lse, delta, bias), sem=("parallel", "arbitrary"), rider=rider)


CONV_BLK = 256
CONV_COL0 = 1536 // CONV_BLK


def _shift_down(u, j, row):
    return jnp.where(row >= j, pltpu.roll(u, j, 0), 0.0)


def _conv_pre(u, w_ref, b_ref, row):
    y = b_ref[...] + w_ref[CONV_K - 1:CONV_K, :] * u
    for j in range(1, CONV_K):
        y = y + w_ref[CONV_K - 1 - j:CONV_K - j, :] * _shift_down(u, j, row)
    return y


def _conv_fwd(proj, convw, convb):
    def body(u_ref, w_ref, b_ref, o_ref):
        u = u_ref[...]
        row = lax.broadcasted_iota(jnp.int32, u.shape, 0)
        y = _conv_pre(u, w_ref, b_ref, row)
        o_ref[...] = y * _sigmoid(y)

    return pl.pallas_call(
        body, name="conv_fwd", grid=(CONV_C // CONV_BLK,),
        in_specs=[pl.BlockSpec((S, CONV_BLK), lambda i: (0, CONV_COL0 + i)),
                  pl.BlockSpec((CONV_K, CONV_BLK), lambda i: (0, i)),
                  pl.BlockSpec((1, CONV_BLK), lambda i: (0, i))],
        out_specs=pl.BlockSpec((S, CONV_BLK), lambda i: (0, i)),
        out_shape=jax.ShapeDtypeStruct((S, CONV_C), F32), compiler_params=_cparams("parallel"),
    )(proj, convw, convb)


def _conv_bwd(dact, proj, convw, convb, dproj):
    def body(da_ref, u_ref, w_ref, b_ref, buf_ref, du_ref, dw_ref, db_ref):
        u = u_ref[...]
        row = lax.broadcasted_iota(jnp.int32, u.shape, 0)
        y = _conv_pre(u, w_ref, b_ref, row)
        sg = _sigmoid(y)
        dy = da_ref[...] * (sg * (1.0 + y * (1.0 - sg)))
        db_ref[...] = jnp.sum(dy, axis=0, keepdims=True)
        du = w_ref[CONV_K - 1:CONV_K, :] * dy
        r8 = lax.broadcasted_iota(jnp.int32, (8, CONV_BLK), 0)
        dw = jnp.where(r8 == CONV_K - 1, jnp.sum(dy * u, axis=0, keepdims=True), 0.0)
        for j in range(1, CONV_K):
            du = du + w_ref[CONV_K - 1 - j:CONV_K - j, :] * jnp.where(row < S - j, pltpu.roll(dy, S - j, 0), 0.0)
            dw = dw + jnp.where(r8 == CONV_K - 1 - j,
                                jnp.sum(dy * _shift_down(u, j, row), axis=0, keepdims=True), 0.0)
        du_ref[...] = du.astype(BF16)
        dw_ref[...] = dw

    return pl.pallas_call(
        body, name="conv_bwd", grid=(CONV_C // CONV_BLK,),
        in_specs=[pl.BlockSpec((S, CONV_BLK), lambda i: (0, i)),
                  pl.BlockSpec((S, CONV_BLK), lambda i: (0, CONV_COL0 + i)),
                  pl.BlockSpec((CONV_K, CONV_BLK), lambda i: (0, i)),
                  pl.BlockSpec((1, CONV_BLK), lambda i: (0, i)), pl.BlockSpec(memory_space=pl.ANY)],
        out_specs=[pl.BlockSpec((S, CONV_BLK), lambda i: (0, CONV_COL0 + i)),
                   pl.BlockSpec((8, CONV_BLK), lambda i: (0, i)), pl.BlockSpec((1, CONV_BLK), lambda i: (0, i))],
        out_shape=[jax.ShapeDtypeStruct(dproj.shape, BF16), jax.ShapeDtypeStruct((8, CONV_C), F32),
                   jax.ShapeDtypeStruct((1, CONV_C), F32)],
        input_output_aliases={4: 0}, compiler_params=_cparams("parallel"),
    )(dact, proj, convw, convb, dproj)


NPAIR = 8


def _ssd_scalars(dtr_ref, dtb_ref, alog_ref):
    z = dtr_ref[...] + dtb_ref[...]
    dt = jnp.maximum(z, 0.0) + jnp.log(1.0 + jnp.exp(-jnp.abs(z)))
    a = -jnp.exp(alog_ref[...])
    r = lax.broadcasted_iota(jnp.int32, (128, 128), 0)
    c = lax.broadcasted_iota(jnp.int32, (128, 128), 1)
    tri = (r >= c).astype(F32)
    cs = _dot_exact(tri, dt * a)
    return z, dt, a, cs, r, c


def _by_lane(cs, dt):
    head = lax.broadcasted_iota(jnp.int32, (128, SSM_W), 0)
    lane = lax.broadcasted_iota(jnp.int32, (128, SSM_W), 1)
    sel = (head == lane // HD).astype(F32)
    cs_l = _dot_exact(cs, sel, "b")
    last_l = cs_l[127:128, :]
    return sel, jnp.exp(cs_l), jnp.exp(last_l - cs_l), _dot_exact(dt, sel, "b")


def _pair_terms(cs, h1, h2):
    return (cs[:, h1:h1 + 1], cs[:, h2:h2 + 1],
            jnp.exp(cs[127:128, h1:h1 + 1]), jnp.exp(cs[127:128, h2:h2 + 1]))


def _gate_norm(y, zv, w):
    yg = y * (zv * _sigmoid(zv))
    outs, rs = [], []
    for g in range(2):
        blk = yg[:, 512 * g:512 * (g + 1)]
        r = lax.rsqrt(jnp.mean(blk * blk, axis=-1, keepdims=True) + EPS)
        outs.append(blk * r)
        rs.append(r)
    return jnp.concatenate(outs, axis=1), rs, yg


def _ssd_fwd(xbc, proj, dtb, alog, dskip_l, ssmw):
    def body(x_ref, b_ref, c_ref, dtr_ref, z_ref, dtb_ref, alog_ref, dsk_ref, w_ref, y_ref, yn_ref, hp_ref, h_ref):
        @pl.when(pl.program_id(0) == 0)
        def _():
            h_ref[...] = jnp.zeros_like(h_ref)

        _, dt, _, cs, r, c = _ssd_scalars(dtr_ref, dtb_ref, alog_ref)
        cst = cs.T
        causal = r >= c
        lo = c < HD
        _, e_all, dte_all, dt_all = _by_lane(cs, dt)
        hp_ref[...] = h_ref[...]
        for g in range(2):
            bg = b_ref[:, 128 * g:128 * (g + 1)]
            cg = c_ref[:, 128 * g:128 * (g + 1)]
            cb = _dot(cg, bg, NT)
            for j in range(4):
                pj = 4 * g + j
                h1, h2 = 2 * pj, 2 * pj + 1
                sl = slice(128 * pj, 128 * (pj + 1))
                xp = x_ref[:, sl]
                c1, c2, cd1, cd2 = _pair_terms(cs, h1, h2)
                e_l, dte_l = e_all[:, sl], dte_all[:, sl]
                xdt = xp * dt_all[:, sl]
                m1 = cb * jnp.exp(jnp.where(causal, c1 - cst[h1:h1 + 1, :], NEG))
                m2 = cb * jnp.exp(jnp.where(causal, c2 - cst[h2:h2 + 1, :], NEG))
                yd = jnp.where(lo, _dot(m1, xdt, NN), _dot(m2, xdt, NN))
                hp = h_ref[pj]
                yo = _dot(cg, hp, NT) * e_l
                st = _dot(xdt * dte_l, bg, TN)
                h_ref[pj] = hp * jnp.where(r < HD, cd1, cd2) + st
                y_ref[:, sl] = yd + yo + dsk_ref[:, sl] * xp
        yn, _, _ = _gate_norm(y_ref[...], z_ref[...], w_ref[...])
        yn_ref[...] = (yn * w_ref[...]).astype(BF16)

    return pl.pallas_call(
        body, name="ssd_fwd", grid=(NCH,),
        in_specs=[pl.BlockSpec((128, SSM_W), lambda i: (i, 0)),
                  pl.BlockSpec((128, 256), lambda i: (i, 4)), pl.BlockSpec((128, 256), lambda i: (i, 5)),
                  pl.BlockSpec((128, 128), lambda i: (i, COL_DT // 128)),
                  pl.BlockSpec((128, SSM_W), lambda i: (i, 3)),
                  pl.BlockSpec((1, 128), lambda i: (0, 0)), pl.BlockSpec((1, 128), lambda i: (0, 0)),
                  pl.BlockSpec((1, SSM_W), lambda i: (0, 0)), pl.BlockSpec((1, SSM_W), lambda i: (0, 0))],
        out_specs=[pl.BlockSpec((128, SSM_W), lambda i: (i, 0)), pl.BlockSpec((128, SSM_W), lambda i: (i, 0)),
                   pl.BlockSpec((None, NPAIR, 128, 128), lambda i: (i, 0, 0, 0))],
        out_shape=[jax.ShapeDtypeStruct((S, SSM_W), F32), jax.ShapeDtypeStruct((S, SSM_W), BF16),
                   jax.ShapeDtypeStruct((NCH, NPAIR, 128, 128), F32)],
        scratch_shapes=[pltpu.VMEM((NPAIR, 128, 128), F32)],
        compiler_params=_cparams("arbitrary"),
    )(xbc, xbc, xbc, proj, proj, dtb, alog, dskip_l, ssmw)


def _ssd_bwd(dmixed, y, xbc, proj, hprev, dtb, alog, dskip_l, ssmw, rider=None):
    def body(dyn_ref, y_ref, x_ref, b_ref, c_ref, dtr_ref, z_ref, hp_ref, dtb_ref, alog_ref, dsk_ref, w_ref,
             dxbc_ref, dz_ref, ddt_ref, dw_ref, dsc_ref, g_ref):
        @pl.when(pl.program_id(0) == 0)
        def _():
            g_ref[...] = jnp.zeros_like(g_ref)
            dsc_ref[...] = jnp.zeros_like(dsc_ref)

        z, dt, a, cs, r, c = _ssd_scalars(dtr_ref, dtb_ref, alog_ref)
        cst = cs.T
        causal = r >= c
        lo = c < HD

        yv = y_ref[...]
        zv = z_ref[...]
        wv = w_ref[...]
        ygn, rs, yg = _gate_norm(yv, zv, wv)
        dyn = dyn_ref[...]
        _acc_rows(dw_ref, dyn * ygn)
        dynw = dyn * wv
        parts = []
        for g in range(2):
            sl = slice(512 * g, 512 * (g + 1))
            a_g, n_g = dynw[:, sl], ygn[:, sl]
            parts.append(rs[g] * (a_g - n_g * jnp.mean(a_g * n_g, axis=-1, keepdims=True)))
        dyg = jnp.concatenate(parts, axis=1)
        sz = _sigmoid(zv)
        dz_ref[...] = (dyg * yv * (sz * (1.0 + zv * (1.0 - sz)))).astype(BF16)
        dy_all = dyg * (zv * sz)

        dcs_cols = jnp.zeros((128, 128), F32)
        dcs_rows = jnp.zeros((128, 128), F32)
        sel, e_all, dte_all, dt_all = _by_lane(cs, dt)
        x_all, b_all, c_all, dsk_all = x_ref[...], b_ref[...], c_ref[...], dsk_ref[...]
        hp_all, g_all = hp_ref[...], g_ref[...]
        g_new, dx_parts, db_parts, dc_parts = [], [], [], []
        dyx_parts, ryo_parts, qx_parts, dxx_parts, gh_parts = [], [], [], [], []
        for g in range(2):
            bg = b_all[:, 128 * g:128 * (g + 1)]
            cg = c_all[:, 128 * g:128 * (g + 1)]
            cb = _dot(cg, bg, NT)
            dcb = jnp.zeros((128, 128), F32)
            db_acc = jnp.zeros((128, NST), F32)
            dc_acc = jnp.zeros((128, NST), F32)
            for j in range(4):
                pj = 4 * g + j
                h1, h2 = 2 * pj, 2 * pj + 1
                sl = slice(128 * pj, 128 * (pj + 1))
                xp = x_all[:, sl]
                dyp = dy_all[:, sl]
                c1, c2, cd1, cd2 = _pair_terms(cs, h1, h2)
                e_l, dte_l, dt_l = e_all[:, sl], dte_all[:, sl], dt_all[:, sl]
                xdt = xp * dt_l
                hp = hp_all[pj]
                gp = g_all[pj]
                dyx_parts.append(dyp * xp)
                dzs = dyp * e_l
                dc_acc = dc_acc + _dot(dzs, hp, NN)
                ryo_parts.append(dyp * (_dot(cg, hp, NT) * e_l))
                qm = _dot(bg, gp, NT)
                dxdt = qm * dte_l
                qx_parts.append(qm * xdt)
                db_acc = db_acc + _dot(xdt * dte_l, gp, NN)
                gh_parts.append(gp * hp)
                g_new.append(_dot(dzs, cg, TN) + jnp.where(r < HD, cd1, cd2) * gp)
                for hh, ch, msk in ((h1, c1, lo), (h2, c2, jnp.logical_not(lo))):
                    lm = jnp.exp(jnp.where(causal, ch - cst[hh:hh + 1, :], NEG))
                    mm = cb * lm
                    dm = jnp.where(causal, _dot(jnp.where(msk, dyp, 0.0), xdt, NT), 0.0)
                    w = dm * mm
                    dcs_cols = dcs_cols + jnp.where(c == hh, jnp.sum(w, axis=1, keepdims=True), 0.0)
                    dcs_rows = dcs_rows + jnp.where(r == hh, jnp.sum(w, axis=0, keepdims=True), 0.0)
                    dcb = dcb + dm * lm
                    dxdt = dxdt + jnp.where(msk, _dot(mm, dyp, TN), 0.0)
                dxx_parts.append(dxdt * xp)
                dx_parts.append(dsk_all[:, sl] * dyp + dxdt * dt_l)
            db_parts.append(db_acc + _dot(dcb, cg, TN))
            dc_parts.append(dc_acc + _dot(dcb, bg, NN))
        g_ref[...] = jnp.stack(g_new)
        dxbc_ref[...] = jnp.concatenate(dx_parts + db_parts + dc_parts, axis=1)

        selt = (lax.broadcasted_iota(jnp.int32, (SSM_W, 128), 0) // HD
                == lax.broadcasted_iota(jnp.int32, (SSM_W, 128), 1)).astype(F32)

        def by_head(parts):
            return _dot_exact(jnp.concatenate(parts, axis=1), selt, "b")

        ddt_x = by_head(dxx_parts)
        dd_row = jnp.sum(by_head(dyx_parts), axis=0, keepdims=True)
        t_all = by_head(qx_parts) * jnp.exp(cs[127:128, :] - cs)
        gh = jnp.sum(_dot_exact(sel, jnp.concatenate(gh_parts, axis=0)), axis=1, keepdims=True)
        gh_row = jnp.broadcast_to(gh, (128, 128)).T[0:1, :]
        at_end = jnp.sum(t_all, axis=0, keepdims=True) + gh_row * jnp.exp(cs[127:128, :])
        dcs = by_head(ryo_parts) - t_all + dcs_cols + jnp.where(r == 127, at_end, 0.0) - dcs_rows.T
        dad = _dot_exact((c >= r).astype(F32), dcs)
        ddt = dad * a + ddt_x
        ddtr = jnp.where(c < 16, ddt * _sigmoid(z), 0.0)
        ddt_ref[...] = ddtr.astype(BF16)
        r8 = lax.broadcasted_iota(jnp.int32, (8, 128), 0)
        dsc_ref[...] += (jnp.where(r8 == 0, jnp.sum(ddtr, axis=0, keepdims=True), 0.0)
                         + jnp.where(r8 == 1, jnp.sum(dad * dt, axis=0, keepdims=True) * a, 0.0)
                         + jnp.where(r8 == 2, dd_row, 0.0))

    rev = NCH - 1
    return _call(
        body, "ssd_bwd", (NCH,),
        [pl.BlockSpec((128, SSM_W), lambda i: (rev - i, 0)),
         pl.BlockSpec((128, SSM_W), lambda i: (rev - i, 0)),
         pl.BlockSpec((128, SSM_W), lambda i: (rev - i, 0)),
         pl.BlockSpec((128, 256), lambda i: (rev - i, 4)), pl.BlockSpec((128, 256), lambda i: (rev - i, 5)),
         pl.BlockSpec((128, 128), lambda i: (rev - i, COL_DT // 128)),
         pl.BlockSpec((128, SSM_W), lambda i: (rev - i, 3)),
         pl.BlockSpec((None, NPAIR, 128, 128), lambda i: (rev - i, 0, 0, 0)),
         pl.BlockSpec((1, 128), lambda i: (0, 0)), pl.BlockSpec((1, 128), lambda i: (0, 0)),
         pl.BlockSpec((1, SSM_W), lambda i: (0, 0)), pl.BlockSpec((1, SSM_W), lambda i: (0, 0))],
        [pl.BlockSpec((128, CONV_C), lambda i: (rev - i, 0)),
         pl.BlockSpec((128, SSM_W), lambda i: (rev - i, 3)),
         pl.BlockSpec((128, 128), lambda i: (rev - i, 0)),
         pl.BlockSpec((1, SSM_W), lambda i: (0, 0)), pl.BlockSpec((8, 128), lambda i: (0, 0))],
        [jax.ShapeDtypeStruct((S, CONV_C), F32), jax.ShapeDtypeStruct((S, WIN_PAD), BF16),
         jax.ShapeDtypeStruct((S, 128), BF16), jax.ShapeDtypeStruct((1, SSM_W), F32),
         jax.ShapeDtypeStruct((8, 128), F32)],
        (dmixed, y, xbc, xbc, xbc, proj, proj, hprev, dtb, alog, dskip_l, ssmw),
        [pltpu.VMEM((NPAIR, 128, 128), F32)], ("arbitrary",), rider)


def _cast_stack(name, slot, arrs, tr, tc):
    n = len(arrs)
    rows, cols = arrs[0].shape

    def body(s_ref, *refs):
        for i in range(n):
            refs[n][i] = refs[i][...].astype(BF16)

    return pl.pallas_call(
        body, name=name,
        grid_spec=pltpu.PrefetchScalarGridSpec(
            num_scalar_prefetch=1, grid=(rows // tr, cols // tc),
            in_specs=[pl.BlockSpec((tr, tc), lambda i, j, sr: (i, j))] * n,
            out_specs=pl.BlockSpec((None, n, tr, tc), lambda i, j, sr: (sr[0], 0, i, j))),
        out_shape=jax.ShapeDtypeStruct((NSH, n, rows, cols), BF16),
        compiler_params=_cparams("parallel", "parallel"),
    )(slot, *arrs)


def _pair_sum(name, c_idx, ps, th):
    n = len(ps)
    _, rows, _ = ps[0].shape

    def body(c_ref, *refs):
        mine, whole, out, theirs = refs[:n], refs[n:2 * n], refs[2 * n:3 * n], refs[3 * n:4 * n]
        send, recv = refs[4 * n], refs[4 * n + 1]
        s, i = pl.program_id(0), pl.program_id(1)
        x, y, c, _ = _place()

        def copies(slot):
            return [_rcopy(whole[k].at[slot, :, pl.ds((1 - c) * HALF, HALF)], theirs[k].at[slot],
                           send.at[slot * n + k], recv.at[slot * n + k], (x, y, 1 - c)) for k in range(n)]

        @pl.when((s == 0) & (i == 0))
        def _():
            for slot in range(NSH):
                for cp in copies(slot):
                    cp.start()

        @pl.when(i == 0)
        def _():
            for slot in range(NSH):
                @pl.when(s == slot)
                def _():
                    for cp in copies(slot):
                        cp.wait()

        rows_i = slice(None) if th == rows else pl.ds(pl.multiple_of(i * th, th), th)
        for k in range(n):
            out[k][...] = (mine[k][...].astype(F32) + theirs[k][s, rows_i, :].astype(F32)).astype(BF16)

    spec = pl.BlockSpec((None, th, HALF), lambda s, i, cr: (s, i, 0))
    return pl.pallas_call(
        body, name=name,
        grid_spec=pltpu.PrefetchScalarGridSpec(
            num_scalar_prefetch=1, grid=(NSH, rows // th),
            in_specs=[pl.BlockSpec((None, th, HALF), lambda s, i, cr: (s, i, cr[0]))] * n + _any_specs(n),
            out_specs=[spec] * n,
            scratch_shapes=[pltpu.VMEM((NSH, rows, HALF), BF16)] * n
            + [pltpu.SemaphoreType.DMA((NSH * n,)), pltpu.SemaphoreType.DMA((NSH * n,))]),
        out_shape=[jax.ShapeDtypeStruct((NSH, rows, HALF), BF16)] * n,
        compiler_params=_cparams("arbitrary", "arbitrary"),
    )(c_idx, *ps, *ps)


def _chip_sum(name, place, cs, ts, th):
    n = len(ts)
    _, rows, _ = ts[0].shape

    def body(p_ref, *refs):
        for i in range(n):
            t = refs[n + i][...].astype(F32)
            refs[2 * n + i][...] = ((refs[i][...].astype(F32) + t[0]) + t[1]) + t[2]

    return pl.pallas_call(
        body, name=name,
        grid_spec=pltpu.PrefetchScalarGridSpec(
            num_scalar_prefetch=1, grid=(rows // th,),
            in_specs=[pl.BlockSpec((None, th, HALF), lambda i, pr: (pr[0], i, 0))] * n
            + [pl.BlockSpec((3, th, HALF), lambda i, pr: (0, i, 0))] * n,
            out_specs=[pl.BlockSpec((th, HALF), lambda i, pr: (i, pr[1]))] * n),
        out_shape=[jax.ShapeDtypeStruct((rows, D), F32)] * n, compiler_params=_cparams("parallel"),
    )(place, *cs, *ts)


def _adamw(name, ws, gs, ms, vs, tr, tc):
    n = len(ws)
    shape = ws[0].shape
    rows, cols, mid = shape[0], shape[-1], shape[1:-1]
    c1 = 1.0 / (1.0 - ADAM_B1 ** ADAM_STEP)
    c2 = 1.0 / (1.0 - ADAM_B2 ** ADAM_STEP)

    def body(*refs):
        for i in range(n):
            w, g, m, v = (refs[k * n + i][...] for k in range(4))
            m2 = ADAM_B1 * m + (1.0 - ADAM_B1) * g
            v2 = ADAM_B2 * v + (1.0 - ADAM_B2) * (g * g)
            refs[4 * n + 4 * i][...] = -ADAM_LR * ((m2 * c1) / (jnp.sqrt(v2 * c2) + ADAM_EPS) + ADAM_WD * w)
            refs[4 * n + 4 * i + 1][...] = m2
            refs[4 * n + 4 * i + 2][...] = v2
            refs[4 * n + 4 * i + 3][...] = g

    spec = pl.BlockSpec((tr,) + mid + (tc,), lambda i, j: (i,) + (0,) * len(mid) + (j,))
    outs = pl.pallas_call(
        body, name=name, grid=(rows // tr, cols // tc), in_specs=[spec] * (4 * n), out_specs=[spec] * (4 * n),
        out_shape=[jax.ShapeDtypeStruct(shape, F32)] * (4 * n),
        compiler_params=_cparams("parallel", "parallel"),
    )(*ws, *gs, *ms, *vs)
    return [tuple(outs[4 * i:4 * i + 4]) for i in range(n)]


def _place():
    x, y, c = lax.axis_index("x"), lax.axis_index("y"), lax.axis_index("c")
    chips = [(1 - x, y), (x, 1 - y), (1 - x, 1 - y)]
    return x, y, c, chips


def _any_specs(n):
    return [pl.BlockSpec(memory_space=pl.ANY)] * n


def _rcopy(src, dst, send_sem, recv_sem, dev):
    return pltpu.make_async_remote_copy(src_ref=src, dst_ref=dst, send_sem=send_sem, recv_sem=recv_sem,
                                        device_id=dev, device_id_type=MESH)


QUARTER = HALF // 2

TO_X, TO_Y, RELAY_X, RELAY_Y, FWD_X, FWD_Y, FWD_D0, FWD_D1 = range(8)


def _gather_rider(bufs, views):
    n = len(bufs)

    def plan(rout, sems):
        send, recv = sems
        x, y, c, _ = _place()
        me, sx, sy, sd = 2 * x + y, 2 * (1 - x) + y, 2 * x + (1 - y), 2 * (1 - x) + (1 - y)
        nx, ny, sib = (1 - x, y, c), (x, 1 - y, c), (x, y, 1 - c)
        mine, other = c * HALF, (1 - c) * HALF
        out = {TO_X: (me, mine, HALF, nx), TO_Y: (me, mine, HALF, ny),
               RELAY_X: (sy, mine, QUARTER, nx), RELAY_Y: (sx, mine + QUARTER, QUARTER, ny),
               FWD_X: (sx, mine, HALF, sib), FWD_Y: (sy, mine, HALF, sib),
               FWD_D0: (sd, mine, QUARTER, sib), FWD_D1: (sd, mine + QUARTER, QUARTER, sib)}
        inn = {TO_X: (sx, mine, HALF), TO_Y: (sy, mine, HALF),
               RELAY_X: (sd, mine, QUARTER), RELAY_Y: (sd, mine + QUARTER, QUARTER),
               FWD_X: (sx, other, HALF), FWD_Y: (sy, other, HALF),
               FWD_D0: (sd, other, QUARTER), FWD_D1: (sd, other + QUARTER, QUARTER)}

        def copy(kind, b):
            slot, col, ncols, dev = out[kind]
            win = views[b](rout[b], slot, col, ncols)
            return _rcopy(win, win, send.at[kind * n + b], recv.at[kind * n + b], dev)

        def land(kind, b):
            slot, col, ncols = inn[kind]
            win = views[b](rout[b], slot, col, ncols)
            return _rcopy(win, win, send.at[kind * n + b], recv.at[kind * n + b], (x, y, c))

        return copy, land

    def start(rin, rout, sems):
        copy, _ = plan(rout, sems)
        for kind in (TO_X, TO_Y):
            for b in range(n):
                copy(kind, b).start()

    def finish(rin, rout, sems):
        copy, land = plan(rout, sems)
        for landed, then in ((TO_X, (FWD_X, RELAY_Y)), (TO_Y, (FWD_Y, RELAY_X)),
                             (RELAY_X, (FWD_D0,)), (RELAY_Y, (FWD_D1,))):
            for b in range(n):
                land(landed, b).wait_recv()
                for kind in then:
                    copy(kind, b).start()
        for kind in (FWD_X, FWD_Y, FWD_D0, FWD_D1):
            for b in range(n):
                land(kind, b).wait_recv()
        for kind in range(8):
            for b in range(n):
                copy(kind, b).wait_send()

    return _Rider(list(bufs), [jax.ShapeDtypeStruct(a.shape, a.dtype) for a in bufs], {b: b for b in range(n)},
                  [pltpu.SemaphoreType.DMA((8 * n,))] * 2, start, finish)


def _small_gather_rider(cw):
    def descs(rin, rout, sems, x, y, c, chips):
        return [_rcopy(rin[0], rout[0].at[2 * x + y], sems[1].at[j], sems[2].at[j], (chip[0], chip[1], c))
                for j, chip in enumerate(chips)]

    def start(rin, rout, sems):
        x, y, c, chips = _place()
        pltpu.make_async_copy(rin[0], rout[0].at[2 * x + y], sems[0].at[0]).start()
        for cp in descs(rin, rout, sems, x, y, c, chips):
            cp.start()

    def finish(rin, rout, sems):
        x, y, c, chips = _place()
        for j, chip in enumerate(chips):
            _rcopy(rin[0], rout[0].at[2 * chip[0] + chip[1]], sems[1].at[j], sems[2].at[j], (x, y, c)).wait_recv()
        for cp in descs(rin, rout, sems, x, y, c, chips):
            cp.wait_send()
        pltpu.make_async_copy(rin[0], rout[0].at[2 * x + y], sems[0].at[0]).wait()

    return _Rider([cw], [jax.ShapeDtypeStruct((NSH,) + cw.shape, cw.dtype)], {},
                  [pltpu.SemaphoreType.DMA((1,)), pltpu.SemaphoreType.DMA((3,)), pltpu.SemaphoreType.DMA((3,))],
                  start, finish)


def _to_chips_rider(cs):
    n = len(cs)

    def descs(rin, rout, sems):
        x, y, c, chips = _place()
        return [_rcopy(rin[i].at[2 * chip[0] + chip[1]], rout[i].at[j], sems[0].at[j * n + i], sems[1].at[j * n + i],
                       (chip[0], chip[1], c)) for j, chip in enumerate(chips) for i in range(n)]

    def start(rin, rout, sems):
        for cp in descs(rin, rout, sems):
            cp.start()

    def finish(rin, rout, sems):
        for cp in descs(rin, rout, sems):
            cp.wait()

    return _Rider(list(cs), [jax.ShapeDtypeStruct((3,) + a.shape[1:], a.dtype) for a in cs], {},
                  [pltpu.SemaphoreType.DMA((3 * n,))] * 2, start, finish)


def _run_riders(name, riders):
    n_in = [len(r.operands) for r in riders]
    n_out = [len(r.out_shapes) for r in riders]
    n_sem = [len(r.sems) for r in riders]

    def body(*refs):
        parts, at = [], 0
        for counts in (n_in, n_out, n_sem):
            group = []
            for k in counts:
                group.append(refs[at:at + k])
                at += k
            parts.append(group)
        for i, r in enumerate(riders):
            r.start(parts[0][i], parts[1][i], parts[2][i])
        for i, r in enumerate(riders):
            r.finish(parts[0][i], parts[1][i], parts[2][i])

    aliases = {}
    for i, r in enumerate(riders):
        for k, v in r.aliases.items():
            aliases[sum(n_in[:i]) + k] = sum(n_out[:i]) + v
    res = pl.pallas_call(
        body, name=name, in_specs=_any_specs(sum(n_in)), out_specs=_any_specs(sum(n_out)),
        out_shape=[s for r in riders for s in r.out_shapes], input_output_aliases=aliases,
        scratch_shapes=[s for r in riders for s in r.sems],
    )(*[a for r in riders for a in r.operands])
    out, at = [], 0
    for k in n_out:
        out.append(list(res[at:at + k]))
        at += k
    return out


def _swap_halves(gs):
    n = len(gs)

    def body(*refs):
        dst, send, recv = refs[n:2 * n], refs[2 * n], refs[2 * n + 1]
        x, y, c, _ = _place()
        cps = []
        for i in range(n):
            mine = dst[i].at[:, pl.ds(c * HALF, HALF)]
            cps.append(pltpu.make_async_remote_copy(
                src_ref=mine, dst_ref=mine, send_sem=send.at[i], recv_sem=recv.at[i],
                device_id=(x, y, 1 - c), device_id_type=MESH))
        for cp in cps:
            cp.start()
        for i in range(n):
            other = dst[i].at[:, pl.ds((1 - c) * HALF, HALF)]
            pltpu.make_async_remote_copy(
                src_ref=other, dst_ref=other, send_sem=send.at[i], recv_sem=recv.at[i],
                device_id=(x, y, c), device_id_type=MESH).wait_recv()
        for cp in cps:
            cp.wait_send()

    return pl.pallas_call(
        body, name="grads_swap_halves", in_specs=_any_specs(n), out_specs=_any_specs(n),
        out_shape=[jax.ShapeDtypeStruct(g.shape, g.dtype) for g in gs],
        input_output_aliases={i: i for i in range(n)},
        scratch_shapes=[pltpu.SemaphoreType.DMA((n,)), pltpu.SemaphoreType.DMA((n,))],
    )(*gs)


SMALL_ROWS = 16


def _allreduce_small(vec):
    def body(v_ref, o_ref, buf, send, recv):
        x, y, c, _ = _place()
        me = 4 * x + 2 * y + c
        buf[me] = v_ref[...]
        cps = []
        for k in range(1, 8):
            peer = (x ^ (k >> 2), y ^ ((k >> 1) & 1), c ^ (k & 1))
            cps.append(pltpu.make_async_remote_copy(
                src_ref=v_ref, dst_ref=buf.at[me], send_sem=send.at[k - 1], recv_sem=recv.at[k - 1],
                device_id=peer, device_id_type=MESH))
        for cp in cps:
            cp.start()
        for k in range(1, 8):
            pltpu.make_async_remote_copy(
                src_ref=v_ref, dst_ref=buf.at[me ^ k], send_sem=send.at[k - 1], recv_sem=recv.at[k - 1],
                device_id=(x, y, c), device_id_type=MESH).wait_recv()
        for cp in cps:
            cp.wait_send()
        t = buf[0]
        for d in range(1, 8):
            t = t + buf[d]
        o_ref[...] = t

    return pl.pallas_call(
        body, name="allreduce_small",
        in_specs=[pl.BlockSpec(memory_space=pltpu.VMEM)], out_specs=pl.BlockSpec(memory_space=pltpu.VMEM),
        out_shape=jax.ShapeDtypeStruct((SMALL_ROWS, D), F32),
        scratch_shapes=[pltpu.VMEM((8, SMALL_ROWS, D), F32), pltpu.SemaphoreType.DMA((7,)),
                        pltpu.SemaphoreType.DMA((7,))],
    )(vec)


def _col_window(ref, slot, col, ncols):
    return ref.at[slot, :, pl.ds(col, ncols)]


def _stack_window(ref, slot, col, ncols):
    return ref.at[slot, :, :, pl.ds(col, ncols)]


def _row_tile(rows):
    for t in range(512, 15, -16):
        if rows % t == 0:
            return t
    return rows


def _same_shape_runs(arrs):
    runs, a = [], 0
    for b in range(1, len(arrs) + 1):
        if b == len(arrs) or arrs[b].shape != arrs[a].shape:
            runs.append((a, b))
            a = b
    return runs


class _Comm:
    def __init__(self):
        x, y, c = lax.axis_index("x"), lax.axis_index("y"), lax.axis_index("c")
        self.c_idx = jnp.reshape(c, (1,)).astype(jnp.int32)
        self.place = jnp.stack([2 * x + y, c]).astype(jnp.int32)
        self.groups = {}

    @staticmethod
    def gather(*bufs):
        return _gather_rider(list(bufs), [_col_window if b.ndim == 3 else _stack_window for b in bufs])

    def reduce_rider(self, tag, names, ps):
        csums = []
        for a, b in _same_shape_runs(ps):
            csums += _pair_sum("pair_sum_%s%d" % (tag, a), self.c_idx, ps[a:b], _row_tile(ps[a].shape[1]))
        self.groups[tag] = [names, csums, None]
        return _to_chips_rider(csums)

    def landed(self, tag, ts):
        self.groups[tag][2] = ts

    def finish(self):
        names, csums, ts = [], [], []
        for group_names, group_csums, group_ts in self.groups.values():
            names += group_names
            csums += group_csums
            ts += group_ts
        order = sorted(range(len(names)), key=lambda i: csums[i].shape[1])
        names, csums, ts = ([v[i] for i in order] for v in (names, csums, ts))
        halves = []
        for a, b in _same_shape_runs(csums):
            halves += _chip_sum("chip_sum_%d" % a, self.place, csums[a:b], ts[a:b], _row_tile(csums[a].shape[1]))
        return dict(zip(names, _swap_halves(halves)))


ROPE_THETA = 10000.0
SMALL_1K = ("ffn1_pre_norm", "ffn1_post_norm", "mix_pre_norm", "ssm_norm", "mix_post_norm",
            "ffn2_pre_norm", "ffn2_post_norm")
SMALL_16 = ("dt_bias", "a_log", "d_skip")
OFF_CONVB = 7 * D
OFF_16 = OFF_CONVB + CONV_C
OFF_CONVW = OFF_16 + 48
OFF_LOSS = OFF_CONVW + CONV_K * CONV_C
SMALL_LEN = SMALL_ROWS * D


def _sds(shape, dtype):
    return jax.ShapeDtypeStruct(shape, dtype)


def _ridden(res, rider):
    return res if rider is not None else (res, None)


def _ffn_down(name, act, w, tail_of, rider=None):
    tail, o_specs, o_shapes = tail_of(TS)
    return _mm(name, [act, w.dn], NN, (S // TS,),
               [pl.BlockSpec((NSH, TS, FS), lambda i: (0, i, 0)),
                pl.BlockSpec((NSH, None, FS, D), lambda i: (0, w.d0, 0, 0))], o_specs, o_shapes, rider, tail)


def _ffn_dw(name, a, b, rider=None):
    return _mm(name, [a, b], TN, (NSH,),
               [pl.BlockSpec((None, S, FS), lambda s: (s, 0, 0)), pl.BlockSpec((S, D), lambda s: (0, 0))],
               pl.BlockSpec((None, FS, D), lambda s: (s, 0, 0)), _sds((NSH, FS, D), BF16), rider)


def _ffn_dn(name, dgate, dup, w, tail_of, rider=None):
    rows = TS // 2
    tail, o_specs, o_shapes = tail_of(rows)
    a2 = pl.BlockSpec((NSH, rows, FS), lambda i: (0, i, 0))
    return _mm(name, [dgate, w.gu, dup, w.gu], NN, (S // rows,),
               [a2, pl.BlockSpec((NSH, None, FS, D), lambda i: (0, w.g0, 0, 0)),
                a2, pl.BlockSpec((NSH, None, FS, D), lambda i: (0, w.g0 + 1, 0, 0))], o_specs, o_shapes, rider, tail)


def _out_proj_dx(dh, wout):
    def body(dh_ref, w_ref, dyn_ref, do_ref):
        dm = _dot(dh_ref[...], w_ref[...], NT)
        dyn_ref[...] = dm[:, D:]
        for b in range(TS // 128):
            for j, blk in enumerate(_rows_to_blocks(dm[128 * b:128 * (b + 1), :D])):
                do_ref[j, b] = blk.astype(BF16)

    return pl.pallas_call(
        body, name="out_proj_dx", grid=(S // TS,),
        in_specs=[pl.BlockSpec((TS, D), lambda i: (i, 0)), pl.BlockSpec((2 * D, D), lambda i: (0, 0))],
        out_specs=[pl.BlockSpec((TS, D), lambda i: (i, 0)),
                   pl.BlockSpec((NKV, TS // 128, HD, QROWS), lambda i: (0, i, 0, 0))],
        out_shape=[_sds((S, D), F32), _sds((NKV, NCH, HD, QROWS), BF16)], compiler_params=_cparams("parallel"),
    )(dh, wout)


def _heads(t, n):
    return t.reshape(S, n, HD).transpose(1, 0, 2)


def _pad128(v):
    return jnp.pad(v, ((0, 0), (0, 128 - v.shape[1])))


def _local_step(x, positions, tgt, sp, gu1, d1, f2, wint, wout, convw, comm=None):
    inv_freq = ROPE_THETA ** (-jnp.arange(0, HD, 2, dtype=F32) / HD)
    ang = positions.astype(F32)[:, None] * inv_freq
    ang = jnp.concatenate([ang, ang, ang, ang], axis=-1)
    cos, sin = jnp.cos(ang), jnp.sin(ang)
    dtb, alog = _pad128(sp["dt_bias"]), _pad128(sp["a_log"])
    dskip_l = jnp.repeat(sp["d_skip"], HD, axis=1)
    convb = sp["conv_b"]

    n1 = _prenorm("prenorm1", x, sp["ffn1_pre_norm"])
    rider = comm.gather(d1) if comm else None
    (fg1, fu1, act1), got = _ridden(_ffn_up("ffn1_up", n1, _FfnW(gu1, 0, d1, 0), rider), rider)
    if comm:
        d1, = got
    w1 = _FfnW(gu1, 0, d1, 0)
    rider = comm.gather(wint) if comm else None
    (h1, x1, n2), got = _ridden(_ffn_down(
        "ffn1_down", act1, w1,
        lambda rows: _tail_postres(rows, x, sp["ffn1_post_norm"], 0.5, sp["mix_pre_norm"]), rider), rider)
    if comm:
        wint, = got
    wint_pad = jnp.pad(wint.reshape(WIN_COLS, D), ((0, WIN_PAD - WIN_COLS), (0, 0)))

    pw = WIN_PAD // 3
    proj = _mm("in_proj", [n2, wint_pad], NT, (S // TS, 3),
               [pl.BlockSpec((TS, D), lambda i, j: (i, 0)), pl.BlockSpec((pw, D), lambda i, j: (j, 0))],
               pl.BlockSpec((TS, pw), lambda i, j: (i, j)), _sds((S, WIN_PAD), F32))
    qt = _rope_q(proj, cos, sin)
    k_rot, v_bf, kt, vt = _rope_kv(proj, cos, sin)
    kh, vh = _heads(k_rot, NKV), _heads(v_bf, NKV)
    bias = _bias_table()
    rider = comm.gather(f2, wout) if comm else None
    (ot, lse, attn), got = _ridden(_attn_fwd(qt, kh, vt, bias, rider), rider)
    if comm:
        f2, wout = got
    w2 = _FfnW(f2, 0, f2, 2)
    wout = wout.reshape(2 * D, D)
    xbc = _conv_fwd(proj, convw, convb)
    y, yn, hprev = _ssd_fwd(xbc, proj, dtb, alog, dskip_l, sp["ssm_norm"])
    mixed = jnp.concatenate([attn, yn], axis=1)
    tail, o_specs, o_shapes = _tail_postres(TS, x1, sp["mix_post_norm"], 1.0, sp["ffn2_pre_norm"])
    h2, x2, n3 = _mm("out_proj", [mixed, wout], NN, (S // TS,),
                     [pl.BlockSpec((TS, 2 * D), lambda i: (i, 0)), pl.BlockSpec((2 * D, D), lambda i: (0, 0))],
                     o_specs, o_shapes, None, tail)

    fg2, fu2, act2 = _ffn_up("ffn2_up", n3, w2)
    dy, dh3, dp3, loss = _ffn_down(
        "ffn2_down", act2, w2, lambda rows: _tail_final(rows, x2, sp["ffn2_post_norm"], tgt, 0.5))

    dgate2, dup2 = _ffn_dact("ffn2_dact", dh3, w2, fg2, fu2)
    dws2 = [_ffn_dw("ffn2_dwg", dgate2, n3), _ffn_dw("ffn2_dwu", dup2, n3), _ffn_dw("ffn2_dwd", act2, dh3)]
    dx2, dh2, dg3, dp2 = _ffn_dn(
        "ffn2_dn", dgate2, dup2, w2,
        lambda rows: _tail_mid_bwd(rows, dy, x2, sp["ffn2_pre_norm"], h2, sp["mix_post_norm"], 1.0))

    dyn, dot_ = _out_proj_dx(dh2, wout)
    dwout = _mm("out_proj_dw", [mixed, dh2], TN, (2,),
                [pl.BlockSpec((S, D), lambda m: (0, m)), pl.BlockSpec((S, D), lambda m: (0, 0))],
                pl.BlockSpec((D, D), lambda m: (m, 0)), _sds((2 * D, D), BF16))
    dwout = dwout.reshape(NSH, 2 * D // NSH, D)

    def riding(tag, names, ps, call):
        rider = comm.reduce_rider(tag, names, ps) if comm else None
        res, got = _ridden(call(rider), rider)
        if comm:
            comm.landed(tag, got)
        return res

    dxbc, dproj, ddt, dssm, dsc = _ssd_bwd(dyn, y, xbc, proj, hprev, dtb, alog, dskip_l, sp["ssm_norm"])
    dproj, dcw8, dcb = _conv_bwd(dxbc, proj, convw, convb, dproj)
    delta = _attn_delta(ot, dot_)
    dqt, dkh, dvh = riding("a", BIG[3:6] + ("w_out",), dws2 + [dwout], lambda rider: _attn_bwd(
        qt, kh, kt, vh, dot_, lse, delta, bias, rider))
    dproj = _rope_dq(dqt, cos, sin, dproj)
    dproj = _rope_dkv(dkh, dvh, cos, sin, dproj)
    dproj = lax.dynamic_update_slice(dproj, ddt, (0, COL_DT))
    dwint = _mm("in_proj_dw", [dproj, n2], TN, (3,),
                [pl.BlockSpec((S, pw), lambda j: (0, j)), pl.BlockSpec((S, D), lambda j: (0, 0))],
                pl.BlockSpec((pw, D), lambda j: (j, 0)), _sds((WIN_PAD, D), BF16))
    dwint = dwint[:WIN_COLS].reshape(NSH, WIN_SH, D)

    tail, o_specs, o_shapes = _tail_mid_bwd(TS, dx2, x1, sp["mix_pre_norm"], h1, sp["ffn1_post_norm"], 0.5)
    dx1, dh1, dg2, dp1 = riding("b", ("w_in",), [dwint], lambda rider: _mm(
        "in_proj_dx", [dproj, wint_pad], NN, (S // TS,),
        [pl.BlockSpec((TS, WIN_PAD), lambda i: (i, 0)), pl.BlockSpec((WIN_PAD, D), lambda i: (0, 0))],
        o_specs, o_shapes, rider, tail))

    dwd1 = _ffn_dw("ffn1_dwd", act1, dh1)
    dgate1, dup1 = riding("d", BIG[2:3], [dwd1], lambda rider: _ffn_dact("ffn1_dact", dh1, w1, fg1, fu1, rider))
    dwg1, dwu1 = _ffn_dw("ffn1_dwg", dgate1, n1), _ffn_dw("ffn1_dwu", dup1, n1)
    grad_x, dg1 = riding("g", BIG[0:2], [dwg1, dwu1], lambda rider: _ffn_dn(
        "ffn1_dn", dgate1, dup1, w1, lambda rows: _tail_first_bwd(rows, dx1, x, sp["ffn1_pre_norm"]), rider))
    dws1 = [dwg1, dwu1, dwd1]

    small = jnp.concatenate([
        dg1[0], dp1[0], dg2[0], dssm[0], dp2[0], dg3[0], dp3[0], dcb[0],
        dsc[0, :16], dsc[1, :16], dsc[2, :16], dcw8[:CONV_K].reshape(-1), loss[0, :1]])
    small = jnp.pad(small, (0, SMALL_LEN - small.shape[0])).reshape(SMALL_ROWS, D)
    if comm is None:
        return grad_x, dws1 + dws2 + [dwint, dwout], small
    return grad_x, comm.finish(), small


WEIGHTS = ("ffn1_pre_norm", "ffn1_w_gate", "ffn1_w_up", "ffn1_w_down", "ffn1_post_norm", "mix_pre_norm", "w_in",
           "conv_w", "conv_b", "dt_bias", "a_log", "d_skip", "ssm_norm", "w_out", "mix_post_norm", "ffn2_pre_norm",
           "ffn2_w_gate", "ffn2_w_up", "ffn2_w_down", "ffn2_post_norm")
BIG = ("ffn1_w_gate", "ffn1_w_up", "ffn1_w_down", "ffn2_w_gate", "ffn2_w_up", "ffn2_w_down", "w_in", "w_out")
TRANSPOSED = ("ffn1_w_gate", "ffn1_w_up", "ffn2_w_gate", "ffn2_w_up", "w_in")
SMALL_ORDER = SMALL_1K + ("conv_b",) + SMALL_16
CONVW_SH = CONV_C // NSH


def _shard2d(t, name):
    return t[0].T if name in TRANSPOSED else t[0]


def _unshard2d(t, name):
    return (t.T if name in TRANSPOSED else t)[None]


def _rows3d(t):
    return t.transpose(2, 0, 1)


def _pack_small(d, prefix, shard_of_convw):
    flat = jnp.concatenate([d[prefix + n][0] for n in SMALL_ORDER] + [shard_of_convw.reshape(-1)])
    return jnp.pad(flat, (0, SMALL_LEN - flat.shape[0])).reshape(SMALL_ROWS, D)


def _unpack_small(block, like):
    flat = block.reshape(-1)
    out, off = {}, 0
    for n in SMALL_ORDER:
        size = like[n].shape[1]
        out[n] = flat[off:off + size].reshape(1, size)
        off += size
    out["conv_w"] = flat[off:off + CONV_K * CONVW_SH].reshape(1, CONV_K, CONVW_SH)
    return out


def kernel(x, positions, ffn1_pre_norm, ffn1_w_gate, ffn1_w_up, ffn1_w_down, ffn1_post_norm, mix_pre_norm, w_in, conv_w, conv_b, dt_bias, a_log, d_skip, ssm_norm, w_out, mix_post_norm, ffn2_pre_norm, ffn2_w_gate, ffn2_w_up, ffn2_w_down, ffn2_post_norm, loss_target, m_ffn1_pre_norm, m_ffn1_w_gate, m_ffn1_w_up, m_ffn1_w_down, m_ffn1_post_norm, m_mix_pre_norm, m_w_in, m_conv_w, m_conv_b, m_dt_bias, m_a_log, m_d_skip, m_ssm_norm, m_w_out, m_mix_post_norm, m_ffn2_pre_norm, m_ffn2_w_gate, m_ffn2_w_up, m_ffn2_w_down, m_ffn2_post_norm, v_ffn1_pre_norm, v_ffn1_w_gate, v_ffn1_w_up, v_ffn1_w_down, v_ffn1_post_norm, v_mix_pre_norm, v_w_in, v_conv_w, v_conv_b, v_dt_bias, v_a_log, v_d_skip, v_ssm_norm, v_w_out, v_mix_post_norm, v_ffn2_pre_norm, v_ffn2_w_gate, v_ffn2_w_up, v_ffn2_w_down, v_ffn2_post_norm):
    given = dict(locals())
    xi, yi = lax.axis_index("x"), lax.axis_index("y")

    shard = jnp.reshape(2 * xi + yi, (1,)).astype(jnp.int32)
    big = {p + n: _shard2d(given[p + n], n) for n in BIG for p in ("", "m_", "v_")}
    gu1 = _cast_stack("cast_ffn1_gate_up", shard, [big[n] for n in BIG[0:2]], 176, D)
    d1 = _cast_stack("cast_ffn1_down", shard, [big[BIG[2]]], 176, D)
    f2 = _cast_stack("cast_ffn2", shard, [big[n] for n in BIG[3:6]], 176, D)
    winsh = _cast_stack("cast_w_in", shard, [big["w_in"]], WIN_SH, 256).reshape(NSH, WIN_SH, D)
    woutsh = _cast_stack("cast_w_out", shard, [big["w_out"]], 256, D).reshape(NSH, 2 * D // NSH, D)
    comm = _Comm()
    (gu1,), (cwf,) = _run_riders("gather_ffn1_gate_up", [comm.gather(gu1), _small_gather_rider(conv_w[0])])
    convw = cwf.transpose(1, 0, 2).reshape(CONV_K, CONV_C)

    sp = {n: given[n] for n in SMALL_ORDER}
    grad_x, big_grads, small = _local_step(x[0], positions[0], loss_target[0], sp, gu1, d1, f2, winsh, woutsh,
                                           convw, comm)

    tot = _allreduce_small(small).reshape(-1)
    loss = tot[OFF_LOSS]
    small_grads, off = {}, 0
    for n in SMALL_ORDER:
        size = given[n].shape[1]
        small_grads[n] = tot[off:off + size].reshape(1, size)
        off += size
    dconvw = tot[OFF_CONVW:OFF_CONVW + CONV_K * CONV_C].reshape(CONV_K, NSH, CONVW_SH)
    dconvw = lax.dynamic_index_in_dim(dconvw, 2 * xi + yi, axis=1, keepdims=False)
    small_grads["conv_w"] = dconvw.reshape(1, CONV_K, CONVW_SH)

    upd = {}
    for names, tr in ((BIG[0:3], 176), (BIG[3:6], 176), (BIG[7:8], 256)):
        res = _adamw("adamw_" + names[0], [big[n] for n in names], [big_grads[n] for n in names],
                     [big["m_" + n] for n in names], [big["v_" + n] for n in names], tr, D)
        for n, r in zip(names, res):
            upd[n] = tuple(_unshard2d(t, n) for t in r)
    g_win = big_grads["w_in"].reshape(WIN_SH, 1, D)
    res, = _adamw("adamw_w_in", [_rows3d(w_in)], [g_win], [_rows3d(m_w_in)], [_rows3d(v_w_in)], WIN_SH // 4, D)
    upd["w_in"] = tuple(t.transpose(1, 2, 0) for t in res)
    (dl, m2, v2, _), = _adamw(
        "adamw_small", [_pack_small(given, "", conv_w[0])], [_pack_small(small_grads, "", dconvw)],
        [_pack_small(given, "m_", m_conv_w[0])], [_pack_small(given, "v_", v_conv_w[0])], SMALL_ROWS, D)
    dl, m2, v2 = (_unpack_small(t, given) for t in (dl, m2, v2))
    for n in SMALL_ORDER + ("conv_w",):
        upd[n] = (dl[n], m2[n], v2[n], small_grads[n])

    return (loss, grad_x[None], *[upd[n][3] for n in WEIGHTS], *[upd[n][0] for n in WEIGHTS],
            *[upd[n][1] for n in WEIGHTS], *[upd[n][2] for n in WEIGHTS])
```

```python
import functools
import typing

import jax
import jax.numpy as jnp
from jax import lax
from jax.experimental import pallas as pl
from jax.experimental.pallas import tpu as pltpu

F32 = jnp.float32
BF16 = jnp.bfloat16

S = 2048
D = 1024
FF = 2816
NSH = 4
FS = FF // NSH
HALF = D // 2
HD = 64
NKV = 4
NQ_PER_KV = 4
KVW = NKV * HD
QCOLS = NQ_PER_KV * HD
CONV_C = 1536
CONV_K = 4
SSM_W = 1024
NST = 128
NCH = S // 128
WIN_COLS = 4112
WIN_SH = WIN_COLS // NSH
WIN_PAD = 4224
COL_DT = 4096
EPS = 1e-6
NEG = -1e30

ADAM_LR = 0.001
ADAM_B1 = 0.9
ADAM_B2 = 0.999
ADAM_EPS = 1e-08
ADAM_WD = 0.01
ADAM_STEP = 10

VMEM_LIMIT = 56 * 1024 * 1024
TS = 512
TR = 256

NN = (((1,), (0,)), ((), ()))
NT = (((1,), (1,)), ((), ()))
TN = (((0,), (0,)), ((), ()))
MESH = pl.DeviceIdType.MESH


def _cparams(*sem):
    return pltpu.CompilerParams(dimension_semantics=sem, vmem_limit_bytes=VMEM_LIMIT)


def _dot(a, b, dims):
    return lax.dot_general(a.astype(BF16), b.astype(BF16), dims, preferred_element_type=F32)


def _bf16_pieces(v):
    hi = v.astype(BF16)
    rest = v - hi.astype(F32)
    mid = rest.astype(BF16)
    return hi, mid, (rest - mid.astype(F32)).astype(BF16)


def _dot_exact(a, b, ones="a"):
    if ones == "a":
        sel = a.astype(BF16)
        parts = [lax.dot_general(sel, p, NN, preferred_element_type=F32) for p in _bf16_pieces(b)]
    else:
        sel = b.astype(BF16)
        parts = [lax.dot_general(p, sel, NN, preferred_element_type=F32) for p in _bf16_pieces(a)]
    return (parts[2] + parts[1]) + parts[0]


def _sigmoid(v):
    return 1.0 / (1.0 + jnp.exp(-v))


class _Rider(typing.NamedTuple):
    operands: list
    out_shapes: list
    aliases: dict
    sems: list
    start: typing.Callable
    finish: typing.Callable


def _call(body, name, grid, in_specs, out_specs, out_shape, operands, scratch=(), sem=(), rider=None):
    multi = isinstance(out_shape, (list, tuple))
    if rider is None:
        return pl.pallas_call(
            body, name=name, grid=grid, in_specs=in_specs, out_specs=out_specs, out_shape=out_shape,
            scratch_shapes=list(scratch), compiler_params=_cparams(*sem))(*operands)
    outs = list(out_shape) if multi else [out_shape]
    ospecs = list(out_specs) if multi else [out_specs]
    n_in, n_out, n_scr = len(operands), len(outs), len(scratch)
    ri, ro = len(rider.operands), len(rider.out_shapes)

    def wrapped(*refs):
        o0 = n_in + ri
        s0 = o0 + n_out + ro
        rin, rout, rsem = refs[n_in:o0], refs[o0 + n_out:s0], refs[s0 + n_scr:]
        ids = [pl.program_id(a) for a in range(len(grid))]
        first = functools.reduce(jnp.logical_and, [i == 0 for i in ids])
        last = functools.reduce(jnp.logical_and, [i == g - 1 for i, g in zip(ids, grid)])

        @pl.when(first)
        def _():
            rider.start(rin, rout, rsem)

        body(*refs[:n_in], *refs[o0:o0 + n_out], *refs[s0:s0 + n_scr])

        @pl.when(last)
        def _():
            rider.finish(rin, rout, rsem)

    hbm = pl.BlockSpec(memory_space=pl.ANY)
    res = pl.pallas_call(
        wrapped, name=name, grid=grid, in_specs=list(in_specs) + [hbm] * ri, out_specs=ospecs + [hbm] * ro,
        out_shape=outs + list(rider.out_shapes), scratch_shapes=list(scratch) + list(rider.sems),
        input_output_aliases={n_in + k: n_out + v for k, v in rider.aliases.items()},
        compiler_params=_cparams(*(("arbitrary",) * len(grid))))(*operands, *rider.operands)
    main = list(res[:n_out])
    return (main if multi else main[0]), list(res[n_out:])


class _Tail(typing.NamedTuple):
    fn: typing.Callable
    operands: list
    in_specs: list


def _mm(name, operands, dims, grid, in_specs, o_spec, out_shape, rider=None, tail=None):
    npairs = len(operands) // 2
    extra = [] if tail is None else list(tail.operands)
    nin = 2 * npairs + len(extra)

    def body(*refs):
        t = None
        for i in range(npairs):
            a, b = refs[2 * i], refs[2 * i + 1]
            parts = [(a[s], b[s]) for s in range(a.shape[0])] if len(a.shape) == 3 else [(a[...], b[...])]
            for pa, pb in parts:
                d = _dot(pa, pb, dims)
                t = d if t is None else t + d
        if tail is None:
            refs[nin][...] = t.astype(refs[nin].dtype)
        else:
            tail.fn(t, refs[2 * npairs:nin], refs[nin:])

    sem = ("parallel" if tail is None else "arbitrary",) * len(grid)
    specs = list(in_specs) + ([] if tail is None else list(tail.in_specs))
    return _call(body, name, grid, specs, o_spec, out_shape, list(operands) + extra, (), sem, rider)


class _FfnW(typing.NamedTuple):
    gu: jax.Array
    g0: int
    dn: jax.Array
    d0: int


def _ffn_up(name, n, w, rider=None):
    def body(n_ref, wg_ref, wu_ref, fg_ref, fu_ref, a_ref):
        nb = n_ref[...]
        g = _dot(nb, wg_ref[...], NT)
        u = _dot(nb, wu_ref[...], NT)
        sg = _sigmoid(g)
        silu = g * sg
        fg_ref[...] = (u * (sg * (1.0 + g * (1.0 - sg)))).astype(BF16)
        fu_ref[...] = silu.astype(BF16)
        a_ref[...] = (silu * u).astype(BF16)

    out = jax.ShapeDtypeStruct((NSH, S, FS), BF16)
    ospec = pl.BlockSpec((None, TS, FS), lambda s, i: (s, i, 0))
    return _call(
        body, name, (NSH, S // TS),
        [pl.BlockSpec((TS, D), lambda s, i: (i, 0)),
         pl.BlockSpec((None, None, FS, D), lambda s, i: (s, w.g0, 0, 0)),
         pl.BlockSpec((None, None, FS, D), lambda s, i: (s, w.g0 + 1, 0, 0))],
        [ospec, ospec, ospec], [out, out, out], (n, w.gu, w.gu), sem=("parallel", "parallel"), rider=rider)


def _ffn_dact(name, dh, w, fgate, fup, rider=None):
    def body(dh_ref, wd_ref, fg_ref, fu_ref, dg_ref, du_ref):
        da = _dot(dh_ref[...], wd_ref[...], NT)
        dg_ref[...] = (da * fg_ref[...].astype(F32)).astype(BF16)
        du_ref[...] = (da * fu_ref[...].astype(F32)).astype(BF16)

    out = jax.ShapeDtypeStruct((NSH, S, FS), BF16)
    aspec = pl.BlockSpec((None, TS, FS), lambda s, i: (s, i, 0))
    return _call(
        body, name, (NSH, S // TS),
        [pl.BlockSpec((TS, D), lambda s, i: (i, 0)),
         pl.BlockSpec((None, None, FS, D), lambda s, i: (s, w.d0, 0, 0)), aspec, aspec],
        [aspec, aspec], [out, out], (dh, w.dn, fgate, fup), sem=("parallel", "parallel"), rider=rider)


def _rstd(v):
    return lax.rsqrt(jnp.mean(v * v, axis=-1, keepdims=True) + EPS)


def _row_spec():
    return pl.BlockSpec((TR, D), lambda i: (i, 0))


def _vec_spec():
    return pl.BlockSpec((1, D), lambda i: (0, 0))


def _acc_rows(ref, v):
    @pl.when(pl.program_id(0) == 0)
    def _():
        ref[...] = jnp.zeros_like(ref)
    ref[...] += jnp.sum(v, axis=0, keepdims=True)


def _prenorm(name, x, g):
    def body(x_ref, g_ref, n_ref):
        xv = x_ref[...]
        n_ref[...] = (xv * _rstd(xv) * g_ref[...]).astype(BF16)

    return pl.pallas_call(
        body, name=name, grid=(S // TR,), in_specs=[_row_spec(), _vec_spec()], out_specs=_row_spec(),
        out_shape=jax.ShapeDtypeStruct((S, D), BF16), compiler_params=_cparams("parallel"),
    )(x, g)


def _rows_spec(rows):
    return pl.BlockSpec((rows, D), lambda i: (i, 0))


def _rows_f32():
    return jax.ShapeDtypeStruct((S, D), F32)


def _rows_bf16():
    return jax.ShapeDtypeStruct((S, D), BF16)


def _vec_f32():
    return jax.ShapeDtypeStruct((1, D), F32)


def _tail_postres(rows, x, p, alpha, gnext):
    def fn(h, ins, outs):
        x_ref, p_ref, g_ref = ins
        h_ref, xo_ref, n_ref = outs
        h_ref[...] = h
        xo = x_ref[...] + alpha * (h * _rstd(h) * p_ref[...])
        xo_ref[...] = xo
        n_ref[...] = (xo * _rstd(xo) * g_ref[...]).astype(BF16)

    rs = _rows_spec(rows)
    return (_Tail(fn, [x, p, gnext], [rs, _vec_spec(), _vec_spec()]), [rs, rs, rs],
            [_rows_f32(), _rows_f32(), _rows_bf16()])


def _tail_final(rows, x, p, tgt, alpha):
    def fn(h, ins, outs):
        x_ref, p_ref, t_ref = ins
        dy_ref, dh_ref, dp_ref, loss_ref = outs
        r = _rstd(h)
        hn = h * r
        pv = p_ref[...]
        e = x_ref[...] + alpha * (hn * pv) - t_ref[...]
        dy = e * (1.0 / D)
        dy_ref[...] = dy
        du = alpha * dy * pv
        dh_ref[...] = (r * (du - hn * jnp.mean(du * hn, axis=-1, keepdims=True))).astype(BF16)
        _acc_rows(dp_ref, alpha * dy * hn)
        part = 0.5 * jnp.sum(jnp.mean(e * e, axis=-1, keepdims=True), axis=0, keepdims=True)
        _acc_rows(loss_ref, jnp.broadcast_to(part, (1, 128)))

    rs = _rows_spec(rows)
    return (_Tail(fn, [x, p, tgt], [rs, _vec_spec(), rs]),
            [rs, rs, _vec_spec(), pl.BlockSpec((1, 128), lambda i: (0, 0))],
            [_rows_f32(), _rows_bf16(), _vec_f32(), jax.ShapeDtypeStruct((1, 128), F32)])


def _norm_bwd(dn, xv, g_ref, dg_ref):
    r = _rstd(xv)
    xn = xv * r
    dng = dn * g_ref[...]
    _acc_rows(dg_ref, dn * xn)
    return r * (dng - xn * jnp.mean(dng * xn, axis=-1, keepdims=True))


def _tail_mid_bwd(rows, dres, x, g, h, p, alpha):
    def fn(dn, ins, outs):
        dr_ref, x_ref, g_ref, h_ref, p_ref = ins
        dx_ref, dh_ref, dg_ref, dp_ref = outs
        dx = dr_ref[...] + _norm_bwd(dn, x_ref[...], g_ref, dg_ref)
        dx_ref[...] = dx
        hv = h_ref[...]
        r = _rstd(hv)
        hn = hv * r
        du = alpha * dx * p_ref[...]
        dh_ref[...] = (r * (du - hn * jnp.mean(du * hn, axis=-1, keepdims=True))).astype(BF16)
        _acc_rows(dp_ref, alpha * dx * hn)

    rs = _rows_spec(rows)
    return (_Tail(fn, [dres, x, g, h, p], [rs, rs, _vec_spec(), rs, _vec_spec()]),
            [rs, rs, _vec_spec(), _vec_spec()], [_rows_f32(), _rows_bf16(), _vec_f32(), _vec_f32()])


def _tail_first_bwd(rows, dres, x, g):
    def fn(dn, ins, outs):
        dr_ref, x_ref, g_ref = ins
        dx_ref, dg_ref = outs
        dx_ref[...] = dr_ref[...] + _norm_bwd(dn, x_ref[...], g_ref, dg_ref)

    rs = _rows_spec(rows)
    return (_Tail(fn, [dres, x, g], [rs, rs, _vec_spec()]), [rs, _vec_spec()], [_rows_f32(), _vec_f32()])


def _rotate(t, c128, s128, sign, scale):
    width = t.shape[1]
    c = jnp.tile(c128, (1, width // 128))
    sn = jnp.tile(s128, (1, width // 128))
    lane = lax.broadcasted_iota(jnp.int32, t.shape, 1) & (HD - 1)
    rot = jnp.where(lane < HD // 2, -pltpu.roll(t, width - HD // 2, 1), pltpu.roll(t, HD // 2, 1))
    return (t * c + sign * (rot * sn)) * scale


def _rows_to_blocks(y):
    out = []
    for j in range(NKV):
        yt = y[:, QCOLS * j:QCOLS * (j + 1)].T
        out.append(jnp.concatenate([yt[HD * g:HD * (g + 1)] for g in range(NQ_PER_KV)], axis=1))
    return out


def _blocks_to_rows(blocks):
    cols = []
    for b in blocks:
        stacked = jnp.concatenate([b[:, 128 * g:128 * (g + 1)] for g in range(NQ_PER_KV)], axis=0)
        cols.append(stacked.T)
    return jnp.concatenate(cols, axis=1)


def _rope_q(proj, cos, sin):
    def body(t_ref, c_ref, s_ref, o_ref):
        y = _rotate(t_ref[...], c_ref[...], s_ref[...], 1.0, HD ** -0.5)
        for j, blk in enumerate(_rows_to_blocks(y)):
            o_ref[j] = blk.astype(BF16)

    return pl.pallas_call(
        body, name="rope_q", grid=(NCH,),
        in_specs=[pl.BlockSpec((128, D), lambda i: (i, 0)),
                  pl.BlockSpec((128, 128), lambda i: (i, 0)), pl.BlockSpec((128, 128), lambda i: (i, 0))],
        out_specs=pl.BlockSpec((NKV, None, HD, QROWS), lambda i: (0, i, 0, 0)),
        out_shape=jax.ShapeDtypeStruct((NKV, NCH, HD, QROWS), BF16), compiler_params=_cparams("parallel"),
    )(proj, cos, sin)


def _rope_dq(dqt, cos, sin, dproj):
    def body(t_ref, c_ref, s_ref, buf_ref, o_ref):
        t = _blocks_to_rows([t_ref[j] for j in range(NKV)])
        o_ref[...] = _rotate(t, c_ref[...], s_ref[...], -1.0, HD ** -0.5).astype(BF16)

    return pl.pallas_call(
        body, name="rope_dq", grid=(NCH,),
        in_specs=[pl.BlockSpec((NKV, None, HD, QROWS), lambda i: (0, i, 0, 0)),
                  pl.BlockSpec((128, 128), lambda i: (i, 0)), pl.BlockSpec((128, 128), lambda i: (i, 0)),
                  pl.BlockSpec(memory_space=pl.ANY)],
        out_specs=pl.BlockSpec((128, D), lambda i: (i, 0)),
        out_shape=jax.ShapeDtypeStruct(dproj.shape, BF16), input_output_aliases={3: 0},
        compiler_params=_cparams("parallel"),
    )(dqt, cos, sin, dproj)


def _rope_dkv(dkt, dvt, cos, sin, dproj):
    def body(k_ref, v_ref, c_ref, s_ref, buf_ref, o_ref):
        dk = jnp.concatenate([k_ref[j] for j in range(NKV)], axis=0).T
        dv = jnp.concatenate([v_ref[j] for j in range(NKV)], axis=0).T
        dk = _rotate(dk, c_ref[...], s_ref[...], -1.0, 1.0)
        o_ref[...] = jnp.concatenate([dk, dv], axis=1).astype(BF16)

    tspec = pl.BlockSpec((NKV, HD, 128), lambda i: (0, 0, i))
    return pl.pallas_call(
        body, name="rope_dkv", grid=(NCH,),
        in_specs=[tspec, tspec, pl.BlockSpec((128, 128), lambda i: (i, 0)), pl.BlockSpec((128, 128), lambda i: (i, 0)),
                  pl.BlockSpec(memory_space=pl.ANY)],
        out_specs=pl.BlockSpec((128, 2 * KVW), lambda i: (i, D // (2 * KVW))),
        out_shape=jax.ShapeDtypeStruct(dproj.shape, BF16), input_output_aliases={4: 0},
        compiler_params=_cparams("parallel"),
    )(dkt, dvt, cos, sin, dproj)


def _rope_kv(proj, cos, sin):
    def body(t_ref, c_ref, s_ref, k_ref, v_ref, kt_ref, vt_ref):
        t = t_ref[...]
        k = _rotate(t[:, :KVW], c_ref[...], s_ref[...], 1.0, 1.0).astype(BF16)
        v = t[:, KVW:].astype(BF16)
        k_ref[...] = k
        v_ref[...] = v
        kt, vt = k.astype(F32).T, v.astype(F32).T
        for j in range(NKV):
            kt_ref[j] = kt[HD * j:HD * (j + 1)].astype(BF16)
            vt_ref[j] = vt[HD * j:HD * (j + 1)].astype(BF16)

    rows = pl.BlockSpec((128, KVW), lambda i: (i, 0))
    tspec = pl.BlockSpec((NKV, HD, 128), lambda i: (0, 0, i))
    return pl.pallas_call(
        body, name="rope_kv", grid=(NCH,),
        in_specs=[pl.BlockSpec((128, 2 * KVW), lambda i: (i, D // (2 * KVW))),
                  pl.BlockSpec((128, 128), lambda i: (i, 0)), pl.BlockSpec((128, 128), lambda i: (i, 0))],
        out_specs=[rows, rows, tspec, tspec],
        out_shape=[jax.ShapeDtypeStruct((S, KVW), BF16)] * 2 + [jax.ShapeDtypeStruct((NKV, HD, S), BF16)] * 2,
        compiler_params=_cparams("parallel"),
    )(proj, cos, sin)


QROWS = NQ_PER_KV * 128


NBIAS = NCH + 1
KV_PER_STEP = 4


def _bias_table():
    db = lax.broadcasted_iota(jnp.int32, (NBIAS, 128, QROWS), 0) - 1
    ki = lax.broadcasted_iota(jnp.int32, (NBIAS, 128, QROWS), 1)
    qi = lax.broadcasted_iota(jnp.int32, (NBIAS, 128, QROWS), 2) & 127
    d = db * 128 + qi - ki
    cnt = ((d <= 128).astype(F32) + (((d & 3) == 0) & (d <= 512)).astype(F32) + ((d & 15) == 0).astype(F32))
    return jnp.where((d >= 0) & (cnt > 0.0), jnp.log(jnp.maximum(cnt, 1.0)), NEG)


def _qt_spec():
    return pl.BlockSpec((None, None, HD, QROWS), lambda j, i: (j, i, 0, 0))


def _stat_spec():
    return pl.BlockSpec((None, None, 1, QROWS), lambda j, i: (j, i, 0, 0))


def _attn_fwd(qt, kh, vt, bias, rider=None):
    def body(q_ref, k_ref, v_ref, b_ref, o_ref, lse_ref, rows_ref, m_ref, l_ref, acc_ref):
        qb = pl.program_id(1)
        m_ref[...] = jnp.full_like(m_ref, NEG)
        l_ref[...] = jnp.zeros_like(l_ref)
        acc_ref[...] = jnp.zeros_like(acc_ref)

        def keys(off, size, bias_):
            for h in range(KV_PER_STEP):
                m = m_ref[h]
                s = _dot(k_ref[h, pl.ds(off, size), :], q_ref[h], NN) + bias_
                m_new = jnp.maximum(m, jnp.max(s, axis=0, keepdims=True))
                p = jnp.exp(s - m_new)
                a = jnp.exp(m - m_new)
                m_ref[h] = m_new
                l_ref[h] = a * l_ref[h] + jnp.sum(p, axis=0, keepdims=True)
                acc_ref[h] = a * acc_ref[h] + _dot(v_ref[h, :, pl.ds(off, size)], p, NN)

        @pl.loop(0, (qb + 1) // 2)
        def _(i):
            bias2 = jnp.concatenate([b_ref[qb - 2 * i + 1], b_ref[qb - 2 * i]], axis=0)
            keys(pl.multiple_of(i * 256, 256), 256, bias2)

        @pl.when(qb % 2 == 0)
        def _():
            keys(pl.multiple_of(qb * 128, 128), 128, b_ref[1])

        outs = []
        for h in range(KV_PER_STEP):
            outs.append(acc_ref[h] / l_ref[h])
            o_ref[h] = outs[h]
            lse_ref[h] = m_ref[h] + jnp.log(l_ref[h])
        rows_ref[...] = _blocks_to_rows(outs).astype(BF16)

    kvs = KV_PER_STEP
    qspec = pl.BlockSpec((kvs, None, HD, QROWS), lambda j, i: (j, i, 0, 0))
    return _call(
        body, "attn_fwd", (NKV // kvs, NCH),
        [qspec, pl.BlockSpec((kvs, S, HD), lambda j, i: (j, 0, 0)),
         pl.BlockSpec((kvs, HD, S), lambda j, i: (j, 0, 0)),
         pl.BlockSpec((NBIAS, 128, QROWS), lambda j, i: (0, 0, 0))],
        [qspec, pl.BlockSpec((kvs, None, 1, QROWS), lambda j, i: (j, i, 0, 0)),
         pl.BlockSpec((128, QCOLS * kvs), lambda j, i: (i, j))],
        [jax.ShapeDtypeStruct((NKV, NCH, HD, QROWS), F32), jax.ShapeDtypeStruct((NKV, NCH, 1, QROWS), F32),
         jax.ShapeDtypeStruct((S, D), BF16)],
        (qt, kh, vt, bias),
        [pltpu.VMEM((kvs, 1, QROWS), F32), pltpu.VMEM((kvs, 1, QROWS), F32), pltpu.VMEM((kvs, HD, QROWS), F32)],
        ("parallel", "parallel"), rider)


def _attn_delta(ot, dot_):
    def body(o_ref, do_ref, dl_ref):
        dl_ref[...] = jnp.sum(o_ref[...] * do_ref[...].astype(F32), axis=1, keepdims=True)

    spec = pl.BlockSpec((None, NCH, HD, QROWS), lambda j: (j, 0, 0, 0))
    return pl.pallas_call(
        body, name="attn_delta", grid=(NKV,), in_specs=[spec, spec],
        out_specs=pl.BlockSpec((None, NCH, 1, QROWS), lambda j: (j, 0, 0, 0)),
        out_shape=jax.ShapeDtypeStruct((NKV, NCH, 1, QROWS), F32), compiler_params=_cparams("parallel"),
    )(ot, dot_)


def _attn_bwd(qt, kh, kt, vh, dot_, lse, delta, bias, rider=None):
    def body(qt_ref, k_ref, kt_ref, v_ref, dot_ref, lse_ref, dl_ref, b_ref, dq_ref, dk_ref, dv_ref):
        kb = pl.program_id(1)

        @pl.when(kb == 0)
        def _():
            dq_ref[...] = jnp.zeros_like(dq_ref)

        def blocks(carry, qbs):
            out = list(carry)
            for h in range(KV_PER_STEP):
                k, kt_, v = k_ref[h], kt_ref[h], v_ref[h]
                for qb in qbs:
                    st = _dot(k, qt_ref[h, qb], NN) + b_ref[qb - kb + 1]
                    pt = jnp.exp(st - lse_ref[h, qb])
                    dst = pt * (_dot(v, dot_ref[h, qb], NN) - dl_ref[h, qb])
                    dq_ref[h, qb] += _dot(kt_, dst, NN)
                    out[2 * h] = out[2 * h] + _dot(qt_ref[h, qb], dst, NT)
                    out[2 * h + 1] = out[2 * h + 1] + _dot(dot_ref[h, qb], pt, NT)
            return tuple(out)

        res = (jnp.zeros((HD, 128), F32),) * (2 * KV_PER_STEP)
        res = lax.cond(kb % 2 == 1, lambda c: blocks(c, (kb,)), lambda c: c, res)
        res = lax.fori_loop((kb + 1) // 2, NCH // 2, lambda j, c: blocks(c, (2 * j, 2 * j + 1)), res)
        for h in range(KV_PER_STEP):
            dk_ref[h] = res[2 * h]
            dv_ref[h] = res[2 * h + 1]

    kvs = KV_PER_STEP
    tspec = pl.BlockSpec((kvs, NCH, HD, QROWS), lambda j, i: (j, 0, 0, 0))
    kspec = pl.BlockSpec((kvs, 128, HD), lambda j, i: (j, i, 0))
    ktspec = pl.BlockSpec((kvs, HD, 128), lambda j, i: (j, 0, i))
    sspec = pl.BlockSpec((kvs, NCH, 1, QROWS), lambda j, i: (j, 0, 0, 0))
    return _call(
        body, "attn_bwd", (NKV // kvs, NCH),
        [tspec, kspec, ktspec, kspec, tspec, sspec, sspec,
         pl.BlockSpec((NBIAS, 128, QROWS), lambda j, i: (0, 0, 0))],
        [tspec, ktspec, ktspec],
        [jax.ShapeDtypeStruct((NKV, NCH, HD, QROWS), F32),
         jax.ShapeDtypeStruct((NKV, HD, S), F32), jax.ShapeDtypeStruct((NKV, HD, S), F32)],
        (qt, kh, kt, vh, dot_, lse, delta, bias), sem=("parallel", "arbitrary"), rider=rider)


CONV_BLK = 256
CONV_COL0 = 1536 // CONV_BLK


def _shift_down(u, j, row):
    return jnp.where(row >= j, pltpu.roll(u, j, 0), 0.0)


def _conv_pre(u, w_ref, b_ref, row):
    y = b_ref[...] + w_ref[CONV_K - 1:CONV_K, :] * u
    for j in range(1, CONV_K):
        y = y + w_ref[CONV_K - 1 - j:CONV_K - j, :] * _shift_down(u, j, row)
    return y


def _conv_fwd(proj, convw, convb):
    def body(u_ref, w_ref, b_ref, o_ref):
        u = u_ref[...]
        row = lax.broadcasted_iota(jnp.int32, u.shape, 0)
        y = _conv_pre(u, w_ref, b_ref, row)
        o_ref[...] = y * _sigmoid(y)

    return pl.pallas_call(
        body, name="conv_fwd", grid=(CONV_C // CONV_BLK,),
        in_specs=[pl.BlockSpec((S, CONV_BLK), lambda i: (0, CONV_COL0 + i)),
                  pl.BlockSpec((CONV_K, CONV_BLK), lambda i: (0, i)),
                  pl.BlockSpec((1, CONV_BLK), lambda i: (0, i))],
        out_specs=pl.BlockSpec((S, CONV_BLK), lambda i: (0, i)),
        out_shape=jax.ShapeDtypeStruct((S, CONV_C), F32), compiler_params=_cparams("parallel"),
    )(proj, convw, convb)


def _conv_bwd(dact, proj, convw, convb, dproj):
    def body(da_ref, u_ref, w_ref, b_ref, buf_ref, du_ref, dw_ref, db_ref):
        u = u_ref[...]
        row = lax.broadcasted_iota(jnp.int32, u.shape, 0)
        y = _conv_pre(u, w_ref, b_ref, row)
        sg = _sigmoid(y)
        dy = da_ref[...] * (sg * (1.0 + y * (1.0 - sg)))
        db_ref[...] = jnp.sum(dy, axis=0, keepdims=True)
        du = w_ref[CONV_K - 1:CONV_K, :] * dy
        r8 = lax.broadcasted_iota(jnp.int32, (8, CONV_BLK), 0)
        dw = jnp.where(r8 == CONV_K - 1, jnp.sum(dy * u, axis=0, keepdims=True), 0.0)
        for j in range(1, CONV_K):
            du = du + w_ref[CONV_K - 1 - j:CONV_K - j, :] * jnp.where(row < S - j, pltpu.roll(dy, S - j, 0), 0.0)
            dw = dw + jnp.where(r8 == CONV_K - 1 - j,
                                jnp.sum(dy * _shift_down(u, j, row), axis=0, keepdims=True), 0.0)
        du_ref[...] = du.astype(BF16)
        dw_ref[...] = dw

    return pl.pallas_call(
        body, name="conv_bwd", grid=(CONV_C // CONV_BLK,),
        in_specs=[pl.BlockSpec((S, CONV_BLK), lambda i: (0, i)),
                  pl.BlockSpec((S, CONV_BLK), lambda i: (0, CONV_COL0 + i)),
                  pl.BlockSpec((CONV_K, CONV_BLK), lambda i: (0, i)),
                  pl.BlockSpec((1, CONV_BLK), lambda i: (0, i)), pl.BlockSpec(memory_space=pl.ANY)],
        out_specs=[pl.BlockSpec((S, CONV_BLK), lambda i: (0, CONV_COL0 + i)),
                   pl.BlockSpec((8, CONV_BLK), lambda i: (0, i)), pl.BlockSpec((1, CONV_BLK), lambda i: (0, i))],
        out_shape=[jax.ShapeDtypeStruct(dproj.shape, BF16), jax.ShapeDtypeStruct((8, CONV_C), F32),
                   jax.ShapeDtypeStruct((1, CONV_C), F32)],
        input_output_aliases={4: 0}, compiler_params=_cparams("parallel"),
    )(dact, proj, convw, convb, dproj)


NPAIR = 8


def _ssd_scalars(dtr_ref, dtb_ref, alog_ref):
    z = dtr_ref[...] + dtb_ref[...]
    dt = jnp.maximum(z, 0.0) + jnp.log(1.0 + jnp.exp(-jnp.abs(z)))
    a = -jnp.exp(alog_ref[...])
    r = lax.broadcasted_iota(jnp.int32, (128, 128), 0)
    c = lax.broadcasted_iota(jnp.int32, (128, 128), 1)
    tri = (r >= c).astype(F32)
    cs = _dot_exact(tri, dt * a)
    return z, dt, a, cs, r, c


def _by_lane(cs, dt):
    head = lax.broadcasted_iota(jnp.int32, (128, SSM_W), 0)
    lane = lax.broadcasted_iota(jnp.int32, (128, SSM_W), 1)
    sel = (head == lane // HD).astype(F32)
    cs_l = _dot_exact(cs, sel, "b")
    last_l = cs_l[127:128, :]
    return sel, jnp.exp(cs_l), jnp.exp(last_l - cs_l), _dot_exact(dt, sel, "b")


def _pair_terms(cs, h1, h2):
    return (cs[:, h1:h1 + 1], cs[:, h2:h2 + 1],
            jnp.exp(cs[127:128, h1:h1 + 1]), jnp.exp(cs[127:128, h2:h2 + 1]))


def _gate_norm(y, zv, w):
    yg = y * (zv * _sigmoid(zv))
    outs, rs = [], []
    for g in range(2):
        blk = yg[:, 512 * g:512 * (g + 1)]
        r = lax.rsqrt(jnp.mean(blk * blk, axis=-1, keepdims=True) + EPS)
        outs.append(blk * r)
        rs.append(r)
    return jnp.concatenate(outs, axis=1), rs, yg


def _ssd_fwd(xbc, proj, dtb, alog, dskip_l, ssmw):
    def body(x_ref, b_ref, c_ref, dtr_ref, z_ref, dtb_ref, alog_ref, dsk_ref, w_ref, y_ref, yn_ref, hp_ref, h_ref):
        @pl.when(pl.program_id(0) == 0)
        def _():
            h_ref[...] = jnp.zeros_like(h_ref)

        _, dt, _, cs, r, c = _ssd_scalars(dtr_ref, dtb_ref, alog_ref)
        cst = cs.T
        causal = r >= c
        lo = c < HD
        _, e_all, dte_all, dt_all = _by_lane(cs, dt)
        hp_ref[...] = h_ref[...]
        for g in range(2):
            bg = b_ref[:, 128 * g:128 * (g + 1)]
            cg = c_ref[:, 128 * g:128 * (g + 1)]
            cb = _dot(cg, bg, NT)
            for j in range(4):
                pj = 4 * g + j
                h1, h2 = 2 * pj, 2 * pj + 1
                sl = slice(128 * pj, 128 * (pj + 1))
                xp = x_ref[:, sl]
                c1, c2, cd1, cd2 = _pair_terms(cs, h1, h2)
                e_l, dte_l = e_all[:, sl], dte_all[:, sl]
                xdt = xp * dt_all[:, sl]
                m1 = cb * jnp.exp(jnp.where(causal, c1 - cst[h1:h1 + 1, :], NEG))
                m2 = cb * jnp.exp(jnp.where(causal, c2 - cst[h2:h2 + 1, :], NEG))
                yd = jnp.where(lo, _dot(m1, xdt, NN), _dot(m2, xdt, NN))
                hp = h_ref[pj]
                yo = _dot(cg, hp, NT) * e_l
                st = _dot(xdt * dte_l, bg, TN)
                h_ref[pj] = hp * jnp.where(r < HD, cd1, cd2) + st
                y_ref[:, sl] = yd + yo + dsk_ref[:, sl] * xp
        yn, _, _ = _gate_norm(y_ref[...], z_ref[...], w_ref[...])
        yn_ref[...] = (yn * w_ref[...]).astype(BF16)

    return pl.pallas_call(
        body, name="ssd_fwd", grid=(NCH,),
        in_specs=[pl.BlockSpec((128, SSM_W), lambda i: (i, 0)),
                  pl.BlockSpec((128, 256), lambda i: (i, 4)), pl.BlockSpec((128, 256), lambda i: (i, 5)),
                  pl.BlockSpec((128, 128), lambda i: (i, COL_DT // 128)),
                  pl.BlockSpec((128, SSM_W), lambda i: (i, 3)),
                  pl.BlockSpec((1, 128), lambda i: (0, 0)), pl.BlockSpec((1, 128), lambda i: (0, 0)),
                  pl.BlockSpec((1, SSM_W), lambda i: (0, 0)), pl.BlockSpec((1, SSM_W), lambda i: (0, 0))],
        out_specs=[pl.BlockSpec((128, SSM_W), lambda i: (i, 0)), pl.BlockSpec((128, SSM_W), lambda i: (i, 0)),
                   pl.BlockSpec((None, NPAIR, 128, 128), lambda i: (i, 0, 0, 0))],
        out_shape=[jax.ShapeDtypeStruct((S, SSM_W), F32), jax.ShapeDtypeStruct((S, SSM_W), BF16),
                   jax.ShapeDtypeStruct((NCH, NPAIR, 128, 128), F32)],
        scratch_shapes=[pltpu.VMEM((NPAIR, 128, 128), F32)],
        compiler_params=_cparams("arbitrary"),
    )(xbc, xbc, xbc, proj, proj, dtb, alog, dskip_l, ssmw)


def _ssd_bwd(dmixed, y, xbc, proj, hprev, dtb, alog, dskip_l, ssmw, rider=None):
    def body(dyn_ref, y_ref, x_ref, b_ref, c_ref, dtr_ref, z_ref, hp_ref, dtb_ref, alog_ref, dsk_ref, w_ref,
             dxbc_ref, dz_ref, ddt_ref, dw_ref, dsc_ref, g_ref):
        @pl.when(pl.program_id(0) == 0)
        def _():
            g_ref[...] = jnp.zeros_like(g_ref)
            dsc_ref[...] = jnp.zeros_like(dsc_ref)

        z, dt, a, cs, r, c = _ssd_scalars(dtr_ref, dtb_ref, alog_ref)
        cst = cs.T
        causal = r >= c
        lo = c < HD

        yv = y_ref[...]
        zv = z_ref[...]
        wv = w_ref[...]
        ygn, rs, yg = _gate_norm(yv, zv, wv)
        dyn = dyn_ref[...]
        _acc_rows(dw_ref, dyn * ygn)
        dynw = dyn * wv
        parts = []
        for g in range(2):
            sl = slice(512 * g, 512 * (g + 1))
            a_g, n_g = dynw[:, sl], ygn[:, sl]
            parts.append(rs[g] * (a_g - n_g * jnp.mean(a_g * n_g, axis=-1, keepdims=True)))
        dyg = jnp.concatenate(parts, axis=1)
        sz = _sigmoid(zv)
        dz_ref[...] = (dyg * yv * (sz * (1.0 + zv * (1.0 - sz)))).astype(BF16)
        dy_all = dyg * (zv * sz)

        dcs_cols = jnp.zeros((128, 128), F32)
        dcs_rows = jnp.zeros((128, 128), F32)
        sel, e_all, dte_all, dt_all = _by_lane(cs, dt)
        x_all, b_all, c_all, dsk_all = x_ref[...], b_ref[...], c_ref[...], dsk_ref[...]
        hp_all, g_all = hp_ref[...], g_ref[...]
        g_new, dx_parts, db_parts, dc_parts = [], [], [], []
        dyx_parts, ryo_parts, qx_parts, dxx_parts, gh_parts = [], [], [], [], []
        for g in range(2):
            bg = b_all[:, 128 * g:128 * (g + 1)]
            cg = c_all[:, 128 * g:128 * (g + 1)]
            cb = _dot(cg, bg, NT)
            dcb = jnp.zeros((128, 128), F32)
            db_acc = jnp.zeros((128, NST), F32)
            dc_acc = jnp.zeros((128, NST), F32)
            for j in range(4):
                pj = 4 * g + j
                h1, h2 = 2 * pj, 2 * pj + 1
                sl = slice(128 * pj, 128 * (pj + 1))
                xp = x_all[:, sl]
                dyp = dy_all[:, sl]
                c1, c2, cd1, cd2 = _pair_terms(cs, h1, h2)
                e_l, dte_l, dt_l = e_all[:, sl], dte_all[:, sl], dt_all[:, sl]
                xdt = xp * dt_l
                hp = hp_all[pj]
                gp = g_all[pj]
                dyx_parts.append(dyp * xp)
                dzs = dyp * e_l
                dc_acc = dc_acc + _dot(dzs, hp, NN)
                ryo_parts.append(dyp * (_dot(cg, hp, NT) * e_l))
                qm = _dot(bg, gp, NT)
                dxdt = qm * dte_l
                qx_parts.append(qm * xdt)
                db_acc = db_acc + _dot(xdt * dte_l, gp, NN)
                gh_parts.append(gp * hp)
                g_new.append(_dot(dzs, cg, TN) + jnp.where(r < HD, cd1, cd2) * gp)
                for hh, ch, msk in ((h1, c1, lo), (h2, c2, jnp.logical_not(lo))):
                    lm = jnp.exp(jnp.where(causal, ch - cst[hh:hh + 1, :], NEG))
                    mm = cb * lm
                    dm = jnp.where(causal, _dot(jnp.where(msk, dyp, 0.0), xdt, NT), 0.0)
                    w = dm * mm
                    dcs_cols = dcs_cols + jnp.where(c == hh, jnp.sum(w, axis=1, keepdims=True), 0.0)
                    dcs_rows = dcs_rows + jnp.where(r == hh, jnp.sum(w, axis=0, keepdims=True), 0.0)
                    dcb = dcb + dm * lm
                    dxdt = dxdt + jnp.where(msk, _dot(mm, dyp, TN), 0.0)
                dxx_parts.append(dxdt * xp)
                dx_parts.append(dsk_all[:, sl] * dyp + dxdt * dt_l)
            db_parts.append(db_acc + _dot(dcb, cg, TN))
            dc_parts.append(dc_acc + _dot(dcb, bg, NN))
        g_ref[...] = jnp.stack(g_new)
        dxbc_ref[...] = jnp.concatenate(dx_parts + db_parts + dc_parts, axis=1)

        selt = (lax.broadcasted_iota(jnp.int32, (SSM_W, 128), 0) // HD
                == lax.broadcasted_iota(jnp.int32, (SSM_W, 128), 1)).astype(F32)

        def by_head(parts):
            return _dot_exact(jnp.concatenate(parts, axis=1), selt, "b")

        ddt_x = by_head(dxx_parts)
        dd_row = jnp.sum(by_head(dyx_parts), axis=0, keepdims=True)
        t_all = by_head(qx_parts) * jnp.exp(cs[127:128, :] - cs)
        gh = jnp.sum(_dot_exact(sel, jnp.concatenate(gh_parts, axis=0)), axis=1, keepdims=True)
        gh_row = jnp.broadcast_to(gh, (128, 128)).T[0:1, :]
        at_end = jnp.sum(t_all, axis=0, keepdims=True) + gh_row * jnp.exp(cs[127:128, :])
        dcs = by_head(ryo_parts) - t_all + dcs_cols + jnp.where(r == 127, at_end, 0.0) - dcs_rows.T
        dad = _dot_exact((c >= r).astype(F32), dcs)
        ddt = dad * a + ddt_x
        ddtr = jnp.where(c < 16, ddt * _sigmoid(z), 0.0)
        ddt_ref[...] = ddtr.astype(BF16)
        r8 = lax.broadcasted_iota(jnp.int32, (8, 128), 0)
        dsc_ref[...] += (jnp.where(r8 == 0, jnp.sum(ddtr, axis=0, keepdims=True), 0.0)
                         + jnp.where(r8 == 1, jnp.sum(dad * dt, axis=0, keepdims=True) * a, 0.0)
                         + jnp.where(r8 == 2, dd_row, 0.0))

    rev = NCH - 1
    return _call(
        body, "ssd_bwd", (NCH,),
        [pl.BlockSpec((128, SSM_W), lambda i: (rev - i, 0)),
         pl.BlockSpec((128, SSM_W), lambda i: (rev - i, 0)),
         pl.BlockSpec((128, SSM_W), lambda i: (rev - i, 0)),
         pl.BlockSpec((128, 256), lambda i: (rev - i, 4)), pl.BlockSpec((128, 256), lambda i: (rev - i, 5)),
         pl.BlockSpec((128, 128), lambda i: (rev - i, COL_DT // 128)),
         pl.BlockSpec((128, SSM_W), lambda i: (rev - i, 3)),
         pl.BlockSpec((None, NPAIR, 128, 128), lambda i: (rev - i, 0, 0, 0)),
         pl.BlockSpec((1, 128), lambda i: (0, 0)), pl.BlockSpec((1, 128), lambda i: (0, 0)),
         pl.BlockSpec((1, SSM_W), lambda i: (0, 0)), pl.BlockSpec((1, SSM_W), lambda i: (0, 0))],
        [pl.BlockSpec((128, CONV_C), lambda i: (rev - i, 0)),
         pl.BlockSpec((128, SSM_W), lambda i: (rev - i, 3)),
         pl.BlockSpec((128, 128), lambda i: (rev - i, 0)),
         pl.BlockSpec((1, SSM_W), lambda i: (0, 0)), pl.BlockSpec((8, 128), lambda i: (0, 0))],
        [jax.ShapeDtypeStruct((S, CONV_C), F32), jax.ShapeDtypeStruct((S, WIN_PAD), BF16),
         jax.ShapeDtypeStruct((S, 128), BF16), jax.ShapeDtypeStruct((1, SSM_W), F32),
         jax.ShapeDtypeStruct((8, 128), F32)],
        (dmixed, y, xbc, xbc, xbc, proj, proj, hprev, dtb, alog, dskip_l, ssmw),
        [pltpu.VMEM((NPAIR, 128, 128), F32)], ("arbitrary",), rider)


def _cast_stack(name, slot, arrs, tr, tc):
    n = len(arrs)
    rows, cols = arrs[0].shape

    def body(s_ref, *refs):
        for i in range(n):
            refs[n][i] = refs[i][...].astype(BF16)

    return pl.pallas_call(
        body, name=name,
        grid_spec=pltpu.PrefetchScalarGridSpec(
            num_scalar_prefetch=1, grid=(rows // tr, cols // tc),
            in_specs=[pl.BlockSpec((tr, tc), lambda i, j, sr: (i, j))] * n,
            out_specs=pl.BlockSpec((None, n, tr, tc), lambda i, j, sr: (sr[0], 0, i, j))),
        out_shape=jax.ShapeDtypeStruct((NSH, n, rows, cols), BF16),
        compiler_params=_cparams("parallel", "parallel"),
    )(slot, *arrs)


def _pair_sum(name, c_idx, ps, th):
    n = len(ps)
    _, rows, _ = ps[0].shape

    def body(c_ref, *refs):
        mine, whole, out, theirs = refs[:n], refs[n:2 * n], refs[2 * n:3 * n], refs[3 * n:4 * n]
        send, recv = refs[4 * n], refs[4 * n + 1]
        s, i = pl.program_id(0), pl.program_id(1)
        x, y, c, _ = _place()

        def copies(slot):
            return [_rcopy(whole[k].at[slot, :, pl.ds((1 - c) * HALF, HALF)], theirs[k].at[slot],
                           send.at[slot * n + k], recv.at[slot * n + k], (x, y, 1 - c)) for k in range(n)]

        @pl.when((s == 0) & (i == 0))
        def _():
            for slot in range(NSH):
                for cp in copies(slot):
                    cp.start()

        @pl.when(i == 0)
        def _():
            for slot in range(NSH):
                @pl.when(s == slot)
                def _():
                    for cp in copies(slot):
                        cp.wait()

        rows_i = slice(None) if th == rows else pl.ds(pl.multiple_of(i * th, th), th)
        for k in range(n):
            out[k][...] = (mine[k][...].astype(F32) + theirs[k][s, rows_i, :].astype(F32)).astype(BF16)

    spec = pl.BlockSpec((None, th, HALF), lambda s, i, cr: (s, i, 0))
    return pl.pallas_call(
        body, name=name,
        grid_spec=pltpu.PrefetchScalarGridSpec(
            num_scalar_prefetch=1, grid=(NSH, rows // th),
            in_specs=[pl.BlockSpec((None, th, HALF), lambda s, i, cr: (s, i, cr[0]))] * n + _any_specs(n),
            out_specs=[spec] * n,
            scratch_shapes=[pltpu.VMEM((NSH, rows, HALF), BF16)] * n
            + [pltpu.SemaphoreType.DMA((NSH * n,)), pltpu.SemaphoreType.DMA((NSH * n,))]),
        out_shape=[jax.ShapeDtypeStruct((NSH, rows, HALF), BF16)] * n,
        compiler_params=_cparams("arbitrary", "arbitrary"),
    )(c_idx, *ps, *ps)


def _chip_sum(name, place, cs, ts, th):
    n = len(ts)
    _, rows, _ = ts[0].shape

    def body(p_ref, *refs):
        for i in range(n):
            t = refs[n + i][...].astype(F32)
            refs[2 * n + i][...] = ((refs[i][...].astype(F32) + t[0]) + t[1]) + t[2]

    return pl.pallas_call(
        body, name=name,
        grid_spec=pltpu.PrefetchScalarGridSpec(
            num_scalar_prefetch=1, grid=(rows // th,),
            in_specs=[pl.BlockSpec((None, th, HALF), lambda i, pr: (pr[0], i, 0))] * n
            + [pl.BlockSpec((3, th, HALF), lambda i, pr: (0, i, 0))] * n,
            out_specs=[pl.BlockSpec((th, HALF), lambda i, pr: (i, pr[1]))] * n),
        out_shape=[jax.ShapeDtypeStruct((rows, D), F32)] * n, compiler_params=_cparams("parallel"),
    )(place, *cs, *ts)


def _adamw(name, ws, gs, ms, vs, tr, tc):
    n = len(ws)
    shape = ws[0].shape
    rows, cols, mid = shape[0], shape[-1], shape[1:-1]
    c1 = 1.0 / (1.0 - ADAM_B1 ** ADAM_STEP)
    c2 = 1.0 / (1.0 - ADAM_B2 ** ADAM_STEP)

    def body(*refs):
        for i in range(n):
            w, g, m, v = (refs[k * n + i][...] for k in range(4))
            m2 = ADAM_B1 * m + (1.0 - ADAM_B1) * g
            v2 = ADAM_B2 * v + (1.0 - ADAM_B2) * (g * g)
            refs[4 * n + 4 * i][...] = -ADAM_LR * ((m2 * c1) / (jnp.sqrt(v2 * c2) + ADAM_EPS) + ADAM_WD * w)
            refs[4 * n + 4 * i + 1][...] = m2
            refs[4 * n + 4 * i + 2][...] = v2
            refs[4 * n + 4 * i + 3][...] = g

    spec = pl.BlockSpec((tr,) + mid + (tc,), lambda i, j: (i,) + (0,) * len(mid) + (j,))
    outs = pl.pallas_call(
        body, name=name, grid=(rows // tr, cols // tc), in_specs=[spec] * (4 * n), out_specs=[spec] * (4 * n),
        out_shape=[jax.ShapeDtypeStruct(shape, F32)] * (4 * n),
        compiler_params=_cparams("parallel", "parallel"),
    )(*ws, *gs, *ms, *vs)
    return [tuple(outs[4 * i:4 * i + 4]) for i in range(n)]


def _place():
    x, y, c = lax.axis_index("x"), lax.axis_index("y"), lax.axis_index("c")
    chips = [(1 - x, y), (x, 1 - y), (1 - x, 1 - y)]
    return x, y, c, chips


def _any_specs(n):
    return [pl.BlockSpec(memory_space=pl.ANY)] * n


def _rcopy(src, dst, send_sem, recv_sem, dev):
    return pltpu.make_async_remote_copy(src_ref=src, dst_ref=dst, send_sem=send_sem, recv_sem=recv_sem,
                                        device_id=dev, device_id_type=MESH)


QUARTER = HALF // 2

TO_X, TO_Y, RELAY_X, RELAY_Y, FWD_X, FWD_Y, FWD_D0, FWD_D1 = range(8)
TO_D = RELAY_X


def _gather_rider(bufs, views, relay):
    n = len(bufs)

    def plan(rout, sems):
        send, recv = sems
        x, y, c, _ = _place()
        me, sx, sy, sd = 2 * x + y, 2 * (1 - x) + y, 2 * x + (1 - y), 2 * (1 - x) + (1 - y)
        nx, ny, nd, sib = (1 - x, y, c), (x, 1 - y, c), (1 - x, 1 - y, c), (x, y, 1 - c)
        mine, other = c * HALF, (1 - c) * HALF
        out = {TO_X: (me, mine, HALF, nx), TO_Y: (me, mine, HALF, ny),
               FWD_X: (sx, mine, HALF, sib), FWD_Y: (sy, mine, HALF, sib)}
        inn = {TO_X: (sx, mine, HALF), TO_Y: (sy, mine, HALF),
               FWD_X: (sx, other, HALF), FWD_Y: (sy, other, HALF)}
        if relay:
            out.update({RELAY_X: (sy, mine, QUARTER, nx), RELAY_Y: (sx, mine + QUARTER, QUARTER, ny),
                        FWD_D0: (sd, mine, QUARTER, sib), FWD_D1: (sd, mine + QUARTER, QUARTER, sib)})
            inn.update({RELAY_X: (sd, mine, QUARTER), RELAY_Y: (sd, mine + QUARTER, QUARTER),
                        FWD_D0: (sd, other, QUARTER), FWD_D1: (sd, other + QUARTER, QUARTER)})
        else:
            out.update({TO_D: (me, mine, HALF, nd), FWD_D0: (sd, mine, HALF, sib)})
            inn.update({TO_D: (sd, mine, HALF), FWD_D0: (sd, other, HALF)})

        def copy(kind, b):
            slot, col, ncols, dev = out[kind]
            win = views[b](rout[b], slot, col, ncols)
            return _rcopy(win, win, send.at[kind * n + b], recv.at[kind * n + b], dev)

        def land(kind, b):
            slot, col, ncols = inn[kind]
            win = views[b](rout[b], slot, col, ncols)
            return _rcopy(win, win, send.at[kind * n + b], recv.at[kind * n + b], (x, y, c))

        return copy, land

    if relay:
        first = (TO_X, TO_Y)
        chain = ((TO_X, (FWD_X, RELAY_Y)), (TO_Y, (FWD_Y, RELAY_X)), (RELAY_X, (FWD_D0,)), (RELAY_Y, (FWD_D1,)))
    else:
        first = (TO_X, TO_Y, TO_D)
        chain = ((TO_X, (FWD_X,)), (TO_Y, (FWD_Y,)), (TO_D, (FWD_D0,)))
    forwards = [k for _, then in chain for k in then if k in (FWD_X, FWD_Y, FWD_D0, FWD_D1)]
    sent = list(first) + [k for _, then in chain for k in then]

    def start(rin, rout, sems):
        copy, _ = plan(rout, sems)
        for kind in first:
            for b in range(n):
                copy(kind, b).start()

    def finish(rin, rout, sems):
        copy, land = plan(rout, sems)
        for landed, then in chain:
            for b in range(n):
                land(landed, b).wait_recv()
                for kind in then:
                    copy(kind, b).start()
        for kind in forwards:
            for b in range(n):
                land(kind, b).wait_recv()
        for kind in sent:
            for b in range(n):
                copy(kind, b).wait_send()

    return _Rider(list(bufs), [jax.ShapeDtypeStruct(a.shape, a.dtype) for a in bufs], {b: b for b in range(n)},
                  [pltpu.SemaphoreType.DMA((8 * n,))] * 2, start, finish)


def _small_gather_rider(cw):
    def descs(rin, rout, sems, x, y, c, chips):
        return [_rcopy(rin[0], rout[0].at[2 * x + y], sems[1].at[j], sems[2].at[j], (chip[0], chip[1], c))
                for j, chip in enumerate(chips)]

    def start(rin, rout, sems):
        x, y, c, chips = _place()
        pltpu.make_async_copy(rin[0], rout[0].at[2 * x + y], sems[0].at[0]).start()
        for cp in descs(rin, rout, sems, x, y, c, chips):
            cp.start()

    def finish(rin, rout, sems):
        x, y, c, chips = _place()
        for j, chip in enumerate(chips):
            _rcopy(rin[0], rout[0].at[2 * chip[0] + chip[1]], sems[1].at[j], sems[2].at[j], (x, y, c)).wait_recv()
        for cp in descs(rin, rout, sems, x, y, c, chips):
            cp.wait_send()
        pltpu.make_async_copy(rin[0], rout[0].at[2 * x + y], sems[0].at[0]).wait()

    return _Rider([cw], [jax.ShapeDtypeStruct((NSH,) + cw.shape, cw.dtype)], {},
                  [pltpu.SemaphoreType.DMA((1,)), pltpu.SemaphoreType.DMA((3,)), pltpu.SemaphoreType.DMA((3,))],
                  start, finish)


def _to_chips_rider(cs):
    n = len(cs)

    def descs(rin, rout, sems):
        x, y, c, chips = _place()
        return [_rcopy(rin[i].at[2 * chip[0] + chip[1]], rout[i].at[j], sems[0].at[j * n + i], sems[1].at[j * n + i],
                       (chip[0], chip[1], c)) for j, chip in enumerate(chips) for i in range(n)]

    def start(rin, rout, sems):
        for cp in descs(rin, rout, sems):
            cp.start()

    def finish(rin, rout, sems):
        for cp in descs(rin, rout, sems):
            cp.wait()

    return _Rider(list(cs), [jax.ShapeDtypeStruct((3,) + a.shape[1:], a.dtype) for a in cs], {},
                  [pltpu.SemaphoreType.DMA((3 * n,))] * 2, start, finish)


def _run_riders(name, riders):
    n_in = [len(r.operands) for r in riders]
    n_out = [len(r.out_shapes) for r in riders]
    n_sem = [len(r.sems) for r in riders]

    def body(*refs):
        parts, at = [], 0
        for counts in (n_in, n_out, n_sem):
            group = []
            for k in counts:
                group.append(refs[at:at + k])
                at += k
            parts.append(group)
        for i, r in enumerate(riders):
            r.start(parts[0][i], parts[1][i], parts[2][i])
        for i, r in enumerate(riders):
            r.finish(parts[0][i], parts[1][i], parts[2][i])

    aliases = {}
    for i, r in enumerate(riders):
        for k, v in r.aliases.items():
            aliases[sum(n_in[:i]) + k] = sum(n_out[:i]) + v
    res = pl.pallas_call(
        body, name=name, in_specs=_any_specs(sum(n_in)), out_specs=_any_specs(sum(n_out)),
        out_shape=[s for r in riders for s in r.out_shapes], input_output_aliases=aliases,
        scratch_shapes=[s for r in riders for s in r.sems],
    )(*[a for r in riders for a in r.operands])
    out, at = [], 0
    for k in n_out:
        out.append(list(res[at:at + k]))
        at += k
    return out


def _swap_halves(gs):
    n = len(gs)

    def body(*refs):
        dst, send, recv = refs[n:2 * n], refs[2 * n], refs[2 * n + 1]
        x, y, c, _ = _place()
        cps = []
        for i in range(n):
            mine = dst[i].at[:, pl.ds(c * HALF, HALF)]
            cps.append(pltpu.make_async_remote_copy(
                src_ref=mine, dst_ref=mine, send_sem=send.at[i], recv_sem=recv.at[i],
                device_id=(x, y, 1 - c), device_id_type=MESH))
        for cp in cps:
            cp.start()
        for i in range(n):
            other = dst[i].at[:, pl.ds((1 - c) * HALF, HALF)]
            pltpu.make_async_remote_copy(
                src_ref=other, dst_ref=other, send_sem=send.at[i], recv_sem=recv.at[i],
                device_id=(x, y, c), device_id_type=MESH).wait_recv()
        for cp in cps:
            cp.wait_send()

    return pl.pallas_call(
        body, name="grads_swap_halves", in_specs=_any_specs(n), out_specs=_any_specs(n),
        out_shape=[jax.ShapeDtypeStruct(g.shape, g.dtype) for g in gs],
        input_output_aliases={i: i for i in range(n)},
        scratch_shapes=[pltpu.SemaphoreType.DMA((n,)), pltpu.SemaphoreType.DMA((n,))],
    )(*gs)


SMALL_ROWS = 16


def _allreduce_small(vec):
    def body(v_ref, o_ref, buf, send, recv):
        x, y, c, _ = _place()
        me = 4 * x + 2 * y + c
        buf[me] = v_ref[...]
        cps = []
        for k in range(1, 8):
            peer = (x ^ (k >> 2), y ^ ((k >> 1) & 1), c ^ (k & 1))
            cps.append(pltpu.make_async_remote_copy(
                src_ref=v_ref, dst_ref=buf.at[me], send_sem=send.at[k - 1], recv_sem=recv.at[k - 1],
                device_id=peer, device_id_type=MESH))
        for cp in cps:
            cp.start()
        for k in range(1, 8):
            pltpu.make_async_remote_copy(
                src_ref=v_ref, dst_ref=buf.at[me ^ k], send_sem=send.at[k - 1], recv_sem=recv.at[k - 1],
                device_id=(x, y, c), device_id_type=MESH).wait_recv()
        for cp in cps:
            cp.wait_send()
        t = buf[0]
        for d in range(1, 8):
            t = t + buf[d]
        o_ref[...] = t

    return pl.pallas_call(
        body, name="allreduce_small",
        in_specs=[pl.BlockSpec(memory_space=pltpu.VMEM)], out_specs=pl.BlockSpec(memory_space=pltpu.VMEM),
        out_shape=jax.ShapeDtypeStruct((SMALL_ROWS, D), F32),
        scratch_shapes=[pltpu.VMEM((8, SMALL_ROWS, D), F32), pltpu.SemaphoreType.DMA((7,)),
                        pltpu.SemaphoreType.DMA((7,))],
    )(vec)


def _col_window(ref, slot, col, ncols):
    return ref.at[slot, :, pl.ds(col, ncols)]


def _stack_window(ref, slot, col, ncols):
    return ref.at[slot, :, :, pl.ds(col, ncols)]


def _row_tile(rows):
    for t in range(512, 15, -16):
        if rows % t == 0:
            return t
    return rows


def _same_shape_runs(arrs):
    runs, a = [], 0
    for b in range(1, len(arrs) + 1):
        if b == len(arrs) or arrs[b].shape != arrs[a].shape:
            runs.append((a, b))
            a = b
    return runs


class _Comm:
    def __init__(self):
        x, y, c = lax.axis_index("x"), lax.axis_index("y"), lax.axis_index("c")
        self.c_idx = jnp.reshape(c, (1,)).astype(jnp.int32)
        self.place = jnp.stack([2 * x + y, c]).astype(jnp.int32)
        self.groups = {}

    @staticmethod
    def gather(*bufs, relay):
        return _gather_rider(list(bufs), [_col_window if b.ndim == 3 else _stack_window for b in bufs], relay)

    def reduce_rider(self, tag, names, ps):
        csums = []
        for a, b in _same_shape_runs(ps):
            csums += _pair_sum("pair_sum_%s%d" % (tag, a), self.c_idx, ps[a:b], _row_tile(ps[a].shape[1]))
        self.groups[tag] = [names, csums, None]
        return _to_chips_rider(csums)

    def landed(self, tag, ts):
        self.groups[tag][2] = ts

    def finish(self):
        names, csums, ts = [], [], []
        for group_names, group_csums, group_ts in self.groups.values():
            names += group_names
            csums += group_csums
            ts += group_ts
        order = sorted(range(len(names)), key=lambda i: csums[i].shape[1])
        names, csums, ts = ([v[i] for i in order] for v in (names, csums, ts))
        halves = []
        for a, b in _same_shape_runs(csums):
            halves += _chip_sum("chip_sum_%d" % a, self.place, csums[a:b], ts[a:b], _row_tile(csums[a].shape[1]))
        return dict(zip(names, _swap_halves(halves)))


ROPE_THETA = 10000.0
SMALL_1K = ("ffn1_pre_norm", "ffn1_post_norm", "mix_pre_norm", "ssm_norm", "mix_post_norm",
            "ffn2_pre_norm", "ffn2_post_norm")
SMALL_16 = ("dt_bias", "a_log", "d_skip")
OFF_CONVB = 7 * D
OFF_16 = OFF_CONVB + CONV_C
OFF_CONVW = OFF_16 + 48
OFF_LOSS = OFF_CONVW + CONV_K * CONV_C
SMALL_LEN = SMALL_ROWS * D


def _sds(shape, dtype):
    return jax.ShapeDtypeStruct(shape, dtype)


def _ridden(res, rider):
    return res if rider is not None else (res, None)


def _ffn_down(name, act, w, tail_of, rider=None):
    tail, o_specs, o_shapes = tail_of(TS)
    return _mm(name, [act, w.dn], NN, (S // TS,),
               [pl.BlockSpec((NSH, TS, FS), lambda i: (0, i, 0)),
                pl.BlockSpec((NSH, None, FS, D), lambda i: (0, w.d0, 0, 0))], o_specs, o_shapes, rider, tail)


def _ffn_dw(name, a, b, rider=None):
    return _mm(name, [a, b], TN, (NSH,),
               [pl.BlockSpec((None, S, FS), lambda s: (s, 0, 0)), pl.BlockSpec((S, D), lambda s: (0, 0))],
               pl.BlockSpec((None, FS, D), lambda s: (s, 0, 0)), _sds((NSH, FS, D), BF16), rider)


def _ffn_dn(name, dgate, dup, w, tail_of, rider=None):
    rows = TS // 2
    tail, o_specs, o_shapes = tail_of(rows)
    a2 = pl.BlockSpec((NSH, rows, FS), lambda i: (0, i, 0))
    return _mm(name, [dgate, w.gu, dup, w.gu], NN, (S // rows,),
               [a2, pl.BlockSpec((NSH, None, FS, D), lambda i: (0, w.g0, 0, 0)),
                a2, pl.BlockSpec((NSH, None, FS, D), lambda i: (0, w.g0 + 1, 0, 0))], o_specs, o_shapes, rider, tail)


def _out_proj_dx(dh, wout):
    def body(dh_ref, w_ref, dyn_ref, do_ref):
        dm = _dot(dh_ref[...], w_ref[...], NT)
        dyn_ref[...] = dm[:, D:]
        for b in range(TS // 128):
            for j, blk in enumerate(_rows_to_blocks(dm[128 * b:128 * (b + 1), :D])):
                do_ref[j, b] = blk.astype(BF16)

    return pl.pallas_call(
        body, name="out_proj_dx", grid=(S // TS,),
        in_specs=[pl.BlockSpec((TS, D), lambda i: (i, 0)), pl.BlockSpec((2 * D, D), lambda i: (0, 0))],
        out_specs=[pl.BlockSpec((TS, D), lambda i: (i, 0)),
                   pl.BlockSpec((NKV, TS // 128, HD, QROWS), lambda i: (0, i, 0, 0))],
        out_shape=[_sds((S, D), F32), _sds((NKV, NCH, HD, QROWS), BF16)], compiler_params=_cparams("parallel"),
    )(dh, wout)


def _heads(t, n):
    return t.reshape(S, n, HD).transpose(1, 0, 2)


def _pad128(v):
    return jnp.pad(v, ((0, 0), (0, 128 - v.shape[1])))


def _local_step(x, positions, tgt, sp, gu1, d1, f2, wint, wout, convw, comm=None):
    inv_freq = ROPE_THETA ** (-jnp.arange(0, HD, 2, dtype=F32) / HD)
    ang = positions.astype(F32)[:, None] * inv_freq
    ang = jnp.concatenate([ang, ang, ang, ang], axis=-1)
    cos, sin = jnp.cos(ang), jnp.sin(ang)
    dtb, alog = _pad128(sp["dt_bias"]), _pad128(sp["a_log"])
    dskip_l = jnp.repeat(sp["d_skip"], HD, axis=1)
    convb = sp["conv_b"]

    n1 = _prenorm("prenorm1", x, sp["ffn1_pre_norm"])
    rider = comm.gather(d1, relay=False) if comm else None
    (fg1, fu1, act1), got = _ridden(_ffn_up("ffn1_up", n1, _FfnW(gu1, 0, d1, 0), rider), rider)
    if comm:
        d1, = got
    w1 = _FfnW(gu1, 0, d1, 0)
    rider = comm.gather(wint, relay=True) if comm else None
    (h1, x1, n2), got = _ridden(_ffn_down(
        "ffn1_down", act1, w1,
        lambda rows: _tail_postres(rows, x, sp["ffn1_post_norm"], 0.5, sp["mix_pre_norm"]), rider), rider)
    if comm:
        wint, = got
    wint_pad = jnp.pad(wint.reshape(WIN_COLS, D), ((0, WIN_PAD - WIN_COLS), (0, 0)))

    pw = WIN_PAD // 3
    proj = _mm("in_proj", [n2, wint_pad], NT, (S // TS, 3),
               [pl.BlockSpec((TS, D), lambda i, j: (i, 0)), pl.BlockSpec((pw, D), lambda i, j: (j, 0))],
               pl.BlockSpec((TS, pw), lambda i, j: (i, j)), _sds((S, WIN_PAD), F32))
    qt = _rope_q(proj, cos, sin)
    k_rot, v_bf, kt, vt = _rope_kv(proj, cos, sin)
    kh, vh = _heads(k_rot, NKV), _heads(v_bf, NKV)
    bias = _bias_table()
    rider = comm.gather(f2, wout, relay=False) if comm else None
    (ot, lse, attn), got = _ridden(_attn_fwd(qt, kh, vt, bias, rider), rider)
    if comm:
        f2, wout = got
    w2 = _FfnW(f2, 0, f2, 2)
    wout = wout.reshape(2 * D, D)
    xbc = _conv_fwd(proj, convw, convb)
    y, yn, hprev = _ssd_fwd(xbc, proj, dtb, alog, dskip_l, sp["ssm_norm"])
    mixed = jnp.concatenate([attn, yn], axis=1)
    tail, o_specs, o_shapes = _tail_postres(TS, x1, sp["mix_post_norm"], 1.0, sp["ffn2_pre_norm"])
    h2, x2, n3 = _mm("out_proj", [mixed, wout], NN, (S // TS,),
                     [pl.BlockSpec((TS, 2 * D), lambda i: (i, 0)), pl.BlockSpec((2 * D, D), lambda i: (0, 0))],
                     o_specs, o_shapes, None, tail)

    fg2, fu2, act2 = _ffn_up("ffn2_up", n3, w2)
    dy, dh3, dp3, loss = _ffn_down(
        "ffn2_down", act2, w2, lambda rows: _tail_final(rows, x2, sp["ffn2_post_norm"], tgt, 0.5))

    dgate2, dup2 = _ffn_dact("ffn2_dact", dh3, w2, fg2, fu2)
    dws2 = [_ffn_dw("ffn2_dwg", dgate2, n3), _ffn_dw("ffn2_dwu", dup2, n3), _ffn_dw("ffn2_dwd", act2, dh3)]
    dx2, dh2, dg3, dp2 = _ffn_dn(
        "ffn2_dn", dgate2, dup2, w2,
        lambda rows: _tail_mid_bwd(rows, dy, x2, sp["ffn2_pre_norm"], h2, sp["mix_post_norm"], 1.0))

    dyn, dot_ = _out_proj_dx(dh2, wout)
    dwout = _mm("out_proj_dw", [mixed, dh2], TN, (2,),
                [pl.BlockSpec((S, D), lambda m: (0, m)), pl.BlockSpec((S, D), lambda m: (0, 0))],
                pl.BlockSpec((D, D), lambda m: (m, 0)), _sds((2 * D, D), BF16))
    dwout = dwout.reshape(NSH, 2 * D // NSH, D)

    def riding(tag, names, ps, call):
        rider = comm.reduce_rider(tag, names, ps) if comm else None
        res, got = _ridden(call(rider), rider)
        if comm:
            comm.landed(tag, got)
        return res

    dxbc, dproj, ddt, dssm, dsc = _ssd_bwd(dyn, y, xbc, proj, hprev, dtb, alog, dskip_l, sp["ssm_norm"])
    dproj, dcw8, dcb = _conv_bwd(dxbc, proj, convw, convb, dproj)
    delta = _attn_delta(ot, dot_)
    dqt, dkh, dvh = riding("a", BIG[3:6] + ("w_out",), dws2 + [dwout], lambda rider: _attn_bwd(
        qt, kh, kt, vh, dot_, lse, delta, bias, rider))
    dproj = _rope_dq(dqt, cos, sin, dproj)
    dproj = _rope_dkv(dkh, dvh, cos, sin, dproj)
    dproj = lax.dynamic_update_slice(dproj, ddt, (0, COL_DT))
    dwint = _mm("in_proj_dw", [dproj, n2], TN, (3,),
                [pl.BlockSpec((S, pw), lambda j: (0, j)), pl.BlockSpec((S, D), lambda j: (0, 0))],
                pl.BlockSpec((pw, D), lambda j: (j, 0)), _sds((WIN_PAD, D), BF16))
    dwint = dwint[:WIN_COLS].reshape(NSH, WIN_SH, D)

    tail, o_specs, o_shapes = _tail_mid_bwd(TS, dx2, x1, sp["mix_pre_norm"], h1, sp["ffn1_post_norm"], 0.5)
    dx1, dh1, dg2, dp1 = riding("b", ("w_in",), [dwint], lambda rider: _mm(
        "in_proj_dx", [dproj, wint_pad], NN, (S // TS,),
        [pl.BlockSpec((TS, WIN_PAD), lambda i: (i, 0)), pl.BlockSpec((WIN_PAD, D), lambda i: (0, 0))],
        o_specs, o_shapes, rider, tail))

    dwd1 = _ffn_dw("ffn1_dwd", act1, dh1)
    dgate1, dup1 = riding("d", BIG[2:3], [dwd1], lambda rider: _ffn_dact("ffn1_dact", dh1, w1, fg1, fu1, rider))
    dwg1, dwu1 = _ffn_dw("ffn1_dwg", dgate1, n1), _ffn_dw("ffn1_dwu", dup1, n1)
    grad_x, dg1 = riding("g", BIG[0:2], [dwg1, dwu1], lambda rider: _ffn_dn(
        "ffn1_dn", dgate1, dup1, w1, lambda rows: _tail_first_bwd(rows, dx1, x, sp["ffn1_pre_norm"]), rider))
    dws1 = [dwg1, dwu1, dwd1]

    small = jnp.concatenate([
        dg1[0], dp1[0], dg2[0], dssm[0], dp2[0], dg3[0], dp3[0], dcb[0],
        dsc[0, :16], dsc[1, :16], dsc[2, :16], dcw8[:CONV_K].reshape(-1), loss[0, :1]])
    small = jnp.pad(small, (0, SMALL_LEN - small.shape[0])).reshape(SMALL_ROWS, D)
    if comm is None:
        return grad_x, dws1 + dws2 + [dwint, dwout], small
    return grad_x, comm.finish(), small


WEIGHTS = ("ffn1_pre_norm", "ffn1_w_gate", "ffn1_w_up", "ffn1_w_down", "ffn1_post_norm", "mix_pre_norm", "w_in",
           "conv_w", "conv_b", "dt_bias", "a_log", "d_skip", "ssm_norm", "w_out", "mix_post_norm", "ffn2_pre_norm",
           "ffn2_w_gate", "ffn2_w_up", "ffn2_w_down", "ffn2_post_norm")
BIG = ("ffn1_w_gate", "ffn1_w_up", "ffn1_w_down", "ffn2_w_gate", "ffn2_w_up", "ffn2_w_down", "w_in", "w_out")
TRANSPOSED = ("ffn1_w_gate", "ffn1_w_up", "ffn2_w_gate", "ffn2_w_up", "w_in")
SMALL_ORDER = SMALL_1K + ("conv_b",) + SMALL_16
CONVW_SH = CONV_C // NSH


def _shard2d(t, name):
    return t[0].T if name in TRANSPOSED else t[0]


def _unshard2d(t, name):
    return (t.T if name in TRANSPOSED else t)[None]


def _rows3d(t):
    return t.transpose(2, 0, 1)


def _pack_small(d, prefix, shard_of_convw):
    flat = jnp.concatenate([d[prefix + n][0] for n in SMALL_ORDER] + [shard_of_convw.reshape(-1)])
    return jnp.pad(flat, (0, SMALL_LEN - flat.shape[0])).reshape(SMALL_ROWS, D)


def _unpack_small(block, like):
    flat = block.reshape(-1)
    out, off = {}, 0
    for n in SMALL_ORDER:
        size = like[n].shape[1]
        out[n] = flat[off:off + size].reshape(1, size)
        off += size
    out["conv_w"] = flat[off:off + CONV_K * CONVW_SH].reshape(1, CONV_K, CONVW_SH)
    return out


def kernel(x, positions, ffn1_pre_norm, ffn1_w_gate, ffn1_w_up, ffn1_w_down, ffn1_post_norm, mix_pre_norm, w_in, conv_w, conv_b, dt_bias, a_log, d_skip, ssm_norm, w_out, mix_post_norm, ffn2_pre_norm, ffn2_w_gate, ffn2_w_up, ffn2_w_down, ffn2_post_norm, loss_target, m_ffn1_pre_norm, m_ffn1_w_gate, m_ffn1_w_up, m_ffn1_w_down, m_ffn1_post_norm, m_mix_pre_norm, m_w_in, m_conv_w, m_conv_b, m_dt_bias, m_a_log, m_d_skip, m_ssm_norm, m_w_out, m_mix_post_norm, m_ffn2_pre_norm, m_ffn2_w_gate, m_ffn2_w_up, m_ffn2_w_down, m_ffn2_post_norm, v_ffn1_pre_norm, v_ffn1_w_gate, v_ffn1_w_up, v_ffn1_w_down, v_ffn1_post_norm, v_mix_pre_norm, v_w_in, v_conv_w, v_conv_b, v_dt_bias, v_a_log, v_d_skip, v_ssm_norm, v_w_out, v_mix_post_norm, v_ffn2_pre_norm, v_ffn2_w_gate, v_ffn2_w_up, v_ffn2_w_down, v_ffn2_post_norm):
    given = dict(locals())
    xi, yi = lax.axis_index("x"), lax.axis_index("y")

    shard = jnp.reshape(2 * xi + yi, (1,)).astype(jnp.int32)
    big = {p + n: _shard2d(given[p + n], n) for n in BIG for p in ("", "m_", "v_")}
    gu1 = _cast_stack("cast_ffn1_gate_up", shard, [big[n] for n in BIG[0:2]], 176, D)
    d1 = _cast_stack("cast_ffn1_down", shard, [big[BIG[2]]], 176, D)
    f2 = _cast_stack("cast_ffn2", shard, [big[n] for n in BIG[3:6]], 176, D)
    winsh = _cast_stack("cast_w_in", shard, [big["w_in"]], WIN_SH, 256).reshape(NSH, WIN_SH, D)
    woutsh = _cast_stack("cast_w_out", shard, [big["w_out"]], 256, D).reshape(NSH, 2 * D // NSH, D)
    comm = _Comm()
    (gu1,), (cwf,) = _run_riders("gather_ffn1_gate_up", [comm.gather(gu1, relay=True), _small_gather_rider(conv_w[0])])
    convw = cwf.transpose(1, 0, 2).reshape(CONV_K, CONV_C)

    sp = {n: given[n] for n in SMALL_ORDER}
    grad_x, big_grads, small = _local_step(x[0], positions[0], loss_target[0], sp, gu1, d1, f2, winsh, woutsh,
                                           convw, comm)

    tot = _allreduce_small(small).reshape(-1)
    loss = tot[OFF_LOSS]
    small_grads, off = {}, 0
    for n in SMALL_ORDER:
        size = given[n].shape[1]
        small_grads[n] = tot[off:off + size].reshape(1, size)
        off += size
    dconvw = tot[OFF_CONVW:OFF_CONVW + CONV_K * CONV_C].reshape(CONV_K, NSH, CONVW_SH)
    dconvw = lax.dynamic_index_in_dim(dconvw, 2 * xi + yi, axis=1, keepdims=False)
    small_grads["conv_w"] = dconvw.reshape(1, CONV_K, CONVW_SH)

    upd = {}
    for names, tr in ((BIG[0:3], 176), (BIG[3:6], 176), (BIG[7:8], 256)):
        res = _adamw("adamw_" + names[0], [big[n] for n in names], [big_grads[n] for n in names],
                     [big["m_" + n] for n in names], [big["v_" + n] for n in names], tr, D)
        for n, r in zip(names, res):
            upd[n] = tuple(_unshard2d(t, n) for t in r)
    g_win = big_grads["w_in"].reshape(WIN_SH, 1, D)
    res, = _adamw("adamw_w_in", [_rows3d(w_in)], [g_win], [_rows3d(m_w_in)], [_rows3d(v_w_in)], WIN_SH // 4, D)
    upd["w_in"] = tuple(t.transpose(1, 2, 0) for t in res)
    (dl, m2, v2, _), = _adamw(
        "adamw_small", [_pack_small(given, "", conv_w[0])], [_pack_small(small_grads, "", dconvw)],
        [_pack_small(given, "m_", m_conv_w[0])], [_pack_small(given, "v_", v_conv_w[0])], SMALL_ROWS, D)
    dl, m2, v2 = (_unpack_small(t, given) for t in (dl, m2, v2))
    for n in SMALL_ORDER + ("conv_w",):
        upd[n] = (dl[n], m2[n], v2[n], small_grads[n])

    return (loss, grad_x[None], *[upd[n][3] for n in WEIGHTS], *[upd[n][0] for n in WEIGHTS],
            *[upd[n][1] for n in WEIGHTS], *[upd[n][2] for n in WEIGHTS])
```

```python
import functools
import typing

import jax
import jax.numpy as jnp
from jax import lax
from jax.experimental import pallas as pl
from jax.experimental.pallas import tpu as pltpu

F32 = jnp.float32
BF16 = jnp.bfloat16

S = 2048
D = 1024
FF = 2816
NSH = 4
FS = FF // NSH
HALF = D // 2
HD = 64
NKV = 4
NQ_PER_KV = 4
KVW = NKV * HD
QCOLS = NQ_PER_KV * HD
CONV_C = 1536
CONV_K = 4
SSM_W = 1024
NST = 128
NCH = S // 128
WIN_COLS = 4112
WIN_SH = WIN_COLS // NSH
WIN_PAD = 4224
COL_DT = 4096
EPS = 1e-6
NEG = -1e30

ADAM_LR = 0.001
ADAM_B1 = 0.9
ADAM_B2 = 0.999
ADAM_EPS = 1e-08
ADAM_WD = 0.01
ADAM_STEP = 10

VMEM_LIMIT = 56 * 1024 * 1024
TS = 512
TR = 256

NN = (((1,), (0,)), ((), ()))
NT = (((1,), (1,)), ((), ()))
TN = (((0,), (0,)), ((), ()))
MESH = pl.DeviceIdType.MESH


def _cparams(*sem):
    return pltpu.CompilerParams(dimension_semantics=sem, vmem_limit_bytes=VMEM_LIMIT)


def _dot(a, b, dims):
    return lax.dot_general(a.astype(BF16), b.astype(BF16), dims, preferred_element_type=F32)


def _bf16_pieces(v):
    hi = v.astype(BF16)
    rest = v - hi.astype(F32)
    mid = rest.astype(BF16)
    return hi, mid, (rest - mid.astype(F32)).astype(BF16)


def _dot_exact(a, b, ones="a"):
    if ones == "a":
        sel = a.astype(BF16)
        parts = [lax.dot_general(sel, p, NN, preferred_element_type=F32) for p in _bf16_pieces(b)]
    else:
        sel = b.astype(BF16)
        parts = [lax.dot_general(p, sel, NN, preferred_element_type=F32) for p in _bf16_pieces(a)]
    return (parts[2] + parts[1]) + parts[0]


def _sigmoid(v):
    return 1.0 / (1.0 + jnp.exp(-v))


class _Rider(typing.NamedTuple):
    operands: list
    out_shapes: list
    aliases: dict
    sems: list
    start: typing.Callable
    finish: typing.Callable


def _call(body, name, grid, in_specs, out_specs, out_shape, operands, scratch=(), sem=(), rider=None):
    multi = isinstance(out_shape, (list, tuple))
    if rider is None:
        return pl.pallas_call(
            body, name=name, grid=grid, in_specs=in_specs, out_specs=out_specs, out_shape=out_shape,
            scratch_shapes=list(scratch), compiler_params=_cparams(*sem))(*operands)
    outs = list(out_shape) if multi else [out_shape]
    ospecs = list(out_specs) if multi else [out_specs]
    n_in, n_out, n_scr = len(operands), len(outs), len(scratch)
    ri, ro = len(rider.operands), len(rider.out_shapes)

    def wrapped(*refs):
        o0 = n_in + ri
        s0 = o0 + n_out + ro
        rin, rout, rsem = refs[n_in:o0], refs[o0 + n_out:s0], refs[s0 + n_scr:]
        ids = [pl.program_id(a) for a in range(len(grid))]
        first = functools.reduce(jnp.logical_and, [i == 0 for i in ids])
        last = functools.reduce(jnp.logical_and, [i == g - 1 for i, g in zip(ids, grid)])

        @pl.when(first)
        def _():
            rider.start(rin, rout, rsem)

        body(*refs[:n_in], *refs[o0:o0 + n_out], *refs[s0:s0 + n_scr])

        @pl.when(last)
        def _():
            rider.finish(rin, rout, rsem)

    hbm = pl.BlockSpec(memory_space=pl.ANY)
    res = pl.pallas_call(
        wrapped, name=name, grid=grid, in_specs=list(in_specs) + [hbm] * ri, out_specs=ospecs + [hbm] * ro,
        out_shape=outs + list(rider.out_shapes), scratch_shapes=list(scratch) + list(rider.sems),
        input_output_aliases={n_in + k: n_out + v for k, v in rider.aliases.items()},
        compiler_params=_cparams(*(("arbitrary",) * len(grid))))(*operands, *rider.operands)
    main = list(res[:n_out])
    return (main if multi else main[0]), list(res[n_out:])


class _Tail(typing.NamedTuple):
    fn: typing.Callable
    operands: list
    in_specs: list


def _mm(name, operands, dims, grid, in_specs, o_spec, out_shape, rider=None, tail=None):
    npairs = len(operands) // 2
    extra = [] if tail is None else list(tail.operands)
    nin = 2 * npairs + len(extra)

    def body(*refs):
        t = None
        for i in range(npairs):
            a, b = refs[2 * i], refs[2 * i + 1]
            parts = [(a[s], b[s]) for s in range(a.shape[0])] if len(a.shape) == 3 else [(a[...], b[...])]
            for pa, pb in parts:
                d = _dot(pa, pb, dims)
                t = d if t is None else t + d
        if tail is None:
            refs[nin][...] = t.astype(refs[nin].dtype)
        else:
            tail.fn(t, refs[2 * npairs:nin], refs[nin:])

    sem = ("parallel" if tail is None else "arbitrary",) * len(grid)
    specs = list(in_specs) + ([] if tail is None else list(tail.in_specs))
    return _call(body, name, grid, specs, o_spec, out_shape, list(operands) + extra, (), sem, rider)


class _FfnW(typing.NamedTuple):
    gu: jax.Array
    g0: int
    dn: jax.Array
    d0: int


def _ffn_up(name, n, w, rider=None):
    def body(n_ref, wg_ref, wu_ref, fg_ref, fu_ref, a_ref):
        nb = n_ref[...]
        g = _dot(nb, wg_ref[...], NT)
        u = _dot(nb, wu_ref[...], NT)
        sg = _sigmoid(g)
        silu = g * sg
        fg_ref[...] = (u * (sg * (1.0 + g * (1.0 - sg)))).astype(BF16)
        fu_ref[...] = silu.astype(BF16)
        a_ref[...] = (silu * u).astype(BF16)

    out = jax.ShapeDtypeStruct((NSH, S, FS), BF16)
    ospec = pl.BlockSpec((None, TS, FS), lambda s, i: (s, i, 0))
    return _call(
        body, name, (NSH, S // TS),
        [pl.BlockSpec((TS, D), lambda s, i: (i, 0)),
         pl.BlockSpec((None, None, FS, D), lambda s, i: (s, w.g0, 0, 0)),
         pl.BlockSpec((None, None, FS, D), lambda s, i: (s, w.g0 + 1, 0, 0))],
        [ospec, ospec, ospec], [out, out, out], (n, w.gu, w.gu), sem=("parallel", "parallel"), rider=rider)


def _ffn_dact(name, dh, w, fgate, fup, rider=None):
    def body(dh_ref, wd_ref, fg_ref, fu_ref, dg_ref, du_ref):
        da = _dot(dh_ref[...], wd_ref[...], NT)
        dg_ref[...] = (da * fg_ref[...].astype(F32)).astype(BF16)
        du_ref[...] = (da * fu_ref[...].astype(F32)).astype(BF16)

    out = jax.ShapeDtypeStruct((NSH, S, FS), BF16)
    aspec = pl.BlockSpec((None, TS, FS), lambda s, i: (s, i, 0))
    return _call(
        body, name, (NSH, S // TS),
        [pl.BlockSpec((TS, D), lambda s, i: (i, 0)),
         pl.BlockSpec((None, None, FS, D), lambda s, i: (s, w.d0, 0, 0)), aspec, aspec],
        [aspec, aspec], [out, out], (dh, w.dn, fgate, fup), sem=("parallel", "parallel"), rider=rider)


def _rstd(v):
    return lax.rsqrt(jnp.mean(v * v, axis=-1, keepdims=True) + EPS)


def _row_spec():
    return pl.BlockSpec((TR, D), lambda i: (i, 0))


def _vec_spec():
    return pl.BlockSpec((1, D), lambda i: (0, 0))


def _acc_rows(ref, v):
    @pl.when(pl.program_id(0) == 0)
    def _():
        ref[...] = jnp.zeros_like(ref)
    ref[...] += jnp.sum(v, axis=0, keepdims=True)


def _prenorm(name, x, g):
    def body(x_ref, g_ref, n_ref):
        xv = x_ref[...]
        n_ref[...] = (xv * _rstd(xv) * g_ref[...]).astype(BF16)

    return pl.pallas_call(
        body, name=name, grid=(S // TR,), in_specs=[_row_spec(), _vec_spec()], out_specs=_row_spec(),
        out_shape=jax.ShapeDtypeStruct((S, D), BF16), compiler_params=_cparams("parallel"),
    )(x, g)


def _rows_spec(rows):
    return pl.BlockSpec((rows, D), lambda i: (i, 0))


def _rows_f32():
    return jax.ShapeDtypeStruct((S, D), F32)


def _rows_bf16():
    return jax.ShapeDtypeStruct((S, D), BF16)


def _vec_f32():
    return jax.ShapeDtypeStruct((1, D), F32)


def _tail_postres(rows, x, p, alpha, gnext):
    def fn(h, ins, outs):
        x_ref, p_ref, g_ref = ins
        h_ref, xo_ref, n_ref = outs
        h_ref[...] = h
        xo = x_ref[...] + alpha * (h * _rstd(h) * p_ref[...])
        xo_ref[...] = xo
        n_ref[...] = (xo * _rstd(xo) * g_ref[...]).astype(BF16)

    rs = _rows_spec(rows)
    return (_Tail(fn, [x, p, gnext], [rs, _vec_spec(), _vec_spec()]), [rs, rs, rs],
            [_rows_f32(), _rows_f32(), _rows_bf16()])


def _tail_final(rows, x, p, tgt, alpha):
    def fn(h, ins, outs):
        x_ref, p_ref, t_ref = ins
        dy_ref, dh_ref, dp_ref, loss_ref = outs
        r = _rstd(h)
        hn = h * r
        pv = p_ref[...]
        e = x_ref[...] + alpha * (hn * pv) - t_ref[...]
        dy = e * (1.0 / D)
        dy_ref[...] = dy
        du = alpha * dy * pv
        dh_ref[...] = (r * (du - hn * jnp.mean(du * hn, axis=-1, keepdims=True))).astype(BF16)
        _acc_rows(dp_ref, alpha * dy * hn)
        part = 0.5 * jnp.sum(jnp.mean(e * e, axis=-1, keepdims=True), axis=0, keepdims=True)
        _acc_rows(loss_ref, jnp.broadcast_to(part, (1, 128)))

    rs = _rows_spec(rows)
    return (_Tail(fn, [x, p, tgt], [rs, _vec_spec(), rs]),
            [rs, rs, _vec_spec(), pl.BlockSpec((1, 128), lambda i: (0, 0))],
            [_rows_f32(), _rows_bf16(), _vec_f32(), jax.ShapeDtypeStruct((1, 128), F32)])


def _norm_bwd(dn, xv, g_ref, dg_ref):
    r = _rstd(xv)
    xn = xv * r
    dng = dn * g_ref[...]
    _acc_rows(dg_ref, dn * xn)
    return r * (dng - xn * jnp.mean(dng * xn, axis=-1, keepdims=True))


def _tail_mid_bwd(rows, dres, x, g, h, p, alpha):
    def fn(dn, ins, outs):
        dr_ref, x_ref, g_ref, h_ref, p_ref = ins
        dx_ref, dh_ref, dg_ref, dp_ref = outs
        dx = dr_ref[...] + _norm_bwd(dn, x_ref[...], g_ref, dg_ref)
        dx_ref[...] = dx
        hv = h_ref[...]
        r = _rstd(hv)
        hn = hv * r
        du = alpha * dx * p_ref[...]
        dh_ref[...] = (r * (du - hn * jnp.mean(du * hn, axis=-1, keepdims=True))).astype(BF16)
        _acc_rows(dp_ref, alpha * dx * hn)

    rs = _rows_spec(rows)
    return (_Tail(fn, [dres, x, g, h, p], [rs, rs, _vec_spec(), rs, _vec_spec()]),
            [rs, rs, _vec_spec(), _vec_spec()], [_rows_f32(), _rows_bf16(), _vec_f32(), _vec_f32()])


def _tail_first_bwd(rows, dres, x, g):
    def fn(dn, ins, outs):
        dr_ref, x_ref, g_ref = ins
        dx_ref, dg_ref = outs
        dx_ref[...] = dr_ref[...] + _norm_bwd(dn, x_ref[...], g_ref, dg_ref)

    rs = _rows_spec(rows)
    return (_Tail(fn, [dres, x, g], [rs, rs, _vec_spec()]), [rs, _vec_spec()], [_rows_f32(), _vec_f32()])


def _rotate(t, c128, s128, sign, scale):
    width = t.shape[1]
    c = jnp.tile(c128, (1, width // 128))
    sn = jnp.tile(s128, (1, width // 128))
    lane = lax.broadcasted_iota(jnp.int32, t.shape, 1) & (HD - 1)
    rot = jnp.where(lane < HD // 2, -pltpu.roll(t, width - HD // 2, 1), pltpu.roll(t, HD // 2, 1))
    return (t * c + sign * (rot * sn)) * scale


def _rows_to_blocks(y):
    out = []
    for j in range(NKV):
        yt = y[:, QCOLS * j:QCOLS * (j + 1)].T
        out.append(jnp.concatenate([yt[HD * g:HD * (g + 1)] for g in range(NQ_PER_KV)], axis=1))
    return out


def _blocks_to_rows(blocks):
    cols = []
    for b in blocks:
        stacked = jnp.concatenate([b[:, 128 * g:128 * (g + 1)] for g in range(NQ_PER_KV)], axis=0)
        cols.append(stacked.T)
    return jnp.concatenate(cols, axis=1)


def _rope_q(proj, cos, sin):
    def body(t_ref, c_ref, s_ref, o_ref):
        y = _rotate(t_ref[...], c_ref[...], s_ref[...], 1.0, HD ** -0.5)
        for j, blk in enumerate(_rows_to_blocks(y)):
            o_ref[j] = blk.astype(BF16)

    return pl.pallas_call(
        body, name="rope_q", grid=(NCH,),
        in_specs=[pl.BlockSpec((128, D), lambda i: (i, 0)),
                  pl.BlockSpec((128, 128), lambda i: (i, 0)), pl.BlockSpec((128, 128), lambda i: (i, 0))],
        out_specs=pl.BlockSpec((NKV, None, HD, QROWS), lambda i: (0, i, 0, 0)),
        out_shape=jax.ShapeDtypeStruct((NKV, NCH, HD, QROWS), BF16), compiler_params=_cparams("parallel"),
    )(proj, cos, sin)


def _rope_dq(dqt, cos, sin, dproj):
    def body(t_ref, c_ref, s_ref, buf_ref, o_ref):
        t = _blocks_to_rows([t_ref[j] for j in range(NKV)])
        o_ref[...] = _rotate(t, c_ref[...], s_ref[...], -1.0, HD ** -0.5).astype(BF16)

    return pl.pallas_call(
        body, name="rope_dq", grid=(NCH,),
        in_specs=[pl.BlockSpec((NKV, None, HD, QROWS), lambda i: (0, i, 0, 0)),
                  pl.BlockSpec((128, 128), lambda i: (i, 0)), pl.BlockSpec((128, 128), lambda i: (i, 0)),
                  pl.BlockSpec(memory_space=pl.ANY)],
        out_specs=pl.BlockSpec((128, D), lambda i: (i, 0)),
        out_shape=jax.ShapeDtypeStruct(dproj.shape, BF16), input_output_aliases={3: 0},
        compiler_params=_cparams("parallel"),
    )(dqt, cos, sin, dproj)


def _rope_dkv(dkt, dvt, cos, sin, dproj):
    def body(k_ref, v_ref, c_ref, s_ref, buf_ref, o_ref):
        dk = jnp.concatenate([k_ref[j] for j in range(NKV)], axis=0).T
        dv = jnp.concatenate([v_ref[j] for j in range(NKV)], axis=0).T
        dk = _rotate(dk, c_ref[...], s_ref[...], -1.0, 1.0)
        o_ref[...] = jnp.concatenate([dk, dv], axis=1).astype(BF16)

    tspec = pl.BlockSpec((NKV, HD, 128), lambda i: (0, 0, i))
    return pl.pallas_call(
        body, name="rope_dkv", grid=(NCH,),
        in_specs=[tspec, tspec, pl.BlockSpec((128, 128), lambda i: (i, 0)), pl.BlockSpec((128, 128), lambda i: (i, 0)),
                  pl.BlockSpec(memory_space=pl.ANY)],
        out_specs=pl.BlockSpec((128, 2 * KVW), lambda i: (i, D // (2 * KVW))),
        out_shape=jax.ShapeDtypeStruct(dproj.shape, BF16), input_output_aliases={4: 0},
        compiler_params=_cparams("parallel"),
    )(dkt, dvt, cos, sin, dproj)


def _rope_kv(proj, cos, sin):
    def body(t_ref, c_ref, s_ref, k_ref, v_ref, kt_ref, vt_ref):
        t = t_ref[...]
        k = _rotate(t[:, :KVW], c_ref[...], s_ref[...], 1.0, 1.0).astype(BF16)
        v = t[:, KVW:].astype(BF16)
        k_ref[...] = k
        v_ref[...] = v
        kt, vt = k.astype(F32).T, v.astype(F32).T
        for j in range(NKV):
            kt_ref[j] = kt[HD * j:HD * (j + 1)].astype(BF16)
            vt_ref[j] = vt[HD * j:HD * (j + 1)].astype(BF16)

    rows = pl.BlockSpec((128, KVW), lambda i: (i, 0))
    tspec = pl.BlockSpec((NKV, HD, 128), lambda i: (0, 0, i))
    return pl.pallas_call(
        body, name="rope_kv", grid=(NCH,),
        in_specs=[pl.BlockSpec((128, 2 * KVW), lambda i: (i, D // (2 * KVW))),
                  pl.BlockSpec((128, 128), lambda i: (i, 0)), pl.BlockSpec((128, 128), lambda i: (i, 0))],
        out_specs=[rows, rows, tspec, tspec],
        out_shape=[jax.ShapeDtypeStruct((S, KVW), BF16)] * 2 + [jax.ShapeDtypeStruct((NKV, HD, S), BF16)] * 2,
        compiler_params=_cparams("parallel"),
    )(proj, cos, sin)


QROWS = NQ_PER_KV * 128


NBIAS = NCH + 1
KV_PER_STEP = 4


def _bias_table():
    db = lax.broadcasted_iota(jnp.int32, (NBIAS, 128, QROWS), 0) - 1
    ki = lax.broadcasted_iota(jnp.int32, (NBIAS, 128, QROWS), 1)
    qi = lax.broadcasted_iota(jnp.int32, (NBIAS, 128, QROWS), 2) & 127
    d = db * 128 + qi - ki
    cnt = ((d <= 128).astype(F32) + (((d & 3) == 0) & (d <= 512)).astype(F32) + ((d & 15) == 0).astype(F32))
    return jnp.where((d >= 0) & (cnt > 0.0), jnp.log(jnp.maximum(cnt, 1.0)), NEG)


def _qt_spec():
    return pl.BlockSpec((None, None, HD, QROWS), lambda j, i: (j, i, 0, 0))


def _stat_spec():
    return pl.BlockSpec((None, None, 1, QROWS), lambda j, i: (j, i, 0, 0))


def _attn_fwd(qt, kh, vt, bias, rider=None):
    def body(q_ref, k_ref, v_ref, b_ref, o_ref, lse_ref, rows_ref, m_ref, l_ref, acc_ref):
        qb = pl.program_id(1)
        m_ref[...] = jnp.full_like(m_ref, NEG)
        l_ref[...] = jnp.zeros_like(l_ref)
        acc_ref[...] = jnp.zeros_like(acc_ref)

        def keys(off, size, bias_):
            for h in range(KV_PER_STEP):
                m = m_ref[h]
                s = _dot(k_ref[h, pl.ds(off, size), :], q_ref[h], NN) + bias_
                m_new = jnp.maximum(m, jnp.max(s, axis=0, keepdims=True))
                p = jnp.exp(s - m_new)
                a = jnp.exp(m - m_new)
                m_ref[h] = m_new
                l_ref[h] = a * l_ref[h] + jnp.sum(p, axis=0, keepdims=True)
                acc_ref[h] = a * acc_ref[h] + _dot(v_ref[h, :, pl.ds(off, size)], p, NN)

        def blocks(first, count):
            bias_ = jnp.concatenate([b_ref[qb - first - j + 1] for j in range(count)], axis=0)
            keys(pl.multiple_of(first * 128, 128), 128 * count, bias_)

        nkb = qb + 1
        @pl.loop(0, nkb // 4)
        def _(i):
            blocks(4 * i, 4)

        @pl.when(nkb % 4 >= 2)
        def _():
            blocks(nkb // 4 * 4, 2)

        @pl.when(nkb % 2 == 1)
        def _():
            blocks(qb, 1)

        outs = []
        for h in range(KV_PER_STEP):
            outs.append(acc_ref[h] / l_ref[h])
            o_ref[h] = outs[h]
            lse_ref[h] = m_ref[h] + jnp.log(l_ref[h])
        rows_ref[...] = _blocks_to_rows(outs).astype(BF16)

    kvs = KV_PER_STEP
    qspec = pl.BlockSpec((kvs, None, HD, QROWS), lambda j, i: (j, i, 0, 0))
    return _call(
        body, "attn_fwd", (NKV // kvs, NCH),
        [qspec, pl.BlockSpec((kvs, S, HD), lambda j, i: (j, 0, 0)),
         pl.BlockSpec((kvs, HD, S), lambda j, i: (j, 0, 0)),
         pl.BlockSpec((NBIAS, 128, QROWS), lambda j, i: (0, 0, 0))],
        [qspec, pl.BlockSpec((kvs, None, 1, QROWS), lambda j, i: (j, i, 0, 0)),
         pl.BlockSpec((128, QCOLS * kvs), lambda j, i: (i, j))],
        [jax.ShapeDtypeStruct((NKV, NCH, HD, QROWS), F32), jax.ShapeDtypeStruct((NKV, NCH, 1, QROWS), F32),
         jax.ShapeDtypeStruct((S, D), BF16)],
        (qt, kh, vt, bias),
        [pltpu.VMEM((kvs, 1, QROWS), F32), pltpu.VMEM((kvs, 1, QROWS), F32), pltpu.VMEM((kvs, HD, QROWS), F32)],
        ("parallel", "parallel"), rider)


def _attn_delta(ot, dot_):
    def body(o_ref, do_ref, dl_ref):
        dl_ref[...] = jnp.sum(o_ref[...] * do_ref[...].astype(F32), axis=1, keepdims=True)

    spec = pl.BlockSpec((None, NCH, HD, QROWS), lambda j: (j, 0, 0, 0))
    return pl.pallas_call(
        body, name="attn_delta", grid=(NKV,), in_specs=[spec, spec],
        out_specs=pl.BlockSpec((None, NCH, 1, QROWS), lambda j: (j, 0, 0, 0)),
        out_shape=jax.ShapeDtypeStruct((NKV, NCH, 1, QROWS), F32), compiler_params=_cparams("parallel"),
    )(ot, dot_)


def _attn_bwd(qt, kh, kt, vh, dot_, lse, delta, bias, rider=None):
    def body(qt_ref, k_ref, kt_ref, v_ref, dot_ref, lse_ref, dl_ref, b_ref, dq_ref, dk_ref, dv_ref):
        kp = pl.program_id(1)

        @pl.when(kp == 0)
        def _():
            dq_ref[...] = jnp.zeros_like(dq_ref)

        dk_ref[...] = jnp.zeros_like(dk_ref)
        dv_ref[...] = jnp.zeros_like(dv_ref)

        @pl.loop(kp, NCH // 2)
        def _(j):
            for h in range(KV_PER_STEP):
                k, kt_, v = k_ref[h], kt_ref[h], v_ref[h]
                for qb in (2 * j, 2 * j + 1):
                    bias2 = jnp.concatenate([b_ref[qb - 2 * kp + 1], b_ref[qb - 2 * kp]], axis=0)
                    st = _dot(k, qt_ref[h, qb], NN) + bias2
                    pt = jnp.exp(st - lse_ref[h, qb])
                    dst = pt * (_dot(v, dot_ref[h, qb], NN) - dl_ref[h, qb])
                    dq_ref[h, qb] += _dot(kt_, dst, NN)
                    dk_ref[h] += _dot(qt_ref[h, qb], dst, NT)
                    dv_ref[h] += _dot(dot_ref[h, qb], pt, NT)

    kvs = KV_PER_STEP
    tspec = pl.BlockSpec((kvs, NCH, HD, QROWS), lambda j, i: (j, 0, 0, 0))
    kspec = pl.BlockSpec((kvs, 256, HD), lambda j, i: (j, i, 0))
    ktspec = pl.BlockSpec((kvs, HD, 256), lambda j, i: (j, 0, i))
    sspec = pl.BlockSpec((kvs, NCH, 1, QROWS), lambda j, i: (j, 0, 0, 0))
    return _call(
        body, "attn_bwd", (NKV // kvs, NCH // 2),
        [tspec, kspec, ktspec, kspec, tspec, sspec, sspec,
         pl.BlockSpec((NBIAS, 128, QROWS), lambda j, i: (0, 0, 0))],
        [tspec, ktspec, ktspec],
        [jax.ShapeDtypeStruct((NKV, NCH, HD, QROWS), F32),
         jax.ShapeDtypeStruct((NKV, HD, S), F32), jax.ShapeDtypeStruct((NKV, HD, S), F32)],
        (qt, kh, kt, vh, dot_, lse, delta, bias), sem=("parallel", "arbitrary"), rider=rider)


CONV_BLK = 256
CONV_COL0 = 1536 // CONV_BLK


def _shift_down(u, j, row):
    return jnp.where(row >= j, pltpu.roll(u, j, 0), 0.0)


def _conv_pre(u, w_ref, b_ref, row):
    y = b_ref[...] + w_ref[CONV_K - 1:CONV_K, :] * u
    for j in range(1, CONV_K):
        y = y + w_ref[CONV_K - 1 - j:CONV_K - j, :] * _shift_down(u, j, row)
    return y


def _conv_fwd(proj, convw, convb):
    def body(u_ref, w_ref, b_ref, o_ref):
        u = u_ref[...]
        row = lax.broadcasted_iota(jnp.int32, u.shape, 0)
        y = _conv_pre(u, w_ref, b_ref, row)
        o_ref[...] = y * _sigmoid(y)

    return pl.pallas_call(
        body, name="conv_fwd", grid=(CONV_C // CONV_BLK,),
        in_specs=[pl.BlockSpec((S, CONV_BLK), lambda i: (0, CONV_COL0 + i)),
                  pl.BlockSpec((CONV_K, CONV_BLK), lambda i: (0, i)),
                  pl.BlockSpec((1, CONV_BLK), lambda i: (0, i))],
        out_specs=pl.BlockSpec((S, CONV_BLK), lambda i: (0, i)),
        out_shape=jax.ShapeDtypeStruct((S, CONV_C), F32), compiler_params=_cparams("parallel"),
    )(proj, convw, convb)


def _conv_bwd(dact, proj, convw, convb, dproj):
    def body(da_ref, u_ref, w_ref, b_ref, buf_ref, du_ref, dw_ref, db_ref):
        u = u_ref[...]
        row = lax.broadcasted_iota(jnp.int32, u.shape, 0)
        y = _conv_pre(u, w_ref, b_ref, row)
        sg = _sigmoid(y)
        dy = da_ref[...] * (sg * (1.0 + y * (1.0 - sg)))
        db_ref[...] = jnp.sum(dy, axis=0, keepdims=True)
        du = w_ref[CONV_K - 1:CONV_K, :] * dy
        r8 = lax.broadcasted_iota(jnp.int32, (8, CONV_BLK), 0)
        dw = jnp.where(r8 == CONV_K - 1, jnp.sum(dy * u, axis=0, keepdims=True), 0.0)
        for j in range(1, CONV_K):
            du = du + w_ref[CONV_K - 1 - j:CONV_K - j, :] * jnp.where(row < S - j, pltpu.roll(dy, S - j, 0), 0.0)
            dw = dw + jnp.where(r8 == CONV_K - 1 - j,
                                jnp.sum(dy * _shift_down(u, j, row), axis=0, keepdims=True), 0.0)
        du_ref[...] = du.astype(BF16)
        dw_ref[...] = dw

    return pl.pallas_call(
        body, name="conv_bwd", grid=(CONV_C // CONV_BLK,),
        in_specs=[pl.BlockSpec((S, CONV_BLK), lambda i: (0, i)),
                  pl.BlockSpec((S, CONV_BLK), lambda i: (0, CONV_COL0 + i)),
                  pl.BlockSpec((CONV_K, CONV_BLK), lambda i: (0, i)),
                  pl.BlockSpec((1, CONV_BLK), lambda i: (0, i)), pl.BlockSpec(memory_space=pl.ANY)],
        out_specs=[pl.BlockSpec((S, CONV_BLK), lambda i: (0, CONV_COL0 + i)),
                   pl.BlockSpec((8, CONV_BLK), lambda i: (0, i)), pl.BlockSpec((1, CONV_BLK), lambda i: (0, i))],
        out_shape=[jax.ShapeDtypeStruct(dproj.shape, BF16), jax.ShapeDtypeStruct((8, CONV_C), F32),
                   jax.ShapeDtypeStruct((1, CONV_C), F32)],
        input_output_aliases={4: 0}, compiler_params=_cparams("parallel"),
    )(dact, proj, convw, convb, dproj)


NPAIR = 8


def _ssd_scalars(dtr_ref, dtb_ref, alog_ref):
    z = dtr_ref[...] + dtb_ref[...]
    dt = jnp.maximum(z, 0.0) + jnp.log(1.0 + jnp.exp(-jnp.abs(z)))
    a = -jnp.exp(alog_ref[...])
    r = lax.broadcasted_iota(jnp.int32, (128, 128), 0)
    c = lax.broadcasted_iota(jnp.int32, (128, 128), 1)
    tri = (r >= c).astype(F32)
    cs = _dot_exact(tri, dt * a)
    return z, dt, a, cs, r, c


def _by_lane(cs, dt):
    head = lax.broadcasted_iota(jnp.int32, (128, SSM_W), 0)
    lane = lax.broadcasted_iota(jnp.int32, (128, SSM_W), 1)
    sel = (head == lane // HD).astype(F32)
    cs_l = _dot_exact(cs, sel, "b")
    last_l = cs_l[127:128, :]
    return sel, jnp.exp(cs_l), jnp.exp(last_l - cs_l), _dot_exact(dt, sel, "b")


def _pair_terms(cs, h1, h2):
    return (cs[:, h1:h1 + 1], cs[:, h2:h2 + 1],
            jnp.exp(cs[127:128, h1:h1 + 1]), jnp.exp(cs[127:128, h2:h2 + 1]))


def _gate_norm(y, zv, w):
    yg = y * (zv * _sigmoid(zv))
    outs, rs = [], []
    for g in range(2):
        blk = yg[:, 512 * g:512 * (g + 1)]
        r = lax.rsqrt(jnp.mean(blk * blk, axis=-1, keepdims=True) + EPS)
        outs.append(blk * r)
        rs.append(r)
    return jnp.concatenate(outs, axis=1), rs, yg


def _ssd_fwd(xbc, proj, dtb, alog, dskip_l, ssmw):
    def body(x_ref, b_ref, c_ref, dtr_ref, z_ref, dtb_ref, alog_ref, dsk_ref, w_ref, y_ref, yn_ref, hp_ref, h_ref):
        @pl.when(pl.program_id(0) == 0)
        def _():
            h_ref[...] = jnp.zeros_like(h_ref)

        _, dt, _, cs, r, c = _ssd_scalars(dtr_ref, dtb_ref, alog_ref)
        cst = cs.T
        causal = r >= c
        lo = c < HD
        _, e_all, dte_all, dt_all = _by_lane(cs, dt)
        hp_ref[...] = h_ref[...]
        for g in range(2):
            bg = b_ref[:, 128 * g:128 * (g + 1)]
            cg = c_ref[:, 128 * g:128 * (g + 1)]
            cb = _dot(cg, bg, NT)
            for j in range(4):
                pj = 4 * g + j
                h1, h2 = 2 * pj, 2 * pj + 1
                sl = slice(128 * pj, 128 * (pj + 1))
                xp = x_ref[:, sl]
                c1, c2, cd1, cd2 = _pair_terms(cs, h1, h2)
                e_l, dte_l = e_all[:, sl], dte_all[:, sl]
                xdt = xp * dt_all[:, sl]
                m1 = cb * jnp.exp(jnp.where(causal, c1 - cst[h1:h1 + 1, :], NEG))
                m2 = cb * jnp.exp(jnp.where(causal, c2 - cst[h2:h2 + 1, :], NEG))
                yd = jnp.where(lo, _dot(m1, xdt, NN), _dot(m2, xdt, NN))
                hp = h_ref[pj]
                yo = _dot(cg, hp, NT) * e_l
                st = _dot(xdt * dte_l, bg, TN)
                h_ref[pj] = hp * jnp.where(r < HD, cd1, cd2) + st
                y_ref[:, sl] = yd + yo + dsk_ref[:, sl] * xp
        yn, _, _ = _gate_norm(y_ref[...], z_ref[...], w_ref[...])
        yn_ref[...] = (yn * w_ref[...]).astype(BF16)

    return pl.pallas_call(
        body, name="ssd_fwd", grid=(NCH,),
        in_specs=[pl.BlockSpec((128, SSM_W), lambda i: (i, 0)),
                  pl.BlockSpec((128, 256), lambda i: (i, 4)), pl.BlockSpec((128, 256), lambda i: (i, 5)),
                  pl.BlockSpec((128, 128), lambda i: (i, COL_DT // 128)),
                  pl.BlockSpec((128, SSM_W), lambda i: (i, 3)),
                  pl.BlockSpec((1, 128), lambda i: (0, 0)), pl.BlockSpec((1, 128), lambda i: (0, 0)),
                  pl.BlockSpec((1, SSM_W), lambda i: (0, 0)), pl.BlockSpec((1, SSM_W), lambda i: (0, 0))],
        out_specs=[pl.BlockSpec((128, SSM_W), lambda i: (i, 0)), pl.BlockSpec((128, SSM_W), lambda i: (i, 0)),
                   pl.BlockSpec((None, NPAIR, 128, 128), lambda i: (i, 0, 0, 0))],
        out_shape=[jax.ShapeDtypeStruct((S, SSM_W), F32), jax.ShapeDtypeStruct((S, SSM_W), BF16),
                   jax.ShapeDtypeStruct((NCH, NPAIR, 128, 128), F32)],
        scratch_shapes=[pltpu.VMEM((NPAIR, 128, 128), F32)],
        compiler_params=_cparams("arbitrary"),
    )(xbc, xbc, xbc, proj, proj, dtb, alog, dskip_l, ssmw)


def _ssd_bwd(dmixed, y, xbc, proj, hprev, dtb, alog, dskip_l, ssmw, rider=None):
    def body(dyn_ref, y_ref, x_ref, b_ref, c_ref, dtr_ref, z_ref, hp_ref, dtb_ref, alog_ref, dsk_ref, w_ref,
             dxbc_ref, dz_ref, ddt_ref, dw_ref, dsc_ref, g_ref):
        @pl.when(pl.program_id(0) == 0)
        def _():
            g_ref[...] = jnp.zeros_like(g_ref)
            dsc_ref[...] = jnp.zeros_like(dsc_ref)

        z, dt, a, cs, r, c = _ssd_scalars(dtr_ref, dtb_ref, alog_ref)
        cst = cs.T
        causal = r >= c
        lo = c < HD

        yv = y_ref[...]
        zv = z_ref[...]
        wv = w_ref[...]
        ygn, rs, yg = _gate_norm(yv, zv, wv)
        dyn = dyn_ref[...]
        _acc_rows(dw_ref, dyn * ygn)
        dynw = dyn * wv
        parts = []
        for g in range(2):
            sl = slice(512 * g, 512 * (g + 1))
            a_g, n_g = dynw[:, sl], ygn[:, sl]
            parts.append(rs[g] * (a_g - n_g * jnp.mean(a_g * n_g, axis=-1, keepdims=True)))
        dyg = jnp.concatenate(parts, axis=1)
        sz = _sigmoid(zv)
        dz_ref[...] = (dyg * yv * (sz * (1.0 + zv * (1.0 - sz)))).astype(BF16)
        dy_all = dyg * (zv * sz)

        dcs_cols = jnp.zeros((128, 128), F32)
        dcs_rows = jnp.zeros((128, 128), F32)
        sel, e_all, dte_all, dt_all = _by_lane(cs, dt)
        x_all, b_all, c_all, dsk_all = x_ref[...], b_ref[...], c_ref[...], dsk_ref[...]
        hp_all, g_all = hp_ref[...], g_ref[...]
        g_new, dx_parts, db_parts, dc_parts = [], [], [], []
        dyx_parts, ryo_parts, qx_parts, dxx_parts, gh_parts = [], [], [], [], []
        for g in range(2):
            bg = b_all[:, 128 * g:128 * (g + 1)]
            cg = c_all[:, 128 * g:128 * (g + 1)]
            cb = _dot(cg, bg, NT)
            dcb = jnp.zeros((128, 128), F32)
            db_acc = jnp.zeros((128, NST), F32)
            dc_acc = jnp.zeros((128, NST), F32)
            for j in range(4):
                pj = 4 * g + j
                h1, h2 = 2 * pj, 2 * pj + 1
                sl = slice(128 * pj, 128 * (pj + 1))
                xp = x_all[:, sl]
                dyp = dy_all[:, sl]
                c1, c2, cd1, cd2 = _pair_terms(cs, h1, h2)
                e_l, dte_l, dt_l = e_all[:, sl], dte_all[:, sl], dt_all[:, sl]
                xdt = xp * dt_l
                hp = hp_all[pj]
                gp = g_all[pj]
                dyx_parts.append(dyp * xp)
                dzs = dyp * e_l
                dc_acc = dc_acc + _dot(dzs, hp, NN)
                ryo_parts.append(dyp * (_dot(cg, hp, NT) * e_l))
                qm = _dot(bg, gp, NT)
                dxdt = qm * dte_l
                qx_parts.append(qm * xdt)
                db_acc = db_acc + _dot(xdt * dte_l, gp, NN)
                gh_parts.append(gp * hp)
                g_new.append(_dot(dzs, cg, TN) + jnp.where(r < HD, cd1, cd2) * gp)
                for hh, ch, msk in ((h1, c1, lo), (h2, c2, jnp.logical_not(lo))):
                    lm = jnp.exp(jnp.where(causal, ch - cst[hh:hh + 1, :], NEG))
                    mm = cb * lm
                    dm = jnp.where(causal, _dot(jnp.where(msk, dyp, 0.0), xdt, NT), 0.0)
                    w = dm * mm
                    dcs_cols = dcs_cols + jnp.where(c == hh, jnp.sum(w, axis=1, keepdims=True), 0.0)
                    dcs_rows = dcs_rows + jnp.where(r == hh, jnp.sum(w, axis=0, keepdims=True), 0.0)
                    dcb = dcb + dm * lm
                    dxdt = dxdt + jnp.where(msk, _dot(mm, dyp, TN), 0.0)
                dxx_parts.append(dxdt * xp)
                dx_parts.append(dsk_all[:, sl] * dyp + dxdt * dt_l)
            db_parts.append(db_acc + _dot(dcb, cg, TN))
            dc_parts.append(dc_acc + _dot(dcb, bg, NN))
        g_ref[...] = jnp.stack(g_new)
        dxbc_ref[...] = jnp.concatenate(dx_parts + db_parts + dc_parts, axis=1)

        selt = (lax.broadcasted_iota(jnp.int32, (SSM_W, 128), 0) // HD
                == lax.broadcasted_iota(jnp.int32, (SSM_W, 128), 1)).astype(F32)

        def by_head(parts):
            return _dot_exact(jnp.concatenate(parts, axis=1), selt, "b")

        ddt_x = by_head(dxx_parts)
        dd_row = jnp.sum(by_head(dyx_parts), axis=0, keepdims=True)
        t_all = by_head(qx_parts) * jnp.exp(cs[127:128, :] - cs)
        gh = jnp.sum(_dot_exact(sel, jnp.concatenate(gh_parts, axis=0)), axis=1, keepdims=True)
        gh_row = jnp.broadcast_to(gh, (128, 128)).T[0:1, :]
        at_end = jnp.sum(t_all, axis=0, keepdims=True) + gh_row * jnp.exp(cs[127:128, :])
        dcs = by_head(ryo_parts) - t_all + dcs_cols + jnp.where(r == 127, at_end, 0.0) - dcs_rows.T
        dad = _dot_exact((c >= r).astype(F32), dcs)
        ddt = dad * a + ddt_x
        ddtr = jnp.where(c < 16, ddt * _sigmoid(z), 0.0)
        ddt_ref[...] = ddtr.astype(BF16)
        r8 = lax.broadcasted_iota(jnp.int32, (8, 128), 0)
        dsc_ref[...] += (jnp.where(r8 == 0, jnp.sum(ddtr, axis=0, keepdims=True), 0.0)
                         + jnp.where(r8 == 1, jnp.sum(dad * dt, axis=0, keepdims=True) * a, 0.0)
                         + jnp.where(r8 == 2, dd_row, 0.0))

    rev = NCH - 1
    return _call(
        body, "ssd_bwd", (NCH,),
        [pl.BlockSpec((128, SSM_W), lambda i: (rev - i, 0)),
         pl.BlockSpec((128, SSM_W), lambda i: (rev - i, 0)),
         pl.BlockSpec((128, SSM_W), lambda i: (rev - i, 0)),
         pl.BlockSpec((128, 256), lambda i: (rev - i, 4)), pl.BlockSpec((128, 256), lambda i: (rev - i, 5)),
         pl.BlockSpec((128, 128), lambda i: (rev - i, COL_DT // 128)),
         pl.BlockSpec((128, SSM_W), lambda i: (rev - i, 3)),
         pl.BlockSpec((None, NPAIR, 128, 128), lambda i: (rev - i, 0, 0, 0)),
         pl.BlockSpec((1, 128), lambda i: (0, 0)), pl.BlockSpec((1, 128), lambda i: (0, 0)),
         pl.BlockSpec((1, SSM_W), lambda i: (0, 0)), pl.BlockSpec((1, SSM_W), lambda i: (0, 0))],
        [pl.BlockSpec((128, CONV_C), lambda i: (rev - i, 0)),
         pl.BlockSpec((128, SSM_W), lambda i: (rev - i, 3)),
         pl.BlockSpec((128, 128), lambda i: (rev - i, 0)),
         pl.BlockSpec((1, SSM_W), lambda i: (0, 0)), pl.BlockSpec((8, 128), lambda i: (0, 0))],
        [jax.ShapeDtypeStruct((S, CONV_C), F32), jax.ShapeDtypeStruct((S, WIN_PAD), BF16),
         jax.ShapeDtypeStruct((S, 128), BF16), jax.ShapeDtypeStruct((1, SSM_W), F32),
         jax.ShapeDtypeStruct((8, 128), F32)],
        (dmixed, y, xbc, xbc, xbc, proj, proj, hprev, dtb, alog, dskip_l, ssmw),
        [pltpu.VMEM((NPAIR, 128, 128), F32)], ("arbitrary",), rider)


def _cast_stack(name, slot, arrs, tr, tc):
    n = len(arrs)
    rows, cols = arrs[0].shape

    def body(s_ref, *refs):
        for i in range(n):
            refs[n][i] = refs[i][...].astype(BF16)

    return pl.pallas_call(
        body, name=name,
        grid_spec=pltpu.PrefetchScalarGridSpec(
            num_scalar_prefetch=1, grid=(rows // tr, cols // tc),
            in_specs=[pl.BlockSpec((tr, tc), lambda i, j, sr: (i, j))] * n,
            out_specs=pl.BlockSpec((None, n, tr, tc), lambda i, j, sr: (sr[0], 0, i, j))),
        out_shape=jax.ShapeDtypeStruct((NSH, n, rows, cols), BF16),
        compiler_params=_cparams("parallel", "parallel"),
    )(slot, *arrs)


def _pair_sum(name, c_idx, ps, th):
    n = len(ps)
    _, rows, _ = ps[0].shape

    def body(c_ref, *refs):
        mine, whole, out, theirs = refs[:n], refs[n:2 * n], refs[2 * n:3 * n], refs[3 * n:4 * n]
        send, recv = refs[4 * n], refs[4 * n + 1]
        s, i = pl.program_id(0), pl.program_id(1)
        x, y, c, _ = _place()

        def copies(slot):
            return [_rcopy(whole[k].at[slot, :, pl.ds((1 - c) * HALF, HALF)], theirs[k].at[slot],
                           send.at[slot * n + k], recv.at[slot * n + k], (x, y, 1 - c)) for k in range(n)]

        @pl.when((s == 0) & (i == 0))
        def _():
            for slot in range(NSH):
                for cp in copies(slot):
                    cp.start()

        @pl.when(i == 0)
        def _():
            for slot in range(NSH):
                @pl.when(s == slot)
                def _():
                    for cp in copies(slot):
                        cp.wait()

        rows_i = slice(None) if th == rows else pl.ds(pl.multiple_of(i * th, th), th)
        for k in range(n):
            out[k][...] = (mine[k][...].astype(F32) + theirs[k][s, rows_i, :].astype(F32)).astype(BF16)

    spec = pl.BlockSpec((None, th, HALF), lambda s, i, cr: (s, i, 0))
    return pl.pallas_call(
        body, name=name,
        grid_spec=pltpu.PrefetchScalarGridSpec(
            num_scalar_prefetch=1, grid=(NSH, rows // th),
            in_specs=[pl.BlockSpec((None, th, HALF), lambda s, i, cr: (s, i, cr[0]))] * n + _any_specs(n),
            out_specs=[spec] * n,
            scratch_shapes=[pltpu.VMEM((NSH, rows, HALF), BF16)] * n
            + [pltpu.SemaphoreType.DMA((NSH * n,)), pltpu.SemaphoreType.DMA((NSH * n,))]),
        out_shape=[jax.ShapeDtypeStruct((NSH, rows, HALF), BF16)] * n,
        compiler_params=_cparams("arbitrary", "arbitrary"),
    )(c_idx, *ps, *ps)


def _chip_sum(name, place, cs, ts, th):
    n = len(ts)
    _, rows, _ = ts[0].shape

    def body(p_ref, *refs):
        for i in range(n):
            t = refs[n + i][...].astype(F32)
            refs[2 * n + i][...] = ((refs[i][...].astype(F32) + t[0]) + t[1]) + t[2]

    return pl.pallas_call(
        body, name=name,
        grid_spec=pltpu.PrefetchScalarGridSpec(
            num_scalar_prefetch=1, grid=(rows // th,),
            in_specs=[pl.BlockSpec((None, th, HALF), lambda i, pr: (pr[0], i, 0))] * n
            + [pl.BlockSpec((3, th, HALF), lambda i, pr: (0, i, 0))] * n,
            out_specs=[pl.BlockSpec((th, HALF), lambda i, pr: (i, pr[1]))] * n),
        out_shape=[jax.ShapeDtypeStruct((rows, D), F32)] * n, compiler_params=_cparams("parallel"),
    )(place, *cs, *ts)


def _adamw(name, ws, gs, ms, vs, tr, tc):
    n = len(ws)
    shape = ws[0].shape
    rows, cols, mid = shape[0], shape[-1], shape[1:-1]
    c1 = 1.0 / (1.0 - ADAM_B1 ** ADAM_STEP)
    c2 = 1.0 / (1.0 - ADAM_B2 ** ADAM_STEP)

    def body(*refs):
        for i in range(n):
            w, g, m, v = (refs[k * n + i][...] for k in range(4))
            m2 = ADAM_B1 * m + (1.0 - ADAM_B1) * g
            v2 = ADAM_B2 * v + (1.0 - ADAM_B2) * (g * g)
            refs[4 * n + 4 * i][...] = -ADAM_LR * ((m2 * c1) / (jnp.sqrt(v2 * c2) + ADAM_EPS) + ADAM_WD * w)
            refs[4 * n + 4 * i + 1][...] = m2
            refs[4 * n + 4 * i + 2][...] = v2
            refs[4 * n + 4 * i + 3][...] = g

    spec = pl.BlockSpec((tr,) + mid + (tc,), lambda i, j: (i,) + (0,) * len(mid) + (j,))
    outs = pl.pallas_call(
        body, name=name, grid=(rows // tr, cols // tc), in_specs=[spec] * (4 * n), out_specs=[spec] * (4 * n),
        out_shape=[jax.ShapeDtypeStruct(shape, F32)] * (4 * n),
        compiler_params=_cparams("parallel", "parallel"),
    )(*ws, *gs, *ms, *vs)
    return [tuple(outs[4 * i:4 * i + 4]) for i in range(n)]


def _place():
    x, y, c = lax.axis_index("x"), lax.axis_index("y"), lax.axis_index("c")
    chips = [(1 - x, y), (x, 1 - y), (1 - x, 1 - y)]
    return x, y, c, chips


def _any_specs(n):
    return [pl.BlockSpec(memory_space=pl.ANY)] * n


def _rcopy(src, dst, send_sem, recv_sem, dev):
    return pltpu.make_async_remote_copy(src_ref=src, dst_ref=dst, send_sem=send_sem, recv_sem=recv_sem,
                                        device_id=dev, device_id_type=MESH)


QUARTER = HALF // 2

TO_X, TO_Y, RELAY_X, RELAY_Y, FWD_X, FWD_Y, FWD_D0, FWD_D1 = range(8)
TO_D = RELAY_X


def _gather_rider(bufs, views, relay):
    n = len(bufs)

    def plan(rout, sems):
        send, recv = sems
        x, y, c, _ = _place()
        me, sx, sy, sd = 2 * x + y, 2 * (1 - x) + y, 2 * x + (1 - y), 2 * (1 - x) + (1 - y)
        nx, ny, nd, sib = (1 - x, y, c), (x, 1 - y, c), (1 - x, 1 - y, c), (x, y, 1 - c)
        mine, other = c * HALF, (1 - c) * HALF
        out = {TO_X: (me, mine, HALF, nx), TO_Y: (me, mine, HALF, ny),
               FWD_X: (sx, mine, HALF, sib), FWD_Y: (sy, mine, HALF, sib)}
        inn = {TO_X: (sx, mine, HALF), TO_Y: (sy, mine, HALF),
               FWD_X: (sx, other, HALF), FWD_Y: (sy, other, HALF)}
        if relay:
            out.update({RELAY_X: (sy, mine, QUARTER, nx), RELAY_Y: (sx, mine + QUARTER, QUARTER, ny),
                        FWD_D0: (sd, mine, QUARTER, sib), FWD_D1: (sd, mine + QUARTER, QUARTER, sib)})
            inn.update({RELAY_X: (sd, mine, QUARTER), RELAY_Y: (sd, mine + QUARTER, QUARTER),
                        FWD_D0: (sd, other, QUARTER), FWD_D1: (sd, other + QUARTER, QUARTER)})
        else:
            out.update({TO_D: (me, mine, HALF, nd), FWD_D0: (sd, mine, HALF, sib)})
            inn.update({TO_D: (sd, mine, HALF), FWD_D0: (sd, other, HALF)})

        def copy(kind, b):
            slot, col, ncols, dev = out[kind]
            win = views[b](rout[b], slot, col, ncols)
            return _rcopy(win, win, send.at[kind * n + b], recv.at[kind * n + b], dev)

        def land(kind, b):
            slot, col, ncols = inn[kind]
            win = views[b](rout[b], slot, col, ncols)
            return _rcopy(win, win, send.at[kind * n + b], recv.at[kind * n + b], (x, y, c))

        return copy, land

    if relay:
        first = (TO_X, TO_Y)
        chain = ((TO_X, (FWD_X, RELAY_Y)), (TO_Y, (FWD_Y, RELAY_X)), (RELAY_X, (FWD_D0,)), (RELAY_Y, (FWD_D1,)))
    else:
        first = (TO_X, TO_Y, TO_D)
        chain = ((TO_X, (FWD_X,)), (TO_Y, (FWD_Y,)), (TO_D, (FWD_D0,)))
    forwards = [k for _, then in chain for k in then if k in (FWD_X, FWD_Y, FWD_D0, FWD_D1)]
    sent = list(first) + [k for _, then in chain for k in then]

    def start(rin, rout, sems):
        copy, _ = plan(rout, sems)
        for kind in first:
            for b in range(n):
                copy(kind, b).start()

    def finish(rin, rout, sems):
        copy, land = plan(rout, sems)
        for landed, then in chain:
            for b in range(n):
                land(landed, b).wait_recv()
                for kind in then:
                    copy(kind, b).start()
        for kind in forwards:
            for b in range(n):
                land(kind, b).wait_recv()
        for kind in sent:
            for b in range(n):
                copy(kind, b).wait_send()

    return _Rider(list(bufs), [jax.ShapeDtypeStruct(a.shape, a.dtype) for a in bufs], {b: b for b in range(n)},
                  [pltpu.SemaphoreType.DMA((8 * n,))] * 2, start, finish)


def _small_gather_rider(cw):
    def descs(rin, rout, sems, x, y, c, chips):
        return [_rcopy(rin[0], rout[0].at[2 * x + y], sems[1].at[j], sems[2].at[j], (chip[0], chip[1], c))
                for j, chip in enumerate(chips)]

    def start(rin, rout, sems):
        x, y, c, chips = _place()
        pltpu.make_async_copy(rin[0], rout[0].at[2 * x + y], sems[0].at[0]).start()
        for cp in descs(rin, rout, sems, x, y, c, chips):
            cp.start()

    def finish(rin, rout, sems):
        x, y, c, chips = _place()
        for j, chip in enumerate(chips):
            _rcopy(rin[0], rout[0].at[2 * chip[0] + chip[1]], sems[1].at[j], sems[2].at[j], (x, y, c)).wait_recv()
        for cp in descs(rin, rout, sems, x, y, c, chips):
            cp.wait_send()
        pltpu.make_async_copy(rin[0], rout[0].at[2 * x + y], sems[0].at[0]).wait()

    return _Rider([cw], [jax.ShapeDtypeStruct((NSH,) + cw.shape, cw.dtype)], {},
                  [pltpu.SemaphoreType.DMA((1,)), pltpu.SemaphoreType.DMA((3,)), pltpu.SemaphoreType.DMA((3,))],
                  start, finish)


def _to_chips_rider(cs):
    n = len(cs)

    def descs(rin, rout, sems):
        x, y, c, chips = _place()
        return [_rcopy(rin[i].at[2 * chip[0] + chip[1]], rout[i].at[j], sems[0].at[j * n + i], sems[1].at[j * n + i],
                       (chip[0], chip[1], c)) for j, chip in enumerate(chips) for i in range(n)]

    def start(rin, rout, sems):
        for cp in descs(rin, rout, sems):
            cp.start()

    def finish(rin, rout, sems):
        for cp in descs(rin, rout, sems):
            cp.wait()

    return _Rider(list(cs), [jax.ShapeDtypeStruct((3,) + a.shape[1:], a.dtype) for a in cs], {},
                  [pltpu.SemaphoreType.DMA((3 * n,))] * 2, start, finish)


def _run_riders(name, riders):
    n_in = [len(r.operands) for r in riders]
    n_out = [len(r.out_shapes) for r in riders]
    n_sem = [len(r.sems) for r in riders]

    def body(*refs):
        parts, at = [], 0
        for counts in (n_in, n_out, n_sem):
            group = []
            for k in counts:
                group.append(refs[at:at + k])
                at += k
            parts.append(group)
        for i, r in enumerate(riders):
            r.start(parts[0][i], parts[1][i], parts[2][i])
        for i, r in enumerate(riders):
            r.finish(parts[0][i], parts[1][i], parts[2][i])

    aliases = {}
    for i, r in enumerate(riders):
        for k, v in r.aliases.items():
            aliases[sum(n_in[:i]) + k] = sum(n_out[:i]) + v
    res = pl.pallas_call(
        body, name=name, in_specs=_any_specs(sum(n_in)), out_specs=_any_specs(sum(n_out)),
        out_shape=[s for r in riders for s in r.out_shapes], input_output_aliases=aliases,
        scratch_shapes=[s for r in riders for s in r.sems],
    )(*[a for r in riders for a in r.operands])
    out, at = [], 0
    for k in n_out:
        out.append(list(res[at:at + k]))
        at += k
    return out


def _swap_halves(gs):
    n = len(gs)

    def body(*refs):
        dst, send, recv = refs[n:2 * n], refs[2 * n], refs[2 * n + 1]
        x, y, c, _ = _place()
        cps = []
        for i in range(n):
            mine = dst[i].at[:, pl.ds(c * HALF, HALF)]
            cps.append(pltpu.make_async_remote_copy(
                src_ref=mine, dst_ref=mine, send_sem=send.at[i], recv_sem=recv.at[i],
                device_id=(x, y, 1 - c), device_id_type=MESH))
        for cp in cps:
            cp.start()
        for i in range(n):
            other = dst[i].at[:, pl.ds((1 - c) * HALF, HALF)]
            pltpu.make_async_remote_copy(
                src_ref=other, dst_ref=other, send_sem=send.at[i], recv_sem=recv.at[i],
                device_id=(x, y, c), device_id_type=MESH).wait_recv()
        for cp in cps:
            cp.wait_send()

    return pl.pallas_call(
        body, name="grads_swap_halves", in_specs=_any_specs(n), out_specs=_any_specs(n),
        out_shape=[jax.ShapeDtypeStruct(g.shape, g.dtype) for g in gs],
        input_output_aliases={i: i for i in range(n)},
        scratch_shapes=[pltpu.SemaphoreType.DMA((n,)), pltpu.SemaphoreType.DMA((n,))],
    )(*gs)


SMALL_ROWS = 16


def _allreduce_small(vec):
    def body(v_ref, o_ref, buf, send, recv):
        x, y, c, _ = _place()
        me = 4 * x + 2 * y + c
        buf[me] = v_ref[...]
        cps = []
        for k in range(1, 8):
            peer = (x ^ (k >> 2), y ^ ((k >> 1) & 1), c ^ (k & 1))
            cps.append(pltpu.make_async_remote_copy(
                src_ref=v_ref, dst_ref=buf.at[me], send_sem=send.at[k - 1], recv_sem=recv.at[k - 1],
                device_id=peer, device_id_type=MESH))
        for cp in cps:
            cp.start()
        for k in range(1, 8):
            pltpu.make_async_remote_copy(
                src_ref=v_ref, dst_ref=buf.at[me ^ k], send_sem=send.at[k - 1], recv_sem=recv.at[k - 1],
                device_id=(x, y, c), device_id_type=MESH).wait_recv()
        for cp in cps:
            cp.wait_send()
        t = buf[0]
        for d in range(1, 8):
            t = t + buf[d]
        o_ref[...] = t

    return pl.pallas_call(
        body, name="allreduce_small",
        in_specs=[pl.BlockSpec(memory_space=pltpu.VMEM)], out_specs=pl.BlockSpec(memory_space=pltpu.VMEM),
        out_shape=jax.ShapeDtypeStruct((SMALL_ROWS, D), F32),
        scratch_shapes=[pltpu.VMEM((8, SMALL_ROWS, D), F32), pltpu.SemaphoreType.DMA((7,)),
                        pltpu.SemaphoreType.DMA((7,))],
    )(vec)


def _col_window(ref, slot, col, ncols):
    return ref.at[slot, :, pl.ds(col, ncols)]


def _stack_window(ref, slot, col, ncols):
    return ref.at[slot, :, :, pl.ds(col, ncols)]


def _row_tile(rows):
    for t in range(512, 15, -16):
        if rows % t == 0:
            return t
    return rows


def _same_shape_runs(arrs):
    runs, a = [], 0
    for b in range(1, len(arrs) + 1):
        if b == len(arrs) or arrs[b].shape != arrs[a].shape:
            runs.append((a, b))
            a = b
    return runs


class _Comm:
    def __init__(self):
        x, y, c = lax.axis_index("x"), lax.axis_index("y"), lax.axis_index("c")
        self.c_idx = jnp.reshape(c, (1,)).astype(jnp.int32)
        self.place = jnp.stack([2 * x + y, c]).astype(jnp.int32)
        self.groups = {}

    @staticmethod
    def gather(*bufs, relay):
        return _gather_rider(list(bufs), [_col_window if b.ndim == 3 else _stack_window for b in bufs], relay)

    def reduce_rider(self, tag, names, ps):
        csums = []
        for a, b in _same_shape_runs(ps):
            csums += _pair_sum("pair_sum_%s%d" % (tag, a), self.c_idx, ps[a:b], _row_tile(ps[a].shape[1]))
        self.groups[tag] = [names, csums, None]
        return _to_chips_rider(csums)

    def landed(self, tag, ts):
        self.groups[tag][2] = ts

    def finish(self):
        names, csums, ts = [], [], []
        for group_names, group_csums, group_ts in self.groups.values():
            names += group_names
            csums += group_csums
            ts += group_ts
        order = sorted(range(len(names)), key=lambda i: csums[i].shape[1])
        names, csums, ts = ([v[i] for i in order] for v in (names, csums, ts))
        halves = []
        for a, b in _same_shape_runs(csums):
            halves += _chip_sum("chip_sum_%d" % a, self.place, csums[a:b], ts[a:b], _row_tile(csums[a].shape[1]))
        return dict(zip(names, _swap_halves(halves)))


ROPE_THETA = 10000.0
SMALL_1K = ("ffn1_pre_norm", "ffn1_post_norm", "mix_pre_norm", "ssm_norm", "mix_post_norm",
            "ffn2_pre_norm", "ffn2_post_norm")
SMALL_16 = ("dt_bias", "a_log", "d_skip")
OFF_CONVB = 7 * D
OFF_16 = OFF_CONVB + CONV_C
OFF_CONVW = OFF_16 + 48
OFF_LOSS = OFF_CONVW + CONV_K * CONV_C
SMALL_LEN = SMALL_ROWS * D


def _sds(shape, dtype):
    return jax.ShapeDtypeStruct(shape, dtype)


def _ridden(res, rider):
    return res if rider is not None else (res, None)


def _ffn_down(name, act, w, tail_of, rider=None):
    tail, o_specs, o_shapes = tail_of(TS)
    return _mm(name, [act, w.dn], NN, (S // TS,),
               [pl.BlockSpec((NSH, TS, FS), lambda i: (0, i, 0)),
                pl.BlockSpec((NSH, None, FS, D), lambda i: (0, w.d0, 0, 0))], o_specs, o_shapes, rider, tail)


def _ffn_dw(name, a, b, rider=None):
    return _mm(name, [a, b], TN, (NSH,),
               [pl.BlockSpec((None, S, FS), lambda s: (s, 0, 0)), pl.BlockSpec((S, D), lambda s: (0, 0))],
               pl.BlockSpec((None, FS, D), lambda s: (s, 0, 0)), _sds((NSH, FS, D), BF16), rider)


def _ffn_dn(name, dgate, dup, w, tail_of, rider=None):
    rows = TS // 2
    tail, o_specs, o_shapes = tail_of(rows)
    a2 = pl.BlockSpec((NSH, rows, FS), lambda i: (0, i, 0))
    return _mm(name, [dgate, w.gu, dup, w.gu], NN, (S // rows,),
               [a2, pl.BlockSpec((NSH, None, FS, D), lambda i: (0, w.g0, 0, 0)),
                a2, pl.BlockSpec((NSH, None, FS, D), lambda i: (0, w.g0 + 1, 0, 0))], o_specs, o_shapes, rider, tail)


def _out_proj_dx(dh, wout):
    def body(dh_ref, w_ref, dyn_ref, do_ref):
        dm = _dot(dh_ref[...], w_ref[...], NT)
        dyn_ref[...] = dm[:, D:]
        for b in range(TS // 128):
            for j, blk in enumerate(_rows_to_blocks(dm[128 * b:128 * (b + 1), :D])):
                do_ref[j, b] = blk.astype(BF16)

    return pl.pallas_call(
        body, name="out_proj_dx", grid=(S // TS,),
        in_specs=[pl.BlockSpec((TS, D), lambda i: (i, 0)), pl.BlockSpec((2 * D, D), lambda i: (0, 0))],
        out_specs=[pl.BlockSpec((TS, D), lambda i: (i, 0)),
                   pl.BlockSpec((NKV, TS // 128, HD, QROWS), lambda i: (0, i, 0, 0))],
        out_shape=[_sds((S, D), F32), _sds((NKV, NCH, HD, QROWS), BF16)], compiler_params=_cparams("parallel"),
    )(dh, wout)


def _heads(t, n):
    return t.reshape(S, n, HD).transpose(1, 0, 2)


def _pad128(v):
    return jnp.pad(v, ((0, 0), (0, 128 - v.shape[1])))


def _local_step(x, positions, tgt, sp, gu1, d1, f2, wint, wout, convw, comm=None):
    inv_freq = ROPE_THETA ** (-jnp.arange(0, HD, 2, dtype=F32) / HD)
    ang = positions.astype(F32)[:, None] * inv_freq
    ang = jnp.concatenate([ang, ang, ang, ang], axis=-1)
    cos, sin = jnp.cos(ang), jnp.sin(ang)
    dtb, alog = _pad128(sp["dt_bias"]), _pad128(sp["a_log"])
    dskip_l = jnp.repeat(sp["d_skip"], HD, axis=1)
    convb = sp["conv_b"]

    n1 = _prenorm("prenorm1", x, sp["ffn1_pre_norm"])
    rider = comm.gather(d1, relay=False) if comm else None
    (fg1, fu1, act1), got = _ridden(_ffn_up("ffn1_up", n1, _FfnW(gu1, 0, d1, 0), rider), rider)
    if comm:
        d1, = got
    w1 = _FfnW(gu1, 0, d1, 0)
    rider = comm.gather(wint, relay=True) if comm else None
    (h1, x1, n2), got = _ridden(_ffn_down(
        "ffn1_down", act1, w1,
        lambda rows: _tail_postres(rows, x, sp["ffn1_post_norm"], 0.5, sp["mix_pre_norm"]), rider), rider)
    if comm:
        wint, = got
    wint_pad = jnp.pad(wint.reshape(WIN_COLS, D), ((0, WIN_PAD - WIN_COLS), (0, 0)))

    pw = WIN_PAD // 3
    proj = _mm("in_proj", [n2, wint_pad], NT, (S // TS, 3),
               [pl.BlockSpec((TS, D), lambda i, j: (i, 0)), pl.BlockSpec((pw, D), lambda i, j: (j, 0))],
               pl.BlockSpec((TS, pw), lambda i, j: (i, j)), _sds((S, WIN_PAD), F32))
    qt = _rope_q(proj, cos, sin)
    k_rot, v_bf, kt, vt = _rope_kv(proj, cos, sin)
    kh, vh = _heads(k_rot, NKV), _heads(v_bf, NKV)
    bias = _bias_table()
    rider = comm.gather(f2, wout, relay=False) if comm else None
    (ot, lse, attn), got = _ridden(_attn_fwd(qt, kh, vt, bias, rider), rider)
    if comm:
        f2, wout = got
    w2 = _FfnW(f2, 0, f2, 2)
    wout = wout.reshape(2 * D, D)
    xbc = _conv_fwd(proj, convw, convb)
    y, yn, hprev = _ssd_fwd(xbc, proj, dtb, alog, dskip_l, sp["ssm_norm"])
    mixed = jnp.concatenate([attn, yn], axis=1)
    tail, o_specs, o_shapes = _tail_postres(TS, x1, sp["mix_post_norm"], 1.0, sp["ffn2_pre_norm"])
    h2, x2, n3 = _mm("out_proj", [mixed, wout], NN, (S // TS,),
                     [pl.BlockSpec((TS, 2 * D), lambda i: (i, 0)), pl.BlockSpec((2 * D, D), lambda i: (0, 0))],
                     o_specs, o_shapes, None, tail)

    fg2, fu2, act2 = _ffn_up("ffn2_up", n3, w2)
    dy, dh3, dp3, loss = _ffn_down(
        "ffn2_down", act2, w2, lambda rows: _tail_final(rows, x2, sp["ffn2_post_norm"], tgt, 0.5))

    dgate2, dup2 = _ffn_dact("ffn2_dact", dh3, w2, fg2, fu2)
    dws2 = [_ffn_dw("ffn2_dwg", dgate2, n3), _ffn_dw("ffn2_dwu", dup2, n3), _ffn_dw("ffn2_dwd", act2, dh3)]
    dx2, dh2, dg3, dp2 = _ffn_dn(
        "ffn2_dn", dgate2, dup2, w2,
        lambda rows: _tail_mid_bwd(rows, dy, x2, sp["ffn2_pre_norm"], h2, sp["mix_post_norm"], 1.0))

    dyn, dot_ = _out_proj_dx(dh2, wout)
    dwout = _mm("out_proj_dw", [mixed, dh2], TN, (2,),
                [pl.BlockSpec((S, D), lambda m: (0, m)), pl.BlockSpec((S, D), lambda m: (0, 0))],
                pl.BlockSpec((D, D), lambda m: (m, 0)), _sds((2 * D, D), BF16))
    dwout = dwout.reshape(NSH, 2 * D // NSH, D)

    def riding(tag, names, ps, call):
        rider = comm.reduce_rider(tag, names, ps) if comm else None
        res, got = _ridden(call(rider), rider)
        if comm:
            comm.landed(tag, got)
        return res

    dxbc, dproj, ddt, dssm, dsc = _ssd_bwd(dyn, y, xbc, proj, hprev, dtb, alog, dskip_l, sp["ssm_norm"])
    dproj, dcw8, dcb = _conv_bwd(dxbc, proj, convw, convb, dproj)
    delta = _attn_delta(ot, dot_)
    dqt, dkh, dvh = riding("a", BIG[3:6] + ("w_out",), dws2 + [dwout], lambda rider: _attn_bwd(
        qt, kh, kt, vh, dot_, lse, delta, bias, rider))
    dproj = _rope_dq(dqt, cos, sin, dproj)
    dproj = _rope_dkv(dkh, dvh, cos, sin, dproj)
    dproj = lax.dynamic_update_slice(dproj, ddt, (0, COL_DT))
    dwint = _mm("in_proj_dw", [dproj, n2], TN, (3,),
                [pl.BlockSpec((S, pw), lambda j: (0, j)), pl.BlockSpec((S, D), lambda j: (0, 0))],
                pl.BlockSpec((pw, D), lambda j: (j, 0)), _sds((WIN_PAD, D), BF16))
    dwint = dwint[:WIN_COLS].reshape(NSH, WIN_SH, D)

    tail, o_specs, o_shapes = _tail_mid_bwd(TS, dx2, x1, sp["mix_pre_norm"], h1, sp["ffn1_post_norm"], 0.5)
    dx1, dh1, dg2, dp1 = riding("b", ("w_in",), [dwint], lambda rider: _mm(
        "in_proj_dx", [dproj, wint_pad], NN, (S // TS,),
        [pl.BlockSpec((TS, WIN_PAD), lambda i: (i, 0)), pl.BlockSpec((WIN_PAD, D), lambda i: (0, 0))],
        o_specs, o_shapes, rider, tail))

    dwd1 = _ffn_dw("ffn1_dwd", act1, dh1)
    dgate1, dup1 = riding("d", BIG[2:3], [dwd1], lambda rider: _ffn_dact("ffn1_dact", dh1, w1, fg1, fu1, rider))
    dwg1, dwu1 = _ffn_dw("ffn1_dwg", dgate1, n1), _ffn_dw("ffn1_dwu", dup1, n1)
    grad_x, dg1 = riding("g", BIG[0:2], [dwg1, dwu1], lambda rider: _ffn_dn(
        "ffn1_dn", dgate1, dup1, w1, lambda rows: _tail_first_bwd(rows, dx1, x, sp["ffn1_pre_norm"]), rider))
    dws1 = [dwg1, dwu1, dwd1]

    small = jnp.concatenate([
        dg1[0], dp1[0], dg2[0], dssm[0], dp2[0], dg3[0], dp3[0], dcb[0],
        dsc[0, :16], dsc[1, :16], dsc[2, :16], dcw8[:CONV_K].reshape(-1), loss[0, :1]])
    small = jnp.pad(small, (0, SMALL_LEN - small.shape[0])).reshape(SMALL_ROWS, D)
    if comm is None:
        return grad_x, dws1 + dws2 + [dwint, dwout], small
    return grad_x, comm.finish(), small


WEIGHTS = ("ffn1_pre_norm", "ffn1_w_gate", "ffn1_w_up", "ffn1_w_down", "ffn1_post_norm", "mix_pre_norm", "w_in",
           "conv_w", "conv_b", "dt_bias", "a_log", "d_skip", "ssm_norm", "w_out", "mix_post_norm", "ffn2_pre_norm",
           "ffn2_w_gate", "ffn2_w_up", "ffn2_w_down", "ffn2_post_norm")
BIG = ("ffn1_w_gate", "ffn1_w_up", "ffn1_w_down", "ffn2_w_gate", "ffn2_w_up", "ffn2_w_down", "w_in", "w_out")
TRANSPOSED = ("ffn1_w_gate", "ffn1_w_up", "ffn2_w_gate", "ffn2_w_up", "w_in")
SMALL_ORDER = SMALL_1K + ("conv_b",) + SMALL_16
CONVW_SH = CONV_C // NSH


def _shard2d(t, name):
    return t[0].T if name in TRANSPOSED else t[0]


def _unshard2d(t, name):
    return (t.T if name in TRANSPOSED else t)[None]


def _rows3d(t):
    return t.transpose(2, 0, 1)


def _pack_small(d, prefix, shard_of_convw):
    flat = jnp.concatenate([d[prefix + n][0] for n in SMALL_ORDER] + [shard_of_convw.reshape(-1)])
    return jnp.pad(flat, (0, SMALL_LEN - flat.shape[0])).reshape(SMALL_ROWS, D)


def _unpack_small(block, like):
    flat = block.reshape(-1)
    out, off = {}, 0
    for n in SMALL_ORDER:
        size = like[n].shape[1]
        out[n] = flat[off:off + size].reshape(1, size)
        off += size
    out["conv_w"] = flat[off:off + CONV_K * CONVW_SH].reshape(1, CONV_K, CONVW_SH)
    return out


def kernel(x, positions, ffn1_pre_norm, ffn1_w_gate, ffn1_w_up, ffn1_w_down, ffn1_post_norm, mix_pre_norm, w_in, conv_w, conv_b, dt_bias, a_log, d_skip, ssm_norm, w_out, mix_post_norm, ffn2_pre_norm, ffn2_w_gate, ffn2_w_up, ffn2_w_down, ffn2_post_norm, loss_target, m_ffn1_pre_norm, m_ffn1_w_gate, m_ffn1_w_up, m_ffn1_w_down, m_ffn1_post_norm, m_mix_pre_norm, m_w_in, m_conv_w, m_conv_b, m_dt_bias, m_a_log, m_d_skip, m_ssm_norm, m_w_out, m_mix_post_norm, m_ffn2_pre_norm, m_ffn2_w_gate, m_ffn2_w_up, m_ffn2_w_down, m_ffn2_post_norm, v_ffn1_pre_norm, v_ffn1_w_gate, v_ffn1_w_up, v_ffn1_w_down, v_ffn1_post_norm, v_mix_pre_norm, v_w_in, v_conv_w, v_conv_b, v_dt_bias, v_a_log, v_d_skip, v_ssm_norm, v_w_out, v_mix_post_norm, v_ffn2_pre_norm, v_ffn2_w_gate, v_ffn2_w_up, v_ffn2_w_down, v_ffn2_post_norm):
    given = dict(locals())
    xi, yi = lax.axis_index("x"), lax.axis_index("y")

    shard = jnp.reshape(2 * xi + yi, (1,)).astype(jnp.int32)
    big = {p + n: _shard2d(given[p + n], n) for n in BIG for p in ("", "m_", "v_")}
    gu1 = _cast_stack("cast_ffn1_gate_up", shard, [big[n] for n in BIG[0:2]], 176, D)
    d1 = _cast_stack("cast_ffn1_down", shard, [big[BIG[2]]], 176, D)
    f2 = _cast_stack("cast_ffn2", shard, [big[n] for n in BIG[3:6]], 176, D)
    winsh = _cast_stack("cast_w_in", shard, [big["w_in"]], WIN_SH, 256).reshape(NSH, WIN_SH, D)
    woutsh = _cast_stack("cast_w_out", shard, [big["w_out"]], 256, D).reshape(NSH, 2 * D // NSH, D)
    comm = _Comm()
    (gu1,), (cwf,) = _run_riders("gather_ffn1_gate_up", [comm.gather(gu1, relay=True), _small_gather_rider(conv_w[0])])
    convw = cwf.transpose(1, 0, 2).reshape(CONV_K, CONV_C)

    sp = {n: given[n] for n in SMALL_ORDER}
    grad_x, big_grads, small = _local_step(x[0], positions[0], loss_target[0], sp, gu1, d1, f2, winsh, woutsh,
                                           convw, comm)

    tot = _allreduce_small(small).reshape(-1)
    loss = tot[OFF_LOSS]
    small_grads, off = {}, 0
    for n in SMALL_ORDER:
        size = given[n].shape[1]
        small_grads[n] = tot[off:off + size].reshape(1, size)
        off += size
    dconvw = tot[OFF_CONVW:OFF_CONVW + CONV_K * CONV_C].reshape(CONV_K, NSH, CONVW_SH)
    dconvw = lax.dynamic_index_in_dim(dconvw, 2 * xi + yi, axis=1, keepdims=False)
    small_grads["conv_w"] = dconvw.reshape(1, CONV_K, CONVW_SH)

    upd = {}
    for names, tr in ((BIG[0:3], 176), (BIG[3:6], 176), (BIG[7:8], 256)):
        res = _adamw("adamw_" + names[0], [big[n] for n in names], [big_grads[n] for n in names],
                     [big["m_" + n] for n in names], [big["v_" + n] for n in names], tr, D)
        for n, r in zip(names, res):
            upd[n] = tuple(_unshard2d(t, n) for t in r)
    g_win = big_grads["w_in"].reshape(WIN_SH, 1, D)
    res, = _adamw("adamw_w_in", [_rows3d(w_in)], [g_win], [_rows3d(m_w_in)], [_rows3d(v_w_in)], WIN_SH // 4, D)
    upd["w_in"] = tuple(t.transpose(1, 2, 0) for t in res)
    (dl, m2, v2, _), = _adamw(
        "adamw_small", [_pack_small(given, "", conv_w[0])], [_pack_small(small_grads, "", dconvw)],
        [_pack_small(given, "m_", m_conv_w[0])], [_pack_small(given, "v_", v_conv_w[0])], SMALL_ROWS, D)
    dl, m2, v2 = (_unpack_small(t, given) for t in (dl, m2, v2))
    for n in SMALL_ORDER + ("conv_w",):
        upd[n] = (dl[n], m2[n], v2[n], small_grads[n])

    return (loss, grad_x[None], *[upd[n][3] for n in WEIGHTS], *[upd[n][0] for n in WEIGHTS],
            *[upd[n][1] for n in WEIGHTS], *[upd[n][2] for n in WEIGHTS])
```

```python
import functools
import typing

import jax
import jax.numpy as jnp
from jax import lax
from jax.experimental import pallas as pl
from jax.experimental.pallas import tpu as pltpu

F32 = jnp.float32
BF16 = jnp.bfloat16

S = 2048
D = 1024
FF = 2816
NSH = 4
FS = FF // NSH
HALF = D // 2
HD = 64
NKV = 4
NQ_PER_KV = 4
KVW = NKV * HD
QCOLS = NQ_PER_KV * HD
CONV_C = 1536
CONV_K = 4
SSM_W = 1024
NST = 128
NCH = S // 128
WIN_COLS = 4112
WIN_SH = WIN_COLS // NSH
WIN_PAD = 4224
COL_DT = 4096
EPS = 1e-6
NEG = -1e30

ADAM_LR = 0.001
ADAM_B1 = 0.9
ADAM_B2 = 0.999
ADAM_EPS = 1e-08
ADAM_WD = 0.01
ADAM_STEP = 10

VMEM_LIMIT = 56 * 1024 * 1024
TS = 512
TR = 256

NN = (((1,), (0,)), ((), ()))
NT = (((1,), (1,)), ((), ()))
TN = (((0,), (0,)), ((), ()))
MESH = pl.DeviceIdType.MESH


def _cparams(*sem):
    return pltpu.CompilerParams(dimension_semantics=sem, vmem_limit_bytes=VMEM_LIMIT)


def _dot(a, b, dims):
    return lax.dot_general(a.astype(BF16), b.astype(BF16), dims, preferred_element_type=F32)


def _bf16_pieces(v):
    hi = v.astype(BF16)
    rest = v - hi.astype(F32)
    mid = rest.astype(BF16)
    return hi, mid, (rest - mid.astype(F32)).astype(BF16)


def _dot_exact(a, b, ones="a"):
    if ones == "a":
        sel = a.astype(BF16)
        parts = [lax.dot_general(sel, p, NN, preferred_element_type=F32) for p in _bf16_pieces(b)]
    else:
        sel = b.astype(BF16)
        parts = [lax.dot_general(p, sel, NN, preferred_element_type=F32) for p in _bf16_pieces(a)]
    return (parts[2] + parts[1]) + parts[0]


def _sigmoid(v):
    return 1.0 / (1.0 + jnp.exp(-v))


class _Rider(typing.NamedTuple):
    operands: list
    out_shapes: list
    aliases: dict
    sems: list
    start: typing.Callable
    finish: typing.Callable


def _call(body, name, grid, in_specs, out_specs, out_shape, operands, scratch=(), sem=(), rider=None):
    multi = isinstance(out_shape, (list, tuple))
    if rider is None:
        return pl.pallas_call(
            body, name=name, grid=grid, in_specs=in_specs, out_specs=out_specs, out_shape=out_shape,
            scratch_shapes=list(scratch), compiler_params=_cparams(*sem))(*operands)
    outs = list(out_shape) if multi else [out_shape]
    ospecs = list(out_specs) if multi else [out_specs]
    n_in, n_out, n_scr = len(operands), len(outs), len(scratch)
    ri, ro = len(rider.operands), len(rider.out_shapes)

    def wrapped(*refs):
        o0 = n_in + ri
        s0 = o0 + n_out + ro
        rin, rout, rsem = refs[n_in:o0], refs[o0 + n_out:s0], refs[s0 + n_scr:]
        ids = [pl.program_id(a) for a in range(len(grid))]
        first = functools.reduce(jnp.logical_and, [i == 0 for i in ids])
        last = functools.reduce(jnp.logical_and, [i == g - 1 for i, g in zip(ids, grid)])

        @pl.when(first)
        def _():
            rider.start(rin, rout, rsem)

        body(*refs[:n_in], *refs[o0:o0 + n_out], *refs[s0:s0 + n_scr])

        @pl.when(last)
        def _():
            rider.finish(rin, rout, rsem)

    hbm = pl.BlockSpec(memory_space=pl.ANY)
    res = pl.pallas_call(
        wrapped, name=name, grid=grid, in_specs=list(in_specs) + [hbm] * ri, out_specs=ospecs + [hbm] * ro,
        out_shape=outs + list(rider.out_shapes), scratch_shapes=list(scratch) + list(rider.sems),
        input_output_aliases={n_in + k: n_out + v for k, v in rider.aliases.items()},
        compiler_params=_cparams(*(("arbitrary",) * len(grid))))(*operands, *rider.operands)
    main = list(res[:n_out])
    return (main if multi else main[0]), list(res[n_out:])


class _Tail(typing.NamedTuple):
    fn: typing.Callable
    operands: list
    in_specs: list


def _mm(name, operands, dims, grid, in_specs, o_spec, out_shape, rider=None, tail=None):
    npairs = len(operands) // 2
    extra = [] if tail is None else list(tail.operands)
    nin = 2 * npairs + len(extra)

    def body(*refs):
        t = None
        for i in range(npairs):
            a, b = refs[2 * i], refs[2 * i + 1]
            parts = [(a[s], b[s]) for s in range(a.shape[0])] if len(a.shape) == 3 else [(a[...], b[...])]
            for pa, pb in parts:
                d = _dot(pa, pb, dims)
                t = d if t is None else t + d
        if tail is None:
            refs[nin][...] = t.astype(refs[nin].dtype)
        else:
            tail.fn(t, refs[2 * npairs:nin], refs[nin:])

    sem = ("parallel" if tail is None else "arbitrary",) * len(grid)
    specs = list(in_specs) + ([] if tail is None else list(tail.in_specs))
    return _call(body, name, grid, specs, o_spec, out_shape, list(operands) + extra, (), sem, rider)


class _FfnW(typing.NamedTuple):
    gu: jax.Array
    g0: int
    dn: jax.Array
    d0: int


def _ffn_up(name, n, w, rider=None):
    def body(n_ref, wg_ref, wu_ref, fg_ref, fu_ref, a_ref):
        nb = n_ref[...]
        g = _dot(nb, wg_ref[...], NT)
        u = _dot(nb, wu_ref[...], NT)
        sg = _sigmoid(g)
        silu = g * sg
        fg_ref[...] = (u * (sg * (1.0 + g * (1.0 - sg)))).astype(BF16)
        fu_ref[...] = silu.astype(BF16)
        a_ref[...] = (silu * u).astype(BF16)

    out = jax.ShapeDtypeStruct((NSH, S, FS), BF16)
    ospec = pl.BlockSpec((None, TS, FS), lambda s, i: (s, i, 0))
    return _call(
        body, name, (NSH, S // TS),
        [pl.BlockSpec((TS, D), lambda s, i: (i, 0)),
         pl.BlockSpec((None, None, FS, D), lambda s, i: (s, w.g0, 0, 0)),
         pl.BlockSpec((None, None, FS, D), lambda s, i: (s, w.g0 + 1, 0, 0))],
        [ospec, ospec, ospec], [out, out, out], (n, w.gu, w.gu), sem=("parallel", "parallel"), rider=rider)


def _ffn_dact(name, dh, w, fgate, fup, rider=None):
    def body(dh_ref, wd_ref, fg_ref, fu_ref, dg_ref, du_ref):
        da = _dot(dh_ref[...], wd_ref[...], NT)
        dg_ref[...] = (da * fg_ref[...].astype(F32)).astype(BF16)
        du_ref[...] = (da * fu_ref[...].astype(F32)).astype(BF16)

    out = jax.ShapeDtypeStruct((NSH, S, FS), BF16)
    aspec = pl.BlockSpec((None, TS, FS), lambda s, i: (s, i, 0))
    return _call(
        body, name, (NSH, S // TS),
        [pl.BlockSpec((TS, D), lambda s, i: (i, 0)),
         pl.BlockSpec((None, None, FS, D), lambda s, i: (s, w.d0, 0, 0)), aspec, aspec],
        [aspec, aspec], [out, out], (dh, w.dn, fgate, fup), sem=("parallel", "parallel"), rider=rider)


def _rstd(v):
    return lax.rsqrt(jnp.mean(v * v, axis=-1, keepdims=True) + EPS)


def _row_spec():
    return pl.BlockSpec((TR, D), lambda i: (i, 0))


def _vec_spec():
    return pl.BlockSpec((1, D), lambda i: (0, 0))


def _acc_rows(ref, v):
    @pl.when(pl.program_id(0) == 0)
    def _():
        ref[...] = jnp.zeros_like(ref)
    ref[...] += jnp.sum(v, axis=0, keepdims=True)


def _prenorm(name, x, g):
    def body(x_ref, g_ref, n_ref):
        xv = x_ref[...]
        n_ref[...] = (xv * _rstd(xv) * g_ref[...]).astype(BF16)

    return pl.pallas_call(
        body, name=name, grid=(S // TR,), in_specs=[_row_spec(), _vec_spec()], out_specs=_row_spec(),
        out_shape=jax.ShapeDtypeStruct((S, D), BF16), compiler_params=_cparams("parallel"),
    )(x, g)


def _rows_spec(rows):
    return pl.BlockSpec((rows, D), lambda i: (i, 0))


def _rows_f32():
    return jax.ShapeDtypeStruct((S, D), F32)


def _rows_bf16():
    return jax.ShapeDtypeStruct((S, D), BF16)


def _vec_f32():
    return jax.ShapeDtypeStruct((1, D), F32)


def _tail_postres(rows, x, p, alpha, gnext):
    def fn(h, ins, outs):
        x_ref, p_ref, g_ref = ins
        h_ref, xo_ref, n_ref = outs
        h_ref[...] = h
        xo = x_ref[...] + alpha * (h * _rstd(h) * p_ref[...])
        xo_ref[...] = xo
        n_ref[...] = (xo * _rstd(xo) * g_ref[...]).astype(BF16)

    rs = _rows_spec(rows)
    return (_Tail(fn, [x, p, gnext], [rs, _vec_spec(), _vec_spec()]), [rs, rs, rs],
            [_rows_f32(), _rows_f32(), _rows_bf16()])


def _tail_final(rows, x, p, tgt, alpha):
    def fn(h, ins, outs):
        x_ref, p_ref, t_ref = ins
        dy_ref, dh_ref, dp_ref, loss_ref = outs
        r = _rstd(h)
        hn = h * r
        pv = p_ref[...]
        e = x_ref[...] + alpha * (hn * pv) - t_ref[...]
        dy = e * (1.0 / D)
        dy_ref[...] = dy
        du = alpha * dy * pv
        dh_ref[...] = (r * (du - hn * jnp.mean(du * hn, axis=-1, keepdims=True))).astype(BF16)
        _acc_rows(dp_ref, alpha * dy * hn)
        part = 0.5 * jnp.sum(jnp.mean(e * e, axis=-1, keepdims=True), axis=0, keepdims=True)
        _acc_rows(loss_ref, jnp.broadcast_to(part, (1, 128)))

    rs = _rows_spec(rows)
    return (_Tail(fn, [x, p, tgt], [rs, _vec_spec(), rs]),
            [rs, rs, _vec_spec(), pl.BlockSpec((1, 128), lambda i: (0, 0))],
            [_rows_f32(), _rows_bf16(), _vec_f32(), jax.ShapeDtypeStruct((1, 128), F32)])


def _norm_bwd(dn, xv, g_ref, dg_ref):
    r = _rstd(xv)
    xn = xv * r
    dng = dn * g_ref[...]
    _acc_rows(dg_ref, dn * xn)
    return r * (dng - xn * jnp.mean(dng * xn, axis=-1, keepdims=True))


def _tail_mid_bwd(rows, dres, x, g, h, p, alpha):
    def fn(dn, ins, outs):
        dr_ref, x_ref, g_ref, h_ref, p_ref = ins
        dx_ref, dh_ref, dg_ref, dp_ref = outs
        dx = dr_ref[...] + _norm_bwd(dn, x_ref[...], g_ref, dg_ref)
        dx_ref[...] = dx
        hv = h_ref[...]
        r = _rstd(hv)
        hn = hv * r
        du = alpha * dx * p_ref[...]
        dh_ref[...] = (r * (du - hn * jnp.mean(du * hn, axis=-1, keepdims=True))).astype(BF16)
        _acc_rows(dp_ref, alpha * dx * hn)

    rs = _rows_spec(rows)
    return (_Tail(fn, [dres, x, g, h, p], [rs, rs, _vec_spec(), rs, _vec_spec()]),
            [rs, rs, _vec_spec(), _vec_spec()], [_rows_f32(), _rows_bf16(), _vec_f32(), _vec_f32()])


def _tail_first_bwd(rows, dres, x, g):
    def fn(dn, ins, outs):
        dr_ref, x_ref, g_ref = ins
        dx_ref, dg_ref = outs
        dx_ref[...] = dr_ref[...] + _norm_bwd(dn, x_ref[...], g_ref, dg_ref)

    rs = _rows_spec(rows)
    return (_Tail(fn, [dres, x, g], [rs, rs, _vec_spec()]), [rs, _vec_spec()], [_rows_f32(), _vec_f32()])


def _rotate(t, c128, s128, sign, scale):
    width = t.shape[1]
    c = jnp.tile(c128, (1, width // 128))
    sn = jnp.tile(s128, (1, width // 128))
    lane = lax.broadcasted_iota(jnp.int32, t.shape, 1) & (HD - 1)
    rot = jnp.where(lane < HD // 2, -pltpu.roll(t, width - HD // 2, 1), pltpu.roll(t, HD // 2, 1))
    return (t * c + sign * (rot * sn)) * scale


def _rows_to_blocks(y):
    out = []
    for j in range(NKV):
        yt = y[:, QCOLS * j:QCOLS * (j + 1)].T
        out.append(jnp.concatenate([yt[HD * g:HD * (g + 1)] for g in range(NQ_PER_KV)], axis=1))
    return out


def _blocks_to_rows(blocks):
    cols = []
    for b in blocks:
        stacked = jnp.concatenate([b[:, 128 * g:128 * (g + 1)] for g in range(NQ_PER_KV)], axis=0)
        cols.append(stacked.T)
    return jnp.concatenate(cols, axis=1)


def _rope_q(proj, cos, sin):
    def body(t_ref, c_ref, s_ref, o_ref):
        y = _rotate(t_ref[...], c_ref[...], s_ref[...], 1.0, HD ** -0.5)
        for j, blk in enumerate(_rows_to_blocks(y)):
            o_ref[j] = blk.astype(BF16)

    return pl.pallas_call(
        body, name="rope_q", grid=(NCH,),
        in_specs=[pl.BlockSpec((128, D), lambda i: (i, 0)),
                  pl.BlockSpec((128, 128), lambda i: (i, 0)), pl.BlockSpec((128, 128), lambda i: (i, 0))],
        out_specs=pl.BlockSpec((NKV, None, HD, QROWS), lambda i: (0, i, 0, 0)),
        out_shape=jax.ShapeDtypeStruct((NKV, NCH, HD, QROWS), BF16), compiler_params=_cparams("parallel"),
    )(proj, cos, sin)


def _rope_dq(dqt, cos, sin, dproj):
    def body(t_ref, c_ref, s_ref, buf_ref, o_ref):
        t = _blocks_to_rows([t_ref[j] for j in range(NKV)])
        o_ref[...] = _rotate(t, c_ref[...], s_ref[...], -1.0, HD ** -0.5).astype(BF16)

    return pl.pallas_call(
        body, name="rope_dq", grid=(NCH,),
        in_specs=[pl.BlockSpec((NKV, None, HD, QROWS), lambda i: (0, i, 0, 0)),
                  pl.BlockSpec((128, 128), lambda i: (i, 0)), pl.BlockSpec((128, 128), lambda i: (i, 0)),
                  pl.BlockSpec(memory_space=pl.ANY)],
        out_specs=pl.BlockSpec((128, D), lambda i: (i, 0)),
        out_shape=jax.ShapeDtypeStruct(dproj.shape, BF16), input_output_aliases={3: 0},
        compiler_params=_cparams("parallel"),
    )(dqt, cos, sin, dproj)


def _rope_dkv(dkt, dvt, cos, sin, dproj):
    def body(k_ref, v_ref, c_ref, s_ref, buf_ref, o_ref):
        dk = jnp.concatenate([k_ref[j] for j in range(NKV)], axis=0).T
        dv = jnp.concatenate([v_ref[j] for j in range(NKV)], axis=0).T
        dk = _rotate(dk, c_ref[...], s_ref[...], -1.0, 1.0)
        o_ref[...] = jnp.concatenate([dk, dv], axis=1).astype(BF16)

    tspec = pl.BlockSpec((NKV, HD, 128), lambda i: (0, 0, i))
    return pl.pallas_call(
        body, name="rope_dkv", grid=(NCH,),
        in_specs=[tspec, tspec, pl.BlockSpec((128, 128), lambda i: (i, 0)), pl.BlockSpec((128, 128), lambda i: (i, 0)),
                  pl.BlockSpec(memory_space=pl.ANY)],
        out_specs=pl.BlockSpec((128, 2 * KVW), lambda i: (i, D // (2 * KVW))),
        out_shape=jax.ShapeDtypeStruct(dproj.shape, BF16), input_output_aliases={4: 0},
        compiler_params=_cparams("parallel"),
    )(dkt, dvt, cos, sin, dproj)


def _rope_kv(proj, cos, sin):
    def body(t_ref, c_ref, s_ref, k_ref, v_ref, kt_ref, vt_ref):
        t = t_ref[...]
        k = _rotate(t[:, :KVW], c_ref[...], s_ref[...], 1.0, 1.0).astype(BF16)
        v = t[:, KVW:].astype(BF16)
        k_ref[...] = k
        v_ref[...] = v
        kt, vt = k.astype(F32).T, v.astype(F32).T
        for j in range(NKV):
            kt_ref[j] = kt[HD * j:HD * (j + 1)].astype(BF16)
            vt_ref[j] = vt[HD * j:HD * (j + 1)].astype(BF16)

    rows = pl.BlockSpec((128, KVW), lambda i: (i, 0))
    tspec = pl.BlockSpec((NKV, HD, 128), lambda i: (0, 0, i))
    return pl.pallas_call(
        body, name="rope_kv", grid=(NCH,),
        in_specs=[pl.BlockSpec((128, 2 * KVW), lambda i: (i, D // (2 * KVW))),
                  pl.BlockSpec((128, 128), lambda i: (i, 0)), pl.BlockSpec((128, 128), lambda i: (i, 0))],
        out_specs=[rows, rows, tspec, tspec],
        out_shape=[jax.ShapeDtypeStruct((S, KVW), BF16)] * 2 + [jax.ShapeDtypeStruct((NKV, HD, S), BF16)] * 2,
        compiler_params=_cparams("parallel"),
    )(proj, cos, sin)


QROWS = NQ_PER_KV * 128


NBIAS = NCH + 1
KV_PER_STEP = 4


def _bias_table():
    db = lax.broadcasted_iota(jnp.int32, (NBIAS, 128, QROWS), 0) - 1
    ki = lax.broadcasted_iota(jnp.int32, (NBIAS, 128, QROWS), 1)
    qi = lax.broadcasted_iota(jnp.int32, (NBIAS, 128, QROWS), 2) & 127
    d = db * 128 + qi - ki
    cnt = ((d <= 128).astype(F32) + (((d & 3) == 0) & (d <= 512)).astype(F32) + ((d & 15) == 0).astype(F32))
    return jnp.where((d >= 0) & (cnt > 0.0), jnp.log(jnp.maximum(cnt, 1.0)), NEG)


def _qt_spec():
    return pl.BlockSpec((None, None, HD, QROWS), lambda j, i: (j, i, 0, 0))


def _stat_spec():
    return pl.BlockSpec((None, None, 1, QROWS), lambda j, i: (j, i, 0, 0))


def _attn_fwd(qt, kh, vt, bias, rider=None):
    def body(q_ref, k_ref, v_ref, b_ref, o_ref, lse_ref, rows_ref, m_ref, l_ref, acc_ref):
        qb = pl.program_id(1)
        m_ref[...] = jnp.full_like(m_ref, NEG)
        l_ref[...] = jnp.zeros_like(l_ref)
        acc_ref[...] = jnp.zeros_like(acc_ref)

        def keys(off, size, bias_):
            for h in range(KV_PER_STEP):
                m = m_ref[h]
                s = _dot(k_ref[h, pl.ds(off, size), :], q_ref[h], NN) + bias_
                m_new = jnp.maximum(m, jnp.max(s, axis=0, keepdims=True))
                p = jnp.exp(s - m_new)
                a = jnp.exp(m - m_new)
                m_ref[h] = m_new
                l_ref[h] = a * l_ref[h] + jnp.sum(p, axis=0, keepdims=True)
                acc_ref[h] = a * acc_ref[h] + _dot(v_ref[h, :, pl.ds(off, size)], p, NN)

        def blocks(first, count):
            bias_ = jnp.concatenate([b_ref[qb - first - j + 1] for j in range(count)], axis=0)
            keys(pl.multiple_of(first * 128, 128), 128 * count, bias_)

        nkb = qb + 1
        @pl.loop(0, nkb // 4)
        def _(i):
            blocks(4 * i, 4)

        @pl.when(nkb % 4 >= 2)
        def _():
            blocks(nkb // 4 * 4, 2)

        @pl.when(nkb % 2 == 1)
        def _():
            blocks(qb, 1)

        outs = []
        for h in range(KV_PER_STEP):
            outs.append(acc_ref[h] / l_ref[h])
            o_ref[h] = outs[h]
            lse_ref[h] = m_ref[h] + jnp.log(l_ref[h])
        rows_ref[...] = _blocks_to_rows(outs).astype(BF16)

    kvs = KV_PER_STEP
    qspec = pl.BlockSpec((kvs, None, HD, QROWS), lambda j, i: (j, i, 0, 0))
    return _call(
        body, "attn_fwd", (NKV // kvs, NCH),
        [qspec, pl.BlockSpec((kvs, S, HD), lambda j, i: (j, 0, 0)),
         pl.BlockSpec((kvs, HD, S), lambda j, i: (j, 0, 0)),
         pl.BlockSpec((NBIAS, 128, QROWS), lambda j, i: (0, 0, 0))],
        [qspec, pl.BlockSpec((kvs, None, 1, QROWS), lambda j, i: (j, i, 0, 0)),
         pl.BlockSpec((128, QCOLS * kvs), lambda j, i: (i, j))],
        [jax.ShapeDtypeStruct((NKV, NCH, HD, QROWS), F32), jax.ShapeDtypeStruct((NKV, NCH, 1, QROWS), F32),
         jax.ShapeDtypeStruct((S, D), BF16)],
        (qt, kh, vt, bias),
        [pltpu.VMEM((kvs, 1, QROWS), F32), pltpu.VMEM((kvs, 1, QROWS), F32), pltpu.VMEM((kvs, HD, QROWS), F32)],
        ("parallel", "parallel"), rider)


def _attn_delta(ot, dot_):
    def body(o_ref, do_ref, dl_ref):
        dl_ref[...] = jnp.sum(o_ref[...] * do_ref[...].astype(F32), axis=1, keepdims=True)

    spec = pl.BlockSpec((None, NCH, HD, QROWS), lambda j: (j, 0, 0, 0))
    return pl.pallas_call(
        body, name="attn_delta", grid=(NKV,), in_specs=[spec, spec],
        out_specs=pl.BlockSpec((None, NCH, 1, QROWS), lambda j: (j, 0, 0, 0)),
        out_shape=jax.ShapeDtypeStruct((NKV, NCH, 1, QROWS), F32), compiler_params=_cparams("parallel"),
    )(ot, dot_)


def _attn_bwd(qt, kh, kt, vh, dot_, lse, delta, bias, rider=None):
    def body(qt_ref, k_ref, kt_ref, v_ref, dot_ref, lse_ref, dl_ref, b_ref, dq_ref, dk_ref, dv_ref):
        kp = pl.program_id(1)

        @pl.when(kp == 0)
        def _():
            dq_ref[...] = jnp.zeros_like(dq_ref)

        dk_ref[...] = jnp.zeros_like(dk_ref)
        dv_ref[...] = jnp.zeros_like(dv_ref)

        @pl.loop(kp, NCH // 2)
        def _(j):
            for h in range(KV_PER_STEP):
                k, kt_, v = k_ref[h], kt_ref[h], v_ref[h]
                for qb in (2 * j, 2 * j + 1):
                    bias2 = jnp.concatenate([b_ref[qb - 2 * kp + 1], b_ref[qb - 2 * kp]], axis=0)
                    st = _dot(k, qt_ref[h, qb], NN) + bias2
                    pt = jnp.exp(st - lse_ref[h, qb])
                    dst = pt * (_dot(v, dot_ref[h, qb], NN) - dl_ref[h, qb])
                    dq_ref[h, qb] += _dot(kt_, dst, NN)
                    dk_ref[h] += _dot(qt_ref[h, qb], dst, NT)
                    dv_ref[h] += _dot(dot_ref[h, qb], pt, NT)

    kvs = KV_PER_STEP
    tspec = pl.BlockSpec((kvs, NCH, HD, QROWS), lambda j, i: (j, 0, 0, 0))
    kspec = pl.BlockSpec((kvs, 256, HD), lambda j, i: (j, i, 0))
    ktspec = pl.BlockSpec((kvs, HD, 256), lambda j, i: (j, 0, i))
    sspec = pl.BlockSpec((kvs, NCH, 1, QROWS), lambda j, i: (j, 0, 0, 0))
    return _call(
        body, "attn_bwd", (NKV // kvs, NCH // 2),
        [tspec, kspec, ktspec, kspec, tspec, sspec, sspec,
         pl.BlockSpec((NBIAS, 128, QROWS), lambda j, i: (0, 0, 0))],
        [tspec, ktspec, ktspec],
        [jax.ShapeDtypeStruct((NKV, NCH, HD, QROWS), F32),
         jax.ShapeDtypeStruct((NKV, HD, S), F32), jax.ShapeDtypeStruct((NKV, HD, S), F32)],
        (qt, kh, kt, vh, dot_, lse, delta, bias), sem=("parallel", "arbitrary"), rider=rider)


CONV_BLK = 256
CONV_COL0 = 1536 // CONV_BLK


def _shift_down(u, j, row):
    return jnp.where(row >= j, pltpu.roll(u, j, 0), 0.0)


def _conv_pre(u, w_ref, b_ref, row):
    y = b_ref[...] + w_ref[CONV_K - 1:CONV_K, :] * u
    for j in range(1, CONV_K):
        y = y + w_ref[CONV_K - 1 - j:CONV_K - j, :] * _shift_down(u, j, row)
    return y


def _conv_fwd(proj, convw, convb):
    def body(u_ref, w_ref, b_ref, o_ref):
        u = u_ref[...]
        row = lax.broadcasted_iota(jnp.int32, u.shape, 0)
        y = _conv_pre(u, w_ref, b_ref, row)
        o_ref[...] = y * _sigmoid(y)

    return pl.pallas_call(
        body, name="conv_fwd", grid=(CONV_C // CONV_BLK,),
        in_specs=[pl.BlockSpec((S, CONV_BLK), lambda i: (0, CONV_COL0 + i)),
                  pl.BlockSpec((CONV_K, CONV_BLK), lambda i: (0, i)),
                  pl.BlockSpec((1, CONV_BLK), lambda i: (0, i))],
        out_specs=pl.BlockSpec((S, CONV_BLK), lambda i: (0, i)),
        out_shape=jax.ShapeDtypeStruct((S, CONV_C), F32), compiler_params=_cparams("parallel"),
    )(proj, convw, convb)


def _conv_bwd(dact, proj, convw, convb, dproj):
    def body(da_ref, u_ref, w_ref, b_ref, buf_ref, du_ref, dw_ref, db_ref):
        u = u_ref[...]
        row = lax.broadcasted_iota(jnp.int32, u.shape, 0)
        y = _conv_pre(u, w_ref, b_ref, row)
        sg = _sigmoid(y)
        dy = da_ref[...] * (sg * (1.0 + y * (1.0 - sg)))
        db_ref[...] = jnp.sum(dy, axis=0, keepdims=True)
        du = w_ref[CONV_K - 1:CONV_K, :] * dy
        r8 = lax.broadcasted_iota(jnp.int32, (8, CONV_BLK), 0)
        dw = jnp.where(r8 == CONV_K - 1, jnp.sum(dy * u, axis=0, keepdims=True), 0.0)
        for j in range(1, CONV_K):
            du = du + w_ref[CONV_K - 1 - j:CONV_K - j, :] * jnp.where(row < S - j, pltpu.roll(dy, S - j, 0), 0.0)
            dw = dw + jnp.where(r8 == CONV_K - 1 - j,
                                jnp.sum(dy * _shift_down(u, j, row), axis=0, keepdims=True), 0.0)
        du_ref[...] = du.astype(BF16)
        dw_ref[...] = dw

    return pl.pallas_call(
        body, name="conv_bwd", grid=(CONV_C // CONV_BLK,),
        in_specs=[pl.BlockSpec((S, CONV_BLK), lambda i: (0, i)),
                  pl.BlockSpec((S, CONV_BLK), lambda i: (0, CONV_COL0 + i)),
                  pl.BlockSpec((CONV_K, CONV_BLK), lambda i: (0, i)),
                  pl.BlockSpec((1, CONV_BLK), lambda i: (0, i)), pl.BlockSpec(memory_space=pl.ANY)],
        out_specs=[pl.BlockSpec((S, CONV_BLK), lambda i: (0, CONV_COL0 + i)),
                   pl.BlockSpec((8, CONV_BLK), lambda i: (0, i)), pl.BlockSpec((1, CONV_BLK), lambda i: (0, i))],
        out_shape=[jax.ShapeDtypeStruct(dproj.shape, BF16), jax.ShapeDtypeStruct((8, CONV_C), F32),
                   jax.ShapeDtypeStruct((1, CONV_C), F32)],
        input_output_aliases={4: 0}, compiler_params=_cparams("parallel"),
    )(dact, proj, convw, convb, dproj)


NPAIR = 8


def _ssd_scalars(dtr_ref, dtb_ref, alog_ref):
    z = dtr_ref[...] + dtb_ref[...]
    dt = jnp.maximum(z, 0.0) + jnp.log(1.0 + jnp.exp(-jnp.abs(z)))
    a = -jnp.exp(alog_ref[...])
    r = lax.broadcasted_iota(jnp.int32, (128, 128), 0)
    c = lax.broadcasted_iota(jnp.int32, (128, 128), 1)
    tri = (r >= c).astype(F32)
    cs = _dot_exact(tri, dt * a)
    return z, dt, a, cs, r, c


def _by_lane(cs, dt):
    head = lax.broadcasted_iota(jnp.int32, (128, SSM_W), 0)
    lane = lax.broadcasted_iota(jnp.int32, (128, SSM_W), 1)
    sel = (head == lane // HD).astype(F32)
    cs_l = _dot_exact(cs, sel, "b")
    last_l = cs_l[127:128, :]
    return sel, jnp.exp(cs_l), jnp.exp(last_l - cs_l), _dot_exact(dt, sel, "b")


def _pair_terms(cs, h1, h2):
    return (cs[:, h1:h1 + 1], cs[:, h2:h2 + 1],
            jnp.exp(cs[127:128, h1:h1 + 1]), jnp.exp(cs[127:128, h2:h2 + 1]))


def _gate_norm(y, zv, w):
    yg = y * (zv * _sigmoid(zv))
    outs, rs = [], []
    for g in range(2):
        blk = yg[:, 512 * g:512 * (g + 1)]
        r = lax.rsqrt(jnp.mean(blk * blk, axis=-1, keepdims=True) + EPS)
        outs.append(blk * r)
        rs.append(r)
    return jnp.concatenate(outs, axis=1), rs, yg


def _ssd_fwd(xbc, proj, dtb, alog, dskip_l, ssmw):
    def body(x_ref, b_ref, c_ref, dtr_ref, z_ref, dtb_ref, alog_ref, dsk_ref, w_ref, y_ref, yn_ref, hp_ref, h_ref):
        @pl.when(pl.program_id(0) == 0)
        def _():
            h_ref[...] = jnp.zeros_like(h_ref)

        _, dt, _, cs, r, c = _ssd_scalars(dtr_ref, dtb_ref, alog_ref)
        cst = cs.T
        causal = r >= c
        lo = c < HD
        _, e_all, dte_all, dt_all = _by_lane(cs, dt)
        hp_ref[...] = h_ref[...]
        for g in range(2):
            bg = b_ref[:, 128 * g:128 * (g + 1)]
            cg = c_ref[:, 128 * g:128 * (g + 1)]
            cb = _dot(cg, bg, NT)
            for j in range(4):
                pj = 4 * g + j
                h1, h2 = 2 * pj, 2 * pj + 1
                sl = slice(128 * pj, 128 * (pj + 1))
                xp = x_ref[:, sl]
                c1, c2, cd1, cd2 = _pair_terms(cs, h1, h2)
                e_l, dte_l = e_all[:, sl], dte_all[:, sl]
                xdt = xp * dt_all[:, sl]
                m1 = cb * jnp.exp(jnp.where(causal, c1 - cst[h1:h1 + 1, :], NEG))
                m2 = cb * jnp.exp(jnp.where(causal, c2 - cst[h2:h2 + 1, :], NEG))
                yd = jnp.where(lo, _dot(m1, xdt, NN), _dot(m2, xdt, NN))
                hp = h_ref[pj]
                yo = _dot(cg, hp, NT) * e_l
                st = _dot(xdt * dte_l, bg, TN)
                h_ref[pj] = hp * jnp.where(r < HD, cd1, cd2) + st
                y_ref[:, sl] = yd + yo + dsk_ref[:, sl] * xp
        yn, _, _ = _gate_norm(y_ref[...], z_ref[...], w_ref[...])
        yn_ref[...] = (yn * w_ref[...]).astype(BF16)

    return pl.pallas_call(
        body, name="ssd_fwd", grid=(NCH,),
        in_specs=[pl.BlockSpec((128, SSM_W), lambda i: (i, 0)),
                  pl.BlockSpec((128, 256), lambda i: (i, 4)), pl.BlockSpec((128, 256), lambda i: (i, 5)),
                  pl.BlockSpec((128, 128), lambda i: (i, COL_DT // 128)),
                  pl.BlockSpec((128, SSM_W), lambda i: (i, 3)),
                  pl.BlockSpec((1, 128), lambda i: (0, 0)), pl.BlockSpec((1, 128), lambda i: (0, 0)),
                  pl.BlockSpec((1, SSM_W), lambda i: (0, 0)), pl.BlockSpec((1, SSM_W), lambda i: (0, 0))],
        out_specs=[pl.BlockSpec((128, SSM_W), lambda i: (i, 0)), pl.BlockSpec((128, SSM_W), lambda i: (i, 0)),
                   pl.BlockSpec((None, NPAIR, 128, 128), lambda i: (i, 0, 0, 0))],
        out_shape=[jax.ShapeDtypeStruct((S, SSM_W), F32), jax.ShapeDtypeStruct((S, SSM_W), BF16),
                   jax.ShapeDtypeStruct((NCH, NPAIR, 128, 128), F32)],
        scratch_shapes=[pltpu.VMEM((NPAIR, 128, 128), F32)],
        compiler_params=_cparams("arbitrary"),
    )(xbc, xbc, xbc, proj, proj, dtb, alog, dskip_l, ssmw)


def _ssd_bwd(dmixed, y, xbc, proj, hprev, dtb, alog, dskip_l, ssmw, rider=None):
    def body(dyn_ref, y_ref, x_ref, b_ref, c_ref, dtr_ref, z_ref, hp_ref, dtb_ref, alog_ref, dsk_ref, w_ref,
             dxbc_ref, dz_ref, ddt_ref, dw_ref, dsc_ref, g_ref):
        @pl.when(pl.program_id(0) == 0)
        def _():
            g_ref[...] = jnp.zeros_like(g_ref)
            dsc_ref[...] = jnp.zeros_like(dsc_ref)

        z, dt, a, cs, r, c = _ssd_scalars(dtr_ref, dtb_ref, alog_ref)
        cst = cs.T
        causal = r >= c
        lo = c < HD

        yv = y_ref[...]
        zv = z_ref[...]
        wv = w_ref[...]
        ygn, rs, yg = _gate_norm(yv, zv, wv)
        dyn = dyn_ref[...]
        _acc_rows(dw_ref, dyn * ygn)
        dynw = dyn * wv
        parts = []
        for g in range(2):
            sl = slice(512 * g, 512 * (g + 1))
            a_g, n_g = dynw[:, sl], ygn[:, sl]
            parts.append(rs[g] * (a_g - n_g * jnp.mean(a_g * n_g, axis=-1, keepdims=True)))
        dyg = jnp.concatenate(parts, axis=1)
        sz = _sigmoid(zv)
        dz_ref[...] = (dyg * yv * (sz * (1.0 + zv * (1.0 - sz)))).astype(BF16)
        dy_all = dyg * (zv * sz)

        dcs_cols = jnp.zeros((128, 128), F32)
        dcs_rows = jnp.zeros((128, 128), F32)
        sel, e_all, dte_all, dt_all = _by_lane(cs, dt)
        x_all, b_all, c_all, dsk_all = x_ref[...], b_ref[...], c_ref[...], dsk_ref[...]
        hp_all, g_all = hp_ref[...], g_ref[...]
        g_new, dx_parts, db_parts, dc_parts = [], [], [], []
        dyx_parts, ryo_parts, qx_parts, dxx_parts, gh_parts = [], [], [], [], []
        for g in range(2):
            bg = b_all[:, 128 * g:128 * (g + 1)]
            cg = c_all[:, 128 * g:128 * (g + 1)]
            cb = _dot(cg, bg, NT)
            dcb = jnp.zeros((128, 128), F32)
            db_acc = jnp.zeros((128, NST), F32)
            dc_acc = jnp.zeros((128, NST), F32)
            for j in range(4):
                pj = 4 * g + j
                h1, h2 = 2 * pj, 2 * pj + 1
                sl = slice(128 * pj, 128 * (pj + 1))
                xp = x_all[:, sl]
                dyp = dy_all[:, sl]
                c1, c2, cd1, cd2 = _pair_terms(cs, h1, h2)
                e_l, dte_l, dt_l = e_all[:, sl], dte_all[:, sl], dt_all[:, sl]
                xdt = xp * dt_l
                hp = hp_all[pj]
                gp = g_all[pj]
                dyx_parts.append(dyp * xp)
                dzs = dyp * e_l
                dc_acc = dc_acc + _dot(dzs, hp, NN)
                ryo_parts.append(dyp * (_dot(cg, hp, NT) * e_l))
                qm = _dot(bg, gp, NT)
                dxdt = qm * dte_l
                qx_parts.append(qm * xdt)
                db_acc = db_acc + _dot(xdt * dte_l, gp, NN)
                gh_parts.append(gp * hp)
                g_new.append(_dot(dzs, cg, TN) + jnp.where(r < HD, cd1, cd2) * gp)
                for hh, ch, msk in ((h1, c1, lo), (h2, c2, jnp.logical_not(lo))):
                    lm = jnp.exp(jnp.where(causal, ch - cst[hh:hh + 1, :], NEG))
                    mm = cb * lm
                    dm = jnp.where(causal, _dot(jnp.where(msk, dyp, 0.0), xdt, NT), 0.0)
                    w = dm * mm
                    dcs_cols = dcs_cols + jnp.where(c == hh, jnp.sum(w, axis=1, keepdims=True), 0.0)
                    dcs_rows = dcs_rows + jnp.where(r == hh, jnp.sum(w, axis=0, keepdims=True), 0.0)
                    dcb = dcb + dm * lm
                    dxdt = dxdt + jnp.where(msk, _dot(mm, dyp, TN), 0.0)
                dxx_parts.append(dxdt * xp)
                dx_parts.append(dsk_all[:, sl] * dyp + dxdt * dt_l)
            db_parts.append(db_acc + _dot(dcb, cg, TN))
            dc_parts.append(dc_acc + _dot(dcb, bg, NN))
        g_ref[...] = jnp.stack(g_new)
        dxbc_ref[...] = jnp.concatenate(dx_parts + db_parts + dc_parts, axis=1)

        selt = (lax.broadcasted_iota(jnp.int32, (SSM_W, 128), 0) // HD
                == lax.broadcasted_iota(jnp.int32, (SSM_W, 128), 1)).astype(F32)

        def by_head(parts):
            return _dot_exact(jnp.concatenate(parts, axis=1), selt, "b")

        ddt_x = by_head(dxx_parts)
        dd_row = jnp.sum(by_head(dyx_parts), axis=0, keepdims=True)
        t_all = by_head(qx_parts) * jnp.exp(cs[127:128, :] - cs)
        gh = jnp.sum(_dot_exact(sel, jnp.concatenate(gh_parts, axis=0)), axis=1, keepdims=True)
        gh_row = jnp.broadcast_to(gh, (128, 128)).T[0:1, :]
        at_end = jnp.sum(t_all, axis=0, keepdims=True) + gh_row * jnp.exp(cs[127:128, :])
        dcs = by_head(ryo_parts) - t_all + dcs_cols + jnp.where(r == 127, at_end, 0.0) - dcs_rows.T
        dad = _dot_exact((c >= r).astype(F32), dcs)
        ddt = dad * a + ddt_x
        ddtr = jnp.where(c < 16, ddt * _sigmoid(z), 0.0)
        ddt_ref[...] = ddtr.astype(BF16)
        r8 = lax.broadcasted_iota(jnp.int32, (8, 128), 0)
        dsc_ref[...] += (jnp.where(r8 == 0, jnp.sum(ddtr, axis=0, keepdims=True), 0.0)
                         + jnp.where(r8 == 1, jnp.sum(dad * dt, axis=0, keepdims=True) * a, 0.0)
                         + jnp.where(r8 == 2, dd_row, 0.0))

    rev = NCH - 1
    return _call(
        body, "ssd_bwd", (NCH,),
        [pl.BlockSpec((128, SSM_W), lambda i: (rev - i, 0)),
         pl.BlockSpec((128, SSM_W), lambda i: (rev - i, 0)),
         pl.BlockSpec((128, SSM_W), lambda i: (rev - i, 0)),
         pl.BlockSpec((128, 256), lambda i: (rev - i, 4)), pl.BlockSpec((128, 256), lambda i: (rev - i, 5)),
         pl.BlockSpec((128, 128), lambda i: (rev - i, COL_DT // 128)),
         pl.BlockSpec((128, SSM_W), lambda i: (rev - i, 3)),
         pl.BlockSpec((None, NPAIR, 128, 128), lambda i: (rev - i, 0, 0, 0)),
         pl.BlockSpec((1, 128), lambda i: (0, 0)), pl.BlockSpec((1, 128), lambda i: (0, 0)),
         pl.BlockSpec((1, SSM_W), lambda i: (0, 0)), pl.BlockSpec((1, SSM_W), lambda i: (0, 0))],
        [pl.BlockSpec((128, CONV_C), lambda i: (rev - i, 0)),
         pl.BlockSpec((128, SSM_W), lambda i: (rev - i, 3)),
         pl.BlockSpec((128, 128), lambda i: (rev - i, 0)),
         pl.BlockSpec((1, SSM_W), lambda i: (0, 0)), pl.BlockSpec((8, 128), lambda i: (0, 0))],
        [jax.ShapeDtypeStruct((S, CONV_C), F32), jax.ShapeDtypeStruct((S, WIN_PAD), BF16),
         jax.ShapeDtypeStruct((S, 128), BF16), jax.ShapeDtypeStruct((1, SSM_W), F32),
         jax.ShapeDtypeStruct((8, 128), F32)],
        (dmixed, y, xbc, xbc, xbc, proj, proj, hprev, dtb, alog, dskip_l, ssmw),
        [pltpu.VMEM((NPAIR, 128, 128), F32)], ("arbitrary",), rider)


def _cast_stack(name, slot, arrs, tr, tc):
    n = len(arrs)
    rows, cols = arrs[0].shape

    def body(s_ref, *refs):
        for i in range(n):
            refs[n][i] = refs[i][...].astype(BF16)

    return pl.pallas_call(
        body, name=name,
        grid_spec=pltpu.PrefetchScalarGridSpec(
            num_scalar_prefetch=1, grid=(rows // tr, cols // tc),
            in_specs=[pl.BlockSpec((tr, tc), lambda i, j, sr: (i, j))] * n,
            out_specs=pl.BlockSpec((None, n, tr, tc), lambda i, j, sr: (sr[0], 0, i, j))),
        out_shape=jax.ShapeDtypeStruct((NSH, n, rows, cols), BF16),
        compiler_params=_cparams("parallel", "parallel"),
    )(slot, *arrs)


def _pair_sum(name, c_idx, ps, th):
    n = len(ps)
    _, rows, _ = ps[0].shape

    def body(c_ref, *refs):
        mine, whole, out, theirs = refs[:n], refs[n:2 * n], refs[2 * n:3 * n], refs[3 * n:4 * n]
        send, recv = refs[4 * n], refs[4 * n + 1]
        s, i = pl.program_id(0), pl.program_id(1)
        x, y, c, _ = _place()

        def copies(slot):
            return [_rcopy(whole[k].at[slot, :, pl.ds((1 - c) * HALF, HALF)], theirs[k].at[slot],
                           send.at[slot * n + k], recv.at[slot * n + k], (x, y, 1 - c)) for k in range(n)]

        @pl.when((s == 0) & (i == 0))
        def _():
            for slot in range(NSH):
                for cp in copies(slot):
                    cp.start()

        @pl.when(i == 0)
        def _():
            for slot in range(NSH):
                @pl.when(s == slot)
                def _():
                    for cp in copies(slot):
                        cp.wait()

        rows_i = slice(None) if th == rows else pl.ds(pl.multiple_of(i * th, th), th)
        for k in range(n):
            out[k][...] = (mine[k][...].astype(F32) + theirs[k][s, rows_i, :].astype(F32)).astype(BF16)

    spec = pl.BlockSpec((None, th, HALF), lambda s, i, cr: (s, i, 0))
    return pl.pallas_call(
        body, name=name,
        grid_spec=pltpu.PrefetchScalarGridSpec(
            num_scalar_prefetch=1, grid=(NSH, rows // th),
            in_specs=[pl.BlockSpec((None, th, HALF), lambda s, i, cr: (s, i, cr[0]))] * n + _any_specs(n),
            out_specs=[spec] * n,
            scratch_shapes=[pltpu.VMEM((NSH, rows, HALF), BF16)] * n
            + [pltpu.SemaphoreType.DMA((NSH * n,)), pltpu.SemaphoreType.DMA((NSH * n,))]),
        out_shape=[jax.ShapeDtypeStruct((NSH, rows, HALF), BF16)] * n,
        compiler_params=_cparams("arbitrary", "arbitrary"),
    )(c_idx, *ps, *ps)


def _pair_add(name, c_idx, ps, theirs, th):
    n = len(ps)
    _, rows, _ = ps[0].shape

    def body(c_ref, *refs):
        for k in range(n):
            refs[2 * n + k][...] = (refs[k][...].astype(F32) + refs[n + k][...].astype(F32)).astype(BF16)

    spec = pl.BlockSpec((None, th, HALF), lambda s, i, cr: (s, i, 0))
    return pl.pallas_call(
        body, name=name,
        grid_spec=pltpu.PrefetchScalarGridSpec(
            num_scalar_prefetch=1, grid=(NSH, rows // th),
            in_specs=[pl.BlockSpec((None, th, HALF), lambda s, i, cr: (s, i, cr[0]))] * n + [spec] * n,
            out_specs=[spec] * n),
        out_shape=[jax.ShapeDtypeStruct((NSH, rows, HALF), BF16)] * n,
        compiler_params=_cparams("parallel", "parallel"),
    )(c_idx, *ps, *theirs)


def _chip_sum(name, place, cs, ts, th):
    n = len(ts)
    _, rows, _ = ts[0].shape

    def body(p_ref, *refs):
        for i in range(n):
            t = refs[n + i][...].astype(F32)
            refs[2 * n + i][...] = ((refs[i][...].astype(F32) + t[0]) + t[1]) + t[2]

    return pl.pallas_call(
        body, name=name,
        grid_spec=pltpu.PrefetchScalarGridSpec(
            num_scalar_prefetch=1, grid=(rows // th,),
            in_specs=[pl.BlockSpec((None, th, HALF), lambda i, pr: (pr[0], i, 0))] * n
            + [pl.BlockSpec((3, th, HALF), lambda i, pr: (0, i, 0))] * n,
            out_specs=[pl.BlockSpec((th, HALF), lambda i, pr: (i, pr[1]))] * n),
        out_shape=[jax.ShapeDtypeStruct((rows, D), F32)] * n, compiler_params=_cparams("parallel"),
    )(place, *cs, *ts)


def _adamw(name, ws, gs, ms, vs, tr, tc):
    n = len(ws)
    shape = ws[0].shape
    rows, cols, mid = shape[0], shape[-1], shape[1:-1]
    c1 = 1.0 / (1.0 - ADAM_B1 ** ADAM_STEP)
    c2 = 1.0 / (1.0 - ADAM_B2 ** ADAM_STEP)

    def body(*refs):
        for i in range(n):
            w, g, m, v = (refs[k * n + i][...] for k in range(4))
            m2 = ADAM_B1 * m + (1.0 - ADAM_B1) * g
            v2 = ADAM_B2 * v + (1.0 - ADAM_B2) * (g * g)
            refs[4 * n + 4 * i][...] = -ADAM_LR * ((m2 * c1) / (jnp.sqrt(v2 * c2) + ADAM_EPS) + ADAM_WD * w)
            refs[4 * n + 4 * i + 1][...] = m2
            refs[4 * n + 4 * i + 2][...] = v2
            refs[4 * n + 4 * i + 3][...] = g

    spec = pl.BlockSpec((tr,) + mid + (tc,), lambda i, j: (i,) + (0,) * len(mid) + (j,))
    outs = pl.pallas_call(
        body, name=name, grid=(rows // tr, cols // tc), in_specs=[spec] * (4 * n), out_specs=[spec] * (4 * n),
        out_shape=[jax.ShapeDtypeStruct(shape, F32)] * (4 * n),
        compiler_params=_cparams("parallel", "parallel"),
    )(*ws, *gs, *ms, *vs)
    return [tuple(outs[4 * i:4 * i + 4]) for i in range(n)]


def _place():
    x, y, c = lax.axis_index("x"), lax.axis_index("y"), lax.axis_index("c")
    chips = [(1 - x, y), (x, 1 - y), (1 - x, 1 - y)]
    return x, y, c, chips


def _any_specs(n):
    return [pl.BlockSpec(memory_space=pl.ANY)] * n


def _rcopy(src, dst, send_sem, recv_sem, dev):
    return pltpu.make_async_remote_copy(src_ref=src, dst_ref=dst, send_sem=send_sem, recv_sem=recv_sem,
                                        device_id=dev, device_id_type=MESH)


QUARTER = HALF // 2

TO_X, TO_Y, RELAY_X, RELAY_Y, FWD_X, FWD_Y, FWD_D0, FWD_D1 = range(8)
TO_D = RELAY_X


def _gather_rider(bufs, views, relay):
    n = len(bufs)

    def plan(rout, sems):
        send, recv = sems
        x, y, c, _ = _place()
        me, sx, sy, sd = 2 * x + y, 2 * (1 - x) + y, 2 * x + (1 - y), 2 * (1 - x) + (1 - y)
        nx, ny, nd, sib = (1 - x, y, c), (x, 1 - y, c), (1 - x, 1 - y, c), (x, y, 1 - c)
        mine, other = c * HALF, (1 - c) * HALF
        out = {TO_X: (me, mine, HALF, nx), TO_Y: (me, mine, HALF, ny),
               FWD_X: (sx, mine, HALF, sib), FWD_Y: (sy, mine, HALF, sib)}
        inn = {TO_X: (sx, mine, HALF), TO_Y: (sy, mine, HALF),
               FWD_X: (sx, other, HALF), FWD_Y: (sy, other, HALF)}
        if relay:
            out.update({RELAY_X: (sy, mine, QUARTER, nx), RELAY_Y: (sx, mine + QUARTER, QUARTER, ny),
                        FWD_D0: (sd, mine, QUARTER, sib), FWD_D1: (sd, mine + QUARTER, QUARTER, sib)})
            inn.update({RELAY_X: (sd, mine, QUARTER), RELAY_Y: (sd, mine + QUARTER, QUARTER),
                        FWD_D0: (sd, other, QUARTER), FWD_D1: (sd, other + QUARTER, QUARTER)})
        else:
            out.update({TO_D: (me, mine, HALF, nd), FWD_D0: (sd, mine, HALF, sib)})
            inn.update({TO_D: (sd, mine, HALF), FWD_D0: (sd, other, HALF)})

        def copy(kind, b):
            slot, col, ncols, dev = out[kind]
            win = views[b](rout[b], slot, col, ncols)
            return _rcopy(win, win, send.at[kind * n + b], recv.at[kind * n + b], dev)

        def land(kind, b):
            slot, col, ncols = inn[kind]
            win = views[b](rout[b], slot, col, ncols)
            return _rcopy(win, win, send.at[kind * n + b], recv.at[kind * n + b], (x, y, c))

        return copy, land

    if relay:
        first = (TO_X, TO_Y)
        chain = ((TO_X, (FWD_X, RELAY_Y)), (TO_Y, (FWD_Y, RELAY_X)), (RELAY_X, (FWD_D0,)), (RELAY_Y, (FWD_D1,)))
    else:
        first = (TO_X, TO_Y, TO_D)
        chain = ((TO_X, (FWD_X,)), (TO_Y, (FWD_Y,)), (TO_D, (FWD_D0,)))
    forwards = [k for _, then in chain for k in then if k in (FWD_X, FWD_Y, FWD_D0, FWD_D1)]
    sent = list(first) + [k for _, then in chain for k in then]

    def start(rin, rout, sems):
        copy, _ = plan(rout, sems)
        for kind in first:
            for b in range(n):
                copy(kind, b).start()

    def finish(rin, rout, sems):
        copy, land = plan(rout, sems)
        for landed, then in chain:
            for b in range(n):
                land(landed, b).wait_recv()
                for kind in then:
                    copy(kind, b).start()
        for kind in forwards:
            for b in range(n):
                land(kind, b).wait_recv()
        for kind in sent:
            for b in range(n):
                copy(kind, b).wait_send()

    return _Rider(list(bufs), [jax.ShapeDtypeStruct(a.shape, a.dtype) for a in bufs], {b: b for b in range(n)},
                  [pltpu.SemaphoreType.DMA((8 * n,))] * 2, start, finish)


def _small_gather_rider(cw):
    def descs(rin, rout, sems, x, y, c, chips):
        return [_rcopy(rin[0], rout[0].at[2 * x + y], sems[1].at[j], sems[2].at[j], (chip[0], chip[1], c))
                for j, chip in enumerate(chips)]

    def start(rin, rout, sems):
        x, y, c, chips = _place()
        pltpu.make_async_copy(rin[0], rout[0].at[2 * x + y], sems[0].at[0]).start()
        for cp in descs(rin, rout, sems, x, y, c, chips):
            cp.start()

    def finish(rin, rout, sems):
        x, y, c, chips = _place()
        for j, chip in enumerate(chips):
            _rcopy(rin[0], rout[0].at[2 * chip[0] + chip[1]], sems[1].at[j], sems[2].at[j], (x, y, c)).wait_recv()
        for cp in descs(rin, rout, sems, x, y, c, chips):
            cp.wait_send()
        pltpu.make_async_copy(rin[0], rout[0].at[2 * x + y], sems[0].at[0]).wait()

    return _Rider([cw], [jax.ShapeDtypeStruct((NSH,) + cw.shape, cw.dtype)], {},
                  [pltpu.SemaphoreType.DMA((1,)), pltpu.SemaphoreType.DMA((3,)), pltpu.SemaphoreType.DMA((3,))],
                  start, finish)


def _to_sibling_rider(ps):
    n = len(ps)

    def descs(rin, rout, sems):
        x, y, c, _ = _place()
        return [_rcopy(rin[i].at[:, :, pl.ds((1 - c) * HALF, HALF)], rout[i], sems[0].at[i], sems[1].at[i],
                       (x, y, 1 - c)) for i in range(n)]

    def start(rin, rout, sems):
        for cp in descs(rin, rout, sems):
            cp.start()

    def finish(rin, rout, sems):
        for cp in descs(rin, rout, sems):
            cp.wait()

    return _Rider(list(ps), [jax.ShapeDtypeStruct(a.shape[:2] + (HALF,), a.dtype) for a in ps], {},
                  [pltpu.SemaphoreType.DMA((n,))] * 2, start, finish)


def _to_chips_rider(cs):
    n = len(cs)

    def descs(rin, rout, sems):
        x, y, c, chips = _place()
        return [_rcopy(rin[i].at[2 * chip[0] + chip[1]], rout[i].at[j], sems[0].at[j * n + i], sems[1].at[j * n + i],
                       (chip[0], chip[1], c)) for j, chip in enumerate(chips) for i in range(n)]

    def start(rin, rout, sems):
        for cp in descs(rin, rout, sems):
            cp.start()

    def finish(rin, rout, sems):
        for cp in descs(rin, rout, sems):
            cp.wait()

    return _Rider(list(cs), [jax.ShapeDtypeStruct((3,) + a.shape[1:], a.dtype) for a in cs], {},
                  [pltpu.SemaphoreType.DMA((3 * n,))] * 2, start, finish)


def _run_riders(name, riders):
    n_in = [len(r.operands) for r in riders]
    n_out = [len(r.out_shapes) for r in riders]
    n_sem = [len(r.sems) for r in riders]

    def body(*refs):
        parts, at = [], 0
        for counts in (n_in, n_out, n_sem):
            group = []
            for k in counts:
                group.append(refs[at:at + k])
                at += k
            parts.append(group)
        for i, r in enumerate(riders):
            r.start(parts[0][i], parts[1][i], parts[2][i])
        for i, r in enumerate(riders):
            r.finish(parts[0][i], parts[1][i], parts[2][i])

    aliases = {}
    for i, r in enumerate(riders):
        for k, v in r.aliases.items():
            aliases[sum(n_in[:i]) + k] = sum(n_out[:i]) + v
    res = pl.pallas_call(
        body, name=name, in_specs=_any_specs(sum(n_in)), out_specs=_any_specs(sum(n_out)),
        out_shape=[s for r in riders for s in r.out_shapes], input_output_aliases=aliases,
        scratch_shapes=[s for r in riders for s in r.sems],
    )(*[a for r in riders for a in r.operands])
    out, at = [], 0
    for k in n_out:
        out.append(list(res[at:at + k]))
        at += k
    return out


def _swap_halves(gs):
    n = len(gs)

    def body(*refs):
        dst, send, recv = refs[n:2 * n], refs[2 * n], refs[2 * n + 1]
        x, y, c, _ = _place()
        cps = []
        for i in range(n):
            mine = dst[i].at[:, pl.ds(c * HALF, HALF)]
            cps.append(pltpu.make_async_remote_copy(
                src_ref=mine, dst_ref=mine, send_sem=send.at[i], recv_sem=recv.at[i],
                device_id=(x, y, 1 - c), device_id_type=MESH))
        for cp in cps:
            cp.start()
        for i in range(n):
            other = dst[i].at[:, pl.ds((1 - c) * HALF, HALF)]
            pltpu.make_async_remote_copy(
                src_ref=other, dst_ref=other, send_sem=send.at[i], recv_sem=recv.at[i],
                device_id=(x, y, c), device_id_type=MESH).wait_recv()
        for cp in cps:
            cp.wait_send()

    return pl.pallas_call(
        body, name="grads_swap_halves", in_specs=_any_specs(n), out_specs=_any_specs(n),
        out_shape=[jax.ShapeDtypeStruct(g.shape, g.dtype) for g in gs],
        input_output_aliases={i: i for i in range(n)},
        scratch_shapes=[pltpu.SemaphoreType.DMA((n,)), pltpu.SemaphoreType.DMA((n,))],
    )(*gs)


SMALL_ROWS = 16


def _allreduce_small(vec):
    def body(v_ref, o_ref, buf, send, recv):
        x, y, c, _ = _place()
        me = 4 * x + 2 * y + c
        buf[me] = v_ref[...]
        cps = []
        for k in range(1, 8):
            peer = (x ^ (k >> 2), y ^ ((k >> 1) & 1), c ^ (k & 1))
            cps.append(pltpu.make_async_remote_copy(
                src_ref=v_ref, dst_ref=buf.at[me], send_sem=send.at[k - 1], recv_sem=recv.at[k - 1],
                device_id=peer, device_id_type=MESH))
        for cp in cps:
            cp.start()
        for k in range(1, 8):
            pltpu.make_async_remote_copy(
                src_ref=v_ref, dst_ref=buf.at[me ^ k], send_sem=send.at[k - 1], recv_sem=recv.at[k - 1],
                device_id=(x, y, c), device_id_type=MESH).wait_recv()
        for cp in cps:
            cp.wait_send()
        t = buf[0]
        for d in range(1, 8):
            t = t + buf[d]
        o_ref[...] = t

    return pl.pallas_call(
        body, name="allreduce_small",
        in_specs=[pl.BlockSpec(memory_space=pltpu.VMEM)], out_specs=pl.BlockSpec(memory_space=pltpu.VMEM),
        out_shape=jax.ShapeDtypeStruct((SMALL_ROWS, D), F32),
        scratch_shapes=[pltpu.VMEM((8, SMALL_ROWS, D), F32), pltpu.SemaphoreType.DMA((7,)),
                        pltpu.SemaphoreType.DMA((7,))],
    )(vec)


def _col_window(ref, slot, col, ncols):
    return ref.at[slot, :, pl.ds(col, ncols)]


def _stack_window(first, count):
    def view(ref, slot, col, ncols):
        return ref.at[slot, pl.ds(first, count), :, pl.ds(col, ncols)]
    return view


def _row_tile(rows):
    for t in range(512, 15, -16):
        if rows % t == 0:
            return t
    return rows


def _same_shape_runs(arrs):
    runs, a = [], 0
    for b in range(1, len(arrs) + 1):
        if b == len(arrs) or arrs[b].shape != arrs[a].shape:
            runs.append((a, b))
            a = b
    return runs


class _Comm:
    def __init__(self):
        x, y, c = lax.axis_index("x"), lax.axis_index("y"), lax.axis_index("c")
        self.c_idx = jnp.reshape(c, (1,)).astype(jnp.int32)
        self.place = jnp.stack([2 * x + y, c]).astype(jnp.int32)
        self.groups = {}

    @staticmethod
    def gather(*bufs, relay, part=None):
        views = [_col_window if b.ndim == 3 else _stack_window(*(part or (0, b.shape[1]))) for b in bufs]
        return _gather_rider(list(bufs), views, relay)

    def reduce_rider(self, tag, names, ps, theirs=None):
        csums = []
        for a, b in _same_shape_runs(ps):
            name, th = "pair_sum_%s%d" % (tag, a), _row_tile(ps[a].shape[1])
            csums += (_pair_sum(name, self.c_idx, ps[a:b], th) if theirs is None else
                      _pair_add(name, self.c_idx, ps[a:b], theirs[a:b], th))
        self.groups[tag] = [names, csums, None]
        return _to_chips_rider(csums)

    def landed(self, tag, ts):
        self.groups[tag][2] = ts

    def finish(self):
        names, csums, ts = [], [], []
        for group_names, group_csums, group_ts in self.groups.values():
            names += group_names
            csums += group_csums
            ts += group_ts
        order = sorted(range(len(names)), key=lambda i: csums[i].shape[1])
        names, csums, ts = ([v[i] for i in order] for v in (names, csums, ts))
        halves = []
        for a, b in _same_shape_runs(csums):
            halves += _chip_sum("chip_sum_%d" % a, self.place, csums[a:b], ts[a:b], _row_tile(csums[a].shape[1]))
        return dict(zip(names, _swap_halves(halves)))


ROPE_THETA = 10000.0
SMALL_1K = ("ffn1_pre_norm", "ffn1_post_norm", "mix_pre_norm", "ssm_norm", "mix_post_norm",
            "ffn2_pre_norm", "ffn2_post_norm")
SMALL_16 = ("dt_bias", "a_log", "d_skip")
OFF_CONVB = 7 * D
OFF_16 = OFF_CONVB + CONV_C
OFF_CONVW = OFF_16 + 48
OFF_LOSS = OFF_CONVW + CONV_K * CONV_C
SMALL_LEN = SMALL_ROWS * D


def _sds(shape, dtype):
    return jax.ShapeDtypeStruct(shape, dtype)


def _ridden(res, rider):
    return res if rider is not None else (res, None)


def _ffn_down(name, act, w, tail_of, rider=None):
    tail, o_specs, o_shapes = tail_of(TS)
    return _mm(name, [act, w.dn], NN, (S // TS,),
               [pl.BlockSpec((NSH, TS, FS), lambda i: (0, i, 0)),
                pl.BlockSpec((NSH, None, FS, D), lambda i: (0, w.d0, 0, 0))], o_specs, o_shapes, rider, tail)


def _ffn_dw(name, a, b, rider=None):
    return _mm(name, [a, b], TN, (NSH,),
               [pl.BlockSpec((None, S, FS), lambda s: (s, 0, 0)), pl.BlockSpec((S, D), lambda s: (0, 0))],
               pl.BlockSpec((None, FS, D), lambda s: (s, 0, 0)), _sds((NSH, FS, D), BF16), rider)


def _ffn_dn(name, dgate, dup, w, tail_of, rider=None):
    rows = TS // 2
    tail, o_specs, o_shapes = tail_of(rows)
    a2 = pl.BlockSpec((NSH, rows, FS), lambda i: (0, i, 0))
    return _mm(name, [dgate, w.gu, dup, w.gu], NN, (S // rows,),
               [a2, pl.BlockSpec((NSH, None, FS, D), lambda i: (0, w.g0, 0, 0)),
                a2, pl.BlockSpec((NSH, None, FS, D), lambda i: (0, w.g0 + 1, 0, 0))], o_specs, o_shapes, rider, tail)


def _out_proj_dx(dh, wout):
    def body(dh_ref, w_ref, dyn_ref, do_ref):
        dm = _dot(dh_ref[...], w_ref[...], NT)
        dyn_ref[...] = dm[:, D:]
        for b in range(TS // 128):
            for j, blk in enumerate(_rows_to_blocks(dm[128 * b:128 * (b + 1), :D])):
                do_ref[j, b] = blk.astype(BF16)

    return pl.pallas_call(
        body, name="out_proj_dx", grid=(S // TS,),
        in_specs=[pl.BlockSpec((TS, D), lambda i: (i, 0)), pl.BlockSpec((2 * D, D), lambda i: (0, 0))],
        out_specs=[pl.BlockSpec((TS, D), lambda i: (i, 0)),
                   pl.BlockSpec((NKV, TS // 128, HD, QROWS), lambda i: (0, i, 0, 0))],
        out_shape=[_sds((S, D), F32), _sds((NKV, NCH, HD, QROWS), BF16)], compiler_params=_cparams("parallel"),
    )(dh, wout)


def _heads(t, n):
    return t.reshape(S, n, HD).transpose(1, 0, 2)


def _pad128(v):
    return jnp.pad(v, ((0, 0), (0, 128 - v.shape[1])))


def _local_step(x, positions, tgt, sp, gu1, d1, f2, wint, wout, convw, comm=None):
    inv_freq = ROPE_THETA ** (-jnp.arange(0, HD, 2, dtype=F32) / HD)
    ang = positions.astype(F32)[:, None] * inv_freq
    ang = jnp.concatenate([ang, ang, ang, ang], axis=-1)
    cos, sin = jnp.cos(ang), jnp.sin(ang)
    dtb, alog = _pad128(sp["dt_bias"]), _pad128(sp["a_log"])
    dskip_l = jnp.repeat(sp["d_skip"], HD, axis=1)
    convb = sp["conv_b"]

    n1 = _prenorm("prenorm1", x, sp["ffn1_pre_norm"])
    rider = comm.gather(d1, relay=False) if comm else None
    (fg1, fu1, act1), got = _ridden(_ffn_up("ffn1_up", n1, _FfnW(gu1, 0, d1, 0), rider), rider)
    if comm:
        d1, = got
    w1 = _FfnW(gu1, 0, d1, 0)
    rider = comm.gather(wint, relay=True) if comm else None
    (h1, x1, n2), got = _ridden(_ffn_down(
        "ffn1_down", act1, w1,
        lambda rows: _tail_postres(rows, x, sp["ffn1_post_norm"], 0.5, sp["mix_pre_norm"]), rider), rider)
    if comm:
        wint, = got
    wint_pad = jnp.pad(wint.reshape(WIN_COLS, D), ((0, WIN_PAD - WIN_COLS), (0, 0)))

    pw = WIN_PAD // 3
    rider = comm.gather(f2, relay=False, part=(0, 1)) if comm else None
    proj, got = _ridden(_mm(
        "in_proj", [n2, wint_pad], NT, (S // TS, 3),
        [pl.BlockSpec((TS, D), lambda i, j: (i, 0)), pl.BlockSpec((pw, D), lambda i, j: (j, 0))],
        pl.BlockSpec((TS, pw), lambda i, j: (i, j)), _sds((S, WIN_PAD), F32), rider), rider)
    if comm:
        f2, = got
    qt = _rope_q(proj, cos, sin)
    k_rot, v_bf, kt, vt = _rope_kv(proj, cos, sin)
    kh, vh = _heads(k_rot, NKV), _heads(v_bf, NKV)
    bias = _bias_table()
    rider = comm.gather(f2, wout, relay=False, part=(1, 2)) if comm else None
    (ot, lse, attn), got = _ridden(_attn_fwd(qt, kh, vt, bias, rider), rider)
    if comm:
        f2, wout = got
    w2 = _FfnW(f2, 0, f2, 2)
    wout = wout.reshape(2 * D, D)
    xbc = _conv_fwd(proj, convw, convb)
    y, yn, hprev = _ssd_fwd(xbc, proj, dtb, alog, dskip_l, sp["ssm_norm"])
    mixed = jnp.concatenate([attn, yn], axis=1)
    tail, o_specs, o_shapes = _tail_postres(TS, x1, sp["mix_post_norm"], 1.0, sp["ffn2_pre_norm"])
    h2, x2, n3 = _mm("out_proj", [mixed, wout], NN, (S // TS,),
                     [pl.BlockSpec((TS, 2 * D), lambda i: (i, 0)), pl.BlockSpec((2 * D, D), lambda i: (0, 0))],
                     o_specs, o_shapes, None, tail)

    fg2, fu2, act2 = _ffn_up("ffn2_up", n3, w2)
    dy, dh3, dp3, loss = _ffn_down(
        "ffn2_down", act2, w2, lambda rows: _tail_final(rows, x2, sp["ffn2_post_norm"], tgt, 0.5))

    dgate2, dup2 = _ffn_dact("ffn2_dact", dh3, w2, fg2, fu2)
    dws2 = [_ffn_dw("ffn2_dwg", dgate2, n3), _ffn_dw("ffn2_dwu", dup2, n3), _ffn_dw("ffn2_dwd", act2, dh3)]
    dx2, dh2, dg3, dp2 = _ffn_dn(
        "ffn2_dn", dgate2, dup2, w2,
        lambda rows: _tail_mid_bwd(rows, dy, x2, sp["ffn2_pre_norm"], h2, sp["mix_post_norm"], 1.0))

    dyn, dot_ = _out_proj_dx(dh2, wout)
    dwout = _mm("out_proj_dw", [mixed, dh2], TN, (2,),
                [pl.BlockSpec((S, D), lambda m: (0, m)), pl.BlockSpec((S, D), lambda m: (0, 0))],
                pl.BlockSpec((D, D), lambda m: (m, 0)), _sds((2 * D, D), BF16))
    dwout = dwout.reshape(NSH, 2 * D // NSH, D)

    def riding(tag, names, ps, call, theirs=None):
        rider = comm.reduce_rider(tag, names, ps, theirs) if comm else None
        res, got = _ridden(call(rider), rider)
        if comm:
            comm.landed(tag, got)
        return res

    rider = _to_sibling_rider(dws2 + [dwout]) if comm else None
    (dxbc, dproj, ddt, dssm, dsc), theirs = _ridden(
        _ssd_bwd(dyn, y, xbc, proj, hprev, dtb, alog, dskip_l, sp["ssm_norm"], rider), rider)
    dproj, dcw8, dcb = _conv_bwd(dxbc, proj, convw, convb, dproj)
    delta = _attn_delta(ot, dot_)
    dqt, dkh, dvh = riding("a", BIG[3:6] + ("w_out",), dws2 + [dwout], lambda rider: _attn_bwd(
        qt, kh, kt, vh, dot_, lse, delta, bias, rider), theirs)
    dproj = _rope_dq(dqt, cos, sin, dproj)
    dproj = _rope_dkv(dkh, dvh, cos, sin, dproj)
    dproj = lax.dynamic_update_slice(dproj, ddt, (0, COL_DT))
    dwint = _mm("in_proj_dw", [dproj, n2], TN, (3,),
                [pl.BlockSpec((S, pw), lambda j: (0, j)), pl.BlockSpec((S, D), lambda j: (0, 0))],
                pl.BlockSpec((pw, D), lambda j: (j, 0)), _sds((WIN_PAD, D), BF16))
    dwint = dwint[:WIN_COLS].reshape(NSH, WIN_SH, D)

    tail, o_specs, o_shapes = _tail_mid_bwd(TS, dx2, x1, sp["mix_pre_norm"], h1, sp["ffn1_post_norm"], 0.5)
    dx1, dh1, dg2, dp1 = riding("b", ("w_in",), [dwint], lambda rider: _mm(
        "in_proj_dx", [dproj, wint_pad], NN, (S // TS,),
        [pl.BlockSpec((TS, WIN_PAD), lambda i: (i, 0)), pl.BlockSpec((WIN_PAD, D), lambda i: (0, 0))],
        o_specs, o_shapes, rider, tail))

    dwd1 = _ffn_dw("ffn1_dwd", act1, dh1)
    dgate1, dup1 = riding("d", BIG[2:3], [dwd1], lambda rider: _ffn_dact("ffn1_dact", dh1, w1, fg1, fu1, rider))
    dwg1, dwu1 = _ffn_dw("ffn1_dwg", dgate1, n1), _ffn_dw("ffn1_dwu", dup1, n1)
    grad_x, dg1 = riding("g", BIG[0:2], [dwg1, dwu1], lambda rider: _ffn_dn(
        "ffn1_dn", dgate1, dup1, w1, lambda rows: _tail_first_bwd(rows, dx1, x, sp["ffn1_pre_norm"]), rider))
    dws1 = [dwg1, dwu1, dwd1]

    small = jnp.concatenate([
        dg1[0], dp1[0], dg2[0], dssm[0], dp2[0], dg3[0], dp3[0], dcb[0],
        dsc[0, :16], dsc[1, :16], dsc[2, :16], dcw8[:CONV_K].reshape(-1), loss[0, :1]])
    small = jnp.pad(small, (0, SMALL_LEN - small.shape[0])).reshape(SMALL_ROWS, D)
    if comm is None:
        return grad_x, dws1 + dws2 + [dwint, dwout], small
    return grad_x, comm.finish(), small


WEIGHTS = ("ffn1_pre_norm", "ffn1_w_gate", "ffn1_w_up", "ffn1_w_down", "ffn1_post_norm", "mix_pre_norm", "w_in",
           "conv_w", "conv_b", "dt_bias", "a_log", "d_skip", "ssm_norm", "w_out", "mix_post_norm", "ffn2_pre_norm",
           "ffn2_w_gate", "ffn2_w_up", "ffn2_w_down", "ffn2_post_norm")
BIG = ("ffn1_w_gate", "ffn1_w_up", "ffn1_w_down", "ffn2_w_gate", "ffn2_w_up", "ffn2_w_down", "w_in", "w_out")
TRANSPOSED = ("ffn1_w_gate", "ffn1_w_up", "ffn2_w_gate", "ffn2_w_up", "w_in")
SMALL_ORDER = SMALL_1K + ("conv_b",) + SMALL_16
CONVW_SH = CONV_C // NSH


def _shard2d(t, name):
    return t[0].T if name in TRANSPOSED else t[0]


def _unshard2d(t, name):
    return (t.T if name in TRANSPOSED else t)[None]


def _rows3d(t):
    return t.transpose(2, 0, 1)


def _pack_small(d, prefix, shard_of_convw):
    flat = jnp.concatenate([d[prefix + n][0] for n in SMALL_ORDER] + [shard_of_convw.reshape(-1)])
    return jnp.pad(flat, (0, SMALL_LEN - flat.shape[0])).reshape(SMALL_ROWS, D)


def _unpack_small(block, like):
    flat = block.reshape(-1)
    out, off = {}, 0
    for n in SMALL_ORDER:
        size = like[n].shape[1]
        out[n] = flat[off:off + size].reshape(1, size)
        off += size
    out["conv_w"] = flat[off:off + CONV_K * CONVW_SH].reshape(1, CONV_K, CONVW_SH)
    return out


def kernel(x, positions, ffn1_pre_norm, ffn1_w_gate, ffn1_w_up, ffn1_w_down, ffn1_post_norm, mix_pre_norm, w_in, conv_w, conv_b, dt_bias, a_log, d_skip, ssm_norm, w_out, mix_post_norm, ffn2_pre_norm, ffn2_w_gate, ffn2_w_up, ffn2_w_down, ffn2_post_norm, loss_target, m_ffn1_pre_norm, m_ffn1_w_gate, m_ffn1_w_up, m_ffn1_w_down, m_ffn1_post_norm, m_mix_pre_norm, m_w_in, m_conv_w, m_conv_b, m_dt_bias, m_a_log, m_d_skip, m_ssm_norm, m_w_out, m_mix_post_norm, m_ffn2_pre_norm, m_ffn2_w_gate, m_ffn2_w_up, m_ffn2_w_down, m_ffn2_post_norm, v_ffn1_pre_norm, v_ffn1_w_gate, v_ffn1_w_up, v_ffn1_w_down, v_ffn1_post_norm, v_mix_pre_norm, v_w_in, v_conv_w, v_conv_b, v_dt_bias, v_a_log, v_d_skip, v_ssm_norm, v_w_out, v_mix_post_norm, v_ffn2_pre_norm, v_ffn2_w_gate, v_ffn2_w_up, v_ffn2_w_down, v_ffn2_post_norm):
    given = dict(locals())
    xi, yi = lax.axis_index("x"), lax.axis_index("y")

    shard = jnp.reshape(2 * xi + yi, (1,)).astype(jnp.int32)
    big = {p + n: _shard2d(given[p + n], n) for n in BIG for p in ("", "m_", "v_")}
    gu1 = _cast_stack("cast_ffn1_gate_up", shard, [big[n] for n in BIG[0:2]], 176, D)
    d1 = _cast_stack("cast_ffn1_down", shard, [big[BIG[2]]], 176, D)
    f2 = _cast_stack("cast_ffn2", shard, [big[n] for n in BIG[3:6]], 176, D)
    winsh = _cast_stack("cast_w_in", shard, [big["w_in"]], WIN_SH, 256).reshape(NSH, WIN_SH, D)
    woutsh = _cast_stack("cast_w_out", shard, [big["w_out"]], 256, D).reshape(NSH, 2 * D // NSH, D)
    comm = _Comm()
    (gu1,), (cwf,) = _run_riders("gather_ffn1_gate_up", [comm.gather(gu1, relay=True), _small_gather_rider(conv_w[0])])
    convw = cwf.transpose(1, 0, 2).reshape(CONV_K, CONV_C)

    sp = {n: given[n] for n in SMALL_ORDER}
    grad_x, big_grads, small = _local_step(x[0], positions[0], loss_target[0], sp, gu1, d1, f2, winsh, woutsh,
                                           convw, comm)

    tot = _allreduce_small(small).reshape(-1)
    loss = tot[OFF_LOSS]
    small_grads, off = {}, 0
    for n in SMALL_ORDER:
        size = given[n].shape[1]
        small_grads[n] = tot[off:off + size].reshape(1, size)
        off += size
    dconvw = tot[OFF_CONVW:OFF_CONVW + CONV_K * CONV_C].reshape(CONV_K, NSH, CONVW_SH)
    dconvw = lax.dynamic_index_in_dim(dconvw, 2 * xi + yi, axis=1, keepdims=False)
    small_grads["conv_w"] = dconvw.reshape(1, CONV_K, CONVW_SH)

    upd = {}
    for names, tr in ((BIG[0:3], 176), (BIG[3:6], 176), (BIG[7:8], 256)):
        res = _adamw("adamw_" + names[0], [big[n] for n in names], [big_grads[n] for n in names],
                     [big["m_" + n] for n in names], [big["v_" + n] for n in names], tr, D)
        for n, r in zip(names, res):
            upd[n] = tuple(_unshard2d(t, n) for t in r)
    g_win = big_grads["w_in"].reshape(WIN_SH, 1, D)
    res, = _adamw("adamw_w_in", [_rows3d(w_in)], [g_win], [_rows3d(m_w_in)], [_rows3d(v_w_in)], WIN_SH // 4, D)
    upd["w_in"] = tuple(t.transpose(1, 2, 0) for t in res)
    (dl, m2, v2, _), = _adamw(
        "adamw_small", [_pack_small(given, "", conv_w[0])], [_pack_small(small_grads, "", dconvw)],
        [_pack_small(given, "m_", m_conv_w[0])], [_pack_small(given, "v_", v_conv_w[0])], SMALL_ROWS, D)
    dl, m2, v2 = (_unpack_small(t, given) for t in (dl, m2, v2))
    for n in SMALL_ORDER + ("conv_w",):
        upd[n] = (dl[n], m2[n], v2[n], small_grads[n])

    return (loss, grad_x[None], *[upd[n][3] for n in WEIGHTS], *[upd[n][0] for n in WEIGHTS],
            *[upd[n][1] for n in WEIGHTS], *[upd[n][2] for n in WEIGHTS])
```

```python
import functools
import typing

import jax
import jax.numpy as jnp
from jax import lax
from jax.experimental import pallas as pl
from jax.experimental.pallas import tpu as pltpu

F32 = jnp.float32
BF16 = jnp.bfloat16

S = 2048
D = 1024
FF = 2816
NSH = 4
FS = FF // NSH
HALF = D // 2
HD = 64
NKV = 4
NQ_PER_KV = 4
KVW = NKV * HD
QCOLS = NQ_PER_KV * HD
CONV_C = 1536
CONV_K = 4
SSM_W = 1024
NST = 128
NCH = S // 128
WIN_COLS = 4112
WIN_SH = WIN_COLS // NSH
WIN_PAD = 4224
COL_DT = 4096
EPS = 1e-6
NEG = -1e30

ADAM_LR = 0.001
ADAM_B1 = 0.9
ADAM_B2 = 0.999
ADAM_EPS = 1e-08
ADAM_WD = 0.01
ADAM_STEP = 10

VMEM_LIMIT = 56 * 1024 * 1024
TS = 512
TR = 256

NN = (((1,), (0,)), ((), ()))
NT = (((1,), (1,)), ((), ()))
TN = (((0,), (0,)), ((), ()))
MESH = pl.DeviceIdType.MESH


def _cparams(*sem):
    return pltpu.CompilerParams(dimension_semantics=sem, vmem_limit_bytes=VMEM_LIMIT)


def _dot(a, b, dims):
    return lax.dot_general(a.astype(BF16), b.astype(BF16), dims, preferred_element_type=F32)


def _bf16_pieces(v):
    hi = v.astype(BF16)
    rest = v - hi.astype(F32)
    mid = rest.astype(BF16)
    return hi, mid, (rest - mid.astype(F32)).astype(BF16)


def _dot_exact(a, b, ones="a"):
    if ones == "a":
        sel = a.astype(BF16)
        parts = [lax.dot_general(sel, p, NN, preferred_element_type=F32) for p in _bf16_pieces(b)]
    else:
        sel = b.astype(BF16)
        parts = [lax.dot_general(p, sel, NN, preferred_element_type=F32) for p in _bf16_pieces(a)]
    return (parts[2] + parts[1]) + parts[0]


def _sigmoid(v):
    return 1.0 / (1.0 + jnp.exp(-v))


class _Rider(typing.NamedTuple):
    operands: list
    out_shapes: list
    aliases: dict
    sems: list
    start: typing.Callable
    finish: typing.Callable


def _call(body, name, grid, in_specs, out_specs, out_shape, operands, scratch=(), sem=(), rider=None):
    multi = isinstance(out_shape, (list, tuple))
    if rider is None:
        return pl.pallas_call(
            body, name=name, grid=grid, in_specs=in_specs, out_specs=out_specs, out_shape=out_shape,
            scratch_shapes=list(scratch), compiler_params=_cparams(*sem))(*operands)
    outs = list(out_shape) if multi else [out_shape]
    ospecs = list(out_specs) if multi else [out_specs]
    n_in, n_out, n_scr = len(operands), len(outs), len(scratch)
    ri, ro = len(rider.operands), len(rider.out_shapes)

    def wrapped(*refs):
        o0 = n_in + ri
        s0 = o0 + n_out + ro
        rin, rout, rsem = refs[n_in:o0], refs[o0 + n_out:s0], refs[s0 + n_scr:]
        ids = [pl.program_id(a) for a in range(len(grid))]
        first = functools.reduce(jnp.logical_and, [i == 0 for i in ids])
        last = functools.reduce(jnp.logical_and, [i == g - 1 for i, g in zip(ids, grid)])

        @pl.when(first)
        def _():
            rider.start(rin, rout, rsem)

        body(*refs[:n_in], *refs[o0:o0 + n_out], *refs[s0:s0 + n_scr])

        @pl.when(last)
        def _():
            rider.finish(rin, rout, rsem)

    hbm = pl.BlockSpec(memory_space=pl.ANY)
    res = pl.pallas_call(
        wrapped, name=name, grid=grid, in_specs=list(in_specs) + [hbm] * ri, out_specs=ospecs + [hbm] * ro,
        out_shape=outs + list(rider.out_shapes), scratch_shapes=list(scratch) + list(rider.sems),
        input_output_aliases={n_in + k: n_out + v for k, v in rider.aliases.items()},
        compiler_params=_cparams(*(("arbitrary",) * len(grid))))(*operands, *rider.operands)
    main = list(res[:n_out])
    return (main if multi else main[0]), list(res[n_out:])


class _Tail(typing.NamedTuple):
    fn: typing.Callable
    operands: list
    in_specs: list


def _mm(name, operands, dims, grid, in_specs, o_spec, out_shape, rider=None, tail=None):
    npairs = len(operands) // 2
    extra = [] if tail is None else list(tail.operands)
    nin = 2 * npairs + len(extra)

    def body(*refs):
        t = None
        for i in range(npairs):
            a, b = refs[2 * i], refs[2 * i + 1]
            parts = [(a[s], b[s]) for s in range(a.shape[0])] if len(a.shape) == 3 else [(a[...], b[...])]
            for pa, pb in parts:
                d = _dot(pa, pb, dims)
                t = d if t is None else t + d
        if tail is None:
            refs[nin][...] = t.astype(refs[nin].dtype)
        else:
            tail.fn(t, refs[2 * npairs:nin], refs[nin:])

    sem = ("parallel" if tail is None else "arbitrary",) * len(grid)
    specs = list(in_specs) + ([] if tail is None else list(tail.in_specs))
    return _call(body, name, grid, specs, o_spec, out_shape, list(operands) + extra, (), sem, rider)


class _FfnW(typing.NamedTuple):
    gu: jax.Array
    g0: int
    dn: jax.Array
    d0: int


def _ffn_up(name, n, w, rider=None):
    def body(n_ref, wg_ref, wu_ref, fg_ref, fu_ref, a_ref):
        nb = n_ref[...]
        g = _dot(nb, wg_ref[...], NT)
        u = _dot(nb, wu_ref[...], NT)
        sg = _sigmoid(g)
        silu = g * sg
        fg_ref[...] = (u * (sg * (1.0 + g * (1.0 - sg)))).astype(BF16)
        fu_ref[...] = silu.astype(BF16)
        a_ref[...] = (silu * u).astype(BF16)

    out = jax.ShapeDtypeStruct((NSH, S, FS), BF16)
    ospec = pl.BlockSpec((None, TS, FS), lambda s, i: (s, i, 0))
    return _call(
        body, name, (NSH, S // TS),
        [pl.BlockSpec((TS, D), lambda s, i: (i, 0)),
         pl.BlockSpec((None, None, FS, D), lambda s, i: (s, w.g0, 0, 0)),
         pl.BlockSpec((None, None, FS, D), lambda s, i: (s, w.g0 + 1, 0, 0))],
        [ospec, ospec, ospec], [out, out, out], (n, w.gu, w.gu), sem=("parallel", "parallel"), rider=rider)


def _ffn_dact(name, dh, w, fgate, fup, rider=None):
    def body(dh_ref, wd_ref, fg_ref, fu_ref, dg_ref, du_ref):
        da = _dot(dh_ref[...], wd_ref[...], NT)
        dg_ref[...] = (da * fg_ref[...].astype(F32)).astype(BF16)
        du_ref[...] = (da * fu_ref[...].astype(F32)).astype(BF16)

    out = jax.ShapeDtypeStruct((NSH, S, FS), BF16)
    aspec = pl.BlockSpec((None, TS, FS), lambda s, i: (s, i, 0))
    return _call(
        body, name, (NSH, S // TS),
        [pl.BlockSpec((TS, D), lambda s, i: (i, 0)),
         pl.BlockSpec((None, None, FS, D), lambda s, i: (s, w.d0, 0, 0)), aspec, aspec],
        [aspec, aspec], [out, out], (dh, w.dn, fgate, fup), sem=("parallel", "parallel"), rider=rider)


def _rstd(v):
    return lax.rsqrt(jnp.mean(v * v, axis=-1, keepdims=True) + EPS)


def _row_spec():
    return pl.BlockSpec((TR, D), lambda i: (i, 0))


def _vec_spec():
    return pl.BlockSpec((1, D), lambda i: (0, 0))


def _acc_rows(ref, v):
    @pl.when(pl.program_id(0) == 0)
    def _():
        ref[...] = jnp.zeros_like(ref)
    ref[...] += jnp.sum(v, axis=0, keepdims=True)


def _prenorm(name, x, g):
    def body(x_ref, g_ref, n_ref):
        xv = x_ref[...]
        n_ref[...] = (xv * _rstd(xv) * g_ref[...]).astype(BF16)

    return pl.pallas_call(
        body, name=name, grid=(S // TR,), in_specs=[_row_spec(), _vec_spec()], out_specs=_row_spec(),
        out_shape=jax.ShapeDtypeStruct((S, D), BF16), compiler_params=_cparams("parallel"),
    )(x, g)


def _rows_spec(rows):
    return pl.BlockSpec((rows, D), lambda i: (i, 0))


def _rows_f32():
    return jax.ShapeDtypeStruct((S, D), F32)


def _rows_bf16():
    return jax.ShapeDtypeStruct((S, D), BF16)


def _vec_f32():
    return jax.ShapeDtypeStruct((1, D), F32)


def _tail_postres(rows, x, p, alpha, gnext):
    def fn(h, ins, outs):
        x_ref, p_ref, g_ref = ins
        h_ref, xo_ref, n_ref = outs
        h_ref[...] = h
        xo = x_ref[...] + alpha * (h * _rstd(h) * p_ref[...])
        xo_ref[...] = xo
        n_ref[...] = (xo * _rstd(xo) * g_ref[...]).astype(BF16)

    rs = _rows_spec(rows)
    return (_Tail(fn, [x, p, gnext], [rs, _vec_spec(), _vec_spec()]), [rs, rs, rs],
            [_rows_f32(), _rows_f32(), _rows_bf16()])


def _tail_final(rows, x, p, tgt, alpha):
    def fn(h, ins, outs):
        x_ref, p_ref, t_ref = ins
        dy_ref, dh_ref, dp_ref, loss_ref = outs
        r = _rstd(h)
        hn = h * r
        pv = p_ref[...]
        e = x_ref[...] + alpha * (hn * pv) - t_ref[...]
        dy = e * (1.0 / D)
        dy_ref[...] = dy
        du = alpha * dy * pv
        dh_ref[...] = (r * (du - hn * jnp.mean(du * hn, axis=-1, keepdims=True))).astype(BF16)
        _acc_rows(dp_ref, alpha * dy * hn)
        part = 0.5 * jnp.sum(jnp.mean(e * e, axis=-1, keepdims=True), axis=0, keepdims=True)
        _acc_rows(loss_ref, jnp.broadcast_to(part, (1, 128)))

    rs = _rows_spec(rows)
    return (_Tail(fn, [x, p, tgt], [rs, _vec_spec(), rs]),
            [rs, rs, _vec_spec(), pl.BlockSpec((1, 128), lambda i: (0, 0))],
            [_rows_f32(), _rows_bf16(), _vec_f32(), jax.ShapeDtypeStruct((1, 128), F32)])


def _norm_bwd(dn, xv, g_ref, dg_ref):
    r = _rstd(xv)
    xn = xv * r
    dng = dn * g_ref[...]
    _acc_rows(dg_ref, dn * xn)
    return r * (dng - xn * jnp.mean(dng * xn, axis=-1, keepdims=True))


def _tail_mid_bwd(rows, dres, x, g, h, p, alpha):
    def fn(dn, ins, outs):
        dr_ref, x_ref, g_ref, h_ref, p_ref = ins
        dx_ref, dh_ref, dg_ref, dp_ref = outs
        dx = dr_ref[...] + _norm_bwd(dn, x_ref[...], g_ref, dg_ref)
        dx_ref[...] = dx
        hv = h_ref[...]
        r = _rstd(hv)
        hn = hv * r
        du = alpha * dx * p_ref[...]
        dh_ref[...] = (r * (du - hn * jnp.mean(du * hn, axis=-1, keepdims=True))).astype(BF16)
        _acc_rows(dp_ref, alpha * dx * hn)

    rs = _rows_spec(rows)
    return (_Tail(fn, [dres, x, g, h, p], [rs, rs, _vec_spec(), rs, _vec_spec()]),
            [rs, rs, _vec_spec(), _vec_spec()], [_rows_f32(), _rows_bf16(), _vec_f32(), _vec_f32()])


def _tail_first_bwd(rows, dres, x, g):
    def fn(dn, ins, outs):
        dr_ref, x_ref, g_ref = ins
        dx_ref, dg_ref = outs
        dx_ref[...] = dr_ref[...] + _norm_bwd(dn, x_ref[...], g_ref, dg_ref)

    rs = _rows_spec(rows)
    return (_Tail(fn, [dres, x, g], [rs, rs, _vec_spec()]), [rs, _vec_spec()], [_rows_f32(), _vec_f32()])


def _rotate(t, c128, s128, sign, scale):
    width = t.shape[1]
    c = jnp.tile(c128, (1, width // 128))
    sn = jnp.tile(s128, (1, width // 128))
    lane = lax.broadcasted_iota(jnp.int32, t.shape, 1) & (HD - 1)
    rot = jnp.where(lane < HD // 2, -pltpu.roll(t, width - HD // 2, 1), pltpu.roll(t, HD // 2, 1))
    return (t * c + sign * (rot * sn)) * scale


def _rows_to_blocks(y):
    out = []
    for j in range(NKV):
        yt = y[:, QCOLS * j:QCOLS * (j + 1)].T
        out.append(jnp.concatenate([yt[HD * g:HD * (g + 1)] for g in range(NQ_PER_KV)], axis=1))
    return out


def _blocks_to_rows(blocks):
    cols = []
    for b in blocks:
        stacked = jnp.concatenate([b[:, 128 * g:128 * (g + 1)] for g in range(NQ_PER_KV)], axis=0)
        cols.append(stacked.T)
    return jnp.concatenate(cols, axis=1)


def _rope_q(proj, cos, sin):
    def body(t_ref, c_ref, s_ref, o_ref):
        y = _rotate(t_ref[...], c_ref[...], s_ref[...], 1.0, HD ** -0.5)
        for j, blk in enumerate(_rows_to_blocks(y)):
            o_ref[j] = blk.astype(BF16)

    return pl.pallas_call(
        body, name="rope_q", grid=(NCH,),
        in_specs=[pl.BlockSpec((128, D), lambda i: (i, 0)),
                  pl.BlockSpec((128, 128), lambda i: (i, 0)), pl.BlockSpec((128, 128), lambda i: (i, 0))],
        out_specs=pl.BlockSpec((NKV, None, HD, QROWS), lambda i: (0, i, 0, 0)),
        out_shape=jax.ShapeDtypeStruct((NKV, NCH, HD, QROWS), BF16), compiler_params=_cparams("parallel"),
    )(proj, cos, sin)


def _rope_dq(dqt, cos, sin, dproj):
    def body(t_ref, c_ref, s_ref, buf_ref, o_ref):
        t = _blocks_to_rows([t_ref[j] for j in range(NKV)])
        o_ref[...] = _rotate(t, c_ref[...], s_ref[...], -1.0, HD ** -0.5).astype(BF16)

    return pl.pallas_call(
        body, name="rope_dq", grid=(NCH,),
        in_specs=[pl.BlockSpec((NKV, None, HD, QROWS), lambda i: (0, i, 0, 0)),
                  pl.BlockSpec((128, 128), lambda i: (i, 0)), pl.BlockSpec((128, 128), lambda i: (i, 0)),
                  pl.BlockSpec(memory_space=pl.ANY)],
        out_specs=pl.BlockSpec((128, D), lambda i: (i, 0)),
        out_shape=jax.ShapeDtypeStruct(dproj.shape, BF16), input_output_aliases={3: 0},
        compiler_params=_cparams("parallel"),
    )(dqt, cos, sin, dproj)


def _rope_dkv(dkt, dvt, cos, sin, dproj):
    def body(k_ref, v_ref, c_ref, s_ref, buf_ref, o_ref):
        dk = jnp.concatenate([k_ref[j] for j in range(NKV)], axis=0).T
        dv = jnp.concatenate([v_ref[j] for j in range(NKV)], axis=0).T
        dk = _rotate(dk, c_ref[...], s_ref[...], -1.0, 1.0)
        o_ref[...] = jnp.concatenate([dk, dv], axis=1).astype(BF16)

    tspec = pl.BlockSpec((NKV, HD, 128), lambda i: (0, 0, i))
    return pl.pallas_call(
        body, name="rope_dkv", grid=(NCH,),
        in_specs=[tspec, tspec, pl.BlockSpec((128, 128), lambda i: (i, 0)), pl.BlockSpec((128, 128), lambda i: (i, 0)),
                  pl.BlockSpec(memory_space=pl.ANY)],
        out_specs=pl.BlockSpec((128, 2 * KVW), lambda i: (i, D // (2 * KVW))),
        out_shape=jax.ShapeDtypeStruct(dproj.shape, BF16), input_output_aliases={4: 0},
        compiler_params=_cparams("parallel"),
    )(dkt, dvt, cos, sin, dproj)


def _rope_kv(proj, cos, sin):
    def body(t_ref, c_ref, s_ref, k_ref, v_ref, kt_ref, vt_ref):
        t = t_ref[...]
        k = _rotate(t[:, :KVW], c_ref[...], s_ref[...], 1.0, 1.0).astype(BF16)
        v = t[:, KVW:].astype(BF16)
        k_ref[...] = k
        v_ref[...] = v
        kt, vt = k.astype(F32).T, v.astype(F32).T
        for j in range(NKV):
            kt_ref[j] = kt[HD * j:HD * (j + 1)].astype(BF16)
            vt_ref[j] = vt[HD * j:HD * (j + 1)].astype(BF16)

    rows = pl.BlockSpec((128, KVW), lambda i: (i, 0))
    tspec = pl.BlockSpec((NKV, HD, 128), lambda i: (0, 0, i))
    return pl.pallas_call(
        body, name="rope_kv", grid=(NCH,),
        in_specs=[pl.BlockSpec((128, 2 * KVW), lambda i: (i, D // (2 * KVW))),
                  pl.BlockSpec((128, 128), lambda i: (i, 0)), pl.BlockSpec((128, 128), lambda i: (i, 0))],
        out_specs=[rows, rows, tspec, tspec],
        out_shape=[jax.ShapeDtypeStruct((S, KVW), BF16)] * 2 + [jax.ShapeDtypeStruct((NKV, HD, S), BF16)] * 2,
        compiler_params=_cparams("parallel"),
    )(proj, cos, sin)


QROWS = NQ_PER_KV * 128


NBIAS = NCH + 1
KV_PER_STEP = 4


def _bias_table():
    db = lax.broadcasted_iota(jnp.int32, (NBIAS, 128, QROWS), 0) - 1
    ki = lax.broadcasted_iota(jnp.int32, (NBIAS, 128, QROWS), 1)
    qi = lax.broadcasted_iota(jnp.int32, (NBIAS, 128, QROWS), 2) & 127
    d = db * 128 + qi - ki
    cnt = ((d <= 128).astype(F32) + (((d & 3) == 0) & (d <= 512)).astype(F32) + ((d & 15) == 0).astype(F32))
    return jnp.where((d >= 0) & (cnt > 0.0), jnp.log(jnp.maximum(cnt, 1.0)), NEG)


def _qt_spec():
    return pl.BlockSpec((None, None, HD, QROWS), lambda j, i: (j, i, 0, 0))


def _stat_spec():
    return pl.BlockSpec((None, None, 1, QROWS), lambda j, i: (j, i, 0, 0))


def _attn_fwd(qt, kh, vt, bias, rider=None):
    def body(q_ref, k_ref, v_ref, b_ref, o_ref, lse_ref, rows_ref, m_ref, l_ref, acc_ref):
        qb = pl.program_id(1)
        m_ref[...] = jnp.full_like(m_ref, NEG)
        l_ref[...] = jnp.zeros_like(l_ref)
        acc_ref[...] = jnp.zeros_like(acc_ref)

        def keys(off, size, bias_):
            for h in range(KV_PER_STEP):
                m = m_ref[h]
                s = _dot(k_ref[h, pl.ds(off, size), :], q_ref[h], NN) + bias_
                m_new = jnp.maximum(m, jnp.max(s, axis=0, keepdims=True))
                p = jnp.exp(s - m_new)
                a = jnp.exp(m - m_new)
                m_ref[h] = m_new
                l_ref[h] = a * l_ref[h] + jnp.sum(p, axis=0, keepdims=True)
                acc_ref[h] = a * acc_ref[h] + _dot(v_ref[h, :, pl.ds(off, size)], p, NN)

        def blocks(first, count):
            bias_ = jnp.concatenate([b_ref[qb - first - j + 1] for j in range(count)], axis=0)
            keys(pl.multiple_of(first * 128, 128), 128 * count, bias_)

        nkb = qb + 1
        @pl.loop(0, nkb // 4)
        def _(i):
            blocks(4 * i, 4)

        @pl.when(nkb % 4 >= 2)
        def _():
            blocks(nkb // 4 * 4, 2)

        @pl.when(nkb % 2 == 1)
        def _():
            blocks(qb, 1)

        outs = []
        for h in range(KV_PER_STEP):
            outs.append(acc_ref[h] / l_ref[h])
            o_ref[h] = outs[h]
            lse_ref[h] = m_ref[h] + jnp.log(l_ref[h])
        rows_ref[...] = _blocks_to_rows(outs).astype(BF16)

    kvs = KV_PER_STEP
    qspec = pl.BlockSpec((kvs, None, HD, QROWS), lambda j, i: (j, i, 0, 0))
    return _call(
        body, "attn_fwd", (NKV // kvs, NCH),
        [qspec, pl.BlockSpec((kvs, S, HD), lambda j, i: (j, 0, 0)),
         pl.BlockSpec((kvs, HD, S), lambda j, i: (j, 0, 0)),
         pl.BlockSpec((NBIAS, 128, QROWS), lambda j, i: (0, 0, 0))],
        [qspec, pl.BlockSpec((kvs, None, 1, QROWS), lambda j, i: (j, i, 0, 0)),
         pl.BlockSpec((128, QCOLS * kvs), lambda j, i: (i, j))],
        [jax.ShapeDtypeStruct((NKV, NCH, HD, QROWS), F32), jax.ShapeDtypeStruct((NKV, NCH, 1, QROWS), F32),
         jax.ShapeDtypeStruct((S, D), BF16)],
        (qt, kh, vt, bias),
        [pltpu.VMEM((kvs, 1, QROWS), F32), pltpu.VMEM((kvs, 1, QROWS), F32), pltpu.VMEM((kvs, HD, QROWS), F32)],
        ("parallel", "parallel"), rider)


def _attn_delta(ot, dot_):
    def body(o_ref, do_ref, dl_ref):
        dl_ref[...] = jnp.sum(o_ref[...] * do_ref[...].astype(F32), axis=1, keepdims=True)

    spec = pl.BlockSpec((None, NCH, HD, QROWS), lambda j: (j, 0, 0, 0))
    return pl.pallas_call(
        body, name="attn_delta", grid=(NKV,), in_specs=[spec, spec],
        out_specs=pl.BlockSpec((None, NCH, 1, QROWS), lambda j: (j, 0, 0, 0)),
        out_shape=jax.ShapeDtypeStruct((NKV, NCH, 1, QROWS), F32), compiler_params=_cparams("parallel"),
    )(ot, dot_)


def _attn_bwd(qt, kh, kt, vh, dot_, lse, delta, bias, rider=None):
    def body(qt_ref, k_ref, kt_ref, v_ref, dot_ref, lse_ref, dl_ref, b_ref, dq_ref, dk_ref, dv_ref):
        kp = pl.program_id(1)

        @pl.when(kp == 0)
        def _():
            dq_ref[...] = jnp.zeros_like(dq_ref)

        dk_ref[...] = jnp.zeros_like(dk_ref)
        dv_ref[...] = jnp.zeros_like(dv_ref)

        @pl.loop(2 * kp, NCH // 2)
        def _(j):
            for h in range(KV_PER_STEP):
                k, kt_, v = k_ref[h], kt_ref[h], v_ref[h]
                for qb in (2 * j, 2 * j + 1):
                    bias2 = jnp.concatenate([b_ref[jnp.maximum(qb - 4 * kp - t + 1, 0)] for t in range(4)], axis=0)
                    st = _dot(k, qt_ref[h, qb], NN) + bias2
                    pt = jnp.exp(st - lse_ref[h, qb])
                    dst = pt * (_dot(v, dot_ref[h, qb], NN) - dl_ref[h, qb])
                    dq_ref[h, qb] += _dot(kt_, dst, NN)
                    dk_ref[h] += _dot(qt_ref[h, qb], dst, NT)
                    dv_ref[h] += _dot(dot_ref[h, qb], pt, NT)

    kvs = KV_PER_STEP
    tspec = pl.BlockSpec((kvs, NCH, HD, QROWS), lambda j, i: (j, 0, 0, 0))
    kspec = pl.BlockSpec((kvs, 512, HD), lambda j, i: (j, i, 0))
    ktspec = pl.BlockSpec((kvs, HD, 512), lambda j, i: (j, 0, i))
    sspec = pl.BlockSpec((kvs, NCH, 1, QROWS), lambda j, i: (j, 0, 0, 0))
    return _call(
        body, "attn_bwd", (NKV // kvs, NCH // 4),
        [tspec, kspec, ktspec, kspec, tspec, sspec, sspec,
         pl.BlockSpec((NBIAS, 128, QROWS), lambda j, i: (0, 0, 0))],
        [tspec, ktspec, ktspec],
        [jax.ShapeDtypeStruct((NKV, NCH, HD, QROWS), F32),
         jax.ShapeDtypeStruct((NKV, HD, S), F32), jax.ShapeDtypeStruct((NKV, HD, S), F32)],
        (qt, kh, kt, vh, dot_, lse, delta, bias), sem=("parallel", "arbitrary"), rider=rider)


CONV_BLK = 256
CONV_COL0 = 1536 // CONV_BLK


def _shift_down(u, j, row):
    return jnp.where(row >= j, pltpu.roll(u, j, 0), 0.0)


def _conv_pre(u, w_ref, b_ref, row):
    y = b_ref[...] + w_ref[CONV_K - 1:CONV_K, :] * u
    for j in range(1, CONV_K):
        y = y + w_ref[CONV_K - 1 - j:CONV_K - j, :] * _shift_down(u, j, row)
    return y


def _conv_fwd(proj, convw, convb):
    def body(u_ref, w_ref, b_ref, o_ref):
        u = u_ref[...]
        row = lax.broadcasted_iota(jnp.int32, u.shape, 0)
        y = _conv_pre(u, w_ref, b_ref, row)
        o_ref[...] = y * _sigmoid(y)

    return pl.pallas_call(
        body, name="conv_fwd", grid=(CONV_C // CONV_BLK,),
        in_specs=[pl.BlockSpec((S, CONV_BLK), lambda i: (0, CONV_COL0 + i)),
                  pl.BlockSpec((CONV_K, CONV_BLK), lambda i: (0, i)),
                  pl.BlockSpec((1, CONV_BLK), lambda i: (0, i))],
        out_specs=pl.BlockSpec((S, CONV_BLK), lambda i: (0, i)),
        out_shape=jax.ShapeDtypeStruct((S, CONV_C), F32), compiler_params=_cparams("parallel"),
    )(proj, convw, convb)


def _conv_bwd(dact, proj, convw, convb, dproj):
    def body(da_ref, u_ref, w_ref, b_ref, buf_ref, du_ref, dw_ref, db_ref):
        u = u_ref[...]
        row = lax.broadcasted_iota(jnp.int32, u.shape, 0)
        y = _conv_pre(u, w_ref, b_ref, row)
        sg = _sigmoid(y)
        dy = da_ref[...] * (sg * (1.0 + y * (1.0 - sg)))
        db_ref[...] = jnp.sum(dy, axis=0, keepdims=True)
        du = w_ref[CONV_K - 1:CONV_K, :] * dy
        r8 = lax.broadcasted_iota(jnp.int32, (8, CONV_BLK), 0)
        dw = jnp.where(r8 == CONV_K - 1, jnp.sum(dy * u, axis=0, keepdims=True), 0.0)
        for j in range(1, CONV_K):
            du = du + w_ref[CONV_K - 1 - j:CONV_K - j, :] * jnp.where(row < S - j, pltpu.roll(dy, S - j, 0), 0.0)
            dw = dw + jnp.where(r8 == CONV_K - 1 - j,
                                jnp.sum(dy * _shift_down(u, j, row), axis=0, keepdims=True), 0.0)
        du_ref[...] = du.astype(BF16)
        dw_ref[...] = dw

    return pl.pallas_call(
        body, name="conv_bwd", grid=(CONV_C // CONV_BLK,),
        in_specs=[pl.BlockSpec((S, CONV_BLK), lambda i: (0, i)),
                  pl.BlockSpec((S, CONV_BLK), lambda i: (0, CONV_COL0 + i)),
                  pl.BlockSpec((CONV_K, CONV_BLK), lambda i: (0, i)),
                  pl.BlockSpec((1, CONV_BLK), lambda i: (0, i)), pl.BlockSpec(memory_space=pl.ANY)],
        out_specs=[pl.BlockSpec((S, CONV_BLK), lambda i: (0, CONV_COL0 + i)),
                   pl.BlockSpec((8, CONV_BLK), lambda i: (0, i)), pl.BlockSpec((1, CONV_BLK), lambda i: (0, i))],
        out_shape=[jax.ShapeDtypeStruct(dproj.shape, BF16), jax.ShapeDtypeStruct((8, CONV_C), F32),
                   jax.ShapeDtypeStruct((1, CONV_C), F32)],
        input_output_aliases={4: 0}, compiler_params=_cparams("parallel"),
    )(dact, proj, convw, convb, dproj)


NPAIR = 8


def _ssd_scalars(dtr_ref, dtb_ref, alog_ref):
    z = dtr_ref[...] + dtb_ref[...]
    dt = jnp.maximum(z, 0.0) + jnp.log(1.0 + jnp.exp(-jnp.abs(z)))
    a = -jnp.exp(alog_ref[...])
    r = lax.broadcasted_iota(jnp.int32, (128, 128), 0)
    c = lax.broadcasted_iota(jnp.int32, (128, 128), 1)
    tri = (r >= c).astype(F32)
    cs = _dot_exact(tri, dt * a)
    return z, dt, a, cs, r, c


def _by_lane(cs, dt):
    head = lax.broadcasted_iota(jnp.int32, (128, SSM_W), 0)
    lane = lax.broadcasted_iota(jnp.int32, (128, SSM_W), 1)
    sel = (head == lane // HD).astype(F32)
    cs_l = _dot_exact(cs, sel, "b")
    last_l = cs_l[127:128, :]
    return sel, jnp.exp(cs_l), jnp.exp(last_l - cs_l), _dot_exact(dt, sel, "b")


def _pair_terms(cs, h1, h2):
    return (cs[:, h1:h1 + 1], cs[:, h2:h2 + 1],
            jnp.exp(cs[127:128, h1:h1 + 1]), jnp.exp(cs[127:128, h2:h2 + 1]))


def _gate_norm(y, zv, w):
    yg = y * (zv * _sigmoid(zv))
    outs, rs = [], []
    for g in range(2):
        blk = yg[:, 512 * g:512 * (g + 1)]
        r = lax.rsqrt(jnp.mean(blk * blk, axis=-1, keepdims=True) + EPS)
        outs.append(blk * r)
        rs.append(r)
    return jnp.concatenate(outs, axis=1), rs, yg


def _ssd_fwd(xbc, proj, dtb, alog, dskip_l, ssmw):
    def body(x_ref, b_ref, c_ref, dtr_ref, z_ref, dtb_ref, alog_ref, dsk_ref, w_ref, y_ref, yn_ref, hp_ref, h_ref):
        @pl.when(pl.program_id(0) == 0)
        def _():
            h_ref[...] = jnp.zeros_like(h_ref)

        _, dt, _, cs, r, c = _ssd_scalars(dtr_ref, dtb_ref, alog_ref)
        cst = cs.T
        causal = r >= c
        lo = c < HD
        _, e_all, dte_all, dt_all = _by_lane(cs, dt)
        hp_ref[...] = h_ref[...]
        for g in range(2):
            bg = b_ref[:, 128 * g:128 * (g + 1)]
            cg = c_ref[:, 128 * g:128 * (g + 1)]
            cb = _dot(cg, bg, NT)
            for j in range(4):
                pj = 4 * g + j
                h1, h2 = 2 * pj, 2 * pj + 1
                sl = slice(128 * pj, 128 * (pj + 1))
                xp = x_ref[:, sl]
                c1, c2, cd1, cd2 = _pair_terms(cs, h1, h2)
                e_l, dte_l = e_all[:, sl], dte_all[:, sl]
                xdt = xp * dt_all[:, sl]
                m1 = cb * jnp.exp(jnp.where(causal, c1 - cst[h1:h1 + 1, :], NEG))
                m2 = cb * jnp.exp(jnp.where(causal, c2 - cst[h2:h2 + 1, :], NEG))
                yd = jnp.where(lo, _dot(m1, xdt, NN), _dot(m2, xdt, NN))
                hp = h_ref[pj]
                yo = _dot(cg, hp, NT) * e_l
                st = _dot(xdt * dte_l, bg, TN)
                h_ref[pj] = hp * jnp.where(r < HD, cd1, cd2) + st
                y_ref[:, sl] = yd + yo + dsk_ref[:, sl] * xp
        yn, _, _ = _gate_norm(y_ref[...], z_ref[...], w_ref[...])
        yn_ref[...] = (yn * w_ref[...]).astype(BF16)

    return pl.pallas_call(
        body, name="ssd_fwd", grid=(NCH,),
        in_specs=[pl.BlockSpec((128, SSM_W), lambda i: (i, 0)),
                  pl.BlockSpec((128, 256), lambda i: (i, 4)), pl.BlockSpec((128, 256), lambda i: (i, 5)),
                  pl.BlockSpec((128, 128), lambda i: (i, COL_DT // 128)),
                  pl.BlockSpec((128, SSM_W), lambda i: (i, 3)),
                  pl.BlockSpec((1, 128), lambda i: (0, 0)), pl.BlockSpec((1, 128), lambda i: (0, 0)),
                  pl.BlockSpec((1, SSM_W), lambda i: (0, 0)), pl.BlockSpec((1, SSM_W), lambda i: (0, 0))],
        out_specs=[pl.BlockSpec((128, SSM_W), lambda i: (i, 0)), pl.BlockSpec((128, SSM_W), lambda i: (i, 0)),
                   pl.BlockSpec((None, NPAIR, 128, 128), lambda i: (i, 0, 0, 0))],
        out_shape=[jax.ShapeDtypeStruct((S, SSM_W), F32), jax.ShapeDtypeStruct((S, SSM_W), BF16),
                   jax.ShapeDtypeStruct((NCH, NPAIR, 128, 128), F32)],
        scratch_shapes=[pltpu.VMEM((NPAIR, 128, 128), F32)],
        compiler_params=_cparams("arbitrary"),
    )(xbc, xbc, xbc, proj, proj, dtb, alog, dskip_l, ssmw)


def _ssd_bwd(dmixed, y, xbc, proj, hprev, dtb, alog, dskip_l, ssmw, rider=None):
    def body(dyn_ref, y_ref, x_ref, b_ref, c_ref, dtr_ref, z_ref, hp_ref, dtb_ref, alog_ref, dsk_ref, w_ref,
             dxbc_ref, dz_ref, ddt_ref, dw_ref, dsc_ref, g_ref):
        @pl.when(pl.program_id(0) == 0)
        def _():
            g_ref[...] = jnp.zeros_like(g_ref)
            dsc_ref[...] = jnp.zeros_like(dsc_ref)

        z, dt, a, cs, r, c = _ssd_scalars(dtr_ref, dtb_ref, alog_ref)
        cst = cs.T
        causal = r >= c
        lo = c < HD

        yv = y_ref[...]
        zv = z_ref[...]
        wv = w_ref[...]
        ygn, rs, yg = _gate_norm(yv, zv, wv)
        dyn = dyn_ref[...]
        _acc_rows(dw_ref, dyn * ygn)
        dynw = dyn * wv
        parts = []
        for g in range(2):
            sl = slice(512 * g, 512 * (g + 1))
            a_g, n_g = dynw[:, sl], ygn[:, sl]
            parts.append(rs[g] * (a_g - n_g * jnp.mean(a_g * n_g, axis=-1, keepdims=True)))
        dyg = jnp.concatenate(parts, axis=1)
        sz = _sigmoid(zv)
        dz_ref[...] = (dyg * yv * (sz * (1.0 + zv * (1.0 - sz)))).astype(BF16)
        dy_all = dyg * (zv * sz)

        dcs_cols = jnp.zeros((128, 128), F32)
        dcs_rows = jnp.zeros((128, 128), F32)
        sel, e_all, dte_all, dt_all = _by_lane(cs, dt)
        x_all, b_all, c_all, dsk_all = x_ref[...], b_ref[...], c_ref[...], dsk_ref[...]
        hp_all, g_all = hp_ref[...], g_ref[...]
        g_new, dx_parts, db_parts, dc_parts = [], [], [], []
        dyx_parts, ryo_parts, qx_parts, dxx_parts, gh_parts = [], [], [], [], []
        for g in range(2):
            bg = b_all[:, 128 * g:128 * (g + 1)]
            cg = c_all[:, 128 * g:128 * (g + 1)]
            cb = _dot(cg, bg, NT)
            dcb = jnp.zeros((128, 128), F32)
            db_acc = jnp.zeros((128, NST), F32)
            dc_acc = jnp.zeros((128, NST), F32)
            for j in range(4):
                pj = 4 * g + j
                h1, h2 = 2 * pj, 2 * pj + 1
                sl = slice(128 * pj, 128 * (pj + 1))
                xp = x_all[:, sl]
                dyp = dy_all[:, sl]
                c1, c2, cd1, cd2 = _pair_terms(cs, h1, h2)
                e_l, dte_l, dt_l = e_all[:, sl], dte_all[:, sl], dt_all[:, sl]
                xdt = xp * dt_l
                hp = hp_all[pj]
                gp = g_all[pj]
                dyx_parts.append(dyp * xp)
                dzs = dyp * e_l
                dc_acc = dc_acc + _dot(dzs, hp, NN)
                ryo_parts.append(dyp * (_dot(cg, hp, NT) * e_l))
                qm = _dot(bg, gp, NT)
                dxdt = qm * dte_l
                qx_parts.append(qm * xdt)
                db_acc = db_acc + _dot(xdt * dte_l, gp, NN)
                gh_parts.append(gp * hp)
                g_new.append(_dot(dzs, cg, TN) + jnp.where(r < HD, cd1, cd2) * gp)
                for hh, ch, msk in ((h1, c1, lo), (h2, c2, jnp.logical_not(lo))):
                    lm = jnp.exp(jnp.where(causal, ch - cst[hh:hh + 1, :], NEG))
                    mm = cb * lm
                    dm = jnp.where(causal, _dot(jnp.where(msk, dyp, 0.0), xdt, NT), 0.0)
                    w = dm * mm
                    dcs_cols = dcs_cols + jnp.where(c == hh, jnp.sum(w, axis=1, keepdims=True), 0.0)
                    dcs_rows = dcs_rows + jnp.where(r == hh, jnp.sum(w, axis=0, keepdims=True), 0.0)
                    dcb = dcb + dm * lm
                    dxdt = dxdt + jnp.where(msk, _dot(mm, dyp, TN), 0.0)
                dxx_parts.append(dxdt * xp)
                dx_parts.append(dsk_all[:, sl] * dyp + dxdt * dt_l)
            db_parts.append(db_acc + _dot(dcb, cg, TN))
            dc_parts.append(dc_acc + _dot(dcb, bg, NN))
        g_ref[...] = jnp.stack(g_new)
        dxbc_ref[...] = jnp.concatenate(dx_parts + db_parts + dc_parts, axis=1)

        selt = (lax.broadcasted_iota(jnp.int32, (SSM_W, 128), 0) // HD
                == lax.broadcasted_iota(jnp.int32, (SSM_W, 128), 1)).astype(F32)

        def by_head(parts):
            return _dot_exact(jnp.concatenate(parts, axis=1), selt, "b")

        ddt_x = by_head(dxx_parts)
        dd_row = jnp.sum(by_head(dyx_parts), axis=0, keepdims=True)
        t_all = by_head(qx_parts) * jnp.exp(cs[127:128, :] - cs)
        gh = jnp.sum(_dot_exact(sel, jnp.concatenate(gh_parts, axis=0)), axis=1, keepdims=True)
        gh_row = jnp.broadcast_to(gh, (128, 128)).T[0:1, :]
        at_end = jnp.sum(t_all, axis=0, keepdims=True) + gh_row * jnp.exp(cs[127:128, :])
        dcs = by_head(ryo_parts) - t_all + dcs_cols + jnp.where(r == 127, at_end, 0.0) - dcs_rows.T
        dad = _dot_exact((c >= r).astype(F32), dcs)
        ddt = dad * a + ddt_x
        ddtr = jnp.where(c < 16, ddt * _sigmoid(z), 0.0)
        ddt_ref[...] = ddtr.astype(BF16)
        r8 = lax.broadcasted_iota(jnp.int32, (8, 128), 0)
        dsc_ref[...] += (jnp.where(r8 == 0, jnp.sum(ddtr, axis=0, keepdims=True), 0.0)
                         + jnp.where(r8 == 1, jnp.sum(dad * dt, axis=0, keepdims=True) * a, 0.0)
                         + jnp.where(r8 == 2, dd_row, 0.0))

    rev = NCH - 1
    return _call(
        body, "ssd_bwd", (NCH,),
        [pl.BlockSpec((128, SSM_W), lambda i: (rev - i, 0)),
         pl.BlockSpec((128, SSM_W), lambda i: (rev - i, 0)),
         pl.BlockSpec((128, SSM_W), lambda i: (rev - i, 0)),
         pl.BlockSpec((128, 256), lambda i: (rev - i, 4)), pl.BlockSpec((128, 256), lambda i: (rev - i, 5)),
         pl.BlockSpec((128, 128), lambda i: (rev - i, COL_DT // 128)),
         pl.BlockSpec((128, SSM_W), lambda i: (rev - i, 3)),
         pl.BlockSpec((None, NPAIR, 128, 128), lambda i: (rev - i, 0, 0, 0)),
         pl.BlockSpec((1, 128), lambda i: (0, 0)), pl.BlockSpec((1, 128), lambda i: (0, 0)),
         pl.BlockSpec((1, SSM_W), lambda i: (0, 0)), pl.BlockSpec((1, SSM_W), lambda i: (0, 0))],
        [pl.BlockSpec((128, CONV_C), lambda i: (rev - i, 0)),
         pl.BlockSpec((128, SSM_W), lambda i: (rev - i, 3)),
         pl.BlockSpec((128, 128), lambda i: (rev - i, 0)),
         pl.BlockSpec((1, SSM_W), lambda i: (0, 0)), pl.BlockSpec((8, 128), lambda i: (0, 0))],
        [jax.ShapeDtypeStruct((S, CONV_C), F32), jax.ShapeDtypeStruct((S, WIN_PAD), BF16),
         jax.ShapeDtypeStruct((S, 128), BF16), jax.ShapeDtypeStruct((1, SSM_W), F32),
         jax.ShapeDtypeStruct((8, 128), F32)],
        (dmixed, y, xbc, xbc, xbc, proj, proj, hprev, dtb, alog, dskip_l, ssmw),
        [pltpu.VMEM((NPAIR, 128, 128), F32)], ("arbitrary",), rider)


def _cast_stack(name, slot, arrs, tr, tc):
    n = len(arrs)
    rows, cols = arrs[0].shape

    def body(s_ref, *refs):
        for i in range(n):
            refs[n][i] = refs[i][...].astype(BF16)

    return pl.pallas_call(
        body, name=name,
        grid_spec=pltpu.PrefetchScalarGridSpec(
            num_scalar_prefetch=1, grid=(rows // tr, cols // tc),
            in_specs=[pl.BlockSpec((tr, tc), lambda i, j, sr: (i, j))] * n,
            out_specs=pl.BlockSpec((None, n, tr, tc), lambda i, j, sr: (sr[0], 0, i, j))),
        out_shape=jax.ShapeDtypeStruct((NSH, n, rows, cols), BF16),
        compiler_params=_cparams("parallel", "parallel"),
    )(slot, *arrs)


def _pair_sum(name, c_idx, ps, th):
    n = len(ps)
    _, rows, _ = ps[0].shape

    def body(c_ref, *refs):
        mine, whole, out, theirs = refs[:n], refs[n:2 * n], refs[2 * n:3 * n], refs[3 * n:4 * n]
        send, recv = refs[4 * n], refs[4 * n + 1]
        s, i = pl.program_id(0), pl.program_id(1)
        x, y, c, _ = _place()

        def copies(slot):
            return [_rcopy(whole[k].at[slot, :, pl.ds((1 - c) * HALF, HALF)], theirs[k].at[slot],
                           send.at[slot * n + k], recv.at[slot * n + k], (x, y, 1 - c)) for k in range(n)]

        @pl.when((s == 0) & (i == 0))
        def _():
            for slot in range(NSH):
                for cp in copies(slot):
                    cp.start()

        @pl.when(i == 0)
        def _():
            for slot in range(NSH):
                @pl.when(s == slot)
                def _():
                    for cp in copies(slot):
                        cp.wait()

        rows_i = slice(None) if th == rows else pl.ds(pl.multiple_of(i * th, th), th)
        for k in range(n):
            out[k][...] = (mine[k][...].astype(F32) + theirs[k][s, rows_i, :].astype(F32)).astype(BF16)

    spec = pl.BlockSpec((None, th, HALF), lambda s, i, cr: (s, i, 0))
    return pl.pallas_call(
        body, name=name,
        grid_spec=pltpu.PrefetchScalarGridSpec(
            num_scalar_prefetch=1, grid=(NSH, rows // th),
            in_specs=[pl.BlockSpec((None, th, HALF), lambda s, i, cr: (s, i, cr[0]))] * n + _any_specs(n),
            out_specs=[spec] * n,
            scratch_shapes=[pltpu.VMEM((NSH, rows, HALF), BF16)] * n
            + [pltpu.SemaphoreType.DMA((NSH * n,)), pltpu.SemaphoreType.DMA((NSH * n,))]),
        out_shape=[jax.ShapeDtypeStruct((NSH, rows, HALF), BF16)] * n,
        compiler_params=_cparams("arbitrary", "arbitrary"),
    )(c_idx, *ps, *ps)


def _pair_add(name, c_idx, ps, theirs, th):
    n = len(ps)
    _, rows, _ = ps[0].shape

    def body(c_ref, *refs):
        for k in range(n):
            refs[2 * n + k][...] = (refs[k][...].astype(F32) + refs[n + k][...].astype(F32)).astype(BF16)

    spec = pl.BlockSpec((None, th, HALF), lambda s, i, cr: (s, i, 0))
    return pl.pallas_call(
        body, name=name,
        grid_spec=pltpu.PrefetchScalarGridSpec(
            num_scalar_prefetch=1, grid=(NSH, rows // th),
            in_specs=[pl.BlockSpec((None, th, HALF), lambda s, i, cr: (s, i, cr[0]))] * n + [spec] * n,
            out_specs=[spec] * n),
        out_shape=[jax.ShapeDtypeStruct((NSH, rows, HALF), BF16)] * n,
        compiler_params=_cparams("parallel", "parallel"),
    )(c_idx, *ps, *theirs)


def _chip_sum(name, place, cs, ts, th):
    n = len(ts)
    _, rows, _ = ts[0].shape

    def body(p_ref, *refs):
        for i in range(n):
            t = refs[n + i][...].astype(F32)
            refs[2 * n + i][...] = ((refs[i][...].astype(F32) + t[0]) + t[1]) + t[2]

    return pl.pallas_call(
        body, name=name,
        grid_spec=pltpu.PrefetchScalarGridSpec(
            num_scalar_prefetch=1, grid=(rows // th,),
            in_specs=[pl.BlockSpec((None, th, HALF), lambda i, pr: (pr[0], i, 0))] * n
            + [pl.BlockSpec((3, th, HALF), lambda i, pr: (0, i, 0))] * n,
            out_specs=[pl.BlockSpec((th, HALF), lambda i, pr: (i, pr[1]))] * n),
        out_shape=[jax.ShapeDtypeStruct((rows, D), F32)] * n, compiler_params=_cparams("parallel"),
    )(place, *cs, *ts)


def _adamw(name, ws, gs, ms, vs, tr, tc):
    n = len(ws)
    shape = ws[0].shape
    rows, cols, mid = shape[0], shape[-1], shape[1:-1]
    c1 = 1.0 / (1.0 - ADAM_B1 ** ADAM_STEP)
    c2 = 1.0 / (1.0 - ADAM_B2 ** ADAM_STEP)

    def body(*refs):
        for i in range(n):
            w, g, m, v = (refs[k * n + i][...] for k in range(4))
            m2 = ADAM_B1 * m + (1.0 - ADAM_B1) * g
            v2 = ADAM_B2 * v + (1.0 - ADAM_B2) * (g * g)
            refs[4 * n + 4 * i][...] = -ADAM_LR * ((m2 * c1) / (jnp.sqrt(v2 * c2) + ADAM_EPS) + ADAM_WD * w)
            refs[4 * n + 4 * i + 1][...] = m2
            refs[4 * n + 4 * i + 2][...] = v2
            refs[4 * n + 4 * i + 3][...] = g

    spec = pl.BlockSpec((tr,) + mid + (tc,), lambda i, j: (i,) + (0,) * len(mid) + (j,))
    outs = pl.pallas_call(
        body, name=name, grid=(rows // tr, cols // tc), in_specs=[spec] * (4 * n), out_specs=[spec] * (4 * n),
        out_shape=[jax.ShapeDtypeStruct(shape, F32)] * (4 * n),
        compiler_params=_cparams("parallel", "parallel"),
    )(*ws, *gs, *ms, *vs)
    return [tuple(outs[4 * i:4 * i + 4]) for i in range(n)]


def _place():
    x, y, c = lax.axis_index("x"), lax.axis_index("y"), lax.axis_index("c")
    chips = [(1 - x, y), (x, 1 - y), (1 - x, 1 - y)]
    return x, y, c, chips


def _any_specs(n):
    return [pl.BlockSpec(memory_space=pl.ANY)] * n


def _rcopy(src, dst, send_sem, recv_sem, dev):
    return pltpu.make_async_remote_copy(src_ref=src, dst_ref=dst, send_sem=send_sem, recv_sem=recv_sem,
                                        device_id=dev, device_id_type=MESH)


QUARTER = HALF // 2

TO_X, TO_Y, RELAY_X, RELAY_Y, FWD_X, FWD_Y, FWD_D0, FWD_D1 = range(8)
TO_D = RELAY_X


def _gather_rider(bufs, views, relay):
    n = len(bufs)

    def plan(rout, sems):
        send, recv = sems
        x, y, c, _ = _place()
        me, sx, sy, sd = 2 * x + y, 2 * (1 - x) + y, 2 * x + (1 - y), 2 * (1 - x) + (1 - y)
        nx, ny, nd, sib = (1 - x, y, c), (x, 1 - y, c), (1 - x, 1 - y, c), (x, y, 1 - c)
        mine, other = c * HALF, (1 - c) * HALF
        out = {TO_X: (me, mine, HALF, nx), TO_Y: (me, mine, HALF, ny),
               FWD_X: (sx, mine, HALF, sib), FWD_Y: (sy, mine, HALF, sib)}
        inn = {TO_X: (sx, mine, HALF), TO_Y: (sy, mine, HALF),
               FWD_X: (sx, other, HALF), FWD_Y: (sy, other, HALF)}
        if relay:
            out.update({RELAY_X: (sy, mine, QUARTER, nx), RELAY_Y: (sx, mine + QUARTER, QUARTER, ny),
                        FWD_D0: (sd, mine, QUARTER, sib), FWD_D1: (sd, mine + QUARTER, QUARTER, sib)})
            inn.update({RELAY_X: (sd, mine, QUARTER), RELAY_Y: (sd, mine + QUARTER, QUARTER),
                        FWD_D0: (sd, other, QUARTER), FWD_D1: (sd, other + QUARTER, QUARTER)})
        else:
            out.update({TO_D: (me, mine, HALF, nd), FWD_D0: (sd, mine, HALF, sib)})
            inn.update({TO_D: (sd, mine, HALF), FWD_D0: (sd, other, HALF)})

        def copy(kind, b):
            slot, col, ncols, dev = out[kind]
            win = views[b](rout[b], slot, col, ncols)
            return _rcopy(win, win, send.at[kind * n + b], recv.at[kind * n + b], dev)

        def land(kind, b):
            slot, col, ncols = inn[kind]
            win = views[b](rout[b], slot, col, ncols)
            return _rcopy(win, win, send.at[kind * n + b], recv.at[kind * n + b], (x, y, c))

        return copy, land

    if relay:
        first = (TO_X, TO_Y)
        chain = ((TO_X, (FWD_X, RELAY_Y)), (TO_Y, (FWD_Y, RELAY_X)), (RELAY_X, (FWD_D0,)), (RELAY_Y, (FWD_D1,)))
    else:
        first = (TO_X, TO_Y, TO_D)
        chain = ((TO_X, (FWD_X,)), (TO_Y, (FWD_Y,)), (TO_D, (FWD_D0,)))
    forwards = [k for _, then in chain for k in then if k in (FWD_X, FWD_Y, FWD_D0, FWD_D1)]
    sent = list(first) + [k for _, then in chain for k in then]

    def start(rin, rout, sems):
        copy, _ = plan(rout, sems)
        for kind in first:
            for b in range(n):
                copy(kind, b).start()

    def finish(rin, rout, sems):
        copy, land = plan(rout, sems)
        for landed, then in chain:
            for b in range(n):
                land(landed, b).wait_recv()
                for kind in then:
                    copy(kind, b).start()
        for kind in forwards:
            for b in range(n):
                land(kind, b).wait_recv()
        for kind in sent:
            for b in range(n):
                copy(kind, b).wait_send()

    return _Rider(list(bufs), [jax.ShapeDtypeStruct(a.shape, a.dtype) for a in bufs], {b: b for b in range(n)},
                  [pltpu.SemaphoreType.DMA((8 * n,))] * 2, start, finish)


def _small_gather_rider(cw):
    def descs(rin, rout, sems, x, y, c, chips):
        return [_rcopy(rin[0], rout[0].at[2 * x + y], sems[1].at[j], sems[2].at[j], (chip[0], chip[1], c))
                for j, chip in enumerate(chips)]

    def start(rin, rout, sems):
        x, y, c, chips = _place()
        pltpu.make_async_copy(rin[0], rout[0].at[2 * x + y], sems[0].at[0]).start()
        for cp in descs(rin, rout, sems, x, y, c, chips):
            cp.start()

    def finish(rin, rout, sems):
        x, y, c, chips = _place()
        for j, chip in enumerate(chips):
            _rcopy(rin[0], rout[0].at[2 * chip[0] + chip[1]], sems[1].at[j], sems[2].at[j], (x, y, c)).wait_recv()
        for cp in descs(rin, rout, sems, x, y, c, chips):
            cp.wait_send()
        pltpu.make_async_copy(rin[0], rout[0].at[2 * x + y], sems[0].at[0]).wait()

    return _Rider([cw], [jax.ShapeDtypeStruct((NSH,) + cw.shape, cw.dtype)], {},
                  [pltpu.SemaphoreType.DMA((1,)), pltpu.SemaphoreType.DMA((3,)), pltpu.SemaphoreType.DMA((3,))],
                  start, finish)


def _to_sibling_rider(ps):
    n = len(ps)

    def descs(rin, rout, sems):
        x, y, c, _ = _place()
        return [_rcopy(rin[i].at[:, :, pl.ds((1 - c) * HALF, HALF)], rout[i], sems[0].at[i], sems[1].at[i],
                       (x, y, 1 - c)) for i in range(n)]

    def start(rin, rout, sems):
        for cp in descs(rin, rout, sems):
            cp.start()

    def finish(rin, rout, sems):
        for cp in descs(rin, rout, sems):
            cp.wait()

    return _Rider(list(ps), [jax.ShapeDtypeStruct(a.shape[:2] + (HALF,), a.dtype) for a in ps], {},
                  [pltpu.SemaphoreType.DMA((n,))] * 2, start, finish)


def _to_chips_rider(cs):
    n = len(cs)

    def descs(rin, rout, sems):
        x, y, c, chips = _place()
        return [_rcopy(rin[i].at[2 * chip[0] + chip[1]], rout[i].at[j], sems[0].at[j * n + i], sems[1].at[j * n + i],
                       (chip[0], chip[1], c)) for j, chip in enumerate(chips) for i in range(n)]

    def start(rin, rout, sems):
        for cp in descs(rin, rout, sems):
            cp.start()

    def finish(rin, rout, sems):
        for cp in descs(rin, rout, sems):
            cp.wait()

    return _Rider(list(cs), [jax.ShapeDtypeStruct((3,) + a.shape[1:], a.dtype) for a in cs], {},
                  [pltpu.SemaphoreType.DMA((3 * n,))] * 2, start, finish)


def _run_riders(name, riders):
    n_in = [len(r.operands) for r in riders]
    n_out = [len(r.out_shapes) for r in riders]
    n_sem = [len(r.sems) for r in riders]

    def body(*refs):
        parts, at = [], 0
        for counts in (n_in, n_out, n_sem):
            group = []
            for k in counts:
                group.append(refs[at:at + k])
                at += k
            parts.append(group)
        for i, r in enumerate(riders):
            r.start(parts[0][i], parts[1][i], parts[2][i])
        for i, r in enumerate(riders):
            r.finish(parts[0][i], parts[1][i], parts[2][i])

    aliases = {}
    for i, r in enumerate(riders):
        for k, v in r.aliases.items():
            aliases[sum(n_in[:i]) + k] = sum(n_out[:i]) + v
    res = pl.pallas_call(
        body, name=name, in_specs=_any_specs(sum(n_in)), out_specs=_any_specs(sum(n_out)),
        out_shape=[s for r in riders for s in r.out_shapes], input_output_aliases=aliases,
        scratch_shapes=[s for r in riders for s in r.sems],
    )(*[a for r in riders for a in r.operands])
    out, at = [], 0
    for k in n_out:
        out.append(list(res[at:at + k]))
        at += k
    return out


SMALL_ROWS = 16


def _swap_halves(gs, vec):
    n = len(gs)

    def body(*refs):
        v_ref, dst, o_ref = refs[n], refs[n + 1:2 * n + 1], refs[2 * n + 1]
        buf, send, recv, vsend, vrecv = refs[2 * n + 2:]
        x, y, c, _ = _place()
        cps = []
        for i in range(n):
            mine = dst[i].at[:, pl.ds(c * HALF, HALF)]
            cps.append(_rcopy(mine, mine, send.at[i], recv.at[i], (x, y, 1 - c)))
        for cp in cps:
            cp.start()

        me = 4 * x + 2 * y + c
        buf[me] = v_ref[...]
        vcps = []
        for k in range(1, 8):
            peer = (x ^ (k >> 2), y ^ ((k >> 1) & 1), c ^ (k & 1))
            vcps.append(_rcopy(v_ref, buf.at[me], vsend.at[k - 1], vrecv.at[k - 1], peer))
        for cp in vcps:
            cp.start()
        for k in range(1, 8):
            _rcopy(v_ref, buf.at[me ^ k], vsend.at[k - 1], vrecv.at[k - 1], (x, y, c)).wait_recv()
        for cp in vcps:
            cp.wait_send()
        t = buf[0]
        for d in range(1, 8):
            t = t + buf[d]
        o_ref[...] = t

        for i in range(n):
            other = dst[i].at[:, pl.ds((1 - c) * HALF, HALF)]
            _rcopy(other, other, send.at[i], recv.at[i], (x, y, c)).wait_recv()
        for cp in cps:
            cp.wait_send()

    vmem = pl.BlockSpec(memory_space=pltpu.VMEM)
    res = pl.pallas_call(
        body, name="grads_swap_halves", in_specs=_any_specs(n) + [vmem], out_specs=_any_specs(n) + [vmem],
        out_shape=[jax.ShapeDtypeStruct(g.shape, g.dtype) for g in gs] + [jax.ShapeDtypeStruct((SMALL_ROWS, D), F32)],
        input_output_aliases={i: i for i in range(n)},
        scratch_shapes=[pltpu.VMEM((8, SMALL_ROWS, D), F32)] + [pltpu.SemaphoreType.DMA((n,))] * 2
        + [pltpu.SemaphoreType.DMA((7,))] * 2,
    )(*gs, vec)
    return list(res[:n]), res[n]


def _col_window(ref, slot, col, ncols):
    return ref.at[slot, :, pl.ds(col, ncols)]


def _stack_window(first, count):
    def view(ref, slot, col, ncols):
        return ref.at[slot, pl.ds(first, count), :, pl.ds(col, ncols)]
    return view


def _row_tile(rows):
    for t in range(512, 15, -16):
        if rows % t == 0:
            return t
    return rows


def _same_shape_runs(arrs):
    runs, a = [], 0
    for b in range(1, len(arrs) + 1):
        if b == len(arrs) or arrs[b].shape != arrs[a].shape:
            runs.append((a, b))
            a = b
    return runs


class _Comm:
    def __init__(self):
        x, y, c = lax.axis_index("x"), lax.axis_index("y"), lax.axis_index("c")
        self.c_idx = jnp.reshape(c, (1,)).astype(jnp.int32)
        self.place = jnp.stack([2 * x + y, c]).astype(jnp.int32)
        self.groups = {}

    @staticmethod
    def gather(*bufs, relay, part=None):
        views = [_col_window if b.ndim == 3 else _stack_window(*(part or (0, b.shape[1]))) for b in bufs]
        return _gather_rider(list(bufs), views, relay)

    def reduce_rider(self, tag, names, ps, theirs=None):
        csums = []
        for a, b in _same_shape_runs(ps):
            name, th = "pair_sum_%s%d" % (tag, a), _row_tile(ps[a].shape[1])
            csums += (_pair_sum(name, self.c_idx, ps[a:b], th) if theirs is None else
                      _pair_add(name, self.c_idx, ps[a:b], theirs[a:b], th))
        self.groups[tag] = [names, csums, None]
        return _to_chips_rider(csums)

    def landed(self, tag, ts):
        self.groups[tag][2] = ts

    def finish(self, small):
        names, csums, ts = [], [], []
        for group_names, group_csums, group_ts in self.groups.values():
            names += group_names
            csums += group_csums
            ts += group_ts
        order = sorted(range(len(names)), key=lambda i: csums[i].shape[1])
        names, csums, ts = ([v[i] for i in order] for v in (names, csums, ts))
        halves = []
        for a, b in _same_shape_runs(csums):
            halves += _chip_sum("chip_sum_%d" % a, self.place, csums[a:b], ts[a:b], _row_tile(csums[a].shape[1]))
        grads, total = _swap_halves(halves, small)
        return dict(zip(names, grads)), total


ROPE_THETA = 10000.0
SMALL_1K = ("ffn1_pre_norm", "ffn1_post_norm", "mix_pre_norm", "ssm_norm", "mix_post_norm",
            "ffn2_pre_norm", "ffn2_post_norm")
SMALL_16 = ("dt_bias", "a_log", "d_skip")
OFF_CONVB = 7 * D
OFF_16 = OFF_CONVB + CONV_C
OFF_CONVW = OFF_16 + 48
OFF_LOSS = OFF_CONVW + CONV_K * CONV_C
SMALL_LEN = SMALL_ROWS * D


def _sds(shape, dtype):
    return jax.ShapeDtypeStruct(shape, dtype)


def _ridden(res, rider):
    return res if rider is not None else (res, None)


def _ffn_down(name, act, w, tail_of, rider=None):
    tail, o_specs, o_shapes = tail_of(TS)
    return _mm(name, [act, w.dn], NN, (S // TS,),
               [pl.BlockSpec((NSH, TS, FS), lambda i: (0, i, 0)),
                pl.BlockSpec((NSH, None, FS, D), lambda i: (0, w.d0, 0, 0))], o_specs, o_shapes, rider, tail)


def _ffn_dw(name, a, b, rider=None):
    return _mm(name, [a, b], TN, (NSH,),
               [pl.BlockSpec((None, S, FS), lambda s: (s, 0, 0)), pl.BlockSpec((S, D), lambda s: (0, 0))],
               pl.BlockSpec((None, FS, D), lambda s: (s, 0, 0)), _sds((NSH, FS, D), BF16), rider)


def _ffn_dn(name, dgate, dup, w, tail_of, rider=None):
    rows = TS // 2
    tail, o_specs, o_shapes = tail_of(rows)
    a2 = pl.BlockSpec((NSH, rows, FS), lambda i: (0, i, 0))
    return _mm(name, [dgate, w.gu, dup, w.gu], NN, (S // rows,),
               [a2, pl.BlockSpec((NSH, None, FS, D), lambda i: (0, w.g0, 0, 0)),
                a2, pl.BlockSpec((NSH, None, FS, D), lambda i: (0, w.g0 + 1, 0, 0))], o_specs, o_shapes, rider, tail)


def _out_proj_dx(dh, wout):
    def body(dh_ref, w_ref, dyn_ref, do_ref):
        dm = _dot(dh_ref[...], w_ref[...], NT)
        dyn_ref[...] = dm[:, D:]
        for b in range(TS // 128):
            for j, blk in enumerate(_rows_to_blocks(dm[128 * b:128 * (b + 1), :D])):
                do_ref[j, b] = blk.astype(BF16)

    return pl.pallas_call(
        body, name="out_proj_dx", grid=(S // TS,),
        in_specs=[pl.BlockSpec((TS, D), lambda i: (i, 0)), pl.BlockSpec((2 * D, D), lambda i: (0, 0))],
        out_specs=[pl.BlockSpec((TS, D), lambda i: (i, 0)),
                   pl.BlockSpec((NKV, TS // 128, HD, QROWS), lambda i: (0, i, 0, 0))],
        out_shape=[_sds((S, D), F32), _sds((NKV, NCH, HD, QROWS), BF16)], compiler_params=_cparams("parallel"),
    )(dh, wout)


def _heads(t, n):
    return t.reshape(S, n, HD).transpose(1, 0, 2)


def _pad128(v):
    return jnp.pad(v, ((0, 0), (0, 128 - v.shape[1])))


def _local_step(x, positions, tgt, sp, gu1, d1, f2, wint, wout, convw, comm=None):
    inv_freq = ROPE_THETA ** (-jnp.arange(0, HD, 2, dtype=F32) / HD)
    ang = positions.astype(F32)[:, None] * inv_freq
    ang = jnp.concatenate([ang, ang, ang, ang], axis=-1)
    cos, sin = jnp.cos(ang), jnp.sin(ang)
    dtb, alog = _pad128(sp["dt_bias"]), _pad128(sp["a_log"])
    dskip_l = jnp.repeat(sp["d_skip"], HD, axis=1)
    convb = sp["conv_b"]

    n1 = _prenorm("prenorm1", x, sp["ffn1_pre_norm"])
    rider = comm.gather(d1, relay=False) if comm else None
    (fg1, fu1, act1), got = _ridden(_ffn_up("ffn1_up", n1, _FfnW(gu1, 0, d1, 0), rider), rider)
    if comm:
        d1, = got
    w1 = _FfnW(gu1, 0, d1, 0)
    rider = comm.gather(wint, relay=True) if comm else None
    (h1, x1, n2), got = _ridden(_ffn_down(
        "ffn1_down", act1, w1,
        lambda rows: _tail_postres(rows, x, sp["ffn1_post_norm"], 0.5, sp["mix_pre_norm"]), rider), rider)
    if comm:
        wint, = got
    wint_pad = jnp.pad(wint.reshape(WIN_COLS, D), ((0, WIN_PAD - WIN_COLS), (0, 0)))

    pw = WIN_PAD // 3
    rider = comm.gather(f2, relay=False, part=(0, 1)) if comm else None
    proj, got = _ridden(_mm(
        "in_proj", [n2, wint_pad], NT, (S // TS, 3),
        [pl.BlockSpec((TS, D), lambda i, j: (i, 0)), pl.BlockSpec((pw, D), lambda i, j: (j, 0))],
        pl.BlockSpec((TS, pw), lambda i, j: (i, j)), _sds((S, WIN_PAD), F32), rider), rider)
    if comm:
        f2, = got
    qt = _rope_q(proj, cos, sin)
    k_rot, v_bf, kt, vt = _rope_kv(proj, cos, sin)
    kh, vh = _heads(k_rot, NKV), _heads(v_bf, NKV)
    bias = _bias_table()
    rider = comm.gather(f2, wout, relay=False, part=(1, 2)) if comm else None
    (ot, lse, attn), got = _ridden(_attn_fwd(qt, kh, vt, bias, rider), rider)
    if comm:
        f2, wout = got
    w2 = _FfnW(f2, 0, f2, 2)
    wout = wout.reshape(2 * D, D)
    xbc = _conv_fwd(proj, convw, convb)
    y, yn, hprev = _ssd_fwd(xbc, proj, dtb, alog, dskip_l, sp["ssm_norm"])
    mixed = jnp.concatenate([attn, yn], axis=1)
    tail, o_specs, o_shapes = _tail_postres(TS, x1, sp["mix_post_norm"], 1.0, sp["ffn2_pre_norm"])
    h2, x2, n3 = _mm("out_proj", [mixed, wout], NN, (S // TS,),
                     [pl.BlockSpec((TS, 2 * D), lambda i: (i, 0)), pl.BlockSpec((2 * D, D), lambda i: (0, 0))],
                     o_specs, o_shapes, None, tail)

    fg2, fu2, act2 = _ffn_up("ffn2_up", n3, w2)
    dy, dh3, dp3, loss = _ffn_down(
        "ffn2_down", act2, w2, lambda rows: _tail_final(rows, x2, sp["ffn2_post_norm"], tgt, 0.5))

    dgate2, dup2 = _ffn_dact("ffn2_dact", dh3, w2, fg2, fu2)
    dws2 = [_ffn_dw("ffn2_dwg", dgate2, n3), _ffn_dw("ffn2_dwu", dup2, n3), _ffn_dw("ffn2_dwd", act2, dh3)]
    dx2, dh2, dg3, dp2 = _ffn_dn(
        "ffn2_dn", dgate2, dup2, w2,
        lambda rows: _tail_mid_bwd(rows, dy, x2, sp["ffn2_pre_norm"], h2, sp["mix_post_norm"], 1.0))

    dyn, dot_ = _out_proj_dx(dh2, wout)
    dwout = _mm("out_proj_dw", [mixed, dh2], TN, (2,),
                [pl.BlockSpec((S, D), lambda m: (0, m)), pl.BlockSpec((S, D), lambda m: (0, 0))],
                pl.BlockSpec((D, D), lambda m: (m, 0)), _sds((2 * D, D), BF16))
    dwout = dwout.reshape(NSH, 2 * D // NSH, D)

    def riding(tag, names, ps, call, theirs=None):
        rider = comm.reduce_rider(tag, names, ps, theirs) if comm else None
        res, got = _ridden(call(rider), rider)
        if comm:
            comm.landed(tag, got)
        return res

    rider = _to_sibling_rider(dws2 + [dwout]) if comm else None
    (dxbc, dproj, ddt, dssm, dsc), theirs = _ridden(
        _ssd_bwd(dyn, y, xbc, proj, hprev, dtb, alog, dskip_l, sp["ssm_norm"], rider), rider)
    dproj, dcw8, dcb = _conv_bwd(dxbc, proj, convw, convb, dproj)
    delta = _attn_delta(ot, dot_)
    dqt, dkh, dvh = riding("a", BIG[3:6] + ("w_out",), dws2 + [dwout], lambda rider: _attn_bwd(
        qt, kh, kt, vh, dot_, lse, delta, bias, rider), theirs)
    dproj = _rope_dq(dqt, cos, sin, dproj)
    dproj = _rope_dkv(dkh, dvh, cos, sin, dproj)
    dproj = lax.dynamic_update_slice(dproj, ddt, (0, COL_DT))
    dwint = _mm("in_proj_dw", [dproj, n2], TN, (3,),
                [pl.BlockSpec((S, pw), lambda j: (0, j)), pl.BlockSpec((S, D), lambda j: (0, 0))],
                pl.BlockSpec((pw, D), lambda j: (j, 0)), _sds((WIN_PAD, D), BF16))
    dwint = dwint[:WIN_COLS].reshape(NSH, WIN_SH, D)

    tail, o_specs, o_shapes = _tail_mid_bwd(TS, dx2, x1, sp["mix_pre_norm"], h1, sp["ffn1_post_norm"], 0.5)
    dx1, dh1, dg2, dp1 = riding("b", ("w_in",), [dwint], lambda rider: _mm(
        "in_proj_dx", [dproj, wint_pad], NN, (S // TS,),
        [pl.BlockSpec((TS, WIN_PAD), lambda i: (i, 0)), pl.BlockSpec((WIN_PAD, D), lambda i: (0, 0))],
        o_specs, o_shapes, rider, tail))

    dwd1 = _ffn_dw("ffn1_dwd", act1, dh1)
    dgate1, dup1 = riding("d", BIG[2:3], [dwd1], lambda rider: _ffn_dact("ffn1_dact", dh1, w1, fg1, fu1, rider))
    dwg1, dwu1 = _ffn_dw("ffn1_dwg", dgate1, n1), _ffn_dw("ffn1_dwu", dup1, n1)
    grad_x, dg1 = riding("g", BIG[0:2], [dwg1, dwu1], lambda rider: _ffn_dn(
        "ffn1_dn", dgate1, dup1, w1, lambda rows: _tail_first_bwd(rows, dx1, x, sp["ffn1_pre_norm"]), rider))
    dws1 = [dwg1, dwu1, dwd1]

    small = jnp.concatenate([
        dg1[0], dp1[0], dg2[0], dssm[0], dp2[0], dg3[0], dp3[0], dcb[0],
        dsc[0, :16], dsc[1, :16], dsc[2, :16], dcw8[:CONV_K].reshape(-1), loss[0, :1]])
    small = jnp.pad(small, (0, SMALL_LEN - small.shape[0])).reshape(SMALL_ROWS, D)
    if comm is None:
        return grad_x, dws1 + dws2 + [dwint, dwout], small
    return (grad_x,) + comm.finish(small)


WEIGHTS = ("ffn1_pre_norm", "ffn1_w_gate", "ffn1_w_up", "ffn1_w_down", "ffn1_post_norm", "mix_pre_norm", "w_in",
           "conv_w", "conv_b", "dt_bias", "a_log", "d_skip", "ssm_norm", "w_out", "mix_post_norm", "ffn2_pre_norm",
           "ffn2_w_gate", "ffn2_w_up", "ffn2_w_down", "ffn2_post_norm")
BIG = ("ffn1_w_gate", "ffn1_w_up", "ffn1_w_down", "ffn2_w_gate", "ffn2_w_up", "ffn2_w_down", "w_in", "w_out")
TRANSPOSED = ("ffn1_w_gate", "ffn1_w_up", "ffn2_w_gate", "ffn2_w_up", "w_in")
SMALL_ORDER = SMALL_1K + ("conv_b",) + SMALL_16
CONVW_SH = CONV_C // NSH


def _shard2d(t, name):
    return t[0].T if name in TRANSPOSED else t[0]


def _unshard2d(t, name):
    return (t.T if name in TRANSPOSED else t)[None]


def _rows3d(t):
    return t.transpose(2, 0, 1)


def _pack_small(d, prefix, shard_of_convw):
    flat = jnp.concatenate([d[prefix + n][0] for n in SMALL_ORDER] + [shard_of_convw.reshape(-1)])
    return jnp.pad(flat, (0, SMALL_LEN - flat.shape[0])).reshape(SMALL_ROWS, D)


def _unpack_small(block, like):
    flat = block.reshape(-1)
    out, off = {}, 0
    for n in SMALL_ORDER:
        size = like[n].shape[1]
        out[n] = flat[off:off + size].reshape(1, size)
        off += size
    out["conv_w"] = flat[off:off + CONV_K * CONVW_SH].reshape(1, CONV_K, CONVW_SH)
    return out


def kernel(x, positions, ffn1_pre_norm, ffn1_w_gate, ffn1_w_up, ffn1_w_down, ffn1_post_norm, mix_pre_norm, w_in, conv_w, conv_b, dt_bias, a_log, d_skip, ssm_norm, w_out, mix_post_norm, ffn2_pre_norm, ffn2_w_gate, ffn2_w_up, ffn2_w_down, ffn2_post_norm, loss_target, m_ffn1_pre_norm, m_ffn1_w_gate, m_ffn1_w_up, m_ffn1_w_down, m_ffn1_post_norm, m_mix_pre_norm, m_w_in, m_conv_w, m_conv_b, m_dt_bias, m_a_log, m_d_skip, m_ssm_norm, m_w_out, m_mix_post_norm, m_ffn2_pre_norm, m_ffn2_w_gate, m_ffn2_w_up, m_ffn2_w_down, m_ffn2_post_norm, v_ffn1_pre_norm, v_ffn1_w_gate, v_ffn1_w_up, v_ffn1_w_down, v_ffn1_post_norm, v_mix_pre_norm, v_w_in, v_conv_w, v_conv_b, v_dt_bias, v_a_log, v_d_skip, v_ssm_norm, v_w_out, v_mix_post_norm, v_ffn2_pre_norm, v_ffn2_w_gate, v_ffn2_w_up, v_ffn2_w_down, v_ffn2_post_norm):
    given = dict(locals())
    xi, yi = lax.axis_index("x"), lax.axis_index("y")

    shard = jnp.reshape(2 * xi + yi, (1,)).astype(jnp.int32)
    big = {p + n: _shard2d(given[p + n], n) for n in BIG for p in ("", "m_", "v_")}
    gu1 = _cast_stack("cast_ffn1_gate_up", shard, [big[n] for n in BIG[0:2]], 176, D)
    d1 = _cast_stack("cast_ffn1_down", shard, [big[BIG[2]]], 176, D)
    f2 = _cast_stack("cast_ffn2", shard, [big[n] for n in BIG[3:6]], 176, D)
    winsh = _cast_stack("cast_w_in", shard, [big["w_in"]], WIN_SH, 256).reshape(NSH, WIN_SH, D)
    woutsh = _cast_stack("cast_w_out", shard, [big["w_out"]], 256, D).reshape(NSH, 2 * D // NSH, D)
    comm = _Comm()
    (gu1,), (cwf,) = _run_riders("gather_ffn1_gate_up", [comm.gather(gu1, relay=True), _small_gather_rider(conv_w[0])])
    convw = cwf.transpose(1, 0, 2).reshape(CONV_K, CONV_C)

    sp = {n: given[n] for n in SMALL_ORDER}
    grad_x, big_grads, small = _local_step(x[0], positions[0], loss_target[0], sp, gu1, d1, f2, winsh, woutsh,
                                           convw, comm)

    tot = small.reshape(-1)
    loss = tot[OFF_LOSS]
    small_grads, off = {}, 0
    for n in SMALL_ORDER:
        size = given[n].shape[1]
        small_grads[n] = tot[off:off + size].reshape(1, size)
        off += size
    dconvw = tot[OFF_CONVW:OFF_CONVW + CONV_K * CONV_C].reshape(CONV_K, NSH, CONVW_SH)
    dconvw = lax.dynamic_index_in_dim(dconvw, 2 * xi + yi, axis=1, keepdims=False)
    small_grads["conv_w"] = dconvw.reshape(1, CONV_K, CONVW_SH)

    upd = {}
    for names, tr in ((BIG[0:3], 176), (BIG[3:6], 176), (BIG[7:8], 256)):
        res = _adamw("adamw_" + names[0], [big[n] for n in names], [big_grads[n] for n in names],
                     [big["m_" + n] for n in names], [big["v_" + n] for n in names], tr, D)
        for n, r in zip(names, res):
            upd[n] = tuple(_unshard2d(t, n) for t in r)
    g_win = big_grads["w_in"].reshape(WIN_SH, 1, D)
    res, = _adamw("adamw_w_in", [_rows3d(w_in)], [g_win], [_rows3d(m_w_in)], [_rows3d(v_w_in)], WIN_SH // 4, D)
    upd["w_in"] = tuple(t.transpose(1, 2, 0) for t in res)
    (dl, m2, v2, _), = _adamw(
        "adamw_small", [_pack_small(given, "", conv_w[0])], [_pack_small(small_grads, "", dconvw)],
        [_pack_small(given, "m_", m_conv_w[0])], [_pack_small(given, "v_", v_conv_w[0])], SMALL_ROWS, D)
    dl, m2, v2 = (_unpack_small(t, given) for t in (dl, m2, v2))
    for n in SMALL_ORDER + ("conv_w",):
        upd[n] = (dl[n], m2[n], v2[n], small_grads[n])

    return (loss, grad_x[None], *[upd[n][3] for n in WEIGHTS], *[upd[n][0] for n in WEIGHTS],
            *[upd[n][1] for n in WEIGHTS], *[upd[n][2] for n in WEIGHTS])
```

```python
import functools
import typing

import jax
import jax.numpy as jnp
from jax import lax
from jax.experimental import pallas as pl
from jax.experimental.pallas import tpu as pltpu

F32 = jnp.float32
BF16 = jnp.bfloat16

S = 2048
D = 1024
FF = 2816
NSH = 4
FS = FF // NSH
HALF = D // 2
HD = 64
NKV = 4
NQ_PER_KV = 4
KVW = NKV * HD
QCOLS = NQ_PER_KV * HD
CONV_C = 1536
CONV_K = 4
SSM_W = 1024
NST = 128
NCH = S // 128
WIN_COLS = 4112
WIN_SH = WIN_COLS // NSH
WIN_PAD = 4224
COL_DT = 4096
EPS = 1e-6
NEG = -1e30

ADAM_LR = 0.001
ADAM_B1 = 0.9
ADAM_B2 = 0.999
ADAM_EPS = 1e-08
ADAM_WD = 0.01
ADAM_STEP = 10

VMEM_LIMIT = 56 * 1024 * 1024
TS = 512
TR = 256

NN = (((1,), (0,)), ((), ()))
NT = (((1,), (1,)), ((), ()))
TN = (((0,), (0,)), ((), ()))
MESH = pl.DeviceIdType.MESH


def _cparams(*sem):
    return pltpu.CompilerParams(dimension_semantics=sem, vmem_limit_bytes=VMEM_LIMIT)


def _dot(a, b, dims):
    return lax.dot_general(a.astype(BF16), b.astype(BF16), dims, preferred_element_type=F32)


def _bf16_pieces(v):
    hi = v.astype(BF16)
    rest = v - hi.astype(F32)
    mid = rest.astype(BF16)
    return hi, mid, (rest - mid.astype(F32)).astype(BF16)


def _dot_exact(a, b, ones="a"):
    if ones == "a":
        sel = a.astype(BF16)
        parts = [lax.dot_general(sel, p, NN, preferred_element_type=F32) for p in _bf16_pieces(b)]
    else:
        sel = b.astype(BF16)
        parts = [lax.dot_general(p, sel, NN, preferred_element_type=F32) for p in _bf16_pieces(a)]
    return (parts[2] + parts[1]) + parts[0]


def _sigmoid(v):
    return 1.0 / (1.0 + jnp.exp(-v))


class _Rider(typing.NamedTuple):
    operands: list
    out_shapes: list
    aliases: dict
    sems: list
    start: typing.Callable
    finish: typing.Callable


def _call(body, name, grid, in_specs, out_specs, out_shape, operands, scratch=(), sem=(), rider=None):
    multi = isinstance(out_shape, (list, tuple))
    if rider is None:
        return pl.pallas_call(
            body, name=name, grid=grid, in_specs=in_specs, out_specs=out_specs, out_shape=out_shape,
            scratch_shapes=list(scratch), compiler_params=_cparams(*sem))(*operands)
    outs = list(out_shape) if multi else [out_shape]
    ospecs = list(out_specs) if multi else [out_specs]
    n_in, n_out, n_scr = len(operands), len(outs), len(scratch)
    ri, ro = len(rider.operands), len(rider.out_shapes)

    def wrapped(*refs):
        o0 = n_in + ri
        s0 = o0 + n_out + ro
        rin, rout, rsem = refs[n_in:o0], refs[o0 + n_out:s0], refs[s0 + n_scr:]
        ids = [pl.program_id(a) for a in range(len(grid))]
        first = functools.reduce(jnp.logical_and, [i == 0 for i in ids])
        last = functools.reduce(jnp.logical_and, [i == g - 1 for i, g in zip(ids, grid)])

        @pl.when(first)
        def _():
            rider.start(rin, rout, rsem)

        body(*refs[:n_in], *refs[o0:o0 + n_out], *refs[s0:s0 + n_scr])

        @pl.when(last)
        def _():
            rider.finish(rin, rout, rsem)

    hbm = pl.BlockSpec(memory_space=pl.ANY)
    res = pl.pallas_call(
        wrapped, name=name, grid=grid, in_specs=list(in_specs) + [hbm] * ri, out_specs=ospecs + [hbm] * ro,
        out_shape=outs + list(rider.out_shapes), scratch_shapes=list(scratch) + list(rider.sems),
        input_output_aliases={n_in + k: n_out + v for k, v in rider.aliases.items()},
        compiler_params=_cparams(*(("arbitrary",) * len(grid))))(*operands, *rider.operands)
    main = list(res[:n_out])
    return (main if multi else main[0]), list(res[n_out:])


class _Tail(typing.NamedTuple):
    fn: typing.Callable
    operands: list
    in_specs: list


def _mm(name, operands, dims, grid, in_specs, o_spec, out_shape, rider=None, tail=None):
    npairs = len(operands) // 2
    extra = [] if tail is None else list(tail.operands)
    nin = 2 * npairs + len(extra)

    def body(*refs):
        t = None
        for i in range(npairs):
            a, b = refs[2 * i], refs[2 * i + 1]
            parts = [(a[s], b[s]) for s in range(a.shape[0])] if len(a.shape) == 3 else [(a[...], b[...])]
            for pa, pb in parts:
                d = _dot(pa, pb, dims)
                t = d if t is None else t + d
        if tail is None:
            refs[nin][...] = t.astype(refs[nin].dtype)
        else:
            tail.fn(t, refs[2 * npairs:nin], refs[nin:])

    sem = ("parallel" if tail is None else "arbitrary",) * len(grid)
    specs = list(in_specs) + ([] if tail is None else list(tail.in_specs))
    return _call(body, name, grid, specs, o_spec, out_shape, list(operands) + extra, (), sem, rider)


class _FfnW(typing.NamedTuple):
    gu: jax.Array
    g0: int
    dn: jax.Array
    d0: int


def _ffn_up(name, n, w, rider=None):
    def body(n_ref, wg_ref, wu_ref, fg_ref, fu_ref, a_ref):
        nb = n_ref[...]
        g = _dot(nb, wg_ref[...], NT)
        u = _dot(nb, wu_ref[...], NT)
        sg = _sigmoid(g)
        silu = g * sg
        fg_ref[...] = (u * (sg * (1.0 + g * (1.0 - sg)))).astype(BF16)
        fu_ref[...] = silu.astype(BF16)
        a_ref[...] = (silu * u).astype(BF16)

    out = jax.ShapeDtypeStruct((NSH, S, FS), BF16)
    ospec = pl.BlockSpec((None, TS, FS), lambda s, i: (s, i, 0))
    return _call(
        body, name, (NSH, S // TS),
        [pl.BlockSpec((TS, D), lambda s, i: (i, 0)),
         pl.BlockSpec((None, None, FS, D), lambda s, i: (s, w.g0, 0, 0)),
         pl.BlockSpec((None, None, FS, D), lambda s, i: (s, w.g0 + 1, 0, 0))],
        [ospec, ospec, ospec], [out, out, out], (n, w.gu, w.gu), sem=("parallel", "parallel"), rider=rider)


def _ffn_dact(name, dh, w, fgate, fup, rider=None):
    def body(dh_ref, wd_ref, fg_ref, fu_ref, dg_ref, du_ref):
        da = _dot(dh_ref[...], wd_ref[...], NT)
        dg_ref[...] = (da * fg_ref[...].astype(F32)).astype(BF16)
        du_ref[...] = (da * fu_ref[...].astype(F32)).astype(BF16)

    out = jax.ShapeDtypeStruct((NSH, S, FS), BF16)
    aspec = pl.BlockSpec((None, TS, FS), lambda s, i: (s, i, 0))
    return _call(
        body, name, (NSH, S // TS),
        [pl.BlockSpec((TS, D), lambda s, i: (i, 0)),
         pl.BlockSpec((None, None, FS, D), lambda s, i: (s, w.d0, 0, 0)), aspec, aspec],
        [aspec, aspec], [out, out], (dh, w.dn, fgate, fup), sem=("parallel", "parallel"), rider=rider)


def _rstd(v):
    return lax.rsqrt(jnp.mean(v * v, axis=-1, keepdims=True) + EPS)


def _row_spec():
    return pl.BlockSpec((TR, D), lambda i: (i, 0))


def _vec_spec():
    return pl.BlockSpec((1, D), lambda i: (0, 0))


def _acc_rows(ref, v):
    @pl.when(pl.program_id(0) == 0)
    def _():
        ref[...] = jnp.zeros_like(ref)
    ref[...] += jnp.sum(v, axis=0, keepdims=True)


def _prenorm(name, x, g):
    def body(x_ref, g_ref, n_ref):
        xv = x_ref[...]
        n_ref[...] = (xv * _rstd(xv) * g_ref[...]).astype(BF16)

    return pl.pallas_call(
        body, name=name, grid=(S // TR,), in_specs=[_row_spec(), _vec_spec()], out_specs=_row_spec(),
        out_shape=jax.ShapeDtypeStruct((S, D), BF16), compiler_params=_cparams("parallel"),
    )(x, g)


def _rows_spec(rows):
    return pl.BlockSpec((rows, D), lambda i: (i, 0))


def _rows_f32():
    return jax.ShapeDtypeStruct((S, D), F32)


def _rows_bf16():
    return jax.ShapeDtypeStruct((S, D), BF16)


def _vec_f32():
    return jax.ShapeDtypeStruct((1, D), F32)


def _tail_postres(rows, x, p, alpha, gnext):
    def fn(h, ins, outs):
        x_ref, p_ref, g_ref = ins
        h_ref, xo_ref, n_ref = outs
        h_ref[...] = h
        xo = x_ref[...] + alpha * (h * _rstd(h) * p_ref[...])
        xo_ref[...] = xo
        n_ref[...] = (xo * _rstd(xo) * g_ref[...]).astype(BF16)

    rs = _rows_spec(rows)
    return (_Tail(fn, [x, p, gnext], [rs, _vec_spec(), _vec_spec()]), [rs, rs, rs],
            [_rows_f32(), _rows_f32(), _rows_bf16()])


def _tail_final(rows, x, p, tgt, alpha):
    def fn(h, ins, outs):
        x_ref, p_ref, t_ref = ins
        dy_ref, dh_ref, dp_ref, loss_ref = outs
        r = _rstd(h)
        hn = h * r
        pv = p_ref[...]
        e = x_ref[...] + alpha * (hn * pv) - t_ref[...]
        dy = e * (1.0 / D)
        dy_ref[...] = dy
        du = alpha * dy * pv
        dh_ref[...] = (r * (du - hn * jnp.mean(du * hn, axis=-1, keepdims=True))).astype(BF16)
        _acc_rows(dp_ref, alpha * dy * hn)
        part = 0.5 * jnp.sum(jnp.mean(e * e, axis=-1, keepdims=True), axis=0, keepdims=True)
        _acc_rows(loss_ref, jnp.broadcast_to(part, (1, 128)))

    rs = _rows_spec(rows)
    return (_Tail(fn, [x, p, tgt], [rs, _vec_spec(), rs]),
            [rs, rs, _vec_spec(), pl.BlockSpec((1, 128), lambda i: (0, 0))],
            [_rows_f32(), _rows_bf16(), _vec_f32(), jax.ShapeDtypeStruct((1, 128), F32)])


def _norm_bwd(dn, xv, g_ref, dg_ref):
    r = _rstd(xv)
    xn = xv * r
    dng = dn * g_ref[...]
    _acc_rows(dg_ref, dn * xn)
    return r * (dng - xn * jnp.mean(dng * xn, axis=-1, keepdims=True))


def _tail_mid_bwd(rows, dres, x, g, h, p, alpha):
    def fn(dn, ins, outs):
        dr_ref, x_ref, g_ref, h_ref, p_ref = ins
        dx_ref, dh_ref, dg_ref, dp_ref = outs
        dx = dr_ref[...] + _norm_bwd(dn, x_ref[...], g_ref, dg_ref)
        dx_ref[...] = dx
        hv = h_ref[...]
        r = _rstd(hv)
        hn = hv * r
        du = alpha * dx * p_ref[...]
        dh_ref[...] = (r * (du - hn * jnp.mean(du * hn, axis=-1, keepdims=True))).astype(BF16)
        _acc_rows(dp_ref, alpha * dx * hn)

    rs = _rows_spec(rows)
    return (_Tail(fn, [dres, x, g, h, p], [rs, rs, _vec_spec(), rs, _vec_spec()]),
            [rs, rs, _vec_spec(), _vec_spec()], [_rows_f32(), _rows_bf16(), _vec_f32(), _vec_f32()])


def _tail_first_bwd(rows, dres, x, g):
    def fn(dn, ins, outs):
        dr_ref, x_ref, g_ref = ins
        dx_ref, dg_ref = outs
        dx_ref[...] = dr_ref[...] + _norm_bwd(dn, x_ref[...], g_ref, dg_ref)

    rs = _rows_spec(rows)
    return (_Tail(fn, [dres, x, g], [rs, rs, _vec_spec()]), [rs, _vec_spec()], [_rows_f32(), _vec_f32()])


def _rotate(t, c128, s128, sign, scale):
    width = t.shape[1]
    c = jnp.tile(c128, (1, width // 128))
    sn = jnp.tile(s128, (1, width // 128))
    lane = lax.broadcasted_iota(jnp.int32, t.shape, 1) & (HD - 1)
    rot = jnp.where(lane < HD // 2, -pltpu.roll(t, width - HD // 2, 1), pltpu.roll(t, HD // 2, 1))
    return (t * c + sign * (rot * sn)) * scale


def _rows_to_blocks(y):
    out = []
    for j in range(NKV):
        yt = y[:, QCOLS * j:QCOLS * (j + 1)].T
        out.append(jnp.concatenate([yt[HD * g:HD * (g + 1)] for g in range(NQ_PER_KV)], axis=1))
    return out


def _blocks_to_rows(blocks):
    cols = []
    for b in blocks:
        stacked = jnp.concatenate([b[:, 128 * g:128 * (g + 1)] for g in range(NQ_PER_KV)], axis=0)
        cols.append(stacked.T)
    return jnp.concatenate(cols, axis=1)


def _rope_q(proj, cos, sin):
    def body(t_ref, c_ref, s_ref, o_ref):
        y = _rotate(t_ref[...], c_ref[...], s_ref[...], 1.0, HD ** -0.5)
        for j, blk in enumerate(_rows_to_blocks(y)):
            o_ref[j] = blk.astype(BF16)

    return pl.pallas_call(
        body, name="rope_q", grid=(NCH,),
        in_specs=[pl.BlockSpec((128, D), lambda i: (i, 0)),
                  pl.BlockSpec((128, 128), lambda i: (i, 0)), pl.BlockSpec((128, 128), lambda i: (i, 0))],
        out_specs=pl.BlockSpec((NKV, None, HD, QROWS), lambda i: (0, i, 0, 0)),
        out_shape=jax.ShapeDtypeStruct((NKV, NCH, HD, QROWS), BF16), compiler_params=_cparams("parallel"),
    )(proj, cos, sin)


def _rope_dq(dqt, cos, sin, dproj):
    def body(t_ref, c_ref, s_ref, buf_ref, o_ref):
        t = _blocks_to_rows([t_ref[j] for j in range(NKV)])
        o_ref[...] = _rotate(t, c_ref[...], s_ref[...], -1.0, HD ** -0.5).astype(BF16)

    return pl.pallas_call(
        body, name="rope_dq", grid=(NCH,),
        in_specs=[pl.BlockSpec((NKV, None, HD, QROWS), lambda i: (0, i, 0, 0)),
                  pl.BlockSpec((128, 128), lambda i: (i, 0)), pl.BlockSpec((128, 128), lambda i: (i, 0)),
                  pl.BlockSpec(memory_space=pl.ANY)],
        out_specs=pl.BlockSpec((128, D), lambda i: (i, 0)),
        out_shape=jax.ShapeDtypeStruct(dproj.shape, BF16), input_output_aliases={3: 0},
        compiler_params=_cparams("parallel"),
    )(dqt, cos, sin, dproj)


def _rope_dkv(dkt, dvt, cos, sin, dproj):
    def body(k_ref, v_ref, c_ref, s_ref, buf_ref, o_ref):
        dk = jnp.concatenate([k_ref[j] for j in range(NKV)], axis=0).T
        dv = jnp.concatenate([v_ref[j] for j in range(NKV)], axis=0).T
        dk = _rotate(dk, c_ref[...], s_ref[...], -1.0, 1.0)
        o_ref[...] = jnp.concatenate([dk, dv], axis=1).astype(BF16)

    tspec = pl.BlockSpec((NKV, HD, 128), lambda i: (0, 0, i))
    return pl.pallas_call(
        body, name="rope_dkv", grid=(NCH,),
        in_specs=[tspec, tspec, pl.BlockSpec((128, 128), lambda i: (i, 0)), pl.BlockSpec((128, 128), lambda i: (i, 0)),
                  pl.BlockSpec(memory_space=pl.ANY)],
        out_specs=pl.BlockSpec((128, 2 * KVW), lambda i: (i, D // (2 * KVW))),
        out_shape=jax.ShapeDtypeStruct(dproj.shape, BF16), input_output_aliases={4: 0},
        compiler_params=_cparams("parallel"),
    )(dkt, dvt, cos, sin, dproj)


def _rope_kv(proj, cos, sin):
    def body(t_ref, c_ref, s_ref, k_ref, v_ref, kt_ref, vt_ref):
        t = t_ref[...]
        k = _rotate(t[:, :KVW], c_ref[...], s_ref[...], 1.0, 1.0).astype(BF16)
        v = t[:, KVW:].astype(BF16)
        k_ref[...] = k
        v_ref[...] = v
        kt, vt = k.astype(F32).T, v.astype(F32).T
        for j in range(NKV):
            kt_ref[j] = kt[HD * j:HD * (j + 1)].astype(BF16)
            vt_ref[j] = vt[HD * j:HD * (j + 1)].astype(BF16)

    rows = pl.BlockSpec((128, KVW), lambda i: (i, 0))
    tspec = pl.BlockSpec((NKV, HD, 128), lambda i: (0, 0, i))
    return pl.pallas_call(
        body, name="rope_kv", grid=(NCH,),
        in_specs=[pl.BlockSpec((128, 2 * KVW), lambda i: (i, D // (2 * KVW))),
                  pl.BlockSpec((128, 128), lambda i: (i, 0)), pl.BlockSpec((128, 128), lambda i: (i, 0))],
        out_specs=[rows, rows, tspec, tspec],
        out_shape=[jax.ShapeDtypeStruct((S, KVW), BF16)] * 2 + [jax.ShapeDtypeStruct((NKV, HD, S), BF16)] * 2,
        compiler_params=_cparams("parallel"),
    )(proj, cos, sin)


QROWS = NQ_PER_KV * 128


NBIAS = NCH + 1
KV_PER_STEP = 4


def _bias_table():
    db = lax.broadcasted_iota(jnp.int32, (NBIAS, 128, QROWS), 0) - 1
    ki = lax.broadcasted_iota(jnp.int32, (NBIAS, 128, QROWS), 1)
    qi = lax.broadcasted_iota(jnp.int32, (NBIAS, 128, QROWS), 2) & 127
    d = db * 128 + qi - ki
    cnt = ((d <= 128).astype(F32) + (((d & 3) == 0) & (d <= 512)).astype(F32) + ((d & 15) == 0).astype(F32))
    return jnp.where((d >= 0) & (cnt > 0.0), jnp.log(jnp.maximum(cnt, 1.0)), NEG)


def _qt_spec():
    return pl.BlockSpec((None, None, HD, QROWS), lambda j, i: (j, i, 0, 0))


def _stat_spec():
    return pl.BlockSpec((None, None, 1, QROWS), lambda j, i: (j, i, 0, 0))


def _attn_fwd(qt, kh, vt, bias, rider=None):
    def body(q_ref, k_ref, v_ref, b_ref, o_ref, lse_ref, rows_ref, m_ref, l_ref, acc_ref):
        qb = pl.program_id(1)
        m_ref[...] = jnp.full_like(m_ref, NEG)
        l_ref[...] = jnp.zeros_like(l_ref)
        acc_ref[...] = jnp.zeros_like(acc_ref)

        def keys(off, size, bias_):
            for h in range(KV_PER_STEP):
                m = m_ref[h]
                s = _dot(k_ref[h, pl.ds(off, size), :], q_ref[h], NN) + bias_
                m_new = jnp.maximum(m, jnp.max(s, axis=0, keepdims=True))
                p = jnp.exp(s - m_new)
                a = jnp.exp(m - m_new)
                m_ref[h] = m_new
                l_ref[h] = a * l_ref[h] + jnp.sum(p, axis=0, keepdims=True)
                acc_ref[h] = a * acc_ref[h] + _dot(v_ref[h, :, pl.ds(off, size)], p, NN)

        def blocks(first, count):
            bias_ = jnp.concatenate([b_ref[qb - first - j + 1] for j in range(count)], axis=0)
            keys(pl.multiple_of(first * 128, 128), 128 * count, bias_)

        nkb = qb + 1
        @pl.loop(0, nkb // 4)
        def _(i):
            blocks(4 * i, 4)

        @pl.when(nkb % 4 >= 2)
        def _():
            blocks(nkb // 4 * 4, 2)

        @pl.when(nkb % 2 == 1)
        def _():
            blocks(qb, 1)

        outs = []
        for h in range(KV_PER_STEP):
            outs.append(acc_ref[h] / l_ref[h])
            o_ref[h] = outs[h]
            lse_ref[h] = m_ref[h] + jnp.log(l_ref[h])
        rows_ref[...] = _blocks_to_rows(outs).astype(BF16)

    kvs = KV_PER_STEP
    qspec = pl.BlockSpec((kvs, None, HD, QROWS), lambda j, i: (j, i, 0, 0))
    return _call(
        body, "attn_fwd", (NKV // kvs, NCH),
        [qspec, pl.BlockSpec((kvs, S, HD), lambda j, i: (j, 0, 0)),
         pl.BlockSpec((kvs, HD, S), lambda j, i: (j, 0, 0)),
         pl.BlockSpec((NBIAS, 128, QROWS), lambda j, i: (0, 0, 0))],
        [qspec, pl.BlockSpec((kvs, None, 1, QROWS), lambda j, i: (j, i, 0, 0)),
         pl.BlockSpec((128, QCOLS * kvs), lambda j, i: (i, j))],
        [jax.ShapeDtypeStruct((NKV, NCH, HD, QROWS), F32), jax.ShapeDtypeStruct((NKV, NCH, 1, QROWS), F32),
         jax.ShapeDtypeStruct((S, D), BF16)],
        (qt, kh, vt, bias),
        [pltpu.VMEM((kvs, 1, QROWS), F32), pltpu.VMEM((kvs, 1, QROWS), F32), pltpu.VMEM((kvs, HD, QROWS), F32)],
        ("parallel", "parallel"), rider)


def _attn_delta(ot, dot_):
    def body(o_ref, do_ref, dl_ref):
        dl_ref[...] = jnp.sum(o_ref[...] * do_ref[...].astype(F32), axis=1, keepdims=True)

    spec = pl.BlockSpec((None, NCH, HD, QROWS), lambda j: (j, 0, 0, 0))
    return pl.pallas_call(
        body, name="attn_delta", grid=(NKV,), in_specs=[spec, spec],
        out_specs=pl.BlockSpec((None, NCH, 1, QROWS), lambda j: (j, 0, 0, 0)),
        out_shape=jax.ShapeDtypeStruct((NKV, NCH, 1, QROWS), F32), compiler_params=_cparams("parallel"),
    )(ot, dot_)


def _attn_bwd(qt, kh, kt, vh, dot_, lse, delta, bias, rider=None):
    def body(qt_ref, k_ref, kt_ref, v_ref, dot_ref, lse_ref, dl_ref, b_ref, dq_ref, dk_ref, dv_ref):
        kp = pl.program_id(1)

        @pl.when(kp == 0)
        def _():
            dq_ref[...] = jnp.zeros_like(dq_ref)

        dk_ref[...] = jnp.zeros_like(dk_ref)
        dv_ref[...] = jnp.zeros_like(dv_ref)

        @pl.loop(2 * kp, NCH // 2)
        def _(j):
            for h in range(KV_PER_STEP):
                k, kt_, v = k_ref[h], kt_ref[h], v_ref[h]
                for qb in (2 * j, 2 * j + 1):
                    bias2 = jnp.concatenate([b_ref[jnp.maximum(qb - 4 * kp - t + 1, 0)] for t in range(4)], axis=0)
                    st = _dot(k, qt_ref[h, qb], NN) + bias2
                    pt = jnp.exp(st - lse_ref[h, qb])
                    dst = pt * (_dot(v, dot_ref[h, qb], NN) - dl_ref[h, qb])
                    dq_ref[h, qb] += _dot(kt_, dst, NN)
                    dk_ref[h] += _dot(qt_ref[h, qb], dst, NT)
                    dv_ref[h] += _dot(dot_ref[h, qb], pt, NT)

    kvs = KV_PER_STEP
    tspec = pl.BlockSpec((kvs, NCH, HD, QROWS), lambda j, i: (j, 0, 0, 0))
    kspec = pl.BlockSpec((kvs, 512, HD), lambda j, i: (j, i, 0))
    ktspec = pl.BlockSpec((kvs, HD, 512), lambda j, i: (j, 0, i))
    sspec = pl.BlockSpec((kvs, NCH, 1, QROWS), lambda j, i: (j, 0, 0, 0))
    return _call(
        body, "attn_bwd", (NKV // kvs, NCH // 4),
        [tspec, kspec, ktspec, kspec, tspec, sspec, sspec,
         pl.BlockSpec((NBIAS, 128, QROWS), lambda j, i: (0, 0, 0))],
        [tspec, ktspec, ktspec],
        [jax.ShapeDtypeStruct((NKV, NCH, HD, QROWS), F32),
         jax.ShapeDtypeStruct((NKV, HD, S), F32), jax.ShapeDtypeStruct((NKV, HD, S), F32)],
        (qt, kh, kt, vh, dot_, lse, delta, bias), sem=("parallel", "arbitrary"), rider=rider)


CONV_BLK = 256
CONV_COL0 = 1536 // CONV_BLK


CONV_ROWS = 128


def _conv_fwd(proj, convw, convb):
    trips = S // CONV_ROWS

    def body(u_ref, w_ref, b_ref, o_ref, y_ref):
        @pl.loop(0, trips)
        def _(c):
            t0 = pl.multiple_of(c * CONV_ROWS, CONV_ROWS)
            before = pl.multiple_of(jnp.maximum(t0 - 8, 0), 8)
            ext = jnp.concatenate([jnp.where(c == 0, 0.0, u_ref[pl.ds(before, 8), :]),
                                   u_ref[pl.ds(t0, CONV_ROWS), :]], axis=0)
            y = b_ref[...] + w_ref[CONV_K - 1:CONV_K, :] * ext[8:]
            for j in range(1, CONV_K):
                y = y + w_ref[CONV_K - 1 - j:CONV_K - j, :] * pltpu.roll(ext, j, 0)[8:]
            y_ref[pl.ds(t0, CONV_ROWS), :] = y
            o_ref[pl.ds(t0, CONV_ROWS), :] = y * _sigmoid(y)

    out = pl.BlockSpec((S, CONV_BLK), lambda i: (0, i))
    return pl.pallas_call(
        body, name="conv_fwd", grid=(CONV_C // CONV_BLK,),
        in_specs=[pl.BlockSpec((S, CONV_BLK), lambda i: (0, CONV_COL0 + i)),
                  pl.BlockSpec((CONV_K, CONV_BLK), lambda i: (0, i)),
                  pl.BlockSpec((1, CONV_BLK), lambda i: (0, i))],
        out_specs=[out, out], out_shape=[jax.ShapeDtypeStruct((S, CONV_C), F32)] * 2,
        compiler_params=_cparams("parallel"),
    )(proj, convw, convb)


def _conv_bwd(dact, ypre, proj, convw, dproj):
    trips = S // CONV_ROWS

    def body(da_ref, y_ref, u_ref, w_ref, buf_ref, du_ref, dw_ref, db_ref):
        dw_ref[...] = jnp.zeros_like(dw_ref)
        db_ref[...] = jnp.zeros_like(db_ref)
        r8 = lax.broadcasted_iota(jnp.int32, (8, CONV_BLK), 0)

        def dy_of(rows):
            y = y_ref[rows, :]
            sg = _sigmoid(y)
            return da_ref[rows, :] * (sg * (1.0 + y * (1.0 - sg)))

        @pl.loop(0, trips)
        def _(c):
            t0 = pl.multiple_of(c * CONV_ROWS, CONV_ROWS)
            after = pl.multiple_of(jnp.minimum(t0 + CONV_ROWS, S - 8), 8)
            ext = jnp.concatenate([dy_of(pl.ds(t0, CONV_ROWS)),
                                   jnp.where(c == trips - 1, 0.0, dy_of(pl.ds(after, 8)))], axis=0)
            u = u_ref[pl.ds(t0, CONV_ROWS), :]
            du, dw = None, jnp.zeros((8, CONV_BLK), F32)
            for j in range(CONV_K):
                dyj = (ext if j == 0 else pltpu.roll(ext, CONV_ROWS + 8 - j, 0))[:CONV_ROWS]
                term = w_ref[CONV_K - 1 - j:CONV_K - j, :] * dyj
                du = term if du is None else du + term
                dw = dw + jnp.where(r8 == CONV_K - 1 - j, jnp.sum(dyj * u, axis=0, keepdims=True), 0.0)
            du_ref[pl.ds(t0, CONV_ROWS), :] = du.astype(BF16)
            dw_ref[...] += dw
            db_ref[...] += jnp.sum(ext[:CONV_ROWS], axis=0, keepdims=True)

    return pl.pallas_call(
        body, name="conv_bwd", grid=(CONV_C // CONV_BLK,),
        in_specs=[pl.BlockSpec((S, CONV_BLK), lambda i: (0, i)), pl.BlockSpec((S, CONV_BLK), lambda i: (0, i)),
                  pl.BlockSpec((S, CONV_BLK), lambda i: (0, CONV_COL0 + i)),
                  pl.BlockSpec((CONV_K, CONV_BLK), lambda i: (0, i)), pl.BlockSpec(memory_space=pl.ANY)],
        out_specs=[pl.BlockSpec((S, CONV_BLK), lambda i: (0, CONV_COL0 + i)),
                   pl.BlockSpec((8, CONV_BLK), lambda i: (0, i)), pl.BlockSpec((1, CONV_BLK), lambda i: (0, i))],
        out_shape=[jax.ShapeDtypeStruct(dproj.shape, BF16), jax.ShapeDtypeStruct((8, CONV_C), F32),
                   jax.ShapeDtypeStruct((1, CONV_C), F32)],
        input_output_aliases={4: 0}, compiler_params=_cparams("parallel"),
    )(dact, ypre, proj, convw, dproj)


NPAIR = 8


def _ssd_scalars(dtr_ref, dtb_ref, alog_ref):
    z = dtr_ref[...] + dtb_ref[...]
    dt = jnp.maximum(z, 0.0) + jnp.log(1.0 + jnp.exp(-jnp.abs(z)))
    a = -jnp.exp(alog_ref[...])
    r = lax.broadcasted_iota(jnp.int32, (128, 128), 0)
    c = lax.broadcasted_iota(jnp.int32, (128, 128), 1)
    tri = (r >= c).astype(F32)
    cs = _dot_exact(tri, dt * a)
    return z, dt, a, cs, r, c


def _by_lane(cs, dt):
    head = lax.broadcasted_iota(jnp.int32, (128, SSM_W), 0)
    lane = lax.broadcasted_iota(jnp.int32, (128, SSM_W), 1)
    sel = (head == lane // HD).astype(F32)
    cs_l = _dot_exact(cs, sel, "b")
    last_l = cs_l[127:128, :]
    return sel, jnp.exp(cs_l), jnp.exp(last_l - cs_l), _dot_exact(dt, sel, "b")


def _pair_terms(cs, h1, h2):
    return (cs[:, h1:h1 + 1], cs[:, h2:h2 + 1],
            jnp.exp(cs[127:128, h1:h1 + 1]), jnp.exp(cs[127:128, h2:h2 + 1]))


def _gate_norm(y, zv, w):
    yg = y * (zv * _sigmoid(zv))
    outs, rs = [], []
    for g in range(2):
        blk = yg[:, 512 * g:512 * (g + 1)]
        r = lax.rsqrt(jnp.mean(blk * blk, axis=-1, keepdims=True) + EPS)
        outs.append(blk * r)
        rs.append(r)
    return jnp.concatenate(outs, axis=1), rs, yg


def _ssd_fwd(xbc, proj, dtb, alog, dskip_l, ssmw):
    def body(x_ref, b_ref, c_ref, dtr_ref, z_ref, dtb_ref, alog_ref, dsk_ref, w_ref, y_ref, yn_ref, hp_ref, h_ref):
        @pl.when(pl.program_id(0) == 0)
        def _():
            h_ref[...] = jnp.zeros_like(h_ref)

        _, dt, _, cs, r, c = _ssd_scalars(dtr_ref, dtb_ref, alog_ref)
        cst = cs.T
        causal = r >= c
        lo = c < HD
        _, e_all, dte_all, dt_all = _by_lane(cs, dt)
        hp_ref[...] = h_ref[...]
        for g in range(2):
            bg = b_ref[:, 128 * g:128 * (g + 1)]
            cg = c_ref[:, 128 * g:128 * (g + 1)]
            cb = _dot(cg, bg, NT)
            for j in range(4):
                pj = 4 * g + j
                h1, h2 = 2 * pj, 2 * pj + 1
                sl = slice(128 * pj, 128 * (pj + 1))
                xp = x_ref[:, sl]
                c1, c2, cd1, cd2 = _pair_terms(cs, h1, h2)
                e_l, dte_l = e_all[:, sl], dte_all[:, sl]
                xdt = xp * dt_all[:, sl]
                m1 = cb * jnp.exp(jnp.where(causal, c1 - cst[h1:h1 + 1, :], NEG))
                m2 = cb * jnp.exp(jnp.where(causal, c2 - cst[h2:h2 + 1, :], NEG))
                yd = jnp.where(lo, _dot(m1, xdt, NN), _dot(m2, xdt, NN))
                hp = h_ref[pj]
                yo = _dot(cg, hp, NT) * e_l
                st = _dot(xdt * dte_l, bg, TN)
                h_ref[pj] = hp * jnp.where(r < HD, cd1, cd2) + st
                y_ref[:, sl] = yd + yo + dsk_ref[:, sl] * xp
        yn, _, _ = _gate_norm(y_ref[...], z_ref[...], w_ref[...])
        yn_ref[...] = (yn * w_ref[...]).astype(BF16)

    return pl.pallas_call(
        body, name="ssd_fwd", grid=(NCH,),
        in_specs=[pl.BlockSpec((128, SSM_W), lambda i: (i, 0)),
                  pl.BlockSpec((128, 256), lambda i: (i, 4)), pl.BlockSpec((128, 256), lambda i: (i, 5)),
                  pl.BlockSpec((128, 128), lambda i: (i, COL_DT // 128)),
                  pl.BlockSpec((128, SSM_W), lambda i: (i, 3)),
                  pl.BlockSpec((1, 128), lambda i: (0, 0)), pl.BlockSpec((1, 128), lambda i: (0, 0)),
                  pl.BlockSpec((1, SSM_W), lambda i: (0, 0)), pl.BlockSpec((1, SSM_W), lambda i: (0, 0))],
        out_specs=[pl.BlockSpec((128, SSM_W), lambda i: (i, 0)), pl.BlockSpec((128, SSM_W), lambda i: (i, 0)),
                   pl.BlockSpec((None, NPAIR, 128, 128), lambda i: (i, 0, 0, 0))],
        out_shape=[jax.ShapeDtypeStruct((S, SSM_W), F32), jax.ShapeDtypeStruct((S, SSM_W), BF16),
                   jax.ShapeDtypeStruct((NCH, NPAIR, 128, 128), F32)],
        scratch_shapes=[pltpu.VMEM((NPAIR, 128, 128), F32)],
        compiler_params=_cparams("arbitrary"),
    )(xbc, xbc, xbc, proj, proj, dtb, alog, dskip_l, ssmw)


def _ssd_bwd(dmixed, y, xbc, proj, hprev, dtb, alog, dskip_l, ssmw, rider=None):
    def body(dyn_ref, y_ref, x_ref, b_ref, c_ref, dtr_ref, z_ref, hp_ref, dtb_ref, alog_ref, dsk_ref, w_ref,
             dxbc_ref, dz_ref, ddt_ref, dw_ref, dsc_ref, g_ref):
        @pl.when(pl.program_id(0) == 0)
        def _():
            g_ref[...] = jnp.zeros_like(g_ref)
            dsc_ref[...] = jnp.zeros_like(dsc_ref)

        z, dt, a, cs, r, c = _ssd_scalars(dtr_ref, dtb_ref, alog_ref)
        cst = cs.T
        causal = r >= c
        lo = c < HD

        yv = y_ref[...]
        zv = z_ref[...]
        wv = w_ref[...]
        ygn, rs, yg = _gate_norm(yv, zv, wv)
        dyn = dyn_ref[...]
        _acc_rows(dw_ref, dyn * ygn)
        dynw = dyn * wv
        parts = []
        for g in range(2):
            sl = slice(512 * g, 512 * (g + 1))
            a_g, n_g = dynw[:, sl], ygn[:, sl]
            parts.append(rs[g] * (a_g - n_g * jnp.mean(a_g * n_g, axis=-1, keepdims=True)))
        dyg = jnp.concatenate(parts, axis=1)
        sz = _sigmoid(zv)
        dz_ref[...] = (dyg * yv * (sz * (1.0 + zv * (1.0 - sz)))).astype(BF16)
        dy_all = dyg * (zv * sz)

        dcs_cols = jnp.zeros((128, 128), F32)
        dcs_rows = jnp.zeros((128, 128), F32)
        sel, e_all, dte_all, dt_all = _by_lane(cs, dt)
        x_all, b_all, c_all, dsk_all = x_ref[...], b_ref[...], c_ref[...], dsk_ref[...]
        hp_all, g_all = hp_ref[...], g_ref[...]
        g_new, dx_parts, db_parts, dc_parts = [], [], [], []
        dyx_parts, ryo_parts, qx_parts, dxx_parts, gh_parts = [], [], [], [], []
        for g in range(2):
            bg = b_all[:, 128 * g:128 * (g + 1)]
            cg = c_all[:, 128 * g:128 * (g + 1)]
            cb = _dot(cg, bg, NT)
            dcb = jnp.zeros((128, 128), F32)
            db_acc = jnp.zeros((128, NST), F32)
            dc_acc = jnp.zeros((128, NST), F32)
            for j in range(4):
                pj = 4 * g + j
                h1, h2 = 2 * pj, 2 * pj + 1
                sl = slice(128 * pj, 128 * (pj + 1))
                xp = x_all[:, sl]
                dyp = dy_all[:, sl]
                c1, c2, cd1, cd2 = _pair_terms(cs, h1, h2)
                e_l, dte_l, dt_l = e_all[:, sl], dte_all[:, sl], dt_all[:, sl]
                xdt = xp * dt_l
                hp = hp_all[pj]
                gp = g_all[pj]
                dyx_parts.append(dyp * xp)
                dzs = dyp * e_l
                dc_acc = dc_acc + _dot(dzs, hp, NN)
                ryo_parts.append(dyp * (_dot(cg, hp, NT) * e_l))
                qm = _dot(bg, gp, NT)
                dxdt = qm * dte_l
                qx_parts.append(qm * xdt)
                db_acc = db_acc + _dot(xdt * dte_l, gp, NN)
                gh_parts.append(gp * hp)
                g_new.append(_dot(dzs, cg, TN) + jnp.where(r < HD, cd1, cd2) * gp)
                for hh, ch, msk in ((h1, c1, lo), (h2, c2, jnp.logical_not(lo))):
                    lm = jnp.exp(jnp.where(causal, ch - cst[hh:hh + 1, :], NEG))
                    mm = cb * lm
                    dm = jnp.where(causal, _dot(jnp.where(msk, dyp, 0.0), xdt, NT), 0.0)
                    w = dm * mm
                    dcs_cols = dcs_cols + jnp.where(c == hh, jnp.sum(w, axis=1, keepdims=True), 0.0)
                    dcs_rows = dcs_rows + jnp.where(r == hh, jnp.sum(w, axis=0, keepdims=True), 0.0)
                    dcb = dcb + dm * lm
                    dxdt = dxdt + jnp.where(msk, _dot(mm, dyp, TN), 0.0)
                dxx_parts.append(dxdt * xp)
                dx_parts.append(dsk_all[:, sl] * dyp + dxdt * dt_l)
            db_parts.append(db_acc + _dot(dcb, cg, TN))
            dc_parts.append(dc_acc + _dot(dcb, bg, NN))
        g_ref[...] = jnp.stack(g_new)
        dxbc_ref[...] = jnp.concatenate(dx_parts + db_parts + dc_parts, axis=1)

        selt = (lax.broadcasted_iota(jnp.int32, (SSM_W, 128), 0) // HD
                == lax.broadcasted_iota(jnp.int32, (SSM_W, 128), 1)).astype(F32)

        def by_head(parts):
            return _dot_exact(jnp.concatenate(parts, axis=1), selt, "b")

        ddt_x = by_head(dxx_parts)
        dd_row = jnp.sum(by_head(dyx_parts), axis=0, keepdims=True)
        t_all = by_head(qx_parts) * jnp.exp(cs[127:128, :] - cs)
        gh = jnp.sum(_dot_exact(sel, jnp.concatenate(gh_parts, axis=0)), axis=1, keepdims=True)
        gh_row = jnp.broadcast_to(gh, (128, 128)).T[0:1, :]
        at_end = jnp.sum(t_all, axis=0, keepdims=True) + gh_row * jnp.exp(cs[127:128, :])
        dcs = by_head(ryo_parts) - t_all + dcs_cols + jnp.where(r == 127, at_end, 0.0) - dcs_rows.T
        dad = _dot_exact((c >= r).astype(F32), dcs)
        ddt = dad * a + ddt_x
        ddtr = jnp.where(c < 16, ddt * _sigmoid(z), 0.0)
        ddt_ref[...] = ddtr.astype(BF16)
        r8 = lax.broadcasted_iota(jnp.int32, (8, 128), 0)
        dsc_ref[...] += (jnp.where(r8 == 0, jnp.sum(ddtr, axis=0, keepdims=True), 0.0)
                         + jnp.where(r8 == 1, jnp.sum(dad * dt, axis=0, keepdims=True) * a, 0.0)
                         + jnp.where(r8 == 2, dd_row, 0.0))

    rev = NCH - 1
    return _call(
        body, "ssd_bwd", (NCH,),
        [pl.BlockSpec((128, SSM_W), lambda i: (rev - i, 0)),
         pl.BlockSpec((128, SSM_W), lambda i: (rev - i, 0)),
         pl.BlockSpec((128, SSM_W), lambda i: (rev - i, 0)),
         pl.BlockSpec((128, 256), lambda i: (rev - i, 4)), pl.BlockSpec((128, 256), lambda i: (rev - i, 5)),
         pl.BlockSpec((128, 128), lambda i: (rev - i, COL_DT // 128)),
         pl.BlockSpec((128, SSM_W), lambda i: (rev - i, 3)),
         pl.BlockSpec((None, NPAIR, 128, 128), lambda i: (rev - i, 0, 0, 0)),
         pl.BlockSpec((1, 128), lambda i: (0, 0)), pl.BlockSpec((1, 128), lambda i: (0, 0)),
         pl.BlockSpec((1, SSM_W), lambda i: (0, 0)), pl.BlockSpec((1, SSM_W), lambda i: (0, 0))],
        [pl.BlockSpec((128, CONV_C), lambda i: (rev - i, 0)),
         pl.BlockSpec((128, SSM_W), lambda i: (rev - i, 3)),
         pl.BlockSpec((128, 128), lambda i: (rev - i, 0)),
         pl.BlockSpec((1, SSM_W), lambda i: (0, 0)), pl.BlockSpec((8, 128), lambda i: (0, 0))],
        [jax.ShapeDtypeStruct((S, CONV_C), F32), jax.ShapeDtypeStruct((S, WIN_PAD), BF16),
         jax.ShapeDtypeStruct((S, 128), BF16), jax.ShapeDtypeStruct((1, SSM_W), F32),
         jax.ShapeDtypeStruct((8, 128), F32)],
        (dmixed, y, xbc, xbc, xbc, proj, proj, hprev, dtb, alog, dskip_l, ssmw),
        [pltpu.VMEM((NPAIR, 128, 128), F32)], ("arbitrary",), rider)


def _cast_stack(name, slot, arrs, tr, tc):
    n = len(arrs)
    rows, cols = arrs[0].shape

    def body(s_ref, *refs):
        for i in range(n):
            refs[n][i] = refs[i][...].astype(BF16)

    return pl.pallas_call(
        body, name=name,
        grid_spec=pltpu.PrefetchScalarGridSpec(
            num_scalar_prefetch=1, grid=(rows // tr, cols // tc),
            in_specs=[pl.BlockSpec((tr, tc), lambda i, j, sr: (i, j))] * n,
            out_specs=pl.BlockSpec((None, n, tr, tc), lambda i, j, sr: (sr[0], 0, i, j))),
        out_shape=jax.ShapeDtypeStruct((NSH, n, rows, cols), BF16),
        compiler_params=_cparams("parallel", "parallel"),
    )(slot, *arrs)


def _pair_sum(name, c_idx, ps, th):
    n = len(ps)
    _, rows, _ = ps[0].shape

    def body(c_ref, *refs):
        mine, whole, out, theirs = refs[:n], refs[n:2 * n], refs[2 * n:3 * n], refs[3 * n:4 * n]
        send, recv = refs[4 * n], refs[4 * n + 1]
        s, i = pl.program_id(0), pl.program_id(1)
        x, y, c, _ = _place()

        def copies(slot):
            return [_rcopy(whole[k].at[slot, :, pl.ds((1 - c) * HALF, HALF)], theirs[k].at[slot],
                           send.at[slot * n + k], recv.at[slot * n + k], (x, y, 1 - c)) for k in range(n)]

        @pl.when((s == 0) & (i == 0))
        def _():
            for slot in range(NSH):
                for cp in copies(slot):
                    cp.start()

        @pl.when(i == 0)
        def _():
            for slot in range(NSH):
                @pl.when(s == slot)
                def _():
                    for cp in copies(slot):
                        cp.wait()

        rows_i = slice(None) if th == rows else pl.ds(pl.multiple_of(i * th, th), th)
        for k in range(n):
            out[k][...] = (mine[k][...].astype(F32) + theirs[k][s, rows_i, :].astype(F32)).astype(BF16)

    spec = pl.BlockSpec((None, th, HALF), lambda s, i, cr: (s, i, 0))
    return pl.pallas_call(
        body, name=name,
        grid_spec=pltpu.PrefetchScalarGridSpec(
            num_scalar_prefetch=1, grid=(NSH, rows // th),
            in_specs=[pl.BlockSpec((None, th, HALF), lambda s, i, cr: (s, i, cr[0]))] * n + _any_specs(n),
            out_specs=[spec] * n,
            scratch_shapes=[pltpu.VMEM((NSH, rows, HALF), BF16)] * n
            + [pltpu.SemaphoreType.DMA((NSH * n,)), pltpu.SemaphoreType.DMA((NSH * n,))]),
        out_shape=[jax.ShapeDtypeStruct((NSH, rows, HALF), BF16)] * n,
        compiler_params=_cparams("arbitrary", "arbitrary"),
    )(c_idx, *ps, *ps)


def _pair_add(name, c_idx, ps, theirs, th):
    n = len(ps)
    _, rows, _ = ps[0].shape

    def body(c_ref, *refs):
        for k in range(n):
            refs[2 * n + k][...] = (refs[k][...].astype(F32) + refs[n + k][...].astype(F32)).astype(BF16)

    spec = pl.BlockSpec((None, th, HALF), lambda s, i, cr: (s, i, 0))
    return pl.pallas_call(
        body, name=name,
        grid_spec=pltpu.PrefetchScalarGridSpec(
            num_scalar_prefetch=1, grid=(NSH, rows // th),
            in_specs=[pl.BlockSpec((None, th, HALF), lambda s, i, cr: (s, i, cr[0]))] * n + [spec] * n,
            out_specs=[spec] * n),
        out_shape=[jax.ShapeDtypeStruct((NSH, rows, HALF), BF16)] * n,
        compiler_params=_cparams("parallel", "parallel"),
    )(c_idx, *ps, *theirs)


def _chip_sum(name, place, cs, ts, th):
    n = len(ts)
    _, rows, _ = ts[0].shape

    def body(p_ref, *refs):
        for i in range(n):
            t = refs[n + i][...].astype(F32)
            refs[2 * n + i][...] = ((refs[i][...].astype(F32) + t[0]) + t[1]) + t[2]

    return pl.pallas_call(
        body, name=name,
        grid_spec=pltpu.PrefetchScalarGridSpec(
            num_scalar_prefetch=1, grid=(rows // th,),
            in_specs=[pl.BlockSpec((None, th, HALF), lambda i, pr: (pr[0], i, 0))] * n
            + [pl.BlockSpec((3, th, HALF), lambda i, pr: (0, i, 0))] * n,
            out_specs=[pl.BlockSpec((th, HALF), lambda i, pr: (i, pr[1]))] * n),
        out_shape=[jax.ShapeDtypeStruct((rows, D), F32)] * n, compiler_params=_cparams("parallel"),
    )(place, *cs, *ts)


def _adamw(name, ws, gs, ms, vs, tr, tc):
    n = len(ws)
    shape = ws[0].shape
    rows, cols, mid = shape[0], shape[-1], shape[1:-1]
    c1 = 1.0 / (1.0 - ADAM_B1 ** ADAM_STEP)
    c2 = 1.0 / (1.0 - ADAM_B2 ** ADAM_STEP)

    def body(*refs):
        for i in range(n):
            w, g, m, v = (refs[k * n + i][...] for k in range(4))
            m2 = ADAM_B1 * m + (1.0 - ADAM_B1) * g
            v2 = ADAM_B2 * v + (1.0 - ADAM_B2) * (g * g)
            refs[4 * n + 4 * i][...] = -ADAM_LR * ((m2 * c1) / (jnp.sqrt(v2 * c2) + ADAM_EPS) + ADAM_WD * w)
            refs[4 * n + 4 * i + 1][...] = m2
            refs[4 * n + 4 * i + 2][...] = v2
            refs[4 * n + 4 * i + 3][...] = g

    spec = pl.BlockSpec((tr,) + mid + (tc,), lambda i, j: (i,) + (0,) * len(mid) + (j,))
    outs = pl.pallas_call(
        body, name=name, grid=(rows // tr, cols // tc), in_specs=[spec] * (4 * n), out_specs=[spec] * (4 * n),
        out_shape=[jax.ShapeDtypeStruct(shape, F32)] * (4 * n),
        compiler_params=_cparams("parallel", "parallel"),
    )(*ws, *gs, *ms, *vs)
    return [tuple(outs[4 * i:4 * i + 4]) for i in range(n)]


def _place():
    x, y, c = lax.axis_index("x"), lax.axis_index("y"), lax.axis_index("c")
    chips = [(1 - x, y), (x, 1 - y), (1 - x, 1 - y)]
    return x, y, c, chips


def _any_specs(n):
    return [pl.BlockSpec(memory_space=pl.ANY)] * n


def _rcopy(src, dst, send_sem, recv_sem, dev):
    return pltpu.make_async_remote_copy(src_ref=src, dst_ref=dst, send_sem=send_sem, recv_sem=recv_sem,
                                        device_id=dev, device_id_type=MESH)


QUARTER = HALF // 2

TO_X, TO_Y, RELAY_X, RELAY_Y, FWD_X, FWD_Y, FWD_D0, FWD_D1 = range(8)
TO_D = RELAY_X


def _gather_rider(bufs, views, relay):
    n = len(bufs)

    def plan(rout, sems):
        send, recv = sems
        x, y, c, _ = _place()
        me, sx, sy, sd = 2 * x + y, 2 * (1 - x) + y, 2 * x + (1 - y), 2 * (1 - x) + (1 - y)
        nx, ny, nd, sib = (1 - x, y, c), (x, 1 - y, c), (1 - x, 1 - y, c), (x, y, 1 - c)
        mine, other = c * HALF, (1 - c) * HALF
        out = {TO_X: (me, mine, HALF, nx), TO_Y: (me, mine, HALF, ny),
               FWD_X: (sx, mine, HALF, sib), FWD_Y: (sy, mine, HALF, sib)}
        inn = {TO_X: (sx, mine, HALF), TO_Y: (sy, mine, HALF),
               FWD_X: (sx, other, HALF), FWD_Y: (sy, other, HALF)}
        if relay:
            out.update({RELAY_X: (sy, mine, QUARTER, nx), RELAY_Y: (sx, mine + QUARTER, QUARTER, ny),
                        FWD_D0: (sd, mine, QUARTER, sib), FWD_D1: (sd, mine + QUARTER, QUARTER, sib)})
            inn.update({RELAY_X: (sd, mine, QUARTER), RELAY_Y: (sd, mine + QUARTER, QUARTER),
                        FWD_D0: (sd, other, QUARTER), FWD_D1: (sd, other + QUARTER, QUARTER)})
        else:
            out.update({TO_D: (me, mine, HALF, nd), FWD_D0: (sd, mine, HALF, sib)})
            inn.update({TO_D: (sd, mine, HALF), FWD_D0: (sd, other, HALF)})

        def copy(kind, b):
            slot, col, ncols, dev = out[kind]
            win = views[b](rout[b], slot, col, ncols)
            return _rcopy(win, win, send.at[kind * n + b], recv.at[kind * n + b], dev)

        def land(kind, b):
            slot, col, ncols = inn[kind]
            win = views[b](rout[b], slot, col, ncols)
            return _rcopy(win, win, send.at[kind * n + b], recv.at[kind * n + b], (x, y, c))

        return copy, land

    if relay:
        first = (TO_X, TO_Y)
        chain = ((TO_X, (FWD_X, RELAY_Y)), (TO_Y, (FWD_Y, RELAY_X)), (RELAY_X, (FWD_D0,)), (RELAY_Y, (FWD_D1,)))
    else:
        first = (TO_X, TO_Y, TO_D)
        chain = ((TO_X, (FWD_X,)), (TO_Y, (FWD_Y,)), (TO_D, (FWD_D0,)))
    forwards = [k for _, then in chain for k in then if k in (FWD_X, FWD_Y, FWD_D0, FWD_D1)]
    sent = list(first) + [k for _, then in chain for k in then]

    def start(rin, rout, sems):
        copy, _ = plan(rout, sems)
        for kind in first:
            for b in range(n):
                copy(kind, b).start()

    def finish(rin, rout, sems):
        copy, land = plan(rout, sems)
        for landed, then in chain:
            for b in range(n):
                land(landed, b).wait_recv()
                for kind in then:
                    copy(kind, b).start()
        for kind in forwards:
            for b in range(n):
                land(kind, b).wait_recv()
        for kind in sent:
            for b in range(n):
                copy(kind, b).wait_send()

    return _Rider(list(bufs), [jax.ShapeDtypeStruct(a.shape, a.dtype) for a in bufs], {b: b for b in range(n)},
                  [pltpu.SemaphoreType.DMA((8 * n,))] * 2, start, finish)


def _small_gather_rider(cw):
    def descs(rin, rout, sems, x, y, c, chips):
        return [_rcopy(rin[0], rout[0].at[2 * x + y], sems[1].at[j], sems[2].at[j], (chip[0], chip[1], c))
                for j, chip in enumerate(chips)]

    def start(rin, rout, sems):
        x, y, c, chips = _place()
        pltpu.make_async_copy(rin[0], rout[0].at[2 * x + y], sems[0].at[0]).start()
        for cp in descs(rin, rout, sems, x, y, c, chips):
            cp.start()

    def finish(rin, rout, sems):
        x, y, c, chips = _place()
        for j, chip in enumerate(chips):
            _rcopy(rin[0], rout[0].at[2 * chip[0] + chip[1]], sems[1].at[j], sems[2].at[j], (x, y, c)).wait_recv()
        for cp in descs(rin, rout, sems, x, y, c, chips):
            cp.wait_send()
        pltpu.make_async_copy(rin[0], rout[0].at[2 * x + y], sems[0].at[0]).wait()

    return _Rider([cw], [jax.ShapeDtypeStruct((NSH,) + cw.shape, cw.dtype)], {},
                  [pltpu.SemaphoreType.DMA((1,)), pltpu.SemaphoreType.DMA((3,)), pltpu.SemaphoreType.DMA((3,))],
                  start, finish)


def _to_sibling_rider(ps):
    n = len(ps)

    def descs(rin, rout, sems):
        x, y, c, _ = _place()
        return [_rcopy(rin[i].at[:, :, pl.ds((1 - c) * HALF, HALF)], rout[i], sems[0].at[i], sems[1].at[i],
                       (x, y, 1 - c)) for i in range(n)]

    def start(rin, rout, sems):
        for cp in descs(rin, rout, sems):
            cp.start()

    def finish(rin, rout, sems):
        for cp in descs(rin, rout, sems):
            cp.wait()

    return _Rider(list(ps), [jax.ShapeDtypeStruct(a.shape[:2] + (HALF,), a.dtype) for a in ps], {},
                  [pltpu.SemaphoreType.DMA((n,))] * 2, start, finish)


def _to_chips_rider(cs):
    n = len(cs)

    def descs(rin, rout, sems):
        x, y, c, chips = _place()
        return [_rcopy(rin[i].at[2 * chip[0] + chip[1]], rout[i].at[j], sems[0].at[j * n + i], sems[1].at[j * n + i],
                       (chip[0], chip[1], c)) for j, chip in enumerate(chips) for i in range(n)]

    def start(rin, rout, sems):
        for cp in descs(rin, rout, sems):
            cp.start()

    def finish(rin, rout, sems):
        for cp in descs(rin, rout, sems):
            cp.wait()

    return _Rider(list(cs), [jax.ShapeDtypeStruct((3,) + a.shape[1:], a.dtype) for a in cs], {},
                  [pltpu.SemaphoreType.DMA((3 * n,))] * 2, start, finish)


def _run_riders(name, riders):
    n_in = [len(r.operands) for r in riders]
    n_out = [len(r.out_shapes) for r in riders]
    n_sem = [len(r.sems) for r in riders]

    def body(*refs):
        parts, at = [], 0
        for counts in (n_in, n_out, n_sem):
            group = []
            for k in counts:
                group.append(refs[at:at + k])
                at += k
            parts.append(group)
        for i, r in enumerate(riders):
            r.start(parts[0][i], parts[1][i], parts[2][i])
        for i, r in enumerate(riders):
            r.finish(parts[0][i], parts[1][i], parts[2][i])

    aliases = {}
    for i, r in enumerate(riders):
        for k, v in r.aliases.items():
            aliases[sum(n_in[:i]) + k] = sum(n_out[:i]) + v
    res = pl.pallas_call(
        body, name=name, in_specs=_any_specs(sum(n_in)), out_specs=_any_specs(sum(n_out)),
        out_shape=[s for r in riders for s in r.out_shapes], input_output_aliases=aliases,
        scratch_shapes=[s for r in riders for s in r.sems],
    )(*[a for r in riders for a in r.operands])
    out, at = [], 0
    for k in n_out:
        out.append(list(res[at:at + k]))
        at += k
    return out


SMALL_ROWS = 16


def _swap_halves(gs, vec):
    n = len(gs)

    def body(*refs):
        v_ref, dst, o_ref = refs[n], refs[n + 1:2 * n + 1], refs[2 * n + 1]
        buf, send, recv, vsend, vrecv = refs[2 * n + 2:]
        x, y, c, _ = _place()
        cps = []
        for i in range(n):
            mine = dst[i].at[:, pl.ds(c * HALF, HALF)]
            cps.append(_rcopy(mine, mine, send.at[i], recv.at[i], (x, y, 1 - c)))
        for cp in cps:
            cp.start()

        me = 4 * x + 2 * y + c
        buf[me] = v_ref[...]
        vcps = []
        for k in range(1, 8):
            peer = (x ^ (k >> 2), y ^ ((k >> 1) & 1), c ^ (k & 1))
            vcps.append(_rcopy(v_ref, buf.at[me], vsend.at[k - 1], vrecv.at[k - 1], peer))
        for cp in vcps:
            cp.start()
        for k in range(1, 8):
            _rcopy(v_ref, buf.at[me ^ k], vsend.at[k - 1], vrecv.at[k - 1], (x, y, c)).wait_recv()
        for cp in vcps:
            cp.wait_send()
        t = buf[0]
        for d in range(1, 8):
            t = t + buf[d]
        o_ref[...] = t

        for i in range(n):
            other = dst[i].at[:, pl.ds((1 - c) * HALF, HALF)]
            _rcopy(other, other, send.at[i], recv.at[i], (x, y, c)).wait_recv()
        for cp in cps:
            cp.wait_send()

    vmem = pl.BlockSpec(memory_space=pltpu.VMEM)
    res = pl.pallas_call(
        body, name="grads_swap_halves", in_specs=_any_specs(n) + [vmem], out_specs=_any_specs(n) + [vmem],
        out_shape=[jax.ShapeDtypeStruct(g.shape, g.dtype) for g in gs] + [jax.ShapeDtypeStruct((SMALL_ROWS, D), F32)],
        input_output_aliases={i: i for i in range(n)},
        scratch_shapes=[pltpu.VMEM((8, SMALL_ROWS, D), F32)] + [pltpu.SemaphoreType.DMA((n,))] * 2
        + [pltpu.SemaphoreType.DMA((7,))] * 2,
    )(*gs, vec)
    return list(res[:n]), res[n]


def _col_window(ref, slot, col, ncols):
    return ref.at[slot, :, pl.ds(col, ncols)]


def _stack_window(first, count):
    def view(ref, slot, col, ncols):
        return ref.at[slot, pl.ds(first, count), :, pl.ds(col, ncols)]
    return view


def _row_tile(rows):
    for t in range(512, 15, -16):
        if rows % t == 0:
            return t
    return rows


def _same_shape_runs(arrs):
    runs, a = [], 0
    for b in range(1, len(arrs) + 1):
        if b == len(arrs) or arrs[b].shape != arrs[a].shape:
            runs.append((a, b))
            a = b
    return runs


class _Comm:
    def __init__(self):
        x, y, c = lax.axis_index("x"), lax.axis_index("y"), lax.axis_index("c")
        self.c_idx = jnp.reshape(c, (1,)).astype(jnp.int32)
        self.place = jnp.stack([2 * x + y, c]).astype(jnp.int32)
        self.groups = {}

    @staticmethod
    def gather(*bufs, relay, part=None):
        views = [_col_window if b.ndim == 3 else _stack_window(*(part or (0, b.shape[1]))) for b in bufs]
        return _gather_rider(list(bufs), views, relay)

    def reduce_rider(self, tag, names, ps, theirs=None):
        csums = []
        for a, b in _same_shape_runs(ps):
            name, th = "pair_sum_%s%d" % (tag, a), _row_tile(ps[a].shape[1])
            csums += (_pair_sum(name, self.c_idx, ps[a:b], th) if theirs is None else
                      _pair_add(name, self.c_idx, ps[a:b], theirs[a:b], th))
        self.groups[tag] = [names, csums, None]
        return _to_chips_rider(csums)

    def landed(self, tag, ts):
        self.groups[tag][2] = ts

    def finish(self, small):
        names, csums, ts = [], [], []
        for group_names, group_csums, group_ts in self.groups.values():
            names += group_names
            csums += group_csums
            ts += group_ts
        order = sorted(range(len(names)), key=lambda i: csums[i].shape[1])
        names, csums, ts = ([v[i] for i in order] for v in (names, csums, ts))
        halves = []
        for a, b in _same_shape_runs(csums):
            halves += _chip_sum("chip_sum_%d" % a, self.place, csums[a:b], ts[a:b], _row_tile(csums[a].shape[1]))
        grads, total = _swap_halves(halves, small)
        return dict(zip(names, grads)), total


ROPE_THETA = 10000.0
SMALL_1K = ("ffn1_pre_norm", "ffn1_post_norm", "mix_pre_norm", "ssm_norm", "mix_post_norm",
            "ffn2_pre_norm", "ffn2_post_norm")
SMALL_16 = ("dt_bias", "a_log", "d_skip")
OFF_CONVB = 7 * D
OFF_16 = OFF_CONVB + CONV_C
OFF_CONVW = OFF_16 + 48
OFF_LOSS = OFF_CONVW + CONV_K * CONV_C
SMALL_LEN = SMALL_ROWS * D


def _sds(shape, dtype):
    return jax.ShapeDtypeStruct(shape, dtype)


def _ridden(res, rider):
    return res if rider is not None else (res, None)


def _ffn_down(name, act, w, tail_of, rider=None):
    tail, o_specs, o_shapes = tail_of(TS)
    return _mm(name, [act, w.dn], NN, (S // TS,),
               [pl.BlockSpec((NSH, TS, FS), lambda i: (0, i, 0)),
                pl.BlockSpec((NSH, None, FS, D), lambda i: (0, w.d0, 0, 0))], o_specs, o_shapes, rider, tail)


def _ffn_dw(name, a, b, rider=None):
    return _mm(name, [a, b], TN, (NSH,),
               [pl.BlockSpec((None, S, FS), lambda s: (s, 0, 0)), pl.BlockSpec((S, D), lambda s: (0, 0))],
               pl.BlockSpec((None, FS, D), lambda s: (s, 0, 0)), _sds((NSH, FS, D), BF16), rider)


def _ffn_dn(name, dgate, dup, w, tail_of, rider=None):
    rows = TS // 2
    tail, o_specs, o_shapes = tail_of(rows)
    a2 = pl.BlockSpec((NSH, rows, FS), lambda i: (0, i, 0))
    return _mm(name, [dgate, w.gu, dup, w.gu], NN, (S // rows,),
               [a2, pl.BlockSpec((NSH, None, FS, D), lambda i: (0, w.g0, 0, 0)),
                a2, pl.BlockSpec((NSH, None, FS, D), lambda i: (0, w.g0 + 1, 0, 0))], o_specs, o_shapes, rider, tail)


def _out_proj_dx(dh, wout):
    def body(dh_ref, w_ref, dyn_ref, do_ref):
        dm = _dot(dh_ref[...], w_ref[...], NT)
        dyn_ref[...] = dm[:, D:]
        for b in range(TS // 128):
            for j, blk in enumerate(_rows_to_blocks(dm[128 * b:128 * (b + 1), :D])):
                do_ref[j, b] = blk.astype(BF16)

    return pl.pallas_call(
        body, name="out_proj_dx", grid=(S // TS,),
        in_specs=[pl.BlockSpec((TS, D), lambda i: (i, 0)), pl.BlockSpec((2 * D, D), lambda i: (0, 0))],
        out_specs=[pl.BlockSpec((TS, D), lambda i: (i, 0)),
                   pl.BlockSpec((NKV, TS // 128, HD, QROWS), lambda i: (0, i, 0, 0))],
        out_shape=[_sds((S, D), F32), _sds((NKV, NCH, HD, QROWS), BF16)], compiler_params=_cparams("parallel"),
    )(dh, wout)


def _heads(t, n):
    return t.reshape(S, n, HD).transpose(1, 0, 2)


def _pad128(v):
    return jnp.pad(v, ((0, 0), (0, 128 - v.shape[1])))


def _local_step(x, positions, tgt, sp, gu1, d1, f2, wint, wout, convw, comm=None):
    inv_freq = ROPE_THETA ** (-jnp.arange(0, HD, 2, dtype=F32) / HD)
    ang = positions.astype(F32)[:, None] * inv_freq
    ang = jnp.concatenate([ang, ang, ang, ang], axis=-1)
    cos, sin = jnp.cos(ang), jnp.sin(ang)
    dtb, alog = _pad128(sp["dt_bias"]), _pad128(sp["a_log"])
    dskip_l = jnp.repeat(sp["d_skip"], HD, axis=1)
    convb = sp["conv_b"]

    n1 = _prenorm("prenorm1", x, sp["ffn1_pre_norm"])
    rider = comm.gather(d1, relay=False) if comm else None
    (fg1, fu1, act1), got = _ridden(_ffn_up("ffn1_up", n1, _FfnW(gu1, 0, d1, 0), rider), rider)
    if comm:
        d1, = got
    w1 = _FfnW(gu1, 0, d1, 0)
    rider = comm.gather(wint, relay=True) if comm else None
    (h1, x1, n2), got = _ridden(_ffn_down(
        "ffn1_down", act1, w1,
        lambda rows: _tail_postres(rows, x, sp["ffn1_post_norm"], 0.5, sp["mix_pre_norm"]), rider), rider)
    if comm:
        wint, = got
    wint_pad = jnp.pad(wint.reshape(WIN_COLS, D), ((0, WIN_PAD - WIN_COLS), (0, 0)))

    pw = WIN_PAD // 3
    rider = comm.gather(f2, relay=False, part=(0, 1)) if comm else None
    proj, got = _ridden(_mm(
        "in_proj", [n2, wint_pad], NT, (S // TS, 3),
        [pl.BlockSpec((TS, D), lambda i, j: (i, 0)), pl.BlockSpec((pw, D), lambda i, j: (j, 0))],
        pl.BlockSpec((TS, pw), lambda i, j: (i, j)), _sds((S, WIN_PAD), F32), rider), rider)
    if comm:
        f2, = got
    qt = _rope_q(proj, cos, sin)
    k_rot, v_bf, kt, vt = _rope_kv(proj, cos, sin)
    kh, vh = _heads(k_rot, NKV), _heads(v_bf, NKV)
    bias = _bias_table()
    rider = comm.gather(f2, wout, relay=False, part=(1, 2)) if comm else None
    (ot, lse, attn), got = _ridden(_attn_fwd(qt, kh, vt, bias, rider), rider)
    if comm:
        f2, wout = got
    w2 = _FfnW(f2, 0, f2, 2)
    wout = wout.reshape(2 * D, D)
    xbc, conv_y = _conv_fwd(proj, convw, convb)
    y, yn, hprev = _ssd_fwd(xbc, proj, dtb, alog, dskip_l, sp["ssm_norm"])
    mixed = jnp.concatenate([attn, yn], axis=1)
    tail, o_specs, o_shapes = _tail_postres(TS, x1, sp["mix_post_norm"], 1.0, sp["ffn2_pre_norm"])
    h2, x2, n3 = _mm("out_proj", [mixed, wout], NN, (S // TS,),
                     [pl.BlockSpec((TS, 2 * D), lambda i: (i, 0)), pl.BlockSpec((2 * D, D), lambda i: (0, 0))],
                     o_specs, o_shapes, None, tail)

    fg2, fu2, act2 = _ffn_up("ffn2_up", n3, w2)
    dy, dh3, dp3, loss = _ffn_down(
        "ffn2_down", act2, w2, lambda rows: _tail_final(rows, x2, sp["ffn2_post_norm"], tgt, 0.5))

    dgate2, dup2 = _ffn_dact("ffn2_dact", dh3, w2, fg2, fu2)
    dws2 = [_ffn_dw("ffn2_dwg", dgate2, n3), _ffn_dw("ffn2_dwu", dup2, n3), _ffn_dw("ffn2_dwd", act2, dh3)]
    dx2, dh2, dg3, dp2 = _ffn_dn(
        "ffn2_dn", dgate2, dup2, w2,
        lambda rows: _tail_mid_bwd(rows, dy, x2, sp["ffn2_pre_norm"], h2, sp["mix_post_norm"], 1.0))

    dyn, dot_ = _out_proj_dx(dh2, wout)
    dwout = _mm("out_proj_dw", [mixed, dh2], TN, (2,),
                [pl.BlockSpec((S, D), lambda m: (0, m)), pl.BlockSpec((S, D), lambda m: (0, 0))],
                pl.BlockSpec((D, D), lambda m: (m, 0)), _sds((2 * D, D), BF16))
    dwout = dwout.reshape(NSH, 2 * D // NSH, D)

    def riding(tag, names, ps, call, theirs=None):
        rider = comm.reduce_rider(tag, names, ps, theirs) if comm else None
        res, got = _ridden(call(rider), rider)
        if comm:
            comm.landed(tag, got)
        return res

    rider = _to_sibling_rider(dws2 + [dwout]) if comm else None
    (dxbc, dproj, ddt, dssm, dsc), theirs = _ridden(
        _ssd_bwd(dyn, y, xbc, proj, hprev, dtb, alog, dskip_l, sp["ssm_norm"], rider), rider)
    dproj, dcw8, dcb = _conv_bwd(dxbc, conv_y, proj, convw, dproj)
    delta = _attn_delta(ot, dot_)
    dqt, dkh, dvh = riding("a", BIG[3:6] + ("w_out",), dws2 + [dwout], lambda rider: _attn_bwd(
        qt, kh, kt, vh, dot_, lse, delta, bias, rider), theirs)
    dproj = _rope_dq(dqt, cos, sin, dproj)
    dproj = _rope_dkv(dkh, dvh, cos, sin, dproj)
    dproj = lax.dynamic_update_slice(dproj, ddt, (0, COL_DT))
    dwint = _mm("in_proj_dw", [dproj, n2], TN, (3,),
                [pl.BlockSpec((S, pw), lambda j: (0, j)), pl.BlockSpec((S, D), lambda j: (0, 0))],
                pl.BlockSpec((pw, D), lambda j: (j, 0)), _sds((WIN_PAD, D), BF16))
    dwint = dwint[:WIN_COLS].reshape(NSH, WIN_SH, D)

    tail, o_specs, o_shapes = _tail_mid_bwd(TS, dx2, x1, sp["mix_pre_norm"], h1, sp["ffn1_post_norm"], 0.5)
    dx1, dh1, dg2, dp1 = riding("b", ("w_in",), [dwint], lambda rider: _mm(
        "in_proj_dx", [dproj, wint_pad], NN, (S // TS,),
        [pl.BlockSpec((TS, WIN_PAD), lambda i: (i, 0)), pl.BlockSpec((WIN_PAD, D), lambda i: (0, 0))],
        o_specs, o_shapes, rider, tail))

    dwd1 = _ffn_dw("ffn1_dwd", act1, dh1)
    dgate1, dup1 = riding("d", BIG[2:3], [dwd1], lambda rider: _ffn_dact("ffn1_dact", dh1, w1, fg1, fu1, rider))
    dwg1, dwu1 = _ffn_dw("ffn1_dwg", dgate1, n1), _ffn_dw("ffn1_dwu", dup1, n1)
    grad_x, dg1 = riding("g", BIG[0:2], [dwg1, dwu1], lambda rider: _ffn_dn(
        "ffn1_dn", dgate1, dup1, w1, lambda rows: _tail_first_bwd(rows, dx1, x, sp["ffn1_pre_norm"]), rider))
    dws1 = [dwg1, dwu1, dwd1]

    small = jnp.concatenate([
        dg1[0], dp1[0], dg2[0], dssm[0], dp2[0], dg3[0], dp3[0], dcb[0],
        dsc[0, :16], dsc[1, :16], dsc[2, :16], dcw8[:CONV_K].reshape(-1), loss[0, :1]])
    small = jnp.pad(small, (0, SMALL_LEN - small.shape[0])).reshape(SMALL_ROWS, D)
    if comm is None:
        return grad_x, dws1 + dws2 + [dwint, dwout], small
    return (grad_x,) + comm.finish(small)


WEIGHTS = ("ffn1_pre_norm", "ffn1_w_gate", "ffn1_w_up", "ffn1_w_down", "ffn1_post_norm", "mix_pre_norm", "w_in",
           "conv_w", "conv_b", "dt_bias", "a_log", "d_skip", "ssm_norm", "w_out", "mix_post_norm", "ffn2_pre_norm",
           "ffn2_w_gate", "ffn2_w_up", "ffn2_w_down", "ffn2_post_norm")
BIG = ("ffn1_w_gate", "ffn1_w_up", "ffn1_w_down", "ffn2_w_gate", "ffn2_w_up", "ffn2_w_down", "w_in", "w_out")
TRANSPOSED = ("ffn1_w_gate", "ffn1_w_up", "ffn2_w_gate", "ffn2_w_up", "w_in")
SMALL_ORDER = SMALL_1K + ("conv_b",) + SMALL_16
CONVW_SH = CONV_C // NSH


def _shard2d(t, name):
    return t[0].T if name in TRANSPOSED else t[0]


def _unshard2d(t, name):
    return (t.T if name in TRANSPOSED else t)[None]


def _rows3d(t):
    return t.transpose(2, 0, 1)


def _pack_small(d, prefix, shard_of_convw):
    flat = jnp.concatenate([d[prefix + n][0] for n in SMALL_ORDER] + [shard_of_convw.reshape(-1)])
    return jnp.pad(flat, (0, SMALL_LEN - flat.shape[0])).reshape(SMALL_ROWS, D)


def _unpack_small(block, like):
    flat = block.reshape(-1)
    out, off = {}, 0
    for n in SMALL_ORDER:
        size = like[n].shape[1]
        out[n] = flat[off:off + size].reshape(1, size)
        off += size
    out["conv_w"] = flat[off:off + CONV_K * CONVW_SH].reshape(1, CONV_K, CONVW_SH)
    return out


def kernel(x, positions, ffn1_pre_norm, ffn1_w_gate, ffn1_w_up, ffn1_w_down, ffn1_post_norm, mix_pre_norm, w_in, conv_w, conv_b, dt_bias, a_log, d_skip, ssm_norm, w_out, mix_post_norm, ffn2_pre_norm, ffn2_w_gate, ffn2_w_up, ffn2_w_down, ffn2_post_norm, loss_target, m_ffn1_pre_norm, m_ffn1_w_gate, m_ffn1_w_up, m_ffn1_w_down, m_ffn1_post_norm, m_mix_pre_norm, m_w_in, m_conv_w, m_conv_b, m_dt_bias, m_a_log, m_d_skip, m_ssm_norm, m_w_out, m_mix_post_norm, m_ffn2_pre_norm, m_ffn2_w_gate, m_ffn2_w_up, m_ffn2_w_down, m_ffn2_post_norm, v_ffn1_pre_norm, v_ffn1_w_gate, v_ffn1_w_up, v_ffn1_w_down, v_ffn1_post_norm, v_mix_pre_norm, v_w_in, v_conv_w, v_conv_b, v_dt_bias, v_a_log, v_d_skip, v_ssm_norm, v_w_out, v_mix_post_norm, v_ffn2_pre_norm, v_ffn2_w_gate, v_ffn2_w_up, v_ffn2_w_down, v_ffn2_post_norm):
    given = dict(locals())
    xi, yi = lax.axis_index("x"), lax.axis_index("y")

    shard = jnp.reshape(2 * xi + yi, (1,)).astype(jnp.int32)
    big = {p + n: _shard2d(given[p + n], n) for n in BIG for p in ("", "m_", "v_")}
    gu1 = _cast_stack("cast_ffn1_gate_up", shard, [big[n] for n in BIG[0:2]], 176, D)
    d1 = _cast_stack("cast_ffn1_down", shard, [big[BIG[2]]], 176, D)
    f2 = _cast_stack("cast_ffn2", shard, [big[n] for n in BIG[3:6]], 176, D)
    winsh = _cast_stack("cast_w_in", shard, [big["w_in"]], WIN_SH, 256).reshape(NSH, WIN_SH, D)
    woutsh = _cast_stack("cast_w_out", shard, [big["w_out"]], 256, D).reshape(NSH, 2 * D // NSH, D)
    comm = _Comm()
    (gu1,), (cwf,) = _run_riders("gather_ffn1_gate_up", [comm.gather(gu1, relay=True), _small_gather_rider(conv_w[0])])
    convw = cwf.transpose(1, 0, 2).reshape(CONV_K, CONV_C)

    sp = {n: given[n] for n in SMALL_ORDER}
    grad_x, big_grads, small = _local_step(x[0], positions[0], loss_target[0], sp, gu1, d1, f2, winsh, woutsh,
                                           convw, comm)

    tot = small.reshape(-1)
    loss = tot[OFF_LOSS]
    small_grads, off = {}, 0
    for n in SMALL_ORDER:
        size = given[n].shape[1]
        small_grads[n] = tot[off:off + size].reshape(1, size)
        off += size
    dconvw = tot[OFF_CONVW:OFF_CONVW + CONV_K * CONV_C].reshape(CONV_K, NSH, CONVW_SH)
    dconvw = lax.dynamic_index_in_dim(dconvw, 2 * xi + yi, axis=1, keepdims=False)
    small_grads["conv_w"] = dconvw.reshape(1, CONV_K, CONVW_SH)

    upd = {}
    for names, tr in ((BIG[0:3], 176), (BIG[3:6], 176), (BIG[7:8], 256)):
        res = _adamw("adamw_" + names[0], [big[n] for n in names], [big_grads[n] for n in names],
                     [big["m_" + n] for n in names], [big["v_" + n] for n in names], tr, D)
        for n, r in zip(names, res):
            upd[n] = tuple(_unshard2d(t, n) for t in r)
    g_win = big_grads["w_in"].reshape(WIN_SH, 1, D)
    res, = _adamw("adamw_w_in", [_rows3d(w_in)], [g_win], [_rows3d(m_w_in)], [_rows3d(v_w_in)], WIN_SH // 4, D)
    upd["w_in"] = tuple(t.transpose(1, 2, 0) for t in res)
    (dl, m2, v2, _), = _adamw(
        "adamw_small", [_pack_small(given, "", conv_w[0])], [_pack_small(small_grads, "", dconvw)],
        [_pack_small(given, "m_", m_conv_w[0])], [_pack_small(given, "v_", v_conv_w[0])], SMALL_ROWS, D)
    dl, m2, v2 = (_unpack_small(t, given) for t in (dl, m2, v2))
    for n in SMALL_ORDER + ("conv_w",):
        upd[n] = (dl[n], m2[n], v2[n], small_grads[n])

    return (loss, grad_x[None], *[upd[n][3] for n in WEIGHTS], *[upd[n][0] for n in WEIGHTS],
            *[upd[n][1] for n in WEIGHTS], *[upd[n][2] for n in WEIGHTS])
```

```python
import functools
import typing

import jax
import jax.numpy as jnp
from jax import lax
from jax.experimental import pallas as pl
from jax.experimental.pallas import tpu as pltpu

F32 = jnp.float32
BF16 = jnp.bfloat16

S = 2048
D = 1024
FF = 2816
NSH = 4
FS = FF // NSH
HALF = D // 2
HD = 64
NKV = 4
NQ_PER_KV = 4
KVW = NKV * HD
QCOLS = NQ_PER_KV * HD
CONV_C = 1536
CONV_K = 4
SSM_W = 1024
NST = 128
NCH = S // 128
WIN_COLS = 4112
WIN_SH = WIN_COLS // NSH
WIN_PAD = 4224
COL_DT = 4096
EPS = 1e-6
NEG = -1e30

ADAM_LR = 0.001
ADAM_B1 = 0.9
ADAM_B2 = 0.999
ADAM_EPS = 1e-08
ADAM_WD = 0.01
ADAM_STEP = 10

VMEM_LIMIT = 56 * 1024 * 1024
TS = 512
TR = 256

NN = (((1,), (0,)), ((), ()))
NT = (((1,), (1,)), ((), ()))
TN = (((0,), (0,)), ((), ()))
MESH = pl.DeviceIdType.MESH


def _cparams(*sem):
    return pltpu.CompilerParams(dimension_semantics=sem, vmem_limit_bytes=VMEM_LIMIT)


def _dot(a, b, dims):
    return lax.dot_general(a.astype(BF16), b.astype(BF16), dims, preferred_element_type=F32)


def _bf16_pieces(v):
    hi = v.astype(BF16)
    rest = v - hi.astype(F32)
    mid = rest.astype(BF16)
    return hi, mid, (rest - mid.astype(F32)).astype(BF16)


def _dot_exact(a, b, ones="a"):
    if ones == "a":
        sel = a.astype(BF16)
        parts = [lax.dot_general(sel, p, NN, preferred_element_type=F32) for p in _bf16_pieces(b)]
    else:
        sel = b.astype(BF16)
        parts = [lax.dot_general(p, sel, NN, preferred_element_type=F32) for p in _bf16_pieces(a)]
    return (parts[2] + parts[1]) + parts[0]


def _sigmoid(v):
    return 1.0 / (1.0 + jnp.exp(-v))


class _Rider(typing.NamedTuple):
    operands: list
    out_shapes: list
    aliases: dict
    sems: list
    start: typing.Callable
    finish: typing.Callable


def _call(body, name, grid, in_specs, out_specs, out_shape, operands, scratch=(), sem=(), rider=None):
    multi = isinstance(out_shape, (list, tuple))
    if rider is None:
        return pl.pallas_call(
            body, name=name, grid=grid, in_specs=in_specs, out_specs=out_specs, out_shape=out_shape,
            scratch_shapes=list(scratch), compiler_params=_cparams(*sem))(*operands)
    outs = list(out_shape) if multi else [out_shape]
    ospecs = list(out_specs) if multi else [out_specs]
    n_in, n_out, n_scr = len(operands), len(outs), len(scratch)
    ri, ro = len(rider.operands), len(rider.out_shapes)

    def wrapped(*refs):
        o0 = n_in + ri
        s0 = o0 + n_out + ro
        rin, rout, rsem = refs[n_in:o0], refs[o0 + n_out:s0], refs[s0 + n_scr:]
        ids = [pl.program_id(a) for a in range(len(grid))]
        first = functools.reduce(jnp.logical_and, [i == 0 for i in ids])
        last = functools.reduce(jnp.logical_and, [i == g - 1 for i, g in zip(ids, grid)])

        @pl.when(first)
        def _():
            rider.start(rin, rout, rsem)

        body(*refs[:n_in], *refs[o0:o0 + n_out], *refs[s0:s0 + n_scr])

        @pl.when(last)
        def _():
            rider.finish(rin, rout, rsem)

    hbm = pl.BlockSpec(memory_space=pl.ANY)
    res = pl.pallas_call(
        wrapped, name=name, grid=grid, in_specs=list(in_specs) + [hbm] * ri, out_specs=ospecs + [hbm] * ro,
        out_shape=outs + list(rider.out_shapes), scratch_shapes=list(scratch) + list(rider.sems),
        input_output_aliases={n_in + k: n_out + v for k, v in rider.aliases.items()},
        compiler_params=_cparams(*(("arbitrary",) * len(grid))))(*operands, *rider.operands)
    main = list(res[:n_out])
    return (main if multi else main[0]), list(res[n_out:])


class _Tail(typing.NamedTuple):
    fn: typing.Callable
    operands: list
    in_specs: list


def _mm(name, operands, dims, grid, in_specs, o_spec, out_shape, rider=None, tail=None):
    npairs = len(operands) // 2
    extra = [] if tail is None else list(tail.operands)
    nin = 2 * npairs + len(extra)

    def body(*refs):
        t = None
        for i in range(npairs):
            a, b = refs[2 * i], refs[2 * i + 1]
            parts = [(a[s], b[s]) for s in range(a.shape[0])] if len(a.shape) == 3 else [(a[...], b[...])]
            for pa, pb in parts:
                d = _dot(pa, pb, dims)
                t = d if t is None else t + d
        if tail is None:
            refs[nin][...] = t.astype(refs[nin].dtype)
        else:
            tail.fn(t, refs[2 * npairs:nin], refs[nin:])

    sem = ("parallel" if tail is None else "arbitrary",) * len(grid)
    specs = list(in_specs) + ([] if tail is None else list(tail.in_specs))
    return _call(body, name, grid, specs, o_spec, out_shape, list(operands) + extra, (), sem, rider)


class _FfnW(typing.NamedTuple):
    gu: jax.Array
    g0: int
    dn: jax.Array
    d0: int


def _ffn_up(name, n, w, rider=None):
    def body(n_ref, wg_ref, wu_ref, fg_ref, fu_ref, a_ref):
        nb = n_ref[...]
        g = _dot(nb, wg_ref[...], NT)
        u = _dot(nb, wu_ref[...], NT)
        sg = _sigmoid(g)
        silu = g * sg
        fg_ref[...] = (u * (sg * (1.0 + g * (1.0 - sg)))).astype(BF16)
        fu_ref[...] = silu.astype(BF16)
        a_ref[...] = (silu * u).astype(BF16)

    out = jax.ShapeDtypeStruct((NSH, S, FS), BF16)
    ospec = pl.BlockSpec((None, TS, FS), lambda s, i: (s, i, 0))
    return _call(
        body, name, (NSH, S // TS),
        [pl.BlockSpec((TS, D), lambda s, i: (i, 0)),
         pl.BlockSpec((None, None, FS, D), lambda s, i: (s, w.g0, 0, 0)),
         pl.BlockSpec((None, None, FS, D), lambda s, i: (s, w.g0 + 1, 0, 0))],
        [ospec, ospec, ospec], [out, out, out], (n, w.gu, w.gu), sem=("parallel", "parallel"), rider=rider)


def _ffn_dact(name, dh, w, fgate, fup, rider=None):
    def body(dh_ref, wd_ref, fg_ref, fu_ref, dg_ref, du_ref):
        da = _dot(dh_ref[...], wd_ref[...], NT)
        dg_ref[...] = (da * fg_ref[...].astype(F32)).astype(BF16)
        du_ref[...] = (da * fu_ref[...].astype(F32)).astype(BF16)

    out = jax.ShapeDtypeStruct((NSH, S, FS), BF16)
    aspec = pl.BlockSpec((None, TS, FS), lambda s, i: (s, i, 0))
    return _call(
        body, name, (NSH, S // TS),
        [pl.BlockSpec((TS, D), lambda s, i: (i, 0)),
         pl.BlockSpec((None, None, FS, D), lambda s, i: (s, w.d0, 0, 0)), aspec, aspec],
        [aspec, aspec], [out, out], (dh, w.dn, fgate, fup), sem=("parallel", "parallel"), rider=rider)


def _rstd(v):
    return lax.rsqrt(jnp.mean(v * v, axis=-1, keepdims=True) + EPS)


def _row_spec():
    return pl.BlockSpec((TR, D), lambda i: (i, 0))


def _vec_spec():
    return pl.BlockSpec((1, D), lambda i: (0, 0))


def _acc_rows(ref, v):
    @pl.when(pl.program_id(0) == 0)
    def _():
        ref[...] = jnp.zeros_like(ref)
    ref[...] += jnp.sum(v, axis=0, keepdims=True)


def _prenorm(name, x, g, rider=None):
    def body(x_ref, g_ref, n_ref):
        xv = x_ref[...]
        n_ref[...] = (xv * _rstd(xv) * g_ref[...]).astype(BF16)

    return _call(body, name, (S // TR,), [_row_spec(), _vec_spec()], _row_spec(),
                 jax.ShapeDtypeStruct((S, D), BF16), (x, g), sem=("parallel",), rider=rider)


def _rows_spec(rows):
    return pl.BlockSpec((rows, D), lambda i: (i, 0))


def _rows_f32():
    return jax.ShapeDtypeStruct((S, D), F32)


def _rows_bf16():
    return jax.ShapeDtypeStruct((S, D), BF16)


def _vec_f32():
    return jax.ShapeDtypeStruct((1, D), F32)


def _tail_postres(rows, x, p, alpha, gnext):
    def fn(h, ins, outs):
        x_ref, p_ref, g_ref = ins
        h_ref, xo_ref, n_ref = outs
        h_ref[...] = h
        xo = x_ref[...] + alpha * (h * _rstd(h) * p_ref[...])
        xo_ref[...] = xo
        n_ref[...] = (xo * _rstd(xo) * g_ref[...]).astype(BF16)

    rs = _rows_spec(rows)
    return (_Tail(fn, [x, p, gnext], [rs, _vec_spec(), _vec_spec()]), [rs, rs, rs],
            [_rows_f32(), _rows_f32(), _rows_bf16()])


def _tail_final(rows, x, p, tgt, alpha):
    def fn(h, ins, outs):
        x_ref, p_ref, t_ref = ins
        dy_ref, dh_ref, dp_ref, loss_ref = outs
        r = _rstd(h)
        hn = h * r
        pv = p_ref[...]
        e = x_ref[...] + alpha * (hn * pv) - t_ref[...]
        dy = e * (1.0 / D)
        dy_ref[...] = dy
        du = alpha * dy * pv
        dh_ref[...] = (r * (du - hn * jnp.mean(du * hn, axis=-1, keepdims=True))).astype(BF16)
        _acc_rows(dp_ref, alpha * dy * hn)
        part = 0.5 * jnp.sum(jnp.mean(e * e, axis=-1, keepdims=True), axis=0, keepdims=True)
        _acc_rows(loss_ref, jnp.broadcast_to(part, (1, 128)))

    rs = _rows_spec(rows)
    return (_Tail(fn, [x, p, tgt], [rs, _vec_spec(), rs]),
            [rs, rs, _vec_spec(), pl.BlockSpec((1, 128), lambda i: (0, 0))],
            [_rows_f32(), _rows_bf16(), _vec_f32(), jax.ShapeDtypeStruct((1, 128), F32)])


def _norm_bwd(dn, xv, g_ref, dg_ref):
    r = _rstd(xv)
    xn = xv * r
    dng = dn * g_ref[...]
    _acc_rows(dg_ref, dn * xn)
    return r * (dng - xn * jnp.mean(dng * xn, axis=-1, keepdims=True))


def _tail_mid_bwd(rows, dres, x, g, h, p, alpha):
    def fn(dn, ins, outs):
        dr_ref, x_ref, g_ref, h_ref, p_ref = ins
        dx_ref, dh_ref, dg_ref, dp_ref = outs
        dx = dr_ref[...] + _norm_bwd(dn, x_ref[...], g_ref, dg_ref)
        dx_ref[...] = dx
        hv = h_ref[...]
        r = _rstd(hv)
        hn = hv * r
        du = alpha * dx * p_ref[...]
        dh_ref[...] = (r * (du - hn * jnp.mean(du * hn, axis=-1, keepdims=True))).astype(BF16)
        _acc_rows(dp_ref, alpha * dx * hn)

    rs = _rows_spec(rows)
    return (_Tail(fn, [dres, x, g, h, p], [rs, rs, _vec_spec(), rs, _vec_spec()]),
            [rs, rs, _vec_spec(), _vec_spec()], [_rows_f32(), _rows_bf16(), _vec_f32(), _vec_f32()])


def _tail_first_bwd(rows, dres, x, g):
    def fn(dn, ins, outs):
        dr_ref, x_ref, g_ref = ins
        dx_ref, dg_ref = outs
        dx_ref[...] = dr_ref[...] + _norm_bwd(dn, x_ref[...], g_ref, dg_ref)

    rs = _rows_spec(rows)
    return (_Tail(fn, [dres, x, g], [rs, rs, _vec_spec()]), [rs, _vec_spec()], [_rows_f32(), _vec_f32()])


def _rotate(t, c128, s128, sign, scale):
    width = t.shape[1]
    c = jnp.tile(c128, (1, width // 128))
    sn = jnp.tile(s128, (1, width // 128))
    lane = lax.broadcasted_iota(jnp.int32, t.shape, 1) & (HD - 1)
    rot = jnp.where(lane < HD // 2, -pltpu.roll(t, width - HD // 2, 1), pltpu.roll(t, HD // 2, 1))
    return (t * c + sign * (rot * sn)) * scale


def _rows_to_blocks(y):
    out = []
    for j in range(NKV):
        yt = y[:, QCOLS * j:QCOLS * (j + 1)].T
        out.append(jnp.concatenate([yt[HD * g:HD * (g + 1)] for g in range(NQ_PER_KV)], axis=1))
    return out


def _blocks_to_rows(blocks):
    cols = []
    for b in blocks:
        stacked = jnp.concatenate([b[:, 128 * g:128 * (g + 1)] for g in range(NQ_PER_KV)], axis=0)
        cols.append(stacked.T)
    return jnp.concatenate(cols, axis=1)


def _rope_q(proj, cos, sin):
    def body(t_ref, c_ref, s_ref, o_ref):
        y = _rotate(t_ref[...], c_ref[...], s_ref[...], 1.0, HD ** -0.5)
        for j, blk in enumerate(_rows_to_blocks(y)):
            o_ref[j] = blk.astype(BF16)

    return pl.pallas_call(
        body, name="rope_q", grid=(NCH,),
        in_specs=[pl.BlockSpec((128, D), lambda i: (i, 0)),
                  pl.BlockSpec((128, 128), lambda i: (i, 0)), pl.BlockSpec((128, 128), lambda i: (i, 0))],
        out_specs=pl.BlockSpec((NKV, None, HD, QROWS), lambda i: (0, i, 0, 0)),
        out_shape=jax.ShapeDtypeStruct((NKV, NCH, HD, QROWS), BF16), compiler_params=_cparams("parallel"),
    )(proj, cos, sin)


def _rope_dq(dqt, cos, sin, dproj):
    def body(t_ref, c_ref, s_ref, buf_ref, o_ref):
        t = _blocks_to_rows([t_ref[j] for j in range(NKV)])
        o_ref[...] = _rotate(t, c_ref[...], s_ref[...], -1.0, HD ** -0.5).astype(BF16)

    return pl.pallas_call(
        body, name="rope_dq", grid=(NCH,),
        in_specs=[pl.BlockSpec((NKV, None, HD, QROWS), lambda i: (0, i, 0, 0)),
                  pl.BlockSpec((128, 128), lambda i: (i, 0)), pl.BlockSpec((128, 128), lambda i: (i, 0)),
                  pl.BlockSpec(memory_space=pl.ANY)],
        out_specs=pl.BlockSpec((128, D), lambda i: (i, 0)),
        out_shape=jax.ShapeDtypeStruct(dproj.shape, BF16), input_output_aliases={3: 0},
        compiler_params=_cparams("parallel"),
    )(dqt, cos, sin, dproj)


def _rope_dkv(dkt, dvt, cos, sin, dproj):
    def body(k_ref, v_ref, c_ref, s_ref, buf_ref, o_ref):
        dk = jnp.concatenate([k_ref[j] for j in range(NKV)], axis=0).T
        dv = jnp.concatenate([v_ref[j] for j in range(NKV)], axis=0).T
        dk = _rotate(dk, c_ref[...], s_ref[...], -1.0, 1.0)
        o_ref[...] = jnp.concatenate([dk, dv], axis=1).astype(BF16)

    tspec = pl.BlockSpec((NKV, HD, 128), lambda i: (0, 0, i))
    return pl.pallas_call(
        body, name="rope_dkv", grid=(NCH,),
        in_specs=[tspec, tspec, pl.BlockSpec((128, 128), lambda i: (i, 0)), pl.BlockSpec((128, 128), lambda i: (i, 0)),
                  pl.BlockSpec(memory_space=pl.ANY)],
        out_specs=pl.BlockSpec((128, 2 * KVW), lambda i: (i, D // (2 * KVW))),
        out_shape=jax.ShapeDtypeStruct(dproj.shape, BF16), input_output_aliases={4: 0},
        compiler_params=_cparams("parallel"),
    )(dkt, dvt, cos, sin, dproj)


def _rope_kv(proj, cos, sin):
    def body(t_ref, c_ref, s_ref, k_ref, v_ref, kt_ref, vt_ref):
        t = t_ref[...]
        k = _rotate(t[:, :KVW], c_ref[...], s_ref[...], 1.0, 1.0).astype(BF16)
        v = t[:, KVW:].astype(BF16)
        k_ref[...] = k
        v_ref[...] = v
        kt, vt = k.astype(F32).T, v.astype(F32).T
        for j in range(NKV):
            kt_ref[j] = kt[HD * j:HD * (j + 1)].astype(BF16)
            vt_ref[j] = vt[HD * j:HD * (j + 1)].astype(BF16)

    rows = pl.BlockSpec((128, KVW), lambda i: (i, 0))
    tspec = pl.BlockSpec((NKV, HD, 128), lambda i: (0, 0, i))
    return pl.pallas_call(
        body, name="rope_kv", grid=(NCH,),
        in_specs=[pl.BlockSpec((128, 2 * KVW), lambda i: (i, D // (2 * KVW))),
                  pl.BlockSpec((128, 128), lambda i: (i, 0)), pl.BlockSpec((128, 128), lambda i: (i, 0))],
        out_specs=[rows, rows, tspec, tspec],
        out_shape=[jax.ShapeDtypeStruct((S, KVW), BF16)] * 2 + [jax.ShapeDtypeStruct((NKV, HD, S), BF16)] * 2,
        compiler_params=_cparams("parallel"),
    )(proj, cos, sin)


QROWS = NQ_PER_KV * 128


NBIAS = NCH + 1
KV_PER_STEP = 4


def _bias_table():
    db = lax.broadcasted_iota(jnp.int32, (NBIAS, 128, QROWS), 0) - 1
    ki = lax.broadcasted_iota(jnp.int32, (NBIAS, 128, QROWS), 1)
    qi = lax.broadcasted_iota(jnp.int32, (NBIAS, 128, QROWS), 2) & 127
    d = db * 128 + qi - ki
    cnt = ((d <= 128).astype(F32) + (((d & 3) == 0) & (d <= 512)).astype(F32) + ((d & 15) == 0).astype(F32))
    return jnp.where((d >= 0) & (cnt > 0.0), jnp.log(jnp.maximum(cnt, 1.0)), NEG)


def _qt_spec():
    return pl.BlockSpec((None, None, HD, QROWS), lambda j, i: (j, i, 0, 0))


def _stat_spec():
    return pl.BlockSpec((None, None, 1, QROWS), lambda j, i: (j, i, 0, 0))


def _attn_fwd(qt, kh, vt, bias, rider=None):
    def body(q_ref, k_ref, v_ref, b_ref, o_ref, lse_ref, rows_ref, m_ref, l_ref, acc_ref):
        qb = pl.program_id(1)
        m_ref[...] = jnp.full_like(m_ref, NEG)
        l_ref[...] = jnp.zeros_like(l_ref)
        acc_ref[...] = jnp.zeros_like(acc_ref)

        def keys(off, size, bias_):
            for h in range(KV_PER_STEP):
                m = m_ref[h]
                s = _dot(k_ref[h, pl.ds(off, size), :], q_ref[h], NN) + bias_
                m_new = jnp.maximum(m, jnp.max(s, axis=0, keepdims=True))
                p = jnp.exp(s - m_new)
                a = jnp.exp(m - m_new)
                m_ref[h] = m_new
                l_ref[h] = a * l_ref[h] + jnp.sum(p, axis=0, keepdims=True)
                acc_ref[h] = a * acc_ref[h] + _dot(v_ref[h, :, pl.ds(off, size)], p, NN)

        def blocks(first, count):
            bias_ = jnp.concatenate([b_ref[qb - first - j + 1] for j in range(count)], axis=0)
            keys(pl.multiple_of(first * 128, 128), 128 * count, bias_)

        nkb = qb + 1
        @pl.loop(0, nkb // 4)
        def _(i):
            blocks(4 * i, 4)

        @pl.when(nkb % 4 >= 2)
        def _():
            blocks(nkb // 4 * 4, 2)

        @pl.when(nkb % 2 == 1)
        def _():
            blocks(qb, 1)

        outs = []
        for h in range(KV_PER_STEP):
            outs.append(acc_ref[h] / l_ref[h])
            o_ref[h] = outs[h]
            lse_ref[h] = m_ref[h] + jnp.log(l_ref[h])
        rows_ref[...] = _blocks_to_rows(outs).astype(BF16)

    kvs = KV_PER_STEP
    qspec = pl.BlockSpec((kvs, None, HD, QROWS), lambda j, i: (j, i, 0, 0))
    return _call(
        body, "attn_fwd", (NKV // kvs, NCH),
        [qspec, pl.BlockSpec((kvs, S, HD), lambda j, i: (j, 0, 0)),
         pl.BlockSpec((kvs, HD, S), lambda j, i: (j, 0, 0)),
         pl.BlockSpec((NBIAS, 128, QROWS), lambda j, i: (0, 0, 0))],
        [qspec, pl.BlockSpec((kvs, None, 1, QROWS), lambda j, i: (j, i, 0, 0)),
         pl.BlockSpec((128, QCOLS * kvs), lambda j, i: (i, j))],
        [jax.ShapeDtypeStruct((NKV, NCH, HD, QROWS), F32), jax.ShapeDtypeStruct((NKV, NCH, 1, QROWS), F32),
         jax.ShapeDtypeStruct((S, D), BF16)],
        (qt, kh, vt, bias),
        [pltpu.VMEM((kvs, 1, QROWS), F32), pltpu.VMEM((kvs, 1, QROWS), F32), pltpu.VMEM((kvs, HD, QROWS), F32)],
        ("parallel", "parallel"), rider)


def _attn_delta(ot, dot_):
    def body(o_ref, do_ref, dl_ref):
        dl_ref[...] = jnp.sum(o_ref[...] * do_ref[...].astype(F32), axis=1, keepdims=True)

    spec = pl.BlockSpec((None, NCH, HD, QROWS), lambda j: (j, 0, 0, 0))
    return pl.pallas_call(
        body, name="attn_delta", grid=(NKV,), in_specs=[spec, spec],
        out_specs=pl.BlockSpec((None, NCH, 1, QROWS), lambda j: (j, 0, 0, 0)),
        out_shape=jax.ShapeDtypeStruct((NKV, NCH, 1, QROWS), F32), compiler_params=_cparams("parallel"),
    )(ot, dot_)


def _attn_bwd(qt, kh, kt, vh, dot_, lse, delta, bias, rider=None):
    def body(qt_ref, k_ref, kt_ref, v_ref, dot_ref, lse_ref, dl_ref, b_ref, dq_ref, dk_ref, dv_ref):
        kp = pl.program_id(1)

        @pl.when(kp == 0)
        def _():
            dq_ref[...] = jnp.zeros_like(dq_ref)

        dk_ref[...] = jnp.zeros_like(dk_ref)
        dv_ref[...] = jnp.zeros_like(dv_ref)

        @pl.loop(2 * kp, NCH // 2)
        def _(j):
            for h in range(KV_PER_STEP):
                k, kt_, v = k_ref[h], kt_ref[h], v_ref[h]
                for qb in (2 * j, 2 * j + 1):
                    bias2 = jnp.concatenate([b_ref[jnp.maximum(qb - 4 * kp - t + 1, 0)] for t in range(4)], axis=0)
                    st = _dot(k, qt_ref[h, qb], NN) + bias2
                    pt = jnp.exp(st - lse_ref[h, qb])
                    dst = pt * (_dot(v, dot_ref[h, qb], NN) - dl_ref[h, qb])
                    dq_ref[h, qb] += _dot(kt_, dst, NN)
                    dk_ref[h] += _dot(qt_ref[h, qb], dst, NT)
                    dv_ref[h] += _dot(dot_ref[h, qb], pt, NT)

    kvs = KV_PER_STEP
    tspec = pl.BlockSpec((kvs, NCH, HD, QROWS), lambda j, i: (j, 0, 0, 0))
    kspec = pl.BlockSpec((kvs, 512, HD), lambda j, i: (j, i, 0))
    ktspec = pl.BlockSpec((kvs, HD, 512), lambda j, i: (j, 0, i))
    sspec = pl.BlockSpec((kvs, NCH, 1, QROWS), lambda j, i: (j, 0, 0, 0))
    return _call(
        body, "attn_bwd", (NKV // kvs, NCH // 4),
        [tspec, kspec, ktspec, kspec, tspec, sspec, sspec,
         pl.BlockSpec((NBIAS, 128, QROWS), lambda j, i: (0, 0, 0))],
        [tspec, ktspec, ktspec],
        [jax.ShapeDtypeStruct((NKV, NCH, HD, QROWS), F32),
         jax.ShapeDtypeStruct((NKV, HD, S), F32), jax.ShapeDtypeStruct((NKV, HD, S), F32)],
        (qt, kh, kt, vh, dot_, lse, delta, bias), sem=("parallel", "arbitrary"), rider=rider)


CONV_BLK = 256
CONV_COL0 = 1536 // CONV_BLK


CONV_ROWS = 128


def _conv_fwd(proj, convw, convb):
    trips = S // CONV_ROWS

    def body(u_ref, w_ref, b_ref, o_ref, y_ref):
        @pl.loop(0, trips)
        def _(c):
            t0 = pl.multiple_of(c * CONV_ROWS, CONV_ROWS)
            before = pl.multiple_of(jnp.maximum(t0 - 8, 0), 8)
            ext = jnp.concatenate([jnp.where(c == 0, 0.0, u_ref[pl.ds(before, 8), :]),
                                   u_ref[pl.ds(t0, CONV_ROWS), :]], axis=0)
            y = b_ref[...] + w_ref[CONV_K - 1:CONV_K, :] * ext[8:]
            for j in range(1, CONV_K):
                y = y + w_ref[CONV_K - 1 - j:CONV_K - j, :] * pltpu.roll(ext, j, 0)[8:]
            y_ref[pl.ds(t0, CONV_ROWS), :] = y
            o_ref[pl.ds(t0, CONV_ROWS), :] = y * _sigmoid(y)

    out = pl.BlockSpec((S, CONV_BLK), lambda i: (0, i))
    return pl.pallas_call(
        body, name="conv_fwd", grid=(CONV_C // CONV_BLK,),
        in_specs=[pl.BlockSpec((S, CONV_BLK), lambda i: (0, CONV_COL0 + i)),
                  pl.BlockSpec((CONV_K, CONV_BLK), lambda i: (0, i)),
                  pl.BlockSpec((1, CONV_BLK), lambda i: (0, i))],
        out_specs=[out, out], out_shape=[jax.ShapeDtypeStruct((S, CONV_C), F32)] * 2,
        compiler_params=_cparams("parallel"),
    )(proj, convw, convb)


def _conv_bwd(dact, ypre, proj, convw, dproj):
    trips = S // CONV_ROWS

    def body(da_ref, y_ref, u_ref, w_ref, buf_ref, du_ref, dw_ref, db_ref):
        dw_ref[...] = jnp.zeros_like(dw_ref)
        db_ref[...] = jnp.zeros_like(db_ref)
        r8 = lax.broadcasted_iota(jnp.int32, (8, CONV_BLK), 0)

        def dy_of(rows):
            y = y_ref[rows, :]
            sg = _sigmoid(y)
            return da_ref[rows, :] * (sg * (1.0 + y * (1.0 - sg)))

        @pl.loop(0, trips)
        def _(c):
            t0 = pl.multiple_of(c * CONV_ROWS, CONV_ROWS)
            after = pl.multiple_of(jnp.minimum(t0 + CONV_ROWS, S - 8), 8)
            ext = jnp.concatenate([dy_of(pl.ds(t0, CONV_ROWS)),
                                   jnp.where(c == trips - 1, 0.0, dy_of(pl.ds(after, 8)))], axis=0)
            u = u_ref[pl.ds(t0, CONV_ROWS), :]
            du, dw = None, jnp.zeros((8, CONV_BLK), F32)
            for j in range(CONV_K):
                dyj = (ext if j == 0 else pltpu.roll(ext, CONV_ROWS + 8 - j, 0))[:CONV_ROWS]
                term = w_ref[CONV_K - 1 - j:CONV_K - j, :] * dyj
                du = term if du is None else du + term
                dw = dw + jnp.where(r8 == CONV_K - 1 - j, jnp.sum(dyj * u, axis=0, keepdims=True), 0.0)
            du_ref[pl.ds(t0, CONV_ROWS), :] = du.astype(BF16)
            dw_ref[...] += dw
            db_ref[...] += jnp.sum(ext[:CONV_ROWS], axis=0, keepdims=True)

    return pl.pallas_call(
        body, name="conv_bwd", grid=(CONV_C // CONV_BLK,),
        in_specs=[pl.BlockSpec((S, CONV_BLK), lambda i: (0, i)), pl.BlockSpec((S, CONV_BLK), lambda i: (0, i)),
                  pl.BlockSpec((S, CONV_BLK), lambda i: (0, CONV_COL0 + i)),
                  pl.BlockSpec((CONV_K, CONV_BLK), lambda i: (0, i)), pl.BlockSpec(memory_space=pl.ANY)],
        out_specs=[pl.BlockSpec((S, CONV_BLK), lambda i: (0, CONV_COL0 + i)),
                   pl.BlockSpec((8, CONV_BLK), lambda i: (0, i)), pl.BlockSpec((1, CONV_BLK), lambda i: (0, i))],
        out_shape=[jax.ShapeDtypeStruct(dproj.shape, BF16), jax.ShapeDtypeStruct((8, CONV_C), F32),
                   jax.ShapeDtypeStruct((1, CONV_C), F32)],
        input_output_aliases={4: 0}, compiler_params=_cparams("parallel"),
    )(dact, ypre, proj, convw, dproj)


NPAIR = 8


def _ssd_scalars(dtr_ref, dtb_ref, alog_ref):
    z = dtr_ref[...] + dtb_ref[...]
    dt = jnp.maximum(z, 0.0) + jnp.log(1.0 + jnp.exp(-jnp.abs(z)))
    a = -jnp.exp(alog_ref[...])
    r = lax.broadcasted_iota(jnp.int32, (128, 128), 0)
    c = lax.broadcasted_iota(jnp.int32, (128, 128), 1)
    tri = (r >= c).astype(F32)
    cs = _dot_exact(tri, dt * a)
    return z, dt, a, cs, r, c


def _by_lane(cs, dt):
    head = lax.broadcasted_iota(jnp.int32, (128, SSM_W), 0)
    lane = lax.broadcasted_iota(jnp.int32, (128, SSM_W), 1)
    sel = (head == lane // HD).astype(F32)
    cs_l = _dot_exact(cs, sel, "b")
    last_l = cs_l[127:128, :]
    return sel, jnp.exp(cs_l), jnp.exp(last_l - cs_l), _dot_exact(dt, sel, "b")


def _pair_terms(cs, h1, h2):
    return (cs[:, h1:h1 + 1], cs[:, h2:h2 + 1],
            jnp.exp(cs[127:128, h1:h1 + 1]), jnp.exp(cs[127:128, h2:h2 + 1]))


def _gate_norm(y, zv, w):
    yg = y * (zv * _sigmoid(zv))
    outs, rs = [], []
    for g in range(2):
        blk = yg[:, 512 * g:512 * (g + 1)]
        r = lax.rsqrt(jnp.mean(blk * blk, axis=-1, keepdims=True) + EPS)
        outs.append(blk * r)
        rs.append(r)
    return jnp.concatenate(outs, axis=1), rs, yg


def _ssd_fwd(xbc, proj, dtb, alog, dskip_l, ssmw):
    def body(x_ref, b_ref, c_ref, dtr_ref, z_ref, dtb_ref, alog_ref, dsk_ref, w_ref, y_ref, yn_ref, hp_ref, h_ref):
        @pl.when(pl.program_id(0) == 0)
        def _():
            h_ref[...] = jnp.zeros_like(h_ref)

        _, dt, _, cs, r, c = _ssd_scalars(dtr_ref, dtb_ref, alog_ref)
        cst = cs.T
        causal = r >= c
        lo = c < HD
        _, e_all, dte_all, dt_all = _by_lane(cs, dt)
        hp_ref[...] = h_ref[...]
        for g in range(2):
            bg = b_ref[:, 128 * g:128 * (g + 1)]
            cg = c_ref[:, 128 * g:128 * (g + 1)]
            cb = _dot(cg, bg, NT)
            for j in range(4):
                pj = 4 * g + j
                h1, h2 = 2 * pj, 2 * pj + 1
                sl = slice(128 * pj, 128 * (pj + 1))
                xp = x_ref[:, sl]
                c1, c2, cd1, cd2 = _pair_terms(cs, h1, h2)
                e_l, dte_l = e_all[:, sl], dte_all[:, sl]
                xdt = xp * dt_all[:, sl]
                m1 = cb * jnp.exp(jnp.where(causal, c1 - cst[h1:h1 + 1, :], NEG))
                m2 = cb * jnp.exp(jnp.where(causal, c2 - cst[h2:h2 + 1, :], NEG))
                yd = jnp.where(lo, _dot(m1, xdt, NN), _dot(m2, xdt, NN))
                hp = h_ref[pj]
                yo = _dot(cg, hp, NT) * e_l
                st = _dot(xdt * dte_l, bg, TN)
                h_ref[pj] = hp * jnp.where(r < HD, cd1, cd2) + st
                y_ref[:, sl] = yd + yo + dsk_ref[:, sl] * xp
        yn, _, _ = _gate_norm(y_ref[...], z_ref[...], w_ref[...])
        yn_ref[...] = (yn * w_ref[...]).astype(BF16)

    return pl.pallas_call(
        body, name="ssd_fwd", grid=(NCH,),
        in_specs=[pl.BlockSpec((128, SSM_W), lambda i: (i, 0)),
                  pl.BlockSpec((128, 256), lambda i: (i, 4)), pl.BlockSpec((128, 256), lambda i: (i, 5)),
                  pl.BlockSpec((128, 128), lambda i: (i, COL_DT // 128)),
                  pl.BlockSpec((128, SSM_W), lambda i: (i, 3)),
                  pl.BlockSpec((1, 128), lambda i: (0, 0)), pl.BlockSpec((1, 128), lambda i: (0, 0)),
                  pl.BlockSpec((1, SSM_W), lambda i: (0, 0)), pl.BlockSpec((1, SSM_W), lambda i: (0, 0))],
        out_specs=[pl.BlockSpec((128, SSM_W), lambda i: (i, 0)), pl.BlockSpec((128, SSM_W), lambda i: (i, 0)),
                   pl.BlockSpec((None, NPAIR, 128, 128), lambda i: (i, 0, 0, 0))],
        out_shape=[jax.ShapeDtypeStruct((S, SSM_W), F32), jax.ShapeDtypeStruct((S, SSM_W), BF16),
                   jax.ShapeDtypeStruct((NCH, NPAIR, 128, 128), F32)],
        scratch_shapes=[pltpu.VMEM((NPAIR, 128, 128), F32)],
        compiler_params=_cparams("arbitrary"),
    )(xbc, xbc, xbc, proj, proj, dtb, alog, dskip_l, ssmw)


def _ssd_bwd(dmixed, y, xbc, proj, hprev, dtb, alog, dskip_l, ssmw, rider=None):
    def body(dyn_ref, y_ref, x_ref, b_ref, c_ref, dtr_ref, z_ref, hp_ref, dtb_ref, alog_ref, dsk_ref, w_ref,
             dxbc_ref, dz_ref, ddt_ref, dw_ref, dsc_ref, g_ref):
        @pl.when(pl.program_id(0) == 0)
        def _():
            g_ref[...] = jnp.zeros_like(g_ref)
            dsc_ref[...] = jnp.zeros_like(dsc_ref)

        z, dt, a, cs, r, c = _ssd_scalars(dtr_ref, dtb_ref, alog_ref)
        cst = cs.T
        causal = r >= c
        lo = c < HD

        yv = y_ref[...]
        zv = z_ref[...]
        wv = w_ref[...]
        ygn, rs, yg = _gate_norm(yv, zv, wv)
        dyn = dyn_ref[...]
        _acc_rows(dw_ref, dyn * ygn)
        dynw = dyn * wv
        parts = []
        for g in range(2):
            sl = slice(512 * g, 512 * (g + 1))
            a_g, n_g = dynw[:, sl], ygn[:, sl]
            parts.append(rs[g] * (a_g - n_g * jnp.mean(a_g * n_g, axis=-1, keepdims=True)))
        dyg = jnp.concatenate(parts, axis=1)
        sz = _sigmoid(zv)
        dz_ref[...] = (dyg * yv * (sz * (1.0 + zv * (1.0 - sz)))).astype(BF16)
        dy_all = dyg * (zv * sz)

        dcs_cols = jnp.zeros((128, 128), F32)
        dcs_rows = jnp.zeros((128, 128), F32)
        sel, e_all, dte_all, dt_all = _by_lane(cs, dt)
        x_all, b_all, c_all, dsk_all = x_ref[...], b_ref[...], c_ref[...], dsk_ref[...]
        hp_all, g_all = hp_ref[...], g_ref[...]
        g_new, dx_parts, db_parts, dc_parts = [], [], [], []
        dyx_parts, ryo_parts, qx_parts, dxx_parts, gh_parts = [], [], [], [], []
        for g in range(2):
            bg = b_all[:, 128 * g:128 * (g + 1)]
            cg = c_all[:, 128 * g:128 * (g + 1)]
            cb = _dot(cg, bg, NT)
            dcb = jnp.zeros((128, 128), F32)
            db_acc = jnp.zeros((128, NST), F32)
            dc_acc = jnp.zeros((128, NST), F32)
            for j in range(4):
                pj = 4 * g + j
                h1, h2 = 2 * pj, 2 * pj + 1
                sl = slice(128 * pj, 128 * (pj + 1))
                xp = x_all[:, sl]
                dyp = dy_all[:, sl]
                c1, c2, cd1, cd2 = _pair_terms(cs, h1, h2)
                e_l, dte_l, dt_l = e_all[:, sl], dte_all[:, sl], dt_all[:, sl]
                xdt = xp * dt_l
                hp = hp_all[pj]
                gp = g_all[pj]
                dyx_parts.append(dyp * xp)
                dzs = dyp * e_l
                dc_acc = dc_acc + _dot(dzs, hp, NN)
                ryo_parts.append(dyp * (_dot(cg, hp, NT) * e_l))
                qm = _dot(bg, gp, NT)
                dxdt = qm * dte_l
                qx_parts.append(qm * xdt)
                db_acc = db_acc + _dot(xdt * dte_l, gp, NN)
                gh_parts.append(gp * hp)
                g_new.append(_dot(dzs, cg, TN) + jnp.where(r < HD, cd1, cd2) * gp)
                for hh, ch, msk in ((h1, c1, lo), (h2, c2, jnp.logical_not(lo))):
                    lm = jnp.exp(jnp.where(causal, ch - cst[hh:hh + 1, :], NEG))
                    mm = cb * lm
                    dm = jnp.where(causal, _dot(jnp.where(msk, dyp, 0.0), xdt, NT), 0.0)
                    w = dm * mm
                    dcs_cols = dcs_cols + jnp.where(c == hh, jnp.sum(w, axis=1, keepdims=True), 0.0)
                    dcs_rows = dcs_rows + jnp.where(r == hh, jnp.sum(w, axis=0, keepdims=True), 0.0)
                    dcb = dcb + dm * lm
                    dxdt = dxdt + jnp.where(msk, _dot(mm, dyp, TN), 0.0)
                dxx_parts.append(dxdt * xp)
                dx_parts.append(dsk_all[:, sl] * dyp + dxdt * dt_l)
            db_parts.append(db_acc + _dot(dcb, cg, TN))
            dc_parts.append(dc_acc + _dot(dcb, bg, NN))
        g_ref[...] = jnp.stack(g_new)
        dxbc_ref[...] = jnp.concatenate(dx_parts + db_parts + dc_parts, axis=1)

        selt = (lax.broadcasted_iota(jnp.int32, (SSM_W, 128), 0) // HD
                == lax.broadcasted_iota(jnp.int32, (SSM_W, 128), 1)).astype(F32)

        def by_head(parts):
            return _dot_exact(jnp.concatenate(parts, axis=1), selt, "b")

        ddt_x = by_head(dxx_parts)
        dd_row = jnp.sum(by_head(dyx_parts), axis=0, keepdims=True)
        t_all = by_head(qx_parts) * jnp.exp(cs[127:128, :] - cs)
        gh = jnp.sum(_dot_exact(sel, jnp.concatenate(gh_parts, axis=0)), axis=1, keepdims=True)
        gh_row = jnp.broadcast_to(gh, (128, 128)).T[0:1, :]
        at_end = jnp.sum(t_all, axis=0, keepdims=True) + gh_row * jnp.exp(cs[127:128, :])
        dcs = by_head(ryo_parts) - t_all + dcs_cols + jnp.where(r == 127, at_end, 0.0) - dcs_rows.T
        dad = _dot_exact((c >= r).astype(F32), dcs)
        ddt = dad * a + ddt_x
        ddtr = jnp.where(c < 16, ddt * _sigmoid(z), 0.0)
        ddt_ref[...] = ddtr.astype(BF16)
        r8 = lax.broadcasted_iota(jnp.int32, (8, 128), 0)
        dsc_ref[...] += (jnp.where(r8 == 0, jnp.sum(ddtr, axis=0, keepdims=True), 0.0)
                         + jnp.where(r8 == 1, jnp.sum(dad * dt, axis=0, keepdims=True) * a, 0.0)
                         + jnp.where(r8 == 2, dd_row, 0.0))

    rev = NCH - 1
    return _call(
        body, "ssd_bwd", (NCH,),
        [pl.BlockSpec((128, SSM_W), lambda i: (rev - i, 0)),
         pl.BlockSpec((128, SSM_W), lambda i: (rev - i, 0)),
         pl.BlockSpec((128, SSM_W), lambda i: (rev - i, 0)),
         pl.BlockSpec((128, 256), lambda i: (rev - i, 4)), pl.BlockSpec((128, 256), lambda i: (rev - i, 5)),
         pl.BlockSpec((128, 128), lambda i: (rev - i, COL_DT // 128)),
         pl.BlockSpec((128, SSM_W), lambda i: (rev - i, 3)),
         pl.BlockSpec((None, NPAIR, 128, 128), lambda i: (rev - i, 0, 0, 0)),
         pl.BlockSpec((1, 128), lambda i: (0, 0)), pl.BlockSpec((1, 128), lambda i: (0, 0)),
         pl.BlockSpec((1, SSM_W), lambda i: (0, 0)), pl.BlockSpec((1, SSM_W), lambda i: (0, 0))],
        [pl.BlockSpec((128, CONV_C), lambda i: (rev - i, 0)),
         pl.BlockSpec((128, SSM_W), lambda i: (rev - i, 3)),
         pl.BlockSpec((128, 128), lambda i: (rev - i, 0)),
         pl.BlockSpec((1, SSM_W), lambda i: (0, 0)), pl.BlockSpec((8, 128), lambda i: (0, 0))],
        [jax.ShapeDtypeStruct((S, CONV_C), F32), jax.ShapeDtypeStruct((S, WIN_PAD), BF16),
         jax.ShapeDtypeStruct((S, 128), BF16), jax.ShapeDtypeStruct((1, SSM_W), F32),
         jax.ShapeDtypeStruct((8, 128), F32)],
        (dmixed, y, xbc, xbc, xbc, proj, proj, hprev, dtb, alog, dskip_l, ssmw),
        [pltpu.VMEM((NPAIR, 128, 128), F32)], ("arbitrary",), rider)


def _cast_stack(name, slot, arrs, tr, tc):
    n = len(arrs)
    rows, cols = arrs[0].shape

    def body(s_ref, *refs):
        for i in range(n):
            refs[n][i] = refs[i][...].astype(BF16)

    return pl.pallas_call(
        body, name=name,
        grid_spec=pltpu.PrefetchScalarGridSpec(
            num_scalar_prefetch=1, grid=(rows // tr, cols // tc),
            in_specs=[pl.BlockSpec((tr, tc), lambda i, j, sr: (i, j))] * n,
            out_specs=pl.BlockSpec((None, n, tr, tc), lambda i, j, sr: (sr[0], 0, i, j))),
        out_shape=jax.ShapeDtypeStruct((NSH, n, rows, cols), BF16),
        compiler_params=_cparams("parallel", "parallel"),
    )(slot, *arrs)


def _pair_sum(name, c_idx, ps, th):
    n = len(ps)
    _, rows, _ = ps[0].shape

    def body(c_ref, *refs):
        mine, whole, out, theirs = refs[:n], refs[n:2 * n], refs[2 * n:3 * n], refs[3 * n:4 * n]
        send, recv = refs[4 * n], refs[4 * n + 1]
        s, i = pl.program_id(0), pl.program_id(1)
        x, y, c, _ = _place()

        def copies(slot):
            return [_rcopy(whole[k].at[slot, :, pl.ds((1 - c) * HALF, HALF)], theirs[k].at[slot],
                           send.at[slot * n + k], recv.at[slot * n + k], (x, y, 1 - c)) for k in range(n)]

        @pl.when((s == 0) & (i == 0))
        def _():
            for slot in range(NSH):
                for cp in copies(slot):
                    cp.start()

        @pl.when(i == 0)
        def _():
            for slot in range(NSH):
                @pl.when(s == slot)
                def _():
                    for cp in copies(slot):
                        cp.wait()

        rows_i = slice(None) if th == rows else pl.ds(pl.multiple_of(i * th, th), th)
        for k in range(n):
            out[k][...] = (mine[k][...].astype(F32) + theirs[k][s, rows_i, :].astype(F32)).astype(BF16)

    spec = pl.BlockSpec((None, th, HALF), lambda s, i, cr: (s, i, 0))
    return pl.pallas_call(
        body, name=name,
        grid_spec=pltpu.PrefetchScalarGridSpec(
            num_scalar_prefetch=1, grid=(NSH, rows // th),
            in_specs=[pl.BlockSpec((None, th, HALF), lambda s, i, cr: (s, i, cr[0]))] * n + _any_specs(n),
            out_specs=[spec] * n,
            scratch_shapes=[pltpu.VMEM((NSH, rows, HALF), BF16)] * n
            + [pltpu.SemaphoreType.DMA((NSH * n,)), pltpu.SemaphoreType.DMA((NSH * n,))]),
        out_shape=[jax.ShapeDtypeStruct((NSH, rows, HALF), BF16)] * n,
        compiler_params=_cparams("arbitrary", "arbitrary"),
    )(c_idx, *ps, *ps)


def _pair_add(name, c_idx, ps, theirs, th):
    n = len(ps)
    _, rows, _ = ps[0].shape

    def body(c_ref, *refs):
        for k in range(n):
            refs[2 * n + k][...] = (refs[k][...].astype(F32) + refs[n + k][...].astype(F32)).astype(BF16)

    spec = pl.BlockSpec((None, th, HALF), lambda s, i, cr: (s, i, 0))
    return pl.pallas_call(
        body, name=name,
        grid_spec=pltpu.PrefetchScalarGridSpec(
            num_scalar_prefetch=1, grid=(NSH, rows // th),
            in_specs=[pl.BlockSpec((None, th, HALF), lambda s, i, cr: (s, i, cr[0]))] * n + [spec] * n,
            out_specs=[spec] * n),
        out_shape=[jax.ShapeDtypeStruct((NSH, rows, HALF), BF16)] * n,
        compiler_params=_cparams("parallel", "parallel"),
    )(c_idx, *ps, *theirs)


def _chip_sum(name, place, cs, ts, th):
    n = len(ts)
    _, rows, _ = ts[0].shape

    def body(p_ref, *refs):
        for i in range(n):
            t = refs[n + i][...].astype(F32)
            refs[2 * n + i][...] = ((refs[i][...].astype(F32) + t[0]) + t[1]) + t[2]

    return pl.pallas_call(
        body, name=name,
        grid_spec=pltpu.PrefetchScalarGridSpec(
            num_scalar_prefetch=1, grid=(rows // th,),
            in_specs=[pl.BlockSpec((None, th, HALF), lambda i, pr: (pr[0], i, 0))] * n
            + [pl.BlockSpec((3, th, HALF), lambda i, pr: (0, i, 0))] * n,
            out_specs=[pl.BlockSpec((th, HALF), lambda i, pr: (i, pr[1]))] * n),
        out_shape=[jax.ShapeDtypeStruct((rows, D), F32)] * n, compiler_params=_cparams("parallel"),
    )(place, *cs, *ts)


def _adamw(name, ws, gs, ms, vs, tr, tc):
    n = len(ws)
    shape = ws[0].shape
    rows, cols, mid = shape[0], shape[-1], shape[1:-1]
    c1 = 1.0 / (1.0 - ADAM_B1 ** ADAM_STEP)
    c2 = 1.0 / (1.0 - ADAM_B2 ** ADAM_STEP)

    def body(*refs):
        for i in range(n):
            w, g, m, v = (refs[k * n + i][...] for k in range(4))
            m2 = ADAM_B1 * m + (1.0 - ADAM_B1) * g
            v2 = ADAM_B2 * v + (1.0 - ADAM_B2) * (g * g)
            refs[4 * n + 4 * i][...] = -ADAM_LR * ((m2 * c1) / (jnp.sqrt(v2 * c2) + ADAM_EPS) + ADAM_WD * w)
            refs[4 * n + 4 * i + 1][...] = m2
            refs[4 * n + 4 * i + 2][...] = v2
            refs[4 * n + 4 * i + 3][...] = g

    spec = pl.BlockSpec((tr,) + mid + (tc,), lambda i, j: (i,) + (0,) * len(mid) + (j,))
    outs = pl.pallas_call(
        body, name=name, grid=(rows // tr, cols // tc), in_specs=[spec] * (4 * n), out_specs=[spec] * (4 * n),
        out_shape=[jax.ShapeDtypeStruct(shape, F32)] * (4 * n),
        compiler_params=_cparams("parallel", "parallel"),
    )(*ws, *gs, *ms, *vs)
    return [tuple(outs[4 * i:4 * i + 4]) for i in range(n)]


def _place():
    x, y, c = lax.axis_index("x"), lax.axis_index("y"), lax.axis_index("c")
    chips = [(1 - x, y), (x, 1 - y), (1 - x, 1 - y)]
    return x, y, c, chips


def _any_specs(n):
    return [pl.BlockSpec(memory_space=pl.ANY)] * n


def _rcopy(src, dst, send_sem, recv_sem, dev):
    return pltpu.make_async_remote_copy(src_ref=src, dst_ref=dst, send_sem=send_sem, recv_sem=recv_sem,
                                        device_id=dev, device_id_type=MESH)


QUARTER = HALF // 2

TO_X, TO_Y, RELAY_X, RELAY_Y, FWD_X, FWD_Y, FWD_D0, FWD_D1 = range(8)
TO_D = RELAY_X


def _gather_rider(bufs, views, relay):
    n = len(bufs)

    def plan(rout, sems):
        send, recv = sems
        x, y, c, _ = _place()
        me, sx, sy, sd = 2 * x + y, 2 * (1 - x) + y, 2 * x + (1 - y), 2 * (1 - x) + (1 - y)
        nx, ny, nd, sib = (1 - x, y, c), (x, 1 - y, c), (1 - x, 1 - y, c), (x, y, 1 - c)
        mine, other = c * HALF, (1 - c) * HALF
        out = {TO_X: (me, mine, HALF, nx), TO_Y: (me, mine, HALF, ny),
               FWD_X: (sx, mine, HALF, sib), FWD_Y: (sy, mine, HALF, sib)}
        inn = {TO_X: (sx, mine, HALF), TO_Y: (sy, mine, HALF),
               FWD_X: (sx, other, HALF), FWD_Y: (sy, other, HALF)}
        if relay:
            out.update({RELAY_X: (sy, mine, QUARTER, nx), RELAY_Y: (sx, mine + QUARTER, QUARTER, ny),
                        FWD_D0: (sd, mine, QUARTER, sib), FWD_D1: (sd, mine + QUARTER, QUARTER, sib)})
            inn.update({RELAY_X: (sd, mine, QUARTER), RELAY_Y: (sd, mine + QUARTER, QUARTER),
                        FWD_D0: (sd, other, QUARTER), FWD_D1: (sd, other + QUARTER, QUARTER)})
        else:
            out.update({TO_D: (me, mine, HALF, nd), FWD_D0: (sd, mine, HALF, sib)})
            inn.update({TO_D: (sd, mine, HALF), FWD_D0: (sd, other, HALF)})

        def copy(kind, b):
            slot, col, ncols, dev = out[kind]
            win = views[b](rout[b], slot, col, ncols)
            return _rcopy(win, win, send.at[kind * n + b], recv.at[kind * n + b], dev)

        def land(kind, b):
            slot, col, ncols = inn[kind]
            win = views[b](rout[b], slot, col, ncols)
            return _rcopy(win, win, send.at[kind * n + b], recv.at[kind * n + b], (x, y, c))

        return copy, land

    if relay:
        first = (TO_X, TO_Y)
        chain = ((TO_X, (FWD_X, RELAY_Y)), (TO_Y, (FWD_Y, RELAY_X)), (RELAY_X, (FWD_D0,)), (RELAY_Y, (FWD_D1,)))
    else:
        first = (TO_X, TO_Y, TO_D)
        chain = ((TO_X, (FWD_X,)), (TO_Y, (FWD_Y,)), (TO_D, (FWD_D0,)))
    forwards = [k for _, then in chain for k in then if k in (FWD_X, FWD_Y, FWD_D0, FWD_D1)]
    sent = list(first) + [k for _, then in chain for k in then]

    def start(rin, rout, sems):
        copy, _ = plan(rout, sems)
        for kind in first:
            for b in range(n):
                copy(kind, b).start()

    def finish(rin, rout, sems):
        copy, land = plan(rout, sems)
        for landed, then in chain:
            for b in range(n):
                land(landed, b).wait_recv()
                for kind in then:
                    copy(kind, b).start()
        for kind in forwards:
            for b in range(n):
                land(kind, b).wait_recv()
        for kind in sent:
            for b in range(n):
                copy(kind, b).wait_send()

    return _Rider(list(bufs), [jax.ShapeDtypeStruct(a.shape, a.dtype) for a in bufs], {b: b for b in range(n)},
                  [pltpu.SemaphoreType.DMA((8 * n,))] * 2, start, finish)


def _small_gather_rider(cw):
    def descs(rin, rout, sems, x, y, c, chips):
        return [_rcopy(rin[0], rout[0].at[2 * x + y], sems[1].at[j], sems[2].at[j], (chip[0], chip[1], c))
                for j, chip in enumerate(chips)]

    def start(rin, rout, sems):
        x, y, c, chips = _place()
        pltpu.make_async_copy(rin[0], rout[0].at[2 * x + y], sems[0].at[0]).start()
        for cp in descs(rin, rout, sems, x, y, c, chips):
            cp.start()

    def finish(rin, rout, sems):
        x, y, c, chips = _place()
        for j, chip in enumerate(chips):
            _rcopy(rin[0], rout[0].at[2 * chip[0] + chip[1]], sems[1].at[j], sems[2].at[j], (x, y, c)).wait_recv()
        for cp in descs(rin, rout, sems, x, y, c, chips):
            cp.wait_send()
        pltpu.make_async_copy(rin[0], rout[0].at[2 * x + y], sems[0].at[0]).wait()

    return _Rider([cw], [jax.ShapeDtypeStruct((NSH,) + cw.shape, cw.dtype)], {},
                  [pltpu.SemaphoreType.DMA((1,)), pltpu.SemaphoreType.DMA((3,)), pltpu.SemaphoreType.DMA((3,))],
                  start, finish)


def _to_sibling_rider(ps):
    n = len(ps)

    def descs(rin, rout, sems):
        x, y, c, _ = _place()
        return [_rcopy(rin[i].at[:, :, pl.ds((1 - c) * HALF, HALF)], rout[i], sems[0].at[i], sems[1].at[i],
                       (x, y, 1 - c)) for i in range(n)]

    def start(rin, rout, sems):
        for cp in descs(rin, rout, sems):
            cp.start()

    def finish(rin, rout, sems):
        for cp in descs(rin, rout, sems):
            cp.wait()

    return _Rider(list(ps), [jax.ShapeDtypeStruct(a.shape[:2] + (HALF,), a.dtype) for a in ps], {},
                  [pltpu.SemaphoreType.DMA((n,))] * 2, start, finish)


def _to_chips_rider(cs):
    n = len(cs)

    def descs(rin, rout, sems):
        x, y, c, chips = _place()
        return [_rcopy(rin[i].at[2 * chip[0] + chip[1]], rout[i].at[j], sems[0].at[j * n + i], sems[1].at[j * n + i],
                       (chip[0], chip[1], c)) for j, chip in enumerate(chips) for i in range(n)]

    def start(rin, rout, sems):
        for cp in descs(rin, rout, sems):
            cp.start()

    def finish(rin, rout, sems):
        for cp in descs(rin, rout, sems):
            cp.wait()

    return _Rider(list(cs), [jax.ShapeDtypeStruct((3,) + a.shape[1:], a.dtype) for a in cs], {},
                  [pltpu.SemaphoreType.DMA((3 * n,))] * 2, start, finish)


def _join_riders(riders):
    counts = [[len(r.operands) for r in riders], [len(r.out_shapes) for r in riders], [len(r.sems) for r in riders]]

    def each(step):
        def run(*refs):
            at = [0, 0, 0]
            for i, r in enumerate(riders):
                parts = [group[at[k]:at[k] + counts[k][i]] for k, group in enumerate(refs)]
                at = [at[k] + counts[k][i] for k in range(3)]
                step(r)(*parts)
        return run

    aliases = {sum(counts[0][:i]) + k: sum(counts[1][:i]) + v
               for i, r in enumerate(riders) for k, v in r.aliases.items()}
    return _Rider([a for r in riders for a in r.operands], [s for r in riders for s in r.out_shapes], aliases,
                  [s for r in riders for s in r.sems], each(lambda r: r.start), each(lambda r: r.finish))


SMALL_ROWS = 16


def _swap_halves(gs, vec):
    n = len(gs)

    def body(*refs):
        v_ref, dst, o_ref = refs[n], refs[n + 1:2 * n + 1], refs[2 * n + 1]
        buf, send, recv, vsend, vrecv = refs[2 * n + 2:]
        x, y, c, _ = _place()
        cps = []
        for i in range(n):
            mine = dst[i].at[:, pl.ds(c * HALF, HALF)]
            cps.append(_rcopy(mine, mine, send.at[i], recv.at[i], (x, y, 1 - c)))
        for cp in cps:
            cp.start()

        me = 4 * x + 2 * y + c
        buf[me] = v_ref[...]
        vcps = []
        for k in range(1, 8):
            peer = (x ^ (k >> 2), y ^ ((k >> 1) & 1), c ^ (k & 1))
            vcps.append(_rcopy(v_ref, buf.at[me], vsend.at[k - 1], vrecv.at[k - 1], peer))
        for cp in vcps:
            cp.start()
        for k in range(1, 8):
            _rcopy(v_ref, buf.at[me ^ k], vsend.at[k - 1], vrecv.at[k - 1], (x, y, c)).wait_recv()
        for cp in vcps:
            cp.wait_send()
        t = buf[0]
        for d in range(1, 8):
            t = t + buf[d]
        o_ref[...] = t

        for i in range(n):
            other = dst[i].at[:, pl.ds((1 - c) * HALF, HALF)]
            _rcopy(other, other, send.at[i], recv.at[i], (x, y, c)).wait_recv()
        for cp in cps:
            cp.wait_send()

    vmem = pl.BlockSpec(memory_space=pltpu.VMEM)
    res = pl.pallas_call(
        body, name="grads_swap_halves", in_specs=_any_specs(n) + [vmem], out_specs=_any_specs(n) + [vmem],
        out_shape=[jax.ShapeDtypeStruct(g.shape, g.dtype) for g in gs] + [jax.ShapeDtypeStruct((SMALL_ROWS, D), F32)],
        input_output_aliases={i: i for i in range(n)},
        scratch_shapes=[pltpu.VMEM((8, SMALL_ROWS, D), F32)] + [pltpu.SemaphoreType.DMA((n,))] * 2
        + [pltpu.SemaphoreType.DMA((7,))] * 2,
    )(*gs, vec)
    return list(res[:n]), res[n]


def _col_window(ref, slot, col, ncols):
    return ref.at[slot, :, pl.ds(col, ncols)]


def _stack_window(first, count):
    def view(ref, slot, col, ncols):
        return ref.at[slot, pl.ds(first, count), :, pl.ds(col, ncols)]
    return view


def _row_tile(rows):
    for t in range(512, 15, -16):
        if rows % t == 0:
            return t
    return rows


def _same_shape_runs(arrs):
    runs, a = [], 0
    for b in range(1, len(arrs) + 1):
        if b == len(arrs) or arrs[b].shape != arrs[a].shape:
            runs.append((a, b))
            a = b
    return runs


class _Comm:
    def __init__(self):
        x, y, c = lax.axis_index("x"), lax.axis_index("y"), lax.axis_index("c")
        self.c_idx = jnp.reshape(c, (1,)).astype(jnp.int32)
        self.place = jnp.stack([2 * x + y, c]).astype(jnp.int32)
        self.groups = {}

    @staticmethod
    def gather(*bufs, relay, part=None):
        views = [_col_window if b.ndim == 3 else _stack_window(*(part or (0, b.shape[1]))) for b in bufs]
        return _gather_rider(list(bufs), views, relay)

    def reduce_rider(self, tag, names, ps, theirs=None):
        csums = []
        for a, b in _same_shape_runs(ps):
            name, th = "pair_sum_%s%d" % (tag, a), _row_tile(ps[a].shape[1])
            csums += (_pair_sum(name, self.c_idx, ps[a:b], th) if theirs is None else
                      _pair_add(name, self.c_idx, ps[a:b], theirs[a:b], th))
        self.groups[tag] = [names, csums, None]
        return _to_chips_rider(csums)

    def landed(self, tag, ts):
        self.groups[tag][2] = ts

    def finish(self, small):
        names, csums, ts = [], [], []
        for group_names, group_csums, group_ts in self.groups.values():
            names += group_names
            csums += group_csums
            ts += group_ts
        order = sorted(range(len(names)), key=lambda i: csums[i].shape[1])
        names, csums, ts = ([v[i] for i in order] for v in (names, csums, ts))
        halves = []
        for a, b in _same_shape_runs(csums):
            halves += _chip_sum("chip_sum_%d" % a, self.place, csums[a:b], ts[a:b], _row_tile(csums[a].shape[1]))
        grads, total = _swap_halves(halves, small)
        return dict(zip(names, grads)), total


ROPE_THETA = 10000.0
SMALL_1K = ("ffn1_pre_norm", "ffn1_post_norm", "mix_pre_norm", "ssm_norm", "mix_post_norm",
            "ffn2_pre_norm", "ffn2_post_norm")
SMALL_16 = ("dt_bias", "a_log", "d_skip")
OFF_CONVB = 7 * D
OFF_16 = OFF_CONVB + CONV_C
OFF_CONVW = OFF_16 + 48
OFF_LOSS = OFF_CONVW + CONV_K * CONV_C
SMALL_LEN = SMALL_ROWS * D


def _sds(shape, dtype):
    return jax.ShapeDtypeStruct(shape, dtype)


def _ridden(res, rider):
    return res if rider is not None else (res, None)


def _ffn_down(name, act, w, tail_of, rider=None):
    tail, o_specs, o_shapes = tail_of(TS)
    return _mm(name, [act, w.dn], NN, (S // TS,),
               [pl.BlockSpec((NSH, TS, FS), lambda i: (0, i, 0)),
                pl.BlockSpec((NSH, None, FS, D), lambda i: (0, w.d0, 0, 0))], o_specs, o_shapes, rider, tail)


def _ffn_dw(name, a, b, rider=None):
    return _mm(name, [a, b], TN, (NSH,),
               [pl.BlockSpec((None, S, FS), lambda s: (s, 0, 0)), pl.BlockSpec((S, D), lambda s: (0, 0))],
               pl.BlockSpec((None, FS, D), lambda s: (s, 0, 0)), _sds((NSH, FS, D), BF16), rider)


def _ffn_dn(name, dgate, dup, w, tail_of, rider=None):
    rows = TS // 2
    tail, o_specs, o_shapes = tail_of(rows)
    a2 = pl.BlockSpec((NSH, rows, FS), lambda i: (0, i, 0))
    return _mm(name, [dgate, w.gu, dup, w.gu], NN, (S // rows,),
               [a2, pl.BlockSpec((NSH, None, FS, D), lambda i: (0, w.g0, 0, 0)),
                a2, pl.BlockSpec((NSH, None, FS, D), lambda i: (0, w.g0 + 1, 0, 0))], o_specs, o_shapes, rider, tail)


def _out_proj_dx(dh, wout):
    def body(dh_ref, w_ref, dyn_ref, do_ref):
        dm = _dot(dh_ref[...], w_ref[...], NT)
        dyn_ref[...] = dm[:, D:]
        for b in range(TS // 128):
            for j, blk in enumerate(_rows_to_blocks(dm[128 * b:128 * (b + 1), :D])):
                do_ref[j, b] = blk.astype(BF16)

    return pl.pallas_call(
        body, name="out_proj_dx", grid=(S // TS,),
        in_specs=[pl.BlockSpec((TS, D), lambda i: (i, 0)), pl.BlockSpec((2 * D, D), lambda i: (0, 0))],
        out_specs=[pl.BlockSpec((TS, D), lambda i: (i, 0)),
                   pl.BlockSpec((NKV, TS // 128, HD, QROWS), lambda i: (0, i, 0, 0))],
        out_shape=[_sds((S, D), F32), _sds((NKV, NCH, HD, QROWS), BF16)], compiler_params=_cparams("parallel"),
    )(dh, wout)


def _heads(t, n):
    return t.reshape(S, n, HD).transpose(1, 0, 2)


def _pad128(v):
    return jnp.pad(v, ((0, 0), (0, 128 - v.shape[1])))


def _local_step(x, positions, tgt, sp, gu1, d1, f2, wint, wout, convw, comm=None):
    inv_freq = ROPE_THETA ** (-jnp.arange(0, HD, 2, dtype=F32) / HD)
    ang = positions.astype(F32)[:, None] * inv_freq
    ang = jnp.concatenate([ang, ang, ang, ang], axis=-1)
    cos, sin = jnp.cos(ang), jnp.sin(ang)
    dtb, alog = _pad128(sp["dt_bias"]), _pad128(sp["a_log"])
    dskip_l = jnp.repeat(sp["d_skip"], HD, axis=1)
    convb = sp["conv_b"]

    rider = _join_riders([comm.gather(gu1, relay=True), _small_gather_rider(convw)]) if comm else None
    n1, got = _ridden(_prenorm("prenorm1", x, sp["ffn1_pre_norm"], rider), rider)
    if comm:
        gu1, convw = got
        convw = convw.transpose(1, 0, 2).reshape(CONV_K, CONV_C)
    rider = comm.gather(d1, relay=False) if comm else None
    (fg1, fu1, act1), got = _ridden(_ffn_up("ffn1_up", n1, _FfnW(gu1, 0, d1, 0), rider), rider)
    if comm:
        d1, = got
    w1 = _FfnW(gu1, 0, d1, 0)
    rider = comm.gather(wint, relay=True) if comm else None
    (h1, x1, n2), got = _ridden(_ffn_down(
        "ffn1_down", act1, w1,
        lambda rows: _tail_postres(rows, x, sp["ffn1_post_norm"], 0.5, sp["mix_pre_norm"]), rider), rider)
    if comm:
        wint, = got
    wint_pad = jnp.pad(wint.reshape(WIN_COLS, D), ((0, WIN_PAD - WIN_COLS), (0, 0)))

    pw = WIN_PAD // 3
    rider = comm.gather(f2, relay=False, part=(0, 1)) if comm else None
    proj, got = _ridden(_mm(
        "in_proj", [n2, wint_pad], NT, (S // TS, 3),
        [pl.BlockSpec((TS, D), lambda i, j: (i, 0)), pl.BlockSpec((pw, D), lambda i, j: (j, 0))],
        pl.BlockSpec((TS, pw), lambda i, j: (i, j)), _sds((S, WIN_PAD), F32), rider), rider)
    if comm:
        f2, = got
    qt = _rope_q(proj, cos, sin)
    k_rot, v_bf, kt, vt = _rope_kv(proj, cos, sin)
    kh, vh = _heads(k_rot, NKV), _heads(v_bf, NKV)
    bias = _bias_table()
    rider = comm.gather(f2, wout, relay=False, part=(1, 2)) if comm else None
    (ot, lse, attn), got = _ridden(_attn_fwd(qt, kh, vt, bias, rider), rider)
    if comm:
        f2, wout = got
    w2 = _FfnW(f2, 0, f2, 2)
    wout = wout.reshape(2 * D, D)
    xbc, conv_y = _conv_fwd(proj, convw, convb)
    y, yn, hprev = _ssd_fwd(xbc, proj, dtb, alog, dskip_l, sp["ssm_norm"])
    mixed = jnp.concatenate([attn, yn], axis=1)
    tail, o_specs, o_shapes = _tail_postres(TS, x1, sp["mix_post_norm"], 1.0, sp["ffn2_pre_norm"])
    h2, x2, n3 = _mm("out_proj", [mixed, wout], NN, (S // TS,),
                     [pl.BlockSpec((TS, 2 * D), lambda i: (i, 0)), pl.BlockSpec((2 * D, D), lambda i: (0, 0))],
                     o_specs, o_shapes, None, tail)

    fg2, fu2, act2 = _ffn_up("ffn2_up", n3, w2)
    dy, dh3, dp3, loss = _ffn_down(
        "ffn2_down", act2, w2, lambda rows: _tail_final(rows, x2, sp["ffn2_post_norm"], tgt, 0.5))

    dgate2, dup2 = _ffn_dact("ffn2_dact", dh3, w2, fg2, fu2)
    dws2 = [_ffn_dw("ffn2_dwg", dgate2, n3), _ffn_dw("ffn2_dwu", dup2, n3), _ffn_dw("ffn2_dwd", act2, dh3)]
    dx2, dh2, dg3, dp2 = _ffn_dn(
        "ffn2_dn", dgate2, dup2, w2,
        lambda rows: _tail_mid_bwd(rows, dy, x2, sp["ffn2_pre_norm"], h2, sp["mix_post_norm"], 1.0))

    dyn, dot_ = _out_proj_dx(dh2, wout)
    dwout = _mm("out_proj_dw", [mixed, dh2], TN, (2,),
                [pl.BlockSpec((S, D), lambda m: (0, m)), pl.BlockSpec((S, D), lambda m: (0, 0))],
                pl.BlockSpec((D, D), lambda m: (m, 0)), _sds((2 * D, D), BF16))
    dwout = dwout.reshape(NSH, 2 * D // NSH, D)

    def riding(tag, names, ps, call, theirs=None):
        rider = comm.reduce_rider(tag, names, ps, theirs) if comm else None
        res, got = _ridden(call(rider), rider)
        if comm:
            comm.landed(tag, got)
        return res

    rider = _to_sibling_rider(dws2 + [dwout]) if comm else None
    (dxbc, dproj, ddt, dssm, dsc), theirs = _ridden(
        _ssd_bwd(dyn, y, xbc, proj, hprev, dtb, alog, dskip_l, sp["ssm_norm"], rider), rider)
    dproj, dcw8, dcb = _conv_bwd(dxbc, conv_y, proj, convw, dproj)
    delta = _attn_delta(ot, dot_)
    dqt, dkh, dvh = riding("a", BIG[3:6] + ("w_out",), dws2 + [dwout], lambda rider: _attn_bwd(
        qt, kh, kt, vh, dot_, lse, delta, bias, rider), theirs)
    dproj = _rope_dq(dqt, cos, sin, dproj)
    dproj = _rope_dkv(dkh, dvh, cos, sin, dproj)
    dproj = lax.dynamic_update_slice(dproj, ddt, (0, COL_DT))
    dwint = _mm("in_proj_dw", [dproj, n2], TN, (3,),
                [pl.BlockSpec((S, pw), lambda j: (0, j)), pl.BlockSpec((S, D), lambda j: (0, 0))],
                pl.BlockSpec((pw, D), lambda j: (j, 0)), _sds((WIN_PAD, D), BF16))
    dwint = dwint[:WIN_COLS].reshape(NSH, WIN_SH, D)

    tail, o_specs, o_shapes = _tail_mid_bwd(TS, dx2, x1, sp["mix_pre_norm"], h1, sp["ffn1_post_norm"], 0.5)
    dx1, dh1, dg2, dp1 = riding("b", ("w_in",), [dwint], lambda rider: _mm(
        "in_proj_dx", [dproj, wint_pad], NN, (S // TS,),
        [pl.BlockSpec((TS, WIN_PAD), lambda i: (i, 0)), pl.BlockSpec((WIN_PAD, D), lambda i: (0, 0))],
        o_specs, o_shapes, rider, tail))

    dwd1 = _ffn_dw("ffn1_dwd", act1, dh1)
    dgate1, dup1 = riding("d", BIG[2:3], [dwd1], lambda rider: _ffn_dact("ffn1_dact", dh1, w1, fg1, fu1, rider))
    dwg1, dwu1 = _ffn_dw("ffn1_dwg", dgate1, n1), _ffn_dw("ffn1_dwu", dup1, n1)
    grad_x, dg1 = riding("g", BIG[0:2], [dwg1, dwu1], lambda rider: _ffn_dn(
        "ffn1_dn", dgate1, dup1, w1, lambda rows: _tail_first_bwd(rows, dx1, x, sp["ffn1_pre_norm"]), rider))
    dws1 = [dwg1, dwu1, dwd1]

    small = jnp.concatenate([
        dg1[0], dp1[0], dg2[0], dssm[0], dp2[0], dg3[0], dp3[0], dcb[0],
        dsc[0, :16], dsc[1, :16], dsc[2, :16], dcw8[:CONV_K].reshape(-1), loss[0, :1]])
    small = jnp.pad(small, (0, SMALL_LEN - small.shape[0])).reshape(SMALL_ROWS, D)
    if comm is None:
        return grad_x, dws1 + dws2 + [dwint, dwout], small
    return (grad_x,) + comm.finish(small)


WEIGHTS = ("ffn1_pre_norm", "ffn1_w_gate", "ffn1_w_up", "ffn1_w_down", "ffn1_post_norm", "mix_pre_norm", "w_in",
           "conv_w", "conv_b", "dt_bias", "a_log", "d_skip", "ssm_norm", "w_out", "mix_post_norm", "ffn2_pre_norm",
           "ffn2_w_gate", "ffn2_w_up", "ffn2_w_down", "ffn2_post_norm")
BIG = ("ffn1_w_gate", "ffn1_w_up", "ffn1_w_down", "ffn2_w_gate", "ffn2_w_up", "ffn2_w_down", "w_in", "w_out")
TRANSPOSED = ("ffn1_w_gate", "ffn1_w_up", "ffn2_w_gate", "ffn2_w_up", "w_in")
SMALL_ORDER = SMALL_1K + ("conv_b",) + SMALL_16
CONVW_SH = CONV_C // NSH


def _shard2d(t, name):
    return t[0].T if name in TRANSPOSED else t[0]


def _unshard2d(t, name):
    return (t.T if name in TRANSPOSED else t)[None]


def _rows3d(t):
    return t.transpose(2, 0, 1)


def _pack_small(d, prefix, shard_of_convw):
    flat = jnp.concatenate([d[prefix + n][0] for n in SMALL_ORDER] + [shard_of_convw.reshape(-1)])
    return jnp.pad(flat, (0, SMALL_LEN - flat.shape[0])).reshape(SMALL_ROWS, D)


def _unpack_small(block, like):
    flat = block.reshape(-1)
    out, off = {}, 0
    for n in SMALL_ORDER:
        size = like[n].shape[1]
        out[n] = flat[off:off + size].reshape(1, size)
        off += size
    out["conv_w"] = flat[off:off + CONV_K * CONVW_SH].reshape(1, CONV_K, CONVW_SH)
    return out


def kernel(x, positions, ffn1_pre_norm, ffn1_w_gate, ffn1_w_up, ffn1_w_down, ffn1_post_norm, mix_pre_norm, w_in, conv_w, conv_b, dt_bias, a_log, d_skip, ssm_norm, w_out, mix_post_norm, ffn2_pre_norm, ffn2_w_gate, ffn2_w_up, ffn2_w_down, ffn2_post_norm, loss_target, m_ffn1_pre_norm, m_ffn1_w_gate, m_ffn1_w_up, m_ffn1_w_down, m_ffn1_post_norm, m_mix_pre_norm, m_w_in, m_conv_w, m_conv_b, m_dt_bias, m_a_log, m_d_skip, m_ssm_norm, m_w_out, m_mix_post_norm, m_ffn2_pre_norm, m_ffn2_w_gate, m_ffn2_w_up, m_ffn2_w_down, m_ffn2_post_norm, v_ffn1_pre_norm, v_ffn1_w_gate, v_ffn1_w_up, v_ffn1_w_down, v_ffn1_post_norm, v_mix_pre_norm, v_w_in, v_conv_w, v_conv_b, v_dt_bias, v_a_log, v_d_skip, v_ssm_norm, v_w_out, v_mix_post_norm, v_ffn2_pre_norm, v_ffn2_w_gate, v_ffn2_w_up, v_ffn2_w_down, v_ffn2_post_norm):
    given = dict(locals())
    xi, yi = lax.axis_index("x"), lax.axis_index("y")

    shard = jnp.reshape(2 * xi + yi, (1,)).astype(jnp.int32)
    big = {p + n: _shard2d(given[p + n], n) for n in BIG for p in ("", "m_", "v_")}
    gu1 = _cast_stack("cast_ffn1_gate_up", shard, [big[n] for n in BIG[0:2]], 176, D)
    d1 = _cast_stack("cast_ffn1_down", shard, [big[BIG[2]]], 176, D)
    f2 = _cast_stack("cast_ffn2", shard, [big[n] for n in BIG[3:6]], 176, D)
    winsh = _cast_stack("cast_w_in", shard, [big["w_in"]], WIN_SH, 256).reshape(NSH, WIN_SH, D)
    woutsh = _cast_stack("cast_w_out", shard, [big["w_out"]], 256, D).reshape(NSH, 2 * D // NSH, D)
    comm = _Comm()

    sp = {n: given[n] for n in SMALL_ORDER}
    grad_x, big_grads, small = _local_step(x[0], positions[0], loss_target[0], sp, gu1, d1, f2, winsh, woutsh,
                                           conv_w[0], comm)

    tot = small.reshape(-1)
    loss = tot[OFF_LOSS]
    small_grads, off = {}, 0
    for n in SMALL_ORDER:
        size = given[n].shape[1]
        small_grads[n] = tot[off:off + size].reshape(1, size)
        off += size
    dconvw = tot[OFF_CONVW:OFF_CONVW + CONV_K * CONV_C].reshape(CONV_K, NSH, CONVW_SH)
    dconvw = lax.dynamic_index_in_dim(dconvw, 2 * xi + yi, axis=1, keepdims=False)
    small_grads["conv_w"] = dconvw.reshape(1, CONV_K, CONVW_SH)

    upd = {}
    for names, tr in ((BIG[0:3], 176), (BIG[3:6], 176), (BIG[7:8], 256)):
        res = _adamw("adamw_" + names[0], [big[n] for n in names], [big_grads[n] for n in names],
                     [big["m_" + n] for n in names], [big["v_" + n] for n in names], tr, D)
        for n, r in zip(names, res):
            upd[n] = tuple(_unshard2d(t, n) for t in r)
    g_win = big_grads["w_in"].reshape(WIN_SH, 1, D)
    res, = _adamw("adamw_w_in", [_rows3d(w_in)], [g_win], [_rows3d(m_w_in)], [_rows3d(v_w_in)], WIN_SH // 4, D)
    upd["w_in"] = tuple(t.transpose(1, 2, 0) for t in res)
    (dl, m2, v2, _), = _adamw(
        "adamw_small", [_pack_small(given, "", conv_w[0])], [_pack_small(small_grads, "", dconvw)],
        [_pack_small(given, "m_", m_conv_w[0])], [_pack_small(given, "v_", v_conv_w[0])], SMALL_ROWS, D)
    dl, m2, v2 = (_unpack_small(t, given) for t in (dl, m2, v2))
    for n in SMALL_ORDER + ("conv_w",):
        upd[n] = (dl[n], m2[n], v2[n], small_grads[n])

    return (loss, grad_x[None], *[upd[n][3] for n in WEIGHTS], *[upd[n][0] for n in WEIGHTS],
            *[upd[n][1] for n in WEIGHTS], *[upd[n][2] for n in WEIGHTS])
```

```python
import functools
import typing

import jax
import jax.numpy as jnp
from jax import lax
from jax.experimental import pallas as pl
from jax.experimental.pallas import tpu as pltpu

F32 = jnp.float32
BF16 = jnp.bfloat16

S = 2048
D = 1024
FF = 2816
NSH = 4
FS = FF // NSH
HALF = D // 2
HD = 64
NKV = 4
NQ_PER_KV = 4
KVW = NKV * HD
QCOLS = NQ_PER_KV * HD
CONV_C = 1536
CONV_K = 4
SSM_W = 1024
NST = 128
NCH = S // 128
WIN_COLS = 4112
WIN_SH = WIN_COLS // NSH
WIN_PAD = 4224
COL_DT = 4096
EPS = 1e-6
NEG = -1e30

ADAM_LR = 0.001
ADAM_B1 = 0.9
ADAM_B2 = 0.999
ADAM_EPS = 1e-08
ADAM_WD = 0.01
ADAM_STEP = 10

VMEM_LIMIT = 56 * 1024 * 1024
TS = 512
TR = 256

NN = (((1,), (0,)), ((), ()))
NT = (((1,), (1,)), ((), ()))
TN = (((0,), (0,)), ((), ()))
MESH = pl.DeviceIdType.MESH


def _cparams(*sem):
    return pltpu.CompilerParams(dimension_semantics=sem, vmem_limit_bytes=VMEM_LIMIT)


def _dot(a, b, dims):
    return lax.dot_general(a.astype(BF16), b.astype(BF16), dims, preferred_element_type=F32)


def _bf16_pieces(v):
    hi = v.astype(BF16)
    rest = v - hi.astype(F32)
    mid = rest.astype(BF16)
    return hi, mid, (rest - mid.astype(F32)).astype(BF16)


def _dot_exact(a, b, ones="a"):
    if ones == "a":
        sel = a.astype(BF16)
        parts = [lax.dot_general(sel, p, NN, preferred_element_type=F32) for p in _bf16_pieces(b)]
    else:
        sel = b.astype(BF16)
        parts = [lax.dot_general(p, sel, NN, preferred_element_type=F32) for p in _bf16_pieces(a)]
    return (parts[2] + parts[1]) + parts[0]


def _sigmoid(v):
    return 1.0 / (1.0 + jnp.exp(-v))


class _Rider(typing.NamedTuple):
    operands: list
    out_shapes: list
    aliases: dict
    sems: list
    start: typing.Callable
    finish: typing.Callable


def _call(body, name, grid, in_specs, out_specs, out_shape, operands, scratch=(), sem=(), rider=None, prefetch=0):
    multi = isinstance(out_shape, (list, tuple))

    def launch(kernel, in_specs, out_specs, out_shape, scratch, aliases, sem, args):
        if prefetch:
            how = dict(grid_spec=pltpu.PrefetchScalarGridSpec(
                num_scalar_prefetch=prefetch, grid=grid, in_specs=in_specs, out_specs=out_specs,
                scratch_shapes=scratch))
        else:
            how = dict(grid=grid, in_specs=in_specs, out_specs=out_specs, scratch_shapes=scratch)
        return pl.pallas_call(kernel, name=name, out_shape=out_shape, input_output_aliases=aliases,
                              compiler_params=_cparams(*sem), **how)(*args)

    if rider is None:
        return launch(body, in_specs, out_specs, out_shape, list(scratch), {}, sem, operands)
    outs = list(out_shape) if multi else [out_shape]
    ospecs = list(out_specs) if multi else [out_specs]
    n_in, n_out, n_scr = len(operands) - prefetch, len(outs), len(scratch)
    ri, ro = len(rider.operands), len(rider.out_shapes)

    def wrapped(*refs):
        scalars, refs = refs[:prefetch], refs[prefetch:]
        o0 = n_in + ri
        s0 = o0 + n_out + ro
        rin, rout, rsem = refs[n_in:o0], refs[o0 + n_out:s0], refs[s0 + n_scr:]
        ids = [pl.program_id(a) for a in range(len(grid))]
        first = functools.reduce(jnp.logical_and, [i == 0 for i in ids])
        last = functools.reduce(jnp.logical_and, [i == g - 1 for i, g in zip(ids, grid)])

        @pl.when(first)
        def _():
            rider.start(rin, rout, rsem)

        body(*scalars, *refs[:n_in], *refs[o0:o0 + n_out], *refs[s0:s0 + n_scr])

        @pl.when(last)
        def _():
            rider.finish(rin, rout, rsem)

    hbm = pl.BlockSpec(memory_space=pl.ANY)
    res = launch(wrapped, list(in_specs) + [hbm] * ri, ospecs + [hbm] * ro, outs + list(rider.out_shapes),
                 list(scratch) + list(rider.sems),
                 {prefetch + n_in + k: n_out + v for k, v in rider.aliases.items()},
                 ("arbitrary",) * len(grid), (*operands, *rider.operands))
    main = list(res[:n_out])
    return (main if multi else main[0]), list(res[n_out:])


class _Tail(typing.NamedTuple):
    fn: typing.Callable
    operands: list
    in_specs: list


def _mm(name, operands, dims, grid, in_specs, o_spec, out_shape, rider=None, tail=None):
    npairs = len(operands) // 2
    extra = [] if tail is None else list(tail.operands)
    nin = 2 * npairs + len(extra)

    def body(*refs):
        t = None
        for i in range(npairs):
            a, b = refs[2 * i], refs[2 * i + 1]
            parts = [(a[s], b[s]) for s in range(a.shape[0])] if len(a.shape) == 3 else [(a[...], b[...])]
            for pa, pb in parts:
                d = _dot(pa, pb, dims)
                t = d if t is None else t + d
        if tail is None:
            refs[nin][...] = t.astype(refs[nin].dtype)
        else:
            tail.fn(t, refs[2 * npairs:nin], refs[nin:])

    sem = ("parallel" if tail is None else "arbitrary",) * len(grid)
    specs = list(in_specs) + ([] if tail is None else list(tail.in_specs))
    return _call(body, name, grid, specs, o_spec, out_shape, list(operands) + extra, (), sem, rider)


class _FfnW(typing.NamedTuple):
    gu: jax.Array
    g0: int
    dn: jax.Array
    d0: int


def _ffn_up(name, n, w, rider=None):
    def body(n_ref, wg_ref, wu_ref, fg_ref, fu_ref, a_ref):
        nb = n_ref[...]
        g = _dot(nb, wg_ref[...], NT)
        u = _dot(nb, wu_ref[...], NT)
        sg = _sigmoid(g)
        silu = g * sg
        fg_ref[...] = (u * (sg * (1.0 + g * (1.0 - sg)))).astype(BF16)
        fu_ref[...] = silu.astype(BF16)
        a_ref[...] = (silu * u).astype(BF16)

    out = jax.ShapeDtypeStruct((NSH, S, FS), BF16)
    ospec = pl.BlockSpec((None, TS, FS), lambda s, i: (s, i, 0))
    return _call(
        body, name, (NSH, S // TS),
        [pl.BlockSpec((TS, D), lambda s, i: (i, 0)),
         pl.BlockSpec((None, None, FS, D), lambda s, i: (s, w.g0, 0, 0)),
         pl.BlockSpec((None, None, FS, D), lambda s, i: (s, w.g0 + 1, 0, 0))],
        [ospec, ospec, ospec], [out, out, out], (n, w.gu, w.gu), sem=("parallel", "parallel"), rider=rider)


def _ffn_dact(name, dh, w, fgate, fup, rider=None):
    def body(dh_ref, wd_ref, fg_ref, fu_ref, dg_ref, du_ref):
        da = _dot(dh_ref[...], wd_ref[...], NT)
        dg_ref[...] = (da * fg_ref[...].astype(F32)).astype(BF16)
        du_ref[...] = (da * fu_ref[...].astype(F32)).astype(BF16)

    out = jax.ShapeDtypeStruct((NSH, S, FS), BF16)
    aspec = pl.BlockSpec((None, TS, FS), lambda s, i: (s, i, 0))
    return _call(
        body, name, (NSH, S // TS),
        [pl.BlockSpec((TS, D), lambda s, i: (i, 0)),
         pl.BlockSpec((None, None, FS, D), lambda s, i: (s, w.d0, 0, 0)), aspec, aspec],
        [aspec, aspec], [out, out], (dh, w.dn, fgate, fup), sem=("parallel", "parallel"), rider=rider)


def _rstd(v):
    return lax.rsqrt(jnp.mean(v * v, axis=-1, keepdims=True) + EPS)


def _row_spec():
    return pl.BlockSpec((TR, D), lambda i: (i, 0))


def _vec_spec():
    return pl.BlockSpec((1, D), lambda i: (0, 0))


def _acc_rows(ref, v):
    @pl.when(pl.program_id(0) == 0)
    def _():
        ref[...] = jnp.zeros_like(ref)
    ref[...] += jnp.sum(v, axis=0, keepdims=True)


def _prenorm(name, x, g):
    def body(x_ref, g_ref, n_ref):
        xv = x_ref[...]
        n_ref[...] = (xv * _rstd(xv) * g_ref[...]).astype(BF16)

    return pl.pallas_call(
        body, name=name, grid=(S // TR,), in_specs=[_row_spec(), _vec_spec()], out_specs=_row_spec(),
        out_shape=jax.ShapeDtypeStruct((S, D), BF16), compiler_params=_cparams("parallel"),
    )(x, g)


ENTRY_STEPS = 4


def _prenorm_casts(name, x, g, slot, groups, rider):
    def body(s_ref, x_ref, g_ref, *refs):
        ins, outs = refs[:len(refs) - len(groups) - 1], refs[len(refs) - len(groups) - 1:]
        xv = x_ref[...]
        outs[0][...] = (xv * _rstd(xv) * g_ref[...]).astype(BF16)
        at = 0
        for (arrs, _), out in zip(groups, outs[1:]):
            for k in range(len(arrs)):
                out[k] = ins[at + k][...].astype(BF16)
            at += len(arrs)

    rows = pl.BlockSpec((S // ENTRY_STEPS, D), lambda i, sr: (i, 0))
    in_specs, out_specs, out_shapes = [rows, pl.BlockSpec((1, D), lambda i, sr: (0, 0))], [rows], [_rows_bf16()]
    for arrs, axis in groups:
        r, c = arrs[0].shape
        if axis == 0:
            blk, at, at_out = (r // ENTRY_STEPS, c), (lambda i, sr: (i, 0)), (lambda i, sr: (sr[0], 0, i, 0))
        else:
            blk, at, at_out = (r, c // ENTRY_STEPS), (lambda i, sr: (0, i)), (lambda i, sr: (sr[0], 0, 0, i))
        in_specs += [pl.BlockSpec(blk, at)] * len(arrs)
        out_specs.append(pl.BlockSpec((None, len(arrs)) + blk, at_out))
        out_shapes.append(jax.ShapeDtypeStruct((NSH, len(arrs), r, c), BF16))
    return _call(body, name, (ENTRY_STEPS,), in_specs, out_specs, out_shapes,
                 [slot, x, g] + [a for arrs, _ in groups for a in arrs], rider=rider, prefetch=1)


def _rows_spec(rows):
    return pl.BlockSpec((rows, D), lambda i: (i, 0))


def _rows_f32():
    return jax.ShapeDtypeStruct((S, D), F32)


def _rows_bf16():
    return jax.ShapeDtypeStruct((S, D), BF16)


def _vec_f32():
    return jax.ShapeDtypeStruct((1, D), F32)


def _tail_postres(rows, x, p, alpha, gnext):
    def fn(h, ins, outs):
        x_ref, p_ref, g_ref = ins
        h_ref, xo_ref, n_ref = outs
        h_ref[...] = h
        xo = x_ref[...] + alpha * (h * _rstd(h) * p_ref[...])
        xo_ref[...] = xo
        n_ref[...] = (xo * _rstd(xo) * g_ref[...]).astype(BF16)

    rs = _rows_spec(rows)
    return (_Tail(fn, [x, p, gnext], [rs, _vec_spec(), _vec_spec()]), [rs, rs, rs],
            [_rows_f32(), _rows_f32(), _rows_bf16()])


def _tail_final(rows, x, p, tgt, alpha):
    def fn(h, ins, outs):
        x_ref, p_ref, t_ref = ins
        dy_ref, dh_ref, dp_ref, loss_ref = outs
        r = _rstd(h)
        hn = h * r
        pv = p_ref[...]
        e = x_ref[...] + alpha * (hn * pv) - t_ref[...]
        dy = e * (1.0 / D)
        dy_ref[...] = dy
        du = alpha * dy * pv
        dh_ref[...] = (r * (du - hn * jnp.mean(du * hn, axis=-1, keepdims=True))).astype(BF16)
        _acc_rows(dp_ref, alpha * dy * hn)
        part = 0.5 * jnp.sum(jnp.mean(e * e, axis=-1, keepdims=True), axis=0, keepdims=True)
        _acc_rows(loss_ref, jnp.broadcast_to(part, (1, 128)))

    rs = _rows_spec(rows)
    return (_Tail(fn, [x, p, tgt], [rs, _vec_spec(), rs]),
            [rs, rs, _vec_spec(), pl.BlockSpec((1, 128), lambda i: (0, 0))],
            [_rows_f32(), _rows_bf16(), _vec_f32(), jax.ShapeDtypeStruct((1, 128), F32)])


def _norm_bwd(dn, xv, g_ref, dg_ref):
    r = _rstd(xv)
    xn = xv * r
    dng = dn * g_ref[...]
    _acc_rows(dg_ref, dn * xn)
    return r * (dng - xn * jnp.mean(dng * xn, axis=-1, keepdims=True))


def _tail_mid_bwd(rows, dres, x, g, h, p, alpha):
    def fn(dn, ins, outs):
        dr_ref, x_ref, g_ref, h_ref, p_ref = ins
        dx_ref, dh_ref, dg_ref, dp_ref = outs
        dx = dr_ref[...] + _norm_bwd(dn, x_ref[...], g_ref, dg_ref)
        dx_ref[...] = dx
        hv = h_ref[...]
        r = _rstd(hv)
        hn = hv * r
        du = alpha * dx * p_ref[...]
        dh_ref[...] = (r * (du - hn * jnp.mean(du * hn, axis=-1, keepdims=True))).astype(BF16)
        _acc_rows(dp_ref, alpha * dx * hn)

    rs = _rows_spec(rows)
    return (_Tail(fn, [dres, x, g, h, p], [rs, rs, _vec_spec(), rs, _vec_spec()]),
            [rs, rs, _vec_spec(), _vec_spec()], [_rows_f32(), _rows_bf16(), _vec_f32(), _vec_f32()])


def _tail_first_bwd(rows, dres, x, g):
    def fn(dn, ins, outs):
        dr_ref, x_ref, g_ref = ins
        dx_ref, dg_ref = outs
        dx_ref[...] = dr_ref[...] + _norm_bwd(dn, x_ref[...], g_ref, dg_ref)

    rs = _rows_spec(rows)
    return (_Tail(fn, [dres, x, g], [rs, rs, _vec_spec()]), [rs, _vec_spec()], [_rows_f32(), _vec_f32()])


def _rotate(t, c128, s128, sign, scale):
    width = t.shape[1]
    c = jnp.tile(c128, (1, width // 128))
    sn = jnp.tile(s128, (1, width // 128))
    lane = lax.broadcasted_iota(jnp.int32, t.shape, 1) & (HD - 1)
    rot = jnp.where(lane < HD // 2, -pltpu.roll(t, width - HD // 2, 1), pltpu.roll(t, HD // 2, 1))
    return (t * c + sign * (rot * sn)) * scale


def _rows_to_blocks(y):
    out = []
    for j in range(NKV):
        yt = y[:, QCOLS * j:QCOLS * (j + 1)].T
        out.append(jnp.concatenate([yt[HD * g:HD * (g + 1)] for g in range(NQ_PER_KV)], axis=1))
    return out


def _blocks_to_rows(blocks):
    cols = []
    for b in blocks:
        stacked = jnp.concatenate([b[:, 128 * g:128 * (g + 1)] for g in range(NQ_PER_KV)], axis=0)
        cols.append(stacked.T)
    return jnp.concatenate(cols, axis=1)


def _rope_q(proj, cos, sin):
    def body(t_ref, c_ref, s_ref, o_ref):
        y = _rotate(t_ref[...], c_ref[...], s_ref[...], 1.0, HD ** -0.5)
        for j, blk in enumerate(_rows_to_blocks(y)):
            o_ref[j] = blk.astype(BF16)

    return pl.pallas_call(
        body, name="rope_q", grid=(NCH,),
        in_specs=[pl.BlockSpec((128, D), lambda i: (i, 0)),
                  pl.BlockSpec((128, 128), lambda i: (i, 0)), pl.BlockSpec((128, 128), lambda i: (i, 0))],
        out_specs=pl.BlockSpec((NKV, None, HD, QROWS), lambda i: (0, i, 0, 0)),
        out_shape=jax.ShapeDtypeStruct((NKV, NCH, HD, QROWS), BF16), compiler_params=_cparams("parallel"),
    )(proj, cos, sin)


def _rope_dq(dqt, cos, sin, dproj):
    def body(t_ref, c_ref, s_ref, buf_ref, o_ref):
        t = _blocks_to_rows([t_ref[j] for j in range(NKV)])
        o_ref[...] = _rotate(t, c_ref[...], s_ref[...], -1.0, HD ** -0.5).astype(BF16)

    return pl.pallas_call(
        body, name="rope_dq", grid=(NCH,),
        in_specs=[pl.BlockSpec((NKV, None, HD, QROWS), lambda i: (0, i, 0, 0)),
                  pl.BlockSpec((128, 128), lambda i: (i, 0)), pl.BlockSpec((128, 128), lambda i: (i, 0)),
                  pl.BlockSpec(memory_space=pl.ANY)],
        out_specs=pl.BlockSpec((128, D), lambda i: (i, 0)),
        out_shape=jax.ShapeDtypeStruct(dproj.shape, BF16), input_output_aliases={3: 0},
        compiler_params=_cparams("parallel"),
    )(dqt, cos, sin, dproj)


def _rope_dkv(dkt, dvt, cos, sin, dproj):
    def body(k_ref, v_ref, c_ref, s_ref, buf_ref, o_ref):
        dk = jnp.concatenate([k_ref[j] for j in range(NKV)], axis=0).T
        dv = jnp.concatenate([v_ref[j] for j in range(NKV)], axis=0).T
        dk = _rotate(dk, c_ref[...], s_ref[...], -1.0, 1.0)
        o_ref[...] = jnp.concatenate([dk, dv], axis=1).astype(BF16)

    tspec = pl.BlockSpec((NKV, HD, 128), lambda i: (0, 0, i))
    return pl.pallas_call(
        body, name="rope_dkv", grid=(NCH,),
        in_specs=[tspec, tspec, pl.BlockSpec((128, 128), lambda i: (i, 0)), pl.BlockSpec((128, 128), lambda i: (i, 0)),
                  pl.BlockSpec(memory_space=pl.ANY)],
        out_specs=pl.BlockSpec((128, 2 * KVW), lambda i: (i, D // (2 * KVW))),
        out_shape=jax.ShapeDtypeStruct(dproj.shape, BF16), input_output_aliases={4: 0},
        compiler_params=_cparams("parallel"),
    )(dkt, dvt, cos, sin, dproj)


def _rope_kv(proj, cos, sin):
    def body(t_ref, c_ref, s_ref, k_ref, v_ref, kt_ref, vt_ref):
        t = t_ref[...]
        k = _rotate(t[:, :KVW], c_ref[...], s_ref[...], 1.0, 1.0).astype(BF16)
        v = t[:, KVW:].astype(BF16)
        k_ref[...] = k
        v_ref[...] = v
        kt, vt = k.astype(F32).T, v.astype(F32).T
        for j in range(NKV):
            kt_ref[j] = kt[HD * j:HD * (j + 1)].astype(BF16)
            vt_ref[j] = vt[HD * j:HD * (j + 1)].astype(BF16)

    rows = pl.BlockSpec((128, KVW), lambda i: (i, 0))
    tspec = pl.BlockSpec((NKV, HD, 128), lambda i: (0, 0, i))
    return pl.pallas_call(
        body, name="rope_kv", grid=(NCH,),
        in_specs=[pl.BlockSpec((128, 2 * KVW), lambda i: (i, D // (2 * KVW))),
                  pl.BlockSpec((128, 128), lambda i: (i, 0)), pl.BlockSpec((128, 128), lambda i: (i, 0))],
        out_specs=[rows, rows, tspec, tspec],
        out_shape=[jax.ShapeDtypeStruct((S, KVW), BF16)] * 2 + [jax.ShapeDtypeStruct((NKV, HD, S), BF16)] * 2,
        compiler_params=_cparams("parallel"),
    )(proj, cos, sin)


QROWS = NQ_PER_KV * 128


NBIAS = NCH + 1
KV_PER_STEP = 4


def _bias_table():
    db = lax.broadcasted_iota(jnp.int32, (NBIAS, 128, QROWS), 0) - 1
    ki = lax.broadcasted_iota(jnp.int32, (NBIAS, 128, QROWS), 1)
    qi = lax.broadcasted_iota(jnp.int32, (NBIAS, 128, QROWS), 2) & 127
    d = db * 128 + qi - ki
    cnt = ((d <= 128).astype(F32) + (((d & 3) == 0) & (d <= 512)).astype(F32) + ((d & 15) == 0).astype(F32))
    return jnp.where((d >= 0) & (cnt > 0.0), jnp.log(jnp.maximum(cnt, 1.0)), NEG)


def _qt_spec():
    return pl.BlockSpec((None, None, HD, QROWS), lambda j, i: (j, i, 0, 0))


def _stat_spec():
    return pl.BlockSpec((None, None, 1, QROWS), lambda j, i: (j, i, 0, 0))


def _attn_fwd(qt, kh, vt, bias, rider=None):
    def body(q_ref, k_ref, v_ref, b_ref, o_ref, lse_ref, rows_ref, m_ref, l_ref, acc_ref):
        qb = pl.program_id(1)
        m_ref[...] = jnp.full_like(m_ref, NEG)
        l_ref[...] = jnp.zeros_like(l_ref)
        acc_ref[...] = jnp.zeros_like(acc_ref)

        def keys(off, size, bias_):
            for h in range(KV_PER_STEP):
                m = m_ref[h]
                s = _dot(k_ref[h, pl.ds(off, size), :], q_ref[h], NN) + bias_
                m_new = jnp.maximum(m, jnp.max(s, axis=0, keepdims=True))
                p = jnp.exp(s - m_new)
                a = jnp.exp(m - m_new)
                m_ref[h] = m_new
                l_ref[h] = a * l_ref[h] + jnp.sum(p, axis=0, keepdims=True)
                acc_ref[h] = a * acc_ref[h] + _dot(v_ref[h, :, pl.ds(off, size)], p, NN)

        def blocks(first, count):
            bias_ = jnp.concatenate([b_ref[qb - first - j + 1] for j in range(count)], axis=0)
            keys(pl.multiple_of(first * 128, 128), 128 * count, bias_)

        nkb = qb + 1
        @pl.loop(0, nkb // 4)
        def _(i):
            blocks(4 * i, 4)

        @pl.when(nkb % 4 >= 2)
        def _():
            blocks(nkb // 4 * 4, 2)

        @pl.when(nkb % 2 == 1)
        def _():
            blocks(qb, 1)

        outs = []
        for h in range(KV_PER_STEP):
            outs.append(acc_ref[h] / l_ref[h])
            o_ref[h] = outs[h]
            lse_ref[h] = m_ref[h] + jnp.log(l_ref[h])
        rows_ref[...] = _blocks_to_rows(outs).astype(BF16)

    kvs = KV_PER_STEP
    qspec = pl.BlockSpec((kvs, None, HD, QROWS), lambda j, i: (j, i, 0, 0))
    return _call(
        body, "attn_fwd", (NKV // kvs, NCH),
        [qspec, pl.BlockSpec((kvs, S, HD), lambda j, i: (j, 0, 0)),
         pl.BlockSpec((kvs, HD, S), lambda j, i: (j, 0, 0)),
         pl.BlockSpec((NBIAS, 128, QROWS), lambda j, i: (0, 0, 0))],
        [qspec, pl.BlockSpec((kvs, None, 1, QROWS), lambda j, i: (j, i, 0, 0)),
         pl.BlockSpec((128, QCOLS * kvs), lambda j, i: (i, j))],
        [jax.ShapeDtypeStruct((NKV, NCH, HD, QROWS), F32), jax.ShapeDtypeStruct((NKV, NCH, 1, QROWS), F32),
         jax.ShapeDtypeStruct((S, 2 * D), BF16)],
        (qt, kh, vt, bias),
        [pltpu.VMEM((kvs, 1, QROWS), F32), pltpu.VMEM((kvs, 1, QROWS), F32), pltpu.VMEM((kvs, HD, QROWS), F32)],
        ("parallel", "parallel"), rider)


def _attn_delta(ot, dot_):
    def body(o_ref, do_ref, dl_ref):
        dl_ref[...] = jnp.sum(o_ref[...] * do_ref[...].astype(F32), axis=1, keepdims=True)

    spec = pl.BlockSpec((None, NCH, HD, QROWS), lambda j: (j, 0, 0, 0))
    return pl.pallas_call(
        body, name="attn_delta", grid=(NKV,), in_specs=[spec, spec],
        out_specs=pl.BlockSpec((None, NCH, 1, QROWS), lambda j: (j, 0, 0, 0)),
        out_shape=jax.ShapeDtypeStruct((NKV, NCH, 1, QROWS), F32), compiler_params=_cparams("parallel"),
    )(ot, dot_)


def _attn_bwd(qt, kh, kt, vh, dot_, lse, delta, bias, rider=None):
    def body(qt_ref, k_ref, kt_ref, v_ref, dot_ref, lse_ref, dl_ref, b_ref, dq_ref, dk_ref, dv_ref):
        kp = pl.program_id(1)

        @pl.when(kp == 0)
        def _():
            dq_ref[...] = jnp.zeros_like(dq_ref)

        dk_ref[...] = jnp.zeros_like(dk_ref)
        dv_ref[...] = jnp.zeros_like(dv_ref)

        @pl.loop(2 * kp, NCH // 2)
        def _(j):
            for h in range(KV_PER_STEP):
                k, kt_, v = k_ref[h], kt_ref[h], v_ref[h]
                for qb in (2 * j, 2 * j + 1):
                    bias2 = jnp.concatenate([b_ref[jnp.maximum(qb - 4 * kp - t + 1, 0)] for t in range(4)], axis=0)
                    st = _dot(k, qt_ref[h, qb], NN) + bias2
                    pt = jnp.exp(st - lse_ref[h, qb])
                    dst = pt * (_dot(v, dot_ref[h, qb], NN) - dl_ref[h, qb])
                    dq_ref[h, qb] += _dot(kt_, dst, NN)
                    dk_ref[h] += _dot(qt_ref[h, qb], dst, NT)
                    dv_ref[h] += _dot(dot_ref[h, qb], pt, NT)

    kvs = KV_PER_STEP
    tspec = pl.BlockSpec((kvs, NCH, HD, QROWS), lambda j, i: (j, 0, 0, 0))
    kspec = pl.BlockSpec((kvs, 512, HD), lambda j, i: (j, i, 0))
    ktspec = pl.BlockSpec((kvs, HD, 512), lambda j, i: (j, 0, i))
    sspec = pl.BlockSpec((kvs, NCH, 1, QROWS), lambda j, i: (j, 0, 0, 0))
    return _call(
        body, "attn_bwd", (NKV // kvs, NCH // 4),
        [tspec, kspec, ktspec, kspec, tspec, sspec, sspec,
         pl.BlockSpec((NBIAS, 128, QROWS), lambda j, i: (0, 0, 0))],
        [tspec, ktspec, ktspec],
        [jax.ShapeDtypeStruct((NKV, NCH, HD, QROWS), F32),
         jax.ShapeDtypeStruct((NKV, HD, S), F32), jax.ShapeDtypeStruct((NKV, HD, S), F32)],
        (qt, kh, kt, vh, dot_, lse, delta, bias), sem=("parallel", "arbitrary"), rider=rider)


CONV_BLK = 256
CONV_COL0 = 1536 // CONV_BLK


CONV_ROWS = 128


def _conv_fwd(proj, convw, convb):
    trips = S // CONV_ROWS

    def body(u_ref, w_ref, b_ref, o_ref, y_ref):
        @pl.loop(0, trips)
        def _(c):
            t0 = pl.multiple_of(c * CONV_ROWS, CONV_ROWS)
            before = pl.multiple_of(jnp.maximum(t0 - 8, 0), 8)
            ext = jnp.concatenate([jnp.where(c == 0, 0.0, u_ref[pl.ds(before, 8), :]),
                                   u_ref[pl.ds(t0, CONV_ROWS), :]], axis=0)
            y = b_ref[...] + w_ref[CONV_K - 1:CONV_K, :] * ext[8:]
            for j in range(1, CONV_K):
                y = y + w_ref[CONV_K - 1 - j:CONV_K - j, :] * pltpu.roll(ext, j, 0)[8:]
            y_ref[pl.ds(t0, CONV_ROWS), :] = y
            o_ref[pl.ds(t0, CONV_ROWS), :] = y * _sigmoid(y)

    out = pl.BlockSpec((S, CONV_BLK), lambda i: (0, i))
    return pl.pallas_call(
        body, name="conv_fwd", grid=(CONV_C // CONV_BLK,),
        in_specs=[pl.BlockSpec((S, CONV_BLK), lambda i: (0, CONV_COL0 + i)),
                  pl.BlockSpec((CONV_K, CONV_BLK), lambda i: (0, i)),
                  pl.BlockSpec((1, CONV_BLK), lambda i: (0, i))],
        out_specs=[out, out], out_shape=[jax.ShapeDtypeStruct((S, CONV_C), F32)] * 2,
        compiler_params=_cparams("parallel"),
    )(proj, convw, convb)


def _conv_bwd(dact, ypre, proj, convw, dproj):
    trips = S // CONV_ROWS

    def body(da_ref, y_ref, u_ref, w_ref, buf_ref, du_ref, dw_ref, db_ref):
        dw_ref[...] = jnp.zeros_like(dw_ref)
        db_ref[...] = jnp.zeros_like(db_ref)
        r8 = lax.broadcasted_iota(jnp.int32, (8, CONV_BLK), 0)

        def dy_of(rows):
            y = y_ref[rows, :]
            sg = _sigmoid(y)
            return da_ref[rows, :] * (sg * (1.0 + y * (1.0 - sg)))

        @pl.loop(0, trips)
        def _(c):
            t0 = pl.multiple_of(c * CONV_ROWS, CONV_ROWS)
            after = pl.multiple_of(jnp.minimum(t0 + CONV_ROWS, S - 8), 8)
            ext = jnp.concatenate([dy_of(pl.ds(t0, CONV_ROWS)),
                                   jnp.where(c == trips - 1, 0.0, dy_of(pl.ds(after, 8)))], axis=0)
            u = u_ref[pl.ds(t0, CONV_ROWS), :]
            du, dw = None, jnp.zeros((8, CONV_BLK), F32)
            for j in range(CONV_K):
                dyj = (ext if j == 0 else pltpu.roll(ext, CONV_ROWS + 8 - j, 0))[:CONV_ROWS]
                term = w_ref[CONV_K - 1 - j:CONV_K - j, :] * dyj
                du = term if du is None else du + term
                dw = dw + jnp.where(r8 == CONV_K - 1 - j, jnp.sum(dyj * u, axis=0, keepdims=True), 0.0)
            du_ref[pl.ds(t0, CONV_ROWS), :] = du.astype(BF16)
            dw_ref[...] += dw
            db_ref[...] += jnp.sum(ext[:CONV_ROWS], axis=0, keepdims=True)

    return pl.pallas_call(
        body, name="conv_bwd", grid=(CONV_C // CONV_BLK,),
        in_specs=[pl.BlockSpec((S, CONV_BLK), lambda i: (0, i)), pl.BlockSpec((S, CONV_BLK), lambda i: (0, i)),
                  pl.BlockSpec((S, CONV_BLK), lambda i: (0, CONV_COL0 + i)),
                  pl.BlockSpec((CONV_K, CONV_BLK), lambda i: (0, i)), pl.BlockSpec(memory_space=pl.ANY)],
        out_specs=[pl.BlockSpec((S, CONV_BLK), lambda i: (0, CONV_COL0 + i)),
                   pl.BlockSpec((8, CONV_BLK), lambda i: (0, i)), pl.BlockSpec((1, CONV_BLK), lambda i: (0, i))],
        out_shape=[jax.ShapeDtypeStruct(dproj.shape, BF16), jax.ShapeDtypeStruct((8, CONV_C), F32),
                   jax.ShapeDtypeStruct((1, CONV_C), F32)],
        input_output_aliases={4: 0}, compiler_params=_cparams("parallel"),
    )(dact, ypre, proj, convw, dproj)


NPAIR = 8


def _ssd_scalars(dtr_ref, dtb_ref, alog_ref):
    z = dtr_ref[...] + dtb_ref[...]
    dt = jnp.maximum(z, 0.0) + jnp.log(1.0 + jnp.exp(-jnp.abs(z)))
    a = -jnp.exp(alog_ref[...])
    r = lax.broadcasted_iota(jnp.int32, (128, 128), 0)
    c = lax.broadcasted_iota(jnp.int32, (128, 128), 1)
    tri = (r >= c).astype(F32)
    cs = _dot_exact(tri, dt * a)
    return z, dt, a, cs, r, c


def _by_lane(cs, dt):
    head = lax.broadcasted_iota(jnp.int32, (128, SSM_W), 0)
    lane = lax.broadcasted_iota(jnp.int32, (128, SSM_W), 1)
    sel = (head == lane // HD).astype(F32)
    cs_l = _dot_exact(cs, sel, "b")
    last_l = cs_l[127:128, :]
    return sel, jnp.exp(cs_l), jnp.exp(last_l - cs_l), _dot_exact(dt, sel, "b")


def _pair_terms(cs, h1, h2):
    return (cs[:, h1:h1 + 1], cs[:, h2:h2 + 1],
            jnp.exp(cs[127:128, h1:h1 + 1]), jnp.exp(cs[127:128, h2:h2 + 1]))


def _gate_norm(y, zv, w):
    yg = y * (zv * _sigmoid(zv))
    outs, rs = [], []
    for g in range(2):
        blk = yg[:, 512 * g:512 * (g + 1)]
        r = lax.rsqrt(jnp.mean(blk * blk, axis=-1, keepdims=True) + EPS)
        outs.append(blk * r)
        rs.append(r)
    return jnp.concatenate(outs, axis=1), rs, yg


def _ssd_fwd(xbc, proj, dtb, alog, dskip_l, ssmw, mixed):
    def body(x_ref, b_ref, c_ref, dtr_ref, z_ref, dtb_ref, alog_ref, dsk_ref, w_ref, buf_ref,
             y_ref, yn_ref, hp_ref, h_ref):
        @pl.when(pl.program_id(0) == 0)
        def _():
            h_ref[...] = jnp.zeros_like(h_ref)

        _, dt, _, cs, r, c = _ssd_scalars(dtr_ref, dtb_ref, alog_ref)
        cst = cs.T
        causal = r >= c
        lo = c < HD
        _, e_all, dte_all, dt_all = _by_lane(cs, dt)
        hp_ref[...] = h_ref[...]
        for g in range(2):
            bg = b_ref[:, 128 * g:128 * (g + 1)]
            cg = c_ref[:, 128 * g:128 * (g + 1)]
            cb = _dot(cg, bg, NT)
            for j in range(4):
                pj = 4 * g + j
                h1, h2 = 2 * pj, 2 * pj + 1
                sl = slice(128 * pj, 128 * (pj + 1))
                xp = x_ref[:, sl]
                c1, c2, cd1, cd2 = _pair_terms(cs, h1, h2)
                e_l, dte_l = e_all[:, sl], dte_all[:, sl]
                xdt = xp * dt_all[:, sl]
                m1 = cb * jnp.exp(jnp.where(causal, c1 - cst[h1:h1 + 1, :], NEG))
                m2 = cb * jnp.exp(jnp.where(causal, c2 - cst[h2:h2 + 1, :], NEG))
                yd = jnp.where(lo, _dot(m1, xdt, NN), _dot(m2, xdt, NN))
                hp = h_ref[pj]
                yo = _dot(cg, hp, NT) * e_l
                st = _dot(xdt * dte_l, bg, TN)
                h_ref[pj] = hp * jnp.where(r < HD, cd1, cd2) + st
                y_ref[:, sl] = yd + yo + dsk_ref[:, sl] * xp
        yn, _, _ = _gate_norm(y_ref[...], z_ref[...], w_ref[...])
        yn_ref[...] = (yn * w_ref[...]).astype(BF16)

    return pl.pallas_call(
        body, name="ssd_fwd", grid=(NCH,),
        in_specs=[pl.BlockSpec((128, SSM_W), lambda i: (i, 0)),
                  pl.BlockSpec((128, 256), lambda i: (i, 4)), pl.BlockSpec((128, 256), lambda i: (i, 5)),
                  pl.BlockSpec((128, 128), lambda i: (i, COL_DT // 128)),
                  pl.BlockSpec((128, SSM_W), lambda i: (i, 3)),
                  pl.BlockSpec((1, 128), lambda i: (0, 0)), pl.BlockSpec((1, 128), lambda i: (0, 0)),
                  pl.BlockSpec((1, SSM_W), lambda i: (0, 0)), pl.BlockSpec((1, SSM_W), lambda i: (0, 0)),
                  pl.BlockSpec(memory_space=pl.ANY)],
        out_specs=[pl.BlockSpec((128, SSM_W), lambda i: (i, 0)), pl.BlockSpec((128, SSM_W), lambda i: (i, 1)),
                   pl.BlockSpec((None, NPAIR, 128, 128), lambda i: (i, 0, 0, 0))],
        out_shape=[jax.ShapeDtypeStruct((S, SSM_W), F32), jax.ShapeDtypeStruct(mixed.shape, BF16),
                   jax.ShapeDtypeStruct((NCH, NPAIR, 128, 128), F32)],
        scratch_shapes=[pltpu.VMEM((NPAIR, 128, 128), F32)],
        input_output_aliases={9: 1}, compiler_params=_cparams("arbitrary"),
    )(xbc, xbc, xbc, proj, proj, dtb, alog, dskip_l, ssmw, mixed)


def _ssd_bwd(dmixed, y, xbc, proj, hprev, dtb, alog, dskip_l, ssmw, rider=None):
    def body(dyn_ref, y_ref, x_ref, b_ref, c_ref, dtr_ref, z_ref, hp_ref, dtb_ref, alog_ref, dsk_ref, w_ref,
             dxbc_ref, dz_ref, ddt_ref, dw_ref, dsc_ref, g_ref):
        @pl.when(pl.program_id(0) == 0)
        def _():
            g_ref[...] = jnp.zeros_like(g_ref)
            dsc_ref[...] = jnp.zeros_like(dsc_ref)

        z, dt, a, cs, r, c = _ssd_scalars(dtr_ref, dtb_ref, alog_ref)
        cst = cs.T
        causal = r >= c
        lo = c < HD

        yv = y_ref[...]
        zv = z_ref[...]
        wv = w_ref[...]
        ygn, rs, yg = _gate_norm(yv, zv, wv)
        dyn = dyn_ref[...]
        _acc_rows(dw_ref, dyn * ygn)
        dynw = dyn * wv
        parts = []
        for g in range(2):
            sl = slice(512 * g, 512 * (g + 1))
            a_g, n_g = dynw[:, sl], ygn[:, sl]
            parts.append(rs[g] * (a_g - n_g * jnp.mean(a_g * n_g, axis=-1, keepdims=True)))
        dyg = jnp.concatenate(parts, axis=1)
        sz = _sigmoid(zv)
        dz_ref[...] = (dyg * yv * (sz * (1.0 + zv * (1.0 - sz)))).astype(BF16)
        dy_all = dyg * (zv * sz)

        dcs_cols = jnp.zeros((128, 128), F32)
        dcs_rows = jnp.zeros((128, 128), F32)
        sel, e_all, dte_all, dt_all = _by_lane(cs, dt)
        x_all, b_all, c_all, dsk_all = x_ref[...], b_ref[...], c_ref[...], dsk_ref[...]
        hp_all, g_all = hp_ref[...], g_ref[...]
        g_new, dx_parts, db_parts, dc_parts = [], [], [], []
        dyx_parts, ryo_parts, qx_parts, dxx_parts, gh_parts = [], [], [], [], []
        for g in range(2):
            bg = b_all[:, 128 * g:128 * (g + 1)]
            cg = c_all[:, 128 * g:128 * (g + 1)]
            cb = _dot(cg, bg, NT)
            dcb = jnp.zeros((128, 128), F32)
            db_acc = jnp.zeros((128, NST), F32)
            dc_acc = jnp.zeros((128, NST), F32)
            for j in range(4):
                pj = 4 * g + j
                h1, h2 = 2 * pj, 2 * pj + 1
                sl = slice(128 * pj, 128 * (pj + 1))
                xp = x_all[:, sl]
                dyp = dy_all[:, sl]
                c1, c2, cd1, cd2 = _pair_terms(cs, h1, h2)
                e_l, dte_l, dt_l = e_all[:, sl], dte_all[:, sl], dt_all[:, sl]
                xdt = xp * dt_l
                hp = hp_all[pj]
                gp = g_all[pj]
                dyx_parts.append(dyp * xp)
                dzs = dyp * e_l
                dc_acc = dc_acc + _dot(dzs, hp, NN)
                ryo_parts.append(dyp * (_dot(cg, hp, NT) * e_l))
                qm = _dot(bg, gp, NT)
                dxdt = qm * dte_l
                qx_parts.append(qm * xdt)
                db_acc = db_acc + _dot(xdt * dte_l, gp, NN)
                gh_parts.append(gp * hp)
                g_new.append(_dot(dzs, cg, TN) + jnp.where(r < HD, cd1, cd2) * gp)
                for hh, ch, msk in ((h1, c1, lo), (h2, c2, jnp.logical_not(lo))):
                    lm = jnp.exp(jnp.where(causal, ch - cst[hh:hh + 1, :], NEG))
                    mm = cb * lm
                    dm = jnp.where(causal, _dot(jnp.where(msk, dyp, 0.0), xdt, NT), 0.0)
                    w = dm * mm
                    dcs_cols = dcs_cols + jnp.where(c == hh, jnp.sum(w, axis=1, keepdims=True), 0.0)
                    dcs_rows = dcs_rows + jnp.where(r == hh, jnp.sum(w, axis=0, keepdims=True), 0.0)
                    dcb = dcb + dm * lm
                    dxdt = dxdt + jnp.where(msk, _dot(mm, dyp, TN), 0.0)
                dxx_parts.append(dxdt * xp)
                dx_parts.append(dsk_all[:, sl] * dyp + dxdt * dt_l)
            db_parts.append(db_acc + _dot(dcb, cg, TN))
            dc_parts.append(dc_acc + _dot(dcb, bg, NN))
        g_ref[...] = jnp.stack(g_new)
        dxbc_ref[...] = jnp.concatenate(dx_parts + db_parts + dc_parts, axis=1)

        selt = (lax.broadcasted_iota(jnp.int32, (SSM_W, 128), 0) // HD
                == lax.broadcasted_iota(jnp.int32, (SSM_W, 128), 1)).astype(F32)

        def by_head(parts):
            return _dot_exact(jnp.concatenate(parts, axis=1), selt, "b")

        ddt_x = by_head(dxx_parts)
        dd_row = jnp.sum(by_head(dyx_parts), axis=0, keepdims=True)
        t_all = by_head(qx_parts) * jnp.exp(cs[127:128, :] - cs)
        gh = jnp.sum(_dot_exact(sel, jnp.concatenate(gh_parts, axis=0)), axis=1, keepdims=True)
        gh_row = jnp.broadcast_to(gh, (128, 128)).T[0:1, :]
        at_end = jnp.sum(t_all, axis=0, keepdims=True) + gh_row * jnp.exp(cs[127:128, :])
        dcs = by_head(ryo_parts) - t_all + dcs_cols + jnp.where(r == 127, at_end, 0.0) - dcs_rows.T
        dad = _dot_exact((c >= r).astype(F32), dcs)
        ddt = dad * a + ddt_x
        ddtr = jnp.where(c < 16, ddt * _sigmoid(z), 0.0)
        ddt_ref[...] = ddtr.astype(BF16)
        r8 = lax.broadcasted_iota(jnp.int32, (8, 128), 0)
        dsc_ref[...] += (jnp.where(r8 == 0, jnp.sum(ddtr, axis=0, keepdims=True), 0.0)
                         + jnp.where(r8 == 1, jnp.sum(dad * dt, axis=0, keepdims=True) * a, 0.0)
                         + jnp.where(r8 == 2, dd_row, 0.0))

    rev = NCH - 1
    return _call(
        body, "ssd_bwd", (NCH,),
        [pl.BlockSpec((128, SSM_W), lambda i: (rev - i, 0)),
         pl.BlockSpec((128, SSM_W), lambda i: (rev - i, 0)),
         pl.BlockSpec((128, SSM_W), lambda i: (rev - i, 0)),
         pl.BlockSpec((128, 256), lambda i: (rev - i, 4)), pl.BlockSpec((128, 256), lambda i: (rev - i, 5)),
         pl.BlockSpec((128, 128), lambda i: (rev - i, COL_DT // 128)),
         pl.BlockSpec((128, SSM_W), lambda i: (rev - i, 3)),
         pl.BlockSpec((None, NPAIR, 128, 128), lambda i: (rev - i, 0, 0, 0)),
         pl.BlockSpec((1, 128), lambda i: (0, 0)), pl.BlockSpec((1, 128), lambda i: (0, 0)),
         pl.BlockSpec((1, SSM_W), lambda i: (0, 0)), pl.BlockSpec((1, SSM_W), lambda i: (0, 0))],
        [pl.BlockSpec((128, CONV_C), lambda i: (rev - i, 0)),
         pl.BlockSpec((128, SSM_W), lambda i: (rev - i, 3)),
         pl.BlockSpec((128, 128), lambda i: (rev - i, 0)),
         pl.BlockSpec((1, SSM_W), lambda i: (0, 0)), pl.BlockSpec((8, 128), lambda i: (0, 0))],
        [jax.ShapeDtypeStruct((S, CONV_C), F32), jax.ShapeDtypeStruct((S, WIN_PAD), BF16),
         jax.ShapeDtypeStruct((S, 128), BF16), jax.ShapeDtypeStruct((1, SSM_W), F32),
         jax.ShapeDtypeStruct((8, 128), F32)],
        (dmixed, y, xbc, xbc, xbc, proj, proj, hprev, dtb, alog, dskip_l, ssmw),
        [pltpu.VMEM((NPAIR, 128, 128), F32)], ("arbitrary",), rider)


def _cast_stack(name, slot, arrs, tr, tc):
    n = len(arrs)
    rows, cols = arrs[0].shape

    def body(s_ref, *refs):
        for i in range(n):
            refs[n][i] = refs[i][...].astype(BF16)

    return pl.pallas_call(
        body, name=name,
        grid_spec=pltpu.PrefetchScalarGridSpec(
            num_scalar_prefetch=1, grid=(rows // tr, cols // tc),
            in_specs=[pl.BlockSpec((tr, tc), lambda i, j, sr: (i, j))] * n,
            out_specs=pl.BlockSpec((None, n, tr, tc), lambda i, j, sr: (sr[0], 0, i, j))),
        out_shape=jax.ShapeDtypeStruct((NSH, n, rows, cols), BF16),
        compiler_params=_cparams("parallel", "parallel"),
    )(slot, *arrs)


def _pair_sum(name, c_idx, ps, th):
    n = len(ps)
    _, rows, _ = ps[0].shape

    def body(c_ref, *refs):
        mine, whole, out, theirs = refs[:n], refs[n:2 * n], refs[2 * n:3 * n], refs[3 * n:4 * n]
        send, recv = refs[4 * n], refs[4 * n + 1]
        s, i = pl.program_id(0), pl.program_id(1)
        x, y, c, _ = _place()

        def copies(slot):
            return [_rcopy(whole[k].at[slot, :, pl.ds((1 - c) * HALF, HALF)], theirs[k].at[slot],
                           send.at[slot * n + k], recv.at[slot * n + k], (x, y, 1 - c)) for k in range(n)]

        @pl.when((s == 0) & (i == 0))
        def _():
            for slot in range(NSH):
                for cp in copies(slot):
                    cp.start()

        @pl.when(i == 0)
        def _():
            for slot in range(NSH):
                @pl.when(s == slot)
                def _():
                    for cp in copies(slot):
                        cp.wait()

        rows_i = slice(None) if th == rows else pl.ds(pl.multiple_of(i * th, th), th)
        for k in range(n):
            out[k][...] = (mine[k][...].astype(F32) + theirs[k][s, rows_i, :].astype(F32)).astype(BF16)

    spec = pl.BlockSpec((None, th, HALF), lambda s, i, cr: (s, i, 0))
    return pl.pallas_call(
        body, name=name,
        grid_spec=pltpu.PrefetchScalarGridSpec(
            num_scalar_prefetch=1, grid=(NSH, rows // th),
            in_specs=[pl.BlockSpec((None, th, HALF), lambda s, i, cr: (s, i, cr[0]))] * n + _any_specs(n),
            out_specs=[spec] * n,
            scratch_shapes=[pltpu.VMEM((NSH, rows, HALF), BF16)] * n
            + [pltpu.SemaphoreType.DMA((NSH * n,)), pltpu.SemaphoreType.DMA((NSH * n,))]),
        out_shape=[jax.ShapeDtypeStruct((NSH, rows, HALF), BF16)] * n,
        compiler_params=_cparams("arbitrary", "arbitrary"),
    )(c_idx, *ps, *ps)


def _pair_add(name, c_idx, ps, theirs, th):
    n = len(ps)
    _, rows, _ = ps[0].shape

    def body(c_ref, *refs):
        for k in range(n):
            refs[2 * n + k][...] = (refs[k][...].astype(F32) + refs[n + k][...].astype(F32)).astype(BF16)

    spec = pl.BlockSpec((None, th, HALF), lambda s, i, cr: (s, i, 0))
    return pl.pallas_call(
        body, name=name,
        grid_spec=pltpu.PrefetchScalarGridSpec(
            num_scalar_prefetch=1, grid=(NSH, rows // th),
            in_specs=[pl.BlockSpec((None, th, HALF), lambda s, i, cr: (s, i, cr[0]))] * n + [spec] * n,
            out_specs=[spec] * n),
        out_shape=[jax.ShapeDtypeStruct((NSH, rows, HALF), BF16)] * n,
        compiler_params=_cparams("parallel", "parallel"),
    )(c_idx, *ps, *theirs)


def _chip_sum(name, place, cs, ts, th):
    n = len(ts)
    _, rows, _ = ts[0].shape

    def body(p_ref, *refs):
        for i in range(n):
            t = refs[n + i][...].astype(F32)
            refs[2 * n + i][...] = ((refs[i][...].astype(F32) + t[0]) + t[1]) + t[2]

    return pl.pallas_call(
        body, name=name,
        grid_spec=pltpu.PrefetchScalarGridSpec(
            num_scalar_prefetch=1, grid=(rows // th,),
            in_specs=[pl.BlockSpec((None, th, HALF), lambda i, pr: (pr[0], i, 0))] * n
            + [pl.BlockSpec((3, th, HALF), lambda i, pr: (0, i, 0))] * n,
            out_specs=[pl.BlockSpec((th, HALF), lambda i, pr: (i, pr[1]))] * n),
        out_shape=[jax.ShapeDtypeStruct((rows, D), F32)] * n, compiler_params=_cparams("parallel"),
    )(place, *cs, *ts)


def _adamw(name, ws, gs, ms, vs, tr, tc):
    n = len(ws)
    shape = ws[0].shape
    rows, cols, mid = shape[0], shape[-1], shape[1:-1]
    c1 = 1.0 / (1.0 - ADAM_B1 ** ADAM_STEP)
    c2 = 1.0 / (1.0 - ADAM_B2 ** ADAM_STEP)

    def body(*refs):
        for i in range(n):
            w, g, m, v = (refs[k * n + i][...] for k in range(4))
            m2 = ADAM_B1 * m + (1.0 - ADAM_B1) * g
            v2 = ADAM_B2 * v + (1.0 - ADAM_B2) * (g * g)
            refs[4 * n + 4 * i][...] = -ADAM_LR * ((m2 * c1) / (jnp.sqrt(v2 * c2) + ADAM_EPS) + ADAM_WD * w)
            refs[4 * n + 4 * i + 1][...] = m2
            refs[4 * n + 4 * i + 2][...] = v2
            refs[4 * n + 4 * i + 3][...] = g

    spec = pl.BlockSpec((tr,) + mid + (tc,), lambda i, j: (i,) + (0,) * len(mid) + (j,))
    outs = pl.pallas_call(
        body, name=name, grid=(rows // tr, cols // tc), in_specs=[spec] * (4 * n), out_specs=[spec] * (4 * n),
        out_shape=[jax.ShapeDtypeStruct(shape, F32)] * (4 * n),
        compiler_params=_cparams("parallel", "parallel"),
    )(*ws, *gs, *ms, *vs)
    return [tuple(outs[4 * i:4 * i + 4]) for i in range(n)]


def _place():
    x, y, c = lax.axis_index("x"), lax.axis_index("y"), lax.axis_index("c")
    chips = [(1 - x, y), (x, 1 - y), (1 - x, 1 - y)]
    return x, y, c, chips


def _any_specs(n):
    return [pl.BlockSpec(memory_space=pl.ANY)] * n


def _rcopy(src, dst, send_sem, recv_sem, dev):
    return pltpu.make_async_remote_copy(src_ref=src, dst_ref=dst, send_sem=send_sem, recv_sem=recv_sem,
                                        device_id=dev, device_id_type=MESH)


QUARTER = HALF // 2

TO_X, TO_Y, RELAY_X, RELAY_Y, FWD_X, FWD_Y, FWD_D0, FWD_D1 = range(8)
TO_D = RELAY_X


def _gather_rider(bufs, views, relay):
    n = len(bufs)

    def plan(rout, sems):
        send, recv = sems
        x, y, c, _ = _place()
        me, sx, sy, sd = 2 * x + y, 2 * (1 - x) + y, 2 * x + (1 - y), 2 * (1 - x) + (1 - y)
        nx, ny, nd, sib = (1 - x, y, c), (x, 1 - y, c), (1 - x, 1 - y, c), (x, y, 1 - c)
        mine, other = c * HALF, (1 - c) * HALF
        out = {TO_X: (me, mine, HALF, nx), TO_Y: (me, mine, HALF, ny),
               FWD_X: (sx, mine, HALF, sib), FWD_Y: (sy, mine, HALF, sib)}
        inn = {TO_X: (sx, mine, HALF), TO_Y: (sy, mine, HALF),
               FWD_X: (sx, other, HALF), FWD_Y: (sy, other, HALF)}
        if relay:
            out.update({RELAY_X: (sy, mine, QUARTER, nx), RELAY_Y: (sx, mine + QUARTER, QUARTER, ny),
                        FWD_D0: (sd, mine, QUARTER, sib), FWD_D1: (sd, mine + QUARTER, QUARTER, sib)})
            inn.update({RELAY_X: (sd, mine, QUARTER), RELAY_Y: (sd, mine + QUARTER, QUARTER),
                        FWD_D0: (sd, other, QUARTER), FWD_D1: (sd, other + QUARTER, QUARTER)})
        else:
            out.update({TO_D: (me, mine, HALF, nd), FWD_D0: (sd, mine, HALF, sib)})
            inn.update({TO_D: (sd, mine, HALF), FWD_D0: (sd, other, HALF)})

        def copy(kind, b):
            slot, col, ncols, dev = out[kind]
            win = views[b](rout[b], slot, col, ncols)
            return _rcopy(win, win, send.at[kind * n + b], recv.at[kind * n + b], dev)

        def land(kind, b):
            slot, col, ncols = inn[kind]
            win = views[b](rout[b], slot, col, ncols)
            return _rcopy(win, win, send.at[kind * n + b], recv.at[kind * n + b], (x, y, c))

        return copy, land

    if relay:
        first = (TO_X, TO_Y)
        chain = ((TO_X, (FWD_X, RELAY_Y)), (TO_Y, (FWD_Y, RELAY_X)), (RELAY_X, (FWD_D0,)), (RELAY_Y, (FWD_D1,)))
    else:
        first = (TO_X, TO_Y, TO_D)
        chain = ((TO_X, (FWD_X,)), (TO_Y, (FWD_Y,)), (TO_D, (FWD_D0,)))
    forwards = [k for _, then in chain for k in then if k in (FWD_X, FWD_Y, FWD_D0, FWD_D1)]
    sent = list(first) + [k for _, then in chain for k in then]

    def start(rin, rout, sems):
        copy, _ = plan(rout, sems)
        for kind in first:
            for b in range(n):
                copy(kind, b).start()

    def finish(rin, rout, sems):
        copy, land = plan(rout, sems)
        for landed, then in chain:
            for b in range(n):
                land(landed, b).wait_recv()
                for kind in then:
                    copy(kind, b).start()
        for kind in forwards:
            for b in range(n):
                land(kind, b).wait_recv()
        for kind in sent:
            for b in range(n):
                copy(kind, b).wait_send()

    return _Rider(list(bufs), [jax.ShapeDtypeStruct(a.shape, a.dtype) for a in bufs], {b: b for b in range(n)},
                  [pltpu.SemaphoreType.DMA((8 * n,))] * 2, start, finish)


def _small_gather_rider(cw):
    def descs(rin, rout, sems, x, y, c, chips):
        return [_rcopy(rin[0], rout[0].at[2 * x + y], sems[1].at[j], sems[2].at[j], (chip[0], chip[1], c))
                for j, chip in enumerate(chips)]

    def start(rin, rout, sems):
        x, y, c, chips = _place()
        pltpu.make_async_copy(rin[0], rout[0].at[2 * x + y], sems[0].at[0]).start()
        for cp in descs(rin, rout, sems, x, y, c, chips):
            cp.start()

    def finish(rin, rout, sems):
        x, y, c, chips = _place()
        for j, chip in enumerate(chips):
            _rcopy(rin[0], rout[0].at[2 * chip[0] + chip[1]], sems[1].at[j], sems[2].at[j], (x, y, c)).wait_recv()
        for cp in descs(rin, rout, sems, x, y, c, chips):
            cp.wait_send()
        pltpu.make_async_copy(rin[0], rout[0].at[2 * x + y], sems[0].at[0]).wait()

    return _Rider([cw], [jax.ShapeDtypeStruct((NSH,) + cw.shape, cw.dtype)], {},
                  [pltpu.SemaphoreType.DMA((1,)), pltpu.SemaphoreType.DMA((3,)), pltpu.SemaphoreType.DMA((3,))],
                  start, finish)


def _to_sibling_rider(ps):
    n = len(ps)

    def descs(rin, rout, sems):
        x, y, c, _ = _place()
        return [_rcopy(rin[i].at[:, :, pl.ds((1 - c) * HALF, HALF)], rout[i], sems[0].at[i], sems[1].at[i],
                       (x, y, 1 - c)) for i in range(n)]

    def start(rin, rout, sems):
        for cp in descs(rin, rout, sems):
            cp.start()

    def finish(rin, rout, sems):
        for cp in descs(rin, rout, sems):
            cp.wait()

    return _Rider(list(ps), [jax.ShapeDtypeStruct(a.shape[:2] + (HALF,), a.dtype) for a in ps], {},
                  [pltpu.SemaphoreType.DMA((n,))] * 2, start, finish)


def _to_chips_rider(cs):
    n = len(cs)

    def descs(rin, rout, sems):
        x, y, c, chips = _place()
        return [_rcopy(rin[i].at[2 * chip[0] + chip[1]], rout[i].at[j], sems[0].at[j * n + i], sems[1].at[j * n + i],
                       (chip[0], chip[1], c)) for j, chip in enumerate(chips) for i in range(n)]

    def start(rin, rout, sems):
        for cp in descs(rin, rout, sems):
            cp.start()

    def finish(rin, rout, sems):
        for cp in descs(rin, rout, sems):
            cp.wait()

    return _Rider(list(cs), [jax.ShapeDtypeStruct((3,) + a.shape[1:], a.dtype) for a in cs], {},
                  [pltpu.SemaphoreType.DMA((3 * n,))] * 2, start, finish)


def _join_riders(riders):
    counts = [[len(r.operands) for r in riders], [len(r.out_shapes) for r in riders], [len(r.sems) for r in riders]]

    def each(step):
        def run(*refs):
            at = [0, 0, 0]
            for i, r in enumerate(riders):
                parts = [group[at[k]:at[k] + counts[k][i]] for k, group in enumerate(refs)]
                at = [at[k] + counts[k][i] for k in range(3)]
                step(r)(*parts)
        return run

    aliases = {sum(counts[0][:i]) + k: sum(counts[1][:i]) + v
               for i, r in enumerate(riders) for k, v in r.aliases.items()}
    return _Rider([a for r in riders for a in r.operands], [s for r in riders for s in r.out_shapes], aliases,
                  [s for r in riders for s in r.sems], each(lambda r: r.start), each(lambda r: r.finish))


SMALL_ROWS = 16


def _swap_halves(gs, vec):
    n = len(gs)

    def body(*refs):
        v_ref, dst, o_ref = refs[n], refs[n + 1:2 * n + 1], refs[2 * n + 1]
        buf, send, recv, vsend, vrecv = refs[2 * n + 2:]
        x, y, c, _ = _place()
        cps = []
        for i in range(n):
            mine = dst[i].at[:, pl.ds(c * HALF, HALF)]
            cps.append(_rcopy(mine, mine, send.at[i], recv.at[i], (x, y, 1 - c)))
        for cp in cps:
            cp.start()

        me = 4 * x + 2 * y + c
        buf[me] = v_ref[...]
        vcps = []
        for k in range(1, 8):
            peer = (x ^ (k >> 2), y ^ ((k >> 1) & 1), c ^ (k & 1))
            vcps.append(_rcopy(v_ref, buf.at[me], vsend.at[k - 1], vrecv.at[k - 1], peer))
        for cp in vcps:
            cp.start()
        for k in range(1, 8):
            _rcopy(v_ref, buf.at[me ^ k], vsend.at[k - 1], vrecv.at[k - 1], (x, y, c)).wait_recv()
        for cp in vcps:
            cp.wait_send()
        t = buf[0]
        for d in range(1, 8):
            t = t + buf[d]
        o_ref[...] = t

        for i in range(n):
            other = dst[i].at[:, pl.ds((1 - c) * HALF, HALF)]
            _rcopy(other, other, send.at[i], recv.at[i], (x, y, c)).wait_recv()
        for cp in cps:
            cp.wait_send()

    vmem = pl.BlockSpec(memory_space=pltpu.VMEM)
    res = pl.pallas_call(
        body, name="grads_swap_halves", in_specs=_any_specs(n) + [vmem], out_specs=_any_specs(n) + [vmem],
        out_shape=[jax.ShapeDtypeStruct(g.shape, g.dtype) for g in gs] + [jax.ShapeDtypeStruct((SMALL_ROWS, D), F32)],
        input_output_aliases={i: i for i in range(n)},
        scratch_shapes=[pltpu.VMEM((8, SMALL_ROWS, D), F32)] + [pltpu.SemaphoreType.DMA((n,))] * 2
        + [pltpu.SemaphoreType.DMA((7,))] * 2,
    )(*gs, vec)
    return list(res[:n]), res[n]


def _col_window(ref, slot, col, ncols):
    return ref.at[slot, :, pl.ds(col, ncols)]


def _stack_window(first, count):
    def view(ref, slot, col, ncols):
        return ref.at[slot, pl.ds(first, count), :, pl.ds(col, ncols)]
    return view


def _row_tile(rows):
    for t in range(512, 15, -16):
        if rows % t == 0:
            return t
    return rows


def _same_shape_runs(arrs):
    runs, a = [], 0
    for b in range(1, len(arrs) + 1):
        if b == len(arrs) or arrs[b].shape != arrs[a].shape:
            runs.append((a, b))
            a = b
    return runs


class _Comm:
    def __init__(self):
        x, y, c = lax.axis_index("x"), lax.axis_index("y"), lax.axis_index("c")
        self.c_idx = jnp.reshape(c, (1,)).astype(jnp.int32)
        self.shard = jnp.reshape(2 * x + y, (1,)).astype(jnp.int32)
        self.place = jnp.stack([2 * x + y, c]).astype(jnp.int32)
        self.groups = {}

    @staticmethod
    def gather(*bufs, relay, part=None):
        views = [_col_window if b.ndim == 3 else _stack_window(*(part or (0, b.shape[1]))) for b in bufs]
        return _gather_rider(list(bufs), views, relay)

    def reduce_rider(self, tag, names, ps, theirs=None):
        csums = []
        for a, b in _same_shape_runs(ps):
            name, th = "pair_sum_%s%d" % (tag, a), _row_tile(ps[a].shape[1])
            csums += (_pair_sum(name, self.c_idx, ps[a:b], th) if theirs is None else
                      _pair_add(name, self.c_idx, ps[a:b], theirs[a:b], th))
        self.groups[tag] = [names, csums, None]
        return _to_chips_rider(csums)

    def landed(self, tag, ts):
        self.groups[tag][2] = ts

    def finish(self, small):
        names, csums, ts = [], [], []
        for group_names, group_csums, group_ts in self.groups.values():
            names += group_names
            csums += group_csums
            ts += group_ts
        order = sorted(range(len(names)), key=lambda i: csums[i].shape[1])
        names, csums, ts = ([v[i] for i in order] for v in (names, csums, ts))
        halves = []
        for a, b in _same_shape_runs(csums):
            halves += _chip_sum("chip_sum_%d" % a, self.place, csums[a:b], ts[a:b], _row_tile(csums[a].shape[1]))
        grads, total = _swap_halves(halves, small)
        return dict(zip(names, grads)), total


ROPE_THETA = 10000.0
SMALL_1K = ("ffn1_pre_norm", "ffn1_post_norm", "mix_pre_norm", "ssm_norm", "mix_post_norm",
            "ffn2_pre_norm", "ffn2_post_norm")
SMALL_16 = ("dt_bias", "a_log", "d_skip")
OFF_CONVB = 7 * D
OFF_16 = OFF_CONVB + CONV_C
OFF_CONVW = OFF_16 + 48
OFF_LOSS = OFF_CONVW + CONV_K * CONV_C
SMALL_LEN = SMALL_ROWS * D


def _sds(shape, dtype):
    return jax.ShapeDtypeStruct(shape, dtype)


def _ridden(res, rider):
    return res if rider is not None else (res, None)


def _ffn_down(name, act, w, tail_of, rider=None):
    tail, o_specs, o_shapes = tail_of(TS)
    return _mm(name, [act, w.dn], NN, (S // TS,),
               [pl.BlockSpec((NSH, TS, FS), lambda i: (0, i, 0)),
                pl.BlockSpec((NSH, None, FS, D), lambda i: (0, w.d0, 0, 0))], o_specs, o_shapes, rider, tail)


def _ffn_dw(name, a, b, rider=None):
    return _mm(name, [a, b], TN, (NSH,),
               [pl.BlockSpec((None, S, FS), lambda s: (s, 0, 0)), pl.BlockSpec((S, D), lambda s: (0, 0))],
               pl.BlockSpec((None, FS, D), lambda s: (s, 0, 0)), _sds((NSH, FS, D), BF16), rider)


def _ffn_dn(name, dgate, dup, w, tail_of, rider=None):
    rows = TS // 2
    tail, o_specs, o_shapes = tail_of(rows)
    a2 = pl.BlockSpec((NSH, rows, FS), lambda i: (0, i, 0))
    return _mm(name, [dgate, w.gu, dup, w.gu], NN, (S // rows,),
               [a2, pl.BlockSpec((NSH, None, FS, D), lambda i: (0, w.g0, 0, 0)),
                a2, pl.BlockSpec((NSH, None, FS, D), lambda i: (0, w.g0 + 1, 0, 0))], o_specs, o_shapes, rider, tail)


def _out_proj_dx(dh, wout):
    def body(dh_ref, w_ref, dyn_ref, do_ref):
        dm = _dot(dh_ref[...], w_ref[...], NT)
        dyn_ref[...] = dm[:, D:]
        for b in range(TS // 128):
            for j, blk in enumerate(_rows_to_blocks(dm[128 * b:128 * (b + 1), :D])):
                do_ref[j, b] = blk.astype(BF16)

    return pl.pallas_call(
        body, name="out_proj_dx", grid=(S // TS,),
        in_specs=[pl.BlockSpec((TS, D), lambda i: (i, 0)), pl.BlockSpec((2 * D, D), lambda i: (0, 0))],
        out_specs=[pl.BlockSpec((TS, D), lambda i: (i, 0)),
                   pl.BlockSpec((NKV, TS // 128, HD, QROWS), lambda i: (0, i, 0, 0))],
        out_shape=[_sds((S, D), F32), _sds((NKV, NCH, HD, QROWS), BF16)], compiler_params=_cparams("parallel"),
    )(dh, wout)


def _heads(t, n):
    return t.reshape(S, n, HD).transpose(1, 0, 2)


def _pad128(v):
    return jnp.pad(v, ((0, 0), (0, 128 - v.shape[1])))


def _local_step(x, positions, tgt, sp, gu1, d1, f2, wint, wout, convw, comm=None):
    inv_freq = ROPE_THETA ** (-jnp.arange(0, HD, 2, dtype=F32) / HD)
    ang = positions.astype(F32)[:, None] * inv_freq
    ang = jnp.concatenate([ang, ang, ang, ang], axis=-1)
    cos, sin = jnp.cos(ang), jnp.sin(ang)
    dtb, alog = _pad128(sp["dt_bias"]), _pad128(sp["a_log"])
    dskip_l = jnp.repeat(sp["d_skip"], HD, axis=1)
    convb = sp["conv_b"]

    if comm:
        rider = _join_riders([comm.gather(gu1, relay=True), _small_gather_rider(convw)])
        (n1, d1, f2, wint, wout), (gu1, convw) = _prenorm_casts(
            "prenorm1", x, sp["ffn1_pre_norm"], comm.shard, [(d1, 0), (f2, 0), (wint, 1), (wout, 0)], rider)
        wint, wout = wint.reshape(NSH, WIN_SH, D), wout.reshape(NSH, 2 * D // NSH, D)
        convw = convw.transpose(1, 0, 2).reshape(CONV_K, CONV_C)
    else:
        n1 = _prenorm("prenorm1", x, sp["ffn1_pre_norm"])
    rider = comm.gather(d1, relay=False) if comm else None
    (fg1, fu1, act1), got = _ridden(_ffn_up("ffn1_up", n1, _FfnW(gu1, 0, d1, 0), rider), rider)
    if comm:
        d1, = got
    w1 = _FfnW(gu1, 0, d1, 0)
    rider = comm.gather(wint, relay=True) if comm else None
    (h1, x1, n2), got = _ridden(_ffn_down(
        "ffn1_down", act1, w1,
        lambda rows: _tail_postres(rows, x, sp["ffn1_post_norm"], 0.5, sp["mix_pre_norm"]), rider), rider)
    if comm:
        wint, = got
    wint_pad = jnp.pad(wint.reshape(WIN_COLS, D), ((0, WIN_PAD - WIN_COLS), (0, 0)))

    pw = WIN_PAD // 3
    rider = comm.gather(f2, relay=False, part=(0, 1)) if comm else None
    proj, got = _ridden(_mm(
        "in_proj", [n2, wint_pad], NT, (S // TS, 3),
        [pl.BlockSpec((TS, D), lambda i, j: (i, 0)), pl.BlockSpec((pw, D), lambda i, j: (j, 0))],
        pl.BlockSpec((TS, pw), lambda i, j: (i, j)), _sds((S, WIN_PAD), F32), rider), rider)
    if comm:
        f2, = got
    qt = _rope_q(proj, cos, sin)
    k_rot, v_bf, kt, vt = _rope_kv(proj, cos, sin)
    kh, vh = _heads(k_rot, NKV), _heads(v_bf, NKV)
    bias = _bias_table()
    rider = comm.gather(f2, wout, relay=False, part=(1, 2)) if comm else None
    (ot, lse, mixed), got = _ridden(_attn_fwd(qt, kh, vt, bias, rider), rider)
    if comm:
        f2, wout = got
    w2 = _FfnW(f2, 0, f2, 2)
    wout = wout.reshape(2 * D, D)
    xbc, conv_y = _conv_fwd(proj, convw, convb)
    y, mixed, hprev = _ssd_fwd(xbc, proj, dtb, alog, dskip_l, sp["ssm_norm"], mixed)
    tail, o_specs, o_shapes = _tail_postres(TS, x1, sp["mix_post_norm"], 1.0, sp["ffn2_pre_norm"])
    h2, x2, n3 = _mm("out_proj", [mixed, wout], NN, (S // TS,),
                     [pl.BlockSpec((TS, 2 * D), lambda i: (i, 0)), pl.BlockSpec((2 * D, D), lambda i: (0, 0))],
                     o_specs, o_shapes, None, tail)

    fg2, fu2, act2 = _ffn_up("ffn2_up", n3, w2)
    dy, dh3, dp3, loss = _ffn_down(
        "ffn2_down", act2, w2, lambda rows: _tail_final(rows, x2, sp["ffn2_post_norm"], tgt, 0.5))

    dgate2, dup2 = _ffn_dact("ffn2_dact", dh3, w2, fg2, fu2)
    dws2 = [_ffn_dw("ffn2_dwg", dgate2, n3), _ffn_dw("ffn2_dwu", dup2, n3), _ffn_dw("ffn2_dwd", act2, dh3)]
    dx2, dh2, dg3, dp2 = _ffn_dn(
        "ffn2_dn", dgate2, dup2, w2,
        lambda rows: _tail_mid_bwd(rows, dy, x2, sp["ffn2_pre_norm"], h2, sp["mix_post_norm"], 1.0))

    dyn, dot_ = _out_proj_dx(dh2, wout)
    dwout = _mm("out_proj_dw", [mixed, dh2], TN, (2,),
                [pl.BlockSpec((S, D), lambda m: (0, m)), pl.BlockSpec((S, D), lambda m: (0, 0))],
                pl.BlockSpec((D, D), lambda m: (m, 0)), _sds((2 * D, D), BF16))
    dwout = dwout.reshape(NSH, 2 * D // NSH, D)

    def riding(tag, names, ps, call, theirs=None):
        rider = comm.reduce_rider(tag, names, ps, theirs) if comm else None
        res, got = _ridden(call(rider), rider)
        if comm:
            comm.landed(tag, got)
        return res

    rider = _to_sibling_rider(dws2 + [dwout]) if comm else None
    (dxbc, dproj, ddt, dssm, dsc), theirs = _ridden(
        _ssd_bwd(dyn, y, xbc, proj, hprev, dtb, alog, dskip_l, sp["ssm_norm"], rider), rider)
    dproj, dcw8, dcb = _conv_bwd(dxbc, conv_y, proj, convw, dproj)
    delta = _attn_delta(ot, dot_)
    dqt, dkh, dvh = riding("a", BIG[3:6] + ("w_out",), dws2 + [dwout], lambda rider: _attn_bwd(
        qt, kh, kt, vh, dot_, lse, delta, bias, rider), theirs)
    dproj = _rope_dq(dqt, cos, sin, dproj)
    dproj = _rope_dkv(dkh, dvh, cos, sin, dproj)
    dproj = lax.dynamic_update_slice(dproj, ddt, (0, COL_DT))
    dwint = _mm("in_proj_dw", [dproj, n2], TN, (3,),
                [pl.BlockSpec((S, pw), lambda j: (0, j)), pl.BlockSpec((S, D), lambda j: (0, 0))],
                pl.BlockSpec((pw, D), lambda j: (j, 0)), _sds((WIN_PAD, D), BF16))
    dwint = dwint[:WIN_COLS].reshape(NSH, WIN_SH, D)

    tail, o_specs, o_shapes = _tail_mid_bwd(TS, dx2, x1, sp["mix_pre_norm"], h1, sp["ffn1_post_norm"], 0.5)
    dx1, dh1, dg2, dp1 = riding("b", ("w_in",), [dwint], lambda rider: _mm(
        "in_proj_dx", [dproj, wint_pad], NN, (S // TS,),
        [pl.BlockSpec((TS, WIN_PAD), lambda i: (i, 0)), pl.BlockSpec((WIN_PAD, D), lambda i: (0, 0))],
        o_specs, o_shapes, rider, tail))

    dwd1 = _ffn_dw("ffn1_dwd", act1, dh1)
    dgate1, dup1 = riding("d", BIG[2:3], [dwd1], lambda rider: _ffn_dact("ffn1_dact", dh1, w1, fg1, fu1, rider))
    dwg1, dwu1 = _ffn_dw("ffn1_dwg", dgate1, n1), _ffn_dw("ffn1_dwu", dup1, n1)
    grad_x, dg1 = riding("g", BIG[0:2], [dwg1, dwu1], lambda rider: _ffn_dn(
        "ffn1_dn", dgate1, dup1, w1, lambda rows: _tail_first_bwd(rows, dx1, x, sp["ffn1_pre_norm"]), rider))
    dws1 = [dwg1, dwu1, dwd1]

    small = jnp.concatenate([
        dg1[0], dp1[0], dg2[0], dssm[0], dp2[0], dg3[0], dp3[0], dcb[0],
        dsc[0, :16], dsc[1, :16], dsc[2, :16], dcw8[:CONV_K].reshape(-1), loss[0, :1]])
    small = jnp.pad(small, (0, SMALL_LEN - small.shape[0])).reshape(SMALL_ROWS, D)
    if comm is None:
        return grad_x, dws1 + dws2 + [dwint, dwout], small
    return (grad_x,) + comm.finish(small)


WEIGHTS = ("ffn1_pre_norm", "ffn1_w_gate", "ffn1_w_up", "ffn1_w_down", "ffn1_post_norm", "mix_pre_norm", "w_in",
           "conv_w", "conv_b", "dt_bias", "a_log", "d_skip", "ssm_norm", "w_out", "mix_post_norm", "ffn2_pre_norm",
           "ffn2_w_gate", "ffn2_w_up", "ffn2_w_down", "ffn2_post_norm")
BIG = ("ffn1_w_gate", "ffn1_w_up", "ffn1_w_down", "ffn2_w_gate", "ffn2_w_up", "ffn2_w_down", "w_in", "w_out")
TRANSPOSED = ("ffn1_w_gate", "ffn1_w_up", "ffn2_w_gate", "ffn2_w_up", "w_in")
SMALL_ORDER = SMALL_1K + ("conv_b",) + SMALL_16
CONVW_SH = CONV_C // NSH


def _shard2d(t, name):
    return t[0].T if name in TRANSPOSED else t[0]


def _unshard2d(t, name):
    return (t.T if name in TRANSPOSED else t)[None]


def _rows3d(t):
    return t.transpose(2, 0, 1)


def _pack_small(d, prefix, shard_of_convw):
    flat = jnp.concatenate([d[prefix + n][0] for n in SMALL_ORDER] + [shard_of_convw.reshape(-1)])
    return jnp.pad(flat, (0, SMALL_LEN - flat.shape[0])).reshape(SMALL_ROWS, D)


def _unpack_small(block, like):
    flat = block.reshape(-1)
    out, off = {}, 0
    for n in SMALL_ORDER:
        size = like[n].shape[1]
        out[n] = flat[off:off + size].reshape(1, size)
        off += size
    out["conv_w"] = flat[off:off + CONV_K * CONVW_SH].reshape(1, CONV_K, CONVW_SH)
    return out


def kernel(x, positions, ffn1_pre_norm, ffn1_w_gate, ffn1_w_up, ffn1_w_down, ffn1_post_norm, mix_pre_norm, w_in, conv_w, conv_b, dt_bias, a_log, d_skip, ssm_norm, w_out, mix_post_norm, ffn2_pre_norm, ffn2_w_gate, ffn2_w_up, ffn2_w_down, ffn2_post_norm, loss_target, m_ffn1_pre_norm, m_ffn1_w_gate, m_ffn1_w_up, m_ffn1_w_down, m_ffn1_post_norm, m_mix_pre_norm, m_w_in, m_conv_w, m_conv_b, m_dt_bias, m_a_log, m_d_skip, m_ssm_norm, m_w_out, m_mix_post_norm, m_ffn2_pre_norm, m_ffn2_w_gate, m_ffn2_w_up, m_ffn2_w_down, m_ffn2_post_norm, v_ffn1_pre_norm, v_ffn1_w_gate, v_ffn1_w_up, v_ffn1_w_down, v_ffn1_post_norm, v_mix_pre_norm, v_w_in, v_conv_w, v_conv_b, v_dt_bias, v_a_log, v_d_skip, v_ssm_norm, v_w_out, v_mix_post_norm, v_ffn2_pre_norm, v_ffn2_w_gate, v_ffn2_w_up, v_ffn2_w_down, v_ffn2_post_norm):
    given = dict(locals())
    xi, yi = lax.axis_index("x"), lax.axis_index("y")

    comm = _Comm()
    big = {p + n: _shard2d(given[p + n], n) for n in BIG for p in ("", "m_", "v_")}
    gu1 = _cast_stack("cast_ffn1_gate_up", comm.shard, [big[n] for n in BIG[0:2]], 176, D)

    sp = {n: given[n] for n in SMALL_ORDER}
    grad_x, big_grads, small = _local_step(
        x[0], positions[0], loss_target[0], sp, gu1, [big[BIG[2]]], [big[n] for n in BIG[3:6]], [big["w_in"]],
        [big["w_out"]], conv_w[0], comm)

    tot = small.reshape(-1)
    loss = tot[OFF_LOSS]
    small_grads, off = {}, 0
    for n in SMALL_ORDER:
        size = given[n].shape[1]
        small_grads[n] = tot[off:off + size].reshape(1, size)
        off += size
    dconvw = tot[OFF_CONVW:OFF_CONVW + CONV_K * CONV_C].reshape(CONV_K, NSH, CONVW_SH)
    dconvw = lax.dynamic_index_in_dim(dconvw, 2 * xi + yi, axis=1, keepdims=False)
    small_grads["conv_w"] = dconvw.reshape(1, CONV_K, CONVW_SH)

    upd = {}
    for names, tr in ((BIG[0:3], 176), (BIG[3:6], 176), (BIG[7:8], 256)):
        res = _adamw("adamw_" + names[0], [big[n] for n in names], [big_grads[n] for n in names],
                     [big["m_" + n] for n in names], [big["v_" + n] for n in names], tr, D)
        for n, r in zip(names, res):
            upd[n] = tuple(_unshard2d(t, n) for t in r)
    g_win = big_grads["w_in"].reshape(WIN_SH, 1, D)
    res, = _adamw("adamw_w_in", [_rows3d(w_in)], [g_win], [_rows3d(m_w_in)], [_rows3d(v_w_in)], WIN_SH // 4, D)
    upd["w_in"] = tuple(t.transpose(1, 2, 0) for t in res)
    (dl, m2, v2, _), = _adamw(
        "adamw_small", [_pack_small(given, "", conv_w[0])], [_pack_small(small_grads, "", dconvw)],
        [_pack_small(given, "m_", m_conv_w[0])], [_pack_small(given, "v_", v_conv_w[0])], SMALL_ROWS, D)
    dl, m2, v2 = (_unpack_small(t, given) for t in (dl, m2, v2))
    for n in SMALL_ORDER + ("conv_w",):
        upd[n] = (dl[n], m2[n], v2[n], small_grads[n])

    return (loss, grad_x[None], *[upd[n][3] for n in WEIGHTS], *[upd[n][0] for n in WEIGHTS],
            *[upd[n][1] for n in WEIGHTS], *[upd[n][2] for n in WEIGHTS])
```

```python
import functools
import typing

import jax
import jax.numpy as jnp
from jax import lax
from jax.experimental import pallas as pl
from jax.experimental.pallas import tpu as pltpu

F32 = jnp.float32
BF16 = jnp.bfloat16

S = 2048
D = 1024
FF = 2816
NSH = 4
FS = FF // NSH
HALF = D // 2
HD = 64
NKV = 4
NQ_PER_KV = 4
KVW = NKV * HD
QCOLS = NQ_PER_KV * HD
CONV_C = 1536
CONV_K = 4
SSM_W = 1024
NST = 128
NCH = S // 128
WIN_COLS = 4112
WIN_SH = WIN_COLS // NSH
WIN_PAD = 4224
COL_DT = 4096
EPS = 1e-6
NEG = -1e30

ADAM_LR = 0.001
ADAM_B1 = 0.9
ADAM_B2 = 0.999
ADAM_EPS = 1e-08
ADAM_WD = 0.01
ADAM_STEP = 10

VMEM_LIMIT = 56 * 1024 * 1024
TS = 512
TR = 256

NN = (((1,), (0,)), ((), ()))
NT = (((1,), (1,)), ((), ()))
TN = (((0,), (0,)), ((), ()))
MESH = pl.DeviceIdType.MESH


def _cparams(*sem):
    return pltpu.CompilerParams(dimension_semantics=sem, vmem_limit_bytes=VMEM_LIMIT)


def _dot(a, b, dims):
    return lax.dot_general(a.astype(BF16), b.astype(BF16), dims, preferred_element_type=F32)


def _bf16_pieces(v):
    hi = v.astype(BF16)
    rest = v - hi.astype(F32)
    mid = rest.astype(BF16)
    return hi, mid, (rest - mid.astype(F32)).astype(BF16)


def _dot_exact(a, b, ones="a"):
    if ones == "a":
        sel = a.astype(BF16)
        parts = [lax.dot_general(sel, p, NN, preferred_element_type=F32) for p in _bf16_pieces(b)]
    else:
        sel = b.astype(BF16)
        parts = [lax.dot_general(p, sel, NN, preferred_element_type=F32) for p in _bf16_pieces(a)]
    return (parts[2] + parts[1]) + parts[0]


def _sigmoid(v):
    return 1.0 / (1.0 + jnp.exp(-v))


class _Rider(typing.NamedTuple):
    operands: list
    out_shapes: list
    aliases: dict
    sems: list
    start: typing.Callable
    finish: typing.Callable
    between: typing.Callable = None
    at: int = None


def _call(body, name, grid, in_specs, out_specs, out_shape, operands, scratch=(), sem=(), rider=None, prefetch=0):
    multi = isinstance(out_shape, (list, tuple))

    def launch(kernel, in_specs, out_specs, out_shape, scratch, aliases, sem, args):
        if prefetch:
            how = dict(grid_spec=pltpu.PrefetchScalarGridSpec(
                num_scalar_prefetch=prefetch, grid=grid, in_specs=in_specs, out_specs=out_specs,
                scratch_shapes=scratch))
        else:
            how = dict(grid=grid, in_specs=in_specs, out_specs=out_specs, scratch_shapes=scratch)
        return pl.pallas_call(kernel, name=name, out_shape=out_shape, input_output_aliases=aliases,
                              compiler_params=_cparams(*sem), **how)(*args)

    if rider is None:
        return launch(body, in_specs, out_specs, out_shape, list(scratch), {}, sem, operands)
    outs = list(out_shape) if multi else [out_shape]
    ospecs = list(out_specs) if multi else [out_specs]
    n_in, n_out, n_scr = len(operands) - prefetch, len(outs), len(scratch)
    ri, ro = len(rider.operands), len(rider.out_shapes)

    def wrapped(*refs):
        scalars, refs = refs[:prefetch], refs[prefetch:]
        o0 = n_in + ri
        s0 = o0 + n_out + ro
        rin, rout, rsem = refs[n_in:o0], refs[o0 + n_out:s0], refs[s0 + n_scr:]
        ids = [pl.program_id(a) for a in range(len(grid))]
        first = functools.reduce(jnp.logical_and, [i == 0 for i in ids])
        last = functools.reduce(jnp.logical_and, [i == g - 1 for i, g in zip(ids, grid)])

        @pl.when(first)
        def _():
            rider.start(rin, rout, rsem)

        if rider.between is not None:
            steps = functools.reduce(lambda a, b: a * b, grid)
            step = functools.reduce(lambda a, ig: a * ig[1] + ig[0], zip(ids, grid), 0)

            @pl.when(step == (2 * steps // 3 if rider.at is None else rider.at))
            def _():
                rider.between(rin, rout, rsem)

        body(*scalars, *refs[:n_in], *refs[o0:o0 + n_out], *refs[s0:s0 + n_scr])

        @pl.when(last)
        def _():
            rider.finish(rin, rout, rsem)

    hbm = pl.BlockSpec(memory_space=pl.ANY)
    res = launch(wrapped, list(in_specs) + [hbm] * ri, ospecs + [hbm] * ro, outs + list(rider.out_shapes),
                 list(scratch) + list(rider.sems),
                 {prefetch + n_in + k: n_out + v for k, v in rider.aliases.items()},
                 ("arbitrary",) * len(grid), (*operands, *rider.operands))
    main = list(res[:n_out])
    return (main if multi else main[0]), list(res[n_out:])


class _Tail(typing.NamedTuple):
    fn: typing.Callable
    operands: list
    in_specs: list


def _mm(name, operands, dims, grid, in_specs, o_spec, out_shape, rider=None, tail=None):
    npairs = len(operands) // 2
    extra = [] if tail is None else list(tail.operands)
    nin = 2 * npairs + len(extra)

    def body(*refs):
        t = None
        for i in range(npairs):
            a, b = refs[2 * i], refs[2 * i + 1]
            parts = [(a[s], b[s]) for s in range(a.shape[0])] if len(a.shape) == 3 else [(a[...], b[...])]
            for pa, pb in parts:
                d = _dot(pa, pb, dims)
                t = d if t is None else t + d
        if tail is None:
            refs[nin][...] = t.astype(refs[nin].dtype)
        else:
            tail.fn(t, refs[2 * npairs:nin], refs[nin:])

    sem = ("parallel" if tail is None else "arbitrary",) * len(grid)
    specs = list(in_specs) + ([] if tail is None else list(tail.in_specs))
    return _call(body, name, grid, specs, o_spec, out_shape, list(operands) + extra, (), sem, rider)


class _FfnW(typing.NamedTuple):
    gu: jax.Array
    g0: int
    dn: jax.Array
    d0: int


def _ffn_up(name, n, w, rider=None):
    def body(n_ref, wg_ref, wu_ref, fg_ref, fu_ref, a_ref):
        nb = n_ref[...]
        g = _dot(nb, wg_ref[...], NT)
        u = _dot(nb, wu_ref[...], NT)
        sg = _sigmoid(g)
        silu = g * sg
        fg_ref[...] = (u * (sg * (1.0 + g * (1.0 - sg)))).astype(BF16)
        fu_ref[...] = silu.astype(BF16)
        a_ref[...] = (silu * u).astype(BF16)

    out = jax.ShapeDtypeStruct((NSH, S, FS), BF16)
    ospec = pl.BlockSpec((None, TS, FS), lambda s, i: (s, i, 0))
    return _call(
        body, name, (NSH, S // TS),
        [pl.BlockSpec((TS, D), lambda s, i: (i, 0)),
         pl.BlockSpec((None, None, FS, D), lambda s, i: (s, w.g0, 0, 0)),
         pl.BlockSpec((None, None, FS, D), lambda s, i: (s, w.g0 + 1, 0, 0))],
        [ospec, ospec, ospec], [out, out, out], (n, w.gu, w.gu), sem=("parallel", "parallel"), rider=rider)


def _ffn_dact(name, dh, w, fgate, fup, rider=None):
    def body(dh_ref, wd_ref, fg_ref, fu_ref, dg_ref, du_ref):
        da = _dot(dh_ref[...], wd_ref[...], NT)
        dg_ref[...] = (da * fg_ref[...].astype(F32)).astype(BF16)
        du_ref[...] = (da * fu_ref[...].astype(F32)).astype(BF16)

    out = jax.ShapeDtypeStruct((NSH, S, FS), BF16)
    aspec = pl.BlockSpec((None, TS, FS), lambda s, i: (s, i, 0))
    return _call(
        body, name, (NSH, S // TS),
        [pl.BlockSpec((TS, D), lambda s, i: (i, 0)),
         pl.BlockSpec((None, None, FS, D), lambda s, i: (s, w.d0, 0, 0)), aspec, aspec],
        [aspec, aspec], [out, out], (dh, w.dn, fgate, fup), sem=("parallel", "parallel"), rider=rider)


def _rstd(v):
    return lax.rsqrt(jnp.mean(v * v, axis=-1, keepdims=True) + EPS)


def _row_spec():
    return pl.BlockSpec((TR, D), lambda i: (i, 0))


def _vec_spec():
    return pl.BlockSpec((1, D), lambda i: (0, 0))


def _acc_rows(ref, v):
    @pl.when(pl.program_id(0) == 0)
    def _():
        ref[...] = jnp.zeros_like(ref)
    ref[...] += jnp.sum(v, axis=0, keepdims=True)


def _prenorm(name, x, g):
    def body(x_ref, g_ref, n_ref):
        xv = x_ref[...]
        n_ref[...] = (xv * _rstd(xv) * g_ref[...]).astype(BF16)

    return pl.pallas_call(
        body, name=name, grid=(S // TR,), in_specs=[_row_spec(), _vec_spec()], out_specs=_row_spec(),
        out_shape=jax.ShapeDtypeStruct((S, D), BF16), compiler_params=_cparams("parallel"),
    )(x, g)


ENTRY_STEPS = 4


def _prenorm_casts(name, x, g, slot, groups, rider):
    def body(s_ref, x_ref, g_ref, *refs):
        ins, outs = refs[:len(refs) - len(groups) - 1], refs[len(refs) - len(groups) - 1:]
        xv = x_ref[...]
        outs[0][...] = (xv * _rstd(xv) * g_ref[...]).astype(BF16)
        at = 0
        for (arrs, _), out in zip(groups, outs[1:]):
            for k in range(len(arrs)):
                out[k] = ins[at + k][...].astype(BF16)
            at += len(arrs)

    rows = pl.BlockSpec((S // ENTRY_STEPS, D), lambda i, sr: (i, 0))
    in_specs, out_specs, out_shapes = [rows, pl.BlockSpec((1, D), lambda i, sr: (0, 0))], [rows], [_rows_bf16()]
    for arrs, axis in groups:
        r, c = arrs[0].shape
        if axis == 0:
            blk, at, at_out = (r // ENTRY_STEPS, c), (lambda i, sr: (i, 0)), (lambda i, sr: (sr[0], 0, i, 0))
        else:
            blk, at, at_out = (r, c // ENTRY_STEPS), (lambda i, sr: (0, i)), (lambda i, sr: (sr[0], 0, 0, i))
        in_specs += [pl.BlockSpec(blk, at)] * len(arrs)
        out_specs.append(pl.BlockSpec((None, len(arrs)) + blk, at_out))
        out_shapes.append(jax.ShapeDtypeStruct((NSH, len(arrs), r, c), BF16))
    return _call(body, name, (ENTRY_STEPS,), in_specs, out_specs, out_shapes,
                 [slot, x, g] + [a for arrs, _ in groups for a in arrs], rider=rider, prefetch=1)


def _rows_spec(rows):
    return pl.BlockSpec((rows, D), lambda i: (i, 0))


def _rows_f32():
    return jax.ShapeDtypeStruct((S, D), F32)


def _rows_bf16():
    return jax.ShapeDtypeStruct((S, D), BF16)


def _vec_f32():
    return jax.ShapeDtypeStruct((1, D), F32)


def _tail_postres(rows, x, p, alpha, gnext):
    def fn(h, ins, outs):
        x_ref, p_ref, g_ref = ins
        h_ref, xo_ref, n_ref = outs
        h_ref[...] = h
        xo = x_ref[...] + alpha * (h * _rstd(h) * p_ref[...])
        xo_ref[...] = xo
        n_ref[...] = (xo * _rstd(xo) * g_ref[...]).astype(BF16)

    rs = _rows_spec(rows)
    return (_Tail(fn, [x, p, gnext], [rs, _vec_spec(), _vec_spec()]), [rs, rs, rs],
            [_rows_f32(), _rows_f32(), _rows_bf16()])


def _tail_final(rows, x, p, tgt, alpha):
    def fn(h, ins, outs):
        x_ref, p_ref, t_ref = ins
        dy_ref, dh_ref, dp_ref, loss_ref = outs
        r = _rstd(h)
        hn = h * r
        pv = p_ref[...]
        e = x_ref[...] + alpha * (hn * pv) - t_ref[...]
        dy = e * (1.0 / D)
        dy_ref[...] = dy
        du = alpha * dy * pv
        dh_ref[...] = (r * (du - hn * jnp.mean(du * hn, axis=-1, keepdims=True))).astype(BF16)
        _acc_rows(dp_ref, alpha * dy * hn)
        part = 0.5 * jnp.sum(jnp.mean(e * e, axis=-1, keepdims=True), axis=0, keepdims=True)
        _acc_rows(loss_ref, jnp.broadcast_to(part, (1, 128)))

    rs = _rows_spec(rows)
    return (_Tail(fn, [x, p, tgt], [rs, _vec_spec(), rs]),
            [rs, rs, _vec_spec(), pl.BlockSpec((1, 128), lambda i: (0, 0))],
            [_rows_f32(), _rows_bf16(), _vec_f32(), jax.ShapeDtypeStruct((1, 128), F32)])


def _norm_bwd(dn, xv, g_ref, dg_ref):
    r = _rstd(xv)
    xn = xv * r
    dng = dn * g_ref[...]
    _acc_rows(dg_ref, dn * xn)
    return r * (dng - xn * jnp.mean(dng * xn, axis=-1, keepdims=True))


def _tail_mid_bwd(rows, dres, x, g, h, p, alpha):
    def fn(dn, ins, outs):
        dr_ref, x_ref, g_ref, h_ref, p_ref = ins
        dx_ref, dh_ref, dg_ref, dp_ref = outs
        dx = dr_ref[...] + _norm_bwd(dn, x_ref[...], g_ref, dg_ref)
        dx_ref[...] = dx
        hv = h_ref[...]
        r = _rstd(hv)
        hn = hv * r
        du = alpha * dx * p_ref[...]
        dh_ref[...] = (r * (du - hn * jnp.mean(du * hn, axis=-1, keepdims=True))).astype(BF16)
        _acc_rows(dp_ref, alpha * dx * hn)

    rs = _rows_spec(rows)
    return (_Tail(fn, [dres, x, g, h, p], [rs, rs, _vec_spec(), rs, _vec_spec()]),
            [rs, rs, _vec_spec(), _vec_spec()], [_rows_f32(), _rows_bf16(), _vec_f32(), _vec_f32()])


def _tail_first_bwd(rows, dres, x, g):
    def fn(dn, ins, outs):
        dr_ref, x_ref, g_ref = ins
        dx_ref, dg_ref = outs
        dx_ref[...] = dr_ref[...] + _norm_bwd(dn, x_ref[...], g_ref, dg_ref)

    rs = _rows_spec(rows)
    return (_Tail(fn, [dres, x, g], [rs, rs, _vec_spec()]), [rs, _vec_spec()], [_rows_f32(), _vec_f32()])


def _rotate(t, c128, s128, sign, scale):
    width = t.shape[1]
    c = jnp.tile(c128, (1, width // 128))
    sn = jnp.tile(s128, (1, width // 128))
    lane = lax.broadcasted_iota(jnp.int32, t.shape, 1) & (HD - 1)
    rot = jnp.where(lane < HD // 2, -pltpu.roll(t, width - HD // 2, 1), pltpu.roll(t, HD // 2, 1))
    return (t * c + sign * (rot * sn)) * scale


def _rows_to_blocks(y):
    out = []
    for j in range(NKV):
        yt = y[:, QCOLS * j:QCOLS * (j + 1)].T
        out.append(jnp.concatenate([yt[HD * g:HD * (g + 1)] for g in range(NQ_PER_KV)], axis=1))
    return out


def _blocks_to_rows(blocks):
    cols = []
    for b in blocks:
        stacked = jnp.concatenate([b[:, 128 * g:128 * (g + 1)] for g in range(NQ_PER_KV)], axis=0)
        cols.append(stacked.T)
    return jnp.concatenate(cols, axis=1)


def _rope_q(proj, cos, sin):
    def body(t_ref, c_ref, s_ref, o_ref):
        y = _rotate(t_ref[...], c_ref[...], s_ref[...], 1.0, HD ** -0.5)
        for j, blk in enumerate(_rows_to_blocks(y)):
            o_ref[j] = blk.astype(BF16)

    return pl.pallas_call(
        body, name="rope_q", grid=(NCH,),
        in_specs=[pl.BlockSpec((128, D), lambda i: (i, 0)),
                  pl.BlockSpec((128, 128), lambda i: (i, 0)), pl.BlockSpec((128, 128), lambda i: (i, 0))],
        out_specs=pl.BlockSpec((NKV, None, HD, QROWS), lambda i: (0, i, 0, 0)),
        out_shape=jax.ShapeDtypeStruct((NKV, NCH, HD, QROWS), BF16), compiler_params=_cparams("parallel"),
    )(proj, cos, sin)


def _rope_dq(dqt, cos, sin, dproj):
    def body(t_ref, c_ref, s_ref, buf_ref, o_ref):
        t = _blocks_to_rows([t_ref[j] for j in range(NKV)])
        o_ref[...] = _rotate(t, c_ref[...], s_ref[...], -1.0, HD ** -0.5).astype(BF16)

    return pl.pallas_call(
        body, name="rope_dq", grid=(NCH,),
        in_specs=[pl.BlockSpec((NKV, None, HD, QROWS), lambda i: (0, i, 0, 0)),
                  pl.BlockSpec((128, 128), lambda i: (i, 0)), pl.BlockSpec((128, 128), lambda i: (i, 0)),
                  pl.BlockSpec(memory_space=pl.ANY)],
        out_specs=pl.BlockSpec((128, D), lambda i: (i, 0)),
        out_shape=jax.ShapeDtypeStruct(dproj.shape, BF16), input_output_aliases={3: 0},
        compiler_params=_cparams("parallel"),
    )(dqt, cos, sin, dproj)


def _rope_dkv(dkt, dvt, cos, sin, dproj):
    def body(k_ref, v_ref, c_ref, s_ref, buf_ref, o_ref):
        dk = jnp.concatenate([k_ref[j] for j in range(NKV)], axis=0).T
        dv = jnp.concatenate([v_ref[j] for j in range(NKV)], axis=0).T
        dk = _rotate(dk, c_ref[...], s_ref[...], -1.0, 1.0)
        o_ref[...] = jnp.concatenate([dk, dv], axis=1).astype(BF16)

    tspec = pl.BlockSpec((NKV, HD, 128), lambda i: (0, 0, i))
    return pl.pallas_call(
        body, name="rope_dkv", grid=(NCH,),
        in_specs=[tspec, tspec, pl.BlockSpec((128, 128), lambda i: (i, 0)), pl.BlockSpec((128, 128), lambda i: (i, 0)),
                  pl.BlockSpec(memory_space=pl.ANY)],
        out_specs=pl.BlockSpec((128, 2 * KVW), lambda i: (i, D // (2 * KVW))),
        out_shape=jax.ShapeDtypeStruct(dproj.shape, BF16), input_output_aliases={4: 0},
        compiler_params=_cparams("parallel"),
    )(dkt, dvt, cos, sin, dproj)


def _rope_kv(proj, cos, sin):
    def body(t_ref, c_ref, s_ref, k_ref, v_ref, kt_ref, vt_ref):
        t = t_ref[...]
        k = _rotate(t[:, :KVW], c_ref[...], s_ref[...], 1.0, 1.0).astype(BF16)
        v = t[:, KVW:].astype(BF16)
        k_ref[...] = k
        v_ref[...] = v
        kt, vt = k.astype(F32).T, v.astype(F32).T
        for j in range(NKV):
            kt_ref[j] = kt[HD * j:HD * (j + 1)].astype(BF16)
            vt_ref[j] = vt[HD * j:HD * (j + 1)].astype(BF16)

    rows = pl.BlockSpec((128, KVW), lambda i: (i, 0))
    tspec = pl.BlockSpec((NKV, HD, 128), lambda i: (0, 0, i))
    return pl.pallas_call(
        body, name="rope_kv", grid=(NCH,),
        in_specs=[pl.BlockSpec((128, 2 * KVW), lambda i: (i, D // (2 * KVW))),
                  pl.BlockSpec((128, 128), lambda i: (i, 0)), pl.BlockSpec((128, 128), lambda i: (i, 0))],
        out_specs=[rows, rows, tspec, tspec],
        out_shape=[jax.ShapeDtypeStruct((S, KVW), BF16)] * 2 + [jax.ShapeDtypeStruct((NKV, HD, S), BF16)] * 2,
        compiler_params=_cparams("parallel"),
    )(proj, cos, sin)


QROWS = NQ_PER_KV * 128


NBIAS = NCH + 1
KV_PER_STEP = 4


def _bias_table():
    db = lax.broadcasted_iota(jnp.int32, (NBIAS, 128, QROWS), 0) - 1
    ki = lax.broadcasted_iota(jnp.int32, (NBIAS, 128, QROWS), 1)
    qi = lax.broadcasted_iota(jnp.int32, (NBIAS, 128, QROWS), 2) & 127
    d = db * 128 + qi - ki
    cnt = ((d <= 128).astype(F32) + (((d & 3) == 0) & (d <= 512)).astype(F32) + ((d & 15) == 0).astype(F32))
    return jnp.where((d >= 0) & (cnt > 0.0), jnp.log(jnp.maximum(cnt, 1.0)), NEG)


def _qt_spec():
    return pl.BlockSpec((None, None, HD, QROWS), lambda j, i: (j, i, 0, 0))


def _stat_spec():
    return pl.BlockSpec((None, None, 1, QROWS), lambda j, i: (j, i, 0, 0))


def _attn_fwd(qt, kh, vt, bias, rider=None):
    def body(q_ref, k_ref, v_ref, b_ref, o_ref, lse_ref, rows_ref, m_ref, l_ref, acc_ref):
        qb = pl.program_id(1)
        m_ref[...] = jnp.full_like(m_ref, NEG)
        l_ref[...] = jnp.zeros_like(l_ref)
        acc_ref[...] = jnp.zeros_like(acc_ref)

        def keys(off, size, bias_):
            for h in range(KV_PER_STEP):
                m = m_ref[h]
                s = _dot(k_ref[h, pl.ds(off, size), :], q_ref[h], NN) + bias_
                m_new = jnp.maximum(m, jnp.max(s, axis=0, keepdims=True))
                p = jnp.exp(s - m_new)
                a = jnp.exp(m - m_new)
                m_ref[h] = m_new
                l_ref[h] = a * l_ref[h] + jnp.sum(p, axis=0, keepdims=True)
                acc_ref[h] = a * acc_ref[h] + _dot(v_ref[h, :, pl.ds(off, size)], p, NN)

        def blocks(first, count):
            bias_ = jnp.concatenate([b_ref[qb - first - j + 1] for j in range(count)], axis=0)
            keys(pl.multiple_of(first * 128, 128), 128 * count, bias_)

        nkb = qb + 1
        @pl.loop(0, nkb // 4)
        def _(i):
            blocks(4 * i, 4)

        @pl.when(nkb % 4 >= 2)
        def _():
            blocks(nkb // 4 * 4, 2)

        @pl.when(nkb % 2 == 1)
        def _():
            blocks(qb, 1)

        outs = []
        for h in range(KV_PER_STEP):
            outs.append(acc_ref[h] / l_ref[h])
            o_ref[h] = outs[h]
            lse_ref[h] = m_ref[h] + jnp.log(l_ref[h])
        rows_ref[...] = _blocks_to_rows(outs).astype(BF16)

    kvs = KV_PER_STEP
    qspec = pl.BlockSpec((kvs, None, HD, QROWS), lambda j, i: (j, i, 0, 0))
    return _call(
        body, "attn_fwd", (NKV // kvs, NCH),
        [qspec, pl.BlockSpec((kvs, S, HD), lambda j, i: (j, 0, 0)),
         pl.BlockSpec((kvs, HD, S), lambda j, i: (j, 0, 0)),
         pl.BlockSpec((NBIAS, 128, QROWS), lambda j, i: (0, 0, 0))],
        [qspec, pl.BlockSpec((kvs, None, 1, QROWS), lambda j, i: (j, i, 0, 0)),
         pl.BlockSpec((128, QCOLS * kvs), lambda j, i: (i, j))],
        [jax.ShapeDtypeStruct((NKV, NCH, HD, QROWS), F32), jax.ShapeDtypeStruct((NKV, NCH, 1, QROWS), F32),
         jax.ShapeDtypeStruct((S, 2 * D), BF16)],
        (qt, kh, vt, bias),
        [pltpu.VMEM((kvs, 1, QROWS), F32), pltpu.VMEM((kvs, 1, QROWS), F32), pltpu.VMEM((kvs, HD, QROWS), F32)],
        ("parallel", "parallel"), rider)


def _attn_delta(ot, dot_):
    def body(o_ref, do_ref, dl_ref):
        dl_ref[...] = jnp.sum(o_ref[...] * do_ref[...].astype(F32), axis=1, keepdims=True)

    spec = pl.BlockSpec((None, NCH, HD, QROWS), lambda j: (j, 0, 0, 0))
    return pl.pallas_call(
        body, name="attn_delta", grid=(NKV,), in_specs=[spec, spec],
        out_specs=pl.BlockSpec((None, NCH, 1, QROWS), lambda j: (j, 0, 0, 0)),
        out_shape=jax.ShapeDtypeStruct((NKV, NCH, 1, QROWS), F32), compiler_params=_cparams("parallel"),
    )(ot, dot_)


def _attn_bwd(qt, kh, kt, vh, dot_, lse, delta, bias, rider=None):
    def body(qt_ref, k_ref, kt_ref, v_ref, dot_ref, lse_ref, dl_ref, b_ref, dq_ref, dk_ref, dv_ref):
        kp = pl.program_id(1)

        @pl.when(kp == 0)
        def _():
            dq_ref[...] = jnp.zeros_like(dq_ref)

        dk_ref[...] = jnp.zeros_like(dk_ref)
        dv_ref[...] = jnp.zeros_like(dv_ref)

        @pl.loop(2 * kp, NCH // 2)
        def _(j):
            for h in range(KV_PER_STEP):
                k, kt_, v = k_ref[h], kt_ref[h], v_ref[h]
                for qb in (2 * j, 2 * j + 1):
                    bias2 = jnp.concatenate([b_ref[jnp.maximum(qb - 4 * kp - t + 1, 0)] for t in range(4)], axis=0)
                    st = _dot(k, qt_ref[h, qb], NN) + bias2
                    pt = jnp.exp(st - lse_ref[h, qb])
                    dst = pt * (_dot(v, dot_ref[h, qb], NN) - dl_ref[h, qb])
                    dq_ref[h, qb] += _dot(kt_, dst, NN)
                    dk_ref[h] += _dot(qt_ref[h, qb], dst, NT)
                    dv_ref[h] += _dot(dot_ref[h, qb], pt, NT)

    kvs = KV_PER_STEP
    tspec = pl.BlockSpec((kvs, NCH, HD, QROWS), lambda j, i: (j, 0, 0, 0))
    kspec = pl.BlockSpec((kvs, 512, HD), lambda j, i: (j, i, 0))
    ktspec = pl.BlockSpec((kvs, HD, 512), lambda j, i: (j, 0, i))
    sspec = pl.BlockSpec((kvs, NCH, 1, QROWS), lambda j, i: (j, 0, 0, 0))
    return _call(
        body, "attn_bwd", (NKV // kvs, NCH // 4),
        [tspec, kspec, ktspec, kspec, tspec, sspec, sspec,
         pl.BlockSpec((NBIAS, 128, QROWS), lambda j, i: (0, 0, 0))],
        [tspec, ktspec, ktspec],
        [jax.ShapeDtypeStruct((NKV, NCH, HD, QROWS), F32),
         jax.ShapeDtypeStruct((NKV, HD, S), F32), jax.ShapeDtypeStruct((NKV, HD, S), F32)],
        (qt, kh, kt, vh, dot_, lse, delta, bias), sem=("parallel", "arbitrary"), rider=rider)


CONV_BLK = 256
CONV_COL0 = 1536 // CONV_BLK


CONV_ROWS = 128


def _conv_fwd(proj, convw, convb):
    trips = S // CONV_ROWS

    def body(u_ref, w_ref, b_ref, o_ref, y_ref):
        @pl.loop(0, trips)
        def _(c):
            t0 = pl.multiple_of(c * CONV_ROWS, CONV_ROWS)
            before = pl.multiple_of(jnp.maximum(t0 - 8, 0), 8)
            ext = jnp.concatenate([jnp.where(c == 0, 0.0, u_ref[pl.ds(before, 8), :]),
                                   u_ref[pl.ds(t0, CONV_ROWS), :]], axis=0)
            y = b_ref[...] + w_ref[CONV_K - 1:CONV_K, :] * ext[8:]
            for j in range(1, CONV_K):
                y = y + w_ref[CONV_K - 1 - j:CONV_K - j, :] * pltpu.roll(ext, j, 0)[8:]
            y_ref[pl.ds(t0, CONV_ROWS), :] = y
            o_ref[pl.ds(t0, CONV_ROWS), :] = y * _sigmoid(y)

    out = pl.BlockSpec((S, CONV_BLK), lambda i: (0, i))
    return pl.pallas_call(
        body, name="conv_fwd", grid=(CONV_C // CONV_BLK,),
        in_specs=[pl.BlockSpec((S, CONV_BLK), lambda i: (0, CONV_COL0 + i)),
                  pl.BlockSpec((CONV_K, CONV_BLK), lambda i: (0, i)),
                  pl.BlockSpec((1, CONV_BLK), lambda i: (0, i))],
        out_specs=[out, out], out_shape=[jax.ShapeDtypeStruct((S, CONV_C), F32)] * 2,
        compiler_params=_cparams("parallel"),
    )(proj, convw, convb)


def _conv_bwd(dact, ypre, proj, convw, dproj):
    trips = S // CONV_ROWS

    def body(da_ref, y_ref, u_ref, w_ref, buf_ref, du_ref, dw_ref, db_ref):
        dw_ref[...] = jnp.zeros_like(dw_ref)
        db_ref[...] = jnp.zeros_like(db_ref)
        r8 = lax.broadcasted_iota(jnp.int32, (8, CONV_BLK), 0)

        def dy_of(rows):
            y = y_ref[rows, :]
            sg = _sigmoid(y)
            return da_ref[rows, :] * (sg * (1.0 + y * (1.0 - sg)))

        @pl.loop(0, trips)
        def _(c):
            t0 = pl.multiple_of(c * CONV_ROWS, CONV_ROWS)
            after = pl.multiple_of(jnp.minimum(t0 + CONV_ROWS, S - 8), 8)
            ext = jnp.concatenate([dy_of(pl.ds(t0, CONV_ROWS)),
                                   jnp.where(c == trips - 1, 0.0, dy_of(pl.ds(after, 8)))], axis=0)
            u = u_ref[pl.ds(t0, CONV_ROWS), :]
            du, dw = None, jnp.zeros((8, CONV_BLK), F32)
            for j in range(CONV_K):
                dyj = (ext if j == 0 else pltpu.roll(ext, CONV_ROWS + 8 - j, 0))[:CONV_ROWS]
                term = w_ref[CONV_K - 1 - j:CONV_K - j, :] * dyj
                du = term if du is None else du + term
                dw = dw + jnp.where(r8 == CONV_K - 1 - j, jnp.sum(dyj * u, axis=0, keepdims=True), 0.0)
            du_ref[pl.ds(t0, CONV_ROWS), :] = du.astype(BF16)
            dw_ref[...] += dw
            db_ref[...] += jnp.sum(ext[:CONV_ROWS], axis=0, keepdims=True)

    return pl.pallas_call(
        body, name="conv_bwd", grid=(CONV_C // CONV_BLK,),
        in_specs=[pl.BlockSpec((S, CONV_BLK), lambda i: (0, i)), pl.BlockSpec((S, CONV_BLK), lambda i: (0, i)),
                  pl.BlockSpec((S, CONV_BLK), lambda i: (0, CONV_COL0 + i)),
                  pl.BlockSpec((CONV_K, CONV_BLK), lambda i: (0, i)), pl.BlockSpec(memory_space=pl.ANY)],
        out_specs=[pl.BlockSpec((S, CONV_BLK), lambda i: (0, CONV_COL0 + i)),
                   pl.BlockSpec((8, CONV_BLK), lambda i: (0, i)), pl.BlockSpec((1, CONV_BLK), lambda i: (0, i))],
        out_shape=[jax.ShapeDtypeStruct(dproj.shape, BF16), jax.ShapeDtypeStruct((8, CONV_C), F32),
                   jax.ShapeDtypeStruct((1, CONV_C), F32)],
        input_output_aliases={4: 0}, compiler_params=_cparams("parallel"),
    )(dact, ypre, proj, convw, dproj)


NPAIR = 8


def _ssd_scalars(dtr_ref, dtb_ref, alog_ref):
    z = dtr_ref[...] + dtb_ref[...]
    dt = jnp.maximum(z, 0.0) + jnp.log(1.0 + jnp.exp(-jnp.abs(z)))
    a = -jnp.exp(alog_ref[...])
    r = lax.broadcasted_iota(jnp.int32, (128, 128), 0)
    c = lax.broadcasted_iota(jnp.int32, (128, 128), 1)
    tri = (r >= c).astype(F32)
    cs = _dot_exact(tri, dt * a)
    return z, dt, a, cs, r, c


def _by_lane(cs, dt):
    head = lax.broadcasted_iota(jnp.int32, (128, SSM_W), 0)
    lane = lax.broadcasted_iota(jnp.int32, (128, SSM_W), 1)
    sel = (head == lane // HD).astype(F32)
    cs_l = _dot_exact(cs, sel, "b")
    last_l = cs_l[127:128, :]
    return sel, jnp.exp(cs_l), jnp.exp(last_l - cs_l), _dot_exact(dt, sel, "b")


def _pair_terms(cs, h1, h2):
    return (cs[:, h1:h1 + 1], cs[:, h2:h2 + 1],
            jnp.exp(cs[127:128, h1:h1 + 1]), jnp.exp(cs[127:128, h2:h2 + 1]))


def _gate_norm(y, zv, w):
    yg = y * (zv * _sigmoid(zv))
    outs, rs = [], []
    for g in range(2):
        blk = yg[:, 512 * g:512 * (g + 1)]
        r = lax.rsqrt(jnp.mean(blk * blk, axis=-1, keepdims=True) + EPS)
        outs.append(blk * r)
        rs.append(r)
    return jnp.concatenate(outs, axis=1), rs, yg


def _ssd_fwd(xbc, proj, dtb, alog, dskip_l, ssmw, mixed):
    def body(x_ref, b_ref, c_ref, dtr_ref, z_ref, dtb_ref, alog_ref, dsk_ref, w_ref, buf_ref,
             y_ref, yn_ref, hp_ref, h_ref):
        @pl.when(pl.program_id(0) == 0)
        def _():
            h_ref[...] = jnp.zeros_like(h_ref)

        _, dt, _, cs, r, c = _ssd_scalars(dtr_ref, dtb_ref, alog_ref)
        cst = cs.T
        causal = r >= c
        lo = c < HD
        _, e_all, dte_all, dt_all = _by_lane(cs, dt)
        hp_ref[...] = h_ref[...]
        for g in range(2):
            bg = b_ref[:, 128 * g:128 * (g + 1)]
            cg = c_ref[:, 128 * g:128 * (g + 1)]
            cb = _dot(cg, bg, NT)
            for j in range(4):
                pj = 4 * g + j
                h1, h2 = 2 * pj, 2 * pj + 1
                sl = slice(128 * pj, 128 * (pj + 1))
                xp = x_ref[:, sl]
                c1, c2, cd1, cd2 = _pair_terms(cs, h1, h2)
                e_l, dte_l = e_all[:, sl], dte_all[:, sl]
                xdt = xp * dt_all[:, sl]
                m1 = cb * jnp.exp(jnp.where(causal, c1 - cst[h1:h1 + 1, :], NEG))
                m2 = cb * jnp.exp(jnp.where(causal, c2 - cst[h2:h2 + 1, :], NEG))
                yd = jnp.where(lo, _dot(m1, xdt, NN), _dot(m2, xdt, NN))
                hp = h_ref[pj]
                yo = _dot(cg, hp, NT) * e_l
                st = _dot(xdt * dte_l, bg, TN)
                h_ref[pj] = hp * jnp.where(r < HD, cd1, cd2) + st
                y_ref[:, sl] = yd + yo + dsk_ref[:, sl] * xp
        yn, _, _ = _gate_norm(y_ref[...], z_ref[...], w_ref[...])
        yn_ref[...] = (yn * w_ref[...]).astype(BF16)

    return pl.pallas_call(
        body, name="ssd_fwd", grid=(NCH,),
        in_specs=[pl.BlockSpec((128, SSM_W), lambda i: (i, 0)),
                  pl.BlockSpec((128, 256), lambda i: (i, 4)), pl.BlockSpec((128, 256), lambda i: (i, 5)),
                  pl.BlockSpec((128, 128), lambda i: (i, COL_DT // 128)),
                  pl.BlockSpec((128, SSM_W), lambda i: (i, 3)),
                  pl.BlockSpec((1, 128), lambda i: (0, 0)), pl.BlockSpec((1, 128), lambda i: (0, 0)),
                  pl.BlockSpec((1, SSM_W), lambda i: (0, 0)), pl.BlockSpec((1, SSM_W), lambda i: (0, 0)),
                  pl.BlockSpec(memory_space=pl.ANY)],
        out_specs=[pl.BlockSpec((128, SSM_W), lambda i: (i, 0)), pl.BlockSpec((128, SSM_W), lambda i: (i, 1)),
                   pl.BlockSpec((None, NPAIR, 128, 128), lambda i: (i, 0, 0, 0))],
        out_shape=[jax.ShapeDtypeStruct((S, SSM_W), F32), jax.ShapeDtypeStruct(mixed.shape, BF16),
                   jax.ShapeDtypeStruct((NCH, NPAIR, 128, 128), F32)],
        scratch_shapes=[pltpu.VMEM((NPAIR, 128, 128), F32)],
        input_output_aliases={9: 1}, compiler_params=_cparams("arbitrary"),
    )(xbc, xbc, xbc, proj, proj, dtb, alog, dskip_l, ssmw, mixed)


def _ssd_bwd(dmixed, y, xbc, proj, hprev, dtb, alog, dskip_l, ssmw, rider=None):
    def body(dyn_ref, y_ref, x_ref, b_ref, c_ref, dtr_ref, z_ref, hp_ref, dtb_ref, alog_ref, dsk_ref, w_ref,
             dxbc_ref, dz_ref, ddt_ref, dw_ref, dsc_ref, g_ref):
        @pl.when(pl.program_id(0) == 0)
        def _():
            g_ref[...] = jnp.zeros_like(g_ref)
            dsc_ref[...] = jnp.zeros_like(dsc_ref)

        z, dt, a, cs, r, c = _ssd_scalars(dtr_ref, dtb_ref, alog_ref)
        cst = cs.T
        causal = r >= c
        lo = c < HD

        yv = y_ref[...]
        zv = z_ref[...]
        wv = w_ref[...]
        ygn, rs, yg = _gate_norm(yv, zv, wv)
        dyn = dyn_ref[...]
        _acc_rows(dw_ref, dyn * ygn)
        dynw = dyn * wv
        parts = []
        for g in range(2):
            sl = slice(512 * g, 512 * (g + 1))
            a_g, n_g = dynw[:, sl], ygn[:, sl]
            parts.append(rs[g] * (a_g - n_g * jnp.mean(a_g * n_g, axis=-1, keepdims=True)))
        dyg = jnp.concatenate(parts, axis=1)
        sz = _sigmoid(zv)
        dz_ref[...] = (dyg * yv * (sz * (1.0 + zv * (1.0 - sz)))).astype(BF16)
        dy_all = dyg * (zv * sz)

        dcs_cols = jnp.zeros((128, 128), F32)
        dcs_rows = jnp.zeros((128, 128), F32)
        sel, e_all, dte_all, dt_all = _by_lane(cs, dt)
        x_all, b_all, c_all, dsk_all = x_ref[...], b_ref[...], c_ref[...], dsk_ref[...]
        hp_all, g_all = hp_ref[...], g_ref[...]
        g_new, dx_parts, db_parts, dc_parts = [], [], [], []
        dyx_parts, ryo_parts, qx_parts, dxx_parts, gh_parts = [], [], [], [], []
        for g in range(2):
            bg = b_all[:, 128 * g:128 * (g + 1)]
            cg = c_all[:, 128 * g:128 * (g + 1)]
            cb = _dot(cg, bg, NT)
            dcb = jnp.zeros((128, 128), F32)
            db_acc = jnp.zeros((128, NST), F32)
            dc_acc = jnp.zeros((128, NST), F32)
            for j in range(4):
                pj = 4 * g + j
                h1, h2 = 2 * pj, 2 * pj + 1
                sl = slice(128 * pj, 128 * (pj + 1))
                xp = x_all[:, sl]
                dyp = dy_all[:, sl]
                c1, c2, cd1, cd2 = _pair_terms(cs, h1, h2)
                e_l, dte_l, dt_l = e_all[:, sl], dte_all[:, sl], dt_all[:, sl]
                xdt = xp * dt_l
                hp = hp_all[pj]
                gp = g_all[pj]
                dyx_parts.append(dyp * xp)
                dzs = dyp * e_l
                dc_acc = dc_acc + _dot(dzs, hp, NN)
                ryo_parts.append(dyp * (_dot(cg, hp, NT) * e_l))
                qm = _dot(bg, gp, NT)
                dxdt = qm * dte_l
                qx_parts.append(qm * xdt)
                db_acc = db_acc + _dot(xdt * dte_l, gp, NN)
                gh_parts.append(gp * hp)
                g_new.append(_dot(dzs, cg, TN) + jnp.where(r < HD, cd1, cd2) * gp)
                for hh, ch, msk in ((h1, c1, lo), (h2, c2, jnp.logical_not(lo))):
                    lm = jnp.exp(jnp.where(causal, ch - cst[hh:hh + 1, :], NEG))
                    mm = cb * lm
                    dm = jnp.where(causal, _dot(jnp.where(msk, dyp, 0.0), xdt, NT), 0.0)
                    w = dm * mm
                    dcs_cols = dcs_cols + jnp.where(c == hh, jnp.sum(w, axis=1, keepdims=True), 0.0)
                    dcs_rows = dcs_rows + jnp.where(r == hh, jnp.sum(w, axis=0, keepdims=True), 0.0)
                    dcb = dcb + dm * lm
                    dxdt = dxdt + jnp.where(msk, _dot(mm, dyp, TN), 0.0)
                dxx_parts.append(dxdt * xp)
                dx_parts.append(dsk_all[:, sl] * dyp + dxdt * dt_l)
            db_parts.append(db_acc + _dot(dcb, cg, TN))
            dc_parts.append(dc_acc + _dot(dcb, bg, NN))
        g_ref[...] = jnp.stack(g_new)
        dxbc_ref[...] = jnp.concatenate(dx_parts + db_parts + dc_parts, axis=1)

        selt = (lax.broadcasted_iota(jnp.int32, (SSM_W, 128), 0) // HD
                == lax.broadcasted_iota(jnp.int32, (SSM_W, 128), 1)).astype(F32)

        def by_head(parts):
            return _dot_exact(jnp.concatenate(parts, axis=1), selt, "b")

        ddt_x = by_head(dxx_parts)
        dd_row = jnp.sum(by_head(dyx_parts), axis=0, keepdims=True)
        t_all = by_head(qx_parts) * jnp.exp(cs[127:128, :] - cs)
        gh = jnp.sum(_dot_exact(sel, jnp.concatenate(gh_parts, axis=0)), axis=1, keepdims=True)
        gh_row = jnp.broadcast_to(gh, (128, 128)).T[0:1, :]
        at_end = jnp.sum(t_all, axis=0, keepdims=True) + gh_row * jnp.exp(cs[127:128, :])
        dcs = by_head(ryo_parts) - t_all + dcs_cols + jnp.where(r == 127, at_end, 0.0) - dcs_rows.T
        dad = _dot_exact((c >= r).astype(F32), dcs)
        ddt = dad * a + ddt_x
        ddtr = jnp.where(c < 16, ddt * _sigmoid(z), 0.0)
        ddt_ref[...] = ddtr.astype(BF16)
        r8 = lax.broadcasted_iota(jnp.int32, (8, 128), 0)
        dsc_ref[...] += (jnp.where(r8 == 0, jnp.sum(ddtr, axis=0, keepdims=True), 0.0)
                         + jnp.where(r8 == 1, jnp.sum(dad * dt, axis=0, keepdims=True) * a, 0.0)
                         + jnp.where(r8 == 2, dd_row, 0.0))

    rev = NCH - 1
    return _call(
        body, "ssd_bwd", (NCH,),
        [pl.BlockSpec((128, SSM_W), lambda i: (rev - i, 0)),
         pl.BlockSpec((128, SSM_W), lambda i: (rev - i, 0)),
         pl.BlockSpec((128, SSM_W), lambda i: (rev - i, 0)),
         pl.BlockSpec((128, 256), lambda i: (rev - i, 4)), pl.BlockSpec((128, 256), lambda i: (rev - i, 5)),
         pl.BlockSpec((128, 128), lambda i: (rev - i, COL_DT // 128)),
         pl.BlockSpec((128, SSM_W), lambda i: (rev - i, 3)),
         pl.BlockSpec((None, NPAIR, 128, 128), lambda i: (rev - i, 0, 0, 0)),
         pl.BlockSpec((1, 128), lambda i: (0, 0)), pl.BlockSpec((1, 128), lambda i: (0, 0)),
         pl.BlockSpec((1, SSM_W), lambda i: (0, 0)), pl.BlockSpec((1, SSM_W), lambda i: (0, 0))],
        [pl.BlockSpec((128, CONV_C), lambda i: (rev - i, 0)),
         pl.BlockSpec((128, SSM_W), lambda i: (rev - i, 3)),
         pl.BlockSpec((128, 128), lambda i: (rev - i, 0)),
         pl.BlockSpec((1, SSM_W), lambda i: (0, 0)), pl.BlockSpec((8, 128), lambda i: (0, 0))],
        [jax.ShapeDtypeStruct((S, CONV_C), F32), jax.ShapeDtypeStruct((S, WIN_PAD), BF16),
         jax.ShapeDtypeStruct((S, 128), BF16), jax.ShapeDtypeStruct((1, SSM_W), F32),
         jax.ShapeDtypeStruct((8, 128), F32)],
        (dmixed, y, xbc, xbc, xbc, proj, proj, hprev, dtb, alog, dskip_l, ssmw),
        [pltpu.VMEM((NPAIR, 128, 128), F32)], ("arbitrary",), rider)


def _cast_stack(name, slot, arrs, tr, tc):
    n = len(arrs)
    rows, cols = arrs[0].shape

    def body(s_ref, *refs):
        for i in range(n):
            refs[n][i] = refs[i][...].astype(BF16)

    return pl.pallas_call(
        body, name=name,
        grid_spec=pltpu.PrefetchScalarGridSpec(
            num_scalar_prefetch=1, grid=(rows // tr, cols // tc),
            in_specs=[pl.BlockSpec((tr, tc), lambda i, j, sr: (i, j))] * n,
            out_specs=pl.BlockSpec((None, n, tr, tc), lambda i, j, sr: (sr[0], 0, i, j))),
        out_shape=jax.ShapeDtypeStruct((NSH, n, rows, cols), BF16),
        compiler_params=_cparams("parallel", "parallel"),
    )(slot, *arrs)


def _pair_sum(name, c_idx, ps, th):
    n = len(ps)
    _, rows, _ = ps[0].shape

    def body(c_ref, *refs):
        mine, whole, out, theirs = refs[:n], refs[n:2 * n], refs[2 * n:3 * n], refs[3 * n:4 * n]
        send, recv = refs[4 * n], refs[4 * n + 1]
        s, i = pl.program_id(0), pl.program_id(1)
        x, y, c, _ = _place()

        def copies(slot):
            return [_rcopy(whole[k].at[slot, :, pl.ds((1 - c) * HALF, HALF)], theirs[k].at[slot],
                           send.at[slot * n + k], recv.at[slot * n + k], (x, y, 1 - c)) for k in range(n)]

        @pl.when((s == 0) & (i == 0))
        def _():
            for slot in range(NSH):
                for cp in copies(slot):
                    cp.start()

        @pl.when(i == 0)
        def _():
            for slot in range(NSH):
                @pl.when(s == slot)
                def _():
                    for cp in copies(slot):
                        cp.wait()

        rows_i = slice(None) if th == rows else pl.ds(pl.multiple_of(i * th, th), th)
        for k in range(n):
            out[k][...] = (mine[k][...].astype(F32) + theirs[k][s, rows_i, :].astype(F32)).astype(BF16)

    spec = pl.BlockSpec((None, th, HALF), lambda s, i, cr: (s, i, 0))
    return pl.pallas_call(
        body, name=name,
        grid_spec=pltpu.PrefetchScalarGridSpec(
            num_scalar_prefetch=1, grid=(NSH, rows // th),
            in_specs=[pl.BlockSpec((None, th, HALF), lambda s, i, cr: (s, i, cr[0]))] * n + _any_specs(n),
            out_specs=[spec] * n,
            scratch_shapes=[pltpu.VMEM((NSH, rows, HALF), BF16)] * n
            + [pltpu.SemaphoreType.DMA((NSH * n,)), pltpu.SemaphoreType.DMA((NSH * n,))]),
        out_shape=[jax.ShapeDtypeStruct((NSH, rows, HALF), BF16)] * n,
        compiler_params=_cparams("arbitrary", "arbitrary"),
    )(c_idx, *ps, *ps)


def _pair_add(name, c_idx, ps, theirs, th):
    n = len(ps)
    _, rows, _ = ps[0].shape

    def body(c_ref, *refs):
        for k in range(n):
            refs[2 * n + k][...] = (refs[k][...].astype(F32) + refs[n + k][...].astype(F32)).astype(BF16)

    spec = pl.BlockSpec((None, th, HALF), lambda s, i, cr: (s, i, 0))
    return pl.pallas_call(
        body, name=name,
        grid_spec=pltpu.PrefetchScalarGridSpec(
            num_scalar_prefetch=1, grid=(NSH, rows // th),
            in_specs=[pl.BlockSpec((None, th, HALF), lambda s, i, cr: (s, i, cr[0]))] * n + [spec] * n,
            out_specs=[spec] * n),
        out_shape=[jax.ShapeDtypeStruct((NSH, rows, HALF), BF16)] * n,
        compiler_params=_cparams("parallel", "parallel"),
    )(c_idx, *ps, *theirs)


def _chip_sum(name, place, cs, ts, th):
    n = len(ts)
    _, rows, _ = ts[0].shape

    def body(p_ref, *refs):
        for i in range(n):
            t = refs[n + i][...].astype(F32)
            refs[2 * n + i][...] = ((refs[i][...].astype(F32) + t[0]) + t[1]) + t[2]

    return pl.pallas_call(
        body, name=name,
        grid_spec=pltpu.PrefetchScalarGridSpec(
            num_scalar_prefetch=1, grid=(rows // th,),
            in_specs=[pl.BlockSpec((None, th, HALF), lambda i, pr: (pr[0], i, 0))] * n
            + [pl.BlockSpec((3, th, HALF), lambda i, pr: (0, i, 0))] * n,
            out_specs=[pl.BlockSpec((th, HALF), lambda i, pr: (i, pr[1]))] * n),
        out_shape=[jax.ShapeDtypeStruct((rows, D), F32)] * n, compiler_params=_cparams("parallel"),
    )(place, *cs, *ts)


def _adamw(name, ws, gs, ms, vs, tr, tc):
    n = len(ws)
    shape = ws[0].shape
    rows, cols, mid = shape[0], shape[-1], shape[1:-1]
    c1 = 1.0 / (1.0 - ADAM_B1 ** ADAM_STEP)
    c2 = 1.0 / (1.0 - ADAM_B2 ** ADAM_STEP)

    def body(*refs):
        for i in range(n):
            w, g, m, v = (refs[k * n + i][...] for k in range(4))
            m2 = ADAM_B1 * m + (1.0 - ADAM_B1) * g
            v2 = ADAM_B2 * v + (1.0 - ADAM_B2) * (g * g)
            refs[4 * n + 4 * i][...] = -ADAM_LR * ((m2 * c1) / (jnp.sqrt(v2 * c2) + ADAM_EPS) + ADAM_WD * w)
            refs[4 * n + 4 * i + 1][...] = m2
            refs[4 * n + 4 * i + 2][...] = v2
            refs[4 * n + 4 * i + 3][...] = g

    spec = pl.BlockSpec((tr,) + mid + (tc,), lambda i, j: (i,) + (0,) * len(mid) + (j,))
    outs = pl.pallas_call(
        body, name=name, grid=(rows // tr, cols // tc), in_specs=[spec] * (4 * n), out_specs=[spec] * (4 * n),
        out_shape=[jax.ShapeDtypeStruct(shape, F32)] * (4 * n),
        compiler_params=_cparams("parallel", "parallel"),
    )(*ws, *gs, *ms, *vs)
    return [tuple(outs[4 * i:4 * i + 4]) for i in range(n)]


def _place():
    x, y, c = lax.axis_index("x"), lax.axis_index("y"), lax.axis_index("c")
    chips = [(1 - x, y), (x, 1 - y), (1 - x, 1 - y)]
    return x, y, c, chips


def _any_specs(n):
    return [pl.BlockSpec(memory_space=pl.ANY)] * n


def _rcopy(src, dst, send_sem, recv_sem, dev):
    return pltpu.make_async_remote_copy(src_ref=src, dst_ref=dst, send_sem=send_sem, recv_sem=recv_sem,
                                        device_id=dev, device_id_type=MESH)


QUARTER = HALF // 2

TO_X, TO_Y, RELAY_X, RELAY_Y, FWD_X, FWD_Y, FWD_D0, FWD_D1 = range(8)
TO_D = RELAY_X


def _gather_rider(bufs, views, relay, at=None):
    n = len(bufs)

    def plan(rout, sems):
        send, recv = sems
        x, y, c, _ = _place()
        me, sx, sy, sd = 2 * x + y, 2 * (1 - x) + y, 2 * x + (1 - y), 2 * (1 - x) + (1 - y)
        nx, ny, nd, sib = (1 - x, y, c), (x, 1 - y, c), (1 - x, 1 - y, c), (x, y, 1 - c)
        mine, other = c * HALF, (1 - c) * HALF
        out = {TO_X: (me, mine, HALF, nx), TO_Y: (me, mine, HALF, ny),
               FWD_X: (sx, mine, HALF, sib), FWD_Y: (sy, mine, HALF, sib)}
        inn = {TO_X: (sx, mine, HALF), TO_Y: (sy, mine, HALF),
               FWD_X: (sx, other, HALF), FWD_Y: (sy, other, HALF)}
        if relay:
            out.update({RELAY_X: (sy, mine, QUARTER, nx), RELAY_Y: (sx, mine + QUARTER, QUARTER, ny),
                        FWD_D0: (sd, mine, QUARTER, sib), FWD_D1: (sd, mine + QUARTER, QUARTER, sib)})
            inn.update({RELAY_X: (sd, mine, QUARTER), RELAY_Y: (sd, mine + QUARTER, QUARTER),
                        FWD_D0: (sd, other, QUARTER), FWD_D1: (sd, other + QUARTER, QUARTER)})
        else:
            out.update({TO_D: (me, mine, HALF, nd), FWD_D0: (sd, mine, HALF, sib)})
            inn.update({TO_D: (sd, mine, HALF), FWD_D0: (sd, other, HALF)})

        def copy(kind, b):
            slot, col, ncols, dev = out[kind]
            win = views[b](rout[b], slot, col, ncols)
            return _rcopy(win, win, send.at[kind * n + b], recv.at[kind * n + b], dev)

        def land(kind, b):
            slot, col, ncols = inn[kind]
            win = views[b](rout[b], slot, col, ncols)
            return _rcopy(win, win, send.at[kind * n + b], recv.at[kind * n + b], (x, y, c))

        return copy, land

    if relay:
        first = (TO_X, TO_Y)
        chain = ((TO_X, (FWD_X, RELAY_Y)), (TO_Y, (FWD_Y, RELAY_X)), (RELAY_X, (FWD_D0,)), (RELAY_Y, (FWD_D1,)))
    else:
        first = (TO_X, TO_Y, TO_D)
        chain = ((TO_X, (FWD_X,)), (TO_Y, (FWD_Y,)), (TO_D, (FWD_D0,)))
    forwards = [k for _, then in chain for k in then if k in (FWD_X, FWD_Y, FWD_D0, FWD_D1)]
    sent = list(first) + [k for _, then in chain for k in then]

    def start(rin, rout, sems):
        copy, _ = plan(rout, sems)
        for kind in first:
            for b in range(n):
                copy(kind, b).start()

    def pass_on(rout, sems, links):
        copy, land = plan(rout, sems)
        for landed, then in links:
            for b in range(n):
                land(landed, b).wait_recv()
                for kind in then:
                    copy(kind, b).start()

    def between(rin, rout, sems):
        pass_on(rout, sems, chain[:2])

    def finish(rin, rout, sems):
        pass_on(rout, sems, chain[2:] if relay else chain)
        copy, land = plan(rout, sems)
        for kind in forwards:
            for b in range(n):
                land(kind, b).wait_recv()
        for kind in sent:
            for b in range(n):
                copy(kind, b).wait_send()

    return _Rider(list(bufs), [jax.ShapeDtypeStruct(a.shape, a.dtype) for a in bufs], {b: b for b in range(n)},
                  [pltpu.SemaphoreType.DMA((8 * n,))] * 2, start, finish, between if relay else None, at)


def _small_gather_rider(cw):
    def descs(rin, rout, sems, x, y, c, chips):
        return [_rcopy(rin[0], rout[0].at[2 * x + y], sems[1].at[j], sems[2].at[j], (chip[0], chip[1], c))
                for j, chip in enumerate(chips)]

    def start(rin, rout, sems):
        x, y, c, chips = _place()
        pltpu.make_async_copy(rin[0], rout[0].at[2 * x + y], sems[0].at[0]).start()
        for cp in descs(rin, rout, sems, x, y, c, chips):
            cp.start()

    def finish(rin, rout, sems):
        x, y, c, chips = _place()
        for j, chip in enumerate(chips):
            _rcopy(rin[0], rout[0].at[2 * chip[0] + chip[1]], sems[1].at[j], sems[2].at[j], (x, y, c)).wait_recv()
        for cp in descs(rin, rout, sems, x, y, c, chips):
            cp.wait_send()
        pltpu.make_async_copy(rin[0], rout[0].at[2 * x + y], sems[0].at[0]).wait()

    return _Rider([cw], [jax.ShapeDtypeStruct((NSH,) + cw.shape, cw.dtype)], {},
                  [pltpu.SemaphoreType.DMA((1,)), pltpu.SemaphoreType.DMA((3,)), pltpu.SemaphoreType.DMA((3,))],
                  start, finish)


def _to_sibling_rider(ps):
    n = len(ps)

    def descs(rin, rout, sems):
        x, y, c, _ = _place()
        return [_rcopy(rin[i].at[:, :, pl.ds((1 - c) * HALF, HALF)], rout[i], sems[0].at[i], sems[1].at[i],
                       (x, y, 1 - c)) for i in range(n)]

    def start(rin, rout, sems):
        for cp in descs(rin, rout, sems):
            cp.start()

    def finish(rin, rout, sems):
        for cp in descs(rin, rout, sems):
            cp.wait()

    return _Rider(list(ps), [jax.ShapeDtypeStruct(a.shape[:2] + (HALF,), a.dtype) for a in ps], {},
                  [pltpu.SemaphoreType.DMA((n,))] * 2, start, finish)


def _to_chips_rider(cs):
    n = len(cs)

    def descs(rin, rout, sems):
        x, y, c, chips = _place()
        return [_rcopy(rin[i].at[2 * chip[0] + chip[1]], rout[i].at[j], sems[0].at[j * n + i], sems[1].at[j * n + i],
                       (chip[0], chip[1], c)) for j, chip in enumerate(chips) for i in range(n)]

    def start(rin, rout, sems):
        for cp in descs(rin, rout, sems):
            cp.start()

    def finish(rin, rout, sems):
        for cp in descs(rin, rout, sems):
            cp.wait()

    return _Rider(list(cs), [jax.ShapeDtypeStruct((3,) + a.shape[1:], a.dtype) for a in cs], {},
                  [pltpu.SemaphoreType.DMA((3 * n,))] * 2, start, finish)


def _join_riders(riders):
    counts = [[len(r.operands) for r in riders], [len(r.out_shapes) for r in riders], [len(r.sems) for r in riders]]

    def each(step):
        def run(*refs):
            at = [0, 0, 0]
            for i, r in enumerate(riders):
                parts = [group[at[k]:at[k] + counts[k][i]] for k, group in enumerate(refs)]
                at = [at[k] + counts[k][i] for k in range(3)]
                step(r)(*parts)
        return run

    aliases = {sum(counts[0][:i]) + k: sum(counts[1][:i]) + v
               for i, r in enumerate(riders) for k, v in r.aliases.items()}
    return _Rider([a for r in riders for a in r.operands], [s for r in riders for s in r.out_shapes], aliases,
                  [s for r in riders for s in r.sems], each(lambda r: r.start), each(lambda r: r.finish),
                  each(lambda r: r.between or (lambda *refs: None)))


SMALL_ROWS = 16


def _swap_halves(gs, vec):
    n = len(gs)

    def body(*refs):
        v_ref, dst, o_ref = refs[n], refs[n + 1:2 * n + 1], refs[2 * n + 1]
        buf, send, recv, vsend, vrecv = refs[2 * n + 2:]
        x, y, c, _ = _place()
        cps = []
        for i in range(n):
            mine = dst[i].at[:, pl.ds(c * HALF, HALF)]
            cps.append(_rcopy(mine, mine, send.at[i], recv.at[i], (x, y, 1 - c)))
        for cp in cps:
            cp.start()

        me = 4 * x + 2 * y + c
        buf[me] = v_ref[...]
        vcps = []
        for k in range(1, 8):
            peer = (x ^ (k >> 2), y ^ ((k >> 1) & 1), c ^ (k & 1))
            vcps.append(_rcopy(v_ref, buf.at[me], vsend.at[k - 1], vrecv.at[k - 1], peer))
        for cp in vcps:
            cp.start()
        for k in range(1, 8):
            _rcopy(v_ref, buf.at[me ^ k], vsend.at[k - 1], vrecv.at[k - 1], (x, y, c)).wait_recv()
        for cp in vcps:
            cp.wait_send()
        t = buf[0]
        for d in range(1, 8):
            t = t + buf[d]
        o_ref[...] = t

        for i in range(n):
            other = dst[i].at[:, pl.ds((1 - c) * HALF, HALF)]
            _rcopy(other, other, send.at[i], recv.at[i], (x, y, c)).wait_recv()
        for cp in cps:
            cp.wait_send()

    vmem = pl.BlockSpec(memory_space=pltpu.VMEM)
    res = pl.pallas_call(
        body, name="grads_swap_halves", in_specs=_any_specs(n) + [vmem], out_specs=_any_specs(n) + [vmem],
        out_shape=[jax.ShapeDtypeStruct(g.shape, g.dtype) for g in gs] + [jax.ShapeDtypeStruct((SMALL_ROWS, D), F32)],
        input_output_aliases={i: i for i in range(n)},
        scratch_shapes=[pltpu.VMEM((8, SMALL_ROWS, D), F32)] + [pltpu.SemaphoreType.DMA((n,))] * 2
        + [pltpu.SemaphoreType.DMA((7,))] * 2,
    )(*gs, vec)
    return list(res[:n]), res[n]


def _col_window(ref, slot, col, ncols):
    return ref.at[slot, :, pl.ds(col, ncols)]


def _stack_window(first, count):
    def view(ref, slot, col, ncols):
        return ref.at[slot, pl.ds(first, count), :, pl.ds(col, ncols)]
    return view


def _row_tile(rows):
    for t in range(512, 15, -16):
        if rows % t == 0:
            return t
    return rows


def _same_shape_runs(arrs):
    runs, a = [], 0
    for b in range(1, len(arrs) + 1):
        if b == len(arrs) or arrs[b].shape != arrs[a].shape:
            runs.append((a, b))
            a = b
    return runs


class _Comm:
    def __init__(self):
        x, y, c = lax.axis_index("x"), lax.axis_index("y"), lax.axis_index("c")
        self.c_idx = jnp.reshape(c, (1,)).astype(jnp.int32)
        self.shard = jnp.reshape(2 * x + y, (1,)).astype(jnp.int32)
        self.place = jnp.stack([2 * x + y, c]).astype(jnp.int32)
        self.groups = {}

    @staticmethod
    def gather(*bufs, relay, part=None, at=None):
        views = [_col_window if b.ndim == 3 else _stack_window(*(part or (0, b.shape[1]))) for b in bufs]
        return _gather_rider(list(bufs), views, relay, at)

    def reduce_rider(self, tag, names, ps, theirs=None):
        csums = []
        for a, b in _same_shape_runs(ps):
            name, th = "pair_sum_%s%d" % (tag, a), _row_tile(ps[a].shape[1])
            csums += (_pair_sum(name, self.c_idx, ps[a:b], th) if theirs is None else
                      _pair_add(name, self.c_idx, ps[a:b], theirs[a:b], th))
        self.groups[tag] = [names, csums, None]
        return _to_chips_rider(csums)

    def landed(self, tag, ts):
        self.groups[tag][2] = ts

    def finish(self, small):
        names, csums, ts = [], [], []
        for group_names, group_csums, group_ts in self.groups.values():
            names += group_names
            csums += group_csums
            ts += group_ts
        order = sorted(range(len(names)), key=lambda i: csums[i].shape[1])
        names, csums, ts = ([v[i] for i in order] for v in (names, csums, ts))
        halves = []
        for a, b in _same_shape_runs(csums):
            halves += _chip_sum("chip_sum_%d" % a, self.place, csums[a:b], ts[a:b], _row_tile(csums[a].shape[1]))
        grads, total = _swap_halves(halves, small)
        return dict(zip(names, grads)), total


ROPE_THETA = 10000.0
SMALL_1K = ("ffn1_pre_norm", "ffn1_post_norm", "mix_pre_norm", "ssm_norm", "mix_post_norm",
            "ffn2_pre_norm", "ffn2_post_norm")
SMALL_16 = ("dt_bias", "a_log", "d_skip")
OFF_CONVB = 7 * D
OFF_16 = OFF_CONVB + CONV_C
OFF_CONVW = OFF_16 + 48
OFF_LOSS = OFF_CONVW + CONV_K * CONV_C
SMALL_LEN = SMALL_ROWS * D


def _sds(shape, dtype):
    return jax.ShapeDtypeStruct(shape, dtype)


def _ridden(res, rider):
    return res if rider is not None else (res, None)


def _ffn_down(name, act, w, tail_of, rider=None):
    tail, o_specs, o_shapes = tail_of(TS)
    return _mm(name, [act, w.dn], NN, (S // TS,),
               [pl.BlockSpec((NSH, TS, FS), lambda i: (0, i, 0)),
                pl.BlockSpec((NSH, None, FS, D), lambda i: (0, w.d0, 0, 0))], o_specs, o_shapes, rider, tail)


def _ffn_dw(name, a, b, rider=None):
    return _mm(name, [a, b], TN, (NSH,),
               [pl.BlockSpec((None, S, FS), lambda s: (s, 0, 0)), pl.BlockSpec((S, D), lambda s: (0, 0))],
               pl.BlockSpec((None, FS, D), lambda s: (s, 0, 0)), _sds((NSH, FS, D), BF16), rider)


def _ffn_dn(name, dgate, dup, w, tail_of, rider=None):
    rows = TS // 2
    tail, o_specs, o_shapes = tail_of(rows)
    a2 = pl.BlockSpec((NSH, rows, FS), lambda i: (0, i, 0))
    return _mm(name, [dgate, w.gu, dup, w.gu], NN, (S // rows,),
               [a2, pl.BlockSpec((NSH, None, FS, D), lambda i: (0, w.g0, 0, 0)),
                a2, pl.BlockSpec((NSH, None, FS, D), lambda i: (0, w.g0 + 1, 0, 0))], o_specs, o_shapes, rider, tail)


def _out_proj_dx(dh, wout):
    def body(dh_ref, w_ref, dyn_ref, do_ref):
        dm = _dot(dh_ref[...], w_ref[...], NT)
        dyn_ref[...] = dm[:, D:]
        for b in range(TS // 128):
            for j, blk in enumerate(_rows_to_blocks(dm[128 * b:128 * (b + 1), :D])):
                do_ref[j, b] = blk.astype(BF16)

    return pl.pallas_call(
        body, name="out_proj_dx", grid=(S // TS,),
        in_specs=[pl.BlockSpec((TS, D), lambda i: (i, 0)), pl.BlockSpec((2 * D, D), lambda i: (0, 0))],
        out_specs=[pl.BlockSpec((TS, D), lambda i: (i, 0)),
                   pl.BlockSpec((NKV, TS // 128, HD, QROWS), lambda i: (0, i, 0, 0))],
        out_shape=[_sds((S, D), F32), _sds((NKV, NCH, HD, QROWS), BF16)], compiler_params=_cparams("parallel"),
    )(dh, wout)


def _heads(t, n):
    return t.reshape(S, n, HD).transpose(1, 0, 2)


def _pad128(v):
    return jnp.pad(v, ((0, 0), (0, 128 - v.shape[1])))


def _local_step(x, positions, tgt, sp, gu1, d1, f2, wint, wout, convw, comm=None):
    inv_freq = ROPE_THETA ** (-jnp.arange(0, HD, 2, dtype=F32) / HD)
    ang = positions.astype(F32)[:, None] * inv_freq
    ang = jnp.concatenate([ang, ang, ang, ang], axis=-1)
    cos, sin = jnp.cos(ang), jnp.sin(ang)
    dtb, alog = _pad128(sp["dt_bias"]), _pad128(sp["a_log"])
    dskip_l = jnp.repeat(sp["d_skip"], HD, axis=1)
    convb = sp["conv_b"]

    if comm:
        rider = _join_riders([comm.gather(gu1, relay=True), _small_gather_rider(convw)])
        (n1, d1, f2, wint, wout), (gu1, convw) = _prenorm_casts(
            "prenorm1", x, sp["ffn1_pre_norm"], comm.shard, [(d1, 0), (f2, 0), (wint, 1), (wout, 0)], rider)
        wint, wout = wint.reshape(NSH, WIN_SH, D), wout.reshape(NSH, 2 * D // NSH, D)
        convw = convw.transpose(1, 0, 2).reshape(CONV_K, CONV_C)
    else:
        n1 = _prenorm("prenorm1", x, sp["ffn1_pre_norm"])
    rider = comm.gather(d1, relay=True) if comm else None
    (fg1, fu1, act1), got = _ridden(_ffn_up("ffn1_up", n1, _FfnW(gu1, 0, d1, 0), rider), rider)
    if comm:
        d1, = got
    w1 = _FfnW(gu1, 0, d1, 0)
    rider = comm.gather(wint, relay=True) if comm else None
    (h1, x1, n2), got = _ridden(_ffn_down(
        "ffn1_down", act1, w1,
        lambda rows: _tail_postres(rows, x, sp["ffn1_post_norm"], 0.5, sp["mix_pre_norm"]), rider), rider)
    if comm:
        wint, = got
    wint_pad = jnp.pad(wint.reshape(WIN_COLS, D), ((0, WIN_PAD - WIN_COLS), (0, 0)))

    pw = WIN_PAD // 3
    rider = comm.gather(f2, relay=True, part=(0, 1)) if comm else None
    proj, got = _ridden(_mm(
        "in_proj", [n2, wint_pad], NT, (S // TS, 3),
        [pl.BlockSpec((TS, D), lambda i, j: (i, 0)), pl.BlockSpec((pw, D), lambda i, j: (j, 0))],
        pl.BlockSpec((TS, pw), lambda i, j: (i, j)), _sds((S, WIN_PAD), F32), rider), rider)
    if comm:
        f2, = got
    qt = _rope_q(proj, cos, sin)
    k_rot, v_bf, kt, vt = _rope_kv(proj, cos, sin)
    kh, vh = _heads(k_rot, NKV), _heads(v_bf, NKV)
    bias = _bias_table()
    rider = comm.gather(f2, wout, relay=True, part=(1, 2), at=13) if comm else None
    (ot, lse, mixed), got = _ridden(_attn_fwd(qt, kh, vt, bias, rider), rider)
    if comm:
        f2, wout = got
    w2 = _FfnW(f2, 0, f2, 2)
    wout = wout.reshape(2 * D, D)
    xbc, conv_y = _conv_fwd(proj, convw, convb)
    y, mixed, hprev = _ssd_fwd(xbc, proj, dtb, alog, dskip_l, sp["ssm_norm"], mixed)
    tail, o_specs, o_shapes = _tail_postres(TS, x1, sp["mix_post_norm"], 1.0, sp["ffn2_pre_norm"])
    h2, x2, n3 = _mm("out_proj", [mixed, wout], NN, (S // TS,),
                     [pl.BlockSpec((TS, 2 * D), lambda i: (i, 0)), pl.BlockSpec((2 * D, D), lambda i: (0, 0))],
                     o_specs, o_shapes, None, tail)

    fg2, fu2, act2 = _ffn_up("ffn2_up", n3, w2)
    dy, dh3, dp3, loss = _ffn_down(
        "ffn2_down", act2, w2, lambda rows: _tail_final(rows, x2, sp["ffn2_post_norm"], tgt, 0.5))

    dgate2, dup2 = _ffn_dact("ffn2_dact", dh3, w2, fg2, fu2)
    dws2 = [_ffn_dw("ffn2_dwg", dgate2, n3), _ffn_dw("ffn2_dwu", dup2, n3), _ffn_dw("ffn2_dwd", act2, dh3)]
    dx2, dh2, dg3, dp2 = _ffn_dn(
        "ffn2_dn", dgate2, dup2, w2,
        lambda rows: _tail_mid_bwd(rows, dy, x2, sp["ffn2_pre_norm"], h2, sp["mix_post_norm"], 1.0))

    dyn, dot_ = _out_proj_dx(dh2, wout)
    dwout = _mm("out_proj_dw", [mixed, dh2], TN, (2,),
                [pl.BlockSpec((S, D), lambda m: (0, m)), pl.BlockSpec((S, D), lambda m: (0, 0))],
                pl.BlockSpec((D, D), lambda m: (m, 0)), _sds((2 * D, D), BF16))
    dwout = dwout.reshape(NSH, 2 * D // NSH, D)

    def riding(tag, names, ps, call, theirs=None):
        rider = comm.reduce_rider(tag, names, ps, theirs) if comm else None
        res, got = _ridden(call(rider), rider)
        if comm:
            comm.landed(tag, got)
        return res

    rider = _to_sibling_rider(dws2 + [dwout]) if comm else None
    (dxbc, dproj, ddt, dssm, dsc), theirs = _ridden(
        _ssd_bwd(dyn, y, xbc, proj, hprev, dtb, alog, dskip_l, sp["ssm_norm"], rider), rider)
    dproj, dcw8, dcb = _conv_bwd(dxbc, conv_y, proj, convw, dproj)
    delta = _attn_delta(ot, dot_)
    dqt, dkh, dvh = riding("a", BIG[3:6] + ("w_out",), dws2 + [dwout], lambda rider: _attn_bwd(
        qt, kh, kt, vh, dot_, lse, delta, bias, rider), theirs)
    dproj = _rope_dq(dqt, cos, sin, dproj)
    dproj = _rope_dkv(dkh, dvh, cos, sin, dproj)
    dproj = lax.dynamic_update_slice(dproj, ddt, (0, COL_DT))
    dwint = _mm("in_proj_dw", [dproj, n2], TN, (3,),
                [pl.BlockSpec((S, pw), lambda j: (0, j)), pl.BlockSpec((S, D), lambda j: (0, 0))],
                pl.BlockSpec((pw, D), lambda j: (j, 0)), _sds((WIN_PAD, D), BF16))
    dwint = dwint[:WIN_COLS].reshape(NSH, WIN_SH, D)

    tail, o_specs, o_shapes = _tail_mid_bwd(TS, dx2, x1, sp["mix_pre_norm"], h1, sp["ffn1_post_norm"], 0.5)
    dx1, dh1, dg2, dp1 = riding("b", ("w_in",), [dwint], lambda rider: _mm(
        "in_proj_dx", [dproj, wint_pad], NN, (S // TS,),
        [pl.BlockSpec((TS, WIN_PAD), lambda i: (i, 0)), pl.BlockSpec((WIN_PAD, D), lambda i: (0, 0))],
        o_specs, o_shapes, rider, tail))

    dwd1 = _ffn_dw("ffn1_dwd", act1, dh1)
    dgate1, dup1 = riding("d", BIG[2:3], [dwd1], lambda rider: _ffn_dact("ffn1_dact", dh1, w1, fg1, fu1, rider))
    dwg1, dwu1 = _ffn_dw("ffn1_dwg", dgate1, n1), _ffn_dw("ffn1_dwu", dup1, n1)
    grad_x, dg1 = riding("g", BIG[0:2], [dwg1, dwu1], lambda rider: _ffn_dn(
        "ffn1_dn", dgate1, dup1, w1, lambda rows: _tail_first_bwd(rows, dx1, x, sp["ffn1_pre_norm"]), rider))
    dws1 = [dwg1, dwu1, dwd1]

    small = jnp.concatenate([
        dg1[0], dp1[0], dg2[0], dssm[0], dp2[0], dg3[0], dp3[0], dcb[0],
        dsc[0, :16], dsc[1, :16], dsc[2, :16], dcw8[:CONV_K].reshape(-1), loss[0, :1]])
    small = jnp.pad(small, (0, SMALL_LEN - small.shape[0])).reshape(SMALL_ROWS, D)
    if comm is None:
        return grad_x, dws1 + dws2 + [dwint, dwout], small
    return (grad_x,) + comm.finish(small)


WEIGHTS = ("ffn1_pre_norm", "ffn1_w_gate", "ffn1_w_up", "ffn1_w_down", "ffn1_post_norm", "mix_pre_norm", "w_in",
           "conv_w", "conv_b", "dt_bias", "a_log", "d_skip", "ssm_norm", "w_out", "mix_post_norm", "ffn2_pre_norm",
           "ffn2_w_gate", "ffn2_w_up", "ffn2_w_down", "ffn2_post_norm")
BIG = ("ffn1_w_gate", "ffn1_w_up", "ffn1_w_down", "ffn2_w_gate", "ffn2_w_up", "ffn2_w_down", "w_in", "w_out")
TRANSPOSED = ("ffn1_w_gate", "ffn1_w_up", "ffn2_w_gate", "ffn2_w_up", "w_in")
SMALL_ORDER = SMALL_1K + ("conv_b",) + SMALL_16
CONVW_SH = CONV_C // NSH


def _shard2d(t, name):
    return t[0].T if name in TRANSPOSED else t[0]


def _unshard2d(t, name):
    return (t.T if name in TRANSPOSED else t)[None]


def _rows3d(t):
    return t.transpose(2, 0, 1)


def _pack_small(d, prefix, shard_of_convw):
    flat = jnp.concatenate([d[prefix + n][0] for n in SMALL_ORDER] + [shard_of_convw.reshape(-1)])
    return jnp.pad(flat, (0, SMALL_LEN - flat.shape[0])).reshape(SMALL_ROWS, D)


def _unpack_small(block, like):
    flat = block.reshape(-1)
    out, off = {}, 0
    for n in SMALL_ORDER:
        size = like[n].shape[1]
        out[n] = flat[off:off + size].reshape(1, size)
        off += size
    out["conv_w"] = flat[off:off + CONV_K * CONVW_SH].reshape(1, CONV_K, CONVW_SH)
    return out


def kernel(x, positions, ffn1_pre_norm, ffn1_w_gate, ffn1_w_up, ffn1_w_down, ffn1_post_norm, mix_pre_norm, w_in, conv_w, conv_b, dt_bias, a_log, d_skip, ssm_norm, w_out, mix_post_norm, ffn2_pre_norm, ffn2_w_gate, ffn2_w_up, ffn2_w_down, ffn2_post_norm, loss_target, m_ffn1_pre_norm, m_ffn1_w_gate, m_ffn1_w_up, m_ffn1_w_down, m_ffn1_post_norm, m_mix_pre_norm, m_w_in, m_conv_w, m_conv_b, m_dt_bias, m_a_log, m_d_skip, m_ssm_norm, m_w_out, m_mix_post_norm, m_ffn2_pre_norm, m_ffn2_w_gate, m_ffn2_w_up, m_ffn2_w_down, m_ffn2_post_norm, v_ffn1_pre_norm, v_ffn1_w_gate, v_ffn1_w_up, v_ffn1_w_down, v_ffn1_post_norm, v_mix_pre_norm, v_w_in, v_conv_w, v_conv_b, v_dt_bias, v_a_log, v_d_skip, v_ssm_norm, v_w_out, v_mix_post_norm, v_ffn2_pre_norm, v_ffn2_w_gate, v_ffn2_w_up, v_ffn2_w_down, v_ffn2_post_norm):
    given = dict(locals())
    xi, yi = lax.axis_index("x"), lax.axis_index("y")

    comm = _Comm()
    big = {p + n: _shard2d(given[p + n], n) for n in BIG for p in ("", "m_", "v_")}
    gu1 = _cast_stack("cast_ffn1_gate_up", comm.shard, [big[n] for n in BIG[0:2]], 176, D)

    sp = {n: given[n] for n in SMALL_ORDER}
    grad_x, big_grads, small = _local_step(
        x[0], positions[0], loss_target[0], sp, gu1, [big[BIG[2]]], [big[n] for n in BIG[3:6]], [big["w_in"]],
        [big["w_out"]], conv_w[0], comm)

    tot = small.reshape(-1)
    loss = tot[OFF_LOSS]
    small_grads, off = {}, 0
    for n in SMALL_ORDER:
        size = given[n].shape[1]
        small_grads[n] = tot[off:off + size].reshape(1, size)
        off += size
    dconvw = tot[OFF_CONVW:OFF_CONVW + CONV_K * CONV_C].reshape(CONV_K, NSH, CONVW_SH)
    dconvw = lax.dynamic_index_in_dim(dconvw, 2 * xi + yi, axis=1, keepdims=False)
    small_grads["conv_w"] = dconvw.reshape(1, CONV_K, CONVW_SH)

    upd = {}
    for names, tr in ((BIG[0:3], 176), (BIG[3:6], 176), (BIG[7:8], 256)):
        res = _adamw("adamw_" + names[0], [big[n] for n in names], [big_grads[n] for n in names],
                     [big["m_" + n] for n in names], [big["v_" + n] for n in names], tr, D)
        for n, r in zip(names, res):
            upd[n] = tuple(_unshard2d(t, n) for t in r)
    g_win = big_grads["w_in"].reshape(WIN_SH, 1, D)
    res, = _adamw("adamw_w_in", [_rows3d(w_in)], [g_win], [_rows3d(m_w_in)], [_rows3d(v_w_in)], WIN_SH // 4, D)
    upd["w_in"] = tuple(t.transpose(1, 2, 0) for t in res)
    (dl, m2, v2, _), = _adamw(
        "adamw_small", [_pack_small(given, "", conv_w[0])], [_pack_small(small_grads, "", dconvw)],
        [_pack_small(given, "m_", m_conv_w[0])], [_pack_small(given, "v_", v_conv_w[0])], SMALL_ROWS, D)
    dl, m2, v2 = (_unpack_small(t, given) for t in (dl, m2, v2))
    for n in SMALL_ORDER + ("conv_w",):
        upd[n] = (dl[n], m2[n], v2[n], small_grads[n])

    return (loss, grad_x[None], *[upd[n][3] for n in WEIGHTS], *[upd[n][0] for n in WEIGHTS],
            *[upd[n][1] for n in WEIGHTS], *[upd[n][2] for n in WEIGHTS])
```

```python
import functools
import typing

import jax
import jax.numpy as jnp
from jax import lax
from jax.experimental import pallas as pl
from jax.experimental.pallas import tpu as pltpu

F32 = jnp.float32
BF16 = jnp.bfloat16

S = 2048
D = 1024
FF = 2816
NSH = 4
FS = FF // NSH
HALF = D // 2
HD = 64
NKV = 4
NQ_PER_KV = 4
KVW = NKV * HD
QCOLS = NQ_PER_KV * HD
CONV_C = 1536
CONV_K = 4
SSM_W = 1024
NST = 128
NCH = S // 128
WIN_COLS = 4112
WIN_SH = WIN_COLS // NSH
WIN_PAD = 4224
COL_DT = 4096
EPS = 1e-6
NEG = -1e30

ADAM_LR = 0.001
ADAM_B1 = 0.9
ADAM_B2 = 0.999
ADAM_EPS = 1e-08
ADAM_WD = 0.01
ADAM_STEP = 10

VMEM_LIMIT = 56 * 1024 * 1024
TS = 512
TR = 256

NN = (((1,), (0,)), ((), ()))
NT = (((1,), (1,)), ((), ()))
TN = (((0,), (0,)), ((), ()))
MESH = pl.DeviceIdType.MESH


def _cparams(*sem):
    return pltpu.CompilerParams(dimension_semantics=sem, vmem_limit_bytes=VMEM_LIMIT)


def _dot(a, b, dims):
    return lax.dot_general(a.astype(BF16), b.astype(BF16), dims, preferred_element_type=F32)


def _bf16_pieces(v):
    hi = v.astype(BF16)
    rest = v - hi.astype(F32)
    mid = rest.astype(BF16)
    return hi, mid, (rest - mid.astype(F32)).astype(BF16)


def _dot_exact(a, b, ones="a"):
    if ones == "a":
        sel = a.astype(BF16)
        parts = [lax.dot_general(sel, p, NN, preferred_element_type=F32) for p in _bf16_pieces(b)]
    else:
        sel = b.astype(BF16)
        parts = [lax.dot_general(p, sel, NN, preferred_element_type=F32) for p in _bf16_pieces(a)]
    return (parts[2] + parts[1]) + parts[0]


def _sigmoid(v):
    return 1.0 / (1.0 + jnp.exp(-v))


class _Rider(typing.NamedTuple):
    operands: list
    out_shapes: list
    aliases: dict
    sems: list
    start: typing.Callable
    finish: typing.Callable
    between: typing.Callable = None
    at: int = None


def _call(body, name, grid, in_specs, out_specs, out_shape, operands, scratch=(), sem=(), rider=None, prefetch=0):
    multi = isinstance(out_shape, (list, tuple))

    def launch(kernel, in_specs, out_specs, out_shape, scratch, aliases, sem, args):
        if prefetch:
            how = dict(grid_spec=pltpu.PrefetchScalarGridSpec(
                num_scalar_prefetch=prefetch, grid=grid, in_specs=in_specs, out_specs=out_specs,
                scratch_shapes=scratch))
        else:
            how = dict(grid=grid, in_specs=in_specs, out_specs=out_specs, scratch_shapes=scratch)
        return pl.pallas_call(kernel, name=name, out_shape=out_shape, input_output_aliases=aliases,
                              compiler_params=_cparams(*sem), **how)(*args)

    if rider is None:
        return launch(body, in_specs, out_specs, out_shape, list(scratch), {}, sem, operands)
    outs = list(out_shape) if multi else [out_shape]
    ospecs = list(out_specs) if multi else [out_specs]
    n_in, n_out, n_scr = len(operands) - prefetch, len(outs), len(scratch)
    ri, ro = len(rider.operands), len(rider.out_shapes)

    def wrapped(*refs):
        scalars, refs = refs[:prefetch], refs[prefetch:]
        o0 = n_in + ri
        s0 = o0 + n_out + ro
        rin, rout, rsem = refs[n_in:o0], refs[o0 + n_out:s0], refs[s0 + n_scr:]
        ids = [pl.program_id(a) for a in range(len(grid))]
        first = functools.reduce(jnp.logical_and, [i == 0 for i in ids])
        last = functools.reduce(jnp.logical_and, [i == g - 1 for i, g in zip(ids, grid)])

        @pl.when(first)
        def _():
            rider.start(rin, rout, rsem)

        if rider.between is not None:
            steps = functools.reduce(lambda a, b: a * b, grid)
            step = functools.reduce(lambda a, ig: a * ig[1] + ig[0], zip(ids, grid), 0)

            @pl.when(step == (2 * steps // 3 if rider.at is None else rider.at))
            def _():
                rider.between(rin, rout, rsem)

        body(*scalars, *refs[:n_in], *refs[o0:o0 + n_out], *refs[s0:s0 + n_scr])

        @pl.when(last)
        def _():
            rider.finish(rin, rout, rsem)

    hbm = pl.BlockSpec(memory_space=pl.ANY)
    res = launch(wrapped, list(in_specs) + [hbm] * ri, ospecs + [hbm] * ro, outs + list(rider.out_shapes),
                 list(scratch) + list(rider.sems),
                 {prefetch + n_in + k: n_out + v for k, v in rider.aliases.items()},
                 ("arbitrary",) * len(grid), (*operands, *rider.operands))
    main = list(res[:n_out])
    return (main if multi else main[0]), list(res[n_out:])


class _Tail(typing.NamedTuple):
    fn: typing.Callable
    operands: list
    in_specs: list


def _mm(name, operands, dims, grid, in_specs, o_spec, out_shape, rider=None, tail=None):
    npairs = len(operands) // 2
    extra = [] if tail is None else list(tail.operands)
    nin = 2 * npairs + len(extra)

    def body(*refs):
        t = None
        for i in range(npairs):
            a, b = refs[2 * i], refs[2 * i + 1]
            parts = [(a[s], b[s]) for s in range(a.shape[0])] if len(a.shape) == 3 else [(a[...], b[...])]
            for pa, pb in parts:
                d = _dot(pa, pb, dims)
                t = d if t is None else t + d
        if tail is None:
            refs[nin][...] = t.astype(refs[nin].dtype)
        else:
            tail.fn(t, refs[2 * npairs:nin], refs[nin:])

    sem = ("parallel" if tail is None else "arbitrary",) * len(grid)
    specs = list(in_specs) + ([] if tail is None else list(tail.in_specs))
    return _call(body, name, grid, specs, o_spec, out_shape, list(operands) + extra, (), sem, rider)


class _FfnW(typing.NamedTuple):
    gu: jax.Array
    g0: int
    dn: jax.Array
    d0: int


def _ffn_up(name, n, w, rider=None):
    def body(n_ref, wg_ref, wu_ref, fg_ref, fu_ref, a_ref):
        nb = n_ref[...]
        g = _dot(nb, wg_ref[...], NT)
        u = _dot(nb, wu_ref[...], NT)
        sg = _sigmoid(g)
        silu = g * sg
        fg_ref[...] = (u * (sg * (1.0 + g * (1.0 - sg)))).astype(BF16)
        fu_ref[...] = silu.astype(BF16)
        a_ref[...] = (silu * u).astype(BF16)

    out = jax.ShapeDtypeStruct((NSH, S, FS), BF16)
    ospec = pl.BlockSpec((None, TS, FS), lambda s, i: (s, i, 0))
    return _call(
        body, name, (NSH, S // TS),
        [pl.BlockSpec((TS, D), lambda s, i: (i, 0)),
         pl.BlockSpec((None, None, FS, D), lambda s, i: (s, w.g0, 0, 0)),
         pl.BlockSpec((None, None, FS, D), lambda s, i: (s, w.g0 + 1, 0, 0))],
        [ospec, ospec, ospec], [out, out, out], (n, w.gu, w.gu), sem=("parallel", "parallel"), rider=rider)


def _ffn_dact(name, dh, w, fgate, fup, rider=None):
    def body(dh_ref, wd_ref, fg_ref, fu_ref, dg_ref, du_ref):
        da = _dot(dh_ref[...], wd_ref[...], NT)
        dg_ref[...] = (da * fg_ref[...].astype(F32)).astype(BF16)
        du_ref[...] = (da * fu_ref[...].astype(F32)).astype(BF16)

    out = jax.ShapeDtypeStruct((NSH, S, FS), BF16)
    aspec = pl.BlockSpec((None, TS, FS), lambda s, i: (s, i, 0))
    return _call(
        body, name, (NSH, S // TS),
        [pl.BlockSpec((TS, D), lambda s, i: (i, 0)),
         pl.BlockSpec((None, None, FS, D), lambda s, i: (s, w.d0, 0, 0)), aspec, aspec],
        [aspec, aspec], [out, out], (dh, w.dn, fgate, fup), sem=("parallel", "parallel"), rider=rider)


def _rstd(v):
    return lax.rsqrt(jnp.mean(v * v, axis=-1, keepdims=True) + EPS)


def _row_spec():
    return pl.BlockSpec((TR, D), lambda i: (i, 0))


def _vec_spec():
    return pl.BlockSpec((1, D), lambda i: (0, 0))


def _acc_rows(ref, v):
    @pl.when(pl.program_id(0) == 0)
    def _():
        ref[...] = jnp.zeros_like(ref)
    ref[...] += jnp.sum(v, axis=0, keepdims=True)


def _prenorm(name, x, g):
    def body(x_ref, g_ref, n_ref):
        xv = x_ref[...]
        n_ref[...] = (xv * _rstd(xv) * g_ref[...]).astype(BF16)

    return pl.pallas_call(
        body, name=name, grid=(S // TR,), in_specs=[_row_spec(), _vec_spec()], out_specs=_row_spec(),
        out_shape=jax.ShapeDtypeStruct((S, D), BF16), compiler_params=_cparams("parallel"),
    )(x, g)


ENTRY_STEPS = 4


def _prenorm_casts(name, x, g, slot, groups, rider):
    def body(s_ref, x_ref, g_ref, *refs):
        ins, outs = refs[:len(refs) - len(groups) - 1], refs[len(refs) - len(groups) - 1:]
        xv = x_ref[...]
        outs[0][...] = (xv * _rstd(xv) * g_ref[...]).astype(BF16)
        at = 0
        for (arrs, _), out in zip(groups, outs[1:]):
            for k in range(len(arrs)):
                out[k] = ins[at + k][...].astype(BF16)
            at += len(arrs)

    rows = pl.BlockSpec((S // ENTRY_STEPS, D), lambda i, sr: (i, 0))
    in_specs, out_specs, out_shapes = [rows, pl.BlockSpec((1, D), lambda i, sr: (0, 0))], [rows], [_rows_bf16()]
    for arrs, axis in groups:
        r, c = arrs[0].shape
        if axis == 0:
            blk, at, at_out = (r // ENTRY_STEPS, c), (lambda i, sr: (i, 0)), (lambda i, sr: (sr[0], 0, i, 0))
        else:
            blk, at, at_out = (r, c // ENTRY_STEPS), (lambda i, sr: (0, i)), (lambda i, sr: (sr[0], 0, 0, i))
        in_specs += [pl.BlockSpec(blk, at)] * len(arrs)
        out_specs.append(pl.BlockSpec((None, len(arrs)) + blk, at_out))
        out_shapes.append(jax.ShapeDtypeStruct((NSH, len(arrs), r, c), BF16))
    return _call(body, name, (ENTRY_STEPS,), in_specs, out_specs, out_shapes,
                 [slot, x, g] + [a for arrs, _ in groups for a in arrs], rider=rider, prefetch=1)


def _rows_spec(rows):
    return pl.BlockSpec((rows, D), lambda i: (i, 0))


def _rows_f32():
    return jax.ShapeDtypeStruct((S, D), F32)


def _rows_bf16():
    return jax.ShapeDtypeStruct((S, D), BF16)


def _vec_f32():
    return jax.ShapeDtypeStruct((1, D), F32)


def _tail_postres(rows, x, p, alpha, gnext):
    def fn(h, ins, outs):
        x_ref, p_ref, g_ref = ins
        h_ref, xo_ref, n_ref = outs
        h_ref[...] = h
        xo = x_ref[...] + alpha * (h * _rstd(h) * p_ref[...])
        xo_ref[...] = xo
        n_ref[...] = (xo * _rstd(xo) * g_ref[...]).astype(BF16)

    rs = _rows_spec(rows)
    return (_Tail(fn, [x, p, gnext], [rs, _vec_spec(), _vec_spec()]), [rs, rs, rs],
            [_rows_f32(), _rows_f32(), _rows_bf16()])


def _tail_final(rows, x, p, tgt, alpha):
    def fn(h, ins, outs):
        x_ref, p_ref, t_ref = ins
        dy_ref, dh_ref, dp_ref, loss_ref = outs
        r = _rstd(h)
        hn = h * r
        pv = p_ref[...]
        e = x_ref[...] + alpha * (hn * pv) - t_ref[...]
        dy = e * (1.0 / D)
        dy_ref[...] = dy
        du = alpha * dy * pv
        dh_ref[...] = (r * (du - hn * jnp.mean(du * hn, axis=-1, keepdims=True))).astype(BF16)
        _acc_rows(dp_ref, alpha * dy * hn)
        part = 0.5 * jnp.sum(jnp.mean(e * e, axis=-1, keepdims=True), axis=0, keepdims=True)
        _acc_rows(loss_ref, jnp.broadcast_to(part, (1, 128)))

    rs = _rows_spec(rows)
    return (_Tail(fn, [x, p, tgt], [rs, _vec_spec(), rs]),
            [rs, rs, _vec_spec(), pl.BlockSpec((1, 128), lambda i: (0, 0))],
            [_rows_f32(), _rows_bf16(), _vec_f32(), jax.ShapeDtypeStruct((1, 128), F32)])


def _norm_bwd(dn, xv, g_ref, dg_ref):
    r = _rstd(xv)
    xn = xv * r
    dng = dn * g_ref[...]
    _acc_rows(dg_ref, dn * xn)
    return r * (dng - xn * jnp.mean(dng * xn, axis=-1, keepdims=True))


def _tail_mid_bwd(rows, dres, x, g, h, p, alpha):
    def fn(dn, ins, outs):
        dr_ref, x_ref, g_ref, h_ref, p_ref = ins
        dx_ref, dh_ref, dg_ref, dp_ref = outs
        dx = dr_ref[...] + _norm_bwd(dn, x_ref[...], g_ref, dg_ref)
        dx_ref[...] = dx
        hv = h_ref[...]
        r = _rstd(hv)
        hn = hv * r
        du = alpha * dx * p_ref[...]
        dh_ref[...] = (r * (du - hn * jnp.mean(du * hn, axis=-1, keepdims=True))).astype(BF16)
        _acc_rows(dp_ref, alpha * dx * hn)

    rs = _rows_spec(rows)
    return (_Tail(fn, [dres, x, g, h, p], [rs, rs, _vec_spec(), rs, _vec_spec()]),
            [rs, rs, _vec_spec(), _vec_spec()], [_rows_f32(), _rows_bf16(), _vec_f32(), _vec_f32()])


def _tail_first_bwd(rows, dres, x, g):
    def fn(dn, ins, outs):
        dr_ref, x_ref, g_ref = ins
        dx_ref, dg_ref = outs
        dx_ref[...] = dr_ref[...] + _norm_bwd(dn, x_ref[...], g_ref, dg_ref)

    rs = _rows_spec(rows)
    return (_Tail(fn, [dres, x, g], [rs, rs, _vec_spec()]), [rs, _vec_spec()], [_rows_f32(), _vec_f32()])


def _rotate(t, c128, s128, sign, scale):
    width = t.shape[1]
    c = jnp.tile(c128, (1, width // 128))
    sn = jnp.tile(s128, (1, width // 128))
    lane = lax.broadcasted_iota(jnp.int32, t.shape, 1) & (HD - 1)
    rot = jnp.where(lane < HD // 2, -pltpu.roll(t, width - HD // 2, 1), pltpu.roll(t, HD // 2, 1))
    return (t * c + sign * (rot * sn)) * scale


def _rows_to_blocks(y):
    out = []
    for j in range(NKV):
        yt = y[:, QCOLS * j:QCOLS * (j + 1)].T
        out.append(jnp.concatenate([yt[HD * g:HD * (g + 1)] for g in range(NQ_PER_KV)], axis=1))
    return out


def _blocks_to_rows(blocks):
    cols = []
    for b in blocks:
        stacked = jnp.concatenate([b[:, 128 * g:128 * (g + 1)] for g in range(NQ_PER_KV)], axis=0)
        cols.append(stacked.T)
    return jnp.concatenate(cols, axis=1)


def _rope_q(proj, cos, sin):
    def body(t_ref, c_ref, s_ref, o_ref):
        y = _rotate(t_ref[...], c_ref[...], s_ref[...], 1.0, HD ** -0.5)
        for j, blk in enumerate(_rows_to_blocks(y)):
            o_ref[j] = blk.astype(BF16)

    return pl.pallas_call(
        body, name="rope_q", grid=(NCH,),
        in_specs=[pl.BlockSpec((128, D), lambda i: (i, 0)),
                  pl.BlockSpec((128, 128), lambda i: (i, 0)), pl.BlockSpec((128, 128), lambda i: (i, 0))],
        out_specs=pl.BlockSpec((NKV, None, HD, QROWS), lambda i: (0, i, 0, 0)),
        out_shape=jax.ShapeDtypeStruct((NKV, NCH, HD, QROWS), BF16), compiler_params=_cparams("parallel"),
    )(proj, cos, sin)


def _rope_dq(dqt, cos, sin, dproj):
    def body(t_ref, c_ref, s_ref, buf_ref, o_ref):
        t = _blocks_to_rows([t_ref[j] for j in range(NKV)])
        o_ref[...] = _rotate(t, c_ref[...], s_ref[...], -1.0, HD ** -0.5).astype(BF16)

    return pl.pallas_call(
        body, name="rope_dq", grid=(NCH,),
        in_specs=[pl.BlockSpec((NKV, None, HD, QROWS), lambda i: (0, i, 0, 0)),
                  pl.BlockSpec((128, 128), lambda i: (i, 0)), pl.BlockSpec((128, 128), lambda i: (i, 0)),
                  pl.BlockSpec(memory_space=pl.ANY)],
        out_specs=pl.BlockSpec((128, D), lambda i: (i, 0)),
        out_shape=jax.ShapeDtypeStruct(dproj.shape, BF16), input_output_aliases={3: 0},
        compiler_params=_cparams("parallel"),
    )(dqt, cos, sin, dproj)


def _rope_dkv(dkt, dvt, cos, sin, dproj):
    def body(k_ref, v_ref, c_ref, s_ref, buf_ref, o_ref):
        dk = jnp.concatenate([k_ref[j] for j in range(NKV)], axis=0).T
        dv = jnp.concatenate([v_ref[j] for j in range(NKV)], axis=0).T
        dk = _rotate(dk, c_ref[...], s_ref[...], -1.0, 1.0)
        o_ref[...] = jnp.concatenate([dk, dv], axis=1).astype(BF16)

    tspec = pl.BlockSpec((NKV, HD, 128), lambda i: (0, 0, i))
    return pl.pallas_call(
        body, name="rope_dkv", grid=(NCH,),
        in_specs=[tspec, tspec, pl.BlockSpec((128, 128), lambda i: (i, 0)), pl.BlockSpec((128, 128), lambda i: (i, 0)),
                  pl.BlockSpec(memory_space=pl.ANY)],
        out_specs=pl.BlockSpec((128, 2 * KVW), lambda i: (i, D // (2 * KVW))),
        out_shape=jax.ShapeDtypeStruct(dproj.shape, BF16), input_output_aliases={4: 0},
        compiler_params=_cparams("parallel"),
    )(dkt, dvt, cos, sin, dproj)


def _rope_kv(proj, cos, sin):
    def body(t_ref, c_ref, s_ref, k_ref, v_ref, kt_ref, vt_ref):
        t = t_ref[...]
        k = _rotate(t[:, :KVW], c_ref[...], s_ref[...], 1.0, 1.0).astype(BF16)
        v = t[:, KVW:].astype(BF16)
        k_ref[...] = k
        v_ref[...] = v
        kt, vt = k.astype(F32).T, v.astype(F32).T
        for j in range(NKV):
            kt_ref[j] = kt[HD * j:HD * (j + 1)].astype(BF16)
            vt_ref[j] = vt[HD * j:HD * (j + 1)].astype(BF16)

    rows = pl.BlockSpec((128, KVW), lambda i: (i, 0))
    tspec = pl.BlockSpec((NKV, HD, 128), lambda i: (0, 0, i))
    return pl.pallas_call(
        body, name="rope_kv", grid=(NCH,),
        in_specs=[pl.BlockSpec((128, 2 * KVW), lambda i: (i, D // (2 * KVW))),
                  pl.BlockSpec((128, 128), lambda i: (i, 0)), pl.BlockSpec((128, 128), lambda i: (i, 0))],
        out_specs=[rows, rows, tspec, tspec],
        out_shape=[jax.ShapeDtypeStruct((S, KVW), BF16)] * 2 + [jax.ShapeDtypeStruct((NKV, HD, S), BF16)] * 2,
        compiler_params=_cparams("parallel"),
    )(proj, cos, sin)


QROWS = NQ_PER_KV * 128


NBIAS = NCH + 1
KV_PER_STEP = 4


def _bias_table():
    db = lax.broadcasted_iota(jnp.int32, (NBIAS, 128, QROWS), 0) - 1
    ki = lax.broadcasted_iota(jnp.int32, (NBIAS, 128, QROWS), 1)
    qi = lax.broadcasted_iota(jnp.int32, (NBIAS, 128, QROWS), 2) & 127
    d = db * 128 + qi - ki
    cnt = ((d <= 128).astype(F32) + (((d & 3) == 0) & (d <= 512)).astype(F32) + ((d & 15) == 0).astype(F32))
    return jnp.where((d >= 0) & (cnt > 0.0), jnp.log(jnp.maximum(cnt, 1.0)), NEG)


def _qt_spec():
    return pl.BlockSpec((None, None, HD, QROWS), lambda j, i: (j, i, 0, 0))


def _stat_spec():
    return pl.BlockSpec((None, None, 1, QROWS), lambda j, i: (j, i, 0, 0))


def _attn_fwd(qt, kh, vt, bias, rider=None):
    def body(q_ref, k_ref, v_ref, b_ref, o_ref, lse_ref, rows_ref, m_ref, l_ref, acc_ref):
        qb = pl.program_id(1)
        m_ref[...] = jnp.full_like(m_ref, NEG)
        l_ref[...] = jnp.zeros_like(l_ref)
        acc_ref[...] = jnp.zeros_like(acc_ref)

        def keys(off, size, bias_):
            for h in range(KV_PER_STEP):
                m = m_ref[h]
                s = _dot(k_ref[h, pl.ds(off, size), :], q_ref[h], NN) + bias_
                m_new = jnp.maximum(m, jnp.max(s, axis=0, keepdims=True))
                p = jnp.exp(s - m_new)
                a = jnp.exp(m - m_new)
                m_ref[h] = m_new
                l_ref[h] = a * l_ref[h] + jnp.sum(p, axis=0, keepdims=True)
                acc_ref[h] = a * acc_ref[h] + _dot(v_ref[h, :, pl.ds(off, size)], p, NN)

        def blocks(first, count):
            bias_ = jnp.concatenate([b_ref[qb - first - j + 1] for j in range(count)], axis=0)
            keys(pl.multiple_of(first * 128, 128), 128 * count, bias_)

        nkb = qb + 1
        @pl.loop(0, nkb // 4)
        def _(i):
            blocks(4 * i, 4)

        @pl.when(nkb % 4 >= 2)
        def _():
            blocks(nkb // 4 * 4, 2)

        @pl.when(nkb % 2 == 1)
        def _():
            blocks(qb, 1)

        outs = []
        for h in range(KV_PER_STEP):
            outs.append(acc_ref[h] / l_ref[h])
            o_ref[h] = outs[h]
            lse_ref[h] = m_ref[h] + jnp.log(l_ref[h])
        rows_ref[...] = _blocks_to_rows(outs).astype(BF16)

    kvs = KV_PER_STEP
    qspec = pl.BlockSpec((kvs, None, HD, QROWS), lambda j, i: (j, i, 0, 0))
    return _call(
        body, "attn_fwd", (NKV // kvs, NCH),
        [qspec, pl.BlockSpec((kvs, S, HD), lambda j, i: (j, 0, 0)),
         pl.BlockSpec((kvs, HD, S), lambda j, i: (j, 0, 0)),
         pl.BlockSpec((NBIAS, 128, QROWS), lambda j, i: (0, 0, 0))],
        [qspec, pl.BlockSpec((kvs, None, 1, QROWS), lambda j, i: (j, i, 0, 0)),
         pl.BlockSpec((128, QCOLS * kvs), lambda j, i: (i, j))],
        [jax.ShapeDtypeStruct((NKV, NCH, HD, QROWS), F32), jax.ShapeDtypeStruct((NKV, NCH, 1, QROWS), F32),
         jax.ShapeDtypeStruct((S, 2 * D), BF16)],
        (qt, kh, vt, bias),
        [pltpu.VMEM((kvs, 1, QROWS), F32), pltpu.VMEM((kvs, 1, QROWS), F32), pltpu.VMEM((kvs, HD, QROWS), F32)],
        ("parallel", "parallel"), rider)


def _attn_bwd(qt, kh, kt, vh, dot_, lse, delta, bias, rider=None):
    def body(qt_ref, k_ref, kt_ref, v_ref, dot_ref, lse_ref, dl_ref, b_ref, dq_ref, dk_ref, dv_ref):
        kp = pl.program_id(1)

        @pl.when(kp == 0)
        def _():
            dq_ref[...] = jnp.zeros_like(dq_ref)

        dk_ref[...] = jnp.zeros_like(dk_ref)
        dv_ref[...] = jnp.zeros_like(dv_ref)

        @pl.loop(2 * kp, NCH // 2)
        def _(j):
            for h in range(KV_PER_STEP):
                k, kt_, v = k_ref[h], kt_ref[h], v_ref[h]
                for qb in (2 * j, 2 * j + 1):
                    bias2 = jnp.concatenate([b_ref[jnp.maximum(qb - 4 * kp - t + 1, 0)] for t in range(4)], axis=0)
                    st = _dot(k, qt_ref[h, qb], NN) + bias2
                    pt = jnp.exp(st - lse_ref[h, qb])
                    dst = pt * (_dot(v, dot_ref[h, qb], NN) - dl_ref[h, qb])
                    dq_ref[h, qb] += _dot(kt_, dst, NN)
                    dk_ref[h] += _dot(qt_ref[h, qb], dst, NT)
                    dv_ref[h] += _dot(dot_ref[h, qb], pt, NT)

    kvs = KV_PER_STEP
    tspec = pl.BlockSpec((kvs, NCH, HD, QROWS), lambda j, i: (j, 0, 0, 0))
    kspec = pl.BlockSpec((kvs, 512, HD), lambda j, i: (j, i, 0))
    ktspec = pl.BlockSpec((kvs, HD, 512), lambda j, i: (j, 0, i))
    sspec = pl.BlockSpec((kvs, NCH, 1, QROWS), lambda j, i: (j, 0, 0, 0))
    return _call(
        body, "attn_bwd", (NKV // kvs, NCH // 4),
        [tspec, kspec, ktspec, kspec, tspec, sspec, sspec,
         pl.BlockSpec((NBIAS, 128, QROWS), lambda j, i: (0, 0, 0))],
        [tspec, ktspec, ktspec],
        [jax.ShapeDtypeStruct((NKV, NCH, HD, QROWS), F32),
         jax.ShapeDtypeStruct((NKV, HD, S), F32), jax.ShapeDtypeStruct((NKV, HD, S), F32)],
        (qt, kh, kt, vh, dot_, lse, delta, bias), sem=("parallel", "arbitrary"), rider=rider)


CONV_BLK = 256
CONV_COL0 = 1536 // CONV_BLK


CONV_ROWS = 128


def _conv_fwd(proj, convw, convb):
    trips = S // CONV_ROWS

    def body(u_ref, w_ref, b_ref, o_ref, y_ref):
        @pl.loop(0, trips)
        def _(c):
            t0 = pl.multiple_of(c * CONV_ROWS, CONV_ROWS)
            before = pl.multiple_of(jnp.maximum(t0 - 8, 0), 8)
            ext = jnp.concatenate([jnp.where(c == 0, 0.0, u_ref[pl.ds(before, 8), :]),
                                   u_ref[pl.ds(t0, CONV_ROWS), :]], axis=0)
            y = b_ref[...] + w_ref[CONV_K - 1:CONV_K, :] * ext[8:]
            for j in range(1, CONV_K):
                y = y + w_ref[CONV_K - 1 - j:CONV_K - j, :] * pltpu.roll(ext, j, 0)[8:]
            y_ref[pl.ds(t0, CONV_ROWS), :] = y
            o_ref[pl.ds(t0, CONV_ROWS), :] = y * _sigmoid(y)

    out = pl.BlockSpec((S, CONV_BLK), lambda i: (0, i))
    return pl.pallas_call(
        body, name="conv_fwd", grid=(CONV_C // CONV_BLK,),
        in_specs=[pl.BlockSpec((S, CONV_BLK), lambda i: (0, CONV_COL0 + i)),
                  pl.BlockSpec((CONV_K, CONV_BLK), lambda i: (0, i)),
                  pl.BlockSpec((1, CONV_BLK), lambda i: (0, i))],
        out_specs=[out, out], out_shape=[jax.ShapeDtypeStruct((S, CONV_C), F32)] * 2,
        compiler_params=_cparams("parallel"),
    )(proj, convw, convb)


def _conv_bwd(dact, ypre, proj, convw, dproj):
    trips = S // CONV_ROWS

    def body(da_ref, y_ref, u_ref, w_ref, buf_ref, du_ref, dw_ref, db_ref):
        dw_ref[...] = jnp.zeros_like(dw_ref)
        db_ref[...] = jnp.zeros_like(db_ref)
        r8 = lax.broadcasted_iota(jnp.int32, (8, CONV_BLK), 0)

        def dy_of(rows):
            y = y_ref[rows, :]
            sg = _sigmoid(y)
            return da_ref[rows, :] * (sg * (1.0 + y * (1.0 - sg)))

        @pl.loop(0, trips)
        def _(c):
            t0 = pl.multiple_of(c * CONV_ROWS, CONV_ROWS)
            after = pl.multiple_of(jnp.minimum(t0 + CONV_ROWS, S - 8), 8)
            ext = jnp.concatenate([dy_of(pl.ds(t0, CONV_ROWS)),
                                   jnp.where(c == trips - 1, 0.0, dy_of(pl.ds(after, 8)))], axis=0)
            u = u_ref[pl.ds(t0, CONV_ROWS), :]
            du, dw = None, jnp.zeros((8, CONV_BLK), F32)
            for j in range(CONV_K):
                dyj = (ext if j == 0 else pltpu.roll(ext, CONV_ROWS + 8 - j, 0))[:CONV_ROWS]
                term = w_ref[CONV_K - 1 - j:CONV_K - j, :] * dyj
                du = term if du is None else du + term
                dw = dw + jnp.where(r8 == CONV_K - 1 - j, jnp.sum(dyj * u, axis=0, keepdims=True), 0.0)
            du_ref[pl.ds(t0, CONV_ROWS), :] = du.astype(BF16)
            dw_ref[...] += dw
            db_ref[...] += jnp.sum(ext[:CONV_ROWS], axis=0, keepdims=True)

    return pl.pallas_call(
        body, name="conv_bwd", grid=(CONV_C // CONV_BLK,),
        in_specs=[pl.BlockSpec((S, CONV_BLK), lambda i: (0, i)), pl.BlockSpec((S, CONV_BLK), lambda i: (0, i)),
                  pl.BlockSpec((S, CONV_BLK), lambda i: (0, CONV_COL0 + i)),
                  pl.BlockSpec((CONV_K, CONV_BLK), lambda i: (0, i)), pl.BlockSpec(memory_space=pl.ANY)],
        out_specs=[pl.BlockSpec((S, CONV_BLK), lambda i: (0, CONV_COL0 + i)),
                   pl.BlockSpec((8, CONV_BLK), lambda i: (0, i)), pl.BlockSpec((1, CONV_BLK), lambda i: (0, i))],
        out_shape=[jax.ShapeDtypeStruct(dproj.shape, BF16), jax.ShapeDtypeStruct((8, CONV_C), F32),
                   jax.ShapeDtypeStruct((1, CONV_C), F32)],
        input_output_aliases={4: 0}, compiler_params=_cparams("parallel"),
    )(dact, ypre, proj, convw, dproj)


NPAIR = 8


def _ssd_scalars(dtr_ref, dtb_ref, alog_ref):
    z = dtr_ref[...] + dtb_ref[...]
    dt = jnp.maximum(z, 0.0) + jnp.log(1.0 + jnp.exp(-jnp.abs(z)))
    a = -jnp.exp(alog_ref[...])
    r = lax.broadcasted_iota(jnp.int32, (128, 128), 0)
    c = lax.broadcasted_iota(jnp.int32, (128, 128), 1)
    tri = (r >= c).astype(F32)
    cs = _dot_exact(tri, dt * a)
    return z, dt, a, cs, r, c


def _by_lane(cs, dt):
    head = lax.broadcasted_iota(jnp.int32, (128, SSM_W), 0)
    lane = lax.broadcasted_iota(jnp.int32, (128, SSM_W), 1)
    sel = (head == lane // HD).astype(F32)
    cs_l = _dot_exact(cs, sel, "b")
    last_l = cs_l[127:128, :]
    return sel, jnp.exp(cs_l), jnp.exp(last_l - cs_l), _dot_exact(dt, sel, "b")


def _pair_terms(cs, h1, h2):
    return (cs[:, h1:h1 + 1], cs[:, h2:h2 + 1],
            jnp.exp(cs[127:128, h1:h1 + 1]), jnp.exp(cs[127:128, h2:h2 + 1]))


def _gate_norm(y, zv, w):
    yg = y * (zv * _sigmoid(zv))
    outs, rs = [], []
    for g in range(2):
        blk = yg[:, 512 * g:512 * (g + 1)]
        r = lax.rsqrt(jnp.mean(blk * blk, axis=-1, keepdims=True) + EPS)
        outs.append(blk * r)
        rs.append(r)
    return jnp.concatenate(outs, axis=1), rs, yg


def _ssd_fwd(xbc, proj, dtb, alog, dskip_l, ssmw, mixed):
    def body(x_ref, b_ref, c_ref, dtr_ref, z_ref, dtb_ref, alog_ref, dsk_ref, w_ref, buf_ref,
             y_ref, yn_ref, hp_ref, h_ref):
        @pl.when(pl.program_id(0) == 0)
        def _():
            h_ref[...] = jnp.zeros_like(h_ref)

        _, dt, _, cs, r, c = _ssd_scalars(dtr_ref, dtb_ref, alog_ref)
        cst = cs.T
        causal = r >= c
        lo = c < HD
        _, e_all, dte_all, dt_all = _by_lane(cs, dt)
        hp_ref[...] = h_ref[...]
        for g in range(2):
            bg = b_ref[:, 128 * g:128 * (g + 1)]
            cg = c_ref[:, 128 * g:128 * (g + 1)]
            cb = _dot(cg, bg, NT)
            for j in range(4):
                pj = 4 * g + j
                h1, h2 = 2 * pj, 2 * pj + 1
                sl = slice(128 * pj, 128 * (pj + 1))
                xp = x_ref[:, sl]
                c1, c2, cd1, cd2 = _pair_terms(cs, h1, h2)
                e_l, dte_l = e_all[:, sl], dte_all[:, sl]
                xdt = xp * dt_all[:, sl]
                m1 = cb * jnp.exp(jnp.where(causal, c1 - cst[h1:h1 + 1, :], NEG))
                m2 = cb * jnp.exp(jnp.where(causal, c2 - cst[h2:h2 + 1, :], NEG))
                yd = jnp.where(lo, _dot(m1, xdt, NN), _dot(m2, xdt, NN))
                hp = h_ref[pj]
                yo = _dot(cg, hp, NT) * e_l
                st = _dot(xdt * dte_l, bg, TN)
                h_ref[pj] = hp * jnp.where(r < HD, cd1, cd2) + st
                y_ref[:, sl] = yd + yo + dsk_ref[:, sl] * xp
        yn, _, _ = _gate_norm(y_ref[...], z_ref[...], w_ref[...])
        yn_ref[...] = (yn * w_ref[...]).astype(BF16)

    return pl.pallas_call(
        body, name="ssd_fwd", grid=(NCH,),
        in_specs=[pl.BlockSpec((128, SSM_W), lambda i: (i, 0)),
                  pl.BlockSpec((128, 256), lambda i: (i, 4)), pl.BlockSpec((128, 256), lambda i: (i, 5)),
                  pl.BlockSpec((128, 128), lambda i: (i, COL_DT // 128)),
                  pl.BlockSpec((128, SSM_W), lambda i: (i, 3)),
                  pl.BlockSpec((1, 128), lambda i: (0, 0)), pl.BlockSpec((1, 128), lambda i: (0, 0)),
                  pl.BlockSpec((1, SSM_W), lambda i: (0, 0)), pl.BlockSpec((1, SSM_W), lambda i: (0, 0)),
                  pl.BlockSpec(memory_space=pl.ANY)],
        out_specs=[pl.BlockSpec((128, SSM_W), lambda i: (i, 0)), pl.BlockSpec((128, SSM_W), lambda i: (i, 1)),
                   pl.BlockSpec((None, NPAIR, 128, 128), lambda i: (i, 0, 0, 0))],
        out_shape=[jax.ShapeDtypeStruct((S, SSM_W), F32), jax.ShapeDtypeStruct(mixed.shape, BF16),
                   jax.ShapeDtypeStruct((NCH, NPAIR, 128, 128), F32)],
        scratch_shapes=[pltpu.VMEM((NPAIR, 128, 128), F32)],
        input_output_aliases={9: 1}, compiler_params=_cparams("arbitrary"),
    )(xbc, xbc, xbc, proj, proj, dtb, alog, dskip_l, ssmw, mixed)


def _ssd_bwd(dmixed, y, xbc, proj, hprev, dtb, alog, dskip_l, ssmw, rider=None):
    def body(dyn_ref, y_ref, x_ref, b_ref, c_ref, dtr_ref, z_ref, hp_ref, dtb_ref, alog_ref, dsk_ref, w_ref,
             dxbc_ref, dz_ref, ddt_ref, dw_ref, dsc_ref, g_ref):
        @pl.when(pl.program_id(0) == 0)
        def _():
            g_ref[...] = jnp.zeros_like(g_ref)
            dsc_ref[...] = jnp.zeros_like(dsc_ref)

        z, dt, a, cs, r, c = _ssd_scalars(dtr_ref, dtb_ref, alog_ref)
        cst = cs.T
        causal = r >= c
        lo = c < HD

        yv = y_ref[...]
        zv = z_ref[...]
        wv = w_ref[...]
        ygn, rs, yg = _gate_norm(yv, zv, wv)
        dyn = dyn_ref[...]
        _acc_rows(dw_ref, dyn * ygn)
        dynw = dyn * wv
        parts = []
        for g in range(2):
            sl = slice(512 * g, 512 * (g + 1))
            a_g, n_g = dynw[:, sl], ygn[:, sl]
            parts.append(rs[g] * (a_g - n_g * jnp.mean(a_g * n_g, axis=-1, keepdims=True)))
        dyg = jnp.concatenate(parts, axis=1)
        sz = _sigmoid(zv)
        dz_ref[...] = (dyg * yv * (sz * (1.0 + zv * (1.0 - sz)))).astype(BF16)
        dy_all = dyg * (zv * sz)

        dcs_cols = jnp.zeros((128, 128), F32)
        dcs_rows = jnp.zeros((128, 128), F32)
        sel, e_all, dte_all, dt_all = _by_lane(cs, dt)
        x_all, b_all, c_all, dsk_all = x_ref[...], b_ref[...], c_ref[...], dsk_ref[...]
        hp_all, g_all = hp_ref[...], g_ref[...]
        g_new, dx_parts, db_parts, dc_parts = [], [], [], []
        dyx_parts, ryo_parts, qx_parts, dxx_parts, gh_parts = [], [], [], [], []
        for g in range(2):
            bg = b_all[:, 128 * g:128 * (g + 1)]
            cg = c_all[:, 128 * g:128 * (g + 1)]
            cb = _dot(cg, bg, NT)
            dcb = jnp.zeros((128, 128), F32)
            db_acc = jnp.zeros((128, NST), F32)
            dc_acc = jnp.zeros((128, NST), F32)
            for j in range(4):
                pj = 4 * g + j
                h1, h2 = 2 * pj, 2 * pj + 1
                sl = slice(128 * pj, 128 * (pj + 1))
                xp = x_all[:, sl]
                dyp = dy_all[:, sl]
                c1, c2, cd1, cd2 = _pair_terms(cs, h1, h2)
                e_l, dte_l, dt_l = e_all[:, sl], dte_all[:, sl], dt_all[:, sl]
                xdt = xp * dt_l
                hp = hp_all[pj]
                gp = g_all[pj]
                dyx_parts.append(dyp * xp)
                dzs = dyp * e_l
                dc_acc = dc_acc + _dot(dzs, hp, NN)
                ryo_parts.append(dyp * (_dot(cg, hp, NT) * e_l))
                qm = _dot(bg, gp, NT)
                dxdt = qm * dte_l
                qx_parts.append(qm * xdt)
                db_acc = db_acc + _dot(xdt * dte_l, gp, NN)
                gh_parts.append(gp * hp)
                g_new.append(_dot(dzs, cg, TN) + jnp.where(r < HD, cd1, cd2) * gp)
                for hh, ch, msk in ((h1, c1, lo), (h2, c2, jnp.logical_not(lo))):
                    lm = jnp.exp(jnp.where(causal, ch - cst[hh:hh + 1, :], NEG))
                    mm = cb * lm
                    dm = jnp.where(causal, _dot(jnp.where(msk, dyp, 0.0), xdt, NT), 0.0)
                    w = dm * mm
                    dcs_cols = dcs_cols + jnp.where(c == hh, jnp.sum(w, axis=1, keepdims=True), 0.0)
                    dcs_rows = dcs_rows + jnp.where(r == hh, jnp.sum(w, axis=0, keepdims=True), 0.0)
                    dcb = dcb + dm * lm
                    dxdt = dxdt + jnp.where(msk, _dot(mm, dyp, TN), 0.0)
                dxx_parts.append(dxdt * xp)
                dx_parts.append(dsk_all[:, sl] * dyp + dxdt * dt_l)
            db_parts.append(db_acc + _dot(dcb, cg, TN))
            dc_parts.append(dc_acc + _dot(dcb, bg, NN))
        g_ref[...] = jnp.stack(g_new)
        dxbc_ref[...] = jnp.concatenate(dx_parts + db_parts + dc_parts, axis=1)

        selt = (lax.broadcasted_iota(jnp.int32, (SSM_W, 128), 0) // HD
                == lax.broadcasted_iota(jnp.int32, (SSM_W, 128), 1)).astype(F32)

        def by_head(parts):
            return _dot_exact(jnp.concatenate(parts, axis=1), selt, "b")

        ddt_x = by_head(dxx_parts)
        dd_row = jnp.sum(by_head(dyx_parts), axis=0, keepdims=True)
        t_all = by_head(qx_parts) * jnp.exp(cs[127:128, :] - cs)
        gh = jnp.sum(_dot_exact(sel, jnp.concatenate(gh_parts, axis=0)), axis=1, keepdims=True)
        gh_row = jnp.broadcast_to(gh, (128, 128)).T[0:1, :]
        at_end = jnp.sum(t_all, axis=0, keepdims=True) + gh_row * jnp.exp(cs[127:128, :])
        dcs = by_head(ryo_parts) - t_all + dcs_cols + jnp.where(r == 127, at_end, 0.0) - dcs_rows.T
        dad = _dot_exact((c >= r).astype(F32), dcs)
        ddt = dad * a + ddt_x
        ddtr = jnp.where(c < 16, ddt * _sigmoid(z), 0.0)
        ddt_ref[...] = ddtr.astype(BF16)
        r8 = lax.broadcasted_iota(jnp.int32, (8, 128), 0)
        dsc_ref[...] += (jnp.where(r8 == 0, jnp.sum(ddtr, axis=0, keepdims=True), 0.0)
                         + jnp.where(r8 == 1, jnp.sum(dad * dt, axis=0, keepdims=True) * a, 0.0)
                         + jnp.where(r8 == 2, dd_row, 0.0))

    rev = NCH - 1
    return _call(
        body, "ssd_bwd", (NCH,),
        [pl.BlockSpec((128, SSM_W), lambda i: (rev - i, 0)),
         pl.BlockSpec((128, SSM_W), lambda i: (rev - i, 0)),
         pl.BlockSpec((128, SSM_W), lambda i: (rev - i, 0)),
         pl.BlockSpec((128, 256), lambda i: (rev - i, 4)), pl.BlockSpec((128, 256), lambda i: (rev - i, 5)),
         pl.BlockSpec((128, 128), lambda i: (rev - i, COL_DT // 128)),
         pl.BlockSpec((128, SSM_W), lambda i: (rev - i, 3)),
         pl.BlockSpec((None, NPAIR, 128, 128), lambda i: (rev - i, 0, 0, 0)),
         pl.BlockSpec((1, 128), lambda i: (0, 0)), pl.BlockSpec((1, 128), lambda i: (0, 0)),
         pl.BlockSpec((1, SSM_W), lambda i: (0, 0)), pl.BlockSpec((1, SSM_W), lambda i: (0, 0))],
        [pl.BlockSpec((128, CONV_C), lambda i: (rev - i, 0)),
         pl.BlockSpec((128, SSM_W), lambda i: (rev - i, 3)),
         pl.BlockSpec((128, 128), lambda i: (rev - i, 0)),
         pl.BlockSpec((1, SSM_W), lambda i: (0, 0)), pl.BlockSpec((8, 128), lambda i: (0, 0))],
        [jax.ShapeDtypeStruct((S, CONV_C), F32), jax.ShapeDtypeStruct((S, WIN_PAD), BF16),
         jax.ShapeDtypeStruct((S, 128), BF16), jax.ShapeDtypeStruct((1, SSM_W), F32),
         jax.ShapeDtypeStruct((8, 128), F32)],
        (dmixed, y, xbc, xbc, xbc, proj, proj, hprev, dtb, alog, dskip_l, ssmw),
        [pltpu.VMEM((NPAIR, 128, 128), F32)], ("arbitrary",), rider)


def _cast_stack(name, slot, arrs, tr, tc):
    n = len(arrs)
    rows, cols = arrs[0].shape

    def body(s_ref, *refs):
        for i in range(n):
            refs[n][i] = refs[i][...].astype(BF16)

    return pl.pallas_call(
        body, name=name,
        grid_spec=pltpu.PrefetchScalarGridSpec(
            num_scalar_prefetch=1, grid=(rows // tr, cols // tc),
            in_specs=[pl.BlockSpec((tr, tc), lambda i, j, sr: (i, j))] * n,
            out_specs=pl.BlockSpec((None, n, tr, tc), lambda i, j, sr: (sr[0], 0, i, j))),
        out_shape=jax.ShapeDtypeStruct((NSH, n, rows, cols), BF16),
        compiler_params=_cparams("parallel", "parallel"),
    )(slot, *arrs)


def _pair_sum(name, c_idx, ps, th):
    n = len(ps)
    _, rows, _ = ps[0].shape

    def body(c_ref, *refs):
        mine, whole, out, theirs = refs[:n], refs[n:2 * n], refs[2 * n:3 * n], refs[3 * n:4 * n]
        send, recv = refs[4 * n], refs[4 * n + 1]
        s, i = pl.program_id(0), pl.program_id(1)
        x, y, c, _ = _place()

        def copies(slot):
            return [_rcopy(whole[k].at[slot, :, pl.ds((1 - c) * HALF, HALF)], theirs[k].at[slot],
                           send.at[slot * n + k], recv.at[slot * n + k], (x, y, 1 - c)) for k in range(n)]

        @pl.when((s == 0) & (i == 0))
        def _():
            for slot in range(NSH):
                for cp in copies(slot):
                    cp.start()

        @pl.when(i == 0)
        def _():
            for slot in range(NSH):
                @pl.when(s == slot)
                def _():
                    for cp in copies(slot):
                        cp.wait()

        rows_i = slice(None) if th == rows else pl.ds(pl.multiple_of(i * th, th), th)
        for k in range(n):
            out[k][...] = (mine[k][...].astype(F32) + theirs[k][s, rows_i, :].astype(F32)).astype(BF16)

    spec = pl.BlockSpec((None, th, HALF), lambda s, i, cr: (s, i, 0))
    return pl.pallas_call(
        body, name=name,
        grid_spec=pltpu.PrefetchScalarGridSpec(
            num_scalar_prefetch=1, grid=(NSH, rows // th),
            in_specs=[pl.BlockSpec((None, th, HALF), lambda s, i, cr: (s, i, cr[0]))] * n + _any_specs(n),
            out_specs=[spec] * n,
            scratch_shapes=[pltpu.VMEM((NSH, rows, HALF), BF16)] * n
            + [pltpu.SemaphoreType.DMA((NSH * n,)), pltpu.SemaphoreType.DMA((NSH * n,))]),
        out_shape=[jax.ShapeDtypeStruct((NSH, rows, HALF), BF16)] * n,
        compiler_params=_cparams("arbitrary", "arbitrary"),
    )(c_idx, *ps, *ps)


def _pair_add(name, c_idx, ps, theirs, th):
    n = len(ps)
    _, rows, _ = ps[0].shape

    def body(c_ref, *refs):
        for k in range(n):
            refs[2 * n + k][...] = (refs[k][...].astype(F32) + refs[n + k][...].astype(F32)).astype(BF16)

    spec = pl.BlockSpec((None, th, HALF), lambda s, i, cr: (s, i, 0))
    return pl.pallas_call(
        body, name=name,
        grid_spec=pltpu.PrefetchScalarGridSpec(
            num_scalar_prefetch=1, grid=(NSH, rows // th),
            in_specs=[pl.BlockSpec((None, th, HALF), lambda s, i, cr: (s, i, cr[0]))] * n + [spec] * n,
            out_specs=[spec] * n),
        out_shape=[jax.ShapeDtypeStruct((NSH, rows, HALF), BF16)] * n,
        compiler_params=_cparams("parallel", "parallel"),
    )(c_idx, *ps, *theirs)


def _chip_sum(name, place, cs, ts, th):
    n = len(ts)
    _, rows, _ = ts[0].shape

    def body(p_ref, *refs):
        for i in range(n):
            t = refs[n + i][...].astype(F32)
            refs[2 * n + i][...] = ((refs[i][...].astype(F32) + t[0]) + t[1]) + t[2]

    return pl.pallas_call(
        body, name=name,
        grid_spec=pltpu.PrefetchScalarGridSpec(
            num_scalar_prefetch=1, grid=(rows // th,),
            in_specs=[pl.BlockSpec((None, th, HALF), lambda i, pr: (pr[0], i, 0))] * n
            + [pl.BlockSpec((3, th, HALF), lambda i, pr: (0, i, 0))] * n,
            out_specs=[pl.BlockSpec((th, HALF), lambda i, pr: (i, pr[1]))] * n),
        out_shape=[jax.ShapeDtypeStruct((rows, D), F32)] * n, compiler_params=_cparams("parallel"),
    )(place, *cs, *ts)


def _adamw(name, ws, gs, ms, vs, tr, tc):
    n = len(ws)
    shape = ws[0].shape
    rows, cols, mid = shape[0], shape[-1], shape[1:-1]
    c1 = 1.0 / (1.0 - ADAM_B1 ** ADAM_STEP)
    c2 = 1.0 / (1.0 - ADAM_B2 ** ADAM_STEP)

    def body(*refs):
        for i in range(n):
            w, g, m, v = (refs[k * n + i][...] for k in range(4))
            m2 = ADAM_B1 * m + (1.0 - ADAM_B1) * g
            v2 = ADAM_B2 * v + (1.0 - ADAM_B2) * (g * g)
            refs[4 * n + 4 * i][...] = -ADAM_LR * ((m2 * c1) / (jnp.sqrt(v2 * c2) + ADAM_EPS) + ADAM_WD * w)
            refs[4 * n + 4 * i + 1][...] = m2
            refs[4 * n + 4 * i + 2][...] = v2
            refs[4 * n + 4 * i + 3][...] = g

    spec = pl.BlockSpec((tr,) + mid + (tc,), lambda i, j: (i,) + (0,) * len(mid) + (j,))
    outs = pl.pallas_call(
        body, name=name, grid=(rows // tr, cols // tc), in_specs=[spec] * (4 * n), out_specs=[spec] * (4 * n),
        out_shape=[jax.ShapeDtypeStruct(shape, F32)] * (4 * n),
        compiler_params=_cparams("parallel", "parallel"),
    )(*ws, *gs, *ms, *vs)
    return [tuple(outs[4 * i:4 * i + 4]) for i in range(n)]


def _place():
    x, y, c = lax.axis_index("x"), lax.axis_index("y"), lax.axis_index("c")
    chips = [(1 - x, y), (x, 1 - y), (1 - x, 1 - y)]
    return x, y, c, chips


def _any_specs(n):
    return [pl.BlockSpec(memory_space=pl.ANY)] * n


def _rcopy(src, dst, send_sem, recv_sem, dev):
    return pltpu.make_async_remote_copy(src_ref=src, dst_ref=dst, send_sem=send_sem, recv_sem=recv_sem,
                                        device_id=dev, device_id_type=MESH)


QUARTER = HALF // 2

TO_X, TO_Y, RELAY_X, RELAY_Y, FWD_X, FWD_Y, FWD_D0, FWD_D1 = range(8)


def _gather_rider(bufs, views, at=None):
    n = len(bufs)

    def plan(rout, sems):
        send, recv = sems
        x, y, c, _ = _place()
        me, sx, sy, sd = 2 * x + y, 2 * (1 - x) + y, 2 * x + (1 - y), 2 * (1 - x) + (1 - y)
        nx, ny, sib = (1 - x, y, c), (x, 1 - y, c), (x, y, 1 - c)
        mine, other = c * HALF, (1 - c) * HALF
        out = {TO_X: (me, mine, HALF, nx), TO_Y: (me, mine, HALF, ny),
               FWD_X: (sx, mine, HALF, sib), FWD_Y: (sy, mine, HALF, sib),
               RELAY_X: (sy, mine, QUARTER, nx), RELAY_Y: (sx, mine + QUARTER, QUARTER, ny),
               FWD_D0: (sd, mine, QUARTER, sib), FWD_D1: (sd, mine + QUARTER, QUARTER, sib)}
        inn = {TO_X: (sx, mine, HALF), TO_Y: (sy, mine, HALF),
               FWD_X: (sx, other, HALF), FWD_Y: (sy, other, HALF),
               RELAY_X: (sd, mine, QUARTER), RELAY_Y: (sd, mine + QUARTER, QUARTER),
               FWD_D0: (sd, other, QUARTER), FWD_D1: (sd, other + QUARTER, QUARTER)}

        def copy(kind, b):
            slot, col, ncols, dev = out[kind]
            win = views[b](rout[b], slot, col, ncols)
            return _rcopy(win, win, send.at[kind * n + b], recv.at[kind * n + b], dev)

        def land(kind, b):
            slot, col, ncols = inn[kind]
            win = views[b](rout[b], slot, col, ncols)
            return _rcopy(win, win, send.at[kind * n + b], recv.at[kind * n + b], (x, y, c))

        return copy, land

    first = (TO_X, TO_Y)
    chain = ((TO_X, (FWD_X, RELAY_Y)), (TO_Y, (FWD_Y, RELAY_X)), (RELAY_X, (FWD_D0,)), (RELAY_Y, (FWD_D1,)))
    forwards = (FWD_X, FWD_Y, FWD_D0, FWD_D1)
    sent = first + tuple(k for _, then in chain for k in then)

    def start(rin, rout, sems):
        copy, _ = plan(rout, sems)
        for kind in first:
            for b in range(n):
                copy(kind, b).start()

    def pass_on(rout, sems, links):
        copy, land = plan(rout, sems)
        for landed, then in links:
            for b in range(n):
                land(landed, b).wait_recv()
                for kind in then:
                    copy(kind, b).start()

    def between(rin, rout, sems):
        pass_on(rout, sems, chain[:2])

    def finish(rin, rout, sems):
        pass_on(rout, sems, chain[2:])
        copy, land = plan(rout, sems)
        for kind in forwards:
            for b in range(n):
                land(kind, b).wait_recv()
        for kind in sent:
            for b in range(n):
                copy(kind, b).wait_send()

    return _Rider(list(bufs), [jax.ShapeDtypeStruct(a.shape, a.dtype) for a in bufs], {b: b for b in range(n)},
                  [pltpu.SemaphoreType.DMA((8 * n,))] * 2, start, finish, between, at)


def _small_gather_rider(cw):
    def descs(rin, rout, sems, x, y, c, chips):
        return [_rcopy(rin[0], rout[0].at[2 * x + y], sems[1].at[j], sems[2].at[j], (chip[0], chip[1], c))
                for j, chip in enumerate(chips)]

    def start(rin, rout, sems):
        x, y, c, chips = _place()
        pltpu.make_async_copy(rin[0], rout[0].at[2 * x + y], sems[0].at[0]).start()
        for cp in descs(rin, rout, sems, x, y, c, chips):
            cp.start()

    def finish(rin, rout, sems):
        x, y, c, chips = _place()
        for j, chip in enumerate(chips):
            _rcopy(rin[0], rout[0].at[2 * chip[0] + chip[1]], sems[1].at[j], sems[2].at[j], (x, y, c)).wait_recv()
        for cp in descs(rin, rout, sems, x, y, c, chips):
            cp.wait_send()
        pltpu.make_async_copy(rin[0], rout[0].at[2 * x + y], sems[0].at[0]).wait()

    return _Rider([cw], [jax.ShapeDtypeStruct((NSH,) + cw.shape, cw.dtype)], {},
                  [pltpu.SemaphoreType.DMA((1,)), pltpu.SemaphoreType.DMA((3,)), pltpu.SemaphoreType.DMA((3,))],
                  start, finish)


def _to_sibling_rider(ps):
    n = len(ps)

    def descs(rin, rout, sems):
        x, y, c, _ = _place()
        return [_rcopy(rin[i].at[:, :, pl.ds((1 - c) * HALF, HALF)], rout[i], sems[0].at[i], sems[1].at[i],
                       (x, y, 1 - c)) for i in range(n)]

    def start(rin, rout, sems):
        for cp in descs(rin, rout, sems):
            cp.start()

    def finish(rin, rout, sems):
        for cp in descs(rin, rout, sems):
            cp.wait()

    return _Rider(list(ps), [jax.ShapeDtypeStruct(a.shape[:2] + (HALF,), a.dtype) for a in ps], {},
                  [pltpu.SemaphoreType.DMA((n,))] * 2, start, finish)


def _to_chips_rider(cs):
    n = len(cs)

    def descs(rin, rout, sems):
        x, y, c, chips = _place()
        return [_rcopy(rin[i].at[2 * chip[0] + chip[1]], rout[i].at[j], sems[0].at[j * n + i], sems[1].at[j * n + i],
                       (chip[0], chip[1], c)) for j, chip in enumerate(chips) for i in range(n)]

    def start(rin, rout, sems):
        for cp in descs(rin, rout, sems):
            cp.start()

    def finish(rin, rout, sems):
        for cp in descs(rin, rout, sems):
            cp.wait()

    return _Rider(list(cs), [jax.ShapeDtypeStruct((3,) + a.shape[1:], a.dtype) for a in cs], {},
                  [pltpu.SemaphoreType.DMA((3 * n,))] * 2, start, finish)


def _join_riders(riders):
    counts = [[len(r.operands) for r in riders], [len(r.out_shapes) for r in riders], [len(r.sems) for r in riders]]

    def each(step):
        def run(*refs):
            at = [0, 0, 0]
            for i, r in enumerate(riders):
                parts = [group[at[k]:at[k] + counts[k][i]] for k, group in enumerate(refs)]
                at = [at[k] + counts[k][i] for k in range(3)]
                step(r)(*parts)
        return run

    aliases = {sum(counts[0][:i]) + k: sum(counts[1][:i]) + v
               for i, r in enumerate(riders) for k, v in r.aliases.items()}
    return _Rider([a for r in riders for a in r.operands], [s for r in riders for s in r.out_shapes], aliases,
                  [s for r in riders for s in r.sems], each(lambda r: r.start), each(lambda r: r.finish),
                  each(lambda r: r.between or (lambda *refs: None)))


SMALL_ROWS = 16


def _swap_halves(gs, vec):
    n = len(gs)

    def body(*refs):
        v_ref, dst, o_ref = refs[n], refs[n + 1:2 * n + 1], refs[2 * n + 1]
        buf, send, recv, vsend, vrecv = refs[2 * n + 2:]
        x, y, c, _ = _place()
        cps = []
        for i in range(n):
            mine = dst[i].at[:, pl.ds(c * HALF, HALF)]
            cps.append(_rcopy(mine, mine, send.at[i], recv.at[i], (x, y, 1 - c)))
        for cp in cps:
            cp.start()

        me = 4 * x + 2 * y + c
        buf[me] = v_ref[...]
        vcps = []
        for k in range(1, 8):
            peer = (x ^ (k >> 2), y ^ ((k >> 1) & 1), c ^ (k & 1))
            vcps.append(_rcopy(v_ref, buf.at[me], vsend.at[k - 1], vrecv.at[k - 1], peer))
        for cp in vcps:
            cp.start()
        for k in range(1, 8):
            _rcopy(v_ref, buf.at[me ^ k], vsend.at[k - 1], vrecv.at[k - 1], (x, y, c)).wait_recv()
        for cp in vcps:
            cp.wait_send()
        t = buf[0]
        for d in range(1, 8):
            t = t + buf[d]
        o_ref[...] = t

        for i in range(n):
            other = dst[i].at[:, pl.ds((1 - c) * HALF, HALF)]
            _rcopy(other, other, send.at[i], recv.at[i], (x, y, c)).wait_recv()
        for cp in cps:
            cp.wait_send()

    vmem = pl.BlockSpec(memory_space=pltpu.VMEM)
    res = pl.pallas_call(
        body, name="grads_swap_halves", in_specs=_any_specs(n) + [vmem], out_specs=_any_specs(n) + [vmem],
        out_shape=[jax.ShapeDtypeStruct(g.shape, g.dtype) for g in gs] + [jax.ShapeDtypeStruct((SMALL_ROWS, D), F32)],
        input_output_aliases={i: i for i in range(n)},
        scratch_shapes=[pltpu.VMEM((8, SMALL_ROWS, D), F32)] + [pltpu.SemaphoreType.DMA((n,))] * 2
        + [pltpu.SemaphoreType.DMA((7,))] * 2,
    )(*gs, vec)
    return list(res[:n]), res[n]


def _col_window(ref, slot, col, ncols):
    return ref.at[slot, :, pl.ds(col, ncols)]


def _stack_window(first, count):
    def view(ref, slot, col, ncols):
        return ref.at[slot, pl.ds(first, count), :, pl.ds(col, ncols)]
    return view


def _row_tile(rows):
    for t in range(512, 15, -16):
        if rows % t == 0:
            return t
    return rows


def _same_shape_runs(arrs):
    runs, a = [], 0
    for b in range(1, len(arrs) + 1):
        if b == len(arrs) or arrs[b].shape != arrs[a].shape:
            runs.append((a, b))
            a = b
    return runs


class _Comm:
    def __init__(self):
        x, y, c = lax.axis_index("x"), lax.axis_index("y"), lax.axis_index("c")
        self.c_idx = jnp.reshape(c, (1,)).astype(jnp.int32)
        self.shard = jnp.reshape(2 * x + y, (1,)).astype(jnp.int32)
        self.place = jnp.stack([2 * x + y, c]).astype(jnp.int32)
        self.groups = {}

    @staticmethod
    def gather(*bufs, part=None, at=None):
        views = [_col_window if b.ndim == 3 else _stack_window(*(part or (0, b.shape[1]))) for b in bufs]
        return _gather_rider(list(bufs), views, at)

    def reduce_rider(self, tag, names, ps, theirs=None):
        csums = []
        for a, b in _same_shape_runs(ps):
            name, th = "pair_sum_%s%d" % (tag, a), _row_tile(ps[a].shape[1])
            csums += (_pair_sum(name, self.c_idx, ps[a:b], th) if theirs is None else
                      _pair_add(name, self.c_idx, ps[a:b], theirs[a:b], th))
        self.groups[tag] = [names, csums, None]
        return _to_chips_rider(csums)

    def landed(self, tag, ts):
        self.groups[tag][2] = ts

    def finish(self, small):
        names, csums, ts = [], [], []
        for group_names, group_csums, group_ts in self.groups.values():
            names += group_names
            csums += group_csums
            ts += group_ts
        order = sorted(range(len(names)), key=lambda i: csums[i].shape[1])
        names, csums, ts = ([v[i] for i in order] for v in (names, csums, ts))
        halves = []
        for a, b in _same_shape_runs(csums):
            halves += _chip_sum("chip_sum_%d" % a, self.place, csums[a:b], ts[a:b], _row_tile(csums[a].shape[1]))
        grads, total = _swap_halves(halves, small)
        return dict(zip(names, grads)), total


ROPE_THETA = 10000.0
SMALL_1K = ("ffn1_pre_norm", "ffn1_post_norm", "mix_pre_norm", "ssm_norm", "mix_post_norm",
            "ffn2_pre_norm", "ffn2_post_norm")
SMALL_16 = ("dt_bias", "a_log", "d_skip")
OFF_CONVB = 7 * D
OFF_16 = OFF_CONVB + CONV_C
OFF_CONVW = OFF_16 + 48
OFF_LOSS = OFF_CONVW + CONV_K * CONV_C
SMALL_LEN = SMALL_ROWS * D


def _sds(shape, dtype):
    return jax.ShapeDtypeStruct(shape, dtype)


def _ridden(res, rider):
    return res if rider is not None else (res, None)


def _ffn_down(name, act, w, tail_of, rider=None):
    tail, o_specs, o_shapes = tail_of(TS)
    return _mm(name, [act, w.dn], NN, (S // TS,),
               [pl.BlockSpec((NSH, TS, FS), lambda i: (0, i, 0)),
                pl.BlockSpec((NSH, None, FS, D), lambda i: (0, w.d0, 0, 0))], o_specs, o_shapes, rider, tail)


def _ffn_dw(name, a, b, rider=None):
    return _mm(name, [a, b], TN, (NSH,),
               [pl.BlockSpec((None, S, FS), lambda s: (s, 0, 0)), pl.BlockSpec((S, D), lambda s: (0, 0))],
               pl.BlockSpec((None, FS, D), lambda s: (s, 0, 0)), _sds((NSH, FS, D), BF16), rider)


def _ffn_dn(name, dgate, dup, w, tail_of, rider=None):
    rows = TS // 2
    tail, o_specs, o_shapes = tail_of(rows)
    a2 = pl.BlockSpec((NSH, rows, FS), lambda i: (0, i, 0))
    return _mm(name, [dgate, w.gu, dup, w.gu], NN, (S // rows,),
               [a2, pl.BlockSpec((NSH, None, FS, D), lambda i: (0, w.g0, 0, 0)),
                a2, pl.BlockSpec((NSH, None, FS, D), lambda i: (0, w.g0 + 1, 0, 0))], o_specs, o_shapes, rider, tail)


def _out_proj_dx(dh, wout, ot):
    def body(dh_ref, w_ref, o_ref, dyn_ref, do_ref, dl_ref):
        dm = _dot(dh_ref[...], w_ref[...], NT)
        dyn_ref[...] = dm[:, D:]
        for b in range(TS // 128):
            for j, blk in enumerate(_rows_to_blocks(dm[128 * b:128 * (b + 1), :D])):
                do = blk.astype(BF16)
                do_ref[j, b] = do
                dl_ref[j, b] = jnp.sum(o_ref[j, b] * do.astype(F32), axis=0, keepdims=True)

    blocks = pl.BlockSpec((NKV, TS // 128, HD, QROWS), lambda i: (0, i, 0, 0))
    return pl.pallas_call(
        body, name="out_proj_dx", grid=(S // TS,),
        in_specs=[pl.BlockSpec((TS, D), lambda i: (i, 0)), pl.BlockSpec((2 * D, D), lambda i: (0, 0)), blocks],
        out_specs=[pl.BlockSpec((TS, D), lambda i: (i, 0)), blocks,
                   pl.BlockSpec((NKV, TS // 128, 1, QROWS), lambda i: (0, i, 0, 0))],
        out_shape=[_sds((S, D), F32), _sds((NKV, NCH, HD, QROWS), BF16), _sds((NKV, NCH, 1, QROWS), F32)],
        compiler_params=_cparams("parallel"),
    )(dh, wout, ot)


def _heads(t, n):
    return t.reshape(S, n, HD).transpose(1, 0, 2)


def _pad128(v):
    return jnp.pad(v, ((0, 0), (0, 128 - v.shape[1])))


def _local_step(x, positions, tgt, sp, gu1, d1, f2, wint, wout, convw, comm=None):
    inv_freq = ROPE_THETA ** (-jnp.arange(0, HD, 2, dtype=F32) / HD)
    ang = positions.astype(F32)[:, None] * inv_freq
    ang = jnp.concatenate([ang, ang, ang, ang], axis=-1)
    cos, sin = jnp.cos(ang), jnp.sin(ang)
    dtb, alog = _pad128(sp["dt_bias"]), _pad128(sp["a_log"])
    dskip_l = jnp.repeat(sp["d_skip"], HD, axis=1)
    convb = sp["conv_b"]

    if comm:
        rider = _join_riders([comm.gather(gu1), _small_gather_rider(convw)])
        (n1, d1, f2, wint, wout), (gu1, convw) = _prenorm_casts(
            "prenorm1", x, sp["ffn1_pre_norm"], comm.shard, [(d1, 0), (f2, 0), (wint, 1), (wout, 0)], rider)
        wint, wout = wint.reshape(NSH, WIN_SH, D), wout.reshape(NSH, 2 * D // NSH, D)
        convw = convw.transpose(1, 0, 2).reshape(CONV_K, CONV_C)
    else:
        n1 = _prenorm("prenorm1", x, sp["ffn1_pre_norm"])
    rider = comm.gather(d1) if comm else None
    (fg1, fu1, act1), got = _ridden(_ffn_up("ffn1_up", n1, _FfnW(gu1, 0, d1, 0), rider), rider)
    if comm:
        d1, = got
    w1 = _FfnW(gu1, 0, d1, 0)
    rider = comm.gather(wint) if comm else None
    (h1, x1, n2), got = _ridden(_ffn_down(
        "ffn1_down", act1, w1,
        lambda rows: _tail_postres(rows, x, sp["ffn1_post_norm"], 0.5, sp["mix_pre_norm"]), rider), rider)
    if comm:
        wint, = got
    wint_pad = jnp.pad(wint.reshape(WIN_COLS, D), ((0, WIN_PAD - WIN_COLS), (0, 0)))

    pw = WIN_PAD // 3
    rider = comm.gather(f2, part=(0, 1)) if comm else None
    proj, got = _ridden(_mm(
        "in_proj", [n2, wint_pad], NT, (S // TS, 3),
        [pl.BlockSpec((TS, D), lambda i, j: (i, 0)), pl.BlockSpec((pw, D), lambda i, j: (j, 0))],
        pl.BlockSpec((TS, pw), lambda i, j: (i, j)), _sds((S, WIN_PAD), F32), rider), rider)
    if comm:
        f2, = got
    qt = _rope_q(proj, cos, sin)
    k_rot, v_bf, kt, vt = _rope_kv(proj, cos, sin)
    kh, vh = _heads(k_rot, NKV), _heads(v_bf, NKV)
    bias = _bias_table()
    rider = comm.gather(f2, wout, part=(1, 2), at=13) if comm else None
    (ot, lse, mixed), got = _ridden(_attn_fwd(qt, kh, vt, bias, rider), rider)
    if comm:
        f2, wout = got
    w2 = _FfnW(f2, 0, f2, 2)
    wout = wout.reshape(2 * D, D)
    xbc, conv_y = _conv_fwd(proj, convw, convb)
    y, mixed, hprev = _ssd_fwd(xbc, proj, dtb, alog, dskip_l, sp["ssm_norm"], mixed)
    tail, o_specs, o_shapes = _tail_postres(TS, x1, sp["mix_post_norm"], 1.0, sp["ffn2_pre_norm"])
    h2, x2, n3 = _mm("out_proj", [mixed, wout], NN, (S // TS,),
                     [pl.BlockSpec((TS, 2 * D), lambda i: (i, 0)), pl.BlockSpec((2 * D, D), lambda i: (0, 0))],
                     o_specs, o_shapes, None, tail)

    fg2, fu2, act2 = _ffn_up("ffn2_up", n3, w2)
    dy, dh3, dp3, loss = _ffn_down(
        "ffn2_down", act2, w2, lambda rows: _tail_final(rows, x2, sp["ffn2_post_norm"], tgt, 0.5))

    dgate2, dup2 = _ffn_dact("ffn2_dact", dh3, w2, fg2, fu2)
    dws2 = [_ffn_dw("ffn2_dwg", dgate2, n3), _ffn_dw("ffn2_dwu", dup2, n3), _ffn_dw("ffn2_dwd", act2, dh3)]
    dx2, dh2, dg3, dp2 = _ffn_dn(
        "ffn2_dn", dgate2, dup2, w2,
        lambda rows: _tail_mid_bwd(rows, dy, x2, sp["ffn2_pre_norm"], h2, sp["mix_post_norm"], 1.0))

    dyn, dot_, delta = _out_proj_dx(dh2, wout, ot)
    dwout = _mm("out_proj_dw", [mixed, dh2], TN, (2,),
                [pl.BlockSpec((S, D), lambda m: (0, m)), pl.BlockSpec((S, D), lambda m: (0, 0))],
                pl.BlockSpec((D, D), lambda m: (m, 0)), _sds((2 * D, D), BF16))
    dwout = dwout.reshape(NSH, 2 * D // NSH, D)

    def riding(tag, names, ps, call, theirs=None):
        rider = comm.reduce_rider(tag, names, ps, theirs) if comm else None
        res, got = _ridden(call(rider), rider)
        if comm:
            comm.landed(tag, got)
        return res

    rider = _to_sibling_rider(dws2 + [dwout]) if comm else None
    (dxbc, dproj, ddt, dssm, dsc), theirs = _ridden(
        _ssd_bwd(dyn, y, xbc, proj, hprev, dtb, alog, dskip_l, sp["ssm_norm"], rider), rider)
    dproj, dcw8, dcb = _conv_bwd(dxbc, conv_y, proj, convw, dproj)
    dqt, dkh, dvh = riding("a", BIG[3:6] + ("w_out",), dws2 + [dwout], lambda rider: _attn_bwd(
        qt, kh, kt, vh, dot_, lse, delta, bias, rider), theirs)
    dproj = _rope_dq(dqt, cos, sin, dproj)
    dproj = _rope_dkv(dkh, dvh, cos, sin, dproj)
    dproj = lax.dynamic_update_slice(dproj, ddt, (0, COL_DT))
    dwint = _mm("in_proj_dw", [dproj, n2], TN, (3,),
                [pl.BlockSpec((S, pw), lambda j: (0, j)), pl.BlockSpec((S, D), lambda j: (0, 0))],
                pl.BlockSpec((pw, D), lambda j: (j, 0)), _sds((WIN_PAD, D), BF16))
    dwint = dwint[:WIN_COLS].reshape(NSH, WIN_SH, D)

    tail, o_specs, o_shapes = _tail_mid_bwd(TS, dx2, x1, sp["mix_pre_norm"], h1, sp["ffn1_post_norm"], 0.5)
    dx1, dh1, dg2, dp1 = riding("b", ("w_in",), [dwint], lambda rider: _mm(
        "in_proj_dx", [dproj, wint_pad], NN, (S // TS,),
        [pl.BlockSpec((TS, WIN_PAD), lambda i: (i, 0)), pl.BlockSpec((WIN_PAD, D), lambda i: (0, 0))],
        o_specs, o_shapes, rider, tail))

    dwd1 = _ffn_dw("ffn1_dwd", act1, dh1)
    dgate1, dup1 = riding("d", BIG[2:3], [dwd1], lambda rider: _ffn_dact("ffn1_dact", dh1, w1, fg1, fu1, rider))
    dwg1, dwu1 = _ffn_dw("ffn1_dwg", dgate1, n1), _ffn_dw("ffn1_dwu", dup1, n1)
    grad_x, dg1 = riding("g", BIG[0:2], [dwg1, dwu1], lambda rider: _ffn_dn(
        "ffn1_dn", dgate1, dup1, w1, lambda rows: _tail_first_bwd(rows, dx1, x, sp["ffn1_pre_norm"]), rider))
    dws1 = [dwg1, dwu1, dwd1]

    small = jnp.concatenate([
        dg1[0], dp1[0], dg2[0], dssm[0], dp2[0], dg3[0], dp3[0], dcb[0],
        dsc[0, :16], dsc[1, :16], dsc[2, :16], dcw8[:CONV_K].reshape(-1), loss[0, :1]])
    small = jnp.pad(small, (0, SMALL_LEN - small.shape[0])).reshape(SMALL_ROWS, D)
    if comm is None:
        return grad_x, dws1 + dws2 + [dwint, dwout], small
    return (grad_x,) + comm.finish(small)


WEIGHTS = ("ffn1_pre_norm", "ffn1_w_gate", "ffn1_w_up", "ffn1_w_down", "ffn1_post_norm", "mix_pre_norm", "w_in",
           "conv_w", "conv_b", "dt_bias", "a_log", "d_skip", "ssm_norm", "w_out", "mix_post_norm", "ffn2_pre_norm",
           "ffn2_w_gate", "ffn2_w_up", "ffn2_w_down", "ffn2_post_norm")
BIG = ("ffn1_w_gate", "ffn1_w_up", "ffn1_w_down", "ffn2_w_gate", "ffn2_w_up", "ffn2_w_down", "w_in", "w_out")
TRANSPOSED = ("ffn1_w_gate", "ffn1_w_up", "ffn2_w_gate", "ffn2_w_up", "w_in")
SMALL_ORDER = SMALL_1K + ("conv_b",) + SMALL_16
CONVW_SH = CONV_C // NSH


def _shard2d(t, name):
    return t[0].T if name in TRANSPOSED else t[0]


def _unshard2d(t, name):
    return (t.T if name in TRANSPOSED else t)[None]


def _rows3d(t):
    return t.transpose(2, 0, 1)


def _pack_small(d, prefix, shard_of_convw):
    flat = jnp.concatenate([d[prefix + n][0] for n in SMALL_ORDER] + [shard_of_convw.reshape(-1)])
    return jnp.pad(flat, (0, SMALL_LEN - flat.shape[0])).reshape(SMALL_ROWS, D)


def _unpack_small(block, like):
    flat = block.reshape(-1)
    out, off = {}, 0
    for n in SMALL_ORDER:
        size = like[n].shape[1]
        out[n] = flat[off:off + size].reshape(1, size)
        off += size
    out["conv_w"] = flat[off:off + CONV_K * CONVW_SH].reshape(1, CONV_K, CONVW_SH)
    return out


def kernel(x, positions, ffn1_pre_norm, ffn1_w_gate, ffn1_w_up, ffn1_w_down, ffn1_post_norm, mix_pre_norm, w_in, conv_w, conv_b, dt_bias, a_log, d_skip, ssm_norm, w_out, mix_post_norm, ffn2_pre_norm, ffn2_w_gate, ffn2_w_up, ffn2_w_down, ffn2_post_norm, loss_target, m_ffn1_pre_norm, m_ffn1_w_gate, m_ffn1_w_up, m_ffn1_w_down, m_ffn1_post_norm, m_mix_pre_norm, m_w_in, m_conv_w, m_conv_b, m_dt_bias, m_a_log, m_d_skip, m_ssm_norm, m_w_out, m_mix_post_norm, m_ffn2_pre_norm, m_ffn2_w_gate, m_ffn2_w_up, m_ffn2_w_down, m_ffn2_post_norm, v_ffn1_pre_norm, v_ffn1_w_gate, v_ffn1_w_up, v_ffn1_w_down, v_ffn1_post_norm, v_mix_pre_norm, v_w_in, v_conv_w, v_conv_b, v_dt_bias, v_a_log, v_d_skip, v_ssm_norm, v_w_out, v_mix_post_norm, v_ffn2_pre_norm, v_ffn2_w_gate, v_ffn2_w_up, v_ffn2_w_down, v_ffn2_post_norm):
    given = dict(locals())
    xi, yi = lax.axis_index("x"), lax.axis_index("y")

    comm = _Comm()
    big = {p + n: _shard2d(given[p + n], n) for n in BIG for p in ("", "m_", "v_")}
    gu1 = _cast_stack("cast_ffn1_gate_up", comm.shard, [big[n] for n in BIG[0:2]], 176, D)

    sp = {n: given[n] for n in SMALL_ORDER}
    grad_x, big_grads, small = _local_step(
        x[0], positions[0], loss_target[0], sp, gu1, [big[BIG[2]]], [big[n] for n in BIG[3:6]], [big["w_in"]],
        [big["w_out"]], conv_w[0], comm)

    tot = small.reshape(-1)
    loss = tot[OFF_LOSS]
    small_grads, off = {}, 0
    for n in SMALL_ORDER:
        size = given[n].shape[1]
        small_grads[n] = tot[off:off + size].reshape(1, size)
        off += size
    dconvw = tot[OFF_CONVW:OFF_CONVW + CONV_K * CONV_C].reshape(CONV_K, NSH, CONVW_SH)
    dconvw = lax.dynamic_index_in_dim(dconvw, 2 * xi + yi, axis=1, keepdims=False)
    small_grads["conv_w"] = dconvw.reshape(1, CONV_K, CONVW_SH)

    upd = {}
    for names, tr in ((BIG[0:3], 176), (BIG[3:6], 176), (BIG[7:8], 256)):
        res = _adamw("adamw_" + names[0], [big[n] for n in names], [big_grads[n] for n in names],
                     [big["m_" + n] for n in names], [big["v_" + n] for n in names], tr, D)
        for n, r in zip(names, res):
            upd[n] = tuple(_unshard2d(t, n) for t in r)
    g_win = big_grads["w_in"].reshape(WIN_SH, 1, D)
    res, = _adamw("adamw_w_in", [_rows3d(w_in)], [g_win], [_rows3d(m_w_in)], [_rows3d(v_w_in)], WIN_SH // 4, D)
    upd["w_in"] = tuple(t.transpose(1, 2, 0) for t in res)
    (dl, m2, v2, _), = _adamw(
        "adamw_small", [_pack_small(given, "", conv_w[0])], [_pack_small(small_grads, "", dconvw)],
        [_pack_small(given, "m_", m_conv_w[0])], [_pack_small(given, "v_", v_conv_w[0])], SMALL_ROWS, D)
    dl, m2, v2 = (_unpack_small(t, given) for t in (dl, m2, v2))
    for n in SMALL_ORDER + ("conv_w",):
        upd[n] = (dl[n], m2[n], v2[n], small_grads[n])

    return (loss, grad_x[None], *[upd[n][3] for n in WEIGHTS], *[upd[n][0] for n in WEIGHTS],
            *[upd[n][1] for n in WEIGHTS], *[upd[n][2] for n in WEIGHTS])
```

```python
import functools
import typing

import jax
import jax.numpy as jnp
from jax import lax
from jax.experimental import pallas as pl
from jax.experimental.pallas import tpu as pltpu

F32 = jnp.float32
BF16 = jnp.bfloat16

S = 2048
D = 1024
FF = 2816
NSH = 4
FS = FF // NSH
HALF = D // 2
HD = 64
NKV = 4
NQ_PER_KV = 4
KVW = NKV * HD
QCOLS = NQ_PER_KV * HD
CONV_C = 1536
CONV_K = 4
SSM_W = 1024
NST = 128
NCH = S // 128
WIN_COLS = 4112
WIN_SH = WIN_COLS // NSH
W_IN_FIRST = 768
WIN_PAD = 4224
COL_DT = 4096
EPS = 1e-6
NEG = -1e30

ADAM_LR = 0.001
ADAM_B1 = 0.9
ADAM_B2 = 0.999
ADAM_EPS = 1e-08
ADAM_WD = 0.01
ADAM_STEP = 10

VMEM_LIMIT = 56 * 1024 * 1024
TS = 512
TR = 256

NN = (((1,), (0,)), ((), ()))
NT = (((1,), (1,)), ((), ()))
TN = (((0,), (0,)), ((), ()))
MESH = pl.DeviceIdType.MESH


def _cparams(*sem):
    return pltpu.CompilerParams(dimension_semantics=sem, vmem_limit_bytes=VMEM_LIMIT)


def _dot(a, b, dims):
    return lax.dot_general(a.astype(BF16), b.astype(BF16), dims, preferred_element_type=F32)


def _bf16_pieces(v):
    hi = v.astype(BF16)
    rest = v - hi.astype(F32)
    mid = rest.astype(BF16)
    return hi, mid, (rest - mid.astype(F32)).astype(BF16)


def _dot_exact(a, b, ones="a"):
    if ones == "a":
        sel = a.astype(BF16)
        parts = [lax.dot_general(sel, p, NN, preferred_element_type=F32) for p in _bf16_pieces(b)]
    else:
        sel = b.astype(BF16)
        parts = [lax.dot_general(p, sel, NN, preferred_element_type=F32) for p in _bf16_pieces(a)]
    return (parts[2] + parts[1]) + parts[0]


def _sigmoid(v):
    return 1.0 / (1.0 + jnp.exp(-v))


class _Rider(typing.NamedTuple):
    operands: list
    out_shapes: list
    aliases: dict
    sems: list
    start: typing.Callable
    finish: typing.Callable
    between: typing.Callable = None
    at: int = None


def _call(body, name, grid, in_specs, out_specs, out_shape, operands, scratch=(), sem=(), rider=None, prefetch=0):
    multi = isinstance(out_shape, (list, tuple))

    def launch(kernel, in_specs, out_specs, out_shape, scratch, aliases, sem, args):
        if prefetch:
            how = dict(grid_spec=pltpu.PrefetchScalarGridSpec(
                num_scalar_prefetch=prefetch, grid=grid, in_specs=in_specs, out_specs=out_specs,
                scratch_shapes=scratch))
        else:
            how = dict(grid=grid, in_specs=in_specs, out_specs=out_specs, scratch_shapes=scratch)
        return pl.pallas_call(kernel, name=name, out_shape=out_shape, input_output_aliases=aliases,
                              compiler_params=_cparams(*sem), **how)(*args)

    if rider is None:
        return launch(body, in_specs, out_specs, out_shape, list(scratch), {}, sem, operands)
    outs = list(out_shape) if multi else [out_shape]
    ospecs = list(out_specs) if multi else [out_specs]
    n_in, n_out, n_scr = len(operands) - prefetch, len(outs), len(scratch)
    ri, ro = len(rider.operands), len(rider.out_shapes)

    def wrapped(*refs):
        scalars, refs = refs[:prefetch], refs[prefetch:]
        o0 = n_in + ri
        s0 = o0 + n_out + ro
        rin, rout, rsem = refs[n_in:o0], refs[o0 + n_out:s0], refs[s0 + n_scr:]
        ids = [pl.program_id(a) for a in range(len(grid))]
        first = functools.reduce(jnp.logical_and, [i == 0 for i in ids])
        last = functools.reduce(jnp.logical_and, [i == g - 1 for i, g in zip(ids, grid)])

        @pl.when(first)
        def _():
            rider.start(rin, rout, rsem)

        if rider.between is not None:
            steps = functools.reduce(lambda a, b: a * b, grid)
            step = functools.reduce(lambda a, ig: a * ig[1] + ig[0], zip(ids, grid), 0)

            @pl.when(step == (2 * steps // 3 if rider.at is None else rider.at))
            def _():
                rider.between(rin, rout, rsem)

        body(*scalars, *refs[:n_in], *refs[o0:o0 + n_out], *refs[s0:s0 + n_scr])

        @pl.when(last)
        def _():
            rider.finish(rin, rout, rsem)

    hbm = pl.BlockSpec(memory_space=pl.ANY)
    res = launch(wrapped, list(in_specs) + [hbm] * ri, ospecs + [hbm] * ro, outs + list(rider.out_shapes),
                 list(scratch) + list(rider.sems),
                 {prefetch + n_in + k: n_out + v for k, v in rider.aliases.items()},
                 ("arbitrary",) * len(grid), (*operands, *rider.operands))
    main = list(res[:n_out])
    return (main if multi else main[0]), list(res[n_out:])


class _Tail(typing.NamedTuple):
    fn: typing.Callable
    operands: list
    in_specs: list


def _mm(name, operands, dims, grid, in_specs, o_spec, out_shape, rider=None, tail=None):
    npairs = len(operands) // 2
    extra = [] if tail is None else list(tail.operands)
    nin = 2 * npairs + len(extra)

    def body(*refs):
        t = None
        for i in range(npairs):
            a, b = refs[2 * i], refs[2 * i + 1]
            parts = [(a[s], b[s]) for s in range(a.shape[0])] if len(a.shape) == 3 else [(a[...], b[...])]
            for pa, pb in parts:
                d = _dot(pa, pb, dims)
                t = d if t is None else t + d
        if tail is None:
            refs[nin][...] = t.astype(refs[nin].dtype)
        else:
            tail.fn(t, refs[2 * npairs:nin], refs[nin:])

    sem = ("parallel" if tail is None else "arbitrary",) * len(grid)
    specs = list(in_specs) + ([] if tail is None else list(tail.in_specs))
    return _call(body, name, grid, specs, o_spec, out_shape, list(operands) + extra, (), sem, rider)


class _FfnW(typing.NamedTuple):
    gu: jax.Array
    g0: int
    dn: jax.Array
    d0: int


def _ffn_up(name, n, w, rider=None):
    def body(n_ref, wg_ref, wu_ref, fg_ref, fu_ref, a_ref):
        nb = n_ref[...]
        g = _dot(nb, wg_ref[...], NT)
        u = _dot(nb, wu_ref[...], NT)
        sg = _sigmoid(g)
        silu = g * sg
        fg_ref[...] = (u * (sg * (1.0 + g * (1.0 - sg)))).astype(BF16)
        fu_ref[...] = silu.astype(BF16)
        a_ref[...] = (silu * u).astype(BF16)

    out = jax.ShapeDtypeStruct((NSH, S, FS), BF16)
    ospec = pl.BlockSpec((None, TS, FS), lambda s, i: (s, i, 0))
    return _call(
        body, name, (NSH, S // TS),
        [pl.BlockSpec((TS, D), lambda s, i: (i, 0)),
         pl.BlockSpec((None, None, FS, D), lambda s, i: (s, w.g0, 0, 0)),
         pl.BlockSpec((None, None, FS, D), lambda s, i: (s, w.g0 + 1, 0, 0))],
        [ospec, ospec, ospec], [out, out, out], (n, w.gu, w.gu), sem=("parallel", "parallel"), rider=rider)


def _ffn_dact(name, dh, w, fgate, fup, rider=None):
    def body(dh_ref, wd_ref, fg_ref, fu_ref, dg_ref, du_ref):
        da = _dot(dh_ref[...], wd_ref[...], NT)
        dg_ref[...] = (da * fg_ref[...].astype(F32)).astype(BF16)
        du_ref[...] = (da * fu_ref[...].astype(F32)).astype(BF16)

    out = jax.ShapeDtypeStruct((NSH, S, FS), BF16)
    aspec = pl.BlockSpec((None, TS, FS), lambda s, i: (s, i, 0))
    return _call(
        body, name, (NSH, S // TS),
        [pl.BlockSpec((TS, D), lambda s, i: (i, 0)),
         pl.BlockSpec((None, None, FS, D), lambda s, i: (s, w.d0, 0, 0)), aspec, aspec],
        [aspec, aspec], [out, out], (dh, w.dn, fgate, fup), sem=("parallel", "parallel"), rider=rider)


def _rstd(v):
    return lax.rsqrt(jnp.mean(v * v, axis=-1, keepdims=True) + EPS)


def _row_spec():
    return pl.BlockSpec((TR, D), lambda i: (i, 0))


def _vec_spec():
    return pl.BlockSpec((1, D), lambda i: (0, 0))


def _acc_rows(ref, v):
    @pl.when(pl.program_id(0) == 0)
    def _():
        ref[...] = jnp.zeros_like(ref)
    ref[...] += jnp.sum(v, axis=0, keepdims=True)


def _prenorm(name, x, g):
    def body(x_ref, g_ref, n_ref):
        xv = x_ref[...]
        n_ref[...] = (xv * _rstd(xv) * g_ref[...]).astype(BF16)

    return pl.pallas_call(
        body, name=name, grid=(S // TR,), in_specs=[_row_spec(), _vec_spec()], out_specs=_row_spec(),
        out_shape=jax.ShapeDtypeStruct((S, D), BF16), compiler_params=_cparams("parallel"),
    )(x, g)


ENTRY_STEPS = 4


def _prenorm_casts(name, x, g, slot, groups, rider):
    def body(s_ref, x_ref, g_ref, *refs):
        ins, outs = refs[:len(refs) - len(groups) - 1], refs[len(refs) - len(groups) - 1:]
        xv = x_ref[...]
        outs[0][...] = (xv * _rstd(xv) * g_ref[...]).astype(BF16)
        at = 0
        for (arrs, _), out in zip(groups, outs[1:]):
            for k in range(len(arrs)):
                out[k] = ins[at + k][...].astype(BF16)
            at += len(arrs)

    rows = pl.BlockSpec((S // ENTRY_STEPS, D), lambda i, sr: (i, 0))
    in_specs, out_specs, out_shapes = [rows, pl.BlockSpec((1, D), lambda i, sr: (0, 0))], [rows], [_rows_bf16()]
    for arrs, axis in groups:
        r, c = arrs[0].shape
        if axis == 0:
            blk, at, at_out = (r // ENTRY_STEPS, c), (lambda i, sr: (i, 0)), (lambda i, sr: (sr[0], 0, i, 0))
        else:
            blk, at, at_out = (r, c // ENTRY_STEPS), (lambda i, sr: (0, i)), (lambda i, sr: (sr[0], 0, 0, i))
        in_specs += [pl.BlockSpec(blk, at)] * len(arrs)
        out_specs.append(pl.BlockSpec((None, len(arrs)) + blk, at_out))
        out_shapes.append(jax.ShapeDtypeStruct((NSH, len(arrs), r, c), BF16))
    return _call(body, name, (ENTRY_STEPS,), in_specs, out_specs, out_shapes,
                 [slot, x, g] + [a for arrs, _ in groups for a in arrs], rider=rider, prefetch=1)


def _rows_spec(rows):
    return pl.BlockSpec((rows, D), lambda i: (i, 0))


def _rows_f32():
    return jax.ShapeDtypeStruct((S, D), F32)


def _rows_bf16():
    return jax.ShapeDtypeStruct((S, D), BF16)


def _vec_f32():
    return jax.ShapeDtypeStruct((1, D), F32)


def _tail_postres(rows, x, p, alpha, gnext):
    def fn(h, ins, outs):
        x_ref, p_ref, g_ref = ins
        h_ref, xo_ref, n_ref = outs
        h_ref[...] = h
        xo = x_ref[...] + alpha * (h * _rstd(h) * p_ref[...])
        xo_ref[...] = xo
        n_ref[...] = (xo * _rstd(xo) * g_ref[...]).astype(BF16)

    rs = _rows_spec(rows)
    return (_Tail(fn, [x, p, gnext], [rs, _vec_spec(), _vec_spec()]), [rs, rs, rs],
            [_rows_f32(), _rows_f32(), _rows_bf16()])


def _tail_final(rows, x, p, tgt, alpha):
    def fn(h, ins, outs):
        x_ref, p_ref, t_ref = ins
        dy_ref, dh_ref, dp_ref, loss_ref = outs
        r = _rstd(h)
        hn = h * r
        pv = p_ref[...]
        e = x_ref[...] + alpha * (hn * pv) - t_ref[...]
        dy = e * (1.0 / D)
        dy_ref[...] = dy
        du = alpha * dy * pv
        dh_ref[...] = (r * (du - hn * jnp.mean(du * hn, axis=-1, keepdims=True))).astype(BF16)
        _acc_rows(dp_ref, alpha * dy * hn)
        part = 0.5 * jnp.sum(jnp.mean(e * e, axis=-1, keepdims=True), axis=0, keepdims=True)
        _acc_rows(loss_ref, jnp.broadcast_to(part, (1, 128)))

    rs = _rows_spec(rows)
    return (_Tail(fn, [x, p, tgt], [rs, _vec_spec(), rs]),
            [rs, rs, _vec_spec(), pl.BlockSpec((1, 128), lambda i: (0, 0))],
            [_rows_f32(), _rows_bf16(), _vec_f32(), jax.ShapeDtypeStruct((1, 128), F32)])


def _norm_bwd(dn, xv, g_ref, dg_ref):
    r = _rstd(xv)
    xn = xv * r
    dng = dn * g_ref[...]
    _acc_rows(dg_ref, dn * xn)
    return r * (dng - xn * jnp.mean(dng * xn, axis=-1, keepdims=True))


def _tail_mid_bwd(rows, dres, x, g, h, p, alpha):
    def fn(dn, ins, outs):
        dr_ref, x_ref, g_ref, h_ref, p_ref = ins
        dx_ref, dh_ref, dg_ref, dp_ref = outs
        dx = dr_ref[...] + _norm_bwd(dn, x_ref[...], g_ref, dg_ref)
        dx_ref[...] = dx
        hv = h_ref[...]
        r = _rstd(hv)
        hn = hv * r
        du = alpha * dx * p_ref[...]
        dh_ref[...] = (r * (du - hn * jnp.mean(du * hn, axis=-1, keepdims=True))).astype(BF16)
        _acc_rows(dp_ref, alpha * dx * hn)

    rs = _rows_spec(rows)
    return (_Tail(fn, [dres, x, g, h, p], [rs, rs, _vec_spec(), rs, _vec_spec()]),
            [rs, rs, _vec_spec(), _vec_spec()], [_rows_f32(), _rows_bf16(), _vec_f32(), _vec_f32()])


def _tail_first_bwd(rows, dres, x, g):
    def fn(dn, ins, outs):
        dr_ref, x_ref, g_ref = ins
        dx_ref, dg_ref = outs
        dx_ref[...] = dr_ref[...] + _norm_bwd(dn, x_ref[...], g_ref, dg_ref)

    rs = _rows_spec(rows)
    return (_Tail(fn, [dres, x, g], [rs, rs, _vec_spec()]), [rs, _vec_spec()], [_rows_f32(), _vec_f32()])


def _rotate(t, c128, s128, sign, scale):
    width = t.shape[1]
    c = jnp.tile(c128, (1, width // 128))
    sn = jnp.tile(s128, (1, width // 128))
    lane = lax.broadcasted_iota(jnp.int32, t.shape, 1) & (HD - 1)
    rot = jnp.where(lane < HD // 2, -pltpu.roll(t, width - HD // 2, 1), pltpu.roll(t, HD // 2, 1))
    return (t * c + sign * (rot * sn)) * scale


def _rows_to_blocks(y):
    out = []
    for j in range(NKV):
        yt = y[:, QCOLS * j:QCOLS * (j + 1)].T
        out.append(jnp.concatenate([yt[HD * g:HD * (g + 1)] for g in range(NQ_PER_KV)], axis=1))
    return out


def _blocks_to_rows(blocks):
    cols = []
    for b in blocks:
        stacked = jnp.concatenate([b[:, 128 * g:128 * (g + 1)] for g in range(NQ_PER_KV)], axis=0)
        cols.append(stacked.T)
    return jnp.concatenate(cols, axis=1)


def _rope_q(proj, cos, sin):
    def body(t_ref, c_ref, s_ref, o_ref):
        y = _rotate(t_ref[...], c_ref[...], s_ref[...], 1.0, HD ** -0.5)
        for j, blk in enumerate(_rows_to_blocks(y)):
            o_ref[j] = blk.astype(BF16)

    return pl.pallas_call(
        body, name="rope_q", grid=(NCH,),
        in_specs=[pl.BlockSpec((128, D), lambda i: (i, 0)),
                  pl.BlockSpec((128, 128), lambda i: (i, 0)), pl.BlockSpec((128, 128), lambda i: (i, 0))],
        out_specs=pl.BlockSpec((NKV, None, HD, QROWS), lambda i: (0, i, 0, 0)),
        out_shape=jax.ShapeDtypeStruct((NKV, NCH, HD, QROWS), BF16), compiler_params=_cparams("parallel"),
    )(proj, cos, sin)


def _rope_dq(dqt, cos, sin, dproj):
    def body(t_ref, c_ref, s_ref, buf_ref, o_ref):
        t = _blocks_to_rows([t_ref[j] for j in range(NKV)])
        o_ref[...] = _rotate(t, c_ref[...], s_ref[...], -1.0, HD ** -0.5).astype(BF16)

    return pl.pallas_call(
        body, name="rope_dq", grid=(NCH,),
        in_specs=[pl.BlockSpec((NKV, None, HD, QROWS), lambda i: (0, i, 0, 0)),
                  pl.BlockSpec((128, 128), lambda i: (i, 0)), pl.BlockSpec((128, 128), lambda i: (i, 0)),
                  pl.BlockSpec(memory_space=pl.ANY)],
        out_specs=pl.BlockSpec((128, D), lambda i: (i, 0)),
        out_shape=jax.ShapeDtypeStruct(dproj.shape, BF16), input_output_aliases={3: 0},
        compiler_params=_cparams("parallel"),
    )(dqt, cos, sin, dproj)


def _rope_dkv(dkt, dvt, cos, sin, dproj):
    def body(k_ref, v_ref, c_ref, s_ref, buf_ref, o_ref):
        dk = jnp.concatenate([k_ref[j] for j in range(NKV)], axis=0).T
        dv = jnp.concatenate([v_ref[j] for j in range(NKV)], axis=0).T
        dk = _rotate(dk, c_ref[...], s_ref[...], -1.0, 1.0)
        o_ref[...] = jnp.concatenate([dk, dv], axis=1).astype(BF16)

    tspec = pl.BlockSpec((NKV, HD, 128), lambda i: (0, 0, i))
    return pl.pallas_call(
        body, name="rope_dkv", grid=(NCH,),
        in_specs=[tspec, tspec, pl.BlockSpec((128, 128), lambda i: (i, 0)), pl.BlockSpec((128, 128), lambda i: (i, 0)),
                  pl.BlockSpec(memory_space=pl.ANY)],
        out_specs=pl.BlockSpec((128, 2 * KVW), lambda i: (i, D // (2 * KVW))),
        out_shape=jax.ShapeDtypeStruct(dproj.shape, BF16), input_output_aliases={4: 0},
        compiler_params=_cparams("parallel"),
    )(dkt, dvt, cos, sin, dproj)


def _rope_kv(proj, cos, sin):
    def body(t_ref, c_ref, s_ref, k_ref, v_ref, kt_ref, vt_ref):
        t = t_ref[...]
        k = _rotate(t[:, :KVW], c_ref[...], s_ref[...], 1.0, 1.0).astype(BF16)
        v = t[:, KVW:].astype(BF16)
        k_ref[...] = k
        v_ref[...] = v
        kt, vt = k.astype(F32).T, v.astype(F32).T
        for j in range(NKV):
            kt_ref[j] = kt[HD * j:HD * (j + 1)].astype(BF16)
            vt_ref[j] = vt[HD * j:HD * (j + 1)].astype(BF16)

    rows = pl.BlockSpec((128, KVW), lambda i: (i, 0))
    tspec = pl.BlockSpec((NKV, HD, 128), lambda i: (0, 0, i))
    return pl.pallas_call(
        body, name="rope_kv", grid=(NCH,),
        in_specs=[pl.BlockSpec((128, 2 * KVW), lambda i: (i, D // (2 * KVW))),
                  pl.BlockSpec((128, 128), lambda i: (i, 0)), pl.BlockSpec((128, 128), lambda i: (i, 0))],
        out_specs=[rows, rows, tspec, tspec],
        out_shape=[jax.ShapeDtypeStruct((S, KVW), BF16)] * 2 + [jax.ShapeDtypeStruct((NKV, HD, S), BF16)] * 2,
        compiler_params=_cparams("parallel"),
    )(proj, cos, sin)


QROWS = NQ_PER_KV * 128


NBIAS = NCH + 1
KV_PER_STEP = 4


def _bias_table():
    db = lax.broadcasted_iota(jnp.int32, (NBIAS, 128, QROWS), 0) - 1
    ki = lax.broadcasted_iota(jnp.int32, (NBIAS, 128, QROWS), 1)
    qi = lax.broadcasted_iota(jnp.int32, (NBIAS, 128, QROWS), 2) & 127
    d = db * 128 + qi - ki
    cnt = ((d <= 128).astype(F32) + (((d & 3) == 0) & (d <= 512)).astype(F32) + ((d & 15) == 0).astype(F32))
    return jnp.where((d >= 0) & (cnt > 0.0), jnp.log(jnp.maximum(cnt, 1.0)), NEG)


def _attn_fwd(qt, kh, vt, bias, rider=None):
    def body(q_ref, k_ref, v_ref, b_ref, o_ref, lse_ref, rows_ref, m_ref, l_ref, acc_ref):
        qb = pl.program_id(1)
        m_ref[...] = jnp.full_like(m_ref, NEG)
        l_ref[...] = jnp.zeros_like(l_ref)
        acc_ref[...] = jnp.zeros_like(acc_ref)

        def keys(off, size, bias_):
            for h in range(KV_PER_STEP):
                m = m_ref[h]
                s = _dot(k_ref[h, pl.ds(off, size), :], q_ref[h], NN) + bias_
                m_new = jnp.maximum(m, jnp.max(s, axis=0, keepdims=True))
                p = jnp.exp(s - m_new)
                a = jnp.exp(m - m_new)
                m_ref[h] = m_new
                l_ref[h] = a * l_ref[h] + jnp.sum(p, axis=0, keepdims=True)
                acc_ref[h] = a * acc_ref[h] + _dot(v_ref[h, :, pl.ds(off, size)], p, NN)

        def blocks(first, count):
            bias_ = jnp.concatenate([b_ref[qb - first - j + 1] for j in range(count)], axis=0)
            keys(pl.multiple_of(first * 128, 128), 128 * count, bias_)

        nkb = qb + 1
        @pl.loop(0, nkb // 4)
        def _(i):
            blocks(4 * i, 4)

        @pl.when(nkb % 4 >= 2)
        def _():
            blocks(nkb // 4 * 4, 2)

        @pl.when(nkb % 2 == 1)
        def _():
            blocks(qb, 1)

        outs = []
        for h in range(KV_PER_STEP):
            outs.append(acc_ref[h] / l_ref[h])
            o_ref[h] = outs[h]
            lse_ref[h] = m_ref[h] + jnp.log(l_ref[h])
        rows_ref[...] = _blocks_to_rows(outs).astype(BF16)

    kvs = KV_PER_STEP
    qspec = pl.BlockSpec((kvs, None, HD, QROWS), lambda j, i: (j, i, 0, 0))
    return _call(
        body, "attn_fwd", (NKV // kvs, NCH),
        [qspec, pl.BlockSpec((kvs, S, HD), lambda j, i: (j, 0, 0)),
         pl.BlockSpec((kvs, HD, S), lambda j, i: (j, 0, 0)),
         pl.BlockSpec((NBIAS, 128, QROWS), lambda j, i: (0, 0, 0))],
        [qspec, pl.BlockSpec((kvs, None, 1, QROWS), lambda j, i: (j, i, 0, 0)),
         pl.BlockSpec((128, QCOLS * kvs), lambda j, i: (i, j))],
        [jax.ShapeDtypeStruct((NKV, NCH, HD, QROWS), F32), jax.ShapeDtypeStruct((NKV, NCH, 1, QROWS), F32),
         jax.ShapeDtypeStruct((S, 2 * D), BF16)],
        (qt, kh, vt, bias),
        [pltpu.VMEM((kvs, 1, QROWS), F32), pltpu.VMEM((kvs, 1, QROWS), F32), pltpu.VMEM((kvs, HD, QROWS), F32)],
        ("parallel", "parallel"), rider)


def _attn_bwd(qt, kh, kt, vh, dot_, lse, delta, bias, rider=None):
    def body(qt_ref, k_ref, kt_ref, v_ref, dot_ref, lse_ref, dl_ref, b_ref, dq_ref, dk_ref, dv_ref):
        kp = pl.program_id(1)

        @pl.when(kp == 0)
        def _():
            dq_ref[...] = jnp.zeros_like(dq_ref)

        dk_ref[...] = jnp.zeros_like(dk_ref)
        dv_ref[...] = jnp.zeros_like(dv_ref)

        @pl.loop(2 * kp, NCH // 2)
        def _(j):
            for h in range(KV_PER_STEP):
                k, kt_, v = k_ref[h], kt_ref[h], v_ref[h]
                for qb in (2 * j, 2 * j + 1):
                    bias2 = jnp.concatenate([b_ref[jnp.maximum(qb - 4 * kp - t + 1, 0)] for t in range(4)], axis=0)
                    st = _dot(k, qt_ref[h, qb], NN) + bias2
                    pt = jnp.exp(st - lse_ref[h, qb])
                    dst = pt * (_dot(v, dot_ref[h, qb], NN) - dl_ref[h, qb])
                    dq_ref[h, qb] += _dot(kt_, dst, NN)
                    dk_ref[h] += _dot(qt_ref[h, qb], dst, NT)
                    dv_ref[h] += _dot(dot_ref[h, qb], pt, NT)

    kvs = KV_PER_STEP
    tspec = pl.BlockSpec((kvs, NCH, HD, QROWS), lambda j, i: (j, 0, 0, 0))
    kspec = pl.BlockSpec((kvs, 512, HD), lambda j, i: (j, i, 0))
    ktspec = pl.BlockSpec((kvs, HD, 512), lambda j, i: (j, 0, i))
    sspec = pl.BlockSpec((kvs, NCH, 1, QROWS), lambda j, i: (j, 0, 0, 0))
    return _call(
        body, "attn_bwd", (NKV // kvs, NCH // 4),
        [tspec, kspec, ktspec, kspec, tspec, sspec, sspec,
         pl.BlockSpec((NBIAS, 128, QROWS), lambda j, i: (0, 0, 0))],
        [tspec, ktspec, ktspec],
        [jax.ShapeDtypeStruct((NKV, NCH, HD, QROWS), F32),
         jax.ShapeDtypeStruct((NKV, HD, S), F32), jax.ShapeDtypeStruct((NKV, HD, S), F32)],
        (qt, kh, kt, vh, dot_, lse, delta, bias), sem=("parallel", "arbitrary"), rider=rider)


CONV_BLK = 256
CONV_COL0 = 1536 // CONV_BLK


CONV_ROWS = 128


def _conv_fwd(proj, convw, convb):
    trips = S // CONV_ROWS

    def body(u_ref, w_ref, b_ref, o_ref, y_ref):
        @pl.loop(0, trips)
        def _(c):
            t0 = pl.multiple_of(c * CONV_ROWS, CONV_ROWS)
            before = pl.multiple_of(jnp.maximum(t0 - 8, 0), 8)
            ext = jnp.concatenate([jnp.where(c == 0, 0.0, u_ref[pl.ds(before, 8), :]),
                                   u_ref[pl.ds(t0, CONV_ROWS), :]], axis=0)
            y = b_ref[...] + w_ref[CONV_K - 1:CONV_K, :] * ext[8:]
            for j in range(1, CONV_K):
                y = y + w_ref[CONV_K - 1 - j:CONV_K - j, :] * pltpu.roll(ext, j, 0)[8:]
            y_ref[pl.ds(t0, CONV_ROWS), :] = y
            o_ref[pl.ds(t0, CONV_ROWS), :] = y * _sigmoid(y)

    out = pl.BlockSpec((S, CONV_BLK), lambda i: (0, i))
    return pl.pallas_call(
        body, name="conv_fwd", grid=(CONV_C // CONV_BLK,),
        in_specs=[pl.BlockSpec((S, CONV_BLK), lambda i: (0, CONV_COL0 + i)),
                  pl.BlockSpec((CONV_K, CONV_BLK), lambda i: (0, i)),
                  pl.BlockSpec((1, CONV_BLK), lambda i: (0, i))],
        out_specs=[out, out], out_shape=[jax.ShapeDtypeStruct((S, CONV_C), F32)] * 2,
        compiler_params=_cparams("parallel"),
    )(proj, convw, convb)


def _conv_bwd(dact, ypre, proj, convw, dproj):
    trips = S // CONV_ROWS

    def body(da_ref, y_ref, u_ref, w_ref, buf_ref, du_ref, dw_ref, db_ref):
        dw_ref[...] = jnp.zeros_like(dw_ref)
        db_ref[...] = jnp.zeros_like(db_ref)
        r8 = lax.broadcasted_iota(jnp.int32, (8, CONV_BLK), 0)

        def dy_of(rows):
            y = y_ref[rows, :]
            sg = _sigmoid(y)
            return da_ref[rows, :] * (sg * (1.0 + y * (1.0 - sg)))

        @pl.loop(0, trips)
        def _(c):
            t0 = pl.multiple_of(c * CONV_ROWS, CONV_ROWS)
            after = pl.multiple_of(jnp.minimum(t0 + CONV_ROWS, S - 8), 8)
            ext = jnp.concatenate([dy_of(pl.ds(t0, CONV_ROWS)),
                                   jnp.where(c == trips - 1, 0.0, dy_of(pl.ds(after, 8)))], axis=0)
            u = u_ref[pl.ds(t0, CONV_ROWS), :]
            du, dw = None, jnp.zeros((8, CONV_BLK), F32)
            for j in range(CONV_K):
                dyj = (ext if j == 0 else pltpu.roll(ext, CONV_ROWS + 8 - j, 0))[:CONV_ROWS]
                term = w_ref[CONV_K - 1 - j:CONV_K - j, :] * dyj
                du = term if du is None else du + term
                dw = dw + jnp.where(r8 == CONV_K - 1 - j, jnp.sum(dyj * u, axis=0, keepdims=True), 0.0)
            du_ref[pl.ds(t0, CONV_ROWS), :] = du.astype(BF16)
            dw_ref[...] += dw
            db_ref[...] += jnp.sum(ext[:CONV_ROWS], axis=0, keepdims=True)

    return pl.pallas_call(
        body, name="conv_bwd", grid=(CONV_C // CONV_BLK,),
        in_specs=[pl.BlockSpec((S, CONV_BLK), lambda i: (0, i)), pl.BlockSpec((S, CONV_BLK), lambda i: (0, i)),
                  pl.BlockSpec((S, CONV_BLK), lambda i: (0, CONV_COL0 + i)),
                  pl.BlockSpec((CONV_K, CONV_BLK), lambda i: (0, i)), pl.BlockSpec(memory_space=pl.ANY)],
        out_specs=[pl.BlockSpec((S, CONV_BLK), lambda i: (0, CONV_COL0 + i)),
                   pl.BlockSpec((8, CONV_BLK), lambda i: (0, i)), pl.BlockSpec((1, CONV_BLK), lambda i: (0, i))],
        out_shape=[jax.ShapeDtypeStruct(dproj.shape, BF16), jax.ShapeDtypeStruct((8, CONV_C), F32),
                   jax.ShapeDtypeStruct((1, CONV_C), F32)],
        input_output_aliases={4: 0}, compiler_params=_cparams("parallel"),
    )(dact, ypre, proj, convw, dproj)


NPAIR = 8


def _ssd_scalars(dtr_ref, dtb_ref, alog_ref):
    z = dtr_ref[...] + dtb_ref[...]
    dt = jnp.maximum(z, 0.0) + jnp.log(1.0 + jnp.exp(-jnp.abs(z)))
    a = -jnp.exp(alog_ref[...])
    r = lax.broadcasted_iota(jnp.int32, (128, 128), 0)
    c = lax.broadcasted_iota(jnp.int32, (128, 128), 1)
    tri = (r >= c).astype(F32)
    cs = _dot_exact(tri, dt * a)
    return z, dt, a, cs, r, c


def _by_lane(cs, dt):
    head = lax.broadcasted_iota(jnp.int32, (128, SSM_W), 0)
    lane = lax.broadcasted_iota(jnp.int32, (128, SSM_W), 1)
    sel = (head == lane // HD).astype(F32)
    cs_l = _dot_exact(cs, sel, "b")
    last_l = cs_l[127:128, :]
    return sel, jnp.exp(cs_l), jnp.exp(last_l - cs_l), _dot_exact(dt, sel, "b")


def _pair_terms(cs, h1, h2):
    return (cs[:, h1:h1 + 1], cs[:, h2:h2 + 1],
            jnp.exp(cs[127:128, h1:h1 + 1]), jnp.exp(cs[127:128, h2:h2 + 1]))


def _gate_norm(y, zv, w):
    yg = y * (zv * _sigmoid(zv))
    outs, rs = [], []
    for g in range(2):
        blk = yg[:, 512 * g:512 * (g + 1)]
        r = lax.rsqrt(jnp.mean(blk * blk, axis=-1, keepdims=True) + EPS)
        outs.append(blk * r)
        rs.append(r)
    return jnp.concatenate(outs, axis=1), rs, yg


def _ssd_fwd(xbc, proj, dtb, alog, dskip_l, ssmw, mixed):
    def body(x_ref, b_ref, c_ref, dtr_ref, z_ref, dtb_ref, alog_ref, dsk_ref, w_ref, buf_ref,
             y_ref, yn_ref, hp_ref, h_ref):
        @pl.when(pl.program_id(0) == 0)
        def _():
            h_ref[...] = jnp.zeros_like(h_ref)

        _, dt, _, cs, r, c = _ssd_scalars(dtr_ref, dtb_ref, alog_ref)
        cst = cs.T
        causal = r >= c
        lo = c < HD
        _, e_all, dte_all, dt_all = _by_lane(cs, dt)
        hp_ref[...] = h_ref[...]
        for g in range(2):
            bg = b_ref[:, 128 * g:128 * (g + 1)]
            cg = c_ref[:, 128 * g:128 * (g + 1)]
            cb = _dot(cg, bg, NT)
            for j in range(4):
                pj = 4 * g + j
                h1, h2 = 2 * pj, 2 * pj + 1
                sl = slice(128 * pj, 128 * (pj + 1))
                xp = x_ref[:, sl]
                c1, c2, cd1, cd2 = _pair_terms(cs, h1, h2)
                e_l, dte_l = e_all[:, sl], dte_all[:, sl]
                xdt = xp * dt_all[:, sl]
                m1 = cb * jnp.exp(jnp.where(causal, c1 - cst[h1:h1 + 1, :], NEG))
                m2 = cb * jnp.exp(jnp.where(causal, c2 - cst[h2:h2 + 1, :], NEG))
                yd = jnp.where(lo, _dot(m1, xdt, NN), _dot(m2, xdt, NN))
                hp = h_ref[pj]
                yo = _dot(cg, hp, NT) * e_l
                st = _dot(xdt * dte_l, bg, TN)
                h_ref[pj] = hp * jnp.where(r < HD, cd1, cd2) + st
                y_ref[:, sl] = yd + yo + dsk_ref[:, sl] * xp
        yn, _, _ = _gate_norm(y_ref[...], z_ref[...], w_ref[...])
        yn_ref[...] = (yn * w_ref[...]).astype(BF16)

    return pl.pallas_call(
        body, name="ssd_fwd", grid=(NCH,),
        in_specs=[pl.BlockSpec((128, SSM_W), lambda i: (i, 0)),
                  pl.BlockSpec((128, 256), lambda i: (i, 4)), pl.BlockSpec((128, 256), lambda i: (i, 5)),
                  pl.BlockSpec((128, 128), lambda i: (i, COL_DT // 128)),
                  pl.BlockSpec((128, SSM_W), lambda i: (i, 3)),
                  pl.BlockSpec((1, 128), lambda i: (0, 0)), pl.BlockSpec((1, 128), lambda i: (0, 0)),
                  pl.BlockSpec((1, SSM_W), lambda i: (0, 0)), pl.BlockSpec((1, SSM_W), lambda i: (0, 0)),
                  pl.BlockSpec(memory_space=pl.ANY)],
        out_specs=[pl.BlockSpec((128, SSM_W), lambda i: (i, 0)), pl.BlockSpec((128, SSM_W), lambda i: (i, 1)),
                   pl.BlockSpec((None, NPAIR, 128, 128), lambda i: (i, 0, 0, 0))],
        out_shape=[jax.ShapeDtypeStruct((S, SSM_W), F32), jax.ShapeDtypeStruct(mixed.shape, BF16),
                   jax.ShapeDtypeStruct((NCH, NPAIR, 128, 128), F32)],
        scratch_shapes=[pltpu.VMEM((NPAIR, 128, 128), F32)],
        input_output_aliases={9: 1}, compiler_params=_cparams("arbitrary"),
    )(xbc, xbc, xbc, proj, proj, dtb, alog, dskip_l, ssmw, mixed)


def _ssd_bwd(dmixed, y, xbc, proj, hprev, dtb, alog, dskip_l, ssmw, rider=None):
    def body(dyn_ref, y_ref, x_ref, b_ref, c_ref, dtr_ref, z_ref, hp_ref, dtb_ref, alog_ref, dsk_ref, w_ref,
             dxbc_ref, dz_ref, ddt_ref, dw_ref, dsc_ref, g_ref):
        @pl.when(pl.program_id(0) == 0)
        def _():
            g_ref[...] = jnp.zeros_like(g_ref)
            dsc_ref[...] = jnp.zeros_like(dsc_ref)

        z, dt, a, cs, r, c = _ssd_scalars(dtr_ref, dtb_ref, alog_ref)
        cst = cs.T
        causal = r >= c
        lo = c < HD

        yv = y_ref[...]
        zv = z_ref[...]
        wv = w_ref[...]
        ygn, rs, yg = _gate_norm(yv, zv, wv)
        dyn = dyn_ref[...]
        _acc_rows(dw_ref, dyn * ygn)
        dynw = dyn * wv
        parts = []
        for g in range(2):
            sl = slice(512 * g, 512 * (g + 1))
            a_g, n_g = dynw[:, sl], ygn[:, sl]
            parts.append(rs[g] * (a_g - n_g * jnp.mean(a_g * n_g, axis=-1, keepdims=True)))
        dyg = jnp.concatenate(parts, axis=1)
        sz = _sigmoid(zv)
        dz_ref[...] = (dyg * yv * (sz * (1.0 + zv * (1.0 - sz)))).astype(BF16)
        dy_all = dyg * (zv * sz)

        dcs_cols = jnp.zeros((128, 128), F32)
        dcs_rows = jnp.zeros((128, 128), F32)
        sel, e_all, dte_all, dt_all = _by_lane(cs, dt)
        x_all, b_all, c_all, dsk_all = x_ref[...], b_ref[...], c_ref[...], dsk_ref[...]
        hp_all, g_all = hp_ref[...], g_ref[...]
        g_new, dx_parts, db_parts, dc_parts = [], [], [], []
        dyx_parts, ryo_parts, qx_parts, dxx_parts, gh_parts = [], [], [], [], []
        for g in range(2):
            bg = b_all[:, 128 * g:128 * (g + 1)]
            cg = c_all[:, 128 * g:128 * (g + 1)]
            cb = _dot(cg, bg, NT)
            dcb = jnp.zeros((128, 128), F32)
            db_acc = jnp.zeros((128, NST), F32)
            dc_acc = jnp.zeros((128, NST), F32)
            for j in range(4):
                pj = 4 * g + j
                h1, h2 = 2 * pj, 2 * pj + 1
                sl = slice(128 * pj, 128 * (pj + 1))
                xp = x_all[:, sl]
                dyp = dy_all[:, sl]
                c1, c2, cd1, cd2 = _pair_terms(cs, h1, h2)
                e_l, dte_l, dt_l = e_all[:, sl], dte_all[:, sl], dt_all[:, sl]
                xdt = xp * dt_l
                hp = hp_all[pj]
                gp = g_all[pj]
                dyx_parts.append(dyp * xp)
                dzs = dyp * e_l
                dc_acc = dc_acc + _dot(dzs, hp, NN)
                ryo_parts.append(dyp * (_dot(cg, hp, NT) * e_l))
                qm = _dot(bg, gp, NT)
                dxdt = qm * dte_l
                qx_parts.append(qm * xdt)
                db_acc = db_acc + _dot(xdt * dte_l, gp, NN)
                gh_parts.append(gp * hp)
                g_new.append(_dot(dzs, cg, TN) + jnp.where(r < HD, cd1, cd2) * gp)
                for hh, ch, msk in ((h1, c1, lo), (h2, c2, jnp.logical_not(lo))):
                    lm = jnp.exp(jnp.where(causal, ch - cst[hh:hh + 1, :], NEG))
                    mm = cb * lm
                    dm = jnp.where(causal, _dot(jnp.where(msk, dyp, 0.0), xdt, NT), 0.0)
                    w = dm * mm
                    dcs_cols = dcs_cols + jnp.where(c == hh, jnp.sum(w, axis=1, keepdims=True), 0.0)
                    dcs_rows = dcs_rows + jnp.where(r == hh, jnp.sum(w, axis=0, keepdims=True), 0.0)
                    dcb = dcb + dm * lm
                    dxdt = dxdt + jnp.where(msk, _dot(mm, dyp, TN), 0.0)
                dxx_parts.append(dxdt * xp)
                dx_parts.append(dsk_all[:, sl] * dyp + dxdt * dt_l)
            db_parts.append(db_acc + _dot(dcb, cg, TN))
            dc_parts.append(dc_acc + _dot(dcb, bg, NN))
        g_ref[...] = jnp.stack(g_new)
        dxbc_ref[...] = jnp.concatenate(dx_parts + db_parts + dc_parts, axis=1)

        selt = (lax.broadcasted_iota(jnp.int32, (SSM_W, 128), 0) // HD
                == lax.broadcasted_iota(jnp.int32, (SSM_W, 128), 1)).astype(F32)

        def by_head(parts):
            return _dot_exact(jnp.concatenate(parts, axis=1), selt, "b")

        ddt_x = by_head(dxx_parts)
        dd_row = jnp.sum(by_head(dyx_parts), axis=0, keepdims=True)
        t_all = by_head(qx_parts) * jnp.exp(cs[127:128, :] - cs)
        gh = jnp.sum(_dot_exact(sel, jnp.concatenate(gh_parts, axis=0)), axis=1, keepdims=True)
        gh_row = jnp.broadcast_to(gh, (128, 128)).T[0:1, :]
        at_end = jnp.sum(t_all, axis=0, keepdims=True) + gh_row * jnp.exp(cs[127:128, :])
        dcs = by_head(ryo_parts) - t_all + dcs_cols + jnp.where(r == 127, at_end, 0.0) - dcs_rows.T
        dad = _dot_exact((c >= r).astype(F32), dcs)
        ddt = dad * a + ddt_x
        ddtr = jnp.where(c < 16, ddt * _sigmoid(z), 0.0)
        ddt_ref[...] = ddtr.astype(BF16)
        r8 = lax.broadcasted_iota(jnp.int32, (8, 128), 0)
        dsc_ref[...] += (jnp.where(r8 == 0, jnp.sum(ddtr, axis=0, keepdims=True), 0.0)
                         + jnp.where(r8 == 1, jnp.sum(dad * dt, axis=0, keepdims=True) * a, 0.0)
                         + jnp.where(r8 == 2, dd_row, 0.0))

    rev = NCH - 1
    return _call(
        body, "ssd_bwd", (NCH,),
        [pl.BlockSpec((128, SSM_W), lambda i: (rev - i, 0)),
         pl.BlockSpec((128, SSM_W), lambda i: (rev - i, 0)),
         pl.BlockSpec((128, SSM_W), lambda i: (rev - i, 0)),
         pl.BlockSpec((128, 256), lambda i: (rev - i, 4)), pl.BlockSpec((128, 256), lambda i: (rev - i, 5)),
         pl.BlockSpec((128, 128), lambda i: (rev - i, COL_DT // 128)),
         pl.BlockSpec((128, SSM_W), lambda i: (rev - i, 3)),
         pl.BlockSpec((None, NPAIR, 128, 128), lambda i: (rev - i, 0, 0, 0)),
         pl.BlockSpec((1, 128), lambda i: (0, 0)), pl.BlockSpec((1, 128), lambda i: (0, 0)),
         pl.BlockSpec((1, SSM_W), lambda i: (0, 0)), pl.BlockSpec((1, SSM_W), lambda i: (0, 0))],
        [pl.BlockSpec((128, CONV_C), lambda i: (rev - i, 0)),
         pl.BlockSpec((128, SSM_W), lambda i: (rev - i, 3)),
         pl.BlockSpec((128, 128), lambda i: (rev - i, 0)),
         pl.BlockSpec((1, SSM_W), lambda i: (0, 0)), pl.BlockSpec((8, 128), lambda i: (0, 0))],
        [jax.ShapeDtypeStruct((S, CONV_C), F32), jax.ShapeDtypeStruct((S, WIN_PAD), BF16),
         jax.ShapeDtypeStruct((S, 128), BF16), jax.ShapeDtypeStruct((1, SSM_W), F32),
         jax.ShapeDtypeStruct((8, 128), F32)],
        (dmixed, y, xbc, xbc, xbc, proj, proj, hprev, dtb, alog, dskip_l, ssmw),
        [pltpu.VMEM((NPAIR, 128, 128), F32)], ("arbitrary",), rider)


def _cast_stack(name, slot, arrs, tr, tc):
    n = len(arrs)
    rows, cols = arrs[0].shape

    def body(s_ref, *refs):
        for i in range(n):
            refs[n][i] = refs[i][...].astype(BF16)

    return pl.pallas_call(
        body, name=name,
        grid_spec=pltpu.PrefetchScalarGridSpec(
            num_scalar_prefetch=1, grid=(rows // tr, cols // tc),
            in_specs=[pl.BlockSpec((tr, tc), lambda i, j, sr: (i, j))] * n,
            out_specs=pl.BlockSpec((None, n, tr, tc), lambda i, j, sr: (sr[0], 0, i, j))),
        out_shape=jax.ShapeDtypeStruct((NSH, n, rows, cols), BF16),
        compiler_params=_cparams("parallel", "parallel"),
    )(slot, *arrs)


def _pair_sum(name, c_idx, ps, th):
    n = len(ps)
    _, rows, _ = ps[0].shape

    def body(c_ref, *refs):
        mine, whole, out, theirs = refs[:n], refs[n:2 * n], refs[2 * n:3 * n], refs[3 * n:4 * n]
        send, recv = refs[4 * n], refs[4 * n + 1]
        s, i = pl.program_id(0), pl.program_id(1)
        x, y, c, _ = _place()

        def copies(slot):
            return [_rcopy(whole[k].at[slot, :, pl.ds((1 - c) * HALF, HALF)], theirs[k].at[slot],
                           send.at[slot * n + k], recv.at[slot * n + k], (x, y, 1 - c)) for k in range(n)]

        @pl.when((s == 0) & (i == 0))
        def _():
            for slot in range(NSH):
                for cp in copies(slot):
                    cp.start()

        @pl.when(i == 0)
        def _():
            for slot in range(NSH):
                @pl.when(s == slot)
                def _():
                    for cp in copies(slot):
                        cp.wait()

        rows_i = slice(None) if th == rows else pl.ds(pl.multiple_of(i * th, th), th)
        for k in range(n):
            out[k][...] = (mine[k][...].astype(F32) + theirs[k][s, rows_i, :].astype(F32)).astype(BF16)

    spec = pl.BlockSpec((None, th, HALF), lambda s, i, cr: (s, i, 0))
    return pl.pallas_call(
        body, name=name,
        grid_spec=pltpu.PrefetchScalarGridSpec(
            num_scalar_prefetch=1, grid=(NSH, rows // th),
            in_specs=[pl.BlockSpec((None, th, HALF), lambda s, i, cr: (s, i, cr[0]))] * n + _any_specs(n),
            out_specs=[spec] * n,
            scratch_shapes=[pltpu.VMEM((NSH, rows, HALF), BF16)] * n
            + [pltpu.SemaphoreType.DMA((NSH * n,)), pltpu.SemaphoreType.DMA((NSH * n,))]),
        out_shape=[jax.ShapeDtypeStruct((NSH, rows, HALF), BF16)] * n,
        compiler_params=_cparams("arbitrary", "arbitrary"),
    )(c_idx, *ps, *ps)


def _pair_add(name, c_idx, ps, theirs, th):
    n = len(ps)
    _, rows, _ = ps[0].shape

    def body(c_ref, *refs):
        for k in range(n):
            refs[2 * n + k][...] = (refs[k][...].astype(F32) + refs[n + k][...].astype(F32)).astype(BF16)

    spec = pl.BlockSpec((None, th, HALF), lambda s, i, cr: (s, i, 0))
    return pl.pallas_call(
        body, name=name,
        grid_spec=pltpu.PrefetchScalarGridSpec(
            num_scalar_prefetch=1, grid=(NSH, rows // th),
            in_specs=[pl.BlockSpec((None, th, HALF), lambda s, i, cr: (s, i, cr[0]))] * n + [spec] * n,
            out_specs=[spec] * n),
        out_shape=[jax.ShapeDtypeStruct((NSH, rows, HALF), BF16)] * n,
        compiler_params=_cparams("parallel", "parallel"),
    )(c_idx, *ps, *theirs)


def _chip_sum(name, place, cs, ts, th):
    n = len(ts)
    _, rows, _ = ts[0].shape

    def body(p_ref, *refs):
        for i in range(n):
            t = refs[n + i][...].astype(F32)
            refs[2 * n + i][...] = ((refs[i][...].astype(F32) + t[0]) + t[1]) + t[2]

    return pl.pallas_call(
        body, name=name,
        grid_spec=pltpu.PrefetchScalarGridSpec(
            num_scalar_prefetch=1, grid=(rows // th,),
            in_specs=[pl.BlockSpec((None, th, HALF), lambda i, pr: (pr[0], i, 0))] * n
            + [pl.BlockSpec((3, th, HALF), lambda i, pr: (0, i, 0))] * n,
            out_specs=[pl.BlockSpec((th, HALF), lambda i, pr: (i, pr[1]))] * n),
        out_shape=[jax.ShapeDtypeStruct((rows, D), F32)] * n, compiler_params=_cparams("parallel"),
    )(place, *cs, *ts)


def _adamw(name, ws, gs, ms, vs, tr, tc):
    n = len(ws)
    shape = ws[0].shape
    rows, cols, mid = shape[0], shape[-1], shape[1:-1]
    c1 = 1.0 / (1.0 - ADAM_B1 ** ADAM_STEP)
    c2 = 1.0 / (1.0 - ADAM_B2 ** ADAM_STEP)

    def body(*refs):
        for i in range(n):
            w, g, m, v = (refs[k * n + i][...] for k in range(4))
            m2 = ADAM_B1 * m + (1.0 - ADAM_B1) * g
            v2 = ADAM_B2 * v + (1.0 - ADAM_B2) * (g * g)
            refs[4 * n + 4 * i][...] = -ADAM_LR * ((m2 * c1) / (jnp.sqrt(v2 * c2) + ADAM_EPS) + ADAM_WD * w)
            refs[4 * n + 4 * i + 1][...] = m2
            refs[4 * n + 4 * i + 2][...] = v2
            refs[4 * n + 4 * i + 3][...] = g

    spec = pl.BlockSpec((tr,) + mid + (tc,), lambda i, j: (i,) + (0,) * len(mid) + (j,))
    outs = pl.pallas_call(
        body, name=name, grid=(rows // tr, cols // tc), in_specs=[spec] * (4 * n), out_specs=[spec] * (4 * n),
        out_shape=[jax.ShapeDtypeStruct(shape, F32)] * (4 * n),
        compiler_params=_cparams("parallel", "parallel"),
    )(*ws, *gs, *ms, *vs)
    return [tuple(outs[4 * i:4 * i + 4]) for i in range(n)]


def _place():
    x, y, c = lax.axis_index("x"), lax.axis_index("y"), lax.axis_index("c")
    chips = [(1 - x, y), (x, 1 - y), (1 - x, 1 - y)]
    return x, y, c, chips


def _any_specs(n):
    return [pl.BlockSpec(memory_space=pl.ANY)] * n


def _rcopy(src, dst, send_sem, recv_sem, dev):
    return pltpu.make_async_remote_copy(src_ref=src, dst_ref=dst, send_sem=send_sem, recv_sem=recv_sem,
                                        device_id=dev, device_id_type=MESH)


QUARTER = HALF // 2

TO_X, TO_Y, RELAY_X, RELAY_Y, FWD_X, FWD_Y, FWD_D0, FWD_D1 = range(8)


def _gather_rider(bufs, views, at=None):
    n = len(bufs)

    def plan(rout, sems):
        send, recv = sems
        x, y, c, _ = _place()
        me, sx, sy, sd = 2 * x + y, 2 * (1 - x) + y, 2 * x + (1 - y), 2 * (1 - x) + (1 - y)
        nx, ny, sib = (1 - x, y, c), (x, 1 - y, c), (x, y, 1 - c)
        mine, other = c * HALF, (1 - c) * HALF
        out = {TO_X: (me, mine, HALF, nx), TO_Y: (me, mine, HALF, ny),
               FWD_X: (sx, mine, HALF, sib), FWD_Y: (sy, mine, HALF, sib),
               RELAY_X: (sy, mine, QUARTER, nx), RELAY_Y: (sx, mine + QUARTER, QUARTER, ny),
               FWD_D0: (sd, mine, QUARTER, sib), FWD_D1: (sd, mine + QUARTER, QUARTER, sib)}
        inn = {TO_X: (sx, mine, HALF), TO_Y: (sy, mine, HALF),
               FWD_X: (sx, other, HALF), FWD_Y: (sy, other, HALF),
               RELAY_X: (sd, mine, QUARTER), RELAY_Y: (sd, mine + QUARTER, QUARTER),
               FWD_D0: (sd, other, QUARTER), FWD_D1: (sd, other + QUARTER, QUARTER)}

        def copy(kind, b):
            slot, col, ncols, dev = out[kind]
            win = views[b](rout[b], slot, col, ncols)
            return _rcopy(win, win, send.at[kind * n + b], recv.at[kind * n + b], dev)

        def land(kind, b):
            slot, col, ncols = inn[kind]
            win = views[b](rout[b], slot, col, ncols)
            return _rcopy(win, win, send.at[kind * n + b], recv.at[kind * n + b], (x, y, c))

        return copy, land

    first = (TO_X, TO_Y)
    chain = ((TO_X, (FWD_X, RELAY_Y)), (TO_Y, (FWD_Y, RELAY_X)), (RELAY_X, (FWD_D0,)), (RELAY_Y, (FWD_D1,)))
    forwards = (FWD_X, FWD_Y, FWD_D0, FWD_D1)
    sent = first + tuple(k for _, then in chain for k in then)

    def start(rin, rout, sems):
        copy, _ = plan(rout, sems)
        for kind in first:
            for b in range(n):
                copy(kind, b).start()

    def pass_on(rout, sems, links):
        copy, land = plan(rout, sems)
        for landed, then in links:
            for b in range(n):
                land(landed, b).wait_recv()
                for kind in then:
                    copy(kind, b).start()

    def between(rin, rout, sems):
        pass_on(rout, sems, chain[:2])

    def finish(rin, rout, sems):
        pass_on(rout, sems, chain[2:])
        copy, land = plan(rout, sems)
        for kind in forwards:
            for b in range(n):
                land(kind, b).wait_recv()
        for kind in sent:
            for b in range(n):
                copy(kind, b).wait_send()

    return _Rider(list(bufs), [jax.ShapeDtypeStruct(a.shape, a.dtype) for a in bufs], {b: b for b in range(n)},
                  [pltpu.SemaphoreType.DMA((8 * n,))] * 2, start, finish, between, at)


def _small_gather_rider(cw):
    def descs(rin, rout, sems, x, y, c, chips):
        return [_rcopy(rin[0], rout[0].at[2 * x + y], sems[1].at[j], sems[2].at[j], (chip[0], chip[1], c))
                for j, chip in enumerate(chips)]

    def start(rin, rout, sems):
        x, y, c, chips = _place()
        pltpu.make_async_copy(rin[0], rout[0].at[2 * x + y], sems[0].at[0]).start()
        for cp in descs(rin, rout, sems, x, y, c, chips):
            cp.start()

    def finish(rin, rout, sems):
        x, y, c, chips = _place()
        for j, chip in enumerate(chips):
            _rcopy(rin[0], rout[0].at[2 * chip[0] + chip[1]], sems[1].at[j], sems[2].at[j], (x, y, c)).wait_recv()
        for cp in descs(rin, rout, sems, x, y, c, chips):
            cp.wait_send()
        pltpu.make_async_copy(rin[0], rout[0].at[2 * x + y], sems[0].at[0]).wait()

    return _Rider([cw], [jax.ShapeDtypeStruct((NSH,) + cw.shape, cw.dtype)], {},
                  [pltpu.SemaphoreType.DMA((1,)), pltpu.SemaphoreType.DMA((3,)), pltpu.SemaphoreType.DMA((3,))],
                  start, finish)


def _to_sibling_rider(ps):
    n = len(ps)

    def descs(rin, rout, sems):
        x, y, c, _ = _place()
        return [_rcopy(rin[i].at[:, :, pl.ds((1 - c) * HALF, HALF)], rout[i], sems[0].at[i], sems[1].at[i],
                       (x, y, 1 - c)) for i in range(n)]

    def start(rin, rout, sems):
        for cp in descs(rin, rout, sems):
            cp.start()

    def finish(rin, rout, sems):
        for cp in descs(rin, rout, sems):
            cp.wait()

    return _Rider(list(ps), [jax.ShapeDtypeStruct(a.shape[:2] + (HALF,), a.dtype) for a in ps], {},
                  [pltpu.SemaphoreType.DMA((n,))] * 2, start, finish)


def _to_chips_rider(cs, first=0, count=None, into=None):
    n = len(cs)
    rows = [pl.ds(first, a.shape[1] - first if count is None else count) for a in cs]

    def descs(rin, rout, sems):
        x, y, c, chips = _place()
        return [_rcopy(rin[i].at[2 * chip[0] + chip[1], rows[i]], rout[i].at[j, rows[i]], sems[0].at[j * n + i],
                       sems[1].at[j * n + i], (chip[0], chip[1], c)) for j, chip in enumerate(chips) for i in range(n)]

    def start(rin, rout, sems):
        for cp in descs(rin, rout, sems):
            cp.start()

    def finish(rin, rout, sems):
        for cp in descs(rin, rout, sems):
            cp.wait()

    return _Rider(list(cs) + list(into or []), [jax.ShapeDtypeStruct((3,) + a.shape[1:], a.dtype) for a in cs],
                  {n + i: i for i in range(n)} if into else {}, [pltpu.SemaphoreType.DMA((3 * n,))] * 2, start, finish)


def _join_riders(riders):
    counts = [[len(r.operands) for r in riders], [len(r.out_shapes) for r in riders], [len(r.sems) for r in riders]]

    def each(step):
        def run(*refs):
            at = [0, 0, 0]
            for i, r in enumerate(riders):
                parts = [group[at[k]:at[k] + counts[k][i]] for k, group in enumerate(refs)]
                at = [at[k] + counts[k][i] for k in range(3)]
                step(r)(*parts)
        return run

    aliases = {sum(counts[0][:i]) + k: sum(counts[1][:i]) + v
               for i, r in enumerate(riders) for k, v in r.aliases.items()}
    return _Rider([a for r in riders for a in r.operands], [s for r in riders for s in r.out_shapes], aliases,
                  [s for r in riders for s in r.sems], each(lambda r: r.start), each(lambda r: r.finish),
                  each(lambda r: r.between or (lambda *refs: None)))


SMALL_ROWS = 16


def _swap_halves(gs, vec):
    n = len(gs)

    def body(*refs):
        v_ref, dst, o_ref = refs[n], refs[n + 1:2 * n + 1], refs[2 * n + 1]
        buf, send, recv, vsend, vrecv = refs[2 * n + 2:]
        x, y, c, _ = _place()
        cps = []
        for i in range(n):
            mine = dst[i].at[:, pl.ds(c * HALF, HALF)]
            cps.append(_rcopy(mine, mine, send.at[i], recv.at[i], (x, y, 1 - c)))
        for cp in cps:
            cp.start()

        me = 4 * x + 2 * y + c
        buf[me] = v_ref[...]
        vcps = []
        for k in range(1, 8):
            peer = (x ^ (k >> 2), y ^ ((k >> 1) & 1), c ^ (k & 1))
            vcps.append(_rcopy(v_ref, buf.at[me], vsend.at[k - 1], vrecv.at[k - 1], peer))
        for cp in vcps:
            cp.start()
        for k in range(1, 8):
            _rcopy(v_ref, buf.at[me ^ k], vsend.at[k - 1], vrecv.at[k - 1], (x, y, c)).wait_recv()
        for cp in vcps:
            cp.wait_send()
        t = buf[0]
        for d in range(1, 8):
            t = t + buf[d]
        o_ref[...] = t

        for i in range(n):
            other = dst[i].at[:, pl.ds((1 - c) * HALF, HALF)]
            _rcopy(other, other, send.at[i], recv.at[i], (x, y, c)).wait_recv()
        for cp in cps:
            cp.wait_send()

    vmem = pl.BlockSpec(memory_space=pltpu.VMEM)
    res = pl.pallas_call(
        body, name="grads_swap_halves", in_specs=_any_specs(n) + [vmem], out_specs=_any_specs(n) + [vmem],
        out_shape=[jax.ShapeDtypeStruct(g.shape, g.dtype) for g in gs] + [jax.ShapeDtypeStruct((SMALL_ROWS, D), F32)],
        input_output_aliases={i: i for i in range(n)},
        scratch_shapes=[pltpu.VMEM((8, SMALL_ROWS, D), F32)] + [pltpu.SemaphoreType.DMA((n,))] * 2
        + [pltpu.SemaphoreType.DMA((7,))] * 2,
    )(*gs, vec)
    return list(res[:n]), res[n]


def _col_window(ref, slot, col, ncols):
    return ref.at[slot, :, pl.ds(col, ncols)]


def _stack_window(first, count):
    def view(ref, slot, col, ncols):
        return ref.at[slot, pl.ds(first, count), :, pl.ds(col, ncols)]
    return view


def _row_tile(rows):
    for t in range(512, 15, -16):
        if rows % t == 0:
            return t
    return rows


def _same_shape_runs(arrs):
    runs, a = [], 0
    for b in range(1, len(arrs) + 1):
        if b == len(arrs) or arrs[b].shape != arrs[a].shape:
            runs.append((a, b))
            a = b
    return runs


class _Comm:
    def __init__(self):
        x, y, c = lax.axis_index("x"), lax.axis_index("y"), lax.axis_index("c")
        self.c_idx = jnp.reshape(c, (1,)).astype(jnp.int32)
        self.shard = jnp.reshape(2 * x + y, (1,)).astype(jnp.int32)
        self.place = jnp.stack([2 * x + y, c]).astype(jnp.int32)
        self.groups = {}
        self.sent = {}

    @staticmethod
    def gather(*bufs, part=None, at=None):
        views = [_col_window if b.ndim == 3 else _stack_window(*(part or (0, b.shape[1]))) for b in bufs]
        return _gather_rider(list(bufs), views, at)

    def reduce_rider(self, tag, names, ps, theirs=None, rows=None):
        csums = []
        for a, b in _same_shape_runs(ps):
            name, th = "pair_sum_%s%d" % (tag, a), _row_tile(ps[a].shape[1])
            csums += (_pair_sum(name, self.c_idx, ps[a:b], th) if theirs is None else
                      _pair_add(name, self.c_idx, ps[a:b], theirs[a:b], th))
        self.groups[tag] = [names, csums, None]
        self.sent[tag] = rows
        return _to_chips_rider(csums, 0, rows)

    def rest_rider(self, tag):
        _, csums, ts = self.groups[tag]
        return _to_chips_rider(csums, self.sent[tag], None, ts)

    def landed(self, tag, ts):
        self.groups[tag][2] = ts

    def finish(self, small):
        names, csums, ts = [], [], []
        for group_names, group_csums, group_ts in self.groups.values():
            names += group_names
            csums += group_csums
            ts += group_ts
        order = sorted(range(len(names)), key=lambda i: csums[i].shape[1])
        names, csums, ts = ([v[i] for i in order] for v in (names, csums, ts))
        halves = []
        for a, b in _same_shape_runs(csums):
            halves += _chip_sum("chip_sum_%d" % a, self.place, csums[a:b], ts[a:b], _row_tile(csums[a].shape[1]))
        grads, total = _swap_halves(halves, small)
        return dict(zip(names, grads)), total


ROPE_THETA = 10000.0
SMALL_1K = ("ffn1_pre_norm", "ffn1_post_norm", "mix_pre_norm", "ssm_norm", "mix_post_norm",
            "ffn2_pre_norm", "ffn2_post_norm")
SMALL_16 = ("dt_bias", "a_log", "d_skip")
OFF_CONVB = 7 * D
OFF_16 = OFF_CONVB + CONV_C
OFF_CONVW = OFF_16 + 48
OFF_LOSS = OFF_CONVW + CONV_K * CONV_C
SMALL_LEN = SMALL_ROWS * D


def _sds(shape, dtype):
    return jax.ShapeDtypeStruct(shape, dtype)


def _ridden(res, rider):
    return res if rider is not None else (res, None)


def _ffn_down(name, act, w, tail_of, rider=None):
    tail, o_specs, o_shapes = tail_of(TS)
    return _mm(name, [act, w.dn], NN, (S // TS,),
               [pl.BlockSpec((NSH, TS, FS), lambda i: (0, i, 0)),
                pl.BlockSpec((NSH, None, FS, D), lambda i: (0, w.d0, 0, 0))], o_specs, o_shapes, rider, tail)


def _ffn_dw(name, a, b, rider=None):
    return _mm(name, [a, b], TN, (NSH,),
               [pl.BlockSpec((None, S, FS), lambda s: (s, 0, 0)), pl.BlockSpec((S, D), lambda s: (0, 0))],
               pl.BlockSpec((None, FS, D), lambda s: (s, 0, 0)), _sds((NSH, FS, D), BF16), rider)


def _ffn_dn(name, dgate, dup, w, tail_of, rider=None):
    rows = TS // 2
    tail, o_specs, o_shapes = tail_of(rows)
    a2 = pl.BlockSpec((NSH, rows, FS), lambda i: (0, i, 0))
    return _mm(name, [dgate, w.gu, dup, w.gu], NN, (S // rows,),
               [a2, pl.BlockSpec((NSH, None, FS, D), lambda i: (0, w.g0, 0, 0)),
                a2, pl.BlockSpec((NSH, None, FS, D), lambda i: (0, w.g0 + 1, 0, 0))], o_specs, o_shapes, rider, tail)


def _out_proj_dx(dh, wout, ot):
    def body(dh_ref, w_ref, o_ref, dyn_ref, do_ref, dl_ref):
        dm = _dot(dh_ref[...], w_ref[...], NT)
        dyn_ref[...] = dm[:, D:]
        for b in range(TS // 128):
            for j, blk in enumerate(_rows_to_blocks(dm[128 * b:128 * (b + 1), :D])):
                do = blk.astype(BF16)
                do_ref[j, b] = do
                dl_ref[j, b] = jnp.sum(o_ref[j, b] * do.astype(F32), axis=0, keepdims=True)

    blocks = pl.BlockSpec((NKV, TS // 128, HD, QROWS), lambda i: (0, i, 0, 0))
    return pl.pallas_call(
        body, name="out_proj_dx", grid=(S // TS,),
        in_specs=[pl.BlockSpec((TS, D), lambda i: (i, 0)), pl.BlockSpec((2 * D, D), lambda i: (0, 0)), blocks],
        out_specs=[pl.BlockSpec((TS, D), lambda i: (i, 0)), blocks,
                   pl.BlockSpec((NKV, TS // 128, 1, QROWS), lambda i: (0, i, 0, 0))],
        out_shape=[_sds((S, D), F32), _sds((NKV, NCH, HD, QROWS), BF16), _sds((NKV, NCH, 1, QROWS), F32)],
        compiler_params=_cparams("parallel"),
    )(dh, wout, ot)


def _heads(t, n):
    return t.reshape(S, n, HD).transpose(1, 0, 2)


def _pad128(v):
    return jnp.pad(v, ((0, 0), (0, 128 - v.shape[1])))


def _local_step(x, positions, tgt, sp, gu1, d1, f2, wint, wout, convw, comm=None):
    inv_freq = ROPE_THETA ** (-jnp.arange(0, HD, 2, dtype=F32) / HD)
    ang = positions.astype(F32)[:, None] * inv_freq
    ang = jnp.concatenate([ang, ang, ang, ang], axis=-1)
    cos, sin = jnp.cos(ang), jnp.sin(ang)
    dtb, alog = _pad128(sp["dt_bias"]), _pad128(sp["a_log"])
    dskip_l = jnp.repeat(sp["d_skip"], HD, axis=1)
    convb = sp["conv_b"]

    if comm:
        rider = _join_riders([comm.gather(gu1), _small_gather_rider(convw)])
        (n1, d1, f2, wint, wout), (gu1, convw) = _prenorm_casts(
            "prenorm1", x, sp["ffn1_pre_norm"], comm.shard, [(d1, 0), (f2, 0), (wint, 1), (wout, 0)], rider)
        wint, wout = wint.reshape(NSH, WIN_SH, D), wout.reshape(NSH, 2 * D // NSH, D)
        convw = convw.transpose(1, 0, 2).reshape(CONV_K, CONV_C)
    else:
        n1 = _prenorm("prenorm1", x, sp["ffn1_pre_norm"])
    rider = comm.gather(d1) if comm else None
    (fg1, fu1, act1), got = _ridden(_ffn_up("ffn1_up", n1, _FfnW(gu1, 0, d1, 0), rider), rider)
    if comm:
        d1, = got
    w1 = _FfnW(gu1, 0, d1, 0)
    rider = comm.gather(wint) if comm else None
    (h1, x1, n2), got = _ridden(_ffn_down(
        "ffn1_down", act1, w1,
        lambda rows: _tail_postres(rows, x, sp["ffn1_post_norm"], 0.5, sp["mix_pre_norm"]), rider), rider)
    if comm:
        wint, = got
    wint_pad = jnp.pad(wint.reshape(WIN_COLS, D), ((0, WIN_PAD - WIN_COLS), (0, 0)))

    pw = WIN_PAD // 3
    rider = comm.gather(f2, part=(0, 1)) if comm else None
    proj, got = _ridden(_mm(
        "in_proj", [n2, wint_pad], NT, (S // TS, 3),
        [pl.BlockSpec((TS, D), lambda i, j: (i, 0)), pl.BlockSpec((pw, D), lambda i, j: (j, 0))],
        pl.BlockSpec((TS, pw), lambda i, j: (i, j)), _sds((S, WIN_PAD), F32), rider), rider)
    if comm:
        f2, = got
    qt = _rope_q(proj, cos, sin)
    k_rot, v_bf, kt, vt = _rope_kv(proj, cos, sin)
    kh, vh = _heads(k_rot, NKV), _heads(v_bf, NKV)
    bias = _bias_table()
    rider = comm.gather(f2, wout, part=(1, 2), at=13) if comm else None
    (ot, lse, mixed), got = _ridden(_attn_fwd(qt, kh, vt, bias, rider), rider)
    if comm:
        f2, wout = got
    w2 = _FfnW(f2, 0, f2, 2)
    wout = wout.reshape(2 * D, D)
    xbc, conv_y = _conv_fwd(proj, convw, convb)
    y, mixed, hprev = _ssd_fwd(xbc, proj, dtb, alog, dskip_l, sp["ssm_norm"], mixed)
    tail, o_specs, o_shapes = _tail_postres(TS, x1, sp["mix_post_norm"], 1.0, sp["ffn2_pre_norm"])
    h2, x2, n3 = _mm("out_proj", [mixed, wout], NN, (S // TS,),
                     [pl.BlockSpec((TS, 2 * D), lambda i: (i, 0)), pl.BlockSpec((2 * D, D), lambda i: (0, 0))],
                     o_specs, o_shapes, None, tail)

    fg2, fu2, act2 = _ffn_up("ffn2_up", n3, w2)
    dy, dh3, dp3, loss = _ffn_down(
        "ffn2_down", act2, w2, lambda rows: _tail_final(rows, x2, sp["ffn2_post_norm"], tgt, 0.5))

    dgate2, dup2 = _ffn_dact("ffn2_dact", dh3, w2, fg2, fu2)
    dws2 = [_ffn_dw("ffn2_dwg", dgate2, n3), _ffn_dw("ffn2_dwu", dup2, n3), _ffn_dw("ffn2_dwd", act2, dh3)]
    dx2, dh2, dg3, dp2 = _ffn_dn(
        "ffn2_dn", dgate2, dup2, w2,
        lambda rows: _tail_mid_bwd(rows, dy, x2, sp["ffn2_pre_norm"], h2, sp["mix_post_norm"], 1.0))

    dyn, dot_, delta = _out_proj_dx(dh2, wout, ot)
    dwout = _mm("out_proj_dw", [mixed, dh2], TN, (2,),
                [pl.BlockSpec((S, D), lambda m: (0, m)), pl.BlockSpec((S, D), lambda m: (0, 0))],
                pl.BlockSpec((D, D), lambda m: (m, 0)), _sds((2 * D, D), BF16))
    dwout = dwout.reshape(NSH, 2 * D // NSH, D)

    def riding(tag, names, ps, call, theirs=None, rows=None):
        rider = None
        if comm:
            rider = comm.rest_rider(tag) if names is None else comm.reduce_rider(tag, names, ps, theirs, rows)
        res, got = _ridden(call(rider), rider)
        if comm:
            comm.landed(tag, got)
        return res

    rider = _to_sibling_rider(dws2 + [dwout]) if comm else None
    (dxbc, dproj, ddt, dssm, dsc), theirs = _ridden(
        _ssd_bwd(dyn, y, xbc, proj, hprev, dtb, alog, dskip_l, sp["ssm_norm"], rider), rider)
    dproj, dcw8, dcb = _conv_bwd(dxbc, conv_y, proj, convw, dproj)
    dqt, dkh, dvh = riding("a", BIG[3:6] + ("w_out",), dws2 + [dwout], lambda rider: _attn_bwd(
        qt, kh, kt, vh, dot_, lse, delta, bias, rider), theirs)
    dproj = _rope_dq(dqt, cos, sin, dproj)
    dproj = _rope_dkv(dkh, dvh, cos, sin, dproj)
    dproj = lax.dynamic_update_slice(dproj, ddt, (0, COL_DT))
    dwint = _mm("in_proj_dw", [dproj, n2], TN, (3,),
                [pl.BlockSpec((S, pw), lambda j: (0, j)), pl.BlockSpec((S, D), lambda j: (0, 0))],
                pl.BlockSpec((pw, D), lambda j: (j, 0)), _sds((WIN_PAD, D), BF16))
    dwint = dwint[:WIN_COLS].reshape(NSH, WIN_SH, D)

    tail, o_specs, o_shapes = _tail_mid_bwd(TS, dx2, x1, sp["mix_pre_norm"], h1, sp["ffn1_post_norm"], 0.5)
    dx1, dh1, dg2, dp1 = riding("b", ("w_in",), [dwint], lambda rider: _mm(
        "in_proj_dx", [dproj, wint_pad], NN, (S // TS,),
        [pl.BlockSpec((TS, WIN_PAD), lambda i: (i, 0)), pl.BlockSpec((WIN_PAD, D), lambda i: (0, 0))],
        o_specs, o_shapes, rider, tail), rows=W_IN_FIRST)

    dwd1 = riding("b", None, None, lambda rider: _ffn_dw("ffn1_dwd", act1, dh1, rider))
    dgate1, dup1 = riding("d", BIG[2:3], [dwd1], lambda rider: _ffn_dact("ffn1_dact", dh1, w1, fg1, fu1, rider),
                          rows=FS // 2)
    dwg1 = riding("d", None, None, lambda rider: _ffn_dw("ffn1_dwg", dgate1, n1, rider))
    dwu1 = _ffn_dw("ffn1_dwu", dup1, n1)
    grad_x, dg1 = riding("g", BIG[0:2], [dwg1, dwu1], lambda rider: _ffn_dn(
        "ffn1_dn", dgate1, dup1, w1, lambda rows: _tail_first_bwd(rows, dx1, x, sp["ffn1_pre_norm"]), rider))
    dws1 = [dwg1, dwu1, dwd1]

    small = jnp.concatenate([
        dg1[0], dp1[0], dg2[0], dssm[0], dp2[0], dg3[0], dp3[0], dcb[0],
        dsc[0, :16], dsc[1, :16], dsc[2, :16], dcw8[:CONV_K].reshape(-1), loss[0, :1]])
    small = jnp.pad(small, (0, SMALL_LEN - small.shape[0])).reshape(SMALL_ROWS, D)
    if comm is None:
        return grad_x, dws1 + dws2 + [dwint, dwout], small
    return (grad_x,) + comm.finish(small)


WEIGHTS = ("ffn1_pre_norm", "ffn1_w_gate", "ffn1_w_up", "ffn1_w_down", "ffn1_post_norm", "mix_pre_norm", "w_in",
           "conv_w", "conv_b", "dt_bias", "a_log", "d_skip", "ssm_norm", "w_out", "mix_post_norm", "ffn2_pre_norm",
           "ffn2_w_gate", "ffn2_w_up", "ffn2_w_down", "ffn2_post_norm")
BIG = ("ffn1_w_gate", "ffn1_w_up", "ffn1_w_down", "ffn2_w_gate", "ffn2_w_up", "ffn2_w_down", "w_in", "w_out")
TRANSPOSED = ("ffn1_w_gate", "ffn1_w_up", "ffn2_w_gate", "ffn2_w_up", "w_in")
SMALL_ORDER = SMALL_1K + ("conv_b",) + SMALL_16
CONVW_SH = CONV_C // NSH


def _shard2d(t, name):
    return t[0].T if name in TRANSPOSED else t[0]


def _unshard2d(t, name):
    return (t.T if name in TRANSPOSED else t)[None]


def _rows3d(t):
    return t.transpose(2, 0, 1)


def _pack_small(d, prefix, shard_of_convw):
    flat = jnp.concatenate([d[prefix + n][0] for n in SMALL_ORDER] + [shard_of_convw.reshape(-1)])
    return jnp.pad(flat, (0, SMALL_LEN - flat.shape[0])).reshape(SMALL_ROWS, D)


def _unpack_small(block, like):
    flat = block.reshape(-1)
    out, off = {}, 0
    for n in SMALL_ORDER:
        size = like[n].shape[1]
        out[n] = flat[off:off + size].reshape(1, size)
        off += size
    out["conv_w"] = flat[off:off + CONV_K * CONVW_SH].reshape(1, CONV_K, CONVW_SH)
    return out


def kernel(x, positions, ffn1_pre_norm, ffn1_w_gate, ffn1_w_up, ffn1_w_down, ffn1_post_norm, mix_pre_norm, w_in, conv_w, conv_b, dt_bias, a_log, d_skip, ssm_norm, w_out, mix_post_norm, ffn2_pre_norm, ffn2_w_gate, ffn2_w_up, ffn2_w_down, ffn2_post_norm, loss_target, m_ffn1_pre_norm, m_ffn1_w_gate, m_ffn1_w_up, m_ffn1_w_down, m_ffn1_post_norm, m_mix_pre_norm, m_w_in, m_conv_w, m_conv_b, m_dt_bias, m_a_log, m_d_skip, m_ssm_norm, m_w_out, m_mix_post_norm, m_ffn2_pre_norm, m_ffn2_w_gate, m_ffn2_w_up, m_ffn2_w_down, m_ffn2_post_norm, v_ffn1_pre_norm, v_ffn1_w_gate, v_ffn1_w_up, v_ffn1_w_down, v_ffn1_post_norm, v_mix_pre_norm, v_w_in, v_conv_w, v_conv_b, v_dt_bias, v_a_log, v_d_skip, v_ssm_norm, v_w_out, v_mix_post_norm, v_ffn2_pre_norm, v_ffn2_w_gate, v_ffn2_w_up, v_ffn2_w_down, v_ffn2_post_norm):
    given = dict(locals())
    xi, yi = lax.axis_index("x"), lax.axis_index("y")

    comm = _Comm()
    big = {p + n: _shard2d(given[p + n], n) for n in BIG for p in ("", "m_", "v_")}
    gu1 = _cast_stack("cast_ffn1_gate_up", comm.shard, [big[n] for n in BIG[0:2]], 176, D)

    sp = {n: given[n] for n in SMALL_ORDER}
    grad_x, big_grads, small = _local_step(
        x[0], positions[0], loss_target[0], sp, gu1, [big[BIG[2]]], [big[n] for n in BIG[3:6]], [big["w_in"]],
        [big["w_out"]], conv_w[0], comm)

    tot = small.reshape(-1)
    loss = tot[OFF_LOSS]
    small_grads, off = {}, 0
    for n in SMALL_ORDER:
        size = given[n].shape[1]
        small_grads[n] = tot[off:off + size].reshape(1, size)
        off += size
    dconvw = tot[OFF_CONVW:OFF_CONVW + CONV_K * CONV_C].reshape(CONV_K, NSH, CONVW_SH)
    dconvw = lax.dynamic_index_in_dim(dconvw, 2 * xi + yi, axis=1, keepdims=False)
    small_grads["conv_w"] = dconvw.reshape(1, CONV_K, CONVW_SH)

    upd = {}
    for names, tr in ((BIG[0:3], 176), (BIG[3:6], 176), (BIG[7:8], 256)):
        res = _adamw("adamw_" + names[0], [big[n] for n in names], [big_grads[n] for n in names],
                     [big["m_" + n] for n in names], [big["v_" + n] for n in names], tr, D)
        for n, r in zip(names, res):
            upd[n] = tuple(_unshard2d(t, n) for t in r)
    g_win = big_grads["w_in"].reshape(WIN_SH, 1, D)
    res, = _adamw("adamw_w_in", [_rows3d(w_in)], [g_win], [_rows3d(m_w_in)], [_rows3d(v_w_in)], WIN_SH // 4, D)
    upd["w_in"] = tuple(t.transpose(1, 2, 0) for t in res)
    (dl, m2, v2, _), = _adamw(
        "adamw_small", [_pack_small(given, "", conv_w[0])], [_pack_small(small_grads, "", dconvw)],
        [_pack_small(given, "m_", m_conv_w[0])], [_pack_small(given, "v_", v_conv_w[0])], SMALL_ROWS, D)
    dl, m2, v2 = (_unpack_small(t, given) for t in (dl, m2, v2))
    for n in SMALL_ORDER + ("conv_w",):
        upd[n] = (dl[n], m2[n], v2[n], small_grads[n])

    return (loss, grad_x[None], *[upd[n][3] for n in WEIGHTS], *[upd[n][0] for n in WEIGHTS],
            *[upd[n][1] for n in WEIGHTS], *[upd[n][2] for n in WEIGHTS])
```

```python
import functools
import typing

import jax
import jax.numpy as jnp
from jax import lax
from jax.experimental import pallas as pl
from jax.experimental.pallas import tpu as pltpu

F32 = jnp.float32
BF16 = jnp.bfloat16

S = 2048
D = 1024
FF = 2816
NSH = 4
FS = FF // NSH
HALF = D // 2
HD = 64
NKV = 4
NQ_PER_KV = 4
KVW = NKV * HD
QCOLS = NQ_PER_KV * HD
CONV_C = 1536
CONV_K = 4
SSM_W = 1024
NST = 128
NCH = S // 128
WIN_COLS = 4112
WIN_SH = WIN_COLS // NSH
W_IN_FIRST = 768
WIN_PAD = 4224
COL_DT = 4096
EPS = 1e-6
NEG = -1e30

ADAM_LR = 0.001
ADAM_B1 = 0.9
ADAM_B2 = 0.999
ADAM_EPS = 1e-08
ADAM_WD = 0.01
ADAM_STEP = 10

VMEM_LIMIT = 56 * 1024 * 1024
TS = 512
TR = 256

NN = (((1,), (0,)), ((), ()))
NT = (((1,), (1,)), ((), ()))
TN = (((0,), (0,)), ((), ()))
MESH = pl.DeviceIdType.MESH


def _cparams(*sem):
    return pltpu.CompilerParams(dimension_semantics=sem, vmem_limit_bytes=VMEM_LIMIT)


def _dot(a, b, dims):
    return lax.dot_general(a.astype(BF16), b.astype(BF16), dims, preferred_element_type=F32)


def _bf16_pieces(v):
    hi = v.astype(BF16)
    rest = v - hi.astype(F32)
    mid = rest.astype(BF16)
    return hi, mid, (rest - mid.astype(F32)).astype(BF16)


def _dot_exact(a, b, ones="a"):
    if ones == "a":
        sel = a.astype(BF16)
        parts = [lax.dot_general(sel, p, NN, preferred_element_type=F32) for p in _bf16_pieces(b)]
    else:
        sel = b.astype(BF16)
        parts = [lax.dot_general(p, sel, NN, preferred_element_type=F32) for p in _bf16_pieces(a)]
    return (parts[2] + parts[1]) + parts[0]


def _sigmoid(v):
    return 1.0 / (1.0 + jnp.exp(-v))


class _Rider(typing.NamedTuple):
    operands: list
    out_shapes: list
    aliases: dict
    sems: list
    start: typing.Callable
    finish: typing.Callable
    between: typing.Callable = None
    at: int = None


def _call(body, name, grid, in_specs, out_specs, out_shape, operands, scratch=(), sem=(), rider=None, prefetch=0):
    multi = isinstance(out_shape, (list, tuple))

    def launch(kernel, in_specs, out_specs, out_shape, scratch, aliases, sem, args):
        if prefetch:
            how = dict(grid_spec=pltpu.PrefetchScalarGridSpec(
                num_scalar_prefetch=prefetch, grid=grid, in_specs=in_specs, out_specs=out_specs,
                scratch_shapes=scratch))
        else:
            how = dict(grid=grid, in_specs=in_specs, out_specs=out_specs, scratch_shapes=scratch)
        return pl.pallas_call(kernel, name=name, out_shape=out_shape, input_output_aliases=aliases,
                              compiler_params=_cparams(*sem), **how)(*args)

    if rider is None:
        return launch(body, in_specs, out_specs, out_shape, list(scratch), {}, sem, operands)
    outs = list(out_shape) if multi else [out_shape]
    ospecs = list(out_specs) if multi else [out_specs]
    n_in, n_out, n_scr = len(operands) - prefetch, len(outs), len(scratch)
    ri, ro = len(rider.operands), len(rider.out_shapes)

    def wrapped(*refs):
        scalars, refs = refs[:prefetch], refs[prefetch:]
        o0 = n_in + ri
        s0 = o0 + n_out + ro
        rin, rout, rsem = refs[n_in:o0], refs[o0 + n_out:s0], refs[s0 + n_scr:]
        ids = [pl.program_id(a) for a in range(len(grid))]
        first = functools.reduce(jnp.logical_and, [i == 0 for i in ids])
        last = functools.reduce(jnp.logical_and, [i == g - 1 for i, g in zip(ids, grid)])

        @pl.when(first)
        def _():
            rider.start(rin, rout, rsem)

        if rider.between is not None:
            steps = functools.reduce(lambda a, b: a * b, grid)
            step = functools.reduce(lambda a, ig: a * ig[1] + ig[0], zip(ids, grid), 0)

            @pl.when(step == (2 * steps // 3 if rider.at is None else rider.at))
            def _():
                rider.between(rin, rout, rsem)

        body(*scalars, *refs[:n_in], *refs[o0:o0 + n_out], *refs[s0:s0 + n_scr])

        @pl.when(last)
        def _():
            rider.finish(rin, rout, rsem)

    hbm = pl.BlockSpec(memory_space=pl.ANY)
    res = launch(wrapped, list(in_specs) + [hbm] * ri, ospecs + [hbm] * ro, outs + list(rider.out_shapes),
                 list(scratch) + list(rider.sems),
                 {prefetch + n_in + k: n_out + v for k, v in rider.aliases.items()},
                 ("arbitrary",) * len(grid), (*operands, *rider.operands))
    main = list(res[:n_out])
    return (main if multi else main[0]), list(res[n_out:])


class _Tail(typing.NamedTuple):
    fn: typing.Callable
    operands: list
    in_specs: list


def _mm(name, operands, dims, grid, in_specs, o_spec, out_shape, rider=None, tail=None):
    npairs = len(operands) // 2
    extra = [] if tail is None else list(tail.operands)
    nin = 2 * npairs + len(extra)

    def body(*refs):
        t = None
        for i in range(npairs):
            a, b = refs[2 * i], refs[2 * i + 1]
            parts = [(a[s], b[s]) for s in range(a.shape[0])] if len(a.shape) == 3 else [(a[...], b[...])]
            for pa, pb in parts:
                d = _dot(pa, pb, dims)
                t = d if t is None else t + d
        if tail is None:
            refs[nin][...] = t.astype(refs[nin].dtype)
        else:
            tail.fn(t, refs[2 * npairs:nin], refs[nin:])

    sem = ("parallel" if tail is None else "arbitrary",) * len(grid)
    specs = list(in_specs) + ([] if tail is None else list(tail.in_specs))
    return _call(body, name, grid, specs, o_spec, out_shape, list(operands) + extra, (), sem, rider)


class _FfnW(typing.NamedTuple):
    gu: jax.Array
    g0: int
    dn: jax.Array
    d0: int


def _ffn_up(name, n, w, rider=None):
    def body(n_ref, wg_ref, wu_ref, fg_ref, fu_ref, a_ref):
        nb = n_ref[...]
        g = _dot(nb, wg_ref[...], NT)
        u = _dot(nb, wu_ref[...], NT)
        sg = _sigmoid(g)
        silu = g * sg
        fg_ref[...] = (u * (sg * (1.0 + g * (1.0 - sg)))).astype(BF16)
        fu_ref[...] = silu.astype(BF16)
        a_ref[...] = (silu * u).astype(BF16)

    out = jax.ShapeDtypeStruct((NSH, S, FS), BF16)
    ospec = pl.BlockSpec((None, TS, FS), lambda s, i: (s, i, 0))
    return _call(
        body, name, (NSH, S // TS),
        [pl.BlockSpec((TS, D), lambda s, i: (i, 0)),
         pl.BlockSpec((None, None, FS, D), lambda s, i: (s, w.g0, 0, 0)),
         pl.BlockSpec((None, None, FS, D), lambda s, i: (s, w.g0 + 1, 0, 0))],
        [ospec, ospec, ospec], [out, out, out], (n, w.gu, w.gu), sem=("parallel", "parallel"), rider=rider)


def _ffn_dact(name, dh, w, fgate, fup, rider=None):
    def body(dh_ref, wd_ref, fg_ref, fu_ref, dg_ref, du_ref):
        da = _dot(dh_ref[...], wd_ref[...], NT)
        dg_ref[...] = (da * fg_ref[...].astype(F32)).astype(BF16)
        du_ref[...] = (da * fu_ref[...].astype(F32)).astype(BF16)

    out = jax.ShapeDtypeStruct((NSH, S, FS), BF16)
    aspec = pl.BlockSpec((None, TS, FS), lambda s, i: (s, i, 0))
    return _call(
        body, name, (NSH, S // TS),
        [pl.BlockSpec((TS, D), lambda s, i: (i, 0)),
         pl.BlockSpec((None, None, FS, D), lambda s, i: (s, w.d0, 0, 0)), aspec, aspec],
        [aspec, aspec], [out, out], (dh, w.dn, fgate, fup), sem=("parallel", "parallel"), rider=rider)


def _rstd(v):
    return lax.rsqrt(jnp.mean(v * v, axis=-1, keepdims=True) + EPS)


def _row_spec():
    return pl.BlockSpec((TR, D), lambda i: (i, 0))


def _vec_spec():
    return pl.BlockSpec((1, D), lambda i: (0, 0))


def _acc_rows(ref, v):
    @pl.when(pl.program_id(0) == 0)
    def _():
        ref[...] = jnp.zeros_like(ref)
    ref[...] += jnp.sum(v, axis=0, keepdims=True)


def _prenorm(name, x, g):
    def body(x_ref, g_ref, n_ref):
        xv = x_ref[...]
        n_ref[...] = (xv * _rstd(xv) * g_ref[...]).astype(BF16)

    return pl.pallas_call(
        body, name=name, grid=(S // TR,), in_specs=[_row_spec(), _vec_spec()], out_specs=_row_spec(),
        out_shape=jax.ShapeDtypeStruct((S, D), BF16), compiler_params=_cparams("parallel"),
    )(x, g)


ENTRY_STEPS = 4


def _prenorm_casts(name, x, g, slot, groups, rider):
    def body(s_ref, x_ref, g_ref, *refs):
        ins, outs = refs[:len(refs) - len(groups) - 1], refs[len(refs) - len(groups) - 1:]
        xv = x_ref[...]
        outs[0][...] = (xv * _rstd(xv) * g_ref[...]).astype(BF16)
        at = 0
        for (arrs, _), out in zip(groups, outs[1:]):
            for k in range(len(arrs)):
                out[k] = ins[at + k][...].astype(BF16)
            at += len(arrs)

    rows = pl.BlockSpec((S // ENTRY_STEPS, D), lambda i, sr: (i, 0))
    in_specs, out_specs, out_shapes = [rows, pl.BlockSpec((1, D), lambda i, sr: (0, 0))], [rows], [_rows_bf16()]
    for arrs, axis in groups:
        r, c = arrs[0].shape
        if axis == 0:
            blk, at, at_out = (r // ENTRY_STEPS, c), (lambda i, sr: (i, 0)), (lambda i, sr: (sr[0], 0, i, 0))
        else:
            blk, at, at_out = (r, c // ENTRY_STEPS), (lambda i, sr: (0, i)), (lambda i, sr: (sr[0], 0, 0, i))
        in_specs += [pl.BlockSpec(blk, at)] * len(arrs)
        out_specs.append(pl.BlockSpec((None, len(arrs)) + blk, at_out))
        out_shapes.append(jax.ShapeDtypeStruct((NSH, len(arrs), r, c), BF16))
    return _call(body, name, (ENTRY_STEPS,), in_specs, out_specs, out_shapes,
                 [slot, x, g] + [a for arrs, _ in groups for a in arrs], rider=rider, prefetch=1)


def _rows_spec(rows):
    return pl.BlockSpec((rows, D), lambda i: (i, 0))


def _rows_f32():
    return jax.ShapeDtypeStruct((S, D), F32)


def _rows_bf16():
    return jax.ShapeDtypeStruct((S, D), BF16)


def _vec_f32():
    return jax.ShapeDtypeStruct((1, D), F32)


def _tail_postres(rows, x, p, alpha, gnext):
    def fn(h, ins, outs):
        x_ref, p_ref, g_ref = ins
        h_ref, xo_ref, n_ref = outs
        h_ref[...] = h
        xo = x_ref[...] + alpha * (h * _rstd(h) * p_ref[...])
        xo_ref[...] = xo
        n_ref[...] = (xo * _rstd(xo) * g_ref[...]).astype(BF16)

    rs = _rows_spec(rows)
    return (_Tail(fn, [x, p, gnext], [rs, _vec_spec(), _vec_spec()]), [rs, rs, rs],
            [_rows_f32(), _rows_f32(), _rows_bf16()])


def _tail_final(rows, x, p, tgt, alpha):
    def fn(h, ins, outs):
        x_ref, p_ref, t_ref = ins
        dy_ref, dh_ref, dp_ref, loss_ref = outs
        r = _rstd(h)
        hn = h * r
        pv = p_ref[...]
        e = x_ref[...] + alpha * (hn * pv) - t_ref[...]
        dy = e * (1.0 / D)
        dy_ref[...] = dy
        du = alpha * dy * pv
        dh_ref[...] = (r * (du - hn * jnp.mean(du * hn, axis=-1, keepdims=True))).astype(BF16)
        _acc_rows(dp_ref, alpha * dy * hn)
        part = 0.5 * jnp.sum(jnp.mean(e * e, axis=-1, keepdims=True), axis=0, keepdims=True)
        _acc_rows(loss_ref, jnp.broadcast_to(part, (1, 128)))

    rs = _rows_spec(rows)
    return (_Tail(fn, [x, p, tgt], [rs, _vec_spec(), rs]),
            [rs, rs, _vec_spec(), pl.BlockSpec((1, 128), lambda i: (0, 0))],
            [_rows_f32(), _rows_bf16(), _vec_f32(), jax.ShapeDtypeStruct((1, 128), F32)])


def _norm_bwd(dn, xv, g_ref, dg_ref):
    r = _rstd(xv)
    xn = xv * r
    dng = dn * g_ref[...]
    _acc_rows(dg_ref, dn * xn)
    return r * (dng - xn * jnp.mean(dng * xn, axis=-1, keepdims=True))


def _tail_mid_bwd(rows, dres, x, g, h, p, alpha):
    def fn(dn, ins, outs):
        dr_ref, x_ref, g_ref, h_ref, p_ref = ins
        dx_ref, dh_ref, dg_ref, dp_ref = outs
        dx = dr_ref[...] + _norm_bwd(dn, x_ref[...], g_ref, dg_ref)
        dx_ref[...] = dx
        hv = h_ref[...]
        r = _rstd(hv)
        hn = hv * r
        du = alpha * dx * p_ref[...]
        dh_ref[...] = (r * (du - hn * jnp.mean(du * hn, axis=-1, keepdims=True))).astype(BF16)
        _acc_rows(dp_ref, alpha * dx * hn)

    rs = _rows_spec(rows)
    return (_Tail(fn, [dres, x, g, h, p], [rs, rs, _vec_spec(), rs, _vec_spec()]),
            [rs, rs, _vec_spec(), _vec_spec()], [_rows_f32(), _rows_bf16(), _vec_f32(), _vec_f32()])


def _tail_first_bwd(rows, dres, x, g):
    def fn(dn, ins, outs):
        dr_ref, x_ref, g_ref = ins
        dx_ref, dg_ref = outs
        dx_ref[...] = dr_ref[...] + _norm_bwd(dn, x_ref[...], g_ref, dg_ref)

    rs = _rows_spec(rows)
    return (_Tail(fn, [dres, x, g], [rs, rs, _vec_spec()]), [rs, _vec_spec()], [_rows_f32(), _vec_f32()])


def _rotate(t, c128, s128, sign, scale):
    width = t.shape[1]
    c = jnp.tile(c128, (1, width // 128))
    sn = jnp.tile(s128, (1, width // 128))
    lane = lax.broadcasted_iota(jnp.int32, t.shape, 1) & (HD - 1)
    rot = jnp.where(lane < HD // 2, -pltpu.roll(t, width - HD // 2, 1), pltpu.roll(t, HD // 2, 1))
    return (t * c + sign * (rot * sn)) * scale


def _rows_to_blocks(y):
    out = []
    for j in range(NKV):
        yt = y[:, QCOLS * j:QCOLS * (j + 1)].T
        out.append(jnp.concatenate([yt[HD * g:HD * (g + 1)] for g in range(NQ_PER_KV)], axis=1))
    return out


def _blocks_to_rows(blocks):
    cols = []
    for b in blocks:
        stacked = jnp.concatenate([b[:, 128 * g:128 * (g + 1)] for g in range(NQ_PER_KV)], axis=0)
        cols.append(stacked.T)
    return jnp.concatenate(cols, axis=1)


def _rope_q(proj, cos, sin):
    def body(t_ref, c_ref, s_ref, o_ref):
        y = _rotate(t_ref[...], c_ref[...], s_ref[...], 1.0, HD ** -0.5)
        for j, blk in enumerate(_rows_to_blocks(y)):
            o_ref[j] = blk.astype(BF16)

    return pl.pallas_call(
        body, name="rope_q", grid=(NCH,),
        in_specs=[pl.BlockSpec((128, D), lambda i: (i, 0)),
                  pl.BlockSpec((128, 128), lambda i: (i, 0)), pl.BlockSpec((128, 128), lambda i: (i, 0))],
        out_specs=pl.BlockSpec((NKV, None, HD, QROWS), lambda i: (0, i, 0, 0)),
        out_shape=jax.ShapeDtypeStruct((NKV, NCH, HD, QROWS), BF16), compiler_params=_cparams("parallel"),
    )(proj, cos, sin)


def _rope_dq(dqt, cos, sin, dproj):
    def body(t_ref, c_ref, s_ref, buf_ref, o_ref):
        t = _blocks_to_rows([t_ref[j] for j in range(NKV)])
        o_ref[...] = _rotate(t, c_ref[...], s_ref[...], -1.0, HD ** -0.5).astype(BF16)

    return pl.pallas_call(
        body, name="rope_dq", grid=(NCH,),
        in_specs=[pl.BlockSpec((NKV, None, HD, QROWS), lambda i: (0, i, 0, 0)),
                  pl.BlockSpec((128, 128), lambda i: (i, 0)), pl.BlockSpec((128, 128), lambda i: (i, 0)),
                  pl.BlockSpec(memory_space=pl.ANY)],
        out_specs=pl.BlockSpec((128, D), lambda i: (i, 0)),
        out_shape=jax.ShapeDtypeStruct(dproj.shape, BF16), input_output_aliases={3: 0},
        compiler_params=_cparams("parallel"),
    )(dqt, cos, sin, dproj)


def _rope_dkv(dkt, dvt, cos, sin, dproj):
    def body(k_ref, v_ref, c_ref, s_ref, buf_ref, o_ref):
        dk = jnp.concatenate([k_ref[j] for j in range(NKV)], axis=0).T
        dv = jnp.concatenate([v_ref[j] for j in range(NKV)], axis=0).T
        dk = _rotate(dk, c_ref[...], s_ref[...], -1.0, 1.0)
        o_ref[...] = jnp.concatenate([dk, dv], axis=1).astype(BF16)

    tspec = pl.BlockSpec((NKV, HD, 128), lambda i: (0, 0, i))
    return pl.pallas_call(
        body, name="rope_dkv", grid=(NCH,),
        in_specs=[tspec, tspec, pl.BlockSpec((128, 128), lambda i: (i, 0)), pl.BlockSpec((128, 128), lambda i: (i, 0)),
                  pl.BlockSpec(memory_space=pl.ANY)],
        out_specs=pl.BlockSpec((128, 2 * KVW), lambda i: (i, D // (2 * KVW))),
        out_shape=jax.ShapeDtypeStruct(dproj.shape, BF16), input_output_aliases={4: 0},
        compiler_params=_cparams("parallel"),
    )(dkt, dvt, cos, sin, dproj)


def _rope_kv(proj, cos, sin):
    def body(t_ref, c_ref, s_ref, k_ref, v_ref, kt_ref, vt_ref):
        t = t_ref[...]
        k = _rotate(t[:, :KVW], c_ref[...], s_ref[...], 1.0, 1.0).astype(BF16)
        v = t[:, KVW:].astype(BF16)
        k_ref[...] = k
        v_ref[...] = v
        kt, vt = k.astype(F32).T, v.astype(F32).T
        for j in range(NKV):
            kt_ref[j] = kt[HD * j:HD * (j + 1)].astype(BF16)
            vt_ref[j] = vt[HD * j:HD * (j + 1)].astype(BF16)

    rows = pl.BlockSpec((128, KVW), lambda i: (i, 0))
    tspec = pl.BlockSpec((NKV, HD, 128), lambda i: (0, 0, i))
    return pl.pallas_call(
        body, name="rope_kv", grid=(NCH,),
        in_specs=[pl.BlockSpec((128, 2 * KVW), lambda i: (i, D // (2 * KVW))),
                  pl.BlockSpec((128, 128), lambda i: (i, 0)), pl.BlockSpec((128, 128), lambda i: (i, 0))],
        out_specs=[rows, rows, tspec, tspec],
        out_shape=[jax.ShapeDtypeStruct((S, KVW), BF16)] * 2 + [jax.ShapeDtypeStruct((NKV, HD, S), BF16)] * 2,
        compiler_params=_cparams("parallel"),
    )(proj, cos, sin)


QROWS = NQ_PER_KV * 128


NBIAS = NCH + 1
KV_PER_STEP = 4


def _bias_table():
    db = lax.broadcasted_iota(jnp.int32, (NBIAS, 128, QROWS), 0) - 1
    ki = lax.broadcasted_iota(jnp.int32, (NBIAS, 128, QROWS), 1)
    qi = lax.broadcasted_iota(jnp.int32, (NBIAS, 128, QROWS), 2) & 127
    d = db * 128 + qi - ki
    cnt = ((d <= 128).astype(F32) + (((d & 3) == 0) & (d <= 512)).astype(F32) + ((d & 15) == 0).astype(F32))
    return jnp.where((d >= 0) & (cnt > 0.0), jnp.log(jnp.maximum(cnt, 1.0)), NEG)


def _attn_fwd(qt, kh, vt, bias, rider=None):
    def body(q_ref, k_ref, v_ref, b_ref, o_ref, lse_ref, rows_ref, m_ref, l_ref, acc_ref):
        qb = pl.program_id(1)
        m_ref[...] = jnp.full_like(m_ref, NEG)
        l_ref[...] = jnp.zeros_like(l_ref)
        acc_ref[...] = jnp.zeros_like(acc_ref)

        def keys(off, size, bias_):
            for h in range(KV_PER_STEP):
                m = m_ref[h]
                s = _dot(k_ref[h, pl.ds(off, size), :], q_ref[h], NN) + bias_
                m_new = jnp.maximum(m, jnp.max(s, axis=0, keepdims=True))
                p = jnp.exp(s - m_new)
                a = jnp.exp(m - m_new)
                m_ref[h] = m_new
                l_ref[h] = a * l_ref[h] + jnp.sum(p, axis=0, keepdims=True)
                acc_ref[h] = a * acc_ref[h] + _dot(v_ref[h, :, pl.ds(off, size)], p, NN)

        def blocks(first, count):
            bias_ = jnp.concatenate([b_ref[qb - first - j + 1] for j in range(count)], axis=0)
            keys(pl.multiple_of(first * 128, 128), 128 * count, bias_)

        nkb = qb + 1
        @pl.loop(0, nkb // 4)
        def _(i):
            blocks(4 * i, 4)

        @pl.when(nkb % 4 >= 2)
        def _():
            blocks(nkb // 4 * 4, 2)

        @pl.when(nkb % 2 == 1)
        def _():
            blocks(qb, 1)

        outs = []
        for h in range(KV_PER_STEP):
            outs.append(acc_ref[h] / l_ref[h])
            o_ref[h] = outs[h]
            lse_ref[h] = m_ref[h] + jnp.log(l_ref[h])
        rows_ref[...] = _blocks_to_rows(outs).astype(BF16)

    kvs = KV_PER_STEP
    qspec = pl.BlockSpec((kvs, None, HD, QROWS), lambda j, i: (j, i, 0, 0))
    return _call(
        body, "attn_fwd", (NKV // kvs, NCH),
        [qspec, pl.BlockSpec((kvs, S, HD), lambda j, i: (j, 0, 0)),
         pl.BlockSpec((kvs, HD, S), lambda j, i: (j, 0, 0)),
         pl.BlockSpec((NBIAS, 128, QROWS), lambda j, i: (0, 0, 0))],
        [qspec, pl.BlockSpec((kvs, None, 1, QROWS), lambda j, i: (j, i, 0, 0)),
         pl.BlockSpec((128, QCOLS * kvs), lambda j, i: (i, j))],
        [jax.ShapeDtypeStruct((NKV, NCH, HD, QROWS), F32), jax.ShapeDtypeStruct((NKV, NCH, 1, QROWS), F32),
         jax.ShapeDtypeStruct((S, 2 * D), BF16)],
        (qt, kh, vt, bias),
        [pltpu.VMEM((kvs, 1, QROWS), F32), pltpu.VMEM((kvs, 1, QROWS), F32), pltpu.VMEM((kvs, HD, QROWS), F32)],
        ("parallel", "parallel"), rider)


def _attn_bwd(qt, kh, kt, vh, dot_, lse, delta, bias, rider=None):
    def body(qt_ref, k_ref, kt_ref, v_ref, dot_ref, lse_ref, dl_ref, b_ref, dq_ref, dk_ref, dv_ref):
        kp = pl.program_id(1)

        @pl.when(kp == 0)
        def _():
            dq_ref[...] = jnp.zeros_like(dq_ref)

        dk_ref[...] = jnp.zeros_like(dk_ref)
        dv_ref[...] = jnp.zeros_like(dv_ref)

        @pl.loop(2 * kp, NCH // 2)
        def _(j):
            for h in range(KV_PER_STEP):
                k, kt_, v = k_ref[h], kt_ref[h], v_ref[h]
                for qb in (2 * j, 2 * j + 1):
                    bias2 = jnp.concatenate([b_ref[jnp.maximum(qb - 4 * kp - t + 1, 0)] for t in range(4)], axis=0)
                    st = _dot(k, qt_ref[h, qb], NN) + bias2
                    pt = jnp.exp(st - lse_ref[h, qb])
                    dst = pt * (_dot(v, dot_ref[h, qb], NN) - dl_ref[h, qb])
                    dq_ref[h, qb] += _dot(kt_, dst, NN)
                    dk_ref[h] += _dot(qt_ref[h, qb], dst, NT)
                    dv_ref[h] += _dot(dot_ref[h, qb], pt, NT)

    kvs = KV_PER_STEP
    tspec = pl.BlockSpec((kvs, NCH, HD, QROWS), lambda j, i: (j, 0, 0, 0))
    kspec = pl.BlockSpec((kvs, 512, HD), lambda j, i: (j, i, 0))
    ktspec = pl.BlockSpec((kvs, HD, 512), lambda j, i: (j, 0, i))
    sspec = pl.BlockSpec((kvs, NCH, 1, QROWS), lambda j, i: (j, 0, 0, 0))
    return _call(
        body, "attn_bwd", (NKV // kvs, NCH // 4),
        [tspec, kspec, ktspec, kspec, tspec, sspec, sspec,
         pl.BlockSpec((NBIAS, 128, QROWS), lambda j, i: (0, 0, 0))],
        [tspec, ktspec, ktspec],
        [jax.ShapeDtypeStruct((NKV, NCH, HD, QROWS), F32),
         jax.ShapeDtypeStruct((NKV, HD, S), F32), jax.ShapeDtypeStruct((NKV, HD, S), F32)],
        (qt, kh, kt, vh, dot_, lse, delta, bias), sem=("parallel", "arbitrary"), rider=rider)


CONV_BLK = 256
CONV_COL0 = 1536 // CONV_BLK


CONV_ROWS = 128


def _conv_fwd(proj, convw, convb):
    trips = S // CONV_ROWS

    def body(u_ref, w_ref, b_ref, o_ref, y_ref):
        @pl.loop(0, trips)
        def _(c):
            t0 = pl.multiple_of(c * CONV_ROWS, CONV_ROWS)
            before = pl.multiple_of(jnp.maximum(t0 - 8, 0), 8)
            ext = jnp.concatenate([jnp.where(c == 0, 0.0, u_ref[pl.ds(before, 8), :]),
                                   u_ref[pl.ds(t0, CONV_ROWS), :]], axis=0)
            y = b_ref[...] + w_ref[CONV_K - 1:CONV_K, :] * ext[8:]
            for j in range(1, CONV_K):
                y = y + w_ref[CONV_K - 1 - j:CONV_K - j, :] * pltpu.roll(ext, j, 0)[8:]
            y_ref[pl.ds(t0, CONV_ROWS), :] = y
            o_ref[pl.ds(t0, CONV_ROWS), :] = y * _sigmoid(y)

    out = pl.BlockSpec((S, CONV_BLK), lambda i: (0, i))
    return pl.pallas_call(
        body, name="conv_fwd", grid=(CONV_C // CONV_BLK,),
        in_specs=[pl.BlockSpec((S, CONV_BLK), lambda i: (0, CONV_COL0 + i)),
                  pl.BlockSpec((CONV_K, CONV_BLK), lambda i: (0, i)),
                  pl.BlockSpec((1, CONV_BLK), lambda i: (0, i))],
        out_specs=[out, out], out_shape=[jax.ShapeDtypeStruct((S, CONV_C), F32)] * 2,
        compiler_params=_cparams("parallel"),
    )(proj, convw, convb)


def _conv_bwd(dact, ypre, proj, convw, dproj):
    trips = S // CONV_ROWS

    def body(da_ref, y_ref, u_ref, w_ref, buf_ref, du_ref, dw_ref, db_ref):
        dw_ref[...] = jnp.zeros_like(dw_ref)
        db_ref[...] = jnp.zeros_like(db_ref)
        r8 = lax.broadcasted_iota(jnp.int32, (8, CONV_BLK), 0)

        def dy_of(rows):
            y = y_ref[rows, :]
            sg = _sigmoid(y)
            return da_ref[rows, :] * (sg * (1.0 + y * (1.0 - sg)))

        @pl.loop(0, trips)
        def _(c):
            t0 = pl.multiple_of(c * CONV_ROWS, CONV_ROWS)
            after = pl.multiple_of(jnp.minimum(t0 + CONV_ROWS, S - 8), 8)
            ext = jnp.concatenate([dy_of(pl.ds(t0, CONV_ROWS)),
                                   jnp.where(c == trips - 1, 0.0, dy_of(pl.ds(after, 8)))], axis=0)
            u = u_ref[pl.ds(t0, CONV_ROWS), :]
            du, dw = None, jnp.zeros((8, CONV_BLK), F32)
            for j in range(CONV_K):
                dyj = (ext if j == 0 else pltpu.roll(ext, CONV_ROWS + 8 - j, 0))[:CONV_ROWS]
                term = w_ref[CONV_K - 1 - j:CONV_K - j, :] * dyj
                du = term if du is None else du + term
                dw = dw + jnp.where(r8 == CONV_K - 1 - j, jnp.sum(dyj * u, axis=0, keepdims=True), 0.0)
            du_ref[pl.ds(t0, CONV_ROWS), :] = du.astype(BF16)
            dw_ref[...] += dw
            db_ref[...] += jnp.sum(ext[:CONV_ROWS], axis=0, keepdims=True)

    return pl.pallas_call(
        body, name="conv_bwd", grid=(CONV_C // CONV_BLK,),
        in_specs=[pl.BlockSpec((S, CONV_BLK), lambda i: (0, i)), pl.BlockSpec((S, CONV_BLK), lambda i: (0, i)),
                  pl.BlockSpec((S, CONV_BLK), lambda i: (0, CONV_COL0 + i)),
                  pl.BlockSpec((CONV_K, CONV_BLK), lambda i: (0, i)), pl.BlockSpec(memory_space=pl.ANY)],
        out_specs=[pl.BlockSpec((S, CONV_BLK), lambda i: (0, CONV_COL0 + i)),
                   pl.BlockSpec((8, CONV_BLK), lambda i: (0, i)), pl.BlockSpec((1, CONV_BLK), lambda i: (0, i))],
        out_shape=[jax.ShapeDtypeStruct(dproj.shape, BF16), jax.ShapeDtypeStruct((8, CONV_C), F32),
                   jax.ShapeDtypeStruct((1, CONV_C), F32)],
        input_output_aliases={4: 0}, compiler_params=_cparams("parallel"),
    )(dact, ypre, proj, convw, dproj)


NPAIR = 8


def _ssd_scalars(dtr_ref, dtb_ref, alog_ref):
    z = dtr_ref[...] + dtb_ref[...]
    dt = jnp.maximum(z, 0.0) + jnp.log(1.0 + jnp.exp(-jnp.abs(z)))
    a = -jnp.exp(alog_ref[...])
    r = lax.broadcasted_iota(jnp.int32, (128, 128), 0)
    c = lax.broadcasted_iota(jnp.int32, (128, 128), 1)
    tri = (r >= c).astype(F32)
    cs = _dot_exact(tri, dt * a)
    return z, dt, a, cs, r, c


def _by_lane(cs, dt):
    head = lax.broadcasted_iota(jnp.int32, (128, SSM_W), 0)
    lane = lax.broadcasted_iota(jnp.int32, (128, SSM_W), 1)
    sel = (head == lane // HD).astype(F32)
    cs_l = _dot_exact(cs, sel, "b")
    last_l = cs_l[127:128, :]
    return sel, jnp.exp(cs_l), jnp.exp(last_l - cs_l), _dot_exact(dt, sel, "b")


def _pair_terms(cs, h1, h2):
    return (cs[:, h1:h1 + 1], cs[:, h2:h2 + 1],
            jnp.exp(cs[127:128, h1:h1 + 1]), jnp.exp(cs[127:128, h2:h2 + 1]))


def _gate_norm(y, zv, w):
    yg = y * (zv * _sigmoid(zv))
    outs, rs = [], []
    for g in range(2):
        blk = yg[:, 512 * g:512 * (g + 1)]
        r = lax.rsqrt(jnp.mean(blk * blk, axis=-1, keepdims=True) + EPS)
        outs.append(blk * r)
        rs.append(r)
    return jnp.concatenate(outs, axis=1), rs, yg


def _ssd_fwd(xbc, proj, dtb, alog, dskip_l, ssmw, mixed):
    def body(x_ref, b_ref, c_ref, dtr_ref, z_ref, dtb_ref, alog_ref, dsk_ref, w_ref, buf_ref,
             y_ref, yn_ref, hp_ref, h_ref):
        @pl.when(pl.program_id(0) == 0)
        def _():
            h_ref[...] = jnp.zeros_like(h_ref)

        _, dt, _, cs, r, c = _ssd_scalars(dtr_ref, dtb_ref, alog_ref)
        cst = cs.T
        causal = r >= c
        lo = c < HD
        _, e_all, dte_all, dt_all = _by_lane(cs, dt)
        hp_ref[...] = h_ref[...]
        for g in range(2):
            bg = b_ref[:, 128 * g:128 * (g + 1)]
            cg = c_ref[:, 128 * g:128 * (g + 1)]
            cb = _dot(cg, bg, NT)
            for j in range(4):
                pj = 4 * g + j
                h1, h2 = 2 * pj, 2 * pj + 1
                sl = slice(128 * pj, 128 * (pj + 1))
                xp = x_ref[:, sl]
                c1, c2, cd1, cd2 = _pair_terms(cs, h1, h2)
                e_l, dte_l = e_all[:, sl], dte_all[:, sl]
                xdt = xp * dt_all[:, sl]
                m1 = cb * jnp.exp(jnp.where(causal, c1 - cst[h1:h1 + 1, :], NEG))
                m2 = cb * jnp.exp(jnp.where(causal, c2 - cst[h2:h2 + 1, :], NEG))
                yd = jnp.where(lo, _dot(m1, xdt, NN), _dot(m2, xdt, NN))
                hp = h_ref[pj]
                yo = _dot(cg, hp, NT) * e_l
                st = _dot(xdt * dte_l, bg, TN)
                h_ref[pj] = hp * jnp.where(r < HD, cd1, cd2) + st
                y_ref[:, sl] = yd + yo + dsk_ref[:, sl] * xp
        yn, _, _ = _gate_norm(y_ref[...], z_ref[...], w_ref[...])
        yn_ref[...] = (yn * w_ref[...]).astype(BF16)

    return pl.pallas_call(
        body, name="ssd_fwd", grid=(NCH,),
        in_specs=[pl.BlockSpec((128, SSM_W), lambda i: (i, 0)),
                  pl.BlockSpec((128, 256), lambda i: (i, 4)), pl.BlockSpec((128, 256), lambda i: (i, 5)),
                  pl.BlockSpec((128, 128), lambda i: (i, COL_DT // 128)),
                  pl.BlockSpec((128, SSM_W), lambda i: (i, 3)),
                  pl.BlockSpec((1, 128), lambda i: (0, 0)), pl.BlockSpec((1, 128), lambda i: (0, 0)),
                  pl.BlockSpec((1, SSM_W), lambda i: (0, 0)), pl.BlockSpec((1, SSM_W), lambda i: (0, 0)),
                  pl.BlockSpec(memory_space=pl.ANY)],
        out_specs=[pl.BlockSpec((128, SSM_W), lambda i: (i, 0)), pl.BlockSpec((128, SSM_W), lambda i: (i, 1)),
                   pl.BlockSpec((None, NPAIR, 128, 128), lambda i: (i, 0, 0, 0))],
        out_shape=[jax.ShapeDtypeStruct((S, SSM_W), F32), jax.ShapeDtypeStruct(mixed.shape, BF16),
                   jax.ShapeDtypeStruct((NCH, NPAIR, 128, 128), F32)],
        scratch_shapes=[pltpu.VMEM((NPAIR, 128, 128), F32)],
        input_output_aliases={9: 1}, compiler_params=_cparams("arbitrary"),
    )(xbc, xbc, xbc, proj, proj, dtb, alog, dskip_l, ssmw, mixed)


def _ssd_bwd(dmixed, y, xbc, proj, hprev, dtb, alog, dskip_l, ssmw, rider=None):
    def body(dyn_ref, y_ref, x_ref, b_ref, c_ref, dtr_ref, z_ref, hp_ref, dtb_ref, alog_ref, dsk_ref, w_ref,
             dxbc_ref, dz_ref, ddt_ref, dw_ref, dsc_ref, g_ref):
        @pl.when(pl.program_id(0) == 0)
        def _():
            g_ref[...] = jnp.zeros_like(g_ref)
            dsc_ref[...] = jnp.zeros_like(dsc_ref)

        z, dt, a, cs, r, c = _ssd_scalars(dtr_ref, dtb_ref, alog_ref)
        cst = cs.T
        causal = r >= c
        lo = c < HD

        yv = y_ref[...]
        zv = z_ref[...]
        wv = w_ref[...]
        ygn, rs, yg = _gate_norm(yv, zv, wv)
        dyn = dyn_ref[...]
        _acc_rows(dw_ref, dyn * ygn)
        dynw = dyn * wv
        parts = []
        for g in range(2):
            sl = slice(512 * g, 512 * (g + 1))
            a_g, n_g = dynw[:, sl], ygn[:, sl]
            parts.append(rs[g] * (a_g - n_g * jnp.mean(a_g * n_g, axis=-1, keepdims=True)))
        dyg = jnp.concatenate(parts, axis=1)
        sz = _sigmoid(zv)
        dz_ref[...] = (dyg * yv * (sz * (1.0 + zv * (1.0 - sz)))).astype(BF16)
        dy_all = dyg * (zv * sz)

        dcs_cols = jnp.zeros((128, 128), F32)
        dcs_rows = jnp.zeros((128, 128), F32)
        sel, e_all, dte_all, dt_all = _by_lane(cs, dt)
        x_all, b_all, c_all, dsk_all = x_ref[...], b_ref[...], c_ref[...], dsk_ref[...]
        hp_all, g_all = hp_ref[...], g_ref[...]
        g_new, dx_parts, db_parts, dc_parts = [], [], [], []
        dyx_parts, ryo_parts, qx_parts, dxx_parts, gh_parts = [], [], [], [], []
        for g in range(2):
            bg = b_all[:, 128 * g:128 * (g + 1)]
            cg = c_all[:, 128 * g:128 * (g + 1)]
            cb = _dot(cg, bg, NT)
            dcb = jnp.zeros((128, 128), F32)
            db_acc = jnp.zeros((128, NST), F32)
            dc_acc = jnp.zeros((128, NST), F32)
            for j in range(4):
                pj = 4 * g + j
                h1, h2 = 2 * pj, 2 * pj + 1
                sl = slice(128 * pj, 128 * (pj + 1))
                xp = x_all[:, sl]
                dyp = dy_all[:, sl]
                c1, c2, cd1, cd2 = _pair_terms(cs, h1, h2)
                e_l, dte_l, dt_l = e_all[:, sl], dte_all[:, sl], dt_all[:, sl]
                xdt = xp * dt_l
                hp = hp_all[pj]
                gp = g_all[pj]
                dyx_parts.append(dyp * xp)
                dzs = dyp * e_l
                dc_acc = dc_acc + _dot(dzs, hp, NN)
                ryo_parts.append(dyp * (_dot(cg, hp, NT) * e_l))
                qm = _dot(bg, gp, NT)
                dxdt = qm * dte_l
                qx_parts.append(qm * xdt)
                db_acc = db_acc + _dot(xdt * dte_l, gp, NN)
                gh_parts.append(gp * hp)
                g_new.append(_dot(dzs, cg, TN) + jnp.where(r < HD, cd1, cd2) * gp)
                for hh, ch, msk in ((h1, c1, lo), (h2, c2, jnp.logical_not(lo))):
                    lm = jnp.exp(jnp.where(causal, ch - cst[hh:hh + 1, :], NEG))
                    mm = cb * lm
                    dm = jnp.where(causal, _dot(jnp.where(msk, dyp, 0.0), xdt, NT), 0.0)
                    w = dm * mm
                    dcs_cols = dcs_cols + jnp.where(c == hh, jnp.sum(w, axis=1, keepdims=True), 0.0)
                    dcs_rows = dcs_rows + jnp.where(r == hh, jnp.sum(w, axis=0, keepdims=True), 0.0)
                    dcb = dcb + dm * lm
                    dxdt = dxdt + jnp.where(msk, _dot(mm, dyp, TN), 0.0)
                dxx_parts.append(dxdt * xp)
                dx_parts.append(dsk_all[:, sl] * dyp + dxdt * dt_l)
            db_parts.append(db_acc + _dot(dcb, cg, TN))
            dc_parts.append(dc_acc + _dot(dcb, bg, NN))
        g_ref[...] = jnp.stack(g_new)
        dxbc_ref[...] = jnp.concatenate(dx_parts + db_parts + dc_parts, axis=1)

        selt = (lax.broadcasted_iota(jnp.int32, (SSM_W, 128), 0) // HD
                == lax.broadcasted_iota(jnp.int32, (SSM_W, 128), 1)).astype(F32)

        def by_head(parts):
            return _dot_exact(jnp.concatenate(parts, axis=1), selt, "b")

        ddt_x = by_head(dxx_parts)
        dd_row = jnp.sum(by_head(dyx_parts), axis=0, keepdims=True)
        t_all = by_head(qx_parts) * jnp.exp(cs[127:128, :] - cs)
        gh = jnp.sum(_dot_exact(sel, jnp.concatenate(gh_parts, axis=0)), axis=1, keepdims=True)
        gh_row = jnp.broadcast_to(gh, (128, 128)).T[0:1, :]
        at_end = jnp.sum(t_all, axis=0, keepdims=True) + gh_row * jnp.exp(cs[127:128, :])
        dcs = by_head(ryo_parts) - t_all + dcs_cols + jnp.where(r == 127, at_end, 0.0) - dcs_rows.T
        dad = _dot_exact((c >= r).astype(F32), dcs)
        ddt = dad * a + ddt_x
        ddtr = jnp.where(c < 16, ddt * _sigmoid(z), 0.0)
        ddt_ref[...] = ddtr.astype(BF16)
        r8 = lax.broadcasted_iota(jnp.int32, (8, 128), 0)
        dsc_ref[...] += (jnp.where(r8 == 0, jnp.sum(ddtr, axis=0, keepdims=True), 0.0)
                         + jnp.where(r8 == 1, jnp.sum(dad * dt, axis=0, keepdims=True) * a, 0.0)
                         + jnp.where(r8 == 2, dd_row, 0.0))

    rev = NCH - 1
    return _call(
        body, "ssd_bwd", (NCH,),
        [pl.BlockSpec((128, SSM_W), lambda i: (rev - i, 0)),
         pl.BlockSpec((128, SSM_W), lambda i: (rev - i, 0)),
         pl.BlockSpec((128, SSM_W), lambda i: (rev - i, 0)),
         pl.BlockSpec((128, 256), lambda i: (rev - i, 4)), pl.BlockSpec((128, 256), lambda i: (rev - i, 5)),
         pl.BlockSpec((128, 128), lambda i: (rev - i, COL_DT // 128)),
         pl.BlockSpec((128, SSM_W), lambda i: (rev - i, 3)),
         pl.BlockSpec((None, NPAIR, 128, 128), lambda i: (rev - i, 0, 0, 0)),
         pl.BlockSpec((1, 128), lambda i: (0, 0)), pl.BlockSpec((1, 128), lambda i: (0, 0)),
         pl.BlockSpec((1, SSM_W), lambda i: (0, 0)), pl.BlockSpec((1, SSM_W), lambda i: (0, 0))],
        [pl.BlockSpec((128, CONV_C), lambda i: (rev - i, 0)),
         pl.BlockSpec((128, SSM_W), lambda i: (rev - i, 3)),
         pl.BlockSpec((128, 128), lambda i: (rev - i, 0)),
         pl.BlockSpec((1, SSM_W), lambda i: (0, 0)), pl.BlockSpec((8, 128), lambda i: (0, 0))],
        [jax.ShapeDtypeStruct((S, CONV_C), F32), jax.ShapeDtypeStruct((S, WIN_PAD), BF16),
         jax.ShapeDtypeStruct((S, 128), BF16), jax.ShapeDtypeStruct((1, SSM_W), F32),
         jax.ShapeDtypeStruct((8, 128), F32)],
        (dmixed, y, xbc, xbc, xbc, proj, proj, hprev, dtb, alog, dskip_l, ssmw),
        [pltpu.VMEM((NPAIR, 128, 128), F32)], ("arbitrary",), rider)


def _cast_stack(name, slot, arrs, tr, tc):
    n = len(arrs)
    rows, cols = arrs[0].shape

    def body(s_ref, *refs):
        for i in range(n):
            refs[n][i] = refs[i][...].astype(BF16)

    return pl.pallas_call(
        body, name=name,
        grid_spec=pltpu.PrefetchScalarGridSpec(
            num_scalar_prefetch=1, grid=(rows // tr, cols // tc),
            in_specs=[pl.BlockSpec((tr, tc), lambda i, j, sr: (i, j))] * n,
            out_specs=pl.BlockSpec((None, n, tr, tc), lambda i, j, sr: (sr[0], 0, i, j))),
        out_shape=jax.ShapeDtypeStruct((NSH, n, rows, cols), BF16),
        compiler_params=_cparams("parallel", "parallel"),
    )(slot, *arrs)


def _pair_sum(name, c_idx, ps, th):
    n = len(ps)
    _, rows, _ = ps[0].shape

    def body(c_ref, *refs):
        mine, whole, out, theirs = refs[:n], refs[n:2 * n], refs[2 * n:3 * n], refs[3 * n:4 * n]
        send, recv = refs[4 * n], refs[4 * n + 1]
        s, i = pl.program_id(0), pl.program_id(1)
        x, y, c, _ = _place()

        def copies(slot):
            return [_rcopy(whole[k].at[slot, :, pl.ds((1 - c) * HALF, HALF)], theirs[k].at[slot],
                           send.at[slot * n + k], recv.at[slot * n + k], (x, y, 1 - c)) for k in range(n)]

        @pl.when((s == 0) & (i == 0))
        def _():
            for slot in range(NSH):
                for cp in copies(slot):
                    cp.start()

        @pl.when(i == 0)
        def _():
            for slot in range(NSH):
                @pl.when(s == slot)
                def _():
                    for cp in copies(slot):
                        cp.wait()

        rows_i = slice(None) if th == rows else pl.ds(pl.multiple_of(i * th, th), th)
        for k in range(n):
            out[k][...] = (mine[k][...].astype(F32) + theirs[k][s, rows_i, :].astype(F32)).astype(BF16)

    spec = pl.BlockSpec((None, th, HALF), lambda s, i, cr: (s, i, 0))
    return pl.pallas_call(
        body, name=name,
        grid_spec=pltpu.PrefetchScalarGridSpec(
            num_scalar_prefetch=1, grid=(NSH, rows // th),
            in_specs=[pl.BlockSpec((None, th, HALF), lambda s, i, cr: (s, i, cr[0]))] * n + _any_specs(n),
            out_specs=[spec] * n,
            scratch_shapes=[pltpu.VMEM((NSH, rows, HALF), BF16)] * n
            + [pltpu.SemaphoreType.DMA((NSH * n,)), pltpu.SemaphoreType.DMA((NSH * n,))]),
        out_shape=[jax.ShapeDtypeStruct((NSH, rows, HALF), BF16)] * n,
        compiler_params=_cparams("arbitrary", "arbitrary"),
    )(c_idx, *ps, *ps)


def _pair_add(name, c_idx, ps, theirs, th):
    n = len(ps)
    _, rows, _ = ps[0].shape

    def body(c_ref, *refs):
        for k in range(n):
            refs[2 * n + k][...] = (refs[k][...].astype(F32) + refs[n + k][...].astype(F32)).astype(BF16)

    spec = pl.BlockSpec((None, th, HALF), lambda s, i, cr: (s, i, 0))
    return pl.pallas_call(
        body, name=name,
        grid_spec=pltpu.PrefetchScalarGridSpec(
            num_scalar_prefetch=1, grid=(NSH, rows // th),
            in_specs=[pl.BlockSpec((None, th, HALF), lambda s, i, cr: (s, i, cr[0]))] * n + [spec] * n,
            out_specs=[spec] * n),
        out_shape=[jax.ShapeDtypeStruct((NSH, rows, HALF), BF16)] * n,
        compiler_params=_cparams("parallel", "parallel"),
    )(c_idx, *ps, *theirs)


def _chip_sum(name, place, cs, ts, th):
    n = len(ts)
    _, rows, _ = ts[0].shape

    def body(p_ref, *refs):
        for i in range(n):
            t = refs[n + i][...].astype(F32)
            refs[2 * n + i][...] = ((refs[i][...].astype(F32) + t[0]) + t[1]) + t[2]

    return pl.pallas_call(
        body, name=name,
        grid_spec=pltpu.PrefetchScalarGridSpec(
            num_scalar_prefetch=1, grid=(rows // th,),
            in_specs=[pl.BlockSpec((None, th, HALF), lambda i, pr: (pr[0], i, 0))] * n
            + [pl.BlockSpec((3, th, HALF), lambda i, pr: (0, i, 0))] * n,
            out_specs=[pl.BlockSpec((th, HALF), lambda i, pr: (i, pr[1]))] * n),
        out_shape=[jax.ShapeDtypeStruct((rows, D), F32)] * n, compiler_params=_cparams("parallel"),
    )(place, *cs, *ts)


def _adamw(name, ws, gs, ms, vs, tr, tc):
    n = len(ws)
    shape = ws[0].shape
    rows, cols, mid = shape[0], shape[-1], shape[1:-1]
    c1 = 1.0 / (1.0 - ADAM_B1 ** ADAM_STEP)
    c2 = 1.0 / (1.0 - ADAM_B2 ** ADAM_STEP)

    def body(*refs):
        for i in range(n):
            w, g, m, v = (refs[k * n + i][...] for k in range(4))
            m2 = ADAM_B1 * m + (1.0 - ADAM_B1) * g
            v2 = ADAM_B2 * v + (1.0 - ADAM_B2) * (g * g)
            refs[4 * n + 4 * i][...] = -ADAM_LR * ((m2 * c1) / (jnp.sqrt(v2 * c2) + ADAM_EPS) + ADAM_WD * w)
            refs[4 * n + 4 * i + 1][...] = m2
            refs[4 * n + 4 * i + 2][...] = v2
            refs[4 * n + 4 * i + 3][...] = g

    spec = pl.BlockSpec((tr,) + mid + (tc,), lambda i, j: (i,) + (0,) * len(mid) + (j,))
    outs = pl.pallas_call(
        body, name=name, grid=(rows // tr, cols // tc), in_specs=[spec] * (4 * n), out_specs=[spec] * (4 * n),
        out_shape=[jax.ShapeDtypeStruct(shape, F32)] * (4 * n),
        compiler_params=_cparams("parallel", "parallel"),
    )(*ws, *gs, *ms, *vs)
    return [tuple(outs[4 * i:4 * i + 4]) for i in range(n)]


def _place():
    x, y, c = lax.axis_index("x"), lax.axis_index("y"), lax.axis_index("c")
    chips = [(1 - x, y), (x, 1 - y), (1 - x, 1 - y)]
    return x, y, c, chips


def _any_specs(n):
    return [pl.BlockSpec(memory_space=pl.ANY)] * n


def _rcopy(src, dst, send_sem, recv_sem, dev):
    return pltpu.make_async_remote_copy(src_ref=src, dst_ref=dst, send_sem=send_sem, recv_sem=recv_sem,
                                        device_id=dev, device_id_type=MESH)


QUARTER = HALF // 2

TO_X, TO_Y, RELAY_X, RELAY_Y, FWD_X, FWD_Y, FWD_D0, FWD_D1 = range(8)


def _gather_rider(bufs, views, at=None):
    n = len(bufs)

    def plan(rout, sems):
        send, recv = sems
        x, y, c, _ = _place()
        me, sx, sy, sd = 2 * x + y, 2 * (1 - x) + y, 2 * x + (1 - y), 2 * (1 - x) + (1 - y)
        nx, ny, sib = (1 - x, y, c), (x, 1 - y, c), (x, y, 1 - c)
        mine, other = c * HALF, (1 - c) * HALF
        out = {TO_X: (me, mine, HALF, nx), TO_Y: (me, mine, HALF, ny),
               FWD_X: (sx, mine, HALF, sib), FWD_Y: (sy, mine, HALF, sib),
               RELAY_X: (sy, mine, QUARTER, nx), RELAY_Y: (sx, mine + QUARTER, QUARTER, ny),
               FWD_D0: (sd, mine, QUARTER, sib), FWD_D1: (sd, mine + QUARTER, QUARTER, sib)}
        inn = {TO_X: (sx, mine, HALF), TO_Y: (sy, mine, HALF),
               FWD_X: (sx, other, HALF), FWD_Y: (sy, other, HALF),
               RELAY_X: (sd, mine, QUARTER), RELAY_Y: (sd, mine + QUARTER, QUARTER),
               FWD_D0: (sd, other, QUARTER), FWD_D1: (sd, other + QUARTER, QUARTER)}

        def copy(kind, b):
            slot, col, ncols, dev = out[kind]
            win = views[b](rout[b], slot, col, ncols)
            return _rcopy(win, win, send.at[kind * n + b], recv.at[kind * n + b], dev)

        def land(kind, b):
            slot, col, ncols = inn[kind]
            win = views[b](rout[b], slot, col, ncols)
            return _rcopy(win, win, send.at[kind * n + b], recv.at[kind * n + b], (x, y, c))

        return copy, land

    first = (TO_X, TO_Y)
    chain = ((TO_X, (FWD_X, RELAY_Y)), (TO_Y, (FWD_Y, RELAY_X)), (RELAY_X, (FWD_D0,)), (RELAY_Y, (FWD_D1,)))
    forwards = (FWD_X, FWD_Y, FWD_D0, FWD_D1)
    sent = first + tuple(k for _, then in chain for k in then)

    def start(rin, rout, sems):
        copy, _ = plan(rout, sems)
        for kind in first:
            for b in range(n):
                copy(kind, b).start()

    def pass_on(rout, sems, links):
        copy, land = plan(rout, sems)
        for landed, then in links:
            for b in range(n):
                land(landed, b).wait_recv()
                for kind in then:
                    copy(kind, b).start()

    def between(rin, rout, sems):
        pass_on(rout, sems, chain[:2])

    def finish(rin, rout, sems):
        pass_on(rout, sems, chain[2:])
        copy, land = plan(rout, sems)
        for kind in forwards:
            for b in range(n):
                land(kind, b).wait_recv()
        for kind in sent:
            for b in range(n):
                copy(kind, b).wait_send()

    return _Rider(list(bufs), [jax.ShapeDtypeStruct(a.shape, a.dtype) for a in bufs], {b: b for b in range(n)},
                  [pltpu.SemaphoreType.DMA((8 * n,))] * 2, start, finish, between, at)


def _small_gather_rider(cw):
    def descs(rin, rout, sems, x, y, c, chips):
        return [_rcopy(rin[0], rout[0].at[2 * x + y], sems[1].at[j], sems[2].at[j], (chip[0], chip[1], c))
                for j, chip in enumerate(chips)]

    def start(rin, rout, sems):
        x, y, c, chips = _place()
        pltpu.make_async_copy(rin[0], rout[0].at[2 * x + y], sems[0].at[0]).start()
        for cp in descs(rin, rout, sems, x, y, c, chips):
            cp.start()

    def finish(rin, rout, sems):
        x, y, c, chips = _place()
        for j, chip in enumerate(chips):
            _rcopy(rin[0], rout[0].at[2 * chip[0] + chip[1]], sems[1].at[j], sems[2].at[j], (x, y, c)).wait_recv()
        for cp in descs(rin, rout, sems, x, y, c, chips):
            cp.wait_send()
        pltpu.make_async_copy(rin[0], rout[0].at[2 * x + y], sems[0].at[0]).wait()

    return _Rider([cw], [jax.ShapeDtypeStruct((NSH,) + cw.shape, cw.dtype)], {},
                  [pltpu.SemaphoreType.DMA((1,)), pltpu.SemaphoreType.DMA((3,)), pltpu.SemaphoreType.DMA((3,))],
                  start, finish)


def _to_sibling_rider(ps):
    n = len(ps)

    def descs(rin, rout, sems):
        x, y, c, _ = _place()
        return [_rcopy(rin[i].at[:, :, pl.ds((1 - c) * HALF, HALF)], rout[i], sems[0].at[i], sems[1].at[i],
                       (x, y, 1 - c)) for i in range(n)]

    def start(rin, rout, sems):
        for cp in descs(rin, rout, sems):
            cp.start()

    def finish(rin, rout, sems):
        for cp in descs(rin, rout, sems):
            cp.wait()

    return _Rider(list(ps), [jax.ShapeDtypeStruct(a.shape[:2] + (HALF,), a.dtype) for a in ps], {},
                  [pltpu.SemaphoreType.DMA((n,))] * 2, start, finish)


def _to_chips_rider(cs, first=0, count=None, into=None):
    n = len(cs)
    rows = [pl.ds(first, a.shape[1] - first if count is None else count) for a in cs]

    def descs(rin, rout, sems):
        x, y, c, chips = _place()
        return [_rcopy(rin[i].at[2 * chip[0] + chip[1], rows[i]], rout[i].at[j, rows[i]], sems[0].at[j * n + i],
                       sems[1].at[j * n + i], (chip[0], chip[1], c)) for j, chip in enumerate(chips) for i in range(n)]

    def start(rin, rout, sems):
        for cp in descs(rin, rout, sems):
            cp.start()

    def finish(rin, rout, sems):
        for cp in descs(rin, rout, sems):
            cp.wait()

    return _Rider(list(cs) + list(into or []), [jax.ShapeDtypeStruct((3,) + a.shape[1:], a.dtype) for a in cs],
                  {n + i: i for i in range(n)} if into else {}, [pltpu.SemaphoreType.DMA((3 * n,))] * 2, start, finish)


def _join_riders(riders):
    counts = [[len(r.operands) for r in riders], [len(r.out_shapes) for r in riders], [len(r.sems) for r in riders]]

    def each(step):
        def run(*refs):
            at = [0, 0, 0]
            for i, r in enumerate(riders):
                parts = [group[at[k]:at[k] + counts[k][i]] for k, group in enumerate(refs)]
                at = [at[k] + counts[k][i] for k in range(3)]
                step(r)(*parts)
        return run

    aliases = {sum(counts[0][:i]) + k: sum(counts[1][:i]) + v
               for i, r in enumerate(riders) for k, v in r.aliases.items()}
    return _Rider([a for r in riders for a in r.operands], [s for r in riders for s in r.out_shapes], aliases,
                  [s for r in riders for s in r.sems], each(lambda r: r.start), each(lambda r: r.finish),
                  each(lambda r: r.between or (lambda *refs: None)))


SMALL_ROWS = 16


def _swap_halves(gs, vec):
    n = len(gs)

    def body(*refs):
        v_ref, dst, o_ref = refs[n], refs[n + 1:2 * n + 1], refs[2 * n + 1]
        buf, send, recv, vsend, vrecv = refs[2 * n + 2:]
        x, y, c, _ = _place()
        cps = []
        for i in range(n):
            mine = dst[i].at[:, pl.ds(c * HALF, HALF)]
            cps.append(_rcopy(mine, mine, send.at[i], recv.at[i], (x, y, 1 - c)))
        for cp in cps:
            cp.start()

        me = 4 * x + 2 * y + c
        buf[me] = v_ref[...]
        vcps = []
        for k in range(1, 8):
            peer = (x ^ (k >> 2), y ^ ((k >> 1) & 1), c ^ (k & 1))
            vcps.append(_rcopy(v_ref, buf.at[me], vsend.at[k - 1], vrecv.at[k - 1], peer))
        for cp in vcps:
            cp.start()
        for k in range(1, 8):
            _rcopy(v_ref, buf.at[me ^ k], vsend.at[k - 1], vrecv.at[k - 1], (x, y, c)).wait_recv()
        for cp in vcps:
            cp.wait_send()
        t = buf[0]
        for d in range(1, 8):
            t = t + buf[d]
        o_ref[...] = t

        for i in range(n):
            other = dst[i].at[:, pl.ds((1 - c) * HALF, HALF)]
            _rcopy(other, other, send.at[i], recv.at[i], (x, y, c)).wait_recv()
        for cp in cps:
            cp.wait_send()

    vmem = pl.BlockSpec(memory_space=pltpu.VMEM)
    res = pl.pallas_call(
        body, name="grads_swap_halves", in_specs=_any_specs(n) + [vmem], out_specs=_any_specs(n) + [vmem],
        out_shape=[jax.ShapeDtypeStruct(g.shape, g.dtype) for g in gs] + [jax.ShapeDtypeStruct((SMALL_ROWS, D), F32)],
        input_output_aliases={i: i for i in range(n)},
        scratch_shapes=[pltpu.VMEM((8, SMALL_ROWS, D), F32)] + [pltpu.SemaphoreType.DMA((n,))] * 2
        + [pltpu.SemaphoreType.DMA((7,))] * 2,
    )(*gs, vec)
    return list(res[:n]), res[n]


def _col_window(ref, slot, col, ncols):
    return ref.at[slot, :, pl.ds(col, ncols)]


def _stack_window(first, count):
    def view(ref, slot, col, ncols):
        return ref.at[slot, pl.ds(first, count), :, pl.ds(col, ncols)]
    return view


def _row_tile(rows):
    for t in range(512, 15, -16):
        if rows % t == 0:
            return t
    return rows


def _same_shape_runs(arrs):
    runs, a = [], 0
    for b in range(1, len(arrs) + 1):
        if b == len(arrs) or arrs[b].shape != arrs[a].shape:
            runs.append((a, b))
            a = b
    return runs


class _Comm:
    def __init__(self):
        x, y, c = lax.axis_index("x"), lax.axis_index("y"), lax.axis_index("c")
        self.c_idx = jnp.reshape(c, (1,)).astype(jnp.int32)
        self.shard = jnp.reshape(2 * x + y, (1,)).astype(jnp.int32)
        self.place = jnp.stack([2 * x + y, c]).astype(jnp.int32)
        self.groups = {}
        self.sent = {}

    @staticmethod
    def gather(*bufs, part=None, at=None):
        views = [_col_window if b.ndim == 3 else _stack_window(*(part or (0, b.shape[1]))) for b in bufs]
        return _gather_rider(list(bufs), views, at)

    def reduce_rider(self, tag, names, ps, theirs=None, rows=None):
        csums = []
        for a, b in _same_shape_runs(ps):
            name, th = "pair_sum_%s%d" % (tag, a), _row_tile(ps[a].shape[1])
            csums += (_pair_sum(name, self.c_idx, ps[a:b], th) if theirs is None else
                      _pair_add(name, self.c_idx, ps[a:b], theirs[a:b], th))
        self.groups[tag] = [names, csums, None]
        self.sent[tag] = rows
        return _to_chips_rider(csums, 0, rows)

    def rest_rider(self, tag):
        _, csums, ts = self.groups[tag]
        return _to_chips_rider(csums, self.sent[tag], None, ts)

    def landed(self, tag, ts):
        self.groups[tag][2] = ts

    def finish(self, small):
        names, csums, ts = [], [], []
        for group_names, group_csums, group_ts in self.groups.values():
            names += group_names
            csums += group_csums
            ts += group_ts
        order = sorted(range(len(names)), key=lambda i: csums[i].shape[1])
        names, csums, ts = ([v[i] for i in order] for v in (names, csums, ts))
        halves = []
        for a, b in _same_shape_runs(csums):
            halves += _chip_sum("chip_sum_%d" % a, self.place, csums[a:b], ts[a:b], _row_tile(csums[a].shape[1]))
        grads, total = _swap_halves(halves, small)
        return dict(zip(names, grads)), total


ROPE_THETA = 10000.0
SMALL_1K = ("ffn1_pre_norm", "ffn1_post_norm", "mix_pre_norm", "ssm_norm", "mix_post_norm",
            "ffn2_pre_norm", "ffn2_post_norm")
SMALL_16 = ("dt_bias", "a_log", "d_skip")
OFF_CONVB = 7 * D
OFF_16 = OFF_CONVB + CONV_C
OFF_CONVW = OFF_16 + 48
OFF_LOSS = OFF_CONVW + CONV_K * CONV_C
SMALL_LEN = SMALL_ROWS * D


def _sds(shape, dtype):
    return jax.ShapeDtypeStruct(shape, dtype)


def _ridden(res, rider):
    return res if rider is not None else (res, None)


def _ffn_down(name, act, w, tail_of, rider=None):
    tail, o_specs, o_shapes = tail_of(TS)
    return _mm(name, [act, w.dn], NN, (S // TS,),
               [pl.BlockSpec((NSH, TS, FS), lambda i: (0, i, 0)),
                pl.BlockSpec((NSH, None, FS, D), lambda i: (0, w.d0, 0, 0))], o_specs, o_shapes, rider, tail)


def _ffn_dw(name, a, b, rider=None):
    return _mm(name, [a, b], TN, (NSH,),
               [pl.BlockSpec((None, S, FS), lambda s: (s, 0, 0)), pl.BlockSpec((S, D), lambda s: (0, 0))],
               pl.BlockSpec((None, FS, D), lambda s: (s, 0, 0)), _sds((NSH, FS, D), BF16), rider)


def _ffn_dn(name, dgate, dup, w, tail_of, rider=None):
    rows = TS // 2
    tail, o_specs, o_shapes = tail_of(rows)
    a2 = pl.BlockSpec((NSH, rows, FS), lambda i: (0, i, 0))
    return _mm(name, [dgate, w.gu, dup, w.gu], NN, (S // rows,),
               [a2, pl.BlockSpec((NSH, None, FS, D), lambda i: (0, w.g0, 0, 0)),
                a2, pl.BlockSpec((NSH, None, FS, D), lambda i: (0, w.g0 + 1, 0, 0))], o_specs, o_shapes, rider, tail)


def _out_proj_dx(dh, wout, ot):
    def body(dh_ref, w_ref, o_ref, dyn_ref, do_ref, dl_ref):
        dm = _dot(dh_ref[...], w_ref[...], NT)
        dyn_ref[...] = dm[:, D:]
        for b in range(TS // 128):
            for j, blk in enumerate(_rows_to_blocks(dm[128 * b:128 * (b + 1), :D])):
                do = blk.astype(BF16)
                do_ref[j, b] = do
                dl_ref[j, b] = jnp.sum(o_ref[j, b] * do.astype(F32), axis=0, keepdims=True)

    blocks = pl.BlockSpec((NKV, TS // 128, HD, QROWS), lambda i: (0, i, 0, 0))
    return pl.pallas_call(
        body, name="out_proj_dx", grid=(S // TS,),
        in_specs=[pl.BlockSpec((TS, D), lambda i: (i, 0)), pl.BlockSpec((2 * D, D), lambda i: (0, 0)), blocks],
        out_specs=[pl.BlockSpec((TS, D), lambda i: (i, 0)), blocks,
                   pl.BlockSpec((NKV, TS // 128, 1, QROWS), lambda i: (0, i, 0, 0))],
        out_shape=[_sds((S, D), F32), _sds((NKV, NCH, HD, QROWS), BF16), _sds((NKV, NCH, 1, QROWS), F32)],
        compiler_params=_cparams("parallel"),
    )(dh, wout, ot)


def _heads(t, n):
    return t.reshape(S, n, HD).transpose(1, 0, 2)


def _pad128(v):
    return jnp.pad(v, ((0, 0), (0, 128 - v.shape[1])))


def _local_step(x, positions, tgt, sp, gu1, d1, f2, wint, wout, convw, comm=None):
    inv_freq = ROPE_THETA ** (-jnp.arange(0, HD, 2, dtype=F32) / HD)
    ang = positions.astype(F32)[:, None] * inv_freq
    ang = jnp.concatenate([ang, ang, ang, ang], axis=-1)
    cos, sin = jnp.cos(ang), jnp.sin(ang)
    dtb, alog = _pad128(sp["dt_bias"]), _pad128(sp["a_log"])
    dskip_l = jnp.repeat(sp["d_skip"], HD, axis=1)
    convb = sp["conv_b"]

    if comm:
        rider = _join_riders([comm.gather(gu1), _small_gather_rider(convw)])
        (n1, d1, f2, wint, wout), (gu1, convw) = _prenorm_casts(
            "prenorm1", x, sp["ffn1_pre_norm"], comm.shard, [(d1, 0), (f2, 0), (wint, 1), (wout, 0)], rider)
        wint, wout = wint.reshape(NSH, WIN_SH, D), wout.reshape(NSH, 2 * D // NSH, D)
        convw = convw.transpose(1, 0, 2).reshape(CONV_K, CONV_C)
    else:
        n1 = _prenorm("prenorm1", x, sp["ffn1_pre_norm"])
    rider = comm.gather(d1) if comm else None
    (fg1, fu1, act1), got = _ridden(_ffn_up("ffn1_up", n1, _FfnW(gu1, 0, d1, 0), rider), rider)
    if comm:
        d1, = got
    w1 = _FfnW(gu1, 0, d1, 0)
    rider = comm.gather(wint) if comm else None
    (h1, x1, n2), got = _ridden(_ffn_down(
        "ffn1_down", act1, w1,
        lambda rows: _tail_postres(rows, x, sp["ffn1_post_norm"], 0.5, sp["mix_pre_norm"]), rider), rider)
    if comm:
        wint, = got
    wint_pad = jnp.pad(wint.reshape(WIN_COLS, D), ((0, WIN_PAD - WIN_COLS), (0, 0)))

    pw = WIN_PAD // 3
    rider = comm.gather(f2, part=(0, 1)) if comm else None
    proj, got = _ridden(_mm(
        "in_proj", [n2, wint_pad], NT, (S // TS, 3),
        [pl.BlockSpec((TS, D), lambda i, j: (i, 0)), pl.BlockSpec((pw, D), lambda i, j: (j, 0))],
        pl.BlockSpec((TS, pw), lambda i, j: (i, j)), _sds((S, WIN_PAD), F32), rider), rider)
    if comm:
        f2, = got
    qt = _rope_q(proj, cos, sin)
    k_rot, v_bf, kt, vt = _rope_kv(proj, cos, sin)
    kh, vh = _heads(k_rot, NKV), _heads(v_bf, NKV)
    bias = _bias_table()
    rider = comm.gather(f2, wout, part=(1, 2), at=13) if comm else None
    (ot, lse, mixed), got = _ridden(_attn_fwd(qt, kh, vt, bias, rider), rider)
    if comm:
        f2, wout = got
    w2 = _FfnW(f2, 0, f2, 2)
    wout = wout.reshape(2 * D, D)
    xbc, conv_y = _conv_fwd(proj, convw, convb)
    y, mixed, hprev = _ssd_fwd(xbc, proj, dtb, alog, dskip_l, sp["ssm_norm"], mixed)
    tail, o_specs, o_shapes = _tail_postres(TS, x1, sp["mix_post_norm"], 1.0, sp["ffn2_pre_norm"])
    h2, x2, n3 = _mm("out_proj", [mixed, wout], NN, (S // TS,),
                     [pl.BlockSpec((TS, 2 * D), lambda i: (i, 0)), pl.BlockSpec((2 * D, D), lambda i: (0, 0))],
                     o_specs, o_shapes, None, tail)

    fg2, fu2, act2 = _ffn_up("ffn2_up", n3, w2)
    dy, dh3, dp3, loss = _ffn_down(
        "ffn2_down", act2, w2, lambda rows: _tail_final(rows, x2, sp["ffn2_post_norm"], tgt, 0.5))

    dgate2, dup2 = _ffn_dact("ffn2_dact", dh3, w2, fg2, fu2)
    dws2 = [_ffn_dw("ffn2_dwg", dgate2, n3), _ffn_dw("ffn2_dwu", dup2, n3), _ffn_dw("ffn2_dwd", act2, dh3)]
    dx2, dh2, dg3, dp2 = _ffn_dn(
        "ffn2_dn", dgate2, dup2, w2,
        lambda rows: _tail_mid_bwd(rows, dy, x2, sp["ffn2_pre_norm"], h2, sp["mix_post_norm"], 1.0))

    dyn, dot_, delta = _out_proj_dx(dh2, wout, ot)
    dwout = _mm("out_proj_dw", [mixed, dh2], TN, (2,),
                [pl.BlockSpec((S, D), lambda m: (0, m)), pl.BlockSpec((S, D), lambda m: (0, 0))],
                pl.BlockSpec((D, D), lambda m: (m, 0)), _sds((2 * D, D), BF16))
    dwout = dwout.reshape(NSH, 2 * D // NSH, D)

    def riding(tag, names, ps, call, theirs=None, rows=None):
        rider = None
        if comm:
            rider = comm.rest_rider(tag) if names is None else comm.reduce_rider(tag, names, ps, theirs, rows)
        res, got = _ridden(call(rider), rider)
        if comm:
            comm.landed(tag, got)
        return res

    rider = _to_sibling_rider(dws2 + [dwout]) if comm else None
    (dxbc, dproj, ddt, dssm, dsc), theirs = _ridden(
        _ssd_bwd(dyn, y, xbc, proj, hprev, dtb, alog, dskip_l, sp["ssm_norm"], rider), rider)
    dproj, dcw8, dcb = _conv_bwd(dxbc, conv_y, proj, convw, dproj)
    dqt, dkh, dvh = riding("a", BIG[3:6] + ("w_out",), dws2 + [dwout], lambda rider: _attn_bwd(
        qt, kh, kt, vh, dot_, lse, delta, bias, rider), theirs)
    dproj = _rope_dq(dqt, cos, sin, dproj)
    dproj = _rope_dkv(dkh, dvh, cos, sin, dproj)
    dproj = lax.dynamic_update_slice(dproj, ddt, (0, COL_DT))
    dwint = _mm("in_proj_dw", [dproj, n2], TN, (3,),
                [pl.BlockSpec((S, pw), lambda j: (0, j)), pl.BlockSpec((S, D), lambda j: (0, 0))],
                pl.BlockSpec((pw, D), lambda j: (j, 0)), _sds((WIN_PAD, D), BF16))
    dwint = dwint[:WIN_COLS].reshape(NSH, WIN_SH, D)

    tail, o_specs, o_shapes = _tail_mid_bwd(TS, dx2, x1, sp["mix_pre_norm"], h1, sp["ffn1_post_norm"], 0.5)
    dx1, dh1, dg2, dp1 = riding("b", ("w_in",), [dwint], lambda rider: _mm(
        "in_proj_dx", [dproj, wint_pad], NN, (S // TS,),
        [pl.BlockSpec((TS, WIN_PAD), lambda i: (i, 0)), pl.BlockSpec((WIN_PAD, D), lambda i: (0, 0))],
        o_specs, o_shapes, rider, tail), rows=W_IN_FIRST)

    dwd1 = riding("b", None, None, lambda rider: _ffn_dw("ffn1_dwd", act1, dh1, rider))
    dgate1, dup1 = riding("d", BIG[2:3], [dwd1], lambda rider: _ffn_dact("ffn1_dact", dh1, w1, fg1, fu1, rider))
    dwg1, dwu1 = _ffn_dw("ffn1_dwg", dgate1, n1), _ffn_dw("ffn1_dwu", dup1, n1)
    grad_x, dg1 = riding("g", BIG[0:2], [dwg1, dwu1], lambda rider: _ffn_dn(
        "ffn1_dn", dgate1, dup1, w1, lambda rows: _tail_first_bwd(rows, dx1, x, sp["ffn1_pre_norm"]), rider))
    dws1 = [dwg1, dwu1, dwd1]

    small = jnp.concatenate([
        dg1[0], dp1[0], dg2[0], dssm[0], dp2[0], dg3[0], dp3[0], dcb[0],
        dsc[0, :16], dsc[1, :16], dsc[2, :16], dcw8[:CONV_K].reshape(-1), loss[0, :1]])
    small = jnp.pad(small, (0, SMALL_LEN - small.shape[0])).reshape(SMALL_ROWS, D)
    if comm is None:
        return grad_x, dws1 + dws2 + [dwint, dwout], small
    return (grad_x,) + comm.finish(small)


WEIGHTS = ("ffn1_pre_norm", "ffn1_w_gate", "ffn1_w_up", "ffn1_w_down", "ffn1_post_norm", "mix_pre_norm", "w_in",
           "conv_w", "conv_b", "dt_bias", "a_log", "d_skip", "ssm_norm", "w_out", "mix_post_norm", "ffn2_pre_norm",
           "ffn2_w_gate", "ffn2_w_up", "ffn2_w_down", "ffn2_post_norm")
BIG = ("ffn1_w_gate", "ffn1_w_up", "ffn1_w_down", "ffn2_w_gate", "ffn2_w_up", "ffn2_w_down", "w_in", "w_out")
TRANSPOSED = ("ffn1_w_gate", "ffn1_w_up", "ffn2_w_gate", "ffn2_w_up", "w_in")
SMALL_ORDER = SMALL_1K + ("conv_b",) + SMALL_16
CONVW_SH = CONV_C // NSH


def _shard2d(t, name):
    return t[0].T if name in TRANSPOSED else t[0]


def _unshard2d(t, name):
    return (t.T if name in TRANSPOSED else t)[None]


def _rows3d(t):
    return t.transpose(2, 0, 1)


def _pack_small(d, prefix, shard_of_convw):
    flat = jnp.concatenate([d[prefix + n][0] for n in SMALL_ORDER] + [shard_of_convw.reshape(-1)])
    return jnp.pad(flat, (0, SMALL_LEN - flat.shape[0])).reshape(SMALL_ROWS, D)


def _unpack_small(block, like):
    flat = block.reshape(-1)
    out, off = {}, 0
    for n in SMALL_ORDER:
        size = like[n].shape[1]
        out[n] = flat[off:off + size].reshape(1, size)
        off += size
    out["conv_w"] = flat[off:off + CONV_K * CONVW_SH].reshape(1, CONV_K, CONVW_SH)
    return out


def kernel(x, positions, ffn1_pre_norm, ffn1_w_gate, ffn1_w_up, ffn1_w_down, ffn1_post_norm, mix_pre_norm, w_in, conv_w, conv_b, dt_bias, a_log, d_skip, ssm_norm, w_out, mix_post_norm, ffn2_pre_norm, ffn2_w_gate, ffn2_w_up, ffn2_w_down, ffn2_post_norm, loss_target, m_ffn1_pre_norm, m_ffn1_w_gate, m_ffn1_w_up, m_ffn1_w_down, m_ffn1_post_norm, m_mix_pre_norm, m_w_in, m_conv_w, m_conv_b, m_dt_bias, m_a_log, m_d_skip, m_ssm_norm, m_w_out, m_mix_post_norm, m_ffn2_pre_norm, m_ffn2_w_gate, m_ffn2_w_up, m_ffn2_w_down, m_ffn2_post_norm, v_ffn1_pre_norm, v_ffn1_w_gate, v_ffn1_w_up, v_ffn1_w_down, v_ffn1_post_norm, v_mix_pre_norm, v_w_in, v_conv_w, v_conv_b, v_dt_bias, v_a_log, v_d_skip, v_ssm_norm, v_w_out, v_mix_post_norm, v_ffn2_pre_norm, v_ffn2_w_gate, v_ffn2_w_up, v_ffn2_w_down, v_ffn2_post_norm):
    given = dict(locals())
    xi, yi = lax.axis_index("x"), lax.axis_index("y")

    comm = _Comm()
    big = {p + n: _shard2d(given[p + n], n) for n in BIG for p in ("", "m_", "v_")}
    gu1 = _cast_stack("cast_ffn1_gate_up", comm.shard, [big[n] for n in BIG[0:2]], 176, D)

    sp = {n: given[n] for n in SMALL_ORDER}
    grad_x, big_grads, small = _local_step(
        x[0], positions[0], loss_target[0], sp, gu1, [big[BIG[2]]], [big[n] for n in BIG[3:6]], [big["w_in"]],
        [big["w_out"]], conv_w[0], comm)

    tot = small.reshape(-1)
    loss = tot[OFF_LOSS]
    small_grads, off = {}, 0
    for n in SMALL_ORDER:
        size = given[n].shape[1]
        small_grads[n] = tot[off:off + size].reshape(1, size)
        off += size
    dconvw = tot[OFF_CONVW:OFF_CONVW + CONV_K * CONV_C].reshape(CONV_K, NSH, CONVW_SH)
    dconvw = lax.dynamic_index_in_dim(dconvw, 2 * xi + yi, axis=1, keepdims=False)
    small_grads["conv_w"] = dconvw.reshape(1, CONV_K, CONVW_SH)

    upd = {}
    for names, tr in ((BIG[0:3], 176), (BIG[3:6], 176), (BIG[7:8], 256)):
        res = _adamw("adamw_" + names[0], [big[n] for n in names], [big_grads[n] for n in names],
                     [big["m_" + n] for n in names], [big["v_" + n] for n in names], tr, D)
        for n, r in zip(names, res):
            upd[n] = tuple(_unshard2d(t, n) for t in r)
    g_win = big_grads["w_in"].reshape(WIN_SH, 1, D)
    res, = _adamw("adamw_w_in", [_rows3d(w_in)], [g_win], [_rows3d(m_w_in)], [_rows3d(v_w_in)], WIN_SH // 4, D)
    upd["w_in"] = tuple(t.transpose(1, 2, 0) for t in res)
    (dl, m2, v2, _), = _adamw(
        "adamw_small", [_pack_small(given, "", conv_w[0])], [_pack_small(small_grads, "", dconvw)],
        [_pack_small(given, "m_", m_conv_w[0])], [_pack_small(given, "v_", v_conv_w[0])], SMALL_ROWS, D)
    dl, m2, v2 = (_unpack_small(t, given) for t in (dl, m2, v2))
    for n in SMALL_ORDER + ("conv_w",):
        upd[n] = (dl[n], m2[n], v2[n], small_grads[n])

    return (loss, grad_x[None], *[upd[n][3] for n in WEIGHTS], *[upd[n][0] for n in WEIGHTS],
            *[upd[n][1] for n in WEIGHTS], *[upd[n][2] for n in WEIGHTS])
```

```python
import functools
import typing

import jax
import jax.numpy as jnp
from jax import lax
from jax.experimental import pallas as pl
from jax.experimental.pallas import tpu as pltpu

F32 = jnp.float32
BF16 = jnp.bfloat16

S = 2048
D = 1024
FF = 2816
NSH = 4
FS = FF // NSH
HALF = D // 2
HD = 64
NKV = 4
NQ_PER_KV = 4
KVW = NKV * HD
QCOLS = NQ_PER_KV * HD
CONV_C = 1536
CONV_K = 4
SSM_W = 1024
NST = 128
NCH = S // 128
WIN_COLS = 4112
WIN_SH = WIN_COLS // NSH
W_IN_FIRST = 768
WIN_PAD = 4224
COL_DT = 4096
EPS = 1e-6
NEG = -1e30

ADAM_LR = 0.001
ADAM_B1 = 0.9
ADAM_B2 = 0.999
ADAM_EPS = 1e-08
ADAM_WD = 0.01
ADAM_STEP = 10

VMEM_LIMIT = 56 * 1024 * 1024
TS = 512
TR = 256

NN = (((1,), (0,)), ((), ()))
NT = (((1,), (1,)), ((), ()))
TN = (((0,), (0,)), ((), ()))
MESH = pl.DeviceIdType.MESH


def _cparams(*sem):
    return pltpu.CompilerParams(dimension_semantics=sem, vmem_limit_bytes=VMEM_LIMIT)


def _dot(a, b, dims):
    return lax.dot_general(a.astype(BF16), b.astype(BF16), dims, preferred_element_type=F32)


def _bf16_pieces(v):
    hi = v.astype(BF16)
    rest = v - hi.astype(F32)
    mid = rest.astype(BF16)
    return hi, mid, (rest - mid.astype(F32)).astype(BF16)


def _dot_exact(a, b, ones="a"):
    if ones == "a":
        sel = a.astype(BF16)
        parts = [lax.dot_general(sel, p, NN, preferred_element_type=F32) for p in _bf16_pieces(b)]
    else:
        sel = b.astype(BF16)
        parts = [lax.dot_general(p, sel, NN, preferred_element_type=F32) for p in _bf16_pieces(a)]
    return (parts[2] + parts[1]) + parts[0]


def _sigmoid(v):
    return 1.0 / (1.0 + jnp.exp(-v))


class _Rider(typing.NamedTuple):
    operands: list
    out_shapes: list
    aliases: dict
    sems: list
    start: typing.Callable
    finish: typing.Callable
    between: typing.Callable = None
    at: int = None


def _call(body, name, grid, in_specs, out_specs, out_shape, operands, scratch=(), sem=(), rider=None, prefetch=0):
    multi = isinstance(out_shape, (list, tuple))

    def launch(kernel, in_specs, out_specs, out_shape, scratch, aliases, sem, args):
        if prefetch:
            how = dict(grid_spec=pltpu.PrefetchScalarGridSpec(
                num_scalar_prefetch=prefetch, grid=grid, in_specs=in_specs, out_specs=out_specs,
                scratch_shapes=scratch))
        else:
            how = dict(grid=grid, in_specs=in_specs, out_specs=out_specs, scratch_shapes=scratch)
        return pl.pallas_call(kernel, name=name, out_shape=out_shape, input_output_aliases=aliases,
                              compiler_params=_cparams(*sem), **how)(*args)

    if rider is None:
        return launch(body, in_specs, out_specs, out_shape, list(scratch), {}, sem, operands)
    outs = list(out_shape) if multi else [out_shape]
    ospecs = list(out_specs) if multi else [out_specs]
    n_in, n_out, n_scr = len(operands) - prefetch, len(outs), len(scratch)
    ri, ro = len(rider.operands), len(rider.out_shapes)

    def wrapped(*refs):
        scalars, refs = refs[:prefetch], refs[prefetch:]
        o0 = n_in + ri
        s0 = o0 + n_out + ro
        rin, rout, rsem = refs[n_in:o0], refs[o0 + n_out:s0], refs[s0 + n_scr:]
        ids = [pl.program_id(a) for a in range(len(grid))]
        first = functools.reduce(jnp.logical_and, [i == 0 for i in ids])
        last = functools.reduce(jnp.logical_and, [i == g - 1 for i, g in zip(ids, grid)])

        @pl.when(first)
        def _():
            rider.start(rin, rout, rsem)

        if rider.between is not None:
            steps = functools.reduce(lambda a, b: a * b, grid)
            step = functools.reduce(lambda a, ig: a * ig[1] + ig[0], zip(ids, grid), 0)

            @pl.when(step == (steps // 3 if rider.at is None else rider.at))
            def _():
                rider.between(rin, rout, rsem)

        body(*scalars, *refs[:n_in], *refs[o0:o0 + n_out], *refs[s0:s0 + n_scr])

        @pl.when(last)
        def _():
            rider.finish(rin, rout, rsem)

    hbm = pl.BlockSpec(memory_space=pl.ANY)
    res = launch(wrapped, list(in_specs) + [hbm] * ri, ospecs + [hbm] * ro, outs + list(rider.out_shapes),
                 list(scratch) + list(rider.sems),
                 {prefetch + n_in + k: n_out + v for k, v in rider.aliases.items()},
                 ("arbitrary",) * len(grid), (*operands, *rider.operands))
    main = list(res[:n_out])
    return (main if multi else main[0]), list(res[n_out:])


class _Tail(typing.NamedTuple):
    fn: typing.Callable
    operands: list
    in_specs: list


def _mm(name, operands, dims, grid, in_specs, o_spec, out_shape, rider=None, tail=None):
    npairs = len(operands) // 2
    extra = [] if tail is None else list(tail.operands)
    nin = 2 * npairs + len(extra)

    def body(*refs):
        t = None
        for i in range(npairs):
            a, b = refs[2 * i], refs[2 * i + 1]
            parts = [(a[s], b[s]) for s in range(a.shape[0])] if len(a.shape) == 3 else [(a[...], b[...])]
            for pa, pb in parts:
                d = _dot(pa, pb, dims)
                t = d if t is None else t + d
        if tail is None:
            refs[nin][...] = t.astype(refs[nin].dtype)
        else:
            tail.fn(t, refs[2 * npairs:nin], refs[nin:])

    sem = ("parallel" if tail is None else "arbitrary",) * len(grid)
    specs = list(in_specs) + ([] if tail is None else list(tail.in_specs))
    return _call(body, name, grid, specs, o_spec, out_shape, list(operands) + extra, (), sem, rider)


class _FfnW(typing.NamedTuple):
    gu: jax.Array
    g0: int
    dn: jax.Array
    d0: int


def _ffn_up(name, n, w, rider=None):
    def body(n_ref, wg_ref, wu_ref, fg_ref, fu_ref, a_ref):
        nb = n_ref[...]
        g = _dot(nb, wg_ref[...], NT)
        u = _dot(nb, wu_ref[...], NT)
        sg = _sigmoid(g)
        silu = g * sg
        fg_ref[...] = (u * (sg * (1.0 + g * (1.0 - sg)))).astype(BF16)
        fu_ref[...] = silu.astype(BF16)
        a_ref[...] = (silu * u).astype(BF16)

    out = jax.ShapeDtypeStruct((NSH, S, FS), BF16)
    ospec = pl.BlockSpec((None, TS, FS), lambda s, i: (s, i, 0))
    return _call(
        body, name, (NSH, S // TS),
        [pl.BlockSpec((TS, D), lambda s, i: (i, 0)),
         pl.BlockSpec((None, None, FS, D), lambda s, i: (s, w.g0, 0, 0)),
         pl.BlockSpec((None, None, FS, D), lambda s, i: (s, w.g0 + 1, 0, 0))],
        [ospec, ospec, ospec], [out, out, out], (n, w.gu, w.gu), sem=("parallel", "parallel"), rider=rider)


def _ffn_dact(name, dh, w, fgate, fup, rider=None):
    def body(dh_ref, wd_ref, fg_ref, fu_ref, dg_ref, du_ref):
        da = _dot(dh_ref[...], wd_ref[...], NT)
        dg_ref[...] = (da * fg_ref[...].astype(F32)).astype(BF16)
        du_ref[...] = (da * fu_ref[...].astype(F32)).astype(BF16)

    out = jax.ShapeDtypeStruct((NSH, S, FS), BF16)
    aspec = pl.BlockSpec((None, TS, FS), lambda s, i: (s, i, 0))
    return _call(
        body, name, (NSH, S // TS),
        [pl.BlockSpec((TS, D), lambda s, i: (i, 0)),
         pl.BlockSpec((None, None, FS, D), lambda s, i: (s, w.d0, 0, 0)), aspec, aspec],
        [aspec, aspec], [out, out], (dh, w.dn, fgate, fup), sem=("parallel", "parallel"), rider=rider)


def _rstd(v):
    return lax.rsqrt(jnp.mean(v * v, axis=-1, keepdims=True) + EPS)


def _row_spec():
    return pl.BlockSpec((TR, D), lambda i: (i, 0))


def _vec_spec():
    return pl.BlockSpec((1, D), lambda i: (0, 0))


def _acc_rows(ref, v):
    @pl.when(pl.program_id(0) == 0)
    def _():
        ref[...] = jnp.zeros_like(ref)
    ref[...] += jnp.sum(v, axis=0, keepdims=True)


def _prenorm(name, x, g):
    def body(x_ref, g_ref, n_ref):
        xv = x_ref[...]
        n_ref[...] = (xv * _rstd(xv) * g_ref[...]).astype(BF16)

    return pl.pallas_call(
        body, name=name, grid=(S // TR,), in_specs=[_row_spec(), _vec_spec()], out_specs=_row_spec(),
        out_shape=jax.ShapeDtypeStruct((S, D), BF16), compiler_params=_cparams("parallel"),
    )(x, g)


ENTRY_STEPS = 4


def _prenorm_casts(name, x, g, slot, groups, rider):
    def body(s_ref, x_ref, g_ref, *refs):
        ins, outs = refs[:len(refs) - len(groups) - 1], refs[len(refs) - len(groups) - 1:]
        xv = x_ref[...]
        outs[0][...] = (xv * _rstd(xv) * g_ref[...]).astype(BF16)
        at = 0
        for (arrs, _), out in zip(groups, outs[1:]):
            for k in range(len(arrs)):
                out[k] = ins[at + k][...].astype(BF16)
            at += len(arrs)

    rows = pl.BlockSpec((S // ENTRY_STEPS, D), lambda i, sr: (i, 0))
    in_specs, out_specs, out_shapes = [rows, pl.BlockSpec((1, D), lambda i, sr: (0, 0))], [rows], [_rows_bf16()]
    for arrs, axis in groups:
        r, c = arrs[0].shape
        if axis == 0:
            blk, at, at_out = (r // ENTRY_STEPS, c), (lambda i, sr: (i, 0)), (lambda i, sr: (sr[0], 0, i, 0))
        else:
            blk, at, at_out = (r, c // ENTRY_STEPS), (lambda i, sr: (0, i)), (lambda i, sr: (sr[0], 0, 0, i))
        in_specs += [pl.BlockSpec(blk, at)] * len(arrs)
        out_specs.append(pl.BlockSpec((None, len(arrs)) + blk, at_out))
        out_shapes.append(jax.ShapeDtypeStruct((NSH, len(arrs), r, c), BF16))
    return _call(body, name, (ENTRY_STEPS,), in_specs, out_specs, out_shapes,
                 [slot, x, g] + [a for arrs, _ in groups for a in arrs], rider=rider, prefetch=1)


def _rows_spec(rows):
    return pl.BlockSpec((rows, D), lambda i: (i, 0))


def _rows_f32():
    return jax.ShapeDtypeStruct((S, D), F32)


def _rows_bf16():
    return jax.ShapeDtypeStruct((S, D), BF16)


def _vec_f32():
    return jax.ShapeDtypeStruct((1, D), F32)


def _tail_postres(rows, x, p, alpha, gnext):
    def fn(h, ins, outs):
        x_ref, p_ref, g_ref = ins
        h_ref, xo_ref, n_ref = outs
        h_ref[...] = h
        xo = x_ref[...] + alpha * (h * _rstd(h) * p_ref[...])
        xo_ref[...] = xo
        n_ref[...] = (xo * _rstd(xo) * g_ref[...]).astype(BF16)

    rs = _rows_spec(rows)
    return (_Tail(fn, [x, p, gnext], [rs, _vec_spec(), _vec_spec()]), [rs, rs, rs],
            [_rows_f32(), _rows_f32(), _rows_bf16()])


def _tail_final(rows, x, p, tgt, alpha):
    def fn(h, ins, outs):
        x_ref, p_ref, t_ref = ins
        dy_ref, dh_ref, dp_ref, loss_ref = outs
        r = _rstd(h)
        hn = h * r
        pv = p_ref[...]
        e = x_ref[...] + alpha * (hn * pv) - t_ref[...]
        dy = e * (1.0 / D)
        dy_ref[...] = dy
        du = alpha * dy * pv
        dh_ref[...] = (r * (du - hn * jnp.mean(du * hn, axis=-1, keepdims=True))).astype(BF16)
        _acc_rows(dp_ref, alpha * dy * hn)
        part = 0.5 * jnp.sum(jnp.mean(e * e, axis=-1, keepdims=True), axis=0, keepdims=True)
        _acc_rows(loss_ref, jnp.broadcast_to(part, (1, 128)))

    rs = _rows_spec(rows)
    return (_Tail(fn, [x, p, tgt], [rs, _vec_spec(), rs]),
            [rs, rs, _vec_spec(), pl.BlockSpec((1, 128), lambda i: (0, 0))],
            [_rows_f32(), _rows_bf16(), _vec_f32(), jax.ShapeDtypeStruct((1, 128), F32)])


def _norm_bwd(dn, xv, g_ref, dg_ref):
    r = _rstd(xv)
    xn = xv * r
    dng = dn * g_ref[...]
    _acc_rows(dg_ref, dn * xn)
    return r * (dng - xn * jnp.mean(dng * xn, axis=-1, keepdims=True))


def _tail_mid_bwd(rows, dres, x, g, h, p, alpha):
    def fn(dn, ins, outs):
        dr_ref, x_ref, g_ref, h_ref, p_ref = ins
        dx_ref, dh_ref, dg_ref, dp_ref = outs
        dx = dr_ref[...] + _norm_bwd(dn, x_ref[...], g_ref, dg_ref)
        dx_ref[...] = dx
        hv = h_ref[...]
        r = _rstd(hv)
        hn = hv * r
        du = alpha * dx * p_ref[...]
        dh_ref[...] = (r * (du - hn * jnp.mean(du * hn, axis=-1, keepdims=True))).astype(BF16)
        _acc_rows(dp_ref, alpha * dx * hn)

    rs = _rows_spec(rows)
    return (_Tail(fn, [dres, x, g, h, p], [rs, rs, _vec_spec(), rs, _vec_spec()]),
            [rs, rs, _vec_spec(), _vec_spec()], [_rows_f32(), _rows_bf16(), _vec_f32(), _vec_f32()])


def _tail_first_bwd(rows, dres, x, g):
    def fn(dn, ins, outs):
        dr_ref, x_ref, g_ref = ins
        dx_ref, dg_ref = outs
        dx_ref[...] = dr_ref[...] + _norm_bwd(dn, x_ref[...], g_ref, dg_ref)

    rs = _rows_spec(rows)
    return (_Tail(fn, [dres, x, g], [rs, rs, _vec_spec()]), [rs, _vec_spec()], [_rows_f32(), _vec_f32()])


def _rotate(t, c128, s128, sign, scale):
    width = t.shape[1]
    c = jnp.tile(c128, (1, width // 128))
    sn = jnp.tile(s128, (1, width // 128))
    lane = lax.broadcasted_iota(jnp.int32, t.shape, 1) & (HD - 1)
    rot = jnp.where(lane < HD // 2, -pltpu.roll(t, width - HD // 2, 1), pltpu.roll(t, HD // 2, 1))
    return (t * c + sign * (rot * sn)) * scale


def _rows_to_blocks(y):
    out = []
    for j in range(NKV):
        yt = y[:, QCOLS * j:QCOLS * (j + 1)].T
        out.append(jnp.concatenate([yt[HD * g:HD * (g + 1)] for g in range(NQ_PER_KV)], axis=1))
    return out


def _blocks_to_rows(blocks):
    cols = []
    for b in blocks:
        stacked = jnp.concatenate([b[:, 128 * g:128 * (g + 1)] for g in range(NQ_PER_KV)], axis=0)
        cols.append(stacked.T)
    return jnp.concatenate(cols, axis=1)


def _rope_q(proj, cos, sin):
    def body(t_ref, c_ref, s_ref, o_ref):
        y = _rotate(t_ref[...], c_ref[...], s_ref[...], 1.0, HD ** -0.5)
        for j, blk in enumerate(_rows_to_blocks(y)):
            o_ref[j] = blk.astype(BF16)

    return pl.pallas_call(
        body, name="rope_q", grid=(NCH,),
        in_specs=[pl.BlockSpec((128, D), lambda i: (i, 0)),
                  pl.BlockSpec((128, 128), lambda i: (i, 0)), pl.BlockSpec((128, 128), lambda i: (i, 0))],
        out_specs=pl.BlockSpec((NKV, None, HD, QROWS), lambda i: (0, i, 0, 0)),
        out_shape=jax.ShapeDtypeStruct((NKV, NCH, HD, QROWS), BF16), compiler_params=_cparams("parallel"),
    )(proj, cos, sin)


def _rope_dq(dqt, cos, sin, dproj):
    def body(t_ref, c_ref, s_ref, buf_ref, o_ref):
        t = _blocks_to_rows([t_ref[j] for j in range(NKV)])
        o_ref[...] = _rotate(t, c_ref[...], s_ref[...], -1.0, HD ** -0.5).astype(BF16)

    return pl.pallas_call(
        body, name="rope_dq", grid=(NCH,),
        in_specs=[pl.BlockSpec((NKV, None, HD, QROWS), lambda i: (0, i, 0, 0)),
                  pl.BlockSpec((128, 128), lambda i: (i, 0)), pl.BlockSpec((128, 128), lambda i: (i, 0)),
                  pl.BlockSpec(memory_space=pl.ANY)],
        out_specs=pl.BlockSpec((128, D), lambda i: (i, 0)),
        out_shape=jax.ShapeDtypeStruct(dproj.shape, BF16), input_output_aliases={3: 0},
        compiler_params=_cparams("parallel"),
    )(dqt, cos, sin, dproj)


def _rope_dkv(dkt, dvt, cos, sin, dproj):
    def body(k_ref, v_ref, c_ref, s_ref, buf_ref, o_ref):
        dk = jnp.concatenate([k_ref[j] for j in range(NKV)], axis=0).T
        dv = jnp.concatenate([v_ref[j] for j in range(NKV)], axis=0).T
        dk = _rotate(dk, c_ref[...], s_ref[...], -1.0, 1.0)
        o_ref[...] = jnp.concatenate([dk, dv], axis=1).astype(BF16)

    tspec = pl.BlockSpec((NKV, HD, 128), lambda i: (0, 0, i))
    return pl.pallas_call(
        body, name="rope_dkv", grid=(NCH,),
        in_specs=[tspec, tspec, pl.BlockSpec((128, 128), lambda i: (i, 0)), pl.BlockSpec((128, 128), lambda i: (i, 0)),
                  pl.BlockSpec(memory_space=pl.ANY)],
        out_specs=pl.BlockSpec((128, 2 * KVW), lambda i: (i, D // (2 * KVW))),
        out_shape=jax.ShapeDtypeStruct(dproj.shape, BF16), input_output_aliases={4: 0},
        compiler_params=_cparams("parallel"),
    )(dkt, dvt, cos, sin, dproj)


def _rope_kv(proj, cos, sin):
    def body(t_ref, c_ref, s_ref, k_ref, v_ref, kt_ref, vt_ref):
        t = t_ref[...]
        k = _rotate(t[:, :KVW], c_ref[...], s_ref[...], 1.0, 1.0).astype(BF16)
        v = t[:, KVW:].astype(BF16)
        k_ref[...] = k
        v_ref[...] = v
        kt, vt = k.astype(F32).T, v.astype(F32).T
        for j in range(NKV):
            kt_ref[j] = kt[HD * j:HD * (j + 1)].astype(BF16)
            vt_ref[j] = vt[HD * j:HD * (j + 1)].astype(BF16)

    rows = pl.BlockSpec((128, KVW), lambda i: (i, 0))
    tspec = pl.BlockSpec((NKV, HD, 128), lambda i: (0, 0, i))
    return pl.pallas_call(
        body, name="rope_kv", grid=(NCH,),
        in_specs=[pl.BlockSpec((128, 2 * KVW), lambda i: (i, D // (2 * KVW))),
                  pl.BlockSpec((128, 128), lambda i: (i, 0)), pl.BlockSpec((128, 128), lambda i: (i, 0))],
        out_specs=[rows, rows, tspec, tspec],
        out_shape=[jax.ShapeDtypeStruct((S, KVW), BF16)] * 2 + [jax.ShapeDtypeStruct((NKV, HD, S), BF16)] * 2,
        compiler_params=_cparams("parallel"),
    )(proj, cos, sin)


QROWS = NQ_PER_KV * 128


NBIAS = NCH + 1
KV_PER_STEP = 4


def _bias_table():
    db = lax.broadcasted_iota(jnp.int32, (NBIAS, 128, QROWS), 0) - 1
    ki = lax.broadcasted_iota(jnp.int32, (NBIAS, 128, QROWS), 1)
    qi = lax.broadcasted_iota(jnp.int32, (NBIAS, 128, QROWS), 2) & 127
    d = db * 128 + qi - ki
    cnt = ((d <= 128).astype(F32) + (((d & 3) == 0) & (d <= 512)).astype(F32) + ((d & 15) == 0).astype(F32))
    return jnp.where((d >= 0) & (cnt > 0.0), jnp.log(jnp.maximum(cnt, 1.0)), NEG)


def _attn_fwd(qt, kh, vt, bias, rider=None):
    def body(q_ref, k_ref, v_ref, b_ref, o_ref, lse_ref, rows_ref, m_ref, l_ref, acc_ref):
        qb = pl.program_id(1)
        m_ref[...] = jnp.full_like(m_ref, NEG)
        l_ref[...] = jnp.zeros_like(l_ref)
        acc_ref[...] = jnp.zeros_like(acc_ref)

        def keys(off, size, bias_):
            for h in range(KV_PER_STEP):
                m = m_ref[h]
                s = _dot(k_ref[h, pl.ds(off, size), :], q_ref[h], NN) + bias_
                m_new = jnp.maximum(m, jnp.max(s, axis=0, keepdims=True))
                p = jnp.exp(s - m_new)
                a = jnp.exp(m - m_new)
                m_ref[h] = m_new
                l_ref[h] = a * l_ref[h] + jnp.sum(p, axis=0, keepdims=True)
                acc_ref[h] = a * acc_ref[h] + _dot(v_ref[h, :, pl.ds(off, size)], p, NN)

        def blocks(first, count):
            bias_ = jnp.concatenate([b_ref[qb - first - j + 1] for j in range(count)], axis=0)
            keys(pl.multiple_of(first * 128, 128), 128 * count, bias_)

        nkb = qb + 1
        @pl.loop(0, nkb // 4)
        def _(i):
            blocks(4 * i, 4)

        @pl.when(nkb % 4 >= 2)
        def _():
            blocks(nkb // 4 * 4, 2)

        @pl.when(nkb % 2 == 1)
        def _():
            blocks(qb, 1)

        outs = []
        for h in range(KV_PER_STEP):
            outs.append(acc_ref[h] / l_ref[h])
            o_ref[h] = outs[h]
            lse_ref[h] = m_ref[h] + jnp.log(l_ref[h])
        rows_ref[...] = _blocks_to_rows(outs).astype(BF16)

    kvs = KV_PER_STEP
    qspec = pl.BlockSpec((kvs, None, HD, QROWS), lambda j, i: (j, i, 0, 0))
    return _call(
        body, "attn_fwd", (NKV // kvs, NCH),
        [qspec, pl.BlockSpec((kvs, S, HD), lambda j, i: (j, 0, 0)),
         pl.BlockSpec((kvs, HD, S), lambda j, i: (j, 0, 0)),
         pl.BlockSpec((NBIAS, 128, QROWS), lambda j, i: (0, 0, 0))],
        [qspec, pl.BlockSpec((kvs, None, 1, QROWS), lambda j, i: (j, i, 0, 0)),
         pl.BlockSpec((128, QCOLS * kvs), lambda j, i: (i, j))],
        [jax.ShapeDtypeStruct((NKV, NCH, HD, QROWS), F32), jax.ShapeDtypeStruct((NKV, NCH, 1, QROWS), F32),
         jax.ShapeDtypeStruct((S, 2 * D), BF16)],
        (qt, kh, vt, bias),
        [pltpu.VMEM((kvs, 1, QROWS), F32), pltpu.VMEM((kvs, 1, QROWS), F32), pltpu.VMEM((kvs, HD, QROWS), F32)],
        ("parallel", "parallel"), rider)


def _attn_bwd(qt, kh, kt, vh, dot_, lse, delta, bias, rider=None):
    def body(qt_ref, k_ref, kt_ref, v_ref, dot_ref, lse_ref, dl_ref, b_ref, dq_ref, dk_ref, dv_ref):
        kp = pl.program_id(1)

        @pl.when(kp == 0)
        def _():
            dq_ref[...] = jnp.zeros_like(dq_ref)

        dk_ref[...] = jnp.zeros_like(dk_ref)
        dv_ref[...] = jnp.zeros_like(dv_ref)

        @pl.loop(2 * kp, NCH // 2)
        def _(j):
            for h in range(KV_PER_STEP):
                k, kt_, v = k_ref[h], kt_ref[h], v_ref[h]
                for qb in (2 * j, 2 * j + 1):
                    bias2 = jnp.concatenate([b_ref[jnp.maximum(qb - 4 * kp - t + 1, 0)] for t in range(4)], axis=0)
                    st = _dot(k, qt_ref[h, qb], NN) + bias2
                    pt = jnp.exp(st - lse_ref[h, qb])
                    dst = pt * (_dot(v, dot_ref[h, qb], NN) - dl_ref[h, qb])
                    dq_ref[h, qb] += _dot(kt_, dst, NN)
                    dk_ref[h] += _dot(qt_ref[h, qb], dst, NT)
                    dv_ref[h] += _dot(dot_ref[h, qb], pt, NT)

    kvs = KV_PER_STEP
    tspec = pl.BlockSpec((kvs, NCH, HD, QROWS), lambda j, i: (j, 0, 0, 0))
    kspec = pl.BlockSpec((kvs, 512, HD), lambda j, i: (j, i, 0))
    ktspec = pl.BlockSpec((kvs, HD, 512), lambda j, i: (j, 0, i))
    sspec = pl.BlockSpec((kvs, NCH, 1, QROWS), lambda j, i: (j, 0, 0, 0))
    return _call(
        body, "attn_bwd", (NKV // kvs, NCH // 4),
        [tspec, kspec, ktspec, kspec, tspec, sspec, sspec,
         pl.BlockSpec((NBIAS, 128, QROWS), lambda j, i: (0, 0, 0))],
        [tspec, ktspec, ktspec],
        [jax.ShapeDtypeStruct((NKV, NCH, HD, QROWS), F32),
         jax.ShapeDtypeStruct((NKV, HD, S), F32), jax.ShapeDtypeStruct((NKV, HD, S), F32)],
        (qt, kh, kt, vh, dot_, lse, delta, bias), sem=("parallel", "arbitrary"), rider=rider)


CONV_BLK = 256
CONV_COL0 = 1536 // CONV_BLK


CONV_ROWS = 128


def _conv_fwd(proj, convw, convb):
    trips = S // CONV_ROWS

    def body(u_ref, w_ref, b_ref, o_ref, y_ref):
        @pl.loop(0, trips)
        def _(c):
            t0 = pl.multiple_of(c * CONV_ROWS, CONV_ROWS)
            before = pl.multiple_of(jnp.maximum(t0 - 8, 0), 8)
            ext = jnp.concatenate([jnp.where(c == 0, 0.0, u_ref[pl.ds(before, 8), :]),
                                   u_ref[pl.ds(t0, CONV_ROWS), :]], axis=0)
            y = b_ref[...] + w_ref[CONV_K - 1:CONV_K, :] * ext[8:]
            for j in range(1, CONV_K):
                y = y + w_ref[CONV_K - 1 - j:CONV_K - j, :] * pltpu.roll(ext, j, 0)[8:]
            y_ref[pl.ds(t0, CONV_ROWS), :] = y
            o_ref[pl.ds(t0, CONV_ROWS), :] = y * _sigmoid(y)

    out = pl.BlockSpec((S, CONV_BLK), lambda i: (0, i))
    return pl.pallas_call(
        body, name="conv_fwd", grid=(CONV_C // CONV_BLK,),
        in_specs=[pl.BlockSpec((S, CONV_BLK), lambda i: (0, CONV_COL0 + i)),
                  pl.BlockSpec((CONV_K, CONV_BLK), lambda i: (0, i)),
                  pl.BlockSpec((1, CONV_BLK), lambda i: (0, i))],
        out_specs=[out, out], out_shape=[jax.ShapeDtypeStruct((S, CONV_C), F32)] * 2,
        compiler_params=_cparams("parallel"),
    )(proj, convw, convb)


def _conv_bwd(dact, ypre, proj, convw, dproj):
    trips = S // CONV_ROWS

    def body(da_ref, y_ref, u_ref, w_ref, buf_ref, du_ref, dw_ref, db_ref):
        dw_ref[...] = jnp.zeros_like(dw_ref)
        db_ref[...] = jnp.zeros_like(db_ref)
        r8 = lax.broadcasted_iota(jnp.int32, (8, CONV_BLK), 0)

        def dy_of(rows):
            y = y_ref[rows, :]
            sg = _sigmoid(y)
            return da_ref[rows, :] * (sg * (1.0 + y * (1.0 - sg)))

        @pl.loop(0, trips)
        def _(c):
            t0 = pl.multiple_of(c * CONV_ROWS, CONV_ROWS)
            after = pl.multiple_of(jnp.minimum(t0 + CONV_ROWS, S - 8), 8)
            ext = jnp.concatenate([dy_of(pl.ds(t0, CONV_ROWS)),
                                   jnp.where(c == trips - 1, 0.0, dy_of(pl.ds(after, 8)))], axis=0)
            u = u_ref[pl.ds(t0, CONV_ROWS), :]
            du, dw = None, jnp.zeros((8, CONV_BLK), F32)
            for j in range(CONV_K):
                dyj = (ext if j == 0 else pltpu.roll(ext, CONV_ROWS + 8 - j, 0))[:CONV_ROWS]
                term = w_ref[CONV_K - 1 - j:CONV_K - j, :] * dyj
                du = term if du is None else du + term
                dw = dw + jnp.where(r8 == CONV_K - 1 - j, jnp.sum(dyj * u, axis=0, keepdims=True), 0.0)
            du_ref[pl.ds(t0, CONV_ROWS), :] = du.astype(BF16)
            dw_ref[...] += dw
            db_ref[...] += jnp.sum(ext[:CONV_ROWS], axis=0, keepdims=True)

    return pl.pallas_call(
        body, name="conv_bwd", grid=(CONV_C // CONV_BLK,),
        in_specs=[pl.BlockSpec((S, CONV_BLK), lambda i: (0, i)), pl.BlockSpec((S, CONV_BLK), lambda i: (0, i)),
                  pl.BlockSpec((S, CONV_BLK), lambda i: (0, CONV_COL0 + i)),
                  pl.BlockSpec((CONV_K, CONV_BLK), lambda i: (0, i)), pl.BlockSpec(memory_space=pl.ANY)],
        out_specs=[pl.BlockSpec((S, CONV_BLK), lambda i: (0, CONV_COL0 + i)),
                   pl.BlockSpec((8, CONV_BLK), lambda i: (0, i)), pl.BlockSpec((1, CONV_BLK), lambda i: (0, i))],
        out_shape=[jax.ShapeDtypeStruct(dproj.shape, BF16), jax.ShapeDtypeStruct((8, CONV_C), F32),
                   jax.ShapeDtypeStruct((1, CONV_C), F32)],
        input_output_aliases={4: 0}, compiler_params=_cparams("parallel"),
    )(dact, ypre, proj, convw, dproj)


NPAIR = 8


def _ssd_scalars(dtr_ref, dtb_ref, alog_ref):
    z = dtr_ref[...] + dtb_ref[...]
    dt = jnp.maximum(z, 0.0) + jnp.log(1.0 + jnp.exp(-jnp.abs(z)))
    a = -jnp.exp(alog_ref[...])
    r = lax.broadcasted_iota(jnp.int32, (128, 128), 0)
    c = lax.broadcasted_iota(jnp.int32, (128, 128), 1)
    tri = (r >= c).astype(F32)
    cs = _dot_exact(tri, dt * a)
    return z, dt, a, cs, r, c


def _by_lane(cs, dt):
    head = lax.broadcasted_iota(jnp.int32, (128, SSM_W), 0)
    lane = lax.broadcasted_iota(jnp.int32, (128, SSM_W), 1)
    sel = (head == lane // HD).astype(F32)
    cs_l = _dot_exact(cs, sel, "b")
    last_l = cs_l[127:128, :]
    return sel, jnp.exp(cs_l), jnp.exp(last_l - cs_l), _dot_exact(dt, sel, "b")


def _pair_terms(cs, h1, h2):
    return (cs[:, h1:h1 + 1], cs[:, h2:h2 + 1],
            jnp.exp(cs[127:128, h1:h1 + 1]), jnp.exp(cs[127:128, h2:h2 + 1]))


def _gate_norm(y, zv, w):
    yg = y * (zv * _sigmoid(zv))
    outs, rs = [], []
    for g in range(2):
        blk = yg[:, 512 * g:512 * (g + 1)]
        r = lax.rsqrt(jnp.mean(blk * blk, axis=-1, keepdims=True) + EPS)
        outs.append(blk * r)
        rs.append(r)
    return jnp.concatenate(outs, axis=1), rs, yg


def _ssd_fwd(xbc, proj, dtb, alog, dskip_l, ssmw, mixed):
    def body(x_ref, b_ref, c_ref, dtr_ref, z_ref, dtb_ref, alog_ref, dsk_ref, w_ref, buf_ref,
             y_ref, yn_ref, hp_ref, h_ref):
        @pl.when(pl.program_id(0) == 0)
        def _():
            h_ref[...] = jnp.zeros_like(h_ref)

        _, dt, _, cs, r, c = _ssd_scalars(dtr_ref, dtb_ref, alog_ref)
        cst = cs.T
        causal = r >= c
        lo = c < HD
        _, e_all, dte_all, dt_all = _by_lane(cs, dt)
        hp_ref[...] = h_ref[...]
        for g in range(2):
            bg = b_ref[:, 128 * g:128 * (g + 1)]
            cg = c_ref[:, 128 * g:128 * (g + 1)]
            cb = _dot(cg, bg, NT)
            for j in range(4):
                pj = 4 * g + j
                h1, h2 = 2 * pj, 2 * pj + 1
                sl = slice(128 * pj, 128 * (pj + 1))
                xp = x_ref[:, sl]
                c1, c2, cd1, cd2 = _pair_terms(cs, h1, h2)
                e_l, dte_l = e_all[:, sl], dte_all[:, sl]
                xdt = xp * dt_all[:, sl]
                m1 = cb * jnp.exp(jnp.where(causal, c1 - cst[h1:h1 + 1, :], NEG))
                m2 = cb * jnp.exp(jnp.where(causal, c2 - cst[h2:h2 + 1, :], NEG))
                yd = jnp.where(lo, _dot(m1, xdt, NN), _dot(m2, xdt, NN))
                hp = h_ref[pj]
                yo = _dot(cg, hp, NT) * e_l
                st = _dot(xdt * dte_l, bg, TN)
                h_ref[pj] = hp * jnp.where(r < HD, cd1, cd2) + st
                y_ref[:, sl] = yd + yo + dsk_ref[:, sl] * xp
        yn, _, _ = _gate_norm(y_ref[...], z_ref[...], w_ref[...])
        yn_ref[...] = (yn * w_ref[...]).astype(BF16)

    return pl.pallas_call(
        body, name="ssd_fwd", grid=(NCH,),
        in_specs=[pl.BlockSpec((128, SSM_W), lambda i: (i, 0)),
                  pl.BlockSpec((128, 256), lambda i: (i, 4)), pl.BlockSpec((128, 256), lambda i: (i, 5)),
                  pl.BlockSpec((128, 128), lambda i: (i, COL_DT // 128)),
                  pl.BlockSpec((128, SSM_W), lambda i: (i, 3)),
                  pl.BlockSpec((1, 128), lambda i: (0, 0)), pl.BlockSpec((1, 128), lambda i: (0, 0)),
                  pl.BlockSpec((1, SSM_W), lambda i: (0, 0)), pl.BlockSpec((1, SSM_W), lambda i: (0, 0)),
                  pl.BlockSpec(memory_space=pl.ANY)],
        out_specs=[pl.BlockSpec((128, SSM_W), lambda i: (i, 0)), pl.BlockSpec((128, SSM_W), lambda i: (i, 1)),
                   pl.BlockSpec((None, NPAIR, 128, 128), lambda i: (i, 0, 0, 0))],
        out_shape=[jax.ShapeDtypeStruct((S, SSM_W), F32), jax.ShapeDtypeStruct(mixed.shape, BF16),
                   jax.ShapeDtypeStruct((NCH, NPAIR, 128, 128), F32)],
        scratch_shapes=[pltpu.VMEM((NPAIR, 128, 128), F32)],
        input_output_aliases={9: 1}, compiler_params=_cparams("arbitrary"),
    )(xbc, xbc, xbc, proj, proj, dtb, alog, dskip_l, ssmw, mixed)


def _ssd_bwd(dmixed, y, xbc, proj, hprev, dtb, alog, dskip_l, ssmw, rider=None):
    def body(dyn_ref, y_ref, x_ref, b_ref, c_ref, dtr_ref, z_ref, hp_ref, dtb_ref, alog_ref, dsk_ref, w_ref,
             dxbc_ref, dz_ref, ddt_ref, dw_ref, dsc_ref, g_ref):
        @pl.when(pl.program_id(0) == 0)
        def _():
            g_ref[...] = jnp.zeros_like(g_ref)
            dsc_ref[...] = jnp.zeros_like(dsc_ref)

        z, dt, a, cs, r, c = _ssd_scalars(dtr_ref, dtb_ref, alog_ref)
        cst = cs.T
        causal = r >= c
        lo = c < HD

        yv = y_ref[...]
        zv = z_ref[...]
        wv = w_ref[...]
        ygn, rs, yg = _gate_norm(yv, zv, wv)
        dyn = dyn_ref[...]
        _acc_rows(dw_ref, dyn * ygn)
        dynw = dyn * wv
        parts = []
        for g in range(2):
            sl = slice(512 * g, 512 * (g + 1))
            a_g, n_g = dynw[:, sl], ygn[:, sl]
            parts.append(rs[g] * (a_g - n_g * jnp.mean(a_g * n_g, axis=-1, keepdims=True)))
        dyg = jnp.concatenate(parts, axis=1)
        sz = _sigmoid(zv)
        dz_ref[...] = (dyg * yv * (sz * (1.0 + zv * (1.0 - sz)))).astype(BF16)
        dy_all = dyg * (zv * sz)

        dcs_cols = jnp.zeros((128, 128), F32)
        dcs_rows = jnp.zeros((128, 128), F32)
        sel, e_all, dte_all, dt_all = _by_lane(cs, dt)
        x_all, b_all, c_all, dsk_all = x_ref[...], b_ref[...], c_ref[...], dsk_ref[...]
        hp_all, g_all = hp_ref[...], g_ref[...]
        g_new, dx_parts, db_parts, dc_parts = [], [], [], []
        dyx_parts, ryo_parts, qx_parts, dxx_parts, gh_parts = [], [], [], [], []
        for g in range(2):
            bg = b_all[:, 128 * g:128 * (g + 1)]
            cg = c_all[:, 128 * g:128 * (g + 1)]
            cb = _dot(cg, bg, NT)
            dcb = jnp.zeros((128, 128), F32)
            db_acc = jnp.zeros((128, NST), F32)
            dc_acc = jnp.zeros((128, NST), F32)
            for j in range(4):
                pj = 4 * g + j
                h1, h2 = 2 * pj, 2 * pj + 1
                sl = slice(128 * pj, 128 * (pj + 1))
                xp = x_all[:, sl]
                dyp = dy_all[:, sl]
                c1, c2, cd1, cd2 = _pair_terms(cs, h1, h2)
                e_l, dte_l, dt_l = e_all[:, sl], dte_all[:, sl], dt_all[:, sl]
                xdt = xp * dt_l
                hp = hp_all[pj]
                gp = g_all[pj]
                dyx_parts.append(dyp * xp)
                dzs = dyp * e_l
                dc_acc = dc_acc + _dot(dzs, hp, NN)
                ryo_parts.append(dyp * (_dot(cg, hp, NT) * e_l))
                qm = _dot(bg, gp, NT)
                dxdt = qm * dte_l
                qx_parts.append(qm * xdt)
                db_acc = db_acc + _dot(xdt * dte_l, gp, NN)
                gh_parts.append(gp * hp)
                g_new.append(_dot(dzs, cg, TN) + jnp.where(r < HD, cd1, cd2) * gp)
                for hh, ch, msk in ((h1, c1, lo), (h2, c2, jnp.logical_not(lo))):
                    lm = jnp.exp(jnp.where(causal, ch - cst[hh:hh + 1, :], NEG))
                    mm = cb * lm
                    dm = jnp.where(causal, _dot(jnp.where(msk, dyp, 0.0), xdt, NT), 0.0)
                    w = dm * mm
                    dcs_cols = dcs_cols + jnp.where(c == hh, jnp.sum(w, axis=1, keepdims=True), 0.0)
                    dcs_rows = dcs_rows + jnp.where(r == hh, jnp.sum(w, axis=0, keepdims=True), 0.0)
                    dcb = dcb + dm * lm
                    dxdt = dxdt + jnp.where(msk, _dot(mm, dyp, TN), 0.0)
                dxx_parts.append(dxdt * xp)
                dx_parts.append(dsk_all[:, sl] * dyp + dxdt * dt_l)
            db_parts.append(db_acc + _dot(dcb, cg, TN))
            dc_parts.append(dc_acc + _dot(dcb, bg, NN))
        g_ref[...] = jnp.stack(g_new)
        dxbc_ref[...] = jnp.concatenate(dx_parts + db_parts + dc_parts, axis=1)

        selt = (lax.broadcasted_iota(jnp.int32, (SSM_W, 128), 0) // HD
                == lax.broadcasted_iota(jnp.int32, (SSM_W, 128), 1)).astype(F32)

        def by_head(parts):
            return _dot_exact(jnp.concatenate(parts, axis=1), selt, "b")

        ddt_x = by_head(dxx_parts)
        dd_row = jnp.sum(by_head(dyx_parts), axis=0, keepdims=True)
        t_all = by_head(qx_parts) * jnp.exp(cs[127:128, :] - cs)
        gh = jnp.sum(_dot_exact(sel, jnp.concatenate(gh_parts, axis=0)), axis=1, keepdims=True)
        gh_row = jnp.broadcast_to(gh, (128, 128)).T[0:1, :]
        at_end = jnp.sum(t_all, axis=0, keepdims=True) + gh_row * jnp.exp(cs[127:128, :])
        dcs = by_head(ryo_parts) - t_all + dcs_cols + jnp.where(r == 127, at_end, 0.0) - dcs_rows.T
        dad = _dot_exact((c >= r).astype(F32), dcs)
        ddt = dad * a + ddt_x
        ddtr = jnp.where(c < 16, ddt * _sigmoid(z), 0.0)
        ddt_ref[...] = ddtr.astype(BF16)
        r8 = lax.broadcasted_iota(jnp.int32, (8, 128), 0)
        dsc_ref[...] += (jnp.where(r8 == 0, jnp.sum(ddtr, axis=0, keepdims=True), 0.0)
                         + jnp.where(r8 == 1, jnp.sum(dad * dt, axis=0, keepdims=True) * a, 0.0)
                         + jnp.where(r8 == 2, dd_row, 0.0))

    rev = NCH - 1
    return _call(
        body, "ssd_bwd", (NCH,),
        [pl.BlockSpec((128, SSM_W), lambda i: (rev - i, 0)),
         pl.BlockSpec((128, SSM_W), lambda i: (rev - i, 0)),
         pl.BlockSpec((128, SSM_W), lambda i: (rev - i, 0)),
         pl.BlockSpec((128, 256), lambda i: (rev - i, 4)), pl.BlockSpec((128, 256), lambda i: (rev - i, 5)),
         pl.BlockSpec((128, 128), lambda i: (rev - i, COL_DT // 128)),
         pl.BlockSpec((128, SSM_W), lambda i: (rev - i, 3)),
         pl.BlockSpec((None, NPAIR, 128, 128), lambda i: (rev - i, 0, 0, 0)),
         pl.BlockSpec((1, 128), lambda i: (0, 0)), pl.BlockSpec((1, 128), lambda i: (0, 0)),
         pl.BlockSpec((1, SSM_W), lambda i: (0, 0)), pl.BlockSpec((1, SSM_W), lambda i: (0, 0))],
        [pl.BlockSpec((128, CONV_C), lambda i: (rev - i, 0)),
         pl.BlockSpec((128, SSM_W), lambda i: (rev - i, 3)),
         pl.BlockSpec((128, 128), lambda i: (rev - i, 0)),
         pl.BlockSpec((1, SSM_W), lambda i: (0, 0)), pl.BlockSpec((8, 128), lambda i: (0, 0))],
        [jax.ShapeDtypeStruct((S, CONV_C), F32), jax.ShapeDtypeStruct((S, WIN_PAD), BF16),
         jax.ShapeDtypeStruct((S, 128), BF16), jax.ShapeDtypeStruct((1, SSM_W), F32),
         jax.ShapeDtypeStruct((8, 128), F32)],
        (dmixed, y, xbc, xbc, xbc, proj, proj, hprev, dtb, alog, dskip_l, ssmw),
        [pltpu.VMEM((NPAIR, 128, 128), F32)], ("arbitrary",), rider)


def _cast_stack(name, slot, arrs, tr, tc):
    n = len(arrs)
    rows, cols = arrs[0].shape

    def body(s_ref, *refs):
        for i in range(n):
            refs[n][i] = refs[i][...].astype(BF16)

    return pl.pallas_call(
        body, name=name,
        grid_spec=pltpu.PrefetchScalarGridSpec(
            num_scalar_prefetch=1, grid=(rows // tr, cols // tc),
            in_specs=[pl.BlockSpec((tr, tc), lambda i, j, sr: (i, j))] * n,
            out_specs=pl.BlockSpec((None, n, tr, tc), lambda i, j, sr: (sr[0], 0, i, j))),
        out_shape=jax.ShapeDtypeStruct((NSH, n, rows, cols), BF16),
        compiler_params=_cparams("parallel", "parallel"),
    )(slot, *arrs)


def _pair_sum(name, c_idx, ps, th):
    n = len(ps)
    _, rows, _ = ps[0].shape

    def body(c_ref, *refs):
        mine, whole, out, theirs = refs[:n], refs[n:2 * n], refs[2 * n:3 * n], refs[3 * n:4 * n]
        send, recv = refs[4 * n], refs[4 * n + 1]
        s, i = pl.program_id(0), pl.program_id(1)
        x, y, c, _ = _place()

        def copies(slot):
            return [_rcopy(whole[k].at[slot, :, pl.ds((1 - c) * HALF, HALF)], theirs[k].at[slot],
                           send.at[slot * n + k], recv.at[slot * n + k], (x, y, 1 - c)) for k in range(n)]

        @pl.when((s == 0) & (i == 0))
        def _():
            for slot in range(NSH):
                for cp in copies(slot):
                    cp.start()

        @pl.when(i == 0)
        def _():
            for slot in range(NSH):
                @pl.when(s == slot)
                def _():
                    for cp in copies(slot):
                        cp.wait()

        rows_i = slice(None) if th == rows else pl.ds(pl.multiple_of(i * th, th), th)
        for k in range(n):
            out[k][...] = (mine[k][...].astype(F32) + theirs[k][s, rows_i, :].astype(F32)).astype(BF16)

    spec = pl.BlockSpec((None, th, HALF), lambda s, i, cr: (s, i, 0))
    return pl.pallas_call(
        body, name=name,
        grid_spec=pltpu.PrefetchScalarGridSpec(
            num_scalar_prefetch=1, grid=(NSH, rows // th),
            in_specs=[pl.BlockSpec((None, th, HALF), lambda s, i, cr: (s, i, cr[0]))] * n + _any_specs(n),
            out_specs=[spec] * n,
            scratch_shapes=[pltpu.VMEM((NSH, rows, HALF), BF16)] * n
            + [pltpu.SemaphoreType.DMA((NSH * n,)), pltpu.SemaphoreType.DMA((NSH * n,))]),
        out_shape=[jax.ShapeDtypeStruct((NSH, rows, HALF), BF16)] * n,
        compiler_params=_cparams("arbitrary", "arbitrary"),
    )(c_idx, *ps, *ps)


def _pair_add(name, c_idx, ps, theirs, th):
    n = len(ps)
    _, rows, _ = ps[0].shape

    def body(c_ref, *refs):
        for k in range(n):
            refs[2 * n + k][...] = (refs[k][...].astype(F32) + refs[n + k][...].astype(F32)).astype(BF16)

    spec = pl.BlockSpec((None, th, HALF), lambda s, i, cr: (s, i, 0))
    return pl.pallas_call(
        body, name=name,
        grid_spec=pltpu.PrefetchScalarGridSpec(
            num_scalar_prefetch=1, grid=(NSH, rows // th),
            in_specs=[pl.BlockSpec((None, th, HALF), lambda s, i, cr: (s, i, cr[0]))] * n + [spec] * n,
            out_specs=[spec] * n),
        out_shape=[jax.ShapeDtypeStruct((NSH, rows, HALF), BF16)] * n,
        compiler_params=_cparams("parallel", "parallel"),
    )(c_idx, *ps, *theirs)


def _chip_sum(name, place, cs, ts, th):
    n = len(ts)
    _, rows, _ = ts[0].shape

    def body(p_ref, *refs):
        for i in range(n):
            t = refs[n + i][...].astype(F32)
            refs[2 * n + i][...] = ((refs[i][...].astype(F32) + t[0]) + t[1]) + t[2]

    return pl.pallas_call(
        body, name=name,
        grid_spec=pltpu.PrefetchScalarGridSpec(
            num_scalar_prefetch=1, grid=(rows // th,),
            in_specs=[pl.BlockSpec((None, th, HALF), lambda i, pr: (pr[0], i, 0))] * n
            + [pl.BlockSpec((3, th, HALF), lambda i, pr: (0, i, 0))] * n,
            out_specs=[pl.BlockSpec((th, HALF), lambda i, pr: (i, pr[1]))] * n),
        out_shape=[jax.ShapeDtypeStruct((rows, D), F32)] * n, compiler_params=_cparams("parallel"),
    )(place, *cs, *ts)


def _adamw(name, ws, gs, ms, vs, tr, tc):
    n = len(ws)
    shape = ws[0].shape
    rows, cols, mid = shape[0], shape[-1], shape[1:-1]
    c1 = 1.0 / (1.0 - ADAM_B1 ** ADAM_STEP)
    c2 = 1.0 / (1.0 - ADAM_B2 ** ADAM_STEP)

    def body(*refs):
        for i in range(n):
            w, g, m, v = (refs[k * n + i][...] for k in range(4))
            m2 = ADAM_B1 * m + (1.0 - ADAM_B1) * g
            v2 = ADAM_B2 * v + (1.0 - ADAM_B2) * (g * g)
            refs[4 * n + 4 * i][...] = -ADAM_LR * ((m2 * c1) / (jnp.sqrt(v2 * c2) + ADAM_EPS) + ADAM_WD * w)
            refs[4 * n + 4 * i + 1][...] = m2
            refs[4 * n + 4 * i + 2][...] = v2
            refs[4 * n + 4 * i + 3][...] = g

    spec = pl.BlockSpec((tr,) + mid + (tc,), lambda i, j: (i,) + (0,) * len(mid) + (j,))
    outs = pl.pallas_call(
        body, name=name, grid=(rows // tr, cols // tc), in_specs=[spec] * (4 * n), out_specs=[spec] * (4 * n),
        out_shape=[jax.ShapeDtypeStruct(shape, F32)] * (4 * n),
        compiler_params=_cparams("parallel", "parallel"),
    )(*ws, *gs, *ms, *vs)
    return [tuple(outs[4 * i:4 * i + 4]) for i in range(n)]


def _place():
    x, y, c = lax.axis_index("x"), lax.axis_index("y"), lax.axis_index("c")
    chips = [(1 - x, y), (x, 1 - y), (1 - x, 1 - y)]
    return x, y, c, chips


def _any_specs(n):
    return [pl.BlockSpec(memory_space=pl.ANY)] * n


def _rcopy(src, dst, send_sem, recv_sem, dev):
    return pltpu.make_async_remote_copy(src_ref=src, dst_ref=dst, send_sem=send_sem, recv_sem=recv_sem,
                                        device_id=dev, device_id_type=MESH)


QUARTER = HALF // 2

XA, XB, YA, YB, RX, RY, F_XA, F_XB, F_YA, F_YB, F_D0, F_D1 = range(12)


def _gather_rider(bufs, views, at=None):
    n = len(bufs)

    def plan(rout, sems):
        send, recv = sems
        x, y, c, _ = _place()
        me, sx, sy, sd = 2 * x + y, 2 * (1 - x) + y, 2 * x + (1 - y), 2 * (1 - x) + (1 - y)
        nx, ny, sib = (1 - x, y, c), (x, 1 - y, c), (x, y, 1 - c)
        q0, q1 = c * HALF, c * HALF + QUARTER
        o0, o1 = (1 - c) * HALF, (1 - c) * HALF + QUARTER
        out = {XA: (me, q1, nx), XB: (me, q0, nx), YA: (me, q0, ny), YB: (me, q1, ny),
               RX: (sy, q0, nx), RY: (sx, q1, ny),
               F_XA: (sx, q1, sib), F_XB: (sx, q0, sib), F_YA: (sy, q0, sib), F_YB: (sy, q1, sib),
               F_D0: (sd, q0, sib), F_D1: (sd, q1, sib)}
        inn = {XA: (sx, q1), XB: (sx, q0), YA: (sy, q0), YB: (sy, q1), RX: (sd, q0), RY: (sd, q1),
               F_XA: (sx, o1), F_XB: (sx, o0), F_YA: (sy, o0), F_YB: (sy, o1), F_D0: (sd, o0), F_D1: (sd, o1)}

        def copy(kind, b):
            slot, col, dev = out[kind]
            win = views[b](rout[b], slot, col, QUARTER)
            return _rcopy(win, win, send.at[kind * n + b], recv.at[kind * n + b], dev)

        def land(kind, b):
            slot, col = inn[kind]
            win = views[b](rout[b], slot, col, QUARTER)
            return _rcopy(win, win, send.at[kind * n + b], recv.at[kind * n + b], (x, y, c))

        return copy, land

    first = (XA, YA, XB, YB)
    early = ((XA, (RY, F_XA)), (YA, (RX, F_YA)))
    late = ((XB, (F_XB,)), (YB, (F_YB,)), (RX, (F_D0,)), (RY, (F_D1,)))
    forwards = (F_XA, F_XB, F_YA, F_YB, F_D0, F_D1)
    sent = first + (RX, RY) + forwards

    def start(rin, rout, sems):
        copy, _ = plan(rout, sems)
        for kind in first:
            for b in range(n):
                copy(kind, b).start()

    def pass_on(rout, sems, links):
        copy, land = plan(rout, sems)
        for landed, then in links:
            for b in range(n):
                land(landed, b).wait_recv()
                for kind in then:
                    copy(kind, b).start()

    def between(rin, rout, sems):
        pass_on(rout, sems, early)

    def finish(rin, rout, sems):
        pass_on(rout, sems, late)
        copy, land = plan(rout, sems)
        for kind in forwards:
            for b in range(n):
                land(kind, b).wait_recv()
        for kind in sent:
            for b in range(n):
                copy(kind, b).wait_send()

    return _Rider(list(bufs), [jax.ShapeDtypeStruct(a.shape, a.dtype) for a in bufs], {b: b for b in range(n)},
                  [pltpu.SemaphoreType.DMA((12 * n,))] * 2, start, finish, between, at)


def _small_gather_rider(cw):
    def descs(rin, rout, sems, x, y, c, chips):
        return [_rcopy(rin[0], rout[0].at[2 * x + y], sems[1].at[j], sems[2].at[j], (chip[0], chip[1], c))
                for j, chip in enumerate(chips)]

    def start(rin, rout, sems):
        x, y, c, chips = _place()
        pltpu.make_async_copy(rin[0], rout[0].at[2 * x + y], sems[0].at[0]).start()
        for cp in descs(rin, rout, sems, x, y, c, chips):
            cp.start()

    def finish(rin, rout, sems):
        x, y, c, chips = _place()
        for j, chip in enumerate(chips):
            _rcopy(rin[0], rout[0].at[2 * chip[0] + chip[1]], sems[1].at[j], sems[2].at[j], (x, y, c)).wait_recv()
        for cp in descs(rin, rout, sems, x, y, c, chips):
            cp.wait_send()
        pltpu.make_async_copy(rin[0], rout[0].at[2 * x + y], sems[0].at[0]).wait()

    return _Rider([cw], [jax.ShapeDtypeStruct((NSH,) + cw.shape, cw.dtype)], {},
                  [pltpu.SemaphoreType.DMA((1,)), pltpu.SemaphoreType.DMA((3,)), pltpu.SemaphoreType.DMA((3,))],
                  start, finish)


def _to_sibling_rider(ps):
    n = len(ps)

    def descs(rin, rout, sems):
        x, y, c, _ = _place()
        return [_rcopy(rin[i].at[:, :, pl.ds((1 - c) * HALF, HALF)], rout[i], sems[0].at[i], sems[1].at[i],
                       (x, y, 1 - c)) for i in range(n)]

    def start(rin, rout, sems):
        for cp in descs(rin, rout, sems):
            cp.start()

    def finish(rin, rout, sems):
        for cp in descs(rin, rout, sems):
            cp.wait()

    return _Rider(list(ps), [jax.ShapeDtypeStruct(a.shape[:2] + (HALF,), a.dtype) for a in ps], {},
                  [pltpu.SemaphoreType.DMA((n,))] * 2, start, finish)


def _to_chips_rider(cs, first=0, count=None, into=None):
    n = len(cs)
    rows = [pl.ds(first, a.shape[1] - first if count is None else count) for a in cs]

    def descs(rin, rout, sems):
        x, y, c, chips = _place()
        return [_rcopy(rin[i].at[2 * chip[0] + chip[1], rows[i]], rout[i].at[j, rows[i]], sems[0].at[j * n + i],
                       sems[1].at[j * n + i], (chip[0], chip[1], c)) for j, chip in enumerate(chips) for i in range(n)]

    def start(rin, rout, sems):
        for cp in descs(rin, rout, sems):
            cp.start()

    def finish(rin, rout, sems):
        for cp in descs(rin, rout, sems):
            cp.wait()

    return _Rider(list(cs) + list(into or []), [jax.ShapeDtypeStruct((3,) + a.shape[1:], a.dtype) for a in cs],
                  {n + i: i for i in range(n)} if into else {}, [pltpu.SemaphoreType.DMA((3 * n,))] * 2, start, finish)


def _join_riders(riders):
    counts = [[len(r.operands) for r in riders], [len(r.out_shapes) for r in riders], [len(r.sems) for r in riders]]

    def each(step):
        def run(*refs):
            at = [0, 0, 0]
            for i, r in enumerate(riders):
                parts = [group[at[k]:at[k] + counts[k][i]] for k, group in enumerate(refs)]
                at = [at[k] + counts[k][i] for k in range(3)]
                step(r)(*parts)
        return run

    aliases = {sum(counts[0][:i]) + k: sum(counts[1][:i]) + v
               for i, r in enumerate(riders) for k, v in r.aliases.items()}
    return _Rider([a for r in riders for a in r.operands], [s for r in riders for s in r.out_shapes], aliases,
                  [s for r in riders for s in r.sems], each(lambda r: r.start), each(lambda r: r.finish),
                  each(lambda r: r.between or (lambda *refs: None)))


SMALL_ROWS = 16


def _swap_halves(gs, vec):
    n = len(gs)

    def body(*refs):
        v_ref, dst, o_ref = refs[n], refs[n + 1:2 * n + 1], refs[2 * n + 1]
        buf, send, recv, vsend, vrecv = refs[2 * n + 2:]
        x, y, c, _ = _place()
        cps = []
        for i in range(n):
            mine = dst[i].at[:, pl.ds(c * HALF, HALF)]
            cps.append(_rcopy(mine, mine, send.at[i], recv.at[i], (x, y, 1 - c)))
        for cp in cps:
            cp.start()

        me = 4 * x + 2 * y + c
        buf[me] = v_ref[...]
        vcps = []
        for k in range(1, 8):
            peer = (x ^ (k >> 2), y ^ ((k >> 1) & 1), c ^ (k & 1))
            vcps.append(_rcopy(v_ref, buf.at[me], vsend.at[k - 1], vrecv.at[k - 1], peer))
        for cp in vcps:
            cp.start()
        for k in range(1, 8):
            _rcopy(v_ref, buf.at[me ^ k], vsend.at[k - 1], vrecv.at[k - 1], (x, y, c)).wait_recv()
        for cp in vcps:
            cp.wait_send()
        t = buf[0]
        for d in range(1, 8):
            t = t + buf[d]
        o_ref[...] = t

        for i in range(n):
            other = dst[i].at[:, pl.ds((1 - c) * HALF, HALF)]
            _rcopy(other, other, send.at[i], recv.at[i], (x, y, c)).wait_recv()
        for cp in cps:
            cp.wait_send()

    vmem = pl.BlockSpec(memory_space=pltpu.VMEM)
    res = pl.pallas_call(
        body, name="grads_swap_halves", in_specs=_any_specs(n) + [vmem], out_specs=_any_specs(n) + [vmem],
        out_shape=[jax.ShapeDtypeStruct(g.shape, g.dtype) for g in gs] + [jax.ShapeDtypeStruct((SMALL_ROWS, D), F32)],
        input_output_aliases={i: i for i in range(n)},
        scratch_shapes=[pltpu.VMEM((8, SMALL_ROWS, D), F32)] + [pltpu.SemaphoreType.DMA((n,))] * 2
        + [pltpu.SemaphoreType.DMA((7,))] * 2,
    )(*gs, vec)
    return list(res[:n]), res[n]


def _col_window(ref, slot, col, ncols):
    return ref.at[slot, :, pl.ds(col, ncols)]


def _stack_window(first, count):
    def view(ref, slot, col, ncols):
        return ref.at[slot, pl.ds(first, count), :, pl.ds(col, ncols)]
    return view


def _row_tile(rows):
    for t in range(512, 15, -16):
        if rows % t == 0:
            return t
    return rows


def _same_shape_runs(arrs):
    runs, a = [], 0
    for b in range(1, len(arrs) + 1):
        if b == len(arrs) or arrs[b].shape != arrs[a].shape:
            runs.append((a, b))
            a = b
    return runs


class _Comm:
    def __init__(self):
        x, y, c = lax.axis_index("x"), lax.axis_index("y"), lax.axis_index("c")
        self.c_idx = jnp.reshape(c, (1,)).astype(jnp.int32)
        self.shard = jnp.reshape(2 * x + y, (1,)).astype(jnp.int32)
        self.place = jnp.stack([2 * x + y, c]).astype(jnp.int32)
        self.groups = {}
        self.sent = {}

    @staticmethod
    def gather(*bufs, part=None, at=None):
        views = [_col_window if b.ndim == 3 else _stack_window(*(part or (0, b.shape[1]))) for b in bufs]
        return _gather_rider(list(bufs), views, at)

    def reduce_rider(self, tag, names, ps, theirs=None, rows=None):
        csums = []
        for a, b in _same_shape_runs(ps):
            name, th = "pair_sum_%s%d" % (tag, a), _row_tile(ps[a].shape[1])
            csums += (_pair_sum(name, self.c_idx, ps[a:b], th) if theirs is None else
                      _pair_add(name, self.c_idx, ps[a:b], theirs[a:b], th))
        self.groups[tag] = [names, csums, None]
        self.sent[tag] = rows
        return _to_chips_rider(csums, 0, rows)

    def rest_rider(self, tag):
        _, csums, ts = self.groups[tag]
        return _to_chips_rider(csums, self.sent[tag], None, ts)

    def landed(self, tag, ts):
        self.groups[tag][2] = ts

    def finish(self, small):
        names, csums, ts = [], [], []
        for group_names, group_csums, group_ts in self.groups.values():
            names += group_names
            csums += group_csums
            ts += group_ts
        order = sorted(range(len(names)), key=lambda i: csums[i].shape[1])
        names, csums, ts = ([v[i] for i in order] for v in (names, csums, ts))
        halves = []
        for a, b in _same_shape_runs(csums):
            halves += _chip_sum("chip_sum_%d" % a, self.place, csums[a:b], ts[a:b], _row_tile(csums[a].shape[1]))
        grads, total = _swap_halves(halves, small)
        return dict(zip(names, grads)), total


ROPE_THETA = 10000.0
SMALL_1K = ("ffn1_pre_norm", "ffn1_post_norm", "mix_pre_norm", "ssm_norm", "mix_post_norm",
            "ffn2_pre_norm", "ffn2_post_norm")
SMALL_16 = ("dt_bias", "a_log", "d_skip")
OFF_CONVB = 7 * D
OFF_16 = OFF_CONVB + CONV_C
OFF_CONVW = OFF_16 + 48
OFF_LOSS = OFF_CONVW + CONV_K * CONV_C
SMALL_LEN = SMALL_ROWS * D


def _sds(shape, dtype):
    return jax.ShapeDtypeStruct(shape, dtype)


def _ridden(res, rider):
    return res if rider is not None else (res, None)


def _ffn_down(name, act, w, tail_of, rider=None):
    tail, o_specs, o_shapes = tail_of(TS)
    return _mm(name, [act, w.dn], NN, (S // TS,),
               [pl.BlockSpec((NSH, TS, FS), lambda i: (0, i, 0)),
                pl.BlockSpec((NSH, None, FS, D), lambda i: (0, w.d0, 0, 0))], o_specs, o_shapes, rider, tail)


def _ffn_dw(name, a, b, rider=None):
    return _mm(name, [a, b], TN, (NSH,),
               [pl.BlockSpec((None, S, FS), lambda s: (s, 0, 0)), pl.BlockSpec((S, D), lambda s: (0, 0))],
               pl.BlockSpec((None, FS, D), lambda s: (s, 0, 0)), _sds((NSH, FS, D), BF16), rider)


def _ffn_dn(name, dgate, dup, w, tail_of, rider=None):
    rows = TS // 2
    tail, o_specs, o_shapes = tail_of(rows)
    a2 = pl.BlockSpec((NSH, rows, FS), lambda i: (0, i, 0))
    return _mm(name, [dgate, w.gu, dup, w.gu], NN, (S // rows,),
               [a2, pl.BlockSpec((NSH, None, FS, D), lambda i: (0, w.g0, 0, 0)),
                a2, pl.BlockSpec((NSH, None, FS, D), lambda i: (0, w.g0 + 1, 0, 0))], o_specs, o_shapes, rider, tail)


def _out_proj_dx(dh, wout, ot):
    def body(dh_ref, w_ref, o_ref, dyn_ref, do_ref, dl_ref):
        dm = _dot(dh_ref[...], w_ref[...], NT)
        dyn_ref[...] = dm[:, D:]
        for b in range(TS // 128):
            for j, blk in enumerate(_rows_to_blocks(dm[128 * b:128 * (b + 1), :D])):
                do = blk.astype(BF16)
                do_ref[j, b] = do
                dl_ref[j, b] = jnp.sum(o_ref[j, b] * do.astype(F32), axis=0, keepdims=True)

    blocks = pl.BlockSpec((NKV, TS // 128, HD, QROWS), lambda i: (0, i, 0, 0))
    return pl.pallas_call(
        body, name="out_proj_dx", grid=(S // TS,),
        in_specs=[pl.BlockSpec((TS, D), lambda i: (i, 0)), pl.BlockSpec((2 * D, D), lambda i: (0, 0)), blocks],
        out_specs=[pl.BlockSpec((TS, D), lambda i: (i, 0)), blocks,
                   pl.BlockSpec((NKV, TS // 128, 1, QROWS), lambda i: (0, i, 0, 0))],
        out_shape=[_sds((S, D), F32), _sds((NKV, NCH, HD, QROWS), BF16), _sds((NKV, NCH, 1, QROWS), F32)],
        compiler_params=_cparams("parallel"),
    )(dh, wout, ot)


def _heads(t, n):
    return t.reshape(S, n, HD).transpose(1, 0, 2)


def _pad128(v):
    return jnp.pad(v, ((0, 0), (0, 128 - v.shape[1])))


def _local_step(x, positions, tgt, sp, gu1, d1, f2, wint, wout, convw, comm=None):
    inv_freq = ROPE_THETA ** (-jnp.arange(0, HD, 2, dtype=F32) / HD)
    ang = positions.astype(F32)[:, None] * inv_freq
    ang = jnp.concatenate([ang, ang, ang, ang], axis=-1)
    cos, sin = jnp.cos(ang), jnp.sin(ang)
    dtb, alog = _pad128(sp["dt_bias"]), _pad128(sp["a_log"])
    dskip_l = jnp.repeat(sp["d_skip"], HD, axis=1)
    convb = sp["conv_b"]

    if comm:
        rider = _join_riders([comm.gather(gu1), _small_gather_rider(convw)])
        (n1, d1, f2, wint, wout), (gu1, convw) = _prenorm_casts(
            "prenorm1", x, sp["ffn1_pre_norm"], comm.shard, [(d1, 0), (f2, 0), (wint, 1), (wout, 0)], rider)
        wint, wout = wint.reshape(NSH, WIN_SH, D), wout.reshape(NSH, 2 * D // NSH, D)
        convw = convw.transpose(1, 0, 2).reshape(CONV_K, CONV_C)
    else:
        n1 = _prenorm("prenorm1", x, sp["ffn1_pre_norm"])
    rider = comm.gather(d1) if comm else None
    (fg1, fu1, act1), got = _ridden(_ffn_up("ffn1_up", n1, _FfnW(gu1, 0, d1, 0), rider), rider)
    if comm:
        d1, = got
    w1 = _FfnW(gu1, 0, d1, 0)
    rider = comm.gather(wint) if comm else None
    (h1, x1, n2), got = _ridden(_ffn_down(
        "ffn1_down", act1, w1,
        lambda rows: _tail_postres(rows, x, sp["ffn1_post_norm"], 0.5, sp["mix_pre_norm"]), rider), rider)
    if comm:
        wint, = got
    wint_pad = jnp.pad(wint.reshape(WIN_COLS, D), ((0, WIN_PAD - WIN_COLS), (0, 0)))

    pw = WIN_PAD // 3
    rider = comm.gather(f2, part=(0, 1)) if comm else None
    proj, got = _ridden(_mm(
        "in_proj", [n2, wint_pad], NT, (S // TS, 3),
        [pl.BlockSpec((TS, D), lambda i, j: (i, 0)), pl.BlockSpec((pw, D), lambda i, j: (j, 0))],
        pl.BlockSpec((TS, pw), lambda i, j: (i, j)), _sds((S, WIN_PAD), F32), rider), rider)
    if comm:
        f2, = got
    qt = _rope_q(proj, cos, sin)
    k_rot, v_bf, kt, vt = _rope_kv(proj, cos, sin)
    kh, vh = _heads(k_rot, NKV), _heads(v_bf, NKV)
    bias = _bias_table()
    rider = comm.gather(f2, wout, part=(1, 2), at=9) if comm else None
    (ot, lse, mixed), got = _ridden(_attn_fwd(qt, kh, vt, bias, rider), rider)
    if comm:
        f2, wout = got
    w2 = _FfnW(f2, 0, f2, 2)
    wout = wout.reshape(2 * D, D)
    xbc, conv_y = _conv_fwd(proj, convw, convb)
    y, mixed, hprev = _ssd_fwd(xbc, proj, dtb, alog, dskip_l, sp["ssm_norm"], mixed)
    tail, o_specs, o_shapes = _tail_postres(TS, x1, sp["mix_post_norm"], 1.0, sp["ffn2_pre_norm"])
    h2, x2, n3 = _mm("out_proj", [mixed, wout], NN, (S // TS,),
                     [pl.BlockSpec((TS, 2 * D), lambda i: (i, 0)), pl.BlockSpec((2 * D, D), lambda i: (0, 0))],
                     o_specs, o_shapes, None, tail)

    fg2, fu2, act2 = _ffn_up("ffn2_up", n3, w2)
    dy, dh3, dp3, loss = _ffn_down(
        "ffn2_down", act2, w2, lambda rows: _tail_final(rows, x2, sp["ffn2_post_norm"], tgt, 0.5))

    dgate2, dup2 = _ffn_dact("ffn2_dact", dh3, w2, fg2, fu2)
    dws2 = [_ffn_dw("ffn2_dwg", dgate2, n3), _ffn_dw("ffn2_dwu", dup2, n3), _ffn_dw("ffn2_dwd", act2, dh3)]
    dx2, dh2, dg3, dp2 = _ffn_dn(
        "ffn2_dn", dgate2, dup2, w2,
        lambda rows: _tail_mid_bwd(rows, dy, x2, sp["ffn2_pre_norm"], h2, sp["mix_post_norm"], 1.0))

    dyn, dot_, delta = _out_proj_dx(dh2, wout, ot)
    dwout = _mm("out_proj_dw", [mixed, dh2], TN, (2,),
                [pl.BlockSpec((S, D), lambda m: (0, m)), pl.BlockSpec((S, D), lambda m: (0, 0))],
                pl.BlockSpec((D, D), lambda m: (m, 0)), _sds((2 * D, D), BF16))
    dwout = dwout.reshape(NSH, 2 * D // NSH, D)

    def riding(tag, names, ps, call, theirs=None, rows=None):
        rider = None
        if comm:
            rider = comm.rest_rider(tag) if names is None else comm.reduce_rider(tag, names, ps, theirs, rows)
        res, got = _ridden(call(rider), rider)
        if comm:
            comm.landed(tag, got)
        return res

    rider = _to_sibling_rider(dws2 + [dwout]) if comm else None
    (dxbc, dproj, ddt, dssm, dsc), theirs = _ridden(
        _ssd_bwd(dyn, y, xbc, proj, hprev, dtb, alog, dskip_l, sp["ssm_norm"], rider), rider)
    dproj, dcw8, dcb = _conv_bwd(dxbc, conv_y, proj, convw, dproj)
    dqt, dkh, dvh = riding("a", BIG[3:6] + ("w_out",), dws2 + [dwout], lambda rider: _attn_bwd(
        qt, kh, kt, vh, dot_, lse, delta, bias, rider), theirs)
    dproj = _rope_dq(dqt, cos, sin, dproj)
    dproj = _rope_dkv(dkh, dvh, cos, sin, dproj)
    dproj = lax.dynamic_update_slice(dproj, ddt, (0, COL_DT))
    dwint = _mm("in_proj_dw", [dproj, n2], TN, (3,),
                [pl.BlockSpec((S, pw), lambda j: (0, j)), pl.BlockSpec((S, D), lambda j: (0, 0))],
                pl.BlockSpec((pw, D), lambda j: (j, 0)), _sds((WIN_PAD, D), BF16))
    dwint = dwint[:WIN_COLS].reshape(NSH, WIN_SH, D)

    tail, o_specs, o_shapes = _tail_mid_bwd(TS, dx2, x1, sp["mix_pre_norm"], h1, sp["ffn1_post_norm"], 0.5)
    dx1, dh1, dg2, dp1 = riding("b", ("w_in",), [dwint], lambda rider: _mm(
        "in_proj_dx", [dproj, wint_pad], NN, (S // TS,),
        [pl.BlockSpec((TS, WIN_PAD), lambda i: (i, 0)), pl.BlockSpec((WIN_PAD, D), lambda i: (0, 0))],
        o_specs, o_shapes, rider, tail), rows=W_IN_FIRST)

    dwd1 = riding("b", None, None, lambda rider: _ffn_dw("ffn1_dwd", act1, dh1, rider))
    dgate1, dup1 = riding("d", BIG[2:3], [dwd1], lambda rider: _ffn_dact("ffn1_dact", dh1, w1, fg1, fu1, rider))
    dwg1, dwu1 = _ffn_dw("ffn1_dwg", dgate1, n1), _ffn_dw("ffn1_dwu", dup1, n1)
    grad_x, dg1 = riding("g", BIG[0:2], [dwg1, dwu1], lambda rider: _ffn_dn(
        "ffn1_dn", dgate1, dup1, w1, lambda rows: _tail_first_bwd(rows, dx1, x, sp["ffn1_pre_norm"]), rider))
    dws1 = [dwg1, dwu1, dwd1]

    small = jnp.concatenate([
        dg1[0], dp1[0], dg2[0], dssm[0], dp2[0], dg3[0], dp3[0], dcb[0],
        dsc[0, :16], dsc[1, :16], dsc[2, :16], dcw8[:CONV_K].reshape(-1), loss[0, :1]])
    small = jnp.pad(small, (0, SMALL_LEN - small.shape[0])).reshape(SMALL_ROWS, D)
    if comm is None:
        return grad_x, dws1 + dws2 + [dwint, dwout], small
    return (grad_x,) + comm.finish(small)


WEIGHTS = ("ffn1_pre_norm", "ffn1_w_gate", "ffn1_w_up", "ffn1_w_down", "ffn1_post_norm", "mix_pre_norm", "w_in",
           "conv_w", "conv_b", "dt_bias", "a_log", "d_skip", "ssm_norm", "w_out", "mix_post_norm", "ffn2_pre_norm",
           "ffn2_w_gate", "ffn2_w_up", "ffn2_w_down", "ffn2_post_norm")
BIG = ("ffn1_w_gate", "ffn1_w_up", "ffn1_w_down", "ffn2_w_gate", "ffn2_w_up", "ffn2_w_down", "w_in", "w_out")
TRANSPOSED = ("ffn1_w_gate", "ffn1_w_up", "ffn2_w_gate", "ffn2_w_up", "w_in")
SMALL_ORDER = SMALL_1K + ("conv_b",) + SMALL_16
CONVW_SH = CONV_C // NSH


def _shard2d(t, name):
    return t[0].T if name in TRANSPOSED else t[0]


def _unshard2d(t, name):
    return (t.T if name in TRANSPOSED else t)[None]


def _rows3d(t):
    return t.transpose(2, 0, 1)


def _pack_small(d, prefix, shard_of_convw):
    flat = jnp.concatenate([d[prefix + n][0] for n in SMALL_ORDER] + [shard_of_convw.reshape(-1)])
    return jnp.pad(flat, (0, SMALL_LEN - flat.shape[0])).reshape(SMALL_ROWS, D)


def _unpack_small(block, like):
    flat = block.reshape(-1)
    out, off = {}, 0
    for n in SMALL_ORDER:
        size = like[n].shape[1]
        out[n] = flat[off:off + size].reshape(1, size)
        off += size
    out["conv_w"] = flat[off:off + CONV_K * CONVW_SH].reshape(1, CONV_K, CONVW_SH)
    return out


def kernel(x, positions, ffn1_pre_norm, ffn1_w_gate, ffn1_w_up, ffn1_w_down, ffn1_post_norm, mix_pre_norm, w_in, conv_w, conv_b, dt_bias, a_log, d_skip, ssm_norm, w_out, mix_post_norm, ffn2_pre_norm, ffn2_w_gate, ffn2_w_up, ffn2_w_down, ffn2_post_norm, loss_target, m_ffn1_pre_norm, m_ffn1_w_gate, m_ffn1_w_up, m_ffn1_w_down, m_ffn1_post_norm, m_mix_pre_norm, m_w_in, m_conv_w, m_conv_b, m_dt_bias, m_a_log, m_d_skip, m_ssm_norm, m_w_out, m_mix_post_norm, m_ffn2_pre_norm, m_ffn2_w_gate, m_ffn2_w_up, m_ffn2_w_down, m_ffn2_post_norm, v_ffn1_pre_norm, v_ffn1_w_gate, v_ffn1_w_up, v_ffn1_w_down, v_ffn1_post_norm, v_mix_pre_norm, v_w_in, v_conv_w, v_conv_b, v_dt_bias, v_a_log, v_d_skip, v_ssm_norm, v_w_out, v_mix_post_norm, v_ffn2_pre_norm, v_ffn2_w_gate, v_ffn2_w_up, v_ffn2_w_down, v_ffn2_post_norm):
    given = dict(locals())
    xi, yi = lax.axis_index("x"), lax.axis_index("y")

    comm = _Comm()
    big = {p + n: _shard2d(given[p + n], n) for n in BIG for p in ("", "m_", "v_")}
    gu1 = _cast_stack("cast_ffn1_gate_up", comm.shard, [big[n] for n in BIG[0:2]], 176, D)

    sp = {n: given[n] for n in SMALL_ORDER}
    grad_x, big_grads, small = _local_step(
        x[0], positions[0], loss_target[0], sp, gu1, [big[BIG[2]]], [big[n] for n in BIG[3:6]], [big["w_in"]],
        [big["w_out"]], conv_w[0], comm)

    tot = small.reshape(-1)
    loss = tot[OFF_LOSS]
    small_grads, off = {}, 0
    for n in SMALL_ORDER:
        size = given[n].shape[1]
        small_grads[n] = tot[off:off + size].reshape(1, size)
        off += size
    dconvw = tot[OFF_CONVW:OFF_CONVW + CONV_K * CONV_C].reshape(CONV_K, NSH, CONVW_SH)
    dconvw = lax.dynamic_index_in_dim(dconvw, 2 * xi + yi, axis=1, keepdims=False)
    small_grads["conv_w"] = dconvw.reshape(1, CONV_K, CONVW_SH)

    upd = {}
    for names, tr in ((BIG[0:3], 176), (BIG[3:6], 176), (BIG[7:8], 256)):
        res = _adamw("adamw_" + names[0], [big[n] for n in names], [big_grads[n] for n in names],
                     [big["m_" + n] for n in names], [big["v_" + n] for n in names], tr, D)
        for n, r in zip(names, res):
            upd[n] = tuple(_unshard2d(t, n) for t in r)
    g_win = big_grads["w_in"].reshape(WIN_SH, 1, D)
    res, = _adamw("adamw_w_in", [_rows3d(w_in)], [g_win], [_rows3d(m_w_in)], [_rows3d(v_w_in)], WIN_SH // 4, D)
    upd["w_in"] = tuple(t.transpose(1, 2, 0) for t in res)
    (dl, m2, v2, _), = _adamw(
        "adamw_small", [_pack_small(given, "", conv_w[0])], [_pack_small(small_grads, "", dconvw)],
        [_pack_small(given, "m_", m_conv_w[0])], [_pack_small(given, "v_", v_conv_w[0])], SMALL_ROWS, D)
    dl, m2, v2 = (_unpack_small(t, given) for t in (dl, m2, v2))
    for n in SMALL_ORDER + ("conv_w",):
        upd[n] = (dl[n], m2[n], v2[n], small_grads[n])

    return (loss, grad_x[None], *[upd[n][3] for n in WEIGHTS], *[upd[n][0] for n in WEIGHTS],
            *[upd[n][1] for n in WEIGHTS], *[upd[n][2] for n in WEIGHTS])
```

```python
import functools
import typing

import jax
import jax.numpy as jnp
from jax import lax
from jax.experimental import pallas as pl
from jax.experimental.pallas import tpu as pltpu

F32 = jnp.float32
BF16 = jnp.bfloat16

S = 2048
D = 1024
FF = 2816
NSH = 4
FS = FF // NSH
HALF = D // 2
HD = 64
NKV = 4
NQ_PER_KV = 4
KVW = NKV * HD
QCOLS = NQ_PER_KV * HD
CONV_C = 1536
CONV_K = 4
SSM_W = 1024
NST = 128
NCH = S // 128
WIN_COLS = 4112
WIN_SH = WIN_COLS // NSH
W_IN_FIRST = 768
WIN_PAD = 4224
COL_DT = 4096
EPS = 1e-6
NEG = -1e30

ADAM_LR = 0.001
ADAM_B1 = 0.9
ADAM_B2 = 0.999
ADAM_EPS = 1e-08
ADAM_WD = 0.01
ADAM_STEP = 10

VMEM_LIMIT = 56 * 1024 * 1024
TS = 512
TR = 256

NN = (((1,), (0,)), ((), ()))
NT = (((1,), (1,)), ((), ()))
TN = (((0,), (0,)), ((), ()))
MESH = pl.DeviceIdType.MESH


def _cparams(*sem):
    return pltpu.CompilerParams(dimension_semantics=sem, vmem_limit_bytes=VMEM_LIMIT)


def _dot(a, b, dims):
    return lax.dot_general(a.astype(BF16), b.astype(BF16), dims, preferred_element_type=F32)


def _bf16_pieces(v):
    hi = v.astype(BF16)
    rest = v - hi.astype(F32)
    mid = rest.astype(BF16)
    return hi, mid, (rest - mid.astype(F32)).astype(BF16)


def _dot_exact(a, b, ones="a"):
    if ones == "a":
        sel = a.astype(BF16)
        parts = [lax.dot_general(sel, p, NN, preferred_element_type=F32) for p in _bf16_pieces(b)]
    else:
        sel = b.astype(BF16)
        parts = [lax.dot_general(p, sel, NN, preferred_element_type=F32) for p in _bf16_pieces(a)]
    return (parts[2] + parts[1]) + parts[0]


def _sigmoid(v):
    return 1.0 / (1.0 + jnp.exp(-v))


class _Rider(typing.NamedTuple):
    operands: list
    out_shapes: list
    aliases: dict
    sems: list
    start: typing.Callable
    finish: typing.Callable
    between: typing.Callable = None
    at: int = None


def _call(body, name, grid, in_specs, out_specs, out_shape, operands, scratch=(), sem=(), rider=None, prefetch=0):
    multi = isinstance(out_shape, (list, tuple))

    def launch(kernel, in_specs, out_specs, out_shape, scratch, aliases, sem, args):
        if prefetch:
            how = dict(grid_spec=pltpu.PrefetchScalarGridSpec(
                num_scalar_prefetch=prefetch, grid=grid, in_specs=in_specs, out_specs=out_specs,
                scratch_shapes=scratch))
        else:
            how = dict(grid=grid, in_specs=in_specs, out_specs=out_specs, scratch_shapes=scratch)
        return pl.pallas_call(kernel, name=name, out_shape=out_shape, input_output_aliases=aliases,
                              compiler_params=_cparams(*sem), **how)(*args)

    if rider is None:
        return launch(body, in_specs, out_specs, out_shape, list(scratch), {}, sem, operands)
    outs = list(out_shape) if multi else [out_shape]
    ospecs = list(out_specs) if multi else [out_specs]
    n_in, n_out, n_scr = len(operands) - prefetch, len(outs), len(scratch)
    ri, ro = len(rider.operands), len(rider.out_shapes)

    def wrapped(*refs):
        scalars, refs = refs[:prefetch], refs[prefetch:]
        o0 = n_in + ri
        s0 = o0 + n_out + ro
        rin, rout, rsem = refs[n_in:o0], refs[o0 + n_out:s0], refs[s0 + n_scr:]
        ids = [pl.program_id(a) for a in range(len(grid))]
        first = functools.reduce(jnp.logical_and, [i == 0 for i in ids])
        last = functools.reduce(jnp.logical_and, [i == g - 1 for i, g in zip(ids, grid)])

        @pl.when(first)
        def _():
            rider.start(rin, rout, rsem)

        if rider.between is not None:
            steps = functools.reduce(lambda a, b: a * b, grid)
            step = functools.reduce(lambda a, ig: a * ig[1] + ig[0], zip(ids, grid), 0)

            @pl.when(step == (steps // 3 if rider.at is None else rider.at))
            def _():
                rider.between(rin, rout, rsem)

        body(*scalars, *refs[:n_in], *refs[o0:o0 + n_out], *refs[s0:s0 + n_scr])

        @pl.when(last)
        def _():
            rider.finish(rin, rout, rsem)

    hbm = pl.BlockSpec(memory_space=pl.ANY)
    res = launch(wrapped, list(in_specs) + [hbm] * ri, ospecs + [hbm] * ro, outs + list(rider.out_shapes),
                 list(scratch) + list(rider.sems),
                 {prefetch + n_in + k: n_out + v for k, v in rider.aliases.items()},
                 ("arbitrary",) * len(grid), (*operands, *rider.operands))
    main = list(res[:n_out])
    return (main if multi else main[0]), list(res[n_out:])


class _Tail(typing.NamedTuple):
    fn: typing.Callable
    operands: list
    in_specs: list


def _mm(name, operands, dims, grid, in_specs, o_spec, out_shape, rider=None, tail=None):
    npairs = len(operands) // 2
    extra = [] if tail is None else list(tail.operands)
    nin = 2 * npairs + len(extra)

    def body(*refs):
        t = None
        for i in range(npairs):
            a, b = refs[2 * i], refs[2 * i + 1]
            parts = [(a[s], b[s]) for s in range(a.shape[0])] if len(a.shape) == 3 else [(a[...], b[...])]
            for pa, pb in parts:
                d = _dot(pa, pb, dims)
                t = d if t is None else t + d
        if tail is None:
            refs[nin][...] = t.astype(refs[nin].dtype)
        else:
            tail.fn(t, refs[2 * npairs:nin], refs[nin:])

    sem = ("parallel" if tail is None else "arbitrary",) * len(grid)
    specs = list(in_specs) + ([] if tail is None else list(tail.in_specs))
    return _call(body, name, grid, specs, o_spec, out_shape, list(operands) + extra, (), sem, rider)


class _FfnW(typing.NamedTuple):
    gu: jax.Array
    g0: int
    dn: jax.Array
    d0: int


def _ffn_up(name, n, w, rider=None):
    def body(n_ref, wg_ref, wu_ref, fg_ref, fu_ref, a_ref):
        nb = n_ref[...]
        g = _dot(nb, wg_ref[...], NT)
        u = _dot(nb, wu_ref[...], NT)
        sg = _sigmoid(g)
        silu = g * sg
        fg_ref[...] = (u * (sg * (1.0 + g * (1.0 - sg)))).astype(BF16)
        fu_ref[...] = silu.astype(BF16)
        a_ref[...] = (silu * u).astype(BF16)

    out = jax.ShapeDtypeStruct((NSH, S, FS), BF16)
    ospec = pl.BlockSpec((None, TS, FS), lambda s, i: (s, i, 0))
    return _call(
        body, name, (NSH, S // TS),
        [pl.BlockSpec((TS, D), lambda s, i: (i, 0)),
         pl.BlockSpec((None, None, FS, D), lambda s, i: (s, w.g0, 0, 0)),
         pl.BlockSpec((None, None, FS, D), lambda s, i: (s, w.g0 + 1, 0, 0))],
        [ospec, ospec, ospec], [out, out, out], (n, w.gu, w.gu), sem=("parallel", "parallel"), rider=rider)


def _ffn_dact(name, dh, w, fgate, fup, rider=None):
    def body(dh_ref, wd_ref, fg_ref, fu_ref, dg_ref, du_ref):
        da = _dot(dh_ref[...], wd_ref[...], NT)
        dg_ref[...] = (da * fg_ref[...].astype(F32)).astype(BF16)
        du_ref[...] = (da * fu_ref[...].astype(F32)).astype(BF16)

    out = jax.ShapeDtypeStruct((NSH, S, FS), BF16)
    aspec = pl.BlockSpec((None, TS, FS), lambda s, i: (s, i, 0))
    return _call(
        body, name, (NSH, S // TS),
        [pl.BlockSpec((TS, D), lambda s, i: (i, 0)),
         pl.BlockSpec((None, None, FS, D), lambda s, i: (s, w.d0, 0, 0)), aspec, aspec],
        [aspec, aspec], [out, out], (dh, w.dn, fgate, fup), sem=("parallel", "parallel"), rider=rider)


def _rstd(v):
    return lax.rsqrt(jnp.mean(v * v, axis=-1, keepdims=True) + EPS)


def _row_spec():
    return pl.BlockSpec((TR, D), lambda i: (i, 0))


def _vec_spec():
    return pl.BlockSpec((1, D), lambda i: (0, 0))


def _acc_rows(ref, v):
    @pl.when(pl.program_id(0) == 0)
    def _():
        ref[...] = jnp.zeros_like(ref)
    ref[...] += jnp.sum(v, axis=0, keepdims=True)


def _prenorm(name, x, g):
    def body(x_ref, g_ref, n_ref):
        xv = x_ref[...]
        n_ref[...] = (xv * _rstd(xv) * g_ref[...]).astype(BF16)

    return pl.pallas_call(
        body, name=name, grid=(S // TR,), in_specs=[_row_spec(), _vec_spec()], out_specs=_row_spec(),
        out_shape=jax.ShapeDtypeStruct((S, D), BF16), compiler_params=_cparams("parallel"),
    )(x, g)


ENTRY_STEPS = 4


def _prenorm_casts(name, x, g, slot, groups, rider):
    def body(s_ref, x_ref, g_ref, *refs):
        ins, outs = refs[:len(refs) - len(groups) - 1], refs[len(refs) - len(groups) - 1:]
        xv = x_ref[...]
        outs[0][...] = (xv * _rstd(xv) * g_ref[...]).astype(BF16)
        at = 0
        for (arrs, _), out in zip(groups, outs[1:]):
            for k in range(len(arrs)):
                out[k] = ins[at + k][...].astype(BF16)
            at += len(arrs)

    rows = pl.BlockSpec((S // ENTRY_STEPS, D), lambda i, sr: (i, 0))
    in_specs, out_specs, out_shapes = [rows, pl.BlockSpec((1, D), lambda i, sr: (0, 0))], [rows], [_rows_bf16()]
    for arrs, axis in groups:
        r, c = arrs[0].shape
        if axis == 0:
            blk, at, at_out = (r // ENTRY_STEPS, c), (lambda i, sr: (i, 0)), (lambda i, sr: (sr[0], 0, i, 0))
        else:
            blk, at, at_out = (r, c // ENTRY_STEPS), (lambda i, sr: (0, i)), (lambda i, sr: (sr[0], 0, 0, i))
        in_specs += [pl.BlockSpec(blk, at)] * len(arrs)
        out_specs.append(pl.BlockSpec((None, len(arrs)) + blk, at_out))
        out_shapes.append(jax.ShapeDtypeStruct((NSH, len(arrs), r, c), BF16))
    return _call(body, name, (ENTRY_STEPS,), in_specs, out_specs, out_shapes,
                 [slot, x, g] + [a for arrs, _ in groups for a in arrs], rider=rider, prefetch=1)


def _rows_spec(rows):
    return pl.BlockSpec((rows, D), lambda i: (i, 0))


def _rows_f32():
    return jax.ShapeDtypeStruct((S, D), F32)


def _rows_bf16():
    return jax.ShapeDtypeStruct((S, D), BF16)


def _vec_f32():
    return jax.ShapeDtypeStruct((1, D), F32)


def _tail_postres(rows, x, p, alpha, gnext):
    def fn(h, ins, outs):
        x_ref, p_ref, g_ref = ins
        h_ref, xo_ref, n_ref = outs
        h_ref[...] = h
        xo = x_ref[...] + alpha * (h * _rstd(h) * p_ref[...])
        xo_ref[...] = xo
        n_ref[...] = (xo * _rstd(xo) * g_ref[...]).astype(BF16)

    rs = _rows_spec(rows)
    return (_Tail(fn, [x, p, gnext], [rs, _vec_spec(), _vec_spec()]), [rs, rs, rs],
            [_rows_f32(), _rows_f32(), _rows_bf16()])


def _tail_final(rows, x, p, tgt, alpha):
    def fn(h, ins, outs):
        x_ref, p_ref, t_ref = ins
        dy_ref, dh_ref, dp_ref, loss_ref = outs
        r = _rstd(h)
        hn = h * r
        pv = p_ref[...]
        e = x_ref[...] + alpha * (hn * pv) - t_ref[...]
        dy = e * (1.0 / D)
        dy_ref[...] = dy
        du = alpha * dy * pv
        dh_ref[...] = (r * (du - hn * jnp.mean(du * hn, axis=-1, keepdims=True))).astype(BF16)
        _acc_rows(dp_ref, alpha * dy * hn)
        part = 0.5 * jnp.sum(jnp.mean(e * e, axis=-1, keepdims=True), axis=0, keepdims=True)
        _acc_rows(loss_ref, jnp.broadcast_to(part, (1, 128)))

    rs = _rows_spec(rows)
    return (_Tail(fn, [x, p, tgt], [rs, _vec_spec(), rs]),
            [rs, rs, _vec_spec(), pl.BlockSpec((1, 128), lambda i: (0, 0))],
            [_rows_f32(), _rows_bf16(), _vec_f32(), jax.ShapeDtypeStruct((1, 128), F32)])


def _norm_bwd(dn, xv, g_ref, dg_ref):
    r = _rstd(xv)
    xn = xv * r
    dng = dn * g_ref[...]
    _acc_rows(dg_ref, dn * xn)
    return r * (dng - xn * jnp.mean(dng * xn, axis=-1, keepdims=True))


def _tail_mid_bwd(rows, dres, x, g, h, p, alpha):
    def fn(dn, ins, outs):
        dr_ref, x_ref, g_ref, h_ref, p_ref = ins
        dx_ref, dh_ref, dg_ref, dp_ref = outs
        dx = dr_ref[...] + _norm_bwd(dn, x_ref[...], g_ref, dg_ref)
        dx_ref[...] = dx
        hv = h_ref[...]
        r = _rstd(hv)
        hn = hv * r
        du = alpha * dx * p_ref[...]
        dh_ref[...] = (r * (du - hn * jnp.mean(du * hn, axis=-1, keepdims=True))).astype(BF16)
        _acc_rows(dp_ref, alpha * dx * hn)

    rs = _rows_spec(rows)
    return (_Tail(fn, [dres, x, g, h, p], [rs, rs, _vec_spec(), rs, _vec_spec()]),
            [rs, rs, _vec_spec(), _vec_spec()], [_rows_f32(), _rows_bf16(), _vec_f32(), _vec_f32()])


def _tail_first_bwd(rows, dres, x, g):
    def fn(dn, ins, outs):
        dr_ref, x_ref, g_ref = ins
        dx_ref, dg_ref = outs
        dx_ref[...] = dr_ref[...] + _norm_bwd(dn, x_ref[...], g_ref, dg_ref)

    rs = _rows_spec(rows)
    return (_Tail(fn, [dres, x, g], [rs, rs, _vec_spec()]), [rs, _vec_spec()], [_rows_f32(), _vec_f32()])


def _rotate(t, c128, s128, sign, scale):
    width = t.shape[1]
    c = jnp.tile(c128, (1, width // 128))
    sn = jnp.tile(s128, (1, width // 128))
    lane = lax.broadcasted_iota(jnp.int32, t.shape, 1) & (HD - 1)
    rot = jnp.where(lane < HD // 2, -pltpu.roll(t, width - HD // 2, 1), pltpu.roll(t, HD // 2, 1))
    return (t * c + sign * (rot * sn)) * scale


def _rows_to_blocks(y):
    out = []
    for j in range(NKV):
        yt = y[:, QCOLS * j:QCOLS * (j + 1)].T
        out.append(jnp.concatenate([yt[HD * g:HD * (g + 1)] for g in range(NQ_PER_KV)], axis=1))
    return out


def _blocks_to_rows(blocks):
    cols = []
    for b in blocks:
        stacked = jnp.concatenate([b[:, 128 * g:128 * (g + 1)] for g in range(NQ_PER_KV)], axis=0)
        cols.append(stacked.T)
    return jnp.concatenate(cols, axis=1)


def _rope_q(proj, cos, sin):
    def body(t_ref, c_ref, s_ref, o_ref):
        y = _rotate(t_ref[...], c_ref[...], s_ref[...], 1.0, HD ** -0.5)
        for j, blk in enumerate(_rows_to_blocks(y)):
            o_ref[j] = blk.astype(BF16)

    return pl.pallas_call(
        body, name="rope_q", grid=(NCH,),
        in_specs=[pl.BlockSpec((128, D), lambda i: (i, 0)),
                  pl.BlockSpec((128, 128), lambda i: (i, 0)), pl.BlockSpec((128, 128), lambda i: (i, 0))],
        out_specs=pl.BlockSpec((NKV, None, HD, QROWS), lambda i: (0, i, 0, 0)),
        out_shape=jax.ShapeDtypeStruct((NKV, NCH, HD, QROWS), BF16), compiler_params=_cparams("parallel"),
    )(proj, cos, sin)


def _rope_dq(dqt, cos, sin, dproj):
    def body(t_ref, c_ref, s_ref, buf_ref, o_ref):
        t = _blocks_to_rows([t_ref[j] for j in range(NKV)])
        o_ref[...] = _rotate(t, c_ref[...], s_ref[...], -1.0, HD ** -0.5).astype(BF16)

    return pl.pallas_call(
        body, name="rope_dq", grid=(NCH,),
        in_specs=[pl.BlockSpec((NKV, None, HD, QROWS), lambda i: (0, i, 0, 0)),
                  pl.BlockSpec((128, 128), lambda i: (i, 0)), pl.BlockSpec((128, 128), lambda i: (i, 0)),
                  pl.BlockSpec(memory_space=pl.ANY)],
        out_specs=pl.BlockSpec((128, D), lambda i: (i, 0)),
        out_shape=jax.ShapeDtypeStruct(dproj.shape, BF16), input_output_aliases={3: 0},
        compiler_params=_cparams("parallel"),
    )(dqt, cos, sin, dproj)


def _rope_dkv(dkt, dvt, cos, sin, dproj):
    def body(k_ref, v_ref, c_ref, s_ref, buf_ref, o_ref):
        dk = jnp.concatenate([k_ref[j] for j in range(NKV)], axis=0).T
        dv = jnp.concatenate([v_ref[j] for j in range(NKV)], axis=0).T
        dk = _rotate(dk, c_ref[...], s_ref[...], -1.0, 1.0)
        o_ref[...] = jnp.concatenate([dk, dv], axis=1).astype(BF16)

    tspec = pl.BlockSpec((NKV, HD, 128), lambda i: (0, 0, i))
    return pl.pallas_call(
        body, name="rope_dkv", grid=(NCH,),
        in_specs=[tspec, tspec, pl.BlockSpec((128, 128), lambda i: (i, 0)), pl.BlockSpec((128, 128), lambda i: (i, 0)),
                  pl.BlockSpec(memory_space=pl.ANY)],
        out_specs=pl.BlockSpec((128, 2 * KVW), lambda i: (i, D // (2 * KVW))),
        out_shape=jax.ShapeDtypeStruct(dproj.shape, BF16), input_output_aliases={4: 0},
        compiler_params=_cparams("parallel"),
    )(dkt, dvt, cos, sin, dproj)


def _rope_kv(proj, cos, sin):
    def body(t_ref, c_ref, s_ref, k_ref, v_ref, kt_ref, vt_ref):
        t = t_ref[...]
        k = _rotate(t[:, :KVW], c_ref[...], s_ref[...], 1.0, 1.0).astype(BF16)
        v = t[:, KVW:].astype(BF16)
        k_ref[...] = k
        v_ref[...] = v
        kt, vt = k.astype(F32).T, v.astype(F32).T
        for j in range(NKV):
            kt_ref[j] = kt[HD * j:HD * (j + 1)].astype(BF16)
            vt_ref[j] = vt[HD * j:HD * (j + 1)].astype(BF16)

    rows = pl.BlockSpec((128, KVW), lambda i: (i, 0))
    tspec = pl.BlockSpec((NKV, HD, 128), lambda i: (0, 0, i))
    return pl.pallas_call(
        body, name="rope_kv", grid=(NCH,),
        in_specs=[pl.BlockSpec((128, 2 * KVW), lambda i: (i, D // (2 * KVW))),
                  pl.BlockSpec((128, 128), lambda i: (i, 0)), pl.BlockSpec((128, 128), lambda i: (i, 0))],
        out_specs=[rows, rows, tspec, tspec],
        out_shape=[jax.ShapeDtypeStruct((S, KVW), BF16)] * 2 + [jax.ShapeDtypeStruct((NKV, HD, S), BF16)] * 2,
        compiler_params=_cparams("parallel"),
    )(proj, cos, sin)


QROWS = NQ_PER_KV * 128


NBIAS = NCH + 1
KV_PER_STEP = 4


def _bias_table():
    db = lax.broadcasted_iota(jnp.int32, (NBIAS, 128, QROWS), 0) - 1
    ki = lax.broadcasted_iota(jnp.int32, (NBIAS, 128, QROWS), 1)
    qi = lax.broadcasted_iota(jnp.int32, (NBIAS, 128, QROWS), 2) & 127
    d = db * 128 + qi - ki
    cnt = ((d <= 128).astype(F32) + (((d & 3) == 0) & (d <= 512)).astype(F32) + ((d & 15) == 0).astype(F32))
    return jnp.where((d >= 0) & (cnt > 0.0), jnp.log(jnp.maximum(cnt, 1.0)), NEG)


def _attn_fwd(qt, kh, vt, bias, rider=None):
    def body(q_ref, k_ref, v_ref, b_ref, o_ref, lse_ref, rows_ref, m_ref, l_ref, acc_ref):
        qb = pl.program_id(1)
        m_ref[...] = jnp.full_like(m_ref, NEG)
        l_ref[...] = jnp.zeros_like(l_ref)
        acc_ref[...] = jnp.zeros_like(acc_ref)

        def keys(off, size, bias_):
            for h in range(KV_PER_STEP):
                m = m_ref[h]
                s = _dot(k_ref[h, pl.ds(off, size), :], q_ref[h], NN) + bias_
                m_new = jnp.maximum(m, jnp.max(s, axis=0, keepdims=True))
                p = jnp.exp(s - m_new)
                a = jnp.exp(m - m_new)
                m_ref[h] = m_new
                l_ref[h] = a * l_ref[h] + jnp.sum(p, axis=0, keepdims=True)
                acc_ref[h] = a * acc_ref[h] + _dot(v_ref[h, :, pl.ds(off, size)], p, NN)

        def blocks(first, count):
            bias_ = jnp.concatenate([b_ref[qb - first - j + 1] for j in range(count)], axis=0)
            keys(pl.multiple_of(first * 128, 128), 128 * count, bias_)

        nkb = qb + 1
        @pl.loop(0, nkb // 4)
        def _(i):
            blocks(4 * i, 4)

        @pl.when(nkb % 4 >= 2)
        def _():
            blocks(nkb // 4 * 4, 2)

        @pl.when(nkb % 2 == 1)
        def _():
            blocks(qb, 1)

        outs = []
        for h in range(KV_PER_STEP):
            outs.append(acc_ref[h] / l_ref[h])
            o_ref[h] = outs[h]
            lse_ref[h] = m_ref[h] + jnp.log(l_ref[h])
        rows_ref[...] = _blocks_to_rows(outs).astype(BF16)

    kvs = KV_PER_STEP
    qspec = pl.BlockSpec((kvs, None, HD, QROWS), lambda j, i: (j, i, 0, 0))
    return _call(
        body, "attn_fwd", (NKV // kvs, NCH),
        [qspec, pl.BlockSpec((kvs, S, HD), lambda j, i: (j, 0, 0)),
         pl.BlockSpec((kvs, HD, S), lambda j, i: (j, 0, 0)),
         pl.BlockSpec((NBIAS, 128, QROWS), lambda j, i: (0, 0, 0))],
        [qspec, pl.BlockSpec((kvs, None, 1, QROWS), lambda j, i: (j, i, 0, 0)),
         pl.BlockSpec((128, QCOLS * kvs), lambda j, i: (i, j))],
        [jax.ShapeDtypeStruct((NKV, NCH, HD, QROWS), F32), jax.ShapeDtypeStruct((NKV, NCH, 1, QROWS), F32),
         jax.ShapeDtypeStruct((S, 2 * D), BF16)],
        (qt, kh, vt, bias),
        [pltpu.VMEM((kvs, 1, QROWS), F32), pltpu.VMEM((kvs, 1, QROWS), F32), pltpu.VMEM((kvs, HD, QROWS), F32)],
        ("parallel", "parallel"), rider)


def _attn_bwd(qt, kh, kt, vh, dot_, lse, delta, bias, rider=None):
    def body(qt_ref, k_ref, kt_ref, v_ref, dot_ref, lse_ref, dl_ref, b_ref, dq_ref, dk_ref, dv_ref):
        kp = pl.program_id(1)

        @pl.when(kp == 0)
        def _():
            dq_ref[...] = jnp.zeros_like(dq_ref)

        dk_ref[...] = jnp.zeros_like(dk_ref)
        dv_ref[...] = jnp.zeros_like(dv_ref)

        @pl.loop(2 * kp, NCH // 2)
        def _(j):
            for h in range(KV_PER_STEP):
                k, kt_, v = k_ref[h], kt_ref[h], v_ref[h]
                for qb in (2 * j, 2 * j + 1):
                    bias2 = jnp.concatenate([b_ref[jnp.maximum(qb - 4 * kp - t + 1, 0)] for t in range(4)], axis=0)
                    st = _dot(k, qt_ref[h, qb], NN) + bias2
                    pt = jnp.exp(st - lse_ref[h, qb])
                    dst = pt * (_dot(v, dot_ref[h, qb], NN) - dl_ref[h, qb])
                    dq_ref[h, qb] += _dot(kt_, dst, NN)
                    dk_ref[h] += _dot(qt_ref[h, qb], dst, NT)
                    dv_ref[h] += _dot(dot_ref[h, qb], pt, NT)

    kvs = KV_PER_STEP
    tspec = pl.BlockSpec((kvs, NCH, HD, QROWS), lambda j, i: (j, 0, 0, 0))
    kspec = pl.BlockSpec((kvs, 512, HD), lambda j, i: (j, i, 0))
    ktspec = pl.BlockSpec((kvs, HD, 512), lambda j, i: (j, 0, i))
    sspec = pl.BlockSpec((kvs, NCH, 1, QROWS), lambda j, i: (j, 0, 0, 0))
    return _call(
        body, "attn_bwd", (NKV // kvs, NCH // 4),
        [tspec, kspec, ktspec, kspec, tspec, sspec, sspec,
         pl.BlockSpec((NBIAS, 128, QROWS), lambda j, i: (0, 0, 0))],
        [tspec, ktspec, ktspec],
        [jax.ShapeDtypeStruct((NKV, NCH, HD, QROWS), F32),
         jax.ShapeDtypeStruct((NKV, HD, S), F32), jax.ShapeDtypeStruct((NKV, HD, S), F32)],
        (qt, kh, kt, vh, dot_, lse, delta, bias), sem=("parallel", "arbitrary"), rider=rider)


CONV_BLK = 256
CONV_COL0 = 1536 // CONV_BLK


CONV_ROWS = 128


def _conv_fwd(proj, convw, convb):
    trips = S // CONV_ROWS

    def body(u_ref, w_ref, b_ref, o_ref, y_ref):
        @pl.loop(0, trips)
        def _(c):
            t0 = pl.multiple_of(c * CONV_ROWS, CONV_ROWS)
            before = pl.multiple_of(jnp.maximum(t0 - 8, 0), 8)
            ext = jnp.concatenate([jnp.where(c == 0, 0.0, u_ref[pl.ds(before, 8), :]),
                                   u_ref[pl.ds(t0, CONV_ROWS), :]], axis=0)
            y = b_ref[...] + w_ref[CONV_K - 1:CONV_K, :] * ext[8:]
            for j in range(1, CONV_K):
                y = y + w_ref[CONV_K - 1 - j:CONV_K - j, :] * pltpu.roll(ext, j, 0)[8:]
            y_ref[pl.ds(t0, CONV_ROWS), :] = y
            o_ref[pl.ds(t0, CONV_ROWS), :] = y * _sigmoid(y)

    out = pl.BlockSpec((S, CONV_BLK), lambda i: (0, i))
    return pl.pallas_call(
        body, name="conv_fwd", grid=(CONV_C // CONV_BLK,),
        in_specs=[pl.BlockSpec((S, CONV_BLK), lambda i: (0, CONV_COL0 + i)),
                  pl.BlockSpec((CONV_K, CONV_BLK), lambda i: (0, i)),
                  pl.BlockSpec((1, CONV_BLK), lambda i: (0, i))],
        out_specs=[out, out], out_shape=[jax.ShapeDtypeStruct((S, CONV_C), F32)] * 2,
        compiler_params=_cparams("parallel"),
    )(proj, convw, convb)


def _conv_bwd(dact, ypre, proj, convw, dproj):
    trips = S // CONV_ROWS

    def body(da_ref, y_ref, u_ref, w_ref, buf_ref, du_ref, dw_ref, db_ref):
        dw_ref[...] = jnp.zeros_like(dw_ref)
        db_ref[...] = jnp.zeros_like(db_ref)
        r8 = lax.broadcasted_iota(jnp.int32, (8, CONV_BLK), 0)

        def dy_of(rows):
            y = y_ref[rows, :]
            sg = _sigmoid(y)
            return da_ref[rows, :] * (sg * (1.0 + y * (1.0 - sg)))

        @pl.loop(0, trips)
        def _(c):
            t0 = pl.multiple_of(c * CONV_ROWS, CONV_ROWS)
            after = pl.multiple_of(jnp.minimum(t0 + CONV_ROWS, S - 8), 8)
            ext = jnp.concatenate([dy_of(pl.ds(t0, CONV_ROWS)),
                                   jnp.where(c == trips - 1, 0.0, dy_of(pl.ds(after, 8)))], axis=0)
            u = u_ref[pl.ds(t0, CONV_ROWS), :]
            du, dw = None, jnp.zeros((8, CONV_BLK), F32)
            for j in range(CONV_K):
                dyj = (ext if j == 0 else pltpu.roll(ext, CONV_ROWS + 8 - j, 0))[:CONV_ROWS]
                term = w_ref[CONV_K - 1 - j:CONV_K - j, :] * dyj
                du = term if du is None else du + term
                dw = dw + jnp.where(r8 == CONV_K - 1 - j, jnp.sum(dyj * u, axis=0, keepdims=True), 0.0)
            du_ref[pl.ds(t0, CONV_ROWS), :] = du.astype(BF16)
            dw_ref[...] += dw
            db_ref[...] += jnp.sum(ext[:CONV_ROWS], axis=0, keepdims=True)

    return pl.pallas_call(
        body, name="conv_bwd", grid=(CONV_C // CONV_BLK,),
        in_specs=[pl.BlockSpec((S, CONV_BLK), lambda i: (0, i)), pl.BlockSpec((S, CONV_BLK), lambda i: (0, i)),
                  pl.BlockSpec((S, CONV_BLK), lambda i: (0, CONV_COL0 + i)),
                  pl.BlockSpec((CONV_K, CONV_BLK), lambda i: (0, i)), pl.BlockSpec(memory_space=pl.ANY)],
        out_specs=[pl.BlockSpec((S, CONV_BLK), lambda i: (0, CONV_COL0 + i)),
                   pl.BlockSpec((8, CONV_BLK), lambda i: (0, i)), pl.BlockSpec((1, CONV_BLK), lambda i: (0, i))],
        out_shape=[jax.ShapeDtypeStruct(dproj.shape, BF16), jax.ShapeDtypeStruct((8, CONV_C), F32),
                   jax.ShapeDtypeStruct((1, CONV_C), F32)],
        input_output_aliases={4: 0}, compiler_params=_cparams("parallel"),
    )(dact, ypre, proj, convw, dproj)


NPAIR = 8


def _ssd_scalars(dtr_ref, dtb_ref, alog_ref):
    z = dtr_ref[...] + dtb_ref[...]
    dt = jnp.maximum(z, 0.0) + jnp.log(1.0 + jnp.exp(-jnp.abs(z)))
    a = -jnp.exp(alog_ref[...])
    r = lax.broadcasted_iota(jnp.int32, (128, 128), 0)
    c = lax.broadcasted_iota(jnp.int32, (128, 128), 1)
    tri = (r >= c).astype(F32)
    cs = _dot_exact(tri, dt * a)
    return z, dt, a, cs, r, c


def _by_lane(cs, dt):
    head = lax.broadcasted_iota(jnp.int32, (128, SSM_W), 0)
    lane = lax.broadcasted_iota(jnp.int32, (128, SSM_W), 1)
    sel = (head == lane // HD).astype(F32)
    cs_l = _dot_exact(cs, sel, "b")
    last_l = cs_l[127:128, :]
    return sel, jnp.exp(cs_l), jnp.exp(last_l - cs_l), _dot_exact(dt, sel, "b")


def _pair_terms(cs, h1, h2):
    return (cs[:, h1:h1 + 1], cs[:, h2:h2 + 1],
            jnp.exp(cs[127:128, h1:h1 + 1]), jnp.exp(cs[127:128, h2:h2 + 1]))


def _gate_norm(y, zv, w):
    yg = y * (zv * _sigmoid(zv))
    outs, rs = [], []
    for g in range(2):
        blk = yg[:, 512 * g:512 * (g + 1)]
        r = lax.rsqrt(jnp.mean(blk * blk, axis=-1, keepdims=True) + EPS)
        outs.append(blk * r)
        rs.append(r)
    return jnp.concatenate(outs, axis=1), rs, yg


def _ssd_fwd(xbc, proj, dtb, alog, dskip_l, ssmw, mixed):
    def body(x_ref, b_ref, c_ref, dtr_ref, z_ref, dtb_ref, alog_ref, dsk_ref, w_ref, buf_ref,
             y_ref, yn_ref, hp_ref, h_ref):
        @pl.when(pl.program_id(0) == 0)
        def _():
            h_ref[...] = jnp.zeros_like(h_ref)

        _, dt, _, cs, r, c = _ssd_scalars(dtr_ref, dtb_ref, alog_ref)
        cst = cs.T
        causal = r >= c
        lo = c < HD
        _, e_all, dte_all, dt_all = _by_lane(cs, dt)
        hp_ref[...] = h_ref[...]
        for g in range(2):
            bg = b_ref[:, 128 * g:128 * (g + 1)]
            cg = c_ref[:, 128 * g:128 * (g + 1)]
            cb = _dot(cg, bg, NT)
            for j in range(4):
                pj = 4 * g + j
                h1, h2 = 2 * pj, 2 * pj + 1
                sl = slice(128 * pj, 128 * (pj + 1))
                xp = x_ref[:, sl]
                c1, c2, cd1, cd2 = _pair_terms(cs, h1, h2)
                e_l, dte_l = e_all[:, sl], dte_all[:, sl]
                xdt = xp * dt_all[:, sl]
                m1 = cb * jnp.exp(jnp.where(causal, c1 - cst[h1:h1 + 1, :], NEG))
                m2 = cb * jnp.exp(jnp.where(causal, c2 - cst[h2:h2 + 1, :], NEG))
                yd = jnp.where(lo, _dot(m1, xdt, NN), _dot(m2, xdt, NN))
                hp = h_ref[pj]
                yo = _dot(cg, hp, NT) * e_l
                st = _dot(xdt * dte_l, bg, TN)
                h_ref[pj] = hp * jnp.where(r < HD, cd1, cd2) + st
                y_ref[:, sl] = yd + yo + dsk_ref[:, sl] * xp
        yn, _, _ = _gate_norm(y_ref[...], z_ref[...], w_ref[...])
        yn_ref[...] = (yn * w_ref[...]).astype(BF16)

    return pl.pallas_call(
        body, name="ssd_fwd", grid=(NCH,),
        in_specs=[pl.BlockSpec((128, SSM_W), lambda i: (i, 0)),
                  pl.BlockSpec((128, 256), lambda i: (i, 4)), pl.BlockSpec((128, 256), lambda i: (i, 5)),
                  pl.BlockSpec((128, 128), lambda i: (i, COL_DT // 128)),
                  pl.BlockSpec((128, SSM_W), lambda i: (i, 3)),
                  pl.BlockSpec((1, 128), lambda i: (0, 0)), pl.BlockSpec((1, 128), lambda i: (0, 0)),
                  pl.BlockSpec((1, SSM_W), lambda i: (0, 0)), pl.BlockSpec((1, SSM_W), lambda i: (0, 0)),
                  pl.BlockSpec(memory_space=pl.ANY)],
        out_specs=[pl.BlockSpec((128, SSM_W), lambda i: (i, 0)), pl.BlockSpec((128, SSM_W), lambda i: (i, 1)),
                   pl.BlockSpec((None, NPAIR, 128, 128), lambda i: (i, 0, 0, 0))],
        out_shape=[jax.ShapeDtypeStruct((S, SSM_W), F32), jax.ShapeDtypeStruct(mixed.shape, BF16),
                   jax.ShapeDtypeStruct((NCH, NPAIR, 128, 128), F32)],
        scratch_shapes=[pltpu.VMEM((NPAIR, 128, 128), F32)],
        input_output_aliases={9: 1}, compiler_params=_cparams("arbitrary"),
    )(xbc, xbc, xbc, proj, proj, dtb, alog, dskip_l, ssmw, mixed)


def _ssd_bwd(dmixed, y, xbc, proj, hprev, dtb, alog, dskip_l, ssmw, rider=None):
    def body(dyn_ref, y_ref, x_ref, b_ref, c_ref, dtr_ref, z_ref, hp_ref, dtb_ref, alog_ref, dsk_ref, w_ref,
             dxbc_ref, dz_ref, ddt_ref, dw_ref, dsc_ref, g_ref):
        @pl.when(pl.program_id(0) == 0)
        def _():
            g_ref[...] = jnp.zeros_like(g_ref)
            dsc_ref[...] = jnp.zeros_like(dsc_ref)

        z, dt, a, cs, r, c = _ssd_scalars(dtr_ref, dtb_ref, alog_ref)
        cst = cs.T
        causal = r >= c
        lo = c < HD

        yv = y_ref[...]
        zv = z_ref[...]
        wv = w_ref[...]
        ygn, rs, yg = _gate_norm(yv, zv, wv)
        dyn = dyn_ref[...]
        _acc_rows(dw_ref, dyn * ygn)
        dynw = dyn * wv
        parts = []
        for g in range(2):
            sl = slice(512 * g, 512 * (g + 1))
            a_g, n_g = dynw[:, sl], ygn[:, sl]
            parts.append(rs[g] * (a_g - n_g * jnp.mean(a_g * n_g, axis=-1, keepdims=True)))
        dyg = jnp.concatenate(parts, axis=1)
        sz = _sigmoid(zv)
        dz_ref[...] = (dyg * yv * (sz * (1.0 + zv * (1.0 - sz)))).astype(BF16)
        dy_all = dyg * (zv * sz)

        dcs_cols = jnp.zeros((128, 128), F32)
        dcs_rows = jnp.zeros((128, 128), F32)
        sel, e_all, dte_all, dt_all = _by_lane(cs, dt)
        x_all, b_all, c_all, dsk_all = x_ref[...], b_ref[...], c_ref[...], dsk_ref[...]
        hp_all, g_all = hp_ref[...], g_ref[...]
        g_new, dx_parts, db_parts, dc_parts = [], [], [], []
        dyx_parts, ryo_parts, qx_parts, dxx_parts, gh_parts = [], [], [], [], []
        for g in range(2):
            bg = b_all[:, 128 * g:128 * (g + 1)]
            cg = c_all[:, 128 * g:128 * (g + 1)]
            cb = _dot(cg, bg, NT)
            dcb = jnp.zeros((128, 128), F32)
            db_acc = jnp.zeros((128, NST), F32)
            dc_acc = jnp.zeros((128, NST), F32)
            for j in range(4):
                pj = 4 * g + j
                h1, h2 = 2 * pj, 2 * pj + 1
                sl = slice(128 * pj, 128 * (pj + 1))
                xp = x_all[:, sl]
                dyp = dy_all[:, sl]
                c1, c2, cd1, cd2 = _pair_terms(cs, h1, h2)
                e_l, dte_l, dt_l = e_all[:, sl], dte_all[:, sl], dt_all[:, sl]
                xdt = xp * dt_l
                hp = hp_all[pj]
                gp = g_all[pj]
                dyx_parts.append(dyp * xp)
                dzs = dyp * e_l
                dc_acc = dc_acc + _dot(dzs, hp, NN)
                ryo_parts.append(dyp * (_dot(cg, hp, NT) * e_l))
                qm = _dot(bg, gp, NT)
                dxdt = qm * dte_l
                qx_parts.append(qm * xdt)
                db_acc = db_acc + _dot(xdt * dte_l, gp, NN)
                gh_parts.append(gp * hp)
                g_new.append(_dot(dzs, cg, TN) + jnp.where(r < HD, cd1, cd2) * gp)
                for hh, ch, msk in ((h1, c1, lo), (h2, c2, jnp.logical_not(lo))):
                    lm = jnp.exp(jnp.where(causal, ch - cst[hh:hh + 1, :], NEG))
                    mm = cb * lm
                    dm = jnp.where(causal, _dot(jnp.where(msk, dyp, 0.0), xdt, NT), 0.0)
                    w = dm * mm
                    dcs_cols = dcs_cols + jnp.where(c == hh, jnp.sum(w, axis=1, keepdims=True), 0.0)
                    dcs_rows = dcs_rows + jnp.where(r == hh, jnp.sum(w, axis=0, keepdims=True), 0.0)
                    dcb = dcb + dm * lm
                    dxdt = dxdt + jnp.where(msk, _dot(mm, dyp, TN), 0.0)
                dxx_parts.append(dxdt * xp)
                dx_parts.append(dsk_all[:, sl] * dyp + dxdt * dt_l)
            db_parts.append(db_acc + _dot(dcb, cg, TN))
            dc_parts.append(dc_acc + _dot(dcb, bg, NN))
        g_ref[...] = jnp.stack(g_new)
        dxbc_ref[...] = jnp.concatenate(dx_parts + db_parts + dc_parts, axis=1)

        selt = (lax.broadcasted_iota(jnp.int32, (SSM_W, 128), 0) // HD
                == lax.broadcasted_iota(jnp.int32, (SSM_W, 128), 1)).astype(F32)

        def by_head(parts):
            return _dot_exact(jnp.concatenate(parts, axis=1), selt, "b")

        ddt_x = by_head(dxx_parts)
        dd_row = jnp.sum(by_head(dyx_parts), axis=0, keepdims=True)
        t_all = by_head(qx_parts) * jnp.exp(cs[127:128, :] - cs)
        gh = jnp.sum(_dot_exact(sel, jnp.concatenate(gh_parts, axis=0)), axis=1, keepdims=True)
        gh_row = jnp.broadcast_to(gh, (128, 128)).T[0:1, :]
        at_end = jnp.sum(t_all, axis=0, keepdims=True) + gh_row * jnp.exp(cs[127:128, :])
        dcs = by_head(ryo_parts) - t_all + dcs_cols + jnp.where(r == 127, at_end, 0.0) - dcs_rows.T
        dad = _dot_exact((c >= r).astype(F32), dcs)
        ddt = dad * a + ddt_x
        ddtr = jnp.where(c < 16, ddt * _sigmoid(z), 0.0)
        ddt_ref[...] = ddtr.astype(BF16)
        r8 = lax.broadcasted_iota(jnp.int32, (8, 128), 0)
        dsc_ref[...] += (jnp.where(r8 == 0, jnp.sum(ddtr, axis=0, keepdims=True), 0.0)
                         + jnp.where(r8 == 1, jnp.sum(dad * dt, axis=0, keepdims=True) * a, 0.0)
                         + jnp.where(r8 == 2, dd_row, 0.0))

    rev = NCH - 1
    return _call(
        body, "ssd_bwd", (NCH,),
        [pl.BlockSpec((128, SSM_W), lambda i: (rev - i, 0)),
         pl.BlockSpec((128, SSM_W), lambda i: (rev - i, 0)),
         pl.BlockSpec((128, SSM_W), lambda i: (rev - i, 0)),
         pl.BlockSpec((128, 256), lambda i: (rev - i, 4)), pl.BlockSpec((128, 256), lambda i: (rev - i, 5)),
         pl.BlockSpec((128, 128), lambda i: (rev - i, COL_DT // 128)),
         pl.BlockSpec((128, SSM_W), lambda i: (rev - i, 3)),
         pl.BlockSpec((None, NPAIR, 128, 128), lambda i: (rev - i, 0, 0, 0)),
         pl.BlockSpec((1, 128), lambda i: (0, 0)), pl.BlockSpec((1, 128), lambda i: (0, 0)),
         pl.BlockSpec((1, SSM_W), lambda i: (0, 0)), pl.BlockSpec((1, SSM_W), lambda i: (0, 0))],
        [pl.BlockSpec((128, CONV_C), lambda i: (rev - i, 0)),
         pl.BlockSpec((128, SSM_W), lambda i: (rev - i, 3)),
         pl.BlockSpec((128, 128), lambda i: (rev - i, 0)),
         pl.BlockSpec((1, SSM_W), lambda i: (0, 0)), pl.BlockSpec((8, 128), lambda i: (0, 0))],
        [jax.ShapeDtypeStruct((S, CONV_C), F32), jax.ShapeDtypeStruct((S, WIN_PAD), BF16),
         jax.ShapeDtypeStruct((S, 128), BF16), jax.ShapeDtypeStruct((1, SSM_W), F32),
         jax.ShapeDtypeStruct((8, 128), F32)],
        (dmixed, y, xbc, xbc, xbc, proj, proj, hprev, dtb, alog, dskip_l, ssmw),
        [pltpu.VMEM((NPAIR, 128, 128), F32)], ("arbitrary",), rider)


def _cast_stack(name, slot, arrs, tr, tc):
    n = len(arrs)
    rows, cols = arrs[0].shape

    def body(s_ref, *refs):
        for i in range(n):
            refs[n][i] = refs[i][...].astype(BF16)

    return pl.pallas_call(
        body, name=name,
        grid_spec=pltpu.PrefetchScalarGridSpec(
            num_scalar_prefetch=1, grid=(rows // tr, cols // tc),
            in_specs=[pl.BlockSpec((tr, tc), lambda i, j, sr: (i, j))] * n,
            out_specs=pl.BlockSpec((None, n, tr, tc), lambda i, j, sr: (sr[0], 0, i, j))),
        out_shape=jax.ShapeDtypeStruct((NSH, n, rows, cols), BF16),
        compiler_params=_cparams("parallel", "parallel"),
    )(slot, *arrs)


def _pair_sum(name, c_idx, ps, th):
    n = len(ps)
    _, rows, _ = ps[0].shape

    def body(c_ref, *refs):
        mine, whole, out, theirs = refs[:n], refs[n:2 * n], refs[2 * n:3 * n], refs[3 * n:4 * n]
        send, recv = refs[4 * n], refs[4 * n + 1]
        s, i = pl.program_id(0), pl.program_id(1)
        x, y, c, _ = _place()

        def copies(slot):
            return [_rcopy(whole[k].at[slot, :, pl.ds((1 - c) * HALF, HALF)], theirs[k].at[slot],
                           send.at[slot * n + k], recv.at[slot * n + k], (x, y, 1 - c)) for k in range(n)]

        @pl.when((s == 0) & (i == 0))
        def _():
            for slot in range(NSH):
                for cp in copies(slot):
                    cp.start()

        @pl.when(i == 0)
        def _():
            for slot in range(NSH):
                @pl.when(s == slot)
                def _():
                    for cp in copies(slot):
                        cp.wait()

        rows_i = slice(None) if th == rows else pl.ds(pl.multiple_of(i * th, th), th)
        for k in range(n):
            out[k][...] = (mine[k][...].astype(F32) + theirs[k][s, rows_i, :].astype(F32)).astype(BF16)

    spec = pl.BlockSpec((None, th, HALF), lambda s, i, cr: (s, i, 0))
    return pl.pallas_call(
        body, name=name,
        grid_spec=pltpu.PrefetchScalarGridSpec(
            num_scalar_prefetch=1, grid=(NSH, rows // th),
            in_specs=[pl.BlockSpec((None, th, HALF), lambda s, i, cr: (s, i, cr[0]))] * n + _any_specs(n),
            out_specs=[spec] * n,
            scratch_shapes=[pltpu.VMEM((NSH, rows, HALF), BF16)] * n
            + [pltpu.SemaphoreType.DMA((NSH * n,)), pltpu.SemaphoreType.DMA((NSH * n,))]),
        out_shape=[jax.ShapeDtypeStruct((NSH, rows, HALF), BF16)] * n,
        compiler_params=_cparams("arbitrary", "arbitrary"),
    )(c_idx, *ps, *ps)


def _pair_add(name, c_idx, ps, theirs, th):
    n = len(ps)
    _, rows, _ = ps[0].shape

    def body(c_ref, *refs):
        for k in range(n):
            refs[2 * n + k][...] = (refs[k][...].astype(F32) + refs[n + k][...].astype(F32)).astype(BF16)

    spec = pl.BlockSpec((None, th, HALF), lambda s, i, cr: (s, i, 0))
    return pl.pallas_call(
        body, name=name,
        grid_spec=pltpu.PrefetchScalarGridSpec(
            num_scalar_prefetch=1, grid=(NSH, rows // th),
            in_specs=[pl.BlockSpec((None, th, HALF), lambda s, i, cr: (s, i, cr[0]))] * n + [spec] * n,
            out_specs=[spec] * n),
        out_shape=[jax.ShapeDtypeStruct((NSH, rows, HALF), BF16)] * n,
        compiler_params=_cparams("parallel", "parallel"),
    )(c_idx, *ps, *theirs)


def _chip_sum(name, place, cs, ts, th):
    n = len(ts)
    _, rows, _ = ts[0].shape

    def body(p_ref, *refs):
        for i in range(n):
            t = refs[n + i][...].astype(F32)
            refs[2 * n + i][...] = ((refs[i][...].astype(F32) + t[0]) + t[1]) + t[2]

    return pl.pallas_call(
        body, name=name,
        grid_spec=pltpu.PrefetchScalarGridSpec(
            num_scalar_prefetch=1, grid=(rows // th,),
            in_specs=[pl.BlockSpec((None, th, HALF), lambda i, pr: (pr[0], i, 0))] * n
            + [pl.BlockSpec((3, th, HALF), lambda i, pr: (0, i, 0))] * n,
            out_specs=[pl.BlockSpec((th, HALF), lambda i, pr: (i, pr[1]))] * n),
        out_shape=[jax.ShapeDtypeStruct((rows, D), F32)] * n, compiler_params=_cparams("parallel"),
    )(place, *cs, *ts)


def _adamw(name, ws, gs, ms, vs, tr, tc):
    n = len(ws)
    shape = ws[0].shape
    rows, cols, mid = shape[0], shape[-1], shape[1:-1]
    c1 = 1.0 / (1.0 - ADAM_B1 ** ADAM_STEP)
    c2 = 1.0 / (1.0 - ADAM_B2 ** ADAM_STEP)

    def body(*refs):
        for i in range(n):
            w, g, m, v = (refs[k * n + i][...] for k in range(4))
            m2 = ADAM_B1 * m + (1.0 - ADAM_B1) * g
            v2 = ADAM_B2 * v + (1.0 - ADAM_B2) * (g * g)
            refs[4 * n + 4 * i][...] = -ADAM_LR * ((m2 * c1) / (jnp.sqrt(v2 * c2) + ADAM_EPS) + ADAM_WD * w)
            refs[4 * n + 4 * i + 1][...] = m2
            refs[4 * n + 4 * i + 2][...] = v2
            refs[4 * n + 4 * i + 3][...] = g

    spec = pl.BlockSpec((tr,) + mid + (tc,), lambda i, j: (i,) + (0,) * len(mid) + (j,))
    outs = pl.pallas_call(
        body, name=name, grid=(rows // tr, cols // tc), in_specs=[spec] * (4 * n), out_specs=[spec] * (4 * n),
        out_shape=[jax.ShapeDtypeStruct(shape, F32)] * (4 * n),
        compiler_params=_cparams("parallel", "parallel"),
    )(*ws, *gs, *ms, *vs)
    return [tuple(outs[4 * i:4 * i + 4]) for i in range(n)]


def _place():
    x, y, c = lax.axis_index("x"), lax.axis_index("y"), lax.axis_index("c")
    chips = [(1 - x, y), (x, 1 - y), (1 - x, 1 - y)]
    return x, y, c, chips


def _any_specs(n):
    return [pl.BlockSpec(memory_space=pl.ANY)] * n


def _rcopy(src, dst, send_sem, recv_sem, dev):
    return pltpu.make_async_remote_copy(src_ref=src, dst_ref=dst, send_sem=send_sem, recv_sem=recv_sem,
                                        device_id=dev, device_id_type=MESH)


QUARTER = HALF // 2

XA, XB, YA, YB, RX, RY, F_XA, F_XB, F_YA, F_YB, F_D0, F_D1 = range(12)


def _gather_rider(bufs, views, at=None):
    n = len(bufs)

    def plan(rout, sems):
        send, recv = sems
        x, y, c, _ = _place()
        me, sx, sy, sd = 2 * x + y, 2 * (1 - x) + y, 2 * x + (1 - y), 2 * (1 - x) + (1 - y)
        nx, ny, sib = (1 - x, y, c), (x, 1 - y, c), (x, y, 1 - c)
        q0, q1 = c * HALF, c * HALF + QUARTER
        o0, o1 = (1 - c) * HALF, (1 - c) * HALF + QUARTER
        out = {XA: (me, q1, nx), XB: (me, q0, nx), YA: (me, q0, ny), YB: (me, q1, ny),
               RX: (sy, q0, nx), RY: (sx, q1, ny),
               F_XA: (sx, q1, sib), F_XB: (sx, q0, sib), F_YA: (sy, q0, sib), F_YB: (sy, q1, sib),
               F_D0: (sd, q0, sib), F_D1: (sd, q1, sib)}
        inn = {XA: (sx, q1), XB: (sx, q0), YA: (sy, q0), YB: (sy, q1), RX: (sd, q0), RY: (sd, q1),
               F_XA: (sx, o1), F_XB: (sx, o0), F_YA: (sy, o0), F_YB: (sy, o1), F_D0: (sd, o0), F_D1: (sd, o1)}

        def copy(kind, b):
            slot, col, dev = out[kind]
            win = views[b](rout[b], slot, col, QUARTER)
            return _rcopy(win, win, send.at[kind * n + b], recv.at[kind * n + b], dev)

        def land(kind, b):
            slot, col = inn[kind]
            win = views[b](rout[b], slot, col, QUARTER)
            return _rcopy(win, win, send.at[kind * n + b], recv.at[kind * n + b], (x, y, c))

        return copy, land

    first = (XA, YA, XB, YB)
    early = ((XA, (RY, F_XA)), (YA, (RX, F_YA)))
    late = ((XB, (F_XB,)), (YB, (F_YB,)), (RX, (F_D0,)), (RY, (F_D1,)))
    forwards = (F_XA, F_XB, F_YA, F_YB, F_D0, F_D1)
    sent = first + (RX, RY) + forwards

    def start(rin, rout, sems):
        copy, _ = plan(rout, sems)
        for kind in first:
            for b in range(n):
                copy(kind, b).start()

    def pass_on(rout, sems, links):
        copy, land = plan(rout, sems)
        for landed, then in links:
            for b in range(n):
                land(landed, b).wait_recv()
                for kind in then:
                    copy(kind, b).start()

    def between(rin, rout, sems):
        pass_on(rout, sems, early)

    def finish(rin, rout, sems):
        pass_on(rout, sems, late)
        copy, land = plan(rout, sems)
        for kind in forwards:
            for b in range(n):
                land(kind, b).wait_recv()
        for kind in sent:
            for b in range(n):
                copy(kind, b).wait_send()

    return _Rider(list(bufs), [jax.ShapeDtypeStruct(a.shape, a.dtype) for a in bufs], {b: b for b in range(n)},
                  [pltpu.SemaphoreType.DMA((12 * n,))] * 2, start, finish, between, at)


def _small_gather_rider(cw):
    def descs(rin, rout, sems, x, y, c, chips):
        return [_rcopy(rin[0], rout[0].at[2 * x + y], sems[1].at[j], sems[2].at[j], (chip[0], chip[1], c))
                for j, chip in enumerate(chips)]

    def start(rin, rout, sems):
        x, y, c, chips = _place()
        pltpu.make_async_copy(rin[0], rout[0].at[2 * x + y], sems[0].at[0]).start()
        for cp in descs(rin, rout, sems, x, y, c, chips):
            cp.start()

    def finish(rin, rout, sems):
        x, y, c, chips = _place()
        for j, chip in enumerate(chips):
            _rcopy(rin[0], rout[0].at[2 * chip[0] + chip[1]], sems[1].at[j], sems[2].at[j], (x, y, c)).wait_recv()
        for cp in descs(rin, rout, sems, x, y, c, chips):
            cp.wait_send()
        pltpu.make_async_copy(rin[0], rout[0].at[2 * x + y], sems[0].at[0]).wait()

    return _Rider([cw], [jax.ShapeDtypeStruct((NSH,) + cw.shape, cw.dtype)], {},
                  [pltpu.SemaphoreType.DMA((1,)), pltpu.SemaphoreType.DMA((3,)), pltpu.SemaphoreType.DMA((3,))],
                  start, finish)


def _to_sibling_rider(ps):
    n = len(ps)

    def descs(rin, rout, sems):
        x, y, c, _ = _place()
        return [_rcopy(rin[i].at[:, :, pl.ds((1 - c) * HALF, HALF)], rout[i], sems[0].at[i], sems[1].at[i],
                       (x, y, 1 - c)) for i in range(n)]

    def start(rin, rout, sems):
        for cp in descs(rin, rout, sems):
            cp.start()

    def finish(rin, rout, sems):
        for cp in descs(rin, rout, sems):
            cp.wait()

    return _Rider(list(ps), [jax.ShapeDtypeStruct(a.shape[:2] + (HALF,), a.dtype) for a in ps], {},
                  [pltpu.SemaphoreType.DMA((n,))] * 2, start, finish)


def _to_chips_rider(cs, first=0, count=None, into=None):
    n = len(cs)
    rows = [pl.ds(first, a.shape[1] - first if count is None else count) for a in cs]

    def descs(rin, rout, sems):
        x, y, c, chips = _place()
        return [_rcopy(rin[i].at[2 * chip[0] + chip[1], rows[i]], rout[i].at[j, rows[i]], sems[0].at[j * n + i],
                       sems[1].at[j * n + i], (chip[0], chip[1], c)) for j, chip in enumerate(chips) for i in range(n)]

    def start(rin, rout, sems):
        for cp in descs(rin, rout, sems):
            cp.start()

    def finish(rin, rout, sems):
        for cp in descs(rin, rout, sems):
            cp.wait()

    return _Rider(list(cs) + list(into or []), [jax.ShapeDtypeStruct((3,) + a.shape[1:], a.dtype) for a in cs],
                  {n + i: i for i in range(n)} if into else {}, [pltpu.SemaphoreType.DMA((3 * n,))] * 2, start, finish)


def _join_riders(riders):
    counts = [[len(r.operands) for r in riders], [len(r.out_shapes) for r in riders], [len(r.sems) for r in riders]]

    def each(step):
        def run(*refs):
            at = [0, 0, 0]
            for i, r in enumerate(riders):
                parts = [group[at[k]:at[k] + counts[k][i]] for k, group in enumerate(refs)]
                at = [at[k] + counts[k][i] for k in range(3)]
                step(r)(*parts)
        return run

    aliases = {sum(counts[0][:i]) + k: sum(counts[1][:i]) + v
               for i, r in enumerate(riders) for k, v in r.aliases.items()}
    return _Rider([a for r in riders for a in r.operands], [s for r in riders for s in r.out_shapes], aliases,
                  [s for r in riders for s in r.sems], each(lambda r: r.start), each(lambda r: r.finish),
                  each(lambda r: r.between or (lambda *refs: None)))


SMALL_ROWS = 16


def _swap_halves(gs, vec):
    n = len(gs)

    def body(*refs):
        v_ref, dst, o_ref = refs[n], refs[n + 1:2 * n + 1], refs[2 * n + 1]
        buf, send, recv, vsend, vrecv = refs[2 * n + 2:]
        x, y, c, _ = _place()
        cps = []
        for i in range(n):
            mine = dst[i].at[:, pl.ds(c * HALF, HALF)]
            cps.append(_rcopy(mine, mine, send.at[i], recv.at[i], (x, y, 1 - c)))
        for cp in cps:
            cp.start()

        me = 4 * x + 2 * y + c
        buf[me] = v_ref[...]
        vcps = []
        for k in range(1, 8):
            peer = (x ^ (k >> 2), y ^ ((k >> 1) & 1), c ^ (k & 1))
            vcps.append(_rcopy(v_ref, buf.at[me], vsend.at[k - 1], vrecv.at[k - 1], peer))
        for cp in vcps:
            cp.start()
        for k in range(1, 8):
            _rcopy(v_ref, buf.at[me ^ k], vsend.at[k - 1], vrecv.at[k - 1], (x, y, c)).wait_recv()
        for cp in vcps:
            cp.wait_send()
        t = buf[0]
        for d in range(1, 8):
            t = t + buf[d]
        o_ref[...] = t

        for i in range(n):
            other = dst[i].at[:, pl.ds((1 - c) * HALF, HALF)]
            _rcopy(other, other, send.at[i], recv.at[i], (x, y, c)).wait_recv()
        for cp in cps:
            cp.wait_send()

    vmem = pl.BlockSpec(memory_space=pltpu.VMEM)
    res = pl.pallas_call(
        body, name="grads_swap_halves", in_specs=_any_specs(n) + [vmem], out_specs=_any_specs(n) + [vmem],
        out_shape=[jax.ShapeDtypeStruct(g.shape, g.dtype) for g in gs] + [jax.ShapeDtypeStruct((SMALL_ROWS, D), F32)],
        input_output_aliases={i: i for i in range(n)},
        scratch_shapes=[pltpu.VMEM((8, SMALL_ROWS, D), F32)] + [pltpu.SemaphoreType.DMA((n,))] * 2
        + [pltpu.SemaphoreType.DMA((7,))] * 2,
    )(*gs, vec)
    return list(res[:n]), res[n]


def _col_window(ref, slot, col, ncols):
    return ref.at[slot, :, pl.ds(col, ncols)]


def _stack_window(first, count):
    def view(ref, slot, col, ncols):
        return ref.at[slot, pl.ds(first, count), :, pl.ds(col, ncols)]
    return view


def _row_tile(rows):
    for t in range(512, 15, -16):
        if rows % t == 0:
            return t
    return rows


def _same_shape_runs(arrs):
    runs, a = [], 0
    for b in range(1, len(arrs) + 1):
        if b == len(arrs) or arrs[b].shape != arrs[a].shape:
            runs.append((a, b))
            a = b
    return runs


class _Comm:
    def __init__(self):
        x, y, c = lax.axis_index("x"), lax.axis_index("y"), lax.axis_index("c")
        self.c_idx = jnp.reshape(c, (1,)).astype(jnp.int32)
        self.shard = jnp.reshape(2 * x + y, (1,)).astype(jnp.int32)
        self.place = jnp.stack([2 * x + y, c]).astype(jnp.int32)
        self.groups = {}
        self.sent = {}

    @staticmethod
    def gather(*bufs, part=None, at=None):
        views = [_col_window if b.ndim == 3 else _stack_window(*(part or (0, b.shape[1]))) for b in bufs]
        return _gather_rider(list(bufs), views, at)

    def reduce_rider(self, tag, names, ps, theirs=None, rows=None):
        csums = []
        for a, b in _same_shape_runs(ps):
            name, th = "pair_sum_%s%d" % (tag, a), _row_tile(ps[a].shape[1])
            csums += (_pair_sum(name, self.c_idx, ps[a:b], th) if theirs is None else
                      _pair_add(name, self.c_idx, ps[a:b], theirs[a:b], th))
        self.groups[tag] = [names, csums, None]
        self.sent[tag] = rows
        return _to_chips_rider(csums, 0, rows)

    def rest_rider(self, tag):
        _, csums, ts = self.groups[tag]
        return _to_chips_rider(csums, self.sent[tag], None, ts)

    def landed(self, tag, ts):
        self.groups[tag][2] = ts

    def finish(self, small):
        names, csums, ts = [], [], []
        for group_names, group_csums, group_ts in self.groups.values():
            names += group_names
            csums += group_csums
            ts += group_ts
        order = sorted(range(len(names)), key=lambda i: csums[i].shape[1])
        names, csums, ts = ([v[i] for i in order] for v in (names, csums, ts))
        halves = []
        for a, b in _same_shape_runs(csums):
            halves += _chip_sum("chip_sum_%d" % a, self.place, csums[a:b], ts[a:b], _row_tile(csums[a].shape[1]))
        grads, total = _swap_halves(halves, small)
        return dict(zip(names, grads)), total


ROPE_THETA = 10000.0
SMALL_1K = ("ffn1_pre_norm", "ffn1_post_norm", "mix_pre_norm", "ssm_norm", "mix_post_norm",
            "ffn2_pre_norm", "ffn2_post_norm")
SMALL_16 = ("dt_bias", "a_log", "d_skip")
OFF_CONVB = 7 * D
OFF_16 = OFF_CONVB + CONV_C
OFF_CONVW = OFF_16 + 48
OFF_LOSS = OFF_CONVW + CONV_K * CONV_C
SMALL_LEN = SMALL_ROWS * D


def _sds(shape, dtype):
    return jax.ShapeDtypeStruct(shape, dtype)


def _ridden(res, rider):
    return res if rider is not None else (res, None)


def _ffn_down(name, act, w, tail_of, rider=None):
    tail, o_specs, o_shapes = tail_of(TS)
    return _mm(name, [act, w.dn], NN, (S // TS,),
               [pl.BlockSpec((NSH, TS, FS), lambda i: (0, i, 0)),
                pl.BlockSpec((NSH, None, FS, D), lambda i: (0, w.d0, 0, 0))], o_specs, o_shapes, rider, tail)


def _ffn_dw(name, a, b, rider=None):
    return _mm(name, [a, b], TN, (NSH,),
               [pl.BlockSpec((None, S, FS), lambda s: (s, 0, 0)), pl.BlockSpec((S, D), lambda s: (0, 0))],
               pl.BlockSpec((None, FS, D), lambda s: (s, 0, 0)), _sds((NSH, FS, D), BF16), rider)


def _ffn_dn(name, dgate, dup, w, tail_of, rider=None):
    rows = TS // 2
    tail, o_specs, o_shapes = tail_of(rows)
    a2 = pl.BlockSpec((NSH, rows, FS), lambda i: (0, i, 0))
    return _mm(name, [dgate, w.gu, dup, w.gu], NN, (S // rows,),
               [a2, pl.BlockSpec((NSH, None, FS, D), lambda i: (0, w.g0, 0, 0)),
                a2, pl.BlockSpec((NSH, None, FS, D), lambda i: (0, w.g0 + 1, 0, 0))], o_specs, o_shapes, rider, tail)


def _out_proj_dx(dh, wout, ot):
    def body(dh_ref, w_ref, o_ref, dyn_ref, do_ref, dl_ref):
        dm = _dot(dh_ref[...], w_ref[...], NT)
        dyn_ref[...] = dm[:, D:]
        for b in range(TS // 128):
            for j, blk in enumerate(_rows_to_blocks(dm[128 * b:128 * (b + 1), :D])):
                do = blk.astype(BF16)
                do_ref[j, b] = do
                dl_ref[j, b] = jnp.sum(o_ref[j, b] * do.astype(F32), axis=0, keepdims=True)

    blocks = pl.BlockSpec((NKV, TS // 128, HD, QROWS), lambda i: (0, i, 0, 0))
    return pl.pallas_call(
        body, name="out_proj_dx", grid=(S // TS,),
        in_specs=[pl.BlockSpec((TS, D), lambda i: (i, 0)), pl.BlockSpec((2 * D, D), lambda i: (0, 0)), blocks],
        out_specs=[pl.BlockSpec((TS, D), lambda i: (i, 0)), blocks,
                   pl.BlockSpec((NKV, TS // 128, 1, QROWS), lambda i: (0, i, 0, 0))],
        out_shape=[_sds((S, D), F32), _sds((NKV, NCH, HD, QROWS), BF16), _sds((NKV, NCH, 1, QROWS), F32)],
        compiler_params=_cparams("parallel"),
    )(dh, wout, ot)


def _heads(t, n):
    return t.reshape(S, n, HD).transpose(1, 0, 2)


def _pad128(v):
    return jnp.pad(v, ((0, 0), (0, 128 - v.shape[1])))


def _local_step(x, positions, tgt, sp, gu1, d1, f2, wint, wout, convw, comm=None):
    inv_freq = ROPE_THETA ** (-jnp.arange(0, HD, 2, dtype=F32) / HD)
    ang = positions.astype(F32)[:, None] * inv_freq
    ang = jnp.concatenate([ang, ang, ang, ang], axis=-1)
    cos, sin = jnp.cos(ang), jnp.sin(ang)
    dtb, alog = _pad128(sp["dt_bias"]), _pad128(sp["a_log"])
    dskip_l = jnp.repeat(sp["d_skip"], HD, axis=1)
    convb = sp["conv_b"]

    if comm:
        rider = _join_riders([comm.gather(gu1), _small_gather_rider(convw)])
        (n1, d1, f2, wint, wout), (gu1, convw) = _prenorm_casts(
            "prenorm1", x, sp["ffn1_pre_norm"], comm.shard, [(d1, 0), (f2, 0), (wint, 1), (wout, 0)], rider)
        wint, wout = wint.reshape(NSH, WIN_SH, D), wout.reshape(NSH, 2 * D // NSH, D)
        convw = convw.transpose(1, 0, 2).reshape(CONV_K, CONV_C)
    else:
        n1 = _prenorm("prenorm1", x, sp["ffn1_pre_norm"])
    rider = comm.gather(d1) if comm else None
    (fg1, fu1, act1), got = _ridden(_ffn_up("ffn1_up", n1, _FfnW(gu1, 0, d1, 0), rider), rider)
    if comm:
        d1, = got
    w1 = _FfnW(gu1, 0, d1, 0)
    rider = comm.gather(wint) if comm else None
    (h1, x1, n2), got = _ridden(_ffn_down(
        "ffn1_down", act1, w1,
        lambda rows: _tail_postres(rows, x, sp["ffn1_post_norm"], 0.5, sp["mix_pre_norm"]), rider), rider)
    if comm:
        wint, = got
    wint_pad = jnp.pad(wint.reshape(WIN_COLS, D), ((0, WIN_PAD - WIN_COLS), (0, 0)))

    pw = WIN_PAD // 3
    rider = comm.gather(f2, part=(0, 1)) if comm else None
    proj, got = _ridden(_mm(
        "in_proj", [n2, wint_pad], NT, (S // TS, 3),
        [pl.BlockSpec((TS, D), lambda i, j: (i, 0)), pl.BlockSpec((pw, D), lambda i, j: (j, 0))],
        pl.BlockSpec((TS, pw), lambda i, j: (i, j)), _sds((S, WIN_PAD), F32), rider), rider)
    if comm:
        f2, = got
    qt = _rope_q(proj, cos, sin)
    k_rot, v_bf, kt, vt = _rope_kv(proj, cos, sin)
    kh, vh = _heads(k_rot, NKV), _heads(v_bf, NKV)
    bias = _bias_table()
    rider = comm.gather(f2, wout, part=(1, 2), at=11) if comm else None
    (ot, lse, mixed), got = _ridden(_attn_fwd(qt, kh, vt, bias, rider), rider)
    if comm:
        f2, wout = got
    w2 = _FfnW(f2, 0, f2, 2)
    wout = wout.reshape(2 * D, D)
    xbc, conv_y = _conv_fwd(proj, convw, convb)
    y, mixed, hprev = _ssd_fwd(xbc, proj, dtb, alog, dskip_l, sp["ssm_norm"], mixed)
    tail, o_specs, o_shapes = _tail_postres(TS, x1, sp["mix_post_norm"], 1.0, sp["ffn2_pre_norm"])
    h2, x2, n3 = _mm("out_proj", [mixed, wout], NN, (S // TS,),
                     [pl.BlockSpec((TS, 2 * D), lambda i: (i, 0)), pl.BlockSpec((2 * D, D), lambda i: (0, 0))],
                     o_specs, o_shapes, None, tail)

    fg2, fu2, act2 = _ffn_up("ffn2_up", n3, w2)
    dy, dh3, dp3, loss = _ffn_down(
        "ffn2_down", act2, w2, lambda rows: _tail_final(rows, x2, sp["ffn2_post_norm"], tgt, 0.5))

    dgate2, dup2 = _ffn_dact("ffn2_dact", dh3, w2, fg2, fu2)
    dws2 = [_ffn_dw("ffn2_dwg", dgate2, n3), _ffn_dw("ffn2_dwu", dup2, n3), _ffn_dw("ffn2_dwd", act2, dh3)]
    dx2, dh2, dg3, dp2 = _ffn_dn(
        "ffn2_dn", dgate2, dup2, w2,
        lambda rows: _tail_mid_bwd(rows, dy, x2, sp["ffn2_pre_norm"], h2, sp["mix_post_norm"], 1.0))

    dyn, dot_, delta = _out_proj_dx(dh2, wout, ot)
    dwout = _mm("out_proj_dw", [mixed, dh2], TN, (2,),
                [pl.BlockSpec((S, D), lambda m: (0, m)), pl.BlockSpec((S, D), lambda m: (0, 0))],
                pl.BlockSpec((D, D), lambda m: (m, 0)), _sds((2 * D, D), BF16))
    dwout = dwout.reshape(NSH, 2 * D // NSH, D)

    def riding(tag, names, ps, call, theirs=None, rows=None):
        rider = None
        if comm:
            rider = comm.rest_rider(tag) if names is None else comm.reduce_rider(tag, names, ps, theirs, rows)
        res, got = _ridden(call(rider), rider)
        if comm:
            comm.landed(tag, got)
        return res

    rider = _to_sibling_rider(dws2 + [dwout]) if comm else None
    (dxbc, dproj, ddt, dssm, dsc), theirs = _ridden(
        _ssd_bwd(dyn, y, xbc, proj, hprev, dtb, alog, dskip_l, sp["ssm_norm"], rider), rider)
    dproj, dcw8, dcb = _conv_bwd(dxbc, conv_y, proj, convw, dproj)
    dqt, dkh, dvh = riding("a", BIG[3:6] + ("w_out",), dws2 + [dwout], lambda rider: _attn_bwd(
        qt, kh, kt, vh, dot_, lse, delta, bias, rider), theirs)
    dproj = _rope_dq(dqt, cos, sin, dproj)
    dproj = _rope_dkv(dkh, dvh, cos, sin, dproj)
    dproj = lax.dynamic_update_slice(dproj, ddt, (0, COL_DT))
    dwint = _mm("in_proj_dw", [dproj, n2], TN, (3,),
                [pl.BlockSpec((S, pw), lambda j: (0, j)), pl.BlockSpec((S, D), lambda j: (0, 0))],
                pl.BlockSpec((pw, D), lambda j: (j, 0)), _sds((WIN_PAD, D), BF16))
    dwint = dwint[:WIN_COLS].reshape(NSH, WIN_SH, D)

    tail, o_specs, o_shapes = _tail_mid_bwd(TS, dx2, x1, sp["mix_pre_norm"], h1, sp["ffn1_post_norm"], 0.5)
    dx1, dh1, dg2, dp1 = riding("b", ("w_in",), [dwint], lambda rider: _mm(
        "in_proj_dx", [dproj, wint_pad], NN, (S // TS,),
        [pl.BlockSpec((TS, WIN_PAD), lambda i: (i, 0)), pl.BlockSpec((WIN_PAD, D), lambda i: (0, 0))],
        o_specs, o_shapes, rider, tail), rows=W_IN_FIRST)

    dwd1 = riding("b", None, None, lambda rider: _ffn_dw("ffn1_dwd", act1, dh1, rider))
    dgate1, dup1 = riding("d", BIG[2:3], [dwd1], lambda rider: _ffn_dact("ffn1_dact", dh1, w1, fg1, fu1, rider))
    dwg1, dwu1 = _ffn_dw("ffn1_dwg", dgate1, n1), _ffn_dw("ffn1_dwu", dup1, n1)
    grad_x, dg1 = riding("g", BIG[0:2], [dwg1, dwu1], lambda rider: _ffn_dn(
        "ffn1_dn", dgate1, dup1, w1, lambda rows: _tail_first_bwd(rows, dx1, x, sp["ffn1_pre_norm"]), rider))
    dws1 = [dwg1, dwu1, dwd1]

    small = jnp.concatenate([
        dg1[0], dp1[0], dg2[0], dssm[0], dp2[0], dg3[0], dp3[0], dcb[0],
        dsc[0, :16], dsc[1, :16], dsc[2, :16], dcw8[:CONV_K].reshape(-1), loss[0, :1]])
    small = jnp.pad(small, (0, SMALL_LEN - small.shape[0])).reshape(SMALL_ROWS, D)
    if comm is None:
        return grad_x, dws1 + dws2 + [dwint, dwout], small
    return (grad_x,) + comm.finish(small)


WEIGHTS = ("ffn1_pre_norm", "ffn1_w_gate", "ffn1_w_up", "ffn1_w_down", "ffn1_post_norm", "mix_pre_norm", "w_in",
           "conv_w", "conv_b", "dt_bias", "a_log", "d_skip", "ssm_norm", "w_out", "mix_post_norm", "ffn2_pre_norm",
           "ffn2_w_gate", "ffn2_w_up", "ffn2_w_down", "ffn2_post_norm")
BIG = ("ffn1_w_gate", "ffn1_w_up", "ffn1_w_down", "ffn2_w_gate", "ffn2_w_up", "ffn2_w_down", "w_in", "w_out")
TRANSPOSED = ("ffn1_w_gate", "ffn1_w_up", "ffn2_w_gate", "ffn2_w_up", "w_in")
SMALL_ORDER = SMALL_1K + ("conv_b",) + SMALL_16
CONVW_SH = CONV_C // NSH


def _shard2d(t, name):
    return t[0].T if name in TRANSPOSED else t[0]


def _unshard2d(t, name):
    return (t.T if name in TRANSPOSED else t)[None]


def _rows3d(t):
    return t.transpose(2, 0, 1)


def _pack_small(d, prefix, shard_of_convw):
    flat = jnp.concatenate([d[prefix + n][0] for n in SMALL_ORDER] + [shard_of_convw.reshape(-1)])
    return jnp.pad(flat, (0, SMALL_LEN - flat.shape[0])).reshape(SMALL_ROWS, D)


def _unpack_small(block, like):
    flat = block.reshape(-1)
    out, off = {}, 0
    for n in SMALL_ORDER:
        size = like[n].shape[1]
        out[n] = flat[off:off + size].reshape(1, size)
        off += size
    out["conv_w"] = flat[off:off + CONV_K * CONVW_SH].reshape(1, CONV_K, CONVW_SH)
    return out


def kernel(x, positions, ffn1_pre_norm, ffn1_w_gate, ffn1_w_up, ffn1_w_down, ffn1_post_norm, mix_pre_norm, w_in, conv_w, conv_b, dt_bias, a_log, d_skip, ssm_norm, w_out, mix_post_norm, ffn2_pre_norm, ffn2_w_gate, ffn2_w_up, ffn2_w_down, ffn2_post_norm, loss_target, m_ffn1_pre_norm, m_ffn1_w_gate, m_ffn1_w_up, m_ffn1_w_down, m_ffn1_post_norm, m_mix_pre_norm, m_w_in, m_conv_w, m_conv_b, m_dt_bias, m_a_log, m_d_skip, m_ssm_norm, m_w_out, m_mix_post_norm, m_ffn2_pre_norm, m_ffn2_w_gate, m_ffn2_w_up, m_ffn2_w_down, m_ffn2_post_norm, v_ffn1_pre_norm, v_ffn1_w_gate, v_ffn1_w_up, v_ffn1_w_down, v_ffn1_post_norm, v_mix_pre_norm, v_w_in, v_conv_w, v_conv_b, v_dt_bias, v_a_log, v_d_skip, v_ssm_norm, v_w_out, v_mix_post_norm, v_ffn2_pre_norm, v_ffn2_w_gate, v_ffn2_w_up, v_ffn2_w_down, v_ffn2_post_norm):
    given = dict(locals())
    xi, yi = lax.axis_index("x"), lax.axis_index("y")

    comm = _Comm()
    big = {p + n: _shard2d(given[p + n], n) for n in BIG for p in ("", "m_", "v_")}
    gu1 = _cast_stack("cast_ffn1_gate_up", comm.shard, [big[n] for n in BIG[0:2]], 176, D)

    sp = {n: given[n] for n in SMALL_ORDER}
    grad_x, big_grads, small = _local_step(
        x[0], positions[0], loss_target[0], sp, gu1, [big[BIG[2]]], [big[n] for n in BIG[3:6]], [big["w_in"]],
        [big["w_out"]], conv_w[0], comm)

    tot = small.reshape(-1)
    loss = tot[OFF_LOSS]
    small_grads, off = {}, 0
    for n in SMALL_ORDER:
        size = given[n].shape[1]
        small_grads[n] = tot[off:off + size].reshape(1, size)
        off += size
    dconvw = tot[OFF_CONVW:OFF_CONVW + CONV_K * CONV_C].reshape(CONV_K, NSH, CONVW_SH)
    dconvw = lax.dynamic_index_in_dim(dconvw, 2 * xi + yi, axis=1, keepdims=False)
    small_grads["conv_w"] = dconvw.reshape(1, CONV_K, CONVW_SH)

    upd = {}
    for names, tr in ((BIG[0:3], 176), (BIG[3:6], 176), (BIG[7:8], 256)):
        res = _adamw("adamw_" + names[0], [big[n] for n in names], [big_grads[n] for n in names],
                     [big["m_" + n] for n in names], [big["v_" + n] for n in names], tr, D)
        for n, r in zip(names, res):
            upd[n] = tuple(_unshard2d(t, n) for t in r)
    g_win = big_grads["w_in"].reshape(WIN_SH, 1, D)
    res, = _adamw("adamw_w_in", [_rows3d(w_in)], [g_win], [_rows3d(m_w_in)], [_rows3d(v_w_in)], WIN_SH // 4, D)
    upd["w_in"] = tuple(t.transpose(1, 2, 0) for t in res)
    (dl, m2, v2, _), = _adamw(
        "adamw_small", [_pack_small(given, "", conv_w[0])], [_pack_small(small_grads, "", dconvw)],
        [_pack_small(given, "m_", m_conv_w[0])], [_pack_small(given, "v_", v_conv_w[0])], SMALL_ROWS, D)
    dl, m2, v2 = (_unpack_small(t, given) for t in (dl, m2, v2))
    for n in SMALL_ORDER + ("conv_w",):
        upd[n] = (dl[n], m2[n], v2[n], small_grads[n])

    return (loss, grad_x[None], *[upd[n][3] for n in WEIGHTS], *[upd[n][0] for n in WEIGHTS],
            *[upd[n][1] for n in WEIGHTS], *[upd[n][2] for n in WEIGHTS])
```

```python
import functools
import typing

import jax
import jax.numpy as jnp
from jax import lax
from jax.experimental import pallas as pl
from jax.experimental.pallas import tpu as pltpu

F32 = jnp.float32
BF16 = jnp.bfloat16

S = 2048
D = 1024
FF = 2816
NSH = 4
FS = FF // NSH
HALF = D // 2
HD = 64
NKV = 4
NQ_PER_KV = 4
KVW = NKV * HD
QCOLS = NQ_PER_KV * HD
CONV_C = 1536
CONV_K = 4
SSM_W = 1024
NST = 128
NCH = S // 128
WIN_COLS = 4112
WIN_SH = WIN_COLS // NSH
W_IN_FIRST = 768
WIN_PAD = 4224
COL_DT = 4096
EPS = 1e-6
NEG = -1e30

ADAM_LR = 0.001
ADAM_B1 = 0.9
ADAM_B2 = 0.999
ADAM_EPS = 1e-08
ADAM_WD = 0.01
ADAM_STEP = 10

VMEM_LIMIT = 56 * 1024 * 1024
TS = 512
TR = 256

NN = (((1,), (0,)), ((), ()))
NT = (((1,), (1,)), ((), ()))
TN = (((0,), (0,)), ((), ()))
MESH = pl.DeviceIdType.MESH


def _cparams(*sem):
    return pltpu.CompilerParams(dimension_semantics=sem, vmem_limit_bytes=VMEM_LIMIT)


def _dot(a, b, dims):
    return lax.dot_general(a.astype(BF16), b.astype(BF16), dims, preferred_element_type=F32)


def _bf16_pieces(v):
    hi = v.astype(BF16)
    rest = v - hi.astype(F32)
    mid = rest.astype(BF16)
    return hi, mid, (rest - mid.astype(F32)).astype(BF16)


def _dot_exact(a, b, ones="a"):
    if ones == "a":
        sel = a.astype(BF16)
        parts = [lax.dot_general(sel, p, NN, preferred_element_type=F32) for p in _bf16_pieces(b)]
    else:
        sel = b.astype(BF16)
        parts = [lax.dot_general(p, sel, NN, preferred_element_type=F32) for p in _bf16_pieces(a)]
    return (parts[2] + parts[1]) + parts[0]


def _sigmoid(v):
    return 1.0 / (1.0 + jnp.exp(-v))


class _Rider(typing.NamedTuple):
    operands: list
    out_shapes: list
    aliases: dict
    sems: list
    start: typing.Callable
    finish: typing.Callable
    between: typing.Callable = None
    at: int = None


def _call(body, name, grid, in_specs, out_specs, out_shape, operands, scratch=(), sem=(), rider=None, prefetch=0):
    multi = isinstance(out_shape, (list, tuple))

    def launch(kernel, in_specs, out_specs, out_shape, scratch, aliases, sem, args):
        if prefetch:
            how = dict(grid_spec=pltpu.PrefetchScalarGridSpec(
                num_scalar_prefetch=prefetch, grid=grid, in_specs=in_specs, out_specs=out_specs,
                scratch_shapes=scratch))
        else:
            how = dict(grid=grid, in_specs=in_specs, out_specs=out_specs, scratch_shapes=scratch)
        return pl.pallas_call(kernel, name=name, out_shape=out_shape, input_output_aliases=aliases,
                              compiler_params=_cparams(*sem), **how)(*args)

    if rider is None:
        return launch(body, in_specs, out_specs, out_shape, list(scratch), {}, sem, operands)
    outs = list(out_shape) if multi else [out_shape]
    ospecs = list(out_specs) if multi else [out_specs]
    n_in, n_out, n_scr = len(operands) - prefetch, len(outs), len(scratch)
    ri, ro = len(rider.operands), len(rider.out_shapes)

    def wrapped(*refs):
        scalars, refs = refs[:prefetch], refs[prefetch:]
        o0 = n_in + ri
        s0 = o0 + n_out + ro
        rin, rout, rsem = refs[n_in:o0], refs[o0 + n_out:s0], refs[s0 + n_scr:]
        ids = [pl.program_id(a) for a in range(len(grid))]
        first = functools.reduce(jnp.logical_and, [i == 0 for i in ids])
        last = functools.reduce(jnp.logical_and, [i == g - 1 for i, g in zip(ids, grid)])

        @pl.when(first)
        def _():
            rider.start(rin, rout, rsem)

        if rider.between is not None:
            steps = functools.reduce(lambda a, b: a * b, grid)
            step = functools.reduce(lambda a, ig: a * ig[1] + ig[0], zip(ids, grid), 0)

            @pl.when(step == (steps // 3 if rider.at is None else rider.at))
            def _():
                rider.between(rin, rout, rsem)

        body(*scalars, *refs[:n_in], *refs[o0:o0 + n_out], *refs[s0:s0 + n_scr])

        @pl.when(last)
        def _():
            rider.finish(rin, rout, rsem)

    hbm = pl.BlockSpec(memory_space=pl.ANY)
    res = launch(wrapped, list(in_specs) + [hbm] * ri, ospecs + [hbm] * ro, outs + list(rider.out_shapes),
                 list(scratch) + list(rider.sems),
                 {prefetch + n_in + k: n_out + v for k, v in rider.aliases.items()},
                 ("arbitrary",) * len(grid), (*operands, *rider.operands))
    main = list(res[:n_out])
    return (main if multi else main[0]), list(res[n_out:])


class _Tail(typing.NamedTuple):
    fn: typing.Callable
    operands: list
    in_specs: list


def _mm(name, operands, dims, grid, in_specs, o_spec, out_shape, rider=None, tail=None):
    npairs = len(operands) // 2
    extra = [] if tail is None else list(tail.operands)
    nin = 2 * npairs + len(extra)

    def body(*refs):
        t = None
        for i in range(npairs):
            a, b = refs[2 * i], refs[2 * i + 1]
            parts = [(a[s], b[s]) for s in range(a.shape[0])] if len(a.shape) == 3 else [(a[...], b[...])]
            for pa, pb in parts:
                d = _dot(pa, pb, dims)
                t = d if t is None else t + d
        if tail is None:
            refs[nin][...] = t.astype(refs[nin].dtype)
        else:
            tail.fn(t, refs[2 * npairs:nin], refs[nin:])

    sem = ("parallel" if tail is None else "arbitrary",) * len(grid)
    specs = list(in_specs) + ([] if tail is None else list(tail.in_specs))
    return _call(body, name, grid, specs, o_spec, out_shape, list(operands) + extra, (), sem, rider)


class _FfnW(typing.NamedTuple):
    gu: jax.Array
    g0: int
    dn: jax.Array
    d0: int


def _ffn_up(name, n, w, rider=None):
    def body(n_ref, wg_ref, wu_ref, fg_ref, fu_ref, a_ref):
        nb = n_ref[...]
        g = _dot(nb, wg_ref[...], NT)
        u = _dot(nb, wu_ref[...], NT)
        sg = _sigmoid(g)
        silu = g * sg
        fg_ref[...] = (u * (sg * (1.0 + g * (1.0 - sg)))).astype(BF16)
        fu_ref[...] = silu.astype(BF16)
        a_ref[...] = (silu * u).astype(BF16)

    out = jax.ShapeDtypeStruct((NSH, S, FS), BF16)
    ospec = pl.BlockSpec((None, TS, FS), lambda s, i: (s, i, 0))
    return _call(
        body, name, (NSH, S // TS),
        [pl.BlockSpec((TS, D), lambda s, i: (i, 0)),
         pl.BlockSpec((None, None, FS, D), lambda s, i: (s, w.g0, 0, 0)),
         pl.BlockSpec((None, None, FS, D), lambda s, i: (s, w.g0 + 1, 0, 0))],
        [ospec, ospec, ospec], [out, out, out], (n, w.gu, w.gu), sem=("parallel", "parallel"), rider=rider)


def _ffn_dact(name, dh, w, fgate, fup, rider=None):
    def body(dh_ref, wd_ref, fg_ref, fu_ref, dg_ref, du_ref):
        da = _dot(dh_ref[...], wd_ref[...], NT)
        dg_ref[...] = (da * fg_ref[...].astype(F32)).astype(BF16)
        du_ref[...] = (da * fu_ref[...].astype(F32)).astype(BF16)

    out = jax.ShapeDtypeStruct((NSH, S, FS), BF16)
    aspec = pl.BlockSpec((None, TS, FS), lambda s, i: (s, i, 0))
    return _call(
        body, name, (NSH, S // TS),
        [pl.BlockSpec((TS, D), lambda s, i: (i, 0)),
         pl.BlockSpec((None, None, FS, D), lambda s, i: (s, w.d0, 0, 0)), aspec, aspec],
        [aspec, aspec], [out, out], (dh, w.dn, fgate, fup), sem=("parallel", "parallel"), rider=rider)


def _rstd(v):
    return lax.rsqrt(jnp.mean(v * v, axis=-1, keepdims=True) + EPS)


def _row_spec():
    return pl.BlockSpec((TR, D), lambda i: (i, 0))


def _vec_spec():
    return pl.BlockSpec((1, D), lambda i: (0, 0))


def _acc_rows(ref, v):
    @pl.when(pl.program_id(0) == 0)
    def _():
        ref[...] = jnp.zeros_like(ref)
    ref[...] += jnp.sum(v, axis=0, keepdims=True)


def _prenorm(name, x, g):
    def body(x_ref, g_ref, n_ref):
        xv = x_ref[...]
        n_ref[...] = (xv * _rstd(xv) * g_ref[...]).astype(BF16)

    return pl.pallas_call(
        body, name=name, grid=(S // TR,), in_specs=[_row_spec(), _vec_spec()], out_specs=_row_spec(),
        out_shape=jax.ShapeDtypeStruct((S, D), BF16), compiler_params=_cparams("parallel"),
    )(x, g)


ENTRY_STEPS = 4


def _prenorm_casts(name, x, g, slot, groups, rider):
    def body(s_ref, x_ref, g_ref, *refs):
        ins, outs = refs[:len(refs) - len(groups) - 2], refs[len(refs) - len(groups) - 2:]
        xv = x_ref[...]
        outs[0][...] = (xv * _rstd(xv) * g_ref[...]).astype(BF16)
        at = 0
        for (arrs, _), out in zip(groups, outs[1:]):
            for k in range(len(arrs)):
                out[k] = ins[at + k][...].astype(BF16)
            at += len(arrs)

        @pl.when(pl.program_id(0) == 0)
        def _():
            outs[-1][...] = _bias_table()

    rows = pl.BlockSpec((S // ENTRY_STEPS, D), lambda i, sr: (i, 0))
    in_specs, out_specs, out_shapes = [rows, pl.BlockSpec((1, D), lambda i, sr: (0, 0))], [rows], [_rows_bf16()]
    for arrs, axis in groups:
        r, c = arrs[0].shape
        if axis == 0:
            blk, at, at_out = (r // ENTRY_STEPS, c), (lambda i, sr: (i, 0)), (lambda i, sr: (sr[0], 0, i, 0))
        else:
            blk, at, at_out = (r, c // ENTRY_STEPS), (lambda i, sr: (0, i)), (lambda i, sr: (sr[0], 0, 0, i))
        in_specs += [pl.BlockSpec(blk, at)] * len(arrs)
        out_specs.append(pl.BlockSpec((None, len(arrs)) + blk, at_out))
        out_shapes.append(jax.ShapeDtypeStruct((NSH, len(arrs), r, c), BF16))
    out_specs.append(pl.BlockSpec((NBIAS, 128, QROWS), lambda i, sr: (0, 0, 0)))
    out_shapes.append(jax.ShapeDtypeStruct((NBIAS, 128, QROWS), F32))
    return _call(body, name, (ENTRY_STEPS,), in_specs, out_specs, out_shapes,
                 [slot, x, g] + [a for arrs, _ in groups for a in arrs], rider=rider, prefetch=1)


def _rows_spec(rows):
    return pl.BlockSpec((rows, D), lambda i: (i, 0))


def _rows_f32():
    return jax.ShapeDtypeStruct((S, D), F32)


def _rows_bf16():
    return jax.ShapeDtypeStruct((S, D), BF16)


def _vec_f32():
    return jax.ShapeDtypeStruct((1, D), F32)


def _tail_postres(rows, x, p, alpha, gnext):
    def fn(h, ins, outs):
        x_ref, p_ref, g_ref = ins
        h_ref, xo_ref, n_ref = outs
        h_ref[...] = h
        xo = x_ref[...] + alpha * (h * _rstd(h) * p_ref[...])
        xo_ref[...] = xo
        n_ref[...] = (xo * _rstd(xo) * g_ref[...]).astype(BF16)

    rs = _rows_spec(rows)
    return (_Tail(fn, [x, p, gnext], [rs, _vec_spec(), _vec_spec()]), [rs, rs, rs],
            [_rows_f32(), _rows_f32(), _rows_bf16()])


def _tail_final(rows, x, p, tgt, alpha):
    def fn(h, ins, outs):
        x_ref, p_ref, t_ref = ins
        dy_ref, dh_ref, dp_ref, loss_ref = outs
        r = _rstd(h)
        hn = h * r
        pv = p_ref[...]
        e = x_ref[...] + alpha * (hn * pv) - t_ref[...]
        dy = e * (1.0 / D)
        dy_ref[...] = dy
        du = alpha * dy * pv
        dh_ref[...] = (r * (du - hn * jnp.mean(du * hn, axis=-1, keepdims=True))).astype(BF16)
        _acc_rows(dp_ref, alpha * dy * hn)
        part = 0.5 * jnp.sum(jnp.mean(e * e, axis=-1, keepdims=True), axis=0, keepdims=True)
        _acc_rows(loss_ref, jnp.broadcast_to(part, (1, 128)))

    rs = _rows_spec(rows)
    return (_Tail(fn, [x, p, tgt], [rs, _vec_spec(), rs]),
            [rs, rs, _vec_spec(), pl.BlockSpec((1, 128), lambda i: (0, 0))],
            [_rows_f32(), _rows_bf16(), _vec_f32(), jax.ShapeDtypeStruct((1, 128), F32)])


def _norm_bwd(dn, xv, g_ref, dg_ref):
    r = _rstd(xv)
    xn = xv * r
    dng = dn * g_ref[...]
    _acc_rows(dg_ref, dn * xn)
    return r * (dng - xn * jnp.mean(dng * xn, axis=-1, keepdims=True))


def _tail_mid_bwd(rows, dres, x, g, h, p, alpha):
    def fn(dn, ins, outs):
        dr_ref, x_ref, g_ref, h_ref, p_ref = ins
        dx_ref, dh_ref, dg_ref, dp_ref = outs
        dx = dr_ref[...] + _norm_bwd(dn, x_ref[...], g_ref, dg_ref)
        dx_ref[...] = dx
        hv = h_ref[...]
        r = _rstd(hv)
        hn = hv * r
        du = alpha * dx * p_ref[...]
        dh_ref[...] = (r * (du - hn * jnp.mean(du * hn, axis=-1, keepdims=True))).astype(BF16)
        _acc_rows(dp_ref, alpha * dx * hn)

    rs = _rows_spec(rows)
    return (_Tail(fn, [dres, x, g, h, p], [rs, rs, _vec_spec(), rs, _vec_spec()]),
            [rs, rs, _vec_spec(), _vec_spec()], [_rows_f32(), _rows_bf16(), _vec_f32(), _vec_f32()])


def _tail_first_bwd(rows, dres, x, g):
    def fn(dn, ins, outs):
        dr_ref, x_ref, g_ref = ins
        dx_ref, dg_ref = outs
        dx_ref[...] = dr_ref[...] + _norm_bwd(dn, x_ref[...], g_ref, dg_ref)

    rs = _rows_spec(rows)
    return (_Tail(fn, [dres, x, g], [rs, rs, _vec_spec()]), [rs, _vec_spec()], [_rows_f32(), _vec_f32()])


def _rotate(t, c128, s128, sign, scale):
    width = t.shape[1]
    c = jnp.tile(c128, (1, width // 128))
    sn = jnp.tile(s128, (1, width // 128))
    lane = lax.broadcasted_iota(jnp.int32, t.shape, 1) & (HD - 1)
    rot = jnp.where(lane < HD // 2, -pltpu.roll(t, width - HD // 2, 1), pltpu.roll(t, HD // 2, 1))
    return (t * c + sign * (rot * sn)) * scale


def _rows_to_blocks(y):
    out = []
    for j in range(NKV):
        yt = y[:, QCOLS * j:QCOLS * (j + 1)].T
        out.append(jnp.concatenate([yt[HD * g:HD * (g + 1)] for g in range(NQ_PER_KV)], axis=1))
    return out


def _blocks_to_rows(blocks):
    cols = []
    for b in blocks:
        stacked = jnp.concatenate([b[:, 128 * g:128 * (g + 1)] for g in range(NQ_PER_KV)], axis=0)
        cols.append(stacked.T)
    return jnp.concatenate(cols, axis=1)


def _rope_q(proj, cos, sin):
    def body(t_ref, c_ref, s_ref, o_ref):
        y = _rotate(t_ref[...], c_ref[...], s_ref[...], 1.0, HD ** -0.5)
        for j, blk in enumerate(_rows_to_blocks(y)):
            o_ref[j] = blk.astype(BF16)

    return pl.pallas_call(
        body, name="rope_q", grid=(NCH,),
        in_specs=[pl.BlockSpec((128, D), lambda i: (i, 0)),
                  pl.BlockSpec((128, 128), lambda i: (i, 0)), pl.BlockSpec((128, 128), lambda i: (i, 0))],
        out_specs=pl.BlockSpec((NKV, None, HD, QROWS), lambda i: (0, i, 0, 0)),
        out_shape=jax.ShapeDtypeStruct((NKV, NCH, HD, QROWS), BF16), compiler_params=_cparams("parallel"),
    )(proj, cos, sin)


def _rope_dq(dqt, cos, sin, dproj):
    def body(t_ref, c_ref, s_ref, buf_ref, o_ref):
        t = _blocks_to_rows([t_ref[j] for j in range(NKV)])
        o_ref[...] = _rotate(t, c_ref[...], s_ref[...], -1.0, HD ** -0.5).astype(BF16)

    return pl.pallas_call(
        body, name="rope_dq", grid=(NCH,),
        in_specs=[pl.BlockSpec((NKV, None, HD, QROWS), lambda i: (0, i, 0, 0)),
                  pl.BlockSpec((128, 128), lambda i: (i, 0)), pl.BlockSpec((128, 128), lambda i: (i, 0)),
                  pl.BlockSpec(memory_space=pl.ANY)],
        out_specs=pl.BlockSpec((128, D), lambda i: (i, 0)),
        out_shape=jax.ShapeDtypeStruct(dproj.shape, BF16), input_output_aliases={3: 0},
        compiler_params=_cparams("parallel"),
    )(dqt, cos, sin, dproj)


def _rope_dkv(dkt, dvt, cos, sin, dproj):
    def body(k_ref, v_ref, c_ref, s_ref, buf_ref, o_ref):
        dk = jnp.concatenate([k_ref[j] for j in range(NKV)], axis=0).T
        dv = jnp.concatenate([v_ref[j] for j in range(NKV)], axis=0).T
        dk = _rotate(dk, c_ref[...], s_ref[...], -1.0, 1.0)
        o_ref[...] = jnp.concatenate([dk, dv], axis=1).astype(BF16)

    tspec = pl.BlockSpec((NKV, HD, 128), lambda i: (0, 0, i))
    return pl.pallas_call(
        body, name="rope_dkv", grid=(NCH,),
        in_specs=[tspec, tspec, pl.BlockSpec((128, 128), lambda i: (i, 0)), pl.BlockSpec((128, 128), lambda i: (i, 0)),
                  pl.BlockSpec(memory_space=pl.ANY)],
        out_specs=pl.BlockSpec((128, 2 * KVW), lambda i: (i, D // (2 * KVW))),
        out_shape=jax.ShapeDtypeStruct(dproj.shape, BF16), input_output_aliases={4: 0},
        compiler_params=_cparams("parallel"),
    )(dkt, dvt, cos, sin, dproj)


def _rope_kv(proj, cos, sin):
    def body(t_ref, c_ref, s_ref, k_ref, v_ref, kt_ref, vt_ref):
        t = t_ref[...]
        k = _rotate(t[:, :KVW], c_ref[...], s_ref[...], 1.0, 1.0).astype(BF16)
        v = t[:, KVW:].astype(BF16)
        k_ref[...] = k
        v_ref[...] = v
        kt, vt = k.astype(F32).T, v.astype(F32).T
        for j in range(NKV):
            kt_ref[j] = kt[HD * j:HD * (j + 1)].astype(BF16)
            vt_ref[j] = vt[HD * j:HD * (j + 1)].astype(BF16)

    rows = pl.BlockSpec((128, KVW), lambda i: (i, 0))
    tspec = pl.BlockSpec((NKV, HD, 128), lambda i: (0, 0, i))
    return pl.pallas_call(
        body, name="rope_kv", grid=(NCH,),
        in_specs=[pl.BlockSpec((128, 2 * KVW), lambda i: (i, D // (2 * KVW))),
                  pl.BlockSpec((128, 128), lambda i: (i, 0)), pl.BlockSpec((128, 128), lambda i: (i, 0))],
        out_specs=[rows, rows, tspec, tspec],
        out_shape=[jax.ShapeDtypeStruct((S, KVW), BF16)] * 2 + [jax.ShapeDtypeStruct((NKV, HD, S), BF16)] * 2,
        compiler_params=_cparams("parallel"),
    )(proj, cos, sin)


QROWS = NQ_PER_KV * 128


NBIAS = NCH + 1
KV_PER_STEP = 4


def _bias_table():
    db = lax.broadcasted_iota(jnp.int32, (NBIAS, 128, QROWS), 0) - 1
    ki = lax.broadcasted_iota(jnp.int32, (NBIAS, 128, QROWS), 1)
    qi = lax.broadcasted_iota(jnp.int32, (NBIAS, 128, QROWS), 2) & 127
    d = db * 128 + qi - ki
    cnt = ((d <= 128).astype(F32) + (((d & 3) == 0) & (d <= 512)).astype(F32) + ((d & 15) == 0).astype(F32))
    return jnp.where((d >= 0) & (cnt > 0.0), jnp.log(jnp.maximum(cnt, 1.0)), NEG)


def _attn_fwd(qt, kh, vt, bias, rider=None):
    def body(q_ref, k_ref, v_ref, b_ref, o_ref, lse_ref, rows_ref, m_ref, l_ref, acc_ref):
        qb = pl.program_id(1)
        m_ref[...] = jnp.full_like(m_ref, NEG)
        l_ref[...] = jnp.zeros_like(l_ref)
        acc_ref[...] = jnp.zeros_like(acc_ref)

        def keys(off, size, bias_):
            for h in range(KV_PER_STEP):
                m = m_ref[h]
                s = _dot(k_ref[h, pl.ds(off, size), :], q_ref[h], NN) + bias_
                m_new = jnp.maximum(m, jnp.max(s, axis=0, keepdims=True))
                p = jnp.exp(s - m_new)
                a = jnp.exp(m - m_new)
                m_ref[h] = m_new
                l_ref[h] = a * l_ref[h] + jnp.sum(p, axis=0, keepdims=True)
                acc_ref[h] = a * acc_ref[h] + _dot(v_ref[h, :, pl.ds(off, size)], p, NN)

        def blocks(first, count):
            bias_ = jnp.concatenate([b_ref[qb - first - j + 1] for j in range(count)], axis=0)
            keys(pl.multiple_of(first * 128, 128), 128 * count, bias_)

        nkb = qb + 1
        @pl.loop(0, nkb // 4)
        def _(i):
            blocks(4 * i, 4)

        @pl.when(nkb % 4 >= 2)
        def _():
            blocks(nkb // 4 * 4, 2)

        @pl.when(nkb % 2 == 1)
        def _():
            blocks(qb, 1)

        outs = []
        for h in range(KV_PER_STEP):
            outs.append(acc_ref[h] / l_ref[h])
            o_ref[h] = outs[h]
            lse_ref[h] = m_ref[h] + jnp.log(l_ref[h])
        rows_ref[...] = _blocks_to_rows(outs).astype(BF16)

    kvs = KV_PER_STEP
    qspec = pl.BlockSpec((kvs, None, HD, QROWS), lambda j, i: (j, i, 0, 0))
    return _call(
        body, "attn_fwd", (NKV // kvs, NCH),
        [qspec, pl.BlockSpec((kvs, S, HD), lambda j, i: (j, 0, 0)),
         pl.BlockSpec((kvs, HD, S), lambda j, i: (j, 0, 0)),
         pl.BlockSpec((NBIAS, 128, QROWS), lambda j, i: (0, 0, 0))],
        [qspec, pl.BlockSpec((kvs, None, 1, QROWS), lambda j, i: (j, i, 0, 0)),
         pl.BlockSpec((128, QCOLS * kvs), lambda j, i: (i, j))],
        [jax.ShapeDtypeStruct((NKV, NCH, HD, QROWS), F32), jax.ShapeDtypeStruct((NKV, NCH, 1, QROWS), F32),
         jax.ShapeDtypeStruct((S, 2 * D), BF16)],
        (qt, kh, vt, bias),
        [pltpu.VMEM((kvs, 1, QROWS), F32), pltpu.VMEM((kvs, 1, QROWS), F32), pltpu.VMEM((kvs, HD, QROWS), F32)],
        ("parallel", "parallel"), rider)


def _attn_bwd(qt, kh, kt, vh, dot_, lse, delta, bias, rider=None):
    def body(qt_ref, k_ref, kt_ref, v_ref, dot_ref, lse_ref, dl_ref, b_ref, dq_ref, dk_ref, dv_ref):
        kp = pl.program_id(1)

        @pl.when(kp == 0)
        def _():
            dq_ref[...] = jnp.zeros_like(dq_ref)

        dk_ref[...] = jnp.zeros_like(dk_ref)
        dv_ref[...] = jnp.zeros_like(dv_ref)

        @pl.loop(2 * kp, NCH // 2)
        def _(j):
            for h in range(KV_PER_STEP):
                k, kt_, v = k_ref[h], kt_ref[h], v_ref[h]
                for qb in (2 * j, 2 * j + 1):
                    bias2 = jnp.concatenate([b_ref[jnp.maximum(qb - 4 * kp - t + 1, 0)] for t in range(4)], axis=0)
                    st = _dot(k, qt_ref[h, qb], NN) + bias2
                    pt = jnp.exp(st - lse_ref[h, qb])
                    dst = pt * (_dot(v, dot_ref[h, qb], NN) - dl_ref[h, qb])
                    dq_ref[h, qb] += _dot(kt_, dst, NN)
                    dk_ref[h] += _dot(qt_ref[h, qb], dst, NT)
                    dv_ref[h] += _dot(dot_ref[h, qb], pt, NT)

    kvs = KV_PER_STEP
    tspec = pl.BlockSpec((kvs, NCH, HD, QROWS), lambda j, i: (j, 0, 0, 0))
    kspec = pl.BlockSpec((kvs, 512, HD), lambda j, i: (j, i, 0))
    ktspec = pl.BlockSpec((kvs, HD, 512), lambda j, i: (j, 0, i))
    sspec = pl.BlockSpec((kvs, NCH, 1, QROWS), lambda j, i: (j, 0, 0, 0))
    return _call(
        body, "attn_bwd", (NKV // kvs, NCH // 4),
        [tspec, kspec, ktspec, kspec, tspec, sspec, sspec,
         pl.BlockSpec((NBIAS, 128, QROWS), lambda j, i: (0, 0, 0))],
        [tspec, ktspec, ktspec],
        [jax.ShapeDtypeStruct((NKV, NCH, HD, QROWS), F32),
         jax.ShapeDtypeStruct((NKV, HD, S), F32), jax.ShapeDtypeStruct((NKV, HD, S), F32)],
        (qt, kh, kt, vh, dot_, lse, delta, bias), sem=("parallel", "arbitrary"), rider=rider)


CONV_BLK = 256
CONV_COL0 = 1536 // CONV_BLK


CONV_ROWS = 128


def _conv_fwd(proj, convw, convb):
    trips = S // CONV_ROWS

    def body(u_ref, w_ref, b_ref, o_ref, y_ref):
        @pl.loop(0, trips)
        def _(c):
            t0 = pl.multiple_of(c * CONV_ROWS, CONV_ROWS)
            before = pl.multiple_of(jnp.maximum(t0 - 8, 0), 8)
            ext = jnp.concatenate([jnp.where(c == 0, 0.0, u_ref[pl.ds(before, 8), :]),
                                   u_ref[pl.ds(t0, CONV_ROWS), :]], axis=0)
            y = b_ref[...] + w_ref[CONV_K - 1:CONV_K, :] * ext[8:]
            for j in range(1, CONV_K):
                y = y + w_ref[CONV_K - 1 - j:CONV_K - j, :] * pltpu.roll(ext, j, 0)[8:]
            y_ref[pl.ds(t0, CONV_ROWS), :] = y
            o_ref[pl.ds(t0, CONV_ROWS), :] = y * _sigmoid(y)

    out = pl.BlockSpec((S, CONV_BLK), lambda i: (0, i))
    return pl.pallas_call(
        body, name="conv_fwd", grid=(CONV_C // CONV_BLK,),
        in_specs=[pl.BlockSpec((S, CONV_BLK), lambda i: (0, CONV_COL0 + i)),
                  pl.BlockSpec((CONV_K, CONV_BLK), lambda i: (0, i)),
                  pl.BlockSpec((1, CONV_BLK), lambda i: (0, i))],
        out_specs=[out, out], out_shape=[jax.ShapeDtypeStruct((S, CONV_C), F32)] * 2,
        compiler_params=_cparams("parallel"),
    )(proj, convw, convb)


def _conv_bwd(dact, ypre, proj, convw, dproj):
    trips = S // CONV_ROWS

    def body(da_ref, y_ref, u_ref, w_ref, buf_ref, du_ref, dw_ref, db_ref):
        dw_ref[...] = jnp.zeros_like(dw_ref)
        db_ref[...] = jnp.zeros_like(db_ref)
        r8 = lax.broadcasted_iota(jnp.int32, (8, CONV_BLK), 0)

        def dy_of(rows):
            y = y_ref[rows, :]
            sg = _sigmoid(y)
            return da_ref[rows, :] * (sg * (1.0 + y * (1.0 - sg)))

        @pl.loop(0, trips)
        def _(c):
            t0 = pl.multiple_of(c * CONV_ROWS, CONV_ROWS)
            after = pl.multiple_of(jnp.minimum(t0 + CONV_ROWS, S - 8), 8)
            ext = jnp.concatenate([dy_of(pl.ds(t0, CONV_ROWS)),
                                   jnp.where(c == trips - 1, 0.0, dy_of(pl.ds(after, 8)))], axis=0)
            u = u_ref[pl.ds(t0, CONV_ROWS), :]
            du, dw = None, jnp.zeros((8, CONV_BLK), F32)
            for j in range(CONV_K):
                dyj = (ext if j == 0 else pltpu.roll(ext, CONV_ROWS + 8 - j, 0))[:CONV_ROWS]
                term = w_ref[CONV_K - 1 - j:CONV_K - j, :] * dyj
                du = term if du is None else du + term
                dw = dw + jnp.where(r8 == CONV_K - 1 - j, jnp.sum(dyj * u, axis=0, keepdims=True), 0.0)
            du_ref[pl.ds(t0, CONV_ROWS), :] = du.astype(BF16)
            dw_ref[...] += dw
            db_ref[...] += jnp.sum(ext[:CONV_ROWS], axis=0, keepdims=True)

    return pl.pallas_call(
        body, name="conv_bwd", grid=(CONV_C // CONV_BLK,),
        in_specs=[pl.BlockSpec((S, CONV_BLK), lambda i: (0, i)), pl.BlockSpec((S, CONV_BLK), lambda i: (0, i)),
                  pl.BlockSpec((S, CONV_BLK), lambda i: (0, CONV_COL0 + i)),
                  pl.BlockSpec((CONV_K, CONV_BLK), lambda i: (0, i)), pl.BlockSpec(memory_space=pl.ANY)],
        out_specs=[pl.BlockSpec((S, CONV_BLK), lambda i: (0, CONV_COL0 + i)),
                   pl.BlockSpec((8, CONV_BLK), lambda i: (0, i)), pl.BlockSpec((1, CONV_BLK), lambda i: (0, i))],
        out_shape=[jax.ShapeDtypeStruct(dproj.shape, BF16), jax.ShapeDtypeStruct((8, CONV_C), F32),
                   jax.ShapeDtypeStruct((1, CONV_C), F32)],
        input_output_aliases={4: 0}, compiler_params=_cparams("parallel"),
    )(dact, ypre, proj, convw, dproj)


NPAIR = 8


def _ssd_scalars(dtr_ref, dtb_ref, alog_ref):
    z = dtr_ref[...] + dtb_ref[...]
    dt = jnp.maximum(z, 0.0) + jnp.log(1.0 + jnp.exp(-jnp.abs(z)))
    a = -jnp.exp(alog_ref[...])
    r = lax.broadcasted_iota(jnp.int32, (128, 128), 0)
    c = lax.broadcasted_iota(jnp.int32, (128, 128), 1)
    tri = (r >= c).astype(F32)
    cs = _dot_exact(tri, dt * a)
    return z, dt, a, cs, r, c


def _by_lane(cs, dt):
    head = lax.broadcasted_iota(jnp.int32, (128, SSM_W), 0)
    lane = lax.broadcasted_iota(jnp.int32, (128, SSM_W), 1)
    sel = (head == lane // HD).astype(F32)
    cs_l = _dot_exact(cs, sel, "b")
    last_l = cs_l[127:128, :]
    return sel, jnp.exp(cs_l), jnp.exp(last_l - cs_l), _dot_exact(dt, sel, "b")


def _pair_terms(cs, h1, h2):
    return (cs[:, h1:h1 + 1], cs[:, h2:h2 + 1],
            jnp.exp(cs[127:128, h1:h1 + 1]), jnp.exp(cs[127:128, h2:h2 + 1]))


def _gate_norm(y, zv, w):
    yg = y * (zv * _sigmoid(zv))
    outs, rs = [], []
    for g in range(2):
        blk = yg[:, 512 * g:512 * (g + 1)]
        r = lax.rsqrt(jnp.mean(blk * blk, axis=-1, keepdims=True) + EPS)
        outs.append(blk * r)
        rs.append(r)
    return jnp.concatenate(outs, axis=1), rs, yg


def _ssd_fwd(xbc, proj, dtb, alog, dskip_l, ssmw, mixed):
    def body(x_ref, b_ref, c_ref, dtr_ref, z_ref, dtb_ref, alog_ref, dsk_ref, w_ref, buf_ref,
             y_ref, yn_ref, hp_ref, h_ref):
        @pl.when(pl.program_id(0) == 0)
        def _():
            h_ref[...] = jnp.zeros_like(h_ref)

        _, dt, _, cs, r, c = _ssd_scalars(dtr_ref, dtb_ref, alog_ref)
        cst = cs.T
        causal = r >= c
        lo = c < HD
        _, e_all, dte_all, dt_all = _by_lane(cs, dt)
        hp_ref[...] = h_ref[...]
        for g in range(2):
            bg = b_ref[:, 128 * g:128 * (g + 1)]
            cg = c_ref[:, 128 * g:128 * (g + 1)]
            cb = _dot(cg, bg, NT)
            for j in range(4):
                pj = 4 * g + j
                h1, h2 = 2 * pj, 2 * pj + 1
                sl = slice(128 * pj, 128 * (pj + 1))
                xp = x_ref[:, sl]
                c1, c2, cd1, cd2 = _pair_terms(cs, h1, h2)
                e_l, dte_l = e_all[:, sl], dte_all[:, sl]
                xdt = xp * dt_all[:, sl]
                m1 = cb * jnp.exp(jnp.where(causal, c1 - cst[h1:h1 + 1, :], NEG))
                m2 = cb * jnp.exp(jnp.where(causal, c2 - cst[h2:h2 + 1, :], NEG))
                yd = jnp.where(lo, _dot(m1, xdt, NN), _dot(m2, xdt, NN))
                hp = h_ref[pj]
                yo = _dot(cg, hp, NT) * e_l
                st = _dot(xdt * dte_l, bg, TN)
                h_ref[pj] = hp * jnp.where(r < HD, cd1, cd2) + st
                y_ref[:, sl] = yd + yo + dsk_ref[:, sl] * xp
        yn, _, _ = _gate_norm(y_ref[...], z_ref[...], w_ref[...])
        yn_ref[...] = (yn * w_ref[...]).astype(BF16)

    return pl.pallas_call(
        body, name="ssd_fwd", grid=(NCH,),
        in_specs=[pl.BlockSpec((128, SSM_W), lambda i: (i, 0)),
                  pl.BlockSpec((128, 256), lambda i: (i, 4)), pl.BlockSpec((128, 256), lambda i: (i, 5)),
                  pl.BlockSpec((128, 128), lambda i: (i, COL_DT // 128)),
                  pl.BlockSpec((128, SSM_W), lambda i: (i, 3)),
                  pl.BlockSpec((1, 128), lambda i: (0, 0)), pl.BlockSpec((1, 128), lambda i: (0, 0)),
                  pl.BlockSpec((1, SSM_W), lambda i: (0, 0)), pl.BlockSpec((1, SSM_W), lambda i: (0, 0)),
                  pl.BlockSpec(memory_space=pl.ANY)],
        out_specs=[pl.BlockSpec((128, SSM_W), lambda i: (i, 0)), pl.BlockSpec((128, SSM_W), lambda i: (i, 1)),
                   pl.BlockSpec((None, NPAIR, 128, 128), lambda i: (i, 0, 0, 0))],
        out_shape=[jax.ShapeDtypeStruct((S, SSM_W), F32), jax.ShapeDtypeStruct(mixed.shape, BF16),
                   jax.ShapeDtypeStruct((NCH, NPAIR, 128, 128), F32)],
        scratch_shapes=[pltpu.VMEM((NPAIR, 128, 128), F32)],
        input_output_aliases={9: 1}, compiler_params=_cparams("arbitrary"),
    )(xbc, xbc, xbc, proj, proj, dtb, alog, dskip_l, ssmw, mixed)


def _ssd_bwd(dmixed, y, xbc, proj, hprev, dtb, alog, dskip_l, ssmw, rider=None):
    def body(dyn_ref, y_ref, x_ref, b_ref, c_ref, dtr_ref, z_ref, hp_ref, dtb_ref, alog_ref, dsk_ref, w_ref,
             dxbc_ref, dz_ref, ddt_ref, dw_ref, dsc_ref, g_ref):
        @pl.when(pl.program_id(0) == 0)
        def _():
            g_ref[...] = jnp.zeros_like(g_ref)
            dsc_ref[...] = jnp.zeros_like(dsc_ref)

        z, dt, a, cs, r, c = _ssd_scalars(dtr_ref, dtb_ref, alog_ref)
        cst = cs.T
        causal = r >= c
        lo = c < HD

        yv = y_ref[...]
        zv = z_ref[...]
        wv = w_ref[...]
        ygn, rs, yg = _gate_norm(yv, zv, wv)
        dyn = dyn_ref[...]
        _acc_rows(dw_ref, dyn * ygn)
        dynw = dyn * wv
        parts = []
        for g in range(2):
            sl = slice(512 * g, 512 * (g + 1))
            a_g, n_g = dynw[:, sl], ygn[:, sl]
            parts.append(rs[g] * (a_g - n_g * jnp.mean(a_g * n_g, axis=-1, keepdims=True)))
        dyg = jnp.concatenate(parts, axis=1)
        sz = _sigmoid(zv)
        dz_ref[...] = (dyg * yv * (sz * (1.0 + zv * (1.0 - sz)))).astype(BF16)
        dy_all = dyg * (zv * sz)

        dcs_cols = jnp.zeros((128, 128), F32)
        dcs_rows = jnp.zeros((128, 128), F32)
        sel, e_all, dte_all, dt_all = _by_lane(cs, dt)
        x_all, b_all, c_all, dsk_all = x_ref[...], b_ref[...], c_ref[...], dsk_ref[...]
        hp_all, g_all = hp_ref[...], g_ref[...]
        g_new, dx_parts, db_parts, dc_parts = [], [], [], []
        dyx_parts, ryo_parts, qx_parts, dxx_parts, gh_parts = [], [], [], [], []
        for g in range(2):
            bg = b_all[:, 128 * g:128 * (g + 1)]
            cg = c_all[:, 128 * g:128 * (g + 1)]
            cb = _dot(cg, bg, NT)
            dcb = jnp.zeros((128, 128), F32)
            db_acc = jnp.zeros((128, NST), F32)
            dc_acc = jnp.zeros((128, NST), F32)
            for j in range(4):
                pj = 4 * g + j
                h1, h2 = 2 * pj, 2 * pj + 1
                sl = slice(128 * pj, 128 * (pj + 1))
                xp = x_all[:, sl]
                dyp = dy_all[:, sl]
                c1, c2, cd1, cd2 = _pair_terms(cs, h1, h2)
                e_l, dte_l, dt_l = e_all[:, sl], dte_all[:, sl], dt_all[:, sl]
                xdt = xp * dt_l
                hp = hp_all[pj]
                gp = g_all[pj]
                dyx_parts.append(dyp * xp)
                dzs = dyp * e_l
                dc_acc = dc_acc + _dot(dzs, hp, NN)
                ryo_parts.append(dyp * (_dot(cg, hp, NT) * e_l))
                qm = _dot(bg, gp, NT)
                dxdt = qm * dte_l
                qx_parts.append(qm * xdt)
                db_acc = db_acc + _dot(xdt * dte_l, gp, NN)
                gh_parts.append(gp * hp)
                g_new.append(_dot(dzs, cg, TN) + jnp.where(r < HD, cd1, cd2) * gp)
                for hh, ch, msk in ((h1, c1, lo), (h2, c2, jnp.logical_not(lo))):
                    lm = jnp.exp(jnp.where(causal, ch - cst[hh:hh + 1, :], NEG))
                    mm = cb * lm
                    dm = jnp.where(causal, _dot(jnp.where(msk, dyp, 0.0), xdt, NT), 0.0)
                    w = dm * mm
                    dcs_cols = dcs_cols + jnp.where(c == hh, jnp.sum(w, axis=1, keepdims=True), 0.0)
                    dcs_rows = dcs_rows + jnp.where(r == hh, jnp.sum(w, axis=0, keepdims=True), 0.0)
                    dcb = dcb + dm * lm
                    dxdt = dxdt + jnp.where(msk, _dot(mm, dyp, TN), 0.0)
                dxx_parts.append(dxdt * xp)
                dx_parts.append(dsk_all[:, sl] * dyp + dxdt * dt_l)
            db_parts.append(db_acc + _dot(dcb, cg, TN))
            dc_parts.append(dc_acc + _dot(dcb, bg, NN))
        g_ref[...] = jnp.stack(g_new)
        dxbc_ref[...] = jnp.concatenate(dx_parts + db_parts + dc_parts, axis=1)

        selt = (lax.broadcasted_iota(jnp.int32, (SSM_W, 128), 0) // HD
                == lax.broadcasted_iota(jnp.int32, (SSM_W, 128), 1)).astype(F32)

        def by_head(parts):
            return _dot_exact(jnp.concatenate(parts, axis=1), selt, "b")

        ddt_x = by_head(dxx_parts)
        dd_row = jnp.sum(by_head(dyx_parts), axis=0, keepdims=True)
        t_all = by_head(qx_parts) * jnp.exp(cs[127:128, :] - cs)
        gh = jnp.sum(_dot_exact(sel, jnp.concatenate(gh_parts, axis=0)), axis=1, keepdims=True)
        gh_row = jnp.broadcast_to(gh, (128, 128)).T[0:1, :]
        at_end = jnp.sum(t_all, axis=0, keepdims=True) + gh_row * jnp.exp(cs[127:128, :])
        dcs = by_head(ryo_parts) - t_all + dcs_cols + jnp.where(r == 127, at_end, 0.0) - dcs_rows.T
        dad = _dot_exact((c >= r).astype(F32), dcs)
        ddt = dad * a + ddt_x
        ddtr = jnp.where(c < 16, ddt * _sigmoid(z), 0.0)
        ddt_ref[...] = ddtr.astype(BF16)
        r8 = lax.broadcasted_iota(jnp.int32, (8, 128), 0)
        dsc_ref[...] += (jnp.where(r8 == 0, jnp.sum(ddtr, axis=0, keepdims=True), 0.0)
                         + jnp.where(r8 == 1, jnp.sum(dad * dt, axis=0, keepdims=True) * a, 0.0)
                         + jnp.where(r8 == 2, dd_row, 0.0))

    rev = NCH - 1
    return _call(
        body, "ssd_bwd", (NCH,),
        [pl.BlockSpec((128, SSM_W), lambda i: (rev - i, 0)),
         pl.BlockSpec((128, SSM_W), lambda i: (rev - i, 0)),
         pl.BlockSpec((128, SSM_W), lambda i: (rev - i, 0)),
         pl.BlockSpec((128, 256), lambda i: (rev - i, 4)), pl.BlockSpec((128, 256), lambda i: (rev - i, 5)),
         pl.BlockSpec((128, 128), lambda i: (rev - i, COL_DT // 128)),
         pl.BlockSpec((128, SSM_W), lambda i: (rev - i, 3)),
         pl.BlockSpec((None, NPAIR, 128, 128), lambda i: (rev - i, 0, 0, 0)),
         pl.BlockSpec((1, 128), lambda i: (0, 0)), pl.BlockSpec((1, 128), lambda i: (0, 0)),
         pl.BlockSpec((1, SSM_W), lambda i: (0, 0)), pl.BlockSpec((1, SSM_W), lambda i: (0, 0))],
        [pl.BlockSpec((128, CONV_C), lambda i: (rev - i, 0)),
         pl.BlockSpec((128, SSM_W), lambda i: (rev - i, 3)),
         pl.BlockSpec((128, 128), lambda i: (rev - i, 0)),
         pl.BlockSpec((1, SSM_W), lambda i: (0, 0)), pl.BlockSpec((8, 128), lambda i: (0, 0))],
        [jax.ShapeDtypeStruct((S, CONV_C), F32), jax.ShapeDtypeStruct((S, WIN_PAD), BF16),
         jax.ShapeDtypeStruct((S, 128), BF16), jax.ShapeDtypeStruct((1, SSM_W), F32),
         jax.ShapeDtypeStruct((8, 128), F32)],
        (dmixed, y, xbc, xbc, xbc, proj, proj, hprev, dtb, alog, dskip_l, ssmw),
        [pltpu.VMEM((NPAIR, 128, 128), F32)], ("arbitrary",), rider)


def _cast_stack(name, slot, arrs, tr, tc):
    n = len(arrs)
    rows, cols = arrs[0].shape

    def body(s_ref, *refs):
        for i in range(n):
            refs[n][i] = refs[i][...].astype(BF16)

    return pl.pallas_call(
        body, name=name,
        grid_spec=pltpu.PrefetchScalarGridSpec(
            num_scalar_prefetch=1, grid=(rows // tr, cols // tc),
            in_specs=[pl.BlockSpec((tr, tc), lambda i, j, sr: (i, j))] * n,
            out_specs=pl.BlockSpec((None, n, tr, tc), lambda i, j, sr: (sr[0], 0, i, j))),
        out_shape=jax.ShapeDtypeStruct((NSH, n, rows, cols), BF16),
        compiler_params=_cparams("parallel", "parallel"),
    )(slot, *arrs)


def _pair_sum(name, c_idx, ps, th):
    n = len(ps)
    _, rows, _ = ps[0].shape

    def body(c_ref, *refs):
        mine, whole, out, theirs = refs[:n], refs[n:2 * n], refs[2 * n:3 * n], refs[3 * n:4 * n]
        send, recv = refs[4 * n], refs[4 * n + 1]
        s, i = pl.program_id(0), pl.program_id(1)
        x, y, c, _ = _place()

        def copies(slot):
            return [_rcopy(whole[k].at[slot, :, pl.ds((1 - c) * HALF, HALF)], theirs[k].at[slot],
                           send.at[slot * n + k], recv.at[slot * n + k], (x, y, 1 - c)) for k in range(n)]

        @pl.when((s == 0) & (i == 0))
        def _():
            for slot in range(NSH):
                for cp in copies(slot):
                    cp.start()

        @pl.when(i == 0)
        def _():
            for slot in range(NSH):
                @pl.when(s == slot)
                def _():
                    for cp in copies(slot):
                        cp.wait()

        rows_i = slice(None) if th == rows else pl.ds(pl.multiple_of(i * th, th), th)
        for k in range(n):
            out[k][...] = (mine[k][...].astype(F32) + theirs[k][s, rows_i, :].astype(F32)).astype(BF16)

    spec = pl.BlockSpec((None, th, HALF), lambda s, i, cr: (s, i, 0))
    return pl.pallas_call(
        body, name=name,
        grid_spec=pltpu.PrefetchScalarGridSpec(
            num_scalar_prefetch=1, grid=(NSH, rows // th),
            in_specs=[pl.BlockSpec((None, th, HALF), lambda s, i, cr: (s, i, cr[0]))] * n + _any_specs(n),
            out_specs=[spec] * n,
            scratch_shapes=[pltpu.VMEM((NSH, rows, HALF), BF16)] * n
            + [pltpu.SemaphoreType.DMA((NSH * n,)), pltpu.SemaphoreType.DMA((NSH * n,))]),
        out_shape=[jax.ShapeDtypeStruct((NSH, rows, HALF), BF16)] * n,
        compiler_params=_cparams("arbitrary", "arbitrary"),
    )(c_idx, *ps, *ps)


def _pair_add(name, c_idx, ps, theirs, th):
    n = len(ps)
    _, rows, _ = ps[0].shape

    def body(c_ref, *refs):
        for k in range(n):
            refs[2 * n + k][...] = (refs[k][...].astype(F32) + refs[n + k][...].astype(F32)).astype(BF16)

    spec = pl.BlockSpec((None, th, HALF), lambda s, i, cr: (s, i, 0))
    return pl.pallas_call(
        body, name=name,
        grid_spec=pltpu.PrefetchScalarGridSpec(
            num_scalar_prefetch=1, grid=(NSH, rows // th),
            in_specs=[pl.BlockSpec((None, th, HALF), lambda s, i, cr: (s, i, cr[0]))] * n + [spec] * n,
            out_specs=[spec] * n),
        out_shape=[jax.ShapeDtypeStruct((NSH, rows, HALF), BF16)] * n,
        compiler_params=_cparams("parallel", "parallel"),
    )(c_idx, *ps, *theirs)


def _chip_sum(name, place, cs, ts, th):
    n = len(ts)
    _, rows, _ = ts[0].shape

    def body(p_ref, *refs):
        for i in range(n):
            t = refs[n + i][...].astype(F32)
            refs[2 * n + i][...] = ((refs[i][...].astype(F32) + t[0]) + t[1]) + t[2]

    return pl.pallas_call(
        body, name=name,
        grid_spec=pltpu.PrefetchScalarGridSpec(
            num_scalar_prefetch=1, grid=(rows // th,),
            in_specs=[pl.BlockSpec((None, th, HALF), lambda i, pr: (pr[0], i, 0))] * n
            + [pl.BlockSpec((3, th, HALF), lambda i, pr: (0, i, 0))] * n,
            out_specs=[pl.BlockSpec((th, HALF), lambda i, pr: (i, pr[1]))] * n),
        out_shape=[jax.ShapeDtypeStruct((rows, D), F32)] * n, compiler_params=_cparams("parallel"),
    )(place, *cs, *ts)


def _adamw(name, ws, gs, ms, vs, tr, tc):
    n = len(ws)
    shape = ws[0].shape
    rows, cols, mid = shape[0], shape[-1], shape[1:-1]
    c1 = 1.0 / (1.0 - ADAM_B1 ** ADAM_STEP)
    c2 = 1.0 / (1.0 - ADAM_B2 ** ADAM_STEP)

    def body(*refs):
        for i in range(n):
            w, g, m, v = (refs[k * n + i][...] for k in range(4))
            m2 = ADAM_B1 * m + (1.0 - ADAM_B1) * g
            v2 = ADAM_B2 * v + (1.0 - ADAM_B2) * (g * g)
            refs[4 * n + 4 * i][...] = -ADAM_LR * ((m2 * c1) / (jnp.sqrt(v2 * c2) + ADAM_EPS) + ADAM_WD * w)
            refs[4 * n + 4 * i + 1][...] = m2
            refs[4 * n + 4 * i + 2][...] = v2
            refs[4 * n + 4 * i + 3][...] = g

    spec = pl.BlockSpec((tr,) + mid + (tc,), lambda i, j: (i,) + (0,) * len(mid) + (j,))
    outs = pl.pallas_call(
        body, name=name, grid=(rows // tr, cols // tc), in_specs=[spec] * (4 * n), out_specs=[spec] * (4 * n),
        out_shape=[jax.ShapeDtypeStruct(shape, F32)] * (4 * n),
        compiler_params=_cparams("parallel", "parallel"),
    )(*ws, *gs, *ms, *vs)
    return [tuple(outs[4 * i:4 * i + 4]) for i in range(n)]


def _place():
    x, y, c = lax.axis_index("x"), lax.axis_index("y"), lax.axis_index("c")
    chips = [(1 - x, y), (x, 1 - y), (1 - x, 1 - y)]
    return x, y, c, chips


def _any_specs(n):
    return [pl.BlockSpec(memory_space=pl.ANY)] * n


def _rcopy(src, dst, send_sem, recv_sem, dev):
    return pltpu.make_async_remote_copy(src_ref=src, dst_ref=dst, send_sem=send_sem, recv_sem=recv_sem,
                                        device_id=dev, device_id_type=MESH)


QUARTER = HALF // 2

XA, XB, YA, YB, RX, RY, F_XA, F_XB, F_YA, F_YB, F_D0, F_D1 = range(12)


def _gather_rider(bufs, views, at=None):
    n = len(bufs)

    def plan(rout, sems):
        send, recv = sems
        x, y, c, _ = _place()
        me, sx, sy, sd = 2 * x + y, 2 * (1 - x) + y, 2 * x + (1 - y), 2 * (1 - x) + (1 - y)
        nx, ny, sib = (1 - x, y, c), (x, 1 - y, c), (x, y, 1 - c)
        q0, q1 = c * HALF, c * HALF + QUARTER
        o0, o1 = (1 - c) * HALF, (1 - c) * HALF + QUARTER
        out = {XA: (me, q1, nx), XB: (me, q0, nx), YA: (me, q0, ny), YB: (me, q1, ny),
               RX: (sy, q0, nx), RY: (sx, q1, ny),
               F_XA: (sx, q1, sib), F_XB: (sx, q0, sib), F_YA: (sy, q0, sib), F_YB: (sy, q1, sib),
               F_D0: (sd, q0, sib), F_D1: (sd, q1, sib)}
        inn = {XA: (sx, q1), XB: (sx, q0), YA: (sy, q0), YB: (sy, q1), RX: (sd, q0), RY: (sd, q1),
               F_XA: (sx, o1), F_XB: (sx, o0), F_YA: (sy, o0), F_YB: (sy, o1), F_D0: (sd, o0), F_D1: (sd, o1)}

        def copy(kind, b):
            slot, col, dev = out[kind]
            win = views[b](rout[b], slot, col, QUARTER)
            return _rcopy(win, win, send.at[kind * n + b], recv.at[kind * n + b], dev)

        def land(kind, b):
            slot, col = inn[kind]
            win = views[b](rout[b], slot, col, QUARTER)
            return _rcopy(win, win, send.at[kind * n + b], recv.at[kind * n + b], (x, y, c))

        return copy, land

    first = (XA, YA, XB, YB)
    early = ((XA, (RY, F_XA)), (YA, (RX, F_YA)))
    late = ((XB, (F_XB,)), (YB, (F_YB,)), (RX, (F_D0,)), (RY, (F_D1,)))
    forwards = (F_XA, F_XB, F_YA, F_YB, F_D0, F_D1)
    sent = first + (RX, RY) + forwards

    def start(rin, rout, sems):
        copy, _ = plan(rout, sems)
        for kind in first:
            for b in range(n):
                copy(kind, b).start()

    def pass_on(rout, sems, links):
        copy, land = plan(rout, sems)
        for landed, then in links:
            for b in range(n):
                land(landed, b).wait_recv()
                for kind in then:
                    copy(kind, b).start()

    def between(rin, rout, sems):
        pass_on(rout, sems, early)

    def finish(rin, rout, sems):
        pass_on(rout, sems, late)
        copy, land = plan(rout, sems)
        for kind in forwards:
            for b in range(n):
                land(kind, b).wait_recv()
        for kind in sent:
            for b in range(n):
                copy(kind, b).wait_send()

    return _Rider(list(bufs), [jax.ShapeDtypeStruct(a.shape, a.dtype) for a in bufs], {b: b for b in range(n)},
                  [pltpu.SemaphoreType.DMA((12 * n,))] * 2, start, finish, between, at)


def _small_gather_rider(cw):
    def descs(rin, rout, sems, x, y, c, chips):
        return [_rcopy(rin[0], rout[0].at[2 * x + y], sems[1].at[j], sems[2].at[j], (chip[0], chip[1], c))
                for j, chip in enumerate(chips)]

    def start(rin, rout, sems):
        x, y, c, chips = _place()
        pltpu.make_async_copy(rin[0], rout[0].at[2 * x + y], sems[0].at[0]).start()
        for cp in descs(rin, rout, sems, x, y, c, chips):
            cp.start()

    def finish(rin, rout, sems):
        x, y, c, chips = _place()
        for j, chip in enumerate(chips):
            _rcopy(rin[0], rout[0].at[2 * chip[0] + chip[1]], sems[1].at[j], sems[2].at[j], (x, y, c)).wait_recv()
        for cp in descs(rin, rout, sems, x, y, c, chips):
            cp.wait_send()
        pltpu.make_async_copy(rin[0], rout[0].at[2 * x + y], sems[0].at[0]).wait()

    return _Rider([cw], [jax.ShapeDtypeStruct((NSH,) + cw.shape, cw.dtype)], {},
                  [pltpu.SemaphoreType.DMA((1,)), pltpu.SemaphoreType.DMA((3,)), pltpu.SemaphoreType.DMA((3,))],
                  start, finish)


def _to_sibling_rider(ps):
    n = len(ps)

    def descs(rin, rout, sems):
        x, y, c, _ = _place()
        return [_rcopy(rin[i].at[:, :, pl.ds((1 - c) * HALF, HALF)], rout[i], sems[0].at[i], sems[1].at[i],
                       (x, y, 1 - c)) for i in range(n)]

    def start(rin, rout, sems):
        for cp in descs(rin, rout, sems):
            cp.start()

    def finish(rin, rout, sems):
        for cp in descs(rin, rout, sems):
            cp.wait()

    return _Rider(list(ps), [jax.ShapeDtypeStruct(a.shape[:2] + (HALF,), a.dtype) for a in ps], {},
                  [pltpu.SemaphoreType.DMA((n,))] * 2, start, finish)


def _to_chips_rider(cs, first=0, count=None, into=None):
    n = len(cs)
    rows = [pl.ds(first, a.shape[1] - first if count is None else count) for a in cs]

    def descs(rin, rout, sems):
        x, y, c, chips = _place()
        return [_rcopy(rin[i].at[2 * chip[0] + chip[1], rows[i]], rout[i].at[j, rows[i]], sems[0].at[j * n + i],
                       sems[1].at[j * n + i], (chip[0], chip[1], c)) for j, chip in enumerate(chips) for i in range(n)]

    def start(rin, rout, sems):
        for cp in descs(rin, rout, sems):
            cp.start()

    def finish(rin, rout, sems):
        for cp in descs(rin, rout, sems):
            cp.wait()

    return _Rider(list(cs) + list(into or []), [jax.ShapeDtypeStruct((3,) + a.shape[1:], a.dtype) for a in cs],
                  {n + i: i for i in range(n)} if into else {}, [pltpu.SemaphoreType.DMA((3 * n,))] * 2, start, finish)


def _join_riders(riders):
    counts = [[len(r.operands) for r in riders], [len(r.out_shapes) for r in riders], [len(r.sems) for r in riders]]

    def each(step):
        def run(*refs):
            at = [0, 0, 0]
            for i, r in enumerate(riders):
                parts = [group[at[k]:at[k] + counts[k][i]] for k, group in enumerate(refs)]
                at = [at[k] + counts[k][i] for k in range(3)]
                step(r)(*parts)
        return run

    aliases = {sum(counts[0][:i]) + k: sum(counts[1][:i]) + v
               for i, r in enumerate(riders) for k, v in r.aliases.items()}
    return _Rider([a for r in riders for a in r.operands], [s for r in riders for s in r.out_shapes], aliases,
                  [s for r in riders for s in r.sems], each(lambda r: r.start), each(lambda r: r.finish),
                  each(lambda r: r.between or (lambda *refs: None)))


SMALL_ROWS = 16


def _swap_halves(gs, vec):
    n = len(gs)

    def body(*refs):
        v_ref, dst, o_ref = refs[n], refs[n + 1:2 * n + 1], refs[2 * n + 1]
        buf, send, recv, vsend, vrecv = refs[2 * n + 2:]
        x, y, c, _ = _place()
        cps = []
        for i in range(n):
            mine = dst[i].at[:, pl.ds(c * HALF, HALF)]
            cps.append(_rcopy(mine, mine, send.at[i], recv.at[i], (x, y, 1 - c)))
        for cp in cps:
            cp.start()

        me = 4 * x + 2 * y + c
        buf[me] = v_ref[...]
        vcps = []
        for k in range(1, 8):
            peer = (x ^ (k >> 2), y ^ ((k >> 1) & 1), c ^ (k & 1))
            vcps.append(_rcopy(v_ref, buf.at[me], vsend.at[k - 1], vrecv.at[k - 1], peer))
        for cp in vcps:
            cp.start()
        for k in range(1, 8):
            _rcopy(v_ref, buf.at[me ^ k], vsend.at[k - 1], vrecv.at[k - 1], (x, y, c)).wait_recv()
        for cp in vcps:
            cp.wait_send()
        t = buf[0]
        for d in range(1, 8):
            t = t + buf[d]
        o_ref[...] = t

        for i in range(n):
            other = dst[i].at[:, pl.ds((1 - c) * HALF, HALF)]
            _rcopy(other, other, send.at[i], recv.at[i], (x, y, c)).wait_recv()
        for cp in cps:
            cp.wait_send()

    vmem = pl.BlockSpec(memory_space=pltpu.VMEM)
    res = pl.pallas_call(
        body, name="grads_swap_halves", in_specs=_any_specs(n) + [vmem], out_specs=_any_specs(n) + [vmem],
        out_shape=[jax.ShapeDtypeStruct(g.shape, g.dtype) for g in gs] + [jax.ShapeDtypeStruct((SMALL_ROWS, D), F32)],
        input_output_aliases={i: i for i in range(n)},
        scratch_shapes=[pltpu.VMEM((8, SMALL_ROWS, D), F32)] + [pltpu.SemaphoreType.DMA((n,))] * 2
        + [pltpu.SemaphoreType.DMA((7,))] * 2,
    )(*gs, vec)
    return list(res[:n]), res[n]


def _col_window(ref, slot, col, ncols):
    return ref.at[slot, :, pl.ds(col, ncols)]


def _stack_window(first, count):
    def view(ref, slot, col, ncols):
        return ref.at[slot, pl.ds(first, count), :, pl.ds(col, ncols)]
    return view


def _row_tile(rows):
    for t in range(512, 15, -16):
        if rows % t == 0:
            return t
    return rows


def _same_shape_runs(arrs):
    runs, a = [], 0
    for b in range(1, len(arrs) + 1):
        if b == len(arrs) or arrs[b].shape != arrs[a].shape:
            runs.append((a, b))
            a = b
    return runs


class _Comm:
    def __init__(self):
        x, y, c = lax.axis_index("x"), lax.axis_index("y"), lax.axis_index("c")
        self.c_idx = jnp.reshape(c, (1,)).astype(jnp.int32)
        self.shard = jnp.reshape(2 * x + y, (1,)).astype(jnp.int32)
        self.place = jnp.stack([2 * x + y, c]).astype(jnp.int32)
        self.groups = {}
        self.sent = {}

    @staticmethod
    def gather(*bufs, part=None, at=None):
        views = [_col_window if b.ndim == 3 else _stack_window(*(part or (0, b.shape[1]))) for b in bufs]
        return _gather_rider(list(bufs), views, at)

    def reduce_rider(self, tag, names, ps, theirs=None, rows=None):
        csums = []
        for a, b in _same_shape_runs(ps):
            name, th = "pair_sum_%s%d" % (tag, a), _row_tile(ps[a].shape[1])
            csums += (_pair_sum(name, self.c_idx, ps[a:b], th) if theirs is None else
                      _pair_add(name, self.c_idx, ps[a:b], theirs[a:b], th))
        self.groups[tag] = [names, csums, None]
        self.sent[tag] = rows
        return _to_chips_rider(csums, 0, rows)

    def rest_rider(self, tag):
        _, csums, ts = self.groups[tag]
        return _to_chips_rider(csums, self.sent[tag], None, ts)

    def landed(self, tag, ts):
        self.groups[tag][2] = ts

    def finish(self, small):
        names, csums, ts = [], [], []
        for group_names, group_csums, group_ts in self.groups.values():
            names += group_names
            csums += group_csums
            ts += group_ts
        order = sorted(range(len(names)), key=lambda i: csums[i].shape[1])
        names, csums, ts = ([v[i] for i in order] for v in (names, csums, ts))
        halves = []
        for a, b in _same_shape_runs(csums):
            halves += _chip_sum("chip_sum_%d" % a, self.place, csums[a:b], ts[a:b], _row_tile(csums[a].shape[1]))
        grads, total = _swap_halves(halves, small)
        return dict(zip(names, grads)), total


ROPE_THETA = 10000.0
SMALL_1K = ("ffn1_pre_norm", "ffn1_post_norm", "mix_pre_norm", "ssm_norm", "mix_post_norm",
            "ffn2_pre_norm", "ffn2_post_norm")
SMALL_16 = ("dt_bias", "a_log", "d_skip")
OFF_CONVB = 7 * D
OFF_16 = OFF_CONVB + CONV_C
OFF_CONVW = OFF_16 + 48
OFF_LOSS = OFF_CONVW + CONV_K * CONV_C
SMALL_LEN = SMALL_ROWS * D


def _sds(shape, dtype):
    return jax.ShapeDtypeStruct(shape, dtype)


def _ridden(res, rider):
    return res if rider is not None else (res, None)


def _ffn_down(name, act, w, tail_of, rider=None):
    tail, o_specs, o_shapes = tail_of(TS)
    return _mm(name, [act, w.dn], NN, (S // TS,),
               [pl.BlockSpec((NSH, TS, FS), lambda i: (0, i, 0)),
                pl.BlockSpec((NSH, None, FS, D), lambda i: (0, w.d0, 0, 0))], o_specs, o_shapes, rider, tail)


def _ffn_dw(name, a, b, rider=None):
    return _mm(name, [a, b], TN, (NSH,),
               [pl.BlockSpec((None, S, FS), lambda s: (s, 0, 0)), pl.BlockSpec((S, D), lambda s: (0, 0))],
               pl.BlockSpec((None, FS, D), lambda s: (s, 0, 0)), _sds((NSH, FS, D), BF16), rider)


def _ffn_dn(name, dgate, dup, w, tail_of, rider=None):
    rows = TS // 2
    tail, o_specs, o_shapes = tail_of(rows)
    a2 = pl.BlockSpec((NSH, rows, FS), lambda i: (0, i, 0))
    return _mm(name, [dgate, w.gu, dup, w.gu], NN, (S // rows,),
               [a2, pl.BlockSpec((NSH, None, FS, D), lambda i: (0, w.g0, 0, 0)),
                a2, pl.BlockSpec((NSH, None, FS, D), lambda i: (0, w.g0 + 1, 0, 0))], o_specs, o_shapes, rider, tail)


def _out_proj_dx(dh, wout, ot):
    def body(dh_ref, w_ref, o_ref, dyn_ref, do_ref, dl_ref):
        dm = _dot(dh_ref[...], w_ref[...], NT)
        dyn_ref[...] = dm[:, D:]
        for b in range(TS // 128):
            for j, blk in enumerate(_rows_to_blocks(dm[128 * b:128 * (b + 1), :D])):
                do = blk.astype(BF16)
                do_ref[j, b] = do
                dl_ref[j, b] = jnp.sum(o_ref[j, b] * do.astype(F32), axis=0, keepdims=True)

    blocks = pl.BlockSpec((NKV, TS // 128, HD, QROWS), lambda i: (0, i, 0, 0))
    return pl.pallas_call(
        body, name="out_proj_dx", grid=(S // TS,),
        in_specs=[pl.BlockSpec((TS, D), lambda i: (i, 0)), pl.BlockSpec((2 * D, D), lambda i: (0, 0)), blocks],
        out_specs=[pl.BlockSpec((TS, D), lambda i: (i, 0)), blocks,
                   pl.BlockSpec((NKV, TS // 128, 1, QROWS), lambda i: (0, i, 0, 0))],
        out_shape=[_sds((S, D), F32), _sds((NKV, NCH, HD, QROWS), BF16), _sds((NKV, NCH, 1, QROWS), F32)],
        compiler_params=_cparams("parallel"),
    )(dh, wout, ot)


def _heads(t, n):
    return t.reshape(S, n, HD).transpose(1, 0, 2)


def _pad128(v):
    return jnp.pad(v, ((0, 0), (0, 128 - v.shape[1])))


def _local_step(x, positions, tgt, sp, gu1, d1, f2, wint, wout, convw, comm=None):
    inv_freq = ROPE_THETA ** (-jnp.arange(0, HD, 2, dtype=F32) / HD)
    ang = positions.astype(F32)[:, None] * inv_freq
    ang = jnp.concatenate([ang, ang, ang, ang], axis=-1)
    cos, sin = jnp.cos(ang), jnp.sin(ang)
    dtb, alog = _pad128(sp["dt_bias"]), _pad128(sp["a_log"])
    dskip_l = jnp.repeat(sp["d_skip"], HD, axis=1)
    convb = sp["conv_b"]

    if comm:
        rider = _join_riders([comm.gather(gu1), _small_gather_rider(convw)])
        (n1, d1, f2, wint, wout, bias), (gu1, convw) = _prenorm_casts(
            "prenorm1", x, sp["ffn1_pre_norm"], comm.shard, [(d1, 0), (f2, 0), (wint, 1), (wout, 0)], rider)
        wint, wout = wint.reshape(NSH, WIN_SH, D), wout.reshape(NSH, 2 * D // NSH, D)
        convw = convw.transpose(1, 0, 2).reshape(CONV_K, CONV_C)
    else:
        n1, bias = _prenorm("prenorm1", x, sp["ffn1_pre_norm"]), _bias_table()
    rider = comm.gather(d1) if comm else None
    (fg1, fu1, act1), got = _ridden(_ffn_up("ffn1_up", n1, _FfnW(gu1, 0, d1, 0), rider), rider)
    if comm:
        d1, = got
    w1 = _FfnW(gu1, 0, d1, 0)
    rider = comm.gather(wint) if comm else None
    (h1, x1, n2), got = _ridden(_ffn_down(
        "ffn1_down", act1, w1,
        lambda rows: _tail_postres(rows, x, sp["ffn1_post_norm"], 0.5, sp["mix_pre_norm"]), rider), rider)
    if comm:
        wint, = got
    wint_pad = jnp.pad(wint.reshape(WIN_COLS, D), ((0, WIN_PAD - WIN_COLS), (0, 0)))

    pw = WIN_PAD // 3
    rider = comm.gather(f2, part=(0, 1)) if comm else None
    proj, got = _ridden(_mm(
        "in_proj", [n2, wint_pad], NT, (S // TS, 3),
        [pl.BlockSpec((TS, D), lambda i, j: (i, 0)), pl.BlockSpec((pw, D), lambda i, j: (j, 0))],
        pl.BlockSpec((TS, pw), lambda i, j: (i, j)), _sds((S, WIN_PAD), F32), rider), rider)
    if comm:
        f2, = got
    qt = _rope_q(proj, cos, sin)
    k_rot, v_bf, kt, vt = _rope_kv(proj, cos, sin)
    kh, vh = _heads(k_rot, NKV), _heads(v_bf, NKV)
    rider = comm.gather(f2, wout, part=(1, 2), at=9) if comm else None
    (ot, lse, mixed), got = _ridden(_attn_fwd(qt, kh, vt, bias, rider), rider)
    if comm:
        f2, wout = got
    w2 = _FfnW(f2, 0, f2, 2)
    wout = wout.reshape(2 * D, D)
    xbc, conv_y = _conv_fwd(proj, convw, convb)
    y, mixed, hprev = _ssd_fwd(xbc, proj, dtb, alog, dskip_l, sp["ssm_norm"], mixed)
    tail, o_specs, o_shapes = _tail_postres(TS, x1, sp["mix_post_norm"], 1.0, sp["ffn2_pre_norm"])
    h2, x2, n3 = _mm("out_proj", [mixed, wout], NN, (S // TS,),
                     [pl.BlockSpec((TS, 2 * D), lambda i: (i, 0)), pl.BlockSpec((2 * D, D), lambda i: (0, 0))],
                     o_specs, o_shapes, None, tail)

    fg2, fu2, act2 = _ffn_up("ffn2_up", n3, w2)
    dy, dh3, dp3, loss = _ffn_down(
        "ffn2_down", act2, w2, lambda rows: _tail_final(rows, x2, sp["ffn2_post_norm"], tgt, 0.5))

    dgate2, dup2 = _ffn_dact("ffn2_dact", dh3, w2, fg2, fu2)
    dws2 = [_ffn_dw("ffn2_dwg", dgate2, n3), _ffn_dw("ffn2_dwu", dup2, n3), _ffn_dw("ffn2_dwd", act2, dh3)]
    dx2, dh2, dg3, dp2 = _ffn_dn(
        "ffn2_dn", dgate2, dup2, w2,
        lambda rows: _tail_mid_bwd(rows, dy, x2, sp["ffn2_pre_norm"], h2, sp["mix_post_norm"], 1.0))

    dyn, dot_, delta = _out_proj_dx(dh2, wout, ot)
    dwout = _mm("out_proj_dw", [mixed, dh2], TN, (2,),
                [pl.BlockSpec((S, D), lambda m: (0, m)), pl.BlockSpec((S, D), lambda m: (0, 0))],
                pl.BlockSpec((D, D), lambda m: (m, 0)), _sds((2 * D, D), BF16))
    dwout = dwout.reshape(NSH, 2 * D // NSH, D)

    def riding(tag, names, ps, call, theirs=None, rows=None):
        rider = None
        if comm:
            rider = comm.rest_rider(tag) if names is None else comm.reduce_rider(tag, names, ps, theirs, rows)
        res, got = _ridden(call(rider), rider)
        if comm:
            comm.landed(tag, got)
        return res

    rider = _to_sibling_rider(dws2 + [dwout]) if comm else None
    (dxbc, dproj, ddt, dssm, dsc), theirs = _ridden(
        _ssd_bwd(dyn, y, xbc, proj, hprev, dtb, alog, dskip_l, sp["ssm_norm"], rider), rider)
    dproj, dcw8, dcb = _conv_bwd(dxbc, conv_y, proj, convw, dproj)
    dqt, dkh, dvh = riding("a", BIG[3:6] + ("w_out",), dws2 + [dwout], lambda rider: _attn_bwd(
        qt, kh, kt, vh, dot_, lse, delta, bias, rider), theirs)
    dproj = _rope_dq(dqt, cos, sin, dproj)
    dproj = _rope_dkv(dkh, dvh, cos, sin, dproj)
    dproj = lax.dynamic_update_slice(dproj, ddt, (0, COL_DT))
    dwint = _mm("in_proj_dw", [dproj, n2], TN, (3,),
                [pl.BlockSpec((S, pw), lambda j: (0, j)), pl.BlockSpec((S, D), lambda j: (0, 0))],
                pl.BlockSpec((pw, D), lambda j: (j, 0)), _sds((WIN_PAD, D), BF16))
    dwint = dwint[:WIN_COLS].reshape(NSH, WIN_SH, D)

    tail, o_specs, o_shapes = _tail_mid_bwd(TS, dx2, x1, sp["mix_pre_norm"], h1, sp["ffn1_post_norm"], 0.5)
    dx1, dh1, dg2, dp1 = riding("b", ("w_in",), [dwint], lambda rider: _mm(
        "in_proj_dx", [dproj, wint_pad], NN, (S // TS,),
        [pl.BlockSpec((TS, WIN_PAD), lambda i: (i, 0)), pl.BlockSpec((WIN_PAD, D), lambda i: (0, 0))],
        o_specs, o_shapes, rider, tail), rows=W_IN_FIRST)

    dwd1 = riding("b", None, None, lambda rider: _ffn_dw("ffn1_dwd", act1, dh1, rider))
    dgate1, dup1 = riding("d", BIG[2:3], [dwd1], lambda rider: _ffn_dact("ffn1_dact", dh1, w1, fg1, fu1, rider))
    dwg1, dwu1 = _ffn_dw("ffn1_dwg", dgate1, n1), _ffn_dw("ffn1_dwu", dup1, n1)
    grad_x, dg1 = riding("g", BIG[0:2], [dwg1, dwu1], lambda rider: _ffn_dn(
        "ffn1_dn", dgate1, dup1, w1, lambda rows: _tail_first_bwd(rows, dx1, x, sp["ffn1_pre_norm"]), rider))
    dws1 = [dwg1, dwu1, dwd1]

    small = jnp.concatenate([
        dg1[0], dp1[0], dg2[0], dssm[0], dp2[0], dg3[0], dp3[0], dcb[0],
        dsc[0, :16], dsc[1, :16], dsc[2, :16], dcw8[:CONV_K].reshape(-1), loss[0, :1]])
    small = jnp.pad(small, (0, SMALL_LEN - small.shape[0])).reshape(SMALL_ROWS, D)
    if comm is None:
        return grad_x, dws1 + dws2 + [dwint, dwout], small
    return (grad_x,) + comm.finish(small)


WEIGHTS = ("ffn1_pre_norm", "ffn1_w_gate", "ffn1_w_up", "ffn1_w_down", "ffn1_post_norm", "mix_pre_norm", "w_in",
           "conv_w", "conv_b", "dt_bias", "a_log", "d_skip", "ssm_norm", "w_out", "mix_post_norm", "ffn2_pre_norm",
           "ffn2_w_gate", "ffn2_w_up", "ffn2_w_down", "ffn2_post_norm")
BIG = ("ffn1_w_gate", "ffn1_w_up", "ffn1_w_down", "ffn2_w_gate", "ffn2_w_up", "ffn2_w_down", "w_in", "w_out")
TRANSPOSED = ("ffn1_w_gate", "ffn1_w_up", "ffn2_w_gate", "ffn2_w_up", "w_in")
SMALL_ORDER = SMALL_1K + ("conv_b",) + SMALL_16
CONVW_SH = CONV_C // NSH


def _shard2d(t, name):
    return t[0].T if name in TRANSPOSED else t[0]


def _unshard2d(t, name):
    return (t.T if name in TRANSPOSED else t)[None]


def _rows3d(t):
    return t.transpose(2, 0, 1)


def _pack_small(d, prefix, shard_of_convw):
    flat = jnp.concatenate([d[prefix + n][0] for n in SMALL_ORDER] + [shard_of_convw.reshape(-1)])
    return jnp.pad(flat, (0, SMALL_LEN - flat.shape[0])).reshape(SMALL_ROWS, D)


def _unpack_small(block, like):
    flat = block.reshape(-1)
    out, off = {}, 0
    for n in SMALL_ORDER:
        size = like[n].shape[1]
        out[n] = flat[off:off + size].reshape(1, size)
        off += size
    out["conv_w"] = flat[off:off + CONV_K * CONVW_SH].reshape(1, CONV_K, CONVW_SH)
    return out


def kernel(x, positions, ffn1_pre_norm, ffn1_w_gate, ffn1_w_up, ffn1_w_down, ffn1_post_norm, mix_pre_norm, w_in, conv_w, conv_b, dt_bias, a_log, d_skip, ssm_norm, w_out, mix_post_norm, ffn2_pre_norm, ffn2_w_gate, ffn2_w_up, ffn2_w_down, ffn2_post_norm, loss_target, m_ffn1_pre_norm, m_ffn1_w_gate, m_ffn1_w_up, m_ffn1_w_down, m_ffn1_post_norm, m_mix_pre_norm, m_w_in, m_conv_w, m_conv_b, m_dt_bias, m_a_log, m_d_skip, m_ssm_norm, m_w_out, m_mix_post_norm, m_ffn2_pre_norm, m_ffn2_w_gate, m_ffn2_w_up, m_ffn2_w_down, m_ffn2_post_norm, v_ffn1_pre_norm, v_ffn1_w_gate, v_ffn1_w_up, v_ffn1_w_down, v_ffn1_post_norm, v_mix_pre_norm, v_w_in, v_conv_w, v_conv_b, v_dt_bias, v_a_log, v_d_skip, v_ssm_norm, v_w_out, v_mix_post_norm, v_ffn2_pre_norm, v_ffn2_w_gate, v_ffn2_w_up, v_ffn2_w_down, v_ffn2_post_norm):
    given = dict(locals())
    xi, yi = lax.axis_index("x"), lax.axis_index("y")

    comm = _Comm()
    big = {p + n: _shard2d(given[p + n], n) for n in BIG for p in ("", "m_", "v_")}
    gu1 = _cast_stack("cast_ffn1_gate_up", comm.shard, [big[n] for n in BIG[0:2]], 176, D)

    sp = {n: given[n] for n in SMALL_ORDER}
    grad_x, big_grads, small = _local_step(
        x[0], positions[0], loss_target[0], sp, gu1, [big[BIG[2]]], [big[n] for n in BIG[3:6]], [big["w_in"]],
        [big["w_out"]], conv_w[0], comm)

    tot = small.reshape(-1)
    loss = tot[OFF_LOSS]
    small_grads, off = {}, 0
    for n in SMALL_ORDER:
        size = given[n].shape[1]
        small_grads[n] = tot[off:off + size].reshape(1, size)
        off += size
    dconvw = tot[OFF_CONVW:OFF_CONVW + CONV_K * CONV_C].reshape(CONV_K, NSH, CONVW_SH)
    dconvw = lax.dynamic_index_in_dim(dconvw, 2 * xi + yi, axis=1, keepdims=False)
    small_grads["conv_w"] = dconvw.reshape(1, CONV_K, CONVW_SH)

    upd = {}
    for names, tr in ((BIG[0:3], 176), (BIG[3:6], 176), (BIG[7:8], 256)):
        res = _adamw("adamw_" + names[0], [big[n] for n in names], [big_grads[n] for n in names],
                     [big["m_" + n] for n in names], [big["v_" + n] for n in names], tr, D)
        for n, r in zip(names, res):
            upd[n] = tuple(_unshard2d(t, n) for t in r)
    g_win = big_grads["w_in"].reshape(WIN_SH, 1, D)
    res, = _adamw("adamw_w_in", [_rows3d(w_in)], [g_win], [_rows3d(m_w_in)], [_rows3d(v_w_in)], WIN_SH // 4, D)
    upd["w_in"] = tuple(t.transpose(1, 2, 0) for t in res)
    (dl, m2, v2, _), = _adamw(
        "adamw_small", [_pack_small(given, "", conv_w[0])], [_pack_small(small_grads, "", dconvw)],
        [_pack_small(given, "m_", m_conv_w[0])], [_pack_small(given, "v_", v_conv_w[0])], SMALL_ROWS, D)
    dl, m2, v2 = (_unpack_small(t, given) for t in (dl, m2, v2))
    for n in SMALL_ORDER + ("conv_w",):
        upd[n] = (dl[n], m2[n], v2[n], small_grads[n])

    return (loss, grad_x[None], *[upd[n][3] for n in WEIGHTS], *[upd[n][0] for n in WEIGHTS],
            *[upd[n][1] for n in WEIGHTS], *[upd[n][2] for n in WEIGHTS])
```

```python
import functools
import typing

import jax
import jax.numpy as jnp
from jax import lax
from jax.experimental import pallas as pl
from jax.experimental.pallas import tpu as pltpu

F32 = jnp.float32
BF16 = jnp.bfloat16

S = 2048
D = 1024
FF = 2816
NSH = 4
FS = FF // NSH
HALF = D // 2
HD = 64
NKV = 4
NQ_PER_KV = 4
KVW = NKV * HD
QCOLS = NQ_PER_KV * HD
CONV_C = 1536
CONV_K = 4
SSM_W = 1024
NST = 128
NCH = S // 128
WIN_COLS = 4112
WIN_SH = WIN_COLS // NSH
W_IN_FIRST = 768
WIN_PAD = 4224
COL_DT = 4096
EPS = 1e-6
NEG = -1e30

ADAM_LR = 0.001
ADAM_B1 = 0.9
ADAM_B2 = 0.999
ADAM_EPS = 1e-08
ADAM_WD = 0.01
ADAM_STEP = 10

VMEM_LIMIT = 56 * 1024 * 1024
TS = 512
TR = 256

NN = (((1,), (0,)), ((), ()))
NT = (((1,), (1,)), ((), ()))
TN = (((0,), (0,)), ((), ()))
MESH = pl.DeviceIdType.MESH


def _cparams(*sem):
    return pltpu.CompilerParams(dimension_semantics=sem, vmem_limit_bytes=VMEM_LIMIT)


def _dot(a, b, dims):
    return lax.dot_general(a.astype(BF16), b.astype(BF16), dims, preferred_element_type=F32)


def _bf16_pieces(v):
    hi = v.astype(BF16)
    rest = v - hi.astype(F32)
    mid = rest.astype(BF16)
    return hi, mid, (rest - mid.astype(F32)).astype(BF16)


def _dot_exact(a, b, ones="a"):
    if ones == "a":
        sel = a.astype(BF16)
        parts = [lax.dot_general(sel, p, NN, preferred_element_type=F32) for p in _bf16_pieces(b)]
    else:
        sel = b.astype(BF16)
        parts = [lax.dot_general(p, sel, NN, preferred_element_type=F32) for p in _bf16_pieces(a)]
    return (parts[2] + parts[1]) + parts[0]


def _sigmoid(v):
    return 1.0 / (1.0 + jnp.exp(-v))


class _Rider(typing.NamedTuple):
    operands: list
    out_shapes: list
    aliases: dict
    sems: list
    start: typing.Callable
    finish: typing.Callable
    between: typing.Callable = None
    at: int = None


def _call(body, name, grid, in_specs, out_specs, out_shape, operands, scratch=(), sem=(), rider=None, prefetch=0):
    multi = isinstance(out_shape, (list, tuple))

    def launch(kernel, in_specs, out_specs, out_shape, scratch, aliases, sem, args):
        if prefetch:
            how = dict(grid_spec=pltpu.PrefetchScalarGridSpec(
                num_scalar_prefetch=prefetch, grid=grid, in_specs=in_specs, out_specs=out_specs,
                scratch_shapes=scratch))
        else:
            how = dict(grid=grid, in_specs=in_specs, out_specs=out_specs, scratch_shapes=scratch)
        return pl.pallas_call(kernel, name=name, out_shape=out_shape, input_output_aliases=aliases,
                              compiler_params=_cparams(*sem), **how)(*args)

    if rider is None:
        return launch(body, in_specs, out_specs, out_shape, list(scratch), {}, sem, operands)
    outs = list(out_shape) if multi else [out_shape]
    ospecs = list(out_specs) if multi else [out_specs]
    n_in, n_out, n_scr = len(operands) - prefetch, len(outs), len(scratch)
    ri, ro = len(rider.operands), len(rider.out_shapes)

    def wrapped(*refs):
        scalars, refs = refs[:prefetch], refs[prefetch:]
        o0 = n_in + ri
        s0 = o0 + n_out + ro
        rin, rout, rsem = refs[n_in:o0], refs[o0 + n_out:s0], refs[s0 + n_scr:]
        ids = [pl.program_id(a) for a in range(len(grid))]
        first = functools.reduce(jnp.logical_and, [i == 0 for i in ids])
        last = functools.reduce(jnp.logical_and, [i == g - 1 for i, g in zip(ids, grid)])

        @pl.when(first)
        def _():
            rider.start(rin, rout, rsem)

        if rider.between is not None:
            steps = functools.reduce(lambda a, b: a * b, grid)
            step = functools.reduce(lambda a, ig: a * ig[1] + ig[0], zip(ids, grid), 0)

            @pl.when(step == (steps // 3 if rider.at is None else rider.at))
            def _():
                rider.between(rin, rout, rsem)

        body(*scalars, *refs[:n_in], *refs[o0:o0 + n_out], *refs[s0:s0 + n_scr])

        @pl.when(last)
        def _():
            rider.finish(rin, rout, rsem)

    hbm = pl.BlockSpec(memory_space=pl.ANY)
    res = launch(wrapped, list(in_specs) + [hbm] * ri, ospecs + [hbm] * ro, outs + list(rider.out_shapes),
                 list(scratch) + list(rider.sems),
                 {prefetch + n_in + k: n_out + v for k, v in rider.aliases.items()},
                 ("arbitrary",) * len(grid), (*operands, *rider.operands))
    main = list(res[:n_out])
    return (main if multi else main[0]), list(res[n_out:])


class _Tail(typing.NamedTuple):
    fn: typing.Callable
    operands: list
    in_specs: list


def _mm(name, operands, dims, grid, in_specs, o_spec, out_shape, rider=None, tail=None):
    npairs = len(operands) // 2
    extra = [] if tail is None else list(tail.operands)
    nin = 2 * npairs + len(extra)

    def body(*refs):
        t = None
        for i in range(npairs):
            a, b = refs[2 * i], refs[2 * i + 1]
            parts = [(a[s], b[s]) for s in range(a.shape[0])] if len(a.shape) == 3 else [(a[...], b[...])]
            for pa, pb in parts:
                d = _dot(pa, pb, dims)
                t = d if t is None else t + d
        if tail is None:
            refs[nin][...] = t.astype(refs[nin].dtype)
        else:
            tail.fn(t, refs[2 * npairs:nin], refs[nin:])

    sem = ("parallel" if tail is None else "arbitrary",) * len(grid)
    specs = list(in_specs) + ([] if tail is None else list(tail.in_specs))
    return _call(body, name, grid, specs, o_spec, out_shape, list(operands) + extra, (), sem, rider)


class _FfnW(typing.NamedTuple):
    gu: jax.Array
    g0: int
    dn: jax.Array
    d0: int


def _ffn_up(name, n, w, rider=None):
    def body(n_ref, wg_ref, wu_ref, fg_ref, fu_ref, a_ref):
        nb = n_ref[...]
        g = _dot(nb, wg_ref[...], NT)
        u = _dot(nb, wu_ref[...], NT)
        sg = _sigmoid(g)
        silu = g * sg
        fg_ref[...] = (u * (sg * (1.0 + g * (1.0 - sg)))).astype(BF16)
        fu_ref[...] = silu.astype(BF16)
        a_ref[...] = (silu * u).astype(BF16)

    out = jax.ShapeDtypeStruct((NSH, S, FS), BF16)
    ospec = pl.BlockSpec((None, TS, FS), lambda s, i: (s, i, 0))
    return _call(
        body, name, (NSH, S // TS),
        [pl.BlockSpec((TS, D), lambda s, i: (i, 0)),
         pl.BlockSpec((None, None, FS, D), lambda s, i: (s, w.g0, 0, 0)),
         pl.BlockSpec((None, None, FS, D), lambda s, i: (s, w.g0 + 1, 0, 0))],
        [ospec, ospec, ospec], [out, out, out], (n, w.gu, w.gu), sem=("parallel", "parallel"), rider=rider)


def _ffn_dact(name, dh, w, fgate, fup, rider=None):
    def body(dh_ref, wd_ref, fg_ref, fu_ref, dg_ref, du_ref):
        da = _dot(dh_ref[...], wd_ref[...], NT)
        dg_ref[...] = (da * fg_ref[...].astype(F32)).astype(BF16)
        du_ref[...] = (da * fu_ref[...].astype(F32)).astype(BF16)

    out = jax.ShapeDtypeStruct((NSH, S, FS), BF16)
    aspec = pl.BlockSpec((None, TS, FS), lambda s, i: (s, i, 0))
    return _call(
        body, name, (NSH, S // TS),
        [pl.BlockSpec((TS, D), lambda s, i: (i, 0)),
         pl.BlockSpec((None, None, FS, D), lambda s, i: (s, w.d0, 0, 0)), aspec, aspec],
        [aspec, aspec], [out, out], (dh, w.dn, fgate, fup), sem=("parallel", "parallel"), rider=rider)


def _rstd(v):
    return lax.rsqrt(jnp.mean(v * v, axis=-1, keepdims=True) + EPS)


def _row_spec():
    return pl.BlockSpec((TR, D), lambda i: (i, 0))


def _vec_spec():
    return pl.BlockSpec((1, D), lambda i: (0, 0))


def _acc_rows(ref, v):
    @pl.when(pl.program_id(0) == 0)
    def _():
        ref[...] = jnp.zeros_like(ref)
    ref[...] += jnp.sum(v, axis=0, keepdims=True)


def _prenorm(name, x, g):
    def body(x_ref, g_ref, n_ref):
        xv = x_ref[...]
        n_ref[...] = (xv * _rstd(xv) * g_ref[...]).astype(BF16)

    return pl.pallas_call(
        body, name=name, grid=(S // TR,), in_specs=[_row_spec(), _vec_spec()], out_specs=_row_spec(),
        out_shape=jax.ShapeDtypeStruct((S, D), BF16), compiler_params=_cparams("parallel"),
    )(x, g)


ENTRY_STEPS = 4


def _prenorm_casts(name, x, g, slot, groups, rider):
    def body(s_ref, x_ref, g_ref, *refs):
        ins, outs = refs[:len(refs) - len(groups) - 1], refs[len(refs) - len(groups) - 1:]
        xv = x_ref[...]
        outs[0][...] = (xv * _rstd(xv) * g_ref[...]).astype(BF16)
        at = 0
        for (arrs, _), out in zip(groups, outs[1:]):
            for k in range(len(arrs)):
                out[k] = ins[at + k][...].astype(BF16)
            at += len(arrs)

    rows = pl.BlockSpec((S // ENTRY_STEPS, D), lambda i, sr: (i, 0))
    in_specs, out_specs, out_shapes = [rows, pl.BlockSpec((1, D), lambda i, sr: (0, 0))], [rows], [_rows_bf16()]
    for arrs, axis in groups:
        r, c = arrs[0].shape
        if axis == 0:
            blk, at, at_out = (r // ENTRY_STEPS, c), (lambda i, sr: (i, 0)), (lambda i, sr: (sr[0], 0, i, 0))
        else:
            blk, at, at_out = (r, c // ENTRY_STEPS), (lambda i, sr: (0, i)), (lambda i, sr: (sr[0], 0, 0, i))
        in_specs += [pl.BlockSpec(blk, at)] * len(arrs)
        out_specs.append(pl.BlockSpec((None, len(arrs)) + blk, at_out))
        out_shapes.append(jax.ShapeDtypeStruct((NSH, len(arrs), r, c), BF16))
    return _call(body, name, (ENTRY_STEPS,), in_specs, out_specs, out_shapes,
                 [slot, x, g] + [a for arrs, _ in groups for a in arrs], rider=rider, prefetch=1)


def _rows_spec(rows):
    return pl.BlockSpec((rows, D), lambda i: (i, 0))


def _rows_f32():
    return jax.ShapeDtypeStruct((S, D), F32)


def _rows_bf16():
    return jax.ShapeDtypeStruct((S, D), BF16)


def _vec_f32():
    return jax.ShapeDtypeStruct((1, D), F32)


def _tail_postres(rows, x, p, alpha, gnext):
    def fn(h, ins, outs):
        x_ref, p_ref, g_ref = ins
        h_ref, xo_ref, n_ref = outs
        h_ref[...] = h
        xo = x_ref[...] + alpha * (h * _rstd(h) * p_ref[...])
        xo_ref[...] = xo
        n_ref[...] = (xo * _rstd(xo) * g_ref[...]).astype(BF16)

    rs = _rows_spec(rows)
    return (_Tail(fn, [x, p, gnext], [rs, _vec_spec(), _vec_spec()]), [rs, rs, rs],
            [_rows_f32(), _rows_f32(), _rows_bf16()])


def _tail_final(rows, x, p, tgt, alpha):
    def fn(h, ins, outs):
        x_ref, p_ref, t_ref = ins
        dy_ref, dh_ref, dp_ref, loss_ref = outs
        r = _rstd(h)
        hn = h * r
        pv = p_ref[...]
        e = x_ref[...] + alpha * (hn * pv) - t_ref[...]
        dy = e * (1.0 / D)
        dy_ref[...] = dy
        du = alpha * dy * pv
        dh_ref[...] = (r * (du - hn * jnp.mean(du * hn, axis=-1, keepdims=True))).astype(BF16)
        _acc_rows(dp_ref, alpha * dy * hn)
        part = 0.5 * jnp.sum(jnp.mean(e * e, axis=-1, keepdims=True), axis=0, keepdims=True)
        _acc_rows(loss_ref, jnp.broadcast_to(part, (1, 128)))

    rs = _rows_spec(rows)
    return (_Tail(fn, [x, p, tgt], [rs, _vec_spec(), rs]),
            [rs, rs, _vec_spec(), pl.BlockSpec((1, 128), lambda i: (0, 0))],
            [_rows_f32(), _rows_bf16(), _vec_f32(), jax.ShapeDtypeStruct((1, 128), F32)])


def _norm_bwd(dn, xv, g_ref, dg_ref):
    r = _rstd(xv)
    xn = xv * r
    dng = dn * g_ref[...]
    _acc_rows(dg_ref, dn * xn)
    return r * (dng - xn * jnp.mean(dng * xn, axis=-1, keepdims=True))


def _tail_mid_bwd(rows, dres, x, g, h, p, alpha):
    def fn(dn, ins, outs):
        dr_ref, x_ref, g_ref, h_ref, p_ref = ins
        dx_ref, dh_ref, dg_ref, dp_ref = outs
        dx = dr_ref[...] + _norm_bwd(dn, x_ref[...], g_ref, dg_ref)
        dx_ref[...] = dx
        hv = h_ref[...]
        r = _rstd(hv)
        hn = hv * r
        du = alpha * dx * p_ref[...]
        dh_ref[...] = (r * (du - hn * jnp.mean(du * hn, axis=-1, keepdims=True))).astype(BF16)
        _acc_rows(dp_ref, alpha * dx * hn)

    rs = _rows_spec(rows)
    return (_Tail(fn, [dres, x, g, h, p], [rs, rs, _vec_spec(), rs, _vec_spec()]),
            [rs, rs, _vec_spec(), _vec_spec()], [_rows_f32(), _rows_bf16(), _vec_f32(), _vec_f32()])


def _tail_first_bwd(rows, dres, x, g):
    def fn(dn, ins, outs):
        dr_ref, x_ref, g_ref = ins
        dx_ref, dg_ref = outs
        dx_ref[...] = dr_ref[...] + _norm_bwd(dn, x_ref[...], g_ref, dg_ref)

    rs = _rows_spec(rows)
    return (_Tail(fn, [dres, x, g], [rs, rs, _vec_spec()]), [rs, _vec_spec()], [_rows_f32(), _vec_f32()])


def _rotate(t, c128, s128, sign, scale):
    width = t.shape[1]
    c = jnp.tile(c128, (1, width // 128))
    sn = jnp.tile(s128, (1, width // 128))
    lane = lax.broadcasted_iota(jnp.int32, t.shape, 1) & (HD - 1)
    rot = jnp.where(lane < HD // 2, -pltpu.roll(t, width - HD // 2, 1), pltpu.roll(t, HD // 2, 1))
    return (t * c + sign * (rot * sn)) * scale


def _rows_to_blocks(y):
    out = []
    for j in range(NKV):
        yt = y[:, QCOLS * j:QCOLS * (j + 1)].T
        out.append(jnp.concatenate([yt[HD * g:HD * (g + 1)] for g in range(NQ_PER_KV)], axis=1))
    return out


def _blocks_to_rows(blocks):
    cols = []
    for b in blocks:
        stacked = jnp.concatenate([b[:, 128 * g:128 * (g + 1)] for g in range(NQ_PER_KV)], axis=0)
        cols.append(stacked.T)
    return jnp.concatenate(cols, axis=1)


def _rope_q(proj, cos, sin):
    def body(t_ref, c_ref, s_ref, o_ref):
        y = _rotate(t_ref[...], c_ref[...], s_ref[...], 1.0, HD ** -0.5)
        for j, blk in enumerate(_rows_to_blocks(y)):
            o_ref[j] = blk.astype(BF16)

    return pl.pallas_call(
        body, name="rope_q", grid=(NCH,),
        in_specs=[pl.BlockSpec((128, D), lambda i: (i, 0)),
                  pl.BlockSpec((128, 128), lambda i: (i, 0)), pl.BlockSpec((128, 128), lambda i: (i, 0))],
        out_specs=pl.BlockSpec((NKV, None, HD, QROWS), lambda i: (0, i, 0, 0)),
        out_shape=jax.ShapeDtypeStruct((NKV, NCH, HD, QROWS), BF16), compiler_params=_cparams("parallel"),
    )(proj, cos, sin)


def _rope_dq(dqt, cos, sin, dproj):
    def body(t_ref, c_ref, s_ref, buf_ref, o_ref):
        t = _blocks_to_rows([t_ref[j] for j in range(NKV)])
        o_ref[...] = _rotate(t, c_ref[...], s_ref[...], -1.0, HD ** -0.5).astype(BF16)

    return pl.pallas_call(
        body, name="rope_dq", grid=(NCH,),
        in_specs=[pl.BlockSpec((NKV, None, HD, QROWS), lambda i: (0, i, 0, 0)),
                  pl.BlockSpec((128, 128), lambda i: (i, 0)), pl.BlockSpec((128, 128), lambda i: (i, 0)),
                  pl.BlockSpec(memory_space=pl.ANY)],
        out_specs=pl.BlockSpec((128, D), lambda i: (i, 0)),
        out_shape=jax.ShapeDtypeStruct(dproj.shape, BF16), input_output_aliases={3: 0},
        compiler_params=_cparams("parallel"),
    )(dqt, cos, sin, dproj)


def _rope_dkv(dkt, dvt, cos, sin, dproj):
    def body(k_ref, v_ref, c_ref, s_ref, buf_ref, o_ref):
        dk = jnp.concatenate([k_ref[j] for j in range(NKV)], axis=0).T
        dv = jnp.concatenate([v_ref[j] for j in range(NKV)], axis=0).T
        dk = _rotate(dk, c_ref[...], s_ref[...], -1.0, 1.0)
        o_ref[...] = jnp.concatenate([dk, dv], axis=1).astype(BF16)

    tspec = pl.BlockSpec((NKV, HD, 128), lambda i: (0, 0, i))
    return pl.pallas_call(
        body, name="rope_dkv", grid=(NCH,),
        in_specs=[tspec, tspec, pl.BlockSpec((128, 128), lambda i: (i, 0)), pl.BlockSpec((128, 128), lambda i: (i, 0)),
                  pl.BlockSpec(memory_space=pl.ANY)],
        out_specs=pl.BlockSpec((128, 2 * KVW), lambda i: (i, D // (2 * KVW))),
        out_shape=jax.ShapeDtypeStruct(dproj.shape, BF16), input_output_aliases={4: 0},
        compiler_params=_cparams("parallel"),
    )(dkt, dvt, cos, sin, dproj)


def _rope_kv(proj, cos, sin):
    def body(t_ref, c_ref, s_ref, k_ref, v_ref, kt_ref, vt_ref):
        t = t_ref[...]
        k = _rotate(t[:, :KVW], c_ref[...], s_ref[...], 1.0, 1.0).astype(BF16)
        v = t[:, KVW:].astype(BF16)
        k_ref[...] = k
        v_ref[...] = v
        kt, vt = k.astype(F32).T, v.astype(F32).T
        for j in range(NKV):
            kt_ref[j] = kt[HD * j:HD * (j + 1)].astype(BF16)
            vt_ref[j] = vt[HD * j:HD * (j + 1)].astype(BF16)

    rows = pl.BlockSpec((128, KVW), lambda i: (i, 0))
    tspec = pl.BlockSpec((NKV, HD, 128), lambda i: (0, 0, i))
    return pl.pallas_call(
        body, name="rope_kv", grid=(NCH,),
        in_specs=[pl.BlockSpec((128, 2 * KVW), lambda i: (i, D // (2 * KVW))),
                  pl.BlockSpec((128, 128), lambda i: (i, 0)), pl.BlockSpec((128, 128), lambda i: (i, 0))],
        out_specs=[rows, rows, tspec, tspec],
        out_shape=[jax.ShapeDtypeStruct((S, KVW), BF16)] * 2 + [jax.ShapeDtypeStruct((NKV, HD, S), BF16)] * 2,
        compiler_params=_cparams("parallel"),
    )(proj, cos, sin)


QROWS = NQ_PER_KV * 128


NBIAS = NCH + 1
KV_PER_STEP = 4


def _bias_table():
    db = lax.broadcasted_iota(jnp.int32, (NBIAS, 128, QROWS), 0) - 1
    ki = lax.broadcasted_iota(jnp.int32, (NBIAS, 128, QROWS), 1)
    qi = lax.broadcasted_iota(jnp.int32, (NBIAS, 128, QROWS), 2) & 127
    d = db * 128 + qi - ki
    cnt = ((d <= 128).astype(F32) + (((d & 3) == 0) & (d <= 512)).astype(F32) + ((d & 15) == 0).astype(F32))
    return jnp.where((d >= 0) & (cnt > 0.0), jnp.log(jnp.maximum(cnt, 1.0)), NEG)


def _attn_fwd(qt, kh, vt, bias, rider=None):
    def body(q_ref, k_ref, v_ref, b_ref, o_ref, lse_ref, rows_ref, m_ref, l_ref, acc_ref):
        qb = pl.program_id(1)
        m_ref[...] = jnp.full_like(m_ref, NEG)
        l_ref[...] = jnp.zeros_like(l_ref)
        acc_ref[...] = jnp.zeros_like(acc_ref)

        def keys(off, size, bias_):
            for h in range(KV_PER_STEP):
                m = m_ref[h]
                s = _dot(k_ref[h, pl.ds(off, size), :], q_ref[h], NN) + bias_
                m_new = jnp.maximum(m, jnp.max(s, axis=0, keepdims=True))
                p = jnp.exp(s - m_new)
                a = jnp.exp(m - m_new)
                m_ref[h] = m_new
                l_ref[h] = a * l_ref[h] + jnp.sum(p, axis=0, keepdims=True)
                acc_ref[h] = a * acc_ref[h] + _dot(v_ref[h, :, pl.ds(off, size)], p, NN)

        def blocks(first, count):
            bias_ = jnp.concatenate([b_ref[qb - first - j + 1] for j in range(count)], axis=0)
            keys(pl.multiple_of(first * 128, 128), 128 * count, bias_)

        nkb = qb + 1
        @pl.loop(0, nkb // 4)
        def _(i):
            blocks(4 * i, 4)

        @pl.when(nkb % 4 >= 2)
        def _():
            blocks(nkb // 4 * 4, 2)

        @pl.when(nkb % 2 == 1)
        def _():
            blocks(qb, 1)

        outs = []
        for h in range(KV_PER_STEP):
            outs.append(acc_ref[h] / l_ref[h])
            o_ref[h] = outs[h]
            lse_ref[h] = m_ref[h] + jnp.log(l_ref[h])
        rows_ref[...] = _blocks_to_rows(outs).astype(BF16)

    kvs = KV_PER_STEP
    qspec = pl.BlockSpec((kvs, None, HD, QROWS), lambda j, i: (j, i, 0, 0))
    return _call(
        body, "attn_fwd", (NKV // kvs, NCH),
        [qspec, pl.BlockSpec((kvs, S, HD), lambda j, i: (j, 0, 0)),
         pl.BlockSpec((kvs, HD, S), lambda j, i: (j, 0, 0)),
         pl.BlockSpec((NBIAS, 128, QROWS), lambda j, i: (0, 0, 0))],
        [qspec, pl.BlockSpec((kvs, None, 1, QROWS), lambda j, i: (j, i, 0, 0)),
         pl.BlockSpec((128, QCOLS * kvs), lambda j, i: (i, j))],
        [jax.ShapeDtypeStruct((NKV, NCH, HD, QROWS), F32), jax.ShapeDtypeStruct((NKV, NCH, 1, QROWS), F32),
         jax.ShapeDtypeStruct((S, 2 * D), BF16)],
        (qt, kh, vt, bias),
        [pltpu.VMEM((kvs, 1, QROWS), F32), pltpu.VMEM((kvs, 1, QROWS), F32), pltpu.VMEM((kvs, HD, QROWS), F32)],
        ("parallel", "parallel"), rider)


def _attn_bwd(qt, kh, kt, vh, dot_, lse, delta, bias, rider=None):
    def body(qt_ref, k_ref, kt_ref, v_ref, dot_ref, lse_ref, dl_ref, b_ref, dq_ref, dk_ref, dv_ref):
        kp = pl.program_id(1)

        @pl.when(kp == 0)
        def _():
            dq_ref[...] = jnp.zeros_like(dq_ref)

        dk_ref[...] = jnp.zeros_like(dk_ref)
        dv_ref[...] = jnp.zeros_like(dv_ref)

        @pl.loop(2 * kp, NCH // 2)
        def _(j):
            for h in range(KV_PER_STEP):
                k, kt_, v = k_ref[h], kt_ref[h], v_ref[h]
                for qb in (2 * j, 2 * j + 1):
                    bias2 = jnp.concatenate([b_ref[jnp.maximum(qb - 4 * kp - t + 1, 0)] for t in range(4)], axis=0)
                    st = _dot(k, qt_ref[h, qb], NN) + bias2
                    pt = jnp.exp(st - lse_ref[h, qb])
                    dst = pt * (_dot(v, dot_ref[h, qb], NN) - dl_ref[h, qb])
                    dq_ref[h, qb] += _dot(kt_, dst, NN)
                    dk_ref[h] += _dot(qt_ref[h, qb], dst, NT)
                    dv_ref[h] += _dot(dot_ref[h, qb], pt, NT)

    kvs = KV_PER_STEP
    tspec = pl.BlockSpec((kvs, NCH, HD, QROWS), lambda j, i: (j, 0, 0, 0))
    kspec = pl.BlockSpec((kvs, 512, HD), lambda j, i: (j, i, 0))
    ktspec = pl.BlockSpec((kvs, HD, 512), lambda j, i: (j, 0, i))
    sspec = pl.BlockSpec((kvs, NCH, 1, QROWS), lambda j, i: (j, 0, 0, 0))
    return _call(
        body, "attn_bwd", (NKV // kvs, NCH // 4),
        [tspec, kspec, ktspec, kspec, tspec, sspec, sspec,
         pl.BlockSpec((NBIAS, 128, QROWS), lambda j, i: (0, 0, 0))],
        [tspec, ktspec, ktspec],
        [jax.ShapeDtypeStruct((NKV, NCH, HD, QROWS), F32),
         jax.ShapeDtypeStruct((NKV, HD, S), F32), jax.ShapeDtypeStruct((NKV, HD, S), F32)],
        (qt, kh, kt, vh, dot_, lse, delta, bias), sem=("parallel", "arbitrary"), rider=rider)


CONV_BLK = 256
CONV_COL0 = 1536 // CONV_BLK


CONV_ROWS = 128


def _conv_fwd(proj, convw, convb):
    trips = S // CONV_ROWS

    def body(u_ref, w_ref, b_ref, o_ref, y_ref):
        @pl.loop(0, trips)
        def _(c):
            t0 = pl.multiple_of(c * CONV_ROWS, CONV_ROWS)
            before = pl.multiple_of(jnp.maximum(t0 - 8, 0), 8)
            ext = jnp.concatenate([jnp.where(c == 0, 0.0, u_ref[pl.ds(before, 8), :]),
                                   u_ref[pl.ds(t0, CONV_ROWS), :]], axis=0)
            y = b_ref[...] + w_ref[CONV_K - 1:CONV_K, :] * ext[8:]
            for j in range(1, CONV_K):
                y = y + w_ref[CONV_K - 1 - j:CONV_K - j, :] * pltpu.roll(ext, j, 0)[8:]
            y_ref[pl.ds(t0, CONV_ROWS), :] = y
            o_ref[pl.ds(t0, CONV_ROWS), :] = y * _sigmoid(y)

    out = pl.BlockSpec((S, CONV_BLK), lambda i: (0, i))
    return pl.pallas_call(
        body, name="conv_fwd", grid=(CONV_C // CONV_BLK,),
        in_specs=[pl.BlockSpec((S, CONV_BLK), lambda i: (0, CONV_COL0 + i)),
                  pl.BlockSpec((CONV_K, CONV_BLK), lambda i: (0, i)),
                  pl.BlockSpec((1, CONV_BLK), lambda i: (0, i))],
        out_specs=[out, out], out_shape=[jax.ShapeDtypeStruct((S, CONV_C), F32)] * 2,
        compiler_params=_cparams("parallel"),
    )(proj, convw, convb)


def _conv_bwd(dact, ypre, proj, convw, dproj):
    trips = S // CONV_ROWS

    def body(da_ref, y_ref, u_ref, w_ref, buf_ref, du_ref, dw_ref, db_ref):
        dw_ref[...] = jnp.zeros_like(dw_ref)
        db_ref[...] = jnp.zeros_like(db_ref)
        r8 = lax.broadcasted_iota(jnp.int32, (8, CONV_BLK), 0)

        def dy_of(rows):
            y = y_ref[rows, :]
            sg = _sigmoid(y)
            return da_ref[rows, :] * (sg * (1.0 + y * (1.0 - sg)))

        @pl.loop(0, trips)
        def _(c):
            t0 = pl.multiple_of(c * CONV_ROWS, CONV_ROWS)
            after = pl.multiple_of(jnp.minimum(t0 + CONV_ROWS, S - 8), 8)
            ext = jnp.concatenate([dy_of(pl.ds(t0, CONV_ROWS)),
                                   jnp.where(c == trips - 1, 0.0, dy_of(pl.ds(after, 8)))], axis=0)
            u = u_ref[pl.ds(t0, CONV_ROWS), :]
            du, dw = None, jnp.zeros((8, CONV_BLK), F32)
            for j in range(CONV_K):
                dyj = (ext if j == 0 else pltpu.roll(ext, CONV_ROWS + 8 - j, 0))[:CONV_ROWS]
                term = w_ref[CONV_K - 1 - j:CONV_K - j, :] * dyj
                du = term if du is None else du + term
                dw = dw + jnp.where(r8 == CONV_K - 1 - j, jnp.sum(dyj * u, axis=0, keepdims=True), 0.0)
            du_ref[pl.ds(t0, CONV_ROWS), :] = du.astype(BF16)
            dw_ref[...] += dw
            db_ref[...] += jnp.sum(ext[:CONV_ROWS], axis=0, keepdims=True)

    return pl.pallas_call(
        body, name="conv_bwd", grid=(CONV_C // CONV_BLK,),
        in_specs=[pl.BlockSpec((S, CONV_BLK), lambda i: (0, i)), pl.BlockSpec((S, CONV_BLK), lambda i: (0, i)),
                  pl.BlockSpec((S, CONV_BLK), lambda i: (0, CONV_COL0 + i)),
                  pl.BlockSpec((CONV_K, CONV_BLK), lambda i: (0, i)), pl.BlockSpec(memory_space=pl.ANY)],
        out_specs=[pl.BlockSpec((S, CONV_BLK), lambda i: (0, CONV_COL0 + i)),
                   pl.BlockSpec((8, CONV_BLK), lambda i: (0, i)), pl.BlockSpec((1, CONV_BLK), lambda i: (0, i))],
        out_shape=[jax.ShapeDtypeStruct(dproj.shape, BF16), jax.ShapeDtypeStruct((8, CONV_C), F32),
                   jax.ShapeDtypeStruct((1, CONV_C), F32)],
        input_output_aliases={4: 0}, compiler_params=_cparams("parallel"),
    )(dact, ypre, proj, convw, dproj)


NPAIR = 8


def _ssd_scalars(dtr_ref, dtb_ref, alog_ref):
    z = dtr_ref[...] + dtb_ref[...]
    dt = jnp.maximum(z, 0.0) + jnp.log(1.0 + jnp.exp(-jnp.abs(z)))
    a = -jnp.exp(alog_ref[...])
    r = lax.broadcasted_iota(jnp.int32, (128, 128), 0)
    c = lax.broadcasted_iota(jnp.int32, (128, 128), 1)
    tri = (r >= c).astype(F32)
    cs = _dot_exact(tri, dt * a)
    return z, dt, a, cs, r, c


def _by_lane(cs, dt):
    head = lax.broadcasted_iota(jnp.int32, (128, SSM_W), 0)
    lane = lax.broadcasted_iota(jnp.int32, (128, SSM_W), 1)
    sel = (head == lane // HD).astype(F32)
    cs_l = _dot_exact(cs, sel, "b")
    last_l = cs_l[127:128, :]
    return sel, jnp.exp(cs_l), jnp.exp(last_l - cs_l), _dot_exact(dt, sel, "b")


def _pair_terms(cs, h1, h2):
    return (cs[:, h1:h1 + 1], cs[:, h2:h2 + 1],
            jnp.exp(cs[127:128, h1:h1 + 1]), jnp.exp(cs[127:128, h2:h2 + 1]))


def _gate_norm(y, zv, w):
    yg = y * (zv * _sigmoid(zv))
    outs, rs = [], []
    for g in range(2):
        blk = yg[:, 512 * g:512 * (g + 1)]
        r = lax.rsqrt(jnp.mean(blk * blk, axis=-1, keepdims=True) + EPS)
        outs.append(blk * r)
        rs.append(r)
    return jnp.concatenate(outs, axis=1), rs, yg


def _ssd_fwd(xbc, proj, dtb, alog, dskip_l, ssmw, mixed):
    def body(x_ref, b_ref, c_ref, dtr_ref, z_ref, dtb_ref, alog_ref, dsk_ref, w_ref, buf_ref,
             y_ref, yn_ref, hp_ref, h_ref):
        @pl.when(pl.program_id(0) == 0)
        def _():
            h_ref[...] = jnp.zeros_like(h_ref)

        _, dt, _, cs, r, c = _ssd_scalars(dtr_ref, dtb_ref, alog_ref)
        cst = cs.T
        causal = r >= c
        lo = c < HD
        _, e_all, dte_all, dt_all = _by_lane(cs, dt)
        hp_ref[...] = h_ref[...]
        for g in range(2):
            bg = b_ref[:, 128 * g:128 * (g + 1)]
            cg = c_ref[:, 128 * g:128 * (g + 1)]
            cb = _dot(cg, bg, NT)
            for j in range(4):
                pj = 4 * g + j
                h1, h2 = 2 * pj, 2 * pj + 1
                sl = slice(128 * pj, 128 * (pj + 1))
                xp = x_ref[:, sl]
                c1, c2, cd1, cd2 = _pair_terms(cs, h1, h2)
                e_l, dte_l = e_all[:, sl], dte_all[:, sl]
                xdt = xp * dt_all[:, sl]
                m1 = cb * jnp.exp(jnp.where(causal, c1 - cst[h1:h1 + 1, :], NEG))
                m2 = cb * jnp.exp(jnp.where(causal, c2 - cst[h2:h2 + 1, :], NEG))
                yd = jnp.where(lo, _dot(m1, xdt, NN), _dot(m2, xdt, NN))
                hp = h_ref[pj]
                yo = _dot(cg, hp, NT) * e_l
                st = _dot(xdt * dte_l, bg, TN)
                h_ref[pj] = hp * jnp.where(r < HD, cd1, cd2) + st
                y_ref[:, sl] = yd + yo + dsk_ref[:, sl] * xp
        yn, _, _ = _gate_norm(y_ref[...], z_ref[...], w_ref[...])
        yn_ref[...] = (yn * w_ref[...]).astype(BF16)

    return pl.pallas_call(
        body, name="ssd_fwd", grid=(NCH,),
        in_specs=[pl.BlockSpec((128, SSM_W), lambda i: (i, 0)),
                  pl.BlockSpec((128, 256), lambda i: (i, 4)), pl.BlockSpec((128, 256), lambda i: (i, 5)),
                  pl.BlockSpec((128, 128), lambda i: (i, COL_DT // 128)),
                  pl.BlockSpec((128, SSM_W), lambda i: (i, 3)),
                  pl.BlockSpec((1, 128), lambda i: (0, 0)), pl.BlockSpec((1, 128), lambda i: (0, 0)),
                  pl.BlockSpec((1, SSM_W), lambda i: (0, 0)), pl.BlockSpec((1, SSM_W), lambda i: (0, 0)),
                  pl.BlockSpec(memory_space=pl.ANY)],
        out_specs=[pl.BlockSpec((128, SSM_W), lambda i: (i, 0)), pl.BlockSpec((128, SSM_W), lambda i: (i, 1)),
                   pl.BlockSpec((None, NPAIR, 128, 128), lambda i: (i, 0, 0, 0))],
        out_shape=[jax.ShapeDtypeStruct((S, SSM_W), F32), jax.ShapeDtypeStruct(mixed.shape, BF16),
                   jax.ShapeDtypeStruct((NCH, NPAIR, 128, 128), F32)],
        scratch_shapes=[pltpu.VMEM((NPAIR, 128, 128), F32)],
        input_output_aliases={9: 1}, compiler_params=_cparams("arbitrary"),
    )(xbc, xbc, xbc, proj, proj, dtb, alog, dskip_l, ssmw, mixed)


def _ssd_bwd(dmixed, y, xbc, proj, hprev, dtb, alog, dskip_l, ssmw, rider=None):
    def body(dyn_ref, y_ref, x_ref, b_ref, c_ref, dtr_ref, z_ref, hp_ref, dtb_ref, alog_ref, dsk_ref, w_ref,
             dxbc_ref, dz_ref, ddt_ref, dw_ref, dsc_ref, g_ref):
        @pl.when(pl.program_id(0) == 0)
        def _():
            g_ref[...] = jnp.zeros_like(g_ref)
            dsc_ref[...] = jnp.zeros_like(dsc_ref)

        z, dt, a, cs, r, c = _ssd_scalars(dtr_ref, dtb_ref, alog_ref)
        cst = cs.T
        causal = r >= c
        lo = c < HD

        yv = y_ref[...]
        zv = z_ref[...]
        wv = w_ref[...]
        ygn, rs, yg = _gate_norm(yv, zv, wv)
        dyn = dyn_ref[...]
        _acc_rows(dw_ref, dyn * ygn)
        dynw = dyn * wv
        parts = []
        for g in range(2):
            sl = slice(512 * g, 512 * (g + 1))
            a_g, n_g = dynw[:, sl], ygn[:, sl]
            parts.append(rs[g] * (a_g - n_g * jnp.mean(a_g * n_g, axis=-1, keepdims=True)))
        dyg = jnp.concatenate(parts, axis=1)
        sz = _sigmoid(zv)
        dz_ref[...] = (dyg * yv * (sz * (1.0 + zv * (1.0 - sz)))).astype(BF16)
        dy_all = dyg * (zv * sz)

        dcs_cols = jnp.zeros((128, 128), F32)
        dcs_rows = jnp.zeros((128, 128), F32)
        sel, e_all, dte_all, dt_all = _by_lane(cs, dt)
        x_all, b_all, c_all, dsk_all = x_ref[...], b_ref[...], c_ref[...], dsk_ref[...]
        hp_all, g_all = hp_ref[...], g_ref[...]
        g_new, dx_parts, db_parts, dc_parts = [], [], [], []
        dyx_parts, ryo_parts, qx_parts, dxx_parts, gh_parts = [], [], [], [], []
        for g in range(2):
            bg = b_all[:, 128 * g:128 * (g + 1)]
            cg = c_all[:, 128 * g:128 * (g + 1)]
            cb = _dot(cg, bg, NT)
            dcb = jnp.zeros((128, 128), F32)
            db_acc = jnp.zeros((128, NST), F32)
            dc_acc = jnp.zeros((128, NST), F32)
            for j in range(4):
                pj = 4 * g + j
                h1, h2 = 2 * pj, 2 * pj + 1
                sl = slice(128 * pj, 128 * (pj + 1))
                xp = x_all[:, sl]
                dyp = dy_all[:, sl]
                c1, c2, cd1, cd2 = _pair_terms(cs, h1, h2)
                e_l, dte_l, dt_l = e_all[:, sl], dte_all[:, sl], dt_all[:, sl]
                xdt = xp * dt_l
                hp = hp_all[pj]
                gp = g_all[pj]
                dyx_parts.append(dyp * xp)
                dzs = dyp * e_l
                dc_acc = dc_acc + _dot(dzs, hp, NN)
                ryo_parts.append(dyp * (_dot(cg, hp, NT) * e_l))
                qm = _dot(bg, gp, NT)
                dxdt = qm * dte_l
                qx_parts.append(qm * xdt)
                db_acc = db_acc + _dot(xdt * dte_l, gp, NN)
                gh_parts.append(gp * hp)
                g_new.append(_dot(dzs, cg, TN) + jnp.where(r < HD, cd1, cd2) * gp)
                for hh, ch, msk in ((h1, c1, lo), (h2, c2, jnp.logical_not(lo))):
                    lm = jnp.exp(jnp.where(causal, ch - cst[hh:hh + 1, :], NEG))
                    mm = cb * lm
                    dm = jnp.where(causal, _dot(jnp.where(msk, dyp, 0.0), xdt, NT), 0.0)
                    w = dm * mm
                    dcs_cols = dcs_cols + jnp.where(c == hh, jnp.sum(w, axis=1, keepdims=True), 0.0)
                    dcs_rows = dcs_rows + jnp.where(r == hh, jnp.sum(w, axis=0, keepdims=True), 0.0)
                    dcb = dcb + dm * lm
                    dxdt = dxdt + jnp.where(msk, _dot(mm, dyp, TN), 0.0)
                dxx_parts.append(dxdt * xp)
                dx_parts.append(dsk_all[:, sl] * dyp + dxdt * dt_l)
            db_parts.append(db_acc + _dot(dcb, cg, TN))
            dc_parts.append(dc_acc + _dot(dcb, bg, NN))
        g_ref[...] = jnp.stack(g_new)
        dxbc_ref[...] = jnp.concatenate(dx_parts + db_parts + dc_parts, axis=1)

        selt = (lax.broadcasted_iota(jnp.int32, (SSM_W, 128), 0) // HD
                == lax.broadcasted_iota(jnp.int32, (SSM_W, 128), 1)).astype(F32)

        def by_head(parts):
            return _dot_exact(jnp.concatenate(parts, axis=1), selt, "b")

        ddt_x = by_head(dxx_parts)
        dd_row = jnp.sum(by_head(dyx_parts), axis=0, keepdims=True)
        t_all = by_head(qx_parts) * jnp.exp(cs[127:128, :] - cs)
        gh = jnp.sum(_dot_exact(sel, jnp.concatenate(gh_parts, axis=0)), axis=1, keepdims=True)
        gh_row = jnp.broadcast_to(gh, (128, 128)).T[0:1, :]
        at_end = jnp.sum(t_all, axis=0, keepdims=True) + gh_row * jnp.exp(cs[127:128, :])
        dcs = by_head(ryo_parts) - t_all + dcs_cols + jnp.where(r == 127, at_end, 0.0) - dcs_rows.T
        dad = _dot_exact((c >= r).astype(F32), dcs)
        ddt = dad * a + ddt_x
        ddtr = jnp.where(c < 16, ddt * _sigmoid(z), 0.0)
        ddt_ref[...] = ddtr.astype(BF16)
        r8 = lax.broadcasted_iota(jnp.int32, (8, 128), 0)
        dsc_ref[...] += (jnp.where(r8 == 0, jnp.sum(ddtr, axis=0, keepdims=True), 0.0)
                         + jnp.where(r8 == 1, jnp.sum(dad * dt, axis=0, keepdims=True) * a, 0.0)
                         + jnp.where(r8 == 2, dd_row, 0.0))

    rev = NCH - 1
    return _call(
        body, "ssd_bwd", (NCH,),
        [pl.BlockSpec((128, SSM_W), lambda i: (rev - i, 0)),
         pl.BlockSpec((128, SSM_W), lambda i: (rev - i, 0)),
         pl.BlockSpec((128, SSM_W), lambda i: (rev - i, 0)),
         pl.BlockSpec((128, 256), lambda i: (rev - i, 4)), pl.BlockSpec((128, 256), lambda i: (rev - i, 5)),
         pl.BlockSpec((128, 128), lambda i: (rev - i, COL_DT // 128)),
         pl.BlockSpec((128, SSM_W), lambda i: (rev - i, 3)),
         pl.BlockSpec((None, NPAIR, 128, 128), lambda i: (rev - i, 0, 0, 0)),
         pl.BlockSpec((1, 128), lambda i: (0, 0)), pl.BlockSpec((1, 128), lambda i: (0, 0)),
         pl.BlockSpec((1, SSM_W), lambda i: (0, 0)), pl.BlockSpec((1, SSM_W), lambda i: (0, 0))],
        [pl.BlockSpec((128, CONV_C), lambda i: (rev - i, 0)),
         pl.BlockSpec((128, SSM_W), lambda i: (rev - i, 3)),
         pl.BlockSpec((128, 128), lambda i: (rev - i, 0)),
         pl.BlockSpec((1, SSM_W), lambda i: (0, 0)), pl.BlockSpec((8, 128), lambda i: (0, 0))],
        [jax.ShapeDtypeStruct((S, CONV_C), F32), jax.ShapeDtypeStruct((S, WIN_PAD), BF16),
         jax.ShapeDtypeStruct((S, 128), BF16), jax.ShapeDtypeStruct((1, SSM_W), F32),
         jax.ShapeDtypeStruct((8, 128), F32)],
        (dmixed, y, xbc, xbc, xbc, proj, proj, hprev, dtb, alog, dskip_l, ssmw),
        [pltpu.VMEM((NPAIR, 128, 128), F32)], ("arbitrary",), rider)


def _cast_stack(name, slot, arrs, tr, tc):
    n = len(arrs)
    rows, cols = arrs[0].shape

    def body(s_ref, *refs):
        for i in range(n):
            refs[n][i] = refs[i][...].astype(BF16)

    return pl.pallas_call(
        body, name=name,
        grid_spec=pltpu.PrefetchScalarGridSpec(
            num_scalar_prefetch=1, grid=(rows // tr, cols // tc),
            in_specs=[pl.BlockSpec((tr, tc), lambda i, j, sr: (i, j))] * n,
            out_specs=pl.BlockSpec((None, n, tr, tc), lambda i, j, sr: (sr[0], 0, i, j))),
        out_shape=jax.ShapeDtypeStruct((NSH, n, rows, cols), BF16),
        compiler_params=_cparams("parallel", "parallel"),
    )(slot, *arrs)


def _pair_sum(name, c_idx, ps, th):
    n = len(ps)
    _, rows, _ = ps[0].shape

    def body(c_ref, *refs):
        mine, whole, out, theirs = refs[:n], refs[n:2 * n], refs[2 * n:3 * n], refs[3 * n:4 * n]
        send, recv = refs[4 * n], refs[4 * n + 1]
        s, i = pl.program_id(0), pl.program_id(1)
        x, y, c, _ = _place()

        def copies(slot):
            return [_rcopy(whole[k].at[slot, :, pl.ds((1 - c) * HALF, HALF)], theirs[k].at[slot],
                           send.at[slot * n + k], recv.at[slot * n + k], (x, y, 1 - c)) for k in range(n)]

        @pl.when((s == 0) & (i == 0))
        def _():
            for slot in range(NSH):
                for cp in copies(slot):
                    cp.start()

        @pl.when(i == 0)
        def _():
            for slot in range(NSH):
                @pl.when(s == slot)
                def _():
                    for cp in copies(slot):
                        cp.wait()

        rows_i = slice(None) if th == rows else pl.ds(pl.multiple_of(i * th, th), th)
        for k in range(n):
            out[k][...] = (mine[k][...].astype(F32) + theirs[k][s, rows_i, :].astype(F32)).astype(BF16)

    spec = pl.BlockSpec((None, th, HALF), lambda s, i, cr: (s, i, 0))
    return pl.pallas_call(
        body, name=name,
        grid_spec=pltpu.PrefetchScalarGridSpec(
            num_scalar_prefetch=1, grid=(NSH, rows // th),
            in_specs=[pl.BlockSpec((None, th, HALF), lambda s, i, cr: (s, i, cr[0]))] * n + _any_specs(n),
            out_specs=[spec] * n,
            scratch_shapes=[pltpu.VMEM((NSH, rows, HALF), BF16)] * n
            + [pltpu.SemaphoreType.DMA((NSH * n,)), pltpu.SemaphoreType.DMA((NSH * n,))]),
        out_shape=[jax.ShapeDtypeStruct((NSH, rows, HALF), BF16)] * n,
        compiler_params=_cparams("arbitrary", "arbitrary"),
    )(c_idx, *ps, *ps)


def _pair_add(name, c_idx, ps, theirs, th):
    n = len(ps)
    _, rows, _ = ps[0].shape

    def body(c_ref, *refs):
        for k in range(n):
            refs[2 * n + k][...] = (refs[k][...].astype(F32) + refs[n + k][...].astype(F32)).astype(BF16)

    spec = pl.BlockSpec((None, th, HALF), lambda s, i, cr: (s, i, 0))
    return pl.pallas_call(
        body, name=name,
        grid_spec=pltpu.PrefetchScalarGridSpec(
            num_scalar_prefetch=1, grid=(NSH, rows // th),
            in_specs=[pl.BlockSpec((None, th, HALF), lambda s, i, cr: (s, i, cr[0]))] * n + [spec] * n,
            out_specs=[spec] * n),
        out_shape=[jax.ShapeDtypeStruct((NSH, rows, HALF), BF16)] * n,
        compiler_params=_cparams("parallel", "parallel"),
    )(c_idx, *ps, *theirs)


def _chip_sum(name, place, cs, ts, th):
    n = len(ts)
    _, rows, _ = ts[0].shape

    def body(p_ref, *refs):
        for i in range(n):
            t = refs[n + i][...].astype(F32)
            refs[2 * n + i][...] = ((refs[i][...].astype(F32) + t[0]) + t[1]) + t[2]

    return pl.pallas_call(
        body, name=name,
        grid_spec=pltpu.PrefetchScalarGridSpec(
            num_scalar_prefetch=1, grid=(rows // th,),
            in_specs=[pl.BlockSpec((None, th, HALF), lambda i, pr: (pr[0], i, 0))] * n
            + [pl.BlockSpec((3, th, HALF), lambda i, pr: (0, i, 0))] * n,
            out_specs=[pl.BlockSpec((th, HALF), lambda i, pr: (i, pr[1]))] * n),
        out_shape=[jax.ShapeDtypeStruct((rows, D), F32)] * n, compiler_params=_cparams("parallel"),
    )(place, *cs, *ts)


def _adamw(name, ws, gs, ms, vs, tr, tc):
    n = len(ws)
    shape = ws[0].shape
    rows, cols, mid = shape[0], shape[-1], shape[1:-1]
    c1 = 1.0 / (1.0 - ADAM_B1 ** ADAM_STEP)
    c2 = 1.0 / (1.0 - ADAM_B2 ** ADAM_STEP)

    def body(*refs):
        for i in range(n):
            w, g, m, v = (refs[k * n + i][...] for k in range(4))
            m2 = ADAM_B1 * m + (1.0 - ADAM_B1) * g
            v2 = ADAM_B2 * v + (1.0 - ADAM_B2) * (g * g)
            refs[4 * n + 4 * i][...] = -ADAM_LR * ((m2 * c1) / (jnp.sqrt(v2 * c2) + ADAM_EPS) + ADAM_WD * w)
            refs[4 * n + 4 * i + 1][...] = m2
            refs[4 * n + 4 * i + 2][...] = v2
            refs[4 * n + 4 * i + 3][...] = g

    spec = pl.BlockSpec((tr,) + mid + (tc,), lambda i, j: (i,) + (0,) * len(mid) + (j,))
    outs = pl.pallas_call(
        body, name=name, grid=(rows // tr, cols // tc), in_specs=[spec] * (4 * n), out_specs=[spec] * (4 * n),
        out_shape=[jax.ShapeDtypeStruct(shape, F32)] * (4 * n),
        compiler_params=_cparams("parallel", "parallel"),
    )(*ws, *gs, *ms, *vs)
    return [tuple(outs[4 * i:4 * i + 4]) for i in range(n)]


def _place():
    x, y, c = lax.axis_index("x"), lax.axis_index("y"), lax.axis_index("c")
    chips = [(1 - x, y), (x, 1 - y), (1 - x, 1 - y)]
    return x, y, c, chips


def _any_specs(n):
    return [pl.BlockSpec(memory_space=pl.ANY)] * n


def _rcopy(src, dst, send_sem, recv_sem, dev):
    return pltpu.make_async_remote_copy(src_ref=src, dst_ref=dst, send_sem=send_sem, recv_sem=recv_sem,
                                        device_id=dev, device_id_type=MESH)


QUARTER = HALF // 2

XA, XB, YA, YB, RX, RY, F_XA, F_XB, F_YA, F_YB, F_D0, F_D1 = range(12)


def _gather_rider(bufs, views, at=None):
    n = len(bufs)

    def plan(rout, sems):
        send, recv = sems
        x, y, c, _ = _place()
        me, sx, sy, sd = 2 * x + y, 2 * (1 - x) + y, 2 * x + (1 - y), 2 * (1 - x) + (1 - y)
        nx, ny, sib = (1 - x, y, c), (x, 1 - y, c), (x, y, 1 - c)
        q0, q1 = c * HALF, c * HALF + QUARTER
        o0, o1 = (1 - c) * HALF, (1 - c) * HALF + QUARTER
        out = {XA: (me, q1, nx), XB: (me, q0, nx), YA: (me, q0, ny), YB: (me, q1, ny),
               RX: (sy, q0, nx), RY: (sx, q1, ny),
               F_XA: (sx, q1, sib), F_XB: (sx, q0, sib), F_YA: (sy, q0, sib), F_YB: (sy, q1, sib),
               F_D0: (sd, q0, sib), F_D1: (sd, q1, sib)}
        inn = {XA: (sx, q1), XB: (sx, q0), YA: (sy, q0), YB: (sy, q1), RX: (sd, q0), RY: (sd, q1),
               F_XA: (sx, o1), F_XB: (sx, o0), F_YA: (sy, o0), F_YB: (sy, o1), F_D0: (sd, o0), F_D1: (sd, o1)}

        def copy(kind, b):
            slot, col, dev = out[kind]
            win = views[b](rout[b], slot, col, QUARTER)
            return _rcopy(win, win, send.at[kind * n + b], recv.at[kind * n + b], dev)

        def land(kind, b):
            slot, col = inn[kind]
            win = views[b](rout[b], slot, col, QUARTER)
            return _rcopy(win, win, send.at[kind * n + b], recv.at[kind * n + b], (x, y, c))

        return copy, land

    first = (XA, YA, XB, YB)
    early = ((XA, (RY, F_XA)), (YA, (RX, F_YA)))
    late = ((XB, (F_XB,)), (YB, (F_YB,)), (RX, (F_D0,)), (RY, (F_D1,)))
    forwards = (F_XA, F_XB, F_YA, F_YB, F_D0, F_D1)
    sent = first + (RX, RY) + forwards

    def start(rin, rout, sems):
        copy, _ = plan(rout, sems)
        for kind in first:
            for b in range(n):
                copy(kind, b).start()

    def pass_on(rout, sems, links):
        copy, land = plan(rout, sems)
        for landed, then in links:
            for b in range(n):
                land(landed, b).wait_recv()
                for kind in then:
                    copy(kind, b).start()

    def between(rin, rout, sems):
        pass_on(rout, sems, early)

    def finish(rin, rout, sems):
        pass_on(rout, sems, late)
        copy, land = plan(rout, sems)
        for kind in forwards:
            for b in range(n):
                land(kind, b).wait_recv()
        for kind in sent:
            for b in range(n):
                copy(kind, b).wait_send()

    return _Rider(list(bufs), [jax.ShapeDtypeStruct(a.shape, a.dtype) for a in bufs], {b: b for b in range(n)},
                  [pltpu.SemaphoreType.DMA((12 * n,))] * 2, start, finish, between, at)


def _small_gather_rider(cw):
    def descs(rin, rout, sems, x, y, c, chips):
        return [_rcopy(rin[0], rout[0].at[2 * x + y], sems[1].at[j], sems[2].at[j], (chip[0], chip[1], c))
                for j, chip in enumerate(chips)]

    def start(rin, rout, sems):
        x, y, c, chips = _place()
        pltpu.make_async_copy(rin[0], rout[0].at[2 * x + y], sems[0].at[0]).start()
        for cp in descs(rin, rout, sems, x, y, c, chips):
            cp.start()

    def finish(rin, rout, sems):
        x, y, c, chips = _place()
        for j, chip in enumerate(chips):
            _rcopy(rin[0], rout[0].at[2 * chip[0] + chip[1]], sems[1].at[j], sems[2].at[j], (x, y, c)).wait_recv()
        for cp in descs(rin, rout, sems, x, y, c, chips):
            cp.wait_send()
        pltpu.make_async_copy(rin[0], rout[0].at[2 * x + y], sems[0].at[0]).wait()

    return _Rider([cw], [jax.ShapeDtypeStruct((NSH,) + cw.shape, cw.dtype)], {},
                  [pltpu.SemaphoreType.DMA((1,)), pltpu.SemaphoreType.DMA((3,)), pltpu.SemaphoreType.DMA((3,))],
                  start, finish)


def _to_sibling_rider(ps):
    n = len(ps)

    def descs(rin, rout, sems):
        x, y, c, _ = _place()
        return [_rcopy(rin[i].at[:, :, pl.ds((1 - c) * HALF, HALF)], rout[i], sems[0].at[i], sems[1].at[i],
                       (x, y, 1 - c)) for i in range(n)]

    def start(rin, rout, sems):
        for cp in descs(rin, rout, sems):
            cp.start()

    def finish(rin, rout, sems):
        for cp in descs(rin, rout, sems):
            cp.wait()

    return _Rider(list(ps), [jax.ShapeDtypeStruct(a.shape[:2] + (HALF,), a.dtype) for a in ps], {},
                  [pltpu.SemaphoreType.DMA((n,))] * 2, start, finish)


def _to_chips_rider(cs, first=0, count=None, into=None):
    n = len(cs)
    rows = [pl.ds(first, a.shape[1] - first if count is None else count) for a in cs]

    def descs(rin, rout, sems):
        x, y, c, chips = _place()
        return [_rcopy(rin[i].at[2 * chip[0] + chip[1], rows[i]], rout[i].at[j, rows[i]], sems[0].at[j * n + i],
                       sems[1].at[j * n + i], (chip[0], chip[1], c)) for j, chip in enumerate(chips) for i in range(n)]

    def start(rin, rout, sems):
        for cp in descs(rin, rout, sems):
            cp.start()

    def finish(rin, rout, sems):
        for cp in descs(rin, rout, sems):
            cp.wait()

    return _Rider(list(cs) + list(into or []), [jax.ShapeDtypeStruct((3,) + a.shape[1:], a.dtype) for a in cs],
                  {n + i: i for i in range(n)} if into else {}, [pltpu.SemaphoreType.DMA((3 * n,))] * 2, start, finish)


def _join_riders(riders):
    counts = [[len(r.operands) for r in riders], [len(r.out_shapes) for r in riders], [len(r.sems) for r in riders]]

    def each(step):
        def run(*refs):
            at = [0, 0, 0]
            for i, r in enumerate(riders):
                parts = [group[at[k]:at[k] + counts[k][i]] for k, group in enumerate(refs)]
                at = [at[k] + counts[k][i] for k in range(3)]
                step(r)(*parts)
        return run

    aliases = {sum(counts[0][:i]) + k: sum(counts[1][:i]) + v
               for i, r in enumerate(riders) for k, v in r.aliases.items()}
    return _Rider([a for r in riders for a in r.operands], [s for r in riders for s in r.out_shapes], aliases,
                  [s for r in riders for s in r.sems], each(lambda r: r.start), each(lambda r: r.finish),
                  each(lambda r: r.between or (lambda *refs: None)))


SMALL_ROWS = 16


def _swap_halves(gs, vec):
    n = len(gs)

    def body(*refs):
        v_ref, dst, o_ref = refs[n], refs[n + 1:2 * n + 1], refs[2 * n + 1]
        buf, send, recv, vsend, vrecv = refs[2 * n + 2:]
        x, y, c, _ = _place()
        cps = []
        for i in range(n):
            mine = dst[i].at[:, pl.ds(c * HALF, HALF)]
            cps.append(_rcopy(mine, mine, send.at[i], recv.at[i], (x, y, 1 - c)))
        for cp in cps:
            cp.start()

        me = 4 * x + 2 * y + c
        buf[me] = v_ref[...]
        vcps = []
        for k in range(1, 8):
            peer = (x ^ (k >> 2), y ^ ((k >> 1) & 1), c ^ (k & 1))
            vcps.append(_rcopy(v_ref, buf.at[me], vsend.at[k - 1], vrecv.at[k - 1], peer))
        for cp in vcps:
            cp.start()
        for k in range(1, 8):
            _rcopy(v_ref, buf.at[me ^ k], vsend.at[k - 1], vrecv.at[k - 1], (x, y, c)).wait_recv()
        for cp in vcps:
            cp.wait_send()
        t = buf[0]
        for d in range(1, 8):
            t = t + buf[d]
        o_ref[...] = t

        for i in range(n):
            other = dst[i].at[:, pl.ds((1 - c) * HALF, HALF)]
            _rcopy(other, other, send.at[i], recv.at[i], (x, y, c)).wait_recv()
        for cp in cps:
            cp.wait_send()

    vmem = pl.BlockSpec(memory_space=pltpu.VMEM)
    res = pl.pallas_call(
        body, name="grads_swap_halves", in_specs=_any_specs(n) + [vmem], out_specs=_any_specs(n) + [vmem],
        out_shape=[jax.ShapeDtypeStruct(g.shape, g.dtype) for g in gs] + [jax.ShapeDtypeStruct((SMALL_ROWS, D), F32)],
        input_output_aliases={i: i for i in range(n)},
        scratch_shapes=[pltpu.VMEM((8, SMALL_ROWS, D), F32)] + [pltpu.SemaphoreType.DMA((n,))] * 2
        + [pltpu.SemaphoreType.DMA((7,))] * 2,
    )(*gs, vec)
    return list(res[:n]), res[n]


def _col_window(ref, slot, col, ncols):
    return ref.at[slot, :, pl.ds(col, ncols)]


def _stack_window(first, count):
    def view(ref, slot, col, ncols):
        return ref.at[slot, pl.ds(first, count), :, pl.ds(col, ncols)]
    return view


def _row_tile(rows):
    for t in range(512, 15, -16):
        if rows % t == 0:
            return t
    return rows


def _same_shape_runs(arrs):
    runs, a = [], 0
    for b in range(1, len(arrs) + 1):
        if b == len(arrs) or arrs[b].shape != arrs[a].shape:
            runs.append((a, b))
            a = b
    return runs


class _Comm:
    def __init__(self):
        x, y, c = lax.axis_index("x"), lax.axis_index("y"), lax.axis_index("c")
        self.c_idx = jnp.reshape(c, (1,)).astype(jnp.int32)
        self.shard = jnp.reshape(2 * x + y, (1,)).astype(jnp.int32)
        self.place = jnp.stack([2 * x + y, c]).astype(jnp.int32)
        self.groups = {}
        self.sent = {}

    @staticmethod
    def gather(*bufs, part=None, at=None):
        views = [_col_window if b.ndim == 3 else _stack_window(*(part or (0, b.shape[1]))) for b in bufs]
        return _gather_rider(list(bufs), views, at)

    def reduce_rider(self, tag, names, ps, theirs=None, rows=None):
        csums = []
        for a, b in _same_shape_runs(ps):
            name, th = "pair_sum_%s%d" % (tag, a), _row_tile(ps[a].shape[1])
            csums += (_pair_sum(name, self.c_idx, ps[a:b], th) if theirs is None else
                      _pair_add(name, self.c_idx, ps[a:b], theirs[a:b], th))
        self.groups[tag] = [names, csums, None]
        self.sent[tag] = rows
        return _to_chips_rider(csums, 0, rows)

    def rest_rider(self, tag):
        _, csums, ts = self.groups[tag]
        return _to_chips_rider(csums, self.sent[tag], None, ts)

    def landed(self, tag, ts):
        self.groups[tag][2] = ts

    def finish(self, small):
        names, csums, ts = [], [], []
        for group_names, group_csums, group_ts in self.groups.values():
            names += group_names
            csums += group_csums
            ts += group_ts
        order = sorted(range(len(names)), key=lambda i: csums[i].shape[1])
        names, csums, ts = ([v[i] for i in order] for v in (names, csums, ts))
        halves = []
        for a, b in _same_shape_runs(csums):
            halves += _chip_sum("chip_sum_%d" % a, self.place, csums[a:b], ts[a:b], _row_tile(csums[a].shape[1]))
        grads, total = _swap_halves(halves, small)
        return dict(zip(names, grads)), total


ROPE_THETA = 10000.0
SMALL_1K = ("ffn1_pre_norm", "ffn1_post_norm", "mix_pre_norm", "ssm_norm", "mix_post_norm",
            "ffn2_pre_norm", "ffn2_post_norm")
SMALL_16 = ("dt_bias", "a_log", "d_skip")
OFF_CONVB = 7 * D
OFF_16 = OFF_CONVB + CONV_C
OFF_CONVW = OFF_16 + 48
OFF_LOSS = OFF_CONVW + CONV_K * CONV_C
SMALL_LEN = SMALL_ROWS * D


def _sds(shape, dtype):
    return jax.ShapeDtypeStruct(shape, dtype)


def _ridden(res, rider):
    return res if rider is not None else (res, None)


def _ffn_down(name, act, w, tail_of, rider=None):
    tail, o_specs, o_shapes = tail_of(TS)
    return _mm(name, [act, w.dn], NN, (S // TS,),
               [pl.BlockSpec((NSH, TS, FS), lambda i: (0, i, 0)),
                pl.BlockSpec((NSH, None, FS, D), lambda i: (0, w.d0, 0, 0))], o_specs, o_shapes, rider, tail)


def _ffn_dw(name, a, b, rider=None):
    return _mm(name, [a, b], TN, (NSH,),
               [pl.BlockSpec((None, S, FS), lambda s: (s, 0, 0)), pl.BlockSpec((S, D), lambda s: (0, 0))],
               pl.BlockSpec((None, FS, D), lambda s: (s, 0, 0)), _sds((NSH, FS, D), BF16), rider)


def _ffn_dn(name, dgate, dup, w, tail_of, rider=None):
    rows = TS // 2
    tail, o_specs, o_shapes = tail_of(rows)
    a2 = pl.BlockSpec((NSH, rows, FS), lambda i: (0, i, 0))
    return _mm(name, [dgate, w.gu, dup, w.gu], NN, (S // rows,),
               [a2, pl.BlockSpec((NSH, None, FS, D), lambda i: (0, w.g0, 0, 0)),
                a2, pl.BlockSpec((NSH, None, FS, D), lambda i: (0, w.g0 + 1, 0, 0))], o_specs, o_shapes, rider, tail)


def _out_proj_dx(dh, wout, ot):
    def body(dh_ref, w_ref, o_ref, dyn_ref, do_ref, dl_ref):
        dm = _dot(dh_ref[...], w_ref[...], NT)
        dyn_ref[...] = dm[:, D:]
        for b in range(TS // 128):
            for j, blk in enumerate(_rows_to_blocks(dm[128 * b:128 * (b + 1), :D])):
                do = blk.astype(BF16)
                do_ref[j, b] = do
                dl_ref[j, b] = jnp.sum(o_ref[j, b] * do.astype(F32), axis=0, keepdims=True)

    blocks = pl.BlockSpec((NKV, TS // 128, HD, QROWS), lambda i: (0, i, 0, 0))
    return pl.pallas_call(
        body, name="out_proj_dx", grid=(S // TS,),
        in_specs=[pl.BlockSpec((TS, D), lambda i: (i, 0)), pl.BlockSpec((2 * D, D), lambda i: (0, 0)), blocks],
        out_specs=[pl.BlockSpec((TS, D), lambda i: (i, 0)), blocks,
                   pl.BlockSpec((NKV, TS // 128, 1, QROWS), lambda i: (0, i, 0, 0))],
        out_shape=[_sds((S, D), F32), _sds((NKV, NCH, HD, QROWS), BF16), _sds((NKV, NCH, 1, QROWS), F32)],
        compiler_params=_cparams("parallel"),
    )(dh, wout, ot)


def _heads(t, n):
    return t.reshape(S, n, HD).transpose(1, 0, 2)


def _pad128(v):
    return jnp.pad(v, ((0, 0), (0, 128 - v.shape[1])))


def _local_step(x, positions, tgt, sp, gu1, d1, f2, wint, wout, convw, comm=None):
    inv_freq = ROPE_THETA ** (-jnp.arange(0, HD, 2, dtype=F32) / HD)
    ang = positions.astype(F32)[:, None] * inv_freq
    ang = jnp.concatenate([ang, ang, ang, ang], axis=-1)
    cos, sin = jnp.cos(ang), jnp.sin(ang)
    dtb, alog = _pad128(sp["dt_bias"]), _pad128(sp["a_log"])
    dskip_l = jnp.repeat(sp["d_skip"], HD, axis=1)
    convb = sp["conv_b"]

    if comm:
        rider = _join_riders([comm.gather(gu1), _small_gather_rider(convw)])
        (n1, d1, f2, wint, wout), (gu1, convw) = _prenorm_casts(
            "prenorm1", x, sp["ffn1_pre_norm"], comm.shard, [(d1, 0), (f2, 0), (wint, 1), (wout, 0)], rider)
        wint, wout = wint.reshape(NSH, WIN_SH, D), wout.reshape(NSH, 2 * D // NSH, D)
        convw = convw.transpose(1, 0, 2).reshape(CONV_K, CONV_C)
    else:
        n1 = _prenorm("prenorm1", x, sp["ffn1_pre_norm"])
    rider = comm.gather(d1) if comm else None
    (fg1, fu1, act1), got = _ridden(_ffn_up("ffn1_up", n1, _FfnW(gu1, 0, d1, 0), rider), rider)
    if comm:
        d1, = got
    w1 = _FfnW(gu1, 0, d1, 0)
    rider = comm.gather(wint) if comm else None
    (h1, x1, n2), got = _ridden(_ffn_down(
        "ffn1_down", act1, w1,
        lambda rows: _tail_postres(rows, x, sp["ffn1_post_norm"], 0.5, sp["mix_pre_norm"]), rider), rider)
    if comm:
        wint, = got
    wint_pad = jnp.pad(wint.reshape(WIN_COLS, D), ((0, WIN_PAD - WIN_COLS), (0, 0)))

    pw = WIN_PAD // 3
    rider = comm.gather(f2, part=(0, 1)) if comm else None
    proj, got = _ridden(_mm(
        "in_proj", [n2, wint_pad], NT, (S // TS, 3),
        [pl.BlockSpec((TS, D), lambda i, j: (i, 0)), pl.BlockSpec((pw, D), lambda i, j: (j, 0))],
        pl.BlockSpec((TS, pw), lambda i, j: (i, j)), _sds((S, WIN_PAD), F32), rider), rider)
    if comm:
        f2, = got
    qt = _rope_q(proj, cos, sin)
    k_rot, v_bf, kt, vt = _rope_kv(proj, cos, sin)
    kh, vh = _heads(k_rot, NKV), _heads(v_bf, NKV)
    bias = _bias_table()
    rider = comm.gather(f2, wout, part=(1, 2), at=9) if comm else None
    (ot, lse, mixed), got = _ridden(_attn_fwd(qt, kh, vt, bias, rider), rider)
    if comm:
        f2, wout = got
    w2 = _FfnW(f2, 0, f2, 2)
    wout = wout.reshape(2 * D, D)
    xbc, conv_y = _conv_fwd(proj, convw, convb)
    y, mixed, hprev = _ssd_fwd(xbc, proj, dtb, alog, dskip_l, sp["ssm_norm"], mixed)
    tail, o_specs, o_shapes = _tail_postres(TS, x1, sp["mix_post_norm"], 1.0, sp["ffn2_pre_norm"])
    h2, x2, n3 = _mm("out_proj", [mixed, wout], NN, (S // TS,),
                     [pl.BlockSpec((TS, 2 * D), lambda i: (i, 0)), pl.BlockSpec((2 * D, D), lambda i: (0, 0))],
                     o_specs, o_shapes, None, tail)

    fg2, fu2, act2 = _ffn_up("ffn2_up", n3, w2)
    dy, dh3, dp3, loss = _ffn_down(
        "ffn2_down", act2, w2, lambda rows: _tail_final(rows, x2, sp["ffn2_post_norm"], tgt, 0.5))

    dgate2, dup2 = _ffn_dact("ffn2_dact", dh3, w2, fg2, fu2)
    dws2 = [_ffn_dw("ffn2_dwg", dgate2, n3), _ffn_dw("ffn2_dwu", dup2, n3), _ffn_dw("ffn2_dwd", act2, dh3)]
    dx2, dh2, dg3, dp2 = _ffn_dn(
        "ffn2_dn", dgate2, dup2, w2,
        lambda rows: _tail_mid_bwd(rows, dy, x2, sp["ffn2_pre_norm"], h2, sp["mix_post_norm"], 1.0))

    dyn, dot_, delta = _out_proj_dx(dh2, wout, ot)
    dwout = _mm("out_proj_dw", [mixed, dh2], TN, (2,),
                [pl.BlockSpec((S, D), lambda m: (0, m)), pl.BlockSpec((S, D), lambda m: (0, 0))],
                pl.BlockSpec((D, D), lambda m: (m, 0)), _sds((2 * D, D), BF16))
    dwout = dwout.reshape(NSH, 2 * D // NSH, D)

    def riding(tag, names, ps, call, theirs=None, rows=None):
        rider = None
        if comm:
            rider = comm.rest_rider(tag) if names is None else comm.reduce_rider(tag, names, ps, theirs, rows)
        res, got = _ridden(call(rider), rider)
        if comm:
            comm.landed(tag, got)
        return res

    rider = _to_sibling_rider(dws2 + [dwout]) if comm else None
    (dxbc, dproj, ddt, dssm, dsc), theirs = _ridden(
        _ssd_bwd(dyn, y, xbc, proj, hprev, dtb, alog, dskip_l, sp["ssm_norm"], rider), rider)
    dproj, dcw8, dcb = _conv_bwd(dxbc, conv_y, proj, convw, dproj)
    dqt, dkh, dvh = riding("a", BIG[3:6] + ("w_out",), dws2 + [dwout], lambda rider: _attn_bwd(
        qt, kh, kt, vh, dot_, lse, delta, bias, rider), theirs)
    dproj = _rope_dq(dqt, cos, sin, dproj)
    dproj = _rope_dkv(dkh, dvh, cos, sin, dproj)
    dproj = lax.dynamic_update_slice(dproj, ddt, (0, COL_DT))
    dwint = _mm("in_proj_dw", [dproj, n2], TN, (3,),
                [pl.BlockSpec((S, pw), lambda j: (0, j)), pl.BlockSpec((S, D), lambda j: (0, 0))],
                pl.BlockSpec((pw, D), lambda j: (j, 0)), _sds((WIN_PAD, D), BF16))
    dwint = dwint[:WIN_COLS].reshape(NSH, WIN_SH, D)

    tail, o_specs, o_shapes = _tail_mid_bwd(TS, dx2, x1, sp["mix_pre_norm"], h1, sp["ffn1_post_norm"], 0.5)
    dx1, dh1, dg2, dp1 = riding("b", ("w_in",), [dwint], lambda rider: _mm(
        "in_proj_dx", [dproj, wint_pad], NN, (S // TS,),
        [pl.BlockSpec((TS, WIN_PAD), lambda i: (i, 0)), pl.BlockSpec((WIN_PAD, D), lambda i: (0, 0))],
        o_specs, o_shapes, rider, tail), rows=W_IN_FIRST)

    dwd1 = riding("b", None, None, lambda rider: _ffn_dw("ffn1_dwd", act1, dh1, rider))
    dgate1, dup1 = riding("d", BIG[2:3], [dwd1], lambda rider: _ffn_dact("ffn1_dact", dh1, w1, fg1, fu1, rider))
    dwg1 = _ffn_dw("ffn1_dwg", dgate1, n1)
    dwu1 = riding("g", BIG[0:1], [dwg1], lambda rider: _ffn_dw("ffn1_dwu", dup1, n1, rider), rows=FS // 4)
    rider = _join_riders([comm.rest_rider("g"), comm.reduce_rider("u", BIG[1:2], [dwu1])]) if comm else None
    (grad_x, dg1), got = _ridden(_ffn_dn(
        "ffn1_dn", dgate1, dup1, w1, lambda rows: _tail_first_bwd(rows, dx1, x, sp["ffn1_pre_norm"]), rider), rider)
    if comm:
        comm.landed("g", got[:1])
        comm.landed("u", got[1:])
    dws1 = [dwg1, dwu1, dwd1]

    small = jnp.concatenate([
        dg1[0], dp1[0], dg2[0], dssm[0], dp2[0], dg3[0], dp3[0], dcb[0],
        dsc[0, :16], dsc[1, :16], dsc[2, :16], dcw8[:CONV_K].reshape(-1), loss[0, :1]])
    small = jnp.pad(small, (0, SMALL_LEN - small.shape[0])).reshape(SMALL_ROWS, D)
    if comm is None:
        return grad_x, dws1 + dws2 + [dwint, dwout], small
    return (grad_x,) + comm.finish(small)


WEIGHTS = ("ffn1_pre_norm", "ffn1_w_gate", "ffn1_w_up", "ffn1_w_down", "ffn1_post_norm", "mix_pre_norm", "w_in",
           "conv_w", "conv_b", "dt_bias", "a_log", "d_skip", "ssm_norm", "w_out", "mix_post_norm", "ffn2_pre_norm",
           "ffn2_w_gate", "ffn2_w_up", "ffn2_w_down", "ffn2_post_norm")
BIG = ("ffn1_w_gate", "ffn1_w_up", "ffn1_w_down", "ffn2_w_gate", "ffn2_w_up", "ffn2_w_down", "w_in", "w_out")
TRANSPOSED = ("ffn1_w_gate", "ffn1_w_up", "ffn2_w_gate", "ffn2_w_up", "w_in")
SMALL_ORDER = SMALL_1K + ("conv_b",) + SMALL_16
CONVW_SH = CONV_C // NSH


def _shard2d(t, name):
    return t[0].T if name in TRANSPOSED else t[0]


def _unshard2d(t, name):
    return (t.T if name in TRANSPOSED else t)[None]


def _rows3d(t):
    return t.transpose(2, 0, 1)


def _pack_small(d, prefix, shard_of_convw):
    flat = jnp.concatenate([d[prefix + n][0] for n in SMALL_ORDER] + [shard_of_convw.reshape(-1)])
    return jnp.pad(flat, (0, SMALL_LEN - flat.shape[0])).reshape(SMALL_ROWS, D)


def _unpack_small(block, like):
    flat = block.reshape(-1)
    out, off = {}, 0
    for n in SMALL_ORDER:
        size = like[n].shape[1]
        out[n] = flat[off:off + size].reshape(1, size)
        off += size
    out["conv_w"] = flat[off:off + CONV_K * CONVW_SH].reshape(1, CONV_K, CONVW_SH)
    return out


def kernel(x, positions, ffn1_pre_norm, ffn1_w_gate, ffn1_w_up, ffn1_w_down, ffn1_post_norm, mix_pre_norm, w_in, conv_w, conv_b, dt_bias, a_log, d_skip, ssm_norm, w_out, mix_post_norm, ffn2_pre_norm, ffn2_w_gate, ffn2_w_up, ffn2_w_down, ffn2_post_norm, loss_target, m_ffn1_pre_norm, m_ffn1_w_gate, m_ffn1_w_up, m_ffn1_w_down, m_ffn1_post_norm, m_mix_pre_norm, m_w_in, m_conv_w, m_conv_b, m_dt_bias, m_a_log, m_d_skip, m_ssm_norm, m_w_out, m_mix_post_norm, m_ffn2_pre_norm, m_ffn2_w_gate, m_ffn2_w_up, m_ffn2_w_down, m_ffn2_post_norm, v_ffn1_pre_norm, v_ffn1_w_gate, v_ffn1_w_up, v_ffn1_w_down, v_ffn1_post_norm, v_mix_pre_norm, v_w_in, v_conv_w, v_conv_b, v_dt_bias, v_a_log, v_d_skip, v_ssm_norm, v_w_out, v_mix_post_norm, v_ffn2_pre_norm, v_ffn2_w_gate, v_ffn2_w_up, v_ffn2_w_down, v_ffn2_post_norm):
    given = dict(locals())
    xi, yi = lax.axis_index("x"), lax.axis_index("y")

    comm = _Comm()
    big = {p + n: _shard2d(given[p + n], n) for n in BIG for p in ("", "m_", "v_")}
    gu1 = _cast_stack("cast_ffn1_gate_up", comm.shard, [big[n] for n in BIG[0:2]], 176, D)

    sp = {n: given[n] for n in SMALL_ORDER}
    grad_x, big_grads, small = _local_step(
        x[0], positions[0], loss_target[0], sp, gu1, [big[BIG[2]]], [big[n] for n in BIG[3:6]], [big["w_in"]],
        [big["w_out"]], conv_w[0], comm)

    tot = small.reshape(-1)
    loss = tot[OFF_LOSS]
    small_grads, off = {}, 0
    for n in SMALL_ORDER:
        size = given[n].shape[1]
        small_grads[n] = tot[off:off + size].reshape(1, size)
        off += size
    dconvw = tot[OFF_CONVW:OFF_CONVW + CONV_K * CONV_C].reshape(CONV_K, NSH, CONVW_SH)
    dconvw = lax.dynamic_index_in_dim(dconvw, 2 * xi + yi, axis=1, keepdims=False)
    small_grads["conv_w"] = dconvw.reshape(1, CONV_K, CONVW_SH)

    upd = {}
    for names, tr in ((BIG[0:3], 176), (BIG[3:6], 176), (BIG[7:8], 256)):
        res = _adamw("adamw_" + names[0], [big[n] for n in names], [big_grads[n] for n in names],
                     [big["m_" + n] for n in names], [big["v_" + n] for n in names], tr, D)
        for n, r in zip(names, res):
            upd[n] = tuple(_unshard2d(t, n) for t in r)
    g_win = big_grads["w_in"].reshape(WIN_SH, 1, D)
    res, = _adamw("adamw_w_in", [_rows3d(w_in)], [g_win], [_rows3d(m_w_in)], [_rows3d(v_w_in)], WIN_SH // 4, D)
    upd["w_in"] = tuple(t.transpose(1, 2, 0) for t in res)
    (dl, m2, v2, _), = _adamw(
        "adamw_small", [_pack_small(given, "", conv_w[0])], [_pack_small(small_grads, "", dconvw)],
        [_pack_small(given, "m_", m_conv_w[0])], [_pack_small(given, "v_", v_conv_w[0])], SMALL_ROWS, D)
    dl, m2, v2 = (_unpack_small(t, given) for t in (dl, m2, v2))
    for n in SMALL_ORDER + ("conv_w",):
        upd[n] = (dl[n], m2[n], v2[n], small_grads[n])

    return (loss, grad_x[None], *[upd[n][3] for n in WEIGHTS], *[upd[n][0] for n in WEIGHTS],
            *[upd[n][1] for n in WEIGHTS], *[upd[n][2] for n in WEIGHTS])
```

```python
import functools
import typing

import jax
import jax.numpy as jnp
from jax import lax
from jax.experimental import pallas as pl
from jax.experimental.pallas import tpu as pltpu

F32 = jnp.float32
BF16 = jnp.bfloat16

S = 2048
D = 1024
FF = 2816
NSH = 4
FS = FF // NSH
HALF = D // 2
HD = 64
NKV = 4
NQ_PER_KV = 4
KVW = NKV * HD
QCOLS = NQ_PER_KV * HD
CONV_C = 1536
CONV_K = 4
SSM_W = 1024
NST = 128
NCH = S // 128
WIN_COLS = 4112
WIN_SH = WIN_COLS // NSH
W_IN_FIRST = 768
WIN_PAD = 4224
COL_DT = 4096
EPS = 1e-6
NEG = -1e30

ADAM_LR = 0.001
ADAM_B1 = 0.9
ADAM_B2 = 0.999
ADAM_EPS = 1e-08
ADAM_WD = 0.01
ADAM_STEP = 10

VMEM_LIMIT = 56 * 1024 * 1024
TS = 512
TR = 256

NN = (((1,), (0,)), ((), ()))
NT = (((1,), (1,)), ((), ()))
TN = (((0,), (0,)), ((), ()))
MESH = pl.DeviceIdType.MESH


def _cparams(*sem):
    return pltpu.CompilerParams(dimension_semantics=sem, vmem_limit_bytes=VMEM_LIMIT)


def _dot(a, b, dims):
    return lax.dot_general(a.astype(BF16), b.astype(BF16), dims, preferred_element_type=F32)


def _bf16_pieces(v):
    hi = v.astype(BF16)
    rest = v - hi.astype(F32)
    mid = rest.astype(BF16)
    return hi, mid, (rest - mid.astype(F32)).astype(BF16)


def _dot_exact(a, b, ones="a"):
    if ones == "a":
        sel = a.astype(BF16)
        parts = [lax.dot_general(sel, p, NN, preferred_element_type=F32) for p in _bf16_pieces(b)]
    else:
        sel = b.astype(BF16)
        parts = [lax.dot_general(p, sel, NN, preferred_element_type=F32) for p in _bf16_pieces(a)]
    return (parts[2] + parts[1]) + parts[0]


def _sigmoid(v):
    return 1.0 / (1.0 + jnp.exp(-v))


class _Rider(typing.NamedTuple):
    operands: list
    out_shapes: list
    aliases: dict
    sems: list
    start: typing.Callable
    finish: typing.Callable
    between: typing.Callable = None
    at: int = None


def _call(body, name, grid, in_specs, out_specs, out_shape, operands, scratch=(), sem=(), rider=None, prefetch=0):
    multi = isinstance(out_shape, (list, tuple))

    def launch(kernel, in_specs, out_specs, out_shape, scratch, aliases, sem, args):
        if prefetch:
            how = dict(grid_spec=pltpu.PrefetchScalarGridSpec(
                num_scalar_prefetch=prefetch, grid=grid, in_specs=in_specs, out_specs=out_specs,
                scratch_shapes=scratch))
        else:
            how = dict(grid=grid, in_specs=in_specs, out_specs=out_specs, scratch_shapes=scratch)
        return pl.pallas_call(kernel, name=name, out_shape=out_shape, input_output_aliases=aliases,
                              compiler_params=_cparams(*sem), **how)(*args)

    if rider is None:
        return launch(body, in_specs, out_specs, out_shape, list(scratch), {}, sem, operands)
    outs = list(out_shape) if multi else [out_shape]
    ospecs = list(out_specs) if multi else [out_specs]
    n_in, n_out, n_scr = len(operands) - prefetch, len(outs), len(scratch)
    ri, ro = len(rider.operands), len(rider.out_shapes)

    def wrapped(*refs):
        scalars, refs = refs[:prefetch], refs[prefetch:]
        o0 = n_in + ri
        s0 = o0 + n_out + ro
        rin, rout, rsem = refs[n_in:o0], refs[o0 + n_out:s0], refs[s0 + n_scr:]
        ids = [pl.program_id(a) for a in range(len(grid))]
        first = functools.reduce(jnp.logical_and, [i == 0 for i in ids])
        last = functools.reduce(jnp.logical_and, [i == g - 1 for i, g in zip(ids, grid)])

        @pl.when(first)
        def _():
            rider.start(rin, rout, rsem)

        if rider.between is not None:
            steps = functools.reduce(lambda a, b: a * b, grid)
            step = functools.reduce(lambda a, ig: a * ig[1] + ig[0], zip(ids, grid), 0)

            @pl.when(step == (steps // 3 if rider.at is None else rider.at))
            def _():
                rider.between(rin, rout, rsem)

        body(*scalars, *refs[:n_in], *refs[o0:o0 + n_out], *refs[s0:s0 + n_scr])

        @pl.when(last)
        def _():
            rider.finish(rin, rout, rsem)

    hbm = pl.BlockSpec(memory_space=pl.ANY)
    res = launch(wrapped, list(in_specs) + [hbm] * ri, ospecs + [hbm] * ro, outs + list(rider.out_shapes),
                 list(scratch) + list(rider.sems),
                 {prefetch + n_in + k: n_out + v for k, v in rider.aliases.items()},
                 ("arbitrary",) * len(grid), (*operands, *rider.operands))
    main = list(res[:n_out])
    return (main if multi else main[0]), list(res[n_out:])


class _Tail(typing.NamedTuple):
    fn: typing.Callable
    operands: list
    in_specs: list


def _mm(name, operands, dims, grid, in_specs, o_spec, out_shape, rider=None, tail=None):
    npairs = len(operands) // 2
    extra = [] if tail is None else list(tail.operands)
    nin = 2 * npairs + len(extra)

    def body(*refs):
        t = None
        for i in range(npairs):
            a, b = refs[2 * i], refs[2 * i + 1]
            parts = [(a[s], b[s]) for s in range(a.shape[0])] if len(a.shape) == 3 else [(a[...], b[...])]
            for pa, pb in parts:
                d = _dot(pa, pb, dims)
                t = d if t is None else t + d
        if tail is None:
            refs[nin][...] = t.astype(refs[nin].dtype)
        else:
            tail.fn(t, refs[2 * npairs:nin], refs[nin:])

    sem = ("parallel" if tail is None else "arbitrary",) * len(grid)
    specs = list(in_specs) + ([] if tail is None else list(tail.in_specs))
    return _call(body, name, grid, specs, o_spec, out_shape, list(operands) + extra, (), sem, rider)


class _FfnW(typing.NamedTuple):
    gu: jax.Array
    g0: int
    dn: jax.Array
    d0: int


def _ffn_up(name, n, w, rider=None):
    def body(n_ref, wg_ref, wu_ref, fg_ref, fu_ref, a_ref):
        nb = n_ref[...]
        g = _dot(nb, wg_ref[...], NT)
        u = _dot(nb, wu_ref[...], NT)
        sg = _sigmoid(g)
        silu = g * sg
        fg_ref[...] = (u * (sg * (1.0 + g * (1.0 - sg)))).astype(BF16)
        fu_ref[...] = silu.astype(BF16)
        a_ref[...] = (silu * u).astype(BF16)

    out = jax.ShapeDtypeStruct((NSH, S, FS), BF16)
    ospec = pl.BlockSpec((None, TS, FS), lambda s, i: (s, i, 0))
    return _call(
        body, name, (NSH, S // TS),
        [pl.BlockSpec((TS, D), lambda s, i: (i, 0)),
         pl.BlockSpec((None, None, FS, D), lambda s, i: (s, w.g0, 0, 0)),
         pl.BlockSpec((None, None, FS, D), lambda s, i: (s, w.g0 + 1, 0, 0))],
        [ospec, ospec, ospec], [out, out, out], (n, w.gu, w.gu), sem=("parallel", "parallel"), rider=rider)


def _ffn_dact(name, dh, w, fgate, fup, rider=None):
    def body(dh_ref, wd_ref, fg_ref, fu_ref, dg_ref, du_ref):
        da = _dot(dh_ref[...], wd_ref[...], NT)
        dg_ref[...] = (da * fg_ref[...].astype(F32)).astype(BF16)
        du_ref[...] = (da * fu_ref[...].astype(F32)).astype(BF16)

    out = jax.ShapeDtypeStruct((NSH, S, FS), BF16)
    aspec = pl.BlockSpec((None, TS, FS), lambda s, i: (s, i, 0))
    return _call(
        body, name, (NSH, S // TS),
        [pl.BlockSpec((TS, D), lambda s, i: (i, 0)),
         pl.BlockSpec((None, None, FS, D), lambda s, i: (s, w.d0, 0, 0)), aspec, aspec],
        [aspec, aspec], [out, out], (dh, w.dn, fgate, fup), sem=("parallel", "parallel"), rider=rider)


def _rstd(v):
    return lax.rsqrt(jnp.mean(v * v, axis=-1, keepdims=True) + EPS)


def _row_spec():
    return pl.BlockSpec((TR, D), lambda i: (i, 0))


def _vec_spec():
    return pl.BlockSpec((1, D), lambda i: (0, 0))


def _acc_rows(ref, v):
    @pl.when(pl.program_id(0) == 0)
    def _():
        ref[...] = jnp.zeros_like(ref)
    ref[...] += jnp.sum(v, axis=0, keepdims=True)


def _prenorm(name, x, g):
    def body(x_ref, g_ref, n_ref):
        xv = x_ref[...]
        n_ref[...] = (xv * _rstd(xv) * g_ref[...]).astype(BF16)

    return pl.pallas_call(
        body, name=name, grid=(S // TR,), in_specs=[_row_spec(), _vec_spec()], out_specs=_row_spec(),
        out_shape=jax.ShapeDtypeStruct((S, D), BF16), compiler_params=_cparams("parallel"),
    )(x, g)


ENTRY_STEPS = 4


def _prenorm_casts(name, x, g, ang, slot, groups, rider):
    def body(s_ref, x_ref, g_ref, ang_ref, *refs):
        ins, outs = refs[:len(refs) - len(groups) - 4], refs[len(refs) - len(groups) - 4:]
        xv = x_ref[...]
        outs[0][...] = (xv * _rstd(xv) * g_ref[...]).astype(BF16)
        at = 0
        for (arrs, _), out in zip(groups, outs[1:]):
            for k in range(len(arrs)):
                out[k] = ins[at + k][...].astype(BF16)
            at += len(arrs)
        outs[-3][...] = jnp.cos(ang_ref[...])
        outs[-2][...] = jnp.sin(ang_ref[...])

        @pl.when(pl.program_id(0) == 0)
        def _():
            outs[-1][...] = _bias_table()

    rows = pl.BlockSpec((S // ENTRY_STEPS, D), lambda i, sr: (i, 0))
    trig = pl.BlockSpec((S // ENTRY_STEPS, 128), lambda i, sr: (i, 0))
    in_specs, out_specs, out_shapes = [rows, pl.BlockSpec((1, D), lambda i, sr: (0, 0)), trig], [rows], [_rows_bf16()]
    for arrs, axis in groups:
        r, c = arrs[0].shape
        if axis == 0:
            blk, at, at_out = (r // ENTRY_STEPS, c), (lambda i, sr: (i, 0)), (lambda i, sr: (sr[0], 0, i, 0))
        else:
            blk, at, at_out = (r, c // ENTRY_STEPS), (lambda i, sr: (0, i)), (lambda i, sr: (sr[0], 0, 0, i))
        in_specs += [pl.BlockSpec(blk, at)] * len(arrs)
        out_specs.append(pl.BlockSpec((None, len(arrs)) + blk, at_out))
        out_shapes.append(jax.ShapeDtypeStruct((NSH, len(arrs), r, c), BF16))
    out_specs += [trig, trig, pl.BlockSpec((NBIAS, 128, QROWS), lambda i, sr: (0, 0, 0))]
    out_shapes += [jax.ShapeDtypeStruct((S, 128), F32)] * 2 + [jax.ShapeDtypeStruct((NBIAS, 128, QROWS), F32)]
    return _call(body, name, (ENTRY_STEPS,), in_specs, out_specs, out_shapes,
                 [slot, x, g, ang] + [a for arrs, _ in groups for a in arrs], rider=rider, prefetch=1)


def _rows_spec(rows):
    return pl.BlockSpec((rows, D), lambda i: (i, 0))


def _rows_f32():
    return jax.ShapeDtypeStruct((S, D), F32)


def _rows_bf16():
    return jax.ShapeDtypeStruct((S, D), BF16)


def _vec_f32():
    return jax.ShapeDtypeStruct((1, D), F32)


def _tail_postres(rows, x, p, alpha, gnext):
    def fn(h, ins, outs):
        x_ref, p_ref, g_ref = ins
        h_ref, xo_ref, n_ref = outs
        h_ref[...] = h
        xo = x_ref[...] + alpha * (h * _rstd(h) * p_ref[...])
        xo_ref[...] = xo
        n_ref[...] = (xo * _rstd(xo) * g_ref[...]).astype(BF16)

    rs = _rows_spec(rows)
    return (_Tail(fn, [x, p, gnext], [rs, _vec_spec(), _vec_spec()]), [rs, rs, rs],
            [_rows_f32(), _rows_f32(), _rows_bf16()])


def _tail_final(rows, x, p, tgt, alpha):
    def fn(h, ins, outs):
        x_ref, p_ref, t_ref = ins
        dy_ref, dh_ref, dp_ref, loss_ref = outs
        r = _rstd(h)
        hn = h * r
        pv = p_ref[...]
        e = x_ref[...] + alpha * (hn * pv) - t_ref[...]
        dy = e * (1.0 / D)
        dy_ref[...] = dy
        du = alpha * dy * pv
        dh_ref[...] = (r * (du - hn * jnp.mean(du * hn, axis=-1, keepdims=True))).astype(BF16)
        _acc_rows(dp_ref, alpha * dy * hn)
        part = 0.5 * jnp.sum(jnp.mean(e * e, axis=-1, keepdims=True), axis=0, keepdims=True)
        _acc_rows(loss_ref, jnp.broadcast_to(part, (1, 128)))

    rs = _rows_spec(rows)
    return (_Tail(fn, [x, p, tgt], [rs, _vec_spec(), rs]),
            [rs, rs, _vec_spec(), pl.BlockSpec((1, 128), lambda i: (0, 0))],
            [_rows_f32(), _rows_bf16(), _vec_f32(), jax.ShapeDtypeStruct((1, 128), F32)])


def _norm_bwd(dn, xv, g_ref, dg_ref):
    r = _rstd(xv)
    xn = xv * r
    dng = dn * g_ref[...]
    _acc_rows(dg_ref, dn * xn)
    return r * (dng - xn * jnp.mean(dng * xn, axis=-1, keepdims=True))


def _tail_mid_bwd(rows, dres, x, g, h, p, alpha):
    def fn(dn, ins, outs):
        dr_ref, x_ref, g_ref, h_ref, p_ref = ins
        dx_ref, dh_ref, dg_ref, dp_ref = outs
        dx = dr_ref[...] + _norm_bwd(dn, x_ref[...], g_ref, dg_ref)
        dx_ref[...] = dx
        hv = h_ref[...]
        r = _rstd(hv)
        hn = hv * r
        du = alpha * dx * p_ref[...]
        dh_ref[...] = (r * (du - hn * jnp.mean(du * hn, axis=-1, keepdims=True))).astype(BF16)
        _acc_rows(dp_ref, alpha * dx * hn)

    rs = _rows_spec(rows)
    return (_Tail(fn, [dres, x, g, h, p], [rs, rs, _vec_spec(), rs, _vec_spec()]),
            [rs, rs, _vec_spec(), _vec_spec()], [_rows_f32(), _rows_bf16(), _vec_f32(), _vec_f32()])


def _tail_first_bwd(rows, dres, x, g):
    def fn(dn, ins, outs):
        dr_ref, x_ref, g_ref = ins
        dx_ref, dg_ref = outs
        dx_ref[...] = dr_ref[...] + _norm_bwd(dn, x_ref[...], g_ref, dg_ref)

    rs = _rows_spec(rows)
    return (_Tail(fn, [dres, x, g], [rs, rs, _vec_spec()]), [rs, _vec_spec()], [_rows_f32(), _vec_f32()])


def _rotate(t, c128, s128, sign, scale):
    width = t.shape[1]
    c = jnp.tile(c128, (1, width // 128))
    sn = jnp.tile(s128, (1, width // 128))
    lane = lax.broadcasted_iota(jnp.int32, t.shape, 1) & (HD - 1)
    rot = jnp.where(lane < HD // 2, -pltpu.roll(t, width - HD // 2, 1), pltpu.roll(t, HD // 2, 1))
    return (t * c + sign * (rot * sn)) * scale


def _rows_to_blocks(y):
    out = []
    for j in range(NKV):
        yt = y[:, QCOLS * j:QCOLS * (j + 1)].T
        out.append(jnp.concatenate([yt[HD * g:HD * (g + 1)] for g in range(NQ_PER_KV)], axis=1))
    return out


def _blocks_to_rows(blocks):
    cols = []
    for b in blocks:
        stacked = jnp.concatenate([b[:, 128 * g:128 * (g + 1)] for g in range(NQ_PER_KV)], axis=0)
        cols.append(stacked.T)
    return jnp.concatenate(cols, axis=1)


def _rope_q(proj, cos, sin):
    def body(t_ref, c_ref, s_ref, o_ref):
        y = _rotate(t_ref[...], c_ref[...], s_ref[...], 1.0, HD ** -0.5)
        for j, blk in enumerate(_rows_to_blocks(y)):
            o_ref[j] = blk.astype(BF16)

    return pl.pallas_call(
        body, name="rope_q", grid=(NCH,),
        in_specs=[pl.BlockSpec((128, D), lambda i: (i, 0)),
                  pl.BlockSpec((128, 128), lambda i: (i, 0)), pl.BlockSpec((128, 128), lambda i: (i, 0))],
        out_specs=pl.BlockSpec((NKV, None, HD, QROWS), lambda i: (0, i, 0, 0)),
        out_shape=jax.ShapeDtypeStruct((NKV, NCH, HD, QROWS), BF16), compiler_params=_cparams("parallel"),
    )(proj, cos, sin)


def _rope_dq(dqt, cos, sin, dproj):
    def body(t_ref, c_ref, s_ref, buf_ref, o_ref):
        t = _blocks_to_rows([t_ref[j] for j in range(NKV)])
        o_ref[...] = _rotate(t, c_ref[...], s_ref[...], -1.0, HD ** -0.5).astype(BF16)

    return pl.pallas_call(
        body, name="rope_dq", grid=(NCH,),
        in_specs=[pl.BlockSpec((NKV, None, HD, QROWS), lambda i: (0, i, 0, 0)),
                  pl.BlockSpec((128, 128), lambda i: (i, 0)), pl.BlockSpec((128, 128), lambda i: (i, 0)),
                  pl.BlockSpec(memory_space=pl.ANY)],
        out_specs=pl.BlockSpec((128, D), lambda i: (i, 0)),
        out_shape=jax.ShapeDtypeStruct(dproj.shape, BF16), input_output_aliases={3: 0},
        compiler_params=_cparams("parallel"),
    )(dqt, cos, sin, dproj)


def _rope_dkv(dkt, dvt, cos, sin, dproj):
    def body(k_ref, v_ref, c_ref, s_ref, buf_ref, o_ref):
        dk = jnp.concatenate([k_ref[j] for j in range(NKV)], axis=0).T
        dv = jnp.concatenate([v_ref[j] for j in range(NKV)], axis=0).T
        dk = _rotate(dk, c_ref[...], s_ref[...], -1.0, 1.0)
        o_ref[...] = jnp.concatenate([dk, dv], axis=1).astype(BF16)

    tspec = pl.BlockSpec((NKV, HD, 128), lambda i: (0, 0, i))
    return pl.pallas_call(
        body, name="rope_dkv", grid=(NCH,),
        in_specs=[tspec, tspec, pl.BlockSpec((128, 128), lambda i: (i, 0)), pl.BlockSpec((128, 128), lambda i: (i, 0)),
                  pl.BlockSpec(memory_space=pl.ANY)],
        out_specs=pl.BlockSpec((128, 2 * KVW), lambda i: (i, D // (2 * KVW))),
        out_shape=jax.ShapeDtypeStruct(dproj.shape, BF16), input_output_aliases={4: 0},
        compiler_params=_cparams("parallel"),
    )(dkt, dvt, cos, sin, dproj)


def _rope_kv(proj, cos, sin):
    def body(t_ref, c_ref, s_ref, k_ref, v_ref, kt_ref, vt_ref):
        t = t_ref[...]
        k = _rotate(t[:, :KVW], c_ref[...], s_ref[...], 1.0, 1.0).astype(BF16)
        v = t[:, KVW:].astype(BF16)
        k_ref[...] = k
        v_ref[...] = v
        kt, vt = k.astype(F32).T, v.astype(F32).T
        for j in range(NKV):
            kt_ref[j] = kt[HD * j:HD * (j + 1)].astype(BF16)
            vt_ref[j] = vt[HD * j:HD * (j + 1)].astype(BF16)

    rows = pl.BlockSpec((128, KVW), lambda i: (i, 0))
    tspec = pl.BlockSpec((NKV, HD, 128), lambda i: (0, 0, i))
    return pl.pallas_call(
        body, name="rope_kv", grid=(NCH,),
        in_specs=[pl.BlockSpec((128, 2 * KVW), lambda i: (i, D // (2 * KVW))),
                  pl.BlockSpec((128, 128), lambda i: (i, 0)), pl.BlockSpec((128, 128), lambda i: (i, 0))],
        out_specs=[rows, rows, tspec, tspec],
        out_shape=[jax.ShapeDtypeStruct((S, KVW), BF16)] * 2 + [jax.ShapeDtypeStruct((NKV, HD, S), BF16)] * 2,
        compiler_params=_cparams("parallel"),
    )(proj, cos, sin)


QROWS = NQ_PER_KV * 128


NBIAS = NCH + 1
KV_PER_STEP = 4


def _bias_table():
    db = lax.broadcasted_iota(jnp.int32, (NBIAS, 128, QROWS), 0) - 1
    ki = lax.broadcasted_iota(jnp.int32, (NBIAS, 128, QROWS), 1)
    qi = lax.broadcasted_iota(jnp.int32, (NBIAS, 128, QROWS), 2) & 127
    d = db * 128 + qi - ki
    cnt = ((d <= 128).astype(F32) + (((d & 3) == 0) & (d <= 512)).astype(F32) + ((d & 15) == 0).astype(F32))
    return jnp.where((d >= 0) & (cnt > 0.0), jnp.log(jnp.maximum(cnt, 1.0)), NEG)


def _attn_fwd(qt, kh, vt, bias, rider=None):
    def body(q_ref, k_ref, v_ref, b_ref, o_ref, lse_ref, rows_ref, m_ref, l_ref, acc_ref):
        qb = pl.program_id(1)
        m_ref[...] = jnp.full_like(m_ref, NEG)
        l_ref[...] = jnp.zeros_like(l_ref)
        acc_ref[...] = jnp.zeros_like(acc_ref)

        def keys(off, size, bias_):
            for h in range(KV_PER_STEP):
                m = m_ref[h]
                s = _dot(k_ref[h, pl.ds(off, size), :], q_ref[h], NN) + bias_
                m_new = jnp.maximum(m, jnp.max(s, axis=0, keepdims=True))
                p = jnp.exp(s - m_new)
                a = jnp.exp(m - m_new)
                m_ref[h] = m_new
                l_ref[h] = a * l_ref[h] + jnp.sum(p, axis=0, keepdims=True)
                acc_ref[h] = a * acc_ref[h] + _dot(v_ref[h, :, pl.ds(off, size)], p, NN)

        def blocks(first, count):
            bias_ = jnp.concatenate([b_ref[qb - first - j + 1] for j in range(count)], axis=0)
            keys(pl.multiple_of(first * 128, 128), 128 * count, bias_)

        nkb = qb + 1
        @pl.loop(0, nkb // 4)
        def _(i):
            blocks(4 * i, 4)

        @pl.when(nkb % 4 >= 2)
        def _():
            blocks(nkb // 4 * 4, 2)

        @pl.when(nkb % 2 == 1)
        def _():
            blocks(qb, 1)

        outs = []
        for h in range(KV_PER_STEP):
            outs.append(acc_ref[h] / l_ref[h])
            o_ref[h] = outs[h]
            lse_ref[h] = m_ref[h] + jnp.log(l_ref[h])
        rows_ref[...] = _blocks_to_rows(outs).astype(BF16)

    kvs = KV_PER_STEP
    qspec = pl.BlockSpec((kvs, None, HD, QROWS), lambda j, i: (j, i, 0, 0))
    return _call(
        body, "attn_fwd", (NKV // kvs, NCH),
        [qspec, pl.BlockSpec((kvs, S, HD), lambda j, i: (j, 0, 0)),
         pl.BlockSpec((kvs, HD, S), lambda j, i: (j, 0, 0)),
         pl.BlockSpec((NBIAS, 128, QROWS), lambda j, i: (0, 0, 0))],
        [qspec, pl.BlockSpec((kvs, None, 1, QROWS), lambda j, i: (j, i, 0, 0)),
         pl.BlockSpec((128, QCOLS * kvs), lambda j, i: (i, j))],
        [jax.ShapeDtypeStruct((NKV, NCH, HD, QROWS), F32), jax.ShapeDtypeStruct((NKV, NCH, 1, QROWS), F32),
         jax.ShapeDtypeStruct((S, 2 * D), BF16)],
        (qt, kh, vt, bias),
        [pltpu.VMEM((kvs, 1, QROWS), F32), pltpu.VMEM((kvs, 1, QROWS), F32), pltpu.VMEM((kvs, HD, QROWS), F32)],
        ("parallel", "parallel"), rider)


def _attn_bwd(qt, kh, kt, vh, dot_, lse, delta, bias, rider=None):
    def body(qt_ref, k_ref, kt_ref, v_ref, dot_ref, lse_ref, dl_ref, b_ref, dq_ref, dk_ref, dv_ref):
        kp = pl.program_id(1)

        @pl.when(kp == 0)
        def _():
            dq_ref[...] = jnp.zeros_like(dq_ref)

        dk_ref[...] = jnp.zeros_like(dk_ref)
        dv_ref[...] = jnp.zeros_like(dv_ref)

        @pl.loop(2 * kp, NCH // 2)
        def _(j):
            for h in range(KV_PER_STEP):
                k, kt_, v = k_ref[h], kt_ref[h], v_ref[h]
                for qb in (2 * j, 2 * j + 1):
                    bias2 = jnp.concatenate([b_ref[jnp.maximum(qb - 4 * kp - t + 1, 0)] for t in range(4)], axis=0)
                    st = _dot(k, qt_ref[h, qb], NN) + bias2
                    pt = jnp.exp(st - lse_ref[h, qb])
                    dst = pt * (_dot(v, dot_ref[h, qb], NN) - dl_ref[h, qb])
                    dq_ref[h, qb] += _dot(kt_, dst, NN)
                    dk_ref[h] += _dot(qt_ref[h, qb], dst, NT)
                    dv_ref[h] += _dot(dot_ref[h, qb], pt, NT)

    kvs = KV_PER_STEP
    tspec = pl.BlockSpec((kvs, NCH, HD, QROWS), lambda j, i: (j, 0, 0, 0))
    kspec = pl.BlockSpec((kvs, 512, HD), lambda j, i: (j, i, 0))
    ktspec = pl.BlockSpec((kvs, HD, 512), lambda j, i: (j, 0, i))
    sspec = pl.BlockSpec((kvs, NCH, 1, QROWS), lambda j, i: (j, 0, 0, 0))
    return _call(
        body, "attn_bwd", (NKV // kvs, NCH // 4),
        [tspec, kspec, ktspec, kspec, tspec, sspec, sspec,
         pl.BlockSpec((NBIAS, 128, QROWS), lambda j, i: (0, 0, 0))],
        [tspec, ktspec, ktspec],
        [jax.ShapeDtypeStruct((NKV, NCH, HD, QROWS), F32),
         jax.ShapeDtypeStruct((NKV, HD, S), F32), jax.ShapeDtypeStruct((NKV, HD, S), F32)],
        (qt, kh, kt, vh, dot_, lse, delta, bias), sem=("parallel", "arbitrary"), rider=rider)


CONV_BLK = 256
CONV_COL0 = 1536 // CONV_BLK


CONV_ROWS = 128


def _conv_fwd(proj, convw, convb):
    trips = S // CONV_ROWS

    def body(u_ref, w_ref, b_ref, o_ref, y_ref):
        @pl.loop(0, trips)
        def _(c):
            t0 = pl.multiple_of(c * CONV_ROWS, CONV_ROWS)
            before = pl.multiple_of(jnp.maximum(t0 - 8, 0), 8)
            ext = jnp.concatenate([jnp.where(c == 0, 0.0, u_ref[pl.ds(before, 8), :]),
                                   u_ref[pl.ds(t0, CONV_ROWS), :]], axis=0)
            y = b_ref[...] + w_ref[CONV_K - 1:CONV_K, :] * ext[8:]
            for j in range(1, CONV_K):
                y = y + w_ref[CONV_K - 1 - j:CONV_K - j, :] * pltpu.roll(ext, j, 0)[8:]
            y_ref[pl.ds(t0, CONV_ROWS), :] = y
            o_ref[pl.ds(t0, CONV_ROWS), :] = y * _sigmoid(y)

    out = pl.BlockSpec((S, CONV_BLK), lambda i: (0, i))
    return pl.pallas_call(
        body, name="conv_fwd", grid=(CONV_C // CONV_BLK,),
        in_specs=[pl.BlockSpec((S, CONV_BLK), lambda i: (0, CONV_COL0 + i)),
                  pl.BlockSpec((CONV_K, CONV_BLK), lambda i: (0, i)),
                  pl.BlockSpec((1, CONV_BLK), lambda i: (0, i))],
        out_specs=[out, out], out_shape=[jax.ShapeDtypeStruct((S, CONV_C), F32)] * 2,
        compiler_params=_cparams("parallel"),
    )(proj, convw, convb)


def _conv_bwd(dact, ypre, proj, convw, dproj):
    trips = S // CONV_ROWS

    def body(da_ref, y_ref, u_ref, w_ref, buf_ref, du_ref, dw_ref, db_ref):
        dw_ref[...] = jnp.zeros_like(dw_ref)
        db_ref[...] = jnp.zeros_like(db_ref)
        r8 = lax.broadcasted_iota(jnp.int32, (8, CONV_BLK), 0)

        def dy_of(rows):
            y = y_ref[rows, :]
            sg = _sigmoid(y)
            return da_ref[rows, :] * (sg * (1.0 + y * (1.0 - sg)))

        @pl.loop(0, trips)
        def _(c):
            t0 = pl.multiple_of(c * CONV_ROWS, CONV_ROWS)
            after = pl.multiple_of(jnp.minimum(t0 + CONV_ROWS, S - 8), 8)
            ext = jnp.concatenate([dy_of(pl.ds(t0, CONV_ROWS)),
                                   jnp.where(c == trips - 1, 0.0, dy_of(pl.ds(after, 8)))], axis=0)
            u = u_ref[pl.ds(t0, CONV_ROWS), :]
            du, dw = None, jnp.zeros((8, CONV_BLK), F32)
            for j in range(CONV_K):
                dyj = (ext if j == 0 else pltpu.roll(ext, CONV_ROWS + 8 - j, 0))[:CONV_ROWS]
                term = w_ref[CONV_K - 1 - j:CONV_K - j, :] * dyj
                du = term if du is None else du + term
                dw = dw + jnp.where(r8 == CONV_K - 1 - j, jnp.sum(dyj * u, axis=0, keepdims=True), 0.0)
            du_ref[pl.ds(t0, CONV_ROWS), :] = du.astype(BF16)
            dw_ref[...] += dw
            db_ref[...] += jnp.sum(ext[:CONV_ROWS], axis=0, keepdims=True)

    return pl.pallas_call(
        body, name="conv_bwd", grid=(CONV_C // CONV_BLK,),
        in_specs=[pl.BlockSpec((S, CONV_BLK), lambda i: (0, i)), pl.BlockSpec((S, CONV_BLK), lambda i: (0, i)),
                  pl.BlockSpec((S, CONV_BLK), lambda i: (0, CONV_COL0 + i)),
                  pl.BlockSpec((CONV_K, CONV_BLK), lambda i: (0, i)), pl.BlockSpec(memory_space=pl.ANY)],
        out_specs=[pl.BlockSpec((S, CONV_BLK), lambda i: (0, CONV_COL0 + i)),
                   pl.BlockSpec((8, CONV_BLK), lambda i: (0, i)), pl.BlockSpec((1, CONV_BLK), lambda i: (0, i))],
        out_shape=[jax.ShapeDtypeStruct(dproj.shape, BF16), jax.ShapeDtypeStruct((8, CONV_C), F32),
                   jax.ShapeDtypeStruct((1, CONV_C), F32)],
        input_output_aliases={4: 0}, compiler_params=_cparams("parallel"),
    )(dact, ypre, proj, convw, dproj)


NPAIR = 8


def _ssd_scalars(dtr_ref, dtb_ref, alog_ref):
    z = dtr_ref[...] + dtb_ref[...]
    dt = jnp.maximum(z, 0.0) + jnp.log(1.0 + jnp.exp(-jnp.abs(z)))
    a = -jnp.exp(alog_ref[...])
    r = lax.broadcasted_iota(jnp.int32, (128, 128), 0)
    c = lax.broadcasted_iota(jnp.int32, (128, 128), 1)
    tri = (r >= c).astype(F32)
    cs = _dot_exact(tri, dt * a)
    return z, dt, a, cs, r, c


def _by_lane(cs, dt):
    head = lax.broadcasted_iota(jnp.int32, (128, SSM_W), 0)
    lane = lax.broadcasted_iota(jnp.int32, (128, SSM_W), 1)
    sel = (head == lane // HD).astype(F32)
    cs_l = _dot_exact(cs, sel, "b")
    last_l = cs_l[127:128, :]
    return sel, jnp.exp(cs_l), jnp.exp(last_l - cs_l), _dot_exact(dt, sel, "b")


def _pair_terms(cs, h1, h2):
    return (cs[:, h1:h1 + 1], cs[:, h2:h2 + 1],
            jnp.exp(cs[127:128, h1:h1 + 1]), jnp.exp(cs[127:128, h2:h2 + 1]))


def _gate_norm(y, zv, w):
    yg = y * (zv * _sigmoid(zv))
    outs, rs = [], []
    for g in range(2):
        blk = yg[:, 512 * g:512 * (g + 1)]
        r = lax.rsqrt(jnp.mean(blk * blk, axis=-1, keepdims=True) + EPS)
        outs.append(blk * r)
        rs.append(r)
    return jnp.concatenate(outs, axis=1), rs, yg


def _ssd_fwd(xbc, proj, dtb, alog, dskip_l, ssmw, mixed):
    def body(x_ref, b_ref, c_ref, dtr_ref, z_ref, dtb_ref, alog_ref, dsk_ref, w_ref, buf_ref,
             y_ref, yn_ref, hp_ref, h_ref):
        @pl.when(pl.program_id(0) == 0)
        def _():
            h_ref[...] = jnp.zeros_like(h_ref)

        _, dt, _, cs, r, c = _ssd_scalars(dtr_ref, dtb_ref, alog_ref)
        cst = cs.T
        causal = r >= c
        lo = c < HD
        _, e_all, dte_all, dt_all = _by_lane(cs, dt)
        hp_ref[...] = h_ref[...]
        for g in range(2):
            bg = b_ref[:, 128 * g:128 * (g + 1)]
            cg = c_ref[:, 128 * g:128 * (g + 1)]
            cb = _dot(cg, bg, NT)
            for j in range(4):
                pj = 4 * g + j
                h1, h2 = 2 * pj, 2 * pj + 1
                sl = slice(128 * pj, 128 * (pj + 1))
                xp = x_ref[:, sl]
                c1, c2, cd1, cd2 = _pair_terms(cs, h1, h2)
                e_l, dte_l = e_all[:, sl], dte_all[:, sl]
                xdt = xp * dt_all[:, sl]
                m1 = cb * jnp.exp(jnp.where(causal, c1 - cst[h1:h1 + 1, :], NEG))
                m2 = cb * jnp.exp(jnp.where(causal, c2 - cst[h2:h2 + 1, :], NEG))
                yd = jnp.where(lo, _dot(m1, xdt, NN), _dot(m2, xdt, NN))
                hp = h_ref[pj]
                yo = _dot(cg, hp, NT) * e_l
                st = _dot(xdt * dte_l, bg, TN)
                h_ref[pj] = hp * jnp.where(r < HD, cd1, cd2) + st
                y_ref[:, sl] = yd + yo + dsk_ref[:, sl] * xp
        yn, _, _ = _gate_norm(y_ref[...], z_ref[...], w_ref[...])
        yn_ref[...] = (yn * w_ref[...]).astype(BF16)

    return pl.pallas_call(
        body, name="ssd_fwd", grid=(NCH,),
        in_specs=[pl.BlockSpec((128, SSM_W), lambda i: (i, 0)),
                  pl.BlockSpec((128, 256), lambda i: (i, 4)), pl.BlockSpec((128, 256), lambda i: (i, 5)),
                  pl.BlockSpec((128, 128), lambda i: (i, COL_DT // 128)),
                  pl.BlockSpec((128, SSM_W), lambda i: (i, 3)),
                  pl.BlockSpec((1, 128), lambda i: (0, 0)), pl.BlockSpec((1, 128), lambda i: (0, 0)),
                  pl.BlockSpec((1, SSM_W), lambda i: (0, 0)), pl.BlockSpec((1, SSM_W), lambda i: (0, 0)),
                  pl.BlockSpec(memory_space=pl.ANY)],
        out_specs=[pl.BlockSpec((128, SSM_W), lambda i: (i, 0)), pl.BlockSpec((128, SSM_W), lambda i: (i, 1)),
                   pl.BlockSpec((None, NPAIR, 128, 128), lambda i: (i, 0, 0, 0))],
        out_shape=[jax.ShapeDtypeStruct((S, SSM_W), F32), jax.ShapeDtypeStruct(mixed.shape, BF16),
                   jax.ShapeDtypeStruct((NCH, NPAIR, 128, 128), F32)],
        scratch_shapes=[pltpu.VMEM((NPAIR, 128, 128), F32)],
        input_output_aliases={9: 1}, compiler_params=_cparams("arbitrary"),
    )(xbc, xbc, xbc, proj, proj, dtb, alog, dskip_l, ssmw, mixed)


def _ssd_bwd(dmixed, y, xbc, proj, hprev, dtb, alog, dskip_l, ssmw, rider=None):
    def body(dyn_ref, y_ref, x_ref, b_ref, c_ref, dtr_ref, z_ref, hp_ref, dtb_ref, alog_ref, dsk_ref, w_ref,
             dxbc_ref, dz_ref, ddt_ref, dw_ref, dsc_ref, g_ref):
        @pl.when(pl.program_id(0) == 0)
        def _():
            g_ref[...] = jnp.zeros_like(g_ref)
            dsc_ref[...] = jnp.zeros_like(dsc_ref)

        z, dt, a, cs, r, c = _ssd_scalars(dtr_ref, dtb_ref, alog_ref)
        cst = cs.T
        causal = r >= c
        lo = c < HD

        yv = y_ref[...]
        zv = z_ref[...]
        wv = w_ref[...]
        ygn, rs, yg = _gate_norm(yv, zv, wv)
        dyn = dyn_ref[...]
        _acc_rows(dw_ref, dyn * ygn)
        dynw = dyn * wv
        parts = []
        for g in range(2):
            sl = slice(512 * g, 512 * (g + 1))
            a_g, n_g = dynw[:, sl], ygn[:, sl]
            parts.append(rs[g] * (a_g - n_g * jnp.mean(a_g * n_g, axis=-1, keepdims=True)))
        dyg = jnp.concatenate(parts, axis=1)
        sz = _sigmoid(zv)
        dz_ref[...] = (dyg * yv * (sz * (1.0 + zv * (1.0 - sz)))).astype(BF16)
        dy_all = dyg * (zv * sz)

        dcs_cols = jnp.zeros((128, 128), F32)
        dcs_rows = jnp.zeros((128, 128), F32)
        sel, e_all, dte_all, dt_all = _by_lane(cs, dt)
        x_all, b_all, c_all, dsk_all = x_ref[...], b_ref[...], c_ref[...], dsk_ref[...]
        hp_all, g_all = hp_ref[...], g_ref[...]
        g_new, dx_parts, db_parts, dc_parts = [], [], [], []
        dyx_parts, ryo_parts, qx_parts, dxx_parts, gh_parts = [], [], [], [], []
        for g in range(2):
            bg = b_all[:, 128 * g:128 * (g + 1)]
            cg = c_all[:, 128 * g:128 * (g + 1)]
            cb = _dot(cg, bg, NT)
            dcb = jnp.zeros((128, 128), F32)
            db_acc = jnp.zeros((128, NST), F32)
            dc_acc = jnp.zeros((128, NST), F32)
            for j in range(4):
                pj = 4 * g + j
                h1, h2 = 2 * pj, 2 * pj + 1
                sl = slice(128 * pj, 128 * (pj + 1))
                xp = x_all[:, sl]
                dyp = dy_all[:, sl]
                c1, c2, cd1, cd2 = _pair_terms(cs, h1, h2)
                e_l, dte_l, dt_l = e_all[:, sl], dte_all[:, sl], dt_all[:, sl]
                xdt = xp * dt_l
                hp = hp_all[pj]
                gp = g_all[pj]
                dyx_parts.append(dyp * xp)
                dzs = dyp * e_l
                dc_acc = dc_acc + _dot(dzs, hp, NN)
                ryo_parts.append(dyp * (_dot(cg, hp, NT) * e_l))
                qm = _dot(bg, gp, NT)
                dxdt = qm * dte_l
                qx_parts.append(qm * xdt)
                db_acc = db_acc + _dot(xdt * dte_l, gp, NN)
                gh_parts.append(gp * hp)
                g_new.append(_dot(dzs, cg, TN) + jnp.where(r < HD, cd1, cd2) * gp)
                for hh, ch, msk in ((h1, c1, lo), (h2, c2, jnp.logical_not(lo))):
                    lm = jnp.exp(jnp.where(causal, ch - cst[hh:hh + 1, :], NEG))
                    mm = cb * lm
                    dm = jnp.where(causal, _dot(jnp.where(msk, dyp, 0.0), xdt, NT), 0.0)
                    w = dm * mm
                    dcs_cols = dcs_cols + jnp.where(c == hh, jnp.sum(w, axis=1, keepdims=True), 0.0)
                    dcs_rows = dcs_rows + jnp.where(r == hh, jnp.sum(w, axis=0, keepdims=True), 0.0)
                    dcb = dcb + dm * lm
                    dxdt = dxdt + jnp.where(msk, _dot(mm, dyp, TN), 0.0)
                dxx_parts.append(dxdt * xp)
                dx_parts.append(dsk_all[:, sl] * dyp + dxdt * dt_l)
            db_parts.append(db_acc + _dot(dcb, cg, TN))
            dc_parts.append(dc_acc + _dot(dcb, bg, NN))
        g_ref[...] = jnp.stack(g_new)
        dxbc_ref[...] = jnp.concatenate(dx_parts + db_parts + dc_parts, axis=1)

        selt = (lax.broadcasted_iota(jnp.int32, (SSM_W, 128), 0) // HD
                == lax.broadcasted_iota(jnp.int32, (SSM_W, 128), 1)).astype(F32)

        def by_head(parts):
            return _dot_exact(jnp.concatenate(parts, axis=1), selt, "b")

        ddt_x = by_head(dxx_parts)
        dd_row = jnp.sum(by_head(dyx_parts), axis=0, keepdims=True)
        t_all = by_head(qx_parts) * jnp.exp(cs[127:128, :] - cs)
        gh = jnp.sum(_dot_exact(sel, jnp.concatenate(gh_parts, axis=0)), axis=1, keepdims=True)
        gh_row = jnp.broadcast_to(gh, (128, 128)).T[0:1, :]
        at_end = jnp.sum(t_all, axis=0, keepdims=True) + gh_row * jnp.exp(cs[127:128, :])
        dcs = by_head(ryo_parts) - t_all + dcs_cols + jnp.where(r == 127, at_end, 0.0) - dcs_rows.T
        dad = _dot_exact((c >= r).astype(F32), dcs)
        ddt = dad * a + ddt_x
        ddtr = jnp.where(c < 16, ddt * _sigmoid(z), 0.0)
        ddt_ref[...] = ddtr.astype(BF16)
        r8 = lax.broadcasted_iota(jnp.int32, (8, 128), 0)
        dsc_ref[...] += (jnp.where(r8 == 0, jnp.sum(ddtr, axis=0, keepdims=True), 0.0)
                         + jnp.where(r8 == 1, jnp.sum(dad * dt, axis=0, keepdims=True) * a, 0.0)
                         + jnp.where(r8 == 2, dd_row, 0.0))

    rev = NCH - 1
    return _call(
        body, "ssd_bwd", (NCH,),
        [pl.BlockSpec((128, SSM_W), lambda i: (rev - i, 0)),
         pl.BlockSpec((128, SSM_W), lambda i: (rev - i, 0)),
         pl.BlockSpec((128, SSM_W), lambda i: (rev - i, 0)),
         pl.BlockSpec((128, 256), lambda i: (rev - i, 4)), pl.BlockSpec((128, 256), lambda i: (rev - i, 5)),
         pl.BlockSpec((128, 128), lambda i: (rev - i, COL_DT // 128)),
         pl.BlockSpec((128, SSM_W), lambda i: (rev - i, 3)),
         pl.BlockSpec((None, NPAIR, 128, 128), lambda i: (rev - i, 0, 0, 0)),
         pl.BlockSpec((1, 128), lambda i: (0, 0)), pl.BlockSpec((1, 128), lambda i: (0, 0)),
         pl.BlockSpec((1, SSM_W), lambda i: (0, 0)), pl.BlockSpec((1, SSM_W), lambda i: (0, 0))],
        [pl.BlockSpec((128, CONV_C), lambda i: (rev - i, 0)),
         pl.BlockSpec((128, SSM_W), lambda i: (rev - i, 3)),
         pl.BlockSpec((128, 128), lambda i: (rev - i, 0)),
         pl.BlockSpec((1, SSM_W), lambda i: (0, 0)), pl.BlockSpec((8, 128), lambda i: (0, 0))],
        [jax.ShapeDtypeStruct((S, CONV_C), F32), jax.ShapeDtypeStruct((S, WIN_PAD), BF16),
         jax.ShapeDtypeStruct((S, 128), BF16), jax.ShapeDtypeStruct((1, SSM_W), F32),
         jax.ShapeDtypeStruct((8, 128), F32)],
        (dmixed, y, xbc, xbc, xbc, proj, proj, hprev, dtb, alog, dskip_l, ssmw),
        [pltpu.VMEM((NPAIR, 128, 128), F32)], ("arbitrary",), rider)


def _cast_stack(name, slot, arrs, tr, tc):
    n = len(arrs)
    rows, cols = arrs[0].shape

    def body(s_ref, *refs):
        for i in range(n):
            refs[n][i] = refs[i][...].astype(BF16)

    return pl.pallas_call(
        body, name=name,
        grid_spec=pltpu.PrefetchScalarGridSpec(
            num_scalar_prefetch=1, grid=(rows // tr, cols // tc),
            in_specs=[pl.BlockSpec((tr, tc), lambda i, j, sr: (i, j))] * n,
            out_specs=pl.BlockSpec((None, n, tr, tc), lambda i, j, sr: (sr[0], 0, i, j))),
        out_shape=jax.ShapeDtypeStruct((NSH, n, rows, cols), BF16),
        compiler_params=_cparams("parallel", "parallel"),
    )(slot, *arrs)


def _pair_sum(name, c_idx, ps, th):
    n = len(ps)
    _, rows, _ = ps[0].shape

    def body(c_ref, *refs):
        mine, whole, out, theirs = refs[:n], refs[n:2 * n], refs[2 * n:3 * n], refs[3 * n:4 * n]
        send, recv = refs[4 * n], refs[4 * n + 1]
        s, i = pl.program_id(0), pl.program_id(1)
        x, y, c, _ = _place()

        def copies(slot):
            return [_rcopy(whole[k].at[slot, :, pl.ds((1 - c) * HALF, HALF)], theirs[k].at[slot],
                           send.at[slot * n + k], recv.at[slot * n + k], (x, y, 1 - c)) for k in range(n)]

        @pl.when((s == 0) & (i == 0))
        def _():
            for slot in range(NSH):
                for cp in copies(slot):
                    cp.start()

        @pl.when(i == 0)
        def _():
            for slot in range(NSH):
                @pl.when(s == slot)
                def _():
                    for cp in copies(slot):
                        cp.wait()

        rows_i = slice(None) if th == rows else pl.ds(pl.multiple_of(i * th, th), th)
        for k in range(n):
            out[k][...] = (mine[k][...].astype(F32) + theirs[k][s, rows_i, :].astype(F32)).astype(BF16)

    spec = pl.BlockSpec((None, th, HALF), lambda s, i, cr: (s, i, 0))
    return pl.pallas_call(
        body, name=name,
        grid_spec=pltpu.PrefetchScalarGridSpec(
            num_scalar_prefetch=1, grid=(NSH, rows // th),
            in_specs=[pl.BlockSpec((None, th, HALF), lambda s, i, cr: (s, i, cr[0]))] * n + _any_specs(n),
            out_specs=[spec] * n,
            scratch_shapes=[pltpu.VMEM((NSH, rows, HALF), BF16)] * n
            + [pltpu.SemaphoreType.DMA((NSH * n,)), pltpu.SemaphoreType.DMA((NSH * n,))]),
        out_shape=[jax.ShapeDtypeStruct((NSH, rows, HALF), BF16)] * n,
        compiler_params=_cparams("arbitrary", "arbitrary"),
    )(c_idx, *ps, *ps)


def _pair_add(name, c_idx, ps, theirs, th):
    n = len(ps)
    _, rows, _ = ps[0].shape

    def body(c_ref, *refs):
        for k in range(n):
            refs[2 * n + k][...] = (refs[k][...].astype(F32) + refs[n + k][...].astype(F32)).astype(BF16)

    spec = pl.BlockSpec((None, th, HALF), lambda s, i, cr: (s, i, 0))
    return pl.pallas_call(
        body, name=name,
        grid_spec=pltpu.PrefetchScalarGridSpec(
            num_scalar_prefetch=1, grid=(NSH, rows // th),
            in_specs=[pl.BlockSpec((None, th, HALF), lambda s, i, cr: (s, i, cr[0]))] * n + [spec] * n,
            out_specs=[spec] * n),
        out_shape=[jax.ShapeDtypeStruct((NSH, rows, HALF), BF16)] * n,
        compiler_params=_cparams("parallel", "parallel"),
    )(c_idx, *ps, *theirs)


def _chip_sum(name, place, cs, ts, th):
    n = len(ts)
    _, rows, _ = ts[0].shape

    def body(p_ref, *refs):
        for i in range(n):
            t = refs[n + i][...].astype(F32)
            refs[2 * n + i][...] = ((refs[i][...].astype(F32) + t[0]) + t[1]) + t[2]

    return pl.pallas_call(
        body, name=name,
        grid_spec=pltpu.PrefetchScalarGridSpec(
            num_scalar_prefetch=1, grid=(rows // th,),
            in_specs=[pl.BlockSpec((None, th, HALF), lambda i, pr: (pr[0], i, 0))] * n
            + [pl.BlockSpec((3, th, HALF), lambda i, pr: (0, i, 0))] * n,
            out_specs=[pl.BlockSpec((th, HALF), lambda i, pr: (i, pr[1]))] * n),
        out_shape=[jax.ShapeDtypeStruct((rows, D), F32)] * n, compiler_params=_cparams("parallel"),
    )(place, *cs, *ts)


def _adamw(name, ws, gs, ms, vs, tr, tc):
    n = len(ws)
    shape = ws[0].shape
    rows, cols, mid = shape[0], shape[-1], shape[1:-1]
    c1 = 1.0 / (1.0 - ADAM_B1 ** ADAM_STEP)
    c2 = 1.0 / (1.0 - ADAM_B2 ** ADAM_STEP)

    def body(*refs):
        for i in range(n):
            w, g, m, v = (refs[k * n + i][...] for k in range(4))
            m2 = ADAM_B1 * m + (1.0 - ADAM_B1) * g
            v2 = ADAM_B2 * v + (1.0 - ADAM_B2) * (g * g)
            refs[4 * n + 4 * i][...] = -ADAM_LR * ((m2 * c1) / (jnp.sqrt(v2 * c2) + ADAM_EPS) + ADAM_WD * w)
            refs[4 * n + 4 * i + 1][...] = m2
            refs[4 * n + 4 * i + 2][...] = v2
            refs[4 * n + 4 * i + 3][...] = g

    spec = pl.BlockSpec((tr,) + mid + (tc,), lambda i, j: (i,) + (0,) * len(mid) + (j,))
    outs = pl.pallas_call(
        body, name=name, grid=(rows // tr, cols // tc), in_specs=[spec] * (4 * n), out_specs=[spec] * (4 * n),
        out_shape=[jax.ShapeDtypeStruct(shape, F32)] * (4 * n),
        compiler_params=_cparams("parallel", "parallel"),
    )(*ws, *gs, *ms, *vs)
    return [tuple(outs[4 * i:4 * i + 4]) for i in range(n)]


def _place():
    x, y, c = lax.axis_index("x"), lax.axis_index("y"), lax.axis_index("c")
    chips = [(1 - x, y), (x, 1 - y), (1 - x, 1 - y)]
    return x, y, c, chips


def _any_specs(n):
    return [pl.BlockSpec(memory_space=pl.ANY)] * n


def _rcopy(src, dst, send_sem, recv_sem, dev):
    return pltpu.make_async_remote_copy(src_ref=src, dst_ref=dst, send_sem=send_sem, recv_sem=recv_sem,
                                        device_id=dev, device_id_type=MESH)


QUARTER = HALF // 2

XA, XB, YA, YB, RX, RY, F_XA, F_XB, F_YA, F_YB, F_D0, F_D1 = range(12)


def _gather_rider(bufs, views, at=None):
    n = len(bufs)

    def plan(rout, sems):
        send, recv = sems
        x, y, c, _ = _place()
        me, sx, sy, sd = 2 * x + y, 2 * (1 - x) + y, 2 * x + (1 - y), 2 * (1 - x) + (1 - y)
        nx, ny, sib = (1 - x, y, c), (x, 1 - y, c), (x, y, 1 - c)
        q0, q1 = c * HALF, c * HALF + QUARTER
        o0, o1 = (1 - c) * HALF, (1 - c) * HALF + QUARTER
        out = {XA: (me, q1, nx), XB: (me, q0, nx), YA: (me, q0, ny), YB: (me, q1, ny),
               RX: (sy, q0, nx), RY: (sx, q1, ny),
               F_XA: (sx, q1, sib), F_XB: (sx, q0, sib), F_YA: (sy, q0, sib), F_YB: (sy, q1, sib),
               F_D0: (sd, q0, sib), F_D1: (sd, q1, sib)}
        inn = {XA: (sx, q1), XB: (sx, q0), YA: (sy, q0), YB: (sy, q1), RX: (sd, q0), RY: (sd, q1),
               F_XA: (sx, o1), F_XB: (sx, o0), F_YA: (sy, o0), F_YB: (sy, o1), F_D0: (sd, o0), F_D1: (sd, o1)}

        def copy(kind, b):
            slot, col, dev = out[kind]
            win = views[b](rout[b], slot, col, QUARTER)
            return _rcopy(win, win, send.at[kind * n + b], recv.at[kind * n + b], dev)

        def land(kind, b):
            slot, col = inn[kind]
            win = views[b](rout[b], slot, col, QUARTER)
            return _rcopy(win, win, send.at[kind * n + b], recv.at[kind * n + b], (x, y, c))

        return copy, land

    first = (XA, YA, XB, YB)
    early = ((XA, (RY, F_XA)), (YA, (RX, F_YA)))
    late = ((XB, (F_XB,)), (YB, (F_YB,)), (RX, (F_D0,)), (RY, (F_D1,)))
    forwards = (F_XA, F_XB, F_YA, F_YB, F_D0, F_D1)
    sent = first + (RX, RY) + forwards

    def start(rin, rout, sems):
        copy, _ = plan(rout, sems)
        for kind in first:
            for b in range(n):
                copy(kind, b).start()

    def pass_on(rout, sems, links):
        copy, land = plan(rout, sems)
        for landed, then in links:
            for b in range(n):
                land(landed, b).wait_recv()
                for kind in then:
                    copy(kind, b).start()

    def between(rin, rout, sems):
        pass_on(rout, sems, early)

    def finish(rin, rout, sems):
        pass_on(rout, sems, late)
        copy, land = plan(rout, sems)
        for kind in forwards:
            for b in range(n):
                land(kind, b).wait_recv()
        for kind in sent:
            for b in range(n):
                copy(kind, b).wait_send()

    return _Rider(list(bufs), [jax.ShapeDtypeStruct(a.shape, a.dtype) for a in bufs], {b: b for b in range(n)},
                  [pltpu.SemaphoreType.DMA((12 * n,))] * 2, start, finish, between, at)


def _small_gather_rider(cw):
    def descs(rin, rout, sems, x, y, c, chips):
        return [_rcopy(rin[0], rout[0].at[2 * x + y], sems[1].at[j], sems[2].at[j], (chip[0], chip[1], c))
                for j, chip in enumerate(chips)]

    def start(rin, rout, sems):
        x, y, c, chips = _place()
        pltpu.make_async_copy(rin[0], rout[0].at[2 * x + y], sems[0].at[0]).start()
        for cp in descs(rin, rout, sems, x, y, c, chips):
            cp.start()

    def finish(rin, rout, sems):
        x, y, c, chips = _place()
        for j, chip in enumerate(chips):
            _rcopy(rin[0], rout[0].at[2 * chip[0] + chip[1]], sems[1].at[j], sems[2].at[j], (x, y, c)).wait_recv()
        for cp in descs(rin, rout, sems, x, y, c, chips):
            cp.wait_send()
        pltpu.make_async_copy(rin[0], rout[0].at[2 * x + y], sems[0].at[0]).wait()

    return _Rider([cw], [jax.ShapeDtypeStruct((NSH,) + cw.shape, cw.dtype)], {},
                  [pltpu.SemaphoreType.DMA((1,)), pltpu.SemaphoreType.DMA((3,)), pltpu.SemaphoreType.DMA((3,))],
                  start, finish)


def _to_sibling_rider(ps):
    n = len(ps)

    def descs(rin, rout, sems):
        x, y, c, _ = _place()
        return [_rcopy(rin[i].at[:, :, pl.ds((1 - c) * HALF, HALF)], rout[i], sems[0].at[i], sems[1].at[i],
                       (x, y, 1 - c)) for i in range(n)]

    def start(rin, rout, sems):
        for cp in descs(rin, rout, sems):
            cp.start()

    def finish(rin, rout, sems):
        for cp in descs(rin, rout, sems):
            cp.wait()

    return _Rider(list(ps), [jax.ShapeDtypeStruct(a.shape[:2] + (HALF,), a.dtype) for a in ps], {},
                  [pltpu.SemaphoreType.DMA((n,))] * 2, start, finish)


def _to_chips_rider(cs, first=0, count=None, into=None):
    n = len(cs)
    rows = [pl.ds(first, a.shape[1] - first if count is None else count) for a in cs]

    def descs(rin, rout, sems):
        x, y, c, chips = _place()
        return [_rcopy(rin[i].at[2 * chip[0] + chip[1], rows[i]], rout[i].at[j, rows[i]], sems[0].at[j * n + i],
                       sems[1].at[j * n + i], (chip[0], chip[1], c)) for j, chip in enumerate(chips) for i in range(n)]

    def start(rin, rout, sems):
        for cp in descs(rin, rout, sems):
            cp.start()

    def finish(rin, rout, sems):
        for cp in descs(rin, rout, sems):
            cp.wait()

    return _Rider(list(cs) + list(into or []), [jax.ShapeDtypeStruct((3,) + a.shape[1:], a.dtype) for a in cs],
                  {n + i: i for i in range(n)} if into else {}, [pltpu.SemaphoreType.DMA((3 * n,))] * 2, start, finish)


def _join_riders(riders):
    counts = [[len(r.operands) for r in riders], [len(r.out_shapes) for r in riders], [len(r.sems) for r in riders]]

    def each(step):
        def run(*refs):
            at = [0, 0, 0]
            for i, r in enumerate(riders):
                parts = [group[at[k]:at[k] + counts[k][i]] for k, group in enumerate(refs)]
                at = [at[k] + counts[k][i] for k in range(3)]
                step(r)(*parts)
        return run

    aliases = {sum(counts[0][:i]) + k: sum(counts[1][:i]) + v
               for i, r in enumerate(riders) for k, v in r.aliases.items()}
    return _Rider([a for r in riders for a in r.operands], [s for r in riders for s in r.out_shapes], aliases,
                  [s for r in riders for s in r.sems], each(lambda r: r.start), each(lambda r: r.finish),
                  each(lambda r: r.between or (lambda *refs: None)))


SMALL_ROWS = 16


def _swap_halves(gs, vec):
    n = len(gs)

    def body(*refs):
        v_ref, dst, o_ref = refs[n], refs[n + 1:2 * n + 1], refs[2 * n + 1]
        buf, send, recv, vsend, vrecv = refs[2 * n + 2:]
        x, y, c, _ = _place()
        cps = []
        for i in range(n):
            mine = dst[i].at[:, pl.ds(c * HALF, HALF)]
            cps.append(_rcopy(mine, mine, send.at[i], recv.at[i], (x, y, 1 - c)))
        for cp in cps:
            cp.start()

        me = 4 * x + 2 * y + c
        buf[me] = v_ref[...]
        vcps = []
        for k in range(1, 8):
            peer = (x ^ (k >> 2), y ^ ((k >> 1) & 1), c ^ (k & 1))
            vcps.append(_rcopy(v_ref, buf.at[me], vsend.at[k - 1], vrecv.at[k - 1], peer))
        for cp in vcps:
            cp.start()
        for k in range(1, 8):
            _rcopy(v_ref, buf.at[me ^ k], vsend.at[k - 1], vrecv.at[k - 1], (x, y, c)).wait_recv()
        for cp in vcps:
            cp.wait_send()
        t = buf[0]
        for d in range(1, 8):
            t = t + buf[d]
        o_ref[...] = t

        for i in range(n):
            other = dst[i].at[:, pl.ds((1 - c) * HALF, HALF)]
            _rcopy(other, other, send.at[i], recv.at[i], (x, y, c)).wait_recv()
        for cp in cps:
            cp.wait_send()

    vmem = pl.BlockSpec(memory_space=pltpu.VMEM)
    res = pl.pallas_call(
        body, name="grads_swap_halves", in_specs=_any_specs(n) + [vmem], out_specs=_any_specs(n) + [vmem],
        out_shape=[jax.ShapeDtypeStruct(g.shape, g.dtype) for g in gs] + [jax.ShapeDtypeStruct((SMALL_ROWS, D), F32)],
        input_output_aliases={i: i for i in range(n)},
        scratch_shapes=[pltpu.VMEM((8, SMALL_ROWS, D), F32)] + [pltpu.SemaphoreType.DMA((n,))] * 2
        + [pltpu.SemaphoreType.DMA((7,))] * 2,
    )(*gs, vec)
    return list(res[:n]), res[n]


def _col_window(ref, slot, col, ncols):
    return ref.at[slot, :, pl.ds(col, ncols)]


def _stack_window(first, count):
    def view(ref, slot, col, ncols):
        return ref.at[slot, pl.ds(first, count), :, pl.ds(col, ncols)]
    return view


def _row_tile(rows):
    for t in range(512, 15, -16):
        if rows % t == 0:
            return t
    return rows


def _same_shape_runs(arrs):
    runs, a = [], 0
    for b in range(1, len(arrs) + 1):
        if b == len(arrs) or arrs[b].shape != arrs[a].shape:
            runs.append((a, b))
            a = b
    return runs


class _Comm:
    def __init__(self):
        x, y, c = lax.axis_index("x"), lax.axis_index("y"), lax.axis_index("c")
        self.c_idx = jnp.reshape(c, (1,)).astype(jnp.int32)
        self.shard = jnp.reshape(2 * x + y, (1,)).astype(jnp.int32)
        self.place = jnp.stack([2 * x + y, c]).astype(jnp.int32)
        self.groups = {}
        self.sent = {}

    @staticmethod
    def gather(*bufs, part=None, at=None):
        views = [_col_window if b.ndim == 3 else _stack_window(*(part or (0, b.shape[1]))) for b in bufs]
        return _gather_rider(list(bufs), views, at)

    def reduce_rider(self, tag, names, ps, theirs=None, rows=None):
        csums = []
        for a, b in _same_shape_runs(ps):
            name, th = "pair_sum_%s%d" % (tag, a), _row_tile(ps[a].shape[1])
            csums += (_pair_sum(name, self.c_idx, ps[a:b], th) if theirs is None else
                      _pair_add(name, self.c_idx, ps[a:b], theirs[a:b], th))
        self.groups[tag] = [names, csums, None]
        self.sent[tag] = rows
        return _to_chips_rider(csums, 0, rows)

    def rest_rider(self, tag):
        _, csums, ts = self.groups[tag]
        return _to_chips_rider(csums, self.sent[tag], None, ts)

    def landed(self, tag, ts):
        self.groups[tag][2] = ts

    def finish(self, small):
        names, csums, ts = [], [], []
        for group_names, group_csums, group_ts in self.groups.values():
            names += group_names
            csums += group_csums
            ts += group_ts
        order = sorted(range(len(names)), key=lambda i: csums[i].shape[1])
        names, csums, ts = ([v[i] for i in order] for v in (names, csums, ts))
        halves = []
        for a, b in _same_shape_runs(csums):
            halves += _chip_sum("chip_sum_%d" % a, self.place, csums[a:b], ts[a:b], _row_tile(csums[a].shape[1]))
        grads, total = _swap_halves(halves, small)
        return dict(zip(names, grads)), total


ROPE_THETA = 10000.0
SMALL_1K = ("ffn1_pre_norm", "ffn1_post_norm", "mix_pre_norm", "ssm_norm", "mix_post_norm",
            "ffn2_pre_norm", "ffn2_post_norm")
SMALL_16 = ("dt_bias", "a_log", "d_skip")
OFF_CONVB = 7 * D
OFF_16 = OFF_CONVB + CONV_C
OFF_CONVW = OFF_16 + 48
OFF_LOSS = OFF_CONVW + CONV_K * CONV_C
SMALL_LEN = SMALL_ROWS * D


def _sds(shape, dtype):
    return jax.ShapeDtypeStruct(shape, dtype)


def _ridden(res, rider):
    return res if rider is not None else (res, None)


def _ffn_down(name, act, w, tail_of, rider=None):
    tail, o_specs, o_shapes = tail_of(TS)
    return _mm(name, [act, w.dn], NN, (S // TS,),
               [pl.BlockSpec((NSH, TS, FS), lambda i: (0, i, 0)),
                pl.BlockSpec((NSH, None, FS, D), lambda i: (0, w.d0, 0, 0))], o_specs, o_shapes, rider, tail)


def _ffn_dw(name, a, b, rider=None):
    return _mm(name, [a, b], TN, (NSH,),
               [pl.BlockSpec((None, S, FS), lambda s: (s, 0, 0)), pl.BlockSpec((S, D), lambda s: (0, 0))],
               pl.BlockSpec((None, FS, D), lambda s: (s, 0, 0)), _sds((NSH, FS, D), BF16), rider)


def _ffn_dn(name, dgate, dup, w, tail_of, rider=None):
    rows = TS // 2
    tail, o_specs, o_shapes = tail_of(rows)
    a2 = pl.BlockSpec((NSH, rows, FS), lambda i: (0, i, 0))
    return _mm(name, [dgate, w.gu, dup, w.gu], NN, (S // rows,),
               [a2, pl.BlockSpec((NSH, None, FS, D), lambda i: (0, w.g0, 0, 0)),
                a2, pl.BlockSpec((NSH, None, FS, D), lambda i: (0, w.g0 + 1, 0, 0))], o_specs, o_shapes, rider, tail)


def _out_proj_dx(dh, wout, ot):
    def body(dh_ref, w_ref, o_ref, dyn_ref, do_ref, dl_ref):
        dm = _dot(dh_ref[...], w_ref[...], NT)
        dyn_ref[...] = dm[:, D:]
        for b in range(TS // 128):
            for j, blk in enumerate(_rows_to_blocks(dm[128 * b:128 * (b + 1), :D])):
                do = blk.astype(BF16)
                do_ref[j, b] = do
                dl_ref[j, b] = jnp.sum(o_ref[j, b] * do.astype(F32), axis=0, keepdims=True)

    blocks = pl.BlockSpec((NKV, TS // 128, HD, QROWS), lambda i: (0, i, 0, 0))
    return pl.pallas_call(
        body, name="out_proj_dx", grid=(S // TS,),
        in_specs=[pl.BlockSpec((TS, D), lambda i: (i, 0)), pl.BlockSpec((2 * D, D), lambda i: (0, 0)), blocks],
        out_specs=[pl.BlockSpec((TS, D), lambda i: (i, 0)), blocks,
                   pl.BlockSpec((NKV, TS // 128, 1, QROWS), lambda i: (0, i, 0, 0))],
        out_shape=[_sds((S, D), F32), _sds((NKV, NCH, HD, QROWS), BF16), _sds((NKV, NCH, 1, QROWS), F32)],
        compiler_params=_cparams("parallel"),
    )(dh, wout, ot)


def _heads(t, n):
    return t.reshape(S, n, HD).transpose(1, 0, 2)


def _pad128(v):
    return jnp.pad(v, ((0, 0), (0, 128 - v.shape[1])))


def _local_step(x, positions, tgt, sp, gu1, d1, f2, wint, wout, convw, comm=None):
    inv_freq = ROPE_THETA ** (-jnp.arange(0, HD, 2, dtype=F32) / HD)
    ang = positions.astype(F32)[:, None] * inv_freq
    ang = jnp.concatenate([ang, ang, ang, ang], axis=-1)
    dtb, alog = _pad128(sp["dt_bias"]), _pad128(sp["a_log"])
    dskip_l = jnp.repeat(sp["d_skip"], HD, axis=1)
    convb = sp["conv_b"]

    if comm:
        rider = _join_riders([comm.gather(gu1), _small_gather_rider(convw)])
        (n1, d1, f2, wint, wout, cos, sin, bias), (gu1, convw) = _prenorm_casts(
            "prenorm1", x, sp["ffn1_pre_norm"], ang, comm.shard, [(d1, 0), (f2, 0), (wint, 1), (wout, 0)], rider)
        wint, wout = wint.reshape(NSH, WIN_SH, D), wout.reshape(NSH, 2 * D // NSH, D)
        convw = convw.transpose(1, 0, 2).reshape(CONV_K, CONV_C)
    else:
        n1, cos, sin, bias = _prenorm("prenorm1", x, sp["ffn1_pre_norm"]), jnp.cos(ang), jnp.sin(ang), _bias_table()
    rider = comm.gather(d1) if comm else None
    (fg1, fu1, act1), got = _ridden(_ffn_up("ffn1_up", n1, _FfnW(gu1, 0, d1, 0), rider), rider)
    if comm:
        d1, = got
    w1 = _FfnW(gu1, 0, d1, 0)
    rider = comm.gather(wint) if comm else None
    (h1, x1, n2), got = _ridden(_ffn_down(
        "ffn1_down", act1, w1,
        lambda rows: _tail_postres(rows, x, sp["ffn1_post_norm"], 0.5, sp["mix_pre_norm"]), rider), rider)
    if comm:
        wint, = got
    wint_pad = jnp.pad(wint.reshape(WIN_COLS, D), ((0, WIN_PAD - WIN_COLS), (0, 0)))

    pw = WIN_PAD // 3
    rider = comm.gather(f2, part=(0, 1)) if comm else None
    proj, got = _ridden(_mm(
        "in_proj", [n2, wint_pad], NT, (S // TS, 3),
        [pl.BlockSpec((TS, D), lambda i, j: (i, 0)), pl.BlockSpec((pw, D), lambda i, j: (j, 0))],
        pl.BlockSpec((TS, pw), lambda i, j: (i, j)), _sds((S, WIN_PAD), F32), rider), rider)
    if comm:
        f2, = got
    qt = _rope_q(proj, cos, sin)
    k_rot, v_bf, kt, vt = _rope_kv(proj, cos, sin)
    kh, vh = _heads(k_rot, NKV), _heads(v_bf, NKV)
    rider = comm.gather(f2, wout, part=(1, 2), at=9) if comm else None
    (ot, lse, mixed), got = _ridden(_attn_fwd(qt, kh, vt, bias, rider), rider)
    if comm:
        f2, wout = got
    w2 = _FfnW(f2, 0, f2, 2)
    wout = wout.reshape(2 * D, D)
    xbc, conv_y = _conv_fwd(proj, convw, convb)
    y, mixed, hprev = _ssd_fwd(xbc, proj, dtb, alog, dskip_l, sp["ssm_norm"], mixed)
    tail, o_specs, o_shapes = _tail_postres(TS, x1, sp["mix_post_norm"], 1.0, sp["ffn2_pre_norm"])
    h2, x2, n3 = _mm("out_proj", [mixed, wout], NN, (S // TS,),
                     [pl.BlockSpec((TS, 2 * D), lambda i: (i, 0)), pl.BlockSpec((2 * D, D), lambda i: (0, 0))],
                     o_specs, o_shapes, None, tail)

    fg2, fu2, act2 = _ffn_up("ffn2_up", n3, w2)
    dy, dh3, dp3, loss = _ffn_down(
        "ffn2_down", act2, w2, lambda rows: _tail_final(rows, x2, sp["ffn2_post_norm"], tgt, 0.5))

    dgate2, dup2 = _ffn_dact("ffn2_dact", dh3, w2, fg2, fu2)
    dws2 = [_ffn_dw("ffn2_dwg", dgate2, n3), _ffn_dw("ffn2_dwu", dup2, n3), _ffn_dw("ffn2_dwd", act2, dh3)]
    dx2, dh2, dg3, dp2 = _ffn_dn(
        "ffn2_dn", dgate2, dup2, w2,
        lambda rows: _tail_mid_bwd(rows, dy, x2, sp["ffn2_pre_norm"], h2, sp["mix_post_norm"], 1.0))

    dyn, dot_, delta = _out_proj_dx(dh2, wout, ot)
    dwout = _mm("out_proj_dw", [mixed, dh2], TN, (2,),
                [pl.BlockSpec((S, D), lambda m: (0, m)), pl.BlockSpec((S, D), lambda m: (0, 0))],
                pl.BlockSpec((D, D), lambda m: (m, 0)), _sds((2 * D, D), BF16))
    dwout = dwout.reshape(NSH, 2 * D // NSH, D)

    def riding(tag, names, ps, call, theirs=None, rows=None):
        rider = None
        if comm:
            rider = comm.rest_rider(tag) if names is None else comm.reduce_rider(tag, names, ps, theirs, rows)
        res, got = _ridden(call(rider), rider)
        if comm:
            comm.landed(tag, got)
        return res

    rider = _to_sibling_rider(dws2 + [dwout]) if comm else None
    (dxbc, dproj, ddt, dssm, dsc), theirs = _ridden(
        _ssd_bwd(dyn, y, xbc, proj, hprev, dtb, alog, dskip_l, sp["ssm_norm"], rider), rider)
    dproj, dcw8, dcb = _conv_bwd(dxbc, conv_y, proj, convw, dproj)
    dqt, dkh, dvh = riding("a", BIG[3:6] + ("w_out",), dws2 + [dwout], lambda rider: _attn_bwd(
        qt, kh, kt, vh, dot_, lse, delta, bias, rider), theirs)
    dproj = _rope_dq(dqt, cos, sin, dproj)
    dproj = _rope_dkv(dkh, dvh, cos, sin, dproj)
    dproj = lax.dynamic_update_slice(dproj, ddt, (0, COL_DT))
    dwint = _mm("in_proj_dw", [dproj, n2], TN, (3,),
                [pl.BlockSpec((S, pw), lambda j: (0, j)), pl.BlockSpec((S, D), lambda j: (0, 0))],
                pl.BlockSpec((pw, D), lambda j: (j, 0)), _sds((WIN_PAD, D), BF16))
    dwint = dwint[:WIN_COLS].reshape(NSH, WIN_SH, D)

    tail, o_specs, o_shapes = _tail_mid_bwd(TS, dx2, x1, sp["mix_pre_norm"], h1, sp["ffn1_post_norm"], 0.5)
    dx1, dh1, dg2, dp1 = riding("b", ("w_in",), [dwint], lambda rider: _mm(
        "in_proj_dx", [dproj, wint_pad], NN, (S // TS,),
        [pl.BlockSpec((TS, WIN_PAD), lambda i: (i, 0)), pl.BlockSpec((WIN_PAD, D), lambda i: (0, 0))],
        o_specs, o_shapes, rider, tail), rows=W_IN_FIRST)

    dwd1 = riding("b", None, None, lambda rider: _ffn_dw("ffn1_dwd", act1, dh1, rider))
    dgate1, dup1 = riding("d", BIG[2:3], [dwd1], lambda rider: _ffn_dact("ffn1_dact", dh1, w1, fg1, fu1, rider))
    dwg1, dwu1 = _ffn_dw("ffn1_dwg", dgate1, n1), _ffn_dw("ffn1_dwu", dup1, n1)
    grad_x, dg1 = riding("g", BIG[0:2], [dwg1, dwu1], lambda rider: _ffn_dn(
        "ffn1_dn", dgate1, dup1, w1, lambda rows: _tail_first_bwd(rows, dx1, x, sp["ffn1_pre_norm"]), rider))
    dws1 = [dwg1, dwu1, dwd1]

    small = jnp.concatenate([
        dg1[0], dp1[0], dg2[0], dssm[0], dp2[0], dg3[0], dp3[0], dcb[0],
        dsc[0, :16], dsc[1, :16], dsc[2, :16], dcw8[:CONV_K].reshape(-1), loss[0, :1]])
    small = jnp.pad(small, (0, SMALL_LEN - small.shape[0])).reshape(SMALL_ROWS, D)
    if comm is None:
        return grad_x, dws1 + dws2 + [dwint, dwout], small
    return (grad_x,) + comm.finish(small)


WEIGHTS = ("ffn1_pre_norm", "ffn1_w_gate", "ffn1_w_up", "ffn1_w_down", "ffn1_post_norm", "mix_pre_norm", "w_in",
           "conv_w", "conv_b", "dt_bias", "a_log", "d_skip", "ssm_norm", "w_out", "mix_post_norm", "ffn2_pre_norm",
           "ffn2_w_gate", "ffn2_w_up", "ffn2_w_down", "ffn2_post_norm")
BIG = ("ffn1_w_gate", "ffn1_w_up", "ffn1_w_down", "ffn2_w_gate", "ffn2_w_up", "ffn2_w_down", "w_in", "w_out")
TRANSPOSED = ("ffn1_w_gate", "ffn1_w_up", "ffn2_w_gate", "ffn2_w_up", "w_in")
SMALL_ORDER = SMALL_1K + ("conv_b",) + SMALL_16
CONVW_SH = CONV_C // NSH


def _shard2d(t, name):
    return t[0].T if name in TRANSPOSED else t[0]


def _unshard2d(t, name):
    return (t.T if name in TRANSPOSED else t)[None]


def _rows3d(t):
    return t.transpose(2, 0, 1)


def _pack_small(d, prefix, shard_of_convw):
    flat = jnp.concatenate([d[prefix + n][0] for n in SMALL_ORDER] + [shard_of_convw.reshape(-1)])
    return jnp.pad(flat, (0, SMALL_LEN - flat.shape[0])).reshape(SMALL_ROWS, D)


def _unpack_small(block, like):
    flat = block.reshape(-1)
    out, off = {}, 0
    for n in SMALL_ORDER:
        size = like[n].shape[1]
        out[n] = flat[off:off + size].reshape(1, size)
        off += size
    out["conv_w"] = flat[off:off + CONV_K * CONVW_SH].reshape(1, CONV_K, CONVW_SH)
    return out


def kernel(x, positions, ffn1_pre_norm, ffn1_w_gate, ffn1_w_up, ffn1_w_down, ffn1_post_norm, mix_pre_norm, w_in, conv_w, conv_b, dt_bias, a_log, d_skip, ssm_norm, w_out, mix_post_norm, ffn2_pre_norm, ffn2_w_gate, ffn2_w_up, ffn2_w_down, ffn2_post_norm, loss_target, m_ffn1_pre_norm, m_ffn1_w_gate, m_ffn1_w_up, m_ffn1_w_down, m_ffn1_post_norm, m_mix_pre_norm, m_w_in, m_conv_w, m_conv_b, m_dt_bias, m_a_log, m_d_skip, m_ssm_norm, m_w_out, m_mix_post_norm, m_ffn2_pre_norm, m_ffn2_w_gate, m_ffn2_w_up, m_ffn2_w_down, m_ffn2_post_norm, v_ffn1_pre_norm, v_ffn1_w_gate, v_ffn1_w_up, v_ffn1_w_down, v_ffn1_post_norm, v_mix_pre_norm, v_w_in, v_conv_w, v_conv_b, v_dt_bias, v_a_log, v_d_skip, v_ssm_norm, v_w_out, v_mix_post_norm, v_ffn2_pre_norm, v_ffn2_w_gate, v_ffn2_w_up, v_ffn2_w_down, v_ffn2_post_norm):
    given = dict(locals())
    xi, yi = lax.axis_index("x"), lax.axis_index("y")

    comm = _Comm()
    big = {p + n: _shard2d(given[p + n], n) for n in BIG for p in ("", "m_", "v_")}
    gu1 = _cast_stack("cast_ffn1_gate_up", comm.shard, [big[n] for n in BIG[0:2]], 176, D)

    sp = {n: given[n] for n in SMALL_ORDER}
    grad_x, big_grads, small = _local_step(
        x[0], positions[0], loss_target[0], sp, gu1, [big[BIG[2]]], [big[n] for n in BIG[3:6]], [big["w_in"]],
        [big["w_out"]], conv_w[0], comm)

    tot = small.reshape(-1)
    loss = tot[OFF_LOSS]
    small_grads, off = {}, 0
    for n in SMALL_ORDER:
        size = given[n].shape[1]
        small_grads[n] = tot[off:off + size].reshape(1, size)
        off += size
    dconvw = tot[OFF_CONVW:OFF_CONVW + CONV_K * CONV_C].reshape(CONV_K, NSH, CONVW_SH)
    dconvw = lax.dynamic_index_in_dim(dconvw, 2 * xi + yi, axis=1, keepdims=False)
    small_grads["conv_w"] = dconvw.reshape(1, CONV_K, CONVW_SH)

    upd = {}
    for names, tr in ((BIG[0:3], 176), (BIG[3:6], 176), (BIG[7:8], 256)):
        res = _adamw("adamw_" + names[0], [big[n] for n in names], [big_grads[n] for n in names],
                     [big["m_" + n] for n in names], [big["v_" + n] for n in names], tr, D)
        for n, r in zip(names, res):
            upd[n] = tuple(_unshard2d(t, n) for t in r)
    g_win = big_grads["w_in"].reshape(WIN_SH, 1, D)
    res, = _adamw("adamw_w_in", [_rows3d(w_in)], [g_win], [_rows3d(m_w_in)], [_rows3d(v_w_in)], WIN_SH // 4, D)
    upd["w_in"] = tuple(t.transpose(1, 2, 0) for t in res)
    (dl, m2, v2, _), = _adamw(
        "adamw_small", [_pack_small(given, "", conv_w[0])], [_pack_small(small_grads, "", dconvw)],
        [_pack_small(given, "m_", m_conv_w[0])], [_pack_small(given, "v_", v_conv_w[0])], SMALL_ROWS, D)
    dl, m2, v2 = (_unpack_small(t, given) for t in (dl, m2, v2))
    for n in SMALL_ORDER + ("conv_w",):
        upd[n] = (dl[n], m2[n], v2[n], small_grads[n])

    return (loss, grad_x[None], *[upd[n][3] for n in WEIGHTS], *[upd[n][0] for n in WEIGHTS],
            *[upd[n][1] for n in WEIGHTS], *[upd[n][2] for n in WEIGHTS])
```
